```python
import jax, jax.numpy as jnp
from jax import lax
import numpy as np

D_MODEL = 1024
BATCH = 8
SEQ = 4096
DEPTH = 1

N_META = 16
CHUNK = 128
META_PAD = CHUNK - N_META
SSM_EXPAND = 2
D_INNER = SSM_EXPAND * D_MODEL
SSM_HEAD_DIM = 64
SSM_HEADS = D_INNER // SSM_HEAD_DIM
SSM_GROUPS = 4
D_STATE = 128
SSM_CONV = 4
CONV_DIM = D_INNER + 2 * SSM_GROUPS * D_STATE
ATTN_HEADS = 16
ATTN_KV_HEADS = 4
ATTN_HEAD_DIM = 64
ATTN_GROUP = ATTN_HEADS // ATTN_KV_HEADS
WINDOW = 128
ATTN_WIDTH = ATTN_HEADS * ATTN_HEAD_DIM
FFN_DIM = 2816
FFN_CONV = 3
N_IN = D_INNER + CONV_DIM + SSM_HEADS + (ATTN_HEADS + 2 * ATTN_KV_HEADS) * ATTN_HEAD_DIM + 2 * D_MODEL
EPS = 1e-6
NEG = -1e30

kernel_name = "hybrid_ssd_swa_sink_alibi_convffn"


def _rmsnorm(x, w):
    xf = x.astype(jnp.float32)
    y = xf * lax.rsqrt(jnp.mean(xf * xf, axis=-1, keepdims=True) + EPS)
    return (y * w.astype(jnp.float32)).astype(x.dtype)


def _causal_dwconv(u, w, b):
    k_width = w.shape[0]
    seq_len = u.shape[1]
    up = jnp.pad(u, ((0, 0), (k_width - 1, 0), (0, 0)))
    out = b.astype(u.dtype) + w[k_width - 1].astype(u.dtype) * u
    for k in range(k_width - 1):
        out = out + w[k].astype(u.dtype) * up[:, k:k + seq_len]
    return out


def _segsum_exp(a):
    t = a.shape[-1]
    cs = jnp.cumsum(a, axis=-1)
    mask = jnp.tril(jnp.ones((t, t), dtype=bool))
    diff = cs[..., :, None] - cs[..., None, :]
    return jnp.where(mask, jnp.exp(jnp.where(mask, diff, 0.0)), 0.0)


def _ssd_chunked(x_dt, a_dt, b_in, c_in):
    bsz, lp, n_heads, p_dim = x_dt.shape
    g, n = b_in.shape[2], b_in.shape[3]
    r = n_heads // g
    nc = lp // CHUNK
    xc = x_dt.reshape(bsz, nc, CHUNK, g, r, p_dim)
    ac = a_dt.reshape(bsz, nc, CHUNK, g, r).transpose(0, 3, 4, 1, 2)
    bc = b_in.reshape(bsz, nc, CHUNK, g, n)
    cc = c_in.reshape(bsz, nc, CHUNK, g, n)
    a_cs = jnp.cumsum(ac, axis=-1)
    lmat = _segsum_exp(ac)
    cb = jnp.einsum("bclgn,bcsgn->bgcls", cc, bc)
    y_diag = jnp.einsum("bgcls,bgrcls,bcsgrp->bclgrp", cb, lmat, xc)
    decay_states = jnp.exp(a_cs[..., -1:] - a_cs)
    states = jnp.einsum("bclgn,bgrcl,bclgrp->bcgrpn", bc, decay_states, xc)
    chunk_decay = jnp.exp(a_cs[..., -1])

    def step(h, inp):
        s_c, d_c = inp
        return h * d_c[..., None, None] + s_c, h

    h0 = jnp.zeros_like(states[:, 0])
    _, h_in = lax.scan(step, h0, (jnp.moveaxis(states, 1, 0), jnp.moveaxis(chunk_decay, -1, 0)))
    h_in = jnp.moveaxis(h_in, 0, 1)
    y_off = jnp.einsum("bclgn,bcgrpn,bgrcl->bclgrp", cc, h_in, jnp.exp(a_cs))
    return (y_diag + y_off).reshape(bsz, lp, n_heads, p_dim)


def _ssd_branch(z, xbc, dt_raw, conv_w, conv_b, dt_bias, a_log, d_skip, norm_w, w_out):
    bsz, seq_len, _ = xbc.shape
    xbc = jax.nn.silu(_causal_dwconv(xbc, conv_w, conv_b))
    xs, bs, cs = jnp.split(xbc, [D_INNER, D_INNER + SSM_GROUPS * D_STATE], axis=-1)
    xs = xs.reshape(bsz, seq_len, SSM_HEADS, SSM_HEAD_DIM).astype(jnp.float32)
    bs = bs.reshape(bsz, seq_len, SSM_GROUPS, D_STATE).astype(jnp.float32)
    cs = cs.reshape(bsz, seq_len, SSM_GROUPS, D_STATE).astype(jnp.float32)
    dt = jax.nn.softplus(dt_raw.astype(jnp.float32) + dt_bias.astype(jnp.float32))
    a = -jnp.exp(a_log.astype(jnp.float32))
    pad4 = ((0, 0), (META_PAD, 0), (0, 0), (0, 0))
    x_dt = jnp.pad(xs * dt[..., None], pad4)
    a_dt = jnp.pad(dt * a, ((0, 0), (META_PAD, 0), (0, 0)))
    y = _ssd_chunked(x_dt, a_dt, jnp.pad(bs, pad4), jnp.pad(cs, pad4))[:, META_PAD:]
    y = y + xs * d_skip.astype(jnp.float32)[:, None]
    y = y.reshape(bsz, seq_len, D_INNER).astype(z.dtype)
    y = _rmsnorm(y * jax.nn.silu(z), norm_w)
    return y @ w_out


def _swa_branch(q, k, v, sinks, w_out):
    bsz, seq_len, _ = q.shape
    lp = seq_len + META_PAD
    nb = lp // CHUNK
    scale = ATTN_HEAD_DIM ** -0.5
    q = q.reshape(bsz, seq_len, ATTN_KV_HEADS, ATTN_GROUP, ATTN_HEAD_DIM)
    k = k.reshape(bsz, seq_len, ATTN_KV_HEADS, ATTN_HEAD_DIM)
    v = v.reshape(bsz, seq_len, ATTN_KV_HEADS, ATTN_HEAD_DIM)
    qb = jnp.pad(q, ((0, 0), (META_PAD, 0), (0, 0), (0, 0), (0, 0))).reshape(
        bsz, nb, CHUNK, ATTN_KV_HEADS, ATTN_GROUP, ATTN_HEAD_DIM)
    kpad = ((0, 0), (META_PAD + CHUNK, 0), (0, 0), (0, 0))
    kp = jnp.pad(k, kpad).reshape(bsz, nb + 1, CHUNK, ATTN_KV_HEADS, ATTN_HEAD_DIM)
    vp = jnp.pad(v, kpad).reshape(bsz, nb + 1, CHUNK, ATTN_KV_HEADS, ATTN_HEAD_DIM)
    kb = jnp.concatenate([kp[:, :-1], kp[:, 1:]], axis=2)
    vb = jnp.concatenate([vp[:, :-1], vp[:, 1:]], axis=2)
    q_pos = jnp.arange(nb)[:, None] * CHUNK + jnp.arange(CHUNK)[None, :] - META_PAD
    k_pos = jnp.arange(nb)[:, None] * CHUNK + jnp.arange(2 * CHUNK)[None, :] - CHUNK - META_PAD
    dist = q_pos[:, :, None] - k_pos[:, None, :]
    band_ok = (dist >= 0) & (dist < WINDOW) & (k_pos[:, None, :] >= N_META)
    slopes = jnp.exp2(-8.0 * jnp.arange(1, ATTN_HEADS + 1, dtype=jnp.float32) / ATTN_HEADS)
    slopes = slopes.reshape(ATTN_KV_HEADS, ATTN_GROUP)
    s_band = jnp.einsum("bnqkgd,bnskd->bnkgqs", qb, kb, preferred_element_type=jnp.float32) * scale
    s_band = s_band - slopes[None, None, :, :, None, None] * dist.astype(jnp.float32)[None, :, None, None]
    s_band = jnp.where(band_ok[None, :, None, None], s_band, NEG)
    k_meta, v_meta = k[:, :N_META], v[:, :N_META]
    s_meta = jnp.einsum("bnqkgd,bmkd->bnkgqm", qb, k_meta, preferred_element_type=jnp.float32) * scale
    meta_ok = jnp.arange(N_META)[None, None, :] <= q_pos[:, :, None]
    s_meta = jnp.where(meta_ok[None, :, None, None], s_meta, NEG)
    sink = jnp.broadcast_to(sinks.astype(jnp.float32).reshape(ATTN_KV_HEADS, ATTN_GROUP)[None, None, :, :, None, None],
                            s_band.shape[:-1] + (1,))
    probs = jax.nn.softmax(jnp.concatenate([s_meta, s_band, sink], axis=-1), axis=-1).astype(v.dtype)
    out = (jnp.einsum("bnkgqm,bmkd->bnqkgd", probs[..., :N_META], v_meta)
           + jnp.einsum("bnkgqs,bnskd->bnqkgd", probs[..., N_META:N_META + 2 * CHUNK], vb))
    out = out.reshape(bsz, lp, ATTN_WIDTH)[:, META_PAD:]
    return out @ w_out


def _token_mixer(h, w_in, ssm_conv_w, ssm_conv_b, ssm_dt_bias, ssm_a_log, ssm_d_skip, ssm_norm,
                 w_ssm_out, attn_sinks, w_attn_out, w_mix_out):
    sizes = [D_INNER, CONV_DIM, SSM_HEADS, ATTN_WIDTH, ATTN_KV_HEADS * ATTN_HEAD_DIM,
             ATTN_KV_HEADS * ATTN_HEAD_DIM, 2 * D_MODEL]
    cuts = [int(c) for c in np.cumsum(sizes)[:-1]]
    z, xbc, dt_raw, q, k, v, gate_logits = jnp.split(h @ w_in, cuts, axis=-1)
    y_ssm = _ssd_branch(z, xbc, dt_raw, ssm_conv_w, ssm_conv_b, ssm_dt_bias, ssm_a_log, ssm_d_skip,
                        ssm_norm, w_ssm_out)
    y_attn = _swa_branch(q, k, v, attn_sinks, w_attn_out)
    gates = jax.nn.sigmoid(gate_logits.astype(jnp.float32)).astype(h.dtype)
    g_ssm, g_attn = jnp.split(gates, 2, axis=-1)
    return (g_ssm * y_ssm + g_attn * y_attn) @ w_mix_out


def _conv_ffn(h, w_up, conv_w, conv_b, w_down):
    u = _causal_dwconv(h @ w_up, conv_w, conv_b)
    a, g = jnp.split(u, 2, axis=-1)
    return (jax.nn.silu(a) * g) @ w_down


def _fwd_setup_inputs(seed: int = 0) -> dict:
    key = jax.random.key(seed)
    ks = jax.random.split(key, 24)
    f32 = jnp.float32

    def nrm(k, shape, scale):
        return jax.random.normal(k, shape, f32) * scale

    def gain(k, dim):
        return 1.0 + nrm(k, (DEPTH, dim), 0.01)

    dt0 = jnp.exp(jax.random.uniform(ks[6], (DEPTH, SSM_HEADS), f32, np.log(1e-3), np.log(1e-1)))
    return {
        "x": nrm(ks[0], (BATCH, SEQ, D_MODEL), 1.0),
        "meta_tokens": nrm(ks[1], (N_META, D_MODEL), 1.0),
        "norm_pre_mix": gain(ks[2], D_MODEL),
        "w_in": nrm(ks[3], (DEPTH, D_MODEL, N_IN), D_MODEL ** -0.5),
        "ssm_conv_w": nrm(ks[4], (DEPTH, SSM_CONV, CONV_DIM), 0.5 * SSM_CONV ** -0.5),
        "ssm_conv_b": nrm(ks[5], (DEPTH, CONV_DIM), 0.01),
        "ssm_dt_bias": dt0 + jnp.log(-jnp.expm1(-dt0)),
        "ssm_a_log": jnp.log(jax.random.uniform(ks[7], (DEPTH, SSM_HEADS), f32, 1.0, 16.0)),
        "ssm_d_skip": 1.0 + nrm(ks[8], (DEPTH, SSM_HEADS), 0.01),
        "ssm_norm": gain(ks[9], D_INNER),
        "w_ssm_out": nrm(ks[10], (DEPTH, D_INNER, D_MODEL), D_INNER ** -0.5),
        "attn_sinks": nrm(ks[11], (DEPTH, ATTN_HEADS), 1.0),
        "w_attn_out": nrm(ks[12], (DEPTH, ATTN_WIDTH, D_MODEL), ATTN_WIDTH ** -0.5),
        "w_mix_out": nrm(ks[13], (DEPTH, D_MODEL, D_MODEL), D_MODEL ** -0.5),
        "norm_post_mix": gain(ks[14], D_MODEL),
        "norm_pre_ffn": gain(ks[15], D_MODEL),
        "w_ffn_up": nrm(ks[16], (DEPTH, D_MODEL, 2 * FFN_DIM), D_MODEL ** -0.5),
        "ffn_conv_w": nrm(ks[17], (DEPTH, FFN_CONV, 2 * FFN_DIM), 0.5 * FFN_CONV ** -0.5),
        "ffn_conv_b": nrm(ks[18], (DEPTH, 2 * FFN_DIM), 0.01),
        "w_ffn_down": nrm(ks[19], (DEPTH, FFN_DIM, D_MODEL), FFN_DIM ** -0.5),
        "norm_post_ffn": gain(ks[20], D_MODEL),
    }


def _fwd_reference(x, meta_tokens, norm_pre_mix, w_in, ssm_conv_w, ssm_conv_b, ssm_dt_bias, ssm_a_log,
              ssm_d_skip, ssm_norm, w_ssm_out, attn_sinks, w_attn_out, w_mix_out, norm_post_mix,
              norm_pre_ffn, w_ffn_up, ffn_conv_w, ffn_conv_b, w_ffn_down, norm_post_ffn):
    bsz = x.shape[0]
    meta = jnp.broadcast_to(meta_tokens.astype(x.dtype)[None], (bsz, N_META, D_MODEL))
    h = jnp.concatenate([meta, x], axis=1)
    for l in range(DEPTH):
        mix = _token_mixer(_rmsnorm(h, norm_pre_mix[l]), w_in[l], ssm_conv_w[l], ssm_conv_b[l],
                           ssm_dt_bias[l], ssm_a_log[l], ssm_d_skip[l], ssm_norm[l], w_ssm_out[l],
                           attn_sinks[l], w_attn_out[l], w_mix_out[l])
        h = h + _rmsnorm(mix, norm_post_mix[l])
        ffn = _conv_ffn(_rmsnorm(h, norm_pre_ffn[l]), w_ffn_up[l], ffn_conv_w[l], ffn_conv_b[l], w_ffn_down[l])
        h = h + _rmsnorm(ffn, norm_post_ffn[l])
    return h[:, N_META:]


import jax as _jax
import jax.numpy as _jnp

TWIN_FORMAT = 'train_step'
FWD_PARAMS = ['x', 'meta_tokens', 'norm_pre_mix', 'w_in', 'ssm_conv_w', 'ssm_conv_b', 'ssm_dt_bias', 'ssm_a_log', 'ssm_d_skip', 'ssm_norm', 'w_ssm_out', 'attn_sinks', 'w_attn_out', 'w_mix_out', 'norm_post_mix', 'norm_pre_ffn', 'w_ffn_up', 'ffn_conv_w', 'ffn_conv_b', 'w_ffn_down', 'norm_post_ffn']
TWIN_WEIGHTS = ['meta_tokens', 'norm_pre_mix', 'w_in', 'ssm_conv_w', 'ssm_conv_b', 'ssm_dt_bias', 'ssm_a_log', 'ssm_d_skip', 'ssm_norm', 'w_ssm_out', 'attn_sinks', 'w_attn_out', 'w_mix_out', 'norm_post_mix', 'norm_pre_ffn', 'w_ffn_up', 'ffn_conv_w', 'ffn_conv_b', 'w_ffn_down', 'norm_post_ffn']
TWIN_DIFF_INPUT = 'x'
TWIN_INPUTS = ['x', 'meta_tokens', 'norm_pre_mix', 'w_in', 'ssm_conv_w', 'ssm_conv_b', 'ssm_dt_bias', 'ssm_a_log', 'ssm_d_skip', 'ssm_norm', 'w_ssm_out', 'attn_sinks', 'w_attn_out', 'w_mix_out', 'norm_post_mix', 'norm_pre_ffn', 'w_ffn_up', 'ffn_conv_w', 'ffn_conv_b', 'w_ffn_down', 'norm_post_ffn', 'loss_target', 'm_meta_tokens', 'm_norm_pre_mix', 'm_w_in', 'm_ssm_conv_w', 'm_ssm_conv_b', 'm_ssm_dt_bias', 'm_ssm_a_log', 'm_ssm_d_skip', 'm_ssm_norm', 'm_w_ssm_out', 'm_attn_sinks', 'm_w_attn_out', 'm_w_mix_out', 'm_norm_post_mix', 'm_norm_pre_ffn', 'm_w_ffn_up', 'm_ffn_conv_w', 'm_ffn_conv_b', 'm_w_ffn_down', 'm_norm_post_ffn', 'v_meta_tokens', 'v_norm_pre_mix', 'v_w_in', 'v_ssm_conv_w', 'v_ssm_conv_b', 'v_ssm_dt_bias', 'v_ssm_a_log', 'v_ssm_d_skip', 'v_ssm_norm', 'v_w_ssm_out', 'v_attn_sinks', 'v_w_attn_out', 'v_w_mix_out', 'v_norm_post_mix', 'v_norm_pre_ffn', 'v_w_ffn_up', 'v_ffn_conv_w', 'v_ffn_conv_b', 'v_w_ffn_down', 'v_norm_post_ffn']
TWIN_OUTPUTS = ['loss', 'grad_x', 'grad_meta_tokens', 'grad_norm_pre_mix', 'grad_w_in', 'grad_ssm_conv_w', 'grad_ssm_conv_b', 'grad_ssm_dt_bias', 'grad_ssm_a_log', 'grad_ssm_d_skip', 'grad_ssm_norm', 'grad_w_ssm_out', 'grad_attn_sinks', 'grad_w_attn_out', 'grad_w_mix_out', 'grad_norm_post_mix', 'grad_norm_pre_ffn', 'grad_w_ffn_up', 'grad_ffn_conv_w', 'grad_ffn_conv_b', 'grad_w_ffn_down', 'grad_norm_post_ffn', 'delta_meta_tokens', 'delta_norm_pre_mix', 'delta_w_in', 'delta_ssm_conv_w', 'delta_ssm_conv_b', 'delta_ssm_dt_bias', 'delta_ssm_a_log', 'delta_ssm_d_skip', 'delta_ssm_norm', 'delta_w_ssm_out', 'delta_attn_sinks', 'delta_w_attn_out', 'delta_w_mix_out', 'delta_norm_post_mix', 'delta_norm_pre_ffn', 'delta_w_ffn_up', 'delta_ffn_conv_w', 'delta_ffn_conv_b', 'delta_w_ffn_down', 'delta_norm_post_ffn', 'new_m_meta_tokens', 'new_m_norm_pre_mix', 'new_m_w_in', 'new_m_ssm_conv_w', 'new_m_ssm_conv_b', 'new_m_ssm_dt_bias', 'new_m_ssm_a_log', 'new_m_ssm_d_skip', 'new_m_ssm_norm', 'new_m_w_ssm_out', 'new_m_attn_sinks', 'new_m_w_attn_out', 'new_m_w_mix_out', 'new_m_norm_post_mix', 'new_m_norm_pre_ffn', 'new_m_w_ffn_up', 'new_m_ffn_conv_w', 'new_m_ffn_conv_b', 'new_m_w_ffn_down', 'new_m_norm_post_ffn', 'new_v_meta_tokens', 'new_v_norm_pre_mix', 'new_v_w_in', 'new_v_ssm_conv_w', 'new_v_ssm_conv_b', 'new_v_ssm_dt_bias', 'new_v_ssm_a_log', 'new_v_ssm_d_skip', 'new_v_ssm_norm', 'new_v_w_ssm_out', 'new_v_attn_sinks', 'new_v_w_attn_out', 'new_v_w_mix_out', 'new_v_norm_post_mix', 'new_v_norm_pre_ffn', 'new_v_w_ffn_up', 'new_v_ffn_conv_w', 'new_v_ffn_conv_b', 'new_v_w_ffn_down', 'new_v_norm_post_ffn']
TWIN_LEAF_KINDS = {'loss': 'loss', 'grad_x': 'grad_x', 'grad_meta_tokens': 'grad_w', 'grad_norm_pre_mix': 'grad_w', 'grad_w_in': 'grad_w', 'grad_ssm_conv_w': 'grad_w', 'grad_ssm_conv_b': 'grad_w', 'grad_ssm_dt_bias': 'grad_w', 'grad_ssm_a_log': 'grad_w', 'grad_ssm_d_skip': 'grad_w', 'grad_ssm_norm': 'grad_w', 'grad_w_ssm_out': 'grad_w', 'grad_attn_sinks': 'grad_w', 'grad_w_attn_out': 'grad_w', 'grad_w_mix_out': 'grad_w', 'grad_norm_post_mix': 'grad_w', 'grad_norm_pre_ffn': 'grad_w', 'grad_w_ffn_up': 'grad_w', 'grad_ffn_conv_w': 'grad_w', 'grad_ffn_conv_b': 'grad_w', 'grad_w_ffn_down': 'grad_w', 'grad_norm_post_ffn': 'grad_w', 'delta_meta_tokens': 'delta_w', 'delta_norm_pre_mix': 'delta_w', 'delta_w_in': 'delta_w', 'delta_ssm_conv_w': 'delta_w', 'delta_ssm_conv_b': 'delta_w', 'delta_ssm_dt_bias': 'delta_w', 'delta_ssm_a_log': 'delta_w', 'delta_ssm_d_skip': 'delta_w', 'delta_ssm_norm': 'delta_w', 'delta_w_ssm_out': 'delta_w', 'delta_attn_sinks': 'delta_w', 'delta_w_attn_out': 'delta_w', 'delta_w_mix_out': 'delta_w', 'delta_norm_post_mix': 'delta_w', 'delta_norm_pre_ffn': 'delta_w', 'delta_w_ffn_up': 'delta_w', 'delta_ffn_conv_w': 'delta_w', 'delta_ffn_conv_b': 'delta_w', 'delta_w_ffn_down': 'delta_w', 'delta_norm_post_ffn': 'delta_w', 'new_m_meta_tokens': 'new_m', 'new_m_norm_pre_mix': 'new_m', 'new_m_w_in': 'new_m', 'new_m_ssm_conv_w': 'new_m', 'new_m_ssm_conv_b': 'new_m', 'new_m_ssm_dt_bias': 'new_m', 'new_m_ssm_a_log': 'new_m', 'new_m_ssm_d_skip': 'new_m', 'new_m_ssm_norm': 'new_m', 'new_m_w_ssm_out': 'new_m', 'new_m_attn_sinks': 'new_m', 'new_m_w_attn_out': 'new_m', 'new_m_w_mix_out': 'new_m', 'new_m_norm_post_mix': 'new_m', 'new_m_norm_pre_ffn': 'new_m', 'new_m_w_ffn_up': 'new_m', 'new_m_ffn_conv_w': 'new_m', 'new_m_ffn_conv_b': 'new_m', 'new_m_w_ffn_down': 'new_m', 'new_m_norm_post_ffn': 'new_m', 'new_v_meta_tokens': 'new_v', 'new_v_norm_pre_mix': 'new_v', 'new_v_w_in': 'new_v', 'new_v_ssm_conv_w': 'new_v', 'new_v_ssm_conv_b': 'new_v', 'new_v_ssm_dt_bias': 'new_v', 'new_v_ssm_a_log': 'new_v', 'new_v_ssm_d_skip': 'new_v', 'new_v_ssm_norm': 'new_v', 'new_v_w_ssm_out': 'new_v', 'new_v_attn_sinks': 'new_v', 'new_v_w_attn_out': 'new_v', 'new_v_w_mix_out': 'new_v', 'new_v_norm_post_mix': 'new_v', 'new_v_norm_pre_ffn': 'new_v', 'new_v_w_ffn_up': 'new_v', 'new_v_ffn_conv_w': 'new_v', 'new_v_ffn_conv_b': 'new_v', 'new_v_w_ffn_down': 'new_v', 'new_v_norm_post_ffn': 'new_v'}


def _forward(args):
    return _fwd_reference(*[args[k] for k in FWD_PARAMS])


def _output_shape():
    out = _jax.eval_shape(lambda: _forward(_fwd_setup_inputs(0)))
    return out.shape, out.dtype

N_MICROBATCH = 1
ADAM_LR = 0.001
ADAM_B1 = 0.9
ADAM_B2 = 0.999
ADAM_EPS = 1e-08
ADAM_WD = 0.01
ADAM_STEP = 10
PER_EXAMPLE_BATCH_AXIS = {'x': 0, 'loss_target': 0}
SHARED_INPUTS = []
_WEIGHT_DTYPES = {'meta_tokens': _jnp.float32, 'norm_pre_mix': _jnp.float32, 'w_in': _jnp.float32, 'ssm_conv_w': _jnp.float32, 'ssm_conv_b': _jnp.float32, 'ssm_dt_bias': _jnp.float32, 'ssm_a_log': _jnp.float32, 'ssm_d_skip': _jnp.float32, 'ssm_norm': _jnp.float32, 'w_ssm_out': _jnp.float32, 'attn_sinks': _jnp.float32, 'w_attn_out': _jnp.float32, 'w_mix_out': _jnp.float32, 'norm_post_mix': _jnp.float32, 'norm_pre_ffn': _jnp.float32, 'w_ffn_up': _jnp.float32, 'ffn_conv_w': _jnp.float32, 'ffn_conv_b': _jnp.float32, 'w_ffn_down': _jnp.float32, 'norm_post_ffn': _jnp.float32}
MOMENT_SCALE = {'meta_tokens': 4.136085e-02, 'norm_pre_mix': 7.555875e-01, 'w_in': 2.400926e-01, 'ssm_conv_w': 5.332371e-01, 'ssm_conv_b': 8.032234e-01, 'ssm_dt_bias': 6.119962e-01, 'ssm_a_log': 2.613038e-01, 'ssm_d_skip': 2.562723e+00, 'ssm_norm': 3.151314e-01, 'w_ssm_out': 4.626576e-01, 'attn_sinks': 1.251271e-01, 'w_attn_out': 1.604501e-01, 'w_mix_out': 4.918307e-01, 'norm_post_mix': 3.185929e+01, 'norm_pre_ffn': 4.488401e-01, 'w_ffn_up': 2.054500e-01, 'ffn_conv_w': 4.254274e-01, 'ffn_conv_b': 1.199919e+00, 'w_ffn_down': 3.524340e-01, 'norm_post_ffn': 3.183847e+01}


def _to_microbatches(a, axis):
    t = _jnp.moveaxis(a, axis, 0)
    t = t.reshape((N_MICROBATCH, t.shape[0] // N_MICROBATCH) + t.shape[1:])
    return _jnp.moveaxis(t, 1, axis + 1)


def setup_inputs(seed: int = 0) -> dict:
    inp = _fwd_setup_inputs(seed)
    key = _jax.random.fold_in(_jax.random.key(seed), 7919)
    shape, _ = _output_shape()
    out = dict(inp)
    out["loss_target"] = _jax.random.normal(_jax.random.fold_in(key, 0), shape, _jnp.float32)
    for i, name in enumerate(TWIN_WEIGHTS):
        w = inp[name].astype(_jnp.float32)
        if MOMENT_SCALE is None:
            s = _jnp.sqrt(_jnp.mean(_jnp.square(w)) + 1e-30)
        else:
            s = MOMENT_SCALE[name]
        km, kv = _jax.random.split(_jax.random.fold_in(key, i + 1))
        out[name] = w
        out["m_" + name] = s * _jax.random.normal(km, w.shape, _jnp.float32)
        out["v_" + name] = (s * s) * _jax.random.uniform(kv, w.shape, _jnp.float32, 0.5, 1.5)
    if N_MICROBATCH > 1:
        for name, axis in PER_EXAMPLE_BATCH_AXIS.items():
            out[name] = _to_microbatches(out[name], axis)
    return {'x': out['x'], 'meta_tokens': out['meta_tokens'], 'norm_pre_mix': out['norm_pre_mix'], 'w_in': out['w_in'], 'ssm_conv_w': out['ssm_conv_w'], 'ssm_conv_b': out['ssm_conv_b'], 'ssm_dt_bias': out['ssm_dt_bias'], 'ssm_a_log': out['ssm_a_log'], 'ssm_d_skip': out['ssm_d_skip'], 'ssm_norm': out['ssm_norm'], 'w_ssm_out': out['w_ssm_out'], 'attn_sinks': out['attn_sinks'], 'w_attn_out': out['w_attn_out'], 'w_mix_out': out['w_mix_out'], 'norm_post_mix': out['norm_post_mix'], 'norm_pre_ffn': out['norm_pre_ffn'], 'w_ffn_up': out['w_ffn_up'], 'ffn_conv_w': out['ffn_conv_w'], 'ffn_conv_b': out['ffn_conv_b'], 'w_ffn_down': out['w_ffn_down'], 'norm_post_ffn': out['norm_post_ffn'], 'loss_target': out['loss_target'], 'm_meta_tokens': out['m_meta_tokens'], 'm_norm_pre_mix': out['m_norm_pre_mix'], 'm_w_in': out['m_w_in'], 'm_ssm_conv_w': out['m_ssm_conv_w'], 'm_ssm_conv_b': out['m_ssm_conv_b'], 'm_ssm_dt_bias': out['m_ssm_dt_bias'], 'm_ssm_a_log': out['m_ssm_a_log'], 'm_ssm_d_skip': out['m_ssm_d_skip'], 'm_ssm_norm': out['m_ssm_norm'], 'm_w_ssm_out': out['m_w_ssm_out'], 'm_attn_sinks': out['m_attn_sinks'], 'm_w_attn_out': out['m_w_attn_out'], 'm_w_mix_out': out['m_w_mix_out'], 'm_norm_post_mix': out['m_norm_post_mix'], 'm_norm_pre_ffn': out['m_norm_pre_ffn'], 'm_w_ffn_up': out['m_w_ffn_up'], 'm_ffn_conv_w': out['m_ffn_conv_w'], 'm_ffn_conv_b': out['m_ffn_conv_b'], 'm_w_ffn_down': out['m_w_ffn_down'], 'm_norm_post_ffn': out['m_norm_post_ffn'], 'v_meta_tokens': out['v_meta_tokens'], 'v_norm_pre_mix': out['v_norm_pre_mix'], 'v_w_in': out['v_w_in'], 'v_ssm_conv_w': out['v_ssm_conv_w'], 'v_ssm_conv_b': out['v_ssm_conv_b'], 'v_ssm_dt_bias': out['v_ssm_dt_bias'], 'v_ssm_a_log': out['v_ssm_a_log'], 'v_ssm_d_skip': out['v_ssm_d_skip'], 'v_ssm_norm': out['v_ssm_norm'], 'v_w_ssm_out': out['v_w_ssm_out'], 'v_attn_sinks': out['v_attn_sinks'], 'v_w_attn_out': out['v_w_attn_out'], 'v_w_mix_out': out['v_w_mix_out'], 'v_norm_post_mix': out['v_norm_post_mix'], 'v_norm_pre_ffn': out['v_norm_pre_ffn'], 'v_w_ffn_up': out['v_w_ffn_up'], 'v_ffn_conv_w': out['v_ffn_conv_w'], 'v_ffn_conv_b': out['v_ffn_conv_b'], 'v_w_ffn_down': out['v_w_ffn_down'], 'v_norm_post_ffn': out['v_norm_post_ffn']}


def _loss(weights, diff, rest, loss_target):
    with _jax.named_scope("forward"):
        args = {**rest, TWIN_DIFF_INPUT: diff, **{k: w.astype(_WEIGHT_DTYPES[k]) for k, w in weights.items()}}
        y = _forward(args)
    with _jax.named_scope("loss_head"):
        err = _jnp.square(y.astype(_jnp.float32) - loss_target)
        return 0.5 * _jnp.sum(_jnp.mean(err, axis=-1)) if err.ndim else 0.5 * err


def _adamw(w, g, m, v):
    m = ADAM_B1 * m + (1.0 - ADAM_B1) * g
    v = ADAM_B2 * v + (1.0 - ADAM_B2) * _jnp.square(g)
    m_hat = m / (1.0 - ADAM_B1 ** ADAM_STEP)
    v_hat = v / (1.0 - ADAM_B2 ** ADAM_STEP)
    delta = -ADAM_LR * (m_hat / (_jnp.sqrt(v_hat) + ADAM_EPS) + ADAM_WD * w)
    return delta, m, v


def reference(x, meta_tokens, norm_pre_mix, w_in, ssm_conv_w, ssm_conv_b, ssm_dt_bias, ssm_a_log, ssm_d_skip, ssm_norm, w_ssm_out, attn_sinks, w_attn_out, w_mix_out, norm_post_mix, norm_pre_ffn, w_ffn_up, ffn_conv_w, ffn_conv_b, w_ffn_down, norm_post_ffn, loss_target, m_meta_tokens, m_norm_pre_mix, m_w_in, m_ssm_conv_w, m_ssm_conv_b, m_ssm_dt_bias, m_ssm_a_log, m_ssm_d_skip, m_ssm_norm, m_w_ssm_out, m_attn_sinks, m_w_attn_out, m_w_mix_out, m_norm_post_mix, m_norm_pre_ffn, m_w_ffn_up, m_ffn_conv_w, m_ffn_conv_b, m_w_ffn_down, m_norm_post_ffn, v_meta_tokens, v_norm_pre_mix, v_w_in, v_ssm_conv_w, v_ssm_conv_b, v_ssm_dt_bias, v_ssm_a_log, v_ssm_d_skip, v_ssm_norm, v_w_ssm_out, v_attn_sinks, v_w_attn_out, v_w_mix_out, v_norm_post_mix, v_norm_pre_ffn, v_w_ffn_up, v_ffn_conv_w, v_ffn_conv_b, v_w_ffn_down, v_norm_post_ffn):
    given = dict(x=x, meta_tokens=meta_tokens, norm_pre_mix=norm_pre_mix, w_in=w_in, ssm_conv_w=ssm_conv_w, ssm_conv_b=ssm_conv_b, ssm_dt_bias=ssm_dt_bias, ssm_a_log=ssm_a_log, ssm_d_skip=ssm_d_skip, ssm_norm=ssm_norm, w_ssm_out=w_ssm_out, attn_sinks=attn_sinks, w_attn_out=w_attn_out, w_mix_out=w_mix_out, norm_post_mix=norm_post_mix, norm_pre_ffn=norm_pre_ffn, w_ffn_up=w_ffn_up, ffn_conv_w=ffn_conv_w, ffn_conv_b=ffn_conv_b, w_ffn_down=w_ffn_down, norm_post_ffn=norm_post_ffn, loss_target=loss_target, m_meta_tokens=m_meta_tokens, m_norm_pre_mix=m_norm_pre_mix, m_w_in=m_w_in, m_ssm_conv_w=m_ssm_conv_w, m_ssm_conv_b=m_ssm_conv_b, m_ssm_dt_bias=m_ssm_dt_bias, m_ssm_a_log=m_ssm_a_log, m_ssm_d_skip=m_ssm_d_skip, m_ssm_norm=m_ssm_norm, m_w_ssm_out=m_w_ssm_out, m_attn_sinks=m_attn_sinks, m_w_attn_out=m_w_attn_out, m_w_mix_out=m_w_mix_out, m_norm_post_mix=m_norm_post_mix, m_norm_pre_ffn=m_norm_pre_ffn, m_w_ffn_up=m_w_ffn_up, m_ffn_conv_w=m_ffn_conv_w, m_ffn_conv_b=m_ffn_conv_b, m_w_ffn_down=m_w_ffn_down, m_norm_post_ffn=m_norm_post_ffn, v_meta_tokens=v_meta_tokens, v_norm_pre_mix=v_norm_pre_mix, v_w_in=v_w_in, v_ssm_conv_w=v_ssm_conv_w, v_ssm_conv_b=v_ssm_conv_b, v_ssm_dt_bias=v_ssm_dt_bias, v_ssm_a_log=v_ssm_a_log, v_ssm_d_skip=v_ssm_d_skip, v_ssm_norm=v_ssm_norm, v_w_ssm_out=v_w_ssm_out, v_attn_sinks=v_attn_sinks, v_w_attn_out=v_w_attn_out, v_w_mix_out=v_w_mix_out, v_norm_post_mix=v_norm_post_mix, v_norm_pre_ffn=v_norm_pre_ffn, v_w_ffn_up=v_w_ffn_up, v_ffn_conv_w=v_ffn_conv_w, v_ffn_conv_b=v_ffn_conv_b, v_w_ffn_down=v_w_ffn_down, v_norm_post_ffn=v_norm_post_ffn)
    weights = {n: given[n] for n in TWIN_WEIGHTS}
    shared = {n: given[n] for n in SHARED_INPUTS}
    per_example = {n: given[n] for n in ['x']}
    grad_fn = _jax.value_and_grad(_loss, argnums=(0, 1))

    def one_microbatch(ex, loss_target):
        ex = dict(ex)
        diff = ex.pop(TWIN_DIFF_INPUT)
        return grad_fn(weights, diff, {**shared, **ex}, loss_target)

    if N_MICROBATCH == 1:
        loss, (grad_w, grad_x) = one_microbatch(per_example, given["loss_target"])
    else:
        def body(carry, xs):
            loss_sum, grad_sum = carry
            l_k, (gw_k, gx_k) = one_microbatch(xs[0], xs[1])
            with _jax.named_scope("update"):
                return (loss_sum + l_k, _jax.tree.map(_jnp.add, grad_sum, gw_k)), gx_k

        init = (_jnp.zeros((), _jnp.float32), _jax.tree.map(_jnp.zeros_like, weights))
        (loss, grad_w), grad_x = _jax.lax.scan(body, init, (per_example, given["loss_target"]))
    with _jax.named_scope("update"):
        delta_w, new_m, new_v = {}, {}, {}
        for n in TWIN_WEIGHTS:
            delta_w[n], new_m[n], new_v[n] = _adamw(weights[n], grad_w[n], given["m_" + n], given["v_" + n])
    return (loss, grad_x, *[grad_w[n] for n in TWIN_WEIGHTS], *[delta_w[n] for n in TWIN_WEIGHTS],
            *[new_m[n] for n in TWIN_WEIGHTS], *[new_v[n] for n in TWIN_WEIGHTS])
```

```python
import functools
import math

import jax
import jax.numpy as jnp
from jax import lax
from jax.experimental import pallas as pl
from jax.experimental.pallas import tpu as pltpu

F32 = jnp.float32
BF16 = jnp.bfloat16

D_MODEL = 1024
N_META = 16
T = 128
PAD = T - N_META
D_INNER = 2048
SSM_HEADS = 32
HEAD_P = 64
SSM_GROUPS = 4
GROUP_W = D_INNER // SSM_GROUPS
D_STATE = 128
CONV_DIM = D_INNER + 2 * SSM_GROUPS * D_STATE
ATTN_HEADS = 16
KV_HEADS = 4
ATTN_W = 1024
KV_W = 256
FFN_DIM = 2816
N_IN = 8736
EPS = 1e-6
NEG = -1e30
SCALE = 0.125

P_Z, P_GATE, P_XBC, P_Q, P_K, P_V, P_DT = 0, 2048, 4096, 7168, 8192, 8448, 8704
P_W = 8832

ADAM_LR, ADAM_B1, ADAM_B2, ADAM_EPS, ADAM_WD, ADAM_STEP = 0.001, 0.9, 0.999, 1e-08, 0.01, 10

VMEM_BUDGET = 40 * 1024 * 1024
VMEM_LIMIT = 56 * 1024 * 1024
MESH = pl.DeviceIdType.MESH


def _cparams(n_axes, **kw):
    return pltpu.CompilerParams(dimension_semantics=("arbitrary",) * n_axes, vmem_limit_bytes=VMEM_LIMIT, **kw)


def _sigmoid(x):
    return 1.0 / (1.0 + jnp.exp(-x))


def _silu(x):
    return x * _sigmoid(x)


def _dsilu(x):
    s = _sigmoid(x)
    return s * (1.0 + x * (1.0 - s))


def _softplus(x):
    e = jnp.exp(-jnp.abs(x))
    small = e * (1.0 - e * (0.5 - e * (1.0 / 3.0)))
    return jnp.maximum(x, 0.0) + jnp.where(e < 0.01, small, jnp.log(1.0 + e))


def _rms(x, w):
    r = lax.rsqrt(jnp.mean(x * x, axis=-1, keepdims=True) + EPS)
    return x * r * w


def _rms_bwd(dy, x, w):
    r = lax.rsqrt(jnp.mean(x * x, axis=-1, keepdims=True) + EPS)
    xh = x * r
    g = dy * w
    dx = r * (g - xh * jnp.mean(g * xh, axis=-1, keepdims=True))
    dw = jnp.sum(dy * xh, axis=0, keepdims=True)
    return dx, dw


def _dot(a, b):
    return jnp.dot(a, b, preferred_element_type=F32)


def _dot_nt(a, b):
    return lax.dot_general(a, b, (((1,), (1,)), ((), ())), preferred_element_type=F32)


def _dot_tn(a, b):
    return lax.dot_general(a, b, (((0,), (0,)), ((), ())), preferred_element_type=F32)


def _split3(x):
    hi = x.astype(BF16)
    r = x - hi.astype(F32)
    mid = r.astype(BF16)
    lo = (r - mid.astype(F32)).astype(BF16)
    return hi, mid, lo


def _xdot(x, e):
    hi, mid, lo = _split3(x)
    return _dot(hi, e) + _dot(mid, e) + _dot(lo, e)


def _xdot_l(e, x):
    hi, mid, lo = _split3(x)
    return _dot(e, hi) + _dot(e, mid) + _dot(e, lo)


def _iota(shape, dim):
    return lax.broadcasted_iota(jnp.int32, shape, dim)


def _divisors(n, unit):
    return [t for t in range(unit, n + 1, unit) if n % t == 0]


def _matmul_tiles(m, n, k, a_bytes, b_bytes, o_bytes, m_unit):
    best = None
    for tm in _divisors(m, m_unit):
        for tn in _divisors(n, 128):
            for tk in _divisors(k, 128):
                acc = 0 if tk == k else tm * tn * 4
                vm = 2 * (tm * tk * a_bytes + tk * tn * b_bytes + tm * tn * o_bytes) + acc
                if vm > VMEM_BUDGET:
                    continue
                score = (tm * tn * tk, tk)
                if best is None or score > best[0]:
                    best = (score, (tm, tn, tk))
    return best[1]


def _matmul(name, a, b, mode, out_dtype):
    if mode == "nn":
        (m, k), n = a.shape, b.shape[1]
    elif mode == "nt":
        (m, k), n = a.shape, b.shape[0]
    else:
        (k, m), n = a.shape, b.shape[1]
    ab, bb, ob = a.dtype.itemsize, b.dtype.itemsize, jnp.dtype(out_dtype).itemsize
    tm, tn, tk = _matmul_tiles(m, n, k, ab, bb, ob, 128 if mode == "tn" else 16)
    nk = k // tk
    dot = {"nn": _dot, "nt": _dot_nt, "tn": _dot_tn}[mode]

    def body(a_ref, b_ref, o_ref, *scratch):
        prod = dot(a_ref[...].astype(BF16), b_ref[...].astype(BF16))
        if nk == 1:
            o_ref[...] = prod.astype(o_ref.dtype)
        else:
            acc_ref, = scratch
            kk = pl.program_id(2)

            @pl.when(kk == 0)
            def _():
                acc_ref[...] = prod

            @pl.when(kk > 0)
            def _():
                acc_ref[...] += prod

            @pl.when(kk == nk - 1)
            def _():
                o_ref[...] = acc_ref[...].astype(o_ref.dtype)

    a_spec = pl.BlockSpec((tk, tm), lambda i, j, kk: (kk, i)) if mode == "tn" else pl.BlockSpec((tm, tk), lambda i, j, kk: (i, kk))
    b_spec = pl.BlockSpec((tn, tk), lambda i, j, kk: (j, kk)) if mode == "nt" else pl.BlockSpec((tk, tn), lambda i, j, kk: (kk, j))
    return pl.pallas_call(
        body, name=name, grid=(m // tm, n // tn, nk),
        in_specs=[a_spec, b_spec], out_specs=pl.BlockSpec((tm, tn), lambda i, j, kk: (i, j)),
        out_shape=jax.ShapeDtypeStruct((m, n), out_dtype),
        scratch_shapes=[] if nk == 1 else [pltpu.VMEM((tm, tn), F32)],
        compiler_params=_cparams(3),
    )(a, b)


def _row_tile(n_rows, cap, unit=16):
    return max(t for t in _divisors(n_rows, unit) if t <= cap)


def _rowwise(name, fn, n_rows, tm, row_ins, full_ins, row_outs, acc_outs):
    n_in = len(row_ins) + len(full_ins)
    n_ro = len(row_outs)

    def body(*refs):
        i = pl.program_id(0)
        res = fn(i * tm, *[r[...] for r in refs[:n_in]])
        outs = refs[n_in:]
        for r, v in zip(outs[:n_ro], res[:n_ro]):
            r[...] = v.astype(r.dtype)

        @pl.when(i == 0)
        def _():
            for r, v in zip(outs[n_ro:], res[n_ro:]):
                r[...] = v

        @pl.when(i > 0)
        def _():
            for r, v in zip(outs[n_ro:], res[n_ro:]):
                r[...] += v

    in_specs = [pl.BlockSpec((tm, w), functools.partial(lambda i, cb: (i, cb), cb=cb)) for _, w, cb in row_ins]
    in_specs += [pl.BlockSpec(a.shape, lambda i: (0, 0)) for a in full_ins]
    out_specs = [pl.BlockSpec((tm, w), lambda i: (i, 0)) for w, _ in row_outs]
    out_specs += [pl.BlockSpec((1, w), lambda i: (0, 0)) for w in acc_outs]
    out_shape = [jax.ShapeDtypeStruct((n_rows, w), dt) for w, dt in row_outs]
    out_shape += [jax.ShapeDtypeStruct((1, w), F32) for w in acc_outs]
    return pl.pallas_call(
        body, name=name, grid=(n_rows // tm,), in_specs=in_specs, out_specs=out_specs, out_shape=out_shape,
        compiler_params=_cparams(1),
    )(*[a for a, _, _ in row_ins], *full_ins)


def _valid_rows(first_row, tm, lo):
    return (first_row + _iota((tm, 1), 0)) >= lo


def _halo_specs(tm, tc, col_off, n_tiles):
    r8 = tm // 8
    cur = pl.BlockSpec((tm, tc), lambda i, j: (i, j + col_off))
    prev = pl.BlockSpec((8, tc), lambda i, j: (jnp.maximum(i * r8 - 1, 0), j + col_off))
    nxt = pl.BlockSpec((8, tc), lambda i, j: (jnp.minimum((i + 1) * r8, n_tiles * r8 - 1), j + col_off))
    return cur, prev, nxt


def _ffn_act(name, u_raw, conv_w, conv_b, n_rows):
    tm = _row_tile(n_rows, 384)
    tc = 1408
    nj = FFN_DIM // tc
    taps = conv_w.shape[0]

    def body(ac_ref, ap_ref, gc_ref, gp_ref, wa_ref, wg_ref, ba_ref, bg_ref, f_ref, ext_ref):
        i = pl.program_id(0)

        def conv(cur_ref, prev_ref, w_ref, b_ref):
            ext_ref[0:8, :] = jnp.where(i > 0, prev_ref[...], 0.0)
            ext_ref[8:8 + tm, :] = cur_ref[...]
            w = w_ref[...]
            acc = b_ref[...] + w[taps - 1:taps] * cur_ref[...]
            for k in range(taps - 1):
                acc = acc + w[k:k + 1] * ext_ref[pl.ds(8 - (taps - 1) + k, tm), :]
            return acc

        a = conv(ac_ref, ap_ref, wa_ref, ba_ref)
        g = conv(gc_ref, gp_ref, wg_ref, bg_ref)
        f = jnp.where(_valid_rows(i * tm, tm, PAD), _silu(a) * g, 0.0)
        f_ref[...] = f.astype(f_ref.dtype)

    a_cur, a_prev, _ = _halo_specs(tm, tc, 0, n_rows // tm)
    g_cur, g_prev, _ = _halo_specs(tm, tc, nj, n_rows // tm)
    wspec = lambda off: pl.BlockSpec((taps, tc), lambda i, j: (0, j + off))
    bspec = lambda off: pl.BlockSpec((1, tc), lambda i, j: (0, j + off))
    return pl.pallas_call(
        body, name=name, grid=(n_rows // tm, nj),
        in_specs=[a_cur, a_prev, g_cur, g_prev, wspec(0), wspec(nj), bspec(0), bspec(nj)],
        out_specs=pl.BlockSpec((tm, tc), lambda i, j: (i, j)),
        out_shape=jax.ShapeDtypeStruct((n_rows, FFN_DIM), BF16),
        scratch_shapes=[pltpu.VMEM((tm + 8, tc), F32)],
        compiler_params=_cparams(2),
    )(u_raw, u_raw, u_raw, u_raw, conv_w, conv_w, conv_b, conv_b)


def _conv_act_bwd(name, raw, raw_col0, dact, conv_w, conv_b, n_rows, gated):
    taps = conv_w.shape[0]
    width = conv_w.shape[1] // (2 if gated else 1)
    tm = _row_tile(n_rows, 384)
    tc = 1408 if gated else 512
    nj = width // tc
    n_tiles = n_rows // tm
    co = raw_col0 // tc
    nh = 2 if gated else 1

    def body(*refs):
        i = pl.program_id(1)
        raws = [refs[3 * h:3 * h + 3] for h in range(nh)]
        dc_ref, dn_ref = refs[3 * nh:3 * nh + 2]
        ws = refs[3 * nh + 2:3 * nh + 2 + nh]
        bs = refs[3 * nh + 2 + nh:3 * nh + 2 + 2 * nh]
        outs = refs[3 * nh + 2 + 2 * nh:-2]
        ext_ref, dext_ref = refs[-2:]
        te = tm + 8
        row = i * tm + _iota((te, 1), 0)
        live = (row >= PAD) & (row < n_rows)

        pre = []
        for h in range(nh):
            cur_ref, prev_ref, next_ref = raws[h]
            ext_ref[h, 0:8, :] = jnp.where(i > 0, prev_ref[...], 0.0)
            ext_ref[h, 8:8 + tm, :] = cur_ref[...]
            ext_ref[h, 8 + tm:16 + tm, :] = next_ref[...]
            w = ws[h][...]
            acc = bs[h][...] + jnp.zeros((te, tc), F32)
            for k in range(taps):
                acc = acc + w[k:k + 1] * ext_ref[h, pl.ds(8 - (taps - 1) + k, te), :]
            pre.append(acc)
        d_ext = jnp.concatenate([dc_ref[...], dn_ref[...]], axis=0).astype(F32)
        if gated:
            dpre = [d_ext * pre[1] * _dsilu(pre[0]), d_ext * _silu(pre[0])]
        else:
            dpre = [d_ext * _dsilu(pre[0])]
        for h in range(nh):
            du = jnp.where(live, dpre[h], 0.0)
            dext_ref[...] = du
            w = ws[h][...]
            draw = jnp.zeros((tm, tc), F32)
            for k in range(taps):
                draw = draw + w[k:k + 1] * dext_ref[pl.ds(taps - 1 - k, tm), :]
            draw = jnp.where(_valid_rows(i * tm, tm, PAD), draw, 0.0)
            outs[h][...] = draw.astype(outs[h].dtype)
            du_cur = du[0:tm]
            dw_rows = [jnp.sum(du_cur * ext_ref[h, pl.ds(8 - (taps - 1) + k, tm), :], axis=0, keepdims=True)
                       for k in range(taps)]
            dw_rows.append(jnp.sum(du_cur, axis=0, keepdims=True))
            dw_rows.append(jnp.zeros((8 - taps - 1, tc), F32))
            dwb = jnp.concatenate(dw_rows, axis=0)
            acc_ref = outs[nh + h]

            @pl.when(i == 0)
            def _():
                acc_ref[...] = dwb

            @pl.when(i > 0)
            def _():
                acc_ref[...] += dwb

    in_specs, operands = [], []
    for h in range(nh):
        in_specs += list(_halo_specs(tm, tc, co + h * nj, n_tiles))
        operands += [raw, raw, raw]
    d_cur, _, d_next = _halo_specs(tm, tc, 0, n_tiles)
    in_specs += [d_cur, d_next]
    operands += [dact, dact]
    for h in range(nh):
        in_specs.append(pl.BlockSpec((taps, tc), functools.partial(lambda i, j, off: (0, j + off), off=h * nj)))
        operands.append(conv_w)
    for h in range(nh):
        in_specs.append(pl.BlockSpec((1, tc), functools.partial(lambda i, j, off: (0, j + off), off=h * nj)))
        operands.append(conv_b)
    out_specs = [pl.BlockSpec((tm, tc), lambda i, j: (i, j)) for _ in range(nh)]
    out_specs += [pl.BlockSpec((8, tc), lambda i, j: (0, j)) for _ in range(nh)]
    out_shape = [jax.ShapeDtypeStruct((n_rows, width), BF16) for _ in range(nh)]
    out_shape += [jax.ShapeDtypeStruct((8, width), F32) for _ in range(nh)]
    swap = lambda spec: pl.BlockSpec(spec.block_shape, functools.partial(lambda j, i, f: f(i, j), f=spec.index_map))
    return pl.pallas_call(
        body, name=name, grid=(nj, n_tiles),
        in_specs=[swap(s) for s in in_specs], out_specs=[swap(s) for s in out_specs], out_shape=out_shape,
        scratch_shapes=[pltpu.VMEM((nh, tm + 16, tc), F32), pltpu.VMEM((tm + 8, tc), F32)],
        compiler_params=_cparams(2),
    )(*operands)


def _ssd_specs(n_chunks, rev):
    cidx = (lambda c: n_chunks - 1 - c) if rev else (lambda c: c)
    xg0, bg0, cg0 = P_XBC // GROUP_W, (P_XBC + D_INNER) // D_STATE, (P_XBC + D_INNER + SSM_GROUPS * D_STATE) // D_STATE

    def cur(width, blk0):
        return pl.BlockSpec((T, width), lambda g, c: (cidx(c), blk0 + g))

    def prev(width, blk0):
        return pl.BlockSpec((8, width), lambda g, c: (jnp.maximum(cidx(c) * (T // 8) - 1, 0), blk0 + g))

    specs = [cur(GROUP_W, xg0), prev(GROUP_W, xg0), cur(D_STATE, bg0), prev(D_STATE, bg0),
             cur(D_STATE, cg0), prev(D_STATE, cg0),
             pl.BlockSpec((T, 128), lambda g, c: (cidx(c), P_DT // 128))]
    wx, wb, wc = 0, D_INNER // D_STATE, (D_INNER + SSM_GROUPS * D_STATE) // D_STATE
    specs += [pl.BlockSpec((4, GROUP_W), lambda g, c: (0, g)),
              pl.BlockSpec((4, D_STATE), lambda g, c: (0, wb + g)),
              pl.BlockSpec((4, D_STATE), lambda g, c: (0, wc + g)),
              pl.BlockSpec((1, GROUP_W), lambda g, c: (0, g)),
              pl.BlockSpec((1, D_STATE), lambda g, c: (0, wb + g)),
              pl.BlockSpec((1, D_STATE), lambda g, c: (0, wc + g))]
    specs += [pl.BlockSpec((1, 128), lambda g, c: (0, 0))] * 3
    return specs, cidx


def _ssd_chunk_forward(refs, ext_ref, g, c):
    (xc_ref, xp_ref, bc_ref, bp_ref, cc_ref, cp_ref, dt_ref, wx_ref, wb_ref, wc_ref,
     bx_ref, bb_ref, bcb_ref, dtb_ref, alog_ref, dsk_ref) = refs

    def conv_pre(cur_ref, prev_ref, w_ref, b_ref, width):
        ext_ref[0:8, 0:width] = jnp.where(c > 0, prev_ref[...], 0.0)
        ext_ref[8:8 + T, 0:width] = cur_ref[...]
        w = w_ref[...]
        acc = b_ref[...] + w[3:4] * cur_ref[...]
        for k in range(3):
            acc = acc + w[k:k + 1] * ext_ref[pl.ds(5 + k, T), 0:width]
        return acc

    valid = _valid_rows(c * T, T, PAD)
    v = {}
    v["valid"] = valid
    v["x_pre"] = conv_pre(xc_ref, xp_ref, wx_ref, bx_ref, GROUP_W)
    v["b_pre"] = conv_pre(bc_ref, bp_ref, wb_ref, bb_ref, D_STATE)
    v["c_pre"] = conv_pre(cc_ref, cp_ref, wc_ref, bcb_ref, D_STATE)
    xs = _silu(v["x_pre"])
    bm = jnp.where(valid, _silu(v["b_pre"]), 0.0)
    cm = jnp.where(valid, _silu(v["c_pre"]), 0.0)
    dtr = dt_ref[...] + dtb_ref[...]
    dt = jnp.where(valid, _softplus(dtr), 0.0)
    a_neg = -jnp.exp(alog_ref[...])
    a = dt * a_neg
    tril = _iota((T, T), 0) >= _iota((T, T), 1)
    cs = _xdot_l(tril.astype(BF16), a)
    hh, ll = _iota((128, GROUP_W), 0), _iota((128, GROUP_W), 1)
    expand = (hh == 8 * g + jnp.right_shift(ll, 6)).astype(BF16)
    sh, sj = _iota((128, 128), 0), _iota((128, 128), 1)
    select = ((sh == 8 * g + sj) & (sj < 8)).astype(BF16)
    hh_t, ll_t = _iota((GROUP_W, 128), 1), _iota((GROUP_W, 128), 0)
    v["expand_t"] = (hh_t == 8 * g + jnp.right_shift(ll_t, 6)).astype(BF16)
    v["select_t"] = ((sj == 8 * g + sh) & (sh < 8)).astype(BF16)
    cs_e = _xdot(cs, expand)
    dt_e = _xdot(dt, expand)
    cs_loc = _xdot(cs, select)
    cs_loc_t = cs_loc.T
    cs_last_e = cs_e[T - 1:T, :]
    v.update(xs=xs, bm=bm, cm=cm, dtr=dtr, dt=dt, a_neg=a_neg, tril=tril, expand=expand, select=select,
             cs_e=cs_e, dt_e=dt_e, cs_loc=cs_loc, cs_loc_t=cs_loc_t, cs_last_e=cs_last_e)
    v["xdt"] = xs * dt_e
    v["decay_e"] = jnp.exp(cs_last_e - cs_e)
    v["ecs_e"] = jnp.exp(cs_e)
    v["elast_e"] = jnp.exp(cs_last_e)
    v["d_e"] = _xdot(dsk_ref[...], expand)
    v["gmat"] = _dot_nt(cm.astype(BF16), bm.astype(BF16))
    return v


def _ssd_decay_pair(v, jp):
    out = []
    for j in (2 * jp, 2 * jp + 1):
        diff = v["cs_loc"][:, j:j + 1] - v["cs_loc_t"][j:j + 1, :]
        out.append(jnp.where(v["tril"], jnp.exp(jnp.where(v["tril"], diff, 0.0)), 0.0))
    return out


def _block_diag_pair(xp):
    lane = _iota(xp.shape, 1)
    return jnp.concatenate([jnp.where(lane < HEAD_P, xp, 0.0), jnp.where(lane >= HEAD_P, xp, 0.0)], axis=0)


def _ssd_fwd(p, conv_w, conv_b, dt_bias, a_log, d_skip, n_chunks):
    n_rows = n_chunks * T
    in_specs, _ = _ssd_specs(n_chunks, rev=False)

    def body(*refs):
        y_ref, hin_ref, st_ref, ext_ref = refs[16:]
        g, c = pl.program_id(0), pl.program_id(1)

        @pl.when(c == 0)
        def _():
            st_ref[...] = jnp.zeros_like(st_ref)

        v = _ssd_chunk_forward(refs[:16], ext_ref, g, c)
        state = st_ref[...]
        hin_ref[...] = state
        ys = []
        for jp in range(4):
            l0, l1 = _ssd_decay_pair(v, jp)
            lhs = jnp.concatenate([v["gmat"] * l0, v["gmat"] * l1], axis=1).astype(BF16)
            rhs = _block_diag_pair(v["xdt"][:, 128 * jp:128 * jp + 128]).astype(BF16)
            ys.append(_dot(lhs, rhs))
        y = jnp.concatenate(ys, axis=1)
        y = y + _dot(v["cm"].astype(BF16), state.astype(BF16)) * v["ecs_e"] + v["xs"] * v["d_e"]
        y_ref[...] = y
        s_new = _dot_tn(v["bm"].astype(BF16), (v["xdt"] * v["decay_e"]).astype(BF16))
        st_ref[...] = state * v["elast_e"] + s_new

    return pl.pallas_call(
        body, name="ssd_fwd", grid=(SSM_GROUPS, n_chunks), in_specs=in_specs,
        out_specs=[pl.BlockSpec((T, GROUP_W), lambda g, c: (c, g)),
                   pl.BlockSpec((None, None, D_STATE, GROUP_W), lambda g, c: (g, c, 0, 0))],
        out_shape=[jax.ShapeDtypeStruct((n_rows, D_INNER), F32),
                   jax.ShapeDtypeStruct((SSM_GROUPS, n_chunks, D_STATE, GROUP_W), F32)],
        scratch_shapes=[pltpu.VMEM((D_STATE, GROUP_W), F32), pltpu.VMEM((T + 8, GROUP_W), F32)],
        compiler_params=_cparams(2),
    )(p, p, p, p, p, p, p, conv_w, conv_w, conv_w, conv_b, conv_b, conv_b, dt_bias, a_log, d_skip)


def _ssd_bwd(p, conv_w, conv_b, dt_bias, a_log, d_skip, hin, dy, n_chunks):
    n_rows = n_chunks * T
    in_specs, cidx = _ssd_specs(n_chunks, rev=True)
    in_specs = in_specs + [pl.BlockSpec((None, None, D_STATE, GROUP_W), lambda g, c: (g, cidx(c), 0, 0)),
                           pl.BlockSpec((T, GROUP_W), lambda g, c: (cidx(c), g))]

    def body(*refs):
        hin_ref, dy_ref = refs[16:18]
        dx_ref, db_ref, dc_ref, ddt_ref, dpar_ref, dst_ref, ext_ref = refs[18:]
        g, step = pl.program_id(0), pl.program_id(1)
        c = n_chunks - 1 - step

        @pl.when(step == 0)
        def _():
            dst_ref[...] = jnp.zeros_like(dst_ref)

        v = _ssd_chunk_forward(refs[:16], ext_ref, g, c)
        hin_f = hin_ref[...]
        hin_b = hin_f.astype(BF16)
        dyv = dy_ref[...]
        dst = dst_ref[...]
        dst_b = dst.astype(BF16)
        xs, bm, cm, xdt = v["xs"], v["bm"], v["cm"], v["xdt"]
        bm_b, cm_b = bm.astype(BF16), cm.astype(BF16)

        dd_e = jnp.sum(dyv * xs, axis=0, keepdims=True)
        dxs = dyv * v["d_e"]
        ch = _dot(cm_b, hin_b)
        dch = (dyv * v["ecs_e"]).astype(BF16)
        dcm = _dot_nt(dch, hin_b)
        dhin = _dot_tn(cm_b, dch) + dst * v["elast_e"]
        dcs_e = dyv * ch * v["ecs_e"]
        dxd = _dot(bm_b, dst_b)
        dbm = _dot_nt((xdt * v["decay_e"]).astype(BF16), dst_b)
        dxdt_state = dxd * v["decay_e"]
        q = dxdt_state * xdt
        dcs_e = dcs_e - q
        dlast_e = jnp.sum(q, axis=0, keepdims=True) + jnp.sum(dst * hin_f, axis=0, keepdims=True) * v["elast_e"]
        dg = jnp.zeros((T, T), F32)
        rs_cols = jnp.zeros((T, 128), F32)
        cs_rows = jnp.zeros((128, T), F32)
        lane_i, sub_i = _iota((T, 128), 1), _iota((128, T), 0)
        dxdt_parts = []
        for jp in range(4):
            l0, l1 = _ssd_decay_pair(v, jp)
            m0, m1 = v["gmat"] * l0, v["gmat"] * l1
            xbd = _block_diag_pair(xdt[:, 128 * jp:128 * jp + 128]).astype(BF16)
            dyp = dyv[:, 128 * jp:128 * jp + 128]
            dm = _dot_nt(dyp.astype(BF16), xbd)
            dm0, dm1 = dm[:, 0:T], dm[:, T:2 * T]
            dg = dg + dm0 * l0 + dm1 * l1
            for j, qq in ((2 * jp, dm0 * m0), (2 * jp + 1, dm1 * m1)):
                rs_cols = jnp.where(lane_i == j, jnp.sum(qq, axis=1, keepdims=True), rs_cols)
                cs_rows = jnp.where(sub_i == j, jnp.sum(qq, axis=0, keepdims=True), cs_rows)
            mv = jnp.concatenate([m0, m1], axis=0).astype(BF16)
            dxdt_parts.append(_dot_tn(mv, _block_diag_pair(dyp).astype(BF16)))
        dxdt = jnp.concatenate(dxdt_parts, axis=1) + dxdt_state
        dg_b = dg.astype(BF16)
        dcm = dcm + _dot(dg_b, bm_b)
        dbm = dbm + _dot_tn(dg_b, cm_b)
        expand_t = v["expand_t"]
        dcs_loc = rs_cols - cs_rows.T
        last_row = _iota((T, 1), 0) == T - 1
        dcs_full_e = dcs_e + jnp.where(last_row, dlast_e, 0.0)
        dcs = _xdot(dcs_full_e, expand_t) + _xdot(dcs_loc, v["select_t"])
        triu = (_iota((T, T), 0) <= _iota((T, T), 1)).astype(BF16)
        da = _xdot_l(triu, dcs)
        ddt = da * v["a_neg"] + _xdot(dxdt * xs, expand_t)
        dxs = dxs + dxdt * v["dt_e"]
        ddtr = jnp.where(v["valid"], ddt * _sigmoid(v["dtr"]), 0.0)
        dx_ref[...] = dxs
        db_ref[...] = jnp.where(v["valid"], dbm, 0.0)
        dc_ref[...] = jnp.where(v["valid"], dcm, 0.0)
        ddt_ref[...] = ddtr
        dpar = jnp.concatenate([
            jnp.sum(ddtr, axis=0, keepdims=True),
            jnp.sum(da * v["dt"], axis=0, keepdims=True) * v["a_neg"],
            _xdot(dd_e, expand_t),
            jnp.zeros((5, 128), F32)], axis=0)

        @pl.when(step == 0)
        def _():
            dpar_ref[...] = dpar

        @pl.when(step > 0)
        def _():
            dpar_ref[...] += dpar

        dst_ref[...] = dhin

    return pl.pallas_call(
        body, name="ssd_bwd", grid=(SSM_GROUPS, n_chunks), in_specs=in_specs,
        out_specs=[pl.BlockSpec((T, GROUP_W), lambda g, c: (cidx(c), g)),
                   pl.BlockSpec((T, D_STATE), lambda g, c: (cidx(c), g)),
                   pl.BlockSpec((T, D_STATE), lambda g, c: (cidx(c), g)),
                   pl.BlockSpec((None, T, 128), lambda g, c: (g, cidx(c), 0)),
                   pl.BlockSpec((None, 8, 128), lambda g, c: (g, 0, 0))],
        out_shape=[jax.ShapeDtypeStruct((n_rows, D_INNER), F32),
                   jax.ShapeDtypeStruct((n_rows, SSM_GROUPS * D_STATE), F32),
                   jax.ShapeDtypeStruct((n_rows, SSM_GROUPS * D_STATE), F32),
                   jax.ShapeDtypeStruct((SSM_GROUPS, n_rows, 128), F32),
                   jax.ShapeDtypeStruct((SSM_GROUPS, 8, 128), F32)],
        scratch_shapes=[pltpu.VMEM((D_STATE, GROUP_W), F32), pltpu.VMEM((T + 8, GROUP_W), F32)],
        compiler_params=_cparams(2),
    )(p, p, p, p, p, p, p, conv_w, conv_w, conv_w, conv_b, conv_b, conv_b, dt_bias, a_log, d_skip, hin, dy)


def _alibi_slope(h):
    return 2.0 ** (-8.0 * (h + 1) / ATTN_HEADS)


def _dup_half(x256, kvh):
    xb = x256[:, 128 * (kvh // 2):128 * (kvh // 2) + 128]
    rolled = pltpu.roll(xb, 64, 1)
    lane = _iota(xb.shape, 1)
    if kvh % 2 == 0:
        return jnp.where(lane < 64, xb, rolled)
    return jnp.where(lane < 64, rolled, xb)


def _attn_masks(c):
    qi = _iota((T, 3 * T), 0)
    jj = _iota((T, 3 * T), 1)
    blk = jnp.right_shift(jj, 7)
    j = jnp.bitwise_and(jj, T - 1)
    q_pos = c * T + qi - PAD
    k_pos = (c - 2 + blk) * T + j - PAD
    dist = q_pos - k_pos
    band = (blk > 0) & (dist >= 0) & (dist < T) & (k_pos >= N_META)
    meta = (blk == 0) & (j >= PAD) & (j - PAD <= q_pos)
    distf = jnp.where(blk > 0, dist, 0).astype(F32)
    return band | meta, distf


def _attn_scores(qp, k3, allowed, distf, h0):
    lane = _iota(qp.shape, 1)
    s = []
    for half, h in ((0, h0), (1, h0 + 1)):
        qh = jnp.where((lane < 64) if half == 0 else (lane >= 64), qp, 0.0).astype(BF16)
        sc = _dot_nt(qh, k3) - _alibi_slope(h) * distf
        s.append((qh, jnp.where(allowed, sc, NEG)))
    return s


def _attn_fwd(p, sinks, n_chunks):
    n_rows = n_chunks * T
    kb, vb = P_K // KV_W, P_V // KV_W

    def body(q_ref, kc_ref, kp_ref, km_ref, vc_ref, vp_ref, vm_ref, sink_ref, o_ref, lse_ref):
        c = pl.program_id(0)
        allowed, distf = _attn_masks(c)
        q = q_ref[...] * SCALE
        sinks_v = sink_ref[...]
        lane = _iota((T, 128), 1)
        lse_all = jnp.zeros((T, 128), F32)
        outs = []
        for kvh in range(KV_HEADS):
            k3 = jnp.concatenate([_dup_half(r[...], kvh) for r in (km_ref, kp_ref, kc_ref)], axis=0).astype(BF16)
            v3 = jnp.concatenate([_dup_half(r[...], kvh) for r in (vm_ref, vp_ref, vc_ref)], axis=0)
            v3bd = _block_diag_rows(v3).astype(BF16)
            for pr in range(2):
                h0 = 4 * kvh + 2 * pr
                blk = 2 * kvh + pr
                qp = q[:, 128 * blk:128 * blk + 128]
                probs = []
                for (_, sc), h in zip(_attn_scores(qp, k3, allowed, distf, h0), (h0, h0 + 1)):
                    sink = sinks_v[:, h:h + 1]
                    m = jnp.maximum(jnp.max(sc, axis=1, keepdims=True), sink)
                    e = jnp.exp(sc - m)
                    den = jnp.sum(e, axis=1, keepdims=True) + jnp.exp(sink - m)
                    probs.append(e / den)
                    lse_all = jnp.where(lane == h, m + jnp.log(den), lse_all)
                outs.append(_dot(jnp.concatenate(probs, axis=1).astype(BF16), v3bd))
        o_ref[...] = jnp.concatenate(outs, axis=1).astype(o_ref.dtype)
        lse_ref[...] = lse_all

    blk = lambda width, col: pl.BlockSpec((T, width), lambda c: (c, col))
    prev = lambda width, col: pl.BlockSpec((T, width), lambda c: (jnp.maximum(c - 1, 0), col))
    first = lambda width, col: pl.BlockSpec((T, width), lambda c: (0, col))
    return pl.pallas_call(
        body, name="attn_fwd", grid=(n_chunks,),
        in_specs=[blk(ATTN_W, P_Q // ATTN_W), blk(KV_W, kb), prev(KV_W, kb), first(KV_W, kb),
                  blk(KV_W, vb), prev(KV_W, vb), first(KV_W, vb), pl.BlockSpec((1, 128), lambda c: (0, 0))],
        out_specs=[pl.BlockSpec((T, ATTN_W), lambda c: (c, 0)), pl.BlockSpec((T, 128), lambda c: (c, 0))],
        out_shape=[jax.ShapeDtypeStruct((n_rows, ATTN_W), BF16), jax.ShapeDtypeStruct((n_rows, 128), F32)],
        compiler_params=_cparams(1),
    )(p, p, p, p, p, p, p, sinks)


def _block_diag_rows(x3):
    lane = _iota(x3.shape, 1)
    return jnp.concatenate([jnp.where(lane < 64, x3, 0.0), jnp.where(lane >= 64, x3, 0.0)], axis=0)


def _fold_halves(x):
    return x + pltpu.roll(x, 64, 1)


def _attn_bwd(p, sinks, ao, lse, dao, n_chunks):
    n_rows = n_chunks * T
    kb, vb = P_K // KV_W, P_V // KV_W
    rc = lambda s: n_chunks - 1 - s

    def body(q_ref, kc_ref, kp_ref, km_ref, vc_ref, vp_ref, vm_ref, sink_ref, o_ref, lse_ref, do_ref,
             dq_ref, dk_ref, dv_ref, dsink_ref, kcar_ref, vcar_ref, kmeta_ref, vmeta_ref):
        step = pl.program_id(0)
        c = n_chunks - 1 - step

        @pl.when(step == 0)
        def _():
            for r in (kcar_ref, vcar_ref, kmeta_ref, vmeta_ref):
                r[...] = jnp.zeros_like(r)

        allowed, distf = _attn_masks(c)
        q = q_ref[...] * SCALE
        sinks_v = sink_ref[...]
        lse_v = lse_ref[...]
        ov = o_ref[...].astype(F32)
        dov = do_ref[...].astype(F32)
        lane = _iota((T, 128), 1)
        lane256 = _iota((3 * T, KV_W), 1)
        dsink = jnp.zeros((1, 128), F32)
        dk3_all = jnp.zeros((3 * T, KV_W), F32)
        dv3_all = jnp.zeros((3 * T, KV_W), F32)
        dqs = []
        for kvh in range(KV_HEADS):
            k3 = jnp.concatenate([_dup_half(r[...], kvh) for r in (km_ref, kp_ref, kc_ref)], axis=0).astype(BF16)
            v3 = jnp.concatenate([_dup_half(r[...], kvh) for r in (vm_ref, vp_ref, vc_ref)], axis=0).astype(BF16)
            dk3 = jnp.zeros((3 * T, 128), F32)
            dv3 = jnp.zeros((3 * T, 128), F32)
            for pr in range(2):
                h0 = 4 * kvh + 2 * pr
                blk = 2 * kvh + pr
                qp = q[:, 128 * blk:128 * blk + 128]
                dop = dov[:, 128 * blk:128 * blk + 128]
                prod = dop * ov[:, 128 * blk:128 * blk + 128]
                dq_pair = jnp.zeros((T, 128), F32)
                for half, ((qh, sc), h) in enumerate(zip(_attn_scores(qp, k3, allowed, distf, h0), (h0, h0 + 1))):
                    mine = (lane < 64) if half == 0 else (lane >= 64)
                    lse_h = lse_v[:, h:h + 1]
                    pm = jnp.exp(sc - lse_h)
                    doh = jnp.where(mine, dop, 0.0).astype(BF16)
                    delta = jnp.sum(jnp.where(mine, prod, 0.0), axis=1, keepdims=True)
                    dp = _dot_nt(doh, v3)
                    ds = (pm * (dp - delta)).astype(BF16)
                    p_sink = jnp.exp(sinks_v[:, h:h + 1] - lse_h)
                    dsink = jnp.where(_iota((1, 128), 1) == h, jnp.sum(-p_sink * delta, axis=0, keepdims=True), dsink)
                    dq_pair = jnp.where(mine, _dot(ds, k3), dq_pair)
                    dk3 = dk3 + _dot_tn(ds, qh)
                    dv3 = dv3 + _dot_tn(pm.astype(BF16), doh)
                dqs.append(dq_pair * SCALE)
            in_place = (lane256 >= 64 * kvh) & (lane256 < 64 * kvh + 64)
            wide = lambda x: jnp.concatenate([x, x], axis=1)
            dk3_all = jnp.where(in_place, wide(_fold_halves(dk3)), dk3_all)
            dv3_all = jnp.where(in_place, wide(_fold_halves(dv3)), dv3_all)
        dq_ref[...] = jnp.concatenate(dqs, axis=1).astype(dq_ref.dtype)
        dsink_all = dsink

        @pl.when(step == 0)
        def _():
            dsink_ref[...] = dsink_all

        @pl.when(step > 0)
        def _():
            dsink_ref[...] += dsink_all

        kmeta = kmeta_ref[...] + dk3_all[0:T]
        vmeta = vmeta_ref[...] + dv3_all[0:T]
        kmeta_ref[...] = kmeta
        vmeta_ref[...] = vmeta
        is_first = c == 0
        dk_ref[...] = jnp.where(is_first, kmeta, dk3_all[2 * T:3 * T] + kcar_ref[...]).astype(dk_ref.dtype)
        dv_ref[...] = jnp.where(is_first, vmeta, dv3_all[2 * T:3 * T] + vcar_ref[...]).astype(dv_ref.dtype)
        kcar_ref[...] = dk3_all[T:2 * T]
        vcar_ref[...] = dv3_all[T:2 * T]

    blk = lambda width, col: pl.BlockSpec((T, width), lambda s: (rc(s), col))
    prev = lambda width, col: pl.BlockSpec((T, width), lambda s: (jnp.maximum(rc(s) - 1, 0), col))
    first = lambda width, col: pl.BlockSpec((T, width), lambda s: (0, col))
    return pl.pallas_call(
        body, name="attn_bwd", grid=(n_chunks,),
        in_specs=[blk(ATTN_W, P_Q // ATTN_W), blk(KV_W, kb), prev(KV_W, kb), first(KV_W, kb),
                  blk(KV_W, vb), prev(KV_W, vb), first(KV_W, vb), pl.BlockSpec((1, 128), lambda s: (0, 0)),
                  blk(ATTN_W, 0), blk(128, 0), blk(ATTN_W, 0)],
        out_specs=[blk(ATTN_W, 0), blk(KV_W, 0), blk(KV_W, 0), pl.BlockSpec((1, 128), lambda s: (0, 0))],
        out_shape=[jax.ShapeDtypeStruct((n_rows, ATTN_W), BF16), jax.ShapeDtypeStruct((n_rows, KV_W), BF16),
                   jax.ShapeDtypeStruct((n_rows, KV_W), BF16), jax.ShapeDtypeStruct((1, 128), F32)],
        scratch_shapes=[pltpu.VMEM((T, KV_W), F32)] * 4,
        compiler_params=_cparams(1),
    )(p, p, p, p, p, p, p, sinks, ao, lse, dao)


def _pad_lanes(v, width=128):
    return jnp.pad(v, ((0, 0), (0, width - v.shape[1])))


def _local_step(h0, tgt, w):
    n_rows = h0.shape[0]
    n_chunks = n_rows // T
    tm = _row_tile(n_rows, 384)
    dt_bias, a_log, d_skip = (_pad_lanes(w[k]) for k in ("ssm_dt_bias", "ssm_a_log", "ssm_d_skip"))
    sinks = _pad_lanes(w["attn_sinks"])

    n1, = _rowwise("norm_pre_mix", lambda r0, h, wn: [_rms(h, wn)], n_rows, tm,
                   [(h0, D_MODEL, 0)], [w["norm_pre_mix"]], [(D_MODEL, BF16)], [])
    p = _matmul("in_proj", n1, w["w_cat"], "nn", F32)
    y_ssd, hin = _ssd_fwd(p, w["ssm_conv_w"], w["ssm_conv_b"], dt_bias, a_log, d_skip, n_chunks)
    ao, lse = _attn_fwd(p, sinks, n_chunks)

    def gate_norm(r0, y, z, wn):
        return [_rms(y * _silu(z), wn)]

    yn, = _rowwise("ssm_gate_norm", gate_norm, n_rows, tm, [(y_ssd, D_INNER, 0), (p, D_INNER, P_Z // D_INNER)],
                   [w["ssm_norm"]], [(D_INNER, BF16)], [])
    y_ssm = _matmul("ssm_out", yn, w["w_ssm_out"], "nn", F32)
    y_attn = _matmul("attn_out", ao, w["w_attn_out"], "nn", F32)

    def mix_gate(r0, ys, ya, gs, ga):
        return [_sigmoid(gs) * ys + _sigmoid(ga) * ya]

    gate_ins = [(p, D_MODEL, P_GATE // D_MODEL), (p, D_MODEL, P_GATE // D_MODEL + 1)]
    mixed, = _rowwise("mix_gate", mix_gate, n_rows, tm, [(y_ssm, D_MODEL, 0), (y_attn, D_MODEL, 0)] + gate_ins,
                      [], [(D_MODEL, BF16)], [])
    mix = _matmul("mix_out", mixed, w["w_mix_out"], "nn", F32)

    def post_mix(r0, mx, h, w_post, w_pre):
        h1 = jnp.where(_valid_rows(r0, tm, PAD), h + _rms(mx, w_post), 0.0)
        return [h1, _rms(h1, w_pre)]

    h1, n2 = _rowwise("post_mix", post_mix, n_rows, tm, [(mix, D_MODEL, 0), (h0, D_MODEL, 0)],
                      [w["norm_post_mix"], w["norm_pre_ffn"]], [(D_MODEL, F32), (D_MODEL, BF16)], [])
    u_raw = _matmul("ffn_up", n2, w["w_ffn_up"], "nn", F32)
    f = _ffn_act("ffn_act", u_raw, w["ffn_conv_w"], w["ffn_conv_b"], n_rows)
    ffn = _matmul("ffn_down", f, w["w_ffn_down"], "nn", F32)

    def final(r0, fo, h, t, w_post):
        real = _valid_rows(r0, tm, T)
        err = jnp.where(real, h + _rms(fo, w_post) - t, 0.0)
        dy = err * (1.0 / D_MODEL)
        dffn, dw = _rms_bwd(dy, fo, w_post)
        return [dffn, dy, jnp.sum(err * err, axis=0, keepdims=True), dw]

    dffn, dh2, loss_cols, g_norm_post_ffn = _rowwise(
        "loss_head", final, n_rows, tm, [(ffn, D_MODEL, 0), (h1, D_MODEL, 0), (tgt, D_MODEL, 0)],
        [w["norm_post_ffn"]], [(D_MODEL, BF16), (D_MODEL, F32)], [D_MODEL, D_MODEL])

    g = {"norm_post_ffn": g_norm_post_ffn}
    g["w_ffn_down"] = _matmul("ffn_down_dw", f, dffn, "tn", F32)
    df = _matmul("ffn_down_dx", dffn, w["w_ffn_down"], "nt", F32)
    dua, dug, dwa, dwg = _conv_act_bwd("ffn_act_bwd", u_raw, 0, df, w["ffn_conv_w"], w["ffn_conv_b"], n_rows, True)
    du_raw = jnp.concatenate([dua, dug], axis=1)
    dconv = jnp.concatenate([dwa, dwg], axis=1)
    g["ffn_conv_w"], g["ffn_conv_b"] = dconv[0:3], dconv[3:4]
    g["w_ffn_up"] = _matmul("ffn_up_dw", n2, du_raw, "tn", F32)
    dn2 = _matmul("ffn_up_dx", du_raw, w["w_ffn_up"], "nt", F32)

    def post_mix_bwd(r0, dn, d2, h, mx, w_pre, w_post):
        dx, dw_pre = _rms_bwd(dn, h, w_pre)
        dh1 = jnp.where(_valid_rows(r0, tm, PAD), dx + d2, 0.0)
        dmix, dw_post = _rms_bwd(dh1, mx, w_post)
        return [dh1, dmix, dw_pre, dw_post]

    dh1, dmix, g["norm_pre_ffn"], g["norm_post_mix"] = _rowwise(
        "post_mix_bwd", post_mix_bwd, n_rows, tm,
        [(dn2, D_MODEL, 0), (dh2, D_MODEL, 0), (h1, D_MODEL, 0), (mix, D_MODEL, 0)],
        [w["norm_pre_ffn"], w["norm_post_mix"]], [(D_MODEL, F32), (D_MODEL, BF16)], [D_MODEL, D_MODEL])
    g["w_mix_out"] = _matmul("mix_out_dw", mixed, dmix, "tn", F32)
    dmixed = _matmul("mix_out_dx", dmix, w["w_mix_out"], "nt", F32)

    def mix_gate_bwd(r0, dm, ys, ya, gs, ga):
        ss, sa = _sigmoid(gs), _sigmoid(ga)
        return [dm * ss, dm * sa, dm * ys * ss * (1.0 - ss), dm * ya * sa * (1.0 - sa)]

    dys, dya, dgs, dga = _rowwise(
        "mix_gate_bwd", mix_gate_bwd, n_rows, tm,
        [(dmixed, D_MODEL, 0), (y_ssm, D_MODEL, 0), (y_attn, D_MODEL, 0)] + gate_ins,
        [], [(D_MODEL, BF16)] * 4, [])
    g["w_ssm_out"] = _matmul("ssm_out_dw", yn, dys, "tn", F32)
    dyn = _matmul("ssm_out_dx", dys, w["w_ssm_out"], "nt", F32)
    g["w_attn_out"] = _matmul("attn_out_dw", ao, dya, "tn", F32)
    dao = _matmul("attn_out_dx", dya, w["w_attn_out"], "nt", BF16)

    def gate_norm_bwd(r0, dn, y, z, wn):
        sz = _silu(z)
        dyz, dw = _rms_bwd(dn, y * sz, wn)
        live = _valid_rows(r0, tm, PAD)
        return [jnp.where(live, dyz * sz, 0.0), jnp.where(live, dyz * y * _dsilu(z), 0.0), dw]

    dy_ssd, dz, g["ssm_norm"] = _rowwise(
        "ssm_gate_norm_bwd", gate_norm_bwd, n_rows, tm,
        [(dyn, D_INNER, 0), (y_ssd, D_INNER, 0), (p, D_INNER, P_Z // D_INNER)],
        [w["ssm_norm"]], [(D_INNER, F32), (D_INNER, BF16)], [D_INNER])
    dq, dk, dv, dsink = _attn_bwd(p, sinks, ao, lse, dao, n_chunks)
    g["attn_sinks"] = dsink[:, 0:ATTN_HEADS]
    dxs, dbm, dcm, ddt_parts, dpar = _ssd_bwd(p, w["ssm_conv_w"], w["ssm_conv_b"], dt_bias, a_log, d_skip, hin,
                                              dy_ssd, n_chunks)
    dpar = jnp.sum(dpar, axis=0)
    g["ssm_dt_bias"], g["ssm_a_log"], g["ssm_d_skip"] = (dpar[i:i + 1, 0:SSM_HEADS] for i in range(3))
    ddt = jnp.sum(ddt_parts, axis=0).astype(BF16)
    dact = jnp.concatenate([dxs, dbm, dcm], axis=1)
    dxbc, dconv = _conv_act_bwd("ssm_conv_bwd", p, P_XBC, dact, w["ssm_conv_w"], w["ssm_conv_b"], n_rows, False)
    g["ssm_conv_w"], g["ssm_conv_b"] = dconv[0:4], dconv[4:5]
    dp = jnp.concatenate([dz, dgs, dga, dxbc, dq, dk, dv, ddt], axis=1)
    g["w_cat"] = _matmul("in_proj_dw", n1, dp, "tn", F32)
    dn1 = _matmul("in_proj_dx", dp, w["w_cat"], "nt", F32)

    def pre_mix_bwd(r0, dn, d1, h, wn):
        dx, dw = _rms_bwd(dn, h, wn)
        return [jnp.where(_valid_rows(r0, tm, PAD), dx + d1, 0.0), dw]

    dh0, g["norm_pre_mix"] = _rowwise(
        "pre_mix_bwd", pre_mix_bwd, n_rows, tm, [(dn1, D_MODEL, 0), (dh1, D_MODEL, 0), (h0, D_MODEL, 0)],
        [w["norm_pre_mix"]], [(D_MODEL, F32)], [D_MODEL])
    return jnp.sum(loss_cols), dh0, g


_IN_SECTIONS = [(0, 2048), (6688, 8736), (2048, 5120), (5152, 6176), (6176, 6432), (6432, 6688), (5120, 5152)]


def _to_cat(w_in):
    parts = [w_in[:, a:b] for a, b in _IN_SECTIONS]
    parts.append(jnp.zeros((w_in.shape[0], P_W - N_IN), w_in.dtype))
    return jnp.concatenate(parts, axis=1)


def _from_cat(g_cat):
    offs = [P_Z, P_GATE, P_XBC, P_Q, P_K, P_V, P_DT]
    pieces = {a: g_cat[:, o:o + (b - a)] for (a, b), o in zip(_IN_SECTIONS, offs)}
    return jnp.concatenate([pieces[a] for a in sorted(pieces)], axis=1)


LANES = 1024
_SHARDED = [("w_in", (1024, 2184), 1), ("w_ssm_out", (512, 1024), 0), ("w_attn_out", (256, 1024), 0),
            ("w_mix_out", (256, 1024), 0), ("w_ffn_up", (1024, 1408), 1), ("w_ffn_down", (704, 1024), 0),
            ("ssm_conv_w", (4, 768), 1), ("ffn_conv_w", (3, 1408), 1), ("meta_tokens", (16, 256), 1)]
_SMALL_SHARDED = ("ssm_conv_w", "ffn_conv_w", "meta_tokens")
_REPLICATED = [("norm_pre_mix", 1024), ("ssm_conv_b", 3072), ("ssm_dt_bias", 32), ("ssm_a_log", 32),
               ("ssm_d_skip", 32), ("ssm_norm", 2048), ("attn_sinks", 16), ("norm_post_mix", 1024),
               ("norm_pre_ffn", 1024), ("ffn_conv_b", 5632), ("norm_post_ffn", 1024)]
FLAT_ROWS = 5376
HALF_ROWS = FLAT_ROWS // 2
SMALL_ROWS = 16
WEIGHT_ORDER = ["meta_tokens", "norm_pre_mix", "w_in", "ssm_conv_w", "ssm_conv_b", "ssm_dt_bias", "ssm_a_log",
                "ssm_d_skip", "ssm_norm", "w_ssm_out", "attn_sinks", "w_attn_out", "w_mix_out", "norm_post_mix",
                "norm_pre_ffn", "w_ffn_up", "ffn_conv_w", "ffn_conv_b", "w_ffn_down", "norm_post_ffn"]


def _flatten(parts, rows):
    flat = jnp.concatenate([a.reshape(-1) for a in parts])
    return jnp.pad(flat, (0, rows * LANES - flat.shape[0])).reshape(rows, LANES)


def _unflatten(flat, shapes):
    flat = flat.reshape(-1)
    out, off = [], 0
    for shp in shapes:
        n = math.prod(shp)
        out.append(flat[off:off + n].reshape(shp))
        off += n
    return out


def _shard_of(full, chip, shape, axis):
    return lax.slice_in_dim(full, chip * shape[axis], (chip + 1) * shape[axis], axis=axis)


def _mesh_pos():
    return lax.axis_index("x"), lax.axis_index("y"), lax.axis_index("c")


def _other_chips(x, y):
    return [(1 - x, y), (x, 1 - y), (1 - x, 1 - y)]


def _chip_index(x, y):
    return 2 * x + y


ANY = pl.BlockSpec(memory_space=pl.ANY)


def _run_exchange(name, make_copies, n_copies, ins, out_shapes):
    n_in = len(ins)
    n_out = len(out_shapes)

    def body(*refs):
        in_refs, out_refs = refs[:n_in], refs[n_in:n_in + n_out]
        send_sems, recv_sems = refs[n_in + n_out:]
        copies = [pltpu.make_async_remote_copy(src_ref=s, dst_ref=d, send_sem=send_sems.at[i], recv_sem=recv_sems.at[i],
                                               device_id=dev, device_id_type=MESH)
                  for i, (s, d, dev) in enumerate(make_copies(in_refs, out_refs))]
        assert len(copies) == n_copies
        for cp in copies:
            cp.start()
        for cp in copies:
            cp.wait()

    return pl.pallas_call(
        body, name=name, in_specs=[ANY] * n_in, out_specs=[ANY] * n_out, out_shape=out_shapes,
        scratch_shapes=[pltpu.SemaphoreType.DMA((n_copies,)), pltpu.SemaphoreType.DMA((n_copies,))],
        compiler_params=pltpu.CompilerParams(has_side_effects=True),
    )(*ins)


def _gather_weights(big, small):
    def body(big_ref, small_ref, big_out, small_out, send_sems, recv_sems, local_sems):
        x, y, c = _mesh_pos()
        j = _chip_index(x, y)
        sibling = (x, y, 1 - c)
        chips = _other_chips(x, y)
        idx = [_chip_index(*ch) for ch in chips]

        def half(chip_idx, hc):
            return big_out.at[chip_idx, pl.ds(pl.multiple_of(hc * HALF_ROWS, 8), HALF_ROWS), :]

        def remote(k, src, dst, dev):
            return pltpu.make_async_remote_copy(src_ref=src, dst_ref=dst, send_sem=send_sems.at[k],
                                                recv_sem=recv_sems.at[k], device_id=dev, device_id_type=MESH)

        own_big = pltpu.make_async_copy(big_ref, big_out.at[j], local_sems.at[0])
        own_small = pltpu.make_async_copy(small_ref, small_out.at[j], local_sems.at[1])
        own_big.start()
        own_small.start()
        my_half = big_ref.at[pl.ds(pl.multiple_of(c * HALF_ROWS, 8), HALF_ROWS), :]
        first = [remote(k, my_half, half(j, c), (*ch, c)) for k, ch in enumerate(chips)]
        first += [remote(3 + k, small_ref, small_out.at[j], (*ch, c)) for k, ch in enumerate(chips)]
        for cp in first:
            cp.start()
        passed = [remote(6 + k, half(idx[k], c), half(idx[k], c), sibling) for k in range(3)]
        for k in range(3):
            remote(k, my_half, half(idx[k], c), (*chips[k], c)).wait_recv()
            passed[k].start()
        for k in range(3):
            remote(3 + k, small_ref, small_out.at[idx[k]], (*chips[k], c)).wait_recv()
        for k in range(3):
            remote(6 + k, half(idx[k], 1 - c), half(idx[k], 1 - c), sibling).wait_recv()
        for cp in first + passed:
            cp.wait_send()
        own_big.wait()
        own_small.wait()

    return pl.pallas_call(
        body, name="gather_weights", in_specs=[ANY, ANY], out_specs=[ANY, ANY],
        out_shape=[jax.ShapeDtypeStruct((4,) + big.shape, big.dtype), jax.ShapeDtypeStruct((4,) + small.shape, small.dtype)],
        scratch_shapes=[pltpu.SemaphoreType.DMA((9,)), pltpu.SemaphoreType.DMA((9,)), pltpu.SemaphoreType.DMA((2,))],
        compiler_params=pltpu.CompilerParams(has_side_effects=True),
    )(big, small)


def _sibling_exchange(name, arrays, views, out_shapes):
    def make(in_refs, out_refs):
        x, y, c = _mesh_pos()
        return [(view(r, c), o, (x, y, 1 - c)) for r, o, view in zip(in_refs, out_refs, views)]

    return _run_exchange(name, make, len(arrays), arrays,
                         [jax.ShapeDtypeStruct(s, a.dtype) for s, a in zip(out_shapes, arrays)])


def _chip_exchange(name, psend, psmall):
    def make(in_refs, out_refs):
        x, y, c = _mesh_pos()
        chips = _other_chips(x, y)
        big = [(in_refs[0].at[_chip_index(*ch)], out_refs[0].at[k], (*ch, c)) for k, ch in enumerate(chips)]
        small = [(in_refs[1], out_refs[1].at[k], (*ch, c)) for k, ch in enumerate(chips)]
        return big + small

    return _run_exchange(name, make, 6, [psend, psmall],
                         [jax.ShapeDtypeStruct((3,) + psend.shape[1:], psend.dtype),
                          jax.ShapeDtypeStruct((3,) + psmall.shape, psmall.dtype)])


RED_TILE = 384


def _pair_sum(gbig, rbig, ids):
    nb = HALF_ROWS // RED_TILE

    def body(ids_ref, g_ref, r_ref, send_ref, own_ref):
        s = g_ref[...] + r_ref[...]
        send_ref[...] = s.astype(send_ref.dtype)

        @pl.when(pl.program_id(1) == ids_ref[1])
        def _():
            own_ref[...] = s

    grid_spec = pltpu.PrefetchScalarGridSpec(
        num_scalar_prefetch=1, grid=(nb, 4),
        in_specs=[pl.BlockSpec((None, RED_TILE, LANES), lambda i, j, ids_ref: (j, ids_ref[0] * nb + i, 0)),
                  pl.BlockSpec((None, RED_TILE, LANES), lambda i, j, ids_ref: (j, i, 0))],
        out_specs=[pl.BlockSpec((None, RED_TILE, LANES), lambda i, j, ids_ref: (j, i, 0)),
                   pl.BlockSpec((RED_TILE, LANES), lambda i, j, ids_ref: (i, 0))])
    return pl.pallas_call(
        body, name="pair_sum", grid_spec=grid_spec,
        out_shape=[jax.ShapeDtypeStruct((4, HALF_ROWS, LANES), BF16), jax.ShapeDtypeStruct((HALF_ROWS, LANES), F32)],
        compiler_params=_cparams(2),
    )(ids, gbig, rbig)


def _chip_sum(own, recv):
    def body(o_ref, r_ref, out_ref):
        out_ref[...] = ((o_ref[...] + r_ref[0].astype(F32)) + r_ref[1].astype(F32)) + r_ref[2].astype(F32)

    return pl.pallas_call(
        body, name="chip_sum", grid=(HALF_ROWS // RED_TILE,),
        in_specs=[pl.BlockSpec((RED_TILE, LANES), lambda i: (i, 0)), pl.BlockSpec((3, RED_TILE, LANES), lambda i: (0, i, 0))],
        out_specs=pl.BlockSpec((RED_TILE, LANES), lambda i: (i, 0)),
        out_shape=jax.ShapeDtypeStruct((HALF_ROWS, LANES), F32), compiler_params=_cparams(1),
    )(own, recv)


def _chip_sum_small(own, recv, ids):
    def body(ids_ref, o_ref, r_ref, out_ref):
        j = ids_ref[1]
        total = None
        for i in range(4):
            m = jnp.bitwise_xor(i, j)
            term = jnp.where(m == 0, o_ref[...], jnp.where(m == 2, r_ref[0], jnp.where(m == 1, r_ref[1], r_ref[2])))
            total = term if total is None else total + term
        out_ref[...] = total

    grid_spec = pltpu.PrefetchScalarGridSpec(
        num_scalar_prefetch=1, grid=(1,),
        in_specs=[pl.BlockSpec(own.shape, lambda i, ids_ref: (0, 0)), pl.BlockSpec(recv.shape, lambda i, ids_ref: (0, 0, 0))],
        out_specs=pl.BlockSpec(own.shape, lambda i, ids_ref: (0, 0)))
    return pl.pallas_call(body, name="chip_sum_small", grid_spec=grid_spec,
                          out_shape=jax.ShapeDtypeStruct(own.shape, F32), compiler_params=_cparams(1))(ids, own, recv)


def _adamw(name, w, m, v, g_lo, g_hi):
    rows = w.shape[0]
    half = rows // 2
    tr = _row_tile(half, 384, unit=8)
    nb = half // tr
    c1 = 1.0 / (1.0 - ADAM_B1 ** ADAM_STEP)
    c2 = 1.0 / (1.0 - ADAM_B2 ** ADAM_STEP)

    def body(w_ref, m_ref, v_ref, lo_ref, hi_ref, g_out, d_out, m_out, v_out):
        g = jnp.where(pl.program_id(0) == 0, lo_ref[...], hi_ref[...])
        m_new = ADAM_B1 * m_ref[...] + (1.0 - ADAM_B1) * g
        v_new = ADAM_B2 * v_ref[...] + (1.0 - ADAM_B2) * (g * g)
        d_out[...] = -ADAM_LR * ((m_new * c1) / (jnp.sqrt(v_new * c2) + ADAM_EPS) + ADAM_WD * w_ref[...])
        g_out[...] = g
        m_out[...] = m_new
        v_out[...] = v_new

    full = pl.BlockSpec((tr, LANES), lambda h, i: (h * nb + i, 0))
    part = pl.BlockSpec((tr, LANES), lambda h, i: (i, 0))
    return pl.pallas_call(
        body, name=name, grid=(2, nb), in_specs=[full, full, full, part, part], out_specs=[full] * 4,
        out_shape=[jax.ShapeDtypeStruct((rows, LANES), F32)] * 4, compiler_params=_cparams(2),
    )(w, m, v, g_lo, g_hi)


def _full_weights(big_all, small_all):
    shapes = [shp for _, shp, _ in _SHARDED]
    per_chip = [_unflatten(big_all[i], shapes) for i in range(4)]
    small_shapes = [shp for n, shp, _ in _SHARDED if n in _SMALL_SHARDED]
    small_per_chip = [_unflatten(small_all[i], small_shapes) for i in range(4)]
    full = {}
    for k, (name, _, axis) in enumerate(_SHARDED):
        if name in _SMALL_SHARDED:
            pieces = [small_per_chip[i][_SMALL_SHARDED.index(name)] for i in range(4)]
        else:
            pieces = [per_chip[i][k] for i in range(4)]
        full[name] = jnp.concatenate(pieces, axis=axis)
    return full


def kernel(x, meta_tokens, norm_pre_mix, w_in, ssm_conv_w, ssm_conv_b, ssm_dt_bias, ssm_a_log, ssm_d_skip, ssm_norm, w_ssm_out, attn_sinks, w_attn_out, w_mix_out, norm_post_mix, norm_pre_ffn, w_ffn_up, ffn_conv_w, ffn_conv_b, w_ffn_down, norm_post_ffn, loss_target, m_meta_tokens, m_norm_pre_mix, m_w_in, m_ssm_conv_w, m_ssm_conv_b, m_ssm_dt_bias, m_ssm_a_log, m_ssm_d_skip, m_ssm_norm, m_w_ssm_out, m_attn_sinks, m_w_attn_out, m_w_mix_out, m_norm_post_mix, m_norm_pre_ffn, m_w_ffn_up, m_ffn_conv_w, m_ffn_conv_b, m_w_ffn_down, m_norm_post_ffn, v_meta_tokens, v_norm_pre_mix, v_w_in, v_ssm_conv_w, v_ssm_conv_b, v_ssm_dt_bias, v_ssm_a_log, v_ssm_d_skip, v_ssm_norm, v_w_ssm_out, v_attn_sinks, v_w_attn_out, v_w_mix_out, v_norm_post_mix, v_norm_pre_ffn, v_w_ffn_up, v_ffn_conv_w, v_ffn_conv_b, v_w_ffn_down, v_norm_post_ffn):
    args = dict(locals())
    shard_shape = {n: s for n, s, _ in _SHARDED}
    wts = {n: args[n].reshape(shard_shape.get(n, args[n].shape[-2:])) for n in WEIGHT_ORDER}
    mom = {n: args["m_" + n].reshape(wts[n].shape) for n in WEIGHT_ORDER}
    var = {n: args["v_" + n].reshape(wts[n].shape) for n in WEIGHT_ORDER}
    x_i, y_i, c_i = _mesh_pos()
    chip = _chip_index(x_i, y_i)
    ids = jnp.stack([c_i, chip]).astype(jnp.int32)

    sharded_names = [n for n, _, _ in _SHARDED]
    big = _flatten([wts[n] for n in sharded_names], FLAT_ROWS).astype(BF16)
    small = _flatten([wts[n] for n in _SMALL_SHARDED], SMALL_ROWS)
    big_all, small_all = _gather_weights(big, small)
    full = _full_weights(big_all, small_all)
    w = {n: wts[n] for n, _ in _REPLICATED}
    w.update({n: full[n] for n in ("w_ssm_out", "w_attn_out", "w_mix_out", "w_ffn_up", "w_ffn_down",
                                   "ssm_conv_w", "ffn_conv_w")})
    w["w_cat"] = _to_cat(full["w_in"])

    h0 = jnp.concatenate([jnp.zeros((PAD, D_MODEL), F32), full["meta_tokens"], x[0]], axis=0)
    tgt = jnp.concatenate([jnp.zeros((T, D_MODEL), F32), loss_target[0]], axis=0)
    loss_sum, dh0, g = _local_step(h0, tgt, w)
    loss = lax.psum(loss_sum * (0.5 / D_MODEL), ("x", "y", "c"))
    grad_x = dh0[T:][None]
    g["meta_tokens"] = dh0[PAD:T]
    g["w_in"] = _from_cat(g.pop("w_cat"))

    gbig = jnp.stack([_flatten([_shard_of(g[n], i, shp, ax) for n, shp, ax in _SHARDED], FLAT_ROWS) for i in range(4)])
    gsmall = _flatten([g[n] for n, _ in _REPLICATED], SMALL_ROWS)
    other_half = lambda r, c: r.at[:, pl.ds(pl.multiple_of((1 - c) * HALF_ROWS, 8), HALF_ROWS), :]
    rbig, rsmall = _sibling_exchange("grad_pair_exchange", [gbig, gsmall], [other_half, lambda r, c: r],
                                     [(4, HALF_ROWS, LANES), gsmall.shape])
    psend, pown = _pair_sum(gbig, rbig, ids)
    psmall, = _rowwise("pair_sum_small", lambda r0, a, b: [a + b], SMALL_ROWS, SMALL_ROWS,
                       [(gsmall, LANES, 0), (rsmall, LANES, 0)], [], [(LANES, F32)], [])
    recv_big, recv_small = _chip_exchange("grad_chip_exchange", psend, psmall)
    ghalf = _chip_sum(pown, recv_big)
    gsmall_tot = _chip_sum_small(psmall, recv_small, ids)
    gsib, = _sibling_exchange("grad_half_share", [ghalf], [lambda r, c: r], [ghalf.shape])
    g_lo = jnp.where(c_i == 0, ghalf, gsib)
    g_hi = jnp.where(c_i == 0, gsib, ghalf)

    res_big = _adamw("adamw_sharded", _flatten([wts[n] for n in sharded_names], FLAT_ROWS),
                     _flatten([mom[n] for n in sharded_names], FLAT_ROWS),
                     _flatten([var[n] for n in sharded_names], FLAT_ROWS), g_lo, g_hi)
    rep_names = [n for n, _ in _REPLICATED]
    res_small = _adamw("adamw_replicated", _flatten([wts[n] for n in rep_names], SMALL_ROWS),
                       _flatten([mom[n] for n in rep_names], SMALL_ROWS),
                       _flatten([var[n] for n in rep_names], SMALL_ROWS),
                       gsmall_tot[0:SMALL_ROWS // 2], gsmall_tot[SMALL_ROWS // 2:])
    outs = []
    for kind in range(4):
        vals = dict(zip(sharded_names, _unflatten(res_big[kind], [shp for _, shp, _ in _SHARDED])))
        vals.update(zip(rep_names, _unflatten(res_small[kind], [(1, n) for _, n in _REPLICATED])))
        outs += [vals[n].reshape(args[n].shape) for n in WEIGHT_ORDER]
    return (loss, grad_x, *outs)
```

```python
import math

import jax
import jax.numpy as jnp
from jax import lax
from jax.experimental import pallas as pl
from jax.experimental.pallas import tpu as pltpu

F32 = jnp.float32
BF16 = jnp.bfloat16

D_MODEL = 1024
N_META = 16
T = 128
PAD = T - N_META
D_INNER = 2048
SSM_HEADS = 32
HEAD_P = 64
SSM_GROUPS = 4
GROUP_W = D_INNER // SSM_GROUPS
D_STATE = 128
CONV_DIM = D_INNER + 2 * SSM_GROUPS * D_STATE
ATTN_HEADS = 16
KV_HEADS = 4
ATTN_W = 1024
KV_W = 256
FFN_DIM = 2816
N_IN = 8736
EPS = 1e-6
NEG = -1e30
SCALE = 0.125

P_Q, P_K, P_V, P_DT, P_Z, P_GATE, P_XBC = 0, 1024, 1280, 1536, 2048, 4096, 6144
QKV_W = 1536
P_W = 9216

ADAM_LR, ADAM_B1, ADAM_B2, ADAM_EPS, ADAM_WD, ADAM_STEP = 0.001, 0.9, 0.999, 1e-08, 0.01, 10

VMEM_BUDGET = 40 * 1024 * 1024
VMEM_LIMIT = 56 * 1024 * 1024
MESH = pl.DeviceIdType.MESH


def _cparams(n_axes, **kw):
    return pltpu.CompilerParams(dimension_semantics=("arbitrary",) * n_axes, vmem_limit_bytes=VMEM_LIMIT, **kw)


def _sigmoid(x):
    return 1.0 / (1.0 + jnp.exp(-x))


def _silu(x):
    return x * _sigmoid(x)


def _dsilu(x):
    s = _sigmoid(x)
    return s * (1.0 + x * (1.0 - s))


def _softplus(x):
    e = jnp.exp(-jnp.abs(x))
    small = e * (1.0 - e * (0.5 - e * (1.0 / 3.0)))
    return jnp.maximum(x, 0.0) + jnp.where(e < 0.01, small, jnp.log(1.0 + e))


def _rms(x, w):
    r = lax.rsqrt(jnp.mean(x * x, axis=-1, keepdims=True) + EPS)
    return x * r * w


def _rms_bwd(dy, x, w):
    r = lax.rsqrt(jnp.mean(x * x, axis=-1, keepdims=True) + EPS)
    xh = x * r
    g = dy * w
    dx = r * (g - xh * jnp.mean(g * xh, axis=-1, keepdims=True))
    dw = jnp.sum(dy * xh, axis=0, keepdims=True)
    return dx, dw


def _dot(a, b):
    return jnp.dot(a, b, preferred_element_type=F32)


def _dot_nt(a, b):
    return lax.dot_general(a, b, (((1,), (1,)), ((), ())), preferred_element_type=F32)


def _dot_tn(a, b):
    return lax.dot_general(a, b, (((0,), (0,)), ((), ())), preferred_element_type=F32)


def _split3(x):
    hi = x.astype(BF16)
    r = x - hi.astype(F32)
    mid = r.astype(BF16)
    lo = (r - mid.astype(F32)).astype(BF16)
    return hi, mid, lo


def _xdot(x, e):
    hi, mid, lo = _split3(x)
    return _dot(hi, e) + _dot(mid, e) + _dot(lo, e)


def _xdot_l(e, x):
    hi, mid, lo = _split3(x)
    return _dot(e, hi) + _dot(e, mid) + _dot(e, lo)


def _iota(shape, dim):
    return lax.broadcasted_iota(jnp.int32, shape, dim)


def _divisors(n, unit):
    return [t for t in range(unit, n + 1, unit) if n % t == 0]


def _matmul_tiles(m, n, k, a_bytes, b_bytes, o_bytes, m_unit):
    best = None
    for tm in _divisors(m, m_unit):
        for tn in _divisors(n, 128):
            for tk in _divisors(k, 128):
                acc = 0 if tk == k else tm * tn * 4
                vm = 2 * (tm * tk * a_bytes + tk * tn * b_bytes + tm * tn * o_bytes) + acc
                if vm > VMEM_BUDGET:
                    continue
                score = (tm * tn * tk, tk)
                if best is None or score > best[0]:
                    best = (score, (tm, tn, tk))
    return best[1]


def _matmul(name, a, b, mode, out_dtype):
    if mode == "nn":
        (m, k), n = a.shape, b.shape[1]
    elif mode == "nt":
        (m, k), n = a.shape, b.shape[0]
    else:
        (k, m), n = a.shape, b.shape[1]
    ab, bb, ob = a.dtype.itemsize, b.dtype.itemsize, jnp.dtype(out_dtype).itemsize
    tm, tn, tk = _matmul_tiles(m, n, k, ab, bb, ob, 128 if mode == "tn" else 16)
    nk = k // tk
    dot = {"nn": _dot, "nt": _dot_nt, "tn": _dot_tn}[mode]

    def body(a_ref, b_ref, o_ref, *scratch):
        prod = dot(a_ref[...].astype(BF16), b_ref[...].astype(BF16))
        if nk == 1:
            o_ref[...] = prod.astype(o_ref.dtype)
        else:
            acc_ref, = scratch
            kk = pl.program_id(2)

            @pl.when(kk == 0)
            def _():
                acc_ref[...] = prod

            @pl.when(kk > 0)
            def _():
                acc_ref[...] += prod

            @pl.when(kk == nk - 1)
            def _():
                o_ref[...] = acc_ref[...].astype(o_ref.dtype)

    a_spec = pl.BlockSpec((tk, tm), lambda i, j, kk: (kk, i)) if mode == "tn" else pl.BlockSpec((tm, tk), lambda i, j, kk: (i, kk))
    b_spec = pl.BlockSpec((tn, tk), lambda i, j, kk: (j, kk)) if mode == "nt" else pl.BlockSpec((tk, tn), lambda i, j, kk: (kk, j))
    return pl.pallas_call(
        body, name=name, grid=(m // tm, n // tn, nk),
        in_specs=[a_spec, b_spec], out_specs=pl.BlockSpec((tm, tn), lambda i, j, kk: (i, j)),
        out_shape=jax.ShapeDtypeStruct((m, n), out_dtype),
        scratch_shapes=[] if nk == 1 else [pltpu.VMEM((tm, tn), F32)],
        compiler_params=_cparams(3),
    )(a, b)


def _row_tile(n_rows, cap, unit=16):
    return max(t for t in _divisors(n_rows, unit) if t <= cap)


def _rowwise(name, fn, n_rows, tm, row_ins, full_ins, row_outs, acc_outs):
    n_in = len(row_ins) + len(full_ins)
    n_ro = len(row_outs)
    into = [(k, o[3]) for k, o in enumerate(row_outs) if len(o) > 2 and o[2] == "into"]

    def body(*refs):
        i = pl.program_id(0)
        res = fn(i * tm, *[r[...] for r in refs[:n_in]])
        outs = refs[n_in + len(into):]
        for o, r, v in zip(row_outs, outs[:n_ro], res[:n_ro]):
            if len(o) > 2 and o[2] == "first":
                @pl.when(i == 0)
                def _(r=r, v=v):
                    r[...] = v.astype(r.dtype)
            else:
                r[...] = v.astype(r.dtype)

        @pl.when(i == 0)
        def _():
            for r, v in zip(outs[n_ro:], res[n_ro:]):
                r[...] = v

        @pl.when(i > 0)
        def _():
            for r, v in zip(outs[n_ro:], res[n_ro:]):
                r[...] += v

    def in_spec(entry):
        w, cb = entry[1], entry[2]
        if len(entry) > 3:
            return pl.BlockSpec((tm, w), lambda i: (jnp.maximum(i - 1, 0), cb))
        return pl.BlockSpec((tm, w), lambda i: (i, cb))

    def out_spec(o):
        if len(o) == 2:
            return pl.BlockSpec((tm, o[0]), lambda i: (i, 0)), jax.ShapeDtypeStruct((n_rows, o[0]), o[1])
        if o[2] == "new":
            return pl.BlockSpec((tm, o[0]), lambda i: (i, o[4])), jax.ShapeDtypeStruct((n_rows, o[3]), o[1])
        if o[2] == "into":
            return pl.BlockSpec((tm, o[0]), lambda i: (i, o[4])), jax.ShapeDtypeStruct(o[3].shape, o[3].dtype)
        if o[2] == "first":
            return pl.BlockSpec((tm, o[0]), lambda i: (0, 0)), jax.ShapeDtypeStruct((tm, o[0]), o[1])
        return pl.BlockSpec((tm, o[0]), lambda i: (jnp.maximum(i - 1, 0), 0)), jax.ShapeDtypeStruct((o[3], o[0]), o[1])

    in_specs = [in_spec(e) for e in row_ins]
    in_specs += [pl.BlockSpec(a.shape, lambda i: (0, 0)) for a in full_ins]
    in_specs += [pl.BlockSpec(memory_space=pl.ANY) for _ in into]
    specs_shapes = [out_spec(o) for o in row_outs]
    out_specs = [s for s, _ in specs_shapes] + [pl.BlockSpec((1, w), lambda i: (0, 0)) for w in acc_outs]
    out_shape = [s for _, s in specs_shapes] + [jax.ShapeDtypeStruct((1, w), F32) for w in acc_outs]
    return pl.pallas_call(
        body, name=name, grid=(n_rows // tm,), in_specs=in_specs, out_specs=out_specs, out_shape=out_shape,
        input_output_aliases={n_in + a: k for a, (k, _) in enumerate(into)},
        compiler_params=_cparams(1),
    )(*[e[0] for e in row_ins], *full_ins, *[arr for _, arr in into])


def _valid_rows(first_row, tm, lo):
    return (first_row + _iota((tm, 1), 0)) >= lo


def _halo_specs(tm, tc, col_off, n_tiles):
    r8 = tm // 8
    cur = pl.BlockSpec((tm, tc), lambda i, j: (i, j + col_off))
    prev = pl.BlockSpec((8, tc), lambda i, j: (jnp.maximum(i * r8 - 1, 0), j + col_off))
    nxt = pl.BlockSpec((8, tc), lambda i, j: (jnp.minimum((i + 1) * r8, n_tiles * r8 - 1), j + col_off))
    return cur, prev, nxt


def _ffn_act(name, u_raw, conv_w, conv_b, n_rows):
    tm = _row_tile(n_rows, 384)
    tc = 1408
    nj = FFN_DIM // tc
    taps = conv_w.shape[0]

    def body(ac_ref, ap_ref, gc_ref, gp_ref, wa_ref, wg_ref, ba_ref, bg_ref, f_ref, ext_ref):
        i = pl.program_id(0)

        def conv(cur_ref, prev_ref, w_ref, b_ref):
            ext_ref[0:8, :] = jnp.where(i > 0, prev_ref[...], 0.0)
            ext_ref[8:8 + tm, :] = cur_ref[...]
            w = w_ref[...]
            acc = b_ref[...] + w[taps - 1:taps] * cur_ref[...]
            for k in range(taps - 1):
                acc = acc + w[k:k + 1] * ext_ref[pl.ds(8 - (taps - 1) + k, tm), :]
            return acc

        a = conv(ac_ref, ap_ref, wa_ref, ba_ref)
        g = conv(gc_ref, gp_ref, wg_ref, bg_ref)
        f = jnp.where(_valid_rows(i * tm, tm, PAD), _silu(a) * g, 0.0)
        f_ref[...] = f.astype(f_ref.dtype)

    a_cur, a_prev, _ = _halo_specs(tm, tc, 0, n_rows // tm)
    g_cur, g_prev, _ = _halo_specs(tm, tc, nj, n_rows // tm)
    wspec = lambda off: pl.BlockSpec((taps, tc), lambda i, j: (0, j + off))
    bspec = lambda off: pl.BlockSpec((1, tc), lambda i, j: (0, j + off))
    return pl.pallas_call(
        body, name=name, grid=(n_rows // tm, nj),
        in_specs=[a_cur, a_prev, g_cur, g_prev, wspec(0), wspec(nj), bspec(0), bspec(nj)],
        out_specs=pl.BlockSpec((tm, tc), lambda i, j: (i, j)),
        out_shape=jax.ShapeDtypeStruct((n_rows, FFN_DIM), BF16),
        scratch_shapes=[pltpu.VMEM((tm + 8, tc), F32)],
        compiler_params=_cparams(2),
    )(u_raw, u_raw, u_raw, u_raw, conv_w, conv_w, conv_b, conv_b)


CONV_BWD_ROWS = 128


def _conv_bwd(name, raw, raw_blk, dsrcs, chunk_src, chunk_w, conv_w, conv_b, n_rows, gated, into=None, into_blk=0):
    taps, width = conv_w.shape
    half = width // 2 if gated else width
    tm = CONV_BWD_ROWS
    te = tm + 8
    r8 = tm // 8
    n_tiles = n_rows // tm
    nd = len(dsrcs)

    def body(*refs):
        cur_ref, prev_ref, next_ref = refs[0:3]
        dcur, dnext = refs[3:3 + nd], refs[3 + nd:3 + 2 * nd]
        w_ref, b_ref = refs[3 + 2 * nd:5 + 2 * nd]
        out_ref, acc_ref, ext_ref, dext_ref = refs[-4:]
        i = pl.program_id(0)
        row = i * tm + _iota((te, 1), 0)
        live = (row >= PAD) & (row < n_rows)
        valid = _valid_rows(i * tm, tm, PAD)
        ext_ref[0:8, :] = jnp.where(i > 0, prev_ref[...], 0.0)
        ext_ref[8:8 + tm, :] = cur_ref[...]
        ext_ref[8 + tm:16 + tm, :] = next_ref[...]

        def pre_act(c0):
            acc = b_ref[:, c0:c0 + chunk_w] + jnp.zeros((te, chunk_w), F32)
            for k in range(taps):
                acc = acc + w_ref[k:k + 1, c0:c0 + chunk_w] * ext_ref[pl.ds(8 - (taps - 1) + k, te), c0:c0 + chunk_w]
            return acc

        for q, (src, off) in enumerate(chunk_src):
            a0 = q * chunk_w
            d_ext = jnp.concatenate([dcur[src][:, off:off + chunk_w], dnext[src][:, off:off + chunk_w]], axis=0).astype(F32)
            pre_a = pre_act(a0)
            if gated:
                pre_g = pre_act(half + a0)
                parts = [(a0, d_ext * pre_g * _dsilu(pre_a)), (half + a0, d_ext * _silu(pre_a))]
            else:
                parts = [(a0, d_ext * _dsilu(pre_a))]
            for c0, dpre in parts:
                du = jnp.where(live, dpre, 0.0)
                dext_ref[...] = du
                draw = jnp.zeros((tm, chunk_w), F32)
                for k in range(taps):
                    draw = draw + w_ref[k:k + 1, c0:c0 + chunk_w] * dext_ref[pl.ds(taps - 1 - k, tm), :]
                out_ref[:, c0:c0 + chunk_w] = jnp.where(valid, draw, 0.0).astype(out_ref.dtype)
                du_cur = du[0:tm]
                rows = [jnp.sum(du_cur * ext_ref[pl.ds(8 - (taps - 1) + k, tm), c0:c0 + chunk_w], axis=0, keepdims=True)
                        for k in range(taps)]
                rows.append(jnp.sum(du_cur, axis=0, keepdims=True))
                rows.append(jnp.zeros((8 - taps - 1, chunk_w), F32))
                dwb = jnp.concatenate(rows, axis=0)
                first = i == 0
                acc_ref[:, c0:c0 + chunk_w] = jnp.where(first, dwb, acc_ref[:, c0:c0 + chunk_w] + dwb)

    in_specs = [pl.BlockSpec((tm, width), lambda i: (i, raw_blk)),
                pl.BlockSpec((8, width), lambda i: (jnp.maximum(i * r8 - 1, 0), raw_blk)),
                pl.BlockSpec((8, width), lambda i: (jnp.minimum((i + 1) * r8, n_tiles * r8 - 1), raw_blk))]
    in_specs += [pl.BlockSpec((tm, d.shape[1]), lambda i: (i, 0)) for d in dsrcs]
    in_specs += [pl.BlockSpec((8, d.shape[1]), lambda i: (jnp.minimum((i + 1) * r8, n_tiles * r8 - 1), 0)) for d in dsrcs]
    in_specs += [pl.BlockSpec((taps, width), lambda i: (0, 0)), pl.BlockSpec((1, width), lambda i: (0, 0))]
    operands = [raw, raw, raw] + list(dsrcs) + list(dsrcs) + [conv_w, conv_b]
    aliases = {}
    if into is None:
        out0 = jax.ShapeDtypeStruct((n_rows, width), BF16)
    else:
        in_specs.append(pl.BlockSpec(memory_space=pl.ANY))
        operands.append(into)
        aliases = {len(operands) - 1: 0}
        out0 = jax.ShapeDtypeStruct(into.shape, into.dtype)
    return pl.pallas_call(
        body, name=name, grid=(n_tiles,), in_specs=in_specs,
        out_specs=[pl.BlockSpec((tm, width), lambda i: (i, into_blk)), pl.BlockSpec((8, width), lambda i: (0, 0))],
        out_shape=[out0, jax.ShapeDtypeStruct((8, width), F32)], input_output_aliases=aliases,
        scratch_shapes=[pltpu.VMEM((tm + 16, width), F32), pltpu.VMEM((te, chunk_w), F32)],
        compiler_params=_cparams(1),
    )(*operands)


def _ssd_specs(n_chunks, rev):
    cidx = (lambda c: n_chunks - 1 - c) if rev else (lambda c: c)
    xg0, bg0, cg0 = P_XBC // GROUP_W, (P_XBC + D_INNER) // D_STATE, (P_XBC + D_INNER + SSM_GROUPS * D_STATE) // D_STATE

    def cur(width, blk0):
        return pl.BlockSpec((T, width), lambda g, c: (cidx(c), blk0 + g))

    def prev(width, blk0):
        return pl.BlockSpec((8, width), lambda g, c: (jnp.maximum(cidx(c) * (T // 8) - 1, 0), blk0 + g))

    specs = [cur(GROUP_W, xg0), prev(GROUP_W, xg0), cur(D_STATE, bg0), prev(D_STATE, bg0),
             cur(D_STATE, cg0), prev(D_STATE, cg0),
             pl.BlockSpec((T, 128), lambda g, c: (cidx(c), P_DT // 128))]
    wx, wb, wc = 0, D_INNER // D_STATE, (D_INNER + SSM_GROUPS * D_STATE) // D_STATE
    specs += [pl.BlockSpec((4, GROUP_W), lambda g, c: (0, g)),
              pl.BlockSpec((4, D_STATE), lambda g, c: (0, wb + g)),
              pl.BlockSpec((4, D_STATE), lambda g, c: (0, wc + g)),
              pl.BlockSpec((1, GROUP_W), lambda g, c: (0, g)),
              pl.BlockSpec((1, D_STATE), lambda g, c: (0, wb + g)),
              pl.BlockSpec((1, D_STATE), lambda g, c: (0, wc + g))]
    specs += [pl.BlockSpec((1, 128), lambda g, c: (0, 0))] * 3
    return specs, cidx


def _ssd_chunk_forward(refs, ext_ref, g, c):
    (xc_ref, xp_ref, bc_ref, bp_ref, cc_ref, cp_ref, dt_ref, wx_ref, wb_ref, wc_ref,
     bx_ref, bb_ref, bcb_ref, dtb_ref, alog_ref, dsk_ref) = refs

    def conv_pre(cur_ref, prev_ref, w_ref, b_ref, width):
        ext_ref[0:8, 0:width] = jnp.where(c > 0, prev_ref[...], 0.0)
        ext_ref[8:8 + T, 0:width] = cur_ref[...]
        w = w_ref[...]
        acc = b_ref[...] + w[3:4] * cur_ref[...]
        for k in range(3):
            acc = acc + w[k:k + 1] * ext_ref[pl.ds(5 + k, T), 0:width]
        return acc

    valid = _valid_rows(c * T, T, PAD)
    v = {}
    v["valid"] = valid
    v["x_pre"] = conv_pre(xc_ref, xp_ref, wx_ref, bx_ref, GROUP_W)
    v["b_pre"] = conv_pre(bc_ref, bp_ref, wb_ref, bb_ref, D_STATE)
    v["c_pre"] = conv_pre(cc_ref, cp_ref, wc_ref, bcb_ref, D_STATE)
    xs = _silu(v["x_pre"])
    bm = jnp.where(valid, _silu(v["b_pre"]), 0.0)
    cm = jnp.where(valid, _silu(v["c_pre"]), 0.0)
    dtr = dt_ref[...] + dtb_ref[...]
    dt = jnp.where(valid, _softplus(dtr), 0.0)
    a_neg = -jnp.exp(alog_ref[...])
    a = dt * a_neg
    tril = _iota((T, T), 0) >= _iota((T, T), 1)
    cs = _xdot_l(tril.astype(BF16), a)
    hh, ll = _iota((128, GROUP_W), 0), _iota((128, GROUP_W), 1)
    expand = (hh == 8 * g + jnp.right_shift(ll, 6)).astype(BF16)
    sh, sj = _iota((128, 128), 0), _iota((128, 128), 1)
    select = ((sh == 8 * g + sj) & (sj < 8)).astype(BF16)
    hh_t, ll_t = _iota((GROUP_W, 128), 1), _iota((GROUP_W, 128), 0)
    v["expand_t"] = (hh_t == 8 * g + jnp.right_shift(ll_t, 6)).astype(BF16)
    v["select_t"] = ((sj == 8 * g + sh) & (sh < 8)).astype(BF16)
    cs_e = _xdot(cs, expand)
    dt_e = _xdot(dt, expand)
    cs_loc = _xdot(cs, select)
    cs_loc_t = cs_loc.T
    cs_last_e = cs_e[T - 1:T, :]
    v.update(xs=xs, bm=bm, cm=cm, dtr=dtr, dt=dt, a_neg=a_neg, tril=tril, expand=expand, select=select,
             cs_e=cs_e, dt_e=dt_e, cs_loc=cs_loc, cs_loc_t=cs_loc_t, cs_last_e=cs_last_e)
    v["xdt"] = xs * dt_e
    v["decay_e"] = jnp.exp(cs_last_e - cs_e)
    v["ecs_e"] = jnp.exp(cs_e)
    v["elast_e"] = jnp.exp(cs_last_e)
    v["d_e"] = _xdot(dsk_ref[...], expand)
    v["gmat"] = _dot_nt(cm.astype(BF16), bm.astype(BF16))
    return v


def _ssd_decay_pair(v, jp):
    out = []
    for j in (2 * jp, 2 * jp + 1):
        diff = v["cs_loc"][:, j:j + 1] - v["cs_loc_t"][j:j + 1, :]
        out.append(jnp.where(v["tril"], jnp.exp(jnp.where(v["tril"], diff, 0.0)), 0.0))
    return out


def _block_diag_pair(xp):
    lane = _iota(xp.shape, 1)
    return jnp.concatenate([jnp.where(lane < HEAD_P, xp, 0.0), jnp.where(lane >= HEAD_P, xp, 0.0)], axis=0)


def _ssd_fwd(p, conv_w, conv_b, dt_bias, a_log, d_skip, n_chunks):
    n_rows = n_chunks * T
    in_specs, _ = _ssd_specs(n_chunks, rev=False)

    def body(*refs):
        y_ref, hin_ref, st_ref, ext_ref = refs[16:]
        g, c = pl.program_id(0), pl.program_id(1)

        @pl.when(c == 0)
        def _():
            st_ref[...] = jnp.zeros_like(st_ref)

        v = _ssd_chunk_forward(refs[:16], ext_ref, g, c)
        state = st_ref[...]
        hin_ref[...] = state
        ys = []
        for jp in range(4):
            l0, l1 = _ssd_decay_pair(v, jp)
            lhs = jnp.concatenate([v["gmat"] * l0, v["gmat"] * l1], axis=1).astype(BF16)
            rhs = _block_diag_pair(v["xdt"][:, 128 * jp:128 * jp + 128]).astype(BF16)
            ys.append(_dot(lhs, rhs))
        y = jnp.concatenate(ys, axis=1)
        y = y + _dot(v["cm"].astype(BF16), state.astype(BF16)) * v["ecs_e"] + v["xs"] * v["d_e"]
        y_ref[...] = y
        s_new = _dot_tn(v["bm"].astype(BF16), (v["xdt"] * v["decay_e"]).astype(BF16))
        st_ref[...] = state * v["elast_e"] + s_new

    return pl.pallas_call(
        body, name="ssd_fwd", grid=(SSM_GROUPS, n_chunks), in_specs=in_specs,
        out_specs=[pl.BlockSpec((T, GROUP_W), lambda g, c: (c, g)),
                   pl.BlockSpec((None, None, D_STATE, GROUP_W), lambda g, c: (g, c, 0, 0))],
        out_shape=[jax.ShapeDtypeStruct((n_rows, D_INNER), F32),
                   jax.ShapeDtypeStruct((SSM_GROUPS, n_chunks, D_STATE, GROUP_W), F32)],
        scratch_shapes=[pltpu.VMEM((D_STATE, GROUP_W), F32), pltpu.VMEM((T + 8, GROUP_W), F32)],
        compiler_params=_cparams(2),
    )(p, p, p, p, p, p, p, conv_w, conv_w, conv_w, conv_b, conv_b, conv_b, dt_bias, a_log, d_skip)


def _ssd_bwd(p, conv_w, conv_b, dt_bias, a_log, d_skip, hin, dy, n_chunks):
    n_rows = n_chunks * T
    in_specs, cidx = _ssd_specs(n_chunks, rev=True)
    in_specs = in_specs + [pl.BlockSpec((None, None, D_STATE, GROUP_W), lambda g, c: (g, cidx(c), 0, 0)),
                           pl.BlockSpec((T, GROUP_W), lambda g, c: (cidx(c), g))]

    def body(*refs):
        hin_ref, dy_ref = refs[16:18]
        dx_ref, db_ref, dc_ref, ddt_ref, dpar_ref, dst_ref, ext_ref = refs[18:]
        g, step = pl.program_id(0), pl.program_id(1)
        c = n_chunks - 1 - step

        @pl.when(step == 0)
        def _():
            dst_ref[...] = jnp.zeros_like(dst_ref)

        v = _ssd_chunk_forward(refs[:16], ext_ref, g, c)
        hin_f = hin_ref[...]
        hin_b = hin_f.astype(BF16)
        dyv = dy_ref[...]
        dst = dst_ref[...]
        dst_b = dst.astype(BF16)
        xs, bm, cm, xdt = v["xs"], v["bm"], v["cm"], v["xdt"]
        bm_b, cm_b = bm.astype(BF16), cm.astype(BF16)

        dd_e = jnp.sum(dyv * xs, axis=0, keepdims=True)
        dxs = dyv * v["d_e"]
        ch = _dot(cm_b, hin_b)
        dch = (dyv * v["ecs_e"]).astype(BF16)
        dcm = _dot_nt(dch, hin_b)
        dhin = _dot_tn(cm_b, dch) + dst * v["elast_e"]
        dcs_e = dyv * ch * v["ecs_e"]
        dxd = _dot(bm_b, dst_b)
        dbm = _dot_nt((xdt * v["decay_e"]).astype(BF16), dst_b)
        dxdt_state = dxd * v["decay_e"]
        q = dxdt_state * xdt
        dcs_e = dcs_e - q
        dlast_e = jnp.sum(q, axis=0, keepdims=True) + jnp.sum(dst * hin_f, axis=0, keepdims=True) * v["elast_e"]
        dg = jnp.zeros((T, T), F32)
        rs_cols = jnp.zeros((T, 128), F32)
        cs_rows = jnp.zeros((128, T), F32)
        lane_i, sub_i = _iota((T, 128), 1), _iota((128, T), 0)
        dxdt_parts = []
        for jp in range(4):
            l0, l1 = _ssd_decay_pair(v, jp)
            m0, m1 = v["gmat"] * l0, v["gmat"] * l1
            xbd = _block_diag_pair(xdt[:, 128 * jp:128 * jp + 128]).astype(BF16)
            dyp = dyv[:, 128 * jp:128 * jp + 128]
            dm = _dot_nt(dyp.astype(BF16), xbd)
            dm0, dm1 = dm[:, 0:T], dm[:, T:2 * T]
            dg = dg + dm0 * l0 + dm1 * l1
            for j, qq in ((2 * jp, dm0 * m0), (2 * jp + 1, dm1 * m1)):
                rs_cols = jnp.where(lane_i == j, jnp.sum(qq, axis=1, keepdims=True), rs_cols)
                cs_rows = jnp.where(sub_i == j, jnp.sum(qq, axis=0, keepdims=True), cs_rows)
            mv = jnp.concatenate([m0, m1], axis=0).astype(BF16)
            dxdt_parts.append(_dot_tn(mv, _block_diag_pair(dyp).astype(BF16)))
        dxdt = jnp.concatenate(dxdt_parts, axis=1) + dxdt_state
        dg_b = dg.astype(BF16)
        dcm = dcm + _dot(dg_b, bm_b)
        dbm = dbm + _dot_tn(dg_b, cm_b)
        expand_t = v["expand_t"]
        dcs_loc = rs_cols - cs_rows.T
        last_row = _iota((T, 1), 0) == T - 1
        dcs_full_e = dcs_e + jnp.where(last_row, dlast_e, 0.0)
        dcs = _xdot(dcs_full_e, expand_t) + _xdot(dcs_loc, v["select_t"])
        triu = (_iota((T, T), 0) <= _iota((T, T), 1)).astype(BF16)
        da = _xdot_l(triu, dcs)
        ddt = da * v["a_neg"] + _xdot(dxdt * xs, expand_t)
        dxs = dxs + dxdt * v["dt_e"]
        ddtr = jnp.where(v["valid"], ddt * _sigmoid(v["dtr"]), 0.0)
        dx_ref[...] = dxs
        db_ref[...] = jnp.where(v["valid"], dbm, 0.0)
        dc_ref[...] = jnp.where(v["valid"], dcm, 0.0)
        ddt_ref[...] = ddtr
        dpar = jnp.concatenate([
            jnp.sum(ddtr, axis=0, keepdims=True),
            jnp.sum(da * v["dt"], axis=0, keepdims=True) * v["a_neg"],
            _xdot(dd_e, expand_t),
            jnp.zeros((5, 128), F32)], axis=0)

        @pl.when(step == 0)
        def _():
            dpar_ref[...] = dpar

        @pl.when(step > 0)
        def _():
            dpar_ref[...] += dpar

        dst_ref[...] = dhin

    return pl.pallas_call(
        body, name="ssd_bwd", grid=(SSM_GROUPS, n_chunks), in_specs=in_specs,
        out_specs=[pl.BlockSpec((T, GROUP_W), lambda g, c: (cidx(c), g)),
                   pl.BlockSpec((T, D_STATE), lambda g, c: (cidx(c), g)),
                   pl.BlockSpec((T, D_STATE), lambda g, c: (cidx(c), g)),
                   pl.BlockSpec((T, 128), lambda g, c: (cidx(c), g)),
                   pl.BlockSpec((None, 8, 128), lambda g, c: (g, 0, 0))],
        out_shape=[jax.ShapeDtypeStruct((n_rows, D_INNER), F32),
                   jax.ShapeDtypeStruct((n_rows, SSM_GROUPS * D_STATE), F32),
                   jax.ShapeDtypeStruct((n_rows, SSM_GROUPS * D_STATE), F32),
                   jax.ShapeDtypeStruct((n_rows, SSM_GROUPS * 128), F32),
                   jax.ShapeDtypeStruct((SSM_GROUPS, 8, 128), F32)],
        scratch_shapes=[pltpu.VMEM((D_STATE, GROUP_W), F32), pltpu.VMEM((T + 8, GROUP_W), F32)],
        compiler_params=_cparams(2),
    )(p, p, p, p, p, p, p, conv_w, conv_w, conv_w, conv_b, conv_b, conv_b, dt_bias, a_log, d_skip, hin, dy)


def _alibi_slope(h):
    return 2.0 ** (-8.0 * (h + 1) / ATTN_HEADS)


def _dup_half(x256, kvh):
    xb = x256[:, 128 * (kvh // 2):128 * (kvh // 2) + 128]
    rolled = pltpu.roll(xb, 64, 1)
    lane = _iota(xb.shape, 1)
    if kvh % 2 == 0:
        return jnp.where(lane < 64, xb, rolled)
    return jnp.where(lane < 64, rolled, xb)


def _attn_masks(c):
    qi = _iota((T, 3 * T), 0)
    jj = _iota((T, 3 * T), 1)
    blk = jnp.right_shift(jj, 7)
    j = jnp.bitwise_and(jj, T - 1)
    q_pos = c * T + qi - PAD
    k_pos = (c - 2 + blk) * T + j - PAD
    dist = q_pos - k_pos
    band = (blk > 0) & (dist >= 0) & (dist < T) & (k_pos >= N_META)
    meta = (blk == 0) & (j >= PAD) & (j - PAD <= q_pos)
    distf = jnp.where(blk > 0, dist, 0).astype(F32)
    return band | meta, distf


def _attn_scores(qp, k3, allowed, distf, h0):
    lane = _iota(qp.shape, 1)
    s = []
    for half, h in ((0, h0), (1, h0 + 1)):
        qh = jnp.where((lane < 64) if half == 0 else (lane >= 64), qp, 0.0).astype(BF16)
        sc = _dot_nt(qh, k3) - _alibi_slope(h) * distf
        s.append((qh, jnp.where(allowed, sc, NEG)))
    return s


def _attn_fwd(p, sinks, n_chunks):
    n_rows = n_chunks * T
    kb, vb = P_K // KV_W, P_V // KV_W

    def body(q_ref, kc_ref, kp_ref, km_ref, vc_ref, vp_ref, vm_ref, sink_ref, o_ref, lse_ref):
        c = pl.program_id(0)
        allowed, distf = _attn_masks(c)
        q = q_ref[...] * SCALE
        sinks_v = sink_ref[...]
        lane = _iota((T, 128), 1)
        lse_all = jnp.zeros((T, 128), F32)
        outs = []
        for kvh in range(KV_HEADS):
            k3 = jnp.concatenate([_dup_half(r[...], kvh) for r in (km_ref, kp_ref, kc_ref)], axis=0).astype(BF16)
            v3 = jnp.concatenate([_dup_half(r[...], kvh) for r in (vm_ref, vp_ref, vc_ref)], axis=0)
            v3bd = _block_diag_rows(v3).astype(BF16)
            for pr in range(2):
                h0 = 4 * kvh + 2 * pr
                blk = 2 * kvh + pr
                qp = q[:, 128 * blk:128 * blk + 128]
                probs = []
                for (_, sc), h in zip(_attn_scores(qp, k3, allowed, distf, h0), (h0, h0 + 1)):
                    sink = sinks_v[:, h:h + 1]
                    m = jnp.maximum(jnp.max(sc, axis=1, keepdims=True), sink)
                    e = jnp.exp(sc - m)
                    den = jnp.sum(e, axis=1, keepdims=True) + jnp.exp(sink - m)
                    probs.append(e / den)
                    lse_all = jnp.where(lane == h, m + jnp.log(den), lse_all)
                outs.append(_dot(jnp.concatenate(probs, axis=1).astype(BF16), v3bd))
        o_ref[...] = jnp.concatenate(outs, axis=1).astype(o_ref.dtype)
        lse_ref[...] = lse_all

    blk = lambda width, col: pl.BlockSpec((T, width), lambda c: (c, col))
    prev = lambda width, col: pl.BlockSpec((T, width), lambda c: (jnp.maximum(c - 1, 0), col))
    first = lambda width, col: pl.BlockSpec((T, width), lambda c: (0, col))
    return pl.pallas_call(
        body, name="attn_fwd", grid=(n_chunks,),
        in_specs=[blk(ATTN_W, P_Q // ATTN_W), blk(KV_W, kb), prev(KV_W, kb), first(KV_W, kb),
                  blk(KV_W, vb), prev(KV_W, vb), first(KV_W, vb), pl.BlockSpec((1, 128), lambda c: (0, 0))],
        out_specs=[pl.BlockSpec((T, ATTN_W), lambda c: (c, 0)), pl.BlockSpec((T, 128), lambda c: (c, 0))],
        out_shape=[jax.ShapeDtypeStruct((n_rows, ATTN_W), BF16), jax.ShapeDtypeStruct((n_rows, 128), F32)],
        compiler_params=_cparams(1),
    )(p, p, p, p, p, p, p, sinks)


def _block_diag_rows(x3):
    lane = _iota(x3.shape, 1)
    return jnp.concatenate([jnp.where(lane < 64, x3, 0.0), jnp.where(lane >= 64, x3, 0.0)], axis=0)


def _fold_halves(x):
    return x + pltpu.roll(x, 64, 1)


def _attn_bwd(p, sinks, ao, lse, dao, dp, n_chunks):
    kb, vb = P_K // KV_W, P_V // KV_W
    rc = lambda s: n_chunks - 1 - s

    def body(q_ref, kc_ref, kp_ref, km_ref, vc_ref, vp_ref, vm_ref, sink_ref, o_ref, lse_ref, do_ref, dp_in_ref,
             dqkv_ref, dsink_ref, kcar_ref, vcar_ref, kmeta_ref, vmeta_ref):
        step = pl.program_id(0)
        c = n_chunks - 1 - step

        @pl.when(step == 0)
        def _():
            for r in (kcar_ref, vcar_ref, kmeta_ref, vmeta_ref):
                r[...] = jnp.zeros_like(r)

        allowed, distf = _attn_masks(c)
        q = q_ref[...] * SCALE
        sinks_v = sink_ref[...]
        lse_v = lse_ref[...]
        ov = o_ref[...].astype(F32)
        dov = do_ref[...].astype(F32)
        lane = _iota((T, 128), 1)
        lane256 = _iota((3 * T, KV_W), 1)
        dsink = jnp.zeros((1, 128), F32)
        dk3_all = jnp.zeros((3 * T, KV_W), F32)
        dv3_all = jnp.zeros((3 * T, KV_W), F32)
        dqs = []
        for kvh in range(KV_HEADS):
            k3 = jnp.concatenate([_dup_half(r[...], kvh) for r in (km_ref, kp_ref, kc_ref)], axis=0).astype(BF16)
            v3 = jnp.concatenate([_dup_half(r[...], kvh) for r in (vm_ref, vp_ref, vc_ref)], axis=0).astype(BF16)
            dk3 = jnp.zeros((3 * T, 128), F32)
            dv3 = jnp.zeros((3 * T, 128), F32)
            for pr in range(2):
                h0 = 4 * kvh + 2 * pr
                blk = 2 * kvh + pr
                qp = q[:, 128 * blk:128 * blk + 128]
                dop = dov[:, 128 * blk:128 * blk + 128]
                prod = dop * ov[:, 128 * blk:128 * blk + 128]
                dq_pair = jnp.zeros((T, 128), F32)
                for half, ((qh, sc), h) in enumerate(zip(_attn_scores(qp, k3, allowed, distf, h0), (h0, h0 + 1))):
                    mine = (lane < 64) if half == 0 else (lane >= 64)
                    lse_h = lse_v[:, h:h + 1]
                    pm = jnp.exp(sc - lse_h)
                    doh = jnp.where(mine, dop, 0.0).astype(BF16)
                    delta = jnp.sum(jnp.where(mine, prod, 0.0), axis=1, keepdims=True)
                    dp = _dot_nt(doh, v3)
                    ds = (pm * (dp - delta)).astype(BF16)
                    p_sink = jnp.exp(sinks_v[:, h:h + 1] - lse_h)
                    dsink = jnp.where(_iota((1, 128), 1) == h, jnp.sum(-p_sink * delta, axis=0, keepdims=True), dsink)
                    dq_pair = jnp.where(mine, _dot(ds, k3), dq_pair)
                    dk3 = dk3 + _dot_tn(ds, qh)
                    dv3 = dv3 + _dot_tn(pm.astype(BF16), doh)
                dqs.append(dq_pair * SCALE)
            in_place = (lane256 >= 64 * kvh) & (lane256 < 64 * kvh + 64)
            wide = lambda x: jnp.concatenate([x, x], axis=1)
            dk3_all = jnp.where(in_place, wide(_fold_halves(dk3)), dk3_all)
            dv3_all = jnp.where(in_place, wide(_fold_halves(dv3)), dv3_all)
        dsink_all = dsink

        @pl.when(step == 0)
        def _():
            dsink_ref[...] = dsink_all

        @pl.when(step > 0)
        def _():
            dsink_ref[...] += dsink_all

        kmeta = kmeta_ref[...] + dk3_all[0:T]
        vmeta = vmeta_ref[...] + dv3_all[0:T]
        kmeta_ref[...] = kmeta
        vmeta_ref[...] = vmeta
        is_first = c == 0
        dk = jnp.where(is_first, kmeta, dk3_all[2 * T:3 * T] + kcar_ref[...])
        dv = jnp.where(is_first, vmeta, dv3_all[2 * T:3 * T] + vcar_ref[...])
        dqkv_ref[...] = jnp.concatenate(dqs + [dk, dv], axis=1).astype(dqkv_ref.dtype)
        kcar_ref[...] = dk3_all[T:2 * T]
        vcar_ref[...] = dv3_all[T:2 * T]

    blk = lambda width, col: pl.BlockSpec((T, width), lambda s: (rc(s), col))
    prev = lambda width, col: pl.BlockSpec((T, width), lambda s: (jnp.maximum(rc(s) - 1, 0), col))
    first = lambda width, col: pl.BlockSpec((T, width), lambda s: (0, col))
    return pl.pallas_call(
        body, name="attn_bwd", grid=(n_chunks,),
        in_specs=[blk(ATTN_W, P_Q // ATTN_W), blk(KV_W, kb), prev(KV_W, kb), first(KV_W, kb),
                  blk(KV_W, vb), prev(KV_W, vb), first(KV_W, vb), pl.BlockSpec((1, 128), lambda s: (0, 0)),
                  blk(ATTN_W, 0), blk(128, 0), blk(ATTN_W, 0), pl.BlockSpec(memory_space=pl.ANY)],
        out_specs=[blk(QKV_W, P_Q // QKV_W), pl.BlockSpec((1, 128), lambda s: (0, 0))],
        out_shape=[jax.ShapeDtypeStruct(dp.shape, dp.dtype), jax.ShapeDtypeStruct((1, 128), F32)],
        input_output_aliases={11: 0},
        scratch_shapes=[pltpu.VMEM((T, KV_W), F32)] * 4,
        compiler_params=_cparams(1),
    )(p, p, p, p, p, p, p, sinks, ao, lse, dao, dp)


def _pad_lanes(v, width=128):
    return jnp.pad(v, ((0, 0), (0, width - v.shape[1])))


def _local_step(x, head, tgt, w):
    n_tok = x.shape[0]
    n_rows = n_tok + T
    n_chunks = n_rows // T
    tm = _row_tile(n_rows, 384)
    dt_bias, a_log, d_skip = (_pad_lanes(w[k]) for k in ("ssm_dt_bias", "ssm_a_log", "ssm_d_skip"))
    sinks = _pad_lanes(w["attn_sinks"])
    x_in = (x, D_MODEL, 0, "prev")

    def h0_tile(r0, xt, hd):
        return jnp.where(r0 == 0, hd, xt)

    n1, = _rowwise("norm_pre_mix", lambda r0, xt, hd, wn: [_rms(h0_tile(r0, xt, hd), wn)], n_rows, T,
                   [x_in], [head, w["norm_pre_mix"]], [(D_MODEL, BF16)], [])
    p = _matmul("in_proj", n1, w["w_cat"], "nn", F32)
    y_ssd, hin = _ssd_fwd(p, w["ssm_conv_w"], w["ssm_conv_b"], dt_bias, a_log, d_skip, n_chunks)
    ao, lse = _attn_fwd(p, sinks, n_chunks)

    def gate_norm(r0, y, z, wn):
        return [_rms(y * _silu(z), wn)]

    yn, = _rowwise("ssm_gate_norm", gate_norm, n_rows, tm, [(y_ssd, D_INNER, 0), (p, D_INNER, P_Z // D_INNER)],
                   [w["ssm_norm"]], [(D_INNER, BF16)], [])
    y_ssm = _matmul("ssm_out", yn, w["w_ssm_out"], "nn", F32)
    y_attn = _matmul("attn_out", ao, w["w_attn_out"], "nn", F32)

    def mix_gate(r0, ys, ya, gs, ga):
        return [_sigmoid(gs) * ys + _sigmoid(ga) * ya]

    gate_ins = [(p, D_MODEL, P_GATE // D_MODEL), (p, D_MODEL, P_GATE // D_MODEL + 1)]
    mixed, = _rowwise("mix_gate", mix_gate, n_rows, tm, [(y_ssm, D_MODEL, 0), (y_attn, D_MODEL, 0)] + gate_ins,
                      [], [(D_MODEL, BF16)], [])
    mix = _matmul("mix_out", mixed, w["w_mix_out"], "nn", F32)

    def post_mix(r0, mx, xt, hd, w_post, w_pre):
        h1 = jnp.where(_valid_rows(r0, T, PAD), h0_tile(r0, xt, hd) + _rms(mx, w_post), 0.0)
        return [h1, _rms(h1, w_pre)]

    h1, n2 = _rowwise("post_mix", post_mix, n_rows, T, [(mix, D_MODEL, 0), x_in],
                      [head, w["norm_post_mix"], w["norm_pre_ffn"]], [(D_MODEL, F32), (D_MODEL, BF16)], [])
    u_raw = _matmul("ffn_up", n2, w["w_ffn_up"], "nn", F32)
    f = _ffn_act("ffn_act", u_raw, w["ffn_conv_w"], w["ffn_conv_b"], n_rows)
    ffn = _matmul("ffn_down", f, w["w_ffn_down"], "nn", F32)

    def final(r0, fo, h, t, w_post):
        real = r0 > 0
        err = jnp.where(real, h + _rms(fo, w_post) - t, 0.0)
        dy = err * (1.0 / D_MODEL)
        dffn, dw = _rms_bwd(dy, fo, w_post)
        return [dffn, dy, jnp.sum(err * err, axis=0, keepdims=True), dw]

    dffn, dh2, loss_cols, g_norm_post_ffn = _rowwise(
        "loss_head", final, n_rows, T, [(ffn, D_MODEL, 0), (h1, D_MODEL, 0), (tgt, D_MODEL, 0, "prev")],
        [w["norm_post_ffn"]], [(D_MODEL, BF16), (D_MODEL, F32)], [D_MODEL, D_MODEL])

    g = {"norm_post_ffn": g_norm_post_ffn}
    g["w_ffn_down"] = _matmul("ffn_down_dw", f, dffn, "tn", F32)
    df = _matmul("ffn_down_dx", dffn, w["w_ffn_down"], "nt", F32)
    ffn_chunk = FFN_DIM // 2
    du_raw, dconv = _conv_bwd("ffn_act_bwd", u_raw, 0, [df], [(0, 0), (0, ffn_chunk)], ffn_chunk,
                              w["ffn_conv_w"], w["ffn_conv_b"], n_rows, True)
    g["ffn_conv_w"], g["ffn_conv_b"] = dconv[0:3], dconv[3:4]
    g["w_ffn_up"] = _matmul("ffn_up_dw", n2, du_raw, "tn", F32)
    dn2 = _matmul("ffn_up_dx", du_raw, w["w_ffn_up"], "nt", F32)

    def post_mix_bwd(r0, dn, d2, h, mx, w_pre, w_post):
        dx, dw_pre = _rms_bwd(dn, h, w_pre)
        dh1 = jnp.where(_valid_rows(r0, tm, PAD), dx + d2, 0.0)
        dmix, dw_post = _rms_bwd(dh1, mx, w_post)
        return [dh1, dmix, dw_pre, dw_post]

    dh1, dmix, g["norm_pre_ffn"], g["norm_post_mix"] = _rowwise(
        "post_mix_bwd", post_mix_bwd, n_rows, tm,
        [(dn2, D_MODEL, 0), (dh2, D_MODEL, 0), (h1, D_MODEL, 0), (mix, D_MODEL, 0)],
        [w["norm_pre_ffn"], w["norm_post_mix"]], [(D_MODEL, F32), (D_MODEL, BF16)], [D_MODEL, D_MODEL])
    g["w_mix_out"] = _matmul("mix_out_dw", mixed, dmix, "tn", F32)
    dmixed = _matmul("mix_out_dx", dmix, w["w_mix_out"], "nt", F32)

    def mix_gate_bwd(r0, dm, ys, ya, gs, ga):
        ss, sa = _sigmoid(gs), _sigmoid(ga)
        dgate = jnp.concatenate([dm * ys * ss * (1.0 - ss), dm * ya * sa * (1.0 - sa)], axis=1)
        return [dm * ss, dm * sa, dgate]

    dys, dya, dp = _rowwise(
        "mix_gate_bwd", mix_gate_bwd, n_rows, tm,
        [(dmixed, D_MODEL, 0), (y_ssm, D_MODEL, 0), (y_attn, D_MODEL, 0)] + gate_ins,
        [], [(D_MODEL, BF16), (D_MODEL, BF16), (2 * D_MODEL, BF16, "new", P_W, P_GATE // (2 * D_MODEL))], [])
    g["w_ssm_out"] = _matmul("ssm_out_dw", yn, dys, "tn", F32)
    dyn = _matmul("ssm_out_dx", dys, w["w_ssm_out"], "nt", F32)
    g["w_attn_out"] = _matmul("attn_out_dw", ao, dya, "tn", F32)
    dao = _matmul("attn_out_dx", dya, w["w_attn_out"], "nt", BF16)

    def gate_norm_bwd(r0, dn, y, z, wn):
        sz = _silu(z)
        dyz, dw = _rms_bwd(dn, y * sz, wn)
        live = _valid_rows(r0, tm, PAD)
        return [jnp.where(live, dyz * sz, 0.0), jnp.where(live, dyz * y * _dsilu(z), 0.0), dw]

    dy_ssd, dp, g["ssm_norm"] = _rowwise(
        "ssm_gate_norm_bwd", gate_norm_bwd, n_rows, tm,
        [(dyn, D_INNER, 0), (y_ssd, D_INNER, 0), (p, D_INNER, P_Z // D_INNER)],
        [w["ssm_norm"]], [(D_INNER, F32), (D_INNER, BF16, "into", dp, P_Z // D_INNER)], [D_INNER])
    dp, dsink = _attn_bwd(p, sinks, ao, lse, dao, dp, n_chunks)
    g["attn_sinks"] = dsink[:, 0:ATTN_HEADS]
    dxs, dbm, dcm, ddt_parts, dpar = _ssd_bwd(p, w["ssm_conv_w"], w["ssm_conv_b"], dt_bias, a_log, d_skip, hin,
                                              dy_ssd, n_chunks)
    dpar = jnp.sum(dpar, axis=0)
    g["ssm_dt_bias"], g["ssm_a_log"], g["ssm_d_skip"] = (dpar[i:i + 1, 0:SSM_HEADS] for i in range(3))

    def dt_grad(r0, parts):
        tot = parts[:, 0:128] + parts[:, 128:256] + parts[:, 256:384] + parts[:, 384:512]
        return [jnp.concatenate([tot, jnp.zeros((tm, P_Z - P_DT - 128), F32)], axis=1)]

    dt_w = P_Z - P_DT
    dp, = _rowwise("dt_grad", dt_grad, n_rows, tm, [(ddt_parts, SSM_GROUPS * 128, 0)], [],
                   [(dt_w, BF16, "into", dp, P_DT // dt_w)], [])
    x_chunks = [(0, GROUP_W * q) for q in range(SSM_GROUPS)] + [(1, 0), (2, 0)]
    dp, dconv = _conv_bwd("ssm_conv_bwd", p, P_XBC // CONV_DIM, [dxs, dbm, dcm], x_chunks, GROUP_W,
                          w["ssm_conv_w"], w["ssm_conv_b"], n_rows, False, into=dp, into_blk=P_XBC // CONV_DIM)
    g["ssm_conv_w"], g["ssm_conv_b"] = dconv[0:4], dconv[4:5]
    g["w_cat"] = _matmul("in_proj_dw", n1, dp, "tn", F32)
    dn1 = _matmul("in_proj_dx", dp, w["w_cat"], "nt", F32)

    def pre_mix_bwd(r0, dn, d1, xt, hd, wn):
        dx, dw = _rms_bwd(dn, h0_tile(r0, xt, hd), wn)
        dh0 = jnp.where(_valid_rows(r0, T, PAD), dx + d1, 0.0)
        return [dh0, dh0, dw]

    dx_out, dhead, g["norm_pre_mix"] = _rowwise(
        "pre_mix_bwd", pre_mix_bwd, n_rows, T, [(dn1, D_MODEL, 0), (dh1, D_MODEL, 0), x_in],
        [head, w["norm_pre_mix"]], [(D_MODEL, F32, "prev", n_tok), (D_MODEL, F32, "first")], [D_MODEL])
    return jnp.sum(loss_cols), dx_out, dhead, g


_IN_SECTIONS = [((5152, 6176), P_Q), ((6176, 6432), P_K), ((6432, 6688), P_V), ((5120, 5152), P_DT),
                ((0, 2048), P_Z), ((6688, 8736), P_GATE), ((2048, 5120), P_XBC)]


def _to_cat(w_in):
    parts, at = [], 0
    for (a, b), off in _IN_SECTIONS:
        if off > at:
            parts.append(jnp.zeros((w_in.shape[0], off - at), w_in.dtype))
        parts.append(w_in[:, a:b])
        at = off + (b - a)
    return jnp.concatenate(parts, axis=1)


def _from_cat(g_cat):
    pieces = {a: g_cat[:, off:off + (b - a)] for (a, b), off in _IN_SECTIONS}
    return jnp.concatenate([pieces[a] for a in sorted(pieces)], axis=1)


LANES = 1024
_BIG = [("w_in", 1024, 2184, "chip"), ("w_ssm_out", 512, 1024, "row"), ("w_attn_out", 256, 1024, "row"),
        ("w_mix_out", 256, 1024, "row"), ("w_ffn_up", 1024, 1408, "col"), ("w_ffn_down", 704, 1024, "row"),
        ("small", 32, LANES, "chip")]
_SMALL_SHARDED = [("ssm_conv_w", (4, 768), 1), ("ffn_conv_w", (3, 1408), 1), ("meta_tokens", (16, 256), 1)]
_REPLICATED = [("norm_pre_mix", 1024), ("ssm_conv_b", 3072), ("ssm_dt_bias", 32), ("ssm_a_log", 32),
               ("ssm_d_skip", 32), ("ssm_norm", 2048), ("attn_sinks", 16), ("norm_post_mix", 1024),
               ("norm_pre_ffn", 1024), ("ffn_conv_b", 5632), ("norm_post_ffn", 1024)]
SMALL_ROWS = 16
WEIGHT_ORDER = ["meta_tokens", "norm_pre_mix", "w_in", "ssm_conv_w", "ssm_conv_b", "ssm_dt_bias", "ssm_a_log",
                "ssm_d_skip", "ssm_norm", "w_ssm_out", "attn_sinks", "w_attn_out", "w_mix_out", "norm_post_mix",
                "norm_pre_ffn", "w_ffn_up", "ffn_conv_w", "ffn_conv_b", "w_ffn_down", "norm_post_ffn"]


def _flatten(parts, rows):
    flat = jnp.concatenate([a.reshape(-1) for a in parts])
    return jnp.pad(flat, (0, rows * LANES - flat.shape[0])).reshape(rows, LANES)


def _unflatten(flat, shapes):
    flat = flat.reshape(-1)
    out, off = [], 0
    for shp in shapes:
        n = math.prod(shp)
        out.append(flat[off:off + n].reshape(shp))
        off += n
    return out


def _shard_of(full, chip, shape, axis):
    return lax.slice_in_dim(full, chip * shape[axis], (chip + 1) * shape[axis], axis=axis)


def _full_shape(r, c, layout):
    return {"row": (4 * r, c), "col": (r, 4 * c), "chip": (4, r, c)}[layout]


def _shard_view(ref, r, c, layout, chip):
    if layout == "row":
        return ref.at[pl.ds(pl.multiple_of(chip * r, 16), r), :]
    if layout == "col":
        return ref.at[:, pl.ds(pl.multiple_of(chip * c, 128), c)]
    return ref.at[chip]


def _half_view(ref, r, c, layout, chip, half):
    hr = r // 2
    if layout == "row":
        return ref.at[pl.ds(pl.multiple_of(chip * r + half * hr, 16), hr), :]
    r0 = pl.multiple_of(half * hr, 16)
    if layout == "col":
        return ref.at[pl.ds(r0, hr), pl.ds(pl.multiple_of(chip * c, 128), c)]
    return ref.at[chip, pl.ds(r0, hr), :]


def _mesh_pos():
    return lax.axis_index("x"), lax.axis_index("y"), lax.axis_index("c")


def _other_chips(x, y):
    return [(1 - x, y), (x, 1 - y), (1 - x, 1 - y)]


def _chip_index(x, y):
    return 2 * x + y


ANY = pl.BlockSpec(memory_space=pl.ANY)


def _run_exchange(name, make_copies, n_copies, ins, out_shapes):
    n_in = len(ins)
    n_out = len(out_shapes)

    def body(*refs):
        in_refs, out_refs = refs[:n_in], refs[n_in:n_in + n_out]
        send_sems, recv_sems = refs[n_in + n_out:]
        copies = [pltpu.make_async_remote_copy(src_ref=s, dst_ref=d, send_sem=send_sems.at[i], recv_sem=recv_sems.at[i],
                                               device_id=dev, device_id_type=MESH)
                  for i, (s, d, dev) in enumerate(make_copies(in_refs, out_refs))]
        assert len(copies) == n_copies
        for cp in copies:
            cp.start()
        for cp in copies:
            cp.wait()

    return pl.pallas_call(
        body, name=name, in_specs=[ANY] * n_in, out_specs=[ANY] * n_out, out_shape=out_shapes,
        scratch_shapes=[pltpu.SemaphoreType.DMA((n_copies,)), pltpu.SemaphoreType.DMA((n_copies,))],
        compiler_params=pltpu.CompilerParams(has_side_effects=True),
    )(*ins)


def _gather_weights(shards):
    n = len(_BIG)

    def body(*refs):
        ins, outs = refs[:n], refs[n:2 * n]
        send_sems, recv_sems, local_sems = refs[2 * n:]
        x, y, c = _mesh_pos()
        j = _chip_index(x, y)
        sibling = (x, y, 1 - c)
        chips = _other_chips(x, y)
        idx = [_chip_index(*ch) for ch in chips]

        def remote(k, src, dst, dev):
            return pltpu.make_async_remote_copy(src_ref=src, dst_ref=dst, send_sem=send_sems.at[k],
                                                recv_sem=recv_sems.at[k], device_id=dev, device_id_type=MESH)

        own = [pltpu.make_async_copy(ins[a], _shard_view(outs[a], r, cc, lay, j), local_sems.at[a])
               for a, (_, r, cc, lay) in enumerate(_BIG)]
        for cp in own:
            cp.start()
        first, passed = [], []
        for a, (_, r, cc, lay) in enumerate(_BIG):
            mine = ins[a].at[pl.ds(pl.multiple_of(c * (r // 2), 16), r // 2), :]
            for k, ch in enumerate(chips):
                first.append(remote(6 * a + k, mine, _half_view(outs[a], r, cc, lay, j, c), (*ch, c)))
                landed = _half_view(outs[a], r, cc, lay, idx[k], c)
                passed.append(remote(6 * a + 3 + k, landed, landed, sibling))
        for cp in first:
            cp.start()
        for a, (_, r, cc, lay) in enumerate(_BIG):
            for k in range(3):
                landed = _half_view(outs[a], r, cc, lay, idx[k], c)
                remote(6 * a + k, landed, landed, sibling).wait_recv()
                passed[3 * a + k].start()
        for a, (_, r, cc, lay) in enumerate(_BIG):
            for k in range(3):
                theirs = _half_view(outs[a], r, cc, lay, idx[k], 1 - c)
                remote(6 * a + 3 + k, theirs, theirs, sibling).wait_recv()
        for cp in first + passed:
            cp.wait_send()
        for cp in own:
            cp.wait()

    return pl.pallas_call(
        body, name="gather_weights", in_specs=[ANY] * n, out_specs=[ANY] * n,
        out_shape=[jax.ShapeDtypeStruct(_full_shape(r, cc, lay), s.dtype) for s, (_, r, cc, lay) in zip(shards, _BIG)],
        scratch_shapes=[pltpu.SemaphoreType.DMA((6 * n,)), pltpu.SemaphoreType.DMA((6 * n,)), pltpu.SemaphoreType.DMA((n,))],
        compiler_params=pltpu.CompilerParams(has_side_effects=True),
    )(*shards)


def _grad_pair_exchange(grads, rep):
    n = len(_BIG)

    def make(in_refs, out_refs):
        x, y, c = _mesh_pos()
        sibling = (x, y, 1 - c)
        copies = [(_half_view(in_refs[a], r, cc, lay, i, 1 - c), out_refs[a].at[i], sibling)
                  for a, (_, r, cc, lay) in enumerate(_BIG) for i in range(4)]
        return copies + [(in_refs[n], out_refs[n], sibling)]

    shapes = [jax.ShapeDtypeStruct((4, r // 2, cc), F32) for _, r, cc, _ in _BIG]
    return _run_exchange("grad_pair_exchange", make, 4 * n + 1, list(grads) + [rep],
                         shapes + [jax.ShapeDtypeStruct(rep.shape, F32)])


def _grad_chip_exchange(psends, prep):
    n = len(psends)

    def make(in_refs, out_refs):
        x, y, c = _mesh_pos()
        chips = _other_chips(x, y)
        copies = [(in_refs[a].at[_chip_index(*ch)], out_refs[a].at[k], (*ch, c))
                  for a in range(n) for k, ch in enumerate(chips)]
        return copies + [(in_refs[n], out_refs[n].at[k], (*ch, c)) for k, ch in enumerate(chips)]

    shapes = [jax.ShapeDtypeStruct((3,) + p.shape[1:], p.dtype) for p in psends]
    return _run_exchange("grad_chip_exchange", make, 3 * n + 3, list(psends) + [prep],
                         shapes + [jax.ShapeDtypeStruct((3,) + prep.shape, prep.dtype)])


def _grad_half_share(halves):
    def make(in_refs, out_refs):
        x, y, c = _mesh_pos()
        return [(r, o, (x, y, 1 - c)) for r, o in zip(in_refs, out_refs)]

    return _run_exchange("grad_half_share", make, len(halves), list(halves),
                         [jax.ShapeDtypeStruct(h.shape, h.dtype) for h in halves])


SUM_ROWS = 256
ADAM_ROWS = 128


def _pair_sum(name, grad, recv, ids, r, c, layout):
    hr = r // 2
    tr = _row_tile(hr, SUM_ROWS)
    nb = hr // tr

    def body(ids_ref, g_ref, r_ref, send_ref, own_ref):
        s = g_ref[...] + r_ref[...]
        send_ref[...] = s.astype(send_ref.dtype)

        @pl.when(pl.program_id(1) == ids_ref[1])
        def _():
            own_ref[...] = s

    if layout == "row":
        g_spec = pl.BlockSpec((tr, c), lambda t, j, ids_ref: ((j * r + ids_ref[0] * hr) // tr + t, 0))
    elif layout == "col":
        g_spec = pl.BlockSpec((tr, c), lambda t, j, ids_ref: (ids_ref[0] * nb + t, j))
    else:
        g_spec = pl.BlockSpec((None, tr, c), lambda t, j, ids_ref: (j, ids_ref[0] * nb + t, 0))
    grid_spec = pltpu.PrefetchScalarGridSpec(
        num_scalar_prefetch=1, grid=(nb, 4),
        in_specs=[g_spec, pl.BlockSpec((None, tr, c), lambda t, j, ids_ref: (j, t, 0))],
        out_specs=[pl.BlockSpec((None, tr, c), lambda t, j, ids_ref: (j, t, 0)),
                   pl.BlockSpec((tr, c), lambda t, j, ids_ref: (t, 0))])
    return pl.pallas_call(
        body, name=name, grid_spec=grid_spec,
        out_shape=[jax.ShapeDtypeStruct((4, hr, c), BF16), jax.ShapeDtypeStruct((hr, c), F32)],
        compiler_params=_cparams(2),
    )(ids, grad, recv)


def _chip_sum(name, own, recv):
    hr, c = own.shape
    tr = _row_tile(hr, SUM_ROWS)

    def body(o_ref, r_ref, out_ref):
        out_ref[...] = ((o_ref[...] + r_ref[0].astype(F32)) + r_ref[1].astype(F32)) + r_ref[2].astype(F32)

    return pl.pallas_call(
        body, name=name, grid=(hr // tr,),
        in_specs=[pl.BlockSpec((tr, c), lambda i: (i, 0)), pl.BlockSpec((3, tr, c), lambda i: (0, i, 0))],
        out_specs=pl.BlockSpec((tr, c), lambda i: (i, 0)),
        out_shape=jax.ShapeDtypeStruct((hr, c), F32), compiler_params=_cparams(1),
    )(own, recv)


def _chip_sum_small(own, recv, ids):
    def body(ids_ref, o_ref, r_ref, out_ref):
        j = ids_ref[1]
        total = None
        for i in range(4):
            m = jnp.bitwise_xor(i, j)
            term = jnp.where(m == 0, o_ref[...], jnp.where(m == 2, r_ref[0], jnp.where(m == 1, r_ref[1], r_ref[2])))
            total = term if total is None else total + term
        out_ref[...] = total

    grid_spec = pltpu.PrefetchScalarGridSpec(
        num_scalar_prefetch=1, grid=(1,),
        in_specs=[pl.BlockSpec(own.shape, lambda i, ids_ref: (0, 0)), pl.BlockSpec(recv.shape, lambda i, ids_ref: (0, 0, 0))],
        out_specs=pl.BlockSpec(own.shape, lambda i, ids_ref: (0, 0)))
    return pl.pallas_call(body, name="chip_sum_small", grid_spec=grid_spec,
                          out_shape=jax.ShapeDtypeStruct(own.shape, F32), compiler_params=_cparams(1))(ids, own, recv)


def _adamw(name, w, m, v, mine, theirs, ids):
    rows, cols = w.shape
    half = rows // 2
    tr = _row_tile(half, ADAM_ROWS, unit=8)
    nb = half // tr
    c1 = 1.0 / (1.0 - ADAM_B1 ** ADAM_STEP)
    c2 = 1.0 / (1.0 - ADAM_B2 ** ADAM_STEP)

    def body(ids_ref, w_ref, m_ref, v_ref, mine_ref, theirs_ref, g_out, d_out, m_out, v_out):
        g = jnp.where(pl.program_id(0) == ids_ref[0], mine_ref[...], theirs_ref[...])
        m_new = ADAM_B1 * m_ref[...] + (1.0 - ADAM_B1) * g
        v_new = ADAM_B2 * v_ref[...] + (1.0 - ADAM_B2) * (g * g)
        d_out[...] = -ADAM_LR * ((m_new * c1) / (jnp.sqrt(v_new * c2) + ADAM_EPS) + ADAM_WD * w_ref[...])
        g_out[...] = g
        m_out[...] = m_new
        v_out[...] = v_new

    full = pl.BlockSpec((tr, cols), lambda h, i, ids_ref: (h * nb + i, 0))
    part = pl.BlockSpec((tr, cols), lambda h, i, ids_ref: (i, 0))
    grid_spec = pltpu.PrefetchScalarGridSpec(num_scalar_prefetch=1, grid=(2, nb),
                                             in_specs=[full, full, full, part, part], out_specs=[full] * 4)
    return pl.pallas_call(
        body, name=name, grid_spec=grid_spec,
        out_shape=[jax.ShapeDtypeStruct((rows, cols), F32)] * 4, compiler_params=_cparams(2),
    )(ids, w, m, v, mine, theirs)


def _small_shard(parts):
    return _flatten(parts, _BIG[-1][1])


def kernel(x, meta_tokens, norm_pre_mix, w_in, ssm_conv_w, ssm_conv_b, ssm_dt_bias, ssm_a_log, ssm_d_skip, ssm_norm, w_ssm_out, attn_sinks, w_attn_out, w_mix_out, norm_post_mix, norm_pre_ffn, w_ffn_up, ffn_conv_w, ffn_conv_b, w_ffn_down, norm_post_ffn, loss_target, m_meta_tokens, m_norm_pre_mix, m_w_in, m_ssm_conv_w, m_ssm_conv_b, m_ssm_dt_bias, m_ssm_a_log, m_ssm_d_skip, m_ssm_norm, m_w_ssm_out, m_attn_sinks, m_w_attn_out, m_w_mix_out, m_norm_post_mix, m_norm_pre_ffn, m_w_ffn_up, m_ffn_conv_w, m_ffn_conv_b, m_w_ffn_down, m_norm_post_ffn, v_meta_tokens, v_norm_pre_mix, v_w_in, v_ssm_conv_w, v_ssm_conv_b, v_ssm_dt_bias, v_ssm_a_log, v_ssm_d_skip, v_ssm_norm, v_w_ssm_out, v_attn_sinks, v_w_attn_out, v_w_mix_out, v_norm_post_mix, v_norm_pre_ffn, v_w_ffn_up, v_ffn_conv_w, v_ffn_conv_b, v_w_ffn_down, v_norm_post_ffn):
    args = dict(locals())
    squeeze = lambda a: a.reshape(a.shape[-2:])
    wts = {n: squeeze(args[n]) for n in WEIGHT_ORDER}
    mom = {n: squeeze(args["m_" + n]) for n in WEIGHT_ORDER}
    var = {n: squeeze(args["v_" + n]) for n in WEIGHT_ORDER}
    x_i, y_i, c_i = _mesh_pos()
    ids = jnp.stack([c_i, _chip_index(x_i, y_i)]).astype(jnp.int32)
    big_names = [n for n, _, _, _ in _BIG[:-1]]
    small_names = [n for n, _, _ in _SMALL_SHARDED]
    rep_names = [n for n, _ in _REPLICATED]

    w_small = _small_shard([wts[n] for n in small_names])
    gathered = _gather_weights([wts[n].astype(BF16) for n in big_names] + [w_small])
    w = {n: wts[n] for n in rep_names}
    w.update(zip(big_names[1:], gathered[1:-1]))
    w["w_cat"] = _to_cat(jnp.transpose(gathered[0], (1, 0, 2)).reshape(D_MODEL, N_IN))
    small_all = [_unflatten(gathered[-1][i], [shp for _, shp, _ in _SMALL_SHARDED]) for i in range(4)]
    for k, (n, _, axis) in enumerate(_SMALL_SHARDED):
        w[n] = jnp.concatenate([small_all[i][k] for i in range(4)], axis=axis)

    head = jnp.concatenate([jnp.zeros((PAD, D_MODEL), F32), w["meta_tokens"]], axis=0)
    loss_sum, dx, dhead, g = _local_step(x[0], head, loss_target[0], w)
    loss = lax.psum(loss_sum * (0.5 / D_MODEL), ("x", "y", "c"))
    g["meta_tokens"] = dhead[PAD:]
    g_in = _from_cat(g.pop("w_cat")).reshape(D_MODEL, 4, N_IN // 4)
    g["w_in"] = jnp.transpose(g_in, (1, 0, 2))

    grads = [g[n] for n in big_names]
    grads.append(jnp.stack([_small_shard([_shard_of(g[n], i, shp, ax) for n, shp, ax in _SMALL_SHARDED])
                            for i in range(4)]))
    g_rep = _flatten([g[n] for n in rep_names], SMALL_ROWS)
    *recv, recv_rep = _grad_pair_exchange(grads, g_rep)
    sums = [_pair_sum("pair_sum_" + n, gr, rv, ids, r, c, lay) for gr, rv, (n, r, c, lay) in zip(grads, recv, _BIG)]
    p_rep, = _rowwise("pair_sum_replicated", lambda r0, a, b: [a + b], SMALL_ROWS, SMALL_ROWS,
                      [(g_rep, LANES, 0), (recv_rep, LANES, 0)], [], [(LANES, F32)], [])
    *recv2, recv2_rep = _grad_chip_exchange([s[0] for s in sums], p_rep)
    halves = [_chip_sum("chip_sum_" + n, s[1], rv) for s, rv, (n, _, _, _) in zip(sums, recv2, _BIG)]
    g_rep_tot = _chip_sum_small(p_rep, recv2_rep, ids)
    theirs = _grad_half_share(halves)

    stacks = {"w": wts, "m": mom, "v": var}
    shard_in = {k: [d[n] for n in big_names] + [_small_shard([d[n] for n in small_names])] for k, d in stacks.items()}
    results = {}
    for a, (n, _, _, _) in enumerate(_BIG):
        res = _adamw("adamw_" + n, shard_in["w"][a], shard_in["m"][a], shard_in["v"][a], halves[a], theirs[a], ids)
        if n == "small":
            for kind in range(4):
                parts = _unflatten(res[kind], [shp for _, shp, _ in _SMALL_SHARDED])
                results.update({(kind, sn): parts[k] for k, sn in enumerate(small_names)})
        else:
            results.update({(kind, n): res[kind] for kind in range(4)})
    rep_in = {k: _flatten([d[n] for n in rep_names], SMALL_ROWS) for k, d in stacks.items()}
    ids_lo = ids * jnp.array([0, 1], jnp.int32)
    res = _adamw("adamw_replicated", rep_in["w"], rep_in["m"], rep_in["v"],
                 g_rep_tot[0:SMALL_ROWS // 2], g_rep_tot[SMALL_ROWS // 2:], ids_lo)
    for kind in range(4):
        parts = _unflatten(res[kind], [(1, width) for _, width in _REPLICATED])
        results.update({(kind, rn): parts[k] for k, rn in enumerate(rep_names)})
    outs = [results[kind, n].reshape(args[n].shape) for kind in range(4) for n in WEIGHT_ORDER]
    return (loss, dx[None], *outs)
```

```python
import math

import jax
import jax.numpy as jnp
from jax import lax
from jax.experimental import pallas as pl
from jax.experimental.pallas import tpu as pltpu

F32 = jnp.float32
BF16 = jnp.bfloat16

D_MODEL = 1024
N_META = 16
T = 128
PAD = T - N_META
D_INNER = 2048
SSM_HEADS = 32
HEAD_P = 64
SSM_GROUPS = 4
GROUP_W = D_INNER // SSM_GROUPS
D_STATE = 128
CONV_DIM = D_INNER + 2 * SSM_GROUPS * D_STATE
ATTN_HEADS = 16
KV_HEADS = 4
ATTN_W = 1024
KV_W = 256
FFN_DIM = 2816
N_IN = 8736
EPS = 1e-6
NEG = -1e30
SCALE = 0.125

P_Q, P_K, P_V, P_DT, P_Z, P_GATE, P_XBC = 0, 1024, 1280, 1536, 2048, 4096, 6144
QKV_W = 1536
P_W = 9216

ADAM_LR, ADAM_B1, ADAM_B2, ADAM_EPS, ADAM_WD, ADAM_STEP = 0.001, 0.9, 0.999, 1e-08, 0.01, 10

VMEM_BUDGET = 40 * 1024 * 1024
VMEM_LIMIT = 56 * 1024 * 1024
MESH = pl.DeviceIdType.MESH


def _cparams(n_axes, **kw):
    return pltpu.CompilerParams(dimension_semantics=("arbitrary",) * n_axes, vmem_limit_bytes=VMEM_LIMIT, **kw)


def _sigmoid(x):
    return 1.0 / (1.0 + jnp.exp(-x))


def _silu(x):
    return x * _sigmoid(x)


def _dsilu(x):
    s = _sigmoid(x)
    return s * (1.0 + x * (1.0 - s))


def _softplus(x):
    e = jnp.exp(-jnp.abs(x))
    small = e * (1.0 - e * (0.5 - e * (1.0 / 3.0)))
    return jnp.maximum(x, 0.0) + jnp.where(e < 0.01, small, jnp.log(1.0 + e))


def _rms(x, w):
    r = lax.rsqrt(jnp.mean(x * x, axis=-1, keepdims=True) + EPS)
    return x * r * w


def _rms_bwd(dy, x, w):
    r = lax.rsqrt(jnp.mean(x * x, axis=-1, keepdims=True) + EPS)
    xh = x * r
    g = dy * w
    dx = r * (g - xh * jnp.mean(g * xh, axis=-1, keepdims=True))
    dw = jnp.sum(dy * xh, axis=0, keepdims=True)
    return dx, dw


def _dot(a, b):
    return jnp.dot(a, b, preferred_element_type=F32)


def _dot_nt(a, b):
    return lax.dot_general(a, b, (((1,), (1,)), ((), ())), preferred_element_type=F32)


def _dot_tn(a, b):
    return lax.dot_general(a, b, (((0,), (0,)), ((), ())), preferred_element_type=F32)


def _split3(x):
    hi = x.astype(BF16)
    r = x - hi.astype(F32)
    mid = r.astype(BF16)
    lo = (r - mid.astype(F32)).astype(BF16)
    return hi, mid, lo


def _xdot(x, e):
    hi, mid, lo = _split3(x)
    return _dot(hi, e) + _dot(mid, e) + _dot(lo, e)


def _xdot_l(e, x):
    hi, mid, lo = _split3(x)
    return _dot(e, hi) + _dot(e, mid) + _dot(e, lo)


def _iota(shape, dim):
    return lax.broadcasted_iota(jnp.int32, shape, dim)


def _divisors(n, unit):
    return [t for t in range(unit, n + 1, unit) if n % t == 0]


def _matmul_tiles(m, n, k, a_bytes, b_bytes, o_bytes, m_unit):
    best = None
    for tm in _divisors(m, m_unit):
        for tn in _divisors(n, 128):
            for tk in _divisors(k, 128):
                acc = 0 if tk == k else tm * tn * 4
                vm = 2 * (tm * tk * a_bytes + tk * tn * b_bytes + tm * tn * o_bytes) + acc
                if vm > VMEM_BUDGET:
                    continue
                score = (tm * tn * tk, tk)
                if best is None or score > best[0]:
                    best = (score, (tm, tn, tk))
    return best[1]


def _matmul(name, a, b, mode, out_dtype):
    if mode == "nn":
        (m, k), n = a.shape, b.shape[1]
    elif mode == "nt":
        (m, k), n = a.shape, b.shape[0]
    else:
        (k, m), n = a.shape, b.shape[1]
    ab, bb, ob = a.dtype.itemsize, b.dtype.itemsize, jnp.dtype(out_dtype).itemsize
    tm, tn, tk = _matmul_tiles(m, n, k, ab, bb, ob, 128 if mode == "tn" else 16)
    nk = k // tk
    dot = {"nn": _dot, "nt": _dot_nt, "tn": _dot_tn}[mode]

    def body(a_ref, b_ref, o_ref, *scratch):
        prod = dot(a_ref[...].astype(BF16), b_ref[...].astype(BF16))
        if nk == 1:
            o_ref[...] = prod.astype(o_ref.dtype)
        else:
            acc_ref, = scratch
            kk = pl.program_id(2)

            @pl.when(kk == 0)
            def _():
                acc_ref[...] = prod

            @pl.when(kk > 0)
            def _():
                acc_ref[...] += prod

            @pl.when(kk == nk - 1)
            def _():
                o_ref[...] = acc_ref[...].astype(o_ref.dtype)

    a_spec = pl.BlockSpec((tk, tm), lambda i, j, kk: (kk, i)) if mode == "tn" else pl.BlockSpec((tm, tk), lambda i, j, kk: (i, kk))
    b_spec = pl.BlockSpec((tn, tk), lambda i, j, kk: (j, kk)) if mode == "nt" else pl.BlockSpec((tk, tn), lambda i, j, kk: (kk, j))
    return pl.pallas_call(
        body, name=name, grid=(m // tm, n // tn, nk),
        in_specs=[a_spec, b_spec], out_specs=pl.BlockSpec((tm, tn), lambda i, j, kk: (i, j)),
        out_shape=jax.ShapeDtypeStruct((m, n), out_dtype),
        scratch_shapes=[] if nk == 1 else [pltpu.VMEM((tm, tn), F32)],
        compiler_params=_cparams(3),
    )(a, b)


def _row_tile(n_rows, cap, unit=16):
    return max(t for t in _divisors(n_rows, unit) if t <= cap)


ROW_SUB = 16


def _rowwise(name, fn, n_rows, tm, row_ins, full_ins, row_outs, acc_outs):
    n_in = len(row_ins) + len(full_ins)
    n_ro = len(row_outs)
    into = [(k, o[3]) for k, o in enumerate(row_outs) if len(o) > 2 and o[2] == "into"]

    n_row_in = len(row_ins)
    sub = min(tm, ROW_SUB)

    def body(*refs):
        i = pl.program_id(0)
        outs = refs[n_in + len(into):]

        def group(s, sums):
            rows = pl.ds(pl.multiple_of(s * sub, sub), sub)
            vals = [r[rows, :] for r in refs[:n_row_in]] + [r[...] for r in refs[n_row_in:n_in]]
            res = fn(i * tm + s * sub, *vals)
            for o, r, v in zip(row_outs, outs[:n_ro], res[:n_ro]):
                if len(o) > 2 and o[2] == "first":
                    @pl.when(i == 0)
                    def _(r=r, v=v):
                        r[rows, :] = v.astype(r.dtype)
                else:
                    r[rows, :] = v.astype(r.dtype)
            return tuple(a + v for a, v in zip(sums, res[n_ro:]))

        sums = lax.fori_loop(0, tm // sub, group, tuple(jnp.zeros((1, w), F32) for w in acc_outs))

        @pl.when(i == 0)
        def _():
            for r, v in zip(outs[n_ro:], sums):
                r[...] = v

        @pl.when(i > 0)
        def _():
            for r, v in zip(outs[n_ro:], sums):
                r[...] += v

    def in_spec(entry):
        w, cb = entry[1], entry[2]
        if len(entry) > 3 and entry[3] == "prev":
            return pl.BlockSpec((tm, w), lambda i: (jnp.maximum(i - 1, 0), cb))
        if len(entry) > 3 and entry[3] == "first":
            return pl.BlockSpec((tm, w), lambda i: (0, cb))
        return pl.BlockSpec((tm, w), lambda i: (i, cb))

    def out_spec(o):
        if len(o) == 2:
            return pl.BlockSpec((tm, o[0]), lambda i: (i, 0)), jax.ShapeDtypeStruct((n_rows, o[0]), o[1])
        if o[2] == "new":
            return pl.BlockSpec((tm, o[0]), lambda i: (i, o[4])), jax.ShapeDtypeStruct((n_rows, o[3]), o[1])
        if o[2] == "into":
            return pl.BlockSpec((tm, o[0]), lambda i: (i, o[4])), jax.ShapeDtypeStruct(o[3].shape, o[3].dtype)
        if o[2] == "first":
            return pl.BlockSpec((tm, o[0]), lambda i: (0, 0)), jax.ShapeDtypeStruct((tm, o[0]), o[1])
        return pl.BlockSpec((tm, o[0]), lambda i: (jnp.maximum(i - 1, 0), 0)), jax.ShapeDtypeStruct((o[3], o[0]), o[1])

    in_specs = [in_spec(e) for e in row_ins]
    in_specs += [pl.BlockSpec(a.shape, lambda i: (0, 0)) for a in full_ins]
    in_specs += [pl.BlockSpec(memory_space=pl.ANY) for _ in into]
    specs_shapes = [out_spec(o) for o in row_outs]
    out_specs = [s for s, _ in specs_shapes] + [pl.BlockSpec((1, w), lambda i: (0, 0)) for w in acc_outs]
    out_shape = [s for _, s in specs_shapes] + [jax.ShapeDtypeStruct((1, w), F32) for w in acc_outs]
    return pl.pallas_call(
        body, name=name, grid=(n_rows // tm,), in_specs=in_specs, out_specs=out_specs, out_shape=out_shape,
        input_output_aliases={n_in + a: k for a, (k, _) in enumerate(into)},
        compiler_params=_cparams(1),
    )(*[e[0] for e in row_ins], *full_ins, *[arr for _, arr in into])


def _valid_rows(first_row, tm, lo):
    return (first_row + _iota((tm, 1), 0)) >= lo


CONV_ROWS = 128
CONV_SUB = 16
CONV_LANES = 256


def _conv_specs(tm, width, blk, n_rows, after):
    specs = [pl.BlockSpec((tm, width), lambda i: (i, blk)),
             pl.BlockSpec((8, width), lambda i: (jnp.maximum(i * (tm // 8) - 1, 0), blk))]
    if after:
        specs.append(pl.BlockSpec((16, width), lambda i: (jnp.minimum((i + 1) * (tm // 16), n_rows // 16 - 1), blk)))
    return specs


def _conv_window(win, w_ref, b_ref, taps, c0, cw, n):
    acc = b_ref[:, c0:c0 + cw] + w_ref[taps - 1:taps, c0:c0 + cw] * win[8:8 + n]
    for k in range(taps - 1):
        acc = acc + w_ref[k:k + 1, c0:c0 + cw] * win[8 - (taps - 1) + k:8 - (taps - 1) + k + n]
    return acc


def _ffn_act(name, u_raw, conv_w, conv_b, n_rows):
    tm, sub, cw = CONV_ROWS, CONV_SUB, CONV_LANES
    taps, width = conv_w.shape
    half = width // 2

    def body(cur_ref, prev_ref, w_ref, b_ref, f_ref, ext_ref):
        i = pl.program_id(0)
        ext_ref[0:8, :] = jnp.where(i > 0, prev_ref[...], 0.0)
        ext_ref[8:8 + tm, :] = cur_ref[...]
        for q in range(half // cw):
            a0, g0 = q * cw, half + q * cw

            def group(s, carry):
                r = pl.multiple_of(s * sub, sub)
                a = _conv_window(ext_ref[pl.ds(r, sub + 8), a0:a0 + cw], w_ref, b_ref, taps, a0, cw, sub)
                g = _conv_window(ext_ref[pl.ds(r, sub + 8), g0:g0 + cw], w_ref, b_ref, taps, g0, cw, sub)
                f = jnp.where(_valid_rows(i * tm + r, sub, PAD), _silu(a) * g, 0.0)
                f_ref[pl.ds(r, sub), a0:a0 + cw] = f.astype(f_ref.dtype)
                return carry

            lax.fori_loop(0, tm // sub, group, 0)

    return pl.pallas_call(
        body, name=name, grid=(n_rows // tm,),
        in_specs=_conv_specs(tm, width, 0, n_rows, False) + [pl.BlockSpec((taps, width), lambda i: (0, 0)),
                                                             pl.BlockSpec((1, width), lambda i: (0, 0))],
        out_specs=pl.BlockSpec((tm, half), lambda i: (i, 0)),
        out_shape=jax.ShapeDtypeStruct((n_rows, half), BF16),
        scratch_shapes=[pltpu.VMEM((tm + 8, width), F32)],
        compiler_params=_cparams(1),
    )(u_raw, u_raw, conv_w, conv_b)


def _conv_bwd(name, raw, raw_blk, dsrcs, chunk_src, conv_w, conv_b, n_rows, gated, into=None, into_blk=0):
    taps, width = conv_w.shape
    half = width // 2 if gated else width
    tm, sub, cw = CONV_ROWS, CONV_SUB, CONV_LANES
    te = tm + 16
    nd = len(dsrcs)
    n_parts = 2 if gated else 1

    def body(*refs):
        cur_ref, prev_ref, next_ref = refs[0:3]
        dcur, dnext = refs[3:3 + nd], refs[3 + nd:3 + 2 * nd]
        w_ref, b_ref = refs[3 + 2 * nd:5 + 2 * nd]
        out_ref, acc_ref, ext_ref, du_ref = refs[-4:]
        i = pl.program_id(0)
        ext_ref[0:8, :] = jnp.where(i > 0, prev_ref[...], 0.0)
        ext_ref[8:8 + tm, :] = cur_ref[...]
        ext_ref[8 + tm:24 + tm, :] = next_ref[...]

        for q, (src, off) in enumerate(chunk_src):
            cols = [q * cw, half + q * cw][:n_parts]

            def conv_grad(r, d):
                pre = [_conv_window(ext_ref[pl.ds(r, sub + 8), c0:c0 + cw], w_ref, b_ref, taps, c0, cw, sub) for c0 in cols]
                row = i * tm + r + _iota((sub, 1), 0)
                live = (row >= PAD) & (row < n_rows)
                if gated:
                    dus = [d * pre[1] * _dsilu(pre[0]), d * _silu(pre[0])]
                else:
                    dus = [d * _dsilu(pre[0])]
                for part, du in enumerate(dus):
                    du_ref[part, pl.ds(r, sub), :] = jnp.where(live, du, 0.0)

            def tile_rows(s, carry):
                r = pl.multiple_of(s * sub, sub)
                conv_grad(r, dcur[src][pl.ds(r, sub), off:off + cw].astype(F32))
                return carry

            lax.fori_loop(0, tm // sub, tile_rows, 0)
            conv_grad(tm, dnext[src][:, off:off + cw].astype(F32))

            for part, c0 in enumerate(cols):
                def back(s, sums):
                    r = pl.multiple_of(s * sub, sub)
                    win = du_ref[part, pl.ds(r, sub + 8), :]
                    raw_rows = ext_ref[pl.ds(8 + r, sub), c0:c0 + cw]
                    draw = jnp.zeros((sub, cw), F32)
                    new = []
                    for k in range(taps):
                        shifted = win[taps - 1 - k:taps - 1 - k + sub]
                        draw = draw + w_ref[k:k + 1, c0:c0 + cw] * shifted
                        new.append(sums[k] + jnp.sum(shifted * raw_rows, axis=0, keepdims=True))
                    new.append(sums[taps] + jnp.sum(win[0:sub], axis=0, keepdims=True))
                    out_ref[pl.ds(r, sub), c0:c0 + cw] = jnp.where(_valid_rows(i * tm + r, sub, PAD), draw, 0.0).astype(out_ref.dtype)
                    return tuple(new)

                sums = lax.fori_loop(0, tm // sub, back, tuple(jnp.zeros((1, cw), F32) for _ in range(taps + 1)))
                for k in range(taps + 1):
                    acc_ref[k:k + 1, c0:c0 + cw] = jnp.where(i == 0, sums[k], acc_ref[k:k + 1, c0:c0 + cw] + sums[k])

    in_specs = _conv_specs(tm, width, raw_blk, n_rows, True)
    in_specs += [pl.BlockSpec((tm, d.shape[1]), lambda i: (i, 0)) for d in dsrcs]
    in_specs += [pl.BlockSpec((16, d.shape[1]), lambda i: (jnp.minimum((i + 1) * (tm // 16), n_rows // 16 - 1), 0)) for d in dsrcs]
    in_specs += [pl.BlockSpec((taps, width), lambda i: (0, 0)), pl.BlockSpec((1, width), lambda i: (0, 0))]
    operands = [raw, raw, raw] + list(dsrcs) + list(dsrcs) + [conv_w, conv_b]
    aliases = {}
    if into is None:
        out0 = jax.ShapeDtypeStruct((n_rows, width), BF16)
    else:
        in_specs.append(pl.BlockSpec(memory_space=pl.ANY))
        operands.append(into)
        aliases = {len(operands) - 1: 0}
        out0 = jax.ShapeDtypeStruct(into.shape, into.dtype)
    return pl.pallas_call(
        body, name=name, grid=(n_rows // tm,), in_specs=in_specs,
        out_specs=[pl.BlockSpec((tm, width), lambda i: (i, into_blk)), pl.BlockSpec((8, width), lambda i: (0, 0))],
        out_shape=[out0, jax.ShapeDtypeStruct((8, width), F32)], input_output_aliases=aliases,
        scratch_shapes=[pltpu.VMEM((tm + 24, width), F32), pltpu.VMEM((n_parts, te + 8, cw), F32)],
        compiler_params=_cparams(1),
    )(*operands)


def _ssd_specs(n_chunks, rev):
    cidx = (lambda c: n_chunks - 1 - c) if rev else (lambda c: c)
    xg0, bg0, cg0 = P_XBC // GROUP_W, (P_XBC + D_INNER) // D_STATE, (P_XBC + D_INNER + SSM_GROUPS * D_STATE) // D_STATE

    def cur(width, blk0):
        return pl.BlockSpec((T, width), lambda g, c: (cidx(c), blk0 + g))

    def prev(width, blk0):
        return pl.BlockSpec((8, width), lambda g, c: (jnp.maximum(cidx(c) * (T // 8) - 1, 0), blk0 + g))

    specs = [cur(GROUP_W, xg0), prev(GROUP_W, xg0), cur(D_STATE, bg0), prev(D_STATE, bg0),
             cur(D_STATE, cg0), prev(D_STATE, cg0),
             pl.BlockSpec((T, 128), lambda g, c: (cidx(c), P_DT // 128))]
    wx, wb, wc = 0, D_INNER // D_STATE, (D_INNER + SSM_GROUPS * D_STATE) // D_STATE
    specs += [pl.BlockSpec((4, GROUP_W), lambda g, c: (0, g)),
              pl.BlockSpec((4, D_STATE), lambda g, c: (0, wb + g)),
              pl.BlockSpec((4, D_STATE), lambda g, c: (0, wc + g)),
              pl.BlockSpec((1, GROUP_W), lambda g, c: (0, g)),
              pl.BlockSpec((1, D_STATE), lambda g, c: (0, wb + g)),
              pl.BlockSpec((1, D_STATE), lambda g, c: (0, wc + g))]
    specs += [pl.BlockSpec((1, 128), lambda g, c: (0, 0))] * 3
    return specs, cidx


def _ssd_chunk_forward(refs, ext_ref, g, c):
    (xc_ref, xp_ref, bc_ref, bp_ref, cc_ref, cp_ref, dt_ref, wx_ref, wb_ref, wc_ref,
     bx_ref, bb_ref, bcb_ref, dtb_ref, alog_ref, dsk_ref) = refs

    def conv_pre(cur_ref, prev_ref, w_ref, b_ref, width):
        ext_ref[0:8, 0:width] = jnp.where(c > 0, prev_ref[...], 0.0)
        ext_ref[8:8 + T, 0:width] = cur_ref[...]
        w = w_ref[...]
        acc = b_ref[...] + w[3:4] * cur_ref[...]
        for k in range(3):
            acc = acc + w[k:k + 1] * ext_ref[pl.ds(5 + k, T), 0:width]
        return acc

    valid = _valid_rows(c * T, T, PAD)
    v = {}
    v["valid"] = valid
    v["x_pre"] = conv_pre(xc_ref, xp_ref, wx_ref, bx_ref, GROUP_W)
    v["b_pre"] = conv_pre(bc_ref, bp_ref, wb_ref, bb_ref, D_STATE)
    v["c_pre"] = conv_pre(cc_ref, cp_ref, wc_ref, bcb_ref, D_STATE)
    xs = _silu(v["x_pre"])
    bm = jnp.where(valid, _silu(v["b_pre"]), 0.0)
    cm = jnp.where(valid, _silu(v["c_pre"]), 0.0)
    dtr = dt_ref[...] + dtb_ref[...]
    dt = jnp.where(valid, _softplus(dtr), 0.0)
    a_neg = -jnp.exp(alog_ref[...])
    a = dt * a_neg
    tril = _iota((T, T), 0) >= _iota((T, T), 1)
    cs = _xdot_l(tril.astype(BF16), a)
    hh, ll = _iota((128, GROUP_W), 0), _iota((128, GROUP_W), 1)
    expand = (hh == 8 * g + jnp.right_shift(ll, 6)).astype(BF16)
    sh, sj = _iota((128, 128), 0), _iota((128, 128), 1)
    select = ((sh == 8 * g + sj) & (sj < 8)).astype(BF16)
    hh_t, ll_t = _iota((GROUP_W, 128), 1), _iota((GROUP_W, 128), 0)
    v["expand_t"] = (hh_t == 8 * g + jnp.right_shift(ll_t, 6)).astype(BF16)
    v["select_t"] = ((sj == 8 * g + sh) & (sh < 8)).astype(BF16)
    cs_e = _xdot(cs, expand)
    dt_e = _xdot(dt, expand)
    cs_loc = _xdot(cs, select)
    cs_loc_t = cs_loc.T
    cs_last_e = cs_e[T - 1:T, :]
    v.update(xs=xs, bm=bm, cm=cm, dtr=dtr, dt=dt, a_neg=a_neg, tril=tril, expand=expand, select=select,
             cs_e=cs_e, dt_e=dt_e, cs_loc=cs_loc, cs_loc_t=cs_loc_t, cs_last_e=cs_last_e)
    v["xdt"] = xs * dt_e
    v["decay_e"] = jnp.exp(cs_last_e - cs_e)
    v["ecs_e"] = jnp.exp(cs_e)
    v["elast_e"] = jnp.exp(cs_last_e)
    v["d_e"] = _xdot(dsk_ref[...], expand)
    v["gmat"] = _dot_nt(cm.astype(BF16), bm.astype(BF16))
    return v


def _ssd_decay_pair(v, jp):
    out = []
    for j in (2 * jp, 2 * jp + 1):
        diff = v["cs_loc"][:, j:j + 1] - v["cs_loc_t"][j:j + 1, :]
        out.append(jnp.where(v["tril"], jnp.exp(jnp.where(v["tril"], diff, 0.0)), 0.0))
    return out


def _block_diag_pair(xp):
    lane = _iota(xp.shape, 1)
    return jnp.concatenate([jnp.where(lane < HEAD_P, xp, 0.0), jnp.where(lane >= HEAD_P, xp, 0.0)], axis=0)


def _ssd_fwd(p, conv_w, conv_b, dt_bias, a_log, d_skip, n_chunks):
    n_rows = n_chunks * T
    in_specs, _ = _ssd_specs(n_chunks, rev=False)

    def body(*refs):
        y_ref, hin_ref, st_ref, ext_ref = refs[16:]
        g, c = pl.program_id(0), pl.program_id(1)

        @pl.when(c == 0)
        def _():
            st_ref[...] = jnp.zeros_like(st_ref)

        v = _ssd_chunk_forward(refs[:16], ext_ref, g, c)
        state = st_ref[...]
        hin_ref[...] = state
        ys = []
        for jp in range(4):
            l0, l1 = _ssd_decay_pair(v, jp)
            lhs = jnp.concatenate([v["gmat"] * l0, v["gmat"] * l1], axis=1).astype(BF16)
            rhs = _block_diag_pair(v["xdt"][:, 128 * jp:128 * jp + 128]).astype(BF16)
            ys.append(_dot(lhs, rhs))
        y = jnp.concatenate(ys, axis=1)
        y = y + _dot(v["cm"].astype(BF16), state.astype(BF16)) * v["ecs_e"] + v["xs"] * v["d_e"]
        y_ref[...] = y
        s_new = _dot_tn(v["bm"].astype(BF16), (v["xdt"] * v["decay_e"]).astype(BF16))
        st_ref[...] = state * v["elast_e"] + s_new

    return pl.pallas_call(
        body, name="ssd_fwd", grid=(SSM_GROUPS, n_chunks), in_specs=in_specs,
        out_specs=[pl.BlockSpec((T, GROUP_W), lambda g, c: (c, g)),
                   pl.BlockSpec((None, None, D_STATE, GROUP_W), lambda g, c: (g, c, 0, 0))],
        out_shape=[jax.ShapeDtypeStruct((n_rows, D_INNER), F32),
                   jax.ShapeDtypeStruct((SSM_GROUPS, n_chunks, D_STATE, GROUP_W), F32)],
        scratch_shapes=[pltpu.VMEM((D_STATE, GROUP_W), F32), pltpu.VMEM((T + 8, GROUP_W), F32)],
        compiler_params=_cparams(2),
    )(p, p, p, p, p, p, p, conv_w, conv_w, conv_w, conv_b, conv_b, conv_b, dt_bias, a_log, d_skip)


def _ssd_bwd(p, conv_w, conv_b, dt_bias, a_log, d_skip, hin, dy, n_chunks):
    n_rows = n_chunks * T
    in_specs, cidx = _ssd_specs(n_chunks, rev=True)
    in_specs = in_specs + [pl.BlockSpec((None, None, D_STATE, GROUP_W), lambda g, c: (g, cidx(c), 0, 0)),
                           pl.BlockSpec((T, GROUP_W), lambda g, c: (cidx(c), g))]

    def body(*refs):
        hin_ref, dy_ref = refs[16:18]
        dx_ref, db_ref, dc_ref, ddt_ref, dpar_ref, dst_ref, ext_ref = refs[18:]
        g, step = pl.program_id(0), pl.program_id(1)
        c = n_chunks - 1 - step

        @pl.when(step == 0)
        def _():
            dst_ref[...] = jnp.zeros_like(dst_ref)

        v = _ssd_chunk_forward(refs[:16], ext_ref, g, c)
        hin_f = hin_ref[...]
        hin_b = hin_f.astype(BF16)
        dyv = dy_ref[...]
        dst = dst_ref[...]
        dst_b = dst.astype(BF16)
        xs, bm, cm, xdt = v["xs"], v["bm"], v["cm"], v["xdt"]
        bm_b, cm_b = bm.astype(BF16), cm.astype(BF16)

        dd_e = jnp.sum(dyv * xs, axis=0, keepdims=True)
        dxs = dyv * v["d_e"]
        ch = _dot(cm_b, hin_b)
        dch = (dyv * v["ecs_e"]).astype(BF16)
        dcm = _dot_nt(dch, hin_b)
        dhin = _dot_tn(cm_b, dch) + dst * v["elast_e"]
        dcs_e = dyv * ch * v["ecs_e"]
        dxd = _dot(bm_b, dst_b)
        dbm = _dot_nt((xdt * v["decay_e"]).astype(BF16), dst_b)
        dxdt_state = dxd * v["decay_e"]
        q = dxdt_state * xdt
        dcs_e = dcs_e - q
        dlast_e = jnp.sum(q, axis=0, keepdims=True) + jnp.sum(dst * hin_f, axis=0, keepdims=True) * v["elast_e"]
        dg = jnp.zeros((T, T), F32)
        rs_cols = jnp.zeros((T, 128), F32)
        cs_rows = jnp.zeros((128, T), F32)
        lane_i, sub_i = _iota((T, 128), 1), _iota((128, T), 0)
        dxdt_parts = []
        for jp in range(4):
            l0, l1 = _ssd_decay_pair(v, jp)
            m0, m1 = v["gmat"] * l0, v["gmat"] * l1
            xbd = _block_diag_pair(xdt[:, 128 * jp:128 * jp + 128]).astype(BF16)
            dyp = dyv[:, 128 * jp:128 * jp + 128]
            dm = _dot_nt(dyp.astype(BF16), xbd)
            dm0, dm1 = dm[:, 0:T], dm[:, T:2 * T]
            dg = dg + dm0 * l0 + dm1 * l1
            for j, qq in ((2 * jp, dm0 * m0), (2 * jp + 1, dm1 * m1)):
                rs_cols = jnp.where(lane_i == j, jnp.sum(qq, axis=1, keepdims=True), rs_cols)
                cs_rows = jnp.where(sub_i == j, jnp.sum(qq, axis=0, keepdims=True), cs_rows)
            mv = jnp.concatenate([m0, m1], axis=0).astype(BF16)
            dxdt_parts.append(_dot_tn(mv, _block_diag_pair(dyp).astype(BF16)))
        dxdt = jnp.concatenate(dxdt_parts, axis=1) + dxdt_state
        dg_b = dg.astype(BF16)
        dcm = dcm + _dot(dg_b, bm_b)
        dbm = dbm + _dot_tn(dg_b, cm_b)
        expand_t = v["expand_t"]
        dcs_loc = rs_cols - cs_rows.T
        last_row = _iota((T, 1), 0) == T - 1
        dcs_full_e = dcs_e + jnp.where(last_row, dlast_e, 0.0)
        dcs = _xdot(dcs_full_e, expand_t) + _xdot(dcs_loc, v["select_t"])
        triu = (_iota((T, T), 0) <= _iota((T, T), 1)).astype(BF16)
        da = _xdot_l(triu, dcs)
        ddt = da * v["a_neg"] + _xdot(dxdt * xs, expand_t)
        dxs = dxs + dxdt * v["dt_e"]
        ddtr = jnp.where(v["valid"], ddt * _sigmoid(v["dtr"]), 0.0)
        dx_ref[...] = dxs
        db_ref[...] = jnp.where(v["valid"], dbm, 0.0)
        dc_ref[...] = jnp.where(v["valid"], dcm, 0.0)
        ddt_ref[...] = ddtr
        dpar = jnp.concatenate([
            jnp.sum(ddtr, axis=0, keepdims=True),
            jnp.sum(da * v["dt"], axis=0, keepdims=True) * v["a_neg"],
            _xdot(dd_e, expand_t),
            jnp.zeros((5, 128), F32)], axis=0)

        @pl.when(step == 0)
        def _():
            dpar_ref[...] = dpar

        @pl.when(step > 0)
        def _():
            dpar_ref[...] += dpar

        dst_ref[...] = dhin

    return pl.pallas_call(
        body, name="ssd_bwd", grid=(SSM_GROUPS, n_chunks), in_specs=in_specs,
        out_specs=[pl.BlockSpec((T, GROUP_W), lambda g, c: (cidx(c), g)),
                   pl.BlockSpec((T, D_STATE), lambda g, c: (cidx(c), g)),
                   pl.BlockSpec((T, D_STATE), lambda g, c: (cidx(c), g)),
                   pl.BlockSpec((T, 128), lambda g, c: (cidx(c), g)),
                   pl.BlockSpec((None, 8, 128), lambda g, c: (g, 0, 0))],
        out_shape=[jax.ShapeDtypeStruct((n_rows, D_INNER), F32),
                   jax.ShapeDtypeStruct((n_rows, SSM_GROUPS * D_STATE), F32),
                   jax.ShapeDtypeStruct((n_rows, SSM_GROUPS * D_STATE), F32),
                   jax.ShapeDtypeStruct((n_rows, SSM_GROUPS * 128), F32),
                   jax.ShapeDtypeStruct((SSM_GROUPS, 8, 128), F32)],
        scratch_shapes=[pltpu.VMEM((D_STATE, GROUP_W), F32), pltpu.VMEM((T + 8, GROUP_W), F32)],
        compiler_params=_cparams(2),
    )(p, p, p, p, p, p, p, conv_w, conv_w, conv_w, conv_b, conv_b, conv_b, dt_bias, a_log, d_skip, hin, dy)


def _alibi_slope(h):
    return 2.0 ** (-8.0 * (h + 1) / ATTN_HEADS)


def _dup_half(x256, kvh):
    xb = x256[:, 128 * (kvh // 2):128 * (kvh // 2) + 128]
    rolled = pltpu.roll(xb, 64, 1)
    lane = _iota(xb.shape, 1)
    if kvh % 2 == 0:
        return jnp.where(lane < 64, xb, rolled)
    return jnp.where(lane < 64, rolled, xb)


def _attn_masks(c):
    qi = _iota((T, 3 * T), 0)
    jj = _iota((T, 3 * T), 1)
    blk = jnp.right_shift(jj, 7)
    j = jnp.bitwise_and(jj, T - 1)
    q_pos = c * T + qi - PAD
    k_pos = (c - 2 + blk) * T + j - PAD
    dist = q_pos - k_pos
    band = (blk > 0) & (dist >= 0) & (dist < T) & (k_pos >= N_META)
    meta = (blk == 0) & (j >= PAD) & (j - PAD <= q_pos)
    distf = jnp.where(blk > 0, dist, 0).astype(F32)
    return band | meta, distf


def _attn_scores(qp, k3, allowed, distf, h0):
    lane = _iota(qp.shape, 1)
    s = []
    for half, h in ((0, h0), (1, h0 + 1)):
        qh = jnp.where((lane < 64) if half == 0 else (lane >= 64), qp, 0.0).astype(BF16)
        sc = _dot_nt(qh, k3) - _alibi_slope(h) * distf
        s.append((qh, jnp.where(allowed, sc, NEG)))
    return s


def _attn_fwd(p, sinks, n_chunks):
    n_rows = n_chunks * T
    kb, vb = P_K // KV_W, P_V // KV_W

    def body(q_ref, kc_ref, kp_ref, km_ref, vc_ref, vp_ref, vm_ref, sink_ref, o_ref, lse_ref):
        c = pl.program_id(0)
        allowed, distf = _attn_masks(c)
        q = q_ref[...] * SCALE
        sinks_v = sink_ref[...]
        lane = _iota((T, 128), 1)
        lse_all = jnp.zeros((T, 128), F32)
        outs = []
        for kvh in range(KV_HEADS):
            k3 = jnp.concatenate([_dup_half(r[...], kvh) for r in (km_ref, kp_ref, kc_ref)], axis=0).astype(BF16)
            v3 = jnp.concatenate([_dup_half(r[...], kvh) for r in (vm_ref, vp_ref, vc_ref)], axis=0)
            v3bd = _block_diag_rows(v3).astype(BF16)
            for pr in range(2):
                h0 = 4 * kvh + 2 * pr
                blk = 2 * kvh + pr
                qp = q[:, 128 * blk:128 * blk + 128]
                probs = []
                for (_, sc), h in zip(_attn_scores(qp, k3, allowed, distf, h0), (h0, h0 + 1)):
                    sink = sinks_v[:, h:h + 1]
                    m = jnp.maximum(jnp.max(sc, axis=1, keepdims=True), sink)
                    e = jnp.exp(sc - m)
                    den = jnp.sum(e, axis=1, keepdims=True) + jnp.exp(sink - m)
                    probs.append(e / den)
                    lse_all = jnp.where(lane == h, m + jnp.log(den), lse_all)
                outs.append(_dot(jnp.concatenate(probs, axis=1).astype(BF16), v3bd))
        o_ref[...] = jnp.concatenate(outs, axis=1).astype(o_ref.dtype)
        lse_ref[...] = lse_all

    blk = lambda width, col: pl.BlockSpec((T, width), lambda c: (c, col))
    prev = lambda width, col: pl.BlockSpec((T, width), lambda c: (jnp.maximum(c - 1, 0), col))
    first = lambda width, col: pl.BlockSpec((T, width), lambda c: (0, col))
    return pl.pallas_call(
        body, name="attn_fwd", grid=(n_chunks,),
        in_specs=[blk(ATTN_W, P_Q // ATTN_W), blk(KV_W, kb), prev(KV_W, kb), first(KV_W, kb),
                  blk(KV_W, vb), prev(KV_W, vb), first(KV_W, vb), pl.BlockSpec((1, 128), lambda c: (0, 0))],
        out_specs=[pl.BlockSpec((T, ATTN_W), lambda c: (c, 0)), pl.BlockSpec((T, 128), lambda c: (c, 0))],
        out_shape=[jax.ShapeDtypeStruct((n_rows, ATTN_W), BF16), jax.ShapeDtypeStruct((n_rows, 128), F32)],
        compiler_params=_cparams(1),
    )(p, p, p, p, p, p, p, sinks)


def _block_diag_rows(x3):
    lane = _iota(x3.shape, 1)
    return jnp.concatenate([jnp.where(lane < 64, x3, 0.0), jnp.where(lane >= 64, x3, 0.0)], axis=0)


def _fold_halves(x):
    return x + pltpu.roll(x, 64, 1)


def _attn_bwd(p, sinks, ao, lse, dao, dp, n_chunks):
    kb, vb = P_K // KV_W, P_V // KV_W
    rc = lambda s: n_chunks - 1 - s

    def body(q_ref, kc_ref, kp_ref, km_ref, vc_ref, vp_ref, vm_ref, sink_ref, o_ref, lse_ref, do_ref, dp_in_ref,
             dqkv_ref, dsink_ref, kcar_ref, vcar_ref, kmeta_ref, vmeta_ref):
        step = pl.program_id(0)
        c = n_chunks - 1 - step

        @pl.when(step == 0)
        def _():
            for r in (kcar_ref, vcar_ref, kmeta_ref, vmeta_ref):
                r[...] = jnp.zeros_like(r)

        allowed, distf = _attn_masks(c)
        q = q_ref[...] * SCALE
        sinks_v = sink_ref[...]
        lse_v = lse_ref[...]
        ov = o_ref[...].astype(F32)
        dov = do_ref[...].astype(F32)
        lane = _iota((T, 128), 1)
        lane256 = _iota((3 * T, KV_W), 1)
        dsink = jnp.zeros((1, 128), F32)
        dk3_all = jnp.zeros((3 * T, KV_W), F32)
        dv3_all = jnp.zeros((3 * T, KV_W), F32)
        dqs = []
        for kvh in range(KV_HEADS):
            k3 = jnp.concatenate([_dup_half(r[...], kvh) for r in (km_ref, kp_ref, kc_ref)], axis=0).astype(BF16)
            v3 = jnp.concatenate([_dup_half(r[...], kvh) for r in (vm_ref, vp_ref, vc_ref)], axis=0).astype(BF16)
            dk3 = jnp.zeros((3 * T, 128), F32)
            dv3 = jnp.zeros((3 * T, 128), F32)
            for pr in range(2):
                h0 = 4 * kvh + 2 * pr
                blk = 2 * kvh + pr
                qp = q[:, 128 * blk:128 * blk + 128]
                dop = dov[:, 128 * blk:128 * blk + 128]
                prod = dop * ov[:, 128 * blk:128 * blk + 128]
                dq_pair = jnp.zeros((T, 128), F32)
                for half, ((qh, sc), h) in enumerate(zip(_attn_scores(qp, k3, allowed, distf, h0), (h0, h0 + 1))):
                    mine = (lane < 64) if half == 0 else (lane >= 64)
                    lse_h = lse_v[:, h:h + 1]
                    pm = jnp.exp(sc - lse_h)
                    doh = jnp.where(mine, dop, 0.0).astype(BF16)
                    delta = jnp.sum(jnp.where(mine, prod, 0.0), axis=1, keepdims=True)
                    dp = _dot_nt(doh, v3)
                    ds = (pm * (dp - delta)).astype(BF16)
                    p_sink = jnp.exp(sinks_v[:, h:h + 1] - lse_h)
                    dsink = jnp.where(_iota((1, 128), 1) == h, jnp.sum(-p_sink * delta, axis=0, keepdims=True), dsink)
                    dq_pair = jnp.where(mine, _dot(ds, k3), dq_pair)
                    dk3 = dk3 + _dot_tn(ds, qh)
                    dv3 = dv3 + _dot_tn(pm.astype(BF16), doh)
                dqs.append(dq_pair * SCALE)
            in_place = (lane256 >= 64 * kvh) & (lane256 < 64 * kvh + 64)
            wide = lambda x: jnp.concatenate([x, x], axis=1)
            dk3_all = jnp.where(in_place, wide(_fold_halves(dk3)), dk3_all)
            dv3_all = jnp.where(in_place, wide(_fold_halves(dv3)), dv3_all)
        dsink_all = dsink

        @pl.when(step == 0)
        def _():
            dsink_ref[...] = dsink_all

        @pl.when(step > 0)
        def _():
            dsink_ref[...] += dsink_all

        kmeta = kmeta_ref[...] + dk3_all[0:T]
        vmeta = vmeta_ref[...] + dv3_all[0:T]
        kmeta_ref[...] = kmeta
        vmeta_ref[...] = vmeta
        is_first = c == 0
        dk = jnp.where(is_first, kmeta, dk3_all[2 * T:3 * T] + kcar_ref[...])
        dv = jnp.where(is_first, vmeta, dv3_all[2 * T:3 * T] + vcar_ref[...])
        dqkv_ref[...] = jnp.concatenate(dqs + [dk, dv], axis=1).astype(dqkv_ref.dtype)
        kcar_ref[...] = dk3_all[T:2 * T]
        vcar_ref[...] = dv3_all[T:2 * T]

    blk = lambda width, col: pl.BlockSpec((T, width), lambda s: (rc(s), col))
    prev = lambda width, col: pl.BlockSpec((T, width), lambda s: (jnp.maximum(rc(s) - 1, 0), col))
    first = lambda width, col: pl.BlockSpec((T, width), lambda s: (0, col))
    return pl.pallas_call(
        body, name="attn_bwd", grid=(n_chunks,),
        in_specs=[blk(ATTN_W, P_Q // ATTN_W), blk(KV_W, kb), prev(KV_W, kb), first(KV_W, kb),
                  blk(KV_W, vb), prev(KV_W, vb), first(KV_W, vb), pl.BlockSpec((1, 128), lambda s: (0, 0)),
                  blk(ATTN_W, 0), blk(128, 0), blk(ATTN_W, 0), pl.BlockSpec(memory_space=pl.ANY)],
        out_specs=[blk(QKV_W, P_Q // QKV_W), pl.BlockSpec((1, 128), lambda s: (0, 0))],
        out_shape=[jax.ShapeDtypeStruct(dp.shape, dp.dtype), jax.ShapeDtypeStruct((1, 128), F32)],
        input_output_aliases={11: 0},
        scratch_shapes=[pltpu.VMEM((T, KV_W), F32)] * 4,
        compiler_params=_cparams(1),
    )(p, p, p, p, p, p, p, sinks, ao, lse, dao, dp)


def _pad_lanes(v, width=128):
    return jnp.pad(v, ((0, 0), (0, width - v.shape[1])))


def _local_step(x, head, tgt, w):
    n_tok = x.shape[0]
    n_rows = n_tok + T
    n_chunks = n_rows // T
    tm = _row_tile(n_rows, 384)
    dt_bias, a_log, d_skip = (_pad_lanes(w[k]) for k in ("ssm_dt_bias", "ssm_a_log", "ssm_d_skip"))
    sinks = _pad_lanes(w["attn_sinks"])
    x_in = [(x, D_MODEL, 0, "prev"), (head, D_MODEL, 0, "first")]

    def h0_tile(r0, xt, hd):
        return jnp.where(r0 < T, hd, xt)

    n1, = _rowwise("norm_pre_mix", lambda r0, xt, hd, wn: [_rms(h0_tile(r0, xt, hd), wn)], n_rows, T,
                   x_in, [w["norm_pre_mix"]], [(D_MODEL, BF16)], [])
    p = _matmul("in_proj", n1, w["w_cat"], "nn", F32)
    y_ssd, hin = _ssd_fwd(p, w["ssm_conv_w"], w["ssm_conv_b"], dt_bias, a_log, d_skip, n_chunks)
    ao, lse = _attn_fwd(p, sinks, n_chunks)

    def gate_norm(r0, y, z, wn):
        return [_rms(y * _silu(z), wn)]

    yn, = _rowwise("ssm_gate_norm", gate_norm, n_rows, tm, [(y_ssd, D_INNER, 0), (p, D_INNER, P_Z // D_INNER)],
                   [w["ssm_norm"]], [(D_INNER, BF16)], [])
    y_ssm = _matmul("ssm_out", yn, w["w_ssm_out"], "nn", F32)
    y_attn = _matmul("attn_out", ao, w["w_attn_out"], "nn", F32)

    def mix_gate(r0, ys, ya, gs, ga):
        return [_sigmoid(gs) * ys + _sigmoid(ga) * ya]

    gate_ins = [(p, D_MODEL, P_GATE // D_MODEL), (p, D_MODEL, P_GATE // D_MODEL + 1)]
    mixed, = _rowwise("mix_gate", mix_gate, n_rows, tm, [(y_ssm, D_MODEL, 0), (y_attn, D_MODEL, 0)] + gate_ins,
                      [], [(D_MODEL, BF16)], [])
    mix = _matmul("mix_out", mixed, w["w_mix_out"], "nn", F32)

    def post_mix(r0, mx, xt, hd, w_post, w_pre):
        h1 = jnp.where(_valid_rows(r0, mx.shape[0], PAD), h0_tile(r0, xt, hd) + _rms(mx, w_post), 0.0)
        return [h1, _rms(h1, w_pre)]

    h1, n2 = _rowwise("post_mix", post_mix, n_rows, T, [(mix, D_MODEL, 0)] + x_in,
                      [w["norm_post_mix"], w["norm_pre_ffn"]], [(D_MODEL, F32), (D_MODEL, BF16)], [])
    u_raw = _matmul("ffn_up", n2, w["w_ffn_up"], "nn", F32)
    f = _ffn_act("ffn_act", u_raw, w["ffn_conv_w"], w["ffn_conv_b"], n_rows)
    ffn = _matmul("ffn_down", f, w["w_ffn_down"], "nn", F32)

    def final(r0, fo, h, t, w_post):
        real = r0 >= T
        err = jnp.where(real, h + _rms(fo, w_post) - t, 0.0)
        dy = err * (1.0 / D_MODEL)
        dffn, dw = _rms_bwd(dy, fo, w_post)
        return [dffn, dy, jnp.sum(err * err, axis=0, keepdims=True), dw]

    dffn, dh2, loss_cols, g_norm_post_ffn = _rowwise(
        "loss_head", final, n_rows, T, [(ffn, D_MODEL, 0), (h1, D_MODEL, 0), (tgt, D_MODEL, 0, "prev")],
        [w["norm_post_ffn"]], [(D_MODEL, BF16), (D_MODEL, F32)], [D_MODEL, D_MODEL])

    g = {"norm_post_ffn": g_norm_post_ffn}
    g["w_ffn_down"] = _matmul("ffn_down_dw", f, dffn, "tn", F32)
    df = _matmul("ffn_down_dx", dffn, w["w_ffn_down"], "nt", F32)
    du_raw, dconv = _conv_bwd("ffn_act_bwd", u_raw, 0, [df], [(0, c0) for c0 in range(0, FFN_DIM, CONV_LANES)],
                              w["ffn_conv_w"], w["ffn_conv_b"], n_rows, True)
    g["ffn_conv_w"], g["ffn_conv_b"] = dconv[0:3], dconv[3:4]
    g["w_ffn_up"] = _matmul("ffn_up_dw", n2, du_raw, "tn", F32)
    dn2 = _matmul("ffn_up_dx", du_raw, w["w_ffn_up"], "nt", F32)

    def post_mix_bwd(r0, dn, d2, h, mx, w_pre, w_post):
        dx, dw_pre = _rms_bwd(dn, h, w_pre)
        dh1 = jnp.where(_valid_rows(r0, dn.shape[0], PAD), dx + d2, 0.0)
        dmix, dw_post = _rms_bwd(dh1, mx, w_post)
        return [dh1, dmix, dw_pre, dw_post]

    dh1, dmix, g["norm_pre_ffn"], g["norm_post_mix"] = _rowwise(
        "post_mix_bwd", post_mix_bwd, n_rows, tm,
        [(dn2, D_MODEL, 0), (dh2, D_MODEL, 0), (h1, D_MODEL, 0), (mix, D_MODEL, 0)],
        [w["norm_pre_ffn"], w["norm_post_mix"]], [(D_MODEL, F32), (D_MODEL, BF16)], [D_MODEL, D_MODEL])
    g["w_mix_out"] = _matmul("mix_out_dw", mixed, dmix, "tn", F32)
    dmixed = _matmul("mix_out_dx", dmix, w["w_mix_out"], "nt", F32)

    def mix_gate_bwd(r0, dm, ys, ya, gs, ga):
        ss, sa = _sigmoid(gs), _sigmoid(ga)
        dgate = jnp.concatenate([dm * ys * ss * (1.0 - ss), dm * ya * sa * (1.0 - sa)], axis=1)
        return [dm * ss, dm * sa, dgate]

    dys, dya, dp = _rowwise(
        "mix_gate_bwd", mix_gate_bwd, n_rows, tm,
        [(dmixed, D_MODEL, 0), (y_ssm, D_MODEL, 0), (y_attn, D_MODEL, 0)] + gate_ins,
        [], [(D_MODEL, BF16), (D_MODEL, BF16), (2 * D_MODEL, BF16, "new", P_W, P_GATE // (2 * D_MODEL))], [])
    g["w_ssm_out"] = _matmul("ssm_out_dw", yn, dys, "tn", F32)
    dyn = _matmul("ssm_out_dx", dys, w["w_ssm_out"], "nt", F32)
    g["w_attn_out"] = _matmul("attn_out_dw", ao, dya, "tn", F32)
    dao = _matmul("attn_out_dx", dya, w["w_attn_out"], "nt", BF16)

    def gate_norm_bwd(r0, dn, y, z, wn):
        sz = _silu(z)
        dyz, dw = _rms_bwd(dn, y * sz, wn)
        live = _valid_rows(r0, dn.shape[0], PAD)
        return [jnp.where(live, dyz * sz, 0.0), jnp.where(live, dyz * y * _dsilu(z), 0.0), dw]

    dy_ssd, dp, g["ssm_norm"] = _rowwise(
        "ssm_gate_norm_bwd", gate_norm_bwd, n_rows, tm,
        [(dyn, D_INNER, 0), (y_ssd, D_INNER, 0), (p, D_INNER, P_Z // D_INNER)],
        [w["ssm_norm"]], [(D_INNER, F32), (D_INNER, BF16, "into", dp, P_Z // D_INNER)], [D_INNER])
    dp, dsink = _attn_bwd(p, sinks, ao, lse, dao, dp, n_chunks)
    g["attn_sinks"] = dsink[:, 0:ATTN_HEADS]
    dxs, dbm, dcm, ddt_parts, dpar = _ssd_bwd(p, w["ssm_conv_w"], w["ssm_conv_b"], dt_bias, a_log, d_skip, hin,
                                              dy_ssd, n_chunks)
    dpar = jnp.sum(dpar, axis=0)
    g["ssm_dt_bias"], g["ssm_a_log"], g["ssm_d_skip"] = (dpar[i:i + 1, 0:SSM_HEADS] for i in range(3))

    def dt_grad(r0, parts):
        tot = parts[:, 0:128] + parts[:, 128:256] + parts[:, 256:384] + parts[:, 384:512]
        return [jnp.concatenate([tot, jnp.zeros((parts.shape[0], P_Z - P_DT - 128), F32)], axis=1)]

    dt_w = P_Z - P_DT
    dp, = _rowwise("dt_grad", dt_grad, n_rows, tm, [(ddt_parts, SSM_GROUPS * 128, 0)], [],
                   [(dt_w, BF16, "into", dp, P_DT // dt_w)], [])
    x_chunks = [(src, c0) for src, arr in enumerate((dxs, dbm, dcm)) for c0 in range(0, arr.shape[1], CONV_LANES)]
    dp, dconv = _conv_bwd("ssm_conv_bwd", p, P_XBC // CONV_DIM, [dxs, dbm, dcm], x_chunks,
                          w["ssm_conv_w"], w["ssm_conv_b"], n_rows, False, into=dp, into_blk=P_XBC // CONV_DIM)
    g["ssm_conv_w"], g["ssm_conv_b"] = dconv[0:4], dconv[4:5]
    g["w_cat"] = _matmul("in_proj_dw", n1, dp, "tn", F32)
    dn1 = _matmul("in_proj_dx", dp, w["w_cat"], "nt", F32)

    def pre_mix_bwd(r0, dn, d1, xt, hd, wn):
        dx, dw = _rms_bwd(dn, h0_tile(r0, xt, hd), wn)
        dh0 = jnp.where(_valid_rows(r0, dn.shape[0], PAD), dx + d1, 0.0)
        return [dh0, dh0, dw]

    dx_out, dhead, g["norm_pre_mix"] = _rowwise(
        "pre_mix_bwd", pre_mix_bwd, n_rows, T, [(dn1, D_MODEL, 0), (dh1, D_MODEL, 0)] + x_in,
        [w["norm_pre_mix"]], [(D_MODEL, F32, "prev", n_tok), (D_MODEL, F32, "first")], [D_MODEL])
    return jnp.sum(loss_cols), dx_out, dhead, g


_IN_SECTIONS = [((5152, 6176), P_Q), ((6176, 6432), P_K), ((6432, 6688), P_V), ((5120, 5152), P_DT),
                ((0, 2048), P_Z), ((6688, 8736), P_GATE), ((2048, 5120), P_XBC)]


def _to_cat(w_in):
    parts, at = [], 0
    for (a, b), off in _IN_SECTIONS:
        if off > at:
            parts.append(jnp.zeros((w_in.shape[0], off - at), w_in.dtype))
        parts.append(w_in[:, a:b])
        at = off + (b - a)
    return jnp.concatenate(parts, axis=1)


def _from_cat(g_cat):
    pieces = {a: g_cat[:, off:off + (b - a)] for (a, b), off in _IN_SECTIONS}
    return jnp.concatenate([pieces[a] for a in sorted(pieces)], axis=1)


LANES = 1024
_BIG = [("w_in", 1024, 2184, "chip"), ("w_ssm_out", 512, 1024, "row"), ("w_attn_out", 256, 1024, "row"),
        ("w_mix_out", 256, 1024, "row"), ("w_ffn_up", 1024, 1408, "col"), ("w_ffn_down", 704, 1024, "row"),
        ("small", 32, LANES, "chip")]
_SMALL_SHARDED = [("ssm_conv_w", (4, 768), 1), ("ffn_conv_w", (3, 1408), 1), ("meta_tokens", (16, 256), 1)]
_REPLICATED = [("norm_pre_mix", 1024), ("ssm_conv_b", 3072), ("ssm_dt_bias", 32), ("ssm_a_log", 32),
               ("ssm_d_skip", 32), ("ssm_norm", 2048), ("attn_sinks", 16), ("norm_post_mix", 1024),
               ("norm_pre_ffn", 1024), ("ffn_conv_b", 5632), ("norm_post_ffn", 1024)]
SMALL_ROWS = 16
WEIGHT_ORDER = ["meta_tokens", "norm_pre_mix", "w_in", "ssm_conv_w", "ssm_conv_b", "ssm_dt_bias", "ssm_a_log",
                "ssm_d_skip", "ssm_norm", "w_ssm_out", "attn_sinks", "w_attn_out", "w_mix_out", "norm_post_mix",
                "norm_pre_ffn", "w_ffn_up", "ffn_conv_w", "ffn_conv_b", "w_ffn_down", "norm_post_ffn"]


def _flatten(parts, rows):
    flat = jnp.concatenate([a.reshape(-1) for a in parts])
    return jnp.pad(flat, (0, rows * LANES - flat.shape[0])).reshape(rows, LANES)


def _unflatten(flat, shapes):
    flat = flat.reshape(-1)
    out, off = [], 0
    for shp in shapes:
        n = math.prod(shp)
        out.append(flat[off:off + n].reshape(shp))
        off += n
    return out


def _shard_of(full, chip, shape, axis):
    return lax.slice_in_dim(full, chip * shape[axis], (chip + 1) * shape[axis], axis=axis)


def _full_shape(r, c, layout):
    return {"row": (4 * r, c), "col": (r, 4 * c), "chip": (4, r, c)}[layout]


def _shard_view(ref, r, c, layout, chip):
    if layout == "row":
        return ref.at[pl.ds(pl.multiple_of(chip * r, 16), r), :]
    if layout == "col":
        return ref.at[:, pl.ds(pl.multiple_of(chip * c, 128), c)]
    return ref.at[chip]


def _half_view(ref, r, c, layout, chip, half):
    hr = r // 2
    if layout == "row":
        return ref.at[pl.ds(pl.multiple_of(chip * r + half * hr, 16), hr), :]
    r0 = pl.multiple_of(half * hr, 16)
    if layout == "col":
        return ref.at[pl.ds(r0, hr), pl.ds(pl.multiple_of(chip * c, 128), c)]
    return ref.at[chip, pl.ds(r0, hr), :]


def _mesh_pos():
    return lax.axis_index("x"), lax.axis_index("y"), lax.axis_index("c")


def _other_chips(x, y):
    return [(1 - x, y), (x, 1 - y), (1 - x, 1 - y)]


def _chip_index(x, y):
    return 2 * x + y


ANY = pl.BlockSpec(memory_space=pl.ANY)


def _run_exchange(name, make_copies, n_copies, ins, out_shapes):
    n_in = len(ins)
    n_out = len(out_shapes)

    def body(*refs):
        in_refs, out_refs = refs[:n_in], refs[n_in:n_in + n_out]
        send_sems, recv_sems = refs[n_in + n_out:]
        copies = [pltpu.make_async_remote_copy(src_ref=s, dst_ref=d, send_sem=send_sems.at[i], recv_sem=recv_sems.at[i],
                                               device_id=dev, device_id_type=MESH)
                  for i, (s, d, dev) in enumerate(make_copies(in_refs, out_refs))]
        assert len(copies) == n_copies
        for cp in copies:
            cp.start()
        for cp in copies:
            cp.wait()

    return pl.pallas_call(
        body, name=name, in_specs=[ANY] * n_in, out_specs=[ANY] * n_out, out_shape=out_shapes,
        scratch_shapes=[pltpu.SemaphoreType.DMA((n_copies,)), pltpu.SemaphoreType.DMA((n_copies,))],
        compiler_params=pltpu.CompilerParams(has_side_effects=True),
    )(*ins)


def _gather_weights(shards):
    n = len(_BIG)

    def body(*refs):
        ins, outs = refs[:n], refs[n:2 * n]
        send_sems, recv_sems, local_sems = refs[2 * n:]
        x, y, c = _mesh_pos()
        j = _chip_index(x, y)
        sibling = (x, y, 1 - c)
        chips = _other_chips(x, y)
        idx = [_chip_index(*ch) for ch in chips]

        def remote(k, src, dst, dev):
            return pltpu.make_async_remote_copy(src_ref=src, dst_ref=dst, send_sem=send_sems.at[k],
                                                recv_sem=recv_sems.at[k], device_id=dev, device_id_type=MESH)

        own = [pltpu.make_async_copy(ins[a], _shard_view(outs[a], r, cc, lay, j), local_sems.at[a])
               for a, (_, r, cc, lay) in enumerate(_BIG)]
        for cp in own:
            cp.start()
        first, passed = [], []
        for a, (_, r, cc, lay) in enumerate(_BIG):
            mine = ins[a].at[pl.ds(pl.multiple_of(c * (r // 2), 16), r // 2), :]
            for k, ch in enumerate(chips):
                first.append(remote(6 * a + k, mine, _half_view(outs[a], r, cc, lay, j, c), (*ch, c)))
                landed = _half_view(outs[a], r, cc, lay, idx[k], c)
                passed.append(remote(6 * a + 3 + k, landed, landed, sibling))
        for cp in first:
            cp.start()
        for a, (_, r, cc, lay) in enumerate(_BIG):
            for k in range(3):
                landed = _half_view(outs[a], r, cc, lay, idx[k], c)
                remote(6 * a + k, landed, landed, sibling).wait_recv()
                passed[3 * a + k].start()
        for a, (_, r, cc, lay) in enumerate(_BIG):
            for k in range(3):
                theirs = _half_view(outs[a], r, cc, lay, idx[k], 1 - c)
                remote(6 * a + 3 + k, theirs, theirs, sibling).wait_recv()
        for cp in first + passed:
            cp.wait_send()
        for cp in own:
            cp.wait()

    return pl.pallas_call(
        body, name="gather_weights", in_specs=[ANY] * n, out_specs=[ANY] * n,
        out_shape=[jax.ShapeDtypeStruct(_full_shape(r, cc, lay), s.dtype) for s, (_, r, cc, lay) in zip(shards, _BIG)],
        scratch_shapes=[pltpu.SemaphoreType.DMA((6 * n,)), pltpu.SemaphoreType.DMA((6 * n,)), pltpu.SemaphoreType.DMA((n,))],
        compiler_params=pltpu.CompilerParams(has_side_effects=True),
    )(*shards)


def _grad_pair_exchange(grads, rep):
    n = len(_BIG)

    def make(in_refs, out_refs):
        x, y, c = _mesh_pos()
        sibling = (x, y, 1 - c)
        copies = [(_half_view(in_refs[a], r, cc, lay, i, 1 - c), out_refs[a].at[i], sibling)
                  for a, (_, r, cc, lay) in enumerate(_BIG) for i in range(4)]
        return copies + [(in_refs[n], out_refs[n], sibling)]

    shapes = [jax.ShapeDtypeStruct((4, r // 2, cc), F32) for _, r, cc, _ in _BIG]
    return _run_exchange("grad_pair_exchange", make, 4 * n + 1, list(grads) + [rep],
                         shapes + [jax.ShapeDtypeStruct(rep.shape, F32)])


def _grad_chip_exchange(psends, prep):
    n = len(psends)

    def make(in_refs, out_refs):
        x, y, c = _mesh_pos()
        chips = _other_chips(x, y)
        copies = [(in_refs[a].at[_chip_index(*ch)], out_refs[a].at[k], (*ch, c))
                  for a in range(n) for k, ch in enumerate(chips)]
        return copies + [(in_refs[n], out_refs[n].at[k], (*ch, c)) for k, ch in enumerate(chips)]

    shapes = [jax.ShapeDtypeStruct((3,) + p.shape[1:], p.dtype) for p in psends]
    return _run_exchange("grad_chip_exchange", make, 3 * n + 3, list(psends) + [prep],
                         shapes + [jax.ShapeDtypeStruct((3,) + prep.shape, prep.dtype)])


def _grad_half_share(halves):
    def make(in_refs, out_refs):
        x, y, c = _mesh_pos()
        return [(r, o, (x, y, 1 - c)) for r, o in zip(in_refs, out_refs)]

    return _run_exchange("grad_half_share", make, len(halves), list(halves),
                         [jax.ShapeDtypeStruct(h.shape, h.dtype) for h in halves])


SUM_ROWS = 256
ADAM_ROWS = 128


def _pair_sum(name, grad, recv, ids, r, c, layout):
    hr = r // 2
    tr = _row_tile(hr, SUM_ROWS)
    nb = hr // tr

    def body(ids_ref, g_ref, r_ref, send_ref, own_ref):
        s = g_ref[...] + r_ref[...]
        send_ref[...] = s.astype(send_ref.dtype)

        @pl.when(pl.program_id(1) == ids_ref[1])
        def _():
            own_ref[...] = s

    if layout == "row":
        g_spec = pl.BlockSpec((tr, c), lambda t, j, ids_ref: ((j * r + ids_ref[0] * hr) // tr + t, 0))
    elif layout == "col":
        g_spec = pl.BlockSpec((tr, c), lambda t, j, ids_ref: (ids_ref[0] * nb + t, j))
    else:
        g_spec = pl.BlockSpec((None, tr, c), lambda t, j, ids_ref: (j, ids_ref[0] * nb + t, 0))
    grid_spec = pltpu.PrefetchScalarGridSpec(
        num_scalar_prefetch=1, grid=(nb, 4),
        in_specs=[g_spec, pl.BlockSpec((None, tr, c), lambda t, j, ids_ref: (j, t, 0))],
        out_specs=[pl.BlockSpec((None, tr, c), lambda t, j, ids_ref: (j, t, 0)),
                   pl.BlockSpec((tr, c), lambda t, j, ids_ref: (t, 0))])
    return pl.pallas_call(
        body, name=name, grid_spec=grid_spec,
        out_shape=[jax.ShapeDtypeStruct((4, hr, c), BF16), jax.ShapeDtypeStruct((hr, c), F32)],
        compiler_params=_cparams(2),
    )(ids, grad, recv)


def _chip_sum(name, own, recv):
    hr, c = own.shape
    tr = _row_tile(hr, SUM_ROWS)

    def body(o_ref, r_ref, out_ref):
        out_ref[...] = ((o_ref[...] + r_ref[0].astype(F32)) + r_ref[1].astype(F32)) + r_ref[2].astype(F32)

    return pl.pallas_call(
        body, name=name, grid=(hr // tr,),
        in_specs=[pl.BlockSpec((tr, c), lambda i: (i, 0)), pl.BlockSpec((3, tr, c), lambda i: (0, i, 0))],
        out_specs=pl.BlockSpec((tr, c), lambda i: (i, 0)),
        out_shape=jax.ShapeDtypeStruct((hr, c), F32), compiler_params=_cparams(1),
    )(own, recv)


def _chip_sum_small(own, recv, ids):
    def body(ids_ref, o_ref, r_ref, out_ref):
        j = ids_ref[1]
        total = None
        for i in range(4):
            m = jnp.bitwise_xor(i, j)
            term = jnp.where(m == 0, o_ref[...], jnp.where(m == 2, r_ref[0], jnp.where(m == 1, r_ref[1], r_ref[2])))
            total = term if total is None else total + term
        out_ref[...] = total

    grid_spec = pltpu.PrefetchScalarGridSpec(
        num_scalar_prefetch=1, grid=(1,),
        in_specs=[pl.BlockSpec(own.shape, lambda i, ids_ref: (0, 0)), pl.BlockSpec(recv.shape, lambda i, ids_ref: (0, 0, 0))],
        out_specs=pl.BlockSpec(own.shape, lambda i, ids_ref: (0, 0)))
    return pl.pallas_call(body, name="chip_sum_small", grid_spec=grid_spec,
                          out_shape=jax.ShapeDtypeStruct(own.shape, F32), compiler_params=_cparams(1))(ids, own, recv)


def _adamw(name, w, m, v, mine, theirs, ids):
    rows, cols = w.shape
    half = rows // 2
    tr = _row_tile(half, ADAM_ROWS, unit=8)
    nb = half // tr
    c1 = 1.0 / (1.0 - ADAM_B1 ** ADAM_STEP)
    c2 = 1.0 / (1.0 - ADAM_B2 ** ADAM_STEP)

    def body(ids_ref, w_ref, m_ref, v_ref, mine_ref, theirs_ref, g_out, d_out, m_out, v_out):
        g = jnp.where(pl.program_id(0) == ids_ref[0], mine_ref[...], theirs_ref[...])
        m_new = ADAM_B1 * m_ref[...] + (1.0 - ADAM_B1) * g
        v_new = ADAM_B2 * v_ref[...] + (1.0 - ADAM_B2) * (g * g)
        d_out[...] = -ADAM_LR * ((m_new * c1) / (jnp.sqrt(v_new * c2) + ADAM_EPS) + ADAM_WD * w_ref[...])
        g_out[...] = g
        m_out[...] = m_new
        v_out[...] = v_new

    full = pl.BlockSpec((tr, cols), lambda h, i, ids_ref: (h * nb + i, 0))
    part = pl.BlockSpec((tr, cols), lambda h, i, ids_ref: (i, 0))
    grid_spec = pltpu.PrefetchScalarGridSpec(num_scalar_prefetch=1, grid=(2, nb),
                                             in_specs=[full, full, full, part, part], out_specs=[full] * 4)
    return pl.pallas_call(
        body, name=name, grid_spec=grid_spec,
        out_shape=[jax.ShapeDtypeStruct((rows, cols), F32)] * 4, compiler_params=_cparams(2),
    )(ids, w, m, v, mine, theirs)


def _small_shard(parts):
    return _flatten(parts, _BIG[-1][1])


def kernel(x, meta_tokens, norm_pre_mix, w_in, ssm_conv_w, ssm_conv_b, ssm_dt_bias, ssm_a_log, ssm_d_skip, ssm_norm, w_ssm_out, attn_sinks, w_attn_out, w_mix_out, norm_post_mix, norm_pre_ffn, w_ffn_up, ffn_conv_w, ffn_conv_b, w_ffn_down, norm_post_ffn, loss_target, m_meta_tokens, m_norm_pre_mix, m_w_in, m_ssm_conv_w, m_ssm_conv_b, m_ssm_dt_bias, m_ssm_a_log, m_ssm_d_skip, m_ssm_norm, m_w_ssm_out, m_attn_sinks, m_w_attn_out, m_w_mix_out, m_norm_post_mix, m_norm_pre_ffn, m_w_ffn_up, m_ffn_conv_w, m_ffn_conv_b, m_w_ffn_down, m_norm_post_ffn, v_meta_tokens, v_norm_pre_mix, v_w_in, v_ssm_conv_w, v_ssm_conv_b, v_ssm_dt_bias, v_ssm_a_log, v_ssm_d_skip, v_ssm_norm, v_w_ssm_out, v_attn_sinks, v_w_attn_out, v_w_mix_out, v_norm_post_mix, v_norm_pre_ffn, v_w_ffn_up, v_ffn_conv_w, v_ffn_conv_b, v_w_ffn_down, v_norm_post_ffn):
    args = dict(locals())
    squeeze = lambda a: a.reshape(a.shape[-2:])
    wts = {n: squeeze(args[n]) for n in WEIGHT_ORDER}
    mom = {n: squeeze(args["m_" + n]) for n in WEIGHT_ORDER}
    var = {n: squeeze(args["v_" + n]) for n in WEIGHT_ORDER}
    x_i, y_i, c_i = _mesh_pos()
    ids = jnp.stack([c_i, _chip_index(x_i, y_i)]).astype(jnp.int32)
    big_names = [n for n, _, _, _ in _BIG[:-1]]
    small_names = [n for n, _, _ in _SMALL_SHARDED]
    rep_names = [n for n, _ in _REPLICATED]

    w_small = _small_shard([wts[n] for n in small_names])
    gathered = _gather_weights([wts[n].astype(BF16) for n in big_names] + [w_small])
    w = {n: wts[n] for n in rep_names}
    w.update(zip(big_names[1:], gathered[1:-1]))
    w["w_cat"] = _to_cat(jnp.transpose(gathered[0], (1, 0, 2)).reshape(D_MODEL, N_IN))
    small_all = [_unflatten(gathered[-1][i], [shp for _, shp, _ in _SMALL_SHARDED]) for i in range(4)]
    for k, (n, _, axis) in enumerate(_SMALL_SHARDED):
        w[n] = jnp.concatenate([small_all[i][k] for i in range(4)], axis=axis)

    head = jnp.concatenate([jnp.zeros((PAD, D_MODEL), F32), w["meta_tokens"]], axis=0)
    loss_sum, dx, dhead, g = _local_step(x[0], head, loss_target[0], w)
    loss = lax.psum(loss_sum * (0.5 / D_MODEL), ("x", "y", "c"))
    g["meta_tokens"] = dhead[PAD:]
    g_in = _from_cat(g.pop("w_cat")).reshape(D_MODEL, 4, N_IN // 4)
    g["w_in"] = jnp.transpose(g_in, (1, 0, 2))

    grads = [g[n] for n in big_names]
    grads.append(jnp.stack([_small_shard([_shard_of(g[n], i, shp, ax) for n, shp, ax in _SMALL_SHARDED])
                            for i in range(4)]))
    g_rep = _flatten([g[n] for n in rep_names], SMALL_ROWS)
    *recv, recv_rep = _grad_pair_exchange(grads, g_rep)
    sums = [_pair_sum("pair_sum_" + n, gr, rv, ids, r, c, lay) for gr, rv, (n, r, c, lay) in zip(grads, recv, _BIG)]
    p_rep, = _rowwise("pair_sum_replicated", lambda r0, a, b: [a + b], SMALL_ROWS, SMALL_ROWS,
                      [(g_rep, LANES, 0), (recv_rep, LANES, 0)], [], [(LANES, F32)], [])
    *recv2, recv2_rep = _grad_chip_exchange([s[0] for s in sums], p_rep)
    halves = [_chip_sum("chip_sum_" + n, s[1], rv) for s, rv, (n, _, _, _) in zip(sums, recv2, _BIG)]
    g_rep_tot = _chip_sum_small(p_rep, recv2_rep, ids)
    theirs = _grad_half_share(halves)

    stacks = {"w": wts, "m": mom, "v": var}
    shard_in = {k: [d[n] for n in big_names] + [_small_shard([d[n] for n in small_names])] for k, d in stacks.items()}
    results = {}
    for a, (n, _, _, _) in enumerate(_BIG):
        res = _adamw("adamw_" + n, shard_in["w"][a], shard_in["m"][a], shard_in["v"][a], halves[a], theirs[a], ids)
        if n == "small":
            for kind in range(4):
                parts = _unflatten(res[kind], [shp for _, shp, _ in _SMALL_SHARDED])
                results.update({(kind, sn): parts[k] for k, sn in enumerate(small_names)})
        else:
            results.update({(kind, n): res[kind] for kind in range(4)})
    rep_in = {k: _flatten([d[n] for n in rep_names], SMALL_ROWS) for k, d in stacks.items()}
    ids_lo = ids * jnp.array([0, 1], jnp.int32)
    res = _adamw("adamw_replicated", rep_in["w"], rep_in["m"], rep_in["v"],
                 g_rep_tot[0:SMALL_ROWS // 2], g_rep_tot[SMALL_ROWS // 2:], ids_lo)
    for kind in range(4):
        parts = _unflatten(res[kind], [(1, width) for _, width in _REPLICATED])
        results.update({(kind, rn): parts[k] for k, rn in enumerate(rep_names)})
    outs = [results[kind, n].reshape(args[n].shape) for kind in range(4) for n in WEIGHT_ORDER]
    return (loss, dx[None], *outs)
```

```python
import math

import jax
import jax.numpy as jnp
from jax import lax
from jax.experimental import pallas as pl
from jax.experimental.pallas import tpu as pltpu

F32 = jnp.float32
BF16 = jnp.bfloat16

D_MODEL = 1024
N_META = 16
T = 128
PAD = T - N_META
D_INNER = 2048
SSM_HEADS = 32
HEAD_P = 64
SSM_GROUPS = 4
GROUP_W = D_INNER // SSM_GROUPS
D_STATE = 128
CONV_DIM = D_INNER + 2 * SSM_GROUPS * D_STATE
ATTN_HEADS = 16
KV_HEADS = 4
ATTN_W = 1024
KV_W = 256
FFN_DIM = 2816
N_IN = 8736
EPS = 1e-6
NEG = -1e30
SCALE = 0.125

P_Q, P_K, P_V, P_DT, P_Z, P_GATE, P_XBC = 0, 1024, 1280, 1536, 2048, 4096, 6144
QKV_W = 1536
P_W = 9216

ADAM_LR, ADAM_B1, ADAM_B2, ADAM_EPS, ADAM_WD, ADAM_STEP = 0.001, 0.9, 0.999, 1e-08, 0.01, 10

VMEM_BUDGET = 40 * 1024 * 1024
VMEM_LIMIT = 56 * 1024 * 1024
MESH = pl.DeviceIdType.MESH


def _cparams(n_axes, **kw):
    return pltpu.CompilerParams(dimension_semantics=("arbitrary",) * n_axes, vmem_limit_bytes=VMEM_LIMIT, **kw)


def _sigmoid(x):
    return 1.0 / (1.0 + jnp.exp(-x))


def _silu(x):
    return x * _sigmoid(x)


def _dsilu(x):
    s = _sigmoid(x)
    return s * (1.0 + x * (1.0 - s))


def _softplus(x):
    e = jnp.exp(-jnp.abs(x))
    small = e * (1.0 - e * (0.5 - e * (1.0 / 3.0)))
    return jnp.maximum(x, 0.0) + jnp.where(e < 0.01, small, jnp.log(1.0 + e))


def _rms(x, w):
    r = lax.rsqrt(jnp.mean(x * x, axis=-1, keepdims=True) + EPS)
    return x * r * w


def _rms_bwd(dy, x, w):
    r = lax.rsqrt(jnp.mean(x * x, axis=-1, keepdims=True) + EPS)
    xh = x * r
    g = dy * w
    dx = r * (g - xh * jnp.mean(g * xh, axis=-1, keepdims=True))
    dw = jnp.sum(dy * xh, axis=0, keepdims=True)
    return dx, dw


def _dot(a, b):
    return jnp.dot(a, b, preferred_element_type=F32)


def _dot_nt(a, b):
    return lax.dot_general(a, b, (((1,), (1,)), ((), ())), preferred_element_type=F32)


def _dot_tn(a, b):
    return lax.dot_general(a, b, (((0,), (0,)), ((), ())), preferred_element_type=F32)


def _split3(x):
    hi = x.astype(BF16)
    r = x - hi.astype(F32)
    mid = r.astype(BF16)
    lo = (r - mid.astype(F32)).astype(BF16)
    return hi, mid, lo


def _xdot(x, e):
    hi, mid, lo = _split3(x)
    return _dot(hi, e) + _dot(mid, e) + _dot(lo, e)


def _xdot_l(e, x):
    hi, mid, lo = _split3(x)
    return _dot(e, hi) + _dot(e, mid) + _dot(e, lo)


def _iota(shape, dim):
    return lax.broadcasted_iota(jnp.int32, shape, dim)


def _divisors(n, unit):
    return [t for t in range(unit, n + 1, unit) if n % t == 0]


def _matmul_tiles(m, n, k, a_bytes, b_bytes, o_bytes, m_unit):
    best = None
    for tm in _divisors(m, m_unit):
        for tn in _divisors(n, 128):
            for tk in _divisors(k, 128):
                acc = 0 if tk == k else tm * tn * 4
                vm = 2 * (tm * tk * a_bytes + tk * tn * b_bytes + tm * tn * o_bytes) + acc
                if vm > VMEM_BUDGET:
                    continue
                score = (tm * tn * tk, tk)
                if best is None or score > best[0]:
                    best = (score, (tm, tn, tk))
    return best[1]


def _matmul(name, a, b, mode, out_dtype):
    if mode == "nn":
        (m, k), n = a.shape, b.shape[1]
    elif mode == "nt":
        (m, k), n = a.shape, b.shape[0]
    else:
        (k, m), n = a.shape, b.shape[1]
    ab, bb, ob = a.dtype.itemsize, b.dtype.itemsize, jnp.dtype(out_dtype).itemsize
    tm, tn, tk = _matmul_tiles(m, n, k, ab, bb, ob, 128 if mode == "tn" else 16)
    nk = k // tk
    dot = {"nn": _dot, "nt": _dot_nt, "tn": _dot_tn}[mode]

    def body(a_ref, b_ref, o_ref, *scratch):
        prod = dot(a_ref[...].astype(BF16), b_ref[...].astype(BF16))
        if nk == 1:
            o_ref[...] = prod.astype(o_ref.dtype)
        else:
            acc_ref, = scratch
            kk = pl.program_id(2)

            @pl.when(kk == 0)
            def _():
                acc_ref[...] = prod

            @pl.when(kk > 0)
            def _():
                acc_ref[...] += prod

            @pl.when(kk == nk - 1)
            def _():
                o_ref[...] = acc_ref[...].astype(o_ref.dtype)

    a_spec = pl.BlockSpec((tk, tm), lambda i, j, kk: (kk, i)) if mode == "tn" else pl.BlockSpec((tm, tk), lambda i, j, kk: (i, kk))
    b_spec = pl.BlockSpec((tn, tk), lambda i, j, kk: (j, kk)) if mode == "nt" else pl.BlockSpec((tk, tn), lambda i, j, kk: (kk, j))
    return pl.pallas_call(
        body, name=name, grid=(m // tm, n // tn, nk),
        in_specs=[a_spec, b_spec], out_specs=pl.BlockSpec((tm, tn), lambda i, j, kk: (i, j)),
        out_shape=jax.ShapeDtypeStruct((m, n), out_dtype),
        scratch_shapes=[] if nk == 1 else [pltpu.VMEM((tm, tn), F32)],
        compiler_params=_cparams(3),
    )(a, b)


def _row_tile(n_rows, cap, unit=16):
    return max(t for t in _divisors(n_rows, unit) if t <= cap)


ROW_SUB = 384
GROUP_UNROLL = 4


def _rowwise(name, fn, n_rows, tm, row_ins, full_ins, row_outs, acc_outs):
    n_in = len(row_ins) + len(full_ins)
    n_ro = len(row_outs)
    into = [(k, o[3]) for k, o in enumerate(row_outs) if len(o) > 2 and o[2] == "into"]

    n_row_in = len(row_ins)
    sub = min(tm, ROW_SUB)

    def body(*refs):
        i = pl.program_id(0)
        outs = refs[n_in + len(into):]

        def group(s, sums):
            rows = pl.ds(pl.multiple_of(s * sub, sub), sub)
            vals = [r[rows, :] for r in refs[:n_row_in]] + [r[...] for r in refs[n_row_in:n_in]]
            res = fn(i * tm + s * sub, *vals)
            for o, r, v in zip(row_outs, outs[:n_ro], res[:n_ro]):
                if len(o) > 2 and o[2] == "first":
                    @pl.when(i == 0)
                    def _(r=r, v=v):
                        r[rows, :] = v.astype(r.dtype)
                else:
                    r[rows, :] = v.astype(r.dtype)
            return tuple(a + v for a, v in zip(sums, res[n_ro:]))

        sums = lax.fori_loop(0, tm // sub, group, tuple(jnp.zeros((1, w), F32) for w in acc_outs), unroll=GROUP_UNROLL)

        @pl.when(i == 0)
        def _():
            for r, v in zip(outs[n_ro:], sums):
                r[...] = v

        @pl.when(i > 0)
        def _():
            for r, v in zip(outs[n_ro:], sums):
                r[...] += v

    def in_spec(entry):
        w, cb = entry[1], entry[2]
        if len(entry) > 3 and entry[3] == "prev":
            return pl.BlockSpec((tm, w), lambda i: (jnp.maximum(i - 1, 0), cb))
        if len(entry) > 3 and entry[3] == "first":
            return pl.BlockSpec((tm, w), lambda i: (0, cb))
        return pl.BlockSpec((tm, w), lambda i: (i, cb))

    def out_spec(o):
        if len(o) == 2:
            return pl.BlockSpec((tm, o[0]), lambda i: (i, 0)), jax.ShapeDtypeStruct((n_rows, o[0]), o[1])
        if o[2] == "new":
            return pl.BlockSpec((tm, o[0]), lambda i: (i, o[4])), jax.ShapeDtypeStruct((n_rows, o[3]), o[1])
        if o[2] == "into":
            return pl.BlockSpec((tm, o[0]), lambda i: (i, o[4])), jax.ShapeDtypeStruct(o[3].shape, o[3].dtype)
        if o[2] == "first":
            return pl.BlockSpec((tm, o[0]), lambda i: (0, 0)), jax.ShapeDtypeStruct((tm, o[0]), o[1])
        return pl.BlockSpec((tm, o[0]), lambda i: (jnp.maximum(i - 1, 0), 0)), jax.ShapeDtypeStruct((o[3], o[0]), o[1])

    in_specs = [in_spec(e) for e in row_ins]
    in_specs += [pl.BlockSpec(a.shape, lambda i: (0, 0)) for a in full_ins]
    in_specs += [pl.BlockSpec(memory_space=pl.ANY) for _ in into]
    specs_shapes = [out_spec(o) for o in row_outs]
    out_specs = [s for s, _ in specs_shapes] + [pl.BlockSpec((1, w), lambda i: (0, 0)) for w in acc_outs]
    out_shape = [s for _, s in specs_shapes] + [jax.ShapeDtypeStruct((1, w), F32) for w in acc_outs]
    return pl.pallas_call(
        body, name=name, grid=(n_rows // tm,), in_specs=in_specs, out_specs=out_specs, out_shape=out_shape,
        input_output_aliases={n_in + a: k for a, (k, _) in enumerate(into)},
        compiler_params=_cparams(1),
    )(*[e[0] for e in row_ins], *full_ins, *[arr for _, arr in into])


def _valid_rows(first_row, tm, lo):
    return (first_row + _iota((tm, 1), 0)) >= lo


CONV_ROWS = 128
CONV_SUB = 16
CONV_LANES = 256


def _conv_specs(tm, width, blk, n_rows, after):
    specs = [pl.BlockSpec((tm, width), lambda i: (i, blk)),
             pl.BlockSpec((8, width), lambda i: (jnp.maximum(i * (tm // 8) - 1, 0), blk))]
    if after:
        specs.append(pl.BlockSpec((16, width), lambda i: (jnp.minimum((i + 1) * (tm // 16), n_rows // 16 - 1), blk)))
    return specs


def _conv_window(win, w_ref, b_ref, taps, c0, cw, n):
    acc = b_ref[:, c0:c0 + cw] + w_ref[taps - 1:taps, c0:c0 + cw] * win[8:8 + n]
    for k in range(taps - 1):
        acc = acc + w_ref[k:k + 1, c0:c0 + cw] * win[8 - (taps - 1) + k:8 - (taps - 1) + k + n]
    return acc


def _ffn_act(name, u_raw, conv_w, conv_b, n_rows):
    tm, sub, cw = CONV_ROWS, CONV_SUB, CONV_LANES
    taps, width = conv_w.shape
    half = width // 2

    def body(cur_ref, prev_ref, w_ref, b_ref, f_ref, ext_ref):
        i = pl.program_id(0)
        ext_ref[0:8, :] = jnp.where(i > 0, prev_ref[...], 0.0)
        ext_ref[8:8 + tm, :] = cur_ref[...]
        for q in range(half // cw):
            a0, g0 = q * cw, half + q * cw

            def group(s, carry):
                r = pl.multiple_of(s * sub, sub)
                a = _conv_window(ext_ref[pl.ds(r, sub + 8), a0:a0 + cw], w_ref, b_ref, taps, a0, cw, sub)
                g = _conv_window(ext_ref[pl.ds(r, sub + 8), g0:g0 + cw], w_ref, b_ref, taps, g0, cw, sub)
                f = jnp.where(_valid_rows(i * tm + r, sub, PAD), _silu(a) * g, 0.0)
                f_ref[pl.ds(r, sub), a0:a0 + cw] = f.astype(f_ref.dtype)
                return carry

            lax.fori_loop(0, tm // sub, group, 0, unroll=GROUP_UNROLL)

    return pl.pallas_call(
        body, name=name, grid=(n_rows // tm,),
        in_specs=_conv_specs(tm, width, 0, n_rows, False) + [pl.BlockSpec((taps, width), lambda i: (0, 0)),
                                                             pl.BlockSpec((1, width), lambda i: (0, 0))],
        out_specs=pl.BlockSpec((tm, half), lambda i: (i, 0)),
        out_shape=jax.ShapeDtypeStruct((n_rows, half), BF16),
        scratch_shapes=[pltpu.VMEM((tm + 8, width), F32)],
        compiler_params=_cparams(1),
    )(u_raw, u_raw, conv_w, conv_b)


def _conv_bwd(name, raw, raw_blk, dsrcs, chunk_src, conv_w, conv_b, n_rows, gated, into=None, into_blk=0):
    taps, width = conv_w.shape
    half = width // 2 if gated else width
    tm, sub, cw = CONV_ROWS, CONV_SUB, CONV_LANES
    te = tm + 16
    nd = len(dsrcs)
    n_parts = 2 if gated else 1

    def body(*refs):
        cur_ref, prev_ref, next_ref = refs[0:3]
        dcur, dnext = refs[3:3 + nd], refs[3 + nd:3 + 2 * nd]
        w_ref, b_ref = refs[3 + 2 * nd:5 + 2 * nd]
        out_ref, acc_ref, ext_ref, du_ref = refs[-4:]
        i = pl.program_id(0)
        ext_ref[0:8, :] = jnp.where(i > 0, prev_ref[...], 0.0)
        ext_ref[8:8 + tm, :] = cur_ref[...]
        ext_ref[8 + tm:24 + tm, :] = next_ref[...]

        for q, (src, off) in enumerate(chunk_src):
            cols = [q * cw, half + q * cw][:n_parts]

            def conv_grad(r, d):
                pre = [_conv_window(ext_ref[pl.ds(r, sub + 8), c0:c0 + cw], w_ref, b_ref, taps, c0, cw, sub) for c0 in cols]
                row = i * tm + r + _iota((sub, 1), 0)
                live = (row >= PAD) & (row < n_rows)
                if gated:
                    dus = [d * pre[1] * _dsilu(pre[0]), d * _silu(pre[0])]
                else:
                    dus = [d * _dsilu(pre[0])]
                for part, du in enumerate(dus):
                    du_ref[part, pl.ds(r, sub), :] = jnp.where(live, du, 0.0)

            def tile_rows(s, carry):
                r = pl.multiple_of(s * sub, sub)
                conv_grad(r, dcur[src][pl.ds(r, sub), off:off + cw].astype(F32))
                return carry

            lax.fori_loop(0, tm // sub, tile_rows, 0, unroll=GROUP_UNROLL)
            conv_grad(tm, dnext[src][:, off:off + cw].astype(F32))

            for part, c0 in enumerate(cols):
                taps_w = [w_ref[k:k + 1, c0:c0 + cw] for k in range(taps)]

                def back(s, sums):
                    new = list(sums)
                    for u in range(2):
                        r = pl.multiple_of((2 * s + u) * sub, sub)
                        win = du_ref[part, pl.ds(r, sub + 8), :]
                        raw_rows = ext_ref[pl.ds(8 + r, sub), c0:c0 + cw]
                        draw = jnp.zeros((sub, cw), F32)
                        for k in range(taps):
                            shifted = win[taps - 1 - k:taps - 1 - k + sub]
                            draw = draw + taps_w[k] * shifted
                            new[k] = new[k] + shifted * raw_rows
                        new[taps] = new[taps] + win[0:sub]
                        out_ref[pl.ds(r, sub), c0:c0 + cw] = jnp.where(_valid_rows(i * tm + r, sub, PAD), draw, 0.0).astype(out_ref.dtype)
                    return tuple(new)

                sums = lax.fori_loop(0, tm // (2 * sub), back, tuple(jnp.zeros((sub, cw), F32) for _ in range(taps + 1)))
                for k in range(taps + 1):
                    total = jnp.sum(sums[k], axis=0, keepdims=True)
                    acc_ref[k:k + 1, c0:c0 + cw] = jnp.where(i == 0, total, acc_ref[k:k + 1, c0:c0 + cw] + total)

    in_specs = _conv_specs(tm, width, raw_blk, n_rows, True)
    in_specs += [pl.BlockSpec((tm, d.shape[1]), lambda i: (i, 0)) for d in dsrcs]
    in_specs += [pl.BlockSpec((16, d.shape[1]), lambda i: (jnp.minimum((i + 1) * (tm // 16), n_rows // 16 - 1), 0)) for d in dsrcs]
    in_specs += [pl.BlockSpec((taps, width), lambda i: (0, 0)), pl.BlockSpec((1, width), lambda i: (0, 0))]
    operands = [raw, raw, raw] + list(dsrcs) + list(dsrcs) + [conv_w, conv_b]
    aliases = {}
    if into is None:
        out0 = jax.ShapeDtypeStruct((n_rows, width), BF16)
    else:
        in_specs.append(pl.BlockSpec(memory_space=pl.ANY))
        operands.append(into)
        aliases = {len(operands) - 1: 0}
        out0 = jax.ShapeDtypeStruct(into.shape, into.dtype)
    return pl.pallas_call(
        body, name=name, grid=(n_rows // tm,), in_specs=in_specs,
        out_specs=[pl.BlockSpec((tm, width), lambda i: (i, into_blk)), pl.BlockSpec((8, width), lambda i: (0, 0))],
        out_shape=[out0, jax.ShapeDtypeStruct((8, width), F32)], input_output_aliases=aliases,
        scratch_shapes=[pltpu.VMEM((tm + 24, width), F32), pltpu.VMEM((n_parts, te + 8, cw), F32)],
        compiler_params=_cparams(1),
    )(*operands)


def _ssd_specs(n_chunks, rev):
    cidx = (lambda c: n_chunks - 1 - c) if rev else (lambda c: c)
    xg0, bg0, cg0 = P_XBC // GROUP_W, (P_XBC + D_INNER) // D_STATE, (P_XBC + D_INNER + SSM_GROUPS * D_STATE) // D_STATE

    def cur(width, blk0):
        return pl.BlockSpec((T, width), lambda g, c: (cidx(c), blk0 + g))

    def prev(width, blk0):
        return pl.BlockSpec((8, width), lambda g, c: (jnp.maximum(cidx(c) * (T // 8) - 1, 0), blk0 + g))

    specs = [cur(GROUP_W, xg0), prev(GROUP_W, xg0), cur(D_STATE, bg0), prev(D_STATE, bg0),
             cur(D_STATE, cg0), prev(D_STATE, cg0),
             pl.BlockSpec((T, 128), lambda g, c: (cidx(c), P_DT // 128))]
    wx, wb, wc = 0, D_INNER // D_STATE, (D_INNER + SSM_GROUPS * D_STATE) // D_STATE
    specs += [pl.BlockSpec((4, GROUP_W), lambda g, c: (0, g)),
              pl.BlockSpec((4, D_STATE), lambda g, c: (0, wb + g)),
              pl.BlockSpec((4, D_STATE), lambda g, c: (0, wc + g)),
              pl.BlockSpec((1, GROUP_W), lambda g, c: (0, g)),
              pl.BlockSpec((1, D_STATE), lambda g, c: (0, wb + g)),
              pl.BlockSpec((1, D_STATE), lambda g, c: (0, wc + g))]
    specs += [pl.BlockSpec((1, 128), lambda g, c: (0, 0))] * 3
    return specs, cidx


def _ssd_chunk_forward(refs, ext_ref, g, c):
    (xc_ref, xp_ref, bc_ref, bp_ref, cc_ref, cp_ref, dt_ref, wx_ref, wb_ref, wc_ref,
     bx_ref, bb_ref, bcb_ref, dtb_ref, alog_ref, dsk_ref) = refs

    def conv_pre(cur_ref, prev_ref, w_ref, b_ref, width):
        ext_ref[0:8, 0:width] = jnp.where(c > 0, prev_ref[...], 0.0)
        ext_ref[8:8 + T, 0:width] = cur_ref[...]
        w = w_ref[...]
        acc = b_ref[...] + w[3:4] * cur_ref[...]
        for k in range(3):
            acc = acc + w[k:k + 1] * ext_ref[pl.ds(5 + k, T), 0:width]
        return acc

    valid = _valid_rows(c * T, T, PAD)
    v = {}
    v["valid"] = valid
    v["x_pre"] = conv_pre(xc_ref, xp_ref, wx_ref, bx_ref, GROUP_W)
    v["b_pre"] = conv_pre(bc_ref, bp_ref, wb_ref, bb_ref, D_STATE)
    v["c_pre"] = conv_pre(cc_ref, cp_ref, wc_ref, bcb_ref, D_STATE)
    xs = _silu(v["x_pre"])
    bm = jnp.where(valid, _silu(v["b_pre"]), 0.0)
    cm = jnp.where(valid, _silu(v["c_pre"]), 0.0)
    dtr = dt_ref[...] + dtb_ref[...]
    dt = jnp.where(valid, _softplus(dtr), 0.0)
    a_neg = -jnp.exp(alog_ref[...])
    a = dt * a_neg
    tril = _iota((T, T), 0) >= _iota((T, T), 1)
    cs = _xdot_l(tril.astype(BF16), a)
    hh, ll = _iota((128, GROUP_W), 0), _iota((128, GROUP_W), 1)
    expand = (hh == 8 * g + jnp.right_shift(ll, 6)).astype(BF16)
    sh, sj = _iota((128, 128), 0), _iota((128, 128), 1)
    select = ((sh == 8 * g + sj) & (sj < 8)).astype(BF16)
    hh_t, ll_t = _iota((GROUP_W, 128), 1), _iota((GROUP_W, 128), 0)
    v["expand_t"] = (hh_t == 8 * g + jnp.right_shift(ll_t, 6)).astype(BF16)
    v["select_t"] = ((sj == 8 * g + sh) & (sh < 8)).astype(BF16)
    cs_e = _xdot(cs, expand)
    dt_e = _xdot(dt, expand)
    cs_loc = _xdot(cs, select)
    cs_loc_t = cs_loc.T
    cs_last_e = cs_e[T - 1:T, :]
    v.update(xs=xs, bm=bm, cm=cm, dtr=dtr, dt=dt, a_neg=a_neg, tril=tril, expand=expand, select=select,
             cs_e=cs_e, dt_e=dt_e, cs_loc=cs_loc, cs_loc_t=cs_loc_t, cs_last_e=cs_last_e)
    v["xdt"] = xs * dt_e
    v["decay_e"] = jnp.exp(cs_last_e - cs_e)
    v["ecs_e"] = jnp.exp(cs_e)
    v["elast_e"] = jnp.exp(cs_last_e)
    v["d_e"] = _xdot(dsk_ref[...], expand)
    v["gmat"] = _dot_nt(cm.astype(BF16), bm.astype(BF16))
    return v


def _ssd_decay_pair(v, jp):
    out = []
    for j in (2 * jp, 2 * jp + 1):
        diff = v["cs_loc"][:, j:j + 1] - v["cs_loc_t"][j:j + 1, :]
        out.append(jnp.where(v["tril"], jnp.exp(jnp.where(v["tril"], diff, 0.0)), 0.0))
    return out


def _block_diag_pair(xp):
    lane = _iota(xp.shape, 1)
    return jnp.concatenate([jnp.where(lane < HEAD_P, xp, 0.0), jnp.where(lane >= HEAD_P, xp, 0.0)], axis=0)


def _ssd_fwd(p, conv_w, conv_b, dt_bias, a_log, d_skip, n_chunks):
    n_rows = n_chunks * T
    in_specs, _ = _ssd_specs(n_chunks, rev=False)

    def body(*refs):
        y_ref, hin_ref, st_ref, ext_ref = refs[16:]
        g, c = pl.program_id(0), pl.program_id(1)

        @pl.when(c == 0)
        def _():
            st_ref[...] = jnp.zeros_like(st_ref)

        v = _ssd_chunk_forward(refs[:16], ext_ref, g, c)
        state = st_ref[...]
        hin_ref[...] = state
        ys = []
        for jp in range(4):
            l0, l1 = _ssd_decay_pair(v, jp)
            lhs = jnp.concatenate([v["gmat"] * l0, v["gmat"] * l1], axis=1).astype(BF16)
            rhs = _block_diag_pair(v["xdt"][:, 128 * jp:128 * jp + 128]).astype(BF16)
            ys.append(_dot(lhs, rhs))
        y = jnp.concatenate(ys, axis=1)
        y = y + _dot(v["cm"].astype(BF16), state.astype(BF16)) * v["ecs_e"] + v["xs"] * v["d_e"]
        y_ref[...] = y
        s_new = _dot_tn(v["bm"].astype(BF16), (v["xdt"] * v["decay_e"]).astype(BF16))
        st_ref[...] = state * v["elast_e"] + s_new

    return pl.pallas_call(
        body, name="ssd_fwd", grid=(SSM_GROUPS, n_chunks), in_specs=in_specs,
        out_specs=[pl.BlockSpec((T, GROUP_W), lambda g, c: (c, g)),
                   pl.BlockSpec((None, None, D_STATE, GROUP_W), lambda g, c: (g, c, 0, 0))],
        out_shape=[jax.ShapeDtypeStruct((n_rows, D_INNER), F32),
                   jax.ShapeDtypeStruct((SSM_GROUPS, n_chunks, D_STATE, GROUP_W), F32)],
        scratch_shapes=[pltpu.VMEM((D_STATE, GROUP_W), F32), pltpu.VMEM((T + 8, GROUP_W), F32)],
        compiler_params=_cparams(2),
    )(p, p, p, p, p, p, p, conv_w, conv_w, conv_w, conv_b, conv_b, conv_b, dt_bias, a_log, d_skip)


def _ssd_bwd(p, conv_w, conv_b, dt_bias, a_log, d_skip, hin, dy, n_chunks):
    n_rows = n_chunks * T
    in_specs, cidx = _ssd_specs(n_chunks, rev=True)
    in_specs = in_specs + [pl.BlockSpec((None, None, D_STATE, GROUP_W), lambda g, c: (g, cidx(c), 0, 0)),
                           pl.BlockSpec((T, GROUP_W), lambda g, c: (cidx(c), g))]

    def body(*refs):
        hin_ref, dy_ref = refs[16:18]
        dx_ref, db_ref, dc_ref, ddt_ref, dpar_ref, dst_ref, ext_ref = refs[18:]
        g, step = pl.program_id(0), pl.program_id(1)
        c = n_chunks - 1 - step

        @pl.when(step == 0)
        def _():
            dst_ref[...] = jnp.zeros_like(dst_ref)

        v = _ssd_chunk_forward(refs[:16], ext_ref, g, c)
        hin_f = hin_ref[...]
        hin_b = hin_f.astype(BF16)
        dyv = dy_ref[...]
        dst = dst_ref[...]
        dst_b = dst.astype(BF16)
        xs, bm, cm, xdt = v["xs"], v["bm"], v["cm"], v["xdt"]
        bm_b, cm_b = bm.astype(BF16), cm.astype(BF16)

        dd_e = jnp.sum(dyv * xs, axis=0, keepdims=True)
        dxs = dyv * v["d_e"]
        ch = _dot(cm_b, hin_b)
        dch = (dyv * v["ecs_e"]).astype(BF16)
        dcm = _dot_nt(dch, hin_b)
        dhin = _dot_tn(cm_b, dch) + dst * v["elast_e"]
        dcs_e = dyv * ch * v["ecs_e"]
        dxd = _dot(bm_b, dst_b)
        dbm = _dot_nt((xdt * v["decay_e"]).astype(BF16), dst_b)
        dxdt_state = dxd * v["decay_e"]
        q = dxdt_state * xdt
        dcs_e = dcs_e - q
        dlast_e = jnp.sum(q, axis=0, keepdims=True) + jnp.sum(dst * hin_f, axis=0, keepdims=True) * v["elast_e"]
        dg = jnp.zeros((T, T), F32)
        rs_cols = jnp.zeros((T, 128), F32)
        cs_rows = jnp.zeros((128, T), F32)
        lane_i, sub_i = _iota((T, 128), 1), _iota((128, T), 0)
        dxdt_parts = []
        for jp in range(4):
            l0, l1 = _ssd_decay_pair(v, jp)
            m0, m1 = v["gmat"] * l0, v["gmat"] * l1
            xbd = _block_diag_pair(xdt[:, 128 * jp:128 * jp + 128]).astype(BF16)
            dyp = dyv[:, 128 * jp:128 * jp + 128]
            dm = _dot_nt(dyp.astype(BF16), xbd)
            dm0, dm1 = dm[:, 0:T], dm[:, T:2 * T]
            dg = dg + dm0 * l0 + dm1 * l1
            for j, qq in ((2 * jp, dm0 * m0), (2 * jp + 1, dm1 * m1)):
                rs_cols = jnp.where(lane_i == j, jnp.sum(qq, axis=1, keepdims=True), rs_cols)
                cs_rows = jnp.where(sub_i == j, jnp.sum(qq, axis=0, keepdims=True), cs_rows)
            mv = jnp.concatenate([m0, m1], axis=0).astype(BF16)
            dxdt_parts.append(_dot_tn(mv, _block_diag_pair(dyp).astype(BF16)))
        dxdt = jnp.concatenate(dxdt_parts, axis=1) + dxdt_state
        dg_b = dg.astype(BF16)
        dcm = dcm + _dot(dg_b, bm_b)
        dbm = dbm + _dot_tn(dg_b, cm_b)
        expand_t = v["expand_t"]
        dcs_loc = rs_cols - cs_rows.T
        last_row = _iota((T, 1), 0) == T - 1
        dcs_full_e = dcs_e + jnp.where(last_row, dlast_e, 0.0)
        dcs = _xdot(dcs_full_e, expand_t) + _xdot(dcs_loc, v["select_t"])
        triu = (_iota((T, T), 0) <= _iota((T, T), 1)).astype(BF16)
        da = _xdot_l(triu, dcs)
        ddt = da * v["a_neg"] + _xdot(dxdt * xs, expand_t)
        dxs = dxs + dxdt * v["dt_e"]
        ddtr = jnp.where(v["valid"], ddt * _sigmoid(v["dtr"]), 0.0)
        dx_ref[...] = dxs
        db_ref[...] = jnp.where(v["valid"], dbm, 0.0)
        dc_ref[...] = jnp.where(v["valid"], dcm, 0.0)
        ddt_ref[...] = ddtr
        dpar = jnp.concatenate([
            jnp.sum(ddtr, axis=0, keepdims=True),
            jnp.sum(da * v["dt"], axis=0, keepdims=True) * v["a_neg"],
            _xdot(dd_e, expand_t),
            jnp.zeros((5, 128), F32)], axis=0)

        @pl.when(step == 0)
        def _():
            dpar_ref[...] = dpar

        @pl.when(step > 0)
        def _():
            dpar_ref[...] += dpar

        dst_ref[...] = dhin

    return pl.pallas_call(
        body, name="ssd_bwd", grid=(SSM_GROUPS, n_chunks), in_specs=in_specs,
        out_specs=[pl.BlockSpec((T, GROUP_W), lambda g, c: (cidx(c), g)),
                   pl.BlockSpec((T, D_STATE), lambda g, c: (cidx(c), g)),
                   pl.BlockSpec((T, D_STATE), lambda g, c: (cidx(c), g)),
                   pl.BlockSpec((T, 128), lambda g, c: (cidx(c), g)),
                   pl.BlockSpec((None, 8, 128), lambda g, c: (g, 0, 0))],
        out_shape=[jax.ShapeDtypeStruct((n_rows, D_INNER), F32),
                   jax.ShapeDtypeStruct((n_rows, SSM_GROUPS * D_STATE), F32),
                   jax.ShapeDtypeStruct((n_rows, SSM_GROUPS * D_STATE), F32),
                   jax.ShapeDtypeStruct((n_rows, SSM_GROUPS * 128), F32),
                   jax.ShapeDtypeStruct((SSM_GROUPS, 8, 128), F32)],
        scratch_shapes=[pltpu.VMEM((D_STATE, GROUP_W), F32), pltpu.VMEM((T + 8, GROUP_W), F32)],
        compiler_params=_cparams(2),
    )(p, p, p, p, p, p, p, conv_w, conv_w, conv_w, conv_b, conv_b, conv_b, dt_bias, a_log, d_skip, hin, dy)


def _alibi_slope(h):
    return 2.0 ** (-8.0 * (h + 1) / ATTN_HEADS)


def _dup_half(x256, kvh):
    xb = x256[:, 128 * (kvh // 2):128 * (kvh // 2) + 128]
    rolled = pltpu.roll(xb, 64, 1)
    lane = _iota(xb.shape, 1)
    if kvh % 2 == 0:
        return jnp.where(lane < 64, xb, rolled)
    return jnp.where(lane < 64, rolled, xb)


def _attn_masks(c):
    qi = _iota((T, 3 * T), 0)
    jj = _iota((T, 3 * T), 1)
    blk = jnp.right_shift(jj, 7)
    j = jnp.bitwise_and(jj, T - 1)
    q_pos = c * T + qi - PAD
    k_pos = (c - 2 + blk) * T + j - PAD
    dist = q_pos - k_pos
    band = (blk > 0) & (dist >= 0) & (dist < T) & (k_pos >= N_META)
    meta = (blk == 0) & (j >= PAD) & (j - PAD <= q_pos)
    distf = jnp.where(blk > 0, dist, 0).astype(F32)
    return band | meta, distf


def _attn_scores(qp, k3, allowed, distf, h0):
    lane = _iota(qp.shape, 1)
    s = []
    for half, h in ((0, h0), (1, h0 + 1)):
        qh = jnp.where((lane < 64) if half == 0 else (lane >= 64), qp, 0.0).astype(BF16)
        sc = _dot_nt(qh, k3) - _alibi_slope(h) * distf
        s.append((qh, jnp.where(allowed, sc, NEG)))
    return s


def _attn_fwd(p, sinks, n_chunks):
    n_rows = n_chunks * T
    kb, vb = P_K // KV_W, P_V // KV_W

    def body(q_ref, kc_ref, kp_ref, km_ref, vc_ref, vp_ref, vm_ref, sink_ref, o_ref, lse_ref):
        c = pl.program_id(0)
        allowed, distf = _attn_masks(c)
        q = q_ref[...] * SCALE
        sinks_v = sink_ref[...]
        lane = _iota((T, 128), 1)
        lse_all = jnp.zeros((T, 128), F32)
        outs = []
        for kvh in range(KV_HEADS):
            k3 = jnp.concatenate([_dup_half(r[...], kvh) for r in (km_ref, kp_ref, kc_ref)], axis=0).astype(BF16)
            v3 = jnp.concatenate([_dup_half(r[...], kvh) for r in (vm_ref, vp_ref, vc_ref)], axis=0)
            v3bd = _block_diag_rows(v3).astype(BF16)
            for pr in range(2):
                h0 = 4 * kvh + 2 * pr
                blk = 2 * kvh + pr
                qp = q[:, 128 * blk:128 * blk + 128]
                probs = []
                for (_, sc), h in zip(_attn_scores(qp, k3, allowed, distf, h0), (h0, h0 + 1)):
                    sink = sinks_v[:, h:h + 1]
                    m = jnp.maximum(jnp.max(sc, axis=1, keepdims=True), sink)
                    e = jnp.exp(sc - m)
                    den = jnp.sum(e, axis=1, keepdims=True) + jnp.exp(sink - m)
                    probs.append(e / den)
                    lse_all = jnp.where(lane == h, m + jnp.log(den), lse_all)
                outs.append(_dot(jnp.concatenate(probs, axis=1).astype(BF16), v3bd))
        o_ref[...] = jnp.concatenate(outs, axis=1).astype(o_ref.dtype)
        lse_ref[...] = lse_all

    blk = lambda width, col: pl.BlockSpec((T, width), lambda c: (c, col))
    prev = lambda width, col: pl.BlockSpec((T, width), lambda c: (jnp.maximum(c - 1, 0), col))
    first = lambda width, col: pl.BlockSpec((T, width), lambda c: (0, col))
    return pl.pallas_call(
        body, name="attn_fwd", grid=(n_chunks,),
        in_specs=[blk(ATTN_W, P_Q // ATTN_W), blk(KV_W, kb), prev(KV_W, kb), first(KV_W, kb),
                  blk(KV_W, vb), prev(KV_W, vb), first(KV_W, vb), pl.BlockSpec((1, 128), lambda c: (0, 0))],
        out_specs=[pl.BlockSpec((T, ATTN_W), lambda c: (c, 0)), pl.BlockSpec((T, 128), lambda c: (c, 0))],
        out_shape=[jax.ShapeDtypeStruct((n_rows, ATTN_W), BF16), jax.ShapeDtypeStruct((n_rows, 128), F32)],
        compiler_params=_cparams(1),
    )(p, p, p, p, p, p, p, sinks)


def _block_diag_rows(x3):
    lane = _iota(x3.shape, 1)
    return jnp.concatenate([jnp.where(lane < 64, x3, 0.0), jnp.where(lane >= 64, x3, 0.0)], axis=0)


def _fold_halves(x):
    return x + pltpu.roll(x, 64, 1)


def _attn_bwd(p, sinks, ao, lse, dao, dp, n_chunks):
    kb, vb = P_K // KV_W, P_V // KV_W
    rc = lambda s: n_chunks - 1 - s

    def body(q_ref, kc_ref, kp_ref, km_ref, vc_ref, vp_ref, vm_ref, sink_ref, o_ref, lse_ref, do_ref, dp_in_ref,
             dqkv_ref, dsink_ref, kcar_ref, vcar_ref, kmeta_ref, vmeta_ref):
        step = pl.program_id(0)
        c = n_chunks - 1 - step

        @pl.when(step == 0)
        def _():
            for r in (kcar_ref, vcar_ref, kmeta_ref, vmeta_ref):
                r[...] = jnp.zeros_like(r)

        allowed, distf = _attn_masks(c)
        q = q_ref[...] * SCALE
        sinks_v = sink_ref[...]
        lse_v = lse_ref[...]
        ov = o_ref[...].astype(F32)
        dov = do_ref[...].astype(F32)
        lane = _iota((T, 128), 1)
        lane256 = _iota((3 * T, KV_W), 1)
        dsink = jnp.zeros((1, 128), F32)
        dk3_all = jnp.zeros((3 * T, KV_W), F32)
        dv3_all = jnp.zeros((3 * T, KV_W), F32)
        dqs = []
        for kvh in range(KV_HEADS):
            k3 = jnp.concatenate([_dup_half(r[...], kvh) for r in (km_ref, kp_ref, kc_ref)], axis=0).astype(BF16)
            v3 = jnp.concatenate([_dup_half(r[...], kvh) for r in (vm_ref, vp_ref, vc_ref)], axis=0).astype(BF16)
            dk3 = jnp.zeros((3 * T, 128), F32)
            dv3 = jnp.zeros((3 * T, 128), F32)
            for pr in range(2):
                h0 = 4 * kvh + 2 * pr
                blk = 2 * kvh + pr
                qp = q[:, 128 * blk:128 * blk + 128]
                dop = dov[:, 128 * blk:128 * blk + 128]
                prod = dop * ov[:, 128 * blk:128 * blk + 128]
                dq_pair = jnp.zeros((T, 128), F32)
                for half, ((qh, sc), h) in enumerate(zip(_attn_scores(qp, k3, allowed, distf, h0), (h0, h0 + 1))):
                    mine = (lane < 64) if half == 0 else (lane >= 64)
                    lse_h = lse_v[:, h:h + 1]
                    pm = jnp.exp(sc - lse_h)
                    doh = jnp.where(mine, dop, 0.0).astype(BF16)
                    delta = jnp.sum(jnp.where(mine, prod, 0.0), axis=1, keepdims=True)
                    dp = _dot_nt(doh, v3)
                    ds = (pm * (dp - delta)).astype(BF16)
                    p_sink = jnp.exp(sinks_v[:, h:h + 1] - lse_h)
                    dsink = jnp.where(_iota((1, 128), 1) == h, jnp.sum(-p_sink * delta, axis=0, keepdims=True), dsink)
                    dq_pair = jnp.where(mine, _dot(ds, k3), dq_pair)
                    dk3 = dk3 + _dot_tn(ds, qh)
                    dv3 = dv3 + _dot_tn(pm.astype(BF16), doh)
                dqs.append(dq_pair * SCALE)
            in_place = (lane256 >= 64 * kvh) & (lane256 < 64 * kvh + 64)
            wide = lambda x: jnp.concatenate([x, x], axis=1)
            dk3_all = jnp.where(in_place, wide(_fold_halves(dk3)), dk3_all)
            dv3_all = jnp.where(in_place, wide(_fold_halves(dv3)), dv3_all)
        dsink_all = dsink

        @pl.when(step == 0)
        def _():
            dsink_ref[...] = dsink_all

        @pl.when(step > 0)
        def _():
            dsink_ref[...] += dsink_all

        kmeta = kmeta_ref[...] + dk3_all[0:T]
        vmeta = vmeta_ref[...] + dv3_all[0:T]
        kmeta_ref[...] = kmeta
        vmeta_ref[...] = vmeta
        is_first = c == 0
        dk = jnp.where(is_first, kmeta, dk3_all[2 * T:3 * T] + kcar_ref[...])
        dv = jnp.where(is_first, vmeta, dv3_all[2 * T:3 * T] + vcar_ref[...])
        dqkv_ref[...] = jnp.concatenate(dqs + [dk, dv], axis=1).astype(dqkv_ref.dtype)
        kcar_ref[...] = dk3_all[T:2 * T]
        vcar_ref[...] = dv3_all[T:2 * T]

    blk = lambda width, col: pl.BlockSpec((T, width), lambda s: (rc(s), col))
    prev = lambda width, col: pl.BlockSpec((T, width), lambda s: (jnp.maximum(rc(s) - 1, 0), col))
    first = lambda width, col: pl.BlockSpec((T, width), lambda s: (0, col))
    return pl.pallas_call(
        body, name="attn_bwd", grid=(n_chunks,),
        in_specs=[blk(ATTN_W, P_Q // ATTN_W), blk(KV_W, kb), prev(KV_W, kb), first(KV_W, kb),
                  blk(KV_W, vb), prev(KV_W, vb), first(KV_W, vb), pl.BlockSpec((1, 128), lambda s: (0, 0)),
                  blk(ATTN_W, 0), blk(128, 0), blk(ATTN_W, 0), pl.BlockSpec(memory_space=pl.ANY)],
        out_specs=[blk(QKV_W, P_Q // QKV_W), pl.BlockSpec((1, 128), lambda s: (0, 0))],
        out_shape=[jax.ShapeDtypeStruct(dp.shape, dp.dtype), jax.ShapeDtypeStruct((1, 128), F32)],
        input_output_aliases={11: 0},
        scratch_shapes=[pltpu.VMEM((T, KV_W), F32)] * 4,
        compiler_params=_cparams(1),
    )(p, p, p, p, p, p, p, sinks, ao, lse, dao, dp)


def _pad_lanes(v, width=128):
    return jnp.pad(v, ((0, 0), (0, width - v.shape[1])))


def _local_step(x, head, tgt, w):
    n_tok = x.shape[0]
    n_rows = n_tok + T
    n_chunks = n_rows // T
    tm = _row_tile(n_rows, 384)
    dt_bias, a_log, d_skip = (_pad_lanes(w[k]) for k in ("ssm_dt_bias", "ssm_a_log", "ssm_d_skip"))
    sinks = _pad_lanes(w["attn_sinks"])
    x_in = [(x, D_MODEL, 0, "prev"), (head, D_MODEL, 0, "first")]

    def h0_tile(r0, xt, hd):
        return jnp.where(r0 < T, hd, xt)

    n1, = _rowwise("norm_pre_mix", lambda r0, xt, hd, wn: [_rms(h0_tile(r0, xt, hd), wn)], n_rows, T,
                   x_in, [w["norm_pre_mix"]], [(D_MODEL, BF16)], [])
    p = _matmul("in_proj", n1, w["w_cat"], "nn", F32)
    y_ssd, hin = _ssd_fwd(p, w["ssm_conv_w"], w["ssm_conv_b"], dt_bias, a_log, d_skip, n_chunks)
    ao, lse = _attn_fwd(p, sinks, n_chunks)

    def gate_norm(r0, y, z, wn):
        return [_rms(y * _silu(z), wn)]

    yn, = _rowwise("ssm_gate_norm", gate_norm, n_rows, tm, [(y_ssd, D_INNER, 0), (p, D_INNER, P_Z // D_INNER)],
                   [w["ssm_norm"]], [(D_INNER, BF16)], [])
    y_ssm = _matmul("ssm_out", yn, w["w_ssm_out"], "nn", F32)
    y_attn = _matmul("attn_out", ao, w["w_attn_out"], "nn", F32)

    def mix_gate(r0, ys, ya, gs, ga):
        return [_sigmoid(gs) * ys + _sigmoid(ga) * ya]

    gate_ins = [(p, D_MODEL, P_GATE // D_MODEL), (p, D_MODEL, P_GATE // D_MODEL + 1)]
    mixed, = _rowwise("mix_gate", mix_gate, n_rows, tm, [(y_ssm, D_MODEL, 0), (y_attn, D_MODEL, 0)] + gate_ins,
                      [], [(D_MODEL, BF16)], [])
    mix = _matmul("mix_out", mixed, w["w_mix_out"], "nn", F32)

    def post_mix(r0, mx, xt, hd, w_post, w_pre):
        h1 = jnp.where(_valid_rows(r0, mx.shape[0], PAD), h0_tile(r0, xt, hd) + _rms(mx, w_post), 0.0)
        return [h1, _rms(h1, w_pre)]

    h1, n2 = _rowwise("post_mix", post_mix, n_rows, T, [(mix, D_MODEL, 0)] + x_in,
                      [w["norm_post_mix"], w["norm_pre_ffn"]], [(D_MODEL, F32), (D_MODEL, BF16)], [])
    u_raw = _matmul("ffn_up", n2, w["w_ffn_up"], "nn", F32)
    f = _ffn_act("ffn_act", u_raw, w["ffn_conv_w"], w["ffn_conv_b"], n_rows)
    ffn = _matmul("ffn_down", f, w["w_ffn_down"], "nn", F32)

    def final(r0, fo, h, t, w_post):
        real = r0 >= T
        err = jnp.where(real, h + _rms(fo, w_post) - t, 0.0)
        dy = err * (1.0 / D_MODEL)
        dffn, dw = _rms_bwd(dy, fo, w_post)
        return [dffn, dy, jnp.sum(err * err, axis=0, keepdims=True), dw]

    dffn, dh2, loss_cols, g_norm_post_ffn = _rowwise(
        "loss_head", final, n_rows, T, [(ffn, D_MODEL, 0), (h1, D_MODEL, 0), (tgt, D_MODEL, 0, "prev")],
        [w["norm_post_ffn"]], [(D_MODEL, BF16), (D_MODEL, F32)], [D_MODEL, D_MODEL])

    g = {"norm_post_ffn": g_norm_post_ffn}
    g["w_ffn_down"] = _matmul("ffn_down_dw", f, dffn, "tn", F32)
    df = _matmul("ffn_down_dx", dffn, w["w_ffn_down"], "nt", F32)
    du_raw, dconv = _conv_bwd("ffn_act_bwd", u_raw, 0, [df], [(0, c0) for c0 in range(0, FFN_DIM, CONV_LANES)],
                              w["ffn_conv_w"], w["ffn_conv_b"], n_rows, True)
    g["ffn_conv_w"], g["ffn_conv_b"] = dconv[0:3], dconv[3:4]
    g["w_ffn_up"] = _matmul("ffn_up_dw", n2, du_raw, "tn", F32)
    dn2 = _matmul("ffn_up_dx", du_raw, w["w_ffn_up"], "nt", F32)

    def post_mix_bwd(r0, dn, d2, h, mx, w_pre, w_post):
        dx, dw_pre = _rms_bwd(dn, h, w_pre)
        dh1 = jnp.where(_valid_rows(r0, dn.shape[0], PAD), dx + d2, 0.0)
        dmix, dw_post = _rms_bwd(dh1, mx, w_post)
        return [dh1, dmix, dw_pre, dw_post]

    dh1, dmix, g["norm_pre_ffn"], g["norm_post_mix"] = _rowwise(
        "post_mix_bwd", post_mix_bwd, n_rows, tm,
        [(dn2, D_MODEL, 0), (dh2, D_MODEL, 0), (h1, D_MODEL, 0), (mix, D_MODEL, 0)],
        [w["norm_pre_ffn"], w["norm_post_mix"]], [(D_MODEL, F32), (D_MODEL, BF16)], [D_MODEL, D_MODEL])
    g["w_mix_out"] = _matmul("mix_out_dw", mixed, dmix, "tn", F32)
    dmixed = _matmul("mix_out_dx", dmix, w["w_mix_out"], "nt", F32)

    def mix_gate_bwd(r0, dm, ys, ya, gs, ga):
        ss, sa = _sigmoid(gs), _sigmoid(ga)
        dgate = jnp.concatenate([dm * ys * ss * (1.0 - ss), dm * ya * sa * (1.0 - sa)], axis=1)
        return [dm * ss, dm * sa, dgate]

    dys, dya, dp = _rowwise(
        "mix_gate_bwd", mix_gate_bwd, n_rows, tm,
        [(dmixed, D_MODEL, 0), (y_ssm, D_MODEL, 0), (y_attn, D_MODEL, 0)] + gate_ins,
        [], [(D_MODEL, BF16), (D_MODEL, BF16), (2 * D_MODEL, BF16, "new", P_W, P_GATE // (2 * D_MODEL))], [])
    g["w_ssm_out"] = _matmul("ssm_out_dw", yn, dys, "tn", F32)
    dyn = _matmul("ssm_out_dx", dys, w["w_ssm_out"], "nt", F32)
    g["w_attn_out"] = _matmul("attn_out_dw", ao, dya, "tn", F32)
    dao = _matmul("attn_out_dx", dya, w["w_attn_out"], "nt", BF16)

    def gate_norm_bwd(r0, dn, y, z, wn):
        sz = _silu(z)
        dyz, dw = _rms_bwd(dn, y * sz, wn)
        live = _valid_rows(r0, dn.shape[0], PAD)
        return [jnp.where(live, dyz * sz, 0.0), jnp.where(live, dyz * y * _dsilu(z), 0.0), dw]

    dy_ssd, dp, g["ssm_norm"] = _rowwise(
        "ssm_gate_norm_bwd", gate_norm_bwd, n_rows, tm,
        [(dyn, D_INNER, 0), (y_ssd, D_INNER, 0), (p, D_INNER, P_Z // D_INNER)],
        [w["ssm_norm"]], [(D_INNER, F32), (D_INNER, BF16, "into", dp, P_Z // D_INNER)], [D_INNER])
    dp, dsink = _attn_bwd(p, sinks, ao, lse, dao, dp, n_chunks)
    g["attn_sinks"] = dsink[:, 0:ATTN_HEADS]
    dxs, dbm, dcm, ddt_parts, dpar = _ssd_bwd(p, w["ssm_conv_w"], w["ssm_conv_b"], dt_bias, a_log, d_skip, hin,
                                              dy_ssd, n_chunks)
    dpar = jnp.sum(dpar, axis=0)
    g["ssm_dt_bias"], g["ssm_a_log"], g["ssm_d_skip"] = (dpar[i:i + 1, 0:SSM_HEADS] for i in range(3))

    def dt_grad(r0, parts):
        tot = parts[:, 0:128] + parts[:, 128:256] + parts[:, 256:384] + parts[:, 384:512]
        return [jnp.concatenate([tot, jnp.zeros((parts.shape[0], P_Z - P_DT - 128), F32)], axis=1)]

    dt_w = P_Z - P_DT
    dp, = _rowwise("dt_grad", dt_grad, n_rows, tm, [(ddt_parts, SSM_GROUPS * 128, 0)], [],
                   [(dt_w, BF16, "into", dp, P_DT // dt_w)], [])
    x_chunks = [(src, c0) for src, arr in enumerate((dxs, dbm, dcm)) for c0 in range(0, arr.shape[1], CONV_LANES)]
    dp, dconv = _conv_bwd("ssm_conv_bwd", p, P_XBC // CONV_DIM, [dxs, dbm, dcm], x_chunks,
                          w["ssm_conv_w"], w["ssm_conv_b"], n_rows, False, into=dp, into_blk=P_XBC // CONV_DIM)
    g["ssm_conv_w"], g["ssm_conv_b"] = dconv[0:4], dconv[4:5]
    g["w_cat"] = _matmul("in_proj_dw", n1, dp, "tn", F32)
    dn1 = _matmul("in_proj_dx", dp, w["w_cat"], "nt", F32)

    def pre_mix_bwd(r0, dn, d1, xt, hd, wn):
        dx, dw = _rms_bwd(dn, h0_tile(r0, xt, hd), wn)
        dh0 = jnp.where(_valid_rows(r0, dn.shape[0], PAD), dx + d1, 0.0)
        return [dh0, dh0, dw]

    dx_out, dhead, g["norm_pre_mix"] = _rowwise(
        "pre_mix_bwd", pre_mix_bwd, n_rows, T, [(dn1, D_MODEL, 0), (dh1, D_MODEL, 0)] + x_in,
        [w["norm_pre_mix"]], [(D_MODEL, F32, "prev", n_tok), (D_MODEL, F32, "first")], [D_MODEL])
    return jnp.sum(loss_cols), dx_out, dhead, g


_IN_SECTIONS = [((5152, 6176), P_Q), ((6176, 6432), P_K), ((6432, 6688), P_V), ((5120, 5152), P_DT),
                ((0, 2048), P_Z), ((6688, 8736), P_GATE), ((2048, 5120), P_XBC)]


def _to_cat(w_in):
    parts, at = [], 0
    for (a, b), off in _IN_SECTIONS:
        if off > at:
            parts.append(jnp.zeros((w_in.shape[0], off - at), w_in.dtype))
        parts.append(w_in[:, a:b])
        at = off + (b - a)
    return jnp.concatenate(parts, axis=1)


def _from_cat(g_cat):
    pieces = {a: g_cat[:, off:off + (b - a)] for (a, b), off in _IN_SECTIONS}
    return jnp.concatenate([pieces[a] for a in sorted(pieces)], axis=1)


LANES = 1024
_BIG = [("w_in", 1024, 2184, "chip"), ("w_ssm_out", 512, 1024, "row"), ("w_attn_out", 256, 1024, "row"),
        ("w_mix_out", 256, 1024, "row"), ("w_ffn_up", 1024, 1408, "col"), ("w_ffn_down", 704, 1024, "row"),
        ("small", 32, LANES, "chip")]
_SMALL_SHARDED = [("ssm_conv_w", (4, 768), 1), ("ffn_conv_w", (3, 1408), 1), ("meta_tokens", (16, 256), 1)]
_REPLICATED = [("norm_pre_mix", 1024), ("ssm_conv_b", 3072), ("ssm_dt_bias", 32), ("ssm_a_log", 32),
               ("ssm_d_skip", 32), ("ssm_norm", 2048), ("attn_sinks", 16), ("norm_post_mix", 1024),
               ("norm_pre_ffn", 1024), ("ffn_conv_b", 5632), ("norm_post_ffn", 1024)]
SMALL_ROWS = 16
WEIGHT_ORDER = ["meta_tokens", "norm_pre_mix", "w_in", "ssm_conv_w", "ssm_conv_b", "ssm_dt_bias", "ssm_a_log",
                "ssm_d_skip", "ssm_norm", "w_ssm_out", "attn_sinks", "w_attn_out", "w_mix_out", "norm_post_mix",
                "norm_pre_ffn", "w_ffn_up", "ffn_conv_w", "ffn_conv_b", "w_ffn_down", "norm_post_ffn"]


def _flatten(parts, rows):
    flat = jnp.concatenate([a.reshape(-1) for a in parts])
    return jnp.pad(flat, (0, rows * LANES - flat.shape[0])).reshape(rows, LANES)


def _unflatten(flat, shapes):
    flat = flat.reshape(-1)
    out, off = [], 0
    for shp in shapes:
        n = math.prod(shp)
        out.append(flat[off:off + n].reshape(shp))
        off += n
    return out


def _shard_of(full, chip, shape, axis):
    return lax.slice_in_dim(full, chip * shape[axis], (chip + 1) * shape[axis], axis=axis)


def _full_shape(r, c, layout):
    return {"row": (4 * r, c), "col": (r, 4 * c), "chip": (4, r, c)}[layout]


def _shard_view(ref, r, c, layout, chip):
    if layout == "row":
        return ref.at[pl.ds(pl.multiple_of(chip * r, 16), r), :]
    if layout == "col":
        return ref.at[:, pl.ds(pl.multiple_of(chip * c, 128), c)]
    return ref.at[chip]


def _half_view(ref, r, c, layout, chip, half):
    hr = r // 2
    if layout == "row":
        return ref.at[pl.ds(pl.multiple_of(chip * r + half * hr, 16), hr), :]
    r0 = pl.multiple_of(half * hr, 16)
    if layout == "col":
        return ref.at[pl.ds(r0, hr), pl.ds(pl.multiple_of(chip * c, 128), c)]
    return ref.at[chip, pl.ds(r0, hr), :]


def _mesh_pos():
    return lax.axis_index("x"), lax.axis_index("y"), lax.axis_index("c")


def _other_chips(x, y):
    return [(1 - x, y), (x, 1 - y), (1 - x, 1 - y)]


def _chip_index(x, y):
    return 2 * x + y


ANY = pl.BlockSpec(memory_space=pl.ANY)


def _run_exchange(name, make_copies, n_copies, ins, out_shapes):
    n_in = len(ins)
    n_out = len(out_shapes)

    def body(*refs):
        in_refs, out_refs = refs[:n_in], refs[n_in:n_in + n_out]
        send_sems, recv_sems = refs[n_in + n_out:]
        copies = [pltpu.make_async_remote_copy(src_ref=s, dst_ref=d, send_sem=send_sems.at[i], recv_sem=recv_sems.at[i],
                                               device_id=dev, device_id_type=MESH)
                  for i, (s, d, dev) in enumerate(make_copies(in_refs, out_refs))]
        assert len(copies) == n_copies
        for cp in copies:
            cp.start()
        for cp in copies:
            cp.wait()

    return pl.pallas_call(
        body, name=name, in_specs=[ANY] * n_in, out_specs=[ANY] * n_out, out_shape=out_shapes,
        scratch_shapes=[pltpu.SemaphoreType.DMA((n_copies,)), pltpu.SemaphoreType.DMA((n_copies,))],
        compiler_params=pltpu.CompilerParams(has_side_effects=True),
    )(*ins)


def _gather_weights(shards):
    n = len(_BIG)

    def body(*refs):
        ins, outs = refs[:n], refs[n:2 * n]
        send_sems, recv_sems, local_sems = refs[2 * n:]
        x, y, c = _mesh_pos()
        j = _chip_index(x, y)
        sibling = (x, y, 1 - c)
        chips = _other_chips(x, y)
        idx = [_chip_index(*ch) for ch in chips]

        def remote(k, src, dst, dev):
            return pltpu.make_async_remote_copy(src_ref=src, dst_ref=dst, send_sem=send_sems.at[k],
                                                recv_sem=recv_sems.at[k], device_id=dev, device_id_type=MESH)

        own = [pltpu.make_async_copy(ins[a], _shard_view(outs[a], r, cc, lay, j), local_sems.at[a])
               for a, (_, r, cc, lay) in enumerate(_BIG)]
        for cp in own:
            cp.start()
        first, passed = [], []
        for a, (_, r, cc, lay) in enumerate(_BIG):
            mine = ins[a].at[pl.ds(pl.multiple_of(c * (r // 2), 16), r // 2), :]
            for k, ch in enumerate(chips):
                first.append(remote(6 * a + k, mine, _half_view(outs[a], r, cc, lay, j, c), (*ch, c)))
                landed = _half_view(outs[a], r, cc, lay, idx[k], c)
                passed.append(remote(6 * a + 3 + k, landed, landed, sibling))
        for cp in first:
            cp.start()
        for a, (_, r, cc, lay) in enumerate(_BIG):
            for k in range(3):
                landed = _half_view(outs[a], r, cc, lay, idx[k], c)
                remote(6 * a + k, landed, landed, sibling).wait_recv()
                passed[3 * a + k].start()
        for a, (_, r, cc, lay) in enumerate(_BIG):
            for k in range(3):
                theirs = _half_view(outs[a], r, cc, lay, idx[k], 1 - c)
                remote(6 * a + 3 + k, theirs, theirs, sibling).wait_recv()
        for cp in first + passed:
            cp.wait_send()
        for cp in own:
            cp.wait()

    return pl.pallas_call(
        body, name="gather_weights", in_specs=[ANY] * n, out_specs=[ANY] * n,
        out_shape=[jax.ShapeDtypeStruct(_full_shape(r, cc, lay), s.dtype) for s, (_, r, cc, lay) in zip(shards, _BIG)],
        scratch_shapes=[pltpu.SemaphoreType.DMA((6 * n,)), pltpu.SemaphoreType.DMA((6 * n,)), pltpu.SemaphoreType.DMA((n,))],
        compiler_params=pltpu.CompilerParams(has_side_effects=True),
    )(*shards)


def _grad_pair_exchange(grads, rep):
    n = len(_BIG)

    def make(in_refs, out_refs):
        x, y, c = _mesh_pos()
        sibling = (x, y, 1 - c)
        copies = [(_half_view(in_refs[a], r, cc, lay, i, 1 - c), out_refs[a].at[i], sibling)
                  for a, (_, r, cc, lay) in enumerate(_BIG) for i in range(4)]
        return copies + [(in_refs[n], out_refs[n], sibling)]

    shapes = [jax.ShapeDtypeStruct((4, r // 2, cc), F32) for _, r, cc, _ in _BIG]
    return _run_exchange("grad_pair_exchange", make, 4 * n + 1, list(grads) + [rep],
                         shapes + [jax.ShapeDtypeStruct(rep.shape, F32)])


def _grad_chip_exchange(psends, prep):
    n = len(psends)

    def make(in_refs, out_refs):
        x, y, c = _mesh_pos()
        chips = _other_chips(x, y)
        copies = [(in_refs[a].at[_chip_index(*ch)], out_refs[a].at[k], (*ch, c))
                  for a in range(n) for k, ch in enumerate(chips)]
        return copies + [(in_refs[n], out_refs[n].at[k], (*ch, c)) for k, ch in enumerate(chips)]

    shapes = [jax.ShapeDtypeStruct((3,) + p.shape[1:], p.dtype) for p in psends]
    return _run_exchange("grad_chip_exchange", make, 3 * n + 3, list(psends) + [prep],
                         shapes + [jax.ShapeDtypeStruct((3,) + prep.shape, prep.dtype)])


def _grad_half_share(halves):
    def make(in_refs, out_refs):
        x, y, c = _mesh_pos()
        return [(r, o, (x, y, 1 - c)) for r, o in zip(in_refs, out_refs)]

    return _run_exchange("grad_half_share", make, len(halves), list(halves),
                         [jax.ShapeDtypeStruct(h.shape, h.dtype) for h in halves])


SUM_ROWS = 256
ADAM_ROWS = 128


def _pair_sum(name, grad, recv, ids, r, c, layout):
    hr = r // 2
    tr = _row_tile(hr, SUM_ROWS)
    nb = hr // tr

    def body(ids_ref, g_ref, r_ref, send_ref, own_ref):
        s = g_ref[...] + r_ref[...]
        send_ref[...] = s.astype(send_ref.dtype)

        @pl.when(pl.program_id(1) == ids_ref[1])
        def _():
            own_ref[...] = s

    if layout == "row":
        g_spec = pl.BlockSpec((tr, c), lambda t, j, ids_ref: ((j * r + ids_ref[0] * hr) // tr + t, 0))
    elif layout == "col":
        g_spec = pl.BlockSpec((tr, c), lambda t, j, ids_ref: (ids_ref[0] * nb + t, j))
    else:
        g_spec = pl.BlockSpec((None, tr, c), lambda t, j, ids_ref: (j, ids_ref[0] * nb + t, 0))
    grid_spec = pltpu.PrefetchScalarGridSpec(
        num_scalar_prefetch=1, grid=(nb, 4),
        in_specs=[g_spec, pl.BlockSpec((None, tr, c), lambda t, j, ids_ref: (j, t, 0))],
        out_specs=[pl.BlockSpec((None, tr, c), lambda t, j, ids_ref: (j, t, 0)),
                   pl.BlockSpec((tr, c), lambda t, j, ids_ref: (t, 0))])
    return pl.pallas_call(
        body, name=name, grid_spec=grid_spec,
        out_shape=[jax.ShapeDtypeStruct((4, hr, c), BF16), jax.ShapeDtypeStruct((hr, c), F32)],
        compiler_params=_cparams(2),
    )(ids, grad, recv)


def _chip_sum(name, own, recv):
    hr, c = own.shape
    tr = _row_tile(hr, SUM_ROWS)

    def body(o_ref, r_ref, out_ref):
        out_ref[...] = ((o_ref[...] + r_ref[0].astype(F32)) + r_ref[1].astype(F32)) + r_ref[2].astype(F32)

    return pl.pallas_call(
        body, name=name, grid=(hr // tr,),
        in_specs=[pl.BlockSpec((tr, c), lambda i: (i, 0)), pl.BlockSpec((3, tr, c), lambda i: (0, i, 0))],
        out_specs=pl.BlockSpec((tr, c), lambda i: (i, 0)),
        out_shape=jax.ShapeDtypeStruct((hr, c), F32), compiler_params=_cparams(1),
    )(own, recv)


def _chip_sum_small(own, recv, ids):
    def body(ids_ref, o_ref, r_ref, out_ref):
        j = ids_ref[1]
        total = None
        for i in range(4):
            m = jnp.bitwise_xor(i, j)
            term = jnp.where(m == 0, o_ref[...], jnp.where(m == 2, r_ref[0], jnp.where(m == 1, r_ref[1], r_ref[2])))
            total = term if total is None else total + term
        out_ref[...] = total

    grid_spec = pltpu.PrefetchScalarGridSpec(
        num_scalar_prefetch=1, grid=(1,),
        in_specs=[pl.BlockSpec(own.shape, lambda i, ids_ref: (0, 0)), pl.BlockSpec(recv.shape, lambda i, ids_ref: (0, 0, 0))],
        out_specs=pl.BlockSpec(own.shape, lambda i, ids_ref: (0, 0)))
    return pl.pallas_call(body, name="chip_sum_small", grid_spec=grid_spec,
                          out_shape=jax.ShapeDtypeStruct(own.shape, F32), compiler_params=_cparams(1))(ids, own, recv)


def _adamw(name, w, m, v, mine, theirs, ids):
    rows, cols = w.shape
    half = rows // 2
    tr = _row_tile(half, ADAM_ROWS, unit=8)
    nb = half // tr
    c1 = 1.0 / (1.0 - ADAM_B1 ** ADAM_STEP)
    c2 = 1.0 / (1.0 - ADAM_B2 ** ADAM_STEP)

    def body(ids_ref, w_ref, m_ref, v_ref, mine_ref, theirs_ref, g_out, d_out, m_out, v_out):
        g = jnp.where(pl.program_id(0) == ids_ref[0], mine_ref[...], theirs_ref[...])
        m_new = ADAM_B1 * m_ref[...] + (1.0 - ADAM_B1) * g
        v_new = ADAM_B2 * v_ref[...] + (1.0 - ADAM_B2) * (g * g)
        d_out[...] = -ADAM_LR * ((m_new * c1) / (jnp.sqrt(v_new * c2) + ADAM_EPS) + ADAM_WD * w_ref[...])
        g_out[...] = g
        m_out[...] = m_new
        v_out[...] = v_new

    full = pl.BlockSpec((tr, cols), lambda h, i, ids_ref: (h * nb + i, 0))
    part = pl.BlockSpec((tr, cols), lambda h, i, ids_ref: (i, 0))
    grid_spec = pltpu.PrefetchScalarGridSpec(num_scalar_prefetch=1, grid=(2, nb),
                                             in_specs=[full, full, full, part, part], out_specs=[full] * 4)
    return pl.pallas_call(
        body, name=name, grid_spec=grid_spec,
        out_shape=[jax.ShapeDtypeStruct((rows, cols), F32)] * 4, compiler_params=_cparams(2),
    )(ids, w, m, v, mine, theirs)


def _small_shard(parts):
    return _flatten(parts, _BIG[-1][1])


def kernel(x, meta_tokens, norm_pre_mix, w_in, ssm_conv_w, ssm_conv_b, ssm_dt_bias, ssm_a_log, ssm_d_skip, ssm_norm, w_ssm_out, attn_sinks, w_attn_out, w_mix_out, norm_post_mix, norm_pre_ffn, w_ffn_up, ffn_conv_w, ffn_conv_b, w_ffn_down, norm_post_ffn, loss_target, m_meta_tokens, m_norm_pre_mix, m_w_in, m_ssm_conv_w, m_ssm_conv_b, m_ssm_dt_bias, m_ssm_a_log, m_ssm_d_skip, m_ssm_norm, m_w_ssm_out, m_attn_sinks, m_w_attn_out, m_w_mix_out, m_norm_post_mix, m_norm_pre_ffn, m_w_ffn_up, m_ffn_conv_w, m_ffn_conv_b, m_w_ffn_down, m_norm_post_ffn, v_meta_tokens, v_norm_pre_mix, v_w_in, v_ssm_conv_w, v_ssm_conv_b, v_ssm_dt_bias, v_ssm_a_log, v_ssm_d_skip, v_ssm_norm, v_w_ssm_out, v_attn_sinks, v_w_attn_out, v_w_mix_out, v_norm_post_mix, v_norm_pre_ffn, v_w_ffn_up, v_ffn_conv_w, v_ffn_conv_b, v_w_ffn_down, v_norm_post_ffn):
    args = dict(locals())
    squeeze = lambda a: a.reshape(a.shape[-2:])
    wts = {n: squeeze(args[n]) for n in WEIGHT_ORDER}
    mom = {n: squeeze(args["m_" + n]) for n in WEIGHT_ORDER}
    var = {n: squeeze(args["v_" + n]) for n in WEIGHT_ORDER}
    x_i, y_i, c_i = _mesh_pos()
    ids = jnp.stack([c_i, _chip_index(x_i, y_i)]).astype(jnp.int32)
    big_names = [n for n, _, _, _ in _BIG[:-1]]
    small_names = [n for n, _, _ in _SMALL_SHARDED]
    rep_names = [n for n, _ in _REPLICATED]

    w_small = _small_shard([wts[n] for n in small_names])
    gathered = _gather_weights([wts[n].astype(BF16) for n in big_names] + [w_small])
    w = {n: wts[n] for n in rep_names}
    w.update(zip(big_names[1:], gathered[1:-1]))
    w["w_cat"] = _to_cat(jnp.transpose(gathered[0], (1, 0, 2)).reshape(D_MODEL, N_IN))
    small_all = [_unflatten(gathered[-1][i], [shp for _, shp, _ in _SMALL_SHARDED]) for i in range(4)]
    for k, (n, _, axis) in enumerate(_SMALL_SHARDED):
        w[n] = jnp.concatenate([small_all[i][k] for i in range(4)], axis=axis)

    head = jnp.concatenate([jnp.zeros((PAD, D_MODEL), F32), w["meta_tokens"]], axis=0)
    loss_sum, dx, dhead, g = _local_step(x[0], head, loss_target[0], w)
    loss = lax.psum(loss_sum * (0.5 / D_MODEL), ("x", "y", "c"))
    g["meta_tokens"] = dhead[PAD:]
    g_in = _from_cat(g.pop("w_cat")).reshape(D_MODEL, 4, N_IN // 4)
    g["w_in"] = jnp.transpose(g_in, (1, 0, 2))

    grads = [g[n] for n in big_names]
    grads.append(jnp.stack([_small_shard([_shard_of(g[n], i, shp, ax) for n, shp, ax in _SMALL_SHARDED])
                            for i in range(4)]))
    g_rep = _flatten([g[n] for n in rep_names], SMALL_ROWS)
    *recv, recv_rep = _grad_pair_exchange(grads, g_rep)
    sums = [_pair_sum("pair_sum_" + n, gr, rv, ids, r, c, lay) for gr, rv, (n, r, c, lay) in zip(grads, recv, _BIG)]
    p_rep, = _rowwise("pair_sum_replicated", lambda r0, a, b: [a + b], SMALL_ROWS, SMALL_ROWS,
                      [(g_rep, LANES, 0), (recv_rep, LANES, 0)], [], [(LANES, F32)], [])
    *recv2, recv2_rep = _grad_chip_exchange([s[0] for s in sums], p_rep)
    halves = [_chip_sum("chip_sum_" + n, s[1], rv) for s, rv, (n, _, _, _) in zip(sums, recv2, _BIG)]
    g_rep_tot = _chip_sum_small(p_rep, recv2_rep, ids)
    theirs = _grad_half_share(halves)

    stacks = {"w": wts, "m": mom, "v": var}
    shard_in = {k: [d[n] for n in big_names] + [_small_shard([d[n] for n in small_names])] for k, d in stacks.items()}
    results = {}
    for a, (n, _, _, _) in enumerate(_BIG):
        res = _adamw("adamw_" + n, shard_in["w"][a], shard_in["m"][a], shard_in["v"][a], halves[a], theirs[a], ids)
        if n == "small":
            for kind in range(4):
                parts = _unflatten(res[kind], [shp for _, shp, _ in _SMALL_SHARDED])
                results.update({(kind, sn): parts[k] for k, sn in enumerate(small_names)})
        else:
            results.update({(kind, n): res[kind] for kind in range(4)})
    rep_in = {k: _flatten([d[n] for n in rep_names], SMALL_ROWS) for k, d in stacks.items()}
    ids_lo = ids * jnp.array([0, 1], jnp.int32)
    res = _adamw("adamw_replicated", rep_in["w"], rep_in["m"], rep_in["v"],
                 g_rep_tot[0:SMALL_ROWS // 2], g_rep_tot[SMALL_ROWS // 2:], ids_lo)
    for kind in range(4):
        parts = _unflatten(res[kind], [(1, width) for _, width in _REPLICATED])
        results.update({(kind, rn): parts[k] for k, rn in enumerate(rep_names)})
    outs = [results[kind, n].reshape(args[n].shape) for kind in range(4) for n in WEIGHT_ORDER]
    return (loss, dx[None], *outs)
```

```python
import math
from typing import Any, Callable, NamedTuple, Sequence

import jax
import jax.numpy as jnp
from jax import lax
from jax.experimental import pallas as pl
from jax.experimental.pallas import tpu as pltpu

F32 = jnp.float32
BF16 = jnp.bfloat16

D_MODEL = 1024
N_META = 16
T = 128
PAD = T - N_META
D_INNER = 2048
SSM_HEADS = 32
HEAD_P = 64
SSM_GROUPS = 4
GROUP_W = D_INNER // SSM_GROUPS
D_STATE = 128
CONV_DIM = D_INNER + 2 * SSM_GROUPS * D_STATE
ATTN_HEADS = 16
KV_HEADS = 4
ATTN_W = 1024
KV_W = 256
FFN_DIM = 2816
N_IN = 8736
EPS = 1e-6
NEG = -1e30
SCALE = 0.125

P_Q, P_K, P_V, P_DT, P_Z, P_GATE, P_XBC = 0, 1024, 1280, 1536, 2048, 4096, 6144
QKV_W = 1536
P_W = 9216

ADAM_LR, ADAM_B1, ADAM_B2, ADAM_EPS, ADAM_WD, ADAM_STEP = 0.001, 0.9, 0.999, 1e-08, 0.01, 10

VMEM_BUDGET = 40 * 1024 * 1024
VMEM_LIMIT = 56 * 1024 * 1024
MESH = pl.DeviceIdType.MESH
ANY = pl.BlockSpec(memory_space=pl.ANY)


def _cparams(n_axes, **kw):
    return pltpu.CompilerParams(dimension_semantics=("arbitrary",) * n_axes, vmem_limit_bytes=VMEM_LIMIT, **kw)


class _Exchange(NamedTuple):
    ins: Sequence[Any]
    out_shapes: Sequence[Any]
    make_copies: Callable
    n_copies: int
    aliases: dict = {}


def _call(body, name, grid, in_specs, out_specs, out_shape, operands, scratch_shapes=(), aliases=None, bg=None):
    aliases = dict(aliases or {})
    if bg is None:
        return pl.pallas_call(body, name=name, grid=grid, in_specs=in_specs, out_specs=out_specs, out_shape=out_shape,
                              scratch_shapes=list(scratch_shapes), input_output_aliases=aliases,
                              compiler_params=_cparams(len(grid)))(*operands)
    n_in, n_out, n_scr = len(in_specs), len(out_specs), len(scratch_shapes)
    nb_in, nb_out = len(bg.ins), len(bg.out_shapes)

    def hosted(*refs):
        ins, bg_ins = refs[:n_in], refs[n_in:n_in + nb_in]
        outs = refs[n_in + nb_in:n_in + nb_in + n_out]
        bg_outs = refs[n_in + nb_in + n_out:n_in + nb_in + n_out + nb_out]
        scratch = refs[n_in + nb_in + n_out + nb_out:n_in + nb_in + n_out + nb_out + n_scr]
        send_sems, recv_sems = refs[-2:]
        pids = [pl.program_id(a) for a in range(len(grid))]
        first, last = pids[0] == 0, pids[0] == grid[0] - 1
        for p, g in zip(pids[1:], grid[1:]):
            first, last = first & (p == 0), last & (p == g - 1)
        copies = []
        for k, (src, dst, peer) in enumerate(bg.make_copies(bg_ins, bg_outs)):
            if peer is None:
                copies.append(pltpu.make_async_copy(src, dst, send_sems.at[k]))
            else:
                copies.append(pltpu.make_async_remote_copy(src_ref=src, dst_ref=dst, send_sem=send_sems.at[k],
                                                           recv_sem=recv_sems.at[k], device_id=peer, device_id_type=MESH))
        assert len(copies) == bg.n_copies

        @pl.when(first)
        def _():
            for cp in copies:
                cp.start()

        body(*ins, *outs, *scratch)

        @pl.when(last)
        def _():
            for cp in copies:
                cp.wait()

    aliases = {(k if k < n_in else k + nb_in): v for k, v in aliases.items()}
    aliases.update({n_in + k: n_out + v for k, v in bg.aliases.items()})
    res = pl.pallas_call(
        hosted, name=name, grid=grid, in_specs=list(in_specs) + [ANY] * nb_in, out_specs=list(out_specs) + [ANY] * nb_out,
        out_shape=list(out_shape) + list(bg.out_shapes), input_output_aliases=aliases,
        scratch_shapes=list(scratch_shapes) + [pltpu.SemaphoreType.DMA((bg.n_copies,))] * 2,
        compiler_params=_cparams(len(grid), has_side_effects=True))(*operands, *bg.ins)
    return res[:n_out], res[n_out:]


def _sigmoid(x):
    return 1.0 / (1.0 + jnp.exp(-x))


def _silu(x):
    return x * _sigmoid(x)


def _dsilu(x):
    s = _sigmoid(x)
    return s * (1.0 + x * (1.0 - s))


def _softplus(x):
    e = jnp.exp(-jnp.abs(x))
    small = e * (1.0 - e * (0.5 - e * (1.0 / 3.0)))
    return jnp.maximum(x, 0.0) + jnp.where(e < 0.01, small, jnp.log(1.0 + e))


def _rms(x, w):
    r = lax.rsqrt(jnp.mean(x * x, axis=-1, keepdims=True) + EPS)
    return x * r * w


def _rms_bwd(dy, x, w):
    r = lax.rsqrt(jnp.mean(x * x, axis=-1, keepdims=True) + EPS)
    xh = x * r
    g = dy * w
    dx = r * (g - xh * jnp.mean(g * xh, axis=-1, keepdims=True))
    dw = jnp.sum(dy * xh, axis=0, keepdims=True)
    return dx, dw


def _dot(a, b):
    return jnp.dot(a, b, preferred_element_type=F32)


def _dot_nt(a, b):
    return lax.dot_general(a, b, (((1,), (1,)), ((), ())), preferred_element_type=F32)


def _dot_tn(a, b):
    return lax.dot_general(a, b, (((0,), (0,)), ((), ())), preferred_element_type=F32)


def _split3(x):
    hi = x.astype(BF16)
    r = x - hi.astype(F32)
    mid = r.astype(BF16)
    lo = (r - mid.astype(F32)).astype(BF16)
    return hi, mid, lo


def _xdot(x, e):
    hi, mid, lo = _split3(x)
    return _dot(hi, e) + _dot(mid, e) + _dot(lo, e)


def _xdot_l(e, x):
    hi, mid, lo = _split3(x)
    return _dot(e, hi) + _dot(e, mid) + _dot(e, lo)


def _iota(shape, dim):
    return lax.broadcasted_iota(jnp.int32, shape, dim)


def _divisors(n, unit):
    return [t for t in range(unit, n + 1, unit) if n % t == 0]


def _matmul_tiles(m, n, k, a_bytes, b_bytes, o_bytes, m_unit):
    best = None
    for tm in _divisors(m, m_unit):
        for tn in _divisors(n, 128):
            for tk in _divisors(k, 128):
                acc = 0 if tk == k else tm * tn * 4
                vm = 2 * (tm * tk * a_bytes + tk * tn * b_bytes + tm * tn * o_bytes) + acc
                if vm > VMEM_BUDGET:
                    continue
                score = (tm * tn * tk, tk)
                if best is None or score > best[0]:
                    best = (score, (tm, tn, tk))
    return best[1]


def _matmul(name, a, b, mode, out_dtype, bg=None):
    if mode == "nn":
        (m, k), n = a.shape, b.shape[1]
    elif mode == "nt":
        (m, k), n = a.shape, b.shape[0]
    else:
        (k, m), n = a.shape, b.shape[1]
    ab, bb, ob = a.dtype.itemsize, b.dtype.itemsize, jnp.dtype(out_dtype).itemsize
    tm, tn, tk = _matmul_tiles(m, n, k, ab, bb, ob, 128 if mode == "tn" else 16)
    nk = k // tk
    dot = {"nn": _dot, "nt": _dot_nt, "tn": _dot_tn}[mode]

    def body(a_ref, b_ref, o_ref, *scratch):
        prod = dot(a_ref[...].astype(BF16), b_ref[...].astype(BF16))
        if nk == 1:
            o_ref[...] = prod.astype(o_ref.dtype)
        else:
            acc_ref, = scratch
            kk = pl.program_id(2)

            @pl.when(kk == 0)
            def _():
                acc_ref[...] = prod

            @pl.when(kk > 0)
            def _():
                acc_ref[...] += prod

            @pl.when(kk == nk - 1)
            def _():
                o_ref[...] = acc_ref[...].astype(o_ref.dtype)

    a_spec = pl.BlockSpec((tk, tm), lambda i, j, kk: (kk, i)) if mode == "tn" else pl.BlockSpec((tm, tk), lambda i, j, kk: (i, kk))
    b_spec = pl.BlockSpec((tn, tk), lambda i, j, kk: (j, kk)) if mode == "nt" else pl.BlockSpec((tk, tn), lambda i, j, kk: (kk, j))
    res = _call(body, name, (m // tm, n // tn, nk), [a_spec, b_spec], [pl.BlockSpec((tm, tn), lambda i, j, kk: (i, j))],
                [jax.ShapeDtypeStruct((m, n), out_dtype)], [a, b],
                scratch_shapes=[] if nk == 1 else [pltpu.VMEM((tm, tn), F32)], bg=bg)
    return res[0] if bg is None else (res[0][0], res[1])


def _row_tile(n_rows, cap, unit=16):
    return max(t for t in _divisors(n_rows, unit) if t <= cap)


ROW_SUB = 384
GROUP_UNROLL = 4


def _rowwise(name, fn, n_rows, tm, row_ins, full_ins, row_outs, acc_outs, bg=None):
    n_in = len(row_ins) + len(full_ins)
    n_ro = len(row_outs)
    into = [(k, o[3]) for k, o in enumerate(row_outs) if len(o) > 2 and o[2] == "into"]

    n_row_in = len(row_ins)
    sub = min(tm, ROW_SUB)

    def body(*refs):
        i = pl.program_id(0)
        outs = refs[n_in + len(into):]

        def group(s, sums):
            rows = pl.ds(pl.multiple_of(s * sub, sub), sub)
            vals = [r[rows, :] for r in refs[:n_row_in]] + [r[...] for r in refs[n_row_in:n_in]]
            res = fn(i * tm + s * sub, *vals)
            for o, r, v in zip(row_outs, outs[:n_ro], res[:n_ro]):
                if len(o) > 2 and o[2] == "first":
                    @pl.when(i == 0)
                    def _(r=r, v=v):
                        r[rows, :] = v.astype(r.dtype)
                else:
                    r[rows, :] = v.astype(r.dtype)
            return tuple(a + v for a, v in zip(sums, res[n_ro:]))

        sums = lax.fori_loop(0, tm // sub, group, tuple(jnp.zeros((1, w), F32) for w in acc_outs), unroll=GROUP_UNROLL)

        @pl.when(i == 0)
        def _():
            for r, v in zip(outs[n_ro:], sums):
                r[...] = v

        @pl.when(i > 0)
        def _():
            for r, v in zip(outs[n_ro:], sums):
                r[...] += v

    def in_spec(entry):
        w, cb = entry[1], entry[2]
        if len(entry) > 3 and entry[3] == "prev":
            return pl.BlockSpec((tm, w), lambda i: (jnp.maximum(i - 1, 0), cb))
        if len(entry) > 3 and entry[3] == "first":
            return pl.BlockSpec((tm, w), lambda i: (0, cb))
        return pl.BlockSpec((tm, w), lambda i: (i, cb))

    def out_spec(o):
        if len(o) == 2:
            return pl.BlockSpec((tm, o[0]), lambda i: (i, 0)), jax.ShapeDtypeStruct((n_rows, o[0]), o[1])
        if o[2] == "new":
            return pl.BlockSpec((tm, o[0]), lambda i: (i, o[4])), jax.ShapeDtypeStruct((n_rows, o[3]), o[1])
        if o[2] == "into":
            return pl.BlockSpec((tm, o[0]), lambda i: (i, o[4])), jax.ShapeDtypeStruct(o[3].shape, o[3].dtype)
        if o[2] == "first":
            return pl.BlockSpec((tm, o[0]), lambda i: (0, 0)), jax.ShapeDtypeStruct((tm, o[0]), o[1])
        return pl.BlockSpec((tm, o[0]), lambda i: (jnp.maximum(i - 1, 0), 0)), jax.ShapeDtypeStruct((o[3], o[0]), o[1])

    in_specs = [in_spec(e) for e in row_ins]
    in_specs += [pl.BlockSpec(a.shape, lambda i: (0, 0)) for a in full_ins]
    in_specs += [pl.BlockSpec(memory_space=pl.ANY) for _ in into]
    specs_shapes = [out_spec(o) for o in row_outs]
    out_specs = [s for s, _ in specs_shapes] + [pl.BlockSpec((1, w), lambda i: (0, 0)) for w in acc_outs]
    out_shape = [s for _, s in specs_shapes] + [jax.ShapeDtypeStruct((1, w), F32) for w in acc_outs]
    return _call(body, name, (n_rows // tm,), in_specs, out_specs, out_shape,
                 [e[0] for e in row_ins] + list(full_ins) + [arr for _, arr in into],
                 aliases={n_in + a: k for a, (k, _) in enumerate(into)}, bg=bg)


def _valid_rows(first_row, tm, lo):
    return (first_row + _iota((tm, 1), 0)) >= lo


CONV_ROWS = 128
CONV_SUB = 16
CONV_LANES = 256


def _conv_specs(tm, width, blk, n_rows, after):
    specs = [pl.BlockSpec((tm, width), lambda i: (i, blk)),
             pl.BlockSpec((8, width), lambda i: (jnp.maximum(i * (tm // 8) - 1, 0), blk))]
    if after:
        specs.append(pl.BlockSpec((16, width), lambda i: (jnp.minimum((i + 1) * (tm // 16), n_rows // 16 - 1), blk)))
    return specs


def _conv_window(win, w_ref, b_ref, taps, c0, cw, n):
    acc = b_ref[:, c0:c0 + cw] + w_ref[taps - 1:taps, c0:c0 + cw] * win[8:8 + n]
    for k in range(taps - 1):
        acc = acc + w_ref[k:k + 1, c0:c0 + cw] * win[8 - (taps - 1) + k:8 - (taps - 1) + k + n]
    return acc


def _ffn_act(name, u_raw, conv_w, conv_b, n_rows):
    tm, sub, cw = CONV_ROWS, CONV_SUB, CONV_LANES
    taps, width = conv_w.shape
    half = width // 2

    def body(cur_ref, prev_ref, w_ref, b_ref, f_ref, ext_ref):
        i = pl.program_id(0)
        ext_ref[0:8, :] = jnp.where(i > 0, prev_ref[...], 0.0)
        ext_ref[8:8 + tm, :] = cur_ref[...]
        for q in range(half // cw):
            a0, g0 = q * cw, half + q * cw

            def group(s, carry):
                r = pl.multiple_of(s * sub, sub)
                a = _conv_window(ext_ref[pl.ds(r, sub + 8), a0:a0 + cw], w_ref, b_ref, taps, a0, cw, sub)
                g = _conv_window(ext_ref[pl.ds(r, sub + 8), g0:g0 + cw], w_ref, b_ref, taps, g0, cw, sub)
                f = jnp.where(_valid_rows(i * tm + r, sub, PAD), _silu(a) * g, 0.0)
                f_ref[pl.ds(r, sub), a0:a0 + cw] = f.astype(f_ref.dtype)
                return carry

            lax.fori_loop(0, tm // sub, group, 0, unroll=GROUP_UNROLL)

    return pl.pallas_call(
        body, name=name, grid=(n_rows // tm,),
        in_specs=_conv_specs(tm, width, 0, n_rows, False) + [pl.BlockSpec((taps, width), lambda i: (0, 0)),
                                                             pl.BlockSpec((1, width), lambda i: (0, 0))],
        out_specs=pl.BlockSpec((tm, half), lambda i: (i, 0)),
        out_shape=jax.ShapeDtypeStruct((n_rows, half), BF16),
        scratch_shapes=[pltpu.VMEM((tm + 8, width), F32)],
        compiler_params=_cparams(1),
    )(u_raw, u_raw, conv_w, conv_b)


def _conv_bwd(name, raw, raw_blk, dsrcs, chunk_src, conv_w, conv_b, n_rows, gated, into=None, into_blk=0, bg=None):
    taps, width = conv_w.shape
    half = width // 2 if gated else width
    tm, sub, cw = CONV_ROWS, CONV_SUB, CONV_LANES
    te = tm + 16
    nd = len(dsrcs)
    n_parts = 2 if gated else 1

    def body(*refs):
        cur_ref, prev_ref, next_ref = refs[0:3]
        dcur, dnext = refs[3:3 + nd], refs[3 + nd:3 + 2 * nd]
        w_ref, b_ref = refs[3 + 2 * nd:5 + 2 * nd]
        out_ref, acc_ref, ext_ref, du_ref = refs[-4:]
        i = pl.program_id(0)
        ext_ref[0:8, :] = jnp.where(i > 0, prev_ref[...], 0.0)
        ext_ref[8:8 + tm, :] = cur_ref[...]
        ext_ref[8 + tm:24 + tm, :] = next_ref[...]

        for q, (src, off) in enumerate(chunk_src):
            cols = [q * cw, half + q * cw][:n_parts]

            def conv_grad(r, d):
                pre = [_conv_window(ext_ref[pl.ds(r, sub + 8), c0:c0 + cw], w_ref, b_ref, taps, c0, cw, sub) for c0 in cols]
                row = i * tm + r + _iota((sub, 1), 0)
                live = (row >= PAD) & (row < n_rows)
                if gated:
                    dus = [d * pre[1] * _dsilu(pre[0]), d * _silu(pre[0])]
                else:
                    dus = [d * _dsilu(pre[0])]
                for part, du in enumerate(dus):
                    du_ref[part, pl.ds(r, sub), :] = jnp.where(live, du, 0.0)

            def tile_rows(s, carry):
                r = pl.multiple_of(s * sub, sub)
                conv_grad(r, dcur[src][pl.ds(r, sub), off:off + cw].astype(F32))
                return carry

            lax.fori_loop(0, tm // sub, tile_rows, 0, unroll=GROUP_UNROLL)
            conv_grad(tm, dnext[src][:, off:off + cw].astype(F32))

            for part, c0 in enumerate(cols):
                taps_w = [w_ref[k:k + 1, c0:c0 + cw] for k in range(taps)]

                def back(s, sums):
                    new = list(sums)
                    for u in range(2):
                        r = pl.multiple_of((2 * s + u) * sub, sub)
                        win = du_ref[part, pl.ds(r, sub + 8), :]
                        raw_rows = ext_ref[pl.ds(8 + r, sub), c0:c0 + cw]
                        draw = jnp.zeros((sub, cw), F32)
                        for k in range(taps):
                            shifted = win[taps - 1 - k:taps - 1 - k + sub]
                            draw = draw + taps_w[k] * shifted
                            new[k] = new[k] + shifted * raw_rows
                        new[taps] = new[taps] + win[0:sub]
                        out_ref[pl.ds(r, sub), c0:c0 + cw] = jnp.where(_valid_rows(i * tm + r, sub, PAD), draw, 0.0).astype(out_ref.dtype)
                    return tuple(new)

                sums = lax.fori_loop(0, tm // (2 * sub), back, tuple(jnp.zeros((sub, cw), F32) for _ in range(taps + 1)))
                for k in range(taps + 1):
                    total = jnp.sum(sums[k], axis=0, keepdims=True)
                    acc_ref[k:k + 1, c0:c0 + cw] = jnp.where(i == 0, total, acc_ref[k:k + 1, c0:c0 + cw] + total)

    in_specs = _conv_specs(tm, width, raw_blk, n_rows, True)
    in_specs += [pl.BlockSpec((tm, d.shape[1]), lambda i: (i, 0)) for d in dsrcs]
    in_specs += [pl.BlockSpec((16, d.shape[1]), lambda i: (jnp.minimum((i + 1) * (tm // 16), n_rows // 16 - 1), 0)) for d in dsrcs]
    in_specs += [pl.BlockSpec((taps, width), lambda i: (0, 0)), pl.BlockSpec((1, width), lambda i: (0, 0))]
    operands = [raw, raw, raw] + list(dsrcs) + list(dsrcs) + [conv_w, conv_b]
    aliases = {}
    if into is None:
        out0 = jax.ShapeDtypeStruct((n_rows, width), BF16)
    else:
        in_specs.append(pl.BlockSpec(memory_space=pl.ANY))
        operands.append(into)
        aliases = {len(operands) - 1: 0}
        out0 = jax.ShapeDtypeStruct(into.shape, into.dtype)
    return _call(body, name, (n_rows // tm,), in_specs,
                 [pl.BlockSpec((tm, width), lambda i: (i, into_blk)), pl.BlockSpec((8, width), lambda i: (0, 0))],
                 [out0, jax.ShapeDtypeStruct((8, width), F32)], operands,
                 scratch_shapes=[pltpu.VMEM((tm + 24, width), F32), pltpu.VMEM((n_parts, te + 8, cw), F32)],
                 aliases=aliases, bg=bg)


def _ssd_specs(n_chunks, rev):
    cidx = (lambda c: n_chunks - 1 - c) if rev else (lambda c: c)
    xg0, bg0, cg0 = P_XBC // GROUP_W, (P_XBC + D_INNER) // D_STATE, (P_XBC + D_INNER + SSM_GROUPS * D_STATE) // D_STATE

    def cur(width, blk0):
        return pl.BlockSpec((T, width), lambda g, c: (cidx(c), blk0 + g))

    def prev(width, blk0):
        return pl.BlockSpec((8, width), lambda g, c: (jnp.maximum(cidx(c) * (T // 8) - 1, 0), blk0 + g))

    specs = [cur(GROUP_W, xg0), prev(GROUP_W, xg0), cur(D_STATE, bg0), prev(D_STATE, bg0),
             cur(D_STATE, cg0), prev(D_STATE, cg0),
             pl.BlockSpec((T, 128), lambda g, c: (cidx(c), P_DT // 128))]
    wx, wb, wc = 0, D_INNER // D_STATE, (D_INNER + SSM_GROUPS * D_STATE) // D_STATE
    specs += [pl.BlockSpec((4, GROUP_W), lambda g, c: (0, g)),
              pl.BlockSpec((4, D_STATE), lambda g, c: (0, wb + g)),
              pl.BlockSpec((4, D_STATE), lambda g, c: (0, wc + g)),
              pl.BlockSpec((1, GROUP_W), lambda g, c: (0, g)),
              pl.BlockSpec((1, D_STATE), lambda g, c: (0, wb + g)),
              pl.BlockSpec((1, D_STATE), lambda g, c: (0, wc + g))]
    specs += [pl.BlockSpec((1, 128), lambda g, c: (0, 0))] * 3
    return specs, cidx


def _ssd_chunk_forward(refs, ext_ref, g, c):
    (xc_ref, xp_ref, bc_ref, bp_ref, cc_ref, cp_ref, dt_ref, wx_ref, wb_ref, wc_ref,
     bx_ref, bb_ref, bcb_ref, dtb_ref, alog_ref, dsk_ref) = refs

    def conv_pre(cur_ref, prev_ref, w_ref, b_ref, width):
        ext_ref[0:8, 0:width] = jnp.where(c > 0, prev_ref[...], 0.0)
        ext_ref[8:8 + T, 0:width] = cur_ref[...]
        w = w_ref[...]
        acc = b_ref[...] + w[3:4] * cur_ref[...]
        for k in range(3):
            acc = acc + w[k:k + 1] * ext_ref[pl.ds(5 + k, T), 0:width]
        return acc

    valid = _valid_rows(c * T, T, PAD)
    v = {}
    v["valid"] = valid
    v["x_pre"] = conv_pre(xc_ref, xp_ref, wx_ref, bx_ref, GROUP_W)
    v["b_pre"] = conv_pre(bc_ref, bp_ref, wb_ref, bb_ref, D_STATE)
    v["c_pre"] = conv_pre(cc_ref, cp_ref, wc_ref, bcb_ref, D_STATE)
    xs = _silu(v["x_pre"])
    bm = jnp.where(valid, _silu(v["b_pre"]), 0.0)
    cm = jnp.where(valid, _silu(v["c_pre"]), 0.0)
    dtr = dt_ref[...] + dtb_ref[...]
    dt = jnp.where(valid, _softplus(dtr), 0.0)
    a_neg = -jnp.exp(alog_ref[...])
    a = dt * a_neg
    tril = _iota((T, T), 0) >= _iota((T, T), 1)
    cs = _xdot_l(tril.astype(BF16), a)
    hh, ll = _iota((128, GROUP_W), 0), _iota((128, GROUP_W), 1)
    expand = (hh == 8 * g + jnp.right_shift(ll, 6)).astype(BF16)
    sh, sj = _iota((128, 128), 0), _iota((128, 128), 1)
    select = ((sh == 8 * g + sj) & (sj < 8)).astype(BF16)
    hh_t, ll_t = _iota((GROUP_W, 128), 1), _iota((GROUP_W, 128), 0)
    v["expand_t"] = (hh_t == 8 * g + jnp.right_shift(ll_t, 6)).astype(BF16)
    v["select_t"] = ((sj == 8 * g + sh) & (sh < 8)).astype(BF16)
    cs_e = _xdot(cs, expand)
    dt_e = _xdot(dt, expand)
    cs_loc = _xdot(cs, select)
    cs_loc_t = cs_loc.T
    cs_last_e = cs_e[T - 1:T, :]
    v.update(xs=xs, bm=bm, cm=cm, dtr=dtr, dt=dt, a_neg=a_neg, tril=tril, expand=expand, select=select,
             cs_e=cs_e, dt_e=dt_e, cs_loc=cs_loc, cs_loc_t=cs_loc_t, cs_last_e=cs_last_e)
    v["xdt"] = xs * dt_e
    v["decay_e"] = jnp.exp(cs_last_e - cs_e)
    v["ecs_e"] = jnp.exp(cs_e)
    v["elast_e"] = jnp.exp(cs_last_e)
    v["d_e"] = _xdot(dsk_ref[...], expand)
    v["gmat"] = _dot_nt(cm.astype(BF16), bm.astype(BF16))
    return v


def _ssd_decay_pair(v, jp):
    out = []
    for j in (2 * jp, 2 * jp + 1):
        diff = v["cs_loc"][:, j:j + 1] - v["cs_loc_t"][j:j + 1, :]
        out.append(jnp.where(v["tril"], jnp.exp(jnp.where(v["tril"], diff, 0.0)), 0.0))
    return out


def _block_diag_pair(xp):
    lane = _iota(xp.shape, 1)
    return jnp.concatenate([jnp.where(lane < HEAD_P, xp, 0.0), jnp.where(lane >= HEAD_P, xp, 0.0)], axis=0)


def _ssd_fwd(p, conv_w, conv_b, dt_bias, a_log, d_skip, n_chunks, bg=None):
    n_rows = n_chunks * T
    in_specs, _ = _ssd_specs(n_chunks, rev=False)

    def body(*refs):
        y_ref, hin_ref, st_ref, ext_ref = refs[16:]
        g, c = pl.program_id(0), pl.program_id(1)

        @pl.when(c == 0)
        def _():
            st_ref[...] = jnp.zeros_like(st_ref)

        v = _ssd_chunk_forward(refs[:16], ext_ref, g, c)
        state = st_ref[...]
        hin_ref[...] = state
        ys = []
        for jp in range(4):
            l0, l1 = _ssd_decay_pair(v, jp)
            lhs = jnp.concatenate([v["gmat"] * l0, v["gmat"] * l1], axis=1).astype(BF16)
            rhs = _block_diag_pair(v["xdt"][:, 128 * jp:128 * jp + 128]).astype(BF16)
            ys.append(_dot(lhs, rhs))
        y = jnp.concatenate(ys, axis=1)
        y = y + _dot(v["cm"].astype(BF16), state.astype(BF16)) * v["ecs_e"] + v["xs"] * v["d_e"]
        y_ref[...] = y
        s_new = _dot_tn(v["bm"].astype(BF16), (v["xdt"] * v["decay_e"]).astype(BF16))
        st_ref[...] = state * v["elast_e"] + s_new

    return _call(
        body, "ssd_fwd", (SSM_GROUPS, n_chunks), in_specs,
        [pl.BlockSpec((T, GROUP_W), lambda g, c: (c, g)),
         pl.BlockSpec((None, None, D_STATE, GROUP_W), lambda g, c: (g, c, 0, 0))],
        [jax.ShapeDtypeStruct((n_rows, D_INNER), F32),
         jax.ShapeDtypeStruct((SSM_GROUPS, n_chunks, D_STATE, GROUP_W), F32)],
        [p, p, p, p, p, p, p, conv_w, conv_w, conv_w, conv_b, conv_b, conv_b, dt_bias, a_log, d_skip],
        scratch_shapes=[pltpu.VMEM((D_STATE, GROUP_W), F32), pltpu.VMEM((T + 8, GROUP_W), F32)], bg=bg)


def _ssd_bwd(p, conv_w, conv_b, dt_bias, a_log, d_skip, hin, dy, n_chunks, bg=None):
    n_rows = n_chunks * T
    in_specs, cidx = _ssd_specs(n_chunks, rev=True)
    in_specs = in_specs + [pl.BlockSpec((None, None, D_STATE, GROUP_W), lambda g, c: (g, cidx(c), 0, 0)),
                           pl.BlockSpec((T, GROUP_W), lambda g, c: (cidx(c), g))]

    def body(*refs):
        hin_ref, dy_ref = refs[16:18]
        dx_ref, db_ref, dc_ref, ddt_ref, dpar_ref, dst_ref, ext_ref = refs[18:]
        g, step = pl.program_id(0), pl.program_id(1)
        c = n_chunks - 1 - step

        @pl.when(step == 0)
        def _():
            dst_ref[...] = jnp.zeros_like(dst_ref)

        v = _ssd_chunk_forward(refs[:16], ext_ref, g, c)
        hin_f = hin_ref[...]
        hin_b = hin_f.astype(BF16)
        dyv = dy_ref[...]
        dst = dst_ref[...]
        dst_b = dst.astype(BF16)
        xs, bm, cm, xdt = v["xs"], v["bm"], v["cm"], v["xdt"]
        bm_b, cm_b = bm.astype(BF16), cm.astype(BF16)

        dd_e = jnp.sum(dyv * xs, axis=0, keepdims=True)
        dxs = dyv * v["d_e"]
        ch = _dot(cm_b, hin_b)
        dch = (dyv * v["ecs_e"]).astype(BF16)
        dcm = _dot_nt(dch, hin_b)
        dhin = _dot_tn(cm_b, dch) + dst * v["elast_e"]
        dcs_e = dyv * ch * v["ecs_e"]
        dxd = _dot(bm_b, dst_b)
        dbm = _dot_nt((xdt * v["decay_e"]).astype(BF16), dst_b)
        dxdt_state = dxd * v["decay_e"]
        q = dxdt_state * xdt
        dcs_e = dcs_e - q
        dlast_e = jnp.sum(q, axis=0, keepdims=True) + jnp.sum(dst * hin_f, axis=0, keepdims=True) * v["elast_e"]
        dg = jnp.zeros((T, T), F32)
        rs_cols = jnp.zeros((T, 128), F32)
        cs_rows = jnp.zeros((128, T), F32)
        lane_i, sub_i = _iota((T, 128), 1), _iota((128, T), 0)
        dxdt_parts = []
        for jp in range(4):
            l0, l1 = _ssd_decay_pair(v, jp)
            m0, m1 = v["gmat"] * l0, v["gmat"] * l1
            xbd = _block_diag_pair(xdt[:, 128 * jp:128 * jp + 128]).astype(BF16)
            dyp = dyv[:, 128 * jp:128 * jp + 128]
            dm = _dot_nt(dyp.astype(BF16), xbd)
            dm0, dm1 = dm[:, 0:T], dm[:, T:2 * T]
            dg = dg + dm0 * l0 + dm1 * l1
            for j, qq in ((2 * jp, dm0 * m0), (2 * jp + 1, dm1 * m1)):
                rs_cols = jnp.where(lane_i == j, jnp.sum(qq, axis=1, keepdims=True), rs_cols)
                cs_rows = jnp.where(sub_i == j, jnp.sum(qq, axis=0, keepdims=True), cs_rows)
            mv = jnp.concatenate([m0, m1], axis=0).astype(BF16)
            dxdt_parts.append(_dot_tn(mv, _block_diag_pair(dyp).astype(BF16)))
        dxdt = jnp.concatenate(dxdt_parts, axis=1) + dxdt_state
        dg_b = dg.astype(BF16)
        dcm = dcm + _dot(dg_b, bm_b)
        dbm = dbm + _dot_tn(dg_b, cm_b)
        expand_t = v["expand_t"]
        dcs_loc = rs_cols - cs_rows.T
        last_row = _iota((T, 1), 0) == T - 1
        dcs_full_e = dcs_e + jnp.where(last_row, dlast_e, 0.0)
        dcs = _xdot(dcs_full_e, expand_t) + _xdot(dcs_loc, v["select_t"])
        triu = (_iota((T, T), 0) <= _iota((T, T), 1)).astype(BF16)
        da = _xdot_l(triu, dcs)
        ddt = da * v["a_neg"] + _xdot(dxdt * xs, expand_t)
        dxs = dxs + dxdt * v["dt_e"]
        ddtr = jnp.where(v["valid"], ddt * _sigmoid(v["dtr"]), 0.0)
        dx_ref[...] = dxs
        db_ref[...] = jnp.where(v["valid"], dbm, 0.0)
        dc_ref[...] = jnp.where(v["valid"], dcm, 0.0)
        ddt_ref[...] = ddtr
        dpar = jnp.concatenate([
            jnp.sum(ddtr, axis=0, keepdims=True),
            jnp.sum(da * v["dt"], axis=0, keepdims=True) * v["a_neg"],
            _xdot(dd_e, expand_t),
            jnp.zeros((5, 128), F32)], axis=0)

        @pl.when(step == 0)
        def _():
            dpar_ref[...] = dpar

        @pl.when(step > 0)
        def _():
            dpar_ref[...] += dpar

        dst_ref[...] = dhin

    return _call(
        body, "ssd_bwd", (SSM_GROUPS, n_chunks), in_specs,
        [pl.BlockSpec((T, GROUP_W), lambda g, c: (cidx(c), g)),
         pl.BlockSpec((T, D_STATE), lambda g, c: (cidx(c), g)),
         pl.BlockSpec((T, D_STATE), lambda g, c: (cidx(c), g)),
         pl.BlockSpec((T, 128), lambda g, c: (cidx(c), g)),
         pl.BlockSpec((None, 8, 128), lambda g, c: (g, 0, 0))],
        [jax.ShapeDtypeStruct((n_rows, D_INNER), F32),
         jax.ShapeDtypeStruct((n_rows, SSM_GROUPS * D_STATE), F32),
         jax.ShapeDtypeStruct((n_rows, SSM_GROUPS * D_STATE), F32),
         jax.ShapeDtypeStruct((n_rows, SSM_GROUPS * 128), F32),
         jax.ShapeDtypeStruct((SSM_GROUPS, 8, 128), F32)],
        [p, p, p, p, p, p, p, conv_w, conv_w, conv_w, conv_b, conv_b, conv_b, dt_bias, a_log, d_skip, hin, dy],
        scratch_shapes=[pltpu.VMEM((D_STATE, GROUP_W), F32), pltpu.VMEM((T + 8, GROUP_W), F32)], bg=bg)


def _alibi_slope(h):
    return 2.0 ** (-8.0 * (h + 1) / ATTN_HEADS)


def _dup_half(x256, kvh):
    xb = x256[:, 128 * (kvh // 2):128 * (kvh // 2) + 128]
    rolled = pltpu.roll(xb, 64, 1)
    lane = _iota(xb.shape, 1)
    if kvh % 2 == 0:
        return jnp.where(lane < 64, xb, rolled)
    return jnp.where(lane < 64, rolled, xb)


def _attn_masks(c):
    qi = _iota((T, 3 * T), 0)
    jj = _iota((T, 3 * T), 1)
    blk = jnp.right_shift(jj, 7)
    j = jnp.bitwise_and(jj, T - 1)
    q_pos = c * T + qi - PAD
    k_pos = (c - 2 + blk) * T + j - PAD
    dist = q_pos - k_pos
    band = (blk > 0) & (dist >= 0) & (dist < T) & (k_pos >= N_META)
    meta = (blk == 0) & (j >= PAD) & (j - PAD <= q_pos)
    distf = jnp.where(blk > 0, dist, 0).astype(F32)
    return band | meta, distf


def _attn_scores(qp, k3, allowed, distf, h0):
    lane = _iota(qp.shape, 1)
    s = []
    for half, h in ((0, h0), (1, h0 + 1)):
        qh = jnp.where((lane < 64) if half == 0 else (lane >= 64), qp, 0.0).astype(BF16)
        sc = _dot_nt(qh, k3) - _alibi_slope(h) * distf
        s.append((qh, jnp.where(allowed, sc, NEG)))
    return s


def _attn_fwd(p, sinks, n_chunks, bg=None):
    n_rows = n_chunks * T
    kb, vb = P_K // KV_W, P_V // KV_W

    def body(q_ref, kc_ref, kp_ref, km_ref, vc_ref, vp_ref, vm_ref, sink_ref, o_ref, lse_ref):
        c = pl.program_id(0)
        allowed, distf = _attn_masks(c)
        q = q_ref[...] * SCALE
        sinks_v = sink_ref[...]
        lane = _iota((T, 128), 1)
        lse_all = jnp.zeros((T, 128), F32)
        outs = []
        for kvh in range(KV_HEADS):
            k3 = jnp.concatenate([_dup_half(r[...], kvh) for r in (km_ref, kp_ref, kc_ref)], axis=0).astype(BF16)
            v3 = jnp.concatenate([_dup_half(r[...], kvh) for r in (vm_ref, vp_ref, vc_ref)], axis=0)
            v3bd = _block_diag_rows(v3).astype(BF16)
            for pr in range(2):
                h0 = 4 * kvh + 2 * pr
                blk = 2 * kvh + pr
                qp = q[:, 128 * blk:128 * blk + 128]
                probs = []
                for (_, sc), h in zip(_attn_scores(qp, k3, allowed, distf, h0), (h0, h0 + 1)):
                    sink = sinks_v[:, h:h + 1]
                    m = jnp.maximum(jnp.max(sc, axis=1, keepdims=True), sink)
                    e = jnp.exp(sc - m)
                    den = jnp.sum(e, axis=1, keepdims=True) + jnp.exp(sink - m)
                    probs.append(e / den)
                    lse_all = jnp.where(lane == h, m + jnp.log(den), lse_all)
                outs.append(_dot(jnp.concatenate(probs, axis=1).astype(BF16), v3bd))
        o_ref[...] = jnp.concatenate(outs, axis=1).astype(o_ref.dtype)
        lse_ref[...] = lse_all

    blk = lambda width, col: pl.BlockSpec((T, width), lambda c: (c, col))
    prev = lambda width, col: pl.BlockSpec((T, width), lambda c: (jnp.maximum(c - 1, 0), col))
    first = lambda width, col: pl.BlockSpec((T, width), lambda c: (0, col))
    return _call(
        body, "attn_fwd", (n_chunks,),
        [blk(ATTN_W, P_Q // ATTN_W), blk(KV_W, kb), prev(KV_W, kb), first(KV_W, kb),
         blk(KV_W, vb), prev(KV_W, vb), first(KV_W, vb), pl.BlockSpec((1, 128), lambda c: (0, 0))],
        [pl.BlockSpec((T, ATTN_W), lambda c: (c, 0)), pl.BlockSpec((T, 128), lambda c: (c, 0))],
        [jax.ShapeDtypeStruct((n_rows, ATTN_W), BF16), jax.ShapeDtypeStruct((n_rows, 128), F32)],
        [p, p, p, p, p, p, p, sinks], bg=bg)


def _block_diag_rows(x3):
    lane = _iota(x3.shape, 1)
    return jnp.concatenate([jnp.where(lane < 64, x3, 0.0), jnp.where(lane >= 64, x3, 0.0)], axis=0)


def _fold_halves(x):
    return x + pltpu.roll(x, 64, 1)


def _attn_bwd(p, sinks, ao, lse, dao, dp, n_chunks, bg=None):
    kb, vb = P_K // KV_W, P_V // KV_W
    rc = lambda s: n_chunks - 1 - s

    def body(q_ref, kc_ref, kp_ref, km_ref, vc_ref, vp_ref, vm_ref, sink_ref, o_ref, lse_ref, do_ref, dp_in_ref,
             dqkv_ref, dsink_ref, kcar_ref, vcar_ref, kmeta_ref, vmeta_ref):
        step = pl.program_id(0)
        c = n_chunks - 1 - step

        @pl.when(step == 0)
        def _():
            for r in (kcar_ref, vcar_ref, kmeta_ref, vmeta_ref):
                r[...] = jnp.zeros_like(r)

        allowed, distf = _attn_masks(c)
        q = q_ref[...] * SCALE
        sinks_v = sink_ref[...]
        lse_v = lse_ref[...]
        ov = o_ref[...].astype(F32)
        dov = do_ref[...].astype(F32)
        lane = _iota((T, 128), 1)
        lane256 = _iota((3 * T, KV_W), 1)
        dsink = jnp.zeros((1, 128), F32)
        dk3_all = jnp.zeros((3 * T, KV_W), F32)
        dv3_all = jnp.zeros((3 * T, KV_W), F32)
        dqs = []
        for kvh in range(KV_HEADS):
            k3 = jnp.concatenate([_dup_half(r[...], kvh) for r in (km_ref, kp_ref, kc_ref)], axis=0).astype(BF16)
            v3 = jnp.concatenate([_dup_half(r[...], kvh) for r in (vm_ref, vp_ref, vc_ref)], axis=0).astype(BF16)
            dk3 = jnp.zeros((3 * T, 128), F32)
            dv3 = jnp.zeros((3 * T, 128), F32)
            for pr in range(2):
                h0 = 4 * kvh + 2 * pr
                blk = 2 * kvh + pr
                qp = q[:, 128 * blk:128 * blk + 128]
                dop = dov[:, 128 * blk:128 * blk + 128]
                prod = dop * ov[:, 128 * blk:128 * blk + 128]
                dq_pair = jnp.zeros((T, 128), F32)
                for half, ((qh, sc), h) in enumerate(zip(_attn_scores(qp, k3, allowed, distf, h0), (h0, h0 + 1))):
                    mine = (lane < 64) if half == 0 else (lane >= 64)
                    lse_h = lse_v[:, h:h + 1]
                    pm = jnp.exp(sc - lse_h)
                    doh = jnp.where(mine, dop, 0.0).astype(BF16)
                    delta = jnp.sum(jnp.where(mine, prod, 0.0), axis=1, keepdims=True)
                    dp = _dot_nt(doh, v3)
                    ds = (pm * (dp - delta)).astype(BF16)
                    p_sink = jnp.exp(sinks_v[:, h:h + 1] - lse_h)
                    dsink = jnp.where(_iota((1, 128), 1) == h, jnp.sum(-p_sink * delta, axis=0, keepdims=True), dsink)
                    dq_pair = jnp.where(mine, _dot(ds, k3), dq_pair)
                    dk3 = dk3 + _dot_tn(ds, qh)
                    dv3 = dv3 + _dot_tn(pm.astype(BF16), doh)
                dqs.append(dq_pair * SCALE)
            in_place = (lane256 >= 64 * kvh) & (lane256 < 64 * kvh + 64)
            wide = lambda x: jnp.concatenate([x, x], axis=1)
            dk3_all = jnp.where(in_place, wide(_fold_halves(dk3)), dk3_all)
            dv3_all = jnp.where(in_place, wide(_fold_halves(dv3)), dv3_all)
        dsink_all = dsink

        @pl.when(step == 0)
        def _():
            dsink_ref[...] = dsink_all

        @pl.when(step > 0)
        def _():
            dsink_ref[...] += dsink_all

        kmeta = kmeta_ref[...] + dk3_all[0:T]
        vmeta = vmeta_ref[...] + dv3_all[0:T]
        kmeta_ref[...] = kmeta
        vmeta_ref[...] = vmeta
        is_first = c == 0
        dk = jnp.where(is_first, kmeta, dk3_all[2 * T:3 * T] + kcar_ref[...])
        dv = jnp.where(is_first, vmeta, dv3_all[2 * T:3 * T] + vcar_ref[...])
        dqkv_ref[...] = jnp.concatenate(dqs + [dk, dv], axis=1).astype(dqkv_ref.dtype)
        kcar_ref[...] = dk3_all[T:2 * T]
        vcar_ref[...] = dv3_all[T:2 * T]

    blk = lambda width, col: pl.BlockSpec((T, width), lambda s: (rc(s), col))
    prev = lambda width, col: pl.BlockSpec((T, width), lambda s: (jnp.maximum(rc(s) - 1, 0), col))
    first = lambda width, col: pl.BlockSpec((T, width), lambda s: (0, col))
    return _call(
        body, "attn_bwd", (n_chunks,),
        [blk(ATTN_W, P_Q // ATTN_W), blk(KV_W, kb), prev(KV_W, kb), first(KV_W, kb),
         blk(KV_W, vb), prev(KV_W, vb), first(KV_W, vb), pl.BlockSpec((1, 128), lambda s: (0, 0)),
         blk(ATTN_W, 0), blk(128, 0), blk(ATTN_W, 0), ANY],
        [blk(QKV_W, P_Q // QKV_W), pl.BlockSpec((1, 128), lambda s: (0, 0))],
        [jax.ShapeDtypeStruct(dp.shape, dp.dtype), jax.ShapeDtypeStruct((1, 128), F32)],
        [p, p, p, p, p, p, p, sinks, ao, lse, dao, dp],
        scratch_shapes=[pltpu.VMEM((T, KV_W), F32)] * 4, aliases={11: 0}, bg=bg)


def _pad_lanes(v, width=128):
    return jnp.pad(v, ((0, 0), (0, width - v.shape[1])))


def _local_step(x, head, tgt, plan):
    w, g, run = plan.w, plan.g, plan.run
    n_tok = x.shape[0]
    n_rows = n_tok + T
    n_chunks = n_rows // T
    tm = _row_tile(n_rows, 384)
    dt_bias, a_log, d_skip = (_pad_lanes(w[k]) for k in ("ssm_dt_bias", "ssm_a_log", "ssm_d_skip"))
    sinks = _pad_lanes(w["attn_sinks"])
    x_in = [(x, D_MODEL, 0, "prev"), (head, D_MODEL, 0, "first")]

    def h0_tile(r0, xt, hd):
        return jnp.where(r0 < T, hd, xt)

    n1, = _rowwise("norm_pre_mix", lambda r0, xt, hd, wn: [_rms(h0_tile(r0, xt, hd), wn)], n_rows, T,
                   x_in, [w["norm_pre_mix"]], [(D_MODEL, BF16)], [])
    p = _matmul("in_proj", n1, w["w_cat"], "nn", F32)
    y_ssd, hin = run("ssd_fwd", _ssd_fwd, p, w["ssm_conv_w"], w["ssm_conv_b"], dt_bias, a_log, d_skip, n_chunks)
    ao, lse = run("attn_fwd", _attn_fwd, p, sinks, n_chunks)

    def gate_norm(r0, y, z, wn):
        return [_rms(y * _silu(z), wn)]

    yn, = _rowwise("ssm_gate_norm", gate_norm, n_rows, tm, [(y_ssd, D_INNER, 0), (p, D_INNER, P_Z // D_INNER)],
                   [w["ssm_norm"]], [(D_INNER, BF16)], [])
    y_ssm = _matmul("ssm_out", yn, w["w_ssm_out"], "nn", F32)
    y_attn = _matmul("attn_out", ao, w["w_attn_out"], "nn", F32)

    def mix_gate(r0, ys, ya, gs, ga):
        return [_sigmoid(gs) * ys + _sigmoid(ga) * ya]

    gate_ins = [(p, D_MODEL, P_GATE // D_MODEL), (p, D_MODEL, P_GATE // D_MODEL + 1)]
    mixed, = _rowwise("mix_gate", mix_gate, n_rows, tm, [(y_ssm, D_MODEL, 0), (y_attn, D_MODEL, 0)] + gate_ins,
                      [], [(D_MODEL, BF16)], [])
    mix = _matmul("mix_out", mixed, w["w_mix_out"], "nn", F32)

    def post_mix(r0, mx, xt, hd, w_post, w_pre):
        h1 = jnp.where(_valid_rows(r0, mx.shape[0], PAD), h0_tile(r0, xt, hd) + _rms(mx, w_post), 0.0)
        return [h1, _rms(h1, w_pre)]

    h1, n2 = _rowwise("post_mix", post_mix, n_rows, T, [(mix, D_MODEL, 0)] + x_in,
                      [w["norm_post_mix"], w["norm_pre_ffn"]], [(D_MODEL, F32), (D_MODEL, BF16)], [])
    u_raw = _matmul("ffn_up", n2, w["w_ffn_up"], "nn", F32)
    f = _ffn_act("ffn_act", u_raw, w["ffn_conv_w"], w["ffn_conv_b"], n_rows)
    ffn = _matmul("ffn_down", f, w["w_ffn_down"], "nn", F32)

    def final(r0, fo, h, t, w_post):
        real = r0 >= T
        err = jnp.where(real, h + _rms(fo, w_post) - t, 0.0)
        dy = err * (1.0 / D_MODEL)
        dffn, dw = _rms_bwd(dy, fo, w_post)
        return [dffn, dy, jnp.sum(err * err, axis=0, keepdims=True), dw]

    dffn, dh2, loss_cols, g_norm_post_ffn = _rowwise(
        "loss_head", final, n_rows, T, [(ffn, D_MODEL, 0), (h1, D_MODEL, 0), (tgt, D_MODEL, 0, "prev")],
        [w["norm_post_ffn"]], [(D_MODEL, BF16), (D_MODEL, F32)], [D_MODEL, D_MODEL])

    g["norm_post_ffn"] = g_norm_post_ffn
    g["w_ffn_down"] = _matmul("ffn_down_dw", f, dffn, "tn", F32)
    df = _matmul("ffn_down_dx", dffn, w["w_ffn_down"], "nt", F32)
    du_raw, dconv = _conv_bwd("ffn_act_bwd", u_raw, 0, [df], [(0, c0) for c0 in range(0, FFN_DIM, CONV_LANES)],
                              w["ffn_conv_w"], w["ffn_conv_b"], n_rows, True)
    g["ffn_conv_w"], g["ffn_conv_b"] = dconv[0:3], dconv[3:4]
    g["w_ffn_up"] = _matmul("ffn_up_dw", n2, du_raw, "tn", F32)
    dn2 = run("ffn_up_dx", _matmul, "ffn_up_dx", du_raw, w["w_ffn_up"], "nt", F32)

    def post_mix_bwd(r0, dn, d2, h, mx, w_pre, w_post):
        dx, dw_pre = _rms_bwd(dn, h, w_pre)
        dh1 = jnp.where(_valid_rows(r0, dn.shape[0], PAD), dx + d2, 0.0)
        dmix, dw_post = _rms_bwd(dh1, mx, w_post)
        return [dh1, dmix, dw_pre, dw_post]

    dh1, dmix, g["norm_pre_ffn"], g["norm_post_mix"] = _rowwise(
        "post_mix_bwd", post_mix_bwd, n_rows, tm,
        [(dn2, D_MODEL, 0), (dh2, D_MODEL, 0), (h1, D_MODEL, 0), (mix, D_MODEL, 0)],
        [w["norm_pre_ffn"], w["norm_post_mix"]], [(D_MODEL, F32), (D_MODEL, BF16)], [D_MODEL, D_MODEL])
    g["w_mix_out"] = _matmul("mix_out_dw", mixed, dmix, "tn", F32)
    dmixed = _matmul("mix_out_dx", dmix, w["w_mix_out"], "nt", F32)

    def mix_gate_bwd(r0, dm, ys, ya, gs, ga):
        ss, sa = _sigmoid(gs), _sigmoid(ga)
        dgate = jnp.concatenate([dm * ys * ss * (1.0 - ss), dm * ya * sa * (1.0 - sa)], axis=1)
        return [dm * ss, dm * sa, dgate]

    dys, dya, dp = _rowwise(
        "mix_gate_bwd", mix_gate_bwd, n_rows, tm,
        [(dmixed, D_MODEL, 0), (y_ssm, D_MODEL, 0), (y_attn, D_MODEL, 0)] + gate_ins,
        [], [(D_MODEL, BF16), (D_MODEL, BF16), (2 * D_MODEL, BF16, "new", P_W, P_GATE // (2 * D_MODEL))], [])
    g["w_ssm_out"] = _matmul("ssm_out_dw", yn, dys, "tn", F32)
    dyn = _matmul("ssm_out_dx", dys, w["w_ssm_out"], "nt", F32)
    g["w_attn_out"] = _matmul("attn_out_dw", ao, dya, "tn", F32)
    dao = _matmul("attn_out_dx", dya, w["w_attn_out"], "nt", BF16)

    def gate_norm_bwd(r0, dn, y, z, wn):
        sz = _silu(z)
        dyz, dw = _rms_bwd(dn, y * sz, wn)
        live = _valid_rows(r0, dn.shape[0], PAD)
        return [jnp.where(live, dyz * sz, 0.0), jnp.where(live, dyz * y * _dsilu(z), 0.0), dw]

    dy_ssd, dp, g["ssm_norm"] = run(
        "ssm_gate_norm_bwd", _rowwise, "ssm_gate_norm_bwd", gate_norm_bwd, n_rows, tm,
        [(dyn, D_INNER, 0), (y_ssd, D_INNER, 0), (p, D_INNER, P_Z // D_INNER)],
        [w["ssm_norm"]], [(D_INNER, F32), (D_INNER, BF16, "into", dp, P_Z // D_INNER)], [D_INNER])
    dp, dsink = run("attn_bwd", _attn_bwd, p, sinks, ao, lse, dao, dp, n_chunks)
    g["attn_sinks"] = dsink[:, 0:ATTN_HEADS]
    dxs, dbm, dcm, ddt_parts, dpar = run("ssd_bwd", _ssd_bwd, p, w["ssm_conv_w"], w["ssm_conv_b"], dt_bias, a_log,
                                         d_skip, hin, dy_ssd, n_chunks)
    dpar = jnp.sum(dpar, axis=0)
    g["ssm_dt_bias"], g["ssm_a_log"], g["ssm_d_skip"] = (dpar[i:i + 1, 0:SSM_HEADS] for i in range(3))

    def dt_grad(r0, parts):
        tot = parts[:, 0:128] + parts[:, 128:256] + parts[:, 256:384] + parts[:, 384:512]
        return [jnp.concatenate([tot, jnp.zeros((parts.shape[0], P_Z - P_DT - 128), F32)], axis=1)]

    dt_w = P_Z - P_DT
    dp, = _rowwise("dt_grad", dt_grad, n_rows, tm, [(ddt_parts, SSM_GROUPS * 128, 0)], [],
                   [(dt_w, BF16, "into", dp, P_DT // dt_w)], [])
    x_chunks = [(src, c0) for src, arr in enumerate((dxs, dbm, dcm)) for c0 in range(0, arr.shape[1], CONV_LANES)]
    dp, dconv = run("ssm_conv_bwd", _conv_bwd, "ssm_conv_bwd", p, P_XBC // CONV_DIM, [dxs, dbm, dcm], x_chunks,
                    w["ssm_conv_w"], w["ssm_conv_b"], n_rows, False, into=dp, into_blk=P_XBC // CONV_DIM)
    g["ssm_conv_w"], g["ssm_conv_b"] = dconv[0:4], dconv[4:5]
    g["w_cat"] = _matmul("in_proj_dw", n1, dp, "tn", F32)
    dn1 = run("in_proj_dx", _matmul, "in_proj_dx", dp, w["w_cat"], "nt", F32)

    def pre_mix_bwd(r0, dn, d1, xt, hd, wn):
        dx, dw = _rms_bwd(dn, h0_tile(r0, xt, hd), wn)
        dh0 = jnp.where(_valid_rows(r0, dn.shape[0], PAD), dx + d1, 0.0)
        return [dh0, dh0, dw]

    dx_out, dhead, g["norm_pre_mix"] = _rowwise(
        "pre_mix_bwd", pre_mix_bwd, n_rows, T, [(dn1, D_MODEL, 0), (dh1, D_MODEL, 0)] + x_in,
        [w["norm_pre_mix"]], [(D_MODEL, F32, "prev", n_tok), (D_MODEL, F32, "first")], [D_MODEL])
    return jnp.sum(loss_cols), dx_out, dhead


_IN_SECTIONS = [((5152, 6176), P_Q), ((6176, 6432), P_K), ((6432, 6688), P_V), ((5120, 5152), P_DT),
                ((0, 2048), P_Z), ((6688, 8736), P_GATE), ((2048, 5120), P_XBC)]


def _to_cat(w_in):
    parts, at = [], 0
    for (a, b), off in _IN_SECTIONS:
        if off > at:
            parts.append(jnp.zeros((w_in.shape[0], off - at), w_in.dtype))
        parts.append(w_in[:, a:b])
        at = off + (b - a)
    return jnp.concatenate(parts, axis=1)


def _from_cat(g_cat):
    pieces = {a: g_cat[:, off:off + (b - a)] for (a, b), off in _IN_SECTIONS}
    return jnp.concatenate([pieces[a] for a in sorted(pieces)], axis=1)


LANES = 1024
_BIG = [("w_in", 1024, 2184, "chip"), ("w_ssm_out", 512, 1024, "row"), ("w_attn_out", 256, 1024, "row"),
        ("w_mix_out", 256, 1024, "row"), ("w_ffn_up", 1024, 1408, "col"), ("w_ffn_down", 704, 1024, "row"),
        ("small", 32, LANES, "chip")]
_SMALL_SHARDED = [("ssm_conv_w", (4, 768), 1), ("ffn_conv_w", (3, 1408), 1), ("meta_tokens", (16, 256), 1)]
_REPLICATED = [("norm_pre_mix", 1024), ("ssm_conv_b", 3072), ("ssm_dt_bias", 32), ("ssm_a_log", 32),
               ("ssm_d_skip", 32), ("ssm_norm", 2048), ("attn_sinks", 16), ("norm_post_mix", 1024),
               ("norm_pre_ffn", 1024), ("ffn_conv_b", 5632), ("norm_post_ffn", 1024)]
SMALL_ROWS = 16
WEIGHT_ORDER = ["meta_tokens", "norm_pre_mix", "w_in", "ssm_conv_w", "ssm_conv_b", "ssm_dt_bias", "ssm_a_log",
                "ssm_d_skip", "ssm_norm", "w_ssm_out", "attn_sinks", "w_attn_out", "w_mix_out", "norm_post_mix",
                "norm_pre_ffn", "w_ffn_up", "ffn_conv_w", "ffn_conv_b", "w_ffn_down", "norm_post_ffn"]


def _flatten(parts, rows):
    flat = jnp.concatenate([a.reshape(-1) for a in parts])
    return jnp.pad(flat, (0, rows * LANES - flat.shape[0])).reshape(rows, LANES)


def _unflatten(flat, shapes):
    flat = flat.reshape(-1)
    out, off = [], 0
    for shp in shapes:
        n = math.prod(shp)
        out.append(flat[off:off + n].reshape(shp))
        off += n
    return out


def _shard_of(full, chip, shape, axis):
    return lax.slice_in_dim(full, chip * shape[axis], (chip + 1) * shape[axis], axis=axis)


def _full_shape(r, c, layout):
    return {"row": (4 * r, c), "col": (r, 4 * c), "chip": (4, r, c)}[layout]


def _shard_view(ref, r, c, layout, chip):
    if layout == "row":
        return ref.at[pl.ds(pl.multiple_of(chip * r, 16), r), :]
    if layout == "col":
        return ref.at[:, pl.ds(pl.multiple_of(chip * c, 128), c)]
    return ref.at[chip]


def _half_view(ref, r, c, layout, chip, half):
    hr = r // 2
    if layout == "row":
        return ref.at[pl.ds(pl.multiple_of(chip * r + half * hr, 16), hr), :]
    r0 = pl.multiple_of(half * hr, 16)
    if layout == "col":
        return ref.at[pl.ds(r0, hr), pl.ds(pl.multiple_of(chip * c, 128), c)]
    return ref.at[chip, pl.ds(r0, hr), :]


def _mesh_pos():
    return lax.axis_index("x"), lax.axis_index("y"), lax.axis_index("c")


def _other_chips(x, y):
    return [(1 - x, y), (x, 1 - y), (1 - x, 1 - y)]


def _chip_index(x, y):
    return 2 * x + y


def _run_exchange(name, ex):
    n_in, n_out = len(ex.ins), len(ex.out_shapes)

    def body(*refs):
        in_refs, out_refs = refs[:n_in], refs[n_in:n_in + n_out]
        send_sems, recv_sems = refs[n_in + n_out:]
        copies = [pltpu.make_async_remote_copy(src_ref=s, dst_ref=d, send_sem=send_sems.at[i], recv_sem=recv_sems.at[i],
                                               device_id=dev, device_id_type=MESH)
                  for i, (s, d, dev) in enumerate(ex.make_copies(in_refs, out_refs))]
        assert len(copies) == ex.n_copies
        for cp in copies:
            cp.start()
        for cp in copies:
            cp.wait()

    return pl.pallas_call(
        body, name=name, in_specs=[ANY] * n_in, out_specs=[ANY] * n_out, out_shape=list(ex.out_shapes),
        scratch_shapes=[pltpu.SemaphoreType.DMA((ex.n_copies,)), pltpu.SemaphoreType.DMA((ex.n_copies,))],
        compiler_params=pltpu.CompilerParams(has_side_effects=True),
    )(*ex.ins)


def _join(*exs):
    def make(in_refs, out_refs):
        copies, i0, o0 = [], 0, 0
        for ex in exs:
            copies += ex.make_copies(in_refs[i0:i0 + len(ex.ins)], out_refs[o0:o0 + len(ex.out_shapes)])
            i0, o0 = i0 + len(ex.ins), o0 + len(ex.out_shapes)
        return copies

    return _Exchange([a for ex in exs for a in ex.ins], [s for ex in exs for s in ex.out_shapes], make,
                     sum(ex.n_copies for ex in exs))


def _split(exs, results):
    out, o0 = [], 0
    for ex in exs:
        out.append(list(results[o0:o0 + len(ex.out_shapes)]))
        o0 += len(ex.out_shapes)
    return out


def _gather_ici(entries, shards):
    def make(in_refs, out_refs):
        x, y, c = _mesh_pos()
        j = _chip_index(x, y)
        copies = []
        for ref_in, ref_out, (_, r, cc, lay) in zip(in_refs, out_refs, entries):
            copies.append((ref_in, _shard_view(ref_out, r, cc, lay, j), None))
            mine = ref_in.at[pl.ds(pl.multiple_of(c * (r // 2), 16), r // 2), :]
            copies += [(mine, _half_view(ref_out, r, cc, lay, j, c), (*ch, c)) for ch in _other_chips(x, y)]
        return copies

    shapes = [jax.ShapeDtypeStruct(_full_shape(r, cc, lay), s.dtype) for s, (_, r, cc, lay) in zip(shards, entries)]
    return _Exchange(list(shards), shapes, make, 4 * len(entries))


def _gather_pass_on(entries, fulls):
    def make(in_refs, out_refs):
        x, y, c = _mesh_pos()
        copies = []
        for ref, (_, r, cc, lay) in zip(out_refs, entries):
            for ch in _other_chips(x, y):
                landed = _half_view(ref, r, cc, lay, _chip_index(*ch), c)
                copies.append((landed, landed, (x, y, 1 - c)))
        return copies

    return _Exchange(list(fulls), [jax.ShapeDtypeStruct(f.shape, f.dtype) for f in fulls], make, 3 * len(entries),
                     {a: a for a in range(len(entries))})


def _gather_weights(entries, shards):
    n = len(entries)

    def body(*refs):
        ins, outs = refs[:n], refs[n:2 * n]
        send_sems, recv_sems, local_sems = refs[2 * n:]
        x, y, c = _mesh_pos()
        j = _chip_index(x, y)
        sibling = (x, y, 1 - c)
        chips = _other_chips(x, y)
        idx = [_chip_index(*ch) for ch in chips]

        def remote(k, src, dst, dev):
            return pltpu.make_async_remote_copy(src_ref=src, dst_ref=dst, send_sem=send_sems.at[k],
                                                recv_sem=recv_sems.at[k], device_id=dev, device_id_type=MESH)

        own = [pltpu.make_async_copy(ins[a], _shard_view(outs[a], r, cc, lay, j), local_sems.at[a])
               for a, (_, r, cc, lay) in enumerate(entries)]
        for cp in own:
            cp.start()
        first, passed = [], []
        for a, (_, r, cc, lay) in enumerate(entries):
            mine = ins[a].at[pl.ds(pl.multiple_of(c * (r // 2), 16), r // 2), :]
            for k, ch in enumerate(chips):
                first.append(remote(6 * a + k, mine, _half_view(outs[a], r, cc, lay, j, c), (*ch, c)))
                landed = _half_view(outs[a], r, cc, lay, idx[k], c)
                passed.append(remote(6 * a + 3 + k, landed, landed, sibling))
        for cp in first:
            cp.start()
        for a, (_, r, cc, lay) in enumerate(entries):
            for k in range(3):
                landed = _half_view(outs[a], r, cc, lay, idx[k], c)
                remote(6 * a + k, landed, landed, sibling).wait_recv()
                passed[3 * a + k].start()
        for a, (_, r, cc, lay) in enumerate(entries):
            for k in range(3):
                theirs = _half_view(outs[a], r, cc, lay, idx[k], 1 - c)
                remote(6 * a + 3 + k, theirs, theirs, sibling).wait_recv()
        for cp in first + passed:
            cp.wait_send()
        for cp in own:
            cp.wait()

    return pl.pallas_call(
        body, name="gather_weights", in_specs=[ANY] * n, out_specs=[ANY] * n,
        out_shape=[jax.ShapeDtypeStruct(_full_shape(r, cc, lay), s.dtype) for s, (_, r, cc, lay) in zip(shards, entries)],
        scratch_shapes=[pltpu.SemaphoreType.DMA((6 * n,)), pltpu.SemaphoreType.DMA((6 * n,)), pltpu.SemaphoreType.DMA((n,))],
        compiler_params=pltpu.CompilerParams(has_side_effects=True),
    )(*shards)


def _pair_exchange(entries, grads):
    def make(in_refs, out_refs):
        x, y, c = _mesh_pos()
        return [(_half_view(ref_in, r, cc, lay, i, 1 - c), ref_out.at[i], (x, y, 1 - c))
                for ref_in, ref_out, (_, r, cc, lay) in zip(in_refs, out_refs, entries) for i in range(4)]

    return _Exchange(list(grads), [jax.ShapeDtypeStruct((4, r // 2, cc), F32) for _, r, cc, _ in entries], make,
                     4 * len(entries))


def _whole_to_sibling(arrays):
    def make(in_refs, out_refs):
        x, y, c = _mesh_pos()
        return [(r, o, (x, y, 1 - c)) for r, o in zip(in_refs, out_refs)]

    return _Exchange(list(arrays), [jax.ShapeDtypeStruct(a.shape, a.dtype) for a in arrays], make, len(arrays))


def _chip_exchange(psends):
    def make(in_refs, out_refs):
        x, y, c = _mesh_pos()
        return [(ref_in.at[_chip_index(*ch)], ref_out.at[k], (*ch, c))
                for ref_in, ref_out in zip(in_refs, out_refs) for k, ch in enumerate(_other_chips(x, y))]

    return _Exchange(list(psends), [jax.ShapeDtypeStruct((3,) + p.shape[1:], p.dtype) for p in psends], make,
                     3 * len(psends))


def _to_all_chips(array):
    def make(in_refs, out_refs):
        x, y, c = _mesh_pos()
        return [(in_refs[0], out_refs[0].at[k], (*ch, c)) for k, ch in enumerate(_other_chips(x, y))]

    return _Exchange([array], [jax.ShapeDtypeStruct((3,) + array.shape, array.dtype)], make, 3)


SUM_ROWS = 256
ADAM_ROWS = 128


def _pair_sum(name, grad, recv, ids, r, c, layout):
    hr = r // 2
    tr = _row_tile(hr, SUM_ROWS)
    nb = hr // tr

    def body(ids_ref, g_ref, r_ref, send_ref, own_ref):
        s = g_ref[...] + r_ref[...]
        send_ref[...] = s.astype(send_ref.dtype)

        @pl.when(pl.program_id(1) == ids_ref[1])
        def _():
            own_ref[...] = s

    if layout == "row":
        g_spec = pl.BlockSpec((tr, c), lambda t, j, ids_ref: ((j * r + ids_ref[0] * hr) // tr + t, 0))
    elif layout == "col":
        g_spec = pl.BlockSpec((tr, c), lambda t, j, ids_ref: (ids_ref[0] * nb + t, j))
    else:
        g_spec = pl.BlockSpec((None, tr, c), lambda t, j, ids_ref: (j, ids_ref[0] * nb + t, 0))
    grid_spec = pltpu.PrefetchScalarGridSpec(
        num_scalar_prefetch=1, grid=(nb, 4),
        in_specs=[g_spec, pl.BlockSpec((None, tr, c), lambda t, j, ids_ref: (j, t, 0))],
        out_specs=[pl.BlockSpec((None, tr, c), lambda t, j, ids_ref: (j, t, 0)),
                   pl.BlockSpec((tr, c), lambda t, j, ids_ref: (t, 0))])
    return pl.pallas_call(
        body, name=name, grid_spec=grid_spec,
        out_shape=[jax.ShapeDtypeStruct((4, hr, c), BF16), jax.ShapeDtypeStruct((hr, c), F32)],
        compiler_params=_cparams(2),
    )(ids, grad, recv)


def _chip_sum(name, own, recv):
    hr, c = own.shape
    tr = _row_tile(hr, SUM_ROWS)

    def body(o_ref, r_ref, out_ref):
        out_ref[...] = ((o_ref[...] + r_ref[0].astype(F32)) + r_ref[1].astype(F32)) + r_ref[2].astype(F32)

    return pl.pallas_call(
        body, name=name, grid=(hr // tr,),
        in_specs=[pl.BlockSpec((tr, c), lambda i: (i, 0)), pl.BlockSpec((3, tr, c), lambda i: (0, i, 0))],
        out_specs=pl.BlockSpec((tr, c), lambda i: (i, 0)),
        out_shape=jax.ShapeDtypeStruct((hr, c), F32), compiler_params=_cparams(1),
    )(own, recv)


def _chip_sum_small(own, recv, ids):
    def body(ids_ref, o_ref, r_ref, out_ref):
        j = ids_ref[1]
        total = None
        for i in range(4):
            m = jnp.bitwise_xor(i, j)
            term = jnp.where(m == 0, o_ref[...], jnp.where(m == 2, r_ref[0], jnp.where(m == 1, r_ref[1], r_ref[2])))
            total = term if total is None else total + term
        out_ref[...] = total

    grid_spec = pltpu.PrefetchScalarGridSpec(
        num_scalar_prefetch=1, grid=(1,),
        in_specs=[pl.BlockSpec(own.shape, lambda i, ids_ref: (0, 0)), pl.BlockSpec(recv.shape, lambda i, ids_ref: (0, 0, 0))],
        out_specs=pl.BlockSpec(own.shape, lambda i, ids_ref: (0, 0)))
    return pl.pallas_call(body, name="chip_sum_small", grid_spec=grid_spec,
                          out_shape=jax.ShapeDtypeStruct(own.shape, F32), compiler_params=_cparams(1))(ids, own, recv)


def _adamw(name, w, m, v, mine, theirs, ids):
    rows, cols = w.shape
    half = rows // 2
    tr = _row_tile(half, ADAM_ROWS, unit=8)
    nb = half // tr
    c1 = 1.0 / (1.0 - ADAM_B1 ** ADAM_STEP)
    c2 = 1.0 / (1.0 - ADAM_B2 ** ADAM_STEP)

    def body(ids_ref, w_ref, m_ref, v_ref, mine_ref, theirs_ref, g_out, d_out, m_out, v_out):
        g = jnp.where(pl.program_id(0) == ids_ref[0], mine_ref[...], theirs_ref[...])
        m_new = ADAM_B1 * m_ref[...] + (1.0 - ADAM_B1) * g
        v_new = ADAM_B2 * v_ref[...] + (1.0 - ADAM_B2) * (g * g)
        d_out[...] = -ADAM_LR * ((m_new * c1) / (jnp.sqrt(v_new * c2) + ADAM_EPS) + ADAM_WD * w_ref[...])
        g_out[...] = g
        m_out[...] = m_new
        v_out[...] = v_new

    full = pl.BlockSpec((tr, cols), lambda h, i, ids_ref: (h * nb + i, 0))
    part = pl.BlockSpec((tr, cols), lambda h, i, ids_ref: (i, 0))
    grid_spec = pltpu.PrefetchScalarGridSpec(num_scalar_prefetch=1, grid=(2, nb),
                                             in_specs=[full, full, full, part, part], out_specs=[full] * 4)
    return pl.pallas_call(
        body, name=name, grid_spec=grid_spec,
        out_shape=[jax.ShapeDtypeStruct((rows, cols), F32)] * 4, compiler_params=_cparams(2),
    )(ids, w, m, v, mine, theirs)


def _small_shard(parts):
    return _flatten(parts, _BIG[-1][1])


_ENTRY = {e[0]: e for e in _BIG}
LATE_WEIGHTS = ("w_ssm_out", "w_attn_out", "w_mix_out", "w_ffn_up", "w_ffn_down")
FFN_GRADS = ("w_ffn_down", "w_ffn_up")
MIXER_GRADS = ("w_mix_out", "w_ssm_out", "w_attn_out")


class _StepPlan:
    def __init__(self, w, late_shards, shards, ids):
        self.w, self.g = w, {}
        self.late_shards, self.shards, self.ids = late_shards, shards, ids
        self.sums, self.halves, self.results = {}, {}, {}

    def run(self, name, fn, *args, **kw):
        at = getattr(self, "_at_" + name, None)
        if at is None:
            return fn(*args, **kw)
        exchange, landed = at()
        res, extra = fn(*args, bg=exchange, **kw)
        landed(extra)
        return res

    def _at_ssd_fwd(self):
        entries = [_ENTRY[n] for n in LATE_WEIGHTS]

        def landed(fulls):
            self.partly_gathered = fulls

        return _gather_ici(entries, self.late_shards), landed

    def _at_attn_fwd(self):
        entries = [_ENTRY[n] for n in LATE_WEIGHTS]
        return _gather_pass_on(entries, self.partly_gathered), lambda fulls: self.w.update(zip(LATE_WEIGHTS, fulls))

    def pair_sums(self, names, grads, recv):
        for n, gr, rv in zip(names, grads, recv):
            _, r, c, lay = _ENTRY[n]
            self.sums[n] = _pair_sum("pair_sum_" + n, gr, rv, self.ids, r, c, lay)

    def chip_sums(self, names, recv):
        for n, rv in zip(names, recv):
            self.halves[n] = _chip_sum("chip_sum_" + n, self.sums[n][1], rv)

    def adamw(self, names, theirs):
        for n, th in zip(names, theirs):
            sh = self.shards[n]
            self.results[n] = _adamw("adamw_" + n, sh["w"], sh["m"], sh["v"], self.halves[n], th, self.ids)

    def _pair_stage(self, names, grads):
        return (_pair_exchange([_ENTRY[n] for n in names], grads),
                lambda recv: self.pair_sums(names, grads, recv))

    def _at_ffn_up_dx(self):
        return self._pair_stage(FFN_GRADS, [self.g[n] for n in FFN_GRADS])

    def _at_ssm_gate_norm_bwd(self):
        return self._pair_stage(MIXER_GRADS, [self.g[n] for n in MIXER_GRADS])

    def _at_attn_bwd(self):
        return _chip_exchange([self.sums[n][0] for n in FFN_GRADS]), lambda recv: self.chip_sums(FFN_GRADS, recv)

    def _at_ssd_bwd(self):
        stages = (_chip_exchange([self.sums[n][0] for n in MIXER_GRADS]),
                  _whole_to_sibling([self.halves[n] for n in FFN_GRADS]))

        def landed(extra):
            recv, theirs = _split(stages, extra)
            self.chip_sums(MIXER_GRADS, recv)
            self.adamw(FFN_GRADS, theirs)

        return _join(*stages), landed

    def _at_ssm_conv_bwd(self):
        return _whole_to_sibling([self.halves[n] for n in MIXER_GRADS]), lambda theirs: self.adamw(MIXER_GRADS, theirs)

    def _at_in_proj_dx(self):
        g_in = _from_cat(self.g.pop("w_cat")).reshape(D_MODEL, 4, N_IN // 4)
        return self._pair_stage(("w_in",), [jnp.transpose(g_in, (1, 0, 2))])

    def finish(self, g_small, g_rep, rep_shards):
        stages = (_pair_exchange([_ENTRY["small"]], [g_small]), _whole_to_sibling([g_rep]))
        recv_small, recv_rep = _split(stages, _run_exchange("grad_pair_exchange_tail", _join(*stages)))
        self.pair_sums(("small",), [g_small], recv_small)
        p_rep, = _rowwise("pair_sum_replicated", lambda r0, a, b: [a + b], SMALL_ROWS, SMALL_ROWS,
                          [(g_rep, LANES, 0), (recv_rep[0], LANES, 0)], [], [(LANES, F32)], [])
        last = ("w_in", "small")
        stages = (_chip_exchange([self.sums[n][0] for n in last]), _to_all_chips(p_rep))
        recv, recv_rep = _split(stages, _run_exchange("grad_chip_exchange_tail", _join(*stages)))
        self.chip_sums(last, recv)
        g_rep_tot = _chip_sum_small(p_rep, recv_rep[0], self.ids)
        self.adamw(last, _run_exchange("grad_half_share_tail", _whole_to_sibling([self.halves[n] for n in last])))
        ids_lo = self.ids * jnp.array([0, 1], jnp.int32)
        self.results["replicated"] = _adamw("adamw_replicated", rep_shards["w"], rep_shards["m"], rep_shards["v"],
                                            g_rep_tot[0:SMALL_ROWS // 2], g_rep_tot[SMALL_ROWS // 2:], ids_lo)


def kernel(x, meta_tokens, norm_pre_mix, w_in, ssm_conv_w, ssm_conv_b, ssm_dt_bias, ssm_a_log, ssm_d_skip, ssm_norm, w_ssm_out, attn_sinks, w_attn_out, w_mix_out, norm_post_mix, norm_pre_ffn, w_ffn_up, ffn_conv_w, ffn_conv_b, w_ffn_down, norm_post_ffn, loss_target, m_meta_tokens, m_norm_pre_mix, m_w_in, m_ssm_conv_w, m_ssm_conv_b, m_ssm_dt_bias, m_ssm_a_log, m_ssm_d_skip, m_ssm_norm, m_w_ssm_out, m_attn_sinks, m_w_attn_out, m_w_mix_out, m_norm_post_mix, m_norm_pre_ffn, m_w_ffn_up, m_ffn_conv_w, m_ffn_conv_b, m_w_ffn_down, m_norm_post_ffn, v_meta_tokens, v_norm_pre_mix, v_w_in, v_ssm_conv_w, v_ssm_conv_b, v_ssm_dt_bias, v_ssm_a_log, v_ssm_d_skip, v_ssm_norm, v_w_ssm_out, v_attn_sinks, v_w_attn_out, v_w_mix_out, v_norm_post_mix, v_norm_pre_ffn, v_w_ffn_up, v_ffn_conv_w, v_ffn_conv_b, v_w_ffn_down, v_norm_post_ffn):
    args = dict(locals())
    squeeze = lambda a: a.reshape(a.shape[-2:])
    wts = {n: squeeze(args[n]) for n in WEIGHT_ORDER}
    mom = {n: squeeze(args["m_" + n]) for n in WEIGHT_ORDER}
    var = {n: squeeze(args["v_" + n]) for n in WEIGHT_ORDER}
    x_i, y_i, c_i = _mesh_pos()
    ids = jnp.stack([c_i, _chip_index(x_i, y_i)]).astype(jnp.int32)
    big_names = [n for n, _, _, _ in _BIG[:-1]]
    small_names = [n for n, _, _ in _SMALL_SHARDED]
    rep_names = [n for n, _ in _REPLICATED]

    stacks = {"w": wts, "m": mom, "v": var}
    shards = {n: {k: d[n] for k, d in stacks.items()} for n in big_names}
    shards["small"] = {k: _small_shard([d[n] for n in small_names]) for k, d in stacks.items()}
    rep_shards = {k: _flatten([d[n] for n in rep_names], SMALL_ROWS) for k, d in stacks.items()}

    w_in4, small_all = _gather_weights([_ENTRY["w_in"], _ENTRY["small"]], [wts["w_in"].astype(BF16), shards["small"]["w"]])
    w = {n: wts[n] for n in rep_names}
    w["w_cat"] = _to_cat(jnp.transpose(w_in4, (1, 0, 2)).reshape(D_MODEL, N_IN))
    small_parts = [_unflatten(small_all[i], [shp for _, shp, _ in _SMALL_SHARDED]) for i in range(4)]
    for k, (n, _, axis) in enumerate(_SMALL_SHARDED):
        w[n] = jnp.concatenate([small_parts[i][k] for i in range(4)], axis=axis)
    plan = _StepPlan(w, [wts[n].astype(BF16) for n in LATE_WEIGHTS], shards, ids)

    head = jnp.concatenate([jnp.zeros((PAD, D_MODEL), F32), w["meta_tokens"]], axis=0)
    loss_sum, dx, dhead = _local_step(x[0], head, loss_target[0], plan)
    loss = lax.psum(loss_sum * (0.5 / D_MODEL), ("x", "y", "c"))
    g = plan.g
    g["meta_tokens"] = dhead[PAD:]
    g_small = jnp.stack([_small_shard([_shard_of(g[n], i, shp, ax) for n, shp, ax in _SMALL_SHARDED]) for i in range(4)])
    plan.finish(g_small, _flatten([g[n] for n in rep_names], SMALL_ROWS), rep_shards)

    results = {}
    for kind in range(4):
        results.update({(kind, n): plan.results[n][kind] for n in big_names})
        parts = _unflatten(plan.results["small"][kind], [shp for _, shp, _ in _SMALL_SHARDED])
        results.update({(kind, n): parts[k] for k, n in enumerate(small_names)})
        parts = _unflatten(plan.results["replicated"][kind], [(1, width) for _, width in _REPLICATED])
        results.update({(kind, n): parts[k] for k, n in enumerate(rep_names)})
    outs = [results[kind, n].reshape(args[n].shape) for kind in range(4) for n in WEIGHT_ORDER]
    return (loss, dx[None], *outs)
```

```python
import math
from typing import Any, Callable, NamedTuple, Sequence

import jax
import jax.numpy as jnp
from jax import lax
from jax.experimental import pallas as pl
from jax.experimental.pallas import tpu as pltpu

F32 = jnp.float32
BF16 = jnp.bfloat16

D_MODEL = 1024
N_META = 16
T = 128
PAD = T - N_META
D_INNER = 2048
SSM_HEADS = 32
HEAD_P = 64
SSM_GROUPS = 4
GROUP_W = D_INNER // SSM_GROUPS
D_STATE = 128
CONV_DIM = D_INNER + 2 * SSM_GROUPS * D_STATE
ATTN_HEADS = 16
KV_HEADS = 4
ATTN_W = 1024
KV_W = 256
FFN_DIM = 2816
N_IN = 8736
EPS = 1e-6
NEG = -1e30
SCALE = 0.125

P_Q, P_K, P_V, P_DT, P_Z, P_GATE, P_XBC = 0, 1024, 1280, 1536, 2048, 4096, 6144
QKV_W = 1536
P_W = 9216

ADAM_LR, ADAM_B1, ADAM_B2, ADAM_EPS, ADAM_WD, ADAM_STEP = 0.001, 0.9, 0.999, 1e-08, 0.01, 10

VMEM_BUDGET = 40 * 1024 * 1024
VMEM_LIMIT = 56 * 1024 * 1024
MESH = pl.DeviceIdType.MESH
ANY = pl.BlockSpec(memory_space=pl.ANY)


def _cparams(n_axes, **kw):
    return pltpu.CompilerParams(dimension_semantics=("arbitrary",) * n_axes, vmem_limit_bytes=VMEM_LIMIT, **kw)


class _Exchange(NamedTuple):
    ins: Sequence[Any]
    out_shapes: Sequence[Any]
    make_copies: Callable
    n_copies: int
    aliases: dict = {}


def _call(body, name, grid, in_specs, out_specs, out_shape, operands, scratch_shapes=(), aliases=None, bg=None):
    aliases = dict(aliases or {})
    if bg is None:
        return pl.pallas_call(body, name=name, grid=grid, in_specs=in_specs, out_specs=out_specs, out_shape=out_shape,
                              scratch_shapes=list(scratch_shapes), input_output_aliases=aliases,
                              compiler_params=_cparams(len(grid)))(*operands)
    n_in, n_out, n_scr = len(in_specs), len(out_specs), len(scratch_shapes)
    nb_in, nb_out = len(bg.ins), len(bg.out_shapes)

    def hosted(*refs):
        ins, bg_ins = refs[:n_in], refs[n_in:n_in + nb_in]
        outs = refs[n_in + nb_in:n_in + nb_in + n_out]
        bg_outs = refs[n_in + nb_in + n_out:n_in + nb_in + n_out + nb_out]
        scratch = refs[n_in + nb_in + n_out + nb_out:n_in + nb_in + n_out + nb_out + n_scr]
        send_sems, recv_sems = refs[-2:]
        pids = [pl.program_id(a) for a in range(len(grid))]
        first, last = pids[0] == 0, pids[0] == grid[0] - 1
        for p, g in zip(pids[1:], grid[1:]):
            first, last = first & (p == 0), last & (p == g - 1)
        copies = []
        for k, (src, dst, peer) in enumerate(bg.make_copies(bg_ins, bg_outs)):
            if peer is None:
                copies.append(pltpu.make_async_copy(src, dst, send_sems.at[k]))
            else:
                copies.append(pltpu.make_async_remote_copy(src_ref=src, dst_ref=dst, send_sem=send_sems.at[k],
                                                           recv_sem=recv_sems.at[k], device_id=peer, device_id_type=MESH))
        assert len(copies) == bg.n_copies

        @pl.when(first)
        def _():
            for cp in copies:
                cp.start()

        body(*ins, *outs, *scratch)

        @pl.when(last)
        def _():
            for cp in copies:
                cp.wait()

    aliases = {(k if k < n_in else k + nb_in): v for k, v in aliases.items()}
    aliases.update({n_in + k: n_out + v for k, v in bg.aliases.items()})
    res = pl.pallas_call(
        hosted, name=name, grid=grid, in_specs=list(in_specs) + [ANY] * nb_in, out_specs=list(out_specs) + [ANY] * nb_out,
        out_shape=list(out_shape) + list(bg.out_shapes), input_output_aliases=aliases,
        scratch_shapes=list(scratch_shapes) + [pltpu.SemaphoreType.DMA((bg.n_copies,))] * 2,
        compiler_params=_cparams(len(grid), has_side_effects=True))(*operands, *bg.ins)
    return res[:n_out], res[n_out:]


def _sigmoid(x):
    return 1.0 / (1.0 + jnp.exp(-x))


def _silu(x):
    return x * _sigmoid(x)


def _dsilu(x):
    s = _sigmoid(x)
    return s * (1.0 + x * (1.0 - s))


def _softplus(x):
    e = jnp.exp(-jnp.abs(x))
    small = e * (1.0 - e * (0.5 - e * (1.0 / 3.0)))
    return jnp.maximum(x, 0.0) + jnp.where(e < 0.01, small, jnp.log(1.0 + e))


def _rms(x, w):
    r = lax.rsqrt(jnp.mean(x * x, axis=-1, keepdims=True) + EPS)
    return x * r * w


def _rms_bwd(dy, x, w):
    r = lax.rsqrt(jnp.mean(x * x, axis=-1, keepdims=True) + EPS)
    xh = x * r
    g = dy * w
    dx = r * (g - xh * jnp.mean(g * xh, axis=-1, keepdims=True))
    dw = jnp.sum(dy * xh, axis=0, keepdims=True)
    return dx, dw


def _dot(a, b):
    return jnp.dot(a, b, preferred_element_type=F32)


def _dot_nt(a, b):
    return lax.dot_general(a, b, (((1,), (1,)), ((), ())), preferred_element_type=F32)


def _dot_tn(a, b):
    return lax.dot_general(a, b, (((0,), (0,)), ((), ())), preferred_element_type=F32)


def _split3(x):
    hi = x.astype(BF16)
    r = x - hi.astype(F32)
    mid = r.astype(BF16)
    lo = (r - mid.astype(F32)).astype(BF16)
    return hi, mid, lo


def _xdot(x, e):
    hi, mid, lo = _split3(x)
    return _dot(hi, e) + _dot(mid, e) + _dot(lo, e)


def _xdot_l(e, x):
    hi, mid, lo = _split3(x)
    return _dot(e, hi) + _dot(e, mid) + _dot(e, lo)


def _iota(shape, dim):
    return lax.broadcasted_iota(jnp.int32, shape, dim)


def _divisors(n, unit):
    return [t for t in range(unit, n + 1, unit) if n % t == 0]


MIN_MATMUL_STEPS = 8


def _matmul_tiles(m, n, k, a_bytes, b_bytes, o_bytes, m_unit):
    best = None
    for tm in _divisors(m, m_unit):
        for tn in _divisors(n, 128):
            for tk in _divisors(k, 128):
                acc = 0 if tk == k else tm * tn * 4
                vm = 2 * (tm * tk * a_bytes + tk * tn * b_bytes + tm * tn * o_bytes) + acc
                if vm > VMEM_BUDGET:
                    continue
                steps = (m // tm) * (n // tn) * (k // tk)
                score = (tk == k, min(steps, MIN_MATMUL_STEPS), min(tm, 256), tm * tn * tk)
                if best is None or score > best[0]:
                    best = (score, (tm, tn, tk))
    return best[1]


def _matmul(name, a, b, mode, out_dtype, bg=None):
    if mode == "nn":
        (m, k), n = a.shape, b.shape[1]
    elif mode == "nt":
        (m, k), n = a.shape, b.shape[0]
    else:
        (k, m), n = a.shape, b.shape[1]
    ab, bb, ob = a.dtype.itemsize, b.dtype.itemsize, jnp.dtype(out_dtype).itemsize
    tm, tn, tk = _matmul_tiles(m, n, k, ab, bb, ob, 128 if mode == "tn" else 16)
    nk = k // tk
    dot = {"nn": _dot, "nt": _dot_nt, "tn": _dot_tn}[mode]

    def body(a_ref, b_ref, o_ref, *scratch):
        prod = dot(a_ref[...].astype(BF16), b_ref[...].astype(BF16))
        if nk == 1:
            o_ref[...] = prod.astype(o_ref.dtype)
        else:
            acc_ref, = scratch
            kk = pl.program_id(2)

            @pl.when(kk == 0)
            def _():
                acc_ref[...] = prod

            @pl.when(kk > 0)
            def _():
                acc_ref[...] += prod

            @pl.when(kk == nk - 1)
            def _():
                o_ref[...] = acc_ref[...].astype(o_ref.dtype)

    a_spec = pl.BlockSpec((tk, tm), lambda i, j, kk: (kk, i)) if mode == "tn" else pl.BlockSpec((tm, tk), lambda i, j, kk: (i, kk))
    b_spec = pl.BlockSpec((tn, tk), lambda i, j, kk: (j, kk)) if mode == "nt" else pl.BlockSpec((tk, tn), lambda i, j, kk: (kk, j))
    res = _call(body, name, (m // tm, n // tn, nk), [a_spec, b_spec], [pl.BlockSpec((tm, tn), lambda i, j, kk: (i, j))],
                [jax.ShapeDtypeStruct((m, n), out_dtype)], [a, b],
                scratch_shapes=[] if nk == 1 else [pltpu.VMEM((tm, tn), F32)], bg=bg)
    return res[0] if bg is None else (res[0][0], res[1])


def _row_tile(n_rows, cap, unit=16):
    return max(t for t in _divisors(n_rows, unit) if t <= cap)


ROW_SUB = 384
GROUP_UNROLL = 4


def _rowwise(name, fn, n_rows, tm, row_ins, full_ins, row_outs, acc_outs, bg=None):
    n_in = len(row_ins) + len(full_ins)
    n_ro = len(row_outs)
    into = [(k, o[3]) for k, o in enumerate(row_outs) if len(o) > 2 and o[2] == "into"]

    n_row_in = len(row_ins)
    sub = min(tm, ROW_SUB)

    def body(*refs):
        i = pl.program_id(0)
        outs = refs[n_in + len(into):]

        def group(s, sums):
            rows = pl.ds(pl.multiple_of(s * sub, sub), sub)
            vals = [r[rows, :] for r in refs[:n_row_in]] + [r[...] for r in refs[n_row_in:n_in]]
            res = fn(i * tm + s * sub, *vals)
            for o, r, v in zip(row_outs, outs[:n_ro], res[:n_ro]):
                if len(o) > 2 and o[2] == "first":
                    @pl.when(i == 0)
                    def _(r=r, v=v):
                        r[rows, :] = v.astype(r.dtype)
                else:
                    r[rows, :] = v.astype(r.dtype)
            return tuple(a + v for a, v in zip(sums, res[n_ro:]))

        sums = lax.fori_loop(0, tm // sub, group, tuple(jnp.zeros((1, w), F32) for w in acc_outs), unroll=GROUP_UNROLL)

        @pl.when(i == 0)
        def _():
            for r, v in zip(outs[n_ro:], sums):
                r[...] = v

        @pl.when(i > 0)
        def _():
            for r, v in zip(outs[n_ro:], sums):
                r[...] += v

    def in_spec(entry):
        w, cb = entry[1], entry[2]
        if len(entry) > 3 and entry[3] == "prev":
            return pl.BlockSpec((tm, w), lambda i: (jnp.maximum(i - 1, 0), cb))
        if len(entry) > 3 and entry[3] == "first":
            return pl.BlockSpec((tm, w), lambda i: (0, cb))
        return pl.BlockSpec((tm, w), lambda i: (i, cb))

    def out_spec(o):
        if len(o) == 2:
            return pl.BlockSpec((tm, o[0]), lambda i: (i, 0)), jax.ShapeDtypeStruct((n_rows, o[0]), o[1])
        if o[2] == "new":
            return pl.BlockSpec((tm, o[0]), lambda i: (i, o[4])), jax.ShapeDtypeStruct((n_rows, o[3]), o[1])
        if o[2] == "into":
            return pl.BlockSpec((tm, o[0]), lambda i: (i, o[4])), jax.ShapeDtypeStruct(o[3].shape, o[3].dtype)
        if o[2] == "first":
            return pl.BlockSpec((tm, o[0]), lambda i: (0, 0)), jax.ShapeDtypeStruct((tm, o[0]), o[1])
        return pl.BlockSpec((tm, o[0]), lambda i: (jnp.maximum(i - 1, 0), 0)), jax.ShapeDtypeStruct((o[3], o[0]), o[1])

    in_specs = [in_spec(e) for e in row_ins]
    in_specs += [pl.BlockSpec(a.shape, lambda i: (0, 0)) for a in full_ins]
    in_specs += [pl.BlockSpec(memory_space=pl.ANY) for _ in into]
    specs_shapes = [out_spec(o) for o in row_outs]
    out_specs = [s for s, _ in specs_shapes] + [pl.BlockSpec((1, w), lambda i: (0, 0)) for w in acc_outs]
    out_shape = [s for _, s in specs_shapes] + [jax.ShapeDtypeStruct((1, w), F32) for w in acc_outs]
    return _call(body, name, (n_rows // tm,), in_specs, out_specs, out_shape,
                 [e[0] for e in row_ins] + list(full_ins) + [arr for _, arr in into],
                 aliases={n_in + a: k for a, (k, _) in enumerate(into)}, bg=bg)


def _valid_rows(first_row, tm, lo):
    return (first_row + _iota((tm, 1), 0)) >= lo


CONV_ROWS = 128
CONV_SUB = 16
CONV_LANES = 256


def _conv_specs(tm, width, blk, n_rows, after):
    specs = [pl.BlockSpec((tm, width), lambda i: (i, blk)),
             pl.BlockSpec((8, width), lambda i: (jnp.maximum(i * (tm // 8) - 1, 0), blk))]
    if after:
        specs.append(pl.BlockSpec((16, width), lambda i: (jnp.minimum((i + 1) * (tm // 16), n_rows // 16 - 1), blk)))
    return specs


def _conv_window(win, w_ref, b_ref, taps, c0, cw, n):
    acc = b_ref[:, c0:c0 + cw] + w_ref[taps - 1:taps, c0:c0 + cw] * win[8:8 + n]
    for k in range(taps - 1):
        acc = acc + w_ref[k:k + 1, c0:c0 + cw] * win[8 - (taps - 1) + k:8 - (taps - 1) + k + n]
    return acc


def _ffn_act(name, u_raw, conv_w, conv_b, n_rows):
    tm, sub, cw = CONV_ROWS, CONV_SUB, CONV_LANES
    taps, width = conv_w.shape
    half = width // 2

    def body(cur_ref, prev_ref, w_ref, b_ref, f_ref, ext_ref):
        i = pl.program_id(0)
        ext_ref[0:8, :] = jnp.where(i > 0, prev_ref[...], 0.0)
        ext_ref[8:8 + tm, :] = cur_ref[...]
        for q in range(half // cw):
            a0, g0 = q * cw, half + q * cw

            def group(s, carry):
                r = pl.multiple_of(s * sub, sub)
                a = _conv_window(ext_ref[pl.ds(r, sub + 8), a0:a0 + cw], w_ref, b_ref, taps, a0, cw, sub)
                g = _conv_window(ext_ref[pl.ds(r, sub + 8), g0:g0 + cw], w_ref, b_ref, taps, g0, cw, sub)
                f = jnp.where(_valid_rows(i * tm + r, sub, PAD), _silu(a) * g, 0.0)
                f_ref[pl.ds(r, sub), a0:a0 + cw] = f.astype(f_ref.dtype)
                return carry

            lax.fori_loop(0, tm // sub, group, 0, unroll=GROUP_UNROLL)

    return pl.pallas_call(
        body, name=name, grid=(n_rows // tm,),
        in_specs=_conv_specs(tm, width, 0, n_rows, False) + [pl.BlockSpec((taps, width), lambda i: (0, 0)),
                                                             pl.BlockSpec((1, width), lambda i: (0, 0))],
        out_specs=pl.BlockSpec((tm, half), lambda i: (i, 0)),
        out_shape=jax.ShapeDtypeStruct((n_rows, half), BF16),
        scratch_shapes=[pltpu.VMEM((tm + 8, width), F32)],
        compiler_params=_cparams(1),
    )(u_raw, u_raw, conv_w, conv_b)


def _conv_bwd(name, raw, raw_blk, dsrcs, chunk_src, conv_w, conv_b, n_rows, gated, into=None, into_blk=0, bg=None):
    taps, width = conv_w.shape
    half = width // 2 if gated else width
    tm, sub, cw = CONV_ROWS, CONV_SUB, CONV_LANES
    te = tm + 16
    nd = len(dsrcs)
    n_parts = 2 if gated else 1

    def body(*refs):
        cur_ref, prev_ref, next_ref = refs[0:3]
        dcur, dnext = refs[3:3 + nd], refs[3 + nd:3 + 2 * nd]
        w_ref, b_ref = refs[3 + 2 * nd:5 + 2 * nd]
        out_ref, acc_ref, ext_ref, du_ref = refs[-4:]
        i = pl.program_id(0)
        ext_ref[0:8, :] = jnp.where(i > 0, prev_ref[...], 0.0)
        ext_ref[8:8 + tm, :] = cur_ref[...]
        ext_ref[8 + tm:24 + tm, :] = next_ref[...]

        for q, (src, off) in enumerate(chunk_src):
            cols = [q * cw, half + q * cw][:n_parts]

            def conv_grad(r, d):
                pre = [_conv_window(ext_ref[pl.ds(r, sub + 8), c0:c0 + cw], w_ref, b_ref, taps, c0, cw, sub) for c0 in cols]
                row = i * tm + r + _iota((sub, 1), 0)
                live = (row >= PAD) & (row < n_rows)
                if gated:
                    dus = [d * pre[1] * _dsilu(pre[0]), d * _silu(pre[0])]
                else:
                    dus = [d * _dsilu(pre[0])]
                for part, du in enumerate(dus):
                    du_ref[part, pl.ds(r, sub), :] = jnp.where(live, du, 0.0)

            def tile_rows(s, carry):
                r = pl.multiple_of(s * sub, sub)
                conv_grad(r, dcur[src][pl.ds(r, sub), off:off + cw].astype(F32))
                return carry

            lax.fori_loop(0, tm // sub, tile_rows, 0, unroll=GROUP_UNROLL)
            conv_grad(tm, dnext[src][:, off:off + cw].astype(F32))

            for part, c0 in enumerate(cols):
                taps_w = [w_ref[k:k + 1, c0:c0 + cw] for k in range(taps)]

                def back(s, sums):
                    new = list(sums)
                    for u in range(2):
                        r = pl.multiple_of((2 * s + u) * sub, sub)
                        win = du_ref[part, pl.ds(r, sub + 8), :]
                        raw_rows = ext_ref[pl.ds(8 + r, sub), c0:c0 + cw]
                        draw = jnp.zeros((sub, cw), F32)
                        for k in range(taps):
                            shifted = win[taps - 1 - k:taps - 1 - k + sub]
                            draw = draw + taps_w[k] * shifted
                            new[k] = new[k] + shifted * raw_rows
                        new[taps] = new[taps] + win[0:sub]
                        out_ref[pl.ds(r, sub), c0:c0 + cw] = jnp.where(_valid_rows(i * tm + r, sub, PAD), draw, 0.0).astype(out_ref.dtype)
                    return tuple(new)

                sums = lax.fori_loop(0, tm // (2 * sub), back, tuple(jnp.zeros((sub, cw), F32) for _ in range(taps + 1)))
                for k in range(taps + 1):
                    total = jnp.sum(sums[k], axis=0, keepdims=True)
                    acc_ref[k:k + 1, c0:c0 + cw] = jnp.where(i == 0, total, acc_ref[k:k + 1, c0:c0 + cw] + total)

    in_specs = _conv_specs(tm, width, raw_blk, n_rows, True)
    in_specs += [pl.BlockSpec((tm, d.shape[1]), lambda i: (i, 0)) for d in dsrcs]
    in_specs += [pl.BlockSpec((16, d.shape[1]), lambda i: (jnp.minimum((i + 1) * (tm // 16), n_rows // 16 - 1), 0)) for d in dsrcs]
    in_specs += [pl.BlockSpec((taps, width), lambda i: (0, 0)), pl.BlockSpec((1, width), lambda i: (0, 0))]
    operands = [raw, raw, raw] + list(dsrcs) + list(dsrcs) + [conv_w, conv_b]
    aliases = {}
    if into is None:
        out0 = jax.ShapeDtypeStruct((n_rows, width), BF16)
    else:
        in_specs.append(pl.BlockSpec(memory_space=pl.ANY))
        operands.append(into)
        aliases = {len(operands) - 1: 0}
        out0 = jax.ShapeDtypeStruct(into.shape, into.dtype)
    return _call(body, name, (n_rows // tm,), in_specs,
                 [pl.BlockSpec((tm, width), lambda i: (i, into_blk)), pl.BlockSpec((8, width), lambda i: (0, 0))],
                 [out0, jax.ShapeDtypeStruct((8, width), F32)], operands,
                 scratch_shapes=[pltpu.VMEM((tm + 24, width), F32), pltpu.VMEM((n_parts, te + 8, cw), F32)],
                 aliases=aliases, bg=bg)


def _ssd_specs(n_chunks, rev):
    cidx = (lambda c: n_chunks - 1 - c) if rev else (lambda c: c)
    xg0, bg0, cg0 = P_XBC // GROUP_W, (P_XBC + D_INNER) // D_STATE, (P_XBC + D_INNER + SSM_GROUPS * D_STATE) // D_STATE

    def cur(width, blk0):
        return pl.BlockSpec((T, width), lambda g, c: (cidx(c), blk0 + g))

    def prev(width, blk0):
        return pl.BlockSpec((8, width), lambda g, c: (jnp.maximum(cidx(c) * (T // 8) - 1, 0), blk0 + g))

    specs = [cur(GROUP_W, xg0), prev(GROUP_W, xg0), cur(D_STATE, bg0), prev(D_STATE, bg0),
             cur(D_STATE, cg0), prev(D_STATE, cg0),
             pl.BlockSpec((T, 128), lambda g, c: (cidx(c), P_DT // 128))]
    wx, wb, wc = 0, D_INNER // D_STATE, (D_INNER + SSM_GROUPS * D_STATE) // D_STATE
    specs += [pl.BlockSpec((4, GROUP_W), lambda g, c: (0, g)),
              pl.BlockSpec((4, D_STATE), lambda g, c: (0, wb + g)),
              pl.BlockSpec((4, D_STATE), lambda g, c: (0, wc + g)),
              pl.BlockSpec((1, GROUP_W), lambda g, c: (0, g)),
              pl.BlockSpec((1, D_STATE), lambda g, c: (0, wb + g)),
              pl.BlockSpec((1, D_STATE), lambda g, c: (0, wc + g))]
    specs += [pl.BlockSpec((1, 128), lambda g, c: (0, 0))] * 3
    return specs, cidx


def _ssd_chunk_forward(refs, ext_ref, g, c):
    (xc_ref, xp_ref, bc_ref, bp_ref, cc_ref, cp_ref, dt_ref, wx_ref, wb_ref, wc_ref,
     bx_ref, bb_ref, bcb_ref, dtb_ref, alog_ref, dsk_ref) = refs

    def conv_pre(cur_ref, prev_ref, w_ref, b_ref, width):
        ext_ref[0:8, 0:width] = jnp.where(c > 0, prev_ref[...], 0.0)
        ext_ref[8:8 + T, 0:width] = cur_ref[...]
        w = w_ref[...]
        acc = b_ref[...] + w[3:4] * cur_ref[...]
        for k in range(3):
            acc = acc + w[k:k + 1] * ext_ref[pl.ds(5 + k, T), 0:width]
        return acc

    valid = _valid_rows(c * T, T, PAD)
    v = {}
    v["valid"] = valid
    v["x_pre"] = conv_pre(xc_ref, xp_ref, wx_ref, bx_ref, GROUP_W)
    v["b_pre"] = conv_pre(bc_ref, bp_ref, wb_ref, bb_ref, D_STATE)
    v["c_pre"] = conv_pre(cc_ref, cp_ref, wc_ref, bcb_ref, D_STATE)
    xs = _silu(v["x_pre"])
    bm = jnp.where(valid, _silu(v["b_pre"]), 0.0)
    cm = jnp.where(valid, _silu(v["c_pre"]), 0.0)
    dtr = dt_ref[...] + dtb_ref[...]
    dt = jnp.where(valid, _softplus(dtr), 0.0)
    a_neg = -jnp.exp(alog_ref[...])
    a = dt * a_neg
    tril = _iota((T, T), 0) >= _iota((T, T), 1)
    cs = _xdot_l(tril.astype(BF16), a)
    hh, ll = _iota((128, GROUP_W), 0), _iota((128, GROUP_W), 1)
    expand = (hh == 8 * g + jnp.right_shift(ll, 6)).astype(BF16)
    sh, sj = _iota((128, 128), 0), _iota((128, 128), 1)
    select = ((sh == 8 * g + sj) & (sj < 8)).astype(BF16)
    hh_t, ll_t = _iota((GROUP_W, 128), 1), _iota((GROUP_W, 128), 0)
    v["expand_t"] = (hh_t == 8 * g + jnp.right_shift(ll_t, 6)).astype(BF16)
    v["select_t"] = ((sj == 8 * g + sh) & (sh < 8)).astype(BF16)
    cs_e = _xdot(cs, expand)
    dt_e = _xdot(dt, expand)
    cs_loc = _xdot(cs, select)
    cs_loc_t = cs_loc.T
    cs_last_e = cs_e[T - 1:T, :]
    v.update(xs=xs, bm=bm, cm=cm, dtr=dtr, dt=dt, a_neg=a_neg, tril=tril, expand=expand, select=select,
             cs_e=cs_e, dt_e=dt_e, cs_loc=cs_loc, cs_loc_t=cs_loc_t, cs_last_e=cs_last_e)
    v["xdt"] = xs * dt_e
    v["decay_e"] = jnp.exp(cs_last_e - cs_e)
    v["ecs_e"] = jnp.exp(cs_e)
    v["elast_e"] = jnp.exp(cs_last_e)
    v["d_e"] = _xdot(dsk_ref[...], expand)
    v["gmat"] = _dot_nt(cm.astype(BF16), bm.astype(BF16))
    return v


def _ssd_decay_pair(v, jp):
    out = []
    for j in (2 * jp, 2 * jp + 1):
        diff = v["cs_loc"][:, j:j + 1] - v["cs_loc_t"][j:j + 1, :]
        out.append(jnp.where(v["tril"], jnp.exp(jnp.where(v["tril"], diff, 0.0)), 0.0))
    return out


def _block_diag_pair(xp):
    lane = _iota(xp.shape, 1)
    return jnp.concatenate([jnp.where(lane < HEAD_P, xp, 0.0), jnp.where(lane >= HEAD_P, xp, 0.0)], axis=0)


def _ssd_fwd(p, conv_w, conv_b, dt_bias, a_log, d_skip, n_chunks, bg=None):
    n_rows = n_chunks * T
    in_specs, _ = _ssd_specs(n_chunks, rev=False)

    def body(*refs):
        y_ref, hin_ref, st_ref, ext_ref = refs[16:]
        g, c = pl.program_id(0), pl.program_id(1)

        @pl.when(c == 0)
        def _():
            st_ref[...] = jnp.zeros_like(st_ref)

        v = _ssd_chunk_forward(refs[:16], ext_ref, g, c)
        state = st_ref[...]
        hin_ref[...] = state
        ys = []
        for jp in range(4):
            l0, l1 = _ssd_decay_pair(v, jp)
            lhs = jnp.concatenate([v["gmat"] * l0, v["gmat"] * l1], axis=1).astype(BF16)
            rhs = _block_diag_pair(v["xdt"][:, 128 * jp:128 * jp + 128]).astype(BF16)
            ys.append(_dot(lhs, rhs))
        y = jnp.concatenate(ys, axis=1)
        y = y + _dot(v["cm"].astype(BF16), state.astype(BF16)) * v["ecs_e"] + v["xs"] * v["d_e"]
        y_ref[...] = y
        s_new = _dot_tn(v["bm"].astype(BF16), (v["xdt"] * v["decay_e"]).astype(BF16))
        st_ref[...] = state * v["elast_e"] + s_new

    return _call(
        body, "ssd_fwd", (SSM_GROUPS, n_chunks), in_specs,
        [pl.BlockSpec((T, GROUP_W), lambda g, c: (c, g)),
         pl.BlockSpec((None, None, D_STATE, GROUP_W), lambda g, c: (g, c, 0, 0))],
        [jax.ShapeDtypeStruct((n_rows, D_INNER), F32),
         jax.ShapeDtypeStruct((SSM_GROUPS, n_chunks, D_STATE, GROUP_W), F32)],
        [p, p, p, p, p, p, p, conv_w, conv_w, conv_w, conv_b, conv_b, conv_b, dt_bias, a_log, d_skip],
        scratch_shapes=[pltpu.VMEM((D_STATE, GROUP_W), F32), pltpu.VMEM((T + 8, GROUP_W), F32)], bg=bg)


def _ssd_bwd(p, conv_w, conv_b, dt_bias, a_log, d_skip, hin, dy, n_chunks, bg=None):
    n_rows = n_chunks * T
    in_specs, cidx = _ssd_specs(n_chunks, rev=True)
    in_specs = in_specs + [pl.BlockSpec((None, None, D_STATE, GROUP_W), lambda g, c: (g, cidx(c), 0, 0)),
                           pl.BlockSpec((T, GROUP_W), lambda g, c: (cidx(c), g))]

    def body(*refs):
        hin_ref, dy_ref = refs[16:18]
        dx_ref, db_ref, dc_ref, ddt_ref, dpar_ref, dst_ref, ext_ref = refs[18:]
        g, step = pl.program_id(0), pl.program_id(1)
        c = n_chunks - 1 - step

        @pl.when(step == 0)
        def _():
            dst_ref[...] = jnp.zeros_like(dst_ref)

        v = _ssd_chunk_forward(refs[:16], ext_ref, g, c)
        hin_f = hin_ref[...]
        hin_b = hin_f.astype(BF16)
        dyv = dy_ref[...]
        dst = dst_ref[...]
        dst_b = dst.astype(BF16)
        xs, bm, cm, xdt = v["xs"], v["bm"], v["cm"], v["xdt"]
        bm_b, cm_b = bm.astype(BF16), cm.astype(BF16)

        dd_e = jnp.sum(dyv * xs, axis=0, keepdims=True)
        dxs = dyv * v["d_e"]
        ch = _dot(cm_b, hin_b)
        dch = (dyv * v["ecs_e"]).astype(BF16)
        dcm = _dot_nt(dch, hin_b)
        dhin = _dot_tn(cm_b, dch) + dst * v["elast_e"]
        dcs_e = dyv * ch * v["ecs_e"]
        dxd = _dot(bm_b, dst_b)
        dbm = _dot_nt((xdt * v["decay_e"]).astype(BF16), dst_b)
        dxdt_state = dxd * v["decay_e"]
        q = dxdt_state * xdt
        dcs_e = dcs_e - q
        dlast_e = jnp.sum(q, axis=0, keepdims=True) + jnp.sum(dst * hin_f, axis=0, keepdims=True) * v["elast_e"]
        dg = jnp.zeros((T, T), F32)
        rs_cols = jnp.zeros((T, 128), F32)
        cs_rows = jnp.zeros((128, T), F32)
        lane_i, sub_i = _iota((T, 128), 1), _iota((128, T), 0)
        dxdt_parts = []
        for jp in range(4):
            l0, l1 = _ssd_decay_pair(v, jp)
            m0, m1 = v["gmat"] * l0, v["gmat"] * l1
            xbd = _block_diag_pair(xdt[:, 128 * jp:128 * jp + 128]).astype(BF16)
            dyp = dyv[:, 128 * jp:128 * jp + 128]
            dm = _dot_nt(dyp.astype(BF16), xbd)
            dm0, dm1 = dm[:, 0:T], dm[:, T:2 * T]
            dg = dg + dm0 * l0 + dm1 * l1
            for j, qq in ((2 * jp, dm0 * m0), (2 * jp + 1, dm1 * m1)):
                rs_cols = jnp.where(lane_i == j, jnp.sum(qq, axis=1, keepdims=True), rs_cols)
                cs_rows = jnp.where(sub_i == j, jnp.sum(qq, axis=0, keepdims=True), cs_rows)
            mv = jnp.concatenate([m0, m1], axis=0).astype(BF16)
            dxdt_parts.append(_dot_tn(mv, _block_diag_pair(dyp).astype(BF16)))
        dxdt = jnp.concatenate(dxdt_parts, axis=1) + dxdt_state
        dg_b = dg.astype(BF16)
        dcm = dcm + _dot(dg_b, bm_b)
        dbm = dbm + _dot_tn(dg_b, cm_b)
        expand_t = v["expand_t"]
        dcs_loc = rs_cols - cs_rows.T
        last_row = _iota((T, 1), 0) == T - 1
        dcs_full_e = dcs_e + jnp.where(last_row, dlast_e, 0.0)
        dcs = _xdot(dcs_full_e, expand_t) + _xdot(dcs_loc, v["select_t"])
        triu = (_iota((T, T), 0) <= _iota((T, T), 1)).astype(BF16)
        da = _xdot_l(triu, dcs)
        ddt = da * v["a_neg"] + _xdot(dxdt * xs, expand_t)
        dxs = dxs + dxdt * v["dt_e"]
        ddtr = jnp.where(v["valid"], ddt * _sigmoid(v["dtr"]), 0.0)
        dx_ref[...] = dxs
        db_ref[...] = jnp.where(v["valid"], dbm, 0.0)
        dc_ref[...] = jnp.where(v["valid"], dcm, 0.0)
        ddt_ref[...] = ddtr
        dpar = jnp.concatenate([
            jnp.sum(ddtr, axis=0, keepdims=True),
            jnp.sum(da * v["dt"], axis=0, keepdims=True) * v["a_neg"],
            _xdot(dd_e, expand_t),
            jnp.zeros((5, 128), F32)], axis=0)

        @pl.when(step == 0)
        def _():
            dpar_ref[...] = dpar

        @pl.when(step > 0)
        def _():
            dpar_ref[...] += dpar

        dst_ref[...] = dhin

    return _call(
        body, "ssd_bwd", (SSM_GROUPS, n_chunks), in_specs,
        [pl.BlockSpec((T, GROUP_W), lambda g, c: (cidx(c), g)),
         pl.BlockSpec((T, D_STATE), lambda g, c: (cidx(c), g)),
         pl.BlockSpec((T, D_STATE), lambda g, c: (cidx(c), g)),
         pl.BlockSpec((T, 128), lambda g, c: (cidx(c), g)),
         pl.BlockSpec((None, 8, 128), lambda g, c: (g, 0, 0))],
        [jax.ShapeDtypeStruct((n_rows, D_INNER), F32),
         jax.ShapeDtypeStruct((n_rows, SSM_GROUPS * D_STATE), F32),
         jax.ShapeDtypeStruct((n_rows, SSM_GROUPS * D_STATE), F32),
         jax.ShapeDtypeStruct((n_rows, SSM_GROUPS * 128), F32),
         jax.ShapeDtypeStruct((SSM_GROUPS, 8, 128), F32)],
        [p, p, p, p, p, p, p, conv_w, conv_w, conv_w, conv_b, conv_b, conv_b, dt_bias, a_log, d_skip, hin, dy],
        scratch_shapes=[pltpu.VMEM((D_STATE, GROUP_W), F32), pltpu.VMEM((T + 8, GROUP_W), F32)], bg=bg)


def _alibi_slope(h):
    return 2.0 ** (-8.0 * (h + 1) / ATTN_HEADS)


def _dup_half(x256, kvh):
    xb = x256[:, 128 * (kvh // 2):128 * (kvh // 2) + 128]
    rolled = pltpu.roll(xb, 64, 1)
    lane = _iota(xb.shape, 1)
    if kvh % 2 == 0:
        return jnp.where(lane < 64, xb, rolled)
    return jnp.where(lane < 64, rolled, xb)


def _attn_masks(c):
    qi = _iota((T, 3 * T), 0)
    jj = _iota((T, 3 * T), 1)
    blk = jnp.right_shift(jj, 7)
    j = jnp.bitwise_and(jj, T - 1)
    q_pos = c * T + qi - PAD
    k_pos = (c - 2 + blk) * T + j - PAD
    dist = q_pos - k_pos
    band = (blk > 0) & (dist >= 0) & (dist < T) & (k_pos >= N_META)
    meta = (blk == 0) & (j >= PAD) & (j - PAD <= q_pos)
    distf = jnp.where(blk > 0, dist, 0).astype(F32)
    return band | meta, distf


def _attn_scores(qp, k3, allowed, distf, h0):
    lane = _iota(qp.shape, 1)
    s = []
    for half, h in ((0, h0), (1, h0 + 1)):
        qh = jnp.where((lane < 64) if half == 0 else (lane >= 64), qp, 0.0).astype(BF16)
        sc = _dot_nt(qh, k3) - _alibi_slope(h) * distf
        s.append((qh, jnp.where(allowed, sc, NEG)))
    return s


def _attn_fwd(p, sinks, n_chunks, bg=None):
    n_rows = n_chunks * T
    kb, vb = P_K // KV_W, P_V // KV_W

    def body(q_ref, kc_ref, kp_ref, km_ref, vc_ref, vp_ref, vm_ref, sink_ref, o_ref, lse_ref):
        c = pl.program_id(0)
        allowed, distf = _attn_masks(c)
        q = q_ref[...] * SCALE
        sinks_v = sink_ref[...]
        lane = _iota((T, 128), 1)
        lse_all = jnp.zeros((T, 128), F32)
        outs = []
        for kvh in range(KV_HEADS):
            k3 = jnp.concatenate([_dup_half(r[...], kvh) for r in (km_ref, kp_ref, kc_ref)], axis=0).astype(BF16)
            v3 = jnp.concatenate([_dup_half(r[...], kvh) for r in (vm_ref, vp_ref, vc_ref)], axis=0)
            v3bd = _block_diag_rows(v3).astype(BF16)
            for pr in range(2):
                h0 = 4 * kvh + 2 * pr
                blk = 2 * kvh + pr
                qp = q[:, 128 * blk:128 * blk + 128]
                probs = []
                for (_, sc), h in zip(_attn_scores(qp, k3, allowed, distf, h0), (h0, h0 + 1)):
                    sink = sinks_v[:, h:h + 1]
                    m = jnp.maximum(jnp.max(sc, axis=1, keepdims=True), sink)
                    e = jnp.exp(sc - m)
                    den = jnp.sum(e, axis=1, keepdims=True) + jnp.exp(sink - m)
                    probs.append(e / den)
                    lse_all = jnp.where(lane == h, m + jnp.log(den), lse_all)
                outs.append(_dot(jnp.concatenate(probs, axis=1).astype(BF16), v3bd))
        o_ref[...] = jnp.concatenate(outs, axis=1).astype(o_ref.dtype)
        lse_ref[...] = lse_all

    blk = lambda width, col: pl.BlockSpec((T, width), lambda c: (c, col))
    prev = lambda width, col: pl.BlockSpec((T, width), lambda c: (jnp.maximum(c - 1, 0), col))
    first = lambda width, col: pl.BlockSpec((T, width), lambda c: (0, col))
    return _call(
        body, "attn_fwd", (n_chunks,),
        [blk(ATTN_W, P_Q // ATTN_W), blk(KV_W, kb), prev(KV_W, kb), first(KV_W, kb),
         blk(KV_W, vb), prev(KV_W, vb), first(KV_W, vb), pl.BlockSpec((1, 128), lambda c: (0, 0))],
        [pl.BlockSpec((T, ATTN_W), lambda c: (c, 0)), pl.BlockSpec((T, 128), lambda c: (c, 0))],
        [jax.ShapeDtypeStruct((n_rows, ATTN_W), BF16), jax.ShapeDtypeStruct((n_rows, 128), F32)],
        [p, p, p, p, p, p, p, sinks], bg=bg)


def _block_diag_rows(x3):
    lane = _iota(x3.shape, 1)
    return jnp.concatenate([jnp.where(lane < 64, x3, 0.0), jnp.where(lane >= 64, x3, 0.0)], axis=0)


def _fold_halves(x):
    return x + pltpu.roll(x, 64, 1)


def _attn_bwd(p, sinks, ao, lse, dao, dp, n_chunks, bg=None):
    kb, vb = P_K // KV_W, P_V // KV_W
    rc = lambda s: n_chunks - 1 - s

    def body(q_ref, kc_ref, kp_ref, km_ref, vc_ref, vp_ref, vm_ref, sink_ref, o_ref, lse_ref, do_ref, dp_in_ref,
             dqkv_ref, dsink_ref, kcar_ref, vcar_ref, kmeta_ref, vmeta_ref):
        step = pl.program_id(0)
        c = n_chunks - 1 - step

        @pl.when(step == 0)
        def _():
            for r in (kcar_ref, vcar_ref, kmeta_ref, vmeta_ref):
                r[...] = jnp.zeros_like(r)

        allowed, distf = _attn_masks(c)
        q = q_ref[...] * SCALE
        sinks_v = sink_ref[...]
        lse_v = lse_ref[...]
        ov = o_ref[...].astype(F32)
        dov = do_ref[...].astype(F32)
        lane = _iota((T, 128), 1)
        lane256 = _iota((3 * T, KV_W), 1)
        dsink = jnp.zeros((1, 128), F32)
        dk3_all = jnp.zeros((3 * T, KV_W), F32)
        dv3_all = jnp.zeros((3 * T, KV_W), F32)
        dqs = []
        for kvh in range(KV_HEADS):
            k3 = jnp.concatenate([_dup_half(r[...], kvh) for r in (km_ref, kp_ref, kc_ref)], axis=0).astype(BF16)
            v3 = jnp.concatenate([_dup_half(r[...], kvh) for r in (vm_ref, vp_ref, vc_ref)], axis=0).astype(BF16)
            dk3 = jnp.zeros((3 * T, 128), F32)
            dv3 = jnp.zeros((3 * T, 128), F32)
            for pr in range(2):
                h0 = 4 * kvh + 2 * pr
                blk = 2 * kvh + pr
                qp = q[:, 128 * blk:128 * blk + 128]
                dop = dov[:, 128 * blk:128 * blk + 128]
                prod = dop * ov[:, 128 * blk:128 * blk + 128]
                dq_pair = jnp.zeros((T, 128), F32)
                for half, ((qh, sc), h) in enumerate(zip(_attn_scores(qp, k3, allowed, distf, h0), (h0, h0 + 1))):
                    mine = (lane < 64) if half == 0 else (lane >= 64)
                    lse_h = lse_v[:, h:h + 1]
                    pm = jnp.exp(sc - lse_h)
                    doh = jnp.where(mine, dop, 0.0).astype(BF16)
                    delta = jnp.sum(jnp.where(mine, prod, 0.0), axis=1, keepdims=True)
                    dp = _dot_nt(doh, v3)
                    ds = (pm * (dp - delta)).astype(BF16)
                    p_sink = jnp.exp(sinks_v[:, h:h + 1] - lse_h)
                    dsink = jnp.where(_iota((1, 128), 1) == h, jnp.sum(-p_sink * delta, axis=0, keepdims=True), dsink)
                    dq_pair = jnp.where(mine, _dot(ds, k3), dq_pair)
                    dk3 = dk3 + _dot_tn(ds, qh)
                    dv3 = dv3 + _dot_tn(pm.astype(BF16), doh)
                dqs.append(dq_pair * SCALE)
            in_place = (lane256 >= 64 * kvh) & (lane256 < 64 * kvh + 64)
            wide = lambda x: jnp.concatenate([x, x], axis=1)
            dk3_all = jnp.where(in_place, wide(_fold_halves(dk3)), dk3_all)
            dv3_all = jnp.where(in_place, wide(_fold_halves(dv3)), dv3_all)
        dsink_all = dsink

        @pl.when(step == 0)
        def _():
            dsink_ref[...] = dsink_all

        @pl.when(step > 0)
        def _():
            dsink_ref[...] += dsink_all

        kmeta = kmeta_ref[...] + dk3_all[0:T]
        vmeta = vmeta_ref[...] + dv3_all[0:T]
        kmeta_ref[...] = kmeta
        vmeta_ref[...] = vmeta
        is_first = c == 0
        dk = jnp.where(is_first, kmeta, dk3_all[2 * T:3 * T] + kcar_ref[...])
        dv = jnp.where(is_first, vmeta, dv3_all[2 * T:3 * T] + vcar_ref[...])
        dqkv_ref[...] = jnp.concatenate(dqs + [dk, dv], axis=1).astype(dqkv_ref.dtype)
        kcar_ref[...] = dk3_all[T:2 * T]
        vcar_ref[...] = dv3_all[T:2 * T]

    blk = lambda width, col: pl.BlockSpec((T, width), lambda s: (rc(s), col))
    prev = lambda width, col: pl.BlockSpec((T, width), lambda s: (jnp.maximum(rc(s) - 1, 0), col))
    first = lambda width, col: pl.BlockSpec((T, width), lambda s: (0, col))
    return _call(
        body, "attn_bwd", (n_chunks,),
        [blk(ATTN_W, P_Q // ATTN_W), blk(KV_W, kb), prev(KV_W, kb), first(KV_W, kb),
         blk(KV_W, vb), prev(KV_W, vb), first(KV_W, vb), pl.BlockSpec((1, 128), lambda s: (0, 0)),
         blk(ATTN_W, 0), blk(128, 0), blk(ATTN_W, 0), ANY],
        [blk(QKV_W, P_Q // QKV_W), pl.BlockSpec((1, 128), lambda s: (0, 0))],
        [jax.ShapeDtypeStruct(dp.shape, dp.dtype), jax.ShapeDtypeStruct((1, 128), F32)],
        [p, p, p, p, p, p, p, sinks, ao, lse, dao, dp],
        scratch_shapes=[pltpu.VMEM((T, KV_W), F32)] * 4, aliases={11: 0}, bg=bg)


def _pad_lanes(v, width=128):
    return jnp.pad(v, ((0, 0), (0, width - v.shape[1])))


def _local_step(x, head, tgt, plan):
    w, g, run = plan.w, plan.g, plan.run
    n_tok = x.shape[0]
    n_rows = n_tok + T
    n_chunks = n_rows // T
    tm = _row_tile(n_rows, 384)
    dt_bias, a_log, d_skip = (_pad_lanes(w[k]) for k in ("ssm_dt_bias", "ssm_a_log", "ssm_d_skip"))
    sinks = _pad_lanes(w["attn_sinks"])
    x_in = [(x, D_MODEL, 0, "prev"), (head, D_MODEL, 0, "first")]

    def h0_tile(r0, xt, hd):
        return jnp.where(r0 < T, hd, xt)

    n1, = _rowwise("norm_pre_mix", lambda r0, xt, hd, wn: [_rms(h0_tile(r0, xt, hd), wn)], n_rows, T,
                   x_in, [w["norm_pre_mix"]], [(D_MODEL, BF16)], [])
    p = _matmul("in_proj", n1, w["w_cat"], "nn", F32)
    y_ssd, hin = run("ssd_fwd", _ssd_fwd, p, w["ssm_conv_w"], w["ssm_conv_b"], dt_bias, a_log, d_skip, n_chunks)
    ao, lse = run("attn_fwd", _attn_fwd, p, sinks, n_chunks)

    def gate_norm(r0, y, z, wn):
        return [_rms(y * _silu(z), wn)]

    yn, = _rowwise("ssm_gate_norm", gate_norm, n_rows, tm, [(y_ssd, D_INNER, 0), (p, D_INNER, P_Z // D_INNER)],
                   [w["ssm_norm"]], [(D_INNER, BF16)], [])
    y_ssm = _matmul("ssm_out", yn, w["w_ssm_out"], "nn", F32)
    y_attn = _matmul("attn_out", ao, w["w_attn_out"], "nn", F32)

    def mix_gate(r0, ys, ya, gs, ga):
        return [_sigmoid(gs) * ys + _sigmoid(ga) * ya]

    gate_ins = [(p, D_MODEL, P_GATE // D_MODEL), (p, D_MODEL, P_GATE // D_MODEL + 1)]
    mixed, = _rowwise("mix_gate", mix_gate, n_rows, tm, [(y_ssm, D_MODEL, 0), (y_attn, D_MODEL, 0)] + gate_ins,
                      [], [(D_MODEL, BF16)], [])
    mix = _matmul("mix_out", mixed, w["w_mix_out"], "nn", F32)

    def post_mix(r0, mx, xt, hd, w_post, w_pre):
        h1 = jnp.where(_valid_rows(r0, mx.shape[0], PAD), h0_tile(r0, xt, hd) + _rms(mx, w_post), 0.0)
        return [h1, _rms(h1, w_pre)]

    h1, n2 = _rowwise("post_mix", post_mix, n_rows, T, [(mix, D_MODEL, 0)] + x_in,
                      [w["norm_post_mix"], w["norm_pre_ffn"]], [(D_MODEL, F32), (D_MODEL, BF16)], [])
    u_raw = _matmul("ffn_up", n2, w["w_ffn_up"], "nn", F32)
    f = _ffn_act("ffn_act", u_raw, w["ffn_conv_w"], w["ffn_conv_b"], n_rows)
    ffn = _matmul("ffn_down", f, w["w_ffn_down"], "nn", F32)

    def final(r0, fo, h, t, w_post):
        real = r0 >= T
        err = jnp.where(real, h + _rms(fo, w_post) - t, 0.0)
        dy = err * (1.0 / D_MODEL)
        dffn, dw = _rms_bwd(dy, fo, w_post)
        return [dffn, dy, jnp.sum(err * err, axis=0, keepdims=True), dw]

    dffn, dh2, loss_cols, g_norm_post_ffn = _rowwise(
        "loss_head", final, n_rows, T, [(ffn, D_MODEL, 0), (h1, D_MODEL, 0), (tgt, D_MODEL, 0, "prev")],
        [w["norm_post_ffn"]], [(D_MODEL, BF16), (D_MODEL, F32)], [D_MODEL, D_MODEL])

    g["norm_post_ffn"] = g_norm_post_ffn
    g["w_ffn_down"] = _matmul("ffn_down_dw", f, dffn, "tn", F32)
    df = _matmul("ffn_down_dx", dffn, w["w_ffn_down"], "nt", F32)
    du_raw, dconv = _conv_bwd("ffn_act_bwd", u_raw, 0, [df], [(0, c0) for c0 in range(0, FFN_DIM, CONV_LANES)],
                              w["ffn_conv_w"], w["ffn_conv_b"], n_rows, True)
    g["ffn_conv_w"], g["ffn_conv_b"] = dconv[0:3], dconv[3:4]
    g["w_ffn_up"] = _matmul("ffn_up_dw", n2, du_raw, "tn", F32)
    dn2 = run("ffn_up_dx", _matmul, "ffn_up_dx", du_raw, w["w_ffn_up"], "nt", F32)

    def post_mix_bwd(r0, dn, d2, h, mx, w_pre, w_post):
        dx, dw_pre = _rms_bwd(dn, h, w_pre)
        dh1 = jnp.where(_valid_rows(r0, dn.shape[0], PAD), dx + d2, 0.0)
        dmix, dw_post = _rms_bwd(dh1, mx, w_post)
        return [dh1, dmix, dw_pre, dw_post]

    dh1, dmix, g["norm_pre_ffn"], g["norm_post_mix"] = _rowwise(
        "post_mix_bwd", post_mix_bwd, n_rows, tm,
        [(dn2, D_MODEL, 0), (dh2, D_MODEL, 0), (h1, D_MODEL, 0), (mix, D_MODEL, 0)],
        [w["norm_pre_ffn"], w["norm_post_mix"]], [(D_MODEL, F32), (D_MODEL, BF16)], [D_MODEL, D_MODEL])
    g["w_mix_out"] = _matmul("mix_out_dw", mixed, dmix, "tn", F32)
    dmixed = _matmul("mix_out_dx", dmix, w["w_mix_out"], "nt", F32)

    def mix_gate_bwd(r0, dm, ys, ya, gs, ga):
        ss, sa = _sigmoid(gs), _sigmoid(ga)
        dgate = jnp.concatenate([dm * ys * ss * (1.0 - ss), dm * ya * sa * (1.0 - sa)], axis=1)
        return [dm * ss, dm * sa, dgate]

    dys, dya, dp = _rowwise(
        "mix_gate_bwd", mix_gate_bwd, n_rows, tm,
        [(dmixed, D_MODEL, 0), (y_ssm, D_MODEL, 0), (y_attn, D_MODEL, 0)] + gate_ins,
        [], [(D_MODEL, BF16), (D_MODEL, BF16), (2 * D_MODEL, BF16, "new", P_W, P_GATE // (2 * D_MODEL))], [])
    g["w_ssm_out"] = _matmul("ssm_out_dw", yn, dys, "tn", F32)
    dyn = _matmul("ssm_out_dx", dys, w["w_ssm_out"], "nt", F32)
    g["w_attn_out"] = _matmul("attn_out_dw", ao, dya, "tn", F32)
    dao = _matmul("attn_out_dx", dya, w["w_attn_out"], "nt", BF16)

    def gate_norm_bwd(r0, dn, y, z, wn):
        sz = _silu(z)
        dyz, dw = _rms_bwd(dn, y * sz, wn)
        live = _valid_rows(r0, dn.shape[0], PAD)
        return [jnp.where(live, dyz * sz, 0.0), jnp.where(live, dyz * y * _dsilu(z), 0.0), dw]

    dy_ssd, dp, g["ssm_norm"] = run(
        "ssm_gate_norm_bwd", _rowwise, "ssm_gate_norm_bwd", gate_norm_bwd, n_rows, tm,
        [(dyn, D_INNER, 0), (y_ssd, D_INNER, 0), (p, D_INNER, P_Z // D_INNER)],
        [w["ssm_norm"]], [(D_INNER, F32), (D_INNER, BF16, "into", dp, P_Z // D_INNER)], [D_INNER])
    dp, dsink = run("attn_bwd", _attn_bwd, p, sinks, ao, lse, dao, dp, n_chunks)
    g["attn_sinks"] = dsink[:, 0:ATTN_HEADS]
    dxs, dbm, dcm, ddt_parts, dpar = run("ssd_bwd", _ssd_bwd, p, w["ssm_conv_w"], w["ssm_conv_b"], dt_bias, a_log,
                                         d_skip, hin, dy_ssd, n_chunks)
    dpar = jnp.sum(dpar, axis=0)
    g["ssm_dt_bias"], g["ssm_a_log"], g["ssm_d_skip"] = (dpar[i:i + 1, 0:SSM_HEADS] for i in range(3))

    def dt_grad(r0, parts):
        tot = parts[:, 0:128] + parts[:, 128:256] + parts[:, 256:384] + parts[:, 384:512]
        return [jnp.concatenate([tot, jnp.zeros((parts.shape[0], P_Z - P_DT - 128), F32)], axis=1)]

    dt_w = P_Z - P_DT
    dp, = _rowwise("dt_grad", dt_grad, n_rows, tm, [(ddt_parts, SSM_GROUPS * 128, 0)], [],
                   [(dt_w, BF16, "into", dp, P_DT // dt_w)], [])
    x_chunks = [(src, c0) for src, arr in enumerate((dxs, dbm, dcm)) for c0 in range(0, arr.shape[1], CONV_LANES)]
    dp, dconv = run("ssm_conv_bwd", _conv_bwd, "ssm_conv_bwd", p, P_XBC // CONV_DIM, [dxs, dbm, dcm], x_chunks,
                    w["ssm_conv_w"], w["ssm_conv_b"], n_rows, False, into=dp, into_blk=P_XBC // CONV_DIM)
    g["ssm_conv_w"], g["ssm_conv_b"] = dconv[0:4], dconv[4:5]
    g["w_cat"] = _matmul("in_proj_dw", n1, dp, "tn", F32)
    dn1 = run("in_proj_dx", _matmul, "in_proj_dx", dp, w["w_cat"], "nt", F32)

    def pre_mix_bwd(r0, dn, d1, xt, hd, wn):
        dx, dw = _rms_bwd(dn, h0_tile(r0, xt, hd), wn)
        dh0 = jnp.where(_valid_rows(r0, dn.shape[0], PAD), dx + d1, 0.0)
        return [dh0, dh0, dw]

    dx_out, dhead, g["norm_pre_mix"] = _rowwise(
        "pre_mix_bwd", pre_mix_bwd, n_rows, T, [(dn1, D_MODEL, 0), (dh1, D_MODEL, 0)] + x_in,
        [w["norm_pre_mix"]], [(D_MODEL, F32, "prev", n_tok), (D_MODEL, F32, "first")], [D_MODEL])
    return jnp.sum(loss_cols), dx_out, dhead


_IN_SECTIONS = [((5152, 6176), P_Q), ((6176, 6432), P_K), ((6432, 6688), P_V), ((5120, 5152), P_DT),
                ((0, 2048), P_Z), ((6688, 8736), P_GATE), ((2048, 5120), P_XBC)]


def _to_cat(w_in):
    parts, at = [], 0
    for (a, b), off in _IN_SECTIONS:
        if off > at:
            parts.append(jnp.zeros((w_in.shape[0], off - at), w_in.dtype))
        parts.append(w_in[:, a:b])
        at = off + (b - a)
    return jnp.concatenate(parts, axis=1)


def _from_cat(g_cat):
    pieces = {a: g_cat[:, off:off + (b - a)] for (a, b), off in _IN_SECTIONS}
    return jnp.concatenate([pieces[a] for a in sorted(pieces)], axis=1)


LANES = 1024
_BIG = [("w_in", 1024, 2184, "chip"), ("w_ssm_out", 512, 1024, "row"), ("w_attn_out", 256, 1024, "row"),
        ("w_mix_out", 256, 1024, "row"), ("w_ffn_up", 1024, 1408, "col"), ("w_ffn_down", 704, 1024, "row"),
        ("small", 32, LANES, "chip")]
_SMALL_SHARDED = [("ssm_conv_w", (4, 768), 1), ("ffn_conv_w", (3, 1408), 1), ("meta_tokens", (16, 256), 1)]
_REPLICATED = [("norm_pre_mix", 1024), ("ssm_conv_b", 3072), ("ssm_dt_bias", 32), ("ssm_a_log", 32),
               ("ssm_d_skip", 32), ("ssm_norm", 2048), ("attn_sinks", 16), ("norm_post_mix", 1024),
               ("norm_pre_ffn", 1024), ("ffn_conv_b", 5632), ("norm_post_ffn", 1024)]
SMALL_ROWS = 16
WEIGHT_ORDER = ["meta_tokens", "norm_pre_mix", "w_in", "ssm_conv_w", "ssm_conv_b", "ssm_dt_bias", "ssm_a_log",
                "ssm_d_skip", "ssm_norm", "w_ssm_out", "attn_sinks", "w_attn_out", "w_mix_out", "norm_post_mix",
                "norm_pre_ffn", "w_ffn_up", "ffn_conv_w", "ffn_conv_b", "w_ffn_down", "norm_post_ffn"]


def _flatten(parts, rows):
    flat = jnp.concatenate([a.reshape(-1) for a in parts])
    return jnp.pad(flat, (0, rows * LANES - flat.shape[0])).reshape(rows, LANES)


def _unflatten(flat, shapes):
    flat = flat.reshape(-1)
    out, off = [], 0
    for shp in shapes:
        n = math.prod(shp)
        out.append(flat[off:off + n].reshape(shp))
        off += n
    return out


def _shard_of(full, chip, shape, axis):
    return lax.slice_in_dim(full, chip * shape[axis], (chip + 1) * shape[axis], axis=axis)


def _full_shape(r, c, layout):
    return {"row": (4 * r, c), "col": (r, 4 * c), "chip": (4, r, c)}[layout]


def _shard_view(ref, r, c, layout, chip):
    if layout == "row":
        return ref.at[pl.ds(pl.multiple_of(chip * r, 16), r), :]
    if layout == "col":
        return ref.at[:, pl.ds(pl.multiple_of(chip * c, 128), c)]
    return ref.at[chip]


def _half_view(ref, r, c, layout, chip, half):
    hr = r // 2
    if layout == "row":
        return ref.at[pl.ds(pl.multiple_of(chip * r + half * hr, 16), hr), :]
    r0 = pl.multiple_of(half * hr, 16)
    if layout == "col":
        return ref.at[pl.ds(r0, hr), pl.ds(pl.multiple_of(chip * c, 128), c)]
    return ref.at[chip, pl.ds(r0, hr), :]


def _mesh_pos():
    return lax.axis_index("x"), lax.axis_index("y"), lax.axis_index("c")


def _other_chips(x, y):
    return [(1 - x, y), (x, 1 - y), (1 - x, 1 - y)]


def _chip_index(x, y):
    return 2 * x + y


def _run_exchange(name, ex):
    n_in, n_out = len(ex.ins), len(ex.out_shapes)

    def body(*refs):
        in_refs, out_refs = refs[:n_in], refs[n_in:n_in + n_out]
        send_sems, recv_sems = refs[n_in + n_out:]
        copies = [pltpu.make_async_remote_copy(src_ref=s, dst_ref=d, send_sem=send_sems.at[i], recv_sem=recv_sems.at[i],
                                               device_id=dev, device_id_type=MESH)
                  for i, (s, d, dev) in enumerate(ex.make_copies(in_refs, out_refs))]
        assert len(copies) == ex.n_copies
        for cp in copies:
            cp.start()
        for cp in copies:
            cp.wait()

    return pl.pallas_call(
        body, name=name, in_specs=[ANY] * n_in, out_specs=[ANY] * n_out, out_shape=list(ex.out_shapes),
        scratch_shapes=[pltpu.SemaphoreType.DMA((ex.n_copies,)), pltpu.SemaphoreType.DMA((ex.n_copies,))],
        compiler_params=pltpu.CompilerParams(has_side_effects=True),
    )(*ex.ins)


def _join(*exs):
    def make(in_refs, out_refs):
        copies, i0, o0 = [], 0, 0
        for ex in exs:
            copies += ex.make_copies(in_refs[i0:i0 + len(ex.ins)], out_refs[o0:o0 + len(ex.out_shapes)])
            i0, o0 = i0 + len(ex.ins), o0 + len(ex.out_shapes)
        return copies

    return _Exchange([a for ex in exs for a in ex.ins], [s for ex in exs for s in ex.out_shapes], make,
                     sum(ex.n_copies for ex in exs))


def _split(exs, results):
    out, o0 = [], 0
    for ex in exs:
        out.append(list(results[o0:o0 + len(ex.out_shapes)]))
        o0 += len(ex.out_shapes)
    return out


def _gather_ici(entries, shards):
    def make(in_refs, out_refs):
        x, y, c = _mesh_pos()
        j = _chip_index(x, y)
        copies = []
        for ref_in, ref_out, (_, r, cc, lay) in zip(in_refs, out_refs, entries):
            copies.append((ref_in, _shard_view(ref_out, r, cc, lay, j), None))
            mine = ref_in.at[pl.ds(pl.multiple_of(c * (r // 2), 16), r // 2), :]
            copies += [(mine, _half_view(ref_out, r, cc, lay, j, c), (*ch, c)) for ch in _other_chips(x, y)]
        return copies

    shapes = [jax.ShapeDtypeStruct(_full_shape(r, cc, lay), s.dtype) for s, (_, r, cc, lay) in zip(shards, entries)]
    return _Exchange(list(shards), shapes, make, 4 * len(entries))


def _gather_pass_on(entries, fulls):
    def make(in_refs, out_refs):
        x, y, c = _mesh_pos()
        copies = []
        for ref, (_, r, cc, lay) in zip(out_refs, entries):
            for ch in _other_chips(x, y):
                landed = _half_view(ref, r, cc, lay, _chip_index(*ch), c)
                copies.append((landed, landed, (x, y, 1 - c)))
        return copies

    return _Exchange(list(fulls), [jax.ShapeDtypeStruct(f.shape, f.dtype) for f in fulls], make, 3 * len(entries),
                     {a: a for a in range(len(entries))})


def _gather_weights(entries, shards):
    n = len(entries)

    def body(*refs):
        ins, outs = refs[:n], refs[n:2 * n]
        send_sems, recv_sems, local_sems = refs[2 * n:]
        x, y, c = _mesh_pos()
        j = _chip_index(x, y)
        sibling = (x, y, 1 - c)
        chips = _other_chips(x, y)
        idx = [_chip_index(*ch) for ch in chips]

        def remote(k, src, dst, dev):
            return pltpu.make_async_remote_copy(src_ref=src, dst_ref=dst, send_sem=send_sems.at[k],
                                                recv_sem=recv_sems.at[k], device_id=dev, device_id_type=MESH)

        own = [pltpu.make_async_copy(ins[a], _shard_view(outs[a], r, cc, lay, j), local_sems.at[a])
               for a, (_, r, cc, lay) in enumerate(entries)]
        for cp in own:
            cp.start()
        first, passed = [], []
        for a, (_, r, cc, lay) in enumerate(entries):
            mine = ins[a].at[pl.ds(pl.multiple_of(c * (r // 2), 16), r // 2), :]
            for k, ch in enumerate(chips):
                first.append(remote(6 * a + k, mine, _half_view(outs[a], r, cc, lay, j, c), (*ch, c)))
                landed = _half_view(outs[a], r, cc, lay, idx[k], c)
                passed.append(remote(6 * a + 3 + k, landed, landed, sibling))
        for cp in first:
            cp.start()
        for a, (_, r, cc, lay) in enumerate(entries):
            for k in range(3):
                landed = _half_view(outs[a], r, cc, lay, idx[k], c)
                remote(6 * a + k, landed, landed, sibling).wait_recv()
                passed[3 * a + k].start()
        for a, (_, r, cc, lay) in enumerate(entries):
            for k in range(3):
                theirs = _half_view(outs[a], r, cc, lay, idx[k], 1 - c)
                remote(6 * a + 3 + k, theirs, theirs, sibling).wait_recv()
        for cp in first + passed:
            cp.wait_send()
        for cp in own:
            cp.wait()

    return pl.pallas_call(
        body, name="gather_weights", in_specs=[ANY] * n, out_specs=[ANY] * n,
        out_shape=[jax.ShapeDtypeStruct(_full_shape(r, cc, lay), s.dtype) for s, (_, r, cc, lay) in zip(shards, entries)],
        scratch_shapes=[pltpu.SemaphoreType.DMA((6 * n,)), pltpu.SemaphoreType.DMA((6 * n,)), pltpu.SemaphoreType.DMA((n,))],
        compiler_params=pltpu.CompilerParams(has_side_effects=True),
    )(*shards)


def _pair_exchange(entries, grads):
    def make(in_refs, out_refs):
        x, y, c = _mesh_pos()
        return [(_half_view(ref_in, r, cc, lay, i, 1 - c), ref_out.at[i], (x, y, 1 - c))
                for ref_in, ref_out, (_, r, cc, lay) in zip(in_refs, out_refs, entries) for i in range(4)]

    return _Exchange(list(grads), [jax.ShapeDtypeStruct((4, r // 2, cc), F32) for _, r, cc, _ in entries], make,
                     4 * len(entries))


def _whole_to_sibling(arrays):
    def make(in_refs, out_refs):
        x, y, c = _mesh_pos()
        return [(r, o, (x, y, 1 - c)) for r, o in zip(in_refs, out_refs)]

    return _Exchange(list(arrays), [jax.ShapeDtypeStruct(a.shape, a.dtype) for a in arrays], make, len(arrays))


def _chip_exchange(psends):
    def make(in_refs, out_refs):
        x, y, c = _mesh_pos()
        return [(ref_in.at[_chip_index(*ch)], ref_out.at[k], (*ch, c))
                for ref_in, ref_out in zip(in_refs, out_refs) for k, ch in enumerate(_other_chips(x, y))]

    return _Exchange(list(psends), [jax.ShapeDtypeStruct((3,) + p.shape[1:], p.dtype) for p in psends], make,
                     3 * len(psends))


def _to_all_chips(array):
    def make(in_refs, out_refs):
        x, y, c = _mesh_pos()
        return [(in_refs[0], out_refs[0].at[k], (*ch, c)) for k, ch in enumerate(_other_chips(x, y))]

    return _Exchange([array], [jax.ShapeDtypeStruct((3,) + array.shape, array.dtype)], make, 3)


SUM_ROWS = 256
ADAM_ROWS = 128


def _pair_sum(name, grad, recv, ids, r, c, layout):
    hr = r // 2
    tr = _row_tile(hr, SUM_ROWS)
    nb = hr // tr

    def body(ids_ref, g_ref, r_ref, send_ref, own_ref):
        s = g_ref[...] + r_ref[...]
        send_ref[...] = s.astype(send_ref.dtype)

        @pl.when(pl.program_id(1) == ids_ref[1])
        def _():
            own_ref[...] = s

    if layout == "row":
        g_spec = pl.BlockSpec((tr, c), lambda t, j, ids_ref: ((j * r + ids_ref[0] * hr) // tr + t, 0))
    elif layout == "col":
        g_spec = pl.BlockSpec((tr, c), lambda t, j, ids_ref: (ids_ref[0] * nb + t, j))
    else:
        g_spec = pl.BlockSpec((None, tr, c), lambda t, j, ids_ref: (j, ids_ref[0] * nb + t, 0))
    grid_spec = pltpu.PrefetchScalarGridSpec(
        num_scalar_prefetch=1, grid=(nb, 4),
        in_specs=[g_spec, pl.BlockSpec((None, tr, c), lambda t, j, ids_ref: (j, t, 0))],
        out_specs=[pl.BlockSpec((None, tr, c), lambda t, j, ids_ref: (j, t, 0)),
                   pl.BlockSpec((tr, c), lambda t, j, ids_ref: (t, 0))])
    return pl.pallas_call(
        body, name=name, grid_spec=grid_spec,
        out_shape=[jax.ShapeDtypeStruct((4, hr, c), BF16), jax.ShapeDtypeStruct((hr, c), F32)],
        compiler_params=_cparams(2),
    )(ids, grad, recv)


def _chip_sum(name, own, recv):
    hr, c = own.shape
    tr = _row_tile(hr, SUM_ROWS)

    def body(o_ref, r_ref, out_ref):
        out_ref[...] = ((o_ref[...] + r_ref[0].astype(F32)) + r_ref[1].astype(F32)) + r_ref[2].astype(F32)

    return pl.pallas_call(
        body, name=name, grid=(hr // tr,),
        in_specs=[pl.BlockSpec((tr, c), lambda i: (i, 0)), pl.BlockSpec((3, tr, c), lambda i: (0, i, 0))],
        out_specs=pl.BlockSpec((tr, c), lambda i: (i, 0)),
        out_shape=jax.ShapeDtypeStruct((hr, c), F32), compiler_params=_cparams(1),
    )(own, recv)


def _chip_sum_small(own, recv, ids):
    def body(ids_ref, o_ref, r_ref, out_ref):
        j = ids_ref[1]
        total = None
        for i in range(4):
            m = jnp.bitwise_xor(i, j)
            term = jnp.where(m == 0, o_ref[...], jnp.where(m == 2, r_ref[0], jnp.where(m == 1, r_ref[1], r_ref[2])))
            total = term if total is None else total + term
        out_ref[...] = total

    grid_spec = pltpu.PrefetchScalarGridSpec(
        num_scalar_prefetch=1, grid=(1,),
        in_specs=[pl.BlockSpec(own.shape, lambda i, ids_ref: (0, 0)), pl.BlockSpec(recv.shape, lambda i, ids_ref: (0, 0, 0))],
        out_specs=pl.BlockSpec(own.shape, lambda i, ids_ref: (0, 0)))
    return pl.pallas_call(body, name="chip_sum_small", grid_spec=grid_spec,
                          out_shape=jax.ShapeDtypeStruct(own.shape, F32), compiler_params=_cparams(1))(ids, own, recv)


def _adamw(name, w, m, v, mine, theirs, ids):
    rows, cols = w.shape
    half = rows // 2
    tr = _row_tile(half, ADAM_ROWS, unit=8)
    nb = half // tr
    c1 = 1.0 / (1.0 - ADAM_B1 ** ADAM_STEP)
    c2 = 1.0 / (1.0 - ADAM_B2 ** ADAM_STEP)

    def body(ids_ref, w_ref, m_ref, v_ref, mine_ref, theirs_ref, g_out, d_out, m_out, v_out):
        g = jnp.where(pl.program_id(0) == ids_ref[0], mine_ref[...], theirs_ref[...])
        m_new = ADAM_B1 * m_ref[...] + (1.0 - ADAM_B1) * g
        v_new = ADAM_B2 * v_ref[...] + (1.0 - ADAM_B2) * (g * g)
        d_out[...] = -ADAM_LR * ((m_new * c1) / (jnp.sqrt(v_new * c2) + ADAM_EPS) + ADAM_WD * w_ref[...])
        g_out[...] = g
        m_out[...] = m_new
        v_out[...] = v_new

    full = pl.BlockSpec((tr, cols), lambda h, i, ids_ref: (h * nb + i, 0))
    part = pl.BlockSpec((tr, cols), lambda h, i, ids_ref: (i, 0))
    grid_spec = pltpu.PrefetchScalarGridSpec(num_scalar_prefetch=1, grid=(2, nb),
                                             in_specs=[full, full, full, part, part], out_specs=[full] * 4)
    return pl.pallas_call(
        body, name=name, grid_spec=grid_spec,
        out_shape=[jax.ShapeDtypeStruct((rows, cols), F32)] * 4, compiler_params=_cparams(2),
    )(ids, w, m, v, mine, theirs)


def _small_shard(parts):
    return _flatten(parts, _BIG[-1][1])


_ENTRY = {e[0]: e for e in _BIG}
LATE_WEIGHTS = ("w_ssm_out", "w_attn_out", "w_mix_out", "w_ffn_up", "w_ffn_down")
FFN_GRADS = ("w_ffn_down", "w_ffn_up")
MIXER_GRADS = ("w_mix_out", "w_ssm_out", "w_attn_out")


class _StepPlan:
    def __init__(self, w, late_shards, shards, ids):
        self.w, self.g = w, {}
        self.late_shards, self.shards, self.ids = late_shards, shards, ids
        self.sums, self.halves, self.results = {}, {}, {}

    def run(self, name, fn, *args, **kw):
        at = getattr(self, "_at_" + name, None)
        if at is None:
            return fn(*args, **kw)
        exchange, landed = at()
        res, extra = fn(*args, bg=exchange, **kw)
        landed(extra)
        return res

    def _at_ssd_fwd(self):
        entries = [_ENTRY[n] for n in LATE_WEIGHTS]

        def landed(fulls):
            self.partly_gathered = fulls

        return _gather_ici(entries, self.late_shards), landed

    def _at_attn_fwd(self):
        entries = [_ENTRY[n] for n in LATE_WEIGHTS]
        return _gather_pass_on(entries, self.partly_gathered), lambda fulls: self.w.update(zip(LATE_WEIGHTS, fulls))

    def pair_sums(self, names, grads, recv):
        for n, gr, rv in zip(names, grads, recv):
            _, r, c, lay = _ENTRY[n]
            self.sums[n] = _pair_sum("pair_sum_" + n, gr, rv, self.ids, r, c, lay)

    def chip_sums(self, names, recv):
        for n, rv in zip(names, recv):
            self.halves[n] = _chip_sum("chip_sum_" + n, self.sums[n][1], rv)

    def adamw(self, names, theirs):
        for n, th in zip(names, theirs):
            sh = self.shards[n]
            self.results[n] = _adamw("adamw_" + n, sh["w"], sh["m"], sh["v"], self.halves[n], th, self.ids)

    def _pair_stage(self, names, grads):
        return (_pair_exchange([_ENTRY[n] for n in names], grads),
                lambda recv: self.pair_sums(names, grads, recv))

    def _at_ffn_up_dx(self):
        return self._pair_stage(FFN_GRADS, [self.g[n] for n in FFN_GRADS])

    def _at_ssm_gate_norm_bwd(self):
        return self._pair_stage(MIXER_GRADS, [self.g[n] for n in MIXER_GRADS])

    def _at_attn_bwd(self):
        return _chip_exchange([self.sums[n][0] for n in FFN_GRADS]), lambda recv: self.chip_sums(FFN_GRADS, recv)

    def _at_ssd_bwd(self):
        stages = (_chip_exchange([self.sums[n][0] for n in MIXER_GRADS]),
                  _whole_to_sibling([self.halves[n] for n in FFN_GRADS]))

        def landed(extra):
            recv, theirs = _split(stages, extra)
            self.chip_sums(MIXER_GRADS, recv)
            self.adamw(FFN_GRADS, theirs)

        return _join(*stages), landed

    def _at_ssm_conv_bwd(self):
        return _whole_to_sibling([self.halves[n] for n in MIXER_GRADS]), lambda theirs: self.adamw(MIXER_GRADS, theirs)

    def _at_in_proj_dx(self):
        g_in = _from_cat(self.g.pop("w_cat")).reshape(D_MODEL, 4, N_IN // 4)
        grads = [jnp.transpose(g_in, (1, 0, 2))]
        self.pair_sums(("w_in",), grads, _run_exchange("grad_pair_exchange_w_in", _pair_exchange([_ENTRY["w_in"]], grads)))
        return _chip_exchange([self.sums["w_in"][0]]), lambda recv: self.chip_sums(("w_in",), recv)

    def finish(self, g_small, g_rep, rep_shards):
        stages = (_pair_exchange([_ENTRY["small"]], [g_small]), _whole_to_sibling([g_rep]))
        recv_small, recv_rep = _split(stages, _run_exchange("grad_pair_exchange_tail", _join(*stages)))
        self.pair_sums(("small",), [g_small], recv_small)
        p_rep, = _rowwise("pair_sum_replicated", lambda r0, a, b: [a + b], SMALL_ROWS, SMALL_ROWS,
                          [(g_rep, LANES, 0), (recv_rep[0], LANES, 0)], [], [(LANES, F32)], [])
        stages = (_chip_exchange([self.sums["small"][0]]), _to_all_chips(p_rep))
        recv, recv_rep = _split(stages, _run_exchange("grad_chip_exchange_tail", _join(*stages)))
        self.chip_sums(("small",), recv)
        g_rep_tot = _chip_sum_small(p_rep, recv_rep[0], self.ids)
        last = ("w_in", "small")
        self.adamw(last, _run_exchange("grad_half_share_tail", _whole_to_sibling([self.halves[n] for n in last])))
        ids_lo = self.ids * jnp.array([0, 1], jnp.int32)
        self.results["replicated"] = _adamw("adamw_replicated", rep_shards["w"], rep_shards["m"], rep_shards["v"],
                                            g_rep_tot[0:SMALL_ROWS // 2], g_rep_tot[SMALL_ROWS // 2:], ids_lo)


def kernel(x, meta_tokens, norm_pre_mix, w_in, ssm_conv_w, ssm_conv_b, ssm_dt_bias, ssm_a_log, ssm_d_skip, ssm_norm, w_ssm_out, attn_sinks, w_attn_out, w_mix_out, norm_post_mix, norm_pre_ffn, w_ffn_up, ffn_conv_w, ffn_conv_b, w_ffn_down, norm_post_ffn, loss_target, m_meta_tokens, m_norm_pre_mix, m_w_in, m_ssm_conv_w, m_ssm_conv_b, m_ssm_dt_bias, m_ssm_a_log, m_ssm_d_skip, m_ssm_norm, m_w_ssm_out, m_attn_sinks, m_w_attn_out, m_w_mix_out, m_norm_post_mix, m_norm_pre_ffn, m_w_ffn_up, m_ffn_conv_w, m_ffn_conv_b, m_w_ffn_down, m_norm_post_ffn, v_meta_tokens, v_norm_pre_mix, v_w_in, v_ssm_conv_w, v_ssm_conv_b, v_ssm_dt_bias, v_ssm_a_log, v_ssm_d_skip, v_ssm_norm, v_w_ssm_out, v_attn_sinks, v_w_attn_out, v_w_mix_out, v_norm_post_mix, v_norm_pre_ffn, v_w_ffn_up, v_ffn_conv_w, v_ffn_conv_b, v_w_ffn_down, v_norm_post_ffn):
    args = dict(locals())
    squeeze = lambda a: a.reshape(a.shape[-2:])
    wts = {n: squeeze(args[n]) for n in WEIGHT_ORDER}
    mom = {n: squeeze(args["m_" + n]) for n in WEIGHT_ORDER}
    var = {n: squeeze(args["v_" + n]) for n in WEIGHT_ORDER}
    x_i, y_i, c_i = _mesh_pos()
    ids = jnp.stack([c_i, _chip_index(x_i, y_i)]).astype(jnp.int32)
    big_names = [n for n, _, _, _ in _BIG[:-1]]
    small_names = [n for n, _, _ in _SMALL_SHARDED]
    rep_names = [n for n, _ in _REPLICATED]

    stacks = {"w": wts, "m": mom, "v": var}
    shards = {n: {k: d[n] for k, d in stacks.items()} for n in big_names}
    shards["small"] = {k: _small_shard([d[n] for n in small_names]) for k, d in stacks.items()}
    rep_shards = {k: _flatten([d[n] for n in rep_names], SMALL_ROWS) for k, d in stacks.items()}

    w_in4, small_all = _gather_weights([_ENTRY["w_in"], _ENTRY["small"]], [wts["w_in"].astype(BF16), shards["small"]["w"]])
    w = {n: wts[n] for n in rep_names}
    w["w_cat"] = _to_cat(jnp.transpose(w_in4, (1, 0, 2)).reshape(D_MODEL, N_IN))
    small_parts = [_unflatten(small_all[i], [shp for _, shp, _ in _SMALL_SHARDED]) for i in range(4)]
    for k, (n, _, axis) in enumerate(_SMALL_SHARDED):
        w[n] = jnp.concatenate([small_parts[i][k] for i in range(4)], axis=axis)
    plan = _StepPlan(w, [wts[n].astype(BF16) for n in LATE_WEIGHTS], shards, ids)

    head = jnp.concatenate([jnp.zeros((PAD, D_MODEL), F32), w["meta_tokens"]], axis=0)
    loss_sum, dx, dhead = _local_step(x[0], head, loss_target[0], plan)
    loss = lax.psum(loss_sum * (0.5 / D_MODEL), ("x", "y", "c"))
    g = plan.g
    g["meta_tokens"] = dhead[PAD:]
    g_small = jnp.stack([_small_shard([_shard_of(g[n], i, shp, ax) for n, shp, ax in _SMALL_SHARDED]) for i in range(4)])
    plan.finish(g_small, _flatten([g[n] for n in rep_names], SMALL_ROWS), rep_shards)

    results = {}
    for kind in range(4):
        results.update({(kind, n): plan.results[n][kind] for n in big_names})
        parts = _unflatten(plan.results["small"][kind], [shp for _, shp, _ in _SMALL_SHARDED])
        results.update({(kind, n): parts[k] for k, n in enumerate(small_names)})
        parts = _unflatten(plan.results["replicated"][kind], [(1, width) for _, width in _REPLICATED])
        results.update({(kind, n): parts[k] for k, n in enumerate(rep_names)})
    outs = [results[kind, n].reshape(args[n].shape) for kind in range(4) for n in WEIGHT_ORDER]
    return (loss, dx[None], *outs)
```

```python
import math
from typing import Any, Callable, NamedTuple, Sequence

import jax
import jax.numpy as jnp
from jax import lax
from jax.experimental import pallas as pl
from jax.experimental.pallas import tpu as pltpu

F32 = jnp.float32
BF16 = jnp.bfloat16

D_MODEL = 1024
N_META = 16
T = 128
PAD = T - N_META
D_INNER = 2048
SSM_HEADS = 32
HEAD_P = 64
SSM_GROUPS = 4
GROUP_W = D_INNER // SSM_GROUPS
D_STATE = 128
CONV_DIM = D_INNER + 2 * SSM_GROUPS * D_STATE
ATTN_HEADS = 16
KV_HEADS = 4
ATTN_W = 1024
KV_W = 256
FFN_DIM = 2816
N_IN = 8736
EPS = 1e-6
NEG = -1e30
SCALE = 0.125

P_Q, P_K, P_V, P_DT, P_Z, P_GATE, P_XBC = 0, 1024, 1280, 1536, 2048, 4096, 6144
QKV_W = 1536
P_W = 9216

ADAM_LR, ADAM_B1, ADAM_B2, ADAM_EPS, ADAM_WD, ADAM_STEP = 0.001, 0.9, 0.999, 1e-08, 0.01, 10

VMEM_BUDGET = 40 * 1024 * 1024
VMEM_LIMIT = 56 * 1024 * 1024
MESH = pl.DeviceIdType.MESH
ANY = pl.BlockSpec(memory_space=pl.ANY)


def _cparams(n_axes, **kw):
    return pltpu.CompilerParams(dimension_semantics=("arbitrary",) * n_axes, vmem_limit_bytes=VMEM_LIMIT, **kw)


class _Exchange(NamedTuple):
    ins: Sequence[Any]
    out_shapes: Sequence[Any]
    make_copies: Callable
    n_copies: int
    aliases: dict = {}


def _call(body, name, grid, in_specs, out_specs, out_shape, operands, scratch_shapes=(), aliases=None, bg=None):
    aliases = dict(aliases or {})
    if bg is None:
        return pl.pallas_call(body, name=name, grid=grid, in_specs=in_specs, out_specs=out_specs, out_shape=out_shape,
                              scratch_shapes=list(scratch_shapes), input_output_aliases=aliases,
                              compiler_params=_cparams(len(grid)))(*operands)
    n_in, n_out, n_scr = len(in_specs), len(out_specs), len(scratch_shapes)
    nb_in, nb_out = len(bg.ins), len(bg.out_shapes)

    def hosted(*refs):
        ins, bg_ins = refs[:n_in], refs[n_in:n_in + nb_in]
        outs = refs[n_in + nb_in:n_in + nb_in + n_out]
        bg_outs = refs[n_in + nb_in + n_out:n_in + nb_in + n_out + nb_out]
        scratch = refs[n_in + nb_in + n_out + nb_out:n_in + nb_in + n_out + nb_out + n_scr]
        send_sems, recv_sems = refs[-2:]
        pids = [pl.program_id(a) for a in range(len(grid))]
        first, last = pids[0] == 0, pids[0] == grid[0] - 1
        for p, g in zip(pids[1:], grid[1:]):
            first, last = first & (p == 0), last & (p == g - 1)
        copies = []
        for k, (src, dst, peer) in enumerate(bg.make_copies(bg_ins, bg_outs)):
            if peer is None:
                copies.append(pltpu.make_async_copy(src, dst, send_sems.at[k]))
            else:
                copies.append(pltpu.make_async_remote_copy(src_ref=src, dst_ref=dst, send_sem=send_sems.at[k],
                                                           recv_sem=recv_sems.at[k], device_id=peer, device_id_type=MESH))
        assert len(copies) == bg.n_copies

        @pl.when(first)
        def _():
            for cp in copies:
                cp.start()

        body(*ins, *outs, *scratch)

        @pl.when(last)
        def _():
            for cp in copies:
                cp.wait()

    aliases = {(k if k < n_in else k + nb_in): v for k, v in aliases.items()}
    aliases.update({n_in + k: n_out + v for k, v in bg.aliases.items()})
    res = pl.pallas_call(
        hosted, name=name, grid=grid, in_specs=list(in_specs) + [ANY] * nb_in, out_specs=list(out_specs) + [ANY] * nb_out,
        out_shape=list(out_shape) + list(bg.out_shapes), input_output_aliases=aliases,
        scratch_shapes=list(scratch_shapes) + [pltpu.SemaphoreType.DMA((bg.n_copies,))] * 2,
        compiler_params=_cparams(len(grid), has_side_effects=True))(*operands, *bg.ins)
    return res[:n_out], res[n_out:]


def _sigmoid(x):
    return pl.reciprocal(1.0 + jnp.exp(-x), approx=True)


def _silu(x):
    return x * _sigmoid(x)


def _silu_grad(x):
    s = _sigmoid(x)
    return x * s, s * (1.0 + x * (1.0 - s))


def _dsilu(x):
    return _silu_grad(x)[1]


def _softplus(x):
    e = jnp.exp(-jnp.abs(x))
    small = e * (1.0 - e * (0.5 - e * (1.0 / 3.0)))
    return jnp.maximum(x, 0.0) + jnp.where(e < 0.01, small, jnp.log(1.0 + e))


def _rms(x, w):
    r = lax.rsqrt(jnp.mean(x * x, axis=-1, keepdims=True) + EPS)
    return x * r * w


def _rms_bwd(dy, x, w):
    r = lax.rsqrt(jnp.mean(x * x, axis=-1, keepdims=True) + EPS)
    xh = x * r
    g = dy * w
    dx = r * (g - xh * jnp.mean(g * xh, axis=-1, keepdims=True))
    dw = jnp.sum(dy * xh, axis=0, keepdims=True)
    return dx, dw


def _dot(a, b):
    return jnp.dot(a, b, preferred_element_type=F32)


def _dot_nt(a, b):
    return lax.dot_general(a, b, (((1,), (1,)), ((), ())), preferred_element_type=F32)


def _dot_tn(a, b):
    return lax.dot_general(a, b, (((0,), (0,)), ((), ())), preferred_element_type=F32)


def _split3(x):
    hi = x.astype(BF16)
    r = x - hi.astype(F32)
    mid = r.astype(BF16)
    lo = (r - mid.astype(F32)).astype(BF16)
    return hi, mid, lo


def _xdot(x, e):
    hi, mid, lo = _split3(x)
    return _dot(hi, e) + _dot(mid, e) + _dot(lo, e)


def _xdot_l(e, x):
    hi, mid, lo = _split3(x)
    return _dot(e, hi) + _dot(e, mid) + _dot(e, lo)


def _iota(shape, dim):
    return lax.broadcasted_iota(jnp.int32, shape, dim)


def _divisors(n, unit):
    return [t for t in range(unit, n + 1, unit) if n % t == 0]


MIN_MATMUL_STEPS = 8


def _matmul_tiles(m, n, k, a_bytes, b_bytes, o_bytes, m_unit):
    best = None
    for tm in _divisors(m, m_unit):
        for tn in _divisors(n, 128):
            for tk in _divisors(k, 128):
                acc = 0 if tk == k else tm * tn * 4
                vm = 2 * (tm * tk * a_bytes + tk * tn * b_bytes + tm * tn * o_bytes) + acc
                if vm > VMEM_BUDGET:
                    continue
                steps = (m // tm) * (n // tn) * (k // tk)
                score = (tk == k, min(steps, MIN_MATMUL_STEPS), min(tm, 256), tm * tn * tk)
                if best is None or score > best[0]:
                    best = (score, (tm, tn, tk))
    return best[1]


def _matmul(name, a, b, mode, out_dtype, bg=None):
    if mode == "nn":
        (m, k), n = a.shape, b.shape[1]
    elif mode == "nt":
        (m, k), n = a.shape, b.shape[0]
    else:
        (k, m), n = a.shape, b.shape[1]
    ab, bb, ob = a.dtype.itemsize, b.dtype.itemsize, jnp.dtype(out_dtype).itemsize
    tm, tn, tk = _matmul_tiles(m, n, k, ab, bb, ob, 128 if mode == "tn" else 16)
    nk = k // tk
    dot = {"nn": _dot, "nt": _dot_nt, "tn": _dot_tn}[mode]

    def body(a_ref, b_ref, o_ref, *scratch):
        prod = dot(a_ref[...].astype(BF16), b_ref[...].astype(BF16))
        if nk == 1:
            o_ref[...] = prod.astype(o_ref.dtype)
        else:
            acc_ref, = scratch
            kk = pl.program_id(2)

            @pl.when(kk == 0)
            def _():
                acc_ref[...] = prod

            @pl.when(kk > 0)
            def _():
                acc_ref[...] += prod

            @pl.when(kk == nk - 1)
            def _():
                o_ref[...] = acc_ref[...].astype(o_ref.dtype)

    a_spec = pl.BlockSpec((tk, tm), lambda i, j, kk: (kk, i)) if mode == "tn" else pl.BlockSpec((tm, tk), lambda i, j, kk: (i, kk))
    b_spec = pl.BlockSpec((tn, tk), lambda i, j, kk: (j, kk)) if mode == "nt" else pl.BlockSpec((tk, tn), lambda i, j, kk: (kk, j))
    res = _call(body, name, (m // tm, n // tn, nk), [a_spec, b_spec], [pl.BlockSpec((tm, tn), lambda i, j, kk: (i, j))],
                [jax.ShapeDtypeStruct((m, n), out_dtype)], [a, b],
                scratch_shapes=[] if nk == 1 else [pltpu.VMEM((tm, tn), F32)], bg=bg)
    return res[0] if bg is None else (res[0][0], res[1])


def _row_tile(n_rows, cap, unit=16):
    return max(t for t in _divisors(n_rows, unit) if t <= cap)


ROW_SUB = 384
GROUP_UNROLL = 4


def _rowwise(name, fn, n_rows, tm, row_ins, full_ins, row_outs, acc_outs, bg=None):
    n_in = len(row_ins) + len(full_ins)
    n_ro = len(row_outs)
    into = [(k, o[3]) for k, o in enumerate(row_outs) if len(o) > 2 and o[2] == "into"]

    n_row_in = len(row_ins)
    sub = min(tm, ROW_SUB)

    def body(*refs):
        i = pl.program_id(0)
        outs = refs[n_in + len(into):]

        def group(s, sums):
            rows = pl.ds(pl.multiple_of(s * sub, sub), sub)
            vals = [r[rows, :] for r in refs[:n_row_in]] + [r[...] for r in refs[n_row_in:n_in]]
            res = fn(i * tm + s * sub, *vals)
            for o, r, v in zip(row_outs, outs[:n_ro], res[:n_ro]):
                if len(o) > 2 and o[2] == "first":
                    @pl.when(i == 0)
                    def _(r=r, v=v):
                        r[rows, :] = v.astype(r.dtype)
                else:
                    r[rows, :] = v.astype(r.dtype)
            return tuple(a + v for a, v in zip(sums, res[n_ro:]))

        sums = lax.fori_loop(0, tm // sub, group, tuple(jnp.zeros((1, w), F32) for w in acc_outs), unroll=GROUP_UNROLL)

        @pl.when(i == 0)
        def _():
            for r, v in zip(outs[n_ro:], sums):
                r[...] = v

        @pl.when(i > 0)
        def _():
            for r, v in zip(outs[n_ro:], sums):
                r[...] += v

    def in_spec(entry):
        w, cb = entry[1], entry[2]
        if len(entry) > 3 and entry[3] == "prev":
            return pl.BlockSpec((tm, w), lambda i: (jnp.maximum(i - 1, 0), cb))
        if len(entry) > 3 and entry[3] == "first":
            return pl.BlockSpec((tm, w), lambda i: (0, cb))
        return pl.BlockSpec((tm, w), lambda i: (i, cb))

    def out_spec(o):
        if len(o) == 2:
            return pl.BlockSpec((tm, o[0]), lambda i: (i, 0)), jax.ShapeDtypeStruct((n_rows, o[0]), o[1])
        if o[2] == "new":
            return pl.BlockSpec((tm, o[0]), lambda i: (i, o[4])), jax.ShapeDtypeStruct((n_rows, o[3]), o[1])
        if o[2] == "into":
            return pl.BlockSpec((tm, o[0]), lambda i: (i, o[4])), jax.ShapeDtypeStruct(o[3].shape, o[3].dtype)
        if o[2] == "first":
            return pl.BlockSpec((tm, o[0]), lambda i: (0, 0)), jax.ShapeDtypeStruct((tm, o[0]), o[1])
        return pl.BlockSpec((tm, o[0]), lambda i: (jnp.maximum(i - 1, 0), 0)), jax.ShapeDtypeStruct((o[3], o[0]), o[1])

    in_specs = [in_spec(e) for e in row_ins]
    in_specs += [pl.BlockSpec(a.shape, lambda i: (0, 0)) for a in full_ins]
    in_specs += [pl.BlockSpec(memory_space=pl.ANY) for _ in into]
    specs_shapes = [out_spec(o) for o in row_outs]
    out_specs = [s for s, _ in specs_shapes] + [pl.BlockSpec((1, w), lambda i: (0, 0)) for w in acc_outs]
    out_shape = [s for _, s in specs_shapes] + [jax.ShapeDtypeStruct((1, w), F32) for w in acc_outs]
    return _call(body, name, (n_rows // tm,), in_specs, out_specs, out_shape,
                 [e[0] for e in row_ins] + list(full_ins) + [arr for _, arr in into],
                 aliases={n_in + a: k for a, (k, _) in enumerate(into)}, bg=bg)


def _valid_rows(first_row, tm, lo):
    return (first_row + _iota((tm, 1), 0)) >= lo


CONV_ROWS = 128
CONV_SUB = 16
CONV_LANES = 256


def _conv_specs(tm, width, blk, n_rows, after):
    specs = [pl.BlockSpec((tm, width), lambda i: (i, blk)),
             pl.BlockSpec((8, width), lambda i: (jnp.maximum(i * (tm // 8) - 1, 0), blk))]
    if after:
        specs.append(pl.BlockSpec((16, width), lambda i: (jnp.minimum((i + 1) * (tm // 16), n_rows // 16 - 1), blk)))
    return specs


def _conv_window(win, w_ref, b_ref, taps, c0, cw, n):
    acc = b_ref[:, c0:c0 + cw] + w_ref[taps - 1:taps, c0:c0 + cw] * win[8:8 + n]
    for k in range(taps - 1):
        acc = acc + w_ref[k:k + 1, c0:c0 + cw] * win[8 - (taps - 1) + k:8 - (taps - 1) + k + n]
    return acc


def _ffn_act(name, u_raw, conv_w, conv_b, n_rows):
    tm, sub, cw = CONV_ROWS, CONV_SUB, CONV_LANES
    taps, width = conv_w.shape
    half = width // 2

    def body(cur_ref, prev_ref, w_ref, b_ref, f_ref, ext_ref):
        i = pl.program_id(0)
        ext_ref[0:8, :] = jnp.where(i > 0, prev_ref[...], 0.0)
        ext_ref[8:8 + tm, :] = cur_ref[...]
        for q in range(half // cw):
            a0, g0 = q * cw, half + q * cw

            def group(s, carry):
                r = pl.multiple_of(s * sub, sub)
                a = _conv_window(ext_ref[pl.ds(r, sub + 8), a0:a0 + cw], w_ref, b_ref, taps, a0, cw, sub)
                g = _conv_window(ext_ref[pl.ds(r, sub + 8), g0:g0 + cw], w_ref, b_ref, taps, g0, cw, sub)
                f = jnp.where(_valid_rows(i * tm + r, sub, PAD), _silu(a) * g, 0.0)
                f_ref[pl.ds(r, sub), a0:a0 + cw] = f.astype(f_ref.dtype)
                return carry

            lax.fori_loop(0, tm // sub, group, 0, unroll=GROUP_UNROLL)

    return pl.pallas_call(
        body, name=name, grid=(n_rows // tm,),
        in_specs=_conv_specs(tm, width, 0, n_rows, False) + [pl.BlockSpec((taps, width), lambda i: (0, 0)),
                                                             pl.BlockSpec((1, width), lambda i: (0, 0))],
        out_specs=pl.BlockSpec((tm, half), lambda i: (i, 0)),
        out_shape=jax.ShapeDtypeStruct((n_rows, half), BF16),
        scratch_shapes=[pltpu.VMEM((tm + 8, width), F32)],
        compiler_params=_cparams(1),
    )(u_raw, u_raw, conv_w, conv_b)


def _conv_bwd(name, raw, raw_blk, dsrcs, chunk_src, conv_w, conv_b, n_rows, gated, into=None, into_blk=0, bg=None):
    taps, width = conv_w.shape
    half = width // 2 if gated else width
    tm, sub, cw = CONV_ROWS, CONV_SUB, CONV_LANES
    te = tm + 16
    nd = len(dsrcs)
    n_parts = 2 if gated else 1

    def body(*refs):
        cur_ref, prev_ref, next_ref = refs[0:3]
        dcur, dnext = refs[3:3 + nd], refs[3 + nd:3 + 2 * nd]
        w_ref, b_ref = refs[3 + 2 * nd:5 + 2 * nd]
        out_ref, acc_ref, ext_ref, du_ref = refs[-4:]
        i = pl.program_id(0)
        ext_ref[0:8, :] = jnp.where(i > 0, prev_ref[...], 0.0)
        ext_ref[8:8 + tm, :] = cur_ref[...]
        ext_ref[8 + tm:24 + tm, :] = next_ref[...]

        for q, (src, off) in enumerate(chunk_src):
            cols = [q * cw, half + q * cw][:n_parts]

            def conv_grad(r, d):
                pre = [_conv_window(ext_ref[pl.ds(r, sub + 8), c0:c0 + cw], w_ref, b_ref, taps, c0, cw, sub) for c0 in cols]
                row = i * tm + r + _iota((sub, 1), 0)
                live = (row >= PAD) & (row < n_rows)
                if gated:
                    act, dact = _silu_grad(pre[0])
                    dus = [d * pre[1] * dact, d * act]
                else:
                    dus = [d * _dsilu(pre[0])]
                for part, du in enumerate(dus):
                    du_ref[part, pl.ds(r, sub), :] = jnp.where(live, du, 0.0)

            def tile_rows(s, carry):
                r = pl.multiple_of(s * sub, sub)
                conv_grad(r, dcur[src][pl.ds(r, sub), off:off + cw].astype(F32))
                return carry

            lax.fori_loop(0, tm // sub, tile_rows, 0, unroll=GROUP_UNROLL)
            conv_grad(tm, dnext[src][:, off:off + cw].astype(F32))

            for part, c0 in enumerate(cols):
                taps_w = [w_ref[k:k + 1, c0:c0 + cw] for k in range(taps)]

                def back(s, sums):
                    new = list(sums)
                    for u in range(2):
                        r = pl.multiple_of((2 * s + u) * sub, sub)
                        win = du_ref[part, pl.ds(r, sub + 8), :]
                        raw_rows = ext_ref[pl.ds(8 + r, sub), c0:c0 + cw]
                        draw = jnp.zeros((sub, cw), F32)
                        for k in range(taps):
                            shifted = win[taps - 1 - k:taps - 1 - k + sub]
                            draw = draw + taps_w[k] * shifted
                            new[k] = new[k] + shifted * raw_rows
                        new[taps] = new[taps] + win[0:sub]
                        out_ref[pl.ds(r, sub), c0:c0 + cw] = jnp.where(_valid_rows(i * tm + r, sub, PAD), draw, 0.0).astype(out_ref.dtype)
                    return tuple(new)

                sums = lax.fori_loop(0, tm // (2 * sub), back, tuple(jnp.zeros((sub, cw), F32) for _ in range(taps + 1)))
                for k in range(taps + 1):
                    total = jnp.sum(sums[k], axis=0, keepdims=True)
                    acc_ref[k:k + 1, c0:c0 + cw] = jnp.where(i == 0, total, acc_ref[k:k + 1, c0:c0 + cw] + total)

    in_specs = _conv_specs(tm, width, raw_blk, n_rows, True)
    in_specs += [pl.BlockSpec((tm, d.shape[1]), lambda i: (i, 0)) for d in dsrcs]
    in_specs += [pl.BlockSpec((16, d.shape[1]), lambda i: (jnp.minimum((i + 1) * (tm // 16), n_rows // 16 - 1), 0)) for d in dsrcs]
    in_specs += [pl.BlockSpec((taps, width), lambda i: (0, 0)), pl.BlockSpec((1, width), lambda i: (0, 0))]
    operands = [raw, raw, raw] + list(dsrcs) + list(dsrcs) + [conv_w, conv_b]
    aliases = {}
    if into is None:
        out0 = jax.ShapeDtypeStruct((n_rows, width), BF16)
    else:
        in_specs.append(pl.BlockSpec(memory_space=pl.ANY))
        operands.append(into)
        aliases = {len(operands) - 1: 0}
        out0 = jax.ShapeDtypeStruct(into.shape, into.dtype)
    return _call(body, name, (n_rows // tm,), in_specs,
                 [pl.BlockSpec((tm, width), lambda i: (i, into_blk)), pl.BlockSpec((8, width), lambda i: (0, 0))],
                 [out0, jax.ShapeDtypeStruct((8, width), F32)], operands,
                 scratch_shapes=[pltpu.VMEM((tm + 24, width), F32), pltpu.VMEM((n_parts, te + 8, cw), F32)],
                 aliases=aliases, bg=bg)


def _ssd_specs(n_chunks, rev):
    cidx = (lambda c: n_chunks - 1 - c) if rev else (lambda c: c)
    xg0, bg0, cg0 = P_XBC // GROUP_W, (P_XBC + D_INNER) // D_STATE, (P_XBC + D_INNER + SSM_GROUPS * D_STATE) // D_STATE

    def cur(width, blk0):
        return pl.BlockSpec((T, width), lambda g, c: (cidx(c), blk0 + g))

    def prev(width, blk0):
        return pl.BlockSpec((8, width), lambda g, c: (jnp.maximum(cidx(c) * (T // 8) - 1, 0), blk0 + g))

    specs = [cur(GROUP_W, xg0), prev(GROUP_W, xg0), cur(D_STATE, bg0), prev(D_STATE, bg0),
             cur(D_STATE, cg0), prev(D_STATE, cg0),
             pl.BlockSpec((T, 128), lambda g, c: (cidx(c), P_DT // 128))]
    wx, wb, wc = 0, D_INNER // D_STATE, (D_INNER + SSM_GROUPS * D_STATE) // D_STATE
    specs += [pl.BlockSpec((4, GROUP_W), lambda g, c: (0, g)),
              pl.BlockSpec((4, D_STATE), lambda g, c: (0, wb + g)),
              pl.BlockSpec((4, D_STATE), lambda g, c: (0, wc + g)),
              pl.BlockSpec((1, GROUP_W), lambda g, c: (0, g)),
              pl.BlockSpec((1, D_STATE), lambda g, c: (0, wb + g)),
              pl.BlockSpec((1, D_STATE), lambda g, c: (0, wc + g))]
    specs += [pl.BlockSpec((1, 128), lambda g, c: (0, 0))] * 3
    return specs, cidx


def _ssd_chunk_forward(refs, ext_ref, g, c):
    (xc_ref, xp_ref, bc_ref, bp_ref, cc_ref, cp_ref, dt_ref, wx_ref, wb_ref, wc_ref,
     bx_ref, bb_ref, bcb_ref, dtb_ref, alog_ref, dsk_ref) = refs

    def conv_pre(cur_ref, prev_ref, w_ref, b_ref, width):
        ext_ref[0:8, 0:width] = jnp.where(c > 0, prev_ref[...], 0.0)
        ext_ref[8:8 + T, 0:width] = cur_ref[...]
        w = w_ref[...]
        acc = b_ref[...] + w[3:4] * cur_ref[...]
        for k in range(3):
            acc = acc + w[k:k + 1] * ext_ref[pl.ds(5 + k, T), 0:width]
        return acc

    valid = _valid_rows(c * T, T, PAD)
    v = {}
    v["valid"] = valid
    v["x_pre"] = conv_pre(xc_ref, xp_ref, wx_ref, bx_ref, GROUP_W)
    v["b_pre"] = conv_pre(bc_ref, bp_ref, wb_ref, bb_ref, D_STATE)
    v["c_pre"] = conv_pre(cc_ref, cp_ref, wc_ref, bcb_ref, D_STATE)
    xs = _silu(v["x_pre"])
    bm = jnp.where(valid, _silu(v["b_pre"]), 0.0)
    cm = jnp.where(valid, _silu(v["c_pre"]), 0.0)
    dtr = dt_ref[...] + dtb_ref[...]
    dt = jnp.where(valid, _softplus(dtr), 0.0)
    a_neg = -jnp.exp(alog_ref[...])
    a = dt * a_neg
    tril = _iota((T, T), 0) >= _iota((T, T), 1)
    cs = _xdot_l(tril.astype(BF16), a)
    hh, ll = _iota((128, GROUP_W), 0), _iota((128, GROUP_W), 1)
    expand = (hh == 8 * g + jnp.right_shift(ll, 6)).astype(BF16)
    sh, sj = _iota((128, 128), 0), _iota((128, 128), 1)
    select = ((sh == 8 * g + sj) & (sj < 8)).astype(BF16)
    hh_t, ll_t = _iota((GROUP_W, 128), 1), _iota((GROUP_W, 128), 0)
    v["expand_t"] = (hh_t == 8 * g + jnp.right_shift(ll_t, 6)).astype(BF16)
    v["select_t"] = ((sj == 8 * g + sh) & (sh < 8)).astype(BF16)
    cs_e = _xdot(cs, expand)
    dt_e = _xdot(dt, expand)
    cs_loc = _xdot(cs, select)
    cs_loc_t = cs_loc.T
    cs_last_e = cs_e[T - 1:T, :]
    v.update(xs=xs, bm=bm, cm=cm, dtr=dtr, dt=dt, a_neg=a_neg, tril=tril, expand=expand, select=select,
             cs_e=cs_e, dt_e=dt_e, cs_loc=cs_loc, cs_loc_t=cs_loc_t, cs_last_e=cs_last_e)
    v["xdt"] = xs * dt_e
    v["decay_e"] = jnp.exp(cs_last_e - cs_e)
    v["ecs_e"] = jnp.exp(cs_e)
    v["elast_e"] = jnp.exp(cs_last_e)
    v["d_e"] = _xdot(dsk_ref[...], expand)
    v["gmat"] = _dot_nt(cm.astype(BF16), bm.astype(BF16))
    return v


def _ssd_decay_pair(v, jp):
    out = []
    for j in (2 * jp, 2 * jp + 1):
        diff = v["cs_loc"][:, j:j + 1] - v["cs_loc_t"][j:j + 1, :]
        out.append(jnp.where(v["tril"], jnp.exp(jnp.where(v["tril"], diff, 0.0)), 0.0))
    return out


def _block_diag_pair(xp):
    lane = _iota(xp.shape, 1)
    return jnp.concatenate([jnp.where(lane < HEAD_P, xp, 0.0), jnp.where(lane >= HEAD_P, xp, 0.0)], axis=0)


def _ssd_fwd(p, conv_w, conv_b, dt_bias, a_log, d_skip, n_chunks, bg=None):
    n_rows = n_chunks * T
    in_specs, _ = _ssd_specs(n_chunks, rev=False)

    def body(*refs):
        y_ref, hin_ref, st_ref, ext_ref = refs[16:]
        g, c = pl.program_id(0), pl.program_id(1)

        @pl.when(c == 0)
        def _():
            st_ref[...] = jnp.zeros_like(st_ref)

        v = _ssd_chunk_forward(refs[:16], ext_ref, g, c)
        state = st_ref[...]
        hin_ref[...] = state
        ys = []
        for jp in range(4):
            l0, l1 = _ssd_decay_pair(v, jp)
            lhs = jnp.concatenate([v["gmat"] * l0, v["gmat"] * l1], axis=1).astype(BF16)
            rhs = _block_diag_pair(v["xdt"][:, 128 * jp:128 * jp + 128]).astype(BF16)
            ys.append(_dot(lhs, rhs))
        y = jnp.concatenate(ys, axis=1)
        y = y + _dot(v["cm"].astype(BF16), state.astype(BF16)) * v["ecs_e"] + v["xs"] * v["d_e"]
        y_ref[...] = y
        s_new = _dot_tn(v["bm"].astype(BF16), (v["xdt"] * v["decay_e"]).astype(BF16))
        st_ref[...] = state * v["elast_e"] + s_new

    return _call(
        body, "ssd_fwd", (SSM_GROUPS, n_chunks), in_specs,
        [pl.BlockSpec((T, GROUP_W), lambda g, c: (c, g)),
         pl.BlockSpec((None, None, D_STATE, GROUP_W), lambda g, c: (g, c, 0, 0))],
        [jax.ShapeDtypeStruct((n_rows, D_INNER), F32),
         jax.ShapeDtypeStruct((SSM_GROUPS, n_chunks, D_STATE, GROUP_W), F32)],
        [p, p, p, p, p, p, p, conv_w, conv_w, conv_w, conv_b, conv_b, conv_b, dt_bias, a_log, d_skip],
        scratch_shapes=[pltpu.VMEM((D_STATE, GROUP_W), F32), pltpu.VMEM((T + 8, GROUP_W), F32)], bg=bg)


def _ssd_bwd(p, conv_w, conv_b, dt_bias, a_log, d_skip, hin, dy, n_chunks, bg=None):
    n_rows = n_chunks * T
    in_specs, cidx = _ssd_specs(n_chunks, rev=True)
    in_specs = in_specs + [pl.BlockSpec((None, None, D_STATE, GROUP_W), lambda g, c: (g, cidx(c), 0, 0)),
                           pl.BlockSpec((T, GROUP_W), lambda g, c: (cidx(c), g))]

    def body(*refs):
        hin_ref, dy_ref = refs[16:18]
        dx_ref, db_ref, dc_ref, ddt_ref, dpar_ref, dst_ref, ext_ref = refs[18:]
        g, step = pl.program_id(0), pl.program_id(1)
        c = n_chunks - 1 - step

        @pl.when(step == 0)
        def _():
            dst_ref[...] = jnp.zeros_like(dst_ref)

        v = _ssd_chunk_forward(refs[:16], ext_ref, g, c)
        hin_f = hin_ref[...]
        hin_b = hin_f.astype(BF16)
        dyv = dy_ref[...]
        dst = dst_ref[...]
        dst_b = dst.astype(BF16)
        xs, bm, cm, xdt = v["xs"], v["bm"], v["cm"], v["xdt"]
        bm_b, cm_b = bm.astype(BF16), cm.astype(BF16)

        dd_e = jnp.sum(dyv * xs, axis=0, keepdims=True)
        dxs = dyv * v["d_e"]
        ch = _dot(cm_b, hin_b)
        dch = (dyv * v["ecs_e"]).astype(BF16)
        dcm = _dot_nt(dch, hin_b)
        dhin = _dot_tn(cm_b, dch) + dst * v["elast_e"]
        dcs_e = dyv * ch * v["ecs_e"]
        dxd = _dot(bm_b, dst_b)
        dbm = _dot_nt((xdt * v["decay_e"]).astype(BF16), dst_b)
        dxdt_state = dxd * v["decay_e"]
        q = dxdt_state * xdt
        dcs_e = dcs_e - q
        dlast_e = jnp.sum(q, axis=0, keepdims=True) + jnp.sum(dst * hin_f, axis=0, keepdims=True) * v["elast_e"]
        dg = jnp.zeros((T, T), F32)
        rs_cols = jnp.zeros((T, 128), F32)
        cs_rows = jnp.zeros((128, T), F32)
        lane_i, sub_i = _iota((T, 128), 1), _iota((128, T), 0)
        dxdt_parts = []
        for jp in range(4):
            l0, l1 = _ssd_decay_pair(v, jp)
            m0, m1 = v["gmat"] * l0, v["gmat"] * l1
            xbd = _block_diag_pair(xdt[:, 128 * jp:128 * jp + 128]).astype(BF16)
            dyp = dyv[:, 128 * jp:128 * jp + 128]
            dm = _dot_nt(dyp.astype(BF16), xbd)
            dm0, dm1 = dm[:, 0:T], dm[:, T:2 * T]
            dg = dg + dm0 * l0 + dm1 * l1
            for j, qq in ((2 * jp, dm0 * m0), (2 * jp + 1, dm1 * m1)):
                rs_cols = jnp.where(lane_i == j, jnp.sum(qq, axis=1, keepdims=True), rs_cols)
                cs_rows = jnp.where(sub_i == j, jnp.sum(qq, axis=0, keepdims=True), cs_rows)
            mv = jnp.concatenate([m0, m1], axis=0).astype(BF16)
            dxdt_parts.append(_dot_tn(mv, _block_diag_pair(dyp).astype(BF16)))
        dxdt = jnp.concatenate(dxdt_parts, axis=1) + dxdt_state
        dg_b = dg.astype(BF16)
        dcm = dcm + _dot(dg_b, bm_b)
        dbm = dbm + _dot_tn(dg_b, cm_b)
        expand_t = v["expand_t"]
        dcs_loc = rs_cols - cs_rows.T
        last_row = _iota((T, 1), 0) == T - 1
        dcs_full_e = dcs_e + jnp.where(last_row, dlast_e, 0.0)
        dcs = _xdot(dcs_full_e, expand_t) + _xdot(dcs_loc, v["select_t"])
        triu = (_iota((T, T), 0) <= _iota((T, T), 1)).astype(BF16)
        da = _xdot_l(triu, dcs)
        ddt = da * v["a_neg"] + _xdot(dxdt * xs, expand_t)
        dxs = dxs + dxdt * v["dt_e"]
        ddtr = jnp.where(v["valid"], ddt * _sigmoid(v["dtr"]), 0.0)
        dx_ref[...] = dxs
        db_ref[...] = jnp.where(v["valid"], dbm, 0.0)
        dc_ref[...] = jnp.where(v["valid"], dcm, 0.0)
        ddt_ref[...] = ddtr
        dpar = jnp.concatenate([
            jnp.sum(ddtr, axis=0, keepdims=True),
            jnp.sum(da * v["dt"], axis=0, keepdims=True) * v["a_neg"],
            _xdot(dd_e, expand_t),
            jnp.zeros((5, 128), F32)], axis=0)

        @pl.when(step == 0)
        def _():
            dpar_ref[...] = dpar

        @pl.when(step > 0)
        def _():
            dpar_ref[...] += dpar

        dst_ref[...] = dhin

    return _call(
        body, "ssd_bwd", (SSM_GROUPS, n_chunks), in_specs,
        [pl.BlockSpec((T, GROUP_W), lambda g, c: (cidx(c), g)),
         pl.BlockSpec((T, D_STATE), lambda g, c: (cidx(c), g)),
         pl.BlockSpec((T, D_STATE), lambda g, c: (cidx(c), g)),
         pl.BlockSpec((T, 128), lambda g, c: (cidx(c), g)),
         pl.BlockSpec((None, 8, 128), lambda g, c: (g, 0, 0))],
        [jax.ShapeDtypeStruct((n_rows, D_INNER), F32),
         jax.ShapeDtypeStruct((n_rows, SSM_GROUPS * D_STATE), F32),
         jax.ShapeDtypeStruct((n_rows, SSM_GROUPS * D_STATE), F32),
         jax.ShapeDtypeStruct((n_rows, SSM_GROUPS * 128), F32),
         jax.ShapeDtypeStruct((SSM_GROUPS, 8, 128), F32)],
        [p, p, p, p, p, p, p, conv_w, conv_w, conv_w, conv_b, conv_b, conv_b, dt_bias, a_log, d_skip, hin, dy],
        scratch_shapes=[pltpu.VMEM((D_STATE, GROUP_W), F32), pltpu.VMEM((T + 8, GROUP_W), F32)], bg=bg)


def _alibi_slope(h):
    return 2.0 ** (-8.0 * (h + 1) / ATTN_HEADS)


def _dup_half(x256, kvh):
    xb = x256[:, 128 * (kvh // 2):128 * (kvh // 2) + 128]
    rolled = pltpu.roll(xb, 64, 1)
    lane = _iota(xb.shape, 1)
    if kvh % 2 == 0:
        return jnp.where(lane < 64, xb, rolled)
    return jnp.where(lane < 64, rolled, xb)


def _attn_masks(c):
    qi, j = _iota((T, T), 0), _iota((T, T), 1)
    tri = j <= qi
    meta_ok = (j >= PAD) & (j - PAD <= c * T + qi - PAD)
    band_ok = c >= jnp.where(tri, 1, 2)
    dist = jnp.bitwise_and(qi - j, T - 1).astype(F32)
    return tri, meta_ok, band_ok, dist


def _fold(x3, tri):
    return jnp.concatenate([x3[:, 0:T], jnp.where(tri, x3[:, 2 * T:3 * T], x3[:, T:2 * T])], axis=1)


def _unfold(x2, tri):
    band = x2[:, T:2 * T]
    return jnp.concatenate([x2[:, 0:T], jnp.where(tri, 0.0, band), jnp.where(tri, band, 0.0)], axis=1)


def _attn_scores(qp, k3, masks, h0):
    tri, meta_ok, band_ok, dist = masks
    lane = _iota(qp.shape, 1)
    s = []
    for half, h in ((0, h0), (1, h0 + 1)):
        qh = jnp.where((lane < 64) if half == 0 else (lane >= 64), qp, 0.0).astype(BF16)
        raw = _dot_nt(qh, k3)
        band = jnp.where(tri, raw[:, 2 * T:3 * T], raw[:, T:2 * T]) - _alibi_slope(h) * dist
        s.append((qh, jnp.concatenate([jnp.where(meta_ok, raw[:, 0:T], NEG), jnp.where(band_ok, band, NEG)], axis=1)))
    return s


def _attn_fwd(p, sinks, n_chunks, bg=None):
    n_rows = n_chunks * T
    kb, vb = P_K // KV_W, P_V // KV_W

    def body(q_ref, kc_ref, kp_ref, km_ref, vc_ref, vp_ref, vm_ref, sink_ref, o_ref, lse_ref):
        c = pl.program_id(0)
        masks = _attn_masks(c)
        tri = masks[0]
        q = q_ref[...] * SCALE
        sinks_v = sink_ref[...]
        lane = _iota((T, 128), 1)
        lse_all = jnp.zeros((T, 128), F32)
        outs = []
        for kvh in range(KV_HEADS):
            k3 = jnp.concatenate([_dup_half(r[...], kvh) for r in (km_ref, kp_ref, kc_ref)], axis=0).astype(BF16)
            v3 = jnp.concatenate([_dup_half(r[...], kvh) for r in (vm_ref, vp_ref, vc_ref)], axis=0)
            v3bd = _block_diag_rows(v3).astype(BF16)
            for pr in range(2):
                h0 = 4 * kvh + 2 * pr
                blk = 2 * kvh + pr
                qp = q[:, 128 * blk:128 * blk + 128]
                probs = []
                for (_, sc), h in zip(_attn_scores(qp, k3, masks, h0), (h0, h0 + 1)):
                    sink = sinks_v[:, h:h + 1]
                    m = jnp.maximum(jnp.max(sc, axis=1, keepdims=True), sink)
                    e = jnp.exp(sc - m)
                    den = jnp.sum(e, axis=1, keepdims=True) + jnp.exp(sink - m)
                    probs.append(_unfold(e * (1.0 / den), tri))
                    lse_all = jnp.where(lane == h, m + jnp.log(den), lse_all)
                outs.append(_dot(jnp.concatenate(probs, axis=1).astype(BF16), v3bd))
        o_ref[...] = jnp.concatenate(outs, axis=1).astype(o_ref.dtype)
        lse_ref[...] = lse_all

    blk = lambda width, col: pl.BlockSpec((T, width), lambda c: (c, col))
    prev = lambda width, col: pl.BlockSpec((T, width), lambda c: (jnp.maximum(c - 1, 0), col))
    first = lambda width, col: pl.BlockSpec((T, width), lambda c: (0, col))
    return _call(
        body, "attn_fwd", (n_chunks,),
        [blk(ATTN_W, P_Q // ATTN_W), blk(KV_W, kb), prev(KV_W, kb), first(KV_W, kb),
         blk(KV_W, vb), prev(KV_W, vb), first(KV_W, vb), pl.BlockSpec((1, 128), lambda c: (0, 0))],
        [pl.BlockSpec((T, ATTN_W), lambda c: (c, 0)), pl.BlockSpec((T, 128), lambda c: (c, 0))],
        [jax.ShapeDtypeStruct((n_rows, ATTN_W), BF16), jax.ShapeDtypeStruct((n_rows, 128), F32)],
        [p, p, p, p, p, p, p, sinks], bg=bg)


def _block_diag_rows(x3):
    lane = _iota(x3.shape, 1)
    return jnp.concatenate([jnp.where(lane < 64, x3, 0.0), jnp.where(lane >= 64, x3, 0.0)], axis=0)


def _fold_halves(x):
    return x + pltpu.roll(x, 64, 1)


def _attn_bwd(p, sinks, ao, lse, dao, dp, n_chunks, bg=None):
    kb, vb = P_K // KV_W, P_V // KV_W
    rc = lambda s: n_chunks - 1 - s

    def body(q_ref, kc_ref, kp_ref, km_ref, vc_ref, vp_ref, vm_ref, sink_ref, o_ref, lse_ref, do_ref, dp_in_ref,
             dqkv_ref, dsink_ref, kcar_ref, vcar_ref, kmeta_ref, vmeta_ref):
        step = pl.program_id(0)
        c = n_chunks - 1 - step

        @pl.when(step == 0)
        def _():
            for r in (kcar_ref, vcar_ref, kmeta_ref, vmeta_ref):
                r[...] = jnp.zeros_like(r)

        masks = _attn_masks(c)
        tri = masks[0]
        q = q_ref[...] * SCALE
        sinks_v = sink_ref[...]
        lse_v = lse_ref[...]
        ov = o_ref[...].astype(F32)
        dov = do_ref[...].astype(F32)
        lane = _iota((T, 128), 1)
        lane256 = _iota((3 * T, KV_W), 1)
        dsink = jnp.zeros((1, 128), F32)
        dk3_all = jnp.zeros((3 * T, KV_W), F32)
        dv3_all = jnp.zeros((3 * T, KV_W), F32)
        dqs = []
        for kvh in range(KV_HEADS):
            k3 = jnp.concatenate([_dup_half(r[...], kvh) for r in (km_ref, kp_ref, kc_ref)], axis=0).astype(BF16)
            v3 = jnp.concatenate([_dup_half(r[...], kvh) for r in (vm_ref, vp_ref, vc_ref)], axis=0).astype(BF16)
            dk3 = jnp.zeros((3 * T, 128), F32)
            dv3 = jnp.zeros((3 * T, 128), F32)
            for pr in range(2):
                h0 = 4 * kvh + 2 * pr
                blk = 2 * kvh + pr
                qp = q[:, 128 * blk:128 * blk + 128]
                dop = dov[:, 128 * blk:128 * blk + 128]
                prod = dop * ov[:, 128 * blk:128 * blk + 128]
                dq_pair = jnp.zeros((T, 128), F32)
                for half, ((qh, sc), h) in enumerate(zip(_attn_scores(qp, k3, masks, h0), (h0, h0 + 1))):
                    mine = (lane < 64) if half == 0 else (lane >= 64)
                    lse_h = lse_v[:, h:h + 1]
                    pm = jnp.exp(sc - lse_h)
                    doh = jnp.where(mine, dop, 0.0).astype(BF16)
                    delta = jnp.sum(jnp.where(mine, prod, 0.0), axis=1, keepdims=True)
                    dp = _fold(_dot_nt(doh, v3), tri)
                    ds = _unfold(pm * (dp - delta), tri).astype(BF16)
                    p_sink = jnp.exp(sinks_v[:, h:h + 1] - lse_h)
                    dsink = jnp.where(_iota((1, 128), 1) == h, jnp.sum(-p_sink * delta, axis=0, keepdims=True), dsink)
                    dq_pair = jnp.where(mine, _dot(ds, k3), dq_pair)
                    dk3 = dk3 + _dot_tn(ds, qh)
                    dv3 = dv3 + _dot_tn(_unfold(pm, tri).astype(BF16), doh)
                dqs.append(dq_pair * SCALE)
            in_place = (lane256 >= 64 * kvh) & (lane256 < 64 * kvh + 64)
            wide = lambda x: jnp.concatenate([x, x], axis=1)
            dk3_all = jnp.where(in_place, wide(_fold_halves(dk3)), dk3_all)
            dv3_all = jnp.where(in_place, wide(_fold_halves(dv3)), dv3_all)
        dsink_all = dsink

        @pl.when(step == 0)
        def _():
            dsink_ref[...] = dsink_all

        @pl.when(step > 0)
        def _():
            dsink_ref[...] += dsink_all

        kmeta = kmeta_ref[...] + dk3_all[0:T]
        vmeta = vmeta_ref[...] + dv3_all[0:T]
        kmeta_ref[...] = kmeta
        vmeta_ref[...] = vmeta
        is_first = c == 0
        dk = jnp.where(is_first, kmeta, dk3_all[2 * T:3 * T] + kcar_ref[...])
        dv = jnp.where(is_first, vmeta, dv3_all[2 * T:3 * T] + vcar_ref[...])
        dqkv_ref[...] = jnp.concatenate(dqs + [dk, dv], axis=1).astype(dqkv_ref.dtype)
        kcar_ref[...] = dk3_all[T:2 * T]
        vcar_ref[...] = dv3_all[T:2 * T]

    blk = lambda width, col: pl.BlockSpec((T, width), lambda s: (rc(s), col))
    prev = lambda width, col: pl.BlockSpec((T, width), lambda s: (jnp.maximum(rc(s) - 1, 0), col))
    first = lambda width, col: pl.BlockSpec((T, width), lambda s: (0, col))
    return _call(
        body, "attn_bwd", (n_chunks,),
        [blk(ATTN_W, P_Q // ATTN_W), blk(KV_W, kb), prev(KV_W, kb), first(KV_W, kb),
         blk(KV_W, vb), prev(KV_W, vb), first(KV_W, vb), pl.BlockSpec((1, 128), lambda s: (0, 0)),
         blk(ATTN_W, 0), blk(128, 0), blk(ATTN_W, 0), ANY],
        [blk(QKV_W, P_Q // QKV_W), pl.BlockSpec((1, 128), lambda s: (0, 0))],
        [jax.ShapeDtypeStruct(dp.shape, dp.dtype), jax.ShapeDtypeStruct((1, 128), F32)],
        [p, p, p, p, p, p, p, sinks, ao, lse, dao, dp],
        scratch_shapes=[pltpu.VMEM((T, KV_W), F32)] * 4, aliases={11: 0}, bg=bg)


def _pad_lanes(v, width=128):
    return jnp.pad(v, ((0, 0), (0, width - v.shape[1])))


def _local_step(x, head, tgt, plan):
    w, g, run = plan.w, plan.g, plan.run
    n_tok = x.shape[0]
    n_rows = n_tok + T
    n_chunks = n_rows // T
    tm = _row_tile(n_rows, 384)
    dt_bias, a_log, d_skip = (_pad_lanes(w[k]) for k in ("ssm_dt_bias", "ssm_a_log", "ssm_d_skip"))
    sinks = _pad_lanes(w["attn_sinks"])
    x_in = [(x, D_MODEL, 0, "prev"), (head, D_MODEL, 0, "first")]

    def h0_tile(r0, xt, hd):
        return jnp.where(r0 < T, hd, xt)

    n1, = _rowwise("norm_pre_mix", lambda r0, xt, hd, wn: [_rms(h0_tile(r0, xt, hd), wn)], n_rows, T,
                   x_in, [w["norm_pre_mix"]], [(D_MODEL, BF16)], [])
    p = _matmul("in_proj", n1, w["w_cat"], "nn", F32)
    y_ssd, hin = run("ssd_fwd", _ssd_fwd, p, w["ssm_conv_w"], w["ssm_conv_b"], dt_bias, a_log, d_skip, n_chunks)
    ao, lse = run("attn_fwd", _attn_fwd, p, sinks, n_chunks)

    def gate_norm(r0, y, z, wn):
        return [_rms(y * _silu(z), wn)]

    yn, = _rowwise("ssm_gate_norm", gate_norm, n_rows, tm, [(y_ssd, D_INNER, 0), (p, D_INNER, P_Z // D_INNER)],
                   [w["ssm_norm"]], [(D_INNER, BF16)], [])
    y_ssm = _matmul("ssm_out", yn, w["w_ssm_out"], "nn", F32)
    y_attn = _matmul("attn_out", ao, w["w_attn_out"], "nn", F32)

    def mix_gate(r0, ys, ya, gs, ga):
        return [_sigmoid(gs) * ys + _sigmoid(ga) * ya]

    gate_ins = [(p, D_MODEL, P_GATE // D_MODEL), (p, D_MODEL, P_GATE // D_MODEL + 1)]
    mixed, = _rowwise("mix_gate", mix_gate, n_rows, tm, [(y_ssm, D_MODEL, 0), (y_attn, D_MODEL, 0)] + gate_ins,
                      [], [(D_MODEL, BF16)], [])
    mix = _matmul("mix_out", mixed, w["w_mix_out"], "nn", F32)

    def post_mix(r0, mx, xt, hd, w_post, w_pre):
        h1 = jnp.where(_valid_rows(r0, mx.shape[0], PAD), h0_tile(r0, xt, hd) + _rms(mx, w_post), 0.0)
        return [h1, _rms(h1, w_pre)]

    h1, n2 = _rowwise("post_mix", post_mix, n_rows, T, [(mix, D_MODEL, 0)] + x_in,
                      [w["norm_post_mix"], w["norm_pre_ffn"]], [(D_MODEL, F32), (D_MODEL, BF16)], [])
    u_raw = _matmul("ffn_up", n2, w["w_ffn_up"], "nn", F32)
    f = _ffn_act("ffn_act", u_raw, w["ffn_conv_w"], w["ffn_conv_b"], n_rows)
    ffn = _matmul("ffn_down", f, w["w_ffn_down"], "nn", F32)

    def final(r0, fo, h, t, w_post):
        real = r0 >= T
        err = jnp.where(real, h + _rms(fo, w_post) - t, 0.0)
        dy = err * (1.0 / D_MODEL)
        dffn, dw = _rms_bwd(dy, fo, w_post)
        return [dffn, dy, jnp.sum(err * err, axis=0, keepdims=True), dw]

    dffn, dh2, loss_cols, g_norm_post_ffn = _rowwise(
        "loss_head", final, n_rows, T, [(ffn, D_MODEL, 0), (h1, D_MODEL, 0), (tgt, D_MODEL, 0, "prev")],
        [w["norm_post_ffn"]], [(D_MODEL, BF16), (D_MODEL, F32)], [D_MODEL, D_MODEL])

    g["norm_post_ffn"] = g_norm_post_ffn
    g["w_ffn_down"] = _matmul("ffn_down_dw", f, dffn, "tn", F32)
    df = _matmul("ffn_down_dx", dffn, w["w_ffn_down"], "nt", F32)
    du_raw, dconv = _conv_bwd("ffn_act_bwd", u_raw, 0, [df], [(0, c0) for c0 in range(0, FFN_DIM, CONV_LANES)],
                              w["ffn_conv_w"], w["ffn_conv_b"], n_rows, True)
    g["ffn_conv_w"], g["ffn_conv_b"] = dconv[0:3], dconv[3:4]
    g["w_ffn_up"] = _matmul("ffn_up_dw", n2, du_raw, "tn", F32)
    dn2 = run("ffn_up_dx", _matmul, "ffn_up_dx", du_raw, w["w_ffn_up"], "nt", F32)

    def post_mix_bwd(r0, dn, d2, h, mx, w_pre, w_post):
        dx, dw_pre = _rms_bwd(dn, h, w_pre)
        dh1 = jnp.where(_valid_rows(r0, dn.shape[0], PAD), dx + d2, 0.0)
        dmix, dw_post = _rms_bwd(dh1, mx, w_post)
        return [dh1, dmix, dw_pre, dw_post]

    dh1, dmix, g["norm_pre_ffn"], g["norm_post_mix"] = _rowwise(
        "post_mix_bwd", post_mix_bwd, n_rows, tm,
        [(dn2, D_MODEL, 0), (dh2, D_MODEL, 0), (h1, D_MODEL, 0), (mix, D_MODEL, 0)],
        [w["norm_pre_ffn"], w["norm_post_mix"]], [(D_MODEL, F32), (D_MODEL, BF16)], [D_MODEL, D_MODEL])
    g["w_mix_out"] = _matmul("mix_out_dw", mixed, dmix, "tn", F32)
    dmixed = _matmul("mix_out_dx", dmix, w["w_mix_out"], "nt", F32)

    def mix_gate_bwd(r0, dm, ys, ya, gs, ga):
        ss, sa = _sigmoid(gs), _sigmoid(ga)
        dgate = jnp.concatenate([dm * ys * ss * (1.0 - ss), dm * ya * sa * (1.0 - sa)], axis=1)
        return [dm * ss, dm * sa, dgate]

    dys, dya, dp = _rowwise(
        "mix_gate_bwd", mix_gate_bwd, n_rows, tm,
        [(dmixed, D_MODEL, 0), (y_ssm, D_MODEL, 0), (y_attn, D_MODEL, 0)] + gate_ins,
        [], [(D_MODEL, BF16), (D_MODEL, BF16), (2 * D_MODEL, BF16, "new", P_W, P_GATE // (2 * D_MODEL))], [])
    g["w_ssm_out"] = _matmul("ssm_out_dw", yn, dys, "tn", F32)
    dyn = _matmul("ssm_out_dx", dys, w["w_ssm_out"], "nt", F32)
    g["w_attn_out"] = _matmul("attn_out_dw", ao, dya, "tn", F32)
    dao = _matmul("attn_out_dx", dya, w["w_attn_out"], "nt", BF16)

    def gate_norm_bwd(r0, dn, y, z, wn):
        sz, dsz = _silu_grad(z)
        dyz, dw = _rms_bwd(dn, y * sz, wn)
        live = _valid_rows(r0, dn.shape[0], PAD)
        return [jnp.where(live, dyz * sz, 0.0), jnp.where(live, dyz * y * dsz, 0.0), dw]

    dy_ssd, dp, g["ssm_norm"] = run(
        "ssm_gate_norm_bwd", _rowwise, "ssm_gate_norm_bwd", gate_norm_bwd, n_rows, tm,
        [(dyn, D_INNER, 0), (y_ssd, D_INNER, 0), (p, D_INNER, P_Z // D_INNER)],
        [w["ssm_norm"]], [(D_INNER, F32), (D_INNER, BF16, "into", dp, P_Z // D_INNER)], [D_INNER])
    dp, dsink = run("attn_bwd", _attn_bwd, p, sinks, ao, lse, dao, dp, n_chunks)
    g["attn_sinks"] = dsink[:, 0:ATTN_HEADS]
    dxs, dbm, dcm, ddt_parts, dpar = run("ssd_bwd", _ssd_bwd, p, w["ssm_conv_w"], w["ssm_conv_b"], dt_bias, a_log,
                                         d_skip, hin, dy_ssd, n_chunks)
    dpar = jnp.sum(dpar, axis=0)
    g["ssm_dt_bias"], g["ssm_a_log"], g["ssm_d_skip"] = (dpar[i:i + 1, 0:SSM_HEADS] for i in range(3))

    def dt_grad(r0, parts):
        tot = parts[:, 0:128] + parts[:, 128:256] + parts[:, 256:384] + parts[:, 384:512]
        return [jnp.concatenate([tot, jnp.zeros((parts.shape[0], P_Z - P_DT - 128), F32)], axis=1)]

    dt_w = P_Z - P_DT
    dp, = _rowwise("dt_grad", dt_grad, n_rows, tm, [(ddt_parts, SSM_GROUPS * 128, 0)], [],
                   [(dt_w, BF16, "into", dp, P_DT // dt_w)], [])
    x_chunks = [(src, c0) for src, arr in enumerate((dxs, dbm, dcm)) for c0 in range(0, arr.shape[1], CONV_LANES)]
    dp, dconv = run("ssm_conv_bwd", _conv_bwd, "ssm_conv_bwd", p, P_XBC // CONV_DIM, [dxs, dbm, dcm], x_chunks,
                    w["ssm_conv_w"], w["ssm_conv_b"], n_rows, False, into=dp, into_blk=P_XBC // CONV_DIM)
    g["ssm_conv_w"], g["ssm_conv_b"] = dconv[0:4], dconv[4:5]
    g["w_cat"] = _matmul("in_proj_dw", n1, dp, "tn", F32)
    dn1 = run("in_proj_dx", _matmul, "in_proj_dx", dp, w["w_cat"], "nt", F32)

    def pre_mix_bwd(r0, dn, d1, xt, hd, wn):
        dx, dw = _rms_bwd(dn, h0_tile(r0, xt, hd), wn)
        dh0 = jnp.where(_valid_rows(r0, dn.shape[0], PAD), dx + d1, 0.0)
        return [dh0, dh0, dw]

    dx_out, dhead, g["norm_pre_mix"] = _rowwise(
        "pre_mix_bwd", pre_mix_bwd, n_rows, T, [(dn1, D_MODEL, 0), (dh1, D_MODEL, 0)] + x_in,
        [w["norm_pre_mix"]], [(D_MODEL, F32, "prev", n_tok), (D_MODEL, F32, "first")], [D_MODEL])
    return jnp.sum(loss_cols), dx_out, dhead


_IN_SECTIONS = [((5152, 6176), P_Q), ((6176, 6432), P_K), ((6432, 6688), P_V), ((5120, 5152), P_DT),
                ((0, 2048), P_Z), ((6688, 8736), P_GATE), ((2048, 5120), P_XBC)]


def _to_cat(w_in):
    parts, at = [], 0
    for (a, b), off in _IN_SECTIONS:
        if off > at:
            parts.append(jnp.zeros((w_in.shape[0], off - at), w_in.dtype))
        parts.append(w_in[:, a:b])
        at = off + (b - a)
    return jnp.concatenate(parts, axis=1)


def _from_cat(g_cat):
    pieces = {a: g_cat[:, off:off + (b - a)] for (a, b), off in _IN_SECTIONS}
    return jnp.concatenate([pieces[a] for a in sorted(pieces)], axis=1)


LANES = 1024
_BIG = [("w_in", 1024, 2184, "chip"), ("w_ssm_out", 512, 1024, "row"), ("w_attn_out", 256, 1024, "row"),
        ("w_mix_out", 256, 1024, "row"), ("w_ffn_up", 1024, 1408, "col"), ("w_ffn_down", 704, 1024, "row"),
        ("small", 32, LANES, "chip")]
_SMALL_SHARDED = [("ssm_conv_w", (4, 768), 1), ("ffn_conv_w", (3, 1408), 1), ("meta_tokens", (16, 256), 1)]
_REPLICATED = [("norm_pre_mix", 1024), ("ssm_conv_b", 3072), ("ssm_dt_bias", 32), ("ssm_a_log", 32),
               ("ssm_d_skip", 32), ("ssm_norm", 2048), ("attn_sinks", 16), ("norm_post_mix", 1024),
               ("norm_pre_ffn", 1024), ("ffn_conv_b", 5632), ("norm_post_ffn", 1024)]
SMALL_ROWS = 16
WEIGHT_ORDER = ["meta_tokens", "norm_pre_mix", "w_in", "ssm_conv_w", "ssm_conv_b", "ssm_dt_bias", "ssm_a_log",
                "ssm_d_skip", "ssm_norm", "w_ssm_out", "attn_sinks", "w_attn_out", "w_mix_out", "norm_post_mix",
                "norm_pre_ffn", "w_ffn_up", "ffn_conv_w", "ffn_conv_b", "w_ffn_down", "norm_post_ffn"]


def _flatten(parts, rows):
    flat = jnp.concatenate([a.reshape(-1) for a in parts])
    return jnp.pad(flat, (0, rows * LANES - flat.shape[0])).reshape(rows, LANES)


def _unflatten(flat, shapes):
    flat = flat.reshape(-1)
    out, off = [], 0
    for shp in shapes:
        n = math.prod(shp)
        out.append(flat[off:off + n].reshape(shp))
        off += n
    return out


def _shard_of(full, chip, shape, axis):
    return lax.slice_in_dim(full, chip * shape[axis], (chip + 1) * shape[axis], axis=axis)


def _full_shape(r, c, layout):
    return {"row": (4 * r, c), "col": (r, 4 * c), "chip": (4, r, c)}[layout]


def _shard_view(ref, r, c, layout, chip):
    if layout == "row":
        return ref.at[pl.ds(pl.multiple_of(chip * r, 16), r), :]
    if layout == "col":
        return ref.at[:, pl.ds(pl.multiple_of(chip * c, 128), c)]
    return ref.at[chip]


def _half_view(ref, r, c, layout, chip, half):
    hr = r // 2
    if layout == "row":
        return ref.at[pl.ds(pl.multiple_of(chip * r + half * hr, 16), hr), :]
    r0 = pl.multiple_of(half * hr, 16)
    if layout == "col":
        return ref.at[pl.ds(r0, hr), pl.ds(pl.multiple_of(chip * c, 128), c)]
    return ref.at[chip, pl.ds(r0, hr), :]


def _mesh_pos():
    return lax.axis_index("x"), lax.axis_index("y"), lax.axis_index("c")


def _other_chips(x, y):
    return [(1 - x, y), (x, 1 - y), (1 - x, 1 - y)]


def _chip_index(x, y):
    return 2 * x + y


def _run_exchange(name, ex):
    n_in, n_out = len(ex.ins), len(ex.out_shapes)

    def body(*refs):
        in_refs, out_refs = refs[:n_in], refs[n_in:n_in + n_out]
        send_sems, recv_sems = refs[n_in + n_out:]
        copies = [pltpu.make_async_remote_copy(src_ref=s, dst_ref=d, send_sem=send_sems.at[i], recv_sem=recv_sems.at[i],
                                               device_id=dev, device_id_type=MESH)
                  for i, (s, d, dev) in enumerate(ex.make_copies(in_refs, out_refs))]
        assert len(copies) == ex.n_copies
        for cp in copies:
            cp.start()
        for cp in copies:
            cp.wait()

    return pl.pallas_call(
        body, name=name, in_specs=[ANY] * n_in, out_specs=[ANY] * n_out, out_shape=list(ex.out_shapes),
        scratch_shapes=[pltpu.SemaphoreType.DMA((ex.n_copies,)), pltpu.SemaphoreType.DMA((ex.n_copies,))],
        compiler_params=pltpu.CompilerParams(has_side_effects=True),
    )(*ex.ins)


def _join(*exs):
    def make(in_refs, out_refs):
        copies, i0, o0 = [], 0, 0
        for ex in exs:
            copies += ex.make_copies(in_refs[i0:i0 + len(ex.ins)], out_refs[o0:o0 + len(ex.out_shapes)])
            i0, o0 = i0 + len(ex.ins), o0 + len(ex.out_shapes)
        return copies

    return _Exchange([a for ex in exs for a in ex.ins], [s for ex in exs for s in ex.out_shapes], make,
                     sum(ex.n_copies for ex in exs))


def _split(exs, results):
    out, o0 = [], 0
    for ex in exs:
        out.append(list(results[o0:o0 + len(ex.out_shapes)]))
        o0 += len(ex.out_shapes)
    return out


def _gather_ici(entries, shards):
    def make(in_refs, out_refs):
        x, y, c = _mesh_pos()
        j = _chip_index(x, y)
        copies = []
        for ref_in, ref_out, (_, r, cc, lay) in zip(in_refs, out_refs, entries):
            copies.append((ref_in, _shard_view(ref_out, r, cc, lay, j), None))
            mine = ref_in.at[pl.ds(pl.multiple_of(c * (r // 2), 16), r // 2), :]
            copies += [(mine, _half_view(ref_out, r, cc, lay, j, c), (*ch, c)) for ch in _other_chips(x, y)]
        return copies

    shapes = [jax.ShapeDtypeStruct(_full_shape(r, cc, lay), s.dtype) for s, (_, r, cc, lay) in zip(shards, entries)]
    return _Exchange(list(shards), shapes, make, 4 * len(entries))


def _gather_pass_on(entries, fulls):
    def make(in_refs, out_refs):
        x, y, c = _mesh_pos()
        copies = []
        for ref, (_, r, cc, lay) in zip(out_refs, entries):
            for ch in _other_chips(x, y):
                landed = _half_view(ref, r, cc, lay, _chip_index(*ch), c)
                copies.append((landed, landed, (x, y, 1 - c)))
        return copies

    return _Exchange(list(fulls), [jax.ShapeDtypeStruct(f.shape, f.dtype) for f in fulls], make, 3 * len(entries),
                     {a: a for a in range(len(entries))})


def _gather_weights(entries, shards):
    n = len(entries)

    def body(*refs):
        ins, outs = refs[:n], refs[n:2 * n]
        send_sems, recv_sems, local_sems = refs[2 * n:]
        x, y, c = _mesh_pos()
        j = _chip_index(x, y)
        sibling = (x, y, 1 - c)
        chips = _other_chips(x, y)
        idx = [_chip_index(*ch) for ch in chips]

        def remote(k, src, dst, dev):
            return pltpu.make_async_remote_copy(src_ref=src, dst_ref=dst, send_sem=send_sems.at[k],
                                                recv_sem=recv_sems.at[k], device_id=dev, device_id_type=MESH)

        own = [pltpu.make_async_copy(ins[a], _shard_view(outs[a], r, cc, lay, j), local_sems.at[a])
               for a, (_, r, cc, lay) in enumerate(entries)]
        for cp in own:
            cp.start()
        first, passed = [], []
        for a, (_, r, cc, lay) in enumerate(entries):
            mine = ins[a].at[pl.ds(pl.multiple_of(c * (r // 2), 16), r // 2), :]
            for k, ch in enumerate(chips):
                first.append(remote(6 * a + k, mine, _half_view(outs[a], r, cc, lay, j, c), (*ch, c)))
                landed = _half_view(outs[a], r, cc, lay, idx[k], c)
                passed.append(remote(6 * a + 3 + k, landed, landed, sibling))
        for cp in first:
            cp.start()
        for a, (_, r, cc, lay) in enumerate(entries):
            for k in range(3):
                landed = _half_view(outs[a], r, cc, lay, idx[k], c)
                remote(6 * a + k, landed, landed, sibling).wait_recv()
                passed[3 * a + k].start()
        for a, (_, r, cc, lay) in enumerate(entries):
            for k in range(3):
                theirs = _half_view(outs[a], r, cc, lay, idx[k], 1 - c)
                remote(6 * a + 3 + k, theirs, theirs, sibling).wait_recv()
        for cp in first + passed:
            cp.wait_send()
        for cp in own:
            cp.wait()

    return pl.pallas_call(
        body, name="gather_weights", in_specs=[ANY] * n, out_specs=[ANY] * n,
        out_shape=[jax.ShapeDtypeStruct(_full_shape(r, cc, lay), s.dtype) for s, (_, r, cc, lay) in zip(shards, entries)],
        scratch_shapes=[pltpu.SemaphoreType.DMA((6 * n,)), pltpu.SemaphoreType.DMA((6 * n,)), pltpu.SemaphoreType.DMA((n,))],
        compiler_params=pltpu.CompilerParams(has_side_effects=True),
    )(*shards)


def _pair_exchange(entries, grads):
    def make(in_refs, out_refs):
        x, y, c = _mesh_pos()
        return [(_half_view(ref_in, r, cc, lay, i, 1 - c), ref_out.at[i], (x, y, 1 - c))
                for ref_in, ref_out, (_, r, cc, lay) in zip(in_refs, out_refs, entries) for i in range(4)]

    return _Exchange(list(grads), [jax.ShapeDtypeStruct((4, r // 2, cc), F32) for _, r, cc, _ in entries], make,
                     4 * len(entries))


def _whole_to_sibling(arrays):
    def make(in_refs, out_refs):
        x, y, c = _mesh_pos()
        return [(r, o, (x, y, 1 - c)) for r, o in zip(in_refs, out_refs)]

    return _Exchange(list(arrays), [jax.ShapeDtypeStruct(a.shape, a.dtype) for a in arrays], make, len(arrays))


def _chip_exchange(psends):
    def make(in_refs, out_refs):
        x, y, c = _mesh_pos()
        return [(ref_in.at[_chip_index(*ch)], ref_out.at[k], (*ch, c))
                for ref_in, ref_out in zip(in_refs, out_refs) for k, ch in enumerate(_other_chips(x, y))]

    return _Exchange(list(psends), [jax.ShapeDtypeStruct((3,) + p.shape[1:], p.dtype) for p in psends], make,
                     3 * len(psends))


def _to_all_chips(array):
    def make(in_refs, out_refs):
        x, y, c = _mesh_pos()
        return [(in_refs[0], out_refs[0].at[k], (*ch, c)) for k, ch in enumerate(_other_chips(x, y))]

    return _Exchange([array], [jax.ShapeDtypeStruct((3,) + array.shape, array.dtype)], make, 3)


SUM_ROWS = 256
ADAM_ROWS = 128


def _pair_sum(name, grad, recv, ids, r, c, layout):
    hr = r // 2
    tr = _row_tile(hr, SUM_ROWS)
    nb = hr // tr

    def body(ids_ref, g_ref, r_ref, send_ref, own_ref):
        s = g_ref[...] + r_ref[...]
        send_ref[...] = s.astype(send_ref.dtype)

        @pl.when(pl.program_id(1) == ids_ref[1])
        def _():
            own_ref[...] = s

    if layout == "row":
        g_spec = pl.BlockSpec((tr, c), lambda t, j, ids_ref: ((j * r + ids_ref[0] * hr) // tr + t, 0))
    elif layout == "col":
        g_spec = pl.BlockSpec((tr, c), lambda t, j, ids_ref: (ids_ref[0] * nb + t, j))
    else:
        g_spec = pl.BlockSpec((None, tr, c), lambda t, j, ids_ref: (j, ids_ref[0] * nb + t, 0))
    grid_spec = pltpu.PrefetchScalarGridSpec(
        num_scalar_prefetch=1, grid=(nb, 4),
        in_specs=[g_spec, pl.BlockSpec((None, tr, c), lambda t, j, ids_ref: (j, t, 0))],
        out_specs=[pl.BlockSpec((None, tr, c), lambda t, j, ids_ref: (j, t, 0)),
                   pl.BlockSpec((tr, c), lambda t, j, ids_ref: (t, 0))])
    return pl.pallas_call(
        body, name=name, grid_spec=grid_spec,
        out_shape=[jax.ShapeDtypeStruct((4, hr, c), BF16), jax.ShapeDtypeStruct((hr, c), F32)],
        compiler_params=_cparams(2),
    )(ids, grad, recv)


def _chip_sum(name, own, recv):
    hr, c = own.shape
    tr = _row_tile(hr, SUM_ROWS)

    def body(o_ref, r_ref, out_ref):
        out_ref[...] = ((o_ref[...] + r_ref[0].astype(F32)) + r_ref[1].astype(F32)) + r_ref[2].astype(F32)

    return pl.pallas_call(
        body, name=name, grid=(hr // tr,),
        in_specs=[pl.BlockSpec((tr, c), lambda i: (i, 0)), pl.BlockSpec((3, tr, c), lambda i: (0, i, 0))],
        out_specs=pl.BlockSpec((tr, c), lambda i: (i, 0)),
        out_shape=jax.ShapeDtypeStruct((hr, c), F32), compiler_params=_cparams(1),
    )(own, recv)


def _chip_sum_small(own, recv, ids):
    def body(ids_ref, o_ref, r_ref, out_ref):
        j = ids_ref[1]
        total = None
        for i in range(4):
            m = jnp.bitwise_xor(i, j)
            term = jnp.where(m == 0, o_ref[...], jnp.where(m == 2, r_ref[0], jnp.where(m == 1, r_ref[1], r_ref[2])))
            total = term if total is None else total + term
        out_ref[...] = total

    grid_spec = pltpu.PrefetchScalarGridSpec(
        num_scalar_prefetch=1, grid=(1,),
        in_specs=[pl.BlockSpec(own.shape, lambda i, ids_ref: (0, 0)), pl.BlockSpec(recv.shape, lambda i, ids_ref: (0, 0, 0))],
        out_specs=pl.BlockSpec(own.shape, lambda i, ids_ref: (0, 0)))
    return pl.pallas_call(body, name="chip_sum_small", grid_spec=grid_spec,
                          out_shape=jax.ShapeDtypeStruct(own.shape, F32), compiler_params=_cparams(1))(ids, own, recv)


def _adamw(name, w, m, v, mine, theirs, ids):
    rows, cols = w.shape
    half = rows // 2
    tr = _row_tile(half, ADAM_ROWS, unit=8)
    nb = half // tr
    c1 = 1.0 / (1.0 - ADAM_B1 ** ADAM_STEP)
    c2 = 1.0 / (1.0 - ADAM_B2 ** ADAM_STEP)

    def body(ids_ref, w_ref, m_ref, v_ref, mine_ref, theirs_ref, g_out, d_out, m_out, v_out):
        g = jnp.where(pl.program_id(0) == ids_ref[0], mine_ref[...], theirs_ref[...])
        m_new = ADAM_B1 * m_ref[...] + (1.0 - ADAM_B1) * g
        v_new = ADAM_B2 * v_ref[...] + (1.0 - ADAM_B2) * (g * g)
        d_out[...] = -ADAM_LR * ((m_new * c1) / (jnp.sqrt(v_new * c2) + ADAM_EPS) + ADAM_WD * w_ref[...])
        g_out[...] = g
        m_out[...] = m_new
        v_out[...] = v_new

    full = pl.BlockSpec((tr, cols), lambda h, i, ids_ref: (h * nb + i, 0))
    part = pl.BlockSpec((tr, cols), lambda h, i, ids_ref: (i, 0))
    grid_spec = pltpu.PrefetchScalarGridSpec(num_scalar_prefetch=1, grid=(2, nb),
                                             in_specs=[full, full, full, part, part], out_specs=[full] * 4)
    return pl.pallas_call(
        body, name=name, grid_spec=grid_spec,
        out_shape=[jax.ShapeDtypeStruct((rows, cols), F32)] * 4, compiler_params=_cparams(2),
    )(ids, w, m, v, mine, theirs)


def _small_shard(parts):
    return _flatten(parts, _BIG[-1][1])


_ENTRY = {e[0]: e for e in _BIG}
LATE_WEIGHTS = ("w_ssm_out", "w_attn_out", "w_mix_out", "w_ffn_up", "w_ffn_down")
FFN_GRADS = ("w_ffn_down", "w_ffn_up")
MIXER_GRADS = ("w_mix_out", "w_ssm_out", "w_attn_out")


class _StepPlan:
    def __init__(self, w, late_shards, shards, ids):
        self.w, self.g = w, {}
        self.late_shards, self.shards, self.ids = late_shards, shards, ids
        self.sums, self.halves, self.results = {}, {}, {}

    def run(self, name, fn, *args, **kw):
        at = getattr(self, "_at_" + name, None)
        if at is None:
            return fn(*args, **kw)
        exchange, landed = at()
        res, extra = fn(*args, bg=exchange, **kw)
        landed(extra)
        return res

    def _at_ssd_fwd(self):
        entries = [_ENTRY[n] for n in LATE_WEIGHTS]

        def landed(fulls):
            self.partly_gathered = fulls

        return _gather_ici(entries, self.late_shards), landed

    def _at_attn_fwd(self):
        entries = [_ENTRY[n] for n in LATE_WEIGHTS]
        return _gather_pass_on(entries, self.partly_gathered), lambda fulls: self.w.update(zip(LATE_WEIGHTS, fulls))

    def pair_sums(self, names, grads, recv):
        for n, gr, rv in zip(names, grads, recv):
            _, r, c, lay = _ENTRY[n]
            self.sums[n] = _pair_sum("pair_sum_" + n, gr, rv, self.ids, r, c, lay)

    def chip_sums(self, names, recv):
        for n, rv in zip(names, recv):
            self.halves[n] = _chip_sum("chip_sum_" + n, self.sums[n][1], rv)

    def adamw(self, names, theirs):
        for n, th in zip(names, theirs):
            sh = self.shards[n]
            self.results[n] = _adamw("adamw_" + n, sh["w"], sh["m"], sh["v"], self.halves[n], th, self.ids)

    def _pair_stage(self, names, grads):
        return (_pair_exchange([_ENTRY[n] for n in names], grads),
                lambda recv: self.pair_sums(names, grads, recv))

    def _at_ffn_up_dx(self):
        return self._pair_stage(FFN_GRADS, [self.g[n] for n in FFN_GRADS])

    def _at_ssm_gate_norm_bwd(self):
        return self._pair_stage(MIXER_GRADS, [self.g[n] for n in MIXER_GRADS])

    def _at_attn_bwd(self):
        return _chip_exchange([self.sums[n][0] for n in FFN_GRADS]), lambda recv: self.chip_sums(FFN_GRADS, recv)

    def _at_ssd_bwd(self):
        stages = (_chip_exchange([self.sums[n][0] for n in MIXER_GRADS]),
                  _whole_to_sibling([self.halves[n] for n in FFN_GRADS]))

        def landed(extra):
            recv, theirs = _split(stages, extra)
            self.chip_sums(MIXER_GRADS, recv)
            self.adamw(FFN_GRADS, theirs)

        return _join(*stages), landed

    def _at_ssm_conv_bwd(self):
        return _whole_to_sibling([self.halves[n] for n in MIXER_GRADS]), lambda theirs: self.adamw(MIXER_GRADS, theirs)

    def _at_in_proj_dx(self):
        g_in = _from_cat(self.g.pop("w_cat")).reshape(D_MODEL, 4, N_IN // 4)
        grads = [jnp.transpose(g_in, (1, 0, 2))]
        self.pair_sums(("w_in",), grads, _run_exchange("grad_pair_exchange_w_in", _pair_exchange([_ENTRY["w_in"]], grads)))
        return _chip_exchange([self.sums["w_in"][0]]), lambda recv: self.chip_sums(("w_in",), recv)

    def finish(self, g_small, g_rep, rep_shards):
        stages = (_pair_exchange([_ENTRY["small"]], [g_small]), _whole_to_sibling([g_rep]))
        recv_small, recv_rep = _split(stages, _run_exchange("grad_pair_exchange_tail", _join(*stages)))
        self.pair_sums(("small",), [g_small], recv_small)
        p_rep, = _rowwise("pair_sum_replicated", lambda r0, a, b: [a + b], SMALL_ROWS, SMALL_ROWS,
                          [(g_rep, LANES, 0), (recv_rep[0], LANES, 0)], [], [(LANES, F32)], [])
        stages = (_chip_exchange([self.sums["small"][0]]), _to_all_chips(p_rep))
        recv, recv_rep = _split(stages, _run_exchange("grad_chip_exchange_tail", _join(*stages)))
        self.chip_sums(("small",), recv)
        g_rep_tot = _chip_sum_small(p_rep, recv_rep[0], self.ids)
        last = ("w_in", "small")
        self.adamw(last, _run_exchange("grad_half_share_tail", _whole_to_sibling([self.halves[n] for n in last])))
        ids_lo = self.ids * jnp.array([0, 1], jnp.int32)
        self.results["replicated"] = _adamw("adamw_replicated", rep_shards["w"], rep_shards["m"], rep_shards["v"],
                                            g_rep_tot[0:SMALL_ROWS // 2], g_rep_tot[SMALL_ROWS // 2:], ids_lo)


def kernel(x, meta_tokens, norm_pre_mix, w_in, ssm_conv_w, ssm_conv_b, ssm_dt_bias, ssm_a_log, ssm_d_skip, ssm_norm, w_ssm_out, attn_sinks, w_attn_out, w_mix_out, norm_post_mix, norm_pre_ffn, w_ffn_up, ffn_conv_w, ffn_conv_b, w_ffn_down, norm_post_ffn, loss_target, m_meta_tokens, m_norm_pre_mix, m_w_in, m_ssm_conv_w, m_ssm_conv_b, m_ssm_dt_bias, m_ssm_a_log, m_ssm_d_skip, m_ssm_norm, m_w_ssm_out, m_attn_sinks, m_w_attn_out, m_w_mix_out, m_norm_post_mix, m_norm_pre_ffn, m_w_ffn_up, m_ffn_conv_w, m_ffn_conv_b, m_w_ffn_down, m_norm_post_ffn, v_meta_tokens, v_norm_pre_mix, v_w_in, v_ssm_conv_w, v_ssm_conv_b, v_ssm_dt_bias, v_ssm_a_log, v_ssm_d_skip, v_ssm_norm, v_w_ssm_out, v_attn_sinks, v_w_attn_out, v_w_mix_out, v_norm_post_mix, v_norm_pre_ffn, v_w_ffn_up, v_ffn_conv_w, v_ffn_conv_b, v_w_ffn_down, v_norm_post_ffn):
    args = dict(locals())
    squeeze = lambda a: a.reshape(a.shape[-2:])
    wts = {n: squeeze(args[n]) for n in WEIGHT_ORDER}
    mom = {n: squeeze(args["m_" + n]) for n in WEIGHT_ORDER}
    var = {n: squeeze(args["v_" + n]) for n in WEIGHT_ORDER}
    x_i, y_i, c_i = _mesh_pos()
    ids = jnp.stack([c_i, _chip_index(x_i, y_i)]).astype(jnp.int32)
    big_names = [n for n, _, _, _ in _BIG[:-1]]
    small_names = [n for n, _, _ in _SMALL_SHARDED]
    rep_names = [n for n, _ in _REPLICATED]

    stacks = {"w": wts, "m": mom, "v": var}
    shards = {n: {k: d[n] for k, d in stacks.items()} for n in big_names}
    shards["small"] = {k: _small_shard([d[n] for n in small_names]) for k, d in stacks.items()}
    rep_shards = {k: _flatten([d[n] for n in rep_names], SMALL_ROWS) for k, d in stacks.items()}

    w_in4, small_all = _gather_weights([_ENTRY["w_in"], _ENTRY["small"]], [wts["w_in"].astype(BF16), shards["small"]["w"]])
    w = {n: wts[n] for n in rep_names}
    w["w_cat"] = _to_cat(jnp.transpose(w_in4, (1, 0, 2)).reshape(D_MODEL, N_IN))
    small_parts = [_unflatten(small_all[i], [shp for _, shp, _ in _SMALL_SHARDED]) for i in range(4)]
    for k, (n, _, axis) in enumerate(_SMALL_SHARDED):
        w[n] = jnp.concatenate([small_parts[i][k] for i in range(4)], axis=axis)
    plan = _StepPlan(w, [wts[n].astype(BF16) for n in LATE_WEIGHTS], shards, ids)

    head = jnp.concatenate([jnp.zeros((PAD, D_MODEL), F32), w["meta_tokens"]], axis=0)
    loss_sum, dx, dhead = _local_step(x[0], head, loss_target[0], plan)
    loss = lax.psum(loss_sum * (0.5 / D_MODEL), ("x", "y", "c"))
    g = plan.g
    g["meta_tokens"] = dhead[PAD:]
    g_small = jnp.stack([_small_shard([_shard_of(g[n], i, shp, ax) for n, shp, ax in _SMALL_SHARDED]) for i in range(4)])
    plan.finish(g_small, _flatten([g[n] for n in rep_names], SMALL_ROWS), rep_shards)

    results = {}
    for kind in range(4):
        results.update({(kind, n): plan.results[n][kind] for n in big_names})
        parts = _unflatten(plan.results["small"][kind], [shp for _, shp, _ in _SMALL_SHARDED])
        results.update({(kind, n): parts[k] for k, n in enumerate(small_names)})
        parts = _unflatten(plan.results["replicated"][kind], [(1, width) for _, width in _REPLICATED])
        results.update({(kind, n): parts[k] for k, n in enumerate(rep_names)})
    outs = [results[kind, n].reshape(args[n].shape) for kind in range(4) for n in WEIGHT_ORDER]
    return (loss, dx[None], *outs)
```

```python
import math
from typing import Any, Callable, NamedTuple, Sequence

import jax
import jax.numpy as jnp
from jax import lax
from jax.experimental import pallas as pl
from jax.experimental.pallas import tpu as pltpu

F32 = jnp.float32
BF16 = jnp.bfloat16

D_MODEL = 1024
N_META = 16
T = 128
PAD = T - N_META
D_INNER = 2048
SSM_HEADS = 32
HEAD_P = 64
SSM_GROUPS = 4
GROUP_W = D_INNER // SSM_GROUPS
D_STATE = 128
CONV_DIM = D_INNER + 2 * SSM_GROUPS * D_STATE
ATTN_HEADS = 16
KV_HEADS = 4
ATTN_W = 1024
KV_W = 256
FFN_DIM = 2816
N_IN = 8736
EPS = 1e-6
NEG = -1e30
SCALE = 0.125

P_Q, P_K, P_V, P_DT, P_Z, P_GATE, P_XBC = 0, 1024, 1280, 1536, 2048, 4096, 6144
QKV_W = 1536
P_W = 9216

ADAM_LR, ADAM_B1, ADAM_B2, ADAM_EPS, ADAM_WD, ADAM_STEP = 0.001, 0.9, 0.999, 1e-08, 0.01, 10

VMEM_BUDGET = 40 * 1024 * 1024
VMEM_LIMIT = 56 * 1024 * 1024
MESH = pl.DeviceIdType.MESH
ANY = pl.BlockSpec(memory_space=pl.ANY)


def _cparams(n_axes, **kw):
    return pltpu.CompilerParams(dimension_semantics=("arbitrary",) * n_axes, vmem_limit_bytes=VMEM_LIMIT, **kw)


class _Exchange(NamedTuple):
    ins: Sequence[Any]
    out_shapes: Sequence[Any]
    make_copies: Callable
    n_copies: int
    aliases: dict = {}


def _call(body, name, grid, in_specs, out_specs, out_shape, operands, scratch_shapes=(), aliases=None, bg=None):
    aliases = dict(aliases or {})
    if bg is None:
        return pl.pallas_call(body, name=name, grid=grid, in_specs=in_specs, out_specs=out_specs, out_shape=out_shape,
                              scratch_shapes=list(scratch_shapes), input_output_aliases=aliases,
                              compiler_params=_cparams(len(grid)))(*operands)
    n_in, n_out, n_scr = len(in_specs), len(out_specs), len(scratch_shapes)
    nb_in, nb_out = len(bg.ins), len(bg.out_shapes)

    def hosted(*refs):
        ins, bg_ins = refs[:n_in], refs[n_in:n_in + nb_in]
        outs = refs[n_in + nb_in:n_in + nb_in + n_out]
        bg_outs = refs[n_in + nb_in + n_out:n_in + nb_in + n_out + nb_out]
        scratch = refs[n_in + nb_in + n_out + nb_out:n_in + nb_in + n_out + nb_out + n_scr]
        send_sems, recv_sems = refs[-2:]
        pids = [pl.program_id(a) for a in range(len(grid))]
        first, last = pids[0] == 0, pids[0] == grid[0] - 1
        for p, g in zip(pids[1:], grid[1:]):
            first, last = first & (p == 0), last & (p == g - 1)
        copies = []
        for k, (src, dst, peer) in enumerate(bg.make_copies(bg_ins, bg_outs)):
            if peer is None:
                copies.append(pltpu.make_async_copy(src, dst, send_sems.at[k]))
            else:
                copies.append(pltpu.make_async_remote_copy(src_ref=src, dst_ref=dst, send_sem=send_sems.at[k],
                                                           recv_sem=recv_sems.at[k], device_id=peer, device_id_type=MESH))
        assert len(copies) == bg.n_copies

        @pl.when(first)
        def _():
            for cp in copies:
                cp.start()

        body(*ins, *outs, *scratch)

        @pl.when(last)
        def _():
            for cp in copies:
                cp.wait()

    aliases = {(k if k < n_in else k + nb_in): v for k, v in aliases.items()}
    aliases.update({n_in + k: n_out + v for k, v in bg.aliases.items()})
    res = pl.pallas_call(
        hosted, name=name, grid=grid, in_specs=list(in_specs) + [ANY] * nb_in, out_specs=list(out_specs) + [ANY] * nb_out,
        out_shape=list(out_shape) + list(bg.out_shapes), input_output_aliases=aliases,
        scratch_shapes=list(scratch_shapes) + [pltpu.SemaphoreType.DMA((bg.n_copies,))] * 2,
        compiler_params=_cparams(len(grid), has_side_effects=True))(*operands, *bg.ins)
    return res[:n_out], res[n_out:]


def _sigmoid(x):
    return pl.reciprocal(1.0 + jnp.exp(-x), approx=True)


def _silu(x):
    return x * _sigmoid(x)


def _silu_grad(x):
    s = _sigmoid(x)
    return x * s, s * (1.0 + x * (1.0 - s))


def _dsilu(x):
    return _silu_grad(x)[1]


def _softplus(x):
    e = jnp.exp(-jnp.abs(x))
    small = e * (1.0 - e * (0.5 - e * (1.0 / 3.0)))
    return jnp.maximum(x, 0.0) + jnp.where(e < 0.01, small, jnp.log(1.0 + e))


def _rms(x, w):
    r = lax.rsqrt(jnp.mean(x * x, axis=-1, keepdims=True) + EPS)
    return x * r * w


def _rms_bwd(dy, x, w):
    r = lax.rsqrt(jnp.mean(x * x, axis=-1, keepdims=True) + EPS)
    xh = x * r
    g = dy * w
    dx = r * (g - xh * jnp.mean(g * xh, axis=-1, keepdims=True))
    dw = jnp.sum(dy * xh, axis=0, keepdims=True)
    return dx, dw


def _dot(a, b):
    return jnp.dot(a, b, preferred_element_type=F32)


def _dot_nt(a, b):
    return lax.dot_general(a, b, (((1,), (1,)), ((), ())), preferred_element_type=F32)


def _dot_tn(a, b):
    return lax.dot_general(a, b, (((0,), (0,)), ((), ())), preferred_element_type=F32)


def _split3(x):
    hi = x.astype(BF16)
    r = x - hi.astype(F32)
    mid = r.astype(BF16)
    lo = (r - mid.astype(F32)).astype(BF16)
    return hi, mid, lo


def _xdot(x, e):
    hi, mid, lo = _split3(x)
    return _dot(hi, e) + _dot(mid, e) + _dot(lo, e)


def _xdot_l(e, x):
    hi, mid, lo = _split3(x)
    return _dot(e, hi) + _dot(e, mid) + _dot(e, lo)


def _iota(shape, dim):
    return lax.broadcasted_iota(jnp.int32, shape, dim)


def _divisors(n, unit):
    return [t for t in range(unit, n + 1, unit) if n % t == 0]


MIN_MATMUL_STEPS = 8


def _matmul_tiles(m, n, k, a_bytes, b_bytes, o_bytes, m_unit):
    best = None
    for tm in _divisors(m, m_unit):
        for tn in _divisors(n, 128):
            for tk in _divisors(k, 128):
                acc = 0 if tk == k else tm * tn * 4
                vm = 2 * (tm * tk * a_bytes + tk * tn * b_bytes + tm * tn * o_bytes) + acc
                if vm > VMEM_BUDGET:
                    continue
                steps = (m // tm) * (n // tn) * (k // tk)
                score = (tk == k, min(steps, MIN_MATMUL_STEPS), min(tm, 256), tm * tn * tk)
                if best is None or score > best[0]:
                    best = (score, (tm, tn, tk))
    return best[1]


def _matmul(name, a, b, mode, out_dtype, bg=None):
    if mode == "nn":
        (m, k), n = a.shape, b.shape[1]
    elif mode == "nt":
        (m, k), n = a.shape, b.shape[0]
    else:
        (k, m), n = a.shape, b.shape[1]
    ab, bb, ob = a.dtype.itemsize, b.dtype.itemsize, jnp.dtype(out_dtype).itemsize
    tm, tn, tk = _matmul_tiles(m, n, k, ab, bb, ob, 128 if mode == "tn" else 16)
    nk = k // tk
    dot = {"nn": _dot, "nt": _dot_nt, "tn": _dot_tn}[mode]

    def body(a_ref, b_ref, o_ref, *scratch):
        prod = dot(a_ref[...].astype(BF16), b_ref[...].astype(BF16))
        if nk == 1:
            o_ref[...] = prod.astype(o_ref.dtype)
        else:
            acc_ref, = scratch
            kk = pl.program_id(2)

            @pl.when(kk == 0)
            def _():
                acc_ref[...] = prod

            @pl.when(kk > 0)
            def _():
                acc_ref[...] += prod

            @pl.when(kk == nk - 1)
            def _():
                o_ref[...] = acc_ref[...].astype(o_ref.dtype)

    a_spec = pl.BlockSpec((tk, tm), lambda i, j, kk: (kk, i)) if mode == "tn" else pl.BlockSpec((tm, tk), lambda i, j, kk: (i, kk))
    b_spec = pl.BlockSpec((tn, tk), lambda i, j, kk: (j, kk)) if mode == "nt" else pl.BlockSpec((tk, tn), lambda i, j, kk: (kk, j))
    res = _call(body, name, (m // tm, n // tn, nk), [a_spec, b_spec], [pl.BlockSpec((tm, tn), lambda i, j, kk: (i, j))],
                [jax.ShapeDtypeStruct((m, n), out_dtype)], [a, b],
                scratch_shapes=[] if nk == 1 else [pltpu.VMEM((tm, tn), F32)], bg=bg)
    return res[0] if bg is None else (res[0][0], res[1])


def _row_tile(n_rows, cap, unit=16):
    return max(t for t in _divisors(n_rows, unit) if t <= cap)


ROW_SUB = 384
GROUP_UNROLL = 4


def _rowwise(name, fn, n_rows, tm, row_ins, full_ins, row_outs, acc_outs, bg=None):
    n_in = len(row_ins) + len(full_ins)
    n_ro = len(row_outs)
    into = [(k, o[3]) for k, o in enumerate(row_outs) if len(o) > 2 and o[2] == "into"]

    n_row_in = len(row_ins)
    sub = min(tm, ROW_SUB)

    def body(*refs):
        i = pl.program_id(0)
        outs = refs[n_in + len(into):]

        def group(s, sums):
            rows = pl.ds(pl.multiple_of(s * sub, sub), sub)
            vals = [r[rows, :] for r in refs[:n_row_in]] + [r[...] for r in refs[n_row_in:n_in]]
            res = fn(i * tm + s * sub, *vals)
            for o, r, v in zip(row_outs, outs[:n_ro], res[:n_ro]):
                if len(o) > 2 and o[2] == "first":
                    @pl.when(i == 0)
                    def _(r=r, v=v):
                        r[rows, :] = v.astype(r.dtype)
                else:
                    r[rows, :] = v.astype(r.dtype)
            return tuple(a + v for a, v in zip(sums, res[n_ro:]))

        sums = lax.fori_loop(0, tm // sub, group, tuple(jnp.zeros((1, w), F32) for w in acc_outs), unroll=GROUP_UNROLL)

        @pl.when(i == 0)
        def _():
            for r, v in zip(outs[n_ro:], sums):
                r[...] = v

        @pl.when(i > 0)
        def _():
            for r, v in zip(outs[n_ro:], sums):
                r[...] += v

    def in_spec(entry):
        w, cb = entry[1], entry[2]
        if len(entry) > 3 and entry[3] == "prev":
            return pl.BlockSpec((tm, w), lambda i: (jnp.maximum(i - 1, 0), cb))
        if len(entry) > 3 and entry[3] == "first":
            return pl.BlockSpec((tm, w), lambda i: (0, cb))
        return pl.BlockSpec((tm, w), lambda i: (i, cb))

    def out_spec(o):
        if len(o) == 2:
            return pl.BlockSpec((tm, o[0]), lambda i: (i, 0)), jax.ShapeDtypeStruct((n_rows, o[0]), o[1])
        if o[2] == "new":
            return pl.BlockSpec((tm, o[0]), lambda i: (i, o[4])), jax.ShapeDtypeStruct((n_rows, o[3]), o[1])
        if o[2] == "into":
            return pl.BlockSpec((tm, o[0]), lambda i: (i, o[4])), jax.ShapeDtypeStruct(o[3].shape, o[3].dtype)
        if o[2] == "first":
            return pl.BlockSpec((tm, o[0]), lambda i: (0, 0)), jax.ShapeDtypeStruct((tm, o[0]), o[1])
        return pl.BlockSpec((tm, o[0]), lambda i: (jnp.maximum(i - 1, 0), 0)), jax.ShapeDtypeStruct((o[3], o[0]), o[1])

    in_specs = [in_spec(e) for e in row_ins]
    in_specs += [pl.BlockSpec(a.shape, lambda i: (0, 0)) for a in full_ins]
    in_specs += [pl.BlockSpec(memory_space=pl.ANY) for _ in into]
    specs_shapes = [out_spec(o) for o in row_outs]
    out_specs = [s for s, _ in specs_shapes] + [pl.BlockSpec((1, w), lambda i: (0, 0)) for w in acc_outs]
    out_shape = [s for _, s in specs_shapes] + [jax.ShapeDtypeStruct((1, w), F32) for w in acc_outs]
    return _call(body, name, (n_rows // tm,), in_specs, out_specs, out_shape,
                 [e[0] for e in row_ins] + list(full_ins) + [arr for _, arr in into],
                 aliases={n_in + a: k for a, (k, _) in enumerate(into)}, bg=bg)


def _valid_rows(first_row, tm, lo):
    return (first_row + _iota((tm, 1), 0)) >= lo


CONV_ROWS = 128
CONV_SUB = 16
CONV_LANES = 256


def _conv_specs(tm, width, blk, n_rows, after):
    specs = [pl.BlockSpec((tm, width), lambda i: (i, blk)),
             pl.BlockSpec((8, width), lambda i: (jnp.maximum(i * (tm // 8) - 1, 0), blk))]
    if after:
        specs.append(pl.BlockSpec((16, width), lambda i: (jnp.minimum((i + 1) * (tm // 16), n_rows // 16 - 1), blk)))
    return specs


def _conv_window(win, w_ref, b_ref, taps, c0, cw, n):
    acc = b_ref[:, c0:c0 + cw] + w_ref[taps - 1:taps, c0:c0 + cw] * win[8:8 + n]
    for k in range(taps - 1):
        acc = acc + w_ref[k:k + 1, c0:c0 + cw] * win[8 - (taps - 1) + k:8 - (taps - 1) + k + n]
    return acc


def _ffn_act(name, u_raw, conv_w, conv_b, n_rows):
    tm, sub, cw = CONV_ROWS, CONV_SUB, CONV_LANES
    taps, width = conv_w.shape
    half = width // 2

    def body(cur_ref, prev_ref, w_ref, b_ref, f_ref, ext_ref):
        i = pl.program_id(0)
        ext_ref[0:8, :] = jnp.where(i > 0, prev_ref[...], 0.0)
        ext_ref[8:8 + tm, :] = cur_ref[...]
        for q in range(half // cw):
            a0, g0 = q * cw, half + q * cw

            def group(s, carry):
                r = pl.multiple_of(s * sub, sub)
                a = _conv_window(ext_ref[pl.ds(r, sub + 8), a0:a0 + cw], w_ref, b_ref, taps, a0, cw, sub)
                g = _conv_window(ext_ref[pl.ds(r, sub + 8), g0:g0 + cw], w_ref, b_ref, taps, g0, cw, sub)
                f = jnp.where(_valid_rows(i * tm + r, sub, PAD), _silu(a) * g, 0.0)
                f_ref[pl.ds(r, sub), a0:a0 + cw] = f.astype(f_ref.dtype)
                return carry

            lax.fori_loop(0, tm // sub, group, 0, unroll=GROUP_UNROLL)

    return pl.pallas_call(
        body, name=name, grid=(n_rows // tm,),
        in_specs=_conv_specs(tm, width, 0, n_rows, False) + [pl.BlockSpec((taps, width), lambda i: (0, 0)),
                                                             pl.BlockSpec((1, width), lambda i: (0, 0))],
        out_specs=pl.BlockSpec((tm, half), lambda i: (i, 0)),
        out_shape=jax.ShapeDtypeStruct((n_rows, half), BF16),
        scratch_shapes=[pltpu.VMEM((tm + 8, width), F32)],
        compiler_params=_cparams(1),
    )(u_raw, u_raw, conv_w, conv_b)


def _conv_bwd(name, raw, raw_blk, dsrcs, chunk_src, conv_w, conv_b, n_rows, gated, into=None, into_blk=0, bg=None):
    taps, width = conv_w.shape
    half = width // 2 if gated else width
    tm, sub, cw = CONV_ROWS, CONV_SUB, CONV_LANES
    te = tm + 16
    nd = len(dsrcs)
    n_parts = 2 if gated else 1

    def body(*refs):
        cur_ref, prev_ref, next_ref = refs[0:3]
        dcur, dnext = refs[3:3 + nd], refs[3 + nd:3 + 2 * nd]
        w_ref, b_ref = refs[3 + 2 * nd:5 + 2 * nd]
        out_ref, acc_ref, ext_ref, du_ref = refs[-4:]
        i = pl.program_id(0)
        ext_ref[0:8, :] = jnp.where(i > 0, prev_ref[...], 0.0)
        ext_ref[8:8 + tm, :] = cur_ref[...]
        ext_ref[8 + tm:24 + tm, :] = next_ref[...]

        for q, (src, off) in enumerate(chunk_src):
            cols = [q * cw, half + q * cw][:n_parts]

            def conv_grad(r, d):
                pre = [_conv_window(ext_ref[pl.ds(r, sub + 8), c0:c0 + cw], w_ref, b_ref, taps, c0, cw, sub) for c0 in cols]
                row = i * tm + r + _iota((sub, 1), 0)
                live = (row >= PAD) & (row < n_rows)
                if gated:
                    act, dact = _silu_grad(pre[0])
                    dus = [d * pre[1] * dact, d * act]
                else:
                    dus = [d * _dsilu(pre[0])]
                for part, du in enumerate(dus):
                    du_ref[part, pl.ds(r, sub), :] = jnp.where(live, du, 0.0)

            def tile_rows(s, carry):
                r = pl.multiple_of(s * sub, sub)
                conv_grad(r, dcur[src][pl.ds(r, sub), off:off + cw].astype(F32))
                return carry

            lax.fori_loop(0, tm // sub, tile_rows, 0, unroll=GROUP_UNROLL)
            conv_grad(tm, dnext[src][:, off:off + cw].astype(F32))

            for part, c0 in enumerate(cols):
                taps_w = [w_ref[k:k + 1, c0:c0 + cw] for k in range(taps)]

                def back(s, sums):
                    new = list(sums)
                    for u in range(2):
                        r = pl.multiple_of((2 * s + u) * sub, sub)
                        win = du_ref[part, pl.ds(r, sub + 8), :]
                        raw_rows = ext_ref[pl.ds(8 + r, sub), c0:c0 + cw]
                        draw = jnp.zeros((sub, cw), F32)
                        for k in range(taps):
                            shifted = win[taps - 1 - k:taps - 1 - k + sub]
                            draw = draw + taps_w[k] * shifted
                            new[k] = new[k] + shifted * raw_rows
                        new[taps] = new[taps] + win[0:sub]
                        out_ref[pl.ds(r, sub), c0:c0 + cw] = jnp.where(_valid_rows(i * tm + r, sub, PAD), draw, 0.0).astype(out_ref.dtype)
                    return tuple(new)

                sums = lax.fori_loop(0, tm // (2 * sub), back, tuple(jnp.zeros((sub, cw), F32) for _ in range(taps + 1)))
                for k in range(taps + 1):
                    total = jnp.sum(sums[k], axis=0, keepdims=True)
                    acc_ref[k:k + 1, c0:c0 + cw] = jnp.where(i == 0, total, acc_ref[k:k + 1, c0:c0 + cw] + total)

    in_specs = _conv_specs(tm, width, raw_blk, n_rows, True)
    in_specs += [pl.BlockSpec((tm, d.shape[1]), lambda i: (i, 0)) for d in dsrcs]
    in_specs += [pl.BlockSpec((16, d.shape[1]), lambda i: (jnp.minimum((i + 1) * (tm // 16), n_rows // 16 - 1), 0)) for d in dsrcs]
    in_specs += [pl.BlockSpec((taps, width), lambda i: (0, 0)), pl.BlockSpec((1, width), lambda i: (0, 0))]
    operands = [raw, raw, raw] + list(dsrcs) + list(dsrcs) + [conv_w, conv_b]
    aliases = {}
    if into is None:
        out0 = jax.ShapeDtypeStruct((n_rows, width), BF16)
    else:
        in_specs.append(pl.BlockSpec(memory_space=pl.ANY))
        operands.append(into)
        aliases = {len(operands) - 1: 0}
        out0 = jax.ShapeDtypeStruct(into.shape, into.dtype)
    return _call(body, name, (n_rows // tm,), in_specs,
                 [pl.BlockSpec((tm, width), lambda i: (i, into_blk)), pl.BlockSpec((8, width), lambda i: (0, 0))],
                 [out0, jax.ShapeDtypeStruct((8, width), F32)], operands,
                 scratch_shapes=[pltpu.VMEM((tm + 24, width), F32), pltpu.VMEM((n_parts, te + 8, cw), F32)],
                 aliases=aliases, bg=bg)


def _ssd_specs(n_chunks, rev):
    cidx = (lambda c: n_chunks - 1 - c) if rev else (lambda c: c)
    xg0, bg0, cg0 = P_XBC // GROUP_W, (P_XBC + D_INNER) // D_STATE, (P_XBC + D_INNER + SSM_GROUPS * D_STATE) // D_STATE

    def cur(width, blk0):
        return pl.BlockSpec((T, width), lambda g, c: (cidx(c), blk0 + g))

    def prev(width, blk0):
        return pl.BlockSpec((8, width), lambda g, c: (jnp.maximum(cidx(c) * (T // 8) - 1, 0), blk0 + g))

    specs = [cur(GROUP_W, xg0), prev(GROUP_W, xg0), cur(D_STATE, bg0), prev(D_STATE, bg0),
             cur(D_STATE, cg0), prev(D_STATE, cg0),
             pl.BlockSpec((T, 128), lambda g, c: (cidx(c), P_DT // 128))]
    wx, wb, wc = 0, D_INNER // D_STATE, (D_INNER + SSM_GROUPS * D_STATE) // D_STATE
    specs += [pl.BlockSpec((4, GROUP_W), lambda g, c: (0, g)),
              pl.BlockSpec((4, D_STATE), lambda g, c: (0, wb + g)),
              pl.BlockSpec((4, D_STATE), lambda g, c: (0, wc + g)),
              pl.BlockSpec((1, GROUP_W), lambda g, c: (0, g)),
              pl.BlockSpec((1, D_STATE), lambda g, c: (0, wb + g)),
              pl.BlockSpec((1, D_STATE), lambda g, c: (0, wc + g))]
    specs += [pl.BlockSpec((1, 128), lambda g, c: (0, 0))] * 3
    return specs, cidx


def _ssd_chunk_forward(refs, ext_ref, g, c):
    (xc_ref, xp_ref, bc_ref, bp_ref, cc_ref, cp_ref, dt_ref, wx_ref, wb_ref, wc_ref,
     bx_ref, bb_ref, bcb_ref, dtb_ref, alog_ref, dsk_ref) = refs

    def conv_pre(cur_ref, prev_ref, w_ref, b_ref, width):
        ext_ref[0:8, 0:width] = jnp.where(c > 0, prev_ref[...], 0.0)
        ext_ref[8:8 + T, 0:width] = cur_ref[...]
        w = w_ref[...]
        acc = b_ref[...] + w[3:4] * cur_ref[...]
        for k in range(3):
            acc = acc + w[k:k + 1] * ext_ref[pl.ds(5 + k, T), 0:width]
        return acc

    valid = _valid_rows(c * T, T, PAD)
    v = {}
    v["valid"] = valid
    v["x_pre"] = conv_pre(xc_ref, xp_ref, wx_ref, bx_ref, GROUP_W)
    v["b_pre"] = conv_pre(bc_ref, bp_ref, wb_ref, bb_ref, D_STATE)
    v["c_pre"] = conv_pre(cc_ref, cp_ref, wc_ref, bcb_ref, D_STATE)
    xs = _silu(v["x_pre"])
    bm = jnp.where(valid, _silu(v["b_pre"]), 0.0)
    cm = jnp.where(valid, _silu(v["c_pre"]), 0.0)
    dtr = dt_ref[...] + dtb_ref[...]
    dt = jnp.where(valid, _softplus(dtr), 0.0)
    a_neg = -jnp.exp(alog_ref[...])
    a = dt * a_neg
    tril = _iota((T, T), 0) >= _iota((T, T), 1)
    cs = _xdot_l(tril.astype(BF16), a)
    hh, ll = _iota((128, GROUP_W), 0), _iota((128, GROUP_W), 1)
    expand = (hh == 8 * g + jnp.right_shift(ll, 6)).astype(BF16)
    sh, sj = _iota((128, 128), 0), _iota((128, 128), 1)
    select = ((sh == 8 * g + sj) & (sj < 8)).astype(BF16)
    hh_t, ll_t = _iota((GROUP_W, 128), 1), _iota((GROUP_W, 128), 0)
    v["expand_t"] = (hh_t == 8 * g + jnp.right_shift(ll_t, 6)).astype(BF16)
    v["select_t"] = ((sj == 8 * g + sh) & (sh < 8)).astype(BF16)
    cs_e = _xdot(cs, expand)
    dt_e = _xdot(dt, expand)
    cs_loc = _xdot(cs, select)
    cs_loc_t = cs_loc.T
    cs_last_e = cs_e[T - 1:T, :]
    v.update(xs=xs, bm=bm, cm=cm, dtr=dtr, dt=dt, a_neg=a_neg, tril=tril, expand=expand, select=select,
             cs_e=cs_e, dt_e=dt_e, cs_loc=cs_loc, cs_loc_t=cs_loc_t, cs_last_e=cs_last_e)
    v["xdt"] = xs * dt_e
    v["decay_e"] = jnp.exp(cs_last_e - cs_e)
    v["ecs_e"] = jnp.exp(cs_e)
    v["elast_e"] = jnp.exp(cs_last_e)
    v["d_e"] = _xdot(dsk_ref[...], expand)
    v["gmat"] = _dot_nt(cm.astype(BF16), bm.astype(BF16))
    return v


def _ssd_decay_pair(v, jp):
    out = []
    for j in (2 * jp, 2 * jp + 1):
        diff = v["cs_loc"][:, j:j + 1] - v["cs_loc_t"][j:j + 1, :]
        out.append(jnp.where(v["tril"], jnp.exp(jnp.where(v["tril"], diff, 0.0)), 0.0))
    return out


def _block_diag_pair(xp):
    lane = _iota(xp.shape, 1)
    return jnp.concatenate([jnp.where(lane < HEAD_P, xp, 0.0), jnp.where(lane >= HEAD_P, xp, 0.0)], axis=0)


def _ssd_fwd(p, conv_w, conv_b, dt_bias, a_log, d_skip, n_chunks, bg=None):
    n_rows = n_chunks * T
    in_specs, _ = _ssd_specs(n_chunks, rev=False)

    def body(*refs):
        y_ref, hin_ref, st_ref, ext_ref = refs[16:]
        g, c = pl.program_id(0), pl.program_id(1)

        @pl.when(c == 0)
        def _():
            st_ref[...] = jnp.zeros_like(st_ref)

        v = _ssd_chunk_forward(refs[:16], ext_ref, g, c)
        state = st_ref[...]
        hin_ref[...] = state
        ys = []
        for jp in range(4):
            l0, l1 = _ssd_decay_pair(v, jp)
            lhs = jnp.concatenate([v["gmat"] * l0, v["gmat"] * l1], axis=1).astype(BF16)
            rhs = _block_diag_pair(v["xdt"][:, 128 * jp:128 * jp + 128]).astype(BF16)
            ys.append(_dot(lhs, rhs))
        y = jnp.concatenate(ys, axis=1)
        y = y + _dot(v["cm"].astype(BF16), state.astype(BF16)) * v["ecs_e"] + v["xs"] * v["d_e"]
        y_ref[...] = y
        s_new = _dot_tn(v["bm"].astype(BF16), (v["xdt"] * v["decay_e"]).astype(BF16))
        st_ref[...] = state * v["elast_e"] + s_new

    return _call(
        body, "ssd_fwd", (SSM_GROUPS, n_chunks), in_specs,
        [pl.BlockSpec((T, GROUP_W), lambda g, c: (c, g)),
         pl.BlockSpec((None, None, D_STATE, GROUP_W), lambda g, c: (g, c, 0, 0))],
        [jax.ShapeDtypeStruct((n_rows, D_INNER), F32),
         jax.ShapeDtypeStruct((SSM_GROUPS, n_chunks, D_STATE, GROUP_W), F32)],
        [p, p, p, p, p, p, p, conv_w, conv_w, conv_w, conv_b, conv_b, conv_b, dt_bias, a_log, d_skip],
        scratch_shapes=[pltpu.VMEM((D_STATE, GROUP_W), F32), pltpu.VMEM((T + 8, GROUP_W), F32)], bg=bg)


def _ssd_bwd(p, conv_w, conv_b, dt_bias, a_log, d_skip, hin, dy, n_chunks, bg=None):
    n_rows = n_chunks * T
    in_specs, cidx = _ssd_specs(n_chunks, rev=True)
    in_specs = in_specs + [pl.BlockSpec((None, None, D_STATE, GROUP_W), lambda g, c: (g, cidx(c), 0, 0)),
                           pl.BlockSpec((T, GROUP_W), lambda g, c: (cidx(c), g))]

    def body(*refs):
        hin_ref, dy_ref = refs[16:18]
        dx_ref, db_ref, dc_ref, ddt_ref, dpar_ref, dst_ref, ext_ref = refs[18:]
        g, step = pl.program_id(0), pl.program_id(1)
        c = n_chunks - 1 - step

        @pl.when(step == 0)
        def _():
            dst_ref[...] = jnp.zeros_like(dst_ref)

        v = _ssd_chunk_forward(refs[:16], ext_ref, g, c)
        hin_f = hin_ref[...]
        hin_b = hin_f.astype(BF16)
        dyv = dy_ref[...]
        dst = dst_ref[...]
        dst_b = dst.astype(BF16)
        xs, bm, cm, xdt = v["xs"], v["bm"], v["cm"], v["xdt"]
        bm_b, cm_b = bm.astype(BF16), cm.astype(BF16)

        dd_e = jnp.sum(dyv * xs, axis=0, keepdims=True)
        dxs = dyv * v["d_e"]
        ch = _dot(cm_b, hin_b)
        dch = (dyv * v["ecs_e"]).astype(BF16)
        dcm = _dot_nt(dch, hin_b)
        dhin = _dot_tn(cm_b, dch) + dst * v["elast_e"]
        dcs_e = dyv * ch * v["ecs_e"]
        dxd = _dot(bm_b, dst_b)
        dbm = _dot_nt((xdt * v["decay_e"]).astype(BF16), dst_b)
        dxdt_state = dxd * v["decay_e"]
        q = dxdt_state * xdt
        dcs_e = dcs_e - q
        dlast_e = jnp.sum(q, axis=0, keepdims=True) + jnp.sum(dst * hin_f, axis=0, keepdims=True) * v["elast_e"]
        dg = jnp.zeros((T, T), F32)
        rs_cols = jnp.zeros((T, 128), F32)
        cs_rows = jnp.zeros((128, T), F32)
        lane_i, sub_i = _iota((T, 128), 1), _iota((128, T), 0)
        dxdt_parts = []
        for jp in range(4):
            l0, l1 = _ssd_decay_pair(v, jp)
            m0, m1 = v["gmat"] * l0, v["gmat"] * l1
            xbd = _block_diag_pair(xdt[:, 128 * jp:128 * jp + 128]).astype(BF16)
            dyp = dyv[:, 128 * jp:128 * jp + 128]
            dm = _dot_nt(dyp.astype(BF16), xbd)
            dm0, dm1 = dm[:, 0:T], dm[:, T:2 * T]
            dg = dg + dm0 * l0 + dm1 * l1
            for j, qq in ((2 * jp, dm0 * m0), (2 * jp + 1, dm1 * m1)):
                rs_cols = jnp.where(lane_i == j, jnp.sum(qq, axis=1, keepdims=True), rs_cols)
                cs_rows = jnp.where(sub_i == j, jnp.sum(qq, axis=0, keepdims=True), cs_rows)
            mv = jnp.concatenate([m0, m1], axis=0).astype(BF16)
            dxdt_parts.append(_dot_tn(mv, _block_diag_pair(dyp).astype(BF16)))
        dxdt = jnp.concatenate(dxdt_parts, axis=1) + dxdt_state
        dg_b = dg.astype(BF16)
        dcm = dcm + _dot(dg_b, bm_b)
        dbm = dbm + _dot_tn(dg_b, cm_b)
        expand_t = v["expand_t"]
        dcs_loc = rs_cols - cs_rows.T
        last_row = _iota((T, 1), 0) == T - 1
        dcs_full_e = dcs_e + jnp.where(last_row, dlast_e, 0.0)
        dcs = _xdot(dcs_full_e, expand_t) + _xdot(dcs_loc, v["select_t"])
        triu = (_iota((T, T), 0) <= _iota((T, T), 1)).astype(BF16)
        da = _xdot_l(triu, dcs)
        ddt = da * v["a_neg"] + _xdot(dxdt * xs, expand_t)
        dxs = dxs + dxdt * v["dt_e"]
        ddtr = jnp.where(v["valid"], ddt * _sigmoid(v["dtr"]), 0.0)
        dx_ref[...] = dxs
        db_ref[...] = jnp.where(v["valid"], dbm, 0.0)
        dc_ref[...] = jnp.where(v["valid"], dcm, 0.0)
        ddt_ref[...] = ddtr
        dpar = jnp.concatenate([
            jnp.sum(ddtr, axis=0, keepdims=True),
            jnp.sum(da * v["dt"], axis=0, keepdims=True) * v["a_neg"],
            _xdot(dd_e, expand_t),
            jnp.zeros((5, 128), F32)], axis=0)

        @pl.when(step == 0)
        def _():
            dpar_ref[...] = dpar

        @pl.when(step > 0)
        def _():
            dpar_ref[...] += dpar

        dst_ref[...] = dhin

    return _call(
        body, "ssd_bwd", (SSM_GROUPS, n_chunks), in_specs,
        [pl.BlockSpec((T, GROUP_W), lambda g, c: (cidx(c), g)),
         pl.BlockSpec((T, D_STATE), lambda g, c: (cidx(c), g)),
         pl.BlockSpec((T, D_STATE), lambda g, c: (cidx(c), g)),
         pl.BlockSpec((T, 128), lambda g, c: (cidx(c), g)),
         pl.BlockSpec((None, 8, 128), lambda g, c: (g, 0, 0))],
        [jax.ShapeDtypeStruct((n_rows, D_INNER), F32),
         jax.ShapeDtypeStruct((n_rows, SSM_GROUPS * D_STATE), F32),
         jax.ShapeDtypeStruct((n_rows, SSM_GROUPS * D_STATE), F32),
         jax.ShapeDtypeStruct((n_rows, SSM_GROUPS * 128), F32),
         jax.ShapeDtypeStruct((SSM_GROUPS, 8, 128), F32)],
        [p, p, p, p, p, p, p, conv_w, conv_w, conv_w, conv_b, conv_b, conv_b, dt_bias, a_log, d_skip, hin, dy],
        scratch_shapes=[pltpu.VMEM((D_STATE, GROUP_W), F32), pltpu.VMEM((T + 8, GROUP_W), F32)], bg=bg)


def _alibi_slope(h):
    return 2.0 ** (-8.0 * (h + 1) / ATTN_HEADS)


def _dup_half(x256, kvh):
    xb = x256[:, 128 * (kvh // 2):128 * (kvh // 2) + 128]
    rolled = pltpu.roll(xb, 64, 1)
    lane = _iota(xb.shape, 1)
    if kvh % 2 == 0:
        return jnp.where(lane < 64, xb, rolled)
    return jnp.where(lane < 64, rolled, xb)


def _attn_masks(c):
    qi, j = _iota((T, T), 0), _iota((T, T), 1)
    tri = j <= qi
    meta_ok = (j >= PAD) & (j - PAD <= c * T + qi - PAD)
    band_ok = c >= jnp.where(tri, 1, 2)
    dist = jnp.bitwise_and(qi - j, T - 1).astype(F32)
    return tri, meta_ok, band_ok, dist


def _fold(x3, tri):
    return jnp.concatenate([x3[:, 0:T], jnp.where(tri, x3[:, 2 * T:3 * T], x3[:, T:2 * T])], axis=1)


def _unfold(x2, tri):
    band = x2[:, T:2 * T]
    return jnp.concatenate([x2[:, 0:T], jnp.where(tri, 0.0, band), jnp.where(tri, band, 0.0)], axis=1)


def _attn_scores(qp, k3, masks, h0):
    tri, meta_ok, band_ok, dist = masks
    lane = _iota(qp.shape, 1)
    s = []
    for half, h in ((0, h0), (1, h0 + 1)):
        qh = jnp.where((lane < 64) if half == 0 else (lane >= 64), qp, 0.0).astype(BF16)
        raw = _dot_nt(qh, k3)
        band = jnp.where(tri, raw[:, 2 * T:3 * T], raw[:, T:2 * T]) - _alibi_slope(h) * dist
        s.append((qh, jnp.concatenate([jnp.where(meta_ok, raw[:, 0:T], NEG), jnp.where(band_ok, band, NEG)], axis=1)))
    return s


def _attn_fwd(p, sinks, n_chunks, bg=None):
    n_rows = n_chunks * T
    kb, vb = P_K // KV_W, P_V // KV_W

    def body(q_ref, kc_ref, kp_ref, km_ref, vc_ref, vp_ref, vm_ref, sink_ref, o_ref, lse_ref):
        c = pl.program_id(0)
        sinks_v = sink_ref[...]
        masks = _attn_masks(c)
        tri, meta_ok, band_ok, dist = masks
        lane = _iota((T, 128), 1)
        for kvh in range(KV_HEADS):
            k3 = jnp.concatenate([_dup_half(r[...], kvh) for r in (km_ref, kp_ref, kc_ref)], axis=0).astype(BF16)
            v3 = jnp.concatenate([_dup_half(r[...], kvh) for r in (vm_ref, vp_ref, vc_ref)], axis=0)
            v3bd = _block_diag_rows(v3).astype(BF16)
            q2 = q_ref[:, 256 * kvh:256 * kvh + 256] * SCALE
            q4 = jnp.concatenate([jnp.where((lane < 64) if half == 0 else (lane >= 64), q2[:, 128 * pr:128 * pr + 128], 0.0)
                                  for pr in range(2) for half in range(2)], axis=0).astype(BF16)
            raw4 = _dot_nt(q4, k3)
            probs = []
            for hh in range(4):
                h = 4 * kvh + hh
                raw = raw4[T * hh:T * hh + T]
                band = jnp.where(tri, raw[:, 2 * T:3 * T], raw[:, T:2 * T]) - _alibi_slope(h) * dist
                sc = jnp.concatenate([jnp.where(meta_ok, raw[:, 0:T], NEG), jnp.where(band_ok, band, NEG)], axis=1)
                sink = sinks_v[:, h:h + 1]
                m = jnp.maximum(jnp.max(sc, axis=1, keepdims=True), sink)
                e = jnp.exp(sc - m)
                den = jnp.sum(e, axis=1, keepdims=True) + jnp.exp(sink - m)
                probs.append(_unfold(e * (1.0 / den), tri))
                lse_ref[:, h:h + 1] = m + jnp.log(den)
            p4 = jnp.concatenate([jnp.concatenate(probs[0:2], axis=1), jnp.concatenate(probs[2:4], axis=1)], axis=0)
            out = _dot(p4.astype(BF16), v3bd)
            o_ref[:, 256 * kvh:256 * kvh + 256] = jnp.concatenate([out[0:T], out[T:2 * T]], axis=1).astype(o_ref.dtype)

    blk = lambda width, col: pl.BlockSpec((T, width), lambda c: (c, col))
    prev = lambda width, col: pl.BlockSpec((T, width), lambda c: (jnp.maximum(c - 1, 0), col))
    first = lambda width, col: pl.BlockSpec((T, width), lambda c: (0, col))
    return _call(
        body, "attn_fwd", (n_chunks,),
        [blk(ATTN_W, P_Q // ATTN_W), blk(KV_W, kb), prev(KV_W, kb), first(KV_W, kb),
         blk(KV_W, vb), prev(KV_W, vb), first(KV_W, vb), pl.BlockSpec((1, 128), lambda c: (0, 0))],
        [pl.BlockSpec((T, ATTN_W), lambda c: (c, 0)), pl.BlockSpec((T, 128), lambda c: (c, 0))],
        [jax.ShapeDtypeStruct((n_rows, ATTN_W), BF16), jax.ShapeDtypeStruct((n_rows, 128), F32)],
        [p, p, p, p, p, p, p, sinks], bg=bg)


def _block_diag_rows(x3):
    lane = _iota(x3.shape, 1)
    return jnp.concatenate([jnp.where(lane < 64, x3, 0.0), jnp.where(lane >= 64, x3, 0.0)], axis=0)


def _fold_halves(x):
    return x + pltpu.roll(x, 64, 1)


def _attn_bwd(p, sinks, ao, lse, dao, dp, n_chunks, bg=None):
    kb, vb = P_K // KV_W, P_V // KV_W
    rc = lambda s: n_chunks - 1 - s

    def body(q_ref, kc_ref, kp_ref, km_ref, vc_ref, vp_ref, vm_ref, sink_ref, o_ref, lse_ref, do_ref, dp_in_ref,
             dqkv_ref, dsink_ref, kcar_ref, vcar_ref, kmeta_ref, vmeta_ref):
        step = pl.program_id(0)
        c = n_chunks - 1 - step

        @pl.when(step == 0)
        def _():
            for r in (kcar_ref, vcar_ref, kmeta_ref, vmeta_ref):
                r[...] = jnp.zeros_like(r)

        masks = _attn_masks(c)
        tri = masks[0]
        q = q_ref[...] * SCALE
        sinks_v = sink_ref[...]
        lse_v = lse_ref[...]
        ov = o_ref[...].astype(F32)
        dov = do_ref[...].astype(F32)
        lane = _iota((T, 128), 1)
        lane256 = _iota((3 * T, KV_W), 1)
        dsink = jnp.zeros((1, 128), F32)
        dk3_all = jnp.zeros((3 * T, KV_W), F32)
        dv3_all = jnp.zeros((3 * T, KV_W), F32)
        dqs = []
        for kvh in range(KV_HEADS):
            k3 = jnp.concatenate([_dup_half(r[...], kvh) for r in (km_ref, kp_ref, kc_ref)], axis=0).astype(BF16)
            v3 = jnp.concatenate([_dup_half(r[...], kvh) for r in (vm_ref, vp_ref, vc_ref)], axis=0).astype(BF16)
            dk3 = jnp.zeros((3 * T, 128), F32)
            dv3 = jnp.zeros((3 * T, 128), F32)
            for pr in range(2):
                h0 = 4 * kvh + 2 * pr
                blk = 2 * kvh + pr
                qp = q[:, 128 * blk:128 * blk + 128]
                dop = dov[:, 128 * blk:128 * blk + 128]
                prod = dop * ov[:, 128 * blk:128 * blk + 128]
                dq_pair = jnp.zeros((T, 128), F32)
                for half, ((qh, sc), h) in enumerate(zip(_attn_scores(qp, k3, masks, h0), (h0, h0 + 1))):
                    mine = (lane < 64) if half == 0 else (lane >= 64)
                    lse_h = lse_v[:, h:h + 1]
                    pm = jnp.exp(sc - lse_h)
                    doh = jnp.where(mine, dop, 0.0).astype(BF16)
                    delta = jnp.sum(jnp.where(mine, prod, 0.0), axis=1, keepdims=True)
                    dp = _fold(_dot_nt(doh, v3), tri)
                    ds = _unfold(pm * (dp - delta), tri).astype(BF16)
                    p_sink = jnp.exp(sinks_v[:, h:h + 1] - lse_h)
                    dsink = jnp.where(_iota((1, 128), 1) == h, jnp.sum(-p_sink * delta, axis=0, keepdims=True), dsink)
                    dq_pair = jnp.where(mine, _dot(ds, k3), dq_pair)
                    dk3 = dk3 + _dot_tn(ds, qh)
                    dv3 = dv3 + _dot_tn(_unfold(pm, tri).astype(BF16), doh)
                dqs.append(dq_pair * SCALE)
            in_place = (lane256 >= 64 * kvh) & (lane256 < 64 * kvh + 64)
            wide = lambda x: jnp.concatenate([x, x], axis=1)
            dk3_all = jnp.where(in_place, wide(_fold_halves(dk3)), dk3_all)
            dv3_all = jnp.where(in_place, wide(_fold_halves(dv3)), dv3_all)
        dsink_all = dsink

        @pl.when(step == 0)
        def _():
            dsink_ref[...] = dsink_all

        @pl.when(step > 0)
        def _():
            dsink_ref[...] += dsink_all

        kmeta = kmeta_ref[...] + dk3_all[0:T]
        vmeta = vmeta_ref[...] + dv3_all[0:T]
        kmeta_ref[...] = kmeta
        vmeta_ref[...] = vmeta
        is_first = c == 0
        dk = jnp.where(is_first, kmeta, dk3_all[2 * T:3 * T] + kcar_ref[...])
        dv = jnp.where(is_first, vmeta, dv3_all[2 * T:3 * T] + vcar_ref[...])
        dqkv_ref[...] = jnp.concatenate(dqs + [dk, dv], axis=1).astype(dqkv_ref.dtype)
        kcar_ref[...] = dk3_all[T:2 * T]
        vcar_ref[...] = dv3_all[T:2 * T]

    blk = lambda width, col: pl.BlockSpec((T, width), lambda s: (rc(s), col))
    prev = lambda width, col: pl.BlockSpec((T, width), lambda s: (jnp.maximum(rc(s) - 1, 0), col))
    first = lambda width, col: pl.BlockSpec((T, width), lambda s: (0, col))
    return _call(
        body, "attn_bwd", (n_chunks,),
        [blk(ATTN_W, P_Q // ATTN_W), blk(KV_W, kb), prev(KV_W, kb), first(KV_W, kb),
         blk(KV_W, vb), prev(KV_W, vb), first(KV_W, vb), pl.BlockSpec((1, 128), lambda s: (0, 0)),
         blk(ATTN_W, 0), blk(128, 0), blk(ATTN_W, 0), ANY],
        [blk(QKV_W, P_Q // QKV_W), pl.BlockSpec((1, 128), lambda s: (0, 0))],
        [jax.ShapeDtypeStruct(dp.shape, dp.dtype), jax.ShapeDtypeStruct((1, 128), F32)],
        [p, p, p, p, p, p, p, sinks, ao, lse, dao, dp],
        scratch_shapes=[pltpu.VMEM((T, KV_W), F32)] * 4, aliases={11: 0}, bg=bg)


def _pad_lanes(v, width=128):
    return jnp.pad(v, ((0, 0), (0, width - v.shape[1])))


def _local_step(x, head, tgt, plan):
    w, g, run = plan.w, plan.g, plan.run
    n_tok = x.shape[0]
    n_rows = n_tok + T
    n_chunks = n_rows // T
    tm = _row_tile(n_rows, 384)
    dt_bias, a_log, d_skip = (_pad_lanes(w[k]) for k in ("ssm_dt_bias", "ssm_a_log", "ssm_d_skip"))
    sinks = _pad_lanes(w["attn_sinks"])
    x_in = [(x, D_MODEL, 0, "prev"), (head, D_MODEL, 0, "first")]

    def h0_tile(r0, xt, hd):
        return jnp.where(r0 < T, hd, xt)

    n1, = _rowwise("norm_pre_mix", lambda r0, xt, hd, wn: [_rms(h0_tile(r0, xt, hd), wn)], n_rows, T,
                   x_in, [w["norm_pre_mix"]], [(D_MODEL, BF16)], [])
    p = _matmul("in_proj", n1, w["w_cat"], "nn", F32)
    y_ssd, hin = run("ssd_fwd", _ssd_fwd, p, w["ssm_conv_w"], w["ssm_conv_b"], dt_bias, a_log, d_skip, n_chunks)
    ao, lse = run("attn_fwd", _attn_fwd, p, sinks, n_chunks)

    def gate_norm(r0, y, z, wn):
        return [_rms(y * _silu(z), wn)]

    yn, = _rowwise("ssm_gate_norm", gate_norm, n_rows, tm, [(y_ssd, D_INNER, 0), (p, D_INNER, P_Z // D_INNER)],
                   [w["ssm_norm"]], [(D_INNER, BF16)], [])
    y_ssm = _matmul("ssm_out", yn, w["w_ssm_out"], "nn", F32)
    y_attn = _matmul("attn_out", ao, w["w_attn_out"], "nn", F32)

    def mix_gate(r0, ys, ya, gs, ga):
        return [_sigmoid(gs) * ys + _sigmoid(ga) * ya]

    gate_ins = [(p, D_MODEL, P_GATE // D_MODEL), (p, D_MODEL, P_GATE // D_MODEL + 1)]
    mixed, = _rowwise("mix_gate", mix_gate, n_rows, tm, [(y_ssm, D_MODEL, 0), (y_attn, D_MODEL, 0)] + gate_ins,
                      [], [(D_MODEL, BF16)], [])
    mix = _matmul("mix_out", mixed, w["w_mix_out"], "nn", F32)

    def post_mix(r0, mx, xt, hd, w_post, w_pre):
        h1 = jnp.where(_valid_rows(r0, mx.shape[0], PAD), h0_tile(r0, xt, hd) + _rms(mx, w_post), 0.0)
        return [h1, _rms(h1, w_pre)]

    h1, n2 = _rowwise("post_mix", post_mix, n_rows, T, [(mix, D_MODEL, 0)] + x_in,
                      [w["norm_post_mix"], w["norm_pre_ffn"]], [(D_MODEL, F32), (D_MODEL, BF16)], [])
    u_raw = _matmul("ffn_up", n2, w["w_ffn_up"], "nn", F32)
    f = _ffn_act("ffn_act", u_raw, w["ffn_conv_w"], w["ffn_conv_b"], n_rows)
    ffn = _matmul("ffn_down", f, w["w_ffn_down"], "nn", F32)

    def final(r0, fo, h, t, w_post):
        real = r0 >= T
        err = jnp.where(real, h + _rms(fo, w_post) - t, 0.0)
        dy = err * (1.0 / D_MODEL)
        dffn, dw = _rms_bwd(dy, fo, w_post)
        return [dffn, dy, jnp.sum(err * err, axis=0, keepdims=True), dw]

    dffn, dh2, loss_cols, g_norm_post_ffn = _rowwise(
        "loss_head", final, n_rows, T, [(ffn, D_MODEL, 0), (h1, D_MODEL, 0), (tgt, D_MODEL, 0, "prev")],
        [w["norm_post_ffn"]], [(D_MODEL, BF16), (D_MODEL, F32)], [D_MODEL, D_MODEL])

    g["norm_post_ffn"] = g_norm_post_ffn
    g["w_ffn_down"] = _matmul("ffn_down_dw", f, dffn, "tn", F32)
    df = _matmul("ffn_down_dx", dffn, w["w_ffn_down"], "nt", F32)
    du_raw, dconv = _conv_bwd("ffn_act_bwd", u_raw, 0, [df], [(0, c0) for c0 in range(0, FFN_DIM, CONV_LANES)],
                              w["ffn_conv_w"], w["ffn_conv_b"], n_rows, True)
    g["ffn_conv_w"], g["ffn_conv_b"] = dconv[0:3], dconv[3:4]
    g["w_ffn_up"] = _matmul("ffn_up_dw", n2, du_raw, "tn", F32)
    dn2 = run("ffn_up_dx", _matmul, "ffn_up_dx", du_raw, w["w_ffn_up"], "nt", F32)

    def post_mix_bwd(r0, dn, d2, h, mx, w_pre, w_post):
        dx, dw_pre = _rms_bwd(dn, h, w_pre)
        dh1 = jnp.where(_valid_rows(r0, dn.shape[0], PAD), dx + d2, 0.0)
        dmix, dw_post = _rms_bwd(dh1, mx, w_post)
        return [dh1, dmix, dw_pre, dw_post]

    dh1, dmix, g["norm_pre_ffn"], g["norm_post_mix"] = _rowwise(
        "post_mix_bwd", post_mix_bwd, n_rows, tm,
        [(dn2, D_MODEL, 0), (dh2, D_MODEL, 0), (h1, D_MODEL, 0), (mix, D_MODEL, 0)],
        [w["norm_pre_ffn"], w["norm_post_mix"]], [(D_MODEL, F32), (D_MODEL, BF16)], [D_MODEL, D_MODEL])
    g["w_mix_out"] = _matmul("mix_out_dw", mixed, dmix, "tn", F32)
    dmixed = _matmul("mix_out_dx", dmix, w["w_mix_out"], "nt", F32)

    def mix_gate_bwd(r0, dm, ys, ya, gs, ga):
        ss, sa = _sigmoid(gs), _sigmoid(ga)
        dgate = jnp.concatenate([dm * ys * ss * (1.0 - ss), dm * ya * sa * (1.0 - sa)], axis=1)
        return [dm * ss, dm * sa, dgate]

    dys, dya, dp = _rowwise(
        "mix_gate_bwd", mix_gate_bwd, n_rows, tm,
        [(dmixed, D_MODEL, 0), (y_ssm, D_MODEL, 0), (y_attn, D_MODEL, 0)] + gate_ins,
        [], [(D_MODEL, BF16), (D_MODEL, BF16), (2 * D_MODEL, BF16, "new", P_W, P_GATE // (2 * D_MODEL))], [])
    g["w_ssm_out"] = _matmul("ssm_out_dw", yn, dys, "tn", F32)
    dyn = _matmul("ssm_out_dx", dys, w["w_ssm_out"], "nt", F32)
    g["w_attn_out"] = _matmul("attn_out_dw", ao, dya, "tn", F32)
    dao = _matmul("attn_out_dx", dya, w["w_attn_out"], "nt", BF16)

    def gate_norm_bwd(r0, dn, y, z, wn):
        sz, dsz = _silu_grad(z)
        dyz, dw = _rms_bwd(dn, y * sz, wn)
        live = _valid_rows(r0, dn.shape[0], PAD)
        return [jnp.where(live, dyz * sz, 0.0), jnp.where(live, dyz * y * dsz, 0.0), dw]

    dy_ssd, dp, g["ssm_norm"] = run(
        "ssm_gate_norm_bwd", _rowwise, "ssm_gate_norm_bwd", gate_norm_bwd, n_rows, tm,
        [(dyn, D_INNER, 0), (y_ssd, D_INNER, 0), (p, D_INNER, P_Z // D_INNER)],
        [w["ssm_norm"]], [(D_INNER, F32), (D_INNER, BF16, "into", dp, P_Z // D_INNER)], [D_INNER])
    dp, dsink = run("attn_bwd", _attn_bwd, p, sinks, ao, lse, dao, dp, n_chunks)
    g["attn_sinks"] = dsink[:, 0:ATTN_HEADS]
    dxs, dbm, dcm, ddt_parts, dpar = run("ssd_bwd", _ssd_bwd, p, w["ssm_conv_w"], w["ssm_conv_b"], dt_bias, a_log,
                                         d_skip, hin, dy_ssd, n_chunks)
    dpar = jnp.sum(dpar, axis=0)
    g["ssm_dt_bias"], g["ssm_a_log"], g["ssm_d_skip"] = (dpar[i:i + 1, 0:SSM_HEADS] for i in range(3))

    def dt_grad(r0, parts):
        tot = parts[:, 0:128] + parts[:, 128:256] + parts[:, 256:384] + parts[:, 384:512]
        return [jnp.concatenate([tot, jnp.zeros((parts.shape[0], P_Z - P_DT - 128), F32)], axis=1)]

    dt_w = P_Z - P_DT
    dp, = _rowwise("dt_grad", dt_grad, n_rows, tm, [(ddt_parts, SSM_GROUPS * 128, 0)], [],
                   [(dt_w, BF16, "into", dp, P_DT // dt_w)], [])
    x_chunks = [(src, c0) for src, arr in enumerate((dxs, dbm, dcm)) for c0 in range(0, arr.shape[1], CONV_LANES)]
    dp, dconv = run("ssm_conv_bwd", _conv_bwd, "ssm_conv_bwd", p, P_XBC // CONV_DIM, [dxs, dbm, dcm], x_chunks,
                    w["ssm_conv_w"], w["ssm_conv_b"], n_rows, False, into=dp, into_blk=P_XBC // CONV_DIM)
    g["ssm_conv_w"], g["ssm_conv_b"] = dconv[0:4], dconv[4:5]
    g["w_cat"] = _matmul("in_proj_dw", n1, dp, "tn", F32)
    dn1 = run("in_proj_dx", _matmul, "in_proj_dx", dp, w["w_cat"], "nt", F32)

    def pre_mix_bwd(r0, dn, d1, xt, hd, wn):
        dx, dw = _rms_bwd(dn, h0_tile(r0, xt, hd), wn)
        dh0 = jnp.where(_valid_rows(r0, dn.shape[0], PAD), dx + d1, 0.0)
        return [dh0, dh0, dw]

    dx_out, dhead, g["norm_pre_mix"] = _rowwise(
        "pre_mix_bwd", pre_mix_bwd, n_rows, T, [(dn1, D_MODEL, 0), (dh1, D_MODEL, 0)] + x_in,
        [w["norm_pre_mix"]], [(D_MODEL, F32, "prev", n_tok), (D_MODEL, F32, "first")], [D_MODEL])
    return jnp.sum(loss_cols), dx_out, dhead


_IN_SECTIONS = [((5152, 6176), P_Q), ((6176, 6432), P_K), ((6432, 6688), P_V), ((5120, 5152), P_DT),
                ((0, 2048), P_Z), ((6688, 8736), P_GATE), ((2048, 5120), P_XBC)]


IN_SHARD = N_IN // 4


def _shard_pieces(a, b):
    return [(j, max(a, j * IN_SHARD) - j * IN_SHARD, min(b, (j + 1) * IN_SHARD) - j * IN_SHARD)
            for j in range(4) if max(a, j * IN_SHARD) < min(b, (j + 1) * IN_SHARD)]


def _to_cat(w4):
    parts, at = [], 0
    for (a, b), off in _IN_SECTIONS:
        if off > at:
            parts.append(jnp.zeros((w4.shape[1], off - at), w4.dtype))
        parts += [w4[j, :, lo:hi] for j, lo, hi in _shard_pieces(a, b)]
        at = off + (b - a)
    return jnp.concatenate(parts, axis=1)


def _from_cat(g_cat):
    shards = [[] for _ in range(4)]
    for (a, b), off in sorted(_IN_SECTIONS):
        for j, lo, hi in _shard_pieces(a, b):
            start = off + j * IN_SHARD + lo - a
            shards[j].append(g_cat[:, start:start + hi - lo])
    return jnp.stack([jnp.concatenate(s, axis=1) for s in shards])


LANES = 1024
_BIG = [("w_in", 1024, 2184, "chip"), ("w_ssm_out", 512, 1024, "row"), ("w_attn_out", 256, 1024, "row"),
        ("w_mix_out", 256, 1024, "row"), ("w_ffn_up", 1024, 1408, "col"), ("w_ffn_down", 704, 1024, "row"),
        ("small", 32, LANES, "chip")]
_SMALL_SHARDED = [("ssm_conv_w", (4, 768), 1), ("ffn_conv_w", (3, 1408), 1), ("meta_tokens", (16, 256), 1)]
_REPLICATED = [("norm_pre_mix", 1024), ("ssm_conv_b", 3072), ("ssm_dt_bias", 32), ("ssm_a_log", 32),
               ("ssm_d_skip", 32), ("ssm_norm", 2048), ("attn_sinks", 16), ("norm_post_mix", 1024),
               ("norm_pre_ffn", 1024), ("ffn_conv_b", 5632), ("norm_post_ffn", 1024)]
SMALL_ROWS = 16
WEIGHT_ORDER = ["meta_tokens", "norm_pre_mix", "w_in", "ssm_conv_w", "ssm_conv_b", "ssm_dt_bias", "ssm_a_log",
                "ssm_d_skip", "ssm_norm", "w_ssm_out", "attn_sinks", "w_attn_out", "w_mix_out", "norm_post_mix",
                "norm_pre_ffn", "w_ffn_up", "ffn_conv_w", "ffn_conv_b", "w_ffn_down", "norm_post_ffn"]


def _flatten(parts, rows):
    flat = jnp.concatenate([a.reshape(-1) for a in parts])
    return jnp.pad(flat, (0, rows * LANES - flat.shape[0])).reshape(rows, LANES)


def _unflatten(flat, shapes):
    flat = flat.reshape(-1)
    out, off = [], 0
    for shp in shapes:
        n = math.prod(shp)
        out.append(flat[off:off + n].reshape(shp))
        off += n
    return out


def _shard_of(full, chip, shape, axis):
    return lax.slice_in_dim(full, chip * shape[axis], (chip + 1) * shape[axis], axis=axis)


def _full_shape(r, c, layout):
    return {"row": (4 * r, c), "col": (r, 4 * c), "chip": (4, r, c)}[layout]


def _shard_view(ref, r, c, layout, chip):
    if layout == "row":
        return ref.at[pl.ds(pl.multiple_of(chip * r, 16), r), :]
    if layout == "col":
        return ref.at[:, pl.ds(pl.multiple_of(chip * c, 128), c)]
    return ref.at[chip]


def _half_view(ref, r, c, layout, chip, half):
    hr = r // 2
    if layout == "row":
        return ref.at[pl.ds(pl.multiple_of(chip * r + half * hr, 16), hr), :]
    r0 = pl.multiple_of(half * hr, 16)
    if layout == "col":
        return ref.at[pl.ds(r0, hr), pl.ds(pl.multiple_of(chip * c, 128), c)]
    return ref.at[chip, pl.ds(r0, hr), :]


def _mesh_pos():
    return lax.axis_index("x"), lax.axis_index("y"), lax.axis_index("c")


def _other_chips(x, y):
    return [(1 - x, y), (x, 1 - y), (1 - x, 1 - y)]


def _chip_index(x, y):
    return 2 * x + y


def _run_exchange(name, ex):
    n_in, n_out = len(ex.ins), len(ex.out_shapes)

    def body(*refs):
        in_refs, out_refs = refs[:n_in], refs[n_in:n_in + n_out]
        send_sems, recv_sems = refs[n_in + n_out:]
        copies = [pltpu.make_async_remote_copy(src_ref=s, dst_ref=d, send_sem=send_sems.at[i], recv_sem=recv_sems.at[i],
                                               device_id=dev, device_id_type=MESH)
                  for i, (s, d, dev) in enumerate(ex.make_copies(in_refs, out_refs))]
        assert len(copies) == ex.n_copies
        for cp in copies:
            cp.start()
        for cp in copies:
            cp.wait()

    return pl.pallas_call(
        body, name=name, in_specs=[ANY] * n_in, out_specs=[ANY] * n_out, out_shape=list(ex.out_shapes),
        scratch_shapes=[pltpu.SemaphoreType.DMA((ex.n_copies,)), pltpu.SemaphoreType.DMA((ex.n_copies,))],
        compiler_params=pltpu.CompilerParams(has_side_effects=True),
    )(*ex.ins)


def _join(*exs):
    def make(in_refs, out_refs):
        copies, i0, o0 = [], 0, 0
        for ex in exs:
            copies += ex.make_copies(in_refs[i0:i0 + len(ex.ins)], out_refs[o0:o0 + len(ex.out_shapes)])
            i0, o0 = i0 + len(ex.ins), o0 + len(ex.out_shapes)
        return copies

    return _Exchange([a for ex in exs for a in ex.ins], [s for ex in exs for s in ex.out_shapes], make,
                     sum(ex.n_copies for ex in exs))


def _split(exs, results):
    out, o0 = [], 0
    for ex in exs:
        out.append(list(results[o0:o0 + len(ex.out_shapes)]))
        o0 += len(ex.out_shapes)
    return out


def _gather_ici(entries, shards):
    def make(in_refs, out_refs):
        x, y, c = _mesh_pos()
        j = _chip_index(x, y)
        copies = []
        for ref_in, ref_out, (_, r, cc, lay) in zip(in_refs, out_refs, entries):
            copies.append((ref_in, _shard_view(ref_out, r, cc, lay, j), None))
            mine = ref_in.at[pl.ds(pl.multiple_of(c * (r // 2), 16), r // 2), :]
            copies += [(mine, _half_view(ref_out, r, cc, lay, j, c), (*ch, c)) for ch in _other_chips(x, y)]
        return copies

    shapes = [jax.ShapeDtypeStruct(_full_shape(r, cc, lay), s.dtype) for s, (_, r, cc, lay) in zip(shards, entries)]
    return _Exchange(list(shards), shapes, make, 4 * len(entries))


def _gather_pass_on(entries, fulls):
    def make(in_refs, out_refs):
        x, y, c = _mesh_pos()
        copies = []
        for ref, (_, r, cc, lay) in zip(out_refs, entries):
            for ch in _other_chips(x, y):
                landed = _half_view(ref, r, cc, lay, _chip_index(*ch), c)
                copies.append((landed, landed, (x, y, 1 - c)))
        return copies

    return _Exchange(list(fulls), [jax.ShapeDtypeStruct(f.shape, f.dtype) for f in fulls], make, 3 * len(entries),
                     {a: a for a in range(len(entries))})


def _gather_weights(entries, shards):
    n = len(entries)

    def body(*refs):
        ins, outs = refs[:n], refs[n:2 * n]
        send_sems, recv_sems, local_sems = refs[2 * n:]
        x, y, c = _mesh_pos()
        j = _chip_index(x, y)
        sibling = (x, y, 1 - c)
        chips = _other_chips(x, y)
        idx = [_chip_index(*ch) for ch in chips]

        def remote(k, src, dst, dev):
            return pltpu.make_async_remote_copy(src_ref=src, dst_ref=dst, send_sem=send_sems.at[k],
                                                recv_sem=recv_sems.at[k], device_id=dev, device_id_type=MESH)

        own = [pltpu.make_async_copy(ins[a], _shard_view(outs[a], r, cc, lay, j), local_sems.at[a])
               for a, (_, r, cc, lay) in enumerate(entries)]
        for cp in own:
            cp.start()
        first, passed = [], []
        for a, (_, r, cc, lay) in enumerate(entries):
            mine = ins[a].at[pl.ds(pl.multiple_of(c * (r // 2), 16), r // 2), :]
            for k, ch in enumerate(chips):
                first.append(remote(6 * a + k, mine, _half_view(outs[a], r, cc, lay, j, c), (*ch, c)))
                landed = _half_view(outs[a], r, cc, lay, idx[k], c)
                passed.append(remote(6 * a + 3 + k, landed, landed, sibling))
        for cp in first:
            cp.start()
        for a, (_, r, cc, lay) in enumerate(entries):
            for k in range(3):
                landed = _half_view(outs[a], r, cc, lay, idx[k], c)
                remote(6 * a + k, landed, landed, sibling).wait_recv()
                passed[3 * a + k].start()
        for a, (_, r, cc, lay) in enumerate(entries):
            for k in range(3):
                theirs = _half_view(outs[a], r, cc, lay, idx[k], 1 - c)
                remote(6 * a + 3 + k, theirs, theirs, sibling).wait_recv()
        for cp in first + passed:
            cp.wait_send()
        for cp in own:
            cp.wait()

    return pl.pallas_call(
        body, name="gather_weights", in_specs=[ANY] * n, out_specs=[ANY] * n,
        out_shape=[jax.ShapeDtypeStruct(_full_shape(r, cc, lay), s.dtype) for s, (_, r, cc, lay) in zip(shards, entries)],
        scratch_shapes=[pltpu.SemaphoreType.DMA((6 * n,)), pltpu.SemaphoreType.DMA((6 * n,)), pltpu.SemaphoreType.DMA((n,))],
        compiler_params=pltpu.CompilerParams(has_side_effects=True),
    )(*shards)


def _pair_exchange(entries, grads):
    def make(in_refs, out_refs):
        x, y, c = _mesh_pos()
        return [(_half_view(ref_in, r, cc, lay, i, 1 - c), ref_out.at[i], (x, y, 1 - c))
                for ref_in, ref_out, (_, r, cc, lay) in zip(in_refs, out_refs, entries) for i in range(4)]

    return _Exchange(list(grads), [jax.ShapeDtypeStruct((4, r // 2, cc), F32) for _, r, cc, _ in entries], make,
                     4 * len(entries))


def _whole_to_sibling(arrays):
    def make(in_refs, out_refs):
        x, y, c = _mesh_pos()
        return [(r, o, (x, y, 1 - c)) for r, o in zip(in_refs, out_refs)]

    return _Exchange(list(arrays), [jax.ShapeDtypeStruct(a.shape, a.dtype) for a in arrays], make, len(arrays))


def _chip_exchange(psends):
    def make(in_refs, out_refs):
        x, y, c = _mesh_pos()
        return [(ref_in.at[_chip_index(*ch)], ref_out.at[k], (*ch, c))
                for ref_in, ref_out in zip(in_refs, out_refs) for k, ch in enumerate(_other_chips(x, y))]

    return _Exchange(list(psends), [jax.ShapeDtypeStruct((3,) + p.shape[1:], p.dtype) for p in psends], make,
                     3 * len(psends))


def _to_all_chips(array):
    def make(in_refs, out_refs):
        x, y, c = _mesh_pos()
        return [(in_refs[0], out_refs[0].at[k], (*ch, c)) for k, ch in enumerate(_other_chips(x, y))]

    return _Exchange([array], [jax.ShapeDtypeStruct((3,) + array.shape, array.dtype)], make, 3)


SUM_ROWS = 256
ADAM_ROWS = 128


def _pair_sum(name, grad, recv, ids, r, c, layout):
    hr = r // 2
    tr = _row_tile(hr, SUM_ROWS)
    nb = hr // tr

    def body(ids_ref, g_ref, r_ref, send_ref, own_ref):
        s = g_ref[...] + r_ref[...]
        send_ref[...] = s.astype(send_ref.dtype)

        @pl.when(pl.program_id(1) == ids_ref[1])
        def _():
            own_ref[...] = s

    if layout == "row":
        g_spec = pl.BlockSpec((tr, c), lambda t, j, ids_ref: ((j * r + ids_ref[0] * hr) // tr + t, 0))
    elif layout == "col":
        g_spec = pl.BlockSpec((tr, c), lambda t, j, ids_ref: (ids_ref[0] * nb + t, j))
    else:
        g_spec = pl.BlockSpec((None, tr, c), lambda t, j, ids_ref: (j, ids_ref[0] * nb + t, 0))
    grid_spec = pltpu.PrefetchScalarGridSpec(
        num_scalar_prefetch=1, grid=(nb, 4),
        in_specs=[g_spec, pl.BlockSpec((None, tr, c), lambda t, j, ids_ref: (j, t, 0))],
        out_specs=[pl.BlockSpec((None, tr, c), lambda t, j, ids_ref: (j, t, 0)),
                   pl.BlockSpec((tr, c), lambda t, j, ids_ref: (t, 0))])
    return pl.pallas_call(
        body, name=name, grid_spec=grid_spec,
        out_shape=[jax.ShapeDtypeStruct((4, hr, c), BF16), jax.ShapeDtypeStruct((hr, c), F32)],
        compiler_params=_cparams(2),
    )(ids, grad, recv)


def _chip_sum(name, own, recv):
    hr, c = own.shape
    tr = _row_tile(hr, SUM_ROWS)

    def body(o_ref, r_ref, out_ref):
        out_ref[...] = ((o_ref[...] + r_ref[0].astype(F32)) + r_ref[1].astype(F32)) + r_ref[2].astype(F32)

    return pl.pallas_call(
        body, name=name, grid=(hr // tr,),
        in_specs=[pl.BlockSpec((tr, c), lambda i: (i, 0)), pl.BlockSpec((3, tr, c), lambda i: (0, i, 0))],
        out_specs=pl.BlockSpec((tr, c), lambda i: (i, 0)),
        out_shape=jax.ShapeDtypeStruct((hr, c), F32), compiler_params=_cparams(1),
    )(own, recv)


def _chip_sum_small(own, recv, ids):
    def body(ids_ref, o_ref, r_ref, out_ref):
        j = ids_ref[1]
        total = None
        for i in range(4):
            m = jnp.bitwise_xor(i, j)
            term = jnp.where(m == 0, o_ref[...], jnp.where(m == 2, r_ref[0], jnp.where(m == 1, r_ref[1], r_ref[2])))
            total = term if total is None else total + term
        out_ref[...] = total

    grid_spec = pltpu.PrefetchScalarGridSpec(
        num_scalar_prefetch=1, grid=(1,),
        in_specs=[pl.BlockSpec(own.shape, lambda i, ids_ref: (0, 0)), pl.BlockSpec(recv.shape, lambda i, ids_ref: (0, 0, 0))],
        out_specs=pl.BlockSpec(own.shape, lambda i, ids_ref: (0, 0)))
    return pl.pallas_call(body, name="chip_sum_small", grid_spec=grid_spec,
                          out_shape=jax.ShapeDtypeStruct(own.shape, F32), compiler_params=_cparams(1))(ids, own, recv)


def _adamw(name, w, m, v, mine, theirs, ids):
    rows, cols = w.shape
    half = rows // 2
    tr = _row_tile(half, ADAM_ROWS, unit=8)
    nb = half // tr
    c1 = 1.0 / (1.0 - ADAM_B1 ** ADAM_STEP)
    c2 = 1.0 / (1.0 - ADAM_B2 ** ADAM_STEP)

    def body(ids_ref, w_ref, m_ref, v_ref, mine_ref, theirs_ref, g_out, d_out, m_out, v_out):
        g = jnp.where(pl.program_id(0) == ids_ref[0], mine_ref[...], theirs_ref[...])
        m_new = ADAM_B1 * m_ref[...] + (1.0 - ADAM_B1) * g
        v_new = ADAM_B2 * v_ref[...] + (1.0 - ADAM_B2) * (g * g)
        d_out[...] = -ADAM_LR * ((m_new * c1) / (jnp.sqrt(v_new * c2) + ADAM_EPS) + ADAM_WD * w_ref[...])
        g_out[...] = g
        m_out[...] = m_new
        v_out[...] = v_new

    full = pl.BlockSpec((tr, cols), lambda h, i, ids_ref: (h * nb + i, 0))
    part = pl.BlockSpec((tr, cols), lambda h, i, ids_ref: (i, 0))
    grid_spec = pltpu.PrefetchScalarGridSpec(num_scalar_prefetch=1, grid=(2, nb),
                                             in_specs=[full, full, full, part, part], out_specs=[full] * 4)
    return pl.pallas_call(
        body, name=name, grid_spec=grid_spec,
        out_shape=[jax.ShapeDtypeStruct((rows, cols), F32)] * 4, compiler_params=_cparams(2),
    )(ids, w, m, v, mine, theirs)


def _small_shard(parts):
    return _flatten(parts, _BIG[-1][1])


_ENTRY = {e[0]: e for e in _BIG}
LATE_WEIGHTS = ("w_ssm_out", "w_attn_out", "w_mix_out", "w_ffn_up", "w_ffn_down")
FFN_GRADS = ("w_ffn_down", "w_ffn_up")
MIXER_GRADS = ("w_mix_out", "w_ssm_out", "w_attn_out")


class _StepPlan:
    def __init__(self, w, late_shards, shards, ids):
        self.w, self.g = w, {}
        self.late_shards, self.shards, self.ids = late_shards, shards, ids
        self.sums, self.halves, self.results = {}, {}, {}

    def run(self, name, fn, *args, **kw):
        at = getattr(self, "_at_" + name, None)
        if at is None:
            return fn(*args, **kw)
        exchange, landed = at()
        res, extra = fn(*args, bg=exchange, **kw)
        landed(extra)
        return res

    def _at_ssd_fwd(self):
        entries = [_ENTRY[n] for n in LATE_WEIGHTS]

        def landed(fulls):
            self.partly_gathered = fulls

        return _gather_ici(entries, self.late_shards), landed

    def _at_attn_fwd(self):
        entries = [_ENTRY[n] for n in LATE_WEIGHTS]
        return _gather_pass_on(entries, self.partly_gathered), lambda fulls: self.w.update(zip(LATE_WEIGHTS, fulls))

    def pair_sums(self, names, grads, recv):
        for n, gr, rv in zip(names, grads, recv):
            _, r, c, lay = _ENTRY[n]
            self.sums[n] = _pair_sum("pair_sum_" + n, gr, rv, self.ids, r, c, lay)

    def chip_sums(self, names, recv):
        for n, rv in zip(names, recv):
            self.halves[n] = _chip_sum("chip_sum_" + n, self.sums[n][1], rv)

    def adamw(self, names, theirs):
        for n, th in zip(names, theirs):
            sh = self.shards[n]
            self.results[n] = _adamw("adamw_" + n, sh["w"], sh["m"], sh["v"], self.halves[n], th, self.ids)

    def _pair_stage(self, names, grads):
        return (_pair_exchange([_ENTRY[n] for n in names], grads),
                lambda recv: self.pair_sums(names, grads, recv))

    def _at_ffn_up_dx(self):
        return self._pair_stage(FFN_GRADS, [self.g[n] for n in FFN_GRADS])

    def _at_ssm_gate_norm_bwd(self):
        return self._pair_stage(MIXER_GRADS, [self.g[n] for n in MIXER_GRADS])

    def _at_attn_bwd(self):
        return _chip_exchange([self.sums[n][0] for n in FFN_GRADS]), lambda recv: self.chip_sums(FFN_GRADS, recv)

    def _at_ssd_bwd(self):
        stages = (_chip_exchange([self.sums[n][0] for n in MIXER_GRADS]),
                  _whole_to_sibling([self.halves[n] for n in FFN_GRADS]))

        def landed(extra):
            recv, theirs = _split(stages, extra)
            self.chip_sums(MIXER_GRADS, recv)
            self.adamw(FFN_GRADS, theirs)

        return _join(*stages), landed

    def _at_ssm_conv_bwd(self):
        return _whole_to_sibling([self.halves[n] for n in MIXER_GRADS]), lambda theirs: self.adamw(MIXER_GRADS, theirs)

    def _at_in_proj_dx(self):
        grads = [_from_cat(self.g.pop("w_cat"))]
        self.pair_sums(("w_in",), grads, _run_exchange("grad_pair_exchange_w_in", _pair_exchange([_ENTRY["w_in"]], grads)))
        return _chip_exchange([self.sums["w_in"][0]]), lambda recv: self.chip_sums(("w_in",), recv)

    def finish(self, g_small, g_rep, rep_shards):
        stages = (_pair_exchange([_ENTRY["small"]], [g_small]), _whole_to_sibling([g_rep]))
        recv_small, recv_rep = _split(stages, _run_exchange("grad_pair_exchange_tail", _join(*stages)))
        self.pair_sums(("small",), [g_small], recv_small)
        p_rep, = _rowwise("pair_sum_replicated", lambda r0, a, b: [a + b], SMALL_ROWS, SMALL_ROWS,
                          [(g_rep, LANES, 0), (recv_rep[0], LANES, 0)], [], [(LANES, F32)], [])
        stages = (_chip_exchange([self.sums["small"][0]]), _to_all_chips(p_rep))
        recv, recv_rep = _split(stages, _run_exchange("grad_chip_exchange_tail", _join(*stages)))
        self.chip_sums(("small",), recv)
        g_rep_tot = _chip_sum_small(p_rep, recv_rep[0], self.ids)
        last = ("w_in", "small")
        self.adamw(last, _run_exchange("grad_half_share_tail", _whole_to_sibling([self.halves[n] for n in last])))
        ids_lo = self.ids * jnp.array([0, 1], jnp.int32)
        self.results["replicated"] = _adamw("adamw_replicated", rep_shards["w"], rep_shards["m"], rep_shards["v"],
                                            g_rep_tot[0:SMALL_ROWS // 2], g_rep_tot[SMALL_ROWS // 2:], ids_lo)


def kernel(x, meta_tokens, norm_pre_mix, w_in, ssm_conv_w, ssm_conv_b, ssm_dt_bias, ssm_a_log, ssm_d_skip, ssm_norm, w_ssm_out, attn_sinks, w_attn_out, w_mix_out, norm_post_mix, norm_pre_ffn, w_ffn_up, ffn_conv_w, ffn_conv_b, w_ffn_down, norm_post_ffn, loss_target, m_meta_tokens, m_norm_pre_mix, m_w_in, m_ssm_conv_w, m_ssm_conv_b, m_ssm_dt_bias, m_ssm_a_log, m_ssm_d_skip, m_ssm_norm, m_w_ssm_out, m_attn_sinks, m_w_attn_out, m_w_mix_out, m_norm_post_mix, m_norm_pre_ffn, m_w_ffn_up, m_ffn_conv_w, m_ffn_conv_b, m_w_ffn_down, m_norm_post_ffn, v_meta_tokens, v_norm_pre_mix, v_w_in, v_ssm_conv_w, v_ssm_conv_b, v_ssm_dt_bias, v_ssm_a_log, v_ssm_d_skip, v_ssm_norm, v_w_ssm_out, v_attn_sinks, v_w_attn_out, v_w_mix_out, v_norm_post_mix, v_norm_pre_ffn, v_w_ffn_up, v_ffn_conv_w, v_ffn_conv_b, v_w_ffn_down, v_norm_post_ffn):
    args = dict(locals())
    squeeze = lambda a: a.reshape(a.shape[-2:])
    wts = {n: squeeze(args[n]) for n in WEIGHT_ORDER}
    mom = {n: squeeze(args["m_" + n]) for n in WEIGHT_ORDER}
    var = {n: squeeze(args["v_" + n]) for n in WEIGHT_ORDER}
    x_i, y_i, c_i = _mesh_pos()
    ids = jnp.stack([c_i, _chip_index(x_i, y_i)]).astype(jnp.int32)
    big_names = [n for n, _, _, _ in _BIG[:-1]]
    small_names = [n for n, _, _ in _SMALL_SHARDED]
    rep_names = [n for n, _ in _REPLICATED]

    stacks = {"w": wts, "m": mom, "v": var}
    shards = {n: {k: d[n] for k, d in stacks.items()} for n in big_names}
    shards["small"] = {k: _small_shard([d[n] for n in small_names]) for k, d in stacks.items()}
    rep_shards = {k: _flatten([d[n] for n in rep_names], SMALL_ROWS) for k, d in stacks.items()}

    w_in4, small_all = _gather_weights([_ENTRY["w_in"], _ENTRY["small"]], [wts["w_in"].astype(BF16), shards["small"]["w"]])
    w = {n: wts[n] for n in rep_names}
    w["w_cat"] = _to_cat(w_in4)
    small_parts = [_unflatten(small_all[i], [shp for _, shp, _ in _SMALL_SHARDED]) for i in range(4)]
    for k, (n, _, axis) in enumerate(_SMALL_SHARDED):
        w[n] = jnp.concatenate([small_parts[i][k] for i in range(4)], axis=axis)
    plan = _StepPlan(w, [wts[n].astype(BF16) for n in LATE_WEIGHTS], shards, ids)

    head = jnp.concatenate([jnp.zeros((PAD, D_MODEL), F32), w["meta_tokens"]], axis=0)
    loss_sum, dx, dhead = _local_step(x[0], head, loss_target[0], plan)
    loss = lax.psum(loss_sum * (0.5 / D_MODEL), ("x", "y", "c"))
    g = plan.g
    g["meta_tokens"] = dhead[PAD:]
    g_small = jnp.stack([_small_shard([_shard_of(g[n], i, shp, ax) for n, shp, ax in _SMALL_SHARDED]) for i in range(4)])
    plan.finish(g_small, _flatten([g[n] for n in rep_names], SMALL_ROWS), rep_shards)

    results = {}
    for kind in range(4):
        results.update({(kind, n): plan.results[n][kind] for n in big_names})
        parts = _unflatten(plan.results["small"][kind], [shp for _, shp, _ in _SMALL_SHARDED])
        results.update({(kind, n): parts[k] for k, n in enumerate(small_names)})
        parts = _unflatten(plan.results["replicated"][kind], [(1, width) for _, width in _REPLICATED])
        results.update({(kind, n): parts[k] for k, n in enumerate(rep_names)})
    outs = [results[kind, n].reshape(args[n].shape) for kind in range(4) for n in WEIGHT_ORDER]
    return (loss, dx[None], *outs)
```

```python
import math
from typing import Any, Callable, NamedTuple, Sequence

import jax
import jax.numpy as jnp
from jax import lax
from jax.experimental import pallas as pl
from jax.experimental.pallas import tpu as pltpu

F32 = jnp.float32
BF16 = jnp.bfloat16

D_MODEL = 1024
N_META = 16
T = 128
PAD = T - N_META
D_INNER = 2048
SSM_HEADS = 32
HEAD_P = 64
SSM_GROUPS = 4
GROUP_W = D_INNER // SSM_GROUPS
D_STATE = 128
CONV_DIM = D_INNER + 2 * SSM_GROUPS * D_STATE
ATTN_HEADS = 16
KV_HEADS = 4
ATTN_W = 1024
KV_W = 256
FFN_DIM = 2816
N_IN = 8736
EPS = 1e-6
NEG = -1e30
SCALE = 0.125

P_Q, P_K, P_V, P_DT, P_Z, P_GATE, P_XBC = 0, 1024, 1280, 1536, 2048, 4096, 6144
QKV_W = 1536
P_W = 9216

ADAM_LR, ADAM_B1, ADAM_B2, ADAM_EPS, ADAM_WD, ADAM_STEP = 0.001, 0.9, 0.999, 1e-08, 0.01, 10

VMEM_BUDGET = 40 * 1024 * 1024
VMEM_LIMIT = 56 * 1024 * 1024
MESH = pl.DeviceIdType.MESH
ANY = pl.BlockSpec(memory_space=pl.ANY)


def _cparams(n_axes, **kw):
    return pltpu.CompilerParams(dimension_semantics=("arbitrary",) * n_axes, vmem_limit_bytes=VMEM_LIMIT, **kw)


class _Exchange(NamedTuple):
    ins: Sequence[Any]
    out_shapes: Sequence[Any]
    make_copies: Callable
    n_copies: int
    aliases: dict = {}


def _call(body, name, grid, in_specs, out_specs, out_shape, operands, scratch_shapes=(), aliases=None, bg=None):
    aliases = dict(aliases or {})
    if bg is None:
        return pl.pallas_call(body, name=name, grid=grid, in_specs=in_specs, out_specs=out_specs, out_shape=out_shape,
                              scratch_shapes=list(scratch_shapes), input_output_aliases=aliases,
                              compiler_params=_cparams(len(grid)))(*operands)
    n_in, n_out, n_scr = len(in_specs), len(out_specs), len(scratch_shapes)
    nb_in, nb_out = len(bg.ins), len(bg.out_shapes)

    def hosted(*refs):
        ins, bg_ins = refs[:n_in], refs[n_in:n_in + nb_in]
        outs = refs[n_in + nb_in:n_in + nb_in + n_out]
        bg_outs = refs[n_in + nb_in + n_out:n_in + nb_in + n_out + nb_out]
        scratch = refs[n_in + nb_in + n_out + nb_out:n_in + nb_in + n_out + nb_out + n_scr]
        send_sems, recv_sems = refs[-2:]
        pids = [pl.program_id(a) for a in range(len(grid))]
        first, last = pids[0] == 0, pids[0] == grid[0] - 1
        for p, g in zip(pids[1:], grid[1:]):
            first, last = first & (p == 0), last & (p == g - 1)
        copies = []
        for k, (src, dst, peer) in enumerate(bg.make_copies(bg_ins, bg_outs)):
            if peer is None:
                copies.append(pltpu.make_async_copy(src, dst, send_sems.at[k]))
            else:
                copies.append(pltpu.make_async_remote_copy(src_ref=src, dst_ref=dst, send_sem=send_sems.at[k],
                                                           recv_sem=recv_sems.at[k], device_id=peer, device_id_type=MESH))
        assert len(copies) == bg.n_copies

        @pl.when(first)
        def _():
            for cp in copies:
                cp.start()

        body(*ins, *outs, *scratch)

        @pl.when(last)
        def _():
            for cp in copies:
                cp.wait()

    aliases = {(k if k < n_in else k + nb_in): v for k, v in aliases.items()}
    aliases.update({n_in + k: n_out + v for k, v in bg.aliases.items()})
    res = pl.pallas_call(
        hosted, name=name, grid=grid, in_specs=list(in_specs) + [ANY] * nb_in, out_specs=list(out_specs) + [ANY] * nb_out,
        out_shape=list(out_shape) + list(bg.out_shapes), input_output_aliases=aliases,
        scratch_shapes=list(scratch_shapes) + [pltpu.SemaphoreType.DMA((bg.n_copies,))] * 2,
        compiler_params=_cparams(len(grid), has_side_effects=True))(*operands, *bg.ins)
    return res[:n_out], res[n_out:]


def _sigmoid(x):
    return 1.0 / (1.0 + jnp.exp(-x))


def _silu(x):
    return x * _sigmoid(x)


def _silu_grad(x):
    s = _sigmoid(x)
    return x * s, s * (1.0 + x * (1.0 - s))


def _dsilu(x):
    return _silu_grad(x)[1]


def _softplus(x):
    e = jnp.exp(-jnp.abs(x))
    small = e * (1.0 - e * (0.5 - e * (1.0 / 3.0)))
    return jnp.maximum(x, 0.0) + jnp.where(e < 0.01, small, jnp.log(1.0 + e))


def _rms(x, w):
    r = lax.rsqrt(jnp.mean(x * x, axis=-1, keepdims=True) + EPS)
    return x * r * w


def _rms_bwd(dy, x, w):
    r = lax.rsqrt(jnp.mean(x * x, axis=-1, keepdims=True) + EPS)
    xh = x * r
    g = dy * w
    dx = r * (g - xh * jnp.mean(g * xh, axis=-1, keepdims=True))
    dw = jnp.sum(dy * xh, axis=0, keepdims=True)
    return dx, dw


def _dot(a, b):
    return jnp.dot(a, b, preferred_element_type=F32)


def _dot_nt(a, b):
    return lax.dot_general(a, b, (((1,), (1,)), ((), ())), preferred_element_type=F32)


def _dot_tn(a, b):
    return lax.dot_general(a, b, (((0,), (0,)), ((), ())), preferred_element_type=F32)


def _split3(x):
    hi = x.astype(BF16)
    r = x - hi.astype(F32)
    mid = r.astype(BF16)
    lo = (r - mid.astype(F32)).astype(BF16)
    return hi, mid, lo


def _xdot(x, e):
    hi, mid, lo = _split3(x)
    return _dot(hi, e) + _dot(mid, e) + _dot(lo, e)


def _xdot_l(e, x):
    hi, mid, lo = _split3(x)
    return _dot(e, hi) + _dot(e, mid) + _dot(e, lo)


def _iota(shape, dim):
    return lax.broadcasted_iota(jnp.int32, shape, dim)


def _divisors(n, unit):
    return [t for t in range(unit, n + 1, unit) if n % t == 0]


MIN_MATMUL_STEPS = 8


def _matmul_tiles(m, n, k, a_bytes, b_bytes, o_bytes, m_unit):
    best = None
    for tm in _divisors(m, m_unit):
        for tn in _divisors(n, 128):
            for tk in _divisors(k, 128):
                acc = 0 if tk == k else tm * tn * 4
                vm = 2 * (tm * tk * a_bytes + tk * tn * b_bytes + tm * tn * o_bytes) + acc
                if vm > VMEM_BUDGET:
                    continue
                steps = (m // tm) * (n // tn) * (k // tk)
                score = (tk == k, min(steps, MIN_MATMUL_STEPS), min(tm, 256), tm * tn * tk)
                if best is None or score > best[0]:
                    best = (score, (tm, tn, tk))
    return best[1]


def _matmul(name, a, b, mode, out_dtype, bg=None):
    if mode == "nn":
        (m, k), n = a.shape, b.shape[1]
    elif mode == "nt":
        (m, k), n = a.shape, b.shape[0]
    else:
        (k, m), n = a.shape, b.shape[1]
    ab, bb, ob = a.dtype.itemsize, b.dtype.itemsize, jnp.dtype(out_dtype).itemsize
    tm, tn, tk = _matmul_tiles(m, n, k, ab, bb, ob, 128 if mode == "tn" else 16)
    nk = k // tk
    dot = {"nn": _dot, "nt": _dot_nt, "tn": _dot_tn}[mode]

    def body(a_ref, b_ref, o_ref, *scratch):
        prod = dot(a_ref[...].astype(BF16), b_ref[...].astype(BF16))
        if nk == 1:
            o_ref[...] = prod.astype(o_ref.dtype)
        else:
            acc_ref, = scratch
            kk = pl.program_id(2)

            @pl.when(kk == 0)
            def _():
                acc_ref[...] = prod

            @pl.when(kk > 0)
            def _():
                acc_ref[...] += prod

            @pl.when(kk == nk - 1)
            def _():
                o_ref[...] = acc_ref[...].astype(o_ref.dtype)

    a_spec = pl.BlockSpec((tk, tm), lambda i, j, kk: (kk, i)) if mode == "tn" else pl.BlockSpec((tm, tk), lambda i, j, kk: (i, kk))
    b_spec = pl.BlockSpec((tn, tk), lambda i, j, kk: (j, kk)) if mode == "nt" else pl.BlockSpec((tk, tn), lambda i, j, kk: (kk, j))
    res = _call(body, name, (m // tm, n // tn, nk), [a_spec, b_spec], [pl.BlockSpec((tm, tn), lambda i, j, kk: (i, j))],
                [jax.ShapeDtypeStruct((m, n), out_dtype)], [a, b],
                scratch_shapes=[] if nk == 1 else [pltpu.VMEM((tm, tn), F32)], bg=bg)
    return res[0] if bg is None else (res[0][0], res[1])


def _row_tile(n_rows, cap, unit=16):
    return max(t for t in _divisors(n_rows, unit) if t <= cap)


ROW_SUB = 384
GROUP_UNROLL = 4


def _rowwise(name, fn, n_rows, tm, row_ins, full_ins, row_outs, acc_outs, bg=None):
    n_in = len(row_ins) + len(full_ins)
    n_ro = len(row_outs)
    into = [(k, o[3]) for k, o in enumerate(row_outs) if len(o) > 2 and o[2] == "into"]

    n_row_in = len(row_ins)
    sub = min(tm, ROW_SUB)

    def body(*refs):
        i = pl.program_id(0)
        outs = refs[n_in + len(into):]

        def group(s, sums):
            rows = pl.ds(pl.multiple_of(s * sub, sub), sub)
            vals = [r[rows, :] for r in refs[:n_row_in]] + [r[...] for r in refs[n_row_in:n_in]]
            res = fn(i * tm + s * sub, *vals)
            for o, r, v in zip(row_outs, outs[:n_ro], res[:n_ro]):
                if len(o) > 2 and o[2] == "first":
                    @pl.when(i == 0)
                    def _(r=r, v=v):
                        r[rows, :] = v.astype(r.dtype)
                else:
                    r[rows, :] = v.astype(r.dtype)
            return tuple(a + v for a, v in zip(sums, res[n_ro:]))

        sums = lax.fori_loop(0, tm // sub, group, tuple(jnp.zeros((1, w), F32) for w in acc_outs), unroll=GROUP_UNROLL)

        @pl.when(i == 0)
        def _():
            for r, v in zip(outs[n_ro:], sums):
                r[...] = v

        @pl.when(i > 0)
        def _():
            for r, v in zip(outs[n_ro:], sums):
                r[...] += v

    def in_spec(entry):
        w, cb = entry[1], entry[2]
        if len(entry) > 3 and entry[3] == "prev":
            return pl.BlockSpec((tm, w), lambda i: (jnp.maximum(i - 1, 0), cb))
        if len(entry) > 3 and entry[3] == "first":
            return pl.BlockSpec((tm, w), lambda i: (0, cb))
        return pl.BlockSpec((tm, w), lambda i: (i, cb))

    def out_spec(o):
        if len(o) == 2:
            return pl.BlockSpec((tm, o[0]), lambda i: (i, 0)), jax.ShapeDtypeStruct((n_rows, o[0]), o[1])
        if o[2] == "new":
            return pl.BlockSpec((tm, o[0]), lambda i: (i, o[4])), jax.ShapeDtypeStruct((n_rows, o[3]), o[1])
        if o[2] == "into":
            return pl.BlockSpec((tm, o[0]), lambda i: (i, o[4])), jax.ShapeDtypeStruct(o[3].shape, o[3].dtype)
        if o[2] == "first":
            return pl.BlockSpec((tm, o[0]), lambda i: (0, 0)), jax.ShapeDtypeStruct((tm, o[0]), o[1])
        return pl.BlockSpec((tm, o[0]), lambda i: (jnp.maximum(i - 1, 0), 0)), jax.ShapeDtypeStruct((o[3], o[0]), o[1])

    in_specs = [in_spec(e) for e in row_ins]
    in_specs += [pl.BlockSpec(a.shape, lambda i: (0, 0)) for a in full_ins]
    in_specs += [pl.BlockSpec(memory_space=pl.ANY) for _ in into]
    specs_shapes = [out_spec(o) for o in row_outs]
    out_specs = [s for s, _ in specs_shapes] + [pl.BlockSpec((1, w), lambda i: (0, 0)) for w in acc_outs]
    out_shape = [s for _, s in specs_shapes] + [jax.ShapeDtypeStruct((1, w), F32) for w in acc_outs]
    return _call(body, name, (n_rows // tm,), in_specs, out_specs, out_shape,
                 [e[0] for e in row_ins] + list(full_ins) + [arr for _, arr in into],
                 aliases={n_in + a: k for a, (k, _) in enumerate(into)}, bg=bg)


def _valid_rows(first_row, tm, lo):
    return (first_row + _iota((tm, 1), 0)) >= lo


CONV_ROWS = 128
CONV_SUB = 16
CONV_LANES = 256


def _conv_specs(tm, width, blk, n_rows, after):
    specs = [pl.BlockSpec((tm, width), lambda i: (i, blk)),
             pl.BlockSpec((8, width), lambda i: (jnp.maximum(i * (tm // 8) - 1, 0), blk))]
    if after:
        specs.append(pl.BlockSpec((16, width), lambda i: (jnp.minimum((i + 1) * (tm // 16), n_rows // 16 - 1), blk)))
    return specs


def _conv_window(win, w_ref, b_ref, taps, c0, cw, n):
    acc = b_ref[:, c0:c0 + cw] + w_ref[taps - 1:taps, c0:c0 + cw] * win[8:8 + n]
    for k in range(taps - 1):
        acc = acc + w_ref[k:k + 1, c0:c0 + cw] * win[8 - (taps - 1) + k:8 - (taps - 1) + k + n]
    return acc


def _ffn_act(name, u_raw, conv_w, conv_b, n_rows):
    tm, sub, cw = CONV_ROWS, CONV_SUB, CONV_LANES
    taps, width = conv_w.shape
    half = width // 2

    def body(cur_ref, prev_ref, w_ref, b_ref, f_ref, ext_ref):
        i = pl.program_id(0)
        ext_ref[0:8, :] = jnp.where(i > 0, prev_ref[...], 0.0)
        ext_ref[8:8 + tm, :] = cur_ref[...]
        for q in range(half // cw):
            a0, g0 = q * cw, half + q * cw

            def group(s, carry):
                r = pl.multiple_of(s * sub, sub)
                a = _conv_window(ext_ref[pl.ds(r, sub + 8), a0:a0 + cw], w_ref, b_ref, taps, a0, cw, sub)
                g = _conv_window(ext_ref[pl.ds(r, sub + 8), g0:g0 + cw], w_ref, b_ref, taps, g0, cw, sub)
                f = jnp.where(_valid_rows(i * tm + r, sub, PAD), _silu(a) * g, 0.0)
                f_ref[pl.ds(r, sub), a0:a0 + cw] = f.astype(f_ref.dtype)
                return carry

            lax.fori_loop(0, tm // sub, group, 0, unroll=GROUP_UNROLL)

    return pl.pallas_call(
        body, name=name, grid=(n_rows // tm,),
        in_specs=_conv_specs(tm, width, 0, n_rows, False) + [pl.BlockSpec((taps, width), lambda i: (0, 0)),
                                                             pl.BlockSpec((1, width), lambda i: (0, 0))],
        out_specs=pl.BlockSpec((tm, half), lambda i: (i, 0)),
        out_shape=jax.ShapeDtypeStruct((n_rows, half), BF16),
        scratch_shapes=[pltpu.VMEM((tm + 8, width), F32)],
        compiler_params=_cparams(1),
    )(u_raw, u_raw, conv_w, conv_b)


def _conv_bwd(name, raw, raw_blk, dsrcs, chunk_src, conv_w, conv_b, n_rows, gated, into=None, into_blk=0, bg=None):
    taps, width = conv_w.shape
    half = width // 2 if gated else width
    tm, sub, cw = CONV_ROWS, CONV_SUB, CONV_LANES
    te = tm + 16
    nd = len(dsrcs)
    n_parts = 2 if gated else 1

    def body(*refs):
        cur_ref, prev_ref, next_ref = refs[0:3]
        dcur, dnext = refs[3:3 + nd], refs[3 + nd:3 + 2 * nd]
        w_ref, b_ref = refs[3 + 2 * nd:5 + 2 * nd]
        out_ref, acc_ref, ext_ref, du_ref = refs[-4:]
        i = pl.program_id(0)
        ext_ref[0:8, :] = jnp.where(i > 0, prev_ref[...], 0.0)
        ext_ref[8:8 + tm, :] = cur_ref[...]
        ext_ref[8 + tm:24 + tm, :] = next_ref[...]

        for q, (src, off) in enumerate(chunk_src):
            cols = [q * cw, half + q * cw][:n_parts]

            def conv_grad(r, d):
                pre = [_conv_window(ext_ref[pl.ds(r, sub + 8), c0:c0 + cw], w_ref, b_ref, taps, c0, cw, sub) for c0 in cols]
                row = i * tm + r + _iota((sub, 1), 0)
                live = (row >= PAD) & (row < n_rows)
                if gated:
                    act, dact = _silu_grad(pre[0])
                    dus = [d * pre[1] * dact, d * act]
                else:
                    dus = [d * _dsilu(pre[0])]
                for part, du in enumerate(dus):
                    du_ref[part, pl.ds(r, sub), :] = jnp.where(live, du, 0.0)

            def tile_rows(s, carry):
                r = pl.multiple_of(s * sub, sub)
                conv_grad(r, dcur[src][pl.ds(r, sub), off:off + cw].astype(F32))
                return carry

            lax.fori_loop(0, tm // sub, tile_rows, 0, unroll=GROUP_UNROLL)
            conv_grad(tm, dnext[src][:, off:off + cw].astype(F32))

            for part, c0 in enumerate(cols):
                taps_w = [w_ref[k:k + 1, c0:c0 + cw] for k in range(taps)]

                def back(s, sums):
                    new = list(sums)
                    for u in range(2):
                        r = pl.multiple_of((2 * s + u) * sub, sub)
                        win = du_ref[part, pl.ds(r, sub + 8), :]
                        raw_rows = ext_ref[pl.ds(8 + r, sub), c0:c0 + cw]
                        draw = jnp.zeros((sub, cw), F32)
                        for k in range(taps):
                            shifted = win[taps - 1 - k:taps - 1 - k + sub]
                            draw = draw + taps_w[k] * shifted
                            new[k] = new[k] + shifted * raw_rows
                        new[taps] = new[taps] + win[0:sub]
                        out_ref[pl.ds(r, sub), c0:c0 + cw] = jnp.where(_valid_rows(i * tm + r, sub, PAD), draw, 0.0).astype(out_ref.dtype)
                    return tuple(new)

                sums = lax.fori_loop(0, tm // (2 * sub), back, tuple(jnp.zeros((sub, cw), F32) for _ in range(taps + 1)))
                for k in range(taps + 1):
                    total = jnp.sum(sums[k], axis=0, keepdims=True)
                    acc_ref[k:k + 1, c0:c0 + cw] = jnp.where(i == 0, total, acc_ref[k:k + 1, c0:c0 + cw] + total)

    in_specs = _conv_specs(tm, width, raw_blk, n_rows, True)
    in_specs += [pl.BlockSpec((tm, d.shape[1]), lambda i: (i, 0)) for d in dsrcs]
    in_specs += [pl.BlockSpec((16, d.shape[1]), lambda i: (jnp.minimum((i + 1) * (tm // 16), n_rows // 16 - 1), 0)) for d in dsrcs]
    in_specs += [pl.BlockSpec((taps, width), lambda i: (0, 0)), pl.BlockSpec((1, width), lambda i: (0, 0))]
    operands = [raw, raw, raw] + list(dsrcs) + list(dsrcs) + [conv_w, conv_b]
    aliases = {}
    if into is None:
        out0 = jax.ShapeDtypeStruct((n_rows, width), BF16)
    else:
        in_specs.append(pl.BlockSpec(memory_space=pl.ANY))
        operands.append(into)
        aliases = {len(operands) - 1: 0}
        out0 = jax.ShapeDtypeStruct(into.shape, into.dtype)
    return _call(body, name, (n_rows // tm,), in_specs,
                 [pl.BlockSpec((tm, width), lambda i: (i, into_blk)), pl.BlockSpec((8, width), lambda i: (0, 0))],
                 [out0, jax.ShapeDtypeStruct((8, width), F32)], operands,
                 scratch_shapes=[pltpu.VMEM((tm + 24, width), F32), pltpu.VMEM((n_parts, te + 8, cw), F32)],
                 aliases=aliases, bg=bg)


def _ssd_specs(n_chunks, rev, per_step=1):
    cidx = (lambda c: n_chunks - 1 - c) if rev else (lambda c: c)
    xw, nw = per_step * GROUP_W, per_step * D_STATE
    xg0, bg0, cg0 = P_XBC // xw, (P_XBC + D_INNER) // nw, (P_XBC + D_INNER + SSM_GROUPS * D_STATE) // nw

    def cur(width, blk0):
        return pl.BlockSpec((T, width), lambda g, c: (cidx(c), blk0 + g))

    def prev(width, blk0):
        return pl.BlockSpec((8, width), lambda g, c: (jnp.maximum(cidx(c) * (T // 8) - 1, 0), blk0 + g))

    specs = [cur(xw, xg0), prev(xw, xg0), cur(nw, bg0), prev(nw, bg0), cur(nw, cg0), prev(nw, cg0),
             pl.BlockSpec((T, 128), lambda g, c: (cidx(c), P_DT // 128))]
    wb, wc = D_INNER // nw, (D_INNER + SSM_GROUPS * D_STATE) // nw
    specs += [pl.BlockSpec((4, xw), lambda g, c: (0, g)),
              pl.BlockSpec((4, nw), lambda g, c: (0, wb + g)),
              pl.BlockSpec((4, nw), lambda g, c: (0, wc + g)),
              pl.BlockSpec((1, xw), lambda g, c: (0, g)),
              pl.BlockSpec((1, nw), lambda g, c: (0, wb + g)),
              pl.BlockSpec((1, nw), lambda g, c: (0, wc + g))]
    specs += [pl.BlockSpec((1, 128), lambda g, c: (0, 0))] * 3
    return specs, cidx


def _ssd_chunk_forward(refs, ext_ref, g, c):
    (xc_ref, xp_ref, bc_ref, bp_ref, cc_ref, cp_ref, dt_ref, wx_ref, wb_ref, wc_ref,
     bx_ref, bb_ref, bcb_ref, dtb_ref, alog_ref, dsk_ref) = refs

    def conv_pre(cur_ref, prev_ref, w_ref, b_ref, width):
        ext_ref[0:8, 0:width] = jnp.where(c > 0, prev_ref[...], 0.0)
        ext_ref[8:8 + T, 0:width] = cur_ref[...]
        w = w_ref[...]
        acc = b_ref[...] + w[3:4] * cur_ref[...]
        for k in range(3):
            acc = acc + w[k:k + 1] * ext_ref[pl.ds(5 + k, T), 0:width]
        return acc

    valid = _valid_rows(c * T, T, PAD)
    v = {}
    v["valid"] = valid
    v["x_pre"] = conv_pre(xc_ref, xp_ref, wx_ref, bx_ref, GROUP_W)
    v["b_pre"] = conv_pre(bc_ref, bp_ref, wb_ref, bb_ref, D_STATE)
    v["c_pre"] = conv_pre(cc_ref, cp_ref, wc_ref, bcb_ref, D_STATE)
    xs = _silu(v["x_pre"])
    bm = jnp.where(valid, _silu(v["b_pre"]), 0.0)
    cm = jnp.where(valid, _silu(v["c_pre"]), 0.0)
    dtr = dt_ref[...] + dtb_ref[...]
    dt = jnp.where(valid, _softplus(dtr), 0.0)
    a_neg = -jnp.exp(alog_ref[...])
    a = dt * a_neg
    tril = _iota((T, T), 0) >= _iota((T, T), 1)
    cs = _xdot_l(tril.astype(BF16), a)
    hh, ll = _iota((128, GROUP_W), 0), _iota((128, GROUP_W), 1)
    expand = (hh == 8 * g + jnp.right_shift(ll, 6)).astype(BF16)
    sh, sj = _iota((128, 128), 0), _iota((128, 128), 1)
    select = ((sh == 8 * g + sj) & (sj < 8)).astype(BF16)
    hh_t, ll_t = _iota((GROUP_W, 128), 1), _iota((GROUP_W, 128), 0)
    v["expand_t"] = (hh_t == 8 * g + jnp.right_shift(ll_t, 6)).astype(BF16)
    v["select_t"] = ((sj == 8 * g + sh) & (sh < 8)).astype(BF16)
    cs_e = _xdot(cs, expand)
    dt_e = _xdot(dt, expand)
    cs_loc = _xdot(cs, select)
    cs_loc_t = cs_loc.T
    cs_last_e = cs_e[T - 1:T, :]
    v.update(xs=xs, bm=bm, cm=cm, dtr=dtr, dt=dt, a_neg=a_neg, tril=tril, expand=expand, select=select,
             cs_e=cs_e, dt_e=dt_e, cs_loc=cs_loc, cs_loc_t=cs_loc_t, cs_last_e=cs_last_e)
    v["xdt"] = xs * dt_e
    v["decay_e"] = jnp.exp(cs_last_e - cs_e)
    v["ecs_e"] = jnp.exp(cs_e)
    v["elast_e"] = jnp.exp(cs_last_e)
    v["d_e"] = _xdot(dsk_ref[...], expand)
    v["gmat"] = _dot_nt(cm.astype(BF16), bm.astype(BF16))
    return v


def _ssd_decay_pair(v, jp):
    out = []
    for j in (2 * jp, 2 * jp + 1):
        diff = v["cs_loc"][:, j:j + 1] - v["cs_loc_t"][j:j + 1, :]
        out.append(jnp.where(v["tril"], jnp.exp(jnp.where(v["tril"], diff, 0.0)), 0.0))
    return out


def _block_diag_pair(xp):
    lane = _iota(xp.shape, 1)
    return jnp.concatenate([jnp.where(lane < HEAD_P, xp, 0.0), jnp.where(lane >= HEAD_P, xp, 0.0)], axis=0)


SSD_GROUPS_PER_STEP = 4


def _ssd_group_refs(refs, gg):
    x_w, n_w = pl.ds(GROUP_W * gg, GROUP_W), pl.ds(D_STATE * gg, D_STATE)
    lanes = [x_w, x_w, n_w, n_w, n_w, n_w, None, x_w, n_w, n_w, x_w, n_w, n_w, None, None, None]
    return [r if w is None else r.at[:, w] for r, w in zip(refs, lanes)]


def _ssd_fwd(p, conv_w, conv_b, dt_bias, a_log, d_skip, n_chunks, bg=None):
    n_rows = n_chunks * T
    in_specs, _ = _ssd_specs(n_chunks, rev=False, per_step=SSD_GROUPS_PER_STEP)
    per = SSD_GROUPS_PER_STEP

    def body(*refs):
        y_ref, hin_ref, st_ref, ext_ref = refs[16:]
        g2, c = pl.program_id(0), pl.program_id(1)

        @pl.when(c == 0)
        def _():
            st_ref[...] = jnp.zeros_like(st_ref)

        for gg in range(per):
            v = _ssd_chunk_forward(_ssd_group_refs(refs[:16], gg), ext_ref.at[gg], per * g2 + gg, c)
            state = st_ref[gg]
            hin_ref[gg] = state
            ys = []
            for jp in range(4):
                l0, l1 = _ssd_decay_pair(v, jp)
                lhs = jnp.concatenate([v["gmat"] * l0, v["gmat"] * l1], axis=1).astype(BF16)
                rhs = _block_diag_pair(v["xdt"][:, 128 * jp:128 * jp + 128]).astype(BF16)
                ys.append(_dot(lhs, rhs))
            y = jnp.concatenate(ys, axis=1)
            y = y + _dot(v["cm"].astype(BF16), state.astype(BF16)) * v["ecs_e"] + v["xs"] * v["d_e"]
            y_ref[:, GROUP_W * gg:GROUP_W * gg + GROUP_W] = y
            s_new = _dot_tn(v["bm"].astype(BF16), (v["xdt"] * v["decay_e"]).astype(BF16))
            st_ref[gg] = state * v["elast_e"] + s_new

    return _call(
        body, "ssd_fwd", (SSM_GROUPS // per, n_chunks), in_specs,
        [pl.BlockSpec((T, per * GROUP_W), lambda g, c: (c, g)),
         pl.BlockSpec((per, None, D_STATE, GROUP_W), lambda g, c: (g, c, 0, 0))],
        [jax.ShapeDtypeStruct((n_rows, D_INNER), F32),
         jax.ShapeDtypeStruct((SSM_GROUPS, n_chunks, D_STATE, GROUP_W), F32)],
        [p, p, p, p, p, p, p, conv_w, conv_w, conv_w, conv_b, conv_b, conv_b, dt_bias, a_log, d_skip],
        scratch_shapes=[pltpu.VMEM((per, D_STATE, GROUP_W), F32), pltpu.VMEM((per, T + 8, GROUP_W), F32)], bg=bg)


def _ssd_bwd(p, conv_w, conv_b, dt_bias, a_log, d_skip, hin, dy, n_chunks, bg=None):
    n_rows = n_chunks * T
    per = SSD_GROUPS_PER_STEP
    in_specs, cidx = _ssd_specs(n_chunks, rev=True, per_step=per)
    in_specs = in_specs + [pl.BlockSpec((per, None, D_STATE, GROUP_W), lambda g, c: (g, cidx(c), 0, 0)),
                           pl.BlockSpec((T, per * GROUP_W), lambda g, c: (cidx(c), g))]

    def body(*refs):
        hin_ref, dy_ref = refs[16:18]
        dx_ref, db_ref, dc_ref, ddt_ref, dpar_ref, dst_ref, ext_ref = refs[18:]
        for gg in range(per):
            x_w, n_w = pl.ds(GROUP_W * gg, GROUP_W), pl.ds(D_STATE * gg, D_STATE)
            group_body(_ssd_group_refs(refs[:16], gg), hin_ref.at[gg], dy_ref.at[:, x_w], dx_ref.at[:, x_w],
                       db_ref.at[:, n_w], dc_ref.at[:, n_w], ddt_ref.at[:, n_w], dpar_ref.at[gg], dst_ref.at[gg],
                       ext_ref.at[gg], per * pl.program_id(0) + gg)

    def group_body(in_refs, hin_ref, dy_ref, dx_ref, db_ref, dc_ref, ddt_ref, dpar_ref, dst_ref, ext_ref, g):
        step = pl.program_id(1)
        c = n_chunks - 1 - step

        @pl.when(step == 0)
        def _():
            dst_ref[...] = jnp.zeros_like(dst_ref)

        v = _ssd_chunk_forward(in_refs, ext_ref, g, c)
        hin_f = hin_ref[...]
        hin_b = hin_f.astype(BF16)
        dyv = dy_ref[...]
        dst = dst_ref[...]
        dst_b = dst.astype(BF16)
        xs, bm, cm, xdt = v["xs"], v["bm"], v["cm"], v["xdt"]
        bm_b, cm_b = bm.astype(BF16), cm.astype(BF16)

        dd_e = jnp.sum(dyv * xs, axis=0, keepdims=True)
        dxs = dyv * v["d_e"]
        ch = _dot(cm_b, hin_b)
        dch = (dyv * v["ecs_e"]).astype(BF16)
        dcm = _dot_nt(dch, hin_b)
        dhin = _dot_tn(cm_b, dch) + dst * v["elast_e"]
        dcs_e = dyv * ch * v["ecs_e"]
        dxd = _dot(bm_b, dst_b)
        dbm = _dot_nt((xdt * v["decay_e"]).astype(BF16), dst_b)
        dxdt_state = dxd * v["decay_e"]
        q = dxdt_state * xdt
        dcs_e = dcs_e - q
        dlast_e = jnp.sum(q, axis=0, keepdims=True) + jnp.sum(dst * hin_f, axis=0, keepdims=True) * v["elast_e"]
        dg = jnp.zeros((T, T), F32)
        rs_cols = jnp.zeros((T, 128), F32)
        cs_rows = jnp.zeros((128, T), F32)
        lane_i, sub_i = _iota((T, 128), 1), _iota((128, T), 0)
        dxdt_parts = []
        for jp in range(4):
            l0, l1 = _ssd_decay_pair(v, jp)
            m0, m1 = v["gmat"] * l0, v["gmat"] * l1
            xbd = _block_diag_pair(xdt[:, 128 * jp:128 * jp + 128]).astype(BF16)
            dyp = dyv[:, 128 * jp:128 * jp + 128]
            dm = _dot_nt(dyp.astype(BF16), xbd)
            dm0, dm1 = dm[:, 0:T], dm[:, T:2 * T]
            dg = dg + dm0 * l0 + dm1 * l1
            for j, qq in ((2 * jp, dm0 * m0), (2 * jp + 1, dm1 * m1)):
                rs_cols = jnp.where(lane_i == j, jnp.sum(qq, axis=1, keepdims=True), rs_cols)
                cs_rows = jnp.where(sub_i == j, jnp.sum(qq, axis=0, keepdims=True), cs_rows)
            mv = jnp.concatenate([m0, m1], axis=0).astype(BF16)
            dxdt_parts.append(_dot_tn(mv, _block_diag_pair(dyp).astype(BF16)))
        dxdt = jnp.concatenate(dxdt_parts, axis=1) + dxdt_state
        dg_b = dg.astype(BF16)
        dcm = dcm + _dot(dg_b, bm_b)
        dbm = dbm + _dot_tn(dg_b, cm_b)
        expand_t = v["expand_t"]
        dcs_loc = rs_cols - cs_rows.T
        last_row = _iota((T, 1), 0) == T - 1
        dcs_full_e = dcs_e + jnp.where(last_row, dlast_e, 0.0)
        dcs = _xdot(dcs_full_e, expand_t) + _xdot(dcs_loc, v["select_t"])
        triu = (_iota((T, T), 0) <= _iota((T, T), 1)).astype(BF16)
        da = _xdot_l(triu, dcs)
        ddt = da * v["a_neg"] + _xdot(dxdt * xs, expand_t)
        dxs = dxs + dxdt * v["dt_e"]
        ddtr = jnp.where(v["valid"], ddt * _sigmoid(v["dtr"]), 0.0)
        dx_ref[...] = dxs
        db_ref[...] = jnp.where(v["valid"], dbm, 0.0)
        dc_ref[...] = jnp.where(v["valid"], dcm, 0.0)
        ddt_ref[...] = ddtr
        dpar = jnp.concatenate([
            jnp.sum(ddtr, axis=0, keepdims=True),
            jnp.sum(da * v["dt"], axis=0, keepdims=True) * v["a_neg"],
            _xdot(dd_e, expand_t),
            jnp.zeros((5, 128), F32)], axis=0)

        @pl.when(step == 0)
        def _():
            dpar_ref[...] = dpar

        @pl.when(step > 0)
        def _():
            dpar_ref[...] += dpar

        dst_ref[...] = dhin

    return _call(
        body, "ssd_bwd", (SSM_GROUPS // per, n_chunks), in_specs,
        [pl.BlockSpec((T, per * GROUP_W), lambda g, c: (cidx(c), g)),
         pl.BlockSpec((T, per * D_STATE), lambda g, c: (cidx(c), g)),
         pl.BlockSpec((T, per * D_STATE), lambda g, c: (cidx(c), g)),
         pl.BlockSpec((T, per * 128), lambda g, c: (cidx(c), g)),
         pl.BlockSpec((per, 8, 128), lambda g, c: (g, 0, 0))],
        [jax.ShapeDtypeStruct((n_rows, D_INNER), F32),
         jax.ShapeDtypeStruct((n_rows, SSM_GROUPS * D_STATE), F32),
         jax.ShapeDtypeStruct((n_rows, SSM_GROUPS * D_STATE), F32),
         jax.ShapeDtypeStruct((n_rows, SSM_GROUPS * 128), F32),
         jax.ShapeDtypeStruct((SSM_GROUPS, 8, 128), F32)],
        [p, p, p, p, p, p, p, conv_w, conv_w, conv_w, conv_b, conv_b, conv_b, dt_bias, a_log, d_skip, hin, dy],
        scratch_shapes=[pltpu.VMEM((per, D_STATE, GROUP_W), F32), pltpu.VMEM((per, T + 8, GROUP_W), F32)], bg=bg)


def _alibi_slope(h):
    return 2.0 ** (-8.0 * (h + 1) / ATTN_HEADS)


def _dup_half(x256, kvh):
    xb = x256[:, 128 * (kvh // 2):128 * (kvh // 2) + 128]
    rolled = pltpu.roll(xb, 64, 1)
    lane = _iota(xb.shape, 1)
    if kvh % 2 == 0:
        return jnp.where(lane < 64, xb, rolled)
    return jnp.where(lane < 64, rolled, xb)


def _attn_masks(c):
    qi, j = _iota((T, T), 0), _iota((T, T), 1)
    tri = j <= qi
    meta_ok = (j >= PAD) & (j - PAD <= c * T + qi - PAD)
    band_ok = c >= jnp.where(tri, 1, 2)
    dist = jnp.bitwise_and(qi - j, T - 1).astype(F32)
    return tri, meta_ok, band_ok, dist


def _fold(x3, tri):
    return jnp.concatenate([x3[:, 0:T], jnp.where(tri, x3[:, 2 * T:3 * T], x3[:, T:2 * T])], axis=1)


def _unfold(x2, tri):
    band = x2[:, T:2 * T]
    return jnp.concatenate([x2[:, 0:T], jnp.where(tri, 0.0, band), jnp.where(tri, band, 0.0)], axis=1)


def _attn_scores(qp, k3, masks, h0):
    tri, meta_ok, band_ok, dist = masks
    lane = _iota(qp.shape, 1)
    s = []
    for half, h in ((0, h0), (1, h0 + 1)):
        qh = jnp.where((lane < 64) if half == 0 else (lane >= 64), qp, 0.0).astype(BF16)
        raw = _dot_nt(qh, k3)
        band = jnp.where(tri, raw[:, 2 * T:3 * T], raw[:, T:2 * T]) - _alibi_slope(h) * dist
        s.append((qh, jnp.concatenate([jnp.where(meta_ok, raw[:, 0:T], NEG), jnp.where(band_ok, band, NEG)], axis=1)))
    return s


def _attn_fwd(p, sinks, n_chunks, bg=None):
    n_rows = n_chunks * T
    kb, vb = P_K // KV_W, P_V // KV_W

    def body(q_ref, kc_ref, kp_ref, km_ref, vc_ref, vp_ref, vm_ref, sink_ref, o_ref, lse_ref):
        c = pl.program_id(0)
        sinks_v = sink_ref[...]
        masks = _attn_masks(c)
        tri, meta_ok, band_ok, dist = masks
        lane = _iota((T, 128), 1)
        for kvh in range(KV_HEADS):
            k3 = jnp.concatenate([_dup_half(r[...], kvh) for r in (km_ref, kp_ref, kc_ref)], axis=0).astype(BF16)
            v3 = jnp.concatenate([_dup_half(r[...], kvh) for r in (vm_ref, vp_ref, vc_ref)], axis=0)
            v3bd = _block_diag_rows(v3).astype(BF16)
            q2 = q_ref[:, 256 * kvh:256 * kvh + 256] * SCALE
            q4 = jnp.concatenate([jnp.where((lane < 64) if half == 0 else (lane >= 64), q2[:, 128 * pr:128 * pr + 128], 0.0)
                                  for pr in range(2) for half in range(2)], axis=0).astype(BF16)
            raw4 = _dot_nt(q4, k3)
            probs = []
            for hh in range(4):
                h = 4 * kvh + hh
                raw = raw4[T * hh:T * hh + T]
                band = jnp.where(tri, raw[:, 2 * T:3 * T], raw[:, T:2 * T]) - _alibi_slope(h) * dist
                sc = jnp.concatenate([jnp.where(meta_ok, raw[:, 0:T], NEG), jnp.where(band_ok, band, NEG)], axis=1)
                sink = sinks_v[:, h:h + 1]
                m = jnp.maximum(jnp.max(sc, axis=1, keepdims=True), sink)
                e = jnp.exp(sc - m)
                den = jnp.sum(e, axis=1, keepdims=True) + jnp.exp(sink - m)
                probs.append(_unfold(e * (1.0 / den), tri))
                lse_ref[:, h:h + 1] = m + jnp.log(den)
            p4 = jnp.concatenate([jnp.concatenate(probs[0:2], axis=1), jnp.concatenate(probs[2:4], axis=1)], axis=0)
            out = _dot(p4.astype(BF16), v3bd)
            o_ref[:, 256 * kvh:256 * kvh + 256] = jnp.concatenate([out[0:T], out[T:2 * T]], axis=1).astype(o_ref.dtype)

    blk = lambda width, col: pl.BlockSpec((T, width), lambda c: (c, col))
    prev = lambda width, col: pl.BlockSpec((T, width), lambda c: (jnp.maximum(c - 1, 0), col))
    first = lambda width, col: pl.BlockSpec((T, width), lambda c: (0, col))
    return _call(
        body, "attn_fwd", (n_chunks,),
        [blk(ATTN_W, P_Q // ATTN_W), blk(KV_W, kb), prev(KV_W, kb), first(KV_W, kb),
         blk(KV_W, vb), prev(KV_W, vb), first(KV_W, vb), pl.BlockSpec((1, 128), lambda c: (0, 0))],
        [pl.BlockSpec((T, ATTN_W), lambda c: (c, 0)), pl.BlockSpec((T, 128), lambda c: (c, 0))],
        [jax.ShapeDtypeStruct((n_rows, ATTN_W), BF16), jax.ShapeDtypeStruct((n_rows, 128), F32)],
        [p, p, p, p, p, p, p, sinks], bg=bg)


def _block_diag_rows(x3):
    lane = _iota(x3.shape, 1)
    return jnp.concatenate([jnp.where(lane < 64, x3, 0.0), jnp.where(lane >= 64, x3, 0.0)], axis=0)


def _fold_halves(x):
    return x + pltpu.roll(x, 64, 1)


def _attn_bwd(p, sinks, ao, lse, dao, dp, n_chunks, bg=None):
    kb, vb = P_K // KV_W, P_V // KV_W
    rc = lambda s: n_chunks - 1 - s

    def body(q_ref, kc_ref, kp_ref, km_ref, vc_ref, vp_ref, vm_ref, sink_ref, o_ref, lse_ref, do_ref, dp_in_ref,
             dqkv_ref, dsink_ref, kcar_ref, vcar_ref, kmeta_ref, vmeta_ref):
        step = pl.program_id(0)
        c = n_chunks - 1 - step

        @pl.when(step == 0)
        def _():
            for r in (kcar_ref, vcar_ref, kmeta_ref, vmeta_ref):
                r[...] = jnp.zeros_like(r)

        masks = _attn_masks(c)
        tri = masks[0]
        q = q_ref[...] * SCALE
        sinks_v = sink_ref[...]
        lse_v = lse_ref[...]
        ov = o_ref[...].astype(F32)
        dov = do_ref[...].astype(F32)
        lane = _iota((T, 128), 1)
        lane256 = _iota((3 * T, KV_W), 1)
        dsink = jnp.zeros((1, 128), F32)
        dk3_all = jnp.zeros((3 * T, KV_W), F32)
        dv3_all = jnp.zeros((3 * T, KV_W), F32)
        dqs = []
        for kvh in range(KV_HEADS):
            k3 = jnp.concatenate([_dup_half(r[...], kvh) for r in (km_ref, kp_ref, kc_ref)], axis=0).astype(BF16)
            v3 = jnp.concatenate([_dup_half(r[...], kvh) for r in (vm_ref, vp_ref, vc_ref)], axis=0).astype(BF16)
            dk3 = jnp.zeros((3 * T, 128), F32)
            dv3 = jnp.zeros((3 * T, 128), F32)
            for pr in range(2):
                h0 = 4 * kvh + 2 * pr
                blk = 2 * kvh + pr
                qp = q[:, 128 * blk:128 * blk + 128]
                dop = dov[:, 128 * blk:128 * blk + 128]
                prod = dop * ov[:, 128 * blk:128 * blk + 128]
                dq_pair = jnp.zeros((T, 128), F32)
                for half, ((qh, sc), h) in enumerate(zip(_attn_scores(qp, k3, masks, h0), (h0, h0 + 1))):
                    mine = (lane < 64) if half == 0 else (lane >= 64)
                    lse_h = lse_v[:, h:h + 1]
                    pm = jnp.exp(sc - lse_h)
                    doh = jnp.where(mine, dop, 0.0).astype(BF16)
                    delta = jnp.sum(jnp.where(mine, prod, 0.0), axis=1, keepdims=True)
                    dp = _fold(_dot_nt(doh, v3), tri)
                    ds = _unfold(pm * (dp - delta), tri).astype(BF16)
                    p_sink = jnp.exp(sinks_v[:, h:h + 1] - lse_h)
                    dsink = jnp.where(_iota((1, 128), 1) == h, jnp.sum(-p_sink * delta, axis=0, keepdims=True), dsink)
                    dq_pair = jnp.where(mine, _dot(ds, k3), dq_pair)
                    dk3 = dk3 + _dot_tn(ds, qh)
                    dv3 = dv3 + _dot_tn(_unfold(pm, tri).astype(BF16), doh)
                dqs.append(dq_pair * SCALE)
            in_place = (lane256 >= 64 * kvh) & (lane256 < 64 * kvh + 64)
            wide = lambda x: jnp.concatenate([x, x], axis=1)
            dk3_all = jnp.where(in_place, wide(_fold_halves(dk3)), dk3_all)
            dv3_all = jnp.where(in_place, wide(_fold_halves(dv3)), dv3_all)
        dsink_all = dsink

        @pl.when(step == 0)
        def _():
            dsink_ref[...] = dsink_all

        @pl.when(step > 0)
        def _():
            dsink_ref[...] += dsink_all

        kmeta = kmeta_ref[...] + dk3_all[0:T]
        vmeta = vmeta_ref[...] + dv3_all[0:T]
        kmeta_ref[...] = kmeta
        vmeta_ref[...] = vmeta
        is_first = c == 0
        dk = jnp.where(is_first, kmeta, dk3_all[2 * T:3 * T] + kcar_ref[...])
        dv = jnp.where(is_first, vmeta, dv3_all[2 * T:3 * T] + vcar_ref[...])
        dqkv_ref[...] = jnp.concatenate(dqs + [dk, dv], axis=1).astype(dqkv_ref.dtype)
        kcar_ref[...] = dk3_all[T:2 * T]
        vcar_ref[...] = dv3_all[T:2 * T]

    blk = lambda width, col: pl.BlockSpec((T, width), lambda s: (rc(s), col))
    prev = lambda width, col: pl.BlockSpec((T, width), lambda s: (jnp.maximum(rc(s) - 1, 0), col))
    first = lambda width, col: pl.BlockSpec((T, width), lambda s: (0, col))
    return _call(
        body, "attn_bwd", (n_chunks,),
        [blk(ATTN_W, P_Q // ATTN_W), blk(KV_W, kb), prev(KV_W, kb), first(KV_W, kb),
         blk(KV_W, vb), prev(KV_W, vb), first(KV_W, vb), pl.BlockSpec((1, 128), lambda s: (0, 0)),
         blk(ATTN_W, 0), blk(128, 0), blk(ATTN_W, 0), ANY],
        [blk(QKV_W, P_Q // QKV_W), pl.BlockSpec((1, 128), lambda s: (0, 0))],
        [jax.ShapeDtypeStruct(dp.shape, dp.dtype), jax.ShapeDtypeStruct((1, 128), F32)],
        [p, p, p, p, p, p, p, sinks, ao, lse, dao, dp],
        scratch_shapes=[pltpu.VMEM((T, KV_W), F32)] * 4, aliases={11: 0}, bg=bg)


def _pad_lanes(v, width=128):
    return jnp.pad(v, ((0, 0), (0, width - v.shape[1])))


def _local_step(x, head, tgt, plan):
    w, g, run = plan.w, plan.g, plan.run
    n_tok = x.shape[0]
    n_rows = n_tok + T
    n_chunks = n_rows // T
    tm = _row_tile(n_rows, 384)
    dt_bias, a_log, d_skip = (_pad_lanes(w[k]) for k in ("ssm_dt_bias", "ssm_a_log", "ssm_d_skip"))
    sinks = _pad_lanes(w["attn_sinks"])
    x_in = [(x, D_MODEL, 0, "prev"), (head, D_MODEL, 0, "first")]

    def h0_tile(r0, xt, hd):
        return jnp.where(r0 < T, hd, xt)

    n1, = _rowwise("norm_pre_mix", lambda r0, xt, hd, wn: [_rms(h0_tile(r0, xt, hd), wn)], n_rows, T,
                   x_in, [w["norm_pre_mix"]], [(D_MODEL, BF16)], [])
    p = _matmul("in_proj", n1, w["w_cat"], "nn", F32)
    y_ssd, hin = run("ssd_fwd", _ssd_fwd, p, w["ssm_conv_w"], w["ssm_conv_b"], dt_bias, a_log, d_skip, n_chunks)
    ao, lse = run("attn_fwd", _attn_fwd, p, sinks, n_chunks)

    def gate_norm(r0, y, z, wn):
        return [_rms(y * _silu(z), wn)]

    yn, = _rowwise("ssm_gate_norm", gate_norm, n_rows, tm, [(y_ssd, D_INNER, 0), (p, D_INNER, P_Z // D_INNER)],
                   [w["ssm_norm"]], [(D_INNER, BF16)], [])
    y_ssm = _matmul("ssm_out", yn, w["w_ssm_out"], "nn", F32)
    y_attn = _matmul("attn_out", ao, w["w_attn_out"], "nn", F32)

    def mix_gate(r0, ys, ya, gs, ga):
        return [_sigmoid(gs) * ys + _sigmoid(ga) * ya]

    gate_ins = [(p, D_MODEL, P_GATE // D_MODEL), (p, D_MODEL, P_GATE // D_MODEL + 1)]
    mixed, = _rowwise("mix_gate", mix_gate, n_rows, tm, [(y_ssm, D_MODEL, 0), (y_attn, D_MODEL, 0)] + gate_ins,
                      [], [(D_MODEL, BF16)], [])
    mix = _matmul("mix_out", mixed, w["w_mix_out"], "nn", F32)

    def post_mix(r0, mx, xt, hd, w_post, w_pre):
        h1 = jnp.where(_valid_rows(r0, mx.shape[0], PAD), h0_tile(r0, xt, hd) + _rms(mx, w_post), 0.0)
        return [h1, _rms(h1, w_pre)]

    h1, n2 = _rowwise("post_mix", post_mix, n_rows, T, [(mix, D_MODEL, 0)] + x_in,
                      [w["norm_post_mix"], w["norm_pre_ffn"]], [(D_MODEL, F32), (D_MODEL, BF16)], [])
    u_raw = _matmul("ffn_up", n2, w["w_ffn_up"], "nn", F32)
    f = _ffn_act("ffn_act", u_raw, w["ffn_conv_w"], w["ffn_conv_b"], n_rows)
    ffn = _matmul("ffn_down", f, w["w_ffn_down"], "nn", F32)

    def final(r0, fo, h, t, w_post):
        real = r0 >= T
        err = jnp.where(real, h + _rms(fo, w_post) - t, 0.0)
        dy = err * (1.0 / D_MODEL)
        dffn, dw = _rms_bwd(dy, fo, w_post)
        return [dffn, dy, jnp.sum(err * err, axis=0, keepdims=True), dw]

    dffn, dh2, loss_cols, g_norm_post_ffn = _rowwise(
        "loss_head", final, n_rows, T, [(ffn, D_MODEL, 0), (h1, D_MODEL, 0), (tgt, D_MODEL, 0, "prev")],
        [w["norm_post_ffn"]], [(D_MODEL, BF16), (D_MODEL, F32)], [D_MODEL, D_MODEL])

    g["norm_post_ffn"] = g_norm_post_ffn
    g["w_ffn_down"] = _matmul("ffn_down_dw", f, dffn, "tn", F32)
    df = _matmul("ffn_down_dx", dffn, w["w_ffn_down"], "nt", F32)
    du_raw, dconv = _conv_bwd("ffn_act_bwd", u_raw, 0, [df], [(0, c0) for c0 in range(0, FFN_DIM, CONV_LANES)],
                              w["ffn_conv_w"], w["ffn_conv_b"], n_rows, True)
    g["ffn_conv_w"], g["ffn_conv_b"] = dconv[0:3], dconv[3:4]
    g["w_ffn_up"] = _matmul("ffn_up_dw", n2, du_raw, "tn", F32)
    dn2 = run("ffn_up_dx", _matmul, "ffn_up_dx", du_raw, w["w_ffn_up"], "nt", F32)

    def post_mix_bwd(r0, dn, d2, h, mx, w_pre, w_post):
        dx, dw_pre = _rms_bwd(dn, h, w_pre)
        dh1 = jnp.where(_valid_rows(r0, dn.shape[0], PAD), dx + d2, 0.0)
        dmix, dw_post = _rms_bwd(dh1, mx, w_post)
        return [dh1, dmix, dw_pre, dw_post]

    dh1, dmix, g["norm_pre_ffn"], g["norm_post_mix"] = _rowwise(
        "post_mix_bwd", post_mix_bwd, n_rows, tm,
        [(dn2, D_MODEL, 0), (dh2, D_MODEL, 0), (h1, D_MODEL, 0), (mix, D_MODEL, 0)],
        [w["norm_pre_ffn"], w["norm_post_mix"]], [(D_MODEL, F32), (D_MODEL, BF16)], [D_MODEL, D_MODEL])
    g["w_mix_out"] = _matmul("mix_out_dw", mixed, dmix, "tn", F32)
    dmixed = _matmul("mix_out_dx", dmix, w["w_mix_out"], "nt", F32)

    def mix_gate_bwd(r0, dm, ys, ya, gs, ga):
        ss, sa = _sigmoid(gs), _sigmoid(ga)
        dgate = jnp.concatenate([dm * ys * ss * (1.0 - ss), dm * ya * sa * (1.0 - sa)], axis=1)
        return [dm * ss, dm * sa, dgate]

    dys, dya, dp = _rowwise(
        "mix_gate_bwd", mix_gate_bwd, n_rows, tm,
        [(dmixed, D_MODEL, 0), (y_ssm, D_MODEL, 0), (y_attn, D_MODEL, 0)] + gate_ins,
        [], [(D_MODEL, BF16), (D_MODEL, BF16), (2 * D_MODEL, BF16, "new", P_W, P_GATE // (2 * D_MODEL))], [])
    g["w_ssm_out"] = _matmul("ssm_out_dw", yn, dys, "tn", F32)
    dyn = _matmul("ssm_out_dx", dys, w["w_ssm_out"], "nt", F32)
    g["w_attn_out"] = _matmul("attn_out_dw", ao, dya, "tn", F32)
    dao = _matmul("attn_out_dx", dya, w["w_attn_out"], "nt", BF16)

    def gate_norm_bwd(r0, dn, y, z, wn):
        sz, dsz = _silu_grad(z)
        dyz, dw = _rms_bwd(dn, y * sz, wn)
        live = _valid_rows(r0, dn.shape[0], PAD)
        return [jnp.where(live, dyz * sz, 0.0), jnp.where(live, dyz * y * dsz, 0.0), dw]

    dy_ssd, dp, g["ssm_norm"] = run(
        "ssm_gate_norm_bwd", _rowwise, "ssm_gate_norm_bwd", gate_norm_bwd, n_rows, tm,
        [(dyn, D_INNER, 0), (y_ssd, D_INNER, 0), (p, D_INNER, P_Z // D_INNER)],
        [w["ssm_norm"]], [(D_INNER, F32), (D_INNER, BF16, "into", dp, P_Z // D_INNER)], [D_INNER])
    dp, dsink = run("attn_bwd", _attn_bwd, p, sinks, ao, lse, dao, dp, n_chunks)
    g["attn_sinks"] = dsink[:, 0:ATTN_HEADS]
    dxs, dbm, dcm, ddt_parts, dpar = run("ssd_bwd", _ssd_bwd, p, w["ssm_conv_w"], w["ssm_conv_b"], dt_bias, a_log,
                                         d_skip, hin, dy_ssd, n_chunks)
    dpar = jnp.sum(dpar, axis=0)
    g["ssm_dt_bias"], g["ssm_a_log"], g["ssm_d_skip"] = (dpar[i:i + 1, 0:SSM_HEADS] for i in range(3))

    def dt_grad(r0, parts):
        tot = parts[:, 0:128] + parts[:, 128:256] + parts[:, 256:384] + parts[:, 384:512]
        return [jnp.concatenate([tot, jnp.zeros((parts.shape[0], P_Z - P_DT - 128), F32)], axis=1)]

    dt_w = P_Z - P_DT
    dp, = _rowwise("dt_grad", dt_grad, n_rows, tm, [(ddt_parts, SSM_GROUPS * 128, 0)], [],
                   [(dt_w, BF16, "into", dp, P_DT // dt_w)], [])
    x_chunks = [(src, c0) for src, arr in enumerate((dxs, dbm, dcm)) for c0 in range(0, arr.shape[1], CONV_LANES)]
    dp, dconv = run("ssm_conv_bwd", _conv_bwd, "ssm_conv_bwd", p, P_XBC // CONV_DIM, [dxs, dbm, dcm], x_chunks,
                    w["ssm_conv_w"], w["ssm_conv_b"], n_rows, False, into=dp, into_blk=P_XBC // CONV_DIM)
    g["ssm_conv_w"], g["ssm_conv_b"] = dconv[0:4], dconv[4:5]
    g["w_cat"] = _matmul("in_proj_dw", n1, dp, "tn", F32)
    dn1 = run("in_proj_dx", _matmul, "in_proj_dx", dp, w["w_cat"], "nt", F32)

    def pre_mix_bwd(r0, dn, d1, xt, hd, wn):
        dx, dw = _rms_bwd(dn, h0_tile(r0, xt, hd), wn)
        dh0 = jnp.where(_valid_rows(r0, dn.shape[0], PAD), dx + d1, 0.0)
        return [dh0, dh0, dw]

    dx_out, dhead, g["norm_pre_mix"] = _rowwise(
        "pre_mix_bwd", pre_mix_bwd, n_rows, T, [(dn1, D_MODEL, 0), (dh1, D_MODEL, 0)] + x_in,
        [w["norm_pre_mix"]], [(D_MODEL, F32, "prev", n_tok), (D_MODEL, F32, "first")], [D_MODEL])
    return jnp.sum(loss_cols), dx_out, dhead


_IN_SECTIONS = [((5152, 6176), P_Q), ((6176, 6432), P_K), ((6432, 6688), P_V), ((5120, 5152), P_DT),
                ((0, 2048), P_Z), ((6688, 8736), P_GATE), ((2048, 5120), P_XBC)]


IN_SHARD = N_IN // 4


def _shard_pieces(a, b):
    return [(j, max(a, j * IN_SHARD) - j * IN_SHARD, min(b, (j + 1) * IN_SHARD) - j * IN_SHARD)
            for j in range(4) if max(a, j * IN_SHARD) < min(b, (j + 1) * IN_SHARD)]


def _to_cat(w4):
    parts, at = [], 0
    for (a, b), off in _IN_SECTIONS:
        if off > at:
            parts.append(jnp.zeros((w4.shape[1], off - at), w4.dtype))
        parts += [w4[j, :, lo:hi] for j, lo, hi in _shard_pieces(a, b)]
        at = off + (b - a)
    return jnp.concatenate(parts, axis=1)


def _from_cat(g_cat):
    shards = [[] for _ in range(4)]
    for (a, b), off in sorted(_IN_SECTIONS):
        for j, lo, hi in _shard_pieces(a, b):
            start = off + j * IN_SHARD + lo - a
            shards[j].append(g_cat[:, start:start + hi - lo])
    return jnp.stack([jnp.concatenate(s, axis=1) for s in shards])


LANES = 1024
_BIG = [("w_in", 1024, 2184, "chip"), ("w_ssm_out", 512, 1024, "row"), ("w_attn_out", 256, 1024, "row"),
        ("w_mix_out", 256, 1024, "row"), ("w_ffn_up", 1024, 1408, "col"), ("w_ffn_down", 704, 1024, "row"),
        ("small", 32, LANES, "chip")]
_SMALL_SHARDED = [("ssm_conv_w", (4, 768), 1), ("ffn_conv_w", (3, 1408), 1), ("meta_tokens", (16, 256), 1)]
_REPLICATED = [("norm_pre_mix", 1024), ("ssm_conv_b", 3072), ("ssm_dt_bias", 32), ("ssm_a_log", 32),
               ("ssm_d_skip", 32), ("ssm_norm", 2048), ("attn_sinks", 16), ("norm_post_mix", 1024),
               ("norm_pre_ffn", 1024), ("ffn_conv_b", 5632), ("norm_post_ffn", 1024)]
SMALL_ROWS = 16
WEIGHT_ORDER = ["meta_tokens", "norm_pre_mix", "w_in", "ssm_conv_w", "ssm_conv_b", "ssm_dt_bias", "ssm_a_log",
                "ssm_d_skip", "ssm_norm", "w_ssm_out", "attn_sinks", "w_attn_out", "w_mix_out", "norm_post_mix",
                "norm_pre_ffn", "w_ffn_up", "ffn_conv_w", "ffn_conv_b", "w_ffn_down", "norm_post_ffn"]


def _flatten(parts, rows):
    flat = jnp.concatenate([a.reshape(-1) for a in parts])
    return jnp.pad(flat, (0, rows * LANES - flat.shape[0])).reshape(rows, LANES)


def _unflatten(flat, shapes):
    flat = flat.reshape(-1)
    out, off = [], 0
    for shp in shapes:
        n = math.prod(shp)
        out.append(flat[off:off + n].reshape(shp))
        off += n
    return out


def _shard_of(full, chip, shape, axis):
    return lax.slice_in_dim(full, chip * shape[axis], (chip + 1) * shape[axis], axis=axis)


def _full_shape(r, c, layout):
    return {"row": (4 * r, c), "col": (r, 4 * c), "chip": (4, r, c)}[layout]


def _shard_view(ref, r, c, layout, chip):
    if layout == "row":
        return ref.at[pl.ds(pl.multiple_of(chip * r, 16), r), :]
    if layout == "col":
        return ref.at[:, pl.ds(pl.multiple_of(chip * c, 128), c)]
    return ref.at[chip]


def _half_view(ref, r, c, layout, chip, half):
    hr = r // 2
    if layout == "row":
        return ref.at[pl.ds(pl.multiple_of(chip * r + half * hr, 16), hr), :]
    r0 = pl.multiple_of(half * hr, 16)
    if layout == "col":
        return ref.at[pl.ds(r0, hr), pl.ds(pl.multiple_of(chip * c, 128), c)]
    return ref.at[chip, pl.ds(r0, hr), :]


def _mesh_pos():
    return lax.axis_index("x"), lax.axis_index("y"), lax.axis_index("c")


def _other_chips(x, y):
    return [(1 - x, y), (x, 1 - y), (1 - x, 1 - y)]


def _chip_index(x, y):
    return 2 * x + y


def _run_exchange(name, ex):
    n_in, n_out = len(ex.ins), len(ex.out_shapes)

    def body(*refs):
        in_refs, out_refs = refs[:n_in], refs[n_in:n_in + n_out]
        send_sems, recv_sems = refs[n_in + n_out:]
        copies = [pltpu.make_async_remote_copy(src_ref=s, dst_ref=d, send_sem=send_sems.at[i], recv_sem=recv_sems.at[i],
                                               device_id=dev, device_id_type=MESH)
                  for i, (s, d, dev) in enumerate(ex.make_copies(in_refs, out_refs))]
        assert len(copies) == ex.n_copies
        for cp in copies:
            cp.start()
        for cp in copies:
            cp.wait()

    return pl.pallas_call(
        body, name=name, in_specs=[ANY] * n_in, out_specs=[ANY] * n_out, out_shape=list(ex.out_shapes),
        scratch_shapes=[pltpu.SemaphoreType.DMA((ex.n_copies,)), pltpu.SemaphoreType.DMA((ex.n_copies,))],
        compiler_params=pltpu.CompilerParams(has_side_effects=True),
    )(*ex.ins)


def _join(*exs):
    def make(in_refs, out_refs):
        copies, i0, o0 = [], 0, 0
        for ex in exs:
            copies += ex.make_copies(in_refs[i0:i0 + len(ex.ins)], out_refs[o0:o0 + len(ex.out_shapes)])
            i0, o0 = i0 + len(ex.ins), o0 + len(ex.out_shapes)
        return copies

    return _Exchange([a for ex in exs for a in ex.ins], [s for ex in exs for s in ex.out_shapes], make,
                     sum(ex.n_copies for ex in exs))


def _split(exs, results):
    out, o0 = [], 0
    for ex in exs:
        out.append(list(results[o0:o0 + len(ex.out_shapes)]))
        o0 += len(ex.out_shapes)
    return out


def _gather_ici(entries, shards):
    def make(in_refs, out_refs):
        x, y, c = _mesh_pos()
        j = _chip_index(x, y)
        copies = []
        for ref_in, ref_out, (_, r, cc, lay) in zip(in_refs, out_refs, entries):
            copies.append((ref_in, _shard_view(ref_out, r, cc, lay, j), None))
            mine = ref_in.at[pl.ds(pl.multiple_of(c * (r // 2), 16), r // 2), :]
            copies += [(mine, _half_view(ref_out, r, cc, lay, j, c), (*ch, c)) for ch in _other_chips(x, y)]
        return copies

    shapes = [jax.ShapeDtypeStruct(_full_shape(r, cc, lay), s.dtype) for s, (_, r, cc, lay) in zip(shards, entries)]
    return _Exchange(list(shards), shapes, make, 4 * len(entries))


def _gather_pass_on(entries, fulls):
    def make(in_refs, out_refs):
        x, y, c = _mesh_pos()
        copies = []
        for ref, (_, r, cc, lay) in zip(out_refs, entries):
            for ch in _other_chips(x, y):
                landed = _half_view(ref, r, cc, lay, _chip_index(*ch), c)
                copies.append((landed, landed, (x, y, 1 - c)))
        return copies

    return _Exchange(list(fulls), [jax.ShapeDtypeStruct(f.shape, f.dtype) for f in fulls], make, 3 * len(entries),
                     {a: a for a in range(len(entries))})


def _gather_weights(entries, shards):
    n = len(entries)

    def body(*refs):
        ins, outs = refs[:n], refs[n:2 * n]
        send_sems, recv_sems, local_sems = refs[2 * n:]
        x, y, c = _mesh_pos()
        j = _chip_index(x, y)
        sibling = (x, y, 1 - c)
        chips = _other_chips(x, y)
        idx = [_chip_index(*ch) for ch in chips]

        def remote(k, src, dst, dev):
            return pltpu.make_async_remote_copy(src_ref=src, dst_ref=dst, send_sem=send_sems.at[k],
                                                recv_sem=recv_sems.at[k], device_id=dev, device_id_type=MESH)

        own = [pltpu.make_async_copy(ins[a], _shard_view(outs[a], r, cc, lay, j), local_sems.at[a])
               for a, (_, r, cc, lay) in enumerate(entries)]
        for cp in own:
            cp.start()
        first, passed = [], []
        for a, (_, r, cc, lay) in enumerate(entries):
            mine = ins[a].at[pl.ds(pl.multiple_of(c * (r // 2), 16), r // 2), :]
            for k, ch in enumerate(chips):
                first.append(remote(6 * a + k, mine, _half_view(outs[a], r, cc, lay, j, c), (*ch, c)))
                landed = _half_view(outs[a], r, cc, lay, idx[k], c)
                passed.append(remote(6 * a + 3 + k, landed, landed, sibling))
        for cp in first:
            cp.start()
        for a, (_, r, cc, lay) in enumerate(entries):
            for k in range(3):
                landed = _half_view(outs[a], r, cc, lay, idx[k], c)
                remote(6 * a + k, landed, landed, sibling).wait_recv()
                passed[3 * a + k].start()
        for a, (_, r, cc, lay) in enumerate(entries):
            for k in range(3):
                theirs = _half_view(outs[a], r, cc, lay, idx[k], 1 - c)
                remote(6 * a + 3 + k, theirs, theirs, sibling).wait_recv()
        for cp in first + passed:
            cp.wait_send()
        for cp in own:
            cp.wait()

    return pl.pallas_call(
        body, name="gather_weights", in_specs=[ANY] * n, out_specs=[ANY] * n,
        out_shape=[jax.ShapeDtypeStruct(_full_shape(r, cc, lay), s.dtype) for s, (_, r, cc, lay) in zip(shards, entries)],
        scratch_shapes=[pltpu.SemaphoreType.DMA((6 * n,)), pltpu.SemaphoreType.DMA((6 * n,)), pltpu.SemaphoreType.DMA((n,))],
        compiler_params=pltpu.CompilerParams(has_side_effects=True),
    )(*shards)


def _pair_exchange(entries, grads):
    def make(in_refs, out_refs):
        x, y, c = _mesh_pos()
        return [(_half_view(ref_in, r, cc, lay, i, 1 - c), ref_out.at[i], (x, y, 1 - c))
                for ref_in, ref_out, (_, r, cc, lay) in zip(in_refs, out_refs, entries) for i in range(4)]

    return _Exchange(list(grads), [jax.ShapeDtypeStruct((4, r // 2, cc), F32) for _, r, cc, _ in entries], make,
                     4 * len(entries))


def _whole_to_sibling(arrays):
    def make(in_refs, out_refs):
        x, y, c = _mesh_pos()
        return [(r, o, (x, y, 1 - c)) for r, o in zip(in_refs, out_refs)]

    return _Exchange(list(arrays), [jax.ShapeDtypeStruct(a.shape, a.dtype) for a in arrays], make, len(arrays))


def _chip_exchange(psends):
    def make(in_refs, out_refs):
        x, y, c = _mesh_pos()
        return [(ref_in.at[_chip_index(*ch)], ref_out.at[k], (*ch, c))
                for ref_in, ref_out in zip(in_refs, out_refs) for k, ch in enumerate(_other_chips(x, y))]

    return _Exchange(list(psends), [jax.ShapeDtypeStruct((3,) + p.shape[1:], p.dtype) for p in psends], make,
                     3 * len(psends))


def _to_all_chips(array):
    def make(in_refs, out_refs):
        x, y, c = _mesh_pos()
        return [(in_refs[0], out_refs[0].at[k], (*ch, c)) for k, ch in enumerate(_other_chips(x, y))]

    return _Exchange([array], [jax.ShapeDtypeStruct((3,) + array.shape, array.dtype)], make, 3)


SUM_ROWS = 256
ADAM_ROWS = 128


def _pair_sum(name, grad, recv, ids, r, c, layout):
    hr = r // 2
    tr = _row_tile(hr, SUM_ROWS)
    nb = hr // tr

    def body(ids_ref, g_ref, r_ref, send_ref, own_ref):
        s = g_ref[...] + r_ref[...]
        send_ref[...] = s.astype(send_ref.dtype)

        @pl.when(pl.program_id(1) == ids_ref[1])
        def _():
            own_ref[...] = s

    if layout == "row":
        g_spec = pl.BlockSpec((tr, c), lambda t, j, ids_ref: ((j * r + ids_ref[0] * hr) // tr + t, 0))
    elif layout == "col":
        g_spec = pl.BlockSpec((tr, c), lambda t, j, ids_ref: (ids_ref[0] * nb + t, j))
    else:
        g_spec = pl.BlockSpec((None, tr, c), lambda t, j, ids_ref: (j, ids_ref[0] * nb + t, 0))
    grid_spec = pltpu.PrefetchScalarGridSpec(
        num_scalar_prefetch=1, grid=(nb, 4),
        in_specs=[g_spec, pl.BlockSpec((None, tr, c), lambda t, j, ids_ref: (j, t, 0))],
        out_specs=[pl.BlockSpec((None, tr, c), lambda t, j, ids_ref: (j, t, 0)),
                   pl.BlockSpec((tr, c), lambda t, j, ids_ref: (t, 0))])
    return pl.pallas_call(
        body, name=name, grid_spec=grid_spec,
        out_shape=[jax.ShapeDtypeStruct((4, hr, c), BF16), jax.ShapeDtypeStruct((hr, c), F32)],
        compiler_params=_cparams(2),
    )(ids, grad, recv)


def _chip_sum(name, own, recv):
    hr, c = own.shape
    tr = _row_tile(hr, SUM_ROWS)

    def body(o_ref, r_ref, out_ref):
        out_ref[...] = ((o_ref[...] + r_ref[0].astype(F32)) + r_ref[1].astype(F32)) + r_ref[2].astype(F32)

    return pl.pallas_call(
        body, name=name, grid=(hr // tr,),
        in_specs=[pl.BlockSpec((tr, c), lambda i: (i, 0)), pl.BlockSpec((3, tr, c), lambda i: (0, i, 0))],
        out_specs=pl.BlockSpec((tr, c), lambda i: (i, 0)),
        out_shape=jax.ShapeDtypeStruct((hr, c), F32), compiler_params=_cparams(1),
    )(own, recv)


def _chip_sum_small(own, recv, ids):
    def body(ids_ref, o_ref, r_ref, out_ref):
        j = ids_ref[1]
        total = None
        for i in range(4):
            m = jnp.bitwise_xor(i, j)
            term = jnp.where(m == 0, o_ref[...], jnp.where(m == 2, r_ref[0], jnp.where(m == 1, r_ref[1], r_ref[2])))
            total = term if total is None else total + term
        out_ref[...] = total

    grid_spec = pltpu.PrefetchScalarGridSpec(
        num_scalar_prefetch=1, grid=(1,),
        in_specs=[pl.BlockSpec(own.shape, lambda i, ids_ref: (0, 0)), pl.BlockSpec(recv.shape, lambda i, ids_ref: (0, 0, 0))],
        out_specs=pl.BlockSpec(own.shape, lambda i, ids_ref: (0, 0)))
    return pl.pallas_call(body, name="chip_sum_small", grid_spec=grid_spec,
                          out_shape=jax.ShapeDtypeStruct(own.shape, F32), compiler_params=_cparams(1))(ids, own, recv)


def _adamw(name, w, m, v, mine, theirs, ids):
    rows, cols = w.shape
    half = rows // 2
    tr = _row_tile(half, ADAM_ROWS, unit=8)
    nb = half // tr
    c1 = 1.0 / (1.0 - ADAM_B1 ** ADAM_STEP)
    c2 = 1.0 / (1.0 - ADAM_B2 ** ADAM_STEP)

    def body(ids_ref, w_ref, m_ref, v_ref, mine_ref, theirs_ref, g_out, d_out, m_out, v_out):
        g = jnp.where(pl.program_id(0) == ids_ref[0], mine_ref[...], theirs_ref[...])
        m_new = ADAM_B1 * m_ref[...] + (1.0 - ADAM_B1) * g
        v_new = ADAM_B2 * v_ref[...] + (1.0 - ADAM_B2) * (g * g)
        d_out[...] = -ADAM_LR * ((m_new * c1) / (jnp.sqrt(v_new * c2) + ADAM_EPS) + ADAM_WD * w_ref[...])
        g_out[...] = g
        m_out[...] = m_new
        v_out[...] = v_new

    full = pl.BlockSpec((tr, cols), lambda h, i, ids_ref: (h * nb + i, 0))
    part = pl.BlockSpec((tr, cols), lambda h, i, ids_ref: (i, 0))
    grid_spec = pltpu.PrefetchScalarGridSpec(num_scalar_prefetch=1, grid=(2, nb),
                                             in_specs=[full, full, full, part, part], out_specs=[full] * 4)
    return pl.pallas_call(
        body, name=name, grid_spec=grid_spec,
        out_shape=[jax.ShapeDtypeStruct((rows, cols), F32)] * 4, compiler_params=_cparams(2),
    )(ids, w, m, v, mine, theirs)


def _small_shard(parts):
    return _flatten(parts, _BIG[-1][1])


_ENTRY = {e[0]: e for e in _BIG}
LATE_WEIGHTS = ("w_ssm_out", "w_attn_out", "w_mix_out", "w_ffn_up", "w_ffn_down")
FFN_GRADS = ("w_ffn_down", "w_ffn_up")
MIXER_GRADS = ("w_mix_out", "w_ssm_out", "w_attn_out")


class _StepPlan:
    def __init__(self, w, late_shards, shards, ids):
        self.w, self.g = w, {}
        self.late_shards, self.shards, self.ids = late_shards, shards, ids
        self.sums, self.halves, self.results = {}, {}, {}

    def run(self, name, fn, *args, **kw):
        at = getattr(self, "_at_" + name, None)
        if at is None:
            return fn(*args, **kw)
        exchange, landed = at()
        res, extra = fn(*args, bg=exchange, **kw)
        landed(extra)
        return res

    def _at_ssd_fwd(self):
        entries = [_ENTRY[n] for n in LATE_WEIGHTS]

        def landed(fulls):
            self.partly_gathered = fulls

        return _gather_ici(entries, self.late_shards), landed

    def _at_attn_fwd(self):
        entries = [_ENTRY[n] for n in LATE_WEIGHTS]
        return _gather_pass_on(entries, self.partly_gathered), lambda fulls: self.w.update(zip(LATE_WEIGHTS, fulls))

    def pair_sums(self, names, grads, recv):
        for n, gr, rv in zip(names, grads, recv):
            _, r, c, lay = _ENTRY[n]
            self.sums[n] = _pair_sum("pair_sum_" + n, gr, rv, self.ids, r, c, lay)

    def chip_sums(self, names, recv):
        for n, rv in zip(names, recv):
            self.halves[n] = _chip_sum("chip_sum_" + n, self.sums[n][1], rv)

    def adamw(self, names, theirs):
        for n, th in zip(names, theirs):
            sh = self.shards[n]
            self.results[n] = _adamw("adamw_" + n, sh["w"], sh["m"], sh["v"], self.halves[n], th, self.ids)

    def _pair_stage(self, names, grads):
        return (_pair_exchange([_ENTRY[n] for n in names], grads),
                lambda recv: self.pair_sums(names, grads, recv))

    def _at_ffn_up_dx(self):
        return self._pair_stage(FFN_GRADS, [self.g[n] for n in FFN_GRADS])

    def _at_ssm_gate_norm_bwd(self):
        return self._pair_stage(MIXER_GRADS, [self.g[n] for n in MIXER_GRADS])

    def _at_attn_bwd(self):
        return _chip_exchange([self.sums[n][0] for n in FFN_GRADS]), lambda recv: self.chip_sums(FFN_GRADS, recv)

    def _at_ssd_bwd(self):
        stages = (_chip_exchange([self.sums[n][0] for n in MIXER_GRADS]),
                  _whole_to_sibling([self.halves[n] for n in FFN_GRADS]))

        def landed(extra):
            recv, theirs = _split(stages, extra)
            self.chip_sums(MIXER_GRADS, recv)
            self.adamw(FFN_GRADS, theirs)

        return _join(*stages), landed

    def _at_ssm_conv_bwd(self):
        return _whole_to_sibling([self.halves[n] for n in MIXER_GRADS]), lambda theirs: self.adamw(MIXER_GRADS, theirs)

    def _at_in_proj_dx(self):
        grads = [_from_cat(self.g.pop("w_cat"))]
        self.pair_sums(("w_in",), grads, _run_exchange("grad_pair_exchange_w_in", _pair_exchange([_ENTRY["w_in"]], grads)))
        return _chip_exchange([self.sums["w_in"][0]]), lambda recv: self.chip_sums(("w_in",), recv)

    def finish(self, g_small, g_rep, rep_shards):
        stages = (_pair_exchange([_ENTRY["small"]], [g_small]), _whole_to_sibling([g_rep]))
        recv_small, recv_rep = _split(stages, _run_exchange("grad_pair_exchange_tail", _join(*stages)))
        self.pair_sums(("small",), [g_small], recv_small)
        p_rep, = _rowwise("pair_sum_replicated", lambda r0, a, b: [a + b], SMALL_ROWS, SMALL_ROWS,
                          [(g_rep, LANES, 0), (recv_rep[0], LANES, 0)], [], [(LANES, F32)], [])
        stages = (_chip_exchange([self.sums["small"][0]]), _to_all_chips(p_rep))
        recv, recv_rep = _split(stages, _run_exchange("grad_chip_exchange_tail", _join(*stages)))
        self.chip_sums(("small",), recv)
        g_rep_tot = _chip_sum_small(p_rep, recv_rep[0], self.ids)
        last = ("w_in", "small")
        self.adamw(last, _run_exchange("grad_half_share_tail", _whole_to_sibling([self.halves[n] for n in last])))
        ids_lo = self.ids * jnp.array([0, 1], jnp.int32)
        self.results["replicated"] = _adamw("adamw_replicated", rep_shards["w"], rep_shards["m"], rep_shards["v"],
                                            g_rep_tot[0:SMALL_ROWS // 2], g_rep_tot[SMALL_ROWS // 2:], ids_lo)


def kernel(x, meta_tokens, norm_pre_mix, w_in, ssm_conv_w, ssm_conv_b, ssm_dt_bias, ssm_a_log, ssm_d_skip, ssm_norm, w_ssm_out, attn_sinks, w_attn_out, w_mix_out, norm_post_mix, norm_pre_ffn, w_ffn_up, ffn_conv_w, ffn_conv_b, w_ffn_down, norm_post_ffn, loss_target, m_meta_tokens, m_norm_pre_mix, m_w_in, m_ssm_conv_w, m_ssm_conv_b, m_ssm_dt_bias, m_ssm_a_log, m_ssm_d_skip, m_ssm_norm, m_w_ssm_out, m_attn_sinks, m_w_attn_out, m_w_mix_out, m_norm_post_mix, m_norm_pre_ffn, m_w_ffn_up, m_ffn_conv_w, m_ffn_conv_b, m_w_ffn_down, m_norm_post_ffn, v_meta_tokens, v_norm_pre_mix, v_w_in, v_ssm_conv_w, v_ssm_conv_b, v_ssm_dt_bias, v_ssm_a_log, v_ssm_d_skip, v_ssm_norm, v_w_ssm_out, v_attn_sinks, v_w_attn_out, v_w_mix_out, v_norm_post_mix, v_norm_pre_ffn, v_w_ffn_up, v_ffn_conv_w, v_ffn_conv_b, v_w_ffn_down, v_norm_post_ffn):
    args = dict(locals())
    squeeze = lambda a: a.reshape(a.shape[-2:])
    wts = {n: squeeze(args[n]) for n in WEIGHT_ORDER}
    mom = {n: squeeze(args["m_" + n]) for n in WEIGHT_ORDER}
    var = {n: squeeze(args["v_" + n]) for n in WEIGHT_ORDER}
    x_i, y_i, c_i = _mesh_pos()
    ids = jnp.stack([c_i, _chip_index(x_i, y_i)]).astype(jnp.int32)
    big_names = [n for n, _, _, _ in _BIG[:-1]]
    small_names = [n for n, _, _ in _SMALL_SHARDED]
    rep_names = [n for n, _ in _REPLICATED]

    stacks = {"w": wts, "m": mom, "v": var}
    shards = {n: {k: d[n] for k, d in stacks.items()} for n in big_names}
    shards["small"] = {k: _small_shard([d[n] for n in small_names]) for k, d in stacks.items()}
    rep_shards = {k: _flatten([d[n] for n in rep_names], SMALL_ROWS) for k, d in stacks.items()}

    w_in4, small_all = _gather_weights([_ENTRY["w_in"], _ENTRY["small"]], [wts["w_in"].astype(BF16), shards["small"]["w"]])
    w = {n: wts[n] for n in rep_names}
    w["w_cat"] = _to_cat(w_in4)
    small_parts = [_unflatten(small_all[i], [shp for _, shp, _ in _SMALL_SHARDED]) for i in range(4)]
    for k, (n, _, axis) in enumerate(_SMALL_SHARDED):
        w[n] = jnp.concatenate([small_parts[i][k] for i in range(4)], axis=axis)
    plan = _StepPlan(w, [wts[n].astype(BF16) for n in LATE_WEIGHTS], shards, ids)

    head = jnp.concatenate([jnp.zeros((PAD, D_MODEL), F32), w["meta_tokens"]], axis=0)
    loss_sum, dx, dhead = _local_step(x[0], head, loss_target[0], plan)
    loss = lax.psum(loss_sum * (0.5 / D_MODEL), ("x", "y", "c"))
    g = plan.g
    g["meta_tokens"] = dhead[PAD:]
    g_small = jnp.stack([_small_shard([_shard_of(g[n], i, shp, ax) for n, shp, ax in _SMALL_SHARDED]) for i in range(4)])
    plan.finish(g_small, _flatten([g[n] for n in rep_names], SMALL_ROWS), rep_shards)

    results = {}
    for kind in range(4):
        results.update({(kind, n): plan.results[n][kind] for n in big_names})
        parts = _unflatten(plan.results["small"][kind], [shp for _, shp, _ in _SMALL_SHARDED])
        results.update({(kind, n): parts[k] for k, n in enumerate(small_names)})
        parts = _unflatten(plan.results["replicated"][kind], [(1, width) for _, width in _REPLICATED])
        results.update({(kind, n): parts[k] for k, n in enumerate(rep_names)})
    outs = [results[kind, n].reshape(args[n].shape) for kind in range(4) for n in WEIGHT_ORDER]
    return (loss, dx[None], *outs)
```

```python
import math
from typing import Any, Callable, NamedTuple, Sequence

import jax
import jax.numpy as jnp
from jax import lax
from jax.experimental import pallas as pl
from jax.experimental.pallas import tpu as pltpu

F32 = jnp.float32
BF16 = jnp.bfloat16

D_MODEL = 1024
N_META = 16
T = 128
PAD = T - N_META
D_INNER = 2048
SSM_HEADS = 32
HEAD_P = 64
SSM_GROUPS = 4
GROUP_W = D_INNER // SSM_GROUPS
D_STATE = 128
CONV_DIM = D_INNER + 2 * SSM_GROUPS * D_STATE
ATTN_HEADS = 16
KV_HEADS = 4
ATTN_W = 1024
KV_W = 256
FFN_DIM = 2816
N_IN = 8736
EPS = 1e-6
NEG = -1e30
SCALE = 0.125

P_Q, P_K, P_V, P_DT, P_Z, P_GATE, P_XBC = 0, 1024, 1280, 1536, 2048, 4096, 6144
QKV_W = 1536
P_W = 9216

ADAM_LR, ADAM_B1, ADAM_B2, ADAM_EPS, ADAM_WD, ADAM_STEP = 0.001, 0.9, 0.999, 1e-08, 0.01, 10

VMEM_BUDGET = 40 * 1024 * 1024
VMEM_LIMIT = 56 * 1024 * 1024
MESH = pl.DeviceIdType.MESH
ANY = pl.BlockSpec(memory_space=pl.ANY)


def _cparams(n_axes, **kw):
    return pltpu.CompilerParams(dimension_semantics=("arbitrary",) * n_axes, vmem_limit_bytes=VMEM_LIMIT, **kw)


class _Exchange(NamedTuple):
    ins: Sequence[Any]
    out_shapes: Sequence[Any]
    make_copies: Callable
    n_copies: int
    aliases: dict = {}


def _call(body, name, grid, in_specs, out_specs, out_shape, operands, scratch_shapes=(), aliases=None, bg=None):
    aliases = dict(aliases or {})
    if bg is None:
        return pl.pallas_call(body, name=name, grid=grid, in_specs=in_specs, out_specs=out_specs, out_shape=out_shape,
                              scratch_shapes=list(scratch_shapes), input_output_aliases=aliases,
                              compiler_params=_cparams(len(grid)))(*operands)
    n_in, n_out, n_scr = len(in_specs), len(out_specs), len(scratch_shapes)
    nb_in, nb_out = len(bg.ins), len(bg.out_shapes)

    def hosted(*refs):
        ins, bg_ins = refs[:n_in], refs[n_in:n_in + nb_in]
        outs = refs[n_in + nb_in:n_in + nb_in + n_out]
        bg_outs = refs[n_in + nb_in + n_out:n_in + nb_in + n_out + nb_out]
        scratch = refs[n_in + nb_in + n_out + nb_out:n_in + nb_in + n_out + nb_out + n_scr]
        send_sems, recv_sems = refs[-2:]
        pids = [pl.program_id(a) for a in range(len(grid))]
        first, last = pids[0] == 0, pids[0] == grid[0] - 1
        for p, g in zip(pids[1:], grid[1:]):
            first, last = first & (p == 0), last & (p == g - 1)
        copies = []
        for k, (src, dst, peer) in enumerate(bg.make_copies(bg_ins, bg_outs)):
            if peer is None:
                copies.append(pltpu.make_async_copy(src, dst, send_sems.at[k]))
            else:
                copies.append(pltpu.make_async_remote_copy(src_ref=src, dst_ref=dst, send_sem=send_sems.at[k],
                                                           recv_sem=recv_sems.at[k], device_id=peer, device_id_type=MESH))
        assert len(copies) == bg.n_copies

        @pl.when(first)
        def _():
            for cp in copies:
                cp.start()

        body(*ins, *outs, *scratch)

        @pl.when(last)
        def _():
            for cp in copies:
                cp.wait()

    aliases = {(k if k < n_in else k + nb_in): v for k, v in aliases.items()}
    aliases.update({n_in + k: n_out + v for k, v in bg.aliases.items()})
    res = pl.pallas_call(
        hosted, name=name, grid=grid, in_specs=list(in_specs) + [ANY] * nb_in, out_specs=list(out_specs) + [ANY] * nb_out,
        out_shape=list(out_shape) + list(bg.out_shapes), input_output_aliases=aliases,
        scratch_shapes=list(scratch_shapes) + [pltpu.SemaphoreType.DMA((bg.n_copies,))] * 2,
        compiler_params=_cparams(len(grid), has_side_effects=True))(*operands, *bg.ins)
    return res[:n_out], res[n_out:]


def _sigmoid(x):
    return 1.0 / (1.0 + jnp.exp(-x))


def _silu(x):
    return x * _sigmoid(x)


def _silu_grad(x):
    s = _sigmoid(x)
    return x * s, s * (1.0 + x * (1.0 - s))


def _dsilu(x):
    return _silu_grad(x)[1]


def _softplus(x):
    e = jnp.exp(-jnp.abs(x))
    small = e * (1.0 - e * (0.5 - e * (1.0 / 3.0)))
    return jnp.maximum(x, 0.0) + jnp.where(e < 0.01, small, jnp.log(1.0 + e))


def _rms(x, w):
    r = lax.rsqrt(jnp.mean(x * x, axis=-1, keepdims=True) + EPS)
    return x * r * w


def _rms_bwd(dy, x, w):
    r = lax.rsqrt(jnp.mean(x * x, axis=-1, keepdims=True) + EPS)
    xh = x * r
    g = dy * w
    dx = r * (g - xh * jnp.mean(g * xh, axis=-1, keepdims=True))
    dw = jnp.sum(dy * xh, axis=0, keepdims=True)
    return dx, dw


def _dot(a, b):
    return jnp.dot(a, b, preferred_element_type=F32)


def _dot_nt(a, b):
    return lax.dot_general(a, b, (((1,), (1,)), ((), ())), preferred_element_type=F32)


def _dot_tn(a, b):
    return lax.dot_general(a, b, (((0,), (0,)), ((), ())), preferred_element_type=F32)


def _split3(x):
    hi = x.astype(BF16)
    r = x - hi.astype(F32)
    mid = r.astype(BF16)
    lo = (r - mid.astype(F32)).astype(BF16)
    return hi, mid, lo


def _xdot(x, e):
    hi, mid, lo = _split3(x)
    return _dot(hi, e) + _dot(mid, e) + _dot(lo, e)


def _xdot_l(e, x):
    hi, mid, lo = _split3(x)
    return _dot(e, hi) + _dot(e, mid) + _dot(e, lo)


def _iota(shape, dim):
    return lax.broadcasted_iota(jnp.int32, shape, dim)


def _divisors(n, unit):
    return [t for t in range(unit, n + 1, unit) if n % t == 0]


MIN_MATMUL_STEPS = 8


def _matmul_tiles(m, n, k, a_bytes, b_bytes, o_bytes, m_unit):
    best = None
    for tm in _divisors(m, m_unit):
        for tn in _divisors(n, 128):
            for tk in _divisors(k, 128):
                acc = 0 if tk == k else tm * tn * 4
                vm = 2 * (tm * tk * a_bytes + tk * tn * b_bytes + tm * tn * o_bytes) + acc
                if vm > VMEM_BUDGET:
                    continue
                steps = (m // tm) * (n // tn) * (k // tk)
                score = (tk == k, min(steps, MIN_MATMUL_STEPS), min(tm, 256), tm * tn * tk)
                if best is None or score > best[0]:
                    best = (score, (tm, tn, tk))
    return best[1]


def _matmul(name, a, b, mode, out_dtype, bg=None):
    if mode == "nn":
        (m, k), n = a.shape, b.shape[1]
    elif mode == "nt":
        (m, k), n = a.shape, b.shape[0]
    else:
        (k, m), n = a.shape, b.shape[1]
    ab, bb, ob = a.dtype.itemsize, b.dtype.itemsize, jnp.dtype(out_dtype).itemsize
    tm, tn, tk = _matmul_tiles(m, n, k, ab, bb, ob, 128 if mode == "tn" else 16)
    nk = k // tk
    dot = {"nn": _dot, "nt": _dot_nt, "tn": _dot_tn}[mode]

    def body(a_ref, b_ref, o_ref, *scratch):
        prod = dot(a_ref[...].astype(BF16), b_ref[...].astype(BF16))
        if nk == 1:
            o_ref[...] = prod.astype(o_ref.dtype)
        else:
            acc_ref, = scratch
            kk = pl.program_id(2)

            @pl.when(kk == 0)
            def _():
                acc_ref[...] = prod

            @pl.when(kk > 0)
            def _():
                acc_ref[...] += prod

            @pl.when(kk == nk - 1)
            def _():
                o_ref[...] = acc_ref[...].astype(o_ref.dtype)

    a_spec = pl.BlockSpec((tk, tm), lambda i, j, kk: (kk, i)) if mode == "tn" else pl.BlockSpec((tm, tk), lambda i, j, kk: (i, kk))
    b_spec = pl.BlockSpec((tn, tk), lambda i, j, kk: (j, kk)) if mode == "nt" else pl.BlockSpec((tk, tn), lambda i, j, kk: (kk, j))
    res = _call(body, name, (m // tm, n // tn, nk), [a_spec, b_spec], [pl.BlockSpec((tm, tn), lambda i, j, kk: (i, j))],
                [jax.ShapeDtypeStruct((m, n), out_dtype)], [a, b],
                scratch_shapes=[] if nk == 1 else [pltpu.VMEM((tm, tn), F32)], bg=bg)
    return res[0] if bg is None else (res[0][0], res[1])


def _row_tile(n_rows, cap, unit=16):
    return max(t for t in _divisors(n_rows, unit) if t <= cap)


ROW_SUB = 384
GROUP_UNROLL = 4


def _rowwise(name, fn, n_rows, tm, row_ins, full_ins, row_outs, acc_outs, bg=None):
    n_in = len(row_ins) + len(full_ins)
    n_ro = len(row_outs)
    into = [(k, o[3]) for k, o in enumerate(row_outs) if len(o) > 2 and o[2] == "into"]

    n_row_in = len(row_ins)
    sub = min(tm, ROW_SUB)

    def body(*refs):
        i = pl.program_id(0)
        outs = refs[n_in + len(into):]

        def group(s, sums):
            rows = pl.ds(pl.multiple_of(s * sub, sub), sub)
            vals = [r[rows, :] for r in refs[:n_row_in]] + [r[...] for r in refs[n_row_in:n_in]]
            res = fn(i * tm + s * sub, *vals)
            for o, r, v in zip(row_outs, outs[:n_ro], res[:n_ro]):
                if len(o) > 2 and o[2] == "first":
                    @pl.when(i == 0)
                    def _(r=r, v=v):
                        r[rows, :] = v.astype(r.dtype)
                else:
                    r[rows, :] = v.astype(r.dtype)
            return tuple(a + v for a, v in zip(sums, res[n_ro:]))

        sums = lax.fori_loop(0, tm // sub, group, tuple(jnp.zeros((1, w), F32) for w in acc_outs), unroll=GROUP_UNROLL)

        @pl.when(i == 0)
        def _():
            for r, v in zip(outs[n_ro:], sums):
                r[...] = v

        @pl.when(i > 0)
        def _():
            for r, v in zip(outs[n_ro:], sums):
                r[...] += v

    def in_spec(entry):
        w, cb = entry[1], entry[2]
        if len(entry) > 3 and entry[3] == "prev":
            return pl.BlockSpec((tm, w), lambda i: (jnp.maximum(i - 1, 0), cb))
        if len(entry) > 3 and entry[3] == "first":
            return pl.BlockSpec((tm, w), lambda i: (0, cb))
        return pl.BlockSpec((tm, w), lambda i: (i, cb))

    def out_spec(o):
        if len(o) == 2:
            return pl.BlockSpec((tm, o[0]), lambda i: (i, 0)), jax.ShapeDtypeStruct((n_rows, o[0]), o[1])
        if o[2] == "new":
            return pl.BlockSpec((tm, o[0]), lambda i: (i, o[4])), jax.ShapeDtypeStruct((n_rows, o[3]), o[1])
        if o[2] == "into":
            return pl.BlockSpec((tm, o[0]), lambda i: (i, o[4])), jax.ShapeDtypeStruct(o[3].shape, o[3].dtype)
        if o[2] == "first":
            return pl.BlockSpec((tm, o[0]), lambda i: (0, 0)), jax.ShapeDtypeStruct((tm, o[0]), o[1])
        return pl.BlockSpec((tm, o[0]), lambda i: (jnp.maximum(i - 1, 0), 0)), jax.ShapeDtypeStruct((o[3], o[0]), o[1])

    in_specs = [in_spec(e) for e in row_ins]
    in_specs += [pl.BlockSpec(a.shape, lambda i: (0, 0)) for a in full_ins]
    in_specs += [pl.BlockSpec(memory_space=pl.ANY) for _ in into]
    specs_shapes = [out_spec(o) for o in row_outs]
    out_specs = [s for s, _ in specs_shapes] + [pl.BlockSpec((1, w), lambda i: (0, 0)) for w in acc_outs]
    out_shape = [s for _, s in specs_shapes] + [jax.ShapeDtypeStruct((1, w), F32) for w in acc_outs]
    return _call(body, name, (n_rows // tm,), in_specs, out_specs, out_shape,
                 [e[0] for e in row_ins] + list(full_ins) + [arr for _, arr in into],
                 aliases={n_in + a: k for a, (k, _) in enumerate(into)}, bg=bg)


def _valid_rows(first_row, tm, lo):
    return (first_row + _iota((tm, 1), 0)) >= lo


CONV_ROWS = 128
CONV_SUB = 16
CONV_LANES = 256


def _conv_specs(tm, width, blk, n_rows, after):
    specs = [pl.BlockSpec((tm, width), lambda i: (i, blk)),
             pl.BlockSpec((8, width), lambda i: (jnp.maximum(i * (tm // 8) - 1, 0), blk))]
    if after:
        specs.append(pl.BlockSpec((16, width), lambda i: (jnp.minimum((i + 1) * (tm // 16), n_rows // 16 - 1), blk)))
    return specs


def _conv_window(win, w_ref, b_ref, taps, c0, cw, n):
    acc = b_ref[:, c0:c0 + cw] + w_ref[taps - 1:taps, c0:c0 + cw] * win[8:8 + n]
    for k in range(taps - 1):
        acc = acc + w_ref[k:k + 1, c0:c0 + cw] * win[8 - (taps - 1) + k:8 - (taps - 1) + k + n]
    return acc


def _ffn_act(name, u_raw, conv_w, conv_b, n_rows):
    tm, sub, cw = CONV_ROWS, CONV_SUB, CONV_LANES
    taps, width = conv_w.shape
    half = width // 2

    def body(cur_ref, prev_ref, w_ref, b_ref, f_ref, ext_ref):
        i = pl.program_id(0)
        ext_ref[0:8, :] = jnp.where(i > 0, prev_ref[...], 0.0)
        ext_ref[8:8 + tm, :] = cur_ref[...]
        for q in range(half // cw):
            a0, g0 = q * cw, half + q * cw

            def group(s, carry):
                r = pl.multiple_of(s * sub, sub)
                a = _conv_window(ext_ref[pl.ds(r, sub + 8), a0:a0 + cw], w_ref, b_ref, taps, a0, cw, sub)
                g = _conv_window(ext_ref[pl.ds(r, sub + 8), g0:g0 + cw], w_ref, b_ref, taps, g0, cw, sub)
                f = jnp.where(_valid_rows(i * tm + r, sub, PAD), _silu(a) * g, 0.0)
                f_ref[pl.ds(r, sub), a0:a0 + cw] = f.astype(f_ref.dtype)
                return carry

            lax.fori_loop(0, tm // sub, group, 0, unroll=GROUP_UNROLL)

    return pl.pallas_call(
        body, name=name, grid=(n_rows // tm,),
        in_specs=_conv_specs(tm, width, 0, n_rows, False) + [pl.BlockSpec((taps, width), lambda i: (0, 0)),
                                                             pl.BlockSpec((1, width), lambda i: (0, 0))],
        out_specs=pl.BlockSpec((tm, half), lambda i: (i, 0)),
        out_shape=jax.ShapeDtypeStruct((n_rows, half), BF16),
        scratch_shapes=[pltpu.VMEM((tm + 8, width), F32)],
        compiler_params=_cparams(1),
    )(u_raw, u_raw, conv_w, conv_b)


def _conv_bwd(name, raw, raw_blk, dsrcs, chunk_src, conv_w, conv_b, n_rows, gated, into=None, into_blk=0, bg=None):
    taps, width = conv_w.shape
    half = width // 2 if gated else width
    tm, sub, cw = CONV_ROWS, CONV_SUB, CONV_LANES
    te = tm + 16
    nd = len(dsrcs)
    n_parts = 2 if gated else 1

    def body(*refs):
        cur_ref, prev_ref, next_ref = refs[0:3]
        dcur, dnext = refs[3:3 + nd], refs[3 + nd:3 + 2 * nd]
        w_ref, b_ref = refs[3 + 2 * nd:5 + 2 * nd]
        out_ref, acc_ref, ext_ref, du_ref = refs[-4:]
        i = pl.program_id(0)
        ext_ref[0:8, :] = jnp.where(i > 0, prev_ref[...], 0.0)
        ext_ref[8:8 + tm, :] = cur_ref[...]
        ext_ref[8 + tm:24 + tm, :] = next_ref[...]

        for q, (src, off) in enumerate(chunk_src):
            cols = [q * cw, half + q * cw][:n_parts]

            def conv_grad(r, d):
                pre = [_conv_window(ext_ref[pl.ds(r, sub + 8), c0:c0 + cw], w_ref, b_ref, taps, c0, cw, sub) for c0 in cols]
                row = i * tm + r + _iota((sub, 1), 0)
                live = (row >= PAD) & (row < n_rows)
                if gated:
                    act, dact = _silu_grad(pre[0])
                    dus = [d * pre[1] * dact, d * act]
                else:
                    dus = [d * _dsilu(pre[0])]
                for part, du in enumerate(dus):
                    du_ref[part, pl.ds(r, sub), :] = jnp.where(live, du, 0.0)

            def tile_rows(s, carry):
                r = pl.multiple_of(s * sub, sub)
                conv_grad(r, dcur[src][pl.ds(r, sub), off:off + cw].astype(F32))
                return carry

            lax.fori_loop(0, tm // sub, tile_rows, 0, unroll=GROUP_UNROLL)
            conv_grad(tm, dnext[src][:, off:off + cw].astype(F32))

            for part, c0 in enumerate(cols):
                taps_w = [w_ref[k:k + 1, c0:c0 + cw] for k in range(taps)]

                def back(s, sums):
                    new = list(sums)
                    for u in range(2):
                        r = pl.multiple_of((2 * s + u) * sub, sub)
                        win = du_ref[part, pl.ds(r, sub + 8), :]
                        raw_rows = ext_ref[pl.ds(8 + r, sub), c0:c0 + cw]
                        draw = jnp.zeros((sub, cw), F32)
                        for k in range(taps):
                            shifted = win[taps - 1 - k:taps - 1 - k + sub]
                            draw = draw + taps_w[k] * shifted
                            new[k] = new[k] + shifted * raw_rows
                        new[taps] = new[taps] + win[0:sub]
                        out_ref[pl.ds(r, sub), c0:c0 + cw] = jnp.where(_valid_rows(i * tm + r, sub, PAD), draw, 0.0).astype(out_ref.dtype)
                    return tuple(new)

                sums = lax.fori_loop(0, tm // (2 * sub), back, tuple(jnp.zeros((sub, cw), F32) for _ in range(taps + 1)))
                for k in range(taps + 1):
                    total = jnp.sum(sums[k], axis=0, keepdims=True)
                    acc_ref[k:k + 1, c0:c0 + cw] = jnp.where(i == 0, total, acc_ref[k:k + 1, c0:c0 + cw] + total)

    in_specs = _conv_specs(tm, width, raw_blk, n_rows, True)
    in_specs += [pl.BlockSpec((tm, d.shape[1]), lambda i: (i, 0)) for d in dsrcs]
    in_specs += [pl.BlockSpec((16, d.shape[1]), lambda i: (jnp.minimum((i + 1) * (tm // 16), n_rows // 16 - 1), 0)) for d in dsrcs]
    in_specs += [pl.BlockSpec((taps, width), lambda i: (0, 0)), pl.BlockSpec((1, width), lambda i: (0, 0))]
    operands = [raw, raw, raw] + list(dsrcs) + list(dsrcs) + [conv_w, conv_b]
    aliases = {}
    if into is None:
        out0 = jax.ShapeDtypeStruct((n_rows, width), BF16)
    else:
        in_specs.append(pl.BlockSpec(memory_space=pl.ANY))
        operands.append(into)
        aliases = {len(operands) - 1: 0}
        out0 = jax.ShapeDtypeStruct(into.shape, into.dtype)
    return _call(body, name, (n_rows // tm,), in_specs,
                 [pl.BlockSpec((tm, width), lambda i: (i, into_blk)), pl.BlockSpec((8, width), lambda i: (0, 0))],
                 [out0, jax.ShapeDtypeStruct((8, width), F32)], operands,
                 scratch_shapes=[pltpu.VMEM((tm + 24, width), F32), pltpu.VMEM((n_parts, te + 8, cw), F32)],
                 aliases=aliases, bg=bg)


def _ssd_specs(n_chunks, rev, per_step=1):
    cidx = (lambda c: n_chunks - 1 - c) if rev else (lambda c: c)
    xw, nw = per_step * GROUP_W, per_step * D_STATE
    xg0, bg0, cg0 = P_XBC // xw, (P_XBC + D_INNER) // nw, (P_XBC + D_INNER + SSM_GROUPS * D_STATE) // nw

    def cur(width, blk0):
        return pl.BlockSpec((T, width), lambda g, c: (cidx(c), blk0 + g))

    def prev(width, blk0):
        return pl.BlockSpec((8, width), lambda g, c: (jnp.maximum(cidx(c) * (T // 8) - 1, 0), blk0 + g))

    specs = [cur(xw, xg0), prev(xw, xg0), cur(nw, bg0), prev(nw, bg0), cur(nw, cg0), prev(nw, cg0),
             pl.BlockSpec((T, 128), lambda g, c: (cidx(c), P_DT // 128))]
    wb, wc = D_INNER // nw, (D_INNER + SSM_GROUPS * D_STATE) // nw
    specs += [pl.BlockSpec((4, xw), lambda g, c: (0, g)),
              pl.BlockSpec((4, nw), lambda g, c: (0, wb + g)),
              pl.BlockSpec((4, nw), lambda g, c: (0, wc + g)),
              pl.BlockSpec((1, xw), lambda g, c: (0, g)),
              pl.BlockSpec((1, nw), lambda g, c: (0, wb + g)),
              pl.BlockSpec((1, nw), lambda g, c: (0, wc + g))]
    specs += [pl.BlockSpec((1, 128), lambda g, c: (0, 0))] * 3
    return specs, cidx


def _ssd_chunk_forward(refs, ext_ref, g, c):
    (xc_ref, xp_ref, bc_ref, bp_ref, cc_ref, cp_ref, dt_ref, wx_ref, wb_ref, wc_ref,
     bx_ref, bb_ref, bcb_ref, dtb_ref, alog_ref, dsk_ref) = refs

    def conv_pre(cur_ref, prev_ref, w_ref, b_ref, width):
        ext_ref[0:8, 0:width] = jnp.where(c > 0, prev_ref[...], 0.0)
        ext_ref[8:8 + T, 0:width] = cur_ref[...]
        w = w_ref[...]
        acc = b_ref[...] + w[3:4] * cur_ref[...]
        for k in range(3):
            acc = acc + w[k:k + 1] * ext_ref[pl.ds(5 + k, T), 0:width]
        return acc

    valid = _valid_rows(c * T, T, PAD)
    v = {}
    v["valid"] = valid
    v["x_pre"] = conv_pre(xc_ref, xp_ref, wx_ref, bx_ref, GROUP_W)
    v["b_pre"] = conv_pre(bc_ref, bp_ref, wb_ref, bb_ref, D_STATE)
    v["c_pre"] = conv_pre(cc_ref, cp_ref, wc_ref, bcb_ref, D_STATE)
    xs = _silu(v["x_pre"])
    bm = jnp.where(valid, _silu(v["b_pre"]), 0.0)
    cm = jnp.where(valid, _silu(v["c_pre"]), 0.0)
    dtr = dt_ref[...] + dtb_ref[...]
    dt = jnp.where(valid, _softplus(dtr), 0.0)
    a_neg = -jnp.exp(alog_ref[...])
    a = dt * a_neg
    tril = _iota((T, T), 0) >= _iota((T, T), 1)
    cs = _xdot_l(tril.astype(BF16), a)
    hh, ll = _iota((128, GROUP_W), 0), _iota((128, GROUP_W), 1)
    expand = (hh == 8 * g + jnp.right_shift(ll, 6)).astype(BF16)
    sh, sj = _iota((128, 128), 0), _iota((128, 128), 1)
    select = ((sh == 8 * g + sj) & (sj < 8)).astype(BF16)
    hh_t, ll_t = _iota((GROUP_W, 128), 1), _iota((GROUP_W, 128), 0)
    v["expand_t"] = (hh_t == 8 * g + jnp.right_shift(ll_t, 6)).astype(BF16)
    v["select_t"] = ((sj == 8 * g + sh) & (sh < 8)).astype(BF16)
    cs_e = _xdot(cs, expand)
    dt_e = _xdot(dt, expand)
    cs_loc = _xdot(cs, select)
    cs_loc_t = cs_loc.T
    cs_last_e = cs_e[T - 1:T, :]
    v.update(xs=xs, bm=bm, cm=cm, dtr=dtr, dt=dt, a_neg=a_neg, tril=tril, expand=expand, select=select,
             cs_e=cs_e, dt_e=dt_e, cs_loc=cs_loc, cs_loc_t=cs_loc_t, cs_last_e=cs_last_e)
    v["xdt"] = xs * dt_e
    v["decay_e"] = jnp.exp(cs_last_e - cs_e)
    v["ecs_e"] = jnp.exp(cs_e)
    v["elast_e"] = jnp.exp(cs_last_e)
    v["d_e"] = _xdot(dsk_ref[...], expand)
    v["gmat"] = _dot_nt(cm.astype(BF16), bm.astype(BF16))
    return v


def _ssd_decay_pair(v, jp):
    out = []
    for j in (2 * jp, 2 * jp + 1):
        diff = v["cs_loc"][:, j:j + 1] - v["cs_loc_t"][j:j + 1, :]
        out.append(jnp.where(v["tril"], jnp.exp(jnp.where(v["tril"], diff, 0.0)), 0.0))
    return out


def _block_diag_pair(xp):
    lane = _iota(xp.shape, 1)
    return jnp.concatenate([jnp.where(lane < HEAD_P, xp, 0.0), jnp.where(lane >= HEAD_P, xp, 0.0)], axis=0)


SSD_GROUPS_PER_STEP = 4


def _ssd_group_refs(refs, gg):
    x_w, n_w = pl.ds(GROUP_W * gg, GROUP_W), pl.ds(D_STATE * gg, D_STATE)
    lanes = [x_w, x_w, n_w, n_w, n_w, n_w, None, x_w, n_w, n_w, x_w, n_w, n_w, None, None, None]
    return [r if w is None else r.at[:, w] for r, w in zip(refs, lanes)]


def _ssd_fwd(p, conv_w, conv_b, dt_bias, a_log, d_skip, n_chunks, bg=None):
    n_rows = n_chunks * T
    in_specs, _ = _ssd_specs(n_chunks, rev=False, per_step=SSD_GROUPS_PER_STEP)
    per = SSD_GROUPS_PER_STEP

    def body(*refs):
        y_ref, hin_ref, st_ref, ext_ref = refs[16:]
        g2, c = pl.program_id(0), pl.program_id(1)

        @pl.when(c == 0)
        def _():
            st_ref[...] = jnp.zeros_like(st_ref)

        for gg in range(per):
            v = _ssd_chunk_forward(_ssd_group_refs(refs[:16], gg), ext_ref.at[gg], per * g2 + gg, c)
            state = st_ref[gg]
            hin_ref[gg] = state
            ys = []
            for jp in range(4):
                l0, l1 = _ssd_decay_pair(v, jp)
                lhs = jnp.concatenate([v["gmat"] * l0, v["gmat"] * l1], axis=1).astype(BF16)
                rhs = _block_diag_pair(v["xdt"][:, 128 * jp:128 * jp + 128]).astype(BF16)
                ys.append(_dot(lhs, rhs))
            y = jnp.concatenate(ys, axis=1)
            y = y + _dot(v["cm"].astype(BF16), state.astype(BF16)) * v["ecs_e"] + v["xs"] * v["d_e"]
            y_ref[:, GROUP_W * gg:GROUP_W * gg + GROUP_W] = y
            s_new = _dot_tn(v["bm"].astype(BF16), (v["xdt"] * v["decay_e"]).astype(BF16))
            st_ref[gg] = state * v["elast_e"] + s_new

    return _call(
        body, "ssd_fwd", (SSM_GROUPS // per, n_chunks), in_specs,
        [pl.BlockSpec((T, per * GROUP_W), lambda g, c: (c, g)),
         pl.BlockSpec((per, None, D_STATE, GROUP_W), lambda g, c: (g, c, 0, 0))],
        [jax.ShapeDtypeStruct((n_rows, D_INNER), F32),
         jax.ShapeDtypeStruct((SSM_GROUPS, n_chunks, D_STATE, GROUP_W), F32)],
        [p, p, p, p, p, p, p, conv_w, conv_w, conv_w, conv_b, conv_b, conv_b, dt_bias, a_log, d_skip],
        scratch_shapes=[pltpu.VMEM((per, D_STATE, GROUP_W), F32), pltpu.VMEM((per, T + 8, GROUP_W), F32)], bg=bg)


def _ssd_bwd(p, conv_w, conv_b, dt_bias, a_log, d_skip, hin, dy, n_chunks, bg=None):
    n_rows = n_chunks * T
    per = SSD_GROUPS_PER_STEP
    in_specs, cidx = _ssd_specs(n_chunks, rev=True, per_step=per)
    in_specs = in_specs + [pl.BlockSpec((per, None, D_STATE, GROUP_W), lambda g, c: (g, cidx(c), 0, 0)),
                           pl.BlockSpec((T, per * GROUP_W), lambda g, c: (cidx(c), g))]

    def body(*refs):
        hin_ref, dy_ref = refs[16:18]
        dx_ref, db_ref, dc_ref, ddt_ref, dpar_ref, dst_ref, ext_ref = refs[18:]
        for gg in range(per):
            x_w, n_w = pl.ds(GROUP_W * gg, GROUP_W), pl.ds(D_STATE * gg, D_STATE)
            group_body(_ssd_group_refs(refs[:16], gg), hin_ref.at[gg], dy_ref.at[:, x_w], dx_ref.at[:, x_w],
                       db_ref.at[:, n_w], dc_ref.at[:, n_w], ddt_ref.at[:, n_w], dpar_ref.at[gg], dst_ref.at[gg],
                       ext_ref.at[gg], per * pl.program_id(0) + gg)

    def group_body(in_refs, hin_ref, dy_ref, dx_ref, db_ref, dc_ref, ddt_ref, dpar_ref, dst_ref, ext_ref, g):
        step = pl.program_id(1)
        c = n_chunks - 1 - step

        @pl.when(step == 0)
        def _():
            dst_ref[...] = jnp.zeros_like(dst_ref)

        v = _ssd_chunk_forward(in_refs, ext_ref, g, c)
        hin_f = hin_ref[...]
        hin_b = hin_f.astype(BF16)
        dyv = dy_ref[...]
        dst = dst_ref[...]
        dst_b = dst.astype(BF16)
        xs, bm, cm, xdt = v["xs"], v["bm"], v["cm"], v["xdt"]
        bm_b, cm_b = bm.astype(BF16), cm.astype(BF16)

        dd_e = jnp.sum(dyv * xs, axis=0, keepdims=True)
        dxs = dyv * v["d_e"]
        ch = _dot(cm_b, hin_b)
        dch = (dyv * v["ecs_e"]).astype(BF16)
        dcm = _dot_nt(dch, hin_b)
        dhin = _dot_tn(cm_b, dch) + dst * v["elast_e"]
        dcs_e = dyv * ch * v["ecs_e"]
        dxd = _dot(bm_b, dst_b)
        dbm = _dot_nt((xdt * v["decay_e"]).astype(BF16), dst_b)
        dxdt_state = dxd * v["decay_e"]
        q = dxdt_state * xdt
        dcs_e = dcs_e - q
        dlast_e = jnp.sum(q, axis=0, keepdims=True) + jnp.sum(dst * hin_f, axis=0, keepdims=True) * v["elast_e"]
        dg = jnp.zeros((T, T), F32)
        rs_cols = jnp.zeros((T, 128), F32)
        cs_rows = jnp.zeros((128, T), F32)
        lane_i, sub_i = _iota((T, 128), 1), _iota((128, T), 0)
        dxdt_parts = []
        for jp in range(4):
            l0, l1 = _ssd_decay_pair(v, jp)
            m0, m1 = v["gmat"] * l0, v["gmat"] * l1
            xbd = _block_diag_pair(xdt[:, 128 * jp:128 * jp + 128]).astype(BF16)
            dyp = dyv[:, 128 * jp:128 * jp + 128]
            dm = _dot_nt(dyp.astype(BF16), xbd)
            dm0, dm1 = dm[:, 0:T], dm[:, T:2 * T]
            dg = dg + dm0 * l0 + dm1 * l1
            for j, qq in ((2 * jp, dm0 * m0), (2 * jp + 1, dm1 * m1)):
                rs_cols = jnp.where(lane_i == j, jnp.sum(qq, axis=1, keepdims=True), rs_cols)
                cs_rows = jnp.where(sub_i == j, jnp.sum(qq, axis=0, keepdims=True), cs_rows)
            mv = jnp.concatenate([m0, m1], axis=0).astype(BF16)
            dxdt_parts.append(_dot_tn(mv, _block_diag_pair(dyp).astype(BF16)))
        dxdt = jnp.concatenate(dxdt_parts, axis=1) + dxdt_state
        dg_b = dg.astype(BF16)
        dcm = dcm + _dot(dg_b, bm_b)
        dbm = dbm + _dot_tn(dg_b, cm_b)
        expand_t = v["expand_t"]
        dcs_loc = rs_cols - cs_rows.T
        last_row = _iota((T, 1), 0) == T - 1
        dcs_full_e = dcs_e + jnp.where(last_row, dlast_e, 0.0)
        dcs = _xdot(dcs_full_e, expand_t) + _xdot(dcs_loc, v["select_t"])
        triu = (_iota((T, T), 0) <= _iota((T, T), 1)).astype(BF16)
        da = _xdot_l(triu, dcs)
        ddt = da * v["a_neg"] + _xdot(dxdt * xs, expand_t)
        dxs = dxs + dxdt * v["dt_e"]
        ddtr = jnp.where(v["valid"], ddt * _sigmoid(v["dtr"]), 0.0)
        dx_ref[...] = dxs
        db_ref[...] = jnp.where(v["valid"], dbm, 0.0)
        dc_ref[...] = jnp.where(v["valid"], dcm, 0.0)
        ddt_ref[...] = ddtr
        dpar = jnp.concatenate([
            jnp.sum(ddtr, axis=0, keepdims=True),
            jnp.sum(da * v["dt"], axis=0, keepdims=True) * v["a_neg"],
            _xdot(dd_e, expand_t),
            jnp.zeros((5, 128), F32)], axis=0)

        @pl.when(step == 0)
        def _():
            dpar_ref[...] = dpar

        @pl.when(step > 0)
        def _():
            dpar_ref[...] += dpar

        dst_ref[...] = dhin

    return _call(
        body, "ssd_bwd", (SSM_GROUPS // per, n_chunks), in_specs,
        [pl.BlockSpec((T, per * GROUP_W), lambda g, c: (cidx(c), g)),
         pl.BlockSpec((T, per * D_STATE), lambda g, c: (cidx(c), g)),
         pl.BlockSpec((T, per * D_STATE), lambda g, c: (cidx(c), g)),
         pl.BlockSpec((T, per * 128), lambda g, c: (cidx(c), g)),
         pl.BlockSpec((per, 8, 128), lambda g, c: (g, 0, 0))],
        [jax.ShapeDtypeStruct((n_rows, D_INNER), F32),
         jax.ShapeDtypeStruct((n_rows, SSM_GROUPS * D_STATE), F32),
         jax.ShapeDtypeStruct((n_rows, SSM_GROUPS * D_STATE), F32),
         jax.ShapeDtypeStruct((n_rows, SSM_GROUPS * 128), F32),
         jax.ShapeDtypeStruct((SSM_GROUPS, 8, 128), F32)],
        [p, p, p, p, p, p, p, conv_w, conv_w, conv_w, conv_b, conv_b, conv_b, dt_bias, a_log, d_skip, hin, dy],
        scratch_shapes=[pltpu.VMEM((per, D_STATE, GROUP_W), F32), pltpu.VMEM((per, T + 8, GROUP_W), F32)], bg=bg)


def _alibi_slope(h):
    return 2.0 ** (-8.0 * (h + 1) / ATTN_HEADS)


def _dup_half(x256, kvh):
    xb = x256[:, 128 * (kvh // 2):128 * (kvh // 2) + 128]
    rolled = pltpu.roll(xb, 64, 1)
    lane = _iota(xb.shape, 1)
    if kvh % 2 == 0:
        return jnp.where(lane < 64, xb, rolled)
    return jnp.where(lane < 64, rolled, xb)


def _attn_masks(c):
    qi, j = _iota((T, T), 0), _iota((T, T), 1)
    tri = j <= qi
    meta_ok = (j >= PAD) & (j - PAD <= c * T + qi - PAD)
    band_ok = c >= jnp.where(tri, 1, 2)
    dist = jnp.bitwise_and(qi - j, T - 1).astype(F32)
    return tri, meta_ok, band_ok, dist


def _fold(x3, tri):
    return jnp.concatenate([x3[:, 0:T], jnp.where(tri, x3[:, 2 * T:3 * T], x3[:, T:2 * T])], axis=1)


def _unfold(x2, tri):
    band = x2[:, T:2 * T]
    return jnp.concatenate([x2[:, 0:T], jnp.where(tri, 0.0, band), jnp.where(tri, band, 0.0)], axis=1)


def _attn_scores(qp, k3, masks, h0):
    tri, meta_ok, band_ok, dist = masks
    lane = _iota(qp.shape, 1)
    s = []
    for half, h in ((0, h0), (1, h0 + 1)):
        qh = jnp.where((lane < 64) if half == 0 else (lane >= 64), qp, 0.0).astype(BF16)
        raw = _dot_nt(qh, k3)
        band = jnp.where(tri, raw[:, 2 * T:3 * T], raw[:, T:2 * T]) - _alibi_slope(h) * dist
        s.append((qh, jnp.concatenate([jnp.where(meta_ok, raw[:, 0:T], NEG), jnp.where(band_ok, band, NEG)], axis=1)))
    return s


def _attn_fwd(p, sinks, n_chunks, bg=None):
    n_rows = n_chunks * T
    kb, vb = P_K // KV_W, P_V // KV_W

    def body(q_ref, kc_ref, kp_ref, km_ref, vc_ref, vp_ref, vm_ref, sink_ref, o_ref, lse_ref):
        c = pl.program_id(0)
        sinks_v = sink_ref[...]
        masks = _attn_masks(c)
        tri, meta_ok, band_ok, dist = masks
        lane = _iota((T, 128), 1)
        for kvh in range(KV_HEADS):
            k3 = jnp.concatenate([_dup_half(r[...], kvh) for r in (km_ref, kp_ref, kc_ref)], axis=0).astype(BF16)
            v3 = jnp.concatenate([_dup_half(r[...], kvh) for r in (vm_ref, vp_ref, vc_ref)], axis=0)
            v3bd = _block_diag_rows(v3).astype(BF16)
            q2 = q_ref[:, 256 * kvh:256 * kvh + 256] * SCALE
            q4 = jnp.concatenate([jnp.where((lane < 64) if half == 0 else (lane >= 64), q2[:, 128 * pr:128 * pr + 128], 0.0)
                                  for pr in range(2) for half in range(2)], axis=0).astype(BF16)
            raw4 = _dot_nt(q4, k3)
            probs = []
            for hh in range(4):
                h = 4 * kvh + hh
                raw = raw4[T * hh:T * hh + T]
                band = jnp.where(tri, raw[:, 2 * T:3 * T], raw[:, T:2 * T]) - _alibi_slope(h) * dist
                sc = jnp.concatenate([jnp.where(meta_ok, raw[:, 0:T], NEG), jnp.where(band_ok, band, NEG)], axis=1)
                sink = sinks_v[:, h:h + 1]
                m = jnp.maximum(jnp.max(sc, axis=1, keepdims=True), sink)
                e = jnp.exp(sc - m)
                den = jnp.sum(e, axis=1, keepdims=True) + jnp.exp(sink - m)
                probs.append(_unfold(e * (1.0 / den), tri))
                lse_ref[:, h:h + 1] = m + jnp.log(den)
            p4 = jnp.concatenate([jnp.concatenate(probs[0:2], axis=1), jnp.concatenate(probs[2:4], axis=1)], axis=0)
            out = _dot(p4.astype(BF16), v3bd)
            o_ref[:, 256 * kvh:256 * kvh + 256] = jnp.concatenate([out[0:T], out[T:2 * T]], axis=1).astype(o_ref.dtype)

    blk = lambda width, col: pl.BlockSpec((T, width), lambda c: (c, col))
    prev = lambda width, col: pl.BlockSpec((T, width), lambda c: (jnp.maximum(c - 1, 0), col))
    first = lambda width, col: pl.BlockSpec((T, width), lambda c: (0, col))
    return _call(
        body, "attn_fwd", (n_chunks,),
        [blk(ATTN_W, P_Q // ATTN_W), blk(KV_W, kb), prev(KV_W, kb), first(KV_W, kb),
         blk(KV_W, vb), prev(KV_W, vb), first(KV_W, vb), pl.BlockSpec((1, 128), lambda c: (0, 0))],
        [pl.BlockSpec((T, ATTN_W), lambda c: (c, 0)), pl.BlockSpec((T, 128), lambda c: (c, 0))],
        [jax.ShapeDtypeStruct((n_rows, ATTN_W), BF16), jax.ShapeDtypeStruct((n_rows, 128), F32)],
        [p, p, p, p, p, p, p, sinks], bg=bg)


def _block_diag_rows(x3):
    lane = _iota(x3.shape, 1)
    return jnp.concatenate([jnp.where(lane < 64, x3, 0.0), jnp.where(lane >= 64, x3, 0.0)], axis=0)


def _fold_halves(x):
    return x + pltpu.roll(x, 64, 1)


def _attn_bwd(p, sinks, ao, lse, dao, dp, n_chunks, bg=None):
    kb, vb = P_K // KV_W, P_V // KV_W
    rc = lambda s: n_chunks - 1 - s

    def body(q_ref, kc_ref, kp_ref, km_ref, vc_ref, vp_ref, vm_ref, sink_ref, o_ref, lse_ref, do_ref, dp_in_ref,
             dqkv_ref, dsink_ref, kcar_ref, vcar_ref, kmeta_ref, vmeta_ref):
        step = pl.program_id(0)
        c = n_chunks - 1 - step

        @pl.when(step == 0)
        def _():
            for r in (kcar_ref, vcar_ref, kmeta_ref, vmeta_ref):
                r[...] = jnp.zeros_like(r)

        masks = _attn_masks(c)
        tri = masks[0]
        q = q_ref[...] * SCALE
        sinks_v = sink_ref[...]
        lse_v = lse_ref[...]
        ov = o_ref[...].astype(F32)
        dov = do_ref[...].astype(F32)
        lane = _iota((T, 128), 1)
        lane256 = _iota((3 * T, KV_W), 1)
        dsink = jnp.zeros((1, 128), F32)
        dk3_all = jnp.zeros((3 * T, KV_W), F32)
        dv3_all = jnp.zeros((3 * T, KV_W), F32)
        dqs = []
        for kvh in range(KV_HEADS):
            k3 = jnp.concatenate([_dup_half(r[...], kvh) for r in (km_ref, kp_ref, kc_ref)], axis=0).astype(BF16)
            v3 = jnp.concatenate([_dup_half(r[...], kvh) for r in (vm_ref, vp_ref, vc_ref)], axis=0).astype(BF16)
            dk3 = jnp.zeros((3 * T, 128), F32)
            dv3 = jnp.zeros((3 * T, 128), F32)
            for pr in range(2):
                h0 = 4 * kvh + 2 * pr
                blk = 2 * kvh + pr
                qp = q[:, 128 * blk:128 * blk + 128]
                dop = dov[:, 128 * blk:128 * blk + 128]
                prod = dop * ov[:, 128 * blk:128 * blk + 128]
                dq_pair = jnp.zeros((T, 128), F32)
                for half, ((qh, sc), h) in enumerate(zip(_attn_scores(qp, k3, masks, h0), (h0, h0 + 1))):
                    mine = (lane < 64) if half == 0 else (lane >= 64)
                    lse_h = lse_v[:, h:h + 1]
                    pm = jnp.exp(sc - lse_h)
                    doh = jnp.where(mine, dop, 0.0).astype(BF16)
                    delta = jnp.sum(jnp.where(mine, prod, 0.0), axis=1, keepdims=True)
                    dp = _fold(_dot_nt(doh, v3), tri)
                    ds = _unfold(pm * (dp - delta), tri).astype(BF16)
                    p_sink = jnp.exp(sinks_v[:, h:h + 1] - lse_h)
                    dsink = jnp.where(_iota((1, 128), 1) == h, jnp.sum(-p_sink * delta, axis=0, keepdims=True), dsink)
                    dq_pair = jnp.where(mine, _dot(ds, k3), dq_pair)
                    dk3 = dk3 + _dot_tn(ds, qh)
                    dv3 = dv3 + _dot_tn(_unfold(pm, tri).astype(BF16), doh)
                dqs.append(dq_pair * SCALE)
            in_place = (lane256 >= 64 * kvh) & (lane256 < 64 * kvh + 64)
            wide = lambda x: jnp.concatenate([x, x], axis=1)
            dk3_all = jnp.where(in_place, wide(_fold_halves(dk3)), dk3_all)
            dv3_all = jnp.where(in_place, wide(_fold_halves(dv3)), dv3_all)
        dsink_all = dsink

        @pl.when(step == 0)
        def _():
            dsink_ref[...] = dsink_all

        @pl.when(step > 0)
        def _():
            dsink_ref[...] += dsink_all

        kmeta = kmeta_ref[...] + dk3_all[0:T]
        vmeta = vmeta_ref[...] + dv3_all[0:T]
        kmeta_ref[...] = kmeta
        vmeta_ref[...] = vmeta
        is_first = c == 0
        dk = jnp.where(is_first, kmeta, dk3_all[2 * T:3 * T] + kcar_ref[...])
        dv = jnp.where(is_first, vmeta, dv3_all[2 * T:3 * T] + vcar_ref[...])
        dqkv_ref[...] = jnp.concatenate(dqs + [dk, dv], axis=1).astype(dqkv_ref.dtype)
        kcar_ref[...] = dk3_all[T:2 * T]
        vcar_ref[...] = dv3_all[T:2 * T]

    blk = lambda width, col: pl.BlockSpec((T, width), lambda s: (rc(s), col))
    prev = lambda width, col: pl.BlockSpec((T, width), lambda s: (jnp.maximum(rc(s) - 1, 0), col))
    first = lambda width, col: pl.BlockSpec((T, width), lambda s: (0, col))
    return _call(
        body, "attn_bwd", (n_chunks,),
        [blk(ATTN_W, P_Q // ATTN_W), blk(KV_W, kb), prev(KV_W, kb), first(KV_W, kb),
         blk(KV_W, vb), prev(KV_W, vb), first(KV_W, vb), pl.BlockSpec((1, 128), lambda s: (0, 0)),
         blk(ATTN_W, 0), blk(128, 0), blk(ATTN_W, 0), ANY],
        [blk(QKV_W, P_Q // QKV_W), pl.BlockSpec((1, 128), lambda s: (0, 0))],
        [jax.ShapeDtypeStruct(dp.shape, dp.dtype), jax.ShapeDtypeStruct((1, 128), F32)],
        [p, p, p, p, p, p, p, sinks, ao, lse, dao, dp],
        scratch_shapes=[pltpu.VMEM((T, KV_W), F32)] * 4, aliases={11: 0}, bg=bg)


def _pad_lanes(v, width=128):
    return jnp.pad(v, ((0, 0), (0, width - v.shape[1])))


def _local_step(x, head, tgt, plan):
    w, g, run = plan.w, plan.g, plan.run
    n_tok = x.shape[0]
    n_rows = n_tok + T
    n_chunks = n_rows // T
    tm = _row_tile(n_rows, 384)
    dt_bias, a_log, d_skip = (_pad_lanes(w[k]) for k in ("ssm_dt_bias", "ssm_a_log", "ssm_d_skip"))
    sinks = _pad_lanes(w["attn_sinks"])
    x_in = [(x, D_MODEL, 0, "prev"), (head, D_MODEL, 0, "first")]

    def h0_tile(r0, xt, hd):
        return jnp.where(r0 < T, hd, xt)

    n1, = _rowwise("norm_pre_mix", lambda r0, xt, hd, wn: [_rms(h0_tile(r0, xt, hd), wn)], n_rows, T,
                   x_in, [w["norm_pre_mix"]], [(D_MODEL, BF16)], [])
    p = _matmul("in_proj", n1, w["w_cat"], "nn", F32)
    y_ssd, hin = run("ssd_fwd", _ssd_fwd, p, w["ssm_conv_w"], w["ssm_conv_b"], dt_bias, a_log, d_skip, n_chunks)
    ao, lse = run("attn_fwd", _attn_fwd, p, sinks, n_chunks)

    def gate_norm(r0, y, z, wn):
        return [_rms(y * _silu(z), wn)]

    yn, = run("ssm_gate_norm", _rowwise, "ssm_gate_norm", gate_norm, n_rows, tm,
              [(y_ssd, D_INNER, 0), (p, D_INNER, P_Z // D_INNER)], [w["ssm_norm"]], [(D_INNER, BF16)], [])
    y_ssm = _matmul("ssm_out", yn, w["w_ssm_out"], "nn", F32)
    y_attn = _matmul("attn_out", ao, w["w_attn_out"], "nn", F32)

    def mix_gate(r0, ys, ya, gs, ga):
        return [_sigmoid(gs) * ys + _sigmoid(ga) * ya]

    gate_ins = [(p, D_MODEL, P_GATE // D_MODEL), (p, D_MODEL, P_GATE // D_MODEL + 1)]
    mixed, = _rowwise("mix_gate", mix_gate, n_rows, tm, [(y_ssm, D_MODEL, 0), (y_attn, D_MODEL, 0)] + gate_ins,
                      [], [(D_MODEL, BF16)], [])
    mix = _matmul("mix_out", mixed, w["w_mix_out"], "nn", F32)

    def post_mix(r0, mx, xt, hd, w_post, w_pre):
        h1 = jnp.where(_valid_rows(r0, mx.shape[0], PAD), h0_tile(r0, xt, hd) + _rms(mx, w_post), 0.0)
        return [h1, _rms(h1, w_pre)]

    h1, n2 = _rowwise("post_mix", post_mix, n_rows, T, [(mix, D_MODEL, 0)] + x_in,
                      [w["norm_post_mix"], w["norm_pre_ffn"]], [(D_MODEL, F32), (D_MODEL, BF16)], [])
    u_raw = _matmul("ffn_up", n2, w["w_ffn_up"], "nn", F32)
    f = _ffn_act("ffn_act", u_raw, w["ffn_conv_w"], w["ffn_conv_b"], n_rows)
    ffn = _matmul("ffn_down", f, w["w_ffn_down"], "nn", F32)

    def final(r0, fo, h, t, w_post):
        real = r0 >= T
        err = jnp.where(real, h + _rms(fo, w_post) - t, 0.0)
        dy = err * (1.0 / D_MODEL)
        dffn, dw = _rms_bwd(dy, fo, w_post)
        return [dffn, dy, jnp.sum(err * err, axis=0, keepdims=True), dw]

    dffn, dh2, loss_cols, g_norm_post_ffn = _rowwise(
        "loss_head", final, n_rows, T, [(ffn, D_MODEL, 0), (h1, D_MODEL, 0), (tgt, D_MODEL, 0, "prev")],
        [w["norm_post_ffn"]], [(D_MODEL, BF16), (D_MODEL, F32)], [D_MODEL, D_MODEL])

    g["norm_post_ffn"] = g_norm_post_ffn
    g["w_ffn_down"] = _matmul("ffn_down_dw", f, dffn, "tn", F32)
    df = _matmul("ffn_down_dx", dffn, w["w_ffn_down"], "nt", F32)
    du_raw, dconv = _conv_bwd("ffn_act_bwd", u_raw, 0, [df], [(0, c0) for c0 in range(0, FFN_DIM, CONV_LANES)],
                              w["ffn_conv_w"], w["ffn_conv_b"], n_rows, True)
    g["ffn_conv_w"], g["ffn_conv_b"] = dconv[0:3], dconv[3:4]
    g["w_ffn_up"] = _matmul("ffn_up_dw", n2, du_raw, "tn", F32)
    dn2 = run("ffn_up_dx", _matmul, "ffn_up_dx", du_raw, w["w_ffn_up"], "nt", F32)

    def post_mix_bwd(r0, dn, d2, h, mx, w_pre, w_post):
        dx, dw_pre = _rms_bwd(dn, h, w_pre)
        dh1 = jnp.where(_valid_rows(r0, dn.shape[0], PAD), dx + d2, 0.0)
        dmix, dw_post = _rms_bwd(dh1, mx, w_post)
        return [dh1, dmix, dw_pre, dw_post]

    dh1, dmix, g["norm_pre_ffn"], g["norm_post_mix"] = _rowwise(
        "post_mix_bwd", post_mix_bwd, n_rows, tm,
        [(dn2, D_MODEL, 0), (dh2, D_MODEL, 0), (h1, D_MODEL, 0), (mix, D_MODEL, 0)],
        [w["norm_pre_ffn"], w["norm_post_mix"]], [(D_MODEL, F32), (D_MODEL, BF16)], [D_MODEL, D_MODEL])
    g["w_mix_out"] = _matmul("mix_out_dw", mixed, dmix, "tn", F32)
    dmixed = _matmul("mix_out_dx", dmix, w["w_mix_out"], "nt", F32)

    def mix_gate_bwd(r0, dm, ys, ya, gs, ga):
        ss, sa = _sigmoid(gs), _sigmoid(ga)
        dgate = jnp.concatenate([dm * ys * ss * (1.0 - ss), dm * ya * sa * (1.0 - sa)], axis=1)
        return [dm * ss, dm * sa, dgate]

    dys, dya, dp = _rowwise(
        "mix_gate_bwd", mix_gate_bwd, n_rows, tm,
        [(dmixed, D_MODEL, 0), (y_ssm, D_MODEL, 0), (y_attn, D_MODEL, 0)] + gate_ins,
        [], [(D_MODEL, BF16), (D_MODEL, BF16), (2 * D_MODEL, BF16, "new", P_W, P_GATE // (2 * D_MODEL))], [])
    g["w_ssm_out"] = _matmul("ssm_out_dw", yn, dys, "tn", F32)
    dyn = _matmul("ssm_out_dx", dys, w["w_ssm_out"], "nt", F32)
    g["w_attn_out"] = _matmul("attn_out_dw", ao, dya, "tn", F32)
    dao = _matmul("attn_out_dx", dya, w["w_attn_out"], "nt", BF16)

    def gate_norm_bwd(r0, dn, y, z, wn):
        sz, dsz = _silu_grad(z)
        dyz, dw = _rms_bwd(dn, y * sz, wn)
        live = _valid_rows(r0, dn.shape[0], PAD)
        return [jnp.where(live, dyz * sz, 0.0), jnp.where(live, dyz * y * dsz, 0.0), dw]

    dy_ssd, dp, g["ssm_norm"] = run(
        "ssm_gate_norm_bwd", _rowwise, "ssm_gate_norm_bwd", gate_norm_bwd, n_rows, tm,
        [(dyn, D_INNER, 0), (y_ssd, D_INNER, 0), (p, D_INNER, P_Z // D_INNER)],
        [w["ssm_norm"]], [(D_INNER, F32), (D_INNER, BF16, "into", dp, P_Z // D_INNER)], [D_INNER])
    dp, dsink = run("attn_bwd", _attn_bwd, p, sinks, ao, lse, dao, dp, n_chunks)
    g["attn_sinks"] = dsink[:, 0:ATTN_HEADS]
    dxs, dbm, dcm, ddt_parts, dpar = run("ssd_bwd", _ssd_bwd, p, w["ssm_conv_w"], w["ssm_conv_b"], dt_bias, a_log,
                                         d_skip, hin, dy_ssd, n_chunks)
    dpar = jnp.sum(dpar, axis=0)
    g["ssm_dt_bias"], g["ssm_a_log"], g["ssm_d_skip"] = (dpar[i:i + 1, 0:SSM_HEADS] for i in range(3))

    def dt_grad(r0, parts):
        tot = parts[:, 0:128] + parts[:, 128:256] + parts[:, 256:384] + parts[:, 384:512]
        return [jnp.concatenate([tot, jnp.zeros((parts.shape[0], P_Z - P_DT - 128), F32)], axis=1)]

    dt_w = P_Z - P_DT
    dp, = _rowwise("dt_grad", dt_grad, n_rows, tm, [(ddt_parts, SSM_GROUPS * 128, 0)], [],
                   [(dt_w, BF16, "into", dp, P_DT // dt_w)], [])
    x_chunks = [(src, c0) for src, arr in enumerate((dxs, dbm, dcm)) for c0 in range(0, arr.shape[1], CONV_LANES)]
    dp, dconv = run("ssm_conv_bwd", _conv_bwd, "ssm_conv_bwd", p, P_XBC // CONV_DIM, [dxs, dbm, dcm], x_chunks,
                    w["ssm_conv_w"], w["ssm_conv_b"], n_rows, False, into=dp, into_blk=P_XBC // CONV_DIM)
    g["ssm_conv_w"], g["ssm_conv_b"] = dconv[0:4], dconv[4:5]
    g["w_cat"] = _matmul("in_proj_dw", n1, dp, "tn", F32)
    dn1 = run("in_proj_dx", _matmul, "in_proj_dx", dp, w["w_cat"], "nt", F32)

    def pre_mix_bwd(r0, dn, d1, xt, hd, wn):
        dx, dw = _rms_bwd(dn, h0_tile(r0, xt, hd), wn)
        dh0 = jnp.where(_valid_rows(r0, dn.shape[0], PAD), dx + d1, 0.0)
        return [dh0, dh0, dw]

    dx_out, dhead, g["norm_pre_mix"] = _rowwise(
        "pre_mix_bwd", pre_mix_bwd, n_rows, T, [(dn1, D_MODEL, 0), (dh1, D_MODEL, 0)] + x_in,
        [w["norm_pre_mix"]], [(D_MODEL, F32, "prev", n_tok), (D_MODEL, F32, "first")], [D_MODEL])
    return jnp.sum(loss_cols), dx_out, dhead


_IN_SECTIONS = [((5152, 6176), P_Q), ((6176, 6432), P_K), ((6432, 6688), P_V), ((5120, 5152), P_DT),
                ((0, 2048), P_Z), ((6688, 8736), P_GATE), ((2048, 5120), P_XBC)]


IN_SHARD = N_IN // 4


def _shard_pieces(a, b):
    return [(j, max(a, j * IN_SHARD) - j * IN_SHARD, min(b, (j + 1) * IN_SHARD) - j * IN_SHARD)
            for j in range(4) if max(a, j * IN_SHARD) < min(b, (j + 1) * IN_SHARD)]


def _to_cat(w4):
    parts, at = [], 0
    for (a, b), off in _IN_SECTIONS:
        if off > at:
            parts.append(jnp.zeros((w4.shape[1], off - at), w4.dtype))
        parts += [w4[j, :, lo:hi] for j, lo, hi in _shard_pieces(a, b)]
        at = off + (b - a)
    return jnp.concatenate(parts, axis=1)


def _from_cat(g_cat):
    shards = [[] for _ in range(4)]
    for (a, b), off in sorted(_IN_SECTIONS):
        for j, lo, hi in _shard_pieces(a, b):
            start = off + j * IN_SHARD + lo - a
            shards[j].append(g_cat[:, start:start + hi - lo])
    return jnp.stack([jnp.concatenate(s, axis=1) for s in shards])


LANES = 1024
_BIG = [("w_in", 1024, 2184, "chip"), ("w_ssm_out", 512, 1024, "row"), ("w_attn_out", 256, 1024, "row"),
        ("w_mix_out", 256, 1024, "row"), ("w_ffn_up", 1024, 1408, "col"), ("w_ffn_down", 704, 1024, "row"),
        ("small", 32, LANES, "chip")]
_SMALL_SHARDED = [("ssm_conv_w", (4, 768), 1), ("ffn_conv_w", (3, 1408), 1), ("meta_tokens", (16, 256), 1)]
_REPLICATED = [("norm_pre_mix", 1024), ("ssm_conv_b", 3072), ("ssm_dt_bias", 32), ("ssm_a_log", 32),
               ("ssm_d_skip", 32), ("ssm_norm", 2048), ("attn_sinks", 16), ("norm_post_mix", 1024),
               ("norm_pre_ffn", 1024), ("ffn_conv_b", 5632), ("norm_post_ffn", 1024)]
SMALL_ROWS = 16
WEIGHT_ORDER = ["meta_tokens", "norm_pre_mix", "w_in", "ssm_conv_w", "ssm_conv_b", "ssm_dt_bias", "ssm_a_log",
                "ssm_d_skip", "ssm_norm", "w_ssm_out", "attn_sinks", "w_attn_out", "w_mix_out", "norm_post_mix",
                "norm_pre_ffn", "w_ffn_up", "ffn_conv_w", "ffn_conv_b", "w_ffn_down", "norm_post_ffn"]


def _flatten(parts, rows):
    flat = jnp.concatenate([a.reshape(-1) for a in parts])
    return jnp.pad(flat, (0, rows * LANES - flat.shape[0])).reshape(rows, LANES)


def _unflatten(flat, shapes):
    flat = flat.reshape(-1)
    out, off = [], 0
    for shp in shapes:
        n = math.prod(shp)
        out.append(flat[off:off + n].reshape(shp))
        off += n
    return out


def _shard_of(full, chip, shape, axis):
    return lax.slice_in_dim(full, chip * shape[axis], (chip + 1) * shape[axis], axis=axis)


def _full_shape(r, c, layout):
    return {"row": (4 * r, c), "col": (r, 4 * c), "chip": (4, r, c)}[layout]


def _shard_view(ref, r, c, layout, chip):
    if layout == "row":
        return ref.at[pl.ds(pl.multiple_of(chip * r, 16), r), :]
    if layout == "col":
        return ref.at[:, pl.ds(pl.multiple_of(chip * c, 128), c)]
    return ref.at[chip]


def _half_view(ref, r, c, layout, chip, half):
    hr = r // 2
    if layout == "row":
        return ref.at[pl.ds(pl.multiple_of(chip * r + half * hr, 16), hr), :]
    r0 = pl.multiple_of(half * hr, 16)
    if layout == "col":
        return ref.at[pl.ds(r0, hr), pl.ds(pl.multiple_of(chip * c, 128), c)]
    return ref.at[chip, pl.ds(r0, hr), :]


def _mesh_pos():
    return lax.axis_index("x"), lax.axis_index("y"), lax.axis_index("c")


def _other_chips(x, y):
    return [(1 - x, y), (x, 1 - y), (1 - x, 1 - y)]


def _chip_index(x, y):
    return 2 * x + y


def _run_exchange(name, ex):
    n_in, n_out = len(ex.ins), len(ex.out_shapes)

    def body(*refs):
        in_refs, out_refs = refs[:n_in], refs[n_in:n_in + n_out]
        send_sems, recv_sems = refs[n_in + n_out:]
        copies = [pltpu.make_async_remote_copy(src_ref=s, dst_ref=d, send_sem=send_sems.at[i], recv_sem=recv_sems.at[i],
                                               device_id=dev, device_id_type=MESH)
                  for i, (s, d, dev) in enumerate(ex.make_copies(in_refs, out_refs))]
        assert len(copies) == ex.n_copies
        for cp in copies:
            cp.start()
        for cp in copies:
            cp.wait()

    return pl.pallas_call(
        body, name=name, in_specs=[ANY] * n_in, out_specs=[ANY] * n_out, out_shape=list(ex.out_shapes),
        scratch_shapes=[pltpu.SemaphoreType.DMA((ex.n_copies,)), pltpu.SemaphoreType.DMA((ex.n_copies,))],
        compiler_params=pltpu.CompilerParams(has_side_effects=True),
    )(*ex.ins)


def _join(*exs):
    def make(in_refs, out_refs):
        copies, i0, o0 = [], 0, 0
        for ex in exs:
            copies += ex.make_copies(in_refs[i0:i0 + len(ex.ins)], out_refs[o0:o0 + len(ex.out_shapes)])
            i0, o0 = i0 + len(ex.ins), o0 + len(ex.out_shapes)
        return copies

    aliases, i0, o0 = {}, 0, 0
    for ex in exs:
        aliases.update({i0 + k: o0 + v for k, v in ex.aliases.items()})
        i0, o0 = i0 + len(ex.ins), o0 + len(ex.out_shapes)
    return _Exchange([a for ex in exs for a in ex.ins], [s for ex in exs for s in ex.out_shapes], make,
                     sum(ex.n_copies for ex in exs), aliases)


def _split(exs, results):
    out, o0 = [], 0
    for ex in exs:
        out.append(list(results[o0:o0 + len(ex.out_shapes)]))
        o0 += len(ex.out_shapes)
    return out


def _gather_ici(entries, shards):
    def make(in_refs, out_refs):
        x, y, c = _mesh_pos()
        j = _chip_index(x, y)
        copies = []
        for ref_in, ref_out, (_, r, cc, lay) in zip(in_refs, out_refs, entries):
            copies.append((ref_in, _shard_view(ref_out, r, cc, lay, j), None))
            mine = ref_in.at[pl.ds(pl.multiple_of(c * (r // 2), 16), r // 2), :]
            copies += [(mine, _half_view(ref_out, r, cc, lay, j, c), (*ch, c)) for ch in _other_chips(x, y)]
        return copies

    shapes = [jax.ShapeDtypeStruct(_full_shape(r, cc, lay), s.dtype) for s, (_, r, cc, lay) in zip(shards, entries)]
    return _Exchange(list(shards), shapes, make, 4 * len(entries))


def _gather_pass_on(entries, fulls):
    def make(in_refs, out_refs):
        x, y, c = _mesh_pos()
        copies = []
        for ref, (_, r, cc, lay) in zip(out_refs, entries):
            for ch in _other_chips(x, y):
                landed = _half_view(ref, r, cc, lay, _chip_index(*ch), c)
                copies.append((landed, landed, (x, y, 1 - c)))
        return copies

    return _Exchange(list(fulls), [jax.ShapeDtypeStruct(f.shape, f.dtype) for f in fulls], make, 3 * len(entries),
                     {a: a for a in range(len(entries))})


def _gather_weights(entries, shards):
    n = len(entries)

    def body(*refs):
        ins, outs = refs[:n], refs[n:2 * n]
        send_sems, recv_sems, local_sems = refs[2 * n:]
        x, y, c = _mesh_pos()
        j = _chip_index(x, y)
        sibling = (x, y, 1 - c)
        chips = _other_chips(x, y)
        idx = [_chip_index(*ch) for ch in chips]

        def remote(k, src, dst, dev):
            return pltpu.make_async_remote_copy(src_ref=src, dst_ref=dst, send_sem=send_sems.at[k],
                                                recv_sem=recv_sems.at[k], device_id=dev, device_id_type=MESH)

        own = [pltpu.make_async_copy(ins[a], _shard_view(outs[a], r, cc, lay, j), local_sems.at[a])
               for a, (_, r, cc, lay) in enumerate(entries)]
        for cp in own:
            cp.start()
        first, passed = [], []
        for a, (_, r, cc, lay) in enumerate(entries):
            mine = ins[a].at[pl.ds(pl.multiple_of(c * (r // 2), 16), r // 2), :]
            for k, ch in enumerate(chips):
                first.append(remote(6 * a + k, mine, _half_view(outs[a], r, cc, lay, j, c), (*ch, c)))
                landed = _half_view(outs[a], r, cc, lay, idx[k], c)
                passed.append(remote(6 * a + 3 + k, landed, landed, sibling))
        for cp in first:
            cp.start()
        for a, (_, r, cc, lay) in enumerate(entries):
            for k in range(3):
                landed = _half_view(outs[a], r, cc, lay, idx[k], c)
                remote(6 * a + k, landed, landed, sibling).wait_recv()
                passed[3 * a + k].start()
        for a, (_, r, cc, lay) in enumerate(entries):
            for k in range(3):
                theirs = _half_view(outs[a], r, cc, lay, idx[k], 1 - c)
                remote(6 * a + 3 + k, theirs, theirs, sibling).wait_recv()
        for cp in first + passed:
            cp.wait_send()
        for cp in own:
            cp.wait()

    return pl.pallas_call(
        body, name="gather_weights", in_specs=[ANY] * n, out_specs=[ANY] * n,
        out_shape=[jax.ShapeDtypeStruct(_full_shape(r, cc, lay), s.dtype) for s, (_, r, cc, lay) in zip(shards, entries)],
        scratch_shapes=[pltpu.SemaphoreType.DMA((6 * n,)), pltpu.SemaphoreType.DMA((6 * n,)), pltpu.SemaphoreType.DMA((n,))],
        compiler_params=pltpu.CompilerParams(has_side_effects=True),
    )(*shards)


def _pair_exchange(entries, grads):
    def make(in_refs, out_refs):
        x, y, c = _mesh_pos()
        return [(_half_view(ref_in, r, cc, lay, i, 1 - c), ref_out.at[i], (x, y, 1 - c))
                for ref_in, ref_out, (_, r, cc, lay) in zip(in_refs, out_refs, entries) for i in range(4)]

    return _Exchange(list(grads), [jax.ShapeDtypeStruct((4, r // 2, cc), F32) for _, r, cc, _ in entries], make,
                     4 * len(entries))


def _whole_to_sibling(arrays):
    def make(in_refs, out_refs):
        x, y, c = _mesh_pos()
        return [(r, o, (x, y, 1 - c)) for r, o in zip(in_refs, out_refs)]

    return _Exchange(list(arrays), [jax.ShapeDtypeStruct(a.shape, a.dtype) for a in arrays], make, len(arrays))


def _chip_exchange(psends):
    def make(in_refs, out_refs):
        x, y, c = _mesh_pos()
        return [(ref_in.at[_chip_index(*ch)], ref_out.at[k], (*ch, c))
                for ref_in, ref_out in zip(in_refs, out_refs) for k, ch in enumerate(_other_chips(x, y))]

    return _Exchange(list(psends), [jax.ShapeDtypeStruct((3,) + p.shape[1:], p.dtype) for p in psends], make,
                     3 * len(psends))


def _to_all_chips(array):
    def make(in_refs, out_refs):
        x, y, c = _mesh_pos()
        return [(in_refs[0], out_refs[0].at[k], (*ch, c)) for k, ch in enumerate(_other_chips(x, y))]

    return _Exchange([array], [jax.ShapeDtypeStruct((3,) + array.shape, array.dtype)], make, 3)


SUM_ROWS = 256
ADAM_ROWS = 128


def _pair_sum(name, grad, recv, ids, r, c, layout):
    hr = r // 2
    tr = _row_tile(hr, SUM_ROWS)
    nb = hr // tr

    def body(ids_ref, g_ref, r_ref, send_ref, own_ref):
        s = g_ref[...] + r_ref[...]
        send_ref[...] = s.astype(send_ref.dtype)

        @pl.when(pl.program_id(1) == ids_ref[1])
        def _():
            own_ref[...] = s

    if layout == "row":
        g_spec = pl.BlockSpec((tr, c), lambda t, j, ids_ref: ((j * r + ids_ref[0] * hr) // tr + t, 0))
    elif layout == "col":
        g_spec = pl.BlockSpec((tr, c), lambda t, j, ids_ref: (ids_ref[0] * nb + t, j))
    else:
        g_spec = pl.BlockSpec((None, tr, c), lambda t, j, ids_ref: (j, ids_ref[0] * nb + t, 0))
    grid_spec = pltpu.PrefetchScalarGridSpec(
        num_scalar_prefetch=1, grid=(nb, 4),
        in_specs=[g_spec, pl.BlockSpec((None, tr, c), lambda t, j, ids_ref: (j, t, 0))],
        out_specs=[pl.BlockSpec((None, tr, c), lambda t, j, ids_ref: (j, t, 0)),
                   pl.BlockSpec((tr, c), lambda t, j, ids_ref: (t, 0))])
    return pl.pallas_call(
        body, name=name, grid_spec=grid_spec,
        out_shape=[jax.ShapeDtypeStruct((4, hr, c), BF16), jax.ShapeDtypeStruct((hr, c), F32)],
        compiler_params=_cparams(2),
    )(ids, grad, recv)


def _chip_sum(name, own, recv):
    hr, c = own.shape
    tr = _row_tile(hr, SUM_ROWS)

    def body(o_ref, r_ref, out_ref):
        out_ref[...] = ((o_ref[...] + r_ref[0].astype(F32)) + r_ref[1].astype(F32)) + r_ref[2].astype(F32)

    return pl.pallas_call(
        body, name=name, grid=(hr // tr,),
        in_specs=[pl.BlockSpec((tr, c), lambda i: (i, 0)), pl.BlockSpec((3, tr, c), lambda i: (0, i, 0))],
        out_specs=pl.BlockSpec((tr, c), lambda i: (i, 0)),
        out_shape=jax.ShapeDtypeStruct((hr, c), F32), compiler_params=_cparams(1),
    )(own, recv)


def _chip_sum_small(own, recv, ids):
    def body(ids_ref, o_ref, r_ref, out_ref):
        j = ids_ref[1]
        total = None
        for i in range(4):
            m = jnp.bitwise_xor(i, j)
            term = jnp.where(m == 0, o_ref[...], jnp.where(m == 2, r_ref[0], jnp.where(m == 1, r_ref[1], r_ref[2])))
            total = term if total is None else total + term
        out_ref[...] = total

    grid_spec = pltpu.PrefetchScalarGridSpec(
        num_scalar_prefetch=1, grid=(1,),
        in_specs=[pl.BlockSpec(own.shape, lambda i, ids_ref: (0, 0)), pl.BlockSpec(recv.shape, lambda i, ids_ref: (0, 0, 0))],
        out_specs=pl.BlockSpec(own.shape, lambda i, ids_ref: (0, 0)))
    return pl.pallas_call(body, name="chip_sum_small", grid_spec=grid_spec,
                          out_shape=jax.ShapeDtypeStruct(own.shape, F32), compiler_params=_cparams(1))(ids, own, recv)


def _adamw(name, w, m, v, mine, theirs, ids):
    rows, cols = w.shape
    half = rows // 2
    tr = _row_tile(half, ADAM_ROWS, unit=8)
    nb = half // tr
    c1 = 1.0 / (1.0 - ADAM_B1 ** ADAM_STEP)
    c2 = 1.0 / (1.0 - ADAM_B2 ** ADAM_STEP)

    def body(ids_ref, w_ref, m_ref, v_ref, mine_ref, theirs_ref, g_out, d_out, m_out, v_out):
        g = jnp.where(pl.program_id(0) == ids_ref[0], mine_ref[...], theirs_ref[...])
        m_new = ADAM_B1 * m_ref[...] + (1.0 - ADAM_B1) * g
        v_new = ADAM_B2 * v_ref[...] + (1.0 - ADAM_B2) * (g * g)
        d_out[...] = -ADAM_LR * ((m_new * c1) / (jnp.sqrt(v_new * c2) + ADAM_EPS) + ADAM_WD * w_ref[...])
        g_out[...] = g
        m_out[...] = m_new
        v_out[...] = v_new

    full = pl.BlockSpec((tr, cols), lambda h, i, ids_ref: (h * nb + i, 0))
    part = pl.BlockSpec((tr, cols), lambda h, i, ids_ref: (i, 0))
    grid_spec = pltpu.PrefetchScalarGridSpec(num_scalar_prefetch=1, grid=(2, nb),
                                             in_specs=[full, full, full, part, part], out_specs=[full] * 4)
    return pl.pallas_call(
        body, name=name, grid_spec=grid_spec,
        out_shape=[jax.ShapeDtypeStruct((rows, cols), F32)] * 4, compiler_params=_cparams(2),
    )(ids, w, m, v, mine, theirs)


def _small_shard(parts):
    return _flatten(parts, _BIG[-1][1])


_ENTRY = {e[0]: e for e in _BIG}
FFN_MATS = ("w_ffn_down", "w_ffn_up")
MIXER_MATS = ("w_mix_out", "w_ssm_out", "w_attn_out")


class _StepPlan:
    def __init__(self, w, late_shards, shards, ids):
        self.w, self.g = w, {}
        self.late_shards, self.shards, self.ids = late_shards, shards, ids
        self.sums, self.halves, self.results = {}, {}, {}

    def run(self, name, fn, *args, **kw):
        at = getattr(self, "_at_" + name, None)
        if at is None:
            return fn(*args, **kw)
        exchange, landed = at()
        res, extra = fn(*args, bg=exchange, **kw)
        landed(extra)
        return res

    def _at_ssd_fwd(self):
        def landed(fulls):
            self.partly_gathered = fulls

        return _gather_ici([_ENTRY[n] for n in MIXER_MATS], [self.late_shards[n] for n in MIXER_MATS]), landed

    def _at_attn_fwd(self):
        stages = (_gather_pass_on([_ENTRY[n] for n in MIXER_MATS], self.partly_gathered),
                  _gather_ici([_ENTRY[n] for n in FFN_MATS], [self.late_shards[n] for n in FFN_MATS]))

        def landed(extra):
            mixer, self.partly_gathered = _split(stages, extra)
            self.w.update(zip(MIXER_MATS, mixer))

        return _join(*stages), landed

    def _at_ssm_gate_norm(self):
        return (_gather_pass_on([_ENTRY[n] for n in FFN_MATS], self.partly_gathered),
                lambda fulls: self.w.update(zip(FFN_MATS, fulls)))

    def pair_sums(self, names, grads, recv):
        for n, gr, rv in zip(names, grads, recv):
            _, r, c, lay = _ENTRY[n]
            self.sums[n] = _pair_sum("pair_sum_" + n, gr, rv, self.ids, r, c, lay)

    def chip_sums(self, names, recv):
        for n, rv in zip(names, recv):
            self.halves[n] = _chip_sum("chip_sum_" + n, self.sums[n][1], rv)

    def adamw(self, names, theirs):
        for n, th in zip(names, theirs):
            sh = self.shards[n]
            self.results[n] = _adamw("adamw_" + n, sh["w"], sh["m"], sh["v"], self.halves[n], th, self.ids)

    def _pair_stage(self, names, grads):
        return (_pair_exchange([_ENTRY[n] for n in names], grads),
                lambda recv: self.pair_sums(names, grads, recv))

    def _at_ffn_up_dx(self):
        return self._pair_stage(FFN_MATS, [self.g[n] for n in FFN_MATS])

    def _at_ssm_gate_norm_bwd(self):
        return self._pair_stage(MIXER_MATS, [self.g[n] for n in MIXER_MATS])

    def _at_attn_bwd(self):
        return _chip_exchange([self.sums[n][0] for n in FFN_MATS]), lambda recv: self.chip_sums(FFN_MATS, recv)

    def _at_ssd_bwd(self):
        stages = (_chip_exchange([self.sums[n][0] for n in MIXER_MATS]),
                  _whole_to_sibling([self.halves[n] for n in FFN_MATS]))

        def landed(extra):
            recv, theirs = _split(stages, extra)
            self.chip_sums(MIXER_MATS, recv)
            self.adamw(FFN_MATS, theirs)

        return _join(*stages), landed

    def _at_ssm_conv_bwd(self):
        return _whole_to_sibling([self.halves[n] for n in MIXER_MATS]), lambda theirs: self.adamw(MIXER_MATS, theirs)

    def _at_in_proj_dx(self):
        grads = [_from_cat(self.g.pop("w_cat"))]
        self.pair_sums(("w_in",), grads, _run_exchange("grad_pair_exchange_w_in", _pair_exchange([_ENTRY["w_in"]], grads)))
        return _chip_exchange([self.sums["w_in"][0]]), lambda recv: self.chip_sums(("w_in",), recv)

    def finish(self, g_small, g_rep, rep_shards):
        stages = (_pair_exchange([_ENTRY["small"]], [g_small]), _whole_to_sibling([g_rep]))
        recv_small, recv_rep = _split(stages, _run_exchange("grad_pair_exchange_tail", _join(*stages)))
        self.pair_sums(("small",), [g_small], recv_small)
        p_rep, = _rowwise("pair_sum_replicated", lambda r0, a, b: [a + b], SMALL_ROWS, SMALL_ROWS,
                          [(g_rep, LANES, 0), (recv_rep[0], LANES, 0)], [], [(LANES, F32)], [])
        stages = (_chip_exchange([self.sums["small"][0]]), _to_all_chips(p_rep))
        recv, recv_rep = _split(stages, _run_exchange("grad_chip_exchange_tail", _join(*stages)))
        self.chip_sums(("small",), recv)
        g_rep_tot = _chip_sum_small(p_rep, recv_rep[0], self.ids)
        last = ("w_in", "small")
        self.adamw(last, _run_exchange("grad_half_share_tail", _whole_to_sibling([self.halves[n] for n in last])))
        ids_lo = self.ids * jnp.array([0, 1], jnp.int32)
        self.results["replicated"] = _adamw("adamw_replicated", rep_shards["w"], rep_shards["m"], rep_shards["v"],
                                            g_rep_tot[0:SMALL_ROWS // 2], g_rep_tot[SMALL_ROWS // 2:], ids_lo)


def kernel(x, meta_tokens, norm_pre_mix, w_in, ssm_conv_w, ssm_conv_b, ssm_dt_bias, ssm_a_log, ssm_d_skip, ssm_norm, w_ssm_out, attn_sinks, w_attn_out, w_mix_out, norm_post_mix, norm_pre_ffn, w_ffn_up, ffn_conv_w, ffn_conv_b, w_ffn_down, norm_post_ffn, loss_target, m_meta_tokens, m_norm_pre_mix, m_w_in, m_ssm_conv_w, m_ssm_conv_b, m_ssm_dt_bias, m_ssm_a_log, m_ssm_d_skip, m_ssm_norm, m_w_ssm_out, m_attn_sinks, m_w_attn_out, m_w_mix_out, m_norm_post_mix, m_norm_pre_ffn, m_w_ffn_up, m_ffn_conv_w, m_ffn_conv_b, m_w_ffn_down, m_norm_post_ffn, v_meta_tokens, v_norm_pre_mix, v_w_in, v_ssm_conv_w, v_ssm_conv_b, v_ssm_dt_bias, v_ssm_a_log, v_ssm_d_skip, v_ssm_norm, v_w_ssm_out, v_attn_sinks, v_w_attn_out, v_w_mix_out, v_norm_post_mix, v_norm_pre_ffn, v_w_ffn_up, v_ffn_conv_w, v_ffn_conv_b, v_w_ffn_down, v_norm_post_ffn):
    args = dict(locals())
    squeeze = lambda a: a.reshape(a.shape[-2:])
    wts = {n: squeeze(args[n]) for n in WEIGHT_ORDER}
    mom = {n: squeeze(args["m_" + n]) for n in WEIGHT_ORDER}
    var = {n: squeeze(args["v_" + n]) for n in WEIGHT_ORDER}
    x_i, y_i, c_i = _mesh_pos()
    ids = jnp.stack([c_i, _chip_index(x_i, y_i)]).astype(jnp.int32)
    big_names = [n for n, _, _, _ in _BIG[:-1]]
    small_names = [n for n, _, _ in _SMALL_SHARDED]
    rep_names = [n for n, _ in _REPLICATED]

    stacks = {"w": wts, "m": mom, "v": var}
    shards = {n: {k: d[n] for k, d in stacks.items()} for n in big_names}
    shards["small"] = {k: _small_shard([d[n] for n in small_names]) for k, d in stacks.items()}
    rep_shards = {k: _flatten([d[n] for n in rep_names], SMALL_ROWS) for k, d in stacks.items()}

    w_in4, small_all = _gather_weights([_ENTRY["w_in"], _ENTRY["small"]], [wts["w_in"].astype(BF16), shards["small"]["w"]])
    w = {n: wts[n] for n in rep_names}
    w["w_cat"] = _to_cat(w_in4)
    small_parts = [_unflatten(small_all[i], [shp for _, shp, _ in _SMALL_SHARDED]) for i in range(4)]
    for k, (n, _, axis) in enumerate(_SMALL_SHARDED):
        w[n] = jnp.concatenate([small_parts[i][k] for i in range(4)], axis=axis)
    plan = _StepPlan(w, {n: wts[n].astype(BF16) for n in MIXER_MATS + FFN_MATS}, shards, ids)

    head = jnp.concatenate([jnp.zeros((PAD, D_MODEL), F32), w["meta_tokens"]], axis=0)
    loss_sum, dx, dhead = _local_step(x[0], head, loss_target[0], plan)
    loss = lax.psum(loss_sum * (0.5 / D_MODEL), ("x", "y", "c"))
    g = plan.g
    g["meta_tokens"] = dhead[PAD:]
    g_small = jnp.stack([_small_shard([_shard_of(g[n], i, shp, ax) for n, shp, ax in _SMALL_SHARDED]) for i in range(4)])
    plan.finish(g_small, _flatten([g[n] for n in rep_names], SMALL_ROWS), rep_shards)

    results = {}
    for kind in range(4):
        results.update({(kind, n): plan.results[n][kind] for n in big_names})
        parts = _unflatten(plan.results["small"][kind], [shp for _, shp, _ in _SMALL_SHARDED])
        results.update({(kind, n): parts[k] for k, n in enumerate(small_names)})
        parts = _unflatten(plan.results["replicated"][kind], [(1, width) for _, width in _REPLICATED])
        results.update({(kind, n): parts[k] for k, n in enumerate(rep_names)})
    outs = [results[kind, n].reshape(args[n].shape) for kind in range(4) for n in WEIGHT_ORDER]
    return (loss, dx[None], *outs)
```

```python
import math
from typing import Any, Callable, NamedTuple, Sequence

import jax
import jax.numpy as jnp
from jax import lax
from jax.experimental import pallas as pl
from jax.experimental.pallas import tpu as pltpu

F32 = jnp.float32
BF16 = jnp.bfloat16

D_MODEL = 1024
N_META = 16
T = 128
PAD = T - N_META
D_INNER = 2048
SSM_HEADS = 32
HEAD_P = 64
SSM_GROUPS = 4
GROUP_W = D_INNER // SSM_GROUPS
D_STATE = 128
CONV_DIM = D_INNER + 2 * SSM_GROUPS * D_STATE
ATTN_HEADS = 16
KV_HEADS = 4
ATTN_W = 1024
KV_W = 256
FFN_DIM = 2816
N_IN = 8736
EPS = 1e-6
NEG = -1e30
SCALE = 0.125

P_Q, P_K, P_V, P_DT, P_Z, P_GATE, P_XBC = 0, 1024, 1280, 1536, 2048, 4096, 6144
QKV_W = 1536
P_W = 9216

ADAM_LR, ADAM_B1, ADAM_B2, ADAM_EPS, ADAM_WD, ADAM_STEP = 0.001, 0.9, 0.999, 1e-08, 0.01, 10

VMEM_BUDGET = 40 * 1024 * 1024
VMEM_LIMIT = 56 * 1024 * 1024
MESH = pl.DeviceIdType.MESH
ANY = pl.BlockSpec(memory_space=pl.ANY)


def _cparams(n_axes, **kw):
    return pltpu.CompilerParams(dimension_semantics=("arbitrary",) * n_axes, vmem_limit_bytes=VMEM_LIMIT, **kw)


class _Exchange(NamedTuple):
    ins: Sequence[Any]
    out_shapes: Sequence[Any]
    make_copies: Callable
    n_copies: int
    aliases: dict = {}


def _call(body, name, grid, in_specs, out_specs, out_shape, operands, scratch_shapes=(), aliases=None, bg=None):
    aliases = dict(aliases or {})
    if bg is None:
        return pl.pallas_call(body, name=name, grid=grid, in_specs=in_specs, out_specs=out_specs, out_shape=out_shape,
                              scratch_shapes=list(scratch_shapes), input_output_aliases=aliases,
                              compiler_params=_cparams(len(grid)))(*operands)
    n_in, n_out, n_scr = len(in_specs), len(out_specs), len(scratch_shapes)
    nb_in, nb_out = len(bg.ins), len(bg.out_shapes)

    def hosted(*refs):
        ins, bg_ins = refs[:n_in], refs[n_in:n_in + nb_in]
        outs = refs[n_in + nb_in:n_in + nb_in + n_out]
        bg_outs = refs[n_in + nb_in + n_out:n_in + nb_in + n_out + nb_out]
        scratch = refs[n_in + nb_in + n_out + nb_out:n_in + nb_in + n_out + nb_out + n_scr]
        send_sems, recv_sems = refs[-2:]
        pids = [pl.program_id(a) for a in range(len(grid))]
        first, last = pids[0] == 0, pids[0] == grid[0] - 1
        for p, g in zip(pids[1:], grid[1:]):
            first, last = first & (p == 0), last & (p == g - 1)
        copies = []
        for k, (src, dst, peer) in enumerate(bg.make_copies(bg_ins, bg_outs)):
            if peer is None:
                copies.append(pltpu.make_async_copy(src, dst, send_sems.at[k]))
            else:
                copies.append(pltpu.make_async_remote_copy(src_ref=src, dst_ref=dst, send_sem=send_sems.at[k],
                                                           recv_sem=recv_sems.at[k], device_id=peer, device_id_type=MESH))
        assert len(copies) == bg.n_copies

        @pl.when(first)
        def _():
            for cp in copies:
                cp.start()

        body(*ins, *outs, *scratch)

        @pl.when(last)
        def _():
            for cp in copies:
                cp.wait()

    aliases = {(k if k < n_in else k + nb_in): v for k, v in aliases.items()}
    aliases.update({n_in + k: n_out + v for k, v in bg.aliases.items()})
    res = pl.pallas_call(
        hosted, name=name, grid=grid, in_specs=list(in_specs) + [ANY] * nb_in, out_specs=list(out_specs) + [ANY] * nb_out,
        out_shape=list(out_shape) + list(bg.out_shapes), input_output_aliases=aliases,
        scratch_shapes=list(scratch_shapes) + [pltpu.SemaphoreType.DMA((bg.n_copies,))] * 2,
        compiler_params=_cparams(len(grid), has_side_effects=True))(*operands, *bg.ins)
    return res[:n_out], res[n_out:]


def _sigmoid(x):
    return 1.0 / (1.0 + jnp.exp(-x))


def _silu(x):
    return x * _sigmoid(x)


def _silu_grad(x):
    s = _sigmoid(x)
    return x * s, s * (1.0 + x * (1.0 - s))


def _dsilu(x):
    return _silu_grad(x)[1]


def _softplus(x):
    e = jnp.exp(-jnp.abs(x))
    small = e * (1.0 - e * (0.5 - e * (1.0 / 3.0)))
    return jnp.maximum(x, 0.0) + jnp.where(e < 0.01, small, jnp.log(1.0 + e))


def _rms(x, w):
    r = lax.rsqrt(jnp.mean(x * x, axis=-1, keepdims=True) + EPS)
    return x * r * w


def _rms_bwd(dy, x, w):
    r = lax.rsqrt(jnp.mean(x * x, axis=-1, keepdims=True) + EPS)
    xh = x * r
    g = dy * w
    dx = r * (g - xh * jnp.mean(g * xh, axis=-1, keepdims=True))
    dw = jnp.sum(dy * xh, axis=0, keepdims=True)
    return dx, dw


def _dot(a, b):
    return jnp.dot(a, b, preferred_element_type=F32)


def _dot_nt(a, b):
    return lax.dot_general(a, b, (((1,), (1,)), ((), ())), preferred_element_type=F32)


def _dot_tn(a, b):
    return lax.dot_general(a, b, (((0,), (0,)), ((), ())), preferred_element_type=F32)


def _split3(x):
    hi = x.astype(BF16)
    r = x - hi.astype(F32)
    mid = r.astype(BF16)
    lo = (r - mid.astype(F32)).astype(BF16)
    return hi, mid, lo


def _xdot(x, e):
    hi, mid, lo = _split3(x)
    return _dot(hi, e) + _dot(mid, e) + _dot(lo, e)


def _xdot_l(e, x):
    hi, mid, lo = _split3(x)
    return _dot(e, hi) + _dot(e, mid) + _dot(e, lo)


def _iota(shape, dim):
    return lax.broadcasted_iota(jnp.int32, shape, dim)


def _divisors(n, unit):
    return [t for t in range(unit, n + 1, unit) if n % t == 0]


MIN_MATMUL_STEPS = 8


def _matmul_tiles(m, n, k, a_bytes, b_bytes, o_bytes, m_unit):
    best = None
    for tm in _divisors(m, m_unit):
        for tn in _divisors(n, 128):
            for tk in _divisors(k, 128):
                acc = 0 if tk == k else tm * tn * 4
                vm = 2 * (tm * tk * a_bytes + tk * tn * b_bytes + tm * tn * o_bytes) + acc
                if vm > VMEM_BUDGET:
                    continue
                steps = (m // tm) * (n // tn) * (k // tk)
                score = (tk == k, min(steps, MIN_MATMUL_STEPS), min(tm, 256), tm * tn * tk)
                if best is None or score > best[0]:
                    best = (score, (tm, tn, tk))
    return best[1]


def _matmul(name, a, b, mode, out_dtype, bg=None):
    if mode == "nn":
        (m, k), n = a.shape, b.shape[1]
    elif mode == "nt":
        (m, k), n = a.shape, b.shape[0]
    else:
        (k, m), n = a.shape, b.shape[1]
    ab, bb, ob = a.dtype.itemsize, b.dtype.itemsize, jnp.dtype(out_dtype).itemsize
    tm, tn, tk = _matmul_tiles(m, n, k, ab, bb, ob, 128 if mode == "tn" else 16)
    nk = k // tk
    dot = {"nn": _dot, "nt": _dot_nt, "tn": _dot_tn}[mode]

    def body(a_ref, b_ref, o_ref, *scratch):
        prod = dot(a_ref[...].astype(BF16), b_ref[...].astype(BF16))
        if nk == 1:
            o_ref[...] = prod.astype(o_ref.dtype)
        else:
            acc_ref, = scratch
            kk = pl.program_id(2)

            @pl.when(kk == 0)
            def _():
                acc_ref[...] = prod

            @pl.when(kk > 0)
            def _():
                acc_ref[...] += prod

            @pl.when(kk == nk - 1)
            def _():
                o_ref[...] = acc_ref[...].astype(o_ref.dtype)

    a_spec = pl.BlockSpec((tk, tm), lambda i, j, kk: (kk, i)) if mode == "tn" else pl.BlockSpec((tm, tk), lambda i, j, kk: (i, kk))
    b_spec = pl.BlockSpec((tn, tk), lambda i, j, kk: (j, kk)) if mode == "nt" else pl.BlockSpec((tk, tn), lambda i, j, kk: (kk, j))
    res = _call(body, name, (m // tm, n // tn, nk), [a_spec, b_spec], [pl.BlockSpec((tm, tn), lambda i, j, kk: (i, j))],
                [jax.ShapeDtypeStruct((m, n), out_dtype)], [a, b],
                scratch_shapes=[] if nk == 1 else [pltpu.VMEM((tm, tn), F32)], bg=bg)
    return res[0] if bg is None else (res[0][0], res[1])


def _row_tile(n_rows, cap, unit=16):
    return max(t for t in _divisors(n_rows, unit) if t <= cap)


ROW_SUB = 32
GROUP_UNROLL = 4


def _rowwise(name, fn, n_rows, tm, row_ins, full_ins, row_outs, acc_outs, bg=None):
    n_in = len(row_ins) + len(full_ins)
    n_ro = len(row_outs)
    into = [(k, o[3]) for k, o in enumerate(row_outs) if len(o) > 2 and o[2] == "into"]

    n_row_in = len(row_ins)
    sub = min(tm, ROW_SUB)

    def body(*refs):
        i = pl.program_id(0)
        outs = refs[n_in + len(into):]

        sums = tuple(jnp.zeros((1, w), F32) for w in acc_outs)
        for s in range(tm // sub):
            rows = pl.ds(s * sub, sub)
            vals = [r[rows, :] for r in refs[:n_row_in]] + [r[...] for r in refs[n_row_in:n_in]]
            res = fn(i * tm + s * sub, *vals)
            for o, r, v in zip(row_outs, outs[:n_ro], res[:n_ro]):
                if len(o) > 2 and o[2] == "first":
                    @pl.when(i == 0)
                    def _(r=r, v=v, rows=rows):
                        r[rows, :] = v.astype(r.dtype)
                else:
                    r[rows, :] = v.astype(r.dtype)
            sums = tuple(a + v for a, v in zip(sums, res[n_ro:]))

        @pl.when(i == 0)
        def _():
            for r, v in zip(outs[n_ro:], sums):
                r[...] = v

        @pl.when(i > 0)
        def _():
            for r, v in zip(outs[n_ro:], sums):
                r[...] += v

    def in_spec(entry):
        w, cb = entry[1], entry[2]
        if len(entry) > 3 and entry[3] == "prev":
            return pl.BlockSpec((tm, w), lambda i: (jnp.maximum(i - 1, 0), cb))
        if len(entry) > 3 and entry[3] == "first":
            return pl.BlockSpec((tm, w), lambda i: (0, cb))
        return pl.BlockSpec((tm, w), lambda i: (i, cb))

    def out_spec(o):
        if len(o) == 2:
            return pl.BlockSpec((tm, o[0]), lambda i: (i, 0)), jax.ShapeDtypeStruct((n_rows, o[0]), o[1])
        if o[2] == "new":
            return pl.BlockSpec((tm, o[0]), lambda i: (i, o[4])), jax.ShapeDtypeStruct((n_rows, o[3]), o[1])
        if o[2] == "into":
            return pl.BlockSpec((tm, o[0]), lambda i: (i, o[4])), jax.ShapeDtypeStruct(o[3].shape, o[3].dtype)
        if o[2] == "first":
            return pl.BlockSpec((tm, o[0]), lambda i: (0, 0)), jax.ShapeDtypeStruct((tm, o[0]), o[1])
        return pl.BlockSpec((tm, o[0]), lambda i: (jnp.maximum(i - 1, 0), 0)), jax.ShapeDtypeStruct((o[3], o[0]), o[1])

    in_specs = [in_spec(e) for e in row_ins]
    in_specs += [pl.BlockSpec(a.shape, lambda i: (0, 0)) for a in full_ins]
    in_specs += [pl.BlockSpec(memory_space=pl.ANY) for _ in into]
    specs_shapes = [out_spec(o) for o in row_outs]
    out_specs = [s for s, _ in specs_shapes] + [pl.BlockSpec((1, w), lambda i: (0, 0)) for w in acc_outs]
    out_shape = [s for _, s in specs_shapes] + [jax.ShapeDtypeStruct((1, w), F32) for w in acc_outs]
    return _call(body, name, (n_rows // tm,), in_specs, out_specs, out_shape,
                 [e[0] for e in row_ins] + list(full_ins) + [arr for _, arr in into],
                 aliases={n_in + a: k for a, (k, _) in enumerate(into)}, bg=bg)


def _valid_rows(first_row, tm, lo):
    return (first_row + _iota((tm, 1), 0)) >= lo


CONV_ROWS = 128
CONV_SUB = 16
CONV_LANES = 256


def _conv_specs(tm, width, blk, n_rows, after):
    specs = [pl.BlockSpec((tm, width), lambda i: (i, blk)),
             pl.BlockSpec((8, width), lambda i: (jnp.maximum(i * (tm // 8) - 1, 0), blk))]
    if after:
        specs.append(pl.BlockSpec((16, width), lambda i: (jnp.minimum((i + 1) * (tm // 16), n_rows // 16 - 1), blk)))
    return specs


def _conv_window(win, w_ref, b_ref, taps, c0, cw, n):
    acc = b_ref[:, c0:c0 + cw] + w_ref[taps - 1:taps, c0:c0 + cw] * win[8:8 + n]
    for k in range(taps - 1):
        acc = acc + w_ref[k:k + 1, c0:c0 + cw] * win[8 - (taps - 1) + k:8 - (taps - 1) + k + n]
    return acc


def _ffn_act(name, u_raw, conv_w, conv_b, n_rows):
    tm, sub, cw = CONV_ROWS, CONV_SUB, CONV_LANES
    taps, width = conv_w.shape
    half = width // 2

    def body(cur_ref, prev_ref, w_ref, b_ref, f_ref, ext_ref):
        i = pl.program_id(0)
        ext_ref[0:8, :] = jnp.where(i > 0, prev_ref[...], 0.0)
        ext_ref[8:8 + tm, :] = cur_ref[...]
        for q in range(half // cw):
            a0, g0 = q * cw, half + q * cw

            def group(s, carry):
                r = pl.multiple_of(s * sub, sub)
                a = _conv_window(ext_ref[pl.ds(r, sub + 8), a0:a0 + cw], w_ref, b_ref, taps, a0, cw, sub)
                g = _conv_window(ext_ref[pl.ds(r, sub + 8), g0:g0 + cw], w_ref, b_ref, taps, g0, cw, sub)
                f = jnp.where(_valid_rows(i * tm + r, sub, PAD), _silu(a) * g, 0.0)
                f_ref[pl.ds(r, sub), a0:a0 + cw] = f.astype(f_ref.dtype)
                return carry

            lax.fori_loop(0, tm // sub, group, 0, unroll=GROUP_UNROLL)

    return pl.pallas_call(
        body, name=name, grid=(n_rows // tm,),
        in_specs=_conv_specs(tm, width, 0, n_rows, False) + [pl.BlockSpec((taps, width), lambda i: (0, 0)),
                                                             pl.BlockSpec((1, width), lambda i: (0, 0))],
        out_specs=pl.BlockSpec((tm, half), lambda i: (i, 0)),
        out_shape=jax.ShapeDtypeStruct((n_rows, half), BF16),
        scratch_shapes=[pltpu.VMEM((tm + 8, width), F32)],
        compiler_params=_cparams(1),
    )(u_raw, u_raw, conv_w, conv_b)


def _conv_bwd(name, raw, raw_blk, dsrcs, chunk_src, conv_w, conv_b, n_rows, gated, into=None, into_blk=0, bg=None):
    taps, width = conv_w.shape
    half = width // 2 if gated else width
    tm, sub, cw = CONV_ROWS, CONV_SUB, CONV_LANES
    te = tm + 16
    nd = len(dsrcs)
    n_parts = 2 if gated else 1

    def body(*refs):
        cur_ref, prev_ref, next_ref = refs[0:3]
        dcur, dnext = refs[3:3 + nd], refs[3 + nd:3 + 2 * nd]
        w_ref, b_ref = refs[3 + 2 * nd:5 + 2 * nd]
        out_ref, acc_ref, ext_ref, du_ref = refs[-4:]
        i = pl.program_id(0)
        ext_ref[0:8, :] = jnp.where(i > 0, prev_ref[...], 0.0)
        ext_ref[8:8 + tm, :] = cur_ref[...]
        ext_ref[8 + tm:24 + tm, :] = next_ref[...]

        for q, (src, off) in enumerate(chunk_src):
            cols = [q * cw, half + q * cw][:n_parts]

            def conv_grad(r, d):
                pre = [_conv_window(ext_ref[pl.ds(r, sub + 8), c0:c0 + cw], w_ref, b_ref, taps, c0, cw, sub) for c0 in cols]
                row = i * tm + r + _iota((sub, 1), 0)
                live = (row >= PAD) & (row < n_rows)
                if gated:
                    act, dact = _silu_grad(pre[0])
                    dus = [d * pre[1] * dact, d * act]
                else:
                    dus = [d * _dsilu(pre[0])]
                for part, du in enumerate(dus):
                    du_ref[part, pl.ds(r, sub), :] = jnp.where(live, du, 0.0)

            def tile_rows(s, carry):
                r = pl.multiple_of(s * sub, sub)
                conv_grad(r, dcur[src][pl.ds(r, sub), off:off + cw].astype(F32))
                return carry

            lax.fori_loop(0, tm // sub, tile_rows, 0, unroll=GROUP_UNROLL)
            conv_grad(tm, dnext[src][:, off:off + cw].astype(F32))

            for part, c0 in enumerate(cols):
                taps_w = [w_ref[k:k + 1, c0:c0 + cw] for k in range(taps)]

                def back(s, sums):
                    new = list(sums)
                    for u in range(2):
                        r = pl.multiple_of((2 * s + u) * sub, sub)
                        win = du_ref[part, pl.ds(r, sub + 8), :]
                        raw_rows = ext_ref[pl.ds(8 + r, sub), c0:c0 + cw]
                        draw = jnp.zeros((sub, cw), F32)
                        for k in range(taps):
                            shifted = win[taps - 1 - k:taps - 1 - k + sub]
                            draw = draw + taps_w[k] * shifted
                            new[k] = new[k] + shifted * raw_rows
                        new[taps] = new[taps] + win[0:sub]
                        out_ref[pl.ds(r, sub), c0:c0 + cw] = jnp.where(_valid_rows(i * tm + r, sub, PAD), draw, 0.0).astype(out_ref.dtype)
                    return tuple(new)

                sums = lax.fori_loop(0, tm // (2 * sub), back, tuple(jnp.zeros((sub, cw), F32) for _ in range(taps + 1)))
                for k in range(taps + 1):
                    total = jnp.sum(sums[k], axis=0, keepdims=True)
                    acc_ref[k:k + 1, c0:c0 + cw] = jnp.where(i == 0, total, acc_ref[k:k + 1, c0:c0 + cw] + total)

    in_specs = _conv_specs(tm, width, raw_blk, n_rows, True)
    in_specs += [pl.BlockSpec((tm, d.shape[1]), lambda i: (i, 0)) for d in dsrcs]
    in_specs += [pl.BlockSpec((16, d.shape[1]), lambda i: (jnp.minimum((i + 1) * (tm // 16), n_rows // 16 - 1), 0)) for d in dsrcs]
    in_specs += [pl.BlockSpec((taps, width), lambda i: (0, 0)), pl.BlockSpec((1, width), lambda i: (0, 0))]
    operands = [raw, raw, raw] + list(dsrcs) + list(dsrcs) + [conv_w, conv_b]
    aliases = {}
    if into is None:
        out0 = jax.ShapeDtypeStruct((n_rows, width), BF16)
    else:
        in_specs.append(pl.BlockSpec(memory_space=pl.ANY))
        operands.append(into)
        aliases = {len(operands) - 1: 0}
        out0 = jax.ShapeDtypeStruct(into.shape, into.dtype)
    return _call(body, name, (n_rows // tm,), in_specs,
                 [pl.BlockSpec((tm, width), lambda i: (i, into_blk)), pl.BlockSpec((8, width), lambda i: (0, 0))],
                 [out0, jax.ShapeDtypeStruct((8, width), F32)], operands,
                 scratch_shapes=[pltpu.VMEM((tm + 24, width), F32), pltpu.VMEM((n_parts, te + 8, cw), F32)],
                 aliases=aliases, bg=bg)


def _ssd_specs(n_chunks, rev, per_step=1):
    cidx = (lambda c: n_chunks - 1 - c) if rev else (lambda c: c)
    xw, nw = per_step * GROUP_W, per_step * D_STATE
    xg0, bg0, cg0 = P_XBC // xw, (P_XBC + D_INNER) // nw, (P_XBC + D_INNER + SSM_GROUPS * D_STATE) // nw

    def cur(width, blk0):
        return pl.BlockSpec((T, width), lambda g, c: (cidx(c), blk0 + g))

    def prev(width, blk0):
        return pl.BlockSpec((8, width), lambda g, c: (jnp.maximum(cidx(c) * (T // 8) - 1, 0), blk0 + g))

    specs = [cur(xw, xg0), prev(xw, xg0), cur(nw, bg0), prev(nw, bg0), cur(nw, cg0), prev(nw, cg0),
             pl.BlockSpec((T, 128), lambda g, c: (cidx(c), P_DT // 128))]
    wb, wc = D_INNER // nw, (D_INNER + SSM_GROUPS * D_STATE) // nw
    specs += [pl.BlockSpec((4, xw), lambda g, c: (0, g)),
              pl.BlockSpec((4, nw), lambda g, c: (0, wb + g)),
              pl.BlockSpec((4, nw), lambda g, c: (0, wc + g)),
              pl.BlockSpec((1, xw), lambda g, c: (0, g)),
              pl.BlockSpec((1, nw), lambda g, c: (0, wb + g)),
              pl.BlockSpec((1, nw), lambda g, c: (0, wc + g))]
    specs += [pl.BlockSpec((1, 128), lambda g, c: (0, 0))] * 3
    return specs, cidx


def _ssd_chunk_forward(refs, ext_ref, g, c):
    (xc_ref, xp_ref, bc_ref, bp_ref, cc_ref, cp_ref, dt_ref, wx_ref, wb_ref, wc_ref,
     bx_ref, bb_ref, bcb_ref, dtb_ref, alog_ref, dsk_ref) = refs

    def conv_pre(cur_ref, prev_ref, w_ref, b_ref, width):
        ext_ref[0:8, 0:width] = jnp.where(c > 0, prev_ref[...], 0.0)
        ext_ref[8:8 + T, 0:width] = cur_ref[...]
        w = w_ref[...]
        acc = b_ref[...] + w[3:4] * cur_ref[...]
        for k in range(3):
            acc = acc + w[k:k + 1] * ext_ref[pl.ds(5 + k, T), 0:width]
        return acc

    valid = _valid_rows(c * T, T, PAD)
    v = {}
    v["valid"] = valid
    v["x_pre"] = conv_pre(xc_ref, xp_ref, wx_ref, bx_ref, GROUP_W)
    v["b_pre"] = conv_pre(bc_ref, bp_ref, wb_ref, bb_ref, D_STATE)
    v["c_pre"] = conv_pre(cc_ref, cp_ref, wc_ref, bcb_ref, D_STATE)
    xs = _silu(v["x_pre"])
    bm = jnp.where(valid, _silu(v["b_pre"]), 0.0)
    cm = jnp.where(valid, _silu(v["c_pre"]), 0.0)
    dtr = dt_ref[...] + dtb_ref[...]
    dt = jnp.where(valid, _softplus(dtr), 0.0)
    a_neg = -jnp.exp(alog_ref[...])
    a = dt * a_neg
    tril = _iota((T, T), 0) >= _iota((T, T), 1)
    cs = _xdot_l(tril.astype(BF16), a)
    hh, ll = _iota((128, GROUP_W), 0), _iota((128, GROUP_W), 1)
    expand = (hh == 8 * g + jnp.right_shift(ll, 6)).astype(BF16)
    sh, sj = _iota((128, 128), 0), _iota((128, 128), 1)
    select = ((sh == 8 * g + sj) & (sj < 8)).astype(BF16)
    hh_t, ll_t = _iota((GROUP_W, 128), 1), _iota((GROUP_W, 128), 0)
    v["expand_t"] = (hh_t == 8 * g + jnp.right_shift(ll_t, 6)).astype(BF16)
    v["select_t"] = ((sj == 8 * g + sh) & (sh < 8)).astype(BF16)
    cs_e = _xdot(cs, expand)
    dt_e = _xdot(dt, expand)
    cs_loc = _xdot(cs, select)
    cs_loc_t = cs_loc.T
    cs_last_e = cs_e[T - 1:T, :]
    v.update(xs=xs, bm=bm, cm=cm, dtr=dtr, dt=dt, a_neg=a_neg, tril=tril, expand=expand, select=select,
             cs_e=cs_e, dt_e=dt_e, cs_loc=cs_loc, cs_loc_t=cs_loc_t, cs_last_e=cs_last_e)
    v["xdt"] = xs * dt_e
    v["decay_e"] = jnp.exp(cs_last_e - cs_e)
    v["ecs_e"] = jnp.exp(cs_e)
    v["elast_e"] = jnp.exp(cs_last_e)
    v["d_e"] = _xdot(dsk_ref[...], expand)
    v["gmat"] = _dot_nt(cm.astype(BF16), bm.astype(BF16))
    return v


def _ssd_decay_pair(v, jp):
    out = []
    for j in (2 * jp, 2 * jp + 1):
        diff = v["cs_loc"][:, j:j + 1] - v["cs_loc_t"][j:j + 1, :]
        out.append(jnp.where(v["tril"], jnp.exp(jnp.where(v["tril"], diff, 0.0)), 0.0))
    return out


def _block_diag_pair(xp):
    lane = _iota(xp.shape, 1)
    return jnp.concatenate([jnp.where(lane < HEAD_P, xp, 0.0), jnp.where(lane >= HEAD_P, xp, 0.0)], axis=0)


SSD_GROUPS_PER_STEP = 4


def _ssd_group_refs(refs, gg):
    x_w, n_w = pl.ds(GROUP_W * gg, GROUP_W), pl.ds(D_STATE * gg, D_STATE)
    lanes = [x_w, x_w, n_w, n_w, n_w, n_w, None, x_w, n_w, n_w, x_w, n_w, n_w, None, None, None]
    return [r if w is None else r.at[:, w] for r, w in zip(refs, lanes)]


def _ssd_fwd(p, conv_w, conv_b, dt_bias, a_log, d_skip, n_chunks, bg=None):
    n_rows = n_chunks * T
    in_specs, _ = _ssd_specs(n_chunks, rev=False, per_step=SSD_GROUPS_PER_STEP)
    per = SSD_GROUPS_PER_STEP

    def body(*refs):
        y_ref, hin_ref, st_ref, ext_ref = refs[16:]
        g2, c = pl.program_id(0), pl.program_id(1)

        @pl.when(c == 0)
        def _():
            st_ref[...] = jnp.zeros_like(st_ref)

        for gg in range(per):
            v = _ssd_chunk_forward(_ssd_group_refs(refs[:16], gg), ext_ref.at[gg], per * g2 + gg, c)
            state = st_ref[gg]
            hin_ref[gg] = state
            ys = []
            for jp in range(4):
                l0, l1 = _ssd_decay_pair(v, jp)
                lhs = jnp.concatenate([v["gmat"] * l0, v["gmat"] * l1], axis=1).astype(BF16)
                rhs = _block_diag_pair(v["xdt"][:, 128 * jp:128 * jp + 128]).astype(BF16)
                ys.append(_dot(lhs, rhs))
            y = jnp.concatenate(ys, axis=1)
            y = y + _dot(v["cm"].astype(BF16), state.astype(BF16)) * v["ecs_e"] + v["xs"] * v["d_e"]
            y_ref[:, GROUP_W * gg:GROUP_W * gg + GROUP_W] = y
            s_new = _dot_tn(v["bm"].astype(BF16), (v["xdt"] * v["decay_e"]).astype(BF16))
            st_ref[gg] = state * v["elast_e"] + s_new

    return _call(
        body, "ssd_fwd", (SSM_GROUPS // per, n_chunks), in_specs,
        [pl.BlockSpec((T, per * GROUP_W), lambda g, c: (c, g)),
         pl.BlockSpec((per, None, D_STATE, GROUP_W), lambda g, c: (g, c, 0, 0))],
        [jax.ShapeDtypeStruct((n_rows, D_INNER), F32),
         jax.ShapeDtypeStruct((SSM_GROUPS, n_chunks, D_STATE, GROUP_W), F32)],
        [p, p, p, p, p, p, p, conv_w, conv_w, conv_w, conv_b, conv_b, conv_b, dt_bias, a_log, d_skip],
        scratch_shapes=[pltpu.VMEM((per, D_STATE, GROUP_W), F32), pltpu.VMEM((per, T + 8, GROUP_W), F32)], bg=bg)


def _ssd_bwd(p, conv_w, conv_b, dt_bias, a_log, d_skip, hin, dy, n_chunks, bg=None):
    n_rows = n_chunks * T
    per = SSD_GROUPS_PER_STEP
    in_specs, cidx = _ssd_specs(n_chunks, rev=True, per_step=per)
    in_specs = in_specs + [pl.BlockSpec((per, None, D_STATE, GROUP_W), lambda g, c: (g, cidx(c), 0, 0)),
                           pl.BlockSpec((T, per * GROUP_W), lambda g, c: (cidx(c), g))]

    def body(*refs):
        hin_ref, dy_ref = refs[16:18]
        dx_ref, db_ref, dc_ref, ddt_ref, dpar_ref, dst_ref, ext_ref = refs[18:]
        for gg in range(per):
            x_w, n_w = pl.ds(GROUP_W * gg, GROUP_W), pl.ds(D_STATE * gg, D_STATE)
            group_body(_ssd_group_refs(refs[:16], gg), hin_ref.at[gg], dy_ref.at[:, x_w], dx_ref.at[:, x_w],
                       db_ref.at[:, n_w], dc_ref.at[:, n_w], ddt_ref.at[:, n_w], dpar_ref.at[gg], dst_ref.at[gg],
                       ext_ref.at[gg], per * pl.program_id(0) + gg)

    def group_body(in_refs, hin_ref, dy_ref, dx_ref, db_ref, dc_ref, ddt_ref, dpar_ref, dst_ref, ext_ref, g):
        step = pl.program_id(1)
        c = n_chunks - 1 - step

        @pl.when(step == 0)
        def _():
            dst_ref[...] = jnp.zeros_like(dst_ref)

        v = _ssd_chunk_forward(in_refs, ext_ref, g, c)
        hin_f = hin_ref[...]
        hin_b = hin_f.astype(BF16)
        dyv = dy_ref[...]
        dst = dst_ref[...]
        dst_b = dst.astype(BF16)
        xs, bm, cm, xdt = v["xs"], v["bm"], v["cm"], v["xdt"]
        bm_b, cm_b = bm.astype(BF16), cm.astype(BF16)

        dd_e = jnp.sum(dyv * xs, axis=0, keepdims=True)
        dxs = dyv * v["d_e"]
        ch = _dot(cm_b, hin_b)
        dch = (dyv * v["ecs_e"]).astype(BF16)
        dcm = _dot_nt(dch, hin_b)
        dhin = _dot_tn(cm_b, dch) + dst * v["elast_e"]
        dcs_e = dyv * ch * v["ecs_e"]
        dxd = _dot(bm_b, dst_b)
        dbm = _dot_nt((xdt * v["decay_e"]).astype(BF16), dst_b)
        dxdt_state = dxd * v["decay_e"]
        q = dxdt_state * xdt
        dcs_e = dcs_e - q
        dlast_e = jnp.sum(q, axis=0, keepdims=True) + jnp.sum(dst * hin_f, axis=0, keepdims=True) * v["elast_e"]
        dg = jnp.zeros((T, T), F32)
        rs_cols = jnp.zeros((T, 128), F32)
        cs_rows = jnp.zeros((128, T), F32)
        lane_i, sub_i = _iota((T, 128), 1), _iota((128, T), 0)
        dxdt_parts = []
        for jp in range(4):
            l0, l1 = _ssd_decay_pair(v, jp)
            m0, m1 = v["gmat"] * l0, v["gmat"] * l1
            xbd = _block_diag_pair(xdt[:, 128 * jp:128 * jp + 128]).astype(BF16)
            dyp = dyv[:, 128 * jp:128 * jp + 128]
            dm = _dot_nt(dyp.astype(BF16), xbd)
            dm0, dm1 = dm[:, 0:T], dm[:, T:2 * T]
            dg = dg + dm0 * l0 + dm1 * l1
            for j, qq in ((2 * jp, dm0 * m0), (2 * jp + 1, dm1 * m1)):
                rs_cols = jnp.where(lane_i == j, jnp.sum(qq, axis=1, keepdims=True), rs_cols)
                cs_rows = jnp.where(sub_i == j, jnp.sum(qq, axis=0, keepdims=True), cs_rows)
            mv = jnp.concatenate([m0, m1], axis=0).astype(BF16)
            dxdt_parts.append(_dot_tn(mv, _block_diag_pair(dyp).astype(BF16)))
        dxdt = jnp.concatenate(dxdt_parts, axis=1) + dxdt_state
        dg_b = dg.astype(BF16)
        dcm = dcm + _dot(dg_b, bm_b)
        dbm = dbm + _dot_tn(dg_b, cm_b)
        expand_t = v["expand_t"]
        dcs_loc = rs_cols - cs_rows.T
        last_row = _iota((T, 1), 0) == T - 1
        dcs_full_e = dcs_e + jnp.where(last_row, dlast_e, 0.0)
        dcs = _xdot(dcs_full_e, expand_t) + _xdot(dcs_loc, v["select_t"])
        triu = (_iota((T, T), 0) <= _iota((T, T), 1)).astype(BF16)
        da = _xdot_l(triu, dcs)
        ddt = da * v["a_neg"] + _xdot(dxdt * xs, expand_t)
        dxs = dxs + dxdt * v["dt_e"]
        ddtr = jnp.where(v["valid"], ddt * _sigmoid(v["dtr"]), 0.0)
        dx_ref[...] = dxs
        db_ref[...] = jnp.where(v["valid"], dbm, 0.0)
        dc_ref[...] = jnp.where(v["valid"], dcm, 0.0)
        ddt_ref[...] = ddtr
        dpar = jnp.concatenate([
            jnp.sum(ddtr, axis=0, keepdims=True),
            jnp.sum(da * v["dt"], axis=0, keepdims=True) * v["a_neg"],
            _xdot(dd_e, expand_t),
            jnp.zeros((5, 128), F32)], axis=0)

        @pl.when(step == 0)
        def _():
            dpar_ref[...] = dpar

        @pl.when(step > 0)
        def _():
            dpar_ref[...] += dpar

        dst_ref[...] = dhin

    return _call(
        body, "ssd_bwd", (SSM_GROUPS // per, n_chunks), in_specs,
        [pl.BlockSpec((T, per * GROUP_W), lambda g, c: (cidx(c), g)),
         pl.BlockSpec((T, per * D_STATE), lambda g, c: (cidx(c), g)),
         pl.BlockSpec((T, per * D_STATE), lambda g, c: (cidx(c), g)),
         pl.BlockSpec((T, per * 128), lambda g, c: (cidx(c), g)),
         pl.BlockSpec((per, 8, 128), lambda g, c: (g, 0, 0))],
        [jax.ShapeDtypeStruct((n_rows, D_INNER), F32),
         jax.ShapeDtypeStruct((n_rows, SSM_GROUPS * D_STATE), F32),
         jax.ShapeDtypeStruct((n_rows, SSM_GROUPS * D_STATE), F32),
         jax.ShapeDtypeStruct((n_rows, SSM_GROUPS * 128), F32),
         jax.ShapeDtypeStruct((SSM_GROUPS, 8, 128), F32)],
        [p, p, p, p, p, p, p, conv_w, conv_w, conv_w, conv_b, conv_b, conv_b, dt_bias, a_log, d_skip, hin, dy],
        scratch_shapes=[pltpu.VMEM((per, D_STATE, GROUP_W), F32), pltpu.VMEM((per, T + 8, GROUP_W), F32)], bg=bg)


def _alibi_slope(h):
    return 2.0 ** (-8.0 * (h + 1) / ATTN_HEADS)


def _dup_half(x256, kvh):
    xb = x256[:, 128 * (kvh // 2):128 * (kvh // 2) + 128]
    rolled = pltpu.roll(xb, 64, 1)
    lane = _iota(xb.shape, 1)
    if kvh % 2 == 0:
        return jnp.where(lane < 64, xb, rolled)
    return jnp.where(lane < 64, rolled, xb)


def _attn_masks(c):
    qi, j = _iota((T, T), 0), _iota((T, T), 1)
    tri = j <= qi
    meta_ok = (j >= PAD) & (j - PAD <= c * T + qi - PAD)
    band_ok = c >= jnp.where(tri, 1, 2)
    dist = jnp.bitwise_and(qi - j, T - 1).astype(F32)
    return tri, meta_ok, band_ok, dist


def _fold(x3, tri):
    return jnp.concatenate([x3[:, 0:T], jnp.where(tri, x3[:, 2 * T:3 * T], x3[:, T:2 * T])], axis=1)


def _unfold(x2, tri):
    band = x2[:, T:2 * T]
    return jnp.concatenate([x2[:, 0:T], jnp.where(tri, 0.0, band), jnp.where(tri, band, 0.0)], axis=1)


def _attn_scores(qp, k3, masks, h0):
    tri, meta_ok, band_ok, dist = masks
    lane = _iota(qp.shape, 1)
    s = []
    for half, h in ((0, h0), (1, h0 + 1)):
        qh = jnp.where((lane < 64) if half == 0 else (lane >= 64), qp, 0.0).astype(BF16)
        raw = _dot_nt(qh, k3)
        band = jnp.where(tri, raw[:, 2 * T:3 * T], raw[:, T:2 * T]) - _alibi_slope(h) * dist
        s.append((qh, jnp.concatenate([jnp.where(meta_ok, raw[:, 0:T], NEG), jnp.where(band_ok, band, NEG)], axis=1)))
    return s


def _attn_fwd(p, sinks, n_chunks, bg=None):
    n_rows = n_chunks * T
    kb, vb = P_K // KV_W, P_V // KV_W

    def body(q_ref, kc_ref, kp_ref, km_ref, vc_ref, vp_ref, vm_ref, sink_ref, o_ref, lse_ref):
        c = pl.program_id(0)
        sinks_v = sink_ref[...]
        masks = _attn_masks(c)
        tri, meta_ok, band_ok, dist = masks
        lane = _iota((T, 128), 1)
        for kvh in range(KV_HEADS):
            k3 = jnp.concatenate([_dup_half(r[...], kvh) for r in (km_ref, kp_ref, kc_ref)], axis=0).astype(BF16)
            v3 = jnp.concatenate([_dup_half(r[...], kvh) for r in (vm_ref, vp_ref, vc_ref)], axis=0)
            v3bd = _block_diag_rows(v3).astype(BF16)
            q2 = q_ref[:, 256 * kvh:256 * kvh + 256] * SCALE
            q4 = jnp.concatenate([jnp.where((lane < 64) if half == 0 else (lane >= 64), q2[:, 128 * pr:128 * pr + 128], 0.0)
                                  for pr in range(2) for half in range(2)], axis=0).astype(BF16)
            raw4 = _dot_nt(q4, k3)
            probs = []
            for hh in range(4):
                h = 4 * kvh + hh
                raw = raw4[T * hh:T * hh + T]
                band = jnp.where(tri, raw[:, 2 * T:3 * T], raw[:, T:2 * T]) - _alibi_slope(h) * dist
                sc = jnp.concatenate([jnp.where(meta_ok, raw[:, 0:T], NEG), jnp.where(band_ok, band, NEG)], axis=1)
                sink = sinks_v[:, h:h + 1]
                m = jnp.maximum(jnp.max(sc, axis=1, keepdims=True), sink)
                e = jnp.exp(sc - m)
                den = jnp.sum(e, axis=1, keepdims=True) + jnp.exp(sink - m)
                probs.append(_unfold(e * (1.0 / den), tri))
                lse_ref[:, h:h + 1] = m + jnp.log(den)
            p4 = jnp.concatenate([jnp.concatenate(probs[0:2], axis=1), jnp.concatenate(probs[2:4], axis=1)], axis=0)
            out = _dot(p4.astype(BF16), v3bd)
            o_ref[:, 256 * kvh:256 * kvh + 256] = jnp.concatenate([out[0:T], out[T:2 * T]], axis=1).astype(o_ref.dtype)

    blk = lambda width, col: pl.BlockSpec((T, width), lambda c: (c, col))
    prev = lambda width, col: pl.BlockSpec((T, width), lambda c: (jnp.maximum(c - 1, 0), col))
    first = lambda width, col: pl.BlockSpec((T, width), lambda c: (0, col))
    return _call(
        body, "attn_fwd", (n_chunks,),
        [blk(ATTN_W, P_Q // ATTN_W), blk(KV_W, kb), prev(KV_W, kb), first(KV_W, kb),
         blk(KV_W, vb), prev(KV_W, vb), first(KV_W, vb), pl.BlockSpec((1, 128), lambda c: (0, 0))],
        [pl.BlockSpec((T, ATTN_W), lambda c: (c, 0)), pl.BlockSpec((T, 128), lambda c: (c, 0))],
        [jax.ShapeDtypeStruct((n_rows, ATTN_W), BF16), jax.ShapeDtypeStruct((n_rows, 128), F32)],
        [p, p, p, p, p, p, p, sinks], bg=bg)


def _block_diag_rows(x3):
    lane = _iota(x3.shape, 1)
    return jnp.concatenate([jnp.where(lane < 64, x3, 0.0), jnp.where(lane >= 64, x3, 0.0)], axis=0)


def _fold_halves(x):
    return x + pltpu.roll(x, 64, 1)


def _attn_bwd(p, sinks, ao, lse, dao, dp, n_chunks, bg=None):
    kb, vb = P_K // KV_W, P_V // KV_W
    rc = lambda s: n_chunks - 1 - s

    def body(q_ref, kc_ref, kp_ref, km_ref, vc_ref, vp_ref, vm_ref, sink_ref, o_ref, lse_ref, do_ref, dp_in_ref,
             dqkv_ref, dsink_ref, kcar_ref, vcar_ref, kmeta_ref, vmeta_ref):
        step = pl.program_id(0)
        c = n_chunks - 1 - step

        @pl.when(step == 0)
        def _():
            for r in (kcar_ref, vcar_ref, kmeta_ref, vmeta_ref):
                r[...] = jnp.zeros_like(r)

        masks = _attn_masks(c)
        tri = masks[0]
        q = q_ref[...] * SCALE
        sinks_v = sink_ref[...]
        lse_v = lse_ref[...]
        ov = o_ref[...].astype(F32)
        dov = do_ref[...].astype(F32)
        lane = _iota((T, 128), 1)
        lane256 = _iota((3 * T, KV_W), 1)
        dsink = jnp.zeros((1, 128), F32)
        dk3_all = jnp.zeros((3 * T, KV_W), F32)
        dv3_all = jnp.zeros((3 * T, KV_W), F32)
        dqs = []
        for kvh in range(KV_HEADS):
            k3 = jnp.concatenate([_dup_half(r[...], kvh) for r in (km_ref, kp_ref, kc_ref)], axis=0).astype(BF16)
            v3 = jnp.concatenate([_dup_half(r[...], kvh) for r in (vm_ref, vp_ref, vc_ref)], axis=0).astype(BF16)
            dk3 = jnp.zeros((3 * T, 128), F32)
            dv3 = jnp.zeros((3 * T, 128), F32)
            for pr in range(2):
                h0 = 4 * kvh + 2 * pr
                blk = 2 * kvh + pr
                qp = q[:, 128 * blk:128 * blk + 128]
                dop = dov[:, 128 * blk:128 * blk + 128]
                prod = dop * ov[:, 128 * blk:128 * blk + 128]
                dq_pair = jnp.zeros((T, 128), F32)
                for half, ((qh, sc), h) in enumerate(zip(_attn_scores(qp, k3, masks, h0), (h0, h0 + 1))):
                    mine = (lane < 64) if half == 0 else (lane >= 64)
                    lse_h = lse_v[:, h:h + 1]
                    pm = jnp.exp(sc - lse_h)
                    doh = jnp.where(mine, dop, 0.0).astype(BF16)
                    delta = jnp.sum(jnp.where(mine, prod, 0.0), axis=1, keepdims=True)
                    dp = _fold(_dot_nt(doh, v3), tri)
                    ds = _unfold(pm * (dp - delta), tri).astype(BF16)
                    p_sink = jnp.exp(sinks_v[:, h:h + 1] - lse_h)
                    dsink = jnp.where(_iota((1, 128), 1) == h, jnp.sum(-p_sink * delta, axis=0, keepdims=True), dsink)
                    dq_pair = jnp.where(mine, _dot(ds, k3), dq_pair)
                    dk3 = dk3 + _dot_tn(ds, qh)
                    dv3 = dv3 + _dot_tn(_unfold(pm, tri).astype(BF16), doh)
                dqs.append(dq_pair * SCALE)
            in_place = (lane256 >= 64 * kvh) & (lane256 < 64 * kvh + 64)
            wide = lambda x: jnp.concatenate([x, x], axis=1)
            dk3_all = jnp.where(in_place, wide(_fold_halves(dk3)), dk3_all)
            dv3_all = jnp.where(in_place, wide(_fold_halves(dv3)), dv3_all)
        dsink_all = dsink

        @pl.when(step == 0)
        def _():
            dsink_ref[...] = dsink_all

        @pl.when(step > 0)
        def _():
            dsink_ref[...] += dsink_all

        kmeta = kmeta_ref[...] + dk3_all[0:T]
        vmeta = vmeta_ref[...] + dv3_all[0:T]
        kmeta_ref[...] = kmeta
        vmeta_ref[...] = vmeta
        is_first = c == 0
        dk = jnp.where(is_first, kmeta, dk3_all[2 * T:3 * T] + kcar_ref[...])
        dv = jnp.where(is_first, vmeta, dv3_all[2 * T:3 * T] + vcar_ref[...])
        dqkv_ref[...] = jnp.concatenate(dqs + [dk, dv], axis=1).astype(dqkv_ref.dtype)
        kcar_ref[...] = dk3_all[T:2 * T]
        vcar_ref[...] = dv3_all[T:2 * T]

    blk = lambda width, col: pl.BlockSpec((T, width), lambda s: (rc(s), col))
    prev = lambda width, col: pl.BlockSpec((T, width), lambda s: (jnp.maximum(rc(s) - 1, 0), col))
    first = lambda width, col: pl.BlockSpec((T, width), lambda s: (0, col))
    return _call(
        body, "attn_bwd", (n_chunks,),
        [blk(ATTN_W, P_Q // ATTN_W), blk(KV_W, kb), prev(KV_W, kb), first(KV_W, kb),
         blk(KV_W, vb), prev(KV_W, vb), first(KV_W, vb), pl.BlockSpec((1, 128), lambda s: (0, 0)),
         blk(ATTN_W, 0), blk(128, 0), blk(ATTN_W, 0), ANY],
        [blk(QKV_W, P_Q // QKV_W), pl.BlockSpec((1, 128), lambda s: (0, 0))],
        [jax.ShapeDtypeStruct(dp.shape, dp.dtype), jax.ShapeDtypeStruct((1, 128), F32)],
        [p, p, p, p, p, p, p, sinks, ao, lse, dao, dp],
        scratch_shapes=[pltpu.VMEM((T, KV_W), F32)] * 4, aliases={11: 0}, bg=bg)


def _pad_lanes(v, width=128):
    return jnp.pad(v, ((0, 0), (0, width - v.shape[1])))


def _local_step(x, head, tgt, plan):
    w, g, run = plan.w, plan.g, plan.run
    n_tok = x.shape[0]
    n_rows = n_tok + T
    n_chunks = n_rows // T
    tm = _row_tile(n_rows, 384)
    dt_bias, a_log, d_skip = (_pad_lanes(w[k]) for k in ("ssm_dt_bias", "ssm_a_log", "ssm_d_skip"))
    sinks = _pad_lanes(w["attn_sinks"])
    x_in = [(x, D_MODEL, 0, "prev"), (head, D_MODEL, 0, "first")]

    def h0_tile(r0, xt, hd):
        return jnp.where(r0 < T, hd, xt)

    n1, = _rowwise("norm_pre_mix", lambda r0, xt, hd, wn: [_rms(h0_tile(r0, xt, hd), wn)], n_rows, T,
                   x_in, [w["norm_pre_mix"]], [(D_MODEL, BF16)], [])
    p = _matmul("in_proj", n1, w["w_cat"], "nn", F32)
    y_ssd, hin = run("ssd_fwd", _ssd_fwd, p, w["ssm_conv_w"], w["ssm_conv_b"], dt_bias, a_log, d_skip, n_chunks)
    ao, lse = run("attn_fwd", _attn_fwd, p, sinks, n_chunks)

    def gate_norm(r0, y, z, wn):
        return [_rms(y * _silu(z), wn)]

    yn, = run("ssm_gate_norm", _rowwise, "ssm_gate_norm", gate_norm, n_rows, tm,
              [(y_ssd, D_INNER, 0), (p, D_INNER, P_Z // D_INNER)], [w["ssm_norm"]], [(D_INNER, BF16)], [])
    y_ssm = _matmul("ssm_out", yn, w["w_ssm_out"], "nn", F32)
    y_attn = _matmul("attn_out", ao, w["w_attn_out"], "nn", F32)

    def mix_gate(r0, ys, ya, gs, ga):
        return [_sigmoid(gs) * ys + _sigmoid(ga) * ya]

    gate_ins = [(p, D_MODEL, P_GATE // D_MODEL), (p, D_MODEL, P_GATE // D_MODEL + 1)]
    mixed, = _rowwise("mix_gate", mix_gate, n_rows, tm, [(y_ssm, D_MODEL, 0), (y_attn, D_MODEL, 0)] + gate_ins,
                      [], [(D_MODEL, BF16)], [])
    mix = _matmul("mix_out", mixed, w["w_mix_out"], "nn", F32)

    def post_mix(r0, mx, xt, hd, w_post, w_pre):
        h1 = jnp.where(_valid_rows(r0, mx.shape[0], PAD), h0_tile(r0, xt, hd) + _rms(mx, w_post), 0.0)
        return [h1, _rms(h1, w_pre)]

    h1, n2 = _rowwise("post_mix", post_mix, n_rows, T, [(mix, D_MODEL, 0)] + x_in,
                      [w["norm_post_mix"], w["norm_pre_ffn"]], [(D_MODEL, F32), (D_MODEL, BF16)], [])
    u_raw = _matmul("ffn_up", n2, w["w_ffn_up"], "nn", F32)
    f = _ffn_act("ffn_act", u_raw, w["ffn_conv_w"], w["ffn_conv_b"], n_rows)
    ffn = _matmul("ffn_down", f, w["w_ffn_down"], "nn", F32)

    def final(r0, fo, h, t, w_post):
        real = r0 >= T
        err = jnp.where(real, h + _rms(fo, w_post) - t, 0.0)
        dy = err * (1.0 / D_MODEL)
        dffn, dw = _rms_bwd(dy, fo, w_post)
        return [dffn, dy, jnp.sum(err * err, axis=0, keepdims=True), dw]

    dffn, dh2, loss_cols, g_norm_post_ffn = _rowwise(
        "loss_head", final, n_rows, T, [(ffn, D_MODEL, 0), (h1, D_MODEL, 0), (tgt, D_MODEL, 0, "prev")],
        [w["norm_post_ffn"]], [(D_MODEL, BF16), (D_MODEL, F32)], [D_MODEL, D_MODEL])

    g["norm_post_ffn"] = g_norm_post_ffn
    g["w_ffn_down"] = _matmul("ffn_down_dw", f, dffn, "tn", F32)
    df = _matmul("ffn_down_dx", dffn, w["w_ffn_down"], "nt", F32)
    du_raw, dconv = _conv_bwd("ffn_act_bwd", u_raw, 0, [df], [(0, c0) for c0 in range(0, FFN_DIM, CONV_LANES)],
                              w["ffn_conv_w"], w["ffn_conv_b"], n_rows, True)
    g["ffn_conv_w"], g["ffn_conv_b"] = dconv[0:3], dconv[3:4]
    g["w_ffn_up"] = _matmul("ffn_up_dw", n2, du_raw, "tn", F32)
    dn2 = run("ffn_up_dx", _matmul, "ffn_up_dx", du_raw, w["w_ffn_up"], "nt", F32)

    def post_mix_bwd(r0, dn, d2, h, mx, w_pre, w_post):
        dx, dw_pre = _rms_bwd(dn, h, w_pre)
        dh1 = jnp.where(_valid_rows(r0, dn.shape[0], PAD), dx + d2, 0.0)
        dmix, dw_post = _rms_bwd(dh1, mx, w_post)
        return [dh1, dmix, dw_pre, dw_post]

    dh1, dmix, g["norm_pre_ffn"], g["norm_post_mix"] = _rowwise(
        "post_mix_bwd", post_mix_bwd, n_rows, tm,
        [(dn2, D_MODEL, 0), (dh2, D_MODEL, 0), (h1, D_MODEL, 0), (mix, D_MODEL, 0)],
        [w["norm_pre_ffn"], w["norm_post_mix"]], [(D_MODEL, F32), (D_MODEL, BF16)], [D_MODEL, D_MODEL])
    g["w_mix_out"] = _matmul("mix_out_dw", mixed, dmix, "tn", F32)
    dmixed = _matmul("mix_out_dx", dmix, w["w_mix_out"], "nt", F32)

    def mix_gate_bwd(r0, dm, ys, ya, gs, ga):
        ss, sa = _sigmoid(gs), _sigmoid(ga)
        dgate = jnp.concatenate([dm * ys * ss * (1.0 - ss), dm * ya * sa * (1.0 - sa)], axis=1)
        return [dm * ss, dm * sa, dgate]

    dys, dya, dp = _rowwise(
        "mix_gate_bwd", mix_gate_bwd, n_rows, tm,
        [(dmixed, D_MODEL, 0), (y_ssm, D_MODEL, 0), (y_attn, D_MODEL, 0)] + gate_ins,
        [], [(D_MODEL, BF16), (D_MODEL, BF16), (2 * D_MODEL, BF16, "new", P_W, P_GATE // (2 * D_MODEL))], [])
    g["w_ssm_out"] = _matmul("ssm_out_dw", yn, dys, "tn", F32)
    dyn = _matmul("ssm_out_dx", dys, w["w_ssm_out"], "nt", F32)
    g["w_attn_out"] = _matmul("attn_out_dw", ao, dya, "tn", F32)
    dao = _matmul("attn_out_dx", dya, w["w_attn_out"], "nt", BF16)

    def gate_norm_bwd(r0, dn, y, z, wn):
        sz, dsz = _silu_grad(z)
        dyz, dw = _rms_bwd(dn, y * sz, wn)
        live = _valid_rows(r0, dn.shape[0], PAD)
        return [jnp.where(live, dyz * sz, 0.0), jnp.where(live, dyz * y * dsz, 0.0), dw]

    dy_ssd, dp, g["ssm_norm"] = run(
        "ssm_gate_norm_bwd", _rowwise, "ssm_gate_norm_bwd", gate_norm_bwd, n_rows, tm,
        [(dyn, D_INNER, 0), (y_ssd, D_INNER, 0), (p, D_INNER, P_Z // D_INNER)],
        [w["ssm_norm"]], [(D_INNER, F32), (D_INNER, BF16, "into", dp, P_Z // D_INNER)], [D_INNER])
    dp, dsink = run("attn_bwd", _attn_bwd, p, sinks, ao, lse, dao, dp, n_chunks)
    g["attn_sinks"] = dsink[:, 0:ATTN_HEADS]
    dxs, dbm, dcm, ddt_parts, dpar = run("ssd_bwd", _ssd_bwd, p, w["ssm_conv_w"], w["ssm_conv_b"], dt_bias, a_log,
                                         d_skip, hin, dy_ssd, n_chunks)
    dpar = jnp.sum(dpar, axis=0)
    g["ssm_dt_bias"], g["ssm_a_log"], g["ssm_d_skip"] = (dpar[i:i + 1, 0:SSM_HEADS] for i in range(3))

    def dt_grad(r0, parts):
        tot = parts[:, 0:128] + parts[:, 128:256] + parts[:, 256:384] + parts[:, 384:512]
        return [jnp.concatenate([tot, jnp.zeros((parts.shape[0], P_Z - P_DT - 128), F32)], axis=1)]

    dt_w = P_Z - P_DT
    dp, = _rowwise("dt_grad", dt_grad, n_rows, tm, [(ddt_parts, SSM_GROUPS * 128, 0)], [],
                   [(dt_w, BF16, "into", dp, P_DT // dt_w)], [])
    x_chunks = [(src, c0) for src, arr in enumerate((dxs, dbm, dcm)) for c0 in range(0, arr.shape[1], CONV_LANES)]
    dp, dconv = run("ssm_conv_bwd", _conv_bwd, "ssm_conv_bwd", p, P_XBC // CONV_DIM, [dxs, dbm, dcm], x_chunks,
                    w["ssm_conv_w"], w["ssm_conv_b"], n_rows, False, into=dp, into_blk=P_XBC // CONV_DIM)
    g["ssm_conv_w"], g["ssm_conv_b"] = dconv[0:4], dconv[4:5]
    g["w_cat"] = _matmul("in_proj_dw", n1, dp, "tn", F32)
    dn1 = run("in_proj_dx", _matmul, "in_proj_dx", dp, w["w_cat"], "nt", F32)

    def pre_mix_bwd(r0, dn, d1, xt, hd, wn):
        dx, dw = _rms_bwd(dn, h0_tile(r0, xt, hd), wn)
        dh0 = jnp.where(_valid_rows(r0, dn.shape[0], PAD), dx + d1, 0.0)
        return [dh0, dh0, dw]

    dx_out, dhead, g["norm_pre_mix"] = _rowwise(
        "pre_mix_bwd", pre_mix_bwd, n_rows, T, [(dn1, D_MODEL, 0), (dh1, D_MODEL, 0)] + x_in,
        [w["norm_pre_mix"]], [(D_MODEL, F32, "prev", n_tok), (D_MODEL, F32, "first")], [D_MODEL])
    return jnp.sum(loss_cols), dx_out, dhead


_IN_SECTIONS = [((5152, 6176), P_Q), ((6176, 6432), P_K), ((6432, 6688), P_V), ((5120, 5152), P_DT),
                ((0, 2048), P_Z), ((6688, 8736), P_GATE), ((2048, 5120), P_XBC)]


IN_SHARD = N_IN // 4


def _shard_pieces(a, b):
    return [(j, max(a, j * IN_SHARD) - j * IN_SHARD, min(b, (j + 1) * IN_SHARD) - j * IN_SHARD)
            for j in range(4) if max(a, j * IN_SHARD) < min(b, (j + 1) * IN_SHARD)]


def _to_cat(w4):
    parts, at = [], 0
    for (a, b), off in _IN_SECTIONS:
        if off > at:
            parts.append(jnp.zeros((w4.shape[1], off - at), w4.dtype))
        parts += [w4[j, :, lo:hi] for j, lo, hi in _shard_pieces(a, b)]
        at = off + (b - a)
    return jnp.concatenate(parts, axis=1)


def _from_cat(g_cat):
    shards = [[] for _ in range(4)]
    for (a, b), off in sorted(_IN_SECTIONS):
        for j, lo, hi in _shard_pieces(a, b):
            start = off + j * IN_SHARD + lo - a
            shards[j].append(g_cat[:, start:start + hi - lo])
    return jnp.stack([jnp.concatenate(s, axis=1) for s in shards])


LANES = 1024
_BIG = [("w_in", 1024, 2184, "chip"), ("w_ssm_out", 512, 1024, "row"), ("w_attn_out", 256, 1024, "row"),
        ("w_mix_out", 256, 1024, "row"), ("w_ffn_up", 1024, 1408, "col"), ("w_ffn_down", 704, 1024, "row"),
        ("small", 32, LANES, "chip")]
_SMALL_SHARDED = [("ssm_conv_w", (4, 768), 1), ("ffn_conv_w", (3, 1408), 1), ("meta_tokens", (16, 256), 1)]
_REPLICATED = [("norm_pre_mix", 1024), ("ssm_conv_b", 3072), ("ssm_dt_bias", 32), ("ssm_a_log", 32),
               ("ssm_d_skip", 32), ("ssm_norm", 2048), ("attn_sinks", 16), ("norm_post_mix", 1024),
               ("norm_pre_ffn", 1024), ("ffn_conv_b", 5632), ("norm_post_ffn", 1024)]
SMALL_ROWS = 16
WEIGHT_ORDER = ["meta_tokens", "norm_pre_mix", "w_in", "ssm_conv_w", "ssm_conv_b", "ssm_dt_bias", "ssm_a_log",
                "ssm_d_skip", "ssm_norm", "w_ssm_out", "attn_sinks", "w_attn_out", "w_mix_out", "norm_post_mix",
                "norm_pre_ffn", "w_ffn_up", "ffn_conv_w", "ffn_conv_b", "w_ffn_down", "norm_post_ffn"]


def _flatten(parts, rows):
    flat = jnp.concatenate([a.reshape(-1) for a in parts])
    return jnp.pad(flat, (0, rows * LANES - flat.shape[0])).reshape(rows, LANES)


def _unflatten(flat, shapes):
    flat = flat.reshape(-1)
    out, off = [], 0
    for shp in shapes:
        n = math.prod(shp)
        out.append(flat[off:off + n].reshape(shp))
        off += n
    return out


def _shard_of(full, chip, shape, axis):
    return lax.slice_in_dim(full, chip * shape[axis], (chip + 1) * shape[axis], axis=axis)


def _full_shape(r, c, layout):
    return {"row": (4 * r, c), "col": (r, 4 * c), "chip": (4, r, c)}[layout]


def _shard_view(ref, r, c, layout, chip):
    if layout == "row":
        return ref.at[pl.ds(pl.multiple_of(chip * r, 16), r), :]
    if layout == "col":
        return ref.at[:, pl.ds(pl.multiple_of(chip * c, 128), c)]
    return ref.at[chip]


def _half_view(ref, r, c, layout, chip, half):
    hr = r // 2
    if layout == "row":
        return ref.at[pl.ds(pl.multiple_of(chip * r + half * hr, 16), hr), :]
    r0 = pl.multiple_of(half * hr, 16)
    if layout == "col":
        return ref.at[pl.ds(r0, hr), pl.ds(pl.multiple_of(chip * c, 128), c)]
    return ref.at[chip, pl.ds(r0, hr), :]


def _mesh_pos():
    return lax.axis_index("x"), lax.axis_index("y"), lax.axis_index("c")


def _other_chips(x, y):
    return [(1 - x, y), (x, 1 - y), (1 - x, 1 - y)]


def _chip_index(x, y):
    return 2 * x + y


def _run_exchange(name, ex):
    n_in, n_out = len(ex.ins), len(ex.out_shapes)

    def body(*refs):
        in_refs, out_refs = refs[:n_in], refs[n_in:n_in + n_out]
        send_sems, recv_sems = refs[n_in + n_out:]
        copies = [pltpu.make_async_remote_copy(src_ref=s, dst_ref=d, send_sem=send_sems.at[i], recv_sem=recv_sems.at[i],
                                               device_id=dev, device_id_type=MESH)
                  for i, (s, d, dev) in enumerate(ex.make_copies(in_refs, out_refs))]
        assert len(copies) == ex.n_copies
        for cp in copies:
            cp.start()
        for cp in copies:
            cp.wait()

    return pl.pallas_call(
        body, name=name, in_specs=[ANY] * n_in, out_specs=[ANY] * n_out, out_shape=list(ex.out_shapes),
        scratch_shapes=[pltpu.SemaphoreType.DMA((ex.n_copies,)), pltpu.SemaphoreType.DMA((ex.n_copies,))],
        compiler_params=pltpu.CompilerParams(has_side_effects=True),
    )(*ex.ins)


def _join(*exs):
    def make(in_refs, out_refs):
        copies, i0, o0 = [], 0, 0
        for ex in exs:
            copies += ex.make_copies(in_refs[i0:i0 + len(ex.ins)], out_refs[o0:o0 + len(ex.out_shapes)])
            i0, o0 = i0 + len(ex.ins), o0 + len(ex.out_shapes)
        return copies

    aliases, i0, o0 = {}, 0, 0
    for ex in exs:
        aliases.update({i0 + k: o0 + v for k, v in ex.aliases.items()})
        i0, o0 = i0 + len(ex.ins), o0 + len(ex.out_shapes)
    return _Exchange([a for ex in exs for a in ex.ins], [s for ex in exs for s in ex.out_shapes], make,
                     sum(ex.n_copies for ex in exs), aliases)


def _split(exs, results):
    out, o0 = [], 0
    for ex in exs:
        out.append(list(results[o0:o0 + len(ex.out_shapes)]))
        o0 += len(ex.out_shapes)
    return out


def _gather_ici(entries, shards):
    def make(in_refs, out_refs):
        x, y, c = _mesh_pos()
        j = _chip_index(x, y)
        copies = []
        for ref_in, ref_out, (_, r, cc, lay) in zip(in_refs, out_refs, entries):
            copies.append((ref_in, _shard_view(ref_out, r, cc, lay, j), None))
            mine = ref_in.at[pl.ds(pl.multiple_of(c * (r // 2), 16), r // 2), :]
            copies += [(mine, _half_view(ref_out, r, cc, lay, j, c), (*ch, c)) for ch in _other_chips(x, y)]
        return copies

    shapes = [jax.ShapeDtypeStruct(_full_shape(r, cc, lay), s.dtype) for s, (_, r, cc, lay) in zip(shards, entries)]
    return _Exchange(list(shards), shapes, make, 4 * len(entries))


def _gather_pass_on(entries, fulls):
    def make(in_refs, out_refs):
        x, y, c = _mesh_pos()
        copies = []
        for ref, (_, r, cc, lay) in zip(out_refs, entries):
            for ch in _other_chips(x, y):
                landed = _half_view(ref, r, cc, lay, _chip_index(*ch), c)
                copies.append((landed, landed, (x, y, 1 - c)))
        return copies

    return _Exchange(list(fulls), [jax.ShapeDtypeStruct(f.shape, f.dtype) for f in fulls], make, 3 * len(entries),
                     {a: a for a in range(len(entries))})


def _gather_weights(entries, shards):
    n = len(entries)

    def body(*refs):
        ins, outs = refs[:n], refs[n:2 * n]
        send_sems, recv_sems, local_sems = refs[2 * n:]
        x, y, c = _mesh_pos()
        j = _chip_index(x, y)
        sibling = (x, y, 1 - c)
        chips = _other_chips(x, y)
        idx = [_chip_index(*ch) for ch in chips]

        def remote(k, src, dst, dev):
            return pltpu.make_async_remote_copy(src_ref=src, dst_ref=dst, send_sem=send_sems.at[k],
                                                recv_sem=recv_sems.at[k], device_id=dev, device_id_type=MESH)

        own = [pltpu.make_async_copy(ins[a], _shard_view(outs[a], r, cc, lay, j), local_sems.at[a])
               for a, (_, r, cc, lay) in enumerate(entries)]
        for cp in own:
            cp.start()
        first, passed = [], []
        for a, (_, r, cc, lay) in enumerate(entries):
            mine = ins[a].at[pl.ds(pl.multiple_of(c * (r // 2), 16), r // 2), :]
            for k, ch in enumerate(chips):
                first.append(remote(6 * a + k, mine, _half_view(outs[a], r, cc, lay, j, c), (*ch, c)))
                landed = _half_view(outs[a], r, cc, lay, idx[k], c)
                passed.append(remote(6 * a + 3 + k, landed, landed, sibling))
        for cp in first:
            cp.start()
        for a, (_, r, cc, lay) in enumerate(entries):
            for k in range(3):
                landed = _half_view(outs[a], r, cc, lay, idx[k], c)
                remote(6 * a + k, landed, landed, sibling).wait_recv()
                passed[3 * a + k].start()
        for a, (_, r, cc, lay) in enumerate(entries):
            for k in range(3):
                theirs = _half_view(outs[a], r, cc, lay, idx[k], 1 - c)
                remote(6 * a + 3 + k, theirs, theirs, sibling).wait_recv()
        for cp in first + passed:
            cp.wait_send()
        for cp in own:
            cp.wait()

    return pl.pallas_call(
        body, name="gather_weights", in_specs=[ANY] * n, out_specs=[ANY] * n,
        out_shape=[jax.ShapeDtypeStruct(_full_shape(r, cc, lay), s.dtype) for s, (_, r, cc, lay) in zip(shards, entries)],
        scratch_shapes=[pltpu.SemaphoreType.DMA((6 * n,)), pltpu.SemaphoreType.DMA((6 * n,)), pltpu.SemaphoreType.DMA((n,))],
        compiler_params=pltpu.CompilerParams(has_side_effects=True),
    )(*shards)


def _pair_exchange(entries, grads):
    def make(in_refs, out_refs):
        x, y, c = _mesh_pos()
        return [(_half_view(ref_in, r, cc, lay, i, 1 - c), ref_out.at[i], (x, y, 1 - c))
                for ref_in, ref_out, (_, r, cc, lay) in zip(in_refs, out_refs, entries) for i in range(4)]

    return _Exchange(list(grads), [jax.ShapeDtypeStruct((4, r // 2, cc), F32) for _, r, cc, _ in entries], make,
                     4 * len(entries))


def _whole_to_sibling(arrays):
    def make(in_refs, out_refs):
        x, y, c = _mesh_pos()
        return [(r, o, (x, y, 1 - c)) for r, o in zip(in_refs, out_refs)]

    return _Exchange(list(arrays), [jax.ShapeDtypeStruct(a.shape, a.dtype) for a in arrays], make, len(arrays))


def _chip_exchange(psends):
    def make(in_refs, out_refs):
        x, y, c = _mesh_pos()
        return [(ref_in.at[_chip_index(*ch)], ref_out.at[k], (*ch, c))
                for ref_in, ref_out in zip(in_refs, out_refs) for k, ch in enumerate(_other_chips(x, y))]

    return _Exchange(list(psends), [jax.ShapeDtypeStruct((3,) + p.shape[1:], p.dtype) for p in psends], make,
                     3 * len(psends))


def _to_all_chips(array):
    def make(in_refs, out_refs):
        x, y, c = _mesh_pos()
        return [(in_refs[0], out_refs[0].at[k], (*ch, c)) for k, ch in enumerate(_other_chips(x, y))]

    return _Exchange([array], [jax.ShapeDtypeStruct((3,) + array.shape, array.dtype)], make, 3)


SUM_ROWS = 256
ADAM_ROWS = 128


def _pair_sum(name, grad, recv, ids, r, c, layout):
    hr = r // 2
    tr = _row_tile(hr, SUM_ROWS)
    nb = hr // tr

    def body(ids_ref, g_ref, r_ref, send_ref, own_ref):
        s = g_ref[...] + r_ref[...]
        send_ref[...] = s.astype(send_ref.dtype)

        @pl.when(pl.program_id(1) == ids_ref[1])
        def _():
            own_ref[...] = s

    if layout == "row":
        g_spec = pl.BlockSpec((tr, c), lambda t, j, ids_ref: ((j * r + ids_ref[0] * hr) // tr + t, 0))
    elif layout == "col":
        g_spec = pl.BlockSpec((tr, c), lambda t, j, ids_ref: (ids_ref[0] * nb + t, j))
    else:
        g_spec = pl.BlockSpec((None, tr, c), lambda t, j, ids_ref: (j, ids_ref[0] * nb + t, 0))
    grid_spec = pltpu.PrefetchScalarGridSpec(
        num_scalar_prefetch=1, grid=(nb, 4),
        in_specs=[g_spec, pl.BlockSpec((None, tr, c), lambda t, j, ids_ref: (j, t, 0))],
        out_specs=[pl.BlockSpec((None, tr, c), lambda t, j, ids_ref: (j, t, 0)),
                   pl.BlockSpec((tr, c), lambda t, j, ids_ref: (t, 0))])
    return pl.pallas_call(
        body, name=name, grid_spec=grid_spec,
        out_shape=[jax.ShapeDtypeStruct((4, hr, c), BF16), jax.ShapeDtypeStruct((hr, c), F32)],
        compiler_params=_cparams(2),
    )(ids, grad, recv)


def _chip_sum(name, own, recv):
    hr, c = own.shape
    tr = _row_tile(hr, SUM_ROWS)

    def body(o_ref, r_ref, out_ref):
        out_ref[...] = ((o_ref[...] + r_ref[0].astype(F32)) + r_ref[1].astype(F32)) + r_ref[2].astype(F32)

    return pl.pallas_call(
        body, name=name, grid=(hr // tr,),
        in_specs=[pl.BlockSpec((tr, c), lambda i: (i, 0)), pl.BlockSpec((3, tr, c), lambda i: (0, i, 0))],
        out_specs=pl.BlockSpec((tr, c), lambda i: (i, 0)),
        out_shape=jax.ShapeDtypeStruct((hr, c), F32), compiler_params=_cparams(1),
    )(own, recv)


def _chip_sum_small(own, recv, ids):
    def body(ids_ref, o_ref, r_ref, out_ref):
        j = ids_ref[1]
        total = None
        for i in range(4):
            m = jnp.bitwise_xor(i, j)
            term = jnp.where(m == 0, o_ref[...], jnp.where(m == 2, r_ref[0], jnp.where(m == 1, r_ref[1], r_ref[2])))
            total = term if total is None else total + term
        out_ref[...] = total

    grid_spec = pltpu.PrefetchScalarGridSpec(
        num_scalar_prefetch=1, grid=(1,),
        in_specs=[pl.BlockSpec(own.shape, lambda i, ids_ref: (0, 0)), pl.BlockSpec(recv.shape, lambda i, ids_ref: (0, 0, 0))],
        out_specs=pl.BlockSpec(own.shape, lambda i, ids_ref: (0, 0)))
    return pl.pallas_call(body, name="chip_sum_small", grid_spec=grid_spec,
                          out_shape=jax.ShapeDtypeStruct(own.shape, F32), compiler_params=_cparams(1))(ids, own, recv)


def _adamw(name, w, m, v, mine, theirs, ids):
    rows, cols = w.shape
    half = rows // 2
    tr = _row_tile(half, ADAM_ROWS, unit=8)
    nb = half // tr
    c1 = 1.0 / (1.0 - ADAM_B1 ** ADAM_STEP)
    c2 = 1.0 / (1.0 - ADAM_B2 ** ADAM_STEP)

    def body(ids_ref, w_ref, m_ref, v_ref, mine_ref, theirs_ref, g_out, d_out, m_out, v_out):
        g = jnp.where(pl.program_id(0) == ids_ref[0], mine_ref[...], theirs_ref[...])
        m_new = ADAM_B1 * m_ref[...] + (1.0 - ADAM_B1) * g
        v_new = ADAM_B2 * v_ref[...] + (1.0 - ADAM_B2) * (g * g)
        d_out[...] = -ADAM_LR * ((m_new * c1) / (jnp.sqrt(v_new * c2) + ADAM_EPS) + ADAM_WD * w_ref[...])
        g_out[...] = g
        m_out[...] = m_new
        v_out[...] = v_new

    full = pl.BlockSpec((tr, cols), lambda h, i, ids_ref: (h * nb + i, 0))
    part = pl.BlockSpec((tr, cols), lambda h, i, ids_ref: (i, 0))
    grid_spec = pltpu.PrefetchScalarGridSpec(num_scalar_prefetch=1, grid=(2, nb),
                                             in_specs=[full, full, full, part, part], out_specs=[full] * 4)
    return pl.pallas_call(
        body, name=name, grid_spec=grid_spec,
        out_shape=[jax.ShapeDtypeStruct((rows, cols), F32)] * 4, compiler_params=_cparams(2),
    )(ids, w, m, v, mine, theirs)


def _small_shard(parts):
    return _flatten(parts, _BIG[-1][1])


_ENTRY = {e[0]: e for e in _BIG}
FFN_MATS = ("w_ffn_down", "w_ffn_up")
MIXER_MATS = ("w_mix_out", "w_ssm_out", "w_attn_out")


class _StepPlan:
    def __init__(self, w, late_shards, shards, ids):
        self.w, self.g = w, {}
        self.late_shards, self.shards, self.ids = late_shards, shards, ids
        self.sums, self.halves, self.results = {}, {}, {}

    def run(self, name, fn, *args, **kw):
        at = getattr(self, "_at_" + name, None)
        if at is None:
            return fn(*args, **kw)
        exchange, landed = at()
        res, extra = fn(*args, bg=exchange, **kw)
        landed(extra)
        return res

    def _at_ssd_fwd(self):
        def landed(fulls):
            self.partly_gathered = fulls

        return _gather_ici([_ENTRY[n] for n in MIXER_MATS], [self.late_shards[n] for n in MIXER_MATS]), landed

    def _at_attn_fwd(self):
        stages = (_gather_pass_on([_ENTRY[n] for n in MIXER_MATS], self.partly_gathered),
                  _gather_ici([_ENTRY[n] for n in FFN_MATS], [self.late_shards[n] for n in FFN_MATS]))

        def landed(extra):
            mixer, self.partly_gathered = _split(stages, extra)
            self.w.update(zip(MIXER_MATS, mixer))

        return _join(*stages), landed

    def _at_ssm_gate_norm(self):
        return (_gather_pass_on([_ENTRY[n] for n in FFN_MATS], self.partly_gathered),
                lambda fulls: self.w.update(zip(FFN_MATS, fulls)))

    def pair_sums(self, names, grads, recv):
        for n, gr, rv in zip(names, grads, recv):
            _, r, c, lay = _ENTRY[n]
            self.sums[n] = _pair_sum("pair_sum_" + n, gr, rv, self.ids, r, c, lay)

    def chip_sums(self, names, recv):
        for n, rv in zip(names, recv):
            self.halves[n] = _chip_sum("chip_sum_" + n, self.sums[n][1], rv)

    def adamw(self, names, theirs):
        for n, th in zip(names, theirs):
            sh = self.shards[n]
            self.results[n] = _adamw("adamw_" + n, sh["w"], sh["m"], sh["v"], self.halves[n], th, self.ids)

    def _pair_stage(self, names, grads):
        return (_pair_exchange([_ENTRY[n] for n in names], grads),
                lambda recv: self.pair_sums(names, grads, recv))

    def _at_ffn_up_dx(self):
        return self._pair_stage(FFN_MATS, [self.g[n] for n in FFN_MATS])

    def _at_ssm_gate_norm_bwd(self):
        return self._pair_stage(MIXER_MATS, [self.g[n] for n in MIXER_MATS])

    def _at_attn_bwd(self):
        return _chip_exchange([self.sums[n][0] for n in FFN_MATS]), lambda recv: self.chip_sums(FFN_MATS, recv)

    def _at_ssd_bwd(self):
        stages = (_chip_exchange([self.sums[n][0] for n in MIXER_MATS]),
                  _whole_to_sibling([self.halves[n] for n in FFN_MATS]))

        def landed(extra):
            recv, theirs = _split(stages, extra)
            self.chip_sums(MIXER_MATS, recv)
            self.adamw(FFN_MATS, theirs)

        return _join(*stages), landed

    def _at_ssm_conv_bwd(self):
        return _whole_to_sibling([self.halves[n] for n in MIXER_MATS]), lambda theirs: self.adamw(MIXER_MATS, theirs)

    def _at_in_proj_dx(self):
        grads = [_from_cat(self.g.pop("w_cat"))]
        self.pair_sums(("w_in",), grads, _run_exchange("grad_pair_exchange_w_in", _pair_exchange([_ENTRY["w_in"]], grads)))
        return _chip_exchange([self.sums["w_in"][0]]), lambda recv: self.chip_sums(("w_in",), recv)

    def finish(self, g_small, g_rep, rep_shards):
        stages = (_pair_exchange([_ENTRY["small"]], [g_small]), _whole_to_sibling([g_rep]))
        recv_small, recv_rep = _split(stages, _run_exchange("grad_pair_exchange_tail", _join(*stages)))
        self.pair_sums(("small",), [g_small], recv_small)
        p_rep, = _rowwise("pair_sum_replicated", lambda r0, a, b: [a + b], SMALL_ROWS, SMALL_ROWS,
                          [(g_rep, LANES, 0), (recv_rep[0], LANES, 0)], [], [(LANES, F32)], [])
        stages = (_chip_exchange([self.sums["small"][0]]), _to_all_chips(p_rep))
        recv, recv_rep = _split(stages, _run_exchange("grad_chip_exchange_tail", _join(*stages)))
        self.chip_sums(("small",), recv)
        g_rep_tot = _chip_sum_small(p_rep, recv_rep[0], self.ids)
        last = ("w_in", "small")
        self.adamw(last, _run_exchange("grad_half_share_tail", _whole_to_sibling([self.halves[n] for n in last])))
        ids_lo = self.ids * jnp.array([0, 1], jnp.int32)
        self.results["replicated"] = _adamw("adamw_replicated", rep_shards["w"], rep_shards["m"], rep_shards["v"],
                                            g_rep_tot[0:SMALL_ROWS // 2], g_rep_tot[SMALL_ROWS // 2:], ids_lo)


def kernel(x, meta_tokens, norm_pre_mix, w_in, ssm_conv_w, ssm_conv_b, ssm_dt_bias, ssm_a_log, ssm_d_skip, ssm_norm, w_ssm_out, attn_sinks, w_attn_out, w_mix_out, norm_post_mix, norm_pre_ffn, w_ffn_up, ffn_conv_w, ffn_conv_b, w_ffn_down, norm_post_ffn, loss_target, m_meta_tokens, m_norm_pre_mix, m_w_in, m_ssm_conv_w, m_ssm_conv_b, m_ssm_dt_bias, m_ssm_a_log, m_ssm_d_skip, m_ssm_norm, m_w_ssm_out, m_attn_sinks, m_w_attn_out, m_w_mix_out, m_norm_post_mix, m_norm_pre_ffn, m_w_ffn_up, m_ffn_conv_w, m_ffn_conv_b, m_w_ffn_down, m_norm_post_ffn, v_meta_tokens, v_norm_pre_mix, v_w_in, v_ssm_conv_w, v_ssm_conv_b, v_ssm_dt_bias, v_ssm_a_log, v_ssm_d_skip, v_ssm_norm, v_w_ssm_out, v_attn_sinks, v_w_attn_out, v_w_mix_out, v_norm_post_mix, v_norm_pre_ffn, v_w_ffn_up, v_ffn_conv_w, v_ffn_conv_b, v_w_ffn_down, v_norm_post_ffn):
    args = dict(locals())
    squeeze = lambda a: a.reshape(a.shape[-2:])
    wts = {n: squeeze(args[n]) for n in WEIGHT_ORDER}
    mom = {n: squeeze(args["m_" + n]) for n in WEIGHT_ORDER}
    var = {n: squeeze(args["v_" + n]) for n in WEIGHT_ORDER}
    x_i, y_i, c_i = _mesh_pos()
    ids = jnp.stack([c_i, _chip_index(x_i, y_i)]).astype(jnp.int32)
    big_names = [n for n, _, _, _ in _BIG[:-1]]
    small_names = [n for n, _, _ in _SMALL_SHARDED]
    rep_names = [n for n, _ in _REPLICATED]

    stacks = {"w": wts, "m": mom, "v": var}
    shards = {n: {k: d[n] for k, d in stacks.items()} for n in big_names}
    shards["small"] = {k: _small_shard([d[n] for n in small_names]) for k, d in stacks.items()}
    rep_shards = {k: _flatten([d[n] for n in rep_names], SMALL_ROWS) for k, d in stacks.items()}

    w_in4, small_all = _gather_weights([_ENTRY["w_in"], _ENTRY["small"]], [wts["w_in"].astype(BF16), shards["small"]["w"]])
    w = {n: wts[n] for n in rep_names}
    w["w_cat"] = _to_cat(w_in4)
    small_parts = [_unflatten(small_all[i], [shp for _, shp, _ in _SMALL_SHARDED]) for i in range(4)]
    for k, (n, _, axis) in enumerate(_SMALL_SHARDED):
        w[n] = jnp.concatenate([small_parts[i][k] for i in range(4)], axis=axis)
    plan = _StepPlan(w, {n: wts[n].astype(BF16) for n in MIXER_MATS + FFN_MATS}, shards, ids)

    head = jnp.concatenate([jnp.zeros((PAD, D_MODEL), F32), w["meta_tokens"]], axis=0)
    loss_sum, dx, dhead = _local_step(x[0], head, loss_target[0], plan)
    loss = lax.psum(loss_sum * (0.5 / D_MODEL), ("x", "y", "c"))
    g = plan.g
    g["meta_tokens"] = dhead[PAD:]
    g_small = jnp.stack([_small_shard([_shard_of(g[n], i, shp, ax) for n, shp, ax in _SMALL_SHARDED]) for i in range(4)])
    plan.finish(g_small, _flatten([g[n] for n in rep_names], SMALL_ROWS), rep_shards)

    results = {}
    for kind in range(4):
        results.update({(kind, n): plan.results[n][kind] for n in big_names})
        parts = _unflatten(plan.results["small"][kind], [shp for _, shp, _ in _SMALL_SHARDED])
        results.update({(kind, n): parts[k] for k, n in enumerate(small_names)})
        parts = _unflatten(plan.results["replicated"][kind], [(1, width) for _, width in _REPLICATED])
        results.update({(kind, n): parts[k] for k, n in enumerate(rep_names)})
    outs = [results[kind, n].reshape(args[n].shape) for kind in range(4) for n in WEIGHT_ORDER]
    return (loss, dx[None], *outs)
```

```python
import math
from typing import Any, Callable, NamedTuple, Sequence

import jax
import jax.numpy as jnp
from jax import lax
from jax.experimental import pallas as pl
from jax.experimental.pallas import tpu as pltpu

F32 = jnp.float32
BF16 = jnp.bfloat16

D_MODEL = 1024
N_META = 16
T = 128
PAD = T - N_META
D_INNER = 2048
SSM_HEADS = 32
HEAD_P = 64
SSM_GROUPS = 4
GROUP_W = D_INNER // SSM_GROUPS
D_STATE = 128
CONV_DIM = D_INNER + 2 * SSM_GROUPS * D_STATE
ATTN_HEADS = 16
KV_HEADS = 4
ATTN_W = 1024
KV_W = 256
FFN_DIM = 2816
N_IN = 8736
EPS = 1e-6
NEG = -1e30
SCALE = 0.125

P_Q, P_K, P_V, P_DT, P_Z, P_GATE, P_XBC = 0, 1024, 1280, 1536, 2048, 4096, 6144
QKV_W = 1536
P_W = 9216

ADAM_LR, ADAM_B1, ADAM_B2, ADAM_EPS, ADAM_WD, ADAM_STEP = 0.001, 0.9, 0.999, 1e-08, 0.01, 10

VMEM_BUDGET = 40 * 1024 * 1024
VMEM_LIMIT = 56 * 1024 * 1024
MESH = pl.DeviceIdType.MESH
ANY = pl.BlockSpec(memory_space=pl.ANY)


def _cparams(n_axes, **kw):
    return pltpu.CompilerParams(dimension_semantics=("arbitrary",) * n_axes, vmem_limit_bytes=VMEM_LIMIT, **kw)


class _Exchange(NamedTuple):
    ins: Sequence[Any]
    out_shapes: Sequence[Any]
    make_copies: Callable
    n_copies: int
    aliases: dict = {}


def _call(body, name, grid, in_specs, out_specs, out_shape, operands, scratch_shapes=(), aliases=None, bg=None):
    aliases = dict(aliases or {})
    if bg is None:
        return pl.pallas_call(body, name=name, grid=grid, in_specs=in_specs, out_specs=out_specs, out_shape=out_shape,
                              scratch_shapes=list(scratch_shapes), input_output_aliases=aliases,
                              compiler_params=_cparams(len(grid)))(*operands)
    n_in, n_out, n_scr = len(in_specs), len(out_specs), len(scratch_shapes)
    nb_in, nb_out = len(bg.ins), len(bg.out_shapes)

    def hosted(*refs):
        ins, bg_ins = refs[:n_in], refs[n_in:n_in + nb_in]
        outs = refs[n_in + nb_in:n_in + nb_in + n_out]
        bg_outs = refs[n_in + nb_in + n_out:n_in + nb_in + n_out + nb_out]
        scratch = refs[n_in + nb_in + n_out + nb_out:n_in + nb_in + n_out + nb_out + n_scr]
        send_sems, recv_sems = refs[-2:]
        pids = [pl.program_id(a) for a in range(len(grid))]
        first, last = pids[0] == 0, pids[0] == grid[0] - 1
        for p, g in zip(pids[1:], grid[1:]):
            first, last = first & (p == 0), last & (p == g - 1)
        copies = []
        for k, (src, dst, peer) in enumerate(bg.make_copies(bg_ins, bg_outs)):
            if peer is None:
                copies.append(pltpu.make_async_copy(src, dst, send_sems.at[k]))
            else:
                copies.append(pltpu.make_async_remote_copy(src_ref=src, dst_ref=dst, send_sem=send_sems.at[k],
                                                           recv_sem=recv_sems.at[k], device_id=peer, device_id_type=MESH))
        assert len(copies) == bg.n_copies

        @pl.when(first)
        def _():
            for cp in copies:
                cp.start()

        body(*ins, *outs, *scratch)

        @pl.when(last)
        def _():
            for cp in copies:
                cp.wait()

    aliases = {(k if k < n_in else k + nb_in): v for k, v in aliases.items()}
    aliases.update({n_in + k: n_out + v for k, v in bg.aliases.items()})
    res = pl.pallas_call(
        hosted, name=name, grid=grid, in_specs=list(in_specs) + [ANY] * nb_in, out_specs=list(out_specs) + [ANY] * nb_out,
        out_shape=list(out_shape) + list(bg.out_shapes), input_output_aliases=aliases,
        scratch_shapes=list(scratch_shapes) + [pltpu.SemaphoreType.DMA((bg.n_copies,))] * 2,
        compiler_params=_cparams(len(grid), has_side_effects=True))(*operands, *bg.ins)
    return res[:n_out], res[n_out:]


def _sigmoid(x):
    return 1.0 / (1.0 + jnp.exp(-x))


def _silu(x):
    return x * _sigmoid(x)


def _silu_grad(x):
    s = _sigmoid(x)
    return x * s, s * (1.0 + x * (1.0 - s))


def _dsilu(x):
    return _silu_grad(x)[1]


def _softplus(x):
    e = jnp.exp(-jnp.abs(x))
    small = e * (1.0 - e * (0.5 - e * (1.0 / 3.0)))
    return jnp.maximum(x, 0.0) + jnp.where(e < 0.01, small, jnp.log(1.0 + e))


def _rms(x, w):
    r = lax.rsqrt(jnp.mean(x * x, axis=-1, keepdims=True) + EPS)
    return x * r * w


def _rms_bwd(dy, x, w):
    r = lax.rsqrt(jnp.mean(x * x, axis=-1, keepdims=True) + EPS)
    xh = x * r
    g = dy * w
    dx = r * (g - xh * jnp.mean(g * xh, axis=-1, keepdims=True))
    dw = jnp.sum(dy * xh, axis=0, keepdims=True)
    return dx, dw


def _dot(a, b):
    return jnp.dot(a, b, preferred_element_type=F32)


def _dot_nt(a, b):
    return lax.dot_general(a, b, (((1,), (1,)), ((), ())), preferred_element_type=F32)


def _dot_tn(a, b):
    return lax.dot_general(a, b, (((0,), (0,)), ((), ())), preferred_element_type=F32)


def _split3(x):
    hi = x.astype(BF16)
    r = x - hi.astype(F32)
    mid = r.astype(BF16)
    lo = (r - mid.astype(F32)).astype(BF16)
    return hi, mid, lo


def _xdot(x, e):
    hi, mid, lo = _split3(x)
    return _dot(hi, e) + _dot(mid, e) + _dot(lo, e)


def _xdot_l(e, x):
    hi, mid, lo = _split3(x)
    return _dot(e, hi) + _dot(e, mid) + _dot(e, lo)


def _iota(shape, dim):
    return lax.broadcasted_iota(jnp.int32, shape, dim)


def _divisors(n, unit):
    return [t for t in range(unit, n + 1, unit) if n % t == 0]


MIN_MATMUL_STEPS = 8


def _matmul_tiles(m, n, k, a_bytes, b_bytes, o_bytes, m_unit):
    best = None
    for tm in _divisors(m, m_unit):
        for tn in _divisors(n, 128):
            for tk in _divisors(k, 128):
                acc = 0 if tk == k else tm * tn * 4
                vm = 2 * (tm * tk * a_bytes + tk * tn * b_bytes + tm * tn * o_bytes) + acc
                if vm > VMEM_BUDGET:
                    continue
                steps = (m // tm) * (n // tn) * (k // tk)
                score = (tk == k, min(steps, MIN_MATMUL_STEPS), min(tm, 256), tm * tn * tk)
                if best is None or score > best[0]:
                    best = (score, (tm, tn, tk))
    return best[1]


def _matmul(name, a, b, mode, out_dtype, bg=None):
    if mode == "nn":
        (m, k), n = a.shape, b.shape[1]
    elif mode == "nt":
        (m, k), n = a.shape, b.shape[0]
    else:
        (k, m), n = a.shape, b.shape[1]
    ab, bb, ob = a.dtype.itemsize, b.dtype.itemsize, jnp.dtype(out_dtype).itemsize
    tm, tn, tk = _matmul_tiles(m, n, k, ab, bb, ob, 128 if mode == "tn" else 16)
    nk = k // tk
    dot = {"nn": _dot, "nt": _dot_nt, "tn": _dot_tn}[mode]

    def body(a_ref, b_ref, o_ref, *scratch):
        prod = dot(a_ref[...].astype(BF16), b_ref[...].astype(BF16))
        if nk == 1:
            o_ref[...] = prod.astype(o_ref.dtype)
        else:
            acc_ref, = scratch
            kk = pl.program_id(2)

            @pl.when(kk == 0)
            def _():
                acc_ref[...] = prod

            @pl.when(kk > 0)
            def _():
                acc_ref[...] += prod

            @pl.when(kk == nk - 1)
            def _():
                o_ref[...] = acc_ref[...].astype(o_ref.dtype)

    a_spec = pl.BlockSpec((tk, tm), lambda i, j, kk: (kk, i)) if mode == "tn" else pl.BlockSpec((tm, tk), lambda i, j, kk: (i, kk))
    b_spec = pl.BlockSpec((tn, tk), lambda i, j, kk: (j, kk)) if mode == "nt" else pl.BlockSpec((tk, tn), lambda i, j, kk: (kk, j))
    res = _call(body, name, (m // tm, n // tn, nk), [a_spec, b_spec], [pl.BlockSpec((tm, tn), lambda i, j, kk: (i, j))],
                [jax.ShapeDtypeStruct((m, n), out_dtype)], [a, b],
                scratch_shapes=[] if nk == 1 else [pltpu.VMEM((tm, tn), F32)], bg=bg)
    return res[0] if bg is None else (res[0][0], res[1])


def _row_tile(n_rows, cap, unit=16):
    return max(t for t in _divisors(n_rows, unit) if t <= cap)


ROW_SUB = 384
GROUP_UNROLL = 4


def _rowwise(name, fn, n_rows, tm, row_ins, full_ins, row_outs, acc_outs, bg=None):
    n_in = len(row_ins) + len(full_ins)
    n_ro = len(row_outs)
    into = [(k, o[3]) for k, o in enumerate(row_outs) if len(o) > 2 and o[2] == "into"]

    n_row_in = len(row_ins)
    sub = min(tm, ROW_SUB)

    def body(*refs):
        i = pl.program_id(0)
        outs = refs[n_in + len(into):]

        sums = tuple(jnp.zeros((1, w), F32) for w in acc_outs)
        for s in range(tm // sub):
            rows = pl.ds(s * sub, sub)
            vals = [r[rows, :] for r in refs[:n_row_in]] + [r[...] for r in refs[n_row_in:n_in]]
            res = fn(i * tm + s * sub, *vals)
            for o, r, v in zip(row_outs, outs[:n_ro], res[:n_ro]):
                if len(o) > 2 and o[2] == "first":
                    @pl.when(i == 0)
                    def _(r=r, v=v, rows=rows):
                        r[rows, :] = v.astype(r.dtype)
                else:
                    r[rows, :] = v.astype(r.dtype)
            sums = tuple(a + v for a, v in zip(sums, res[n_ro:]))

        @pl.when(i == 0)
        def _():
            for r, v in zip(outs[n_ro:], sums):
                r[...] = v

        @pl.when(i > 0)
        def _():
            for r, v in zip(outs[n_ro:], sums):
                r[...] += v

    def in_spec(entry):
        w, cb = entry[1], entry[2]
        if len(entry) > 3 and entry[3] == "prev":
            return pl.BlockSpec((tm, w), lambda i: (jnp.maximum(i - 1, 0), cb))
        if len(entry) > 3 and entry[3] == "first":
            return pl.BlockSpec((tm, w), lambda i: (0, cb))
        return pl.BlockSpec((tm, w), lambda i: (i, cb))

    def out_spec(o):
        if len(o) == 2:
            return pl.BlockSpec((tm, o[0]), lambda i: (i, 0)), jax.ShapeDtypeStruct((n_rows, o[0]), o[1])
        if o[2] == "new":
            return pl.BlockSpec((tm, o[0]), lambda i: (i, o[4])), jax.ShapeDtypeStruct((n_rows, o[3]), o[1])
        if o[2] == "into":
            return pl.BlockSpec((tm, o[0]), lambda i: (i, o[4])), jax.ShapeDtypeStruct(o[3].shape, o[3].dtype)
        if o[2] == "first":
            return pl.BlockSpec((tm, o[0]), lambda i: (0, 0)), jax.ShapeDtypeStruct((tm, o[0]), o[1])
        return pl.BlockSpec((tm, o[0]), lambda i: (jnp.maximum(i - 1, 0), 0)), jax.ShapeDtypeStruct((o[3], o[0]), o[1])

    in_specs = [in_spec(e) for e in row_ins]
    in_specs += [pl.BlockSpec(a.shape, lambda i: (0, 0)) for a in full_ins]
    in_specs += [pl.BlockSpec(memory_space=pl.ANY) for _ in into]
    specs_shapes = [out_spec(o) for o in row_outs]
    out_specs = [s for s, _ in specs_shapes] + [pl.BlockSpec((1, w), lambda i: (0, 0)) for w in acc_outs]
    out_shape = [s for _, s in specs_shapes] + [jax.ShapeDtypeStruct((1, w), F32) for w in acc_outs]
    return _call(body, name, (n_rows // tm,), in_specs, out_specs, out_shape,
                 [e[0] for e in row_ins] + list(full_ins) + [arr for _, arr in into],
                 aliases={n_in + a: k for a, (k, _) in enumerate(into)}, bg=bg)


def _valid_rows(first_row, tm, lo):
    return (first_row + _iota((tm, 1), 0)) >= lo


CONV_ROWS = 128
CONV_SUB = 16
CONV_LANES = 256


def _conv_specs(tm, width, blk, n_rows, after):
    specs = [pl.BlockSpec((tm, width), lambda i: (i, blk)),
             pl.BlockSpec((8, width), lambda i: (jnp.maximum(i * (tm // 8) - 1, 0), blk))]
    if after:
        specs.append(pl.BlockSpec((16, width), lambda i: (jnp.minimum((i + 1) * (tm // 16), n_rows // 16 - 1), blk)))
    return specs


def _conv_window(win, w_ref, b_ref, taps, c0, cw, n):
    acc = b_ref[:, c0:c0 + cw] + w_ref[taps - 1:taps, c0:c0 + cw] * win[8:8 + n]
    for k in range(taps - 1):
        acc = acc + w_ref[k:k + 1, c0:c0 + cw] * win[8 - (taps - 1) + k:8 - (taps - 1) + k + n]
    return acc


def _ffn_act(name, u_raw, conv_w, conv_b, n_rows):
    tm, sub, cw = CONV_ROWS, CONV_SUB, CONV_LANES
    taps, width = conv_w.shape
    half = width // 2

    def body(cur_ref, prev_ref, w_ref, b_ref, f_ref, ext_ref):
        i = pl.program_id(0)
        ext_ref[0:8, :] = jnp.where(i > 0, prev_ref[...], 0.0)
        ext_ref[8:8 + tm, :] = cur_ref[...]
        for q in range(half // cw):
            a0, g0 = q * cw, half + q * cw

            def group(s, carry):
                r = pl.multiple_of(s * sub, sub)
                a = _conv_window(ext_ref[pl.ds(r, sub + 8), a0:a0 + cw], w_ref, b_ref, taps, a0, cw, sub)
                g = _conv_window(ext_ref[pl.ds(r, sub + 8), g0:g0 + cw], w_ref, b_ref, taps, g0, cw, sub)
                f = jnp.where(_valid_rows(i * tm + r, sub, PAD), _silu(a) * g, 0.0)
                f_ref[pl.ds(r, sub), a0:a0 + cw] = f.astype(f_ref.dtype)
                return carry

            lax.fori_loop(0, tm // sub, group, 0, unroll=GROUP_UNROLL)

    return pl.pallas_call(
        body, name=name, grid=(n_rows // tm,),
        in_specs=_conv_specs(tm, width, 0, n_rows, False) + [pl.BlockSpec((taps, width), lambda i: (0, 0)),
                                                             pl.BlockSpec((1, width), lambda i: (0, 0))],
        out_specs=pl.BlockSpec((tm, half), lambda i: (i, 0)),
        out_shape=jax.ShapeDtypeStruct((n_rows, half), BF16),
        scratch_shapes=[pltpu.VMEM((tm + 8, width), F32)],
        compiler_params=_cparams(1),
    )(u_raw, u_raw, conv_w, conv_b)


def _conv_bwd(name, raw, raw_blk, dsrcs, chunk_src, conv_w, conv_b, n_rows, gated, into=None, into_blk=0, bg=None):
    taps, width = conv_w.shape
    half = width // 2 if gated else width
    tm, sub, cw = CONV_ROWS, CONV_SUB, CONV_LANES
    te = tm + 16
    nd = len(dsrcs)
    n_parts = 2 if gated else 1

    def body(*refs):
        cur_ref, prev_ref, next_ref = refs[0:3]
        dcur, dnext = refs[3:3 + nd], refs[3 + nd:3 + 2 * nd]
        w_ref, b_ref = refs[3 + 2 * nd:5 + 2 * nd]
        out_ref, acc_ref, ext_ref, du_ref = refs[-4:]
        i = pl.program_id(0)
        ext_ref[0:8, :] = jnp.where(i > 0, prev_ref[...], 0.0)
        ext_ref[8:8 + tm, :] = cur_ref[...]
        ext_ref[8 + tm:24 + tm, :] = next_ref[...]

        for q, (src, off) in enumerate(chunk_src):
            cols = [q * cw, half + q * cw][:n_parts]

            def conv_grad(r, d):
                pre = [_conv_window(ext_ref[pl.ds(r, sub + 8), c0:c0 + cw], w_ref, b_ref, taps, c0, cw, sub) for c0 in cols]
                row = i * tm + r + _iota((sub, 1), 0)
                live = (row >= PAD) & (row < n_rows)
                if gated:
                    act, dact = _silu_grad(pre[0])
                    dus = [d * pre[1] * dact, d * act]
                else:
                    dus = [d * _dsilu(pre[0])]
                for part, du in enumerate(dus):
                    du_ref[part, pl.ds(r, sub), :] = jnp.where(live, du, 0.0)

            def tile_rows(s, carry):
                r = pl.multiple_of(s * sub, sub)
                conv_grad(r, dcur[src][pl.ds(r, sub), off:off + cw].astype(F32))
                return carry

            lax.fori_loop(0, tm // sub, tile_rows, 0, unroll=GROUP_UNROLL)
            conv_grad(tm, dnext[src][:, off:off + cw].astype(F32))

            for part, c0 in enumerate(cols):
                taps_w = [w_ref[k:k + 1, c0:c0 + cw] for k in range(taps)]

                def back(s, sums):
                    new = list(sums)
                    for u in range(2):
                        r = pl.multiple_of((2 * s + u) * sub, sub)
                        win = du_ref[part, pl.ds(r, sub + 8), :]
                        raw_rows = ext_ref[pl.ds(8 + r, sub), c0:c0 + cw]
                        draw = jnp.zeros((sub, cw), F32)
                        for k in range(taps):
                            shifted = win[taps - 1 - k:taps - 1 - k + sub]
                            draw = draw + taps_w[k] * shifted
                            new[k] = new[k] + shifted * raw_rows
                        new[taps] = new[taps] + win[0:sub]
                        out_ref[pl.ds(r, sub), c0:c0 + cw] = jnp.where(_valid_rows(i * tm + r, sub, PAD), draw, 0.0).astype(out_ref.dtype)
                    return tuple(new)

                sums = lax.fori_loop(0, tm // (2 * sub), back, tuple(jnp.zeros((sub, cw), F32) for _ in range(taps + 1)))
                for k in range(taps + 1):
                    total = jnp.sum(sums[k], axis=0, keepdims=True)
                    acc_ref[k:k + 1, c0:c0 + cw] = jnp.where(i == 0, total, acc_ref[k:k + 1, c0:c0 + cw] + total)

    in_specs = _conv_specs(tm, width, raw_blk, n_rows, True)
    in_specs += [pl.BlockSpec((tm, d.shape[1]), lambda i: (i, 0)) for d in dsrcs]
    in_specs += [pl.BlockSpec((16, d.shape[1]), lambda i: (jnp.minimum((i + 1) * (tm // 16), n_rows // 16 - 1), 0)) for d in dsrcs]
    in_specs += [pl.BlockSpec((taps, width), lambda i: (0, 0)), pl.BlockSpec((1, width), lambda i: (0, 0))]
    operands = [raw, raw, raw] + list(dsrcs) + list(dsrcs) + [conv_w, conv_b]
    aliases = {}
    if into is None:
        out0 = jax.ShapeDtypeStruct((n_rows, width), BF16)
    else:
        in_specs.append(pl.BlockSpec(memory_space=pl.ANY))
        operands.append(into)
        aliases = {len(operands) - 1: 0}
        out0 = jax.ShapeDtypeStruct(into.shape, into.dtype)
    return _call(body, name, (n_rows // tm,), in_specs,
                 [pl.BlockSpec((tm, width), lambda i: (i, into_blk)), pl.BlockSpec((8, width), lambda i: (0, 0))],
                 [out0, jax.ShapeDtypeStruct((8, width), F32)], operands,
                 scratch_shapes=[pltpu.VMEM((tm + 24, width), F32), pltpu.VMEM((n_parts, te + 8, cw), F32)],
                 aliases=aliases, bg=bg)


def _ssd_specs(n_chunks, rev, per_step=1):
    cidx = (lambda c: n_chunks - 1 - c) if rev else (lambda c: c)
    xw, nw = per_step * GROUP_W, per_step * D_STATE
    xg0, bg0, cg0 = P_XBC // xw, (P_XBC + D_INNER) // nw, (P_XBC + D_INNER + SSM_GROUPS * D_STATE) // nw

    def cur(width, blk0):
        return pl.BlockSpec((T, width), lambda g, c: (cidx(c), blk0 + g))

    def prev(width, blk0):
        return pl.BlockSpec((8, width), lambda g, c: (jnp.maximum(cidx(c) * (T // 8) - 1, 0), blk0 + g))

    specs = [cur(xw, xg0), prev(xw, xg0), cur(nw, bg0), prev(nw, bg0), cur(nw, cg0), prev(nw, cg0),
             pl.BlockSpec((T, 128), lambda g, c: (cidx(c), P_DT // 128))]
    wb, wc = D_INNER // nw, (D_INNER + SSM_GROUPS * D_STATE) // nw
    specs += [pl.BlockSpec((4, xw), lambda g, c: (0, g)),
              pl.BlockSpec((4, nw), lambda g, c: (0, wb + g)),
              pl.BlockSpec((4, nw), lambda g, c: (0, wc + g)),
              pl.BlockSpec((1, xw), lambda g, c: (0, g)),
              pl.BlockSpec((1, nw), lambda g, c: (0, wb + g)),
              pl.BlockSpec((1, nw), lambda g, c: (0, wc + g))]
    specs += [pl.BlockSpec((1, 128), lambda g, c: (0, 0))] * 3
    return specs, cidx


def _ssd_chunk_forward(refs, ext_ref, g, c):
    (xc_ref, xp_ref, bc_ref, bp_ref, cc_ref, cp_ref, dt_ref, wx_ref, wb_ref, wc_ref,
     bx_ref, bb_ref, bcb_ref, dtb_ref, alog_ref, dsk_ref) = refs

    def conv_pre(cur_ref, prev_ref, w_ref, b_ref, width):
        ext_ref[0:8, 0:width] = jnp.where(c > 0, prev_ref[...], 0.0)
        ext_ref[8:8 + T, 0:width] = cur_ref[...]
        w = w_ref[...]
        acc = b_ref[...] + w[3:4] * cur_ref[...]
        for k in range(3):
            acc = acc + w[k:k + 1] * ext_ref[pl.ds(5 + k, T), 0:width]
        return acc

    valid = _valid_rows(c * T, T, PAD)
    v = {}
    v["valid"] = valid
    v["x_pre"] = conv_pre(xc_ref, xp_ref, wx_ref, bx_ref, GROUP_W)
    v["b_pre"] = conv_pre(bc_ref, bp_ref, wb_ref, bb_ref, D_STATE)
    v["c_pre"] = conv_pre(cc_ref, cp_ref, wc_ref, bcb_ref, D_STATE)
    xs = _silu(v["x_pre"])
    bm = jnp.where(valid, _silu(v["b_pre"]), 0.0)
    cm = jnp.where(valid, _silu(v["c_pre"]), 0.0)
    dtr = dt_ref[...] + dtb_ref[...]
    dt = jnp.where(valid, _softplus(dtr), 0.0)
    a_neg = -jnp.exp(alog_ref[...])
    a = dt * a_neg
    tril = _iota((T, T), 0) >= _iota((T, T), 1)
    cs = _xdot_l(tril.astype(BF16), a)
    hh, ll = _iota((128, GROUP_W), 0), _iota((128, GROUP_W), 1)
    expand = (hh == 8 * g + jnp.right_shift(ll, 6)).astype(BF16)
    sh, sj = _iota((128, 128), 0), _iota((128, 128), 1)
    select = ((sh == 8 * g + sj) & (sj < 8)).astype(BF16)
    hh_t, ll_t = _iota((GROUP_W, 128), 1), _iota((GROUP_W, 128), 0)
    v["expand_t"] = (hh_t == 8 * g + jnp.right_shift(ll_t, 6)).astype(BF16)
    v["select_t"] = ((sj == 8 * g + sh) & (sh < 8)).astype(BF16)
    cs_e = _xdot(cs, expand)
    dt_e = _xdot(dt, expand)
    cs_loc = _xdot(cs, select)
    cs_loc_t = cs_loc.T
    cs_last_e = cs_e[T - 1:T, :]
    v.update(xs=xs, bm=bm, cm=cm, dtr=dtr, dt=dt, a_neg=a_neg, tril=tril, expand=expand, select=select,
             cs_e=cs_e, dt_e=dt_e, cs_loc=cs_loc, cs_loc_t=cs_loc_t, cs_last_e=cs_last_e)
    v["xdt"] = xs * dt_e
    v["decay_e"] = jnp.exp(cs_last_e - cs_e)
    v["ecs_e"] = jnp.exp(cs_e)
    v["elast_e"] = jnp.exp(cs_last_e)
    v["d_e"] = _xdot(dsk_ref[...], expand)
    v["gmat"] = _dot_nt(cm.astype(BF16), bm.astype(BF16))
    return v


def _ssd_decay_pair(v, jp):
    out = []
    for j in (2 * jp, 2 * jp + 1):
        diff = v["cs_loc"][:, j:j + 1] - v["cs_loc_t"][j:j + 1, :]
        out.append(jnp.where(v["tril"], jnp.exp(jnp.where(v["tril"], diff, 0.0)), 0.0))
    return out


def _block_diag_pair(xp):
    lane = _iota(xp.shape, 1)
    return jnp.concatenate([jnp.where(lane < HEAD_P, xp, 0.0), jnp.where(lane >= HEAD_P, xp, 0.0)], axis=0)


SSD_GROUPS_PER_STEP = 4


def _ssd_group_refs(refs, gg):
    x_w, n_w = pl.ds(GROUP_W * gg, GROUP_W), pl.ds(D_STATE * gg, D_STATE)
    lanes = [x_w, x_w, n_w, n_w, n_w, n_w, None, x_w, n_w, n_w, x_w, n_w, n_w, None, None, None]
    return [r if w is None else r.at[:, w] for r, w in zip(refs, lanes)]


def _ssd_fwd(p, conv_w, conv_b, dt_bias, a_log, d_skip, n_chunks, bg=None):
    n_rows = n_chunks * T
    in_specs, _ = _ssd_specs(n_chunks, rev=False, per_step=SSD_GROUPS_PER_STEP)
    per = SSD_GROUPS_PER_STEP

    def body(*refs):
        y_ref, hin_ref, st_ref, ext_ref = refs[16:]
        g2, c = pl.program_id(0), pl.program_id(1)

        @pl.when(c == 0)
        def _():
            st_ref[...] = jnp.zeros_like(st_ref)

        for gg in range(per):
            v = _ssd_chunk_forward(_ssd_group_refs(refs[:16], gg), ext_ref.at[gg], per * g2 + gg, c)
            state = st_ref[gg]
            hin_ref[gg] = state
            ys = []
            for jp in range(4):
                l0, l1 = _ssd_decay_pair(v, jp)
                lhs = jnp.concatenate([v["gmat"] * l0, v["gmat"] * l1], axis=1).astype(BF16)
                rhs = _block_diag_pair(v["xdt"][:, 128 * jp:128 * jp + 128]).astype(BF16)
                ys.append(_dot(lhs, rhs))
            y = jnp.concatenate(ys, axis=1)
            y = y + _dot(v["cm"].astype(BF16), state.astype(BF16)) * v["ecs_e"] + v["xs"] * v["d_e"]
            y_ref[:, GROUP_W * gg:GROUP_W * gg + GROUP_W] = y
            s_new = _dot_tn(v["bm"].astype(BF16), (v["xdt"] * v["decay_e"]).astype(BF16))
            st_ref[gg] = state * v["elast_e"] + s_new

    return _call(
        body, "ssd_fwd", (SSM_GROUPS // per, n_chunks), in_specs,
        [pl.BlockSpec((T, per * GROUP_W), lambda g, c: (c, g)),
         pl.BlockSpec((per, None, D_STATE, GROUP_W), lambda g, c: (g, c, 0, 0))],
        [jax.ShapeDtypeStruct((n_rows, D_INNER), F32),
         jax.ShapeDtypeStruct((SSM_GROUPS, n_chunks, D_STATE, GROUP_W), F32)],
        [p, p, p, p, p, p, p, conv_w, conv_w, conv_w, conv_b, conv_b, conv_b, dt_bias, a_log, d_skip],
        scratch_shapes=[pltpu.VMEM((per, D_STATE, GROUP_W), F32), pltpu.VMEM((per, T + 8, GROUP_W), F32)], bg=bg)


def _ssd_bwd(p, conv_w, conv_b, dt_bias, a_log, d_skip, hin, dy, n_chunks, bg=None):
    n_rows = n_chunks * T
    per = SSD_GROUPS_PER_STEP
    in_specs, cidx = _ssd_specs(n_chunks, rev=True, per_step=per)
    in_specs = in_specs + [pl.BlockSpec((per, None, D_STATE, GROUP_W), lambda g, c: (g, cidx(c), 0, 0)),
                           pl.BlockSpec((T, per * GROUP_W), lambda g, c: (cidx(c), g))]

    def body(*refs):
        hin_ref, dy_ref = refs[16:18]
        dx_ref, db_ref, dc_ref, ddt_ref, dpar_ref, dst_ref, ext_ref = refs[18:]
        for gg in range(per):
            x_w, n_w = pl.ds(GROUP_W * gg, GROUP_W), pl.ds(D_STATE * gg, D_STATE)
            group_body(_ssd_group_refs(refs[:16], gg), hin_ref.at[gg], dy_ref.at[:, x_w], dx_ref.at[:, x_w],
                       db_ref.at[:, n_w], dc_ref.at[:, n_w], ddt_ref.at[:, n_w], dpar_ref.at[gg], dst_ref.at[gg],
                       ext_ref.at[gg], per * pl.program_id(0) + gg)

    def group_body(in_refs, hin_ref, dy_ref, dx_ref, db_ref, dc_ref, ddt_ref, dpar_ref, dst_ref, ext_ref, g):
        step = pl.program_id(1)
        c = n_chunks - 1 - step

        @pl.when(step == 0)
        def _():
            dst_ref[...] = jnp.zeros_like(dst_ref)

        v = _ssd_chunk_forward(in_refs, ext_ref, g, c)
        hin_f = hin_ref[...]
        hin_b = hin_f.astype(BF16)
        dyv = dy_ref[...]
        dst = dst_ref[...]
        dst_b = dst.astype(BF16)
        xs, bm, cm, xdt = v["xs"], v["bm"], v["cm"], v["xdt"]
        bm_b, cm_b = bm.astype(BF16), cm.astype(BF16)

        dd_e = jnp.sum(dyv * xs, axis=0, keepdims=True)
        dxs = dyv * v["d_e"]
        ch = _dot(cm_b, hin_b)
        dch = (dyv * v["ecs_e"]).astype(BF16)
        dcm = _dot_nt(dch, hin_b)
        dhin = _dot_tn(cm_b, dch) + dst * v["elast_e"]
        dcs_e = dyv * ch * v["ecs_e"]
        dxd = _dot(bm_b, dst_b)
        dbm = _dot_nt((xdt * v["decay_e"]).astype(BF16), dst_b)
        dxdt_state = dxd * v["decay_e"]
        q = dxdt_state * xdt
        dcs_e = dcs_e - q
        dlast_e = jnp.sum(q, axis=0, keepdims=True) + jnp.sum(dst * hin_f, axis=0, keepdims=True) * v["elast_e"]
        dg = jnp.zeros((T, T), F32)
        rs_cols = jnp.zeros((T, 128), F32)
        cs_rows = jnp.zeros((128, T), F32)
        lane_i, sub_i = _iota((T, 128), 1), _iota((128, T), 0)
        dxdt_parts = []
        for jp in range(4):
            l0, l1 = _ssd_decay_pair(v, jp)
            m0, m1 = v["gmat"] * l0, v["gmat"] * l1
            xbd = _block_diag_pair(xdt[:, 128 * jp:128 * jp + 128]).astype(BF16)
            dyp = dyv[:, 128 * jp:128 * jp + 128]
            dm = _dot_nt(dyp.astype(BF16), xbd)
            dm0, dm1 = dm[:, 0:T], dm[:, T:2 * T]
            dg = dg + dm0 * l0 + dm1 * l1
            for j, qq in ((2 * jp, dm0 * m0), (2 * jp + 1, dm1 * m1)):
                rs_cols = jnp.where(lane_i == j, jnp.sum(qq, axis=1, keepdims=True), rs_cols)
                cs_rows = jnp.where(sub_i == j, jnp.sum(qq, axis=0, keepdims=True), cs_rows)
            mv = jnp.concatenate([m0, m1], axis=0).astype(BF16)
            dxdt_parts.append(_dot_tn(mv, _block_diag_pair(dyp).astype(BF16)))
        dxdt = jnp.concatenate(dxdt_parts, axis=1) + dxdt_state
        dg_b = dg.astype(BF16)
        dcm = dcm + _dot(dg_b, bm_b)
        dbm = dbm + _dot_tn(dg_b, cm_b)
        expand_t = v["expand_t"]
        dcs_loc = rs_cols - cs_rows.T
        last_row = _iota((T, 1), 0) == T - 1
        dcs_full_e = dcs_e + jnp.where(last_row, dlast_e, 0.0)
        dcs = _xdot(dcs_full_e, expand_t) + _xdot(dcs_loc, v["select_t"])
        triu = (_iota((T, T), 0) <= _iota((T, T), 1)).astype(BF16)
        da = _xdot_l(triu, dcs)
        ddt = da * v["a_neg"] + _xdot(dxdt * xs, expand_t)
        dxs = dxs + dxdt * v["dt_e"]
        ddtr = jnp.where(v["valid"], ddt * _sigmoid(v["dtr"]), 0.0)
        dx_ref[...] = dxs
        db_ref[...] = jnp.where(v["valid"], dbm, 0.0)
        dc_ref[...] = jnp.where(v["valid"], dcm, 0.0)
        ddt_ref[...] = ddtr
        dpar = jnp.concatenate([
            jnp.sum(ddtr, axis=0, keepdims=True),
            jnp.sum(da * v["dt"], axis=0, keepdims=True) * v["a_neg"],
            _xdot(dd_e, expand_t),
            jnp.zeros((5, 128), F32)], axis=0)

        @pl.when(step == 0)
        def _():
            dpar_ref[...] = dpar

        @pl.when(step > 0)
        def _():
            dpar_ref[...] += dpar

        dst_ref[...] = dhin

    return _call(
        body, "ssd_bwd", (SSM_GROUPS // per, n_chunks), in_specs,
        [pl.BlockSpec((T, per * GROUP_W), lambda g, c: (cidx(c), g)),
         pl.BlockSpec((T, per * D_STATE), lambda g, c: (cidx(c), g)),
         pl.BlockSpec((T, per * D_STATE), lambda g, c: (cidx(c), g)),
         pl.BlockSpec((T, per * 128), lambda g, c: (cidx(c), g)),
         pl.BlockSpec((per, 8, 128), lambda g, c: (g, 0, 0))],
        [jax.ShapeDtypeStruct((n_rows, D_INNER), F32),
         jax.ShapeDtypeStruct((n_rows, SSM_GROUPS * D_STATE), F32),
         jax.ShapeDtypeStruct((n_rows, SSM_GROUPS * D_STATE), F32),
         jax.ShapeDtypeStruct((n_rows, SSM_GROUPS * 128), F32),
         jax.ShapeDtypeStruct((SSM_GROUPS, 8, 128), F32)],
        [p, p, p, p, p, p, p, conv_w, conv_w, conv_w, conv_b, conv_b, conv_b, dt_bias, a_log, d_skip, hin, dy],
        scratch_shapes=[pltpu.VMEM((per, D_STATE, GROUP_W), F32), pltpu.VMEM((per, T + 8, GROUP_W), F32)], bg=bg)


def _alibi_slope(h):
    return 2.0 ** (-8.0 * (h + 1) / ATTN_HEADS)


def _dup_half(x256, kvh):
    xb = x256[:, 128 * (kvh // 2):128 * (kvh // 2) + 128]
    rolled = pltpu.roll(xb, 64, 1)
    lane = _iota(xb.shape, 1)
    if kvh % 2 == 0:
        return jnp.where(lane < 64, xb, rolled)
    return jnp.where(lane < 64, rolled, xb)


def _attn_masks(c):
    qi, j = _iota((T, T), 0), _iota((T, T), 1)
    tri = j <= qi
    meta_ok = (j >= PAD) & (j - PAD <= c * T + qi - PAD)
    band_ok = c >= jnp.where(tri, 1, 2)
    dist = jnp.bitwise_and(qi - j, T - 1).astype(F32)
    return tri, meta_ok, band_ok, dist


def _fold(x3, tri):
    return jnp.concatenate([x3[:, 0:T], jnp.where(tri, x3[:, 2 * T:3 * T], x3[:, T:2 * T])], axis=1)


def _unfold(x2, tri):
    band = x2[:, T:2 * T]
    return jnp.concatenate([x2[:, 0:T], jnp.where(tri, 0.0, band), jnp.where(tri, band, 0.0)], axis=1)


def _attn_scores(qp, k3, masks, h0):
    tri, meta_ok, band_ok, dist = masks
    lane = _iota(qp.shape, 1)
    s = []
    for half, h in ((0, h0), (1, h0 + 1)):
        qh = jnp.where((lane < 64) if half == 0 else (lane >= 64), qp, 0.0).astype(BF16)
        raw = _dot_nt(qh, k3)
        band = jnp.where(tri, raw[:, 2 * T:3 * T], raw[:, T:2 * T]) - _alibi_slope(h) * dist
        s.append((qh, jnp.concatenate([jnp.where(meta_ok, raw[:, 0:T], NEG), jnp.where(band_ok, band, NEG)], axis=1)))
    return s


def _attn_fwd(p, sinks, n_chunks, bg=None):
    n_rows = n_chunks * T
    kb, vb = P_K // KV_W, P_V // KV_W

    def body(q_ref, kc_ref, kp_ref, km_ref, vc_ref, vp_ref, vm_ref, sink_ref, o_ref, lse_ref):
        c = pl.program_id(0)
        sinks_v = sink_ref[...]
        masks = _attn_masks(c)
        tri, meta_ok, band_ok, dist = masks
        lane = _iota((T, 128), 1)
        for kvh in range(KV_HEADS):
            k3 = jnp.concatenate([_dup_half(r[...], kvh) for r in (km_ref, kp_ref, kc_ref)], axis=0).astype(BF16)
            v3 = jnp.concatenate([_dup_half(r[...], kvh) for r in (vm_ref, vp_ref, vc_ref)], axis=0)
            v3bd = _block_diag_rows(v3).astype(BF16)
            q2 = q_ref[:, 256 * kvh:256 * kvh + 256] * SCALE
            q4 = jnp.concatenate([jnp.where((lane < 64) if half == 0 else (lane >= 64), q2[:, 128 * pr:128 * pr + 128], 0.0)
                                  for pr in range(2) for half in range(2)], axis=0).astype(BF16)
            raw4 = _dot_nt(q4, k3)
            probs = []
            for hh in range(4):
                h = 4 * kvh + hh
                raw = raw4[T * hh:T * hh + T]
                band = jnp.where(tri, raw[:, 2 * T:3 * T], raw[:, T:2 * T]) - _alibi_slope(h) * dist
                sc = jnp.concatenate([jnp.where(meta_ok, raw[:, 0:T], NEG), jnp.where(band_ok, band, NEG)], axis=1)
                sink = sinks_v[:, h:h + 1]
                m = jnp.maximum(jnp.max(sc, axis=1, keepdims=True), sink)
                e = jnp.exp(sc - m)
                den = jnp.sum(e, axis=1, keepdims=True) + jnp.exp(sink - m)
                probs.append(_unfold(e * (1.0 / den), tri))
                lse_ref[:, h:h + 1] = m + jnp.log(den)
            p4 = jnp.concatenate([jnp.concatenate(probs[0:2], axis=1), jnp.concatenate(probs[2:4], axis=1)], axis=0)
            out = _dot(p4.astype(BF16), v3bd)
            o_ref[:, 256 * kvh:256 * kvh + 256] = jnp.concatenate([out[0:T], out[T:2 * T]], axis=1).astype(o_ref.dtype)

    blk = lambda width, col: pl.BlockSpec((T, width), lambda c: (c, col))
    prev = lambda width, col: pl.BlockSpec((T, width), lambda c: (jnp.maximum(c - 1, 0), col))
    first = lambda width, col: pl.BlockSpec((T, width), lambda c: (0, col))
    return _call(
        body, "attn_fwd", (n_chunks,),
        [blk(ATTN_W, P_Q // ATTN_W), blk(KV_W, kb), prev(KV_W, kb), first(KV_W, kb),
         blk(KV_W, vb), prev(KV_W, vb), first(KV_W, vb), pl.BlockSpec((1, 128), lambda c: (0, 0))],
        [pl.BlockSpec((T, ATTN_W), lambda c: (c, 0)), pl.BlockSpec((T, 128), lambda c: (c, 0))],
        [jax.ShapeDtypeStruct((n_rows, ATTN_W), BF16), jax.ShapeDtypeStruct((n_rows, 128), F32)],
        [p, p, p, p, p, p, p, sinks], bg=bg)


def _block_diag_rows(x3):
    lane = _iota(x3.shape, 1)
    return jnp.concatenate([jnp.where(lane < 64, x3, 0.0), jnp.where(lane >= 64, x3, 0.0)], axis=0)


def _fold_halves(x):
    return x + pltpu.roll(x, 64, 1)


def _attn_bwd(p, sinks, ao, lse, dao, dp, n_chunks, bg=None):
    kb, vb = P_K // KV_W, P_V // KV_W
    rc = lambda s: n_chunks - 1 - s

    def body(q_ref, kc_ref, kp_ref, km_ref, vc_ref, vp_ref, vm_ref, sink_ref, o_ref, lse_ref, do_ref, dp_in_ref,
             dqkv_ref, dsink_ref, kcar_ref, vcar_ref, kmeta_ref, vmeta_ref):
        step = pl.program_id(0)
        c = n_chunks - 1 - step

        @pl.when(step == 0)
        def _():
            for r in (kcar_ref, vcar_ref, kmeta_ref, vmeta_ref):
                r[...] = jnp.zeros_like(r)

        masks = _attn_masks(c)
        tri = masks[0]
        q = q_ref[...] * SCALE
        sinks_v = sink_ref[...]
        lse_v = lse_ref[...]
        ov = o_ref[...].astype(F32)
        dov = do_ref[...].astype(F32)
        lane = _iota((T, 128), 1)
        lane256 = _iota((3 * T, KV_W), 1)
        dsink = jnp.zeros((1, 128), F32)
        dk3_all = jnp.zeros((3 * T, KV_W), F32)
        dv3_all = jnp.zeros((3 * T, KV_W), F32)
        dqs = []
        for kvh in range(KV_HEADS):
            k3 = jnp.concatenate([_dup_half(r[...], kvh) for r in (km_ref, kp_ref, kc_ref)], axis=0).astype(BF16)
            v3 = jnp.concatenate([_dup_half(r[...], kvh) for r in (vm_ref, vp_ref, vc_ref)], axis=0).astype(BF16)
            dk3 = jnp.zeros((3 * T, 128), F32)
            dv3 = jnp.zeros((3 * T, 128), F32)
            for pr in range(2):
                h0 = 4 * kvh + 2 * pr
                blk = 2 * kvh + pr
                qp = q[:, 128 * blk:128 * blk + 128]
                dop = dov[:, 128 * blk:128 * blk + 128]
                prod = dop * ov[:, 128 * blk:128 * blk + 128]
                dq_pair = jnp.zeros((T, 128), F32)
                for half, ((qh, sc), h) in enumerate(zip(_attn_scores(qp, k3, masks, h0), (h0, h0 + 1))):
                    mine = (lane < 64) if half == 0 else (lane >= 64)
                    lse_h = lse_v[:, h:h + 1]
                    pm = jnp.exp(sc - lse_h)
                    doh = jnp.where(mine, dop, 0.0).astype(BF16)
                    delta = jnp.sum(jnp.where(mine, prod, 0.0), axis=1, keepdims=True)
                    dp = _fold(_dot_nt(doh, v3), tri)
                    ds = _unfold(pm * (dp - delta), tri).astype(BF16)
                    p_sink = jnp.exp(sinks_v[:, h:h + 1] - lse_h)
                    dsink = jnp.where(_iota((1, 128), 1) == h, jnp.sum(-p_sink * delta, axis=0, keepdims=True), dsink)
                    dq_pair = jnp.where(mine, _dot(ds, k3), dq_pair)
                    dk3 = dk3 + _dot_tn(ds, qh)
                    dv3 = dv3 + _dot_tn(_unfold(pm, tri).astype(BF16), doh)
                dqs.append(dq_pair * SCALE)
            in_place = (lane256 >= 64 * kvh) & (lane256 < 64 * kvh + 64)
            wide = lambda x: jnp.concatenate([x, x], axis=1)
            dk3_all = jnp.where(in_place, wide(_fold_halves(dk3)), dk3_all)
            dv3_all = jnp.where(in_place, wide(_fold_halves(dv3)), dv3_all)
        dsink_all = dsink

        @pl.when(step == 0)
        def _():
            dsink_ref[...] = dsink_all

        @pl.when(step > 0)
        def _():
            dsink_ref[...] += dsink_all

        kmeta = kmeta_ref[...] + dk3_all[0:T]
        vmeta = vmeta_ref[...] + dv3_all[0:T]
        kmeta_ref[...] = kmeta
        vmeta_ref[...] = vmeta
        is_first = c == 0
        dk = jnp.where(is_first, kmeta, dk3_all[2 * T:3 * T] + kcar_ref[...])
        dv = jnp.where(is_first, vmeta, dv3_all[2 * T:3 * T] + vcar_ref[...])
        dqkv_ref[...] = jnp.concatenate(dqs + [dk, dv], axis=1).astype(dqkv_ref.dtype)
        kcar_ref[...] = dk3_all[T:2 * T]
        vcar_ref[...] = dv3_all[T:2 * T]

    blk = lambda width, col: pl.BlockSpec((T, width), lambda s: (rc(s), col))
    prev = lambda width, col: pl.BlockSpec((T, width), lambda s: (jnp.maximum(rc(s) - 1, 0), col))
    first = lambda width, col: pl.BlockSpec((T, width), lambda s: (0, col))
    return _call(
        body, "attn_bwd", (n_chunks,),
        [blk(ATTN_W, P_Q // ATTN_W), blk(KV_W, kb), prev(KV_W, kb), first(KV_W, kb),
         blk(KV_W, vb), prev(KV_W, vb), first(KV_W, vb), pl.BlockSpec((1, 128), lambda s: (0, 0)),
         blk(ATTN_W, 0), blk(128, 0), blk(ATTN_W, 0), ANY],
        [blk(QKV_W, P_Q // QKV_W), pl.BlockSpec((1, 128), lambda s: (0, 0))],
        [jax.ShapeDtypeStruct(dp.shape, dp.dtype), jax.ShapeDtypeStruct((1, 128), F32)],
        [p, p, p, p, p, p, p, sinks, ao, lse, dao, dp],
        scratch_shapes=[pltpu.VMEM((T, KV_W), F32)] * 4, aliases={11: 0}, bg=bg)


def _pad_lanes(v, width=128):
    return jnp.pad(v, ((0, 0), (0, width - v.shape[1])))


def _local_step(x, head, tgt, plan):
    w, g, run = plan.w, plan.g, plan.run
    n_tok = x.shape[0]
    n_rows = n_tok + T
    n_chunks = n_rows // T
    tm = _row_tile(n_rows, 384)
    dt_bias, a_log, d_skip = (_pad_lanes(w[k]) for k in ("ssm_dt_bias", "ssm_a_log", "ssm_d_skip"))
    sinks = _pad_lanes(w["attn_sinks"])
    x_in = [(x, D_MODEL, 0, "prev"), (head, D_MODEL, 0, "first")]

    def h0_tile(r0, xt, hd):
        return jnp.where(r0 < T, hd, xt)

    n1, = _rowwise("norm_pre_mix", lambda r0, xt, hd, wn: [_rms(h0_tile(r0, xt, hd), wn)], n_rows, T,
                   x_in, [w["norm_pre_mix"]], [(D_MODEL, BF16)], [])
    p = _matmul("in_proj", n1, w["w_cat"], "nn", F32)
    y_ssd, hin = run("ssd_fwd", _ssd_fwd, p, w["ssm_conv_w"], w["ssm_conv_b"], dt_bias, a_log, d_skip, n_chunks)
    ao, lse = run("attn_fwd", _attn_fwd, p, sinks, n_chunks)

    def gate_norm(r0, y, z, wn):
        return [_rms(y * _silu(z), wn)]

    yn, = run("ssm_gate_norm", _rowwise, "ssm_gate_norm", gate_norm, n_rows, tm,
              [(y_ssd, D_INNER, 0), (p, D_INNER, P_Z // D_INNER)], [w["ssm_norm"]], [(D_INNER, BF16)], [])
    y_ssm = _matmul("ssm_out", yn, w["w_ssm_out"], "nn", F32)
    y_attn = _matmul("attn_out", ao, w["w_attn_out"], "nn", F32)

    def mix_gate(r0, ys, ya, gs, ga):
        return [_sigmoid(gs) * ys + _sigmoid(ga) * ya]

    gate_ins = [(p, D_MODEL, P_GATE // D_MODEL), (p, D_MODEL, P_GATE // D_MODEL + 1)]
    mixed, = _rowwise("mix_gate", mix_gate, n_rows, tm, [(y_ssm, D_MODEL, 0), (y_attn, D_MODEL, 0)] + gate_ins,
                      [], [(D_MODEL, BF16)], [])
    mix = _matmul("mix_out", mixed, w["w_mix_out"], "nn", F32)

    def post_mix(r0, mx, xt, hd, w_post, w_pre):
        h1 = jnp.where(_valid_rows(r0, mx.shape[0], PAD), h0_tile(r0, xt, hd) + _rms(mx, w_post), 0.0)
        return [h1, _rms(h1, w_pre)]

    h1, n2 = _rowwise("post_mix", post_mix, n_rows, T, [(mix, D_MODEL, 0)] + x_in,
                      [w["norm_post_mix"], w["norm_pre_ffn"]], [(D_MODEL, F32), (D_MODEL, BF16)], [])
    u_raw = _matmul("ffn_up", n2, w["w_ffn_up"], "nn", F32)
    f = _ffn_act("ffn_act", u_raw, w["ffn_conv_w"], w["ffn_conv_b"], n_rows)
    ffn = _matmul("ffn_down", f, w["w_ffn_down"], "nn", F32)

    def final(r0, fo, h, t, w_post):
        real = r0 >= T
        err = jnp.where(real, h + _rms(fo, w_post) - t, 0.0)
        dy = err * (1.0 / D_MODEL)
        dffn, dw = _rms_bwd(dy, fo, w_post)
        return [dffn, dy, jnp.sum(err * err, axis=0, keepdims=True), dw]

    dffn, dh2, loss_cols, g_norm_post_ffn = _rowwise(
        "loss_head", final, n_rows, T, [(ffn, D_MODEL, 0), (h1, D_MODEL, 0), (tgt, D_MODEL, 0, "prev")],
        [w["norm_post_ffn"]], [(D_MODEL, BF16), (D_MODEL, F32)], [D_MODEL, D_MODEL])

    g["norm_post_ffn"] = g_norm_post_ffn
    g["w_ffn_down"] = _matmul("ffn_down_dw", f, dffn, "tn", F32)
    df = _matmul("ffn_down_dx", dffn, w["w_ffn_down"], "nt", F32)
    du_raw, dconv = _conv_bwd("ffn_act_bwd", u_raw, 0, [df], [(0, c0) for c0 in range(0, FFN_DIM, CONV_LANES)],
                              w["ffn_conv_w"], w["ffn_conv_b"], n_rows, True)
    g["ffn_conv_w"], g["ffn_conv_b"] = dconv[0:3], dconv[3:4]
    g["w_ffn_up"] = _matmul("ffn_up_dw", n2, du_raw, "tn", F32)
    dn2 = run("ffn_up_dx", _matmul, "ffn_up_dx", du_raw, w["w_ffn_up"], "nt", F32)

    def post_mix_bwd(r0, dn, d2, h, mx, w_pre, w_post):
        dx, dw_pre = _rms_bwd(dn, h, w_pre)
        dh1 = jnp.where(_valid_rows(r0, dn.shape[0], PAD), dx + d2, 0.0)
        dmix, dw_post = _rms_bwd(dh1, mx, w_post)
        return [dh1, dmix, dw_pre, dw_post]

    dh1, dmix, g["norm_pre_ffn"], g["norm_post_mix"] = _rowwise(
        "post_mix_bwd", post_mix_bwd, n_rows, tm,
        [(dn2, D_MODEL, 0), (dh2, D_MODEL, 0), (h1, D_MODEL, 0), (mix, D_MODEL, 0)],
        [w["norm_pre_ffn"], w["norm_post_mix"]], [(D_MODEL, F32), (D_MODEL, BF16)], [D_MODEL, D_MODEL])
    g["w_mix_out"] = _matmul("mix_out_dw", mixed, dmix, "tn", F32)
    dmixed = _matmul("mix_out_dx", dmix, w["w_mix_out"], "nt", F32)

    def mix_gate_bwd(r0, dm, ys, ya, gs, ga):
        ss, sa = _sigmoid(gs), _sigmoid(ga)
        dgate = jnp.concatenate([dm * ys * ss * (1.0 - ss), dm * ya * sa * (1.0 - sa)], axis=1)
        return [dm * ss, dm * sa, dgate]

    dys, dya, dp = _rowwise(
        "mix_gate_bwd", mix_gate_bwd, n_rows, tm,
        [(dmixed, D_MODEL, 0), (y_ssm, D_MODEL, 0), (y_attn, D_MODEL, 0)] + gate_ins,
        [], [(D_MODEL, BF16), (D_MODEL, BF16), (2 * D_MODEL, BF16, "new", P_W, P_GATE // (2 * D_MODEL))], [])
    g["w_ssm_out"] = _matmul("ssm_out_dw", yn, dys, "tn", F32)
    dyn = _matmul("ssm_out_dx", dys, w["w_ssm_out"], "nt", F32)
    g["w_attn_out"] = _matmul("attn_out_dw", ao, dya, "tn", F32)
    dao = _matmul("attn_out_dx", dya, w["w_attn_out"], "nt", BF16)

    def gate_norm_bwd(r0, dn, y, z, wn):
        sz, dsz = _silu_grad(z)
        dyz, dw = _rms_bwd(dn, y * sz, wn)
        live = _valid_rows(r0, dn.shape[0], PAD)
        return [jnp.where(live, dyz * sz, 0.0), jnp.where(live, dyz * y * dsz, 0.0), dw]

    dy_ssd, dp, g["ssm_norm"] = run(
        "ssm_gate_norm_bwd", _rowwise, "ssm_gate_norm_bwd", gate_norm_bwd, n_rows, tm,
        [(dyn, D_INNER, 0), (y_ssd, D_INNER, 0), (p, D_INNER, P_Z // D_INNER)],
        [w["ssm_norm"]], [(D_INNER, F32), (D_INNER, BF16, "into", dp, P_Z // D_INNER)], [D_INNER])
    dp, dsink = run("attn_bwd", _attn_bwd, p, sinks, ao, lse, dao, dp, n_chunks)
    g["attn_sinks"] = dsink[:, 0:ATTN_HEADS]
    dxs, dbm, dcm, ddt_parts, dpar = run("ssd_bwd", _ssd_bwd, p, w["ssm_conv_w"], w["ssm_conv_b"], dt_bias, a_log,
                                         d_skip, hin, dy_ssd, n_chunks)
    dpar = jnp.sum(dpar, axis=0)
    g["ssm_dt_bias"], g["ssm_a_log"], g["ssm_d_skip"] = (dpar[i:i + 1, 0:SSM_HEADS] for i in range(3))

    def dt_grad(r0, parts):
        tot = parts[:, 0:128] + parts[:, 128:256] + parts[:, 256:384] + parts[:, 384:512]
        return [jnp.concatenate([tot, jnp.zeros((parts.shape[0], P_Z - P_DT - 128), F32)], axis=1)]

    dt_w = P_Z - P_DT
    dp, = _rowwise("dt_grad", dt_grad, n_rows, tm, [(ddt_parts, SSM_GROUPS * 128, 0)], [],
                   [(dt_w, BF16, "into", dp, P_DT // dt_w)], [])
    x_chunks = [(src, c0) for src, arr in enumerate((dxs, dbm, dcm)) for c0 in range(0, arr.shape[1], CONV_LANES)]
    dp, dconv = run("ssm_conv_bwd", _conv_bwd, "ssm_conv_bwd", p, P_XBC // CONV_DIM, [dxs, dbm, dcm], x_chunks,
                    w["ssm_conv_w"], w["ssm_conv_b"], n_rows, False, into=dp, into_blk=P_XBC // CONV_DIM)
    g["ssm_conv_w"], g["ssm_conv_b"] = dconv[0:4], dconv[4:5]
    g["w_cat"] = _matmul("in_proj_dw", n1, dp, "tn", F32)
    dn1 = run("in_proj_dx", _matmul, "in_proj_dx", dp, w["w_cat"], "nt", F32)

    def pre_mix_bwd(r0, dn, d1, xt, hd, wn):
        dx, dw = _rms_bwd(dn, h0_tile(r0, xt, hd), wn)
        dh0 = jnp.where(_valid_rows(r0, dn.shape[0], PAD), dx + d1, 0.0)
        return [dh0, dh0, dw]

    dx_out, dhead, g["norm_pre_mix"] = _rowwise(
        "pre_mix_bwd", pre_mix_bwd, n_rows, T, [(dn1, D_MODEL, 0), (dh1, D_MODEL, 0)] + x_in,
        [w["norm_pre_mix"]], [(D_MODEL, F32, "prev", n_tok), (D_MODEL, F32, "first")], [D_MODEL])
    return jnp.sum(loss_cols), dx_out, dhead


_IN_SECTIONS = [((5152, 6176), P_Q), ((6176, 6432), P_K), ((6432, 6688), P_V), ((5120, 5152), P_DT),
                ((0, 2048), P_Z), ((6688, 8736), P_GATE), ((2048, 5120), P_XBC)]


IN_SHARD = N_IN // 4


def _shard_pieces(a, b):
    return [(j, max(a, j * IN_SHARD) - j * IN_SHARD, min(b, (j + 1) * IN_SHARD) - j * IN_SHARD)
            for j in range(4) if max(a, j * IN_SHARD) < min(b, (j + 1) * IN_SHARD)]


def _to_cat(w4):
    parts, at = [], 0
    for (a, b), off in _IN_SECTIONS:
        if off > at:
            parts.append(jnp.zeros((w4.shape[1], off - at), w4.dtype))
        parts += [w4[j, :, lo:hi] for j, lo, hi in _shard_pieces(a, b)]
        at = off + (b - a)
    return jnp.concatenate(parts, axis=1)


def _from_cat(g_cat):
    shards = [[] for _ in range(4)]
    for (a, b), off in sorted(_IN_SECTIONS):
        for j, lo, hi in _shard_pieces(a, b):
            start = off + j * IN_SHARD + lo - a
            shards[j].append(g_cat[:, start:start + hi - lo])
    return jnp.stack([jnp.concatenate(s, axis=1) for s in shards])


LANES = 1024
_BIG = [("w_in", 1024, 2184, "chip"), ("w_ssm_out", 512, 1024, "row"), ("w_attn_out", 256, 1024, "row"),
        ("w_mix_out", 256, 1024, "row"), ("w_ffn_up", 1024, 1408, "col"), ("w_ffn_down", 704, 1024, "row"),
        ("small", 32, LANES, "chip")]
_SMALL_SHARDED = [("ssm_conv_w", (4, 768), 1), ("ffn_conv_w", (3, 1408), 1), ("meta_tokens", (16, 256), 1)]
_REPLICATED = [("norm_pre_mix", 1024), ("ssm_conv_b", 3072), ("ssm_dt_bias", 32), ("ssm_a_log", 32),
               ("ssm_d_skip", 32), ("ssm_norm", 2048), ("attn_sinks", 16), ("norm_post_mix", 1024),
               ("norm_pre_ffn", 1024), ("ffn_conv_b", 5632), ("norm_post_ffn", 1024)]
SMALL_ROWS = 24


def _rep_rows():
    out, at = [], 0
    for _, width in _REPLICATED:
        out.append((at, -(-width // LANES)))
        at += out[-1][1]
    return out, at


def _in_rows(parts):
    rows = [jnp.pad(a, ((0, 0), (0, -a.shape[1] % LANES))).reshape(-1, LANES) for a in parts]
    flat = jnp.concatenate(rows, axis=0)
    return jnp.pad(flat, ((0, SMALL_ROWS - flat.shape[0]), (0, 0)))
WEIGHT_ORDER = ["meta_tokens", "norm_pre_mix", "w_in", "ssm_conv_w", "ssm_conv_b", "ssm_dt_bias", "ssm_a_log",
                "ssm_d_skip", "ssm_norm", "w_ssm_out", "attn_sinks", "w_attn_out", "w_mix_out", "norm_post_mix",
                "norm_pre_ffn", "w_ffn_up", "ffn_conv_w", "ffn_conv_b", "w_ffn_down", "norm_post_ffn"]


def _flatten(parts, rows):
    flat = jnp.concatenate([a.reshape(-1) for a in parts])
    return jnp.pad(flat, (0, rows * LANES - flat.shape[0])).reshape(rows, LANES)


def _unflatten(flat, shapes):
    flat = flat.reshape(-1)
    out, off = [], 0
    for shp in shapes:
        n = math.prod(shp)
        out.append(flat[off:off + n].reshape(shp))
        off += n
    return out


def _shard_of(full, chip, shape, axis):
    return lax.slice_in_dim(full, chip * shape[axis], (chip + 1) * shape[axis], axis=axis)


def _full_shape(r, c, layout):
    return {"row": (4 * r, c), "col": (r, 4 * c), "chip": (4, r, c)}[layout]


def _shard_view(ref, r, c, layout, chip):
    if layout == "row":
        return ref.at[pl.ds(pl.multiple_of(chip * r, 16), r), :]
    if layout == "col":
        return ref.at[:, pl.ds(pl.multiple_of(chip * c, 128), c)]
    return ref.at[chip]


def _half_view(ref, r, c, layout, chip, half):
    hr = r // 2
    if layout == "row":
        return ref.at[pl.ds(pl.multiple_of(chip * r + half * hr, 16), hr), :]
    r0 = pl.multiple_of(half * hr, 16)
    if layout == "col":
        return ref.at[pl.ds(r0, hr), pl.ds(pl.multiple_of(chip * c, 128), c)]
    return ref.at[chip, pl.ds(r0, hr), :]


def _mesh_pos():
    return lax.axis_index("x"), lax.axis_index("y"), lax.axis_index("c")


def _other_chips(x, y):
    return [(1 - x, y), (x, 1 - y), (1 - x, 1 - y)]


def _chip_index(x, y):
    return 2 * x + y


def _run_exchange(name, ex):
    n_in, n_out = len(ex.ins), len(ex.out_shapes)

    def body(*refs):
        in_refs, out_refs = refs[:n_in], refs[n_in:n_in + n_out]
        send_sems, recv_sems = refs[n_in + n_out:]
        copies = [pltpu.make_async_remote_copy(src_ref=s, dst_ref=d, send_sem=send_sems.at[i], recv_sem=recv_sems.at[i],
                                               device_id=dev, device_id_type=MESH)
                  for i, (s, d, dev) in enumerate(ex.make_copies(in_refs, out_refs))]
        assert len(copies) == ex.n_copies
        for cp in copies:
            cp.start()
        for cp in copies:
            cp.wait()

    return pl.pallas_call(
        body, name=name, in_specs=[ANY] * n_in, out_specs=[ANY] * n_out, out_shape=list(ex.out_shapes),
        scratch_shapes=[pltpu.SemaphoreType.DMA((ex.n_copies,)), pltpu.SemaphoreType.DMA((ex.n_copies,))],
        compiler_params=pltpu.CompilerParams(has_side_effects=True),
    )(*ex.ins)


def _join(*exs):
    def make(in_refs, out_refs):
        copies, i0, o0 = [], 0, 0
        for ex in exs:
            copies += ex.make_copies(in_refs[i0:i0 + len(ex.ins)], out_refs[o0:o0 + len(ex.out_shapes)])
            i0, o0 = i0 + len(ex.ins), o0 + len(ex.out_shapes)
        return copies

    aliases, i0, o0 = {}, 0, 0
    for ex in exs:
        aliases.update({i0 + k: o0 + v for k, v in ex.aliases.items()})
        i0, o0 = i0 + len(ex.ins), o0 + len(ex.out_shapes)
    return _Exchange([a for ex in exs for a in ex.ins], [s for ex in exs for s in ex.out_shapes], make,
                     sum(ex.n_copies for ex in exs), aliases)


def _split(exs, results):
    out, o0 = [], 0
    for ex in exs:
        out.append(list(results[o0:o0 + len(ex.out_shapes)]))
        o0 += len(ex.out_shapes)
    return out


def _gather_ici(entries, shards):
    def make(in_refs, out_refs):
        x, y, c = _mesh_pos()
        j = _chip_index(x, y)
        copies = []
        for ref_in, ref_out, (_, r, cc, lay) in zip(in_refs, out_refs, entries):
            copies.append((ref_in, _shard_view(ref_out, r, cc, lay, j), None))
            mine = ref_in.at[pl.ds(pl.multiple_of(c * (r // 2), 16), r // 2), :]
            copies += [(mine, _half_view(ref_out, r, cc, lay, j, c), (*ch, c)) for ch in _other_chips(x, y)]
        return copies

    shapes = [jax.ShapeDtypeStruct(_full_shape(r, cc, lay), s.dtype) for s, (_, r, cc, lay) in zip(shards, entries)]
    return _Exchange(list(shards), shapes, make, 4 * len(entries))


def _gather_pass_on(entries, fulls):
    def make(in_refs, out_refs):
        x, y, c = _mesh_pos()
        copies = []
        for ref, (_, r, cc, lay) in zip(out_refs, entries):
            for ch in _other_chips(x, y):
                landed = _half_view(ref, r, cc, lay, _chip_index(*ch), c)
                copies.append((landed, landed, (x, y, 1 - c)))
        return copies

    return _Exchange(list(fulls), [jax.ShapeDtypeStruct(f.shape, f.dtype) for f in fulls], make, 3 * len(entries),
                     {a: a for a in range(len(entries))})


def _gather_weights(entries, shards):
    n = len(entries)

    def body(*refs):
        ins, outs = refs[:n], refs[n:2 * n]
        send_sems, recv_sems, local_sems = refs[2 * n:]
        x, y, c = _mesh_pos()
        j = _chip_index(x, y)
        sibling = (x, y, 1 - c)
        chips = _other_chips(x, y)
        idx = [_chip_index(*ch) for ch in chips]

        def remote(k, src, dst, dev):
            return pltpu.make_async_remote_copy(src_ref=src, dst_ref=dst, send_sem=send_sems.at[k],
                                                recv_sem=recv_sems.at[k], device_id=dev, device_id_type=MESH)

        own = [pltpu.make_async_copy(ins[a], _shard_view(outs[a], r, cc, lay, j), local_sems.at[a])
               for a, (_, r, cc, lay) in enumerate(entries)]
        for cp in own:
            cp.start()
        first, passed = [], []
        for a, (_, r, cc, lay) in enumerate(entries):
            mine = ins[a].at[pl.ds(pl.multiple_of(c * (r // 2), 16), r // 2), :]
            for k, ch in enumerate(chips):
                first.append(remote(6 * a + k, mine, _half_view(outs[a], r, cc, lay, j, c), (*ch, c)))
                landed = _half_view(outs[a], r, cc, lay, idx[k], c)
                passed.append(remote(6 * a + 3 + k, landed, landed, sibling))
        for cp in first:
            cp.start()
        for a, (_, r, cc, lay) in enumerate(entries):
            for k in range(3):
                landed = _half_view(outs[a], r, cc, lay, idx[k], c)
                remote(6 * a + k, landed, landed, sibling).wait_recv()
                passed[3 * a + k].start()
        for a, (_, r, cc, lay) in enumerate(entries):
            for k in range(3):
                theirs = _half_view(outs[a], r, cc, lay, idx[k], 1 - c)
                remote(6 * a + 3 + k, theirs, theirs, sibling).wait_recv()
        for cp in first + passed:
            cp.wait_send()
        for cp in own:
            cp.wait()

    return pl.pallas_call(
        body, name="gather_weights", in_specs=[ANY] * n, out_specs=[ANY] * n,
        out_shape=[jax.ShapeDtypeStruct(_full_shape(r, cc, lay), s.dtype) for s, (_, r, cc, lay) in zip(shards, entries)],
        scratch_shapes=[pltpu.SemaphoreType.DMA((6 * n,)), pltpu.SemaphoreType.DMA((6 * n,)), pltpu.SemaphoreType.DMA((n,))],
        compiler_params=pltpu.CompilerParams(has_side_effects=True),
    )(*shards)


def _pair_exchange(entries, grads):
    def make(in_refs, out_refs):
        x, y, c = _mesh_pos()
        return [(_half_view(ref_in, r, cc, lay, i, 1 - c), ref_out.at[i], (x, y, 1 - c))
                for ref_in, ref_out, (_, r, cc, lay) in zip(in_refs, out_refs, entries) for i in range(4)]

    return _Exchange(list(grads), [jax.ShapeDtypeStruct((4, r // 2, cc), F32) for _, r, cc, _ in entries], make,
                     4 * len(entries))


def _whole_to_sibling(arrays):
    def make(in_refs, out_refs):
        x, y, c = _mesh_pos()
        return [(r, o, (x, y, 1 - c)) for r, o in zip(in_refs, out_refs)]

    return _Exchange(list(arrays), [jax.ShapeDtypeStruct(a.shape, a.dtype) for a in arrays], make, len(arrays))


def _chip_exchange(psends):
    def make(in_refs, out_refs):
        x, y, c = _mesh_pos()
        return [(ref_in.at[_chip_index(*ch)], ref_out.at[k], (*ch, c))
                for ref_in, ref_out in zip(in_refs, out_refs) for k, ch in enumerate(_other_chips(x, y))]

    return _Exchange(list(psends), [jax.ShapeDtypeStruct((3,) + p.shape[1:], p.dtype) for p in psends], make,
                     3 * len(psends))


def _to_all_chips(array):
    def make(in_refs, out_refs):
        x, y, c = _mesh_pos()
        return [(in_refs[0], out_refs[0].at[k], (*ch, c)) for k, ch in enumerate(_other_chips(x, y))]

    return _Exchange([array], [jax.ShapeDtypeStruct((3,) + array.shape, array.dtype)], make, 3)


SUM_ROWS = 256
ADAM_ROWS = 128


def _pair_sum(name, grad, recv, ids, r, c, layout):
    hr = r // 2
    tr = _row_tile(hr, SUM_ROWS)
    nb = hr // tr

    def body(ids_ref, g_ref, r_ref, send_ref, own_ref):
        s = g_ref[...] + r_ref[...]
        send_ref[...] = s.astype(send_ref.dtype)

        @pl.when(pl.program_id(1) == ids_ref[1])
        def _():
            own_ref[...] = s

    if layout == "row":
        g_spec = pl.BlockSpec((tr, c), lambda t, j, ids_ref: ((j * r + ids_ref[0] * hr) // tr + t, 0))
    elif layout == "col":
        g_spec = pl.BlockSpec((tr, c), lambda t, j, ids_ref: (ids_ref[0] * nb + t, j))
    else:
        g_spec = pl.BlockSpec((None, tr, c), lambda t, j, ids_ref: (j, ids_ref[0] * nb + t, 0))
    grid_spec = pltpu.PrefetchScalarGridSpec(
        num_scalar_prefetch=1, grid=(nb, 4),
        in_specs=[g_spec, pl.BlockSpec((None, tr, c), lambda t, j, ids_ref: (j, t, 0))],
        out_specs=[pl.BlockSpec((None, tr, c), lambda t, j, ids_ref: (j, t, 0)),
                   pl.BlockSpec((tr, c), lambda t, j, ids_ref: (t, 0))])
    return pl.pallas_call(
        body, name=name, grid_spec=grid_spec,
        out_shape=[jax.ShapeDtypeStruct((4, hr, c), BF16), jax.ShapeDtypeStruct((hr, c), F32)],
        compiler_params=_cparams(2),
    )(ids, grad, recv)


def _chip_sum(name, own, recv):
    hr, c = own.shape
    tr = _row_tile(hr, SUM_ROWS)

    def body(o_ref, r_ref, out_ref):
        out_ref[...] = ((o_ref[...] + r_ref[0].astype(F32)) + r_ref[1].astype(F32)) + r_ref[2].astype(F32)

    return pl.pallas_call(
        body, name=name, grid=(hr // tr,),
        in_specs=[pl.BlockSpec((tr, c), lambda i: (i, 0)), pl.BlockSpec((3, tr, c), lambda i: (0, i, 0))],
        out_specs=pl.BlockSpec((tr, c), lambda i: (i, 0)),
        out_shape=jax.ShapeDtypeStruct((hr, c), F32), compiler_params=_cparams(1),
    )(own, recv)


def _chip_sum_small(own, recv, ids):
    def body(ids_ref, o_ref, r_ref, out_ref):
        j = ids_ref[1]
        total = None
        for i in range(4):
            m = jnp.bitwise_xor(i, j)
            term = jnp.where(m == 0, o_ref[...], jnp.where(m == 2, r_ref[0], jnp.where(m == 1, r_ref[1], r_ref[2])))
            total = term if total is None else total + term
        out_ref[...] = total

    grid_spec = pltpu.PrefetchScalarGridSpec(
        num_scalar_prefetch=1, grid=(1,),
        in_specs=[pl.BlockSpec(own.shape, lambda i, ids_ref: (0, 0)), pl.BlockSpec(recv.shape, lambda i, ids_ref: (0, 0, 0))],
        out_specs=pl.BlockSpec(own.shape, lambda i, ids_ref: (0, 0)))
    return pl.pallas_call(body, name="chip_sum_small", grid_spec=grid_spec,
                          out_shape=jax.ShapeDtypeStruct(own.shape, F32), compiler_params=_cparams(1))(ids, own, recv)


def _adamw(name, w, m, v, mine, theirs, ids):
    rows, cols = w.shape
    half = rows // 2
    tr = _row_tile(half, ADAM_ROWS, unit=8)
    nb = half // tr
    c1 = 1.0 / (1.0 - ADAM_B1 ** ADAM_STEP)
    c2 = 1.0 / (1.0 - ADAM_B2 ** ADAM_STEP)

    def body(ids_ref, w_ref, m_ref, v_ref, mine_ref, theirs_ref, g_out, d_out, m_out, v_out):
        g = jnp.where(pl.program_id(0) == ids_ref[0], mine_ref[...], theirs_ref[...])
        m_new = ADAM_B1 * m_ref[...] + (1.0 - ADAM_B1) * g
        v_new = ADAM_B2 * v_ref[...] + (1.0 - ADAM_B2) * (g * g)
        d_out[...] = -ADAM_LR * ((m_new * c1) / (jnp.sqrt(v_new * c2) + ADAM_EPS) + ADAM_WD * w_ref[...])
        g_out[...] = g
        m_out[...] = m_new
        v_out[...] = v_new

    full = pl.BlockSpec((tr, cols), lambda h, i, ids_ref: (h * nb + i, 0))
    part = pl.BlockSpec((tr, cols), lambda h, i, ids_ref: (i, 0))
    grid_spec = pltpu.PrefetchScalarGridSpec(num_scalar_prefetch=1, grid=(2, nb),
                                             in_specs=[full, full, full, part, part], out_specs=[full] * 4)
    return pl.pallas_call(
        body, name=name, grid_spec=grid_spec,
        out_shape=[jax.ShapeDtypeStruct((rows, cols), F32)] * 4, compiler_params=_cparams(2),
    )(ids, w, m, v, mine, theirs)


def _adamw_replicated(g_rows, ws, ms, vs):
    n = len(ws)
    layout, _ = _rep_rows()
    c1 = 1.0 / (1.0 - ADAM_B1 ** ADAM_STEP)
    c2 = 1.0 / (1.0 - ADAM_B2 ** ADAM_STEP)

    def body(g_ref, *refs):
        w_refs, m_refs, v_refs = refs[0:n], refs[n:2 * n], refs[2 * n:3 * n]
        outs = refs[3 * n:]
        for k, (r0, rows) in enumerate(layout):
            width = w_refs[k].shape[1]
            g = jnp.concatenate([g_ref[r0 + j:r0 + j + 1, :] for j in range(rows)], axis=1)[:, 0:width]
            m_new = ADAM_B1 * m_refs[k][...] + (1.0 - ADAM_B1) * g
            v_new = ADAM_B2 * v_refs[k][...] + (1.0 - ADAM_B2) * (g * g)
            outs[k][...] = g
            outs[n + k][...] = -ADAM_LR * ((m_new * c1) / (jnp.sqrt(v_new * c2) + ADAM_EPS) + ADAM_WD * w_refs[k][...])
            outs[2 * n + k][...] = m_new
            outs[3 * n + k][...] = v_new

    res = pl.pallas_call(body, name="adamw_replicated",
                         out_shape=[jax.ShapeDtypeStruct(w.shape, F32) for _ in range(4) for w in ws])(g_rows, *ws, *ms, *vs)
    return [res[k * n:(k + 1) * n] for k in range(4)]


def _small_shard(parts):
    return _flatten(parts, _BIG[-1][1])


_ENTRY = {e[0]: e for e in _BIG}
FFN_MATS = ("w_ffn_down", "w_ffn_up")
MIXER_MATS = ("w_mix_out", "w_ssm_out", "w_attn_out")


class _StepPlan:
    def __init__(self, w, late_shards, shards, ids):
        self.w, self.g = w, {}
        self.late_shards, self.shards, self.ids = late_shards, shards, ids
        self.sums, self.halves, self.results = {}, {}, {}

    def run(self, name, fn, *args, **kw):
        at = getattr(self, "_at_" + name, None)
        if at is None:
            return fn(*args, **kw)
        exchange, landed = at()
        res, extra = fn(*args, bg=exchange, **kw)
        landed(extra)
        return res

    def _at_ssd_fwd(self):
        def landed(fulls):
            self.partly_gathered = fulls

        return _gather_ici([_ENTRY[n] for n in MIXER_MATS], [self.late_shards[n] for n in MIXER_MATS]), landed

    def _at_attn_fwd(self):
        stages = (_gather_pass_on([_ENTRY[n] for n in MIXER_MATS], self.partly_gathered),
                  _gather_ici([_ENTRY[n] for n in FFN_MATS], [self.late_shards[n] for n in FFN_MATS]))

        def landed(extra):
            mixer, self.partly_gathered = _split(stages, extra)
            self.w.update(zip(MIXER_MATS, mixer))

        return _join(*stages), landed

    def _at_ssm_gate_norm(self):
        return (_gather_pass_on([_ENTRY[n] for n in FFN_MATS], self.partly_gathered),
                lambda fulls: self.w.update(zip(FFN_MATS, fulls)))

    def pair_sums(self, names, grads, recv):
        for n, gr, rv in zip(names, grads, recv):
            _, r, c, lay = _ENTRY[n]
            self.sums[n] = _pair_sum("pair_sum_" + n, gr, rv, self.ids, r, c, lay)

    def chip_sums(self, names, recv):
        for n, rv in zip(names, recv):
            self.halves[n] = _chip_sum("chip_sum_" + n, self.sums[n][1], rv)

    def adamw(self, names, theirs):
        for n, th in zip(names, theirs):
            sh = self.shards[n]
            self.results[n] = _adamw("adamw_" + n, sh["w"], sh["m"], sh["v"], self.halves[n], th, self.ids)

    def _pair_stage(self, names, grads):
        return (_pair_exchange([_ENTRY[n] for n in names], grads),
                lambda recv: self.pair_sums(names, grads, recv))

    def _at_ffn_up_dx(self):
        return self._pair_stage(FFN_MATS, [self.g[n] for n in FFN_MATS])

    def _at_ssm_gate_norm_bwd(self):
        return self._pair_stage(MIXER_MATS, [self.g[n] for n in MIXER_MATS])

    def _at_attn_bwd(self):
        return _chip_exchange([self.sums[n][0] for n in FFN_MATS]), lambda recv: self.chip_sums(FFN_MATS, recv)

    def _at_ssd_bwd(self):
        stages = (_chip_exchange([self.sums[n][0] for n in MIXER_MATS]),
                  _whole_to_sibling([self.halves[n] for n in FFN_MATS]))

        def landed(extra):
            recv, theirs = _split(stages, extra)
            self.chip_sums(MIXER_MATS, recv)
            self.adamw(FFN_MATS, theirs)

        return _join(*stages), landed

    def _at_ssm_conv_bwd(self):
        return _whole_to_sibling([self.halves[n] for n in MIXER_MATS]), lambda theirs: self.adamw(MIXER_MATS, theirs)

    def _at_in_proj_dx(self):
        grads = [_from_cat(self.g.pop("w_cat"))]
        self.pair_sums(("w_in",), grads, _run_exchange("grad_pair_exchange_w_in", _pair_exchange([_ENTRY["w_in"]], grads)))
        return _chip_exchange([self.sums["w_in"][0]]), lambda recv: self.chip_sums(("w_in",), recv)

    def finish(self, g_small, g_rep, rep_shards):
        stages = (_pair_exchange([_ENTRY["small"]], [g_small]), _whole_to_sibling([g_rep]))
        recv_small, recv_rep = _split(stages, _run_exchange("grad_pair_exchange_tail", _join(*stages)))
        self.pair_sums(("small",), [g_small], recv_small)
        p_rep, = _rowwise("pair_sum_replicated", lambda r0, a, b: [a + b], SMALL_ROWS, SMALL_ROWS,
                          [(g_rep, LANES, 0), (recv_rep[0], LANES, 0)], [], [(LANES, F32)], [])
        stages = (_chip_exchange([self.sums["small"][0]]), _to_all_chips(p_rep))
        recv, recv_rep = _split(stages, _run_exchange("grad_chip_exchange_tail", _join(*stages)))
        self.chip_sums(("small",), recv)
        g_rep_tot = _chip_sum_small(p_rep, recv_rep[0], self.ids)
        last = ("w_in", "small")
        self.adamw(last, _run_exchange("grad_half_share_tail", _whole_to_sibling([self.halves[n] for n in last])))
        self.results["replicated"] = _adamw_replicated(g_rep_tot, rep_shards["w"], rep_shards["m"], rep_shards["v"])
        return g_rep_tot[_rep_rows()[1], 0]


def kernel(x, meta_tokens, norm_pre_mix, w_in, ssm_conv_w, ssm_conv_b, ssm_dt_bias, ssm_a_log, ssm_d_skip, ssm_norm, w_ssm_out, attn_sinks, w_attn_out, w_mix_out, norm_post_mix, norm_pre_ffn, w_ffn_up, ffn_conv_w, ffn_conv_b, w_ffn_down, norm_post_ffn, loss_target, m_meta_tokens, m_norm_pre_mix, m_w_in, m_ssm_conv_w, m_ssm_conv_b, m_ssm_dt_bias, m_ssm_a_log, m_ssm_d_skip, m_ssm_norm, m_w_ssm_out, m_attn_sinks, m_w_attn_out, m_w_mix_out, m_norm_post_mix, m_norm_pre_ffn, m_w_ffn_up, m_ffn_conv_w, m_ffn_conv_b, m_w_ffn_down, m_norm_post_ffn, v_meta_tokens, v_norm_pre_mix, v_w_in, v_ssm_conv_w, v_ssm_conv_b, v_ssm_dt_bias, v_ssm_a_log, v_ssm_d_skip, v_ssm_norm, v_w_ssm_out, v_attn_sinks, v_w_attn_out, v_w_mix_out, v_norm_post_mix, v_norm_pre_ffn, v_w_ffn_up, v_ffn_conv_w, v_ffn_conv_b, v_w_ffn_down, v_norm_post_ffn):
    args = dict(locals())
    squeeze = lambda a: a.reshape(a.shape[-2:])
    wts = {n: squeeze(args[n]) for n in WEIGHT_ORDER}
    mom = {n: squeeze(args["m_" + n]) for n in WEIGHT_ORDER}
    var = {n: squeeze(args["v_" + n]) for n in WEIGHT_ORDER}
    x_i, y_i, c_i = _mesh_pos()
    ids = jnp.stack([c_i, _chip_index(x_i, y_i)]).astype(jnp.int32)
    big_names = [n for n, _, _, _ in _BIG[:-1]]
    small_names = [n for n, _, _ in _SMALL_SHARDED]
    rep_names = [n for n, _ in _REPLICATED]

    stacks = {"w": wts, "m": mom, "v": var}
    shards = {n: {k: d[n] for k, d in stacks.items()} for n in big_names}
    shards["small"] = {k: _small_shard([d[n] for n in small_names]) for k, d in stacks.items()}
    rep_shards = {k: [d[n] for n in rep_names] for k, d in stacks.items()}

    w_in4, small_all = _gather_weights([_ENTRY["w_in"], _ENTRY["small"]], [wts["w_in"].astype(BF16), shards["small"]["w"]])
    w = {n: wts[n] for n in rep_names}
    w["w_cat"] = _to_cat(w_in4)
    small_parts = [_unflatten(small_all[i], [shp for _, shp, _ in _SMALL_SHARDED]) for i in range(4)]
    for k, (n, _, axis) in enumerate(_SMALL_SHARDED):
        w[n] = jnp.concatenate([small_parts[i][k] for i in range(4)], axis=axis)
    plan = _StepPlan(w, {n: wts[n].astype(BF16) for n in MIXER_MATS + FFN_MATS}, shards, ids)

    head = jnp.concatenate([jnp.zeros((PAD, D_MODEL), F32), w["meta_tokens"]], axis=0)
    loss_sum, dx, dhead = _local_step(x[0], head, loss_target[0], plan)
    g = plan.g
    g["meta_tokens"] = dhead[PAD:]
    g_small = jnp.stack([_small_shard([_shard_of(g[n], i, shp, ax) for n, shp, ax in _SMALL_SHARDED]) for i in range(4)])
    loss_part = (loss_sum * (0.5 / D_MODEL)).reshape(1, 1)
    loss = plan.finish(g_small, _in_rows([g[n] for n in rep_names] + [loss_part]), rep_shards)

    results = {}
    for kind in range(4):
        results.update({(kind, n): plan.results[n][kind] for n in big_names})
        parts = _unflatten(plan.results["small"][kind], [shp for _, shp, _ in _SMALL_SHARDED])
        results.update({(kind, n): parts[k] for k, n in enumerate(small_names)})
        results.update({(kind, n): plan.results["replicated"][kind][k] for k, n in enumerate(rep_names)})
    outs = [results[kind, n].reshape(args[n].shape) for kind in range(4) for n in WEIGHT_ORDER]
    return (loss, dx[None], *outs)
```

```python
import math
from typing import Any, Callable, NamedTuple, Sequence

import jax
import jax.numpy as jnp
from jax import lax
from jax.experimental import pallas as pl
from jax.experimental.pallas import tpu as pltpu

F32 = jnp.float32
BF16 = jnp.bfloat16

D_MODEL = 1024
N_META = 16
T = 128
PAD = T - N_META
D_INNER = 2048
SSM_HEADS = 32
HEAD_P = 64
SSM_GROUPS = 4
GROUP_W = D_INNER // SSM_GROUPS
D_STATE = 128
CONV_DIM = D_INNER + 2 * SSM_GROUPS * D_STATE
ATTN_HEADS = 16
KV_HEADS = 4
ATTN_W = 1024
KV_W = 256
FFN_DIM = 2816
N_IN = 8736
EPS = 1e-6
NEG = -1e30
SCALE = 0.125

P_Q, P_K, P_V, P_DT, P_Z, P_GATE, P_XBC = 0, 1024, 1280, 1536, 2048, 4096, 6144
QKV_W = 1536
P_W = 9216

ADAM_LR, ADAM_B1, ADAM_B2, ADAM_EPS, ADAM_WD, ADAM_STEP = 0.001, 0.9, 0.999, 1e-08, 0.01, 10

VMEM_BUDGET = 40 * 1024 * 1024
VMEM_LIMIT = 56 * 1024 * 1024
MESH = pl.DeviceIdType.MESH
ANY = pl.BlockSpec(memory_space=pl.ANY)


def _cparams(n_axes, **kw):
    return pltpu.CompilerParams(dimension_semantics=("arbitrary",) * n_axes, vmem_limit_bytes=VMEM_LIMIT, **kw)


class _Exchange(NamedTuple):
    ins: Sequence[Any]
    out_shapes: Sequence[Any]
    make_copies: Callable
    n_copies: int
    aliases: dict = {}


def _call(body, name, grid, in_specs, out_specs, out_shape, operands, scratch_shapes=(), aliases=None, bg=None):
    aliases = dict(aliases or {})
    if bg is None:
        return pl.pallas_call(body, name=name, grid=grid, in_specs=in_specs, out_specs=out_specs, out_shape=out_shape,
                              scratch_shapes=list(scratch_shapes), input_output_aliases=aliases,
                              compiler_params=_cparams(len(grid)))(*operands)
    n_in, n_out, n_scr = len(in_specs), len(out_specs), len(scratch_shapes)
    nb_in, nb_out = len(bg.ins), len(bg.out_shapes)

    def hosted(*refs):
        ins, bg_ins = refs[:n_in], refs[n_in:n_in + nb_in]
        outs = refs[n_in + nb_in:n_in + nb_in + n_out]
        bg_outs = refs[n_in + nb_in + n_out:n_in + nb_in + n_out + nb_out]
        scratch = refs[n_in + nb_in + n_out + nb_out:n_in + nb_in + n_out + nb_out + n_scr]
        send_sems, recv_sems = refs[-2:]
        pids = [pl.program_id(a) for a in range(len(grid))]
        first, last = pids[0] == 0, pids[0] == grid[0] - 1
        for p, g in zip(pids[1:], grid[1:]):
            first, last = first & (p == 0), last & (p == g - 1)
        copies = []
        for k, (src, dst, peer) in enumerate(bg.make_copies(bg_ins, bg_outs)):
            if peer is None:
                copies.append(pltpu.make_async_copy(src, dst, send_sems.at[k]))
            else:
                copies.append(pltpu.make_async_remote_copy(src_ref=src, dst_ref=dst, send_sem=send_sems.at[k],
                                                           recv_sem=recv_sems.at[k], device_id=peer, device_id_type=MESH))
        assert len(copies) == bg.n_copies

        @pl.when(first)
        def _():
            for cp in copies:
                cp.start()

        body(*ins, *outs, *scratch)

        @pl.when(last)
        def _():
            for cp in copies:
                cp.wait()

    aliases = {(k if k < n_in else k + nb_in): v for k, v in aliases.items()}
    aliases.update({n_in + k: n_out + v for k, v in bg.aliases.items()})
    res = pl.pallas_call(
        hosted, name=name, grid=grid, in_specs=list(in_specs) + [ANY] * nb_in, out_specs=list(out_specs) + [ANY] * nb_out,
        out_shape=list(out_shape) + list(bg.out_shapes), input_output_aliases=aliases,
        scratch_shapes=list(scratch_shapes) + [pltpu.SemaphoreType.DMA((bg.n_copies,))] * 2,
        compiler_params=_cparams(len(grid), has_side_effects=True))(*operands, *bg.ins)
    return res[:n_out], res[n_out:]


def _sigmoid(x):
    return 1.0 / (1.0 + jnp.exp(-x))


def _silu(x):
    return x * _sigmoid(x)


def _silu_grad(x):
    s = _sigmoid(x)
    return x * s, s * (1.0 + x * (1.0 - s))


def _dsilu(x):
    return _silu_grad(x)[1]


def _softplus(x):
    e = jnp.exp(-jnp.abs(x))
    small = e * (1.0 - e * (0.5 - e * (1.0 / 3.0)))
    return jnp.maximum(x, 0.0) + jnp.where(e < 0.01, small, jnp.log(1.0 + e))


def _rms(x, w):
    r = lax.rsqrt(jnp.mean(x * x, axis=-1, keepdims=True) + EPS)
    return x * r * w


def _rms_bwd(dy, x, w):
    r = lax.rsqrt(jnp.mean(x * x, axis=-1, keepdims=True) + EPS)
    xh = x * r
    g = dy * w
    dx = r * (g - xh * jnp.mean(g * xh, axis=-1, keepdims=True))
    dw = jnp.sum(dy * xh, axis=0, keepdims=True)
    return dx, dw


def _dot(a, b):
    return jnp.dot(a, b, preferred_element_type=F32)


def _dot_nt(a, b):
    return lax.dot_general(a, b, (((1,), (1,)), ((), ())), preferred_element_type=F32)


def _dot_tn(a, b):
    return lax.dot_general(a, b, (((0,), (0,)), ((), ())), preferred_element_type=F32)


def _split3(x):
    hi = x.astype(BF16)
    r = x - hi.astype(F32)
    mid = r.astype(BF16)
    lo = (r - mid.astype(F32)).astype(BF16)
    return hi, mid, lo


def _xdot(x, e):
    hi, mid, lo = _split3(x)
    return _dot(hi, e) + _dot(mid, e) + _dot(lo, e)


def _xdot_l(e, x):
    hi, mid, lo = _split3(x)
    return _dot(e, hi) + _dot(e, mid) + _dot(e, lo)


def _iota(shape, dim):
    return lax.broadcasted_iota(jnp.int32, shape, dim)


def _divisors(n, unit):
    return [t for t in range(unit, n + 1, unit) if n % t == 0]


MIN_MATMUL_STEPS = 8


def _matmul_tiles(m, n, k, a_bytes, b_bytes, o_bytes, m_unit):
    best = None
    for tm in _divisors(m, m_unit):
        for tn in _divisors(n, 128):
            for tk in _divisors(k, 128):
                acc = 0 if tk == k else tm * tn * 4
                vm = 2 * (tm * tk * a_bytes + tk * tn * b_bytes + tm * tn * o_bytes) + acc
                if vm > VMEM_BUDGET:
                    continue
                steps = (m // tm) * (n // tn) * (k // tk)
                score = (tk == k, min(steps, MIN_MATMUL_STEPS), min(tm, 256), tm * tn * tk)
                if best is None or score > best[0]:
                    best = (score, (tm, tn, tk))
    return best[1]


def _matmul(name, a, b, mode, out_dtype, bg=None):
    if mode == "nn":
        (m, k), n = a.shape, b.shape[1]
    elif mode == "nt":
        (m, k), n = a.shape, b.shape[0]
    else:
        (k, m), n = a.shape, b.shape[1]
    ab, bb, ob = a.dtype.itemsize, b.dtype.itemsize, jnp.dtype(out_dtype).itemsize
    tm, tn, tk = _matmul_tiles(m, n, k, ab, bb, ob, 128 if mode == "tn" else 16)
    nk = k // tk
    dot = {"nn": _dot, "nt": _dot_nt, "tn": _dot_tn}[mode]

    def body(a_ref, b_ref, o_ref, *scratch):
        prod = dot(a_ref[...].astype(BF16), b_ref[...].astype(BF16))
        if nk == 1:
            o_ref[...] = prod.astype(o_ref.dtype)
        else:
            acc_ref, = scratch
            kk = pl.program_id(2)

            @pl.when(kk == 0)
            def _():
                acc_ref[...] = prod

            @pl.when(kk > 0)
            def _():
                acc_ref[...] += prod

            @pl.when(kk == nk - 1)
            def _():
                o_ref[...] = acc_ref[...].astype(o_ref.dtype)

    a_spec = pl.BlockSpec((tk, tm), lambda i, j, kk: (kk, i)) if mode == "tn" else pl.BlockSpec((tm, tk), lambda i, j, kk: (i, kk))
    b_spec = pl.BlockSpec((tn, tk), lambda i, j, kk: (j, kk)) if mode == "nt" else pl.BlockSpec((tk, tn), lambda i, j, kk: (kk, j))
    res = _call(body, name, (m // tm, n // tn, nk), [a_spec, b_spec], [pl.BlockSpec((tm, tn), lambda i, j, kk: (i, j))],
                [jax.ShapeDtypeStruct((m, n), out_dtype)], [a, b],
                scratch_shapes=[] if nk == 1 else [pltpu.VMEM((tm, tn), F32)], bg=bg)
    return res[0] if bg is None else (res[0][0], res[1])


def _row_tile(n_rows, cap, unit=16):
    return max(t for t in _divisors(n_rows, unit) if t <= cap)


ROW_SUB = 384
GROUP_UNROLL = 4


def _rowwise(name, fn, n_rows, tm, row_ins, full_ins, row_outs, acc_outs, bg=None):
    n_in = len(row_ins) + len(full_ins)
    n_ro = len(row_outs)
    into = [(k, o[3]) for k, o in enumerate(row_outs) if len(o) > 2 and o[2] == "into"]

    n_row_in = len(row_ins)
    sub = min(tm, ROW_SUB)

    def body(*refs):
        i = pl.program_id(0)
        outs = refs[n_in + len(into):]

        sums = tuple(jnp.zeros((1, w), F32) for w in acc_outs)
        for s in range(tm // sub):
            rows = pl.ds(s * sub, sub)
            vals = [r[rows, :] for r in refs[:n_row_in]] + [r[...] for r in refs[n_row_in:n_in]]
            res = fn(i * tm + s * sub, *vals)
            for o, r, v in zip(row_outs, outs[:n_ro], res[:n_ro]):
                if len(o) > 2 and o[2] == "first":
                    @pl.when(i == 0)
                    def _(r=r, v=v, rows=rows):
                        r[rows, :] = v.astype(r.dtype)
                else:
                    r[rows, :] = v.astype(r.dtype)
            sums = tuple(a + v for a, v in zip(sums, res[n_ro:]))

        @pl.when(i == 0)
        def _():
            for r, v in zip(outs[n_ro:], sums):
                r[...] = v

        @pl.when(i > 0)
        def _():
            for r, v in zip(outs[n_ro:], sums):
                r[...] += v

    def in_spec(entry):
        w, cb = entry[1], entry[2]
        if len(entry) > 3 and entry[3] == "prev":
            return pl.BlockSpec((tm, w), lambda i: (jnp.maximum(i - 1, 0), cb))
        if len(entry) > 3 and entry[3] == "first":
            return pl.BlockSpec((tm, w), lambda i: (0, cb))
        return pl.BlockSpec((tm, w), lambda i: (i, cb))

    def out_spec(o):
        if len(o) == 2:
            return pl.BlockSpec((tm, o[0]), lambda i: (i, 0)), jax.ShapeDtypeStruct((n_rows, o[0]), o[1])
        if o[2] == "new":
            return pl.BlockSpec((tm, o[0]), lambda i: (i, o[4])), jax.ShapeDtypeStruct((n_rows, o[3]), o[1])
        if o[2] == "into":
            return pl.BlockSpec((tm, o[0]), lambda i: (i, o[4])), jax.ShapeDtypeStruct(o[3].shape, o[3].dtype)
        if o[2] == "first":
            return pl.BlockSpec((tm, o[0]), lambda i: (0, 0)), jax.ShapeDtypeStruct((tm, o[0]), o[1])
        return pl.BlockSpec((tm, o[0]), lambda i: (jnp.maximum(i - 1, 0), 0)), jax.ShapeDtypeStruct((o[3], o[0]), o[1])

    in_specs = [in_spec(e) for e in row_ins]
    in_specs += [pl.BlockSpec(a.shape, lambda i: (0, 0)) for a in full_ins]
    in_specs += [pl.BlockSpec(memory_space=pl.ANY) for _ in into]
    specs_shapes = [out_spec(o) for o in row_outs]
    out_specs = [s for s, _ in specs_shapes] + [pl.BlockSpec((1, w), lambda i: (0, 0)) for w in acc_outs]
    out_shape = [s for _, s in specs_shapes] + [jax.ShapeDtypeStruct((1, w), F32) for w in acc_outs]
    return _call(body, name, (n_rows // tm,), in_specs, out_specs, out_shape,
                 [e[0] for e in row_ins] + list(full_ins) + [arr for _, arr in into],
                 aliases={n_in + a: k for a, (k, _) in enumerate(into)}, bg=bg)


def _valid_rows(first_row, tm, lo):
    return (first_row + _iota((tm, 1), 0)) >= lo


CONV_ROWS = 128
CONV_SUB = 16
CONV_LANES = 256


def _conv_specs(tm, width, blk, n_rows, after):
    specs = [pl.BlockSpec((tm, width), lambda i: (i, blk)),
             pl.BlockSpec((8, width), lambda i: (jnp.maximum(i * (tm // 8) - 1, 0), blk))]
    if after:
        specs.append(pl.BlockSpec((16, width), lambda i: (jnp.minimum((i + 1) * (tm // 16), n_rows // 16 - 1), blk)))
    return specs


def _conv_window(win, w_ref, b_ref, taps, c0, cw, n):
    acc = b_ref[:, c0:c0 + cw] + w_ref[taps - 1:taps, c0:c0 + cw] * win[8:8 + n]
    for k in range(taps - 1):
        acc = acc + w_ref[k:k + 1, c0:c0 + cw] * win[8 - (taps - 1) + k:8 - (taps - 1) + k + n]
    return acc


def _ffn_act(name, u_raw, conv_w, conv_b, n_rows):
    tm, sub, cw = CONV_ROWS, CONV_SUB, CONV_LANES
    taps, width = conv_w.shape
    half = width // 2

    def body(cur_ref, prev_ref, w_ref, b_ref, f_ref, ext_ref):
        i = pl.program_id(0)
        ext_ref[0:8, :] = jnp.where(i > 0, prev_ref[...], 0.0)
        ext_ref[8:8 + tm, :] = cur_ref[...]
        for q in range(half // cw):
            a0, g0 = q * cw, half + q * cw

            def group(s, carry):
                r = pl.multiple_of(s * sub, sub)
                a = _conv_window(ext_ref[pl.ds(r, sub + 8), a0:a0 + cw], w_ref, b_ref, taps, a0, cw, sub)
                g = _conv_window(ext_ref[pl.ds(r, sub + 8), g0:g0 + cw], w_ref, b_ref, taps, g0, cw, sub)
                f = jnp.where(_valid_rows(i * tm + r, sub, PAD), _silu(a) * g, 0.0)
                f_ref[pl.ds(r, sub), a0:a0 + cw] = f.astype(f_ref.dtype)
                return carry

            lax.fori_loop(0, tm // sub, group, 0, unroll=GROUP_UNROLL)

    return pl.pallas_call(
        body, name=name, grid=(n_rows // tm,),
        in_specs=_conv_specs(tm, width, 0, n_rows, False) + [pl.BlockSpec((taps, width), lambda i: (0, 0)),
                                                             pl.BlockSpec((1, width), lambda i: (0, 0))],
        out_specs=pl.BlockSpec((tm, half), lambda i: (i, 0)),
        out_shape=jax.ShapeDtypeStruct((n_rows, half), BF16),
        scratch_shapes=[pltpu.VMEM((tm + 8, width), F32)],
        compiler_params=_cparams(1),
    )(u_raw, u_raw, conv_w, conv_b)


def _conv_bwd(name, raw, raw_blk, dsrcs, chunk_src, conv_w, conv_b, n_rows, gated, into=None, into_blk=0, bg=None):
    taps, width = conv_w.shape
    half = width // 2 if gated else width
    tm, sub, cw = CONV_ROWS, CONV_SUB, CONV_LANES
    te = tm + 16
    nd = len(dsrcs)
    n_parts = 2 if gated else 1

    def body(*refs):
        cur_ref, prev_ref, next_ref = refs[0:3]
        dcur, dnext = refs[3:3 + nd], refs[3 + nd:3 + 2 * nd]
        w_ref, b_ref = refs[3 + 2 * nd:5 + 2 * nd]
        out_ref, acc_ref, ext_ref, du_ref = refs[-4:]
        i = pl.program_id(0)
        ext_ref[0:8, :] = jnp.where(i > 0, prev_ref[...], 0.0)
        ext_ref[8:8 + tm, :] = cur_ref[...]
        ext_ref[8 + tm:24 + tm, :] = next_ref[...]

        for q, (src, off) in enumerate(chunk_src):
            cols = [q * cw, half + q * cw][:n_parts]

            def conv_grad(r, d):
                pre = [_conv_window(ext_ref[pl.ds(r, sub + 8), c0:c0 + cw], w_ref, b_ref, taps, c0, cw, sub) for c0 in cols]
                row = i * tm + r + _iota((sub, 1), 0)
                live = (row >= PAD) & (row < n_rows)
                if gated:
                    act, dact = _silu_grad(pre[0])
                    dus = [d * pre[1] * dact, d * act]
                else:
                    dus = [d * _dsilu(pre[0])]
                for part, du in enumerate(dus):
                    du_ref[part, pl.ds(r, sub), :] = jnp.where(live, du, 0.0)

            def tile_rows(s, carry):
                r = pl.multiple_of(s * sub, sub)
                conv_grad(r, dcur[src][pl.ds(r, sub), off:off + cw].astype(F32))
                return carry

            lax.fori_loop(0, tm // sub, tile_rows, 0, unroll=GROUP_UNROLL)
            conv_grad(tm, dnext[src][:, off:off + cw].astype(F32))

            for part, c0 in enumerate(cols):
                taps_w = [w_ref[k:k + 1, c0:c0 + cw] for k in range(taps)]

                def back(s, sums):
                    new = list(sums)
                    for u in range(2):
                        r = pl.multiple_of((2 * s + u) * sub, sub)
                        win = du_ref[part, pl.ds(r, sub + 8), :]
                        raw_rows = ext_ref[pl.ds(8 + r, sub), c0:c0 + cw]
                        draw = jnp.zeros((sub, cw), F32)
                        for k in range(taps):
                            shifted = win[taps - 1 - k:taps - 1 - k + sub]
                            draw = draw + taps_w[k] * shifted
                            new[k] = new[k] + shifted * raw_rows
                        new[taps] = new[taps] + win[0:sub]
                        out_ref[pl.ds(r, sub), c0:c0 + cw] = jnp.where(_valid_rows(i * tm + r, sub, PAD), draw, 0.0).astype(out_ref.dtype)
                    return tuple(new)

                sums = lax.fori_loop(0, tm // (2 * sub), back, tuple(jnp.zeros((sub, cw), F32) for _ in range(taps + 1)))
                for k in range(taps + 1):
                    total = jnp.sum(sums[k], axis=0, keepdims=True)
                    acc_ref[k:k + 1, c0:c0 + cw] = jnp.where(i == 0, total, acc_ref[k:k + 1, c0:c0 + cw] + total)

    in_specs = _conv_specs(tm, width, raw_blk, n_rows, True)
    in_specs += [pl.BlockSpec((tm, d.shape[1]), lambda i: (i, 0)) for d in dsrcs]
    in_specs += [pl.BlockSpec((16, d.shape[1]), lambda i: (jnp.minimum((i + 1) * (tm // 16), n_rows // 16 - 1), 0)) for d in dsrcs]
    in_specs += [pl.BlockSpec((taps, width), lambda i: (0, 0)), pl.BlockSpec((1, width), lambda i: (0, 0))]
    operands = [raw, raw, raw] + list(dsrcs) + list(dsrcs) + [conv_w, conv_b]
    aliases = {}
    if into is None:
        out0 = jax.ShapeDtypeStruct((n_rows, width), BF16)
    else:
        in_specs.append(pl.BlockSpec(memory_space=pl.ANY))
        operands.append(into)
        aliases = {len(operands) - 1: 0}
        out0 = jax.ShapeDtypeStruct(into.shape, into.dtype)
    return _call(body, name, (n_rows // tm,), in_specs,
                 [pl.BlockSpec((tm, width), lambda i: (i, into_blk)), pl.BlockSpec((8, width), lambda i: (0, 0))],
                 [out0, jax.ShapeDtypeStruct((8, width), F32)], operands,
                 scratch_shapes=[pltpu.VMEM((tm + 24, width), F32), pltpu.VMEM((n_parts, te + 8, cw), F32)],
                 aliases=aliases, bg=bg)


def _ssd_specs(n_chunks, rev, per_step=1):
    cidx = (lambda c: n_chunks - 1 - c) if rev else (lambda c: c)
    xw, nw = per_step * GROUP_W, per_step * D_STATE
    xg0, bg0, cg0 = P_XBC // xw, (P_XBC + D_INNER) // nw, (P_XBC + D_INNER + SSM_GROUPS * D_STATE) // nw

    def cur(width, blk0):
        return pl.BlockSpec((T, width), lambda g, c: (cidx(c), blk0 + g))

    def prev(width, blk0):
        return pl.BlockSpec((8, width), lambda g, c: (jnp.maximum(cidx(c) * (T // 8) - 1, 0), blk0 + g))

    specs = [cur(xw, xg0), prev(xw, xg0), cur(nw, bg0), prev(nw, bg0), cur(nw, cg0), prev(nw, cg0),
             pl.BlockSpec((T, 128), lambda g, c: (cidx(c), P_DT // 128))]
    wb, wc = D_INNER // nw, (D_INNER + SSM_GROUPS * D_STATE) // nw
    specs += [pl.BlockSpec((4, xw), lambda g, c: (0, g)),
              pl.BlockSpec((4, nw), lambda g, c: (0, wb + g)),
              pl.BlockSpec((4, nw), lambda g, c: (0, wc + g)),
              pl.BlockSpec((1, xw), lambda g, c: (0, g)),
              pl.BlockSpec((1, nw), lambda g, c: (0, wb + g)),
              pl.BlockSpec((1, nw), lambda g, c: (0, wc + g))]
    specs += [pl.BlockSpec((1, 128), lambda g, c: (0, 0))] * 3
    return specs, cidx


def _ssd_chunk_forward(refs, ext_ref, g, c):
    (xc_ref, xp_ref, bc_ref, bp_ref, cc_ref, cp_ref, dt_ref, wx_ref, wb_ref, wc_ref,
     bx_ref, bb_ref, bcb_ref, dtb_ref, alog_ref, dsk_ref) = refs

    def conv_pre(cur_ref, prev_ref, w_ref, b_ref, width):
        ext_ref[0:8, 0:width] = jnp.where(c > 0, prev_ref[...], 0.0)
        ext_ref[8:8 + T, 0:width] = cur_ref[...]
        w = w_ref[...]
        acc = b_ref[...] + w[3:4] * cur_ref[...]
        for k in range(3):
            acc = acc + w[k:k + 1] * ext_ref[pl.ds(5 + k, T), 0:width]
        return acc

    valid = _valid_rows(c * T, T, PAD)
    v = {}
    v["valid"] = valid
    v["x_pre"] = conv_pre(xc_ref, xp_ref, wx_ref, bx_ref, GROUP_W)
    v["b_pre"] = conv_pre(bc_ref, bp_ref, wb_ref, bb_ref, D_STATE)
    v["c_pre"] = conv_pre(cc_ref, cp_ref, wc_ref, bcb_ref, D_STATE)
    xs = _silu(v["x_pre"])
    bm = jnp.where(valid, _silu(v["b_pre"]), 0.0)
    cm = jnp.where(valid, _silu(v["c_pre"]), 0.0)
    dtr = dt_ref[...] + dtb_ref[...]
    dt = jnp.where(valid, _softplus(dtr), 0.0)
    a_neg = -jnp.exp(alog_ref[...])
    a = dt * a_neg
    tril = _iota((T, T), 0) >= _iota((T, T), 1)
    cs = _xdot_l(tril.astype(BF16), a)
    hh, ll = _iota((128, GROUP_W), 0), _iota((128, GROUP_W), 1)
    expand = (hh == 8 * g + jnp.right_shift(ll, 6)).astype(BF16)
    sh, sj = _iota((128, 128), 0), _iota((128, 128), 1)
    select = ((sh == 8 * g + sj) & (sj < 8)).astype(BF16)
    hh_t, ll_t = _iota((GROUP_W, 128), 1), _iota((GROUP_W, 128), 0)
    v["expand_t"] = (hh_t == 8 * g + jnp.right_shift(ll_t, 6)).astype(BF16)
    v["select_t"] = ((sj == 8 * g + sh) & (sh < 8)).astype(BF16)
    cs_e = _xdot(cs, expand)
    dt_e = _xdot(dt, expand)
    cs_loc = _xdot(cs, select)
    cs_loc_t = cs_loc.T
    cs_last_e = cs_e[T - 1:T, :]
    v.update(xs=xs, bm=bm, cm=cm, dtr=dtr, dt=dt, a_neg=a_neg, tril=tril, expand=expand, select=select,
             cs_e=cs_e, dt_e=dt_e, cs_loc=cs_loc, cs_loc_t=cs_loc_t, cs_last_e=cs_last_e)
    v["xdt"] = xs * dt_e
    v["decay_e"] = jnp.exp(cs_last_e - cs_e)
    v["ecs_e"] = jnp.exp(cs_e)
    v["elast_e"] = jnp.exp(cs_last_e)
    v["d_e"] = _xdot(dsk_ref[...], expand)
    v["gmat"] = _dot_nt(cm.astype(BF16), bm.astype(BF16))
    return v


def _ssd_decay_pair(v, jp):
    out = []
    for j in (2 * jp, 2 * jp + 1):
        diff = v["cs_loc"][:, j:j + 1] - v["cs_loc_t"][j:j + 1, :]
        out.append(jnp.where(v["tril"], jnp.exp(jnp.where(v["tril"], diff, 0.0)), 0.0))
    return out


def _block_diag_pair(xp):
    lane = _iota(xp.shape, 1)
    return jnp.concatenate([jnp.where(lane < HEAD_P, xp, 0.0), jnp.where(lane >= HEAD_P, xp, 0.0)], axis=0)


SSD_GROUPS_PER_STEP = 4


def _ssd_group_refs(refs, gg):
    x_w, n_w = pl.ds(GROUP_W * gg, GROUP_W), pl.ds(D_STATE * gg, D_STATE)
    lanes = [x_w, x_w, n_w, n_w, n_w, n_w, None, x_w, n_w, n_w, x_w, n_w, n_w, None, None, None]
    return [r if w is None else r.at[:, w] for r, w in zip(refs, lanes)]


def _ssd_fwd(p, conv_w, conv_b, dt_bias, a_log, d_skip, n_chunks, bg=None):
    n_rows = n_chunks * T
    in_specs, _ = _ssd_specs(n_chunks, rev=False, per_step=SSD_GROUPS_PER_STEP)
    per = SSD_GROUPS_PER_STEP

    def body(*refs):
        y_ref, hin_ref, st_ref, ext_ref = refs[16:]
        g2, c = pl.program_id(0), pl.program_id(1)

        @pl.when(c == 0)
        def _():
            st_ref[...] = jnp.zeros_like(st_ref)

        for gg in range(per):
            v = _ssd_chunk_forward(_ssd_group_refs(refs[:16], gg), ext_ref.at[gg], per * g2 + gg, c)
            state = st_ref[gg]
            hin_ref[gg] = state
            ys = []
            for jp in range(4):
                l0, l1 = _ssd_decay_pair(v, jp)
                lhs = jnp.concatenate([v["gmat"] * l0, v["gmat"] * l1], axis=1).astype(BF16)
                rhs = _block_diag_pair(v["xdt"][:, 128 * jp:128 * jp + 128]).astype(BF16)
                ys.append(_dot(lhs, rhs))
            y = jnp.concatenate(ys, axis=1)
            y = y + _dot(v["cm"].astype(BF16), state.astype(BF16)) * v["ecs_e"] + v["xs"] * v["d_e"]
            y_ref[:, GROUP_W * gg:GROUP_W * gg + GROUP_W] = y
            s_new = _dot_tn(v["bm"].astype(BF16), (v["xdt"] * v["decay_e"]).astype(BF16))
            st_ref[gg] = state * v["elast_e"] + s_new

    return _call(
        body, "ssd_fwd", (SSM_GROUPS // per, n_chunks), in_specs,
        [pl.BlockSpec((T, per * GROUP_W), lambda g, c: (c, g)),
         pl.BlockSpec((per, None, D_STATE, GROUP_W), lambda g, c: (g, c, 0, 0))],
        [jax.ShapeDtypeStruct((n_rows, D_INNER), F32),
         jax.ShapeDtypeStruct((SSM_GROUPS, n_chunks, D_STATE, GROUP_W), F32)],
        [p, p, p, p, p, p, p, conv_w, conv_w, conv_w, conv_b, conv_b, conv_b, dt_bias, a_log, d_skip],
        scratch_shapes=[pltpu.VMEM((per, D_STATE, GROUP_W), F32), pltpu.VMEM((per, T + 8, GROUP_W), F32)], bg=bg)


def _ssd_bwd(p, conv_w, conv_b, dt_bias, a_log, d_skip, hin, dy, dp, n_chunks, bg=None):
    n_rows = n_chunks * T
    per = SSD_GROUPS_PER_STEP
    assert per == SSM_GROUPS
    dt_w = P_Z - P_DT
    in_specs, cidx = _ssd_specs(n_chunks, rev=True, per_step=per)
    in_specs = in_specs + [pl.BlockSpec((per, None, D_STATE, GROUP_W), lambda g, c: (g, cidx(c), 0, 0)),
                           pl.BlockSpec((T, per * GROUP_W), lambda g, c: (cidx(c), g)), ANY]

    def body(*refs):
        hin_ref, dy_ref = refs[16:18]
        dx_ref, db_ref, dc_ref, dp_ref, dpar_ref, dst_ref, ext_ref, ddt_ref = refs[19:]
        for gg in range(per):
            x_w, n_w = pl.ds(GROUP_W * gg, GROUP_W), pl.ds(D_STATE * gg, D_STATE)
            group_body(_ssd_group_refs(refs[:16], gg), hin_ref.at[gg], dy_ref.at[:, x_w], dx_ref.at[:, x_w],
                       db_ref.at[:, n_w], dc_ref.at[:, n_w], ddt_ref.at[:, n_w], dpar_ref.at[gg], dst_ref.at[gg],
                       ext_ref.at[gg], per * pl.program_id(0) + gg)
        ddt = ddt_ref[:, 0:128] + ddt_ref[:, 128:256] + ddt_ref[:, 256:384] + ddt_ref[:, 384:512]
        dp_ref[...] = jnp.concatenate([ddt, jnp.zeros((T, dt_w - 128), F32)], axis=1).astype(dp_ref.dtype)

    def group_body(in_refs, hin_ref, dy_ref, dx_ref, db_ref, dc_ref, ddt_ref, dpar_ref, dst_ref, ext_ref, g):
        step = pl.program_id(1)
        c = n_chunks - 1 - step

        @pl.when(step == 0)
        def _():
            dst_ref[...] = jnp.zeros_like(dst_ref)

        v = _ssd_chunk_forward(in_refs, ext_ref, g, c)
        hin_f = hin_ref[...]
        hin_b = hin_f.astype(BF16)
        dyv = dy_ref[...]
        dst = dst_ref[...]
        dst_b = dst.astype(BF16)
        xs, bm, cm, xdt = v["xs"], v["bm"], v["cm"], v["xdt"]
        bm_b, cm_b = bm.astype(BF16), cm.astype(BF16)

        dd_e = jnp.sum(dyv * xs, axis=0, keepdims=True)
        dxs = dyv * v["d_e"]
        ch = _dot(cm_b, hin_b)
        dch = (dyv * v["ecs_e"]).astype(BF16)
        dcm = _dot_nt(dch, hin_b)
        dhin = _dot_tn(cm_b, dch) + dst * v["elast_e"]
        dcs_e = dyv * ch * v["ecs_e"]
        dxd = _dot(bm_b, dst_b)
        dbm = _dot_nt((xdt * v["decay_e"]).astype(BF16), dst_b)
        dxdt_state = dxd * v["decay_e"]
        q = dxdt_state * xdt
        dcs_e = dcs_e - q
        dlast_e = jnp.sum(q, axis=0, keepdims=True) + jnp.sum(dst * hin_f, axis=0, keepdims=True) * v["elast_e"]
        dg = jnp.zeros((T, T), F32)
        rs_cols = jnp.zeros((T, 128), F32)
        cs_rows = jnp.zeros((128, T), F32)
        lane_i, sub_i = _iota((T, 128), 1), _iota((128, T), 0)
        dxdt_parts = []
        for jp in range(4):
            l0, l1 = _ssd_decay_pair(v, jp)
            m0, m1 = v["gmat"] * l0, v["gmat"] * l1
            xbd = _block_diag_pair(xdt[:, 128 * jp:128 * jp + 128]).astype(BF16)
            dyp = dyv[:, 128 * jp:128 * jp + 128]
            dm = _dot_nt(dyp.astype(BF16), xbd)
            dm0, dm1 = dm[:, 0:T], dm[:, T:2 * T]
            dg = dg + dm0 * l0 + dm1 * l1
            for j, qq in ((2 * jp, dm0 * m0), (2 * jp + 1, dm1 * m1)):
                rs_cols = jnp.where(lane_i == j, jnp.sum(qq, axis=1, keepdims=True), rs_cols)
                cs_rows = jnp.where(sub_i == j, jnp.sum(qq, axis=0, keepdims=True), cs_rows)
            mv = jnp.concatenate([m0, m1], axis=0).astype(BF16)
            dxdt_parts.append(_dot_tn(mv, _block_diag_pair(dyp).astype(BF16)))
        dxdt = jnp.concatenate(dxdt_parts, axis=1) + dxdt_state
        dg_b = dg.astype(BF16)
        dcm = dcm + _dot(dg_b, bm_b)
        dbm = dbm + _dot_tn(dg_b, cm_b)
        expand_t = v["expand_t"]
        dcs_loc = rs_cols - cs_rows.T
        last_row = _iota((T, 1), 0) == T - 1
        dcs_full_e = dcs_e + jnp.where(last_row, dlast_e, 0.0)
        dcs = _xdot(dcs_full_e, expand_t) + _xdot(dcs_loc, v["select_t"])
        triu = (_iota((T, T), 0) <= _iota((T, T), 1)).astype(BF16)
        da = _xdot_l(triu, dcs)
        ddt = da * v["a_neg"] + _xdot(dxdt * xs, expand_t)
        dxs = dxs + dxdt * v["dt_e"]
        ddtr = jnp.where(v["valid"], ddt * _sigmoid(v["dtr"]), 0.0)
        dx_ref[...] = dxs
        db_ref[...] = jnp.where(v["valid"], dbm, 0.0)
        dc_ref[...] = jnp.where(v["valid"], dcm, 0.0)
        ddt_ref[...] = ddtr
        dpar = jnp.concatenate([
            jnp.sum(ddtr, axis=0, keepdims=True),
            jnp.sum(da * v["dt"], axis=0, keepdims=True) * v["a_neg"],
            _xdot(dd_e, expand_t),
            jnp.zeros((5, 128), F32)], axis=0)

        @pl.when(step == 0)
        def _():
            dpar_ref[...] = dpar

        @pl.when(step > 0)
        def _():
            dpar_ref[...] += dpar

        dst_ref[...] = dhin

    return _call(
        body, "ssd_bwd", (SSM_GROUPS // per, n_chunks), in_specs,
        [pl.BlockSpec((T, per * GROUP_W), lambda g, c: (cidx(c), g)),
         pl.BlockSpec((T, per * D_STATE), lambda g, c: (cidx(c), g)),
         pl.BlockSpec((T, per * D_STATE), lambda g, c: (cidx(c), g)),
         pl.BlockSpec((T, dt_w), lambda g, c: (cidx(c), P_DT // dt_w)),
         pl.BlockSpec((per, 8, 128), lambda g, c: (g, 0, 0))],
        [jax.ShapeDtypeStruct((n_rows, D_INNER), F32),
         jax.ShapeDtypeStruct((n_rows, SSM_GROUPS * D_STATE), F32),
         jax.ShapeDtypeStruct((n_rows, SSM_GROUPS * D_STATE), F32),
         jax.ShapeDtypeStruct(dp.shape, dp.dtype),
         jax.ShapeDtypeStruct((SSM_GROUPS, 8, 128), F32)],
        [p, p, p, p, p, p, p, conv_w, conv_w, conv_w, conv_b, conv_b, conv_b, dt_bias, a_log, d_skip, hin, dy, dp],
        scratch_shapes=[pltpu.VMEM((per, D_STATE, GROUP_W), F32), pltpu.VMEM((per, T + 8, GROUP_W), F32),
                        pltpu.VMEM((T, per * 128), F32)],
        aliases={18: 3}, bg=bg)


def _alibi_slope(h):
    return 2.0 ** (-8.0 * (h + 1) / ATTN_HEADS)


def _dup_half(x256, kvh):
    xb = x256[:, 128 * (kvh // 2):128 * (kvh // 2) + 128]
    rolled = pltpu.roll(xb, 64, 1)
    lane = _iota(xb.shape, 1)
    if kvh % 2 == 0:
        return jnp.where(lane < 64, xb, rolled)
    return jnp.where(lane < 64, rolled, xb)


def _attn_masks(c):
    qi, j = _iota((T, T), 0), _iota((T, T), 1)
    tri = j <= qi
    meta_ok = (j >= PAD) & (j - PAD <= c * T + qi - PAD)
    band_ok = c >= jnp.where(tri, 1, 2)
    dist = jnp.bitwise_and(qi - j, T - 1).astype(F32)
    return tri, meta_ok, band_ok, dist


def _fold(x3, tri):
    return jnp.concatenate([x3[:, 0:T], jnp.where(tri, x3[:, 2 * T:3 * T], x3[:, T:2 * T])], axis=1)


def _unfold(x2, tri):
    band = x2[:, T:2 * T]
    return jnp.concatenate([x2[:, 0:T], jnp.where(tri, 0.0, band), jnp.where(tri, band, 0.0)], axis=1)


def _attn_scores(qp, k3, masks, h0):
    tri, meta_ok, band_ok, dist = masks
    lane = _iota(qp.shape, 1)
    s = []
    for half, h in ((0, h0), (1, h0 + 1)):
        qh = jnp.where((lane < 64) if half == 0 else (lane >= 64), qp, 0.0).astype(BF16)
        raw = _dot_nt(qh, k3)
        band = jnp.where(tri, raw[:, 2 * T:3 * T], raw[:, T:2 * T]) - _alibi_slope(h) * dist
        s.append((qh, jnp.concatenate([jnp.where(meta_ok, raw[:, 0:T], NEG), jnp.where(band_ok, band, NEG)], axis=1)))
    return s


def _attn_fwd(p, sinks, n_chunks, bg=None):
    n_rows = n_chunks * T
    kb, vb = P_K // KV_W, P_V // KV_W

    def body(q_ref, kc_ref, kp_ref, km_ref, vc_ref, vp_ref, vm_ref, sink_ref, o_ref, lse_ref):
        c = pl.program_id(0)
        sinks_v = sink_ref[...]
        masks = _attn_masks(c)
        tri, meta_ok, band_ok, dist = masks
        lane = _iota((T, 128), 1)
        for kvh in range(KV_HEADS):
            k3 = jnp.concatenate([_dup_half(r[...], kvh) for r in (km_ref, kp_ref, kc_ref)], axis=0).astype(BF16)
            v3 = jnp.concatenate([_dup_half(r[...], kvh) for r in (vm_ref, vp_ref, vc_ref)], axis=0)
            v3bd = _block_diag_rows(v3).astype(BF16)
            q2 = q_ref[:, 256 * kvh:256 * kvh + 256] * SCALE
            q4 = jnp.concatenate([jnp.where((lane < 64) if half == 0 else (lane >= 64), q2[:, 128 * pr:128 * pr + 128], 0.0)
                                  for pr in range(2) for half in range(2)], axis=0).astype(BF16)
            raw4 = _dot_nt(q4, k3)
            probs = []
            for hh in range(4):
                h = 4 * kvh + hh
                raw = raw4[T * hh:T * hh + T]
                band = jnp.where(tri, raw[:, 2 * T:3 * T], raw[:, T:2 * T]) - _alibi_slope(h) * dist
                sc = jnp.concatenate([jnp.where(meta_ok, raw[:, 0:T], NEG), jnp.where(band_ok, band, NEG)], axis=1)
                sink = sinks_v[:, h:h + 1]
                m = jnp.maximum(jnp.max(sc, axis=1, keepdims=True), sink)
                e = jnp.exp(sc - m)
                den = jnp.sum(e, axis=1, keepdims=True) + jnp.exp(sink - m)
                probs.append(_unfold(e * (1.0 / den), tri))
                lse_ref[:, h:h + 1] = m + jnp.log(den)
            p4 = jnp.concatenate([jnp.concatenate(probs[0:2], axis=1), jnp.concatenate(probs[2:4], axis=1)], axis=0)
            out = _dot(p4.astype(BF16), v3bd)
            o_ref[:, 256 * kvh:256 * kvh + 256] = jnp.concatenate([out[0:T], out[T:2 * T]], axis=1).astype(o_ref.dtype)

    blk = lambda width, col: pl.BlockSpec((T, width), lambda c: (c, col))
    prev = lambda width, col: pl.BlockSpec((T, width), lambda c: (jnp.maximum(c - 1, 0), col))
    first = lambda width, col: pl.BlockSpec((T, width), lambda c: (0, col))
    return _call(
        body, "attn_fwd", (n_chunks,),
        [blk(ATTN_W, P_Q // ATTN_W), blk(KV_W, kb), prev(KV_W, kb), first(KV_W, kb),
         blk(KV_W, vb), prev(KV_W, vb), first(KV_W, vb), pl.BlockSpec((1, 128), lambda c: (0, 0))],
        [pl.BlockSpec((T, ATTN_W), lambda c: (c, 0)), pl.BlockSpec((T, 128), lambda c: (c, 0))],
        [jax.ShapeDtypeStruct((n_rows, ATTN_W), BF16), jax.ShapeDtypeStruct((n_rows, 128), F32)],
        [p, p, p, p, p, p, p, sinks], bg=bg)


def _block_diag_rows(x3):
    lane = _iota(x3.shape, 1)
    return jnp.concatenate([jnp.where(lane < 64, x3, 0.0), jnp.where(lane >= 64, x3, 0.0)], axis=0)


def _fold_halves(x):
    return x + pltpu.roll(x, 64, 1)


def _attn_bwd(p, sinks, ao, lse, dao, dp, n_chunks, bg=None):
    kb, vb = P_K // KV_W, P_V // KV_W
    rc = lambda s: n_chunks - 1 - s

    def body(q_ref, kc_ref, kp_ref, km_ref, vc_ref, vp_ref, vm_ref, sink_ref, o_ref, lse_ref, do_ref, dp_in_ref,
             dqkv_ref, dsink_ref, kcar_ref, vcar_ref, kmeta_ref, vmeta_ref):
        step = pl.program_id(0)
        c = n_chunks - 1 - step

        @pl.when(step == 0)
        def _():
            for r in (kcar_ref, vcar_ref, kmeta_ref, vmeta_ref):
                r[...] = jnp.zeros_like(r)

        masks = _attn_masks(c)
        tri = masks[0]
        q = q_ref[...] * SCALE
        sinks_v = sink_ref[...]
        lse_v = lse_ref[...]
        ov = o_ref[...].astype(F32)
        dov = do_ref[...].astype(F32)
        lane = _iota((T, 128), 1)
        lane256 = _iota((3 * T, KV_W), 1)
        dsink = jnp.zeros((1, 128), F32)
        dk3_all = jnp.zeros((3 * T, KV_W), F32)
        dv3_all = jnp.zeros((3 * T, KV_W), F32)
        dqs = []
        for kvh in range(KV_HEADS):
            k3 = jnp.concatenate([_dup_half(r[...], kvh) for r in (km_ref, kp_ref, kc_ref)], axis=0).astype(BF16)
            v3 = jnp.concatenate([_dup_half(r[...], kvh) for r in (vm_ref, vp_ref, vc_ref)], axis=0).astype(BF16)
            dk3 = jnp.zeros((3 * T, 128), F32)
            dv3 = jnp.zeros((3 * T, 128), F32)
            for pr in range(2):
                h0 = 4 * kvh + 2 * pr
                blk = 2 * kvh + pr
                qp = q[:, 128 * blk:128 * blk + 128]
                dop = dov[:, 128 * blk:128 * blk + 128]
                prod = dop * ov[:, 128 * blk:128 * blk + 128]
                dq_pair = jnp.zeros((T, 128), F32)
                for half, ((qh, sc), h) in enumerate(zip(_attn_scores(qp, k3, masks, h0), (h0, h0 + 1))):
                    mine = (lane < 64) if half == 0 else (lane >= 64)
                    lse_h = lse_v[:, h:h + 1]
                    pm = jnp.exp(sc - lse_h)
                    doh = jnp.where(mine, dop, 0.0).astype(BF16)
                    delta = jnp.sum(jnp.where(mine, prod, 0.0), axis=1, keepdims=True)
                    dp = _fold(_dot_nt(doh, v3), tri)
                    ds = _unfold(pm * (dp - delta), tri).astype(BF16)
                    p_sink = jnp.exp(sinks_v[:, h:h + 1] - lse_h)
                    dsink = jnp.where(_iota((1, 128), 1) == h, jnp.sum(-p_sink * delta, axis=0, keepdims=True), dsink)
                    dq_pair = jnp.where(mine, _dot(ds, k3), dq_pair)
                    dk3 = dk3 + _dot_tn(ds, qh)
                    dv3 = dv3 + _dot_tn(_unfold(pm, tri).astype(BF16), doh)
                dqs.append(dq_pair * SCALE)
            in_place = (lane256 >= 64 * kvh) & (lane256 < 64 * kvh + 64)
            wide = lambda x: jnp.concatenate([x, x], axis=1)
            dk3_all = jnp.where(in_place, wide(_fold_halves(dk3)), dk3_all)
            dv3_all = jnp.where(in_place, wide(_fold_halves(dv3)), dv3_all)
        dsink_all = dsink

        @pl.when(step == 0)
        def _():
            dsink_ref[...] = dsink_all

        @pl.when(step > 0)
        def _():
            dsink_ref[...] += dsink_all

        kmeta = kmeta_ref[...] + dk3_all[0:T]
        vmeta = vmeta_ref[...] + dv3_all[0:T]
        kmeta_ref[...] = kmeta
        vmeta_ref[...] = vmeta
        is_first = c == 0
        dk = jnp.where(is_first, kmeta, dk3_all[2 * T:3 * T] + kcar_ref[...])
        dv = jnp.where(is_first, vmeta, dv3_all[2 * T:3 * T] + vcar_ref[...])
        dqkv_ref[...] = jnp.concatenate(dqs + [dk, dv], axis=1).astype(dqkv_ref.dtype)
        kcar_ref[...] = dk3_all[T:2 * T]
        vcar_ref[...] = dv3_all[T:2 * T]

    blk = lambda width, col: pl.BlockSpec((T, width), lambda s: (rc(s), col))
    prev = lambda width, col: pl.BlockSpec((T, width), lambda s: (jnp.maximum(rc(s) - 1, 0), col))
    first = lambda width, col: pl.BlockSpec((T, width), lambda s: (0, col))
    return _call(
        body, "attn_bwd", (n_chunks,),
        [blk(ATTN_W, P_Q // ATTN_W), blk(KV_W, kb), prev(KV_W, kb), first(KV_W, kb),
         blk(KV_W, vb), prev(KV_W, vb), first(KV_W, vb), pl.BlockSpec((1, 128), lambda s: (0, 0)),
         blk(ATTN_W, 0), blk(128, 0), blk(ATTN_W, 0), ANY],
        [blk(QKV_W, P_Q // QKV_W), pl.BlockSpec((1, 128), lambda s: (0, 0))],
        [jax.ShapeDtypeStruct(dp.shape, dp.dtype), jax.ShapeDtypeStruct((1, 128), F32)],
        [p, p, p, p, p, p, p, sinks, ao, lse, dao, dp],
        scratch_shapes=[pltpu.VMEM((T, KV_W), F32)] * 4, aliases={11: 0}, bg=bg)


def _pad_lanes(v, width=128):
    return jnp.pad(v, ((0, 0), (0, width - v.shape[1])))


def _local_step(x, head, tgt, plan):
    w, g, run = plan.w, plan.g, plan.run
    n_tok = x.shape[0]
    n_rows = n_tok + T
    n_chunks = n_rows // T
    tm = _row_tile(n_rows, 384)
    dt_bias, a_log, d_skip = (_pad_lanes(w[k]) for k in ("ssm_dt_bias", "ssm_a_log", "ssm_d_skip"))
    sinks = _pad_lanes(w["attn_sinks"])
    x_in = [(x, D_MODEL, 0, "prev"), (head, D_MODEL, 0, "first")]

    def h0_tile(r0, xt, hd):
        return jnp.where(r0 < T, hd, xt)

    n1, = _rowwise("norm_pre_mix", lambda r0, xt, hd, wn: [_rms(h0_tile(r0, xt, hd), wn)], n_rows, T,
                   x_in, [w["norm_pre_mix"]], [(D_MODEL, BF16)], [])
    p = _matmul("in_proj", n1, w["w_cat"], "nn", F32)
    y_ssd, hin = run("ssd_fwd", _ssd_fwd, p, w["ssm_conv_w"], w["ssm_conv_b"], dt_bias, a_log, d_skip, n_chunks)
    ao, lse = run("attn_fwd", _attn_fwd, p, sinks, n_chunks)

    def gate_norm(r0, y, z, wn):
        return [_rms(y * _silu(z), wn)]

    yn, = run("ssm_gate_norm", _rowwise, "ssm_gate_norm", gate_norm, n_rows, tm,
              [(y_ssd, D_INNER, 0), (p, D_INNER, P_Z // D_INNER)], [w["ssm_norm"]], [(D_INNER, BF16)], [])
    y_ssm = _matmul("ssm_out", yn, w["w_ssm_out"], "nn", F32)
    y_attn = _matmul("attn_out", ao, w["w_attn_out"], "nn", F32)

    def mix_gate(r0, ys, ya, gs, ga):
        return [_sigmoid(gs) * ys + _sigmoid(ga) * ya]

    gate_ins = [(p, D_MODEL, P_GATE // D_MODEL), (p, D_MODEL, P_GATE // D_MODEL + 1)]
    mixed, = _rowwise("mix_gate", mix_gate, n_rows, tm, [(y_ssm, D_MODEL, 0), (y_attn, D_MODEL, 0)] + gate_ins,
                      [], [(D_MODEL, BF16)], [])
    mix = _matmul("mix_out", mixed, w["w_mix_out"], "nn", F32)

    def post_mix(r0, mx, xt, hd, w_post, w_pre):
        h1 = jnp.where(_valid_rows(r0, mx.shape[0], PAD), h0_tile(r0, xt, hd) + _rms(mx, w_post), 0.0)
        return [h1, _rms(h1, w_pre)]

    h1, n2 = _rowwise("post_mix", post_mix, n_rows, T, [(mix, D_MODEL, 0)] + x_in,
                      [w["norm_post_mix"], w["norm_pre_ffn"]], [(D_MODEL, F32), (D_MODEL, BF16)], [])
    u_raw = _matmul("ffn_up", n2, w["w_ffn_up"], "nn", F32)
    f = _ffn_act("ffn_act", u_raw, w["ffn_conv_w"], w["ffn_conv_b"], n_rows)
    ffn = _matmul("ffn_down", f, w["w_ffn_down"], "nn", F32)

    def final(r0, fo, h, t, w_post):
        real = r0 >= T
        err = jnp.where(real, h + _rms(fo, w_post) - t, 0.0)
        dy = err * (1.0 / D_MODEL)
        dffn, dw = _rms_bwd(dy, fo, w_post)
        return [dffn, dy, jnp.sum(err * err, axis=0, keepdims=True), dw]

    dffn, dh2, loss_cols, g_norm_post_ffn = _rowwise(
        "loss_head", final, n_rows, T, [(ffn, D_MODEL, 0), (h1, D_MODEL, 0), (tgt, D_MODEL, 0, "prev")],
        [w["norm_post_ffn"]], [(D_MODEL, BF16), (D_MODEL, F32)], [D_MODEL, D_MODEL])

    g["norm_post_ffn"] = g_norm_post_ffn
    g["w_ffn_down"] = _matmul("ffn_down_dw", f, dffn, "tn", F32)
    df = _matmul("ffn_down_dx", dffn, w["w_ffn_down"], "nt", F32)
    du_raw, dconv = _conv_bwd("ffn_act_bwd", u_raw, 0, [df], [(0, c0) for c0 in range(0, FFN_DIM, CONV_LANES)],
                              w["ffn_conv_w"], w["ffn_conv_b"], n_rows, True)
    g["ffn_conv_w"], g["ffn_conv_b"] = dconv[0:3], dconv[3:4]
    g["w_ffn_up"] = _matmul("ffn_up_dw", n2, du_raw, "tn", F32)
    dn2 = run("ffn_up_dx", _matmul, "ffn_up_dx", du_raw, w["w_ffn_up"], "nt", F32)

    def post_mix_bwd(r0, dn, d2, h, mx, w_pre, w_post):
        dx, dw_pre = _rms_bwd(dn, h, w_pre)
        dh1 = jnp.where(_valid_rows(r0, dn.shape[0], PAD), dx + d2, 0.0)
        dmix, dw_post = _rms_bwd(dh1, mx, w_post)
        return [dh1, dmix, dw_pre, dw_post]

    dh1, dmix, g["norm_pre_ffn"], g["norm_post_mix"] = _rowwise(
        "post_mix_bwd", post_mix_bwd, n_rows, tm,
        [(dn2, D_MODEL, 0), (dh2, D_MODEL, 0), (h1, D_MODEL, 0), (mix, D_MODEL, 0)],
        [w["norm_pre_ffn"], w["norm_post_mix"]], [(D_MODEL, F32), (D_MODEL, BF16)], [D_MODEL, D_MODEL])
    g["w_mix_out"] = _matmul("mix_out_dw", mixed, dmix, "tn", F32)
    dmixed = _matmul("mix_out_dx", dmix, w["w_mix_out"], "nt", F32)

    def mix_gate_bwd(r0, dm, ys, ya, gs, ga):
        ss, sa = _sigmoid(gs), _sigmoid(ga)
        dgate = jnp.concatenate([dm * ys * ss * (1.0 - ss), dm * ya * sa * (1.0 - sa)], axis=1)
        return [dm * ss, dm * sa, dgate]

    dys, dya, dp = _rowwise(
        "mix_gate_bwd", mix_gate_bwd, n_rows, tm,
        [(dmixed, D_MODEL, 0), (y_ssm, D_MODEL, 0), (y_attn, D_MODEL, 0)] + gate_ins,
        [], [(D_MODEL, BF16), (D_MODEL, BF16), (2 * D_MODEL, BF16, "new", P_W, P_GATE // (2 * D_MODEL))], [])
    g["w_ssm_out"] = _matmul("ssm_out_dw", yn, dys, "tn", F32)
    dyn = _matmul("ssm_out_dx", dys, w["w_ssm_out"], "nt", F32)
    g["w_attn_out"] = _matmul("attn_out_dw", ao, dya, "tn", F32)
    dao = _matmul("attn_out_dx", dya, w["w_attn_out"], "nt", BF16)

    def gate_norm_bwd(r0, dn, y, z, wn):
        sz, dsz = _silu_grad(z)
        dyz, dw = _rms_bwd(dn, y * sz, wn)
        live = _valid_rows(r0, dn.shape[0], PAD)
        return [jnp.where(live, dyz * sz, 0.0), jnp.where(live, dyz * y * dsz, 0.0), dw]

    dy_ssd, dp, g["ssm_norm"] = run(
        "ssm_gate_norm_bwd", _rowwise, "ssm_gate_norm_bwd", gate_norm_bwd, n_rows, tm,
        [(dyn, D_INNER, 0), (y_ssd, D_INNER, 0), (p, D_INNER, P_Z // D_INNER)],
        [w["ssm_norm"]], [(D_INNER, F32), (D_INNER, BF16, "into", dp, P_Z // D_INNER)], [D_INNER])
    dp, dsink = run("attn_bwd", _attn_bwd, p, sinks, ao, lse, dao, dp, n_chunks)
    g["attn_sinks"] = dsink[:, 0:ATTN_HEADS]
    dxs, dbm, dcm, dp, dpar = run("ssd_bwd", _ssd_bwd, p, w["ssm_conv_w"], w["ssm_conv_b"], dt_bias, a_log,
                                  d_skip, hin, dy_ssd, dp, n_chunks)
    dpar = jnp.sum(dpar, axis=0)
    g["ssm_dt_bias"], g["ssm_a_log"], g["ssm_d_skip"] = (dpar[i:i + 1, 0:SSM_HEADS] for i in range(3))
    x_chunks = [(src, c0) for src, arr in enumerate((dxs, dbm, dcm)) for c0 in range(0, arr.shape[1], CONV_LANES)]
    dp, dconv = run("ssm_conv_bwd", _conv_bwd, "ssm_conv_bwd", p, P_XBC // CONV_DIM, [dxs, dbm, dcm], x_chunks,
                    w["ssm_conv_w"], w["ssm_conv_b"], n_rows, False, into=dp, into_blk=P_XBC // CONV_DIM)
    g["ssm_conv_w"], g["ssm_conv_b"] = dconv[0:4], dconv[4:5]
    g["w_cat"] = _matmul("in_proj_dw", n1, dp, "tn", F32)
    dn1 = run("in_proj_dx", _matmul, "in_proj_dx", dp, w["w_cat"], "nt", F32)

    def pre_mix_bwd(r0, dn, d1, xt, hd, wn):
        dx, dw = _rms_bwd(dn, h0_tile(r0, xt, hd), wn)
        dh0 = jnp.where(_valid_rows(r0, dn.shape[0], PAD), dx + d1, 0.0)
        return [dh0, dh0, dw]

    dx_out, dhead, g["norm_pre_mix"] = _rowwise(
        "pre_mix_bwd", pre_mix_bwd, n_rows, T, [(dn1, D_MODEL, 0), (dh1, D_MODEL, 0)] + x_in,
        [w["norm_pre_mix"]], [(D_MODEL, F32, "prev", n_tok), (D_MODEL, F32, "first")], [D_MODEL])
    return jnp.sum(loss_cols), dx_out, dhead


_IN_SECTIONS = [((5152, 6176), P_Q), ((6176, 6432), P_K), ((6432, 6688), P_V), ((5120, 5152), P_DT),
                ((0, 2048), P_Z), ((6688, 8736), P_GATE), ((2048, 5120), P_XBC)]


IN_SHARD = N_IN // 4


def _shard_pieces(a, b):
    return [(j, max(a, j * IN_SHARD) - j * IN_SHARD, min(b, (j + 1) * IN_SHARD) - j * IN_SHARD)
            for j in range(4) if max(a, j * IN_SHARD) < min(b, (j + 1) * IN_SHARD)]


def _to_cat(w4):
    parts, at = [], 0
    for (a, b), off in _IN_SECTIONS:
        if off > at:
            parts.append(jnp.zeros((w4.shape[1], off - at), w4.dtype))
        parts += [w4[j, :, lo:hi] for j, lo, hi in _shard_pieces(a, b)]
        at = off + (b - a)
    return jnp.concatenate(parts, axis=1)


def _from_cat(g_cat):
    shards = [[] for _ in range(4)]
    for (a, b), off in sorted(_IN_SECTIONS):
        for j, lo, hi in _shard_pieces(a, b):
            start = off + j * IN_SHARD + lo - a
            shards[j].append(g_cat[:, start:start + hi - lo])
    return jnp.stack([jnp.concatenate(s, axis=1) for s in shards])


LANES = 1024
_BIG = [("w_in", 1024, 2184, "chip"), ("w_ssm_out", 512, 1024, "row"), ("w_attn_out", 256, 1024, "row"),
        ("w_mix_out", 256, 1024, "row"), ("w_ffn_up", 1024, 1408, "col"), ("w_ffn_down", 704, 1024, "row"),
        ("small", 32, LANES, "chip")]
_SMALL_SHARDED = [("ssm_conv_w", (4, 768), 1), ("ffn_conv_w", (3, 1408), 1), ("meta_tokens", (16, 256), 1)]
_REPLICATED = [("norm_pre_mix", 1024), ("ssm_conv_b", 3072), ("ssm_dt_bias", 32), ("ssm_a_log", 32),
               ("ssm_d_skip", 32), ("ssm_norm", 2048), ("attn_sinks", 16), ("norm_post_mix", 1024),
               ("norm_pre_ffn", 1024), ("ffn_conv_b", 5632), ("norm_post_ffn", 1024)]
SMALL_ROWS = 24


def _rep_rows():
    out, at = [], 0
    for _, width in _REPLICATED:
        out.append((at, -(-width // LANES)))
        at += out[-1][1]
    return out, at


def _in_rows(parts):
    rows = [jnp.pad(a, ((0, 0), (0, -a.shape[1] % LANES))).reshape(-1, LANES) for a in parts]
    flat = jnp.concatenate(rows, axis=0)
    return jnp.pad(flat, ((0, SMALL_ROWS - flat.shape[0]), (0, 0)))
WEIGHT_ORDER = ["meta_tokens", "norm_pre_mix", "w_in", "ssm_conv_w", "ssm_conv_b", "ssm_dt_bias", "ssm_a_log",
                "ssm_d_skip", "ssm_norm", "w_ssm_out", "attn_sinks", "w_attn_out", "w_mix_out", "norm_post_mix",
                "norm_pre_ffn", "w_ffn_up", "ffn_conv_w", "ffn_conv_b", "w_ffn_down", "norm_post_ffn"]


def _flatten(parts, rows):
    flat = jnp.concatenate([a.reshape(-1) for a in parts])
    return jnp.pad(flat, (0, rows * LANES - flat.shape[0])).reshape(rows, LANES)


def _unflatten(flat, shapes):
    flat = flat.reshape(-1)
    out, off = [], 0
    for shp in shapes:
        n = math.prod(shp)
        out.append(flat[off:off + n].reshape(shp))
        off += n
    return out


def _shard_of(full, chip, shape, axis):
    return lax.slice_in_dim(full, chip * shape[axis], (chip + 1) * shape[axis], axis=axis)


def _full_shape(r, c, layout):
    return {"row": (4 * r, c), "col": (r, 4 * c), "chip": (4, r, c)}[layout]


def _shard_view(ref, r, c, layout, chip):
    if layout == "row":
        return ref.at[pl.ds(pl.multiple_of(chip * r, 16), r), :]
    if layout == "col":
        return ref.at[:, pl.ds(pl.multiple_of(chip * c, 128), c)]
    return ref.at[chip]


def _half_view(ref, r, c, layout, chip, half):
    hr = r // 2
    if layout == "row":
        return ref.at[pl.ds(pl.multiple_of(chip * r + half * hr, 16), hr), :]
    r0 = pl.multiple_of(half * hr, 16)
    if layout == "col":
        return ref.at[pl.ds(r0, hr), pl.ds(pl.multiple_of(chip * c, 128), c)]
    return ref.at[chip, pl.ds(r0, hr), :]


def _mesh_pos():
    return lax.axis_index("x"), lax.axis_index("y"), lax.axis_index("c")


def _other_chips(x, y):
    return [(1 - x, y), (x, 1 - y), (1 - x, 1 - y)]


def _chip_index(x, y):
    return 2 * x + y


def _run_exchange(name, ex):
    n_in, n_out = len(ex.ins), len(ex.out_shapes)

    def body(*refs):
        in_refs, out_refs = refs[:n_in], refs[n_in:n_in + n_out]
        send_sems, recv_sems = refs[n_in + n_out:]
        copies = [pltpu.make_async_remote_copy(src_ref=s, dst_ref=d, send_sem=send_sems.at[i], recv_sem=recv_sems.at[i],
                                               device_id=dev, device_id_type=MESH)
                  for i, (s, d, dev) in enumerate(ex.make_copies(in_refs, out_refs))]
        assert len(copies) == ex.n_copies
        for cp in copies:
            cp.start()
        for cp in copies:
            cp.wait()

    return pl.pallas_call(
        body, name=name, in_specs=[ANY] * n_in, out_specs=[ANY] * n_out, out_shape=list(ex.out_shapes),
        scratch_shapes=[pltpu.SemaphoreType.DMA((ex.n_copies,)), pltpu.SemaphoreType.DMA((ex.n_copies,))],
        compiler_params=pltpu.CompilerParams(has_side_effects=True),
    )(*ex.ins)


def _join(*exs):
    def make(in_refs, out_refs):
        copies, i0, o0 = [], 0, 0
        for ex in exs:
            copies += ex.make_copies(in_refs[i0:i0 + len(ex.ins)], out_refs[o0:o0 + len(ex.out_shapes)])
            i0, o0 = i0 + len(ex.ins), o0 + len(ex.out_shapes)
        return copies

    aliases, i0, o0 = {}, 0, 0
    for ex in exs:
        aliases.update({i0 + k: o0 + v for k, v in ex.aliases.items()})
        i0, o0 = i0 + len(ex.ins), o0 + len(ex.out_shapes)
    return _Exchange([a for ex in exs for a in ex.ins], [s for ex in exs for s in ex.out_shapes], make,
                     sum(ex.n_copies for ex in exs), aliases)


def _split(exs, results):
    out, o0 = [], 0
    for ex in exs:
        out.append(list(results[o0:o0 + len(ex.out_shapes)]))
        o0 += len(ex.out_shapes)
    return out


def _gather_ici(entries, shards):
    def make(in_refs, out_refs):
        x, y, c = _mesh_pos()
        j = _chip_index(x, y)
        copies = []
        for ref_in, ref_out, (_, r, cc, lay) in zip(in_refs, out_refs, entries):
            copies.append((ref_in, _shard_view(ref_out, r, cc, lay, j), None))
            mine = ref_in.at[pl.ds(pl.multiple_of(c * (r // 2), 16), r // 2), :]
            copies += [(mine, _half_view(ref_out, r, cc, lay, j, c), (*ch, c)) for ch in _other_chips(x, y)]
        return copies

    shapes = [jax.ShapeDtypeStruct(_full_shape(r, cc, lay), s.dtype) for s, (_, r, cc, lay) in zip(shards, entries)]
    return _Exchange(list(shards), shapes, make, 4 * len(entries))


def _gather_pass_on(entries, fulls):
    def make(in_refs, out_refs):
        x, y, c = _mesh_pos()
        copies = []
        for ref, (_, r, cc, lay) in zip(out_refs, entries):
            for ch in _other_chips(x, y):
                landed = _half_view(ref, r, cc, lay, _chip_index(*ch), c)
                copies.append((landed, landed, (x, y, 1 - c)))
        return copies

    return _Exchange(list(fulls), [jax.ShapeDtypeStruct(f.shape, f.dtype) for f in fulls], make, 3 * len(entries),
                     {a: a for a in range(len(entries))})


def _gather_weights(entries, shards):
    n = len(entries)

    def body(*refs):
        ins, outs = refs[:n], refs[n:2 * n]
        send_sems, recv_sems, local_sems = refs[2 * n:]
        x, y, c = _mesh_pos()
        j = _chip_index(x, y)
        sibling = (x, y, 1 - c)
        chips = _other_chips(x, y)
        idx = [_chip_index(*ch) for ch in chips]

        def remote(k, src, dst, dev):
            return pltpu.make_async_remote_copy(src_ref=src, dst_ref=dst, send_sem=send_sems.at[k],
                                                recv_sem=recv_sems.at[k], device_id=dev, device_id_type=MESH)

        own = [pltpu.make_async_copy(ins[a], _shard_view(outs[a], r, cc, lay, j), local_sems.at[a])
               for a, (_, r, cc, lay) in enumerate(entries)]
        for cp in own:
            cp.start()
        first, passed = [], []
        for a, (_, r, cc, lay) in enumerate(entries):
            mine = ins[a].at[pl.ds(pl.multiple_of(c * (r // 2), 16), r // 2), :]
            for k, ch in enumerate(chips):
                first.append(remote(6 * a + k, mine, _half_view(outs[a], r, cc, lay, j, c), (*ch, c)))
                landed = _half_view(outs[a], r, cc, lay, idx[k], c)
                passed.append(remote(6 * a + 3 + k, landed, landed, sibling))
        for cp in first:
            cp.start()
        for a, (_, r, cc, lay) in enumerate(entries):
            for k in range(3):
                landed = _half_view(outs[a], r, cc, lay, idx[k], c)
                remote(6 * a + k, landed, landed, sibling).wait_recv()
                passed[3 * a + k].start()
        for a, (_, r, cc, lay) in enumerate(entries):
            for k in range(3):
                theirs = _half_view(outs[a], r, cc, lay, idx[k], 1 - c)
                remote(6 * a + 3 + k, theirs, theirs, sibling).wait_recv()
        for cp in first + passed:
            cp.wait_send()
        for cp in own:
            cp.wait()

    return pl.pallas_call(
        body, name="gather_weights", in_specs=[ANY] * n, out_specs=[ANY] * n,
        out_shape=[jax.ShapeDtypeStruct(_full_shape(r, cc, lay), s.dtype) for s, (_, r, cc, lay) in zip(shards, entries)],
        scratch_shapes=[pltpu.SemaphoreType.DMA((6 * n,)), pltpu.SemaphoreType.DMA((6 * n,)), pltpu.SemaphoreType.DMA((n,))],
        compiler_params=pltpu.CompilerParams(has_side_effects=True),
    )(*shards)


def _pair_exchange(entries, grads):
    def make(in_refs, out_refs):
        x, y, c = _mesh_pos()
        return [(_half_view(ref_in, r, cc, lay, i, 1 - c), ref_out.at[i], (x, y, 1 - c))
                for ref_in, ref_out, (_, r, cc, lay) in zip(in_refs, out_refs, entries) for i in range(4)]

    return _Exchange(list(grads), [jax.ShapeDtypeStruct((4, r // 2, cc), F32) for _, r, cc, _ in entries], make,
                     4 * len(entries))


def _whole_to_sibling(arrays):
    def make(in_refs, out_refs):
        x, y, c = _mesh_pos()
        return [(r, o, (x, y, 1 - c)) for r, o in zip(in_refs, out_refs)]

    return _Exchange(list(arrays), [jax.ShapeDtypeStruct(a.shape, a.dtype) for a in arrays], make, len(arrays))


def _chip_exchange(psends):
    def make(in_refs, out_refs):
        x, y, c = _mesh_pos()
        return [(ref_in.at[_chip_index(*ch)], ref_out.at[k], (*ch, c))
                for ref_in, ref_out in zip(in_refs, out_refs) for k, ch in enumerate(_other_chips(x, y))]

    return _Exchange(list(psends), [jax.ShapeDtypeStruct((3,) + p.shape[1:], p.dtype) for p in psends], make,
                     3 * len(psends))


def _to_all_chips(array):
    def make(in_refs, out_refs):
        x, y, c = _mesh_pos()
        return [(in_refs[0], out_refs[0].at[k], (*ch, c)) for k, ch in enumerate(_other_chips(x, y))]

    return _Exchange([array], [jax.ShapeDtypeStruct((3,) + array.shape, array.dtype)], make, 3)


SUM_ROWS = 256
ADAM_ROWS = 128


def _pair_sum(name, grad, recv, ids, r, c, layout):
    hr = r // 2
    tr = _row_tile(hr, SUM_ROWS)
    nb = hr // tr

    def body(ids_ref, g_ref, r_ref, send_ref, own_ref):
        s = g_ref[...] + r_ref[...]
        send_ref[...] = s.astype(send_ref.dtype)

        @pl.when(pl.program_id(1) == ids_ref[1])
        def _():
            own_ref[...] = s

    if layout == "row":
        g_spec = pl.BlockSpec((tr, c), lambda t, j, ids_ref: ((j * r + ids_ref[0] * hr) // tr + t, 0))
    elif layout == "col":
        g_spec = pl.BlockSpec((tr, c), lambda t, j, ids_ref: (ids_ref[0] * nb + t, j))
    else:
        g_spec = pl.BlockSpec((None, tr, c), lambda t, j, ids_ref: (j, ids_ref[0] * nb + t, 0))
    grid_spec = pltpu.PrefetchScalarGridSpec(
        num_scalar_prefetch=1, grid=(nb, 4),
        in_specs=[g_spec, pl.BlockSpec((None, tr, c), lambda t, j, ids_ref: (j, t, 0))],
        out_specs=[pl.BlockSpec((None, tr, c), lambda t, j, ids_ref: (j, t, 0)),
                   pl.BlockSpec((tr, c), lambda t, j, ids_ref: (t, 0))])
    return pl.pallas_call(
        body, name=name, grid_spec=grid_spec,
        out_shape=[jax.ShapeDtypeStruct((4, hr, c), BF16), jax.ShapeDtypeStruct((hr, c), F32)],
        compiler_params=_cparams(2),
    )(ids, grad, recv)


def _chip_sum(name, own, recv):
    hr, c = own.shape
    tr = _row_tile(hr, SUM_ROWS)

    def body(o_ref, r_ref, out_ref):
        out_ref[...] = ((o_ref[...] + r_ref[0].astype(F32)) + r_ref[1].astype(F32)) + r_ref[2].astype(F32)

    return pl.pallas_call(
        body, name=name, grid=(hr // tr,),
        in_specs=[pl.BlockSpec((tr, c), lambda i: (i, 0)), pl.BlockSpec((3, tr, c), lambda i: (0, i, 0))],
        out_specs=pl.BlockSpec((tr, c), lambda i: (i, 0)),
        out_shape=jax.ShapeDtypeStruct((hr, c), F32), compiler_params=_cparams(1),
    )(own, recv)


def _chip_sum_small(own, recv, ids):
    def body(ids_ref, o_ref, r_ref, out_ref):
        j = ids_ref[1]
        total = None
        for i in range(4):
            m = jnp.bitwise_xor(i, j)
            term = jnp.where(m == 0, o_ref[...], jnp.where(m == 2, r_ref[0], jnp.where(m == 1, r_ref[1], r_ref[2])))
            total = term if total is None else total + term
        out_ref[...] = total

    grid_spec = pltpu.PrefetchScalarGridSpec(
        num_scalar_prefetch=1, grid=(1,),
        in_specs=[pl.BlockSpec(own.shape, lambda i, ids_ref: (0, 0)), pl.BlockSpec(recv.shape, lambda i, ids_ref: (0, 0, 0))],
        out_specs=pl.BlockSpec(own.shape, lambda i, ids_ref: (0, 0)))
    return pl.pallas_call(body, name="chip_sum_small", grid_spec=grid_spec,
                          out_shape=jax.ShapeDtypeStruct(own.shape, F32), compiler_params=_cparams(1))(ids, own, recv)


def _adamw(name, w, m, v, mine, theirs, ids):
    lead = (None,) * (w.ndim - 2)
    rows, cols = w.shape[-2:]
    half = rows // 2
    tr = _row_tile(half, ADAM_ROWS, unit=8)
    nb = half // tr
    c1 = 1.0 / (1.0 - ADAM_B1 ** ADAM_STEP)
    c2 = 1.0 / (1.0 - ADAM_B2 ** ADAM_STEP)

    def body(ids_ref, w_ref, m_ref, v_ref, mine_ref, theirs_ref, g_out, d_out, m_out, v_out):
        g = jnp.where(pl.program_id(0) == ids_ref[0], mine_ref[...], theirs_ref[...])
        m_new = ADAM_B1 * m_ref[...] + (1.0 - ADAM_B1) * g
        v_new = ADAM_B2 * v_ref[...] + (1.0 - ADAM_B2) * (g * g)
        d_out[...] = -ADAM_LR * ((m_new * c1) / (jnp.sqrt(v_new * c2) + ADAM_EPS) + ADAM_WD * w_ref[...])
        g_out[...] = g
        m_out[...] = m_new
        v_out[...] = v_new

    full = pl.BlockSpec(lead + (tr, cols), lambda h, i, ids_ref: (0,) * len(lead) + (h * nb + i, 0))
    part = pl.BlockSpec((tr, cols), lambda h, i, ids_ref: (i, 0))
    grid_spec = pltpu.PrefetchScalarGridSpec(num_scalar_prefetch=1, grid=(2, nb),
                                             in_specs=[full, full, full, part, part], out_specs=[full] * 4)
    return pl.pallas_call(
        body, name=name, grid_spec=grid_spec,
        out_shape=[jax.ShapeDtypeStruct(w.shape, F32)] * 4, compiler_params=_cparams(2),
    )(ids, w, m, v, mine, theirs)


def _adamw_replicated(g_rows, ws, ms, vs):
    n = len(ws)
    layout, _ = _rep_rows()
    c1 = 1.0 / (1.0 - ADAM_B1 ** ADAM_STEP)
    c2 = 1.0 / (1.0 - ADAM_B2 ** ADAM_STEP)

    def body(g_ref, *refs):
        w_refs, m_refs, v_refs = refs[0:n], refs[n:2 * n], refs[2 * n:3 * n]
        outs = refs[3 * n:]
        for k, (r0, rows) in enumerate(layout):
            width = w_refs[k].shape[1]
            g = jnp.concatenate([g_ref[r0 + j:r0 + j + 1, :] for j in range(rows)], axis=1)[:, 0:width]
            m_new = ADAM_B1 * m_refs[k][...] + (1.0 - ADAM_B1) * g
            v_new = ADAM_B2 * v_refs[k][...] + (1.0 - ADAM_B2) * (g * g)
            outs[k][...] = g
            outs[n + k][...] = -ADAM_LR * ((m_new * c1) / (jnp.sqrt(v_new * c2) + ADAM_EPS) + ADAM_WD * w_refs[k][...])
            outs[2 * n + k][...] = m_new
            outs[3 * n + k][...] = v_new

    res = pl.pallas_call(body, name="adamw_replicated",
                         out_shape=[jax.ShapeDtypeStruct(w.shape, F32) for _ in range(4) for w in ws])(g_rows, *ws, *ms, *vs)
    return [res[k * n:(k + 1) * n] for k in range(4)]


def _small_shard(parts):
    return _flatten(parts, _BIG[-1][1])


_ENTRY = {e[0]: e for e in _BIG}
FFN_MATS = ("w_ffn_down", "w_ffn_up")
MIXER_MATS = ("w_mix_out", "w_ssm_out", "w_attn_out")


class _StepPlan:
    def __init__(self, w, late_shards, shards, ids):
        self.w, self.g = w, {}
        self.late_shards, self.shards, self.ids = late_shards, shards, ids
        self.sums, self.halves, self.results = {}, {}, {}

    def run(self, name, fn, *args, **kw):
        at = getattr(self, "_at_" + name, None)
        if at is None:
            return fn(*args, **kw)
        exchange, landed = at()
        res, extra = fn(*args, bg=exchange, **kw)
        landed(extra)
        return res

    def _at_ssd_fwd(self):
        def landed(fulls):
            self.partly_gathered = fulls

        return _gather_ici([_ENTRY[n] for n in MIXER_MATS], [self.late_shards[n] for n in MIXER_MATS]), landed

    def _at_attn_fwd(self):
        stages = (_gather_pass_on([_ENTRY[n] for n in MIXER_MATS], self.partly_gathered),
                  _gather_ici([_ENTRY[n] for n in FFN_MATS], [self.late_shards[n] for n in FFN_MATS]))

        def landed(extra):
            mixer, self.partly_gathered = _split(stages, extra)
            self.w.update(zip(MIXER_MATS, mixer))

        return _join(*stages), landed

    def _at_ssm_gate_norm(self):
        return (_gather_pass_on([_ENTRY[n] for n in FFN_MATS], self.partly_gathered),
                lambda fulls: self.w.update(zip(FFN_MATS, fulls)))

    def pair_sums(self, names, grads, recv):
        for n, gr, rv in zip(names, grads, recv):
            _, r, c, lay = _ENTRY[n]
            self.sums[n] = _pair_sum("pair_sum_" + n, gr, rv, self.ids, r, c, lay)

    def chip_sums(self, names, recv):
        for n, rv in zip(names, recv):
            self.halves[n] = _chip_sum("chip_sum_" + n, self.sums[n][1], rv)

    def adamw(self, names, theirs):
        for n, th in zip(names, theirs):
            sh = self.shards[n]
            self.results[n] = _adamw("adamw_" + n, sh["w"], sh["m"], sh["v"], self.halves[n], th, self.ids)

    def _pair_stage(self, names, grads):
        return (_pair_exchange([_ENTRY[n] for n in names], grads),
                lambda recv: self.pair_sums(names, grads, recv))

    def _at_ffn_up_dx(self):
        return self._pair_stage(FFN_MATS, [self.g[n] for n in FFN_MATS])

    def _at_ssm_gate_norm_bwd(self):
        return self._pair_stage(MIXER_MATS, [self.g[n] for n in MIXER_MATS])

    def _at_attn_bwd(self):
        return _chip_exchange([self.sums[n][0] for n in FFN_MATS]), lambda recv: self.chip_sums(FFN_MATS, recv)

    def _at_ssd_bwd(self):
        stages = (_chip_exchange([self.sums[n][0] for n in MIXER_MATS]),
                  _whole_to_sibling([self.halves[n] for n in FFN_MATS]))

        def landed(extra):
            recv, theirs = _split(stages, extra)
            self.chip_sums(MIXER_MATS, recv)
            self.adamw(FFN_MATS, theirs)

        return _join(*stages), landed

    def _at_ssm_conv_bwd(self):
        return _whole_to_sibling([self.halves[n] for n in MIXER_MATS]), lambda theirs: self.adamw(MIXER_MATS, theirs)

    def _at_in_proj_dx(self):
        grads = [_from_cat(self.g.pop("w_cat"))]
        self.pair_sums(("w_in",), grads, _run_exchange("grad_pair_exchange_w_in", _pair_exchange([_ENTRY["w_in"]], grads)))
        return _chip_exchange([self.sums["w_in"][0]]), lambda recv: self.chip_sums(("w_in",), recv)

    def finish(self, g_small, g_rep, rep_shards):
        stages = (_pair_exchange([_ENTRY["small"]], [g_small]), _whole_to_sibling([g_rep]))
        recv_small, recv_rep = _split(stages, _run_exchange("grad_pair_exchange_tail", _join(*stages)))
        self.pair_sums(("small",), [g_small], recv_small)
        p_rep, = _rowwise("pair_sum_replicated", lambda r0, a, b: [a + b], SMALL_ROWS, SMALL_ROWS,
                          [(g_rep, LANES, 0), (recv_rep[0], LANES, 0)], [], [(LANES, F32)], [])
        stages = (_chip_exchange([self.sums["small"][0]]), _to_all_chips(p_rep))
        recv, recv_rep = _split(stages, _run_exchange("grad_chip_exchange_tail", _join(*stages)))
        self.chip_sums(("small",), recv)
        g_rep_tot = _chip_sum_small(p_rep, recv_rep[0], self.ids)
        last = ("w_in", "small")
        self.adamw(last, _run_exchange("grad_half_share_tail", _whole_to_sibling([self.halves[n] for n in last])))
        self.results["replicated"] = _adamw_replicated(g_rep_tot, rep_shards["w"], rep_shards["m"], rep_shards["v"])
        return g_rep_tot[_rep_rows()[1], 0]


def kernel(x, meta_tokens, norm_pre_mix, w_in, ssm_conv_w, ssm_conv_b, ssm_dt_bias, ssm_a_log, ssm_d_skip, ssm_norm, w_ssm_out, attn_sinks, w_attn_out, w_mix_out, norm_post_mix, norm_pre_ffn, w_ffn_up, ffn_conv_w, ffn_conv_b, w_ffn_down, norm_post_ffn, loss_target, m_meta_tokens, m_norm_pre_mix, m_w_in, m_ssm_conv_w, m_ssm_conv_b, m_ssm_dt_bias, m_ssm_a_log, m_ssm_d_skip, m_ssm_norm, m_w_ssm_out, m_attn_sinks, m_w_attn_out, m_w_mix_out, m_norm_post_mix, m_norm_pre_ffn, m_w_ffn_up, m_ffn_conv_w, m_ffn_conv_b, m_w_ffn_down, m_norm_post_ffn, v_meta_tokens, v_norm_pre_mix, v_w_in, v_ssm_conv_w, v_ssm_conv_b, v_ssm_dt_bias, v_ssm_a_log, v_ssm_d_skip, v_ssm_norm, v_w_ssm_out, v_attn_sinks, v_w_attn_out, v_w_mix_out, v_norm_post_mix, v_norm_pre_ffn, v_w_ffn_up, v_ffn_conv_w, v_ffn_conv_b, v_w_ffn_down, v_norm_post_ffn):
    args = dict(locals())
    squeeze = lambda a: a.reshape(a.shape[-2:])
    wts = {n: squeeze(args[n]) for n in WEIGHT_ORDER}
    mom = {n: squeeze(args["m_" + n]) for n in WEIGHT_ORDER}
    var = {n: squeeze(args["v_" + n]) for n in WEIGHT_ORDER}
    x_i, y_i, c_i = _mesh_pos()
    ids = jnp.stack([c_i, _chip_index(x_i, y_i)]).astype(jnp.int32)
    big_names = [n for n, _, _, _ in _BIG[:-1]]
    small_names = [n for n, _, _ in _SMALL_SHARDED]
    rep_names = [n for n, _ in _REPLICATED]

    stacks = {"w": wts, "m": mom, "v": var}
    shards = {n: {"w": args[n], "m": args["m_" + n], "v": args["v_" + n]} for n in big_names}
    shards["small"] = {k: _small_shard([d[n] for n in small_names]) for k, d in stacks.items()}
    rep_shards = {k: [d[n] for n in rep_names] for k, d in stacks.items()}

    w_in4, small_all = _gather_weights([_ENTRY["w_in"], _ENTRY["small"]], [wts["w_in"].astype(BF16), shards["small"]["w"]])
    w = {n: wts[n] for n in rep_names}
    w["w_cat"] = _to_cat(w_in4)
    small_parts = [_unflatten(small_all[i], [shp for _, shp, _ in _SMALL_SHARDED]) for i in range(4)]
    for k, (n, _, axis) in enumerate(_SMALL_SHARDED):
        w[n] = jnp.concatenate([small_parts[i][k] for i in range(4)], axis=axis)
    plan = _StepPlan(w, {n: wts[n].astype(BF16) for n in MIXER_MATS + FFN_MATS}, shards, ids)

    head = jnp.concatenate([jnp.zeros((PAD, D_MODEL), F32), w["meta_tokens"]], axis=0)
    loss_sum, dx, dhead = _local_step(x[0], head, loss_target[0], plan)
    g = plan.g
    g["meta_tokens"] = dhead[PAD:]
    g_small = jnp.stack([_small_shard([_shard_of(g[n], i, shp, ax) for n, shp, ax in _SMALL_SHARDED]) for i in range(4)])
    loss_part = (loss_sum * (0.5 / D_MODEL)).reshape(1, 1)
    loss = plan.finish(g_small, _in_rows([g[n] for n in rep_names] + [loss_part]), rep_shards)

    results = {}
    for kind in range(4):
        results.update({(kind, n): plan.results[n][kind] for n in big_names})
        parts = _unflatten(plan.results["small"][kind], [shp for _, shp, _ in _SMALL_SHARDED])
        results.update({(kind, n): parts[k] for k, n in enumerate(small_names)})
        results.update({(kind, n): plan.results["replicated"][kind][k] for k, n in enumerate(rep_names)})
    outs = [results[kind, n].reshape(args[n].shape) for kind in range(4) for n in WEIGHT_ORDER]
    return (loss, dx[None], *outs)
```

```python
import math
from typing import Any, Callable, NamedTuple, Sequence

import jax
import jax.numpy as jnp
from jax import lax
from jax.experimental import pallas as pl
from jax.experimental.pallas import tpu as pltpu

F32 = jnp.float32
BF16 = jnp.bfloat16

D_MODEL = 1024
N_META = 16
T = 128
PAD = T - N_META
D_INNER = 2048
SSM_HEADS = 32
HEAD_P = 64
SSM_GROUPS = 4
GROUP_W = D_INNER // SSM_GROUPS
D_STATE = 128
CONV_DIM = D_INNER + 2 * SSM_GROUPS * D_STATE
ATTN_HEADS = 16
KV_HEADS = 4
ATTN_W = 1024
KV_W = 256
FFN_DIM = 2816
N_IN = 8736
EPS = 1e-6
NEG = -1e30
SCALE = 0.125

P_Q, P_K, P_V, P_DT, P_Z, P_GATE, P_XBC = 0, 1024, 1280, 1536, 2048, 4096, 6144
QKV_W = 1536
P_W = 9216

ADAM_LR, ADAM_B1, ADAM_B2, ADAM_EPS, ADAM_WD, ADAM_STEP = 0.001, 0.9, 0.999, 1e-08, 0.01, 10

VMEM_BUDGET = 40 * 1024 * 1024
VMEM_LIMIT = 56 * 1024 * 1024
MESH = pl.DeviceIdType.MESH
ANY = pl.BlockSpec(memory_space=pl.ANY)


def _cparams(n_axes, **kw):
    return pltpu.CompilerParams(dimension_semantics=("arbitrary",) * n_axes, vmem_limit_bytes=VMEM_LIMIT, **kw)


class _Exchange(NamedTuple):
    ins: Sequence[Any]
    out_shapes: Sequence[Any]
    make_copies: Callable
    n_copies: int
    aliases: dict = {}


def _call(body, name, grid, in_specs, out_specs, out_shape, operands, scratch_shapes=(), aliases=None, bg=None):
    aliases = dict(aliases or {})
    if bg is None:
        return pl.pallas_call(body, name=name, grid=grid, in_specs=in_specs, out_specs=out_specs, out_shape=out_shape,
                              scratch_shapes=list(scratch_shapes), input_output_aliases=aliases,
                              compiler_params=_cparams(len(grid)))(*operands)
    n_in, n_out, n_scr = len(in_specs), len(out_specs), len(scratch_shapes)
    nb_in, nb_out = len(bg.ins), len(bg.out_shapes)

    def hosted(*refs):
        ins, bg_ins = refs[:n_in], refs[n_in:n_in + nb_in]
        outs = refs[n_in + nb_in:n_in + nb_in + n_out]
        bg_outs = refs[n_in + nb_in + n_out:n_in + nb_in + n_out + nb_out]
        scratch = refs[n_in + nb_in + n_out + nb_out:n_in + nb_in + n_out + nb_out + n_scr]
        send_sems, recv_sems = refs[-2:]
        pids = [pl.program_id(a) for a in range(len(grid))]
        first, last = pids[0] == 0, pids[0] == grid[0] - 1
        for p, g in zip(pids[1:], grid[1:]):
            first, last = first & (p == 0), last & (p == g - 1)
        copies = []
        for k, (src, dst, peer) in enumerate(bg.make_copies(bg_ins, bg_outs)):
            if peer is None:
                copies.append(pltpu.make_async_copy(src, dst, send_sems.at[k]))
            else:
                copies.append(pltpu.make_async_remote_copy(src_ref=src, dst_ref=dst, send_sem=send_sems.at[k],
                                                           recv_sem=recv_sems.at[k], device_id=peer, device_id_type=MESH))
        assert len(copies) == bg.n_copies

        @pl.when(first)
        def _():
            for cp in copies:
                cp.start()

        body(*ins, *outs, *scratch)

        @pl.when(last)
        def _():
            for cp in copies:
                cp.wait()

    aliases = {(k if k < n_in else k + nb_in): v for k, v in aliases.items()}
    aliases.update({n_in + k: n_out + v for k, v in bg.aliases.items()})
    res = pl.pallas_call(
        hosted, name=name, grid=grid, in_specs=list(in_specs) + [ANY] * nb_in, out_specs=list(out_specs) + [ANY] * nb_out,
        out_shape=list(out_shape) + list(bg.out_shapes), input_output_aliases=aliases,
        scratch_shapes=list(scratch_shapes) + [pltpu.SemaphoreType.DMA((bg.n_copies,))] * 2,
        compiler_params=_cparams(len(grid), has_side_effects=True))(*operands, *bg.ins)
    return res[:n_out], res[n_out:]


def _sigmoid(x):
    return 1.0 / (1.0 + jnp.exp(-x))


def _silu(x):
    return x * _sigmoid(x)


def _silu_grad(x):
    s = _sigmoid(x)
    return x * s, s * (1.0 + x * (1.0 - s))


def _dsilu(x):
    return _silu_grad(x)[1]


def _softplus(x):
    e = jnp.exp(-jnp.abs(x))
    small = e * (1.0 - e * (0.5 - e * (1.0 / 3.0)))
    return jnp.maximum(x, 0.0) + jnp.where(e < 0.01, small, jnp.log(1.0 + e))


def _rms(x, w):
    r = lax.rsqrt(jnp.mean(x * x, axis=-1, keepdims=True) + EPS)
    return x * r * w


def _rms_bwd(dy, x, w):
    r = lax.rsqrt(jnp.mean(x * x, axis=-1, keepdims=True) + EPS)
    xh = x * r
    g = dy * w
    dx = r * (g - xh * jnp.mean(g * xh, axis=-1, keepdims=True))
    dw = jnp.sum(dy * xh, axis=0, keepdims=True)
    return dx, dw


def _dot(a, b):
    return jnp.dot(a, b, preferred_element_type=F32)


def _dot_nt(a, b):
    return lax.dot_general(a, b, (((1,), (1,)), ((), ())), preferred_element_type=F32)


def _dot_tn(a, b):
    return lax.dot_general(a, b, (((0,), (0,)), ((), ())), preferred_element_type=F32)


def _split3(x):
    hi = x.astype(BF16)
    r = x - hi.astype(F32)
    mid = r.astype(BF16)
    lo = (r - mid.astype(F32)).astype(BF16)
    return hi, mid, lo


def _xdot(x, e):
    hi, mid, lo = _split3(x)
    return _dot(hi, e) + _dot(mid, e) + _dot(lo, e)


def _xdot_l(e, x):
    hi, mid, lo = _split3(x)
    return _dot(e, hi) + _dot(e, mid) + _dot(e, lo)


def _iota(shape, dim):
    return lax.broadcasted_iota(jnp.int32, shape, dim)


def _divisors(n, unit):
    return [t for t in range(unit, n + 1, unit) if n % t == 0]


MIN_MATMUL_STEPS = 8


def _matmul_tiles(m, n, k, a_bytes, b_bytes, o_bytes, m_unit):
    best = None
    for tm in _divisors(m, m_unit):
        for tn in _divisors(n, 128):
            for tk in _divisors(k, 128):
                acc = 0 if tk == k else tm * tn * 4
                vm = 2 * (tm * tk * a_bytes + tk * tn * b_bytes + tm * tn * o_bytes) + acc
                if vm > VMEM_BUDGET:
                    continue
                steps = (m // tm) * (n // tn) * (k // tk)
                score = (tk == k, min(steps, MIN_MATMUL_STEPS), min(tm, 256), tm * tn * tk)
                if best is None or score > best[0]:
                    best = (score, (tm, tn, tk))
    return best[1]


def _matmul(name, a, b, mode, out_dtype, bg=None):
    if mode == "nn":
        (m, k), n = a.shape, b.shape[1]
    elif mode == "nt":
        (m, k), n = a.shape, b.shape[0]
    else:
        (k, m), n = a.shape, b.shape[1]
    ab, bb, ob = a.dtype.itemsize, b.dtype.itemsize, jnp.dtype(out_dtype).itemsize
    tm, tn, tk = _matmul_tiles(m, n, k, ab, bb, ob, 128 if mode == "tn" else 16)
    nk = k // tk
    dot = {"nn": _dot, "nt": _dot_nt, "tn": _dot_tn}[mode]

    def body(a_ref, b_ref, o_ref, *scratch):
        prod = dot(a_ref[...].astype(BF16), b_ref[...].astype(BF16))
        if nk == 1:
            o_ref[...] = prod.astype(o_ref.dtype)
        else:
            acc_ref, = scratch
            kk = pl.program_id(2)

            @pl.when(kk == 0)
            def _():
                acc_ref[...] = prod

            @pl.when(kk > 0)
            def _():
                acc_ref[...] += prod

            @pl.when(kk == nk - 1)
            def _():
                o_ref[...] = acc_ref[...].astype(o_ref.dtype)

    a_spec = pl.BlockSpec((tk, tm), lambda i, j, kk: (kk, i)) if mode == "tn" else pl.BlockSpec((tm, tk), lambda i, j, kk: (i, kk))
    b_spec = pl.BlockSpec((tn, tk), lambda i, j, kk: (j, kk)) if mode == "nt" else pl.BlockSpec((tk, tn), lambda i, j, kk: (kk, j))
    res = _call(body, name, (m // tm, n // tn, nk), [a_spec, b_spec], [pl.BlockSpec((tm, tn), lambda i, j, kk: (i, j))],
                [jax.ShapeDtypeStruct((m, n), out_dtype)], [a, b],
                scratch_shapes=[] if nk == 1 else [pltpu.VMEM((tm, tn), F32)], bg=bg)
    return res[0] if bg is None else (res[0][0], res[1])


def _row_tile(n_rows, cap, unit=16):
    return max(t for t in _divisors(n_rows, unit) if t <= cap)


ROW_SUB = 384
GROUP_UNROLL = 4


def _rowwise(name, fn, n_rows, tm, row_ins, full_ins, row_outs, acc_outs, bg=None):
    n_in = len(row_ins) + len(full_ins)
    n_ro = len(row_outs)
    into = [(k, o[3]) for k, o in enumerate(row_outs) if len(o) > 2 and o[2] == "into"]

    n_row_in = len(row_ins)
    sub = min(tm, ROW_SUB)

    def body(*refs):
        i = pl.program_id(0)
        outs = refs[n_in + len(into):]

        sums = tuple(jnp.zeros((1, w), F32) for w in acc_outs)
        for s in range(tm // sub):
            rows = pl.ds(s * sub, sub)
            vals = [r[rows, :] for r in refs[:n_row_in]] + [r[...] for r in refs[n_row_in:n_in]]
            res = fn(i * tm + s * sub, *vals)
            for o, r, v in zip(row_outs, outs[:n_ro], res[:n_ro]):
                if len(o) > 2 and o[2] == "first":
                    @pl.when(i == 0)
                    def _(r=r, v=v, rows=rows):
                        r[rows, :] = v.astype(r.dtype)
                else:
                    r[rows, :] = v.astype(r.dtype)
            sums = tuple(a + v for a, v in zip(sums, res[n_ro:]))

        @pl.when(i == 0)
        def _():
            for r, v in zip(outs[n_ro:], sums):
                r[...] = v

        @pl.when(i > 0)
        def _():
            for r, v in zip(outs[n_ro:], sums):
                r[...] += v

    def in_spec(entry):
        w, cb = entry[1], entry[2]
        if len(entry) > 3 and entry[3] == "prev":
            return pl.BlockSpec((tm, w), lambda i: (jnp.maximum(i - 1, 0), cb))
        if len(entry) > 3 and entry[3] == "first":
            return pl.BlockSpec((tm, w), lambda i: (0, cb))
        return pl.BlockSpec((tm, w), lambda i: (i, cb))

    def out_spec(o):
        if len(o) == 2:
            return pl.BlockSpec((tm, o[0]), lambda i: (i, 0)), jax.ShapeDtypeStruct((n_rows, o[0]), o[1])
        if o[2] == "new":
            return pl.BlockSpec((tm, o[0]), lambda i: (i, o[4])), jax.ShapeDtypeStruct((n_rows, o[3]), o[1])
        if o[2] == "into":
            return pl.BlockSpec((tm, o[0]), lambda i: (i, o[4])), jax.ShapeDtypeStruct(o[3].shape, o[3].dtype)
        if o[2] == "first":
            return pl.BlockSpec((tm, o[0]), lambda i: (0, 0)), jax.ShapeDtypeStruct((tm, o[0]), o[1])
        return pl.BlockSpec((tm, o[0]), lambda i: (jnp.maximum(i - 1, 0), 0)), jax.ShapeDtypeStruct((o[3], o[0]), o[1])

    in_specs = [in_spec(e) for e in row_ins]
    in_specs += [pl.BlockSpec(a.shape, lambda i: (0, 0)) for a in full_ins]
    in_specs += [pl.BlockSpec(memory_space=pl.ANY) for _ in into]
    specs_shapes = [out_spec(o) for o in row_outs]
    out_specs = [s for s, _ in specs_shapes] + [pl.BlockSpec((1, w), lambda i: (0, 0)) for w in acc_outs]
    out_shape = [s for _, s in specs_shapes] + [jax.ShapeDtypeStruct((1, w), F32) for w in acc_outs]
    return _call(body, name, (n_rows // tm,), in_specs, out_specs, out_shape,
                 [e[0] for e in row_ins] + list(full_ins) + [arr for _, arr in into],
                 aliases={n_in + a: k for a, (k, _) in enumerate(into)}, bg=bg)


def _valid_rows(first_row, tm, lo):
    return (first_row + _iota((tm, 1), 0)) >= lo


CONV_ROWS = 128
CONV_SUB = 16
CONV_LANES = 256


def _conv_specs(tm, width, blk, n_rows, after):
    specs = [pl.BlockSpec((tm, width), lambda i: (i, blk)),
             pl.BlockSpec((8, width), lambda i: (jnp.maximum(i * (tm // 8) - 1, 0), blk))]
    if after:
        specs.append(pl.BlockSpec((16, width), lambda i: (jnp.minimum((i + 1) * (tm // 16), n_rows // 16 - 1), blk)))
    return specs


def _conv_window(win, w_ref, b_ref, taps, c0, cw, n):
    acc = b_ref[:, c0:c0 + cw] + w_ref[taps - 1:taps, c0:c0 + cw] * win[8:8 + n]
    for k in range(taps - 1):
        acc = acc + w_ref[k:k + 1, c0:c0 + cw] * win[8 - (taps - 1) + k:8 - (taps - 1) + k + n]
    return acc


def _ffn_act(name, u_raw, conv_w, conv_b, n_rows):
    tm, sub, cw = CONV_ROWS, CONV_SUB, CONV_LANES
    taps, width = conv_w.shape
    half = width // 2

    def body(cur_ref, prev_ref, w_ref, b_ref, f_ref, ext_ref):
        i = pl.program_id(0)
        ext_ref[0:8, :] = jnp.where(i > 0, prev_ref[...], 0.0)
        ext_ref[8:8 + tm, :] = cur_ref[...]
        for q in range(half // cw):
            a0, g0 = q * cw, half + q * cw

            def group(s, carry):
                r = pl.multiple_of(s * sub, sub)
                a = _conv_window(ext_ref[pl.ds(r, sub + 8), a0:a0 + cw], w_ref, b_ref, taps, a0, cw, sub)
                g = _conv_window(ext_ref[pl.ds(r, sub + 8), g0:g0 + cw], w_ref, b_ref, taps, g0, cw, sub)
                f = jnp.where(_valid_rows(i * tm + r, sub, PAD), _silu(a) * g, 0.0)
                f_ref[pl.ds(r, sub), a0:a0 + cw] = f.astype(f_ref.dtype)
                return carry

            lax.fori_loop(0, tm // sub, group, 0, unroll=GROUP_UNROLL)

    return pl.pallas_call(
        body, name=name, grid=(n_rows // tm,),
        in_specs=_conv_specs(tm, width, 0, n_rows, False) + [pl.BlockSpec((taps, width), lambda i: (0, 0)),
                                                             pl.BlockSpec((1, width), lambda i: (0, 0))],
        out_specs=pl.BlockSpec((tm, half), lambda i: (i, 0)),
        out_shape=jax.ShapeDtypeStruct((n_rows, half), BF16),
        scratch_shapes=[pltpu.VMEM((tm + 8, width), F32)],
        compiler_params=_cparams(1),
    )(u_raw, u_raw, conv_w, conv_b)


def _conv_bwd(name, raw, raw_blk, dsrcs, chunk_src, conv_w, conv_b, n_rows, gated, into=None, into_blk=0, bg=None):
    taps, width = conv_w.shape
    half = width // 2 if gated else width
    tm, sub, cw = CONV_ROWS, CONV_SUB, CONV_LANES
    te = tm + 16
    nd = len(dsrcs)
    n_parts = 2 if gated else 1

    def body(*refs):
        cur_ref, prev_ref, next_ref = refs[0:3]
        dcur, dnext = refs[3:3 + nd], refs[3 + nd:3 + 2 * nd]
        w_ref, b_ref = refs[3 + 2 * nd:5 + 2 * nd]
        out_ref, acc_ref, ext_ref, du_ref = refs[-4:]
        i = pl.program_id(0)
        ext_ref[0:8, :] = jnp.where(i > 0, prev_ref[...], 0.0)
        ext_ref[8:8 + tm, :] = cur_ref[...]
        ext_ref[8 + tm:24 + tm, :] = next_ref[...]

        for q, (src, off) in enumerate(chunk_src):
            cols = [q * cw, half + q * cw][:n_parts]

            def conv_grad(r, d):
                pre = [_conv_window(ext_ref[pl.ds(r, sub + 8), c0:c0 + cw], w_ref, b_ref, taps, c0, cw, sub) for c0 in cols]
                row = i * tm + r + _iota((sub, 1), 0)
                live = (row >= PAD) & (row < n_rows)
                if gated:
                    act, dact = _silu_grad(pre[0])
                    dus = [d * pre[1] * dact, d * act]
                else:
                    dus = [d * _dsilu(pre[0])]
                for part, du in enumerate(dus):
                    du_ref[part, pl.ds(r, sub), :] = jnp.where(live, du, 0.0)

            def tile_rows(s, carry):
                r = pl.multiple_of(s * sub, sub)
                conv_grad(r, dcur[src][pl.ds(r, sub), off:off + cw].astype(F32))
                return carry

            lax.fori_loop(0, tm // sub, tile_rows, 0, unroll=GROUP_UNROLL)
            conv_grad(tm, dnext[src][:, off:off + cw].astype(F32))

            for part, c0 in enumerate(cols):
                taps_w = [w_ref[k:k + 1, c0:c0 + cw] for k in range(taps)]

                def back(s, sums):
                    new = list(sums)
                    for u in range(2):
                        r = pl.multiple_of((2 * s + u) * sub, sub)
                        win = du_ref[part, pl.ds(r, sub + 8), :]
                        raw_rows = ext_ref[pl.ds(8 + r, sub), c0:c0 + cw]
                        draw = jnp.zeros((sub, cw), F32)
                        for k in range(taps):
                            shifted = win[taps - 1 - k:taps - 1 - k + sub]
                            draw = draw + taps_w[k] * shifted
                            new[k] = new[k] + shifted * raw_rows
                        new[taps] = new[taps] + win[0:sub]
                        out_ref[pl.ds(r, sub), c0:c0 + cw] = jnp.where(_valid_rows(i * tm + r, sub, PAD), draw, 0.0).astype(out_ref.dtype)
                    return tuple(new)

                sums = lax.fori_loop(0, tm // (2 * sub), back, tuple(jnp.zeros((sub, cw), F32) for _ in range(taps + 1)))
                for k in range(taps + 1):
                    total = jnp.sum(sums[k], axis=0, keepdims=True)
                    acc_ref[k:k + 1, c0:c0 + cw] = jnp.where(i == 0, total, acc_ref[k:k + 1, c0:c0 + cw] + total)

    in_specs = _conv_specs(tm, width, raw_blk, n_rows, True)
    in_specs += [pl.BlockSpec((tm, d.shape[1]), lambda i: (i, 0)) for d in dsrcs]
    in_specs += [pl.BlockSpec((16, d.shape[1]), lambda i: (jnp.minimum((i + 1) * (tm // 16), n_rows // 16 - 1), 0)) for d in dsrcs]
    in_specs += [pl.BlockSpec((taps, width), lambda i: (0, 0)), pl.BlockSpec((1, width), lambda i: (0, 0))]
    operands = [raw, raw, raw] + list(dsrcs) + list(dsrcs) + [conv_w, conv_b]
    aliases = {}
    if into is None:
        out0 = jax.ShapeDtypeStruct((n_rows, width), BF16)
    else:
        in_specs.append(pl.BlockSpec(memory_space=pl.ANY))
        operands.append(into)
        aliases = {len(operands) - 1: 0}
        out0 = jax.ShapeDtypeStruct(into.shape, into.dtype)
    return _call(body, name, (n_rows // tm,), in_specs,
                 [pl.BlockSpec((tm, width), lambda i: (i, into_blk)), pl.BlockSpec((8, width), lambda i: (0, 0))],
                 [out0, jax.ShapeDtypeStruct((8, width), F32)], operands,
                 scratch_shapes=[pltpu.VMEM((tm + 24, width), F32), pltpu.VMEM((n_parts, te + 8, cw), F32)],
                 aliases=aliases, bg=bg)


def _ssd_specs(n_chunks, rev, per_step=1):
    cidx = (lambda c: n_chunks - 1 - c) if rev else (lambda c: c)
    xw, nw = per_step * GROUP_W, per_step * D_STATE
    xg0, bg0, cg0 = P_XBC // xw, (P_XBC + D_INNER) // nw, (P_XBC + D_INNER + SSM_GROUPS * D_STATE) // nw

    def cur(width, blk0):
        return pl.BlockSpec((T, width), lambda g, c: (cidx(c), blk0 + g))

    def prev(width, blk0):
        return pl.BlockSpec((8, width), lambda g, c: (jnp.maximum(cidx(c) * (T // 8) - 1, 0), blk0 + g))

    specs = [cur(xw, xg0), prev(xw, xg0), cur(nw, bg0), prev(nw, bg0), cur(nw, cg0), prev(nw, cg0),
             pl.BlockSpec((T, 128), lambda g, c: (cidx(c), P_DT // 128))]
    wb, wc = D_INNER // nw, (D_INNER + SSM_GROUPS * D_STATE) // nw
    specs += [pl.BlockSpec((4, xw), lambda g, c: (0, g)),
              pl.BlockSpec((4, nw), lambda g, c: (0, wb + g)),
              pl.BlockSpec((4, nw), lambda g, c: (0, wc + g)),
              pl.BlockSpec((1, xw), lambda g, c: (0, g)),
              pl.BlockSpec((1, nw), lambda g, c: (0, wb + g)),
              pl.BlockSpec((1, nw), lambda g, c: (0, wc + g))]
    specs += [pl.BlockSpec((1, 128), lambda g, c: (0, 0))] * 3
    return specs, cidx


def _ssd_chunk_forward(refs, ext_ref, g, c):
    (xc_ref, xp_ref, bc_ref, bp_ref, cc_ref, cp_ref, dt_ref, wx_ref, wb_ref, wc_ref,
     bx_ref, bb_ref, bcb_ref, dtb_ref, alog_ref, dsk_ref) = refs

    def conv_pre(cur_ref, prev_ref, w_ref, b_ref, width):
        ext_ref[0:8, 0:width] = jnp.where(c > 0, prev_ref[...], 0.0)
        ext_ref[8:8 + T, 0:width] = cur_ref[...]
        w = w_ref[...]
        acc = b_ref[...] + w[3:4] * cur_ref[...]
        for k in range(3):
            acc = acc + w[k:k + 1] * ext_ref[pl.ds(5 + k, T), 0:width]
        return acc

    valid = _valid_rows(c * T, T, PAD)
    v = {}
    v["valid"] = valid
    v["x_pre"] = conv_pre(xc_ref, xp_ref, wx_ref, bx_ref, GROUP_W)
    v["b_pre"] = conv_pre(bc_ref, bp_ref, wb_ref, bb_ref, D_STATE)
    v["c_pre"] = conv_pre(cc_ref, cp_ref, wc_ref, bcb_ref, D_STATE)
    xs = _silu(v["x_pre"])
    bm = jnp.where(valid, _silu(v["b_pre"]), 0.0)
    cm = jnp.where(valid, _silu(v["c_pre"]), 0.0)
    dtr = dt_ref[...] + dtb_ref[...]
    dt = jnp.where(valid, _softplus(dtr), 0.0)
    a_neg = -jnp.exp(alog_ref[...])
    a = dt * a_neg
    tril = _iota((T, T), 0) >= _iota((T, T), 1)
    cs = _xdot_l(tril.astype(BF16), a)
    hh, ll = _iota((128, GROUP_W), 0), _iota((128, GROUP_W), 1)
    expand = (hh == 8 * g + jnp.right_shift(ll, 6)).astype(BF16)
    sh, sj = _iota((128, 128), 0), _iota((128, 128), 1)
    select = ((sh == 8 * g + sj) & (sj < 8)).astype(BF16)
    hh_t, ll_t = _iota((GROUP_W, 128), 1), _iota((GROUP_W, 128), 0)
    v["expand_t"] = (hh_t == 8 * g + jnp.right_shift(ll_t, 6)).astype(BF16)
    v["select_t"] = ((sj == 8 * g + sh) & (sh < 8)).astype(BF16)
    cs_e = _xdot(cs, expand)
    dt_e = _xdot(dt, expand)
    cs_loc = _xdot(cs, select)
    cs_loc_t = cs_loc.T
    cs_last_e = cs_e[T - 1:T, :]
    v.update(xs=xs, bm=bm, cm=cm, dtr=dtr, dt=dt, a_neg=a_neg, tril=tril, expand=expand, select=select,
             cs_e=cs_e, dt_e=dt_e, cs_loc=cs_loc, cs_loc_t=cs_loc_t, cs_last_e=cs_last_e)
    v["xdt"] = xs * dt_e
    v["decay_e"] = jnp.exp(cs_last_e - cs_e)
    v["ecs_e"] = jnp.exp(cs_e)
    v["elast_e"] = jnp.exp(cs_last_e)
    v["d_e"] = _xdot(dsk_ref[...], expand)
    v["gmat"] = _dot_nt(cm.astype(BF16), bm.astype(BF16))
    return v


def _ssd_decay_pair(v, jp):
    out = []
    for j in (2 * jp, 2 * jp + 1):
        diff = v["cs_loc"][:, j:j + 1] - v["cs_loc_t"][j:j + 1, :]
        out.append(jnp.where(v["tril"], jnp.exp(jnp.where(v["tril"], diff, 0.0)), 0.0))
    return out


def _block_diag_pair(xp):
    lane = _iota(xp.shape, 1)
    return jnp.concatenate([jnp.where(lane < HEAD_P, xp, 0.0), jnp.where(lane >= HEAD_P, xp, 0.0)], axis=0)


SSD_GROUPS_PER_STEP = 4


def _ssd_group_refs(refs, gg):
    x_w, n_w = pl.ds(GROUP_W * gg, GROUP_W), pl.ds(D_STATE * gg, D_STATE)
    lanes = [x_w, x_w, n_w, n_w, n_w, n_w, None, x_w, n_w, n_w, x_w, n_w, n_w, None, None, None]
    return [r if w is None else r.at[:, w] for r, w in zip(refs, lanes)]


def _ssd_fwd(p, conv_w, conv_b, dt_bias, a_log, d_skip, n_chunks, bg=None):
    n_rows = n_chunks * T
    in_specs, _ = _ssd_specs(n_chunks, rev=False, per_step=SSD_GROUPS_PER_STEP)
    per = SSD_GROUPS_PER_STEP

    def body(*refs):
        y_ref, hin_ref, st_ref, ext_ref = refs[16:]
        g2, c = pl.program_id(0), pl.program_id(1)

        @pl.when(c == 0)
        def _():
            st_ref[...] = jnp.zeros_like(st_ref)

        for gg in range(per):
            v = _ssd_chunk_forward(_ssd_group_refs(refs[:16], gg), ext_ref.at[gg], per * g2 + gg, c)
            state = st_ref[gg]
            hin_ref[gg] = state
            ys = []
            for jp in range(4):
                l0, l1 = _ssd_decay_pair(v, jp)
                lhs = jnp.concatenate([v["gmat"] * l0, v["gmat"] * l1], axis=1).astype(BF16)
                rhs = _block_diag_pair(v["xdt"][:, 128 * jp:128 * jp + 128]).astype(BF16)
                ys.append(_dot(lhs, rhs))
            y = jnp.concatenate(ys, axis=1)
            y = y + _dot(v["cm"].astype(BF16), state.astype(BF16)) * v["ecs_e"] + v["xs"] * v["d_e"]
            y_ref[:, GROUP_W * gg:GROUP_W * gg + GROUP_W] = y
            s_new = _dot_tn(v["bm"].astype(BF16), (v["xdt"] * v["decay_e"]).astype(BF16))
            st_ref[gg] = state * v["elast_e"] + s_new

    return _call(
        body, "ssd_fwd", (SSM_GROUPS // per, n_chunks), in_specs,
        [pl.BlockSpec((T, per * GROUP_W), lambda g, c: (c, g)),
         pl.BlockSpec((per, None, D_STATE, GROUP_W), lambda g, c: (g, c, 0, 0))],
        [jax.ShapeDtypeStruct((n_rows, D_INNER), F32),
         jax.ShapeDtypeStruct((SSM_GROUPS, n_chunks, D_STATE, GROUP_W), F32)],
        [p, p, p, p, p, p, p, conv_w, conv_w, conv_w, conv_b, conv_b, conv_b, dt_bias, a_log, d_skip],
        scratch_shapes=[pltpu.VMEM((per, D_STATE, GROUP_W), F32), pltpu.VMEM((per, T + 8, GROUP_W), F32)], bg=bg)


def _ssd_bwd(p, conv_w, conv_b, dt_bias, a_log, d_skip, hin, dy, dp, n_chunks, bg=None):
    n_rows = n_chunks * T
    per = SSD_GROUPS_PER_STEP
    assert per == SSM_GROUPS
    dt_w = P_Z - P_DT
    in_specs, cidx = _ssd_specs(n_chunks, rev=True, per_step=per)
    in_specs = in_specs + [pl.BlockSpec((per, None, D_STATE, GROUP_W), lambda g, c: (g, cidx(c), 0, 0)),
                           pl.BlockSpec((T, per * GROUP_W), lambda g, c: (cidx(c), g)), ANY]

    def body(*refs):
        hin_ref, dy_ref = refs[16:18]
        dx_ref, db_ref, dc_ref, dp_ref, dpar_ref, dst_ref, ext_ref, ddt_ref = refs[19:]
        for gg in range(per):
            x_w, n_w = pl.ds(GROUP_W * gg, GROUP_W), pl.ds(D_STATE * gg, D_STATE)
            group_body(_ssd_group_refs(refs[:16], gg), hin_ref.at[gg], dy_ref.at[:, x_w], dx_ref.at[:, x_w],
                       db_ref.at[:, n_w], dc_ref.at[:, n_w], ddt_ref.at[:, n_w], dpar_ref.at[gg], dst_ref.at[gg],
                       ext_ref.at[gg], per * pl.program_id(0) + gg)
        ddt = ddt_ref[:, 0:128] + ddt_ref[:, 128:256] + ddt_ref[:, 256:384] + ddt_ref[:, 384:512]
        dp_ref[...] = jnp.concatenate([ddt, jnp.zeros((T, dt_w - 128), F32)], axis=1).astype(dp_ref.dtype)

    def group_body(in_refs, hin_ref, dy_ref, dx_ref, db_ref, dc_ref, ddt_ref, dpar_ref, dst_ref, ext_ref, g):
        step = pl.program_id(1)
        c = n_chunks - 1 - step

        @pl.when(step == 0)
        def _():
            dst_ref[...] = jnp.zeros_like(dst_ref)

        v = _ssd_chunk_forward(in_refs, ext_ref, g, c)
        hin_f = hin_ref[...]
        hin_b = hin_f.astype(BF16)
        dyv = dy_ref[...]
        dst = dst_ref[...]
        dst_b = dst.astype(BF16)
        xs, bm, cm, xdt = v["xs"], v["bm"], v["cm"], v["xdt"]
        bm_b, cm_b = bm.astype(BF16), cm.astype(BF16)

        dd_e = jnp.sum(dyv * xs, axis=0, keepdims=True)
        dxs = dyv * v["d_e"]
        ch = _dot(cm_b, hin_b)
        dch = (dyv * v["ecs_e"]).astype(BF16)
        dcm = _dot_nt(dch, hin_b)
        dhin = _dot_tn(cm_b, dch) + dst * v["elast_e"]
        dcs_e = dyv * ch * v["ecs_e"]
        dxd = _dot(bm_b, dst_b)
        dbm = _dot_nt((xdt * v["decay_e"]).astype(BF16), dst_b)
        dxdt_state = dxd * v["decay_e"]
        q = dxdt_state * xdt
        dcs_e = dcs_e - q
        dlast_e = jnp.sum(q, axis=0, keepdims=True) + jnp.sum(dst * hin_f, axis=0, keepdims=True) * v["elast_e"]
        dg = jnp.zeros((T, T), F32)
        rs_cols = jnp.zeros((T, 128), F32)
        cs_rows = jnp.zeros((128, T), F32)
        lane_i, sub_i = _iota((T, 128), 1), _iota((128, T), 0)
        dxdt_parts = []
        for jp in range(4):
            l0, l1 = _ssd_decay_pair(v, jp)
            m0, m1 = v["gmat"] * l0, v["gmat"] * l1
            xbd = _block_diag_pair(xdt[:, 128 * jp:128 * jp + 128]).astype(BF16)
            dyp = dyv[:, 128 * jp:128 * jp + 128]
            dm = _dot_nt(dyp.astype(BF16), xbd)
            dm0, dm1 = dm[:, 0:T], dm[:, T:2 * T]
            dg = dg + dm0 * l0 + dm1 * l1
            for j, qq in ((2 * jp, dm0 * m0), (2 * jp + 1, dm1 * m1)):
                rs_cols = jnp.where(lane_i == j, jnp.sum(qq, axis=1, keepdims=True), rs_cols)
                cs_rows = jnp.where(sub_i == j, jnp.sum(qq, axis=0, keepdims=True), cs_rows)
            mv = jnp.concatenate([m0, m1], axis=0).astype(BF16)
            dxdt_parts.append(_dot_tn(mv, _block_diag_pair(dyp).astype(BF16)))
        dxdt = jnp.concatenate(dxdt_parts, axis=1) + dxdt_state
        dg_b = dg.astype(BF16)
        dcm = dcm + _dot(dg_b, bm_b)
        dbm = dbm + _dot_tn(dg_b, cm_b)
        expand_t = v["expand_t"]
        dcs_loc = rs_cols - cs_rows.T
        last_row = _iota((T, 1), 0) == T - 1
        dcs_full_e = dcs_e + jnp.where(last_row, dlast_e, 0.0)
        dcs = _xdot(dcs_full_e, expand_t) + _xdot(dcs_loc, v["select_t"])
        triu = (_iota((T, T), 0) <= _iota((T, T), 1)).astype(BF16)
        da = _xdot_l(triu, dcs)
        ddt = da * v["a_neg"] + _xdot(dxdt * xs, expand_t)
        dxs = dxs + dxdt * v["dt_e"]
        ddtr = jnp.where(v["valid"], ddt * _sigmoid(v["dtr"]), 0.0)
        dx_ref[...] = dxs
        db_ref[...] = jnp.where(v["valid"], dbm, 0.0)
        dc_ref[...] = jnp.where(v["valid"], dcm, 0.0)
        ddt_ref[...] = ddtr
        dpar = jnp.concatenate([
            jnp.sum(ddtr, axis=0, keepdims=True),
            jnp.sum(da * v["dt"], axis=0, keepdims=True) * v["a_neg"],
            _xdot(dd_e, expand_t),
            jnp.zeros((5, 128), F32)], axis=0)

        @pl.when(step == 0)
        def _():
            dpar_ref[...] = dpar

        @pl.when(step > 0)
        def _():
            dpar_ref[...] += dpar

        dst_ref[...] = dhin

    return _call(
        body, "ssd_bwd", (SSM_GROUPS // per, n_chunks), in_specs,
        [pl.BlockSpec((T, per * GROUP_W), lambda g, c: (cidx(c), g)),
         pl.BlockSpec((T, per * D_STATE), lambda g, c: (cidx(c), g)),
         pl.BlockSpec((T, per * D_STATE), lambda g, c: (cidx(c), g)),
         pl.BlockSpec((T, dt_w), lambda g, c: (cidx(c), P_DT // dt_w)),
         pl.BlockSpec((per, 8, 128), lambda g, c: (g, 0, 0))],
        [jax.ShapeDtypeStruct((n_rows, D_INNER), F32),
         jax.ShapeDtypeStruct((n_rows, SSM_GROUPS * D_STATE), F32),
         jax.ShapeDtypeStruct((n_rows, SSM_GROUPS * D_STATE), F32),
         jax.ShapeDtypeStruct(dp.shape, dp.dtype),
         jax.ShapeDtypeStruct((SSM_GROUPS, 8, 128), F32)],
        [p, p, p, p, p, p, p, conv_w, conv_w, conv_w, conv_b, conv_b, conv_b, dt_bias, a_log, d_skip, hin, dy, dp],
        scratch_shapes=[pltpu.VMEM((per, D_STATE, GROUP_W), F32), pltpu.VMEM((per, T + 8, GROUP_W), F32),
                        pltpu.VMEM((T, per * 128), F32)],
        aliases={18: 3}, bg=bg)


def _alibi_slope(h):
    return 2.0 ** (-8.0 * (h + 1) / ATTN_HEADS)


def _dup_half(x256, kvh):
    xb = x256[:, 128 * (kvh // 2):128 * (kvh // 2) + 128]
    rolled = pltpu.roll(xb, 64, 1)
    lane = _iota(xb.shape, 1)
    if kvh % 2 == 0:
        return jnp.where(lane < 64, xb, rolled)
    return jnp.where(lane < 64, rolled, xb)


def _attn_masks(c):
    qi, j = _iota((T, T), 0), _iota((T, T), 1)
    tri = j <= qi
    meta_ok = (j >= PAD) & (j - PAD <= c * T + qi - PAD)
    band_ok = c >= jnp.where(tri, 1, 2)
    dist = jnp.bitwise_and(qi - j, T - 1).astype(F32)
    return tri, meta_ok, band_ok, dist


def _fold(x3, tri):
    return jnp.concatenate([x3[:, 0:T], jnp.where(tri, x3[:, 2 * T:3 * T], x3[:, T:2 * T])], axis=1)


def _unfold(x2, tri):
    band = x2[:, T:2 * T]
    return jnp.concatenate([x2[:, 0:T], jnp.where(tri, 0.0, band), jnp.where(tri, band, 0.0)], axis=1)


def _attn_scores(qp, k3, masks, h0):
    tri, meta_ok, band_ok, dist = masks
    lane = _iota(qp.shape, 1)
    s = []
    for half, h in ((0, h0), (1, h0 + 1)):
        qh = jnp.where((lane < 64) if half == 0 else (lane >= 64), qp, 0.0).astype(BF16)
        raw = _dot_nt(qh, k3)
        band = jnp.where(tri, raw[:, 2 * T:3 * T], raw[:, T:2 * T]) - _alibi_slope(h) * dist
        s.append((qh, jnp.concatenate([jnp.where(meta_ok, raw[:, 0:T], NEG), jnp.where(band_ok, band, NEG)], axis=1)))
    return s


def _attn_fwd(p, sinks, n_chunks, bg=None):
    n_rows = n_chunks * T
    kb, vb = P_K // KV_W, P_V // KV_W

    def body(q_ref, kc_ref, kp_ref, km_ref, vc_ref, vp_ref, vm_ref, sink_ref, o_ref, lse_ref):
        c = pl.program_id(0)
        sinks_v = sink_ref[...]
        masks = _attn_masks(c)
        tri, meta_ok, band_ok, dist = masks
        lane = _iota((T, 128), 1)
        for kvh in range(KV_HEADS):
            k3 = jnp.concatenate([_dup_half(r[...], kvh) for r in (km_ref, kp_ref, kc_ref)], axis=0).astype(BF16)
            v3 = jnp.concatenate([_dup_half(r[...], kvh) for r in (vm_ref, vp_ref, vc_ref)], axis=0)
            v3bd = _block_diag_rows(v3).astype(BF16)
            q2 = q_ref[:, 256 * kvh:256 * kvh + 256] * SCALE
            q4 = jnp.concatenate([jnp.where((lane < 64) if half == 0 else (lane >= 64), q2[:, 128 * pr:128 * pr + 128], 0.0)
                                  for pr in range(2) for half in range(2)], axis=0).astype(BF16)
            raw4 = _dot_nt(q4, k3)
            probs = []
            for hh in range(4):
                h = 4 * kvh + hh
                raw = raw4[T * hh:T * hh + T]
                band = jnp.where(tri, raw[:, 2 * T:3 * T], raw[:, T:2 * T]) - _alibi_slope(h) * dist
                sc = jnp.concatenate([jnp.where(meta_ok, raw[:, 0:T], NEG), jnp.where(band_ok, band, NEG)], axis=1)
                sink = sinks_v[:, h:h + 1]
                m = jnp.maximum(jnp.max(sc, axis=1, keepdims=True), sink)
                e = jnp.exp(sc - m)
                den = jnp.sum(e, axis=1, keepdims=True) + jnp.exp(sink - m)
                probs.append(_unfold(e * (1.0 / den), tri))
                lse_ref[:, h:h + 1] = m + jnp.log(den)
            p4 = jnp.concatenate([jnp.concatenate(probs[0:2], axis=1), jnp.concatenate(probs[2:4], axis=1)], axis=0)
            out = _dot(p4.astype(BF16), v3bd)
            o_ref[:, 256 * kvh:256 * kvh + 256] = jnp.concatenate([out[0:T], out[T:2 * T]], axis=1).astype(o_ref.dtype)

    blk = lambda width, col: pl.BlockSpec((T, width), lambda c: (c, col))
    prev = lambda width, col: pl.BlockSpec((T, width), lambda c: (jnp.maximum(c - 1, 0), col))
    first = lambda width, col: pl.BlockSpec((T, width), lambda c: (0, col))
    return _call(
        body, "attn_fwd", (n_chunks,),
        [blk(ATTN_W, P_Q // ATTN_W), blk(KV_W, kb), prev(KV_W, kb), first(KV_W, kb),
         blk(KV_W, vb), prev(KV_W, vb), first(KV_W, vb), pl.BlockSpec((1, 128), lambda c: (0, 0))],
        [pl.BlockSpec((T, ATTN_W), lambda c: (c, 0)), pl.BlockSpec((T, 128), lambda c: (c, 0))],
        [jax.ShapeDtypeStruct((n_rows, ATTN_W), BF16), jax.ShapeDtypeStruct((n_rows, 128), F32)],
        [p, p, p, p, p, p, p, sinks], bg=bg)


def _block_diag_rows(x3):
    lane = _iota(x3.shape, 1)
    return jnp.concatenate([jnp.where(lane < 64, x3, 0.0), jnp.where(lane >= 64, x3, 0.0)], axis=0)


def _fold_halves(x):
    return x + pltpu.roll(x, 64, 1)


def _attn_bwd(p, sinks, ao, lse, dao, dp, n_chunks, bg=None):
    kb, vb = P_K // KV_W, P_V // KV_W
    rc = lambda s: n_chunks - 1 - s

    def body(q_ref, kc_ref, kp_ref, km_ref, vc_ref, vp_ref, vm_ref, sink_ref, o_ref, lse_ref, do_ref, dp_in_ref,
             dqkv_ref, dsink_ref, kcar_ref, vcar_ref, kmeta_ref, vmeta_ref):
        step = pl.program_id(0)
        c = n_chunks - 1 - step

        @pl.when(step == 0)
        def _():
            for r in (kcar_ref, vcar_ref, kmeta_ref, vmeta_ref):
                r[...] = jnp.zeros_like(r)

        masks = _attn_masks(c)
        tri = masks[0]
        q = q_ref[...] * SCALE
        sinks_v = sink_ref[...]
        lse_v = lse_ref[...]
        ov = o_ref[...].astype(F32)
        dov = do_ref[...].astype(F32)
        lane = _iota((T, 128), 1)
        lane256 = _iota((3 * T, KV_W), 1)
        dsink = jnp.zeros((1, 128), F32)
        dk3_all = jnp.zeros((3 * T, KV_W), F32)
        dv3_all = jnp.zeros((3 * T, KV_W), F32)
        dqs = []
        for kvh in range(KV_HEADS):
            k3 = jnp.concatenate([_dup_half(r[...], kvh) for r in (km_ref, kp_ref, kc_ref)], axis=0).astype(BF16)
            v3 = jnp.concatenate([_dup_half(r[...], kvh) for r in (vm_ref, vp_ref, vc_ref)], axis=0).astype(BF16)
            dk3 = jnp.zeros((3 * T, 128), F32)
            dv3 = jnp.zeros((3 * T, 128), F32)
            for pr in range(2):
                h0 = 4 * kvh + 2 * pr
                blk = 2 * kvh + pr
                qp = q[:, 128 * blk:128 * blk + 128]
                dop = dov[:, 128 * blk:128 * blk + 128]
                prod = dop * ov[:, 128 * blk:128 * blk + 128]
                dq_pair = jnp.zeros((T, 128), F32)
                for half, ((qh, sc), h) in enumerate(zip(_attn_scores(qp, k3, masks, h0), (h0, h0 + 1))):
                    mine = (lane < 64) if half == 0 else (lane >= 64)
                    lse_h = lse_v[:, h:h + 1]
                    pm = jnp.exp(sc - lse_h)
                    doh = jnp.where(mine, dop, 0.0).astype(BF16)
                    delta = jnp.sum(jnp.where(mine, prod, 0.0), axis=1, keepdims=True)
                    dp = _fold(_dot_nt(doh, v3), tri)
                    ds = _unfold(pm * (dp - delta), tri).astype(BF16)
                    p_sink = jnp.exp(sinks_v[:, h:h + 1] - lse_h)
                    dsink = jnp.where(_iota((1, 128), 1) == h, jnp.sum(-p_sink * delta, axis=0, keepdims=True), dsink)
                    dq_pair = jnp.where(mine, _dot(ds, k3), dq_pair)
                    dk3 = dk3 + _dot_tn(ds, qh)
                    dv3 = dv3 + _dot_tn(_unfold(pm, tri).astype(BF16), doh)
                dqs.append(dq_pair * SCALE)
            in_place = (lane256 >= 64 * kvh) & (lane256 < 64 * kvh + 64)
            wide = lambda x: jnp.concatenate([x, x], axis=1)
            dk3_all = jnp.where(in_place, wide(_fold_halves(dk3)), dk3_all)
            dv3_all = jnp.where(in_place, wide(_fold_halves(dv3)), dv3_all)
        dsink_all = dsink

        @pl.when(step == 0)
        def _():
            dsink_ref[...] = dsink_all

        @pl.when(step > 0)
        def _():
            dsink_ref[...] += dsink_all

        kmeta = kmeta_ref[...] + dk3_all[0:T]
        vmeta = vmeta_ref[...] + dv3_all[0:T]
        kmeta_ref[...] = kmeta
        vmeta_ref[...] = vmeta
        is_first = c == 0
        dk = jnp.where(is_first, kmeta, dk3_all[2 * T:3 * T] + kcar_ref[...])
        dv = jnp.where(is_first, vmeta, dv3_all[2 * T:3 * T] + vcar_ref[...])
        dqkv_ref[...] = jnp.concatenate(dqs + [dk, dv], axis=1).astype(dqkv_ref.dtype)
        kcar_ref[...] = dk3_all[T:2 * T]
        vcar_ref[...] = dv3_all[T:2 * T]

    blk = lambda width, col: pl.BlockSpec((T, width), lambda s: (rc(s), col))
    prev = lambda width, col: pl.BlockSpec((T, width), lambda s: (jnp.maximum(rc(s) - 1, 0), col))
    first = lambda width, col: pl.BlockSpec((T, width), lambda s: (0, col))
    return _call(
        body, "attn_bwd", (n_chunks,),
        [blk(ATTN_W, P_Q // ATTN_W), blk(KV_W, kb), prev(KV_W, kb), first(KV_W, kb),
         blk(KV_W, vb), prev(KV_W, vb), first(KV_W, vb), pl.BlockSpec((1, 128), lambda s: (0, 0)),
         blk(ATTN_W, 0), blk(128, 0), blk(ATTN_W, 0), ANY],
        [blk(QKV_W, P_Q // QKV_W), pl.BlockSpec((1, 128), lambda s: (0, 0))],
        [jax.ShapeDtypeStruct(dp.shape, dp.dtype), jax.ShapeDtypeStruct((1, 128), F32)],
        [p, p, p, p, p, p, p, sinks, ao, lse, dao, dp],
        scratch_shapes=[pltpu.VMEM((T, KV_W), F32)] * 4, aliases={11: 0}, bg=bg)


def _pad_lanes(v, width=128):
    return jnp.pad(v, ((0, 0), (0, width - v.shape[1])))


def _local_step(x, head, tgt, plan):
    w, g, run = plan.w, plan.g, plan.run
    n_tok = x.shape[0]
    n_rows = n_tok + T
    n_chunks = n_rows // T
    tm = _row_tile(n_rows, 384)
    dt_bias, a_log, d_skip = (_pad_lanes(w[k]) for k in ("ssm_dt_bias", "ssm_a_log", "ssm_d_skip"))
    sinks = _pad_lanes(w["attn_sinks"])
    x_in = [(x, D_MODEL, 0, "prev"), (head, D_MODEL, 0, "first")]

    def h0_tile(r0, xt, hd):
        return jnp.where(r0 < T, hd, xt)

    n1, = _rowwise("norm_pre_mix", lambda r0, xt, hd, wn: [_rms(h0_tile(r0, xt, hd), wn)], n_rows, T,
                   x_in, [w["norm_pre_mix"]], [(D_MODEL, BF16)], [])
    p = _matmul("in_proj", n1, w["w_cat"], "nn", F32)
    y_ssd, hin = run("ssd_fwd", _ssd_fwd, p, w["ssm_conv_w"], w["ssm_conv_b"], dt_bias, a_log, d_skip, n_chunks)
    ao, lse = run("attn_fwd", _attn_fwd, p, sinks, n_chunks)

    def gate_norm(r0, y, z, wn):
        return [_rms(y * _silu(z), wn)]

    yn, = run("ssm_gate_norm", _rowwise, "ssm_gate_norm", gate_norm, n_rows, tm,
              [(y_ssd, D_INNER, 0), (p, D_INNER, P_Z // D_INNER)], [w["ssm_norm"]], [(D_INNER, BF16)], [])
    y_ssm = _matmul("ssm_out", yn, w["w_ssm_out"], "nn", F32)
    y_attn = _matmul("attn_out", ao, w["w_attn_out"], "nn", F32)

    def mix_gate(r0, ys, ya, gs, ga):
        return [_sigmoid(gs) * ys + _sigmoid(ga) * ya]

    gate_ins = [(p, D_MODEL, P_GATE // D_MODEL), (p, D_MODEL, P_GATE // D_MODEL + 1)]
    mixed, = _rowwise("mix_gate", mix_gate, n_rows, tm, [(y_ssm, D_MODEL, 0), (y_attn, D_MODEL, 0)] + gate_ins,
                      [], [(D_MODEL, BF16)], [])
    mix = _matmul("mix_out", mixed, w["w_mix_out"], "nn", F32)

    def post_mix(r0, mx, xt, hd, w_post, w_pre):
        h1 = jnp.where(_valid_rows(r0, mx.shape[0], PAD), h0_tile(r0, xt, hd) + _rms(mx, w_post), 0.0)
        return [h1, _rms(h1, w_pre)]

    h1, n2 = _rowwise("post_mix", post_mix, n_rows, T, [(mix, D_MODEL, 0)] + x_in,
                      [w["norm_post_mix"], w["norm_pre_ffn"]], [(D_MODEL, F32), (D_MODEL, BF16)], [])
    u_raw = _matmul("ffn_up", n2, w["w_ffn_up"], "nn", F32)
    f = _ffn_act("ffn_act", u_raw, w["ffn_conv_w"], w["ffn_conv_b"], n_rows)
    ffn = _matmul("ffn_down", f, w["w_ffn_down"], "nn", F32)

    def final(r0, fo, h, t, w_post):
        real = r0 >= T
        err = jnp.where(real, h + _rms(fo, w_post) - t, 0.0)
        dy = err * (1.0 / D_MODEL)
        dffn, dw = _rms_bwd(dy, fo, w_post)
        return [dffn, dy, jnp.sum(err * err, axis=0, keepdims=True), dw]

    dffn, dh2, loss_cols, g_norm_post_ffn = _rowwise(
        "loss_head", final, n_rows, T, [(ffn, D_MODEL, 0), (h1, D_MODEL, 0), (tgt, D_MODEL, 0, "prev")],
        [w["norm_post_ffn"]], [(D_MODEL, BF16), (D_MODEL, F32)], [D_MODEL, D_MODEL])

    g["norm_post_ffn"] = g_norm_post_ffn
    g["w_ffn_down"] = _matmul("ffn_down_dw", f, dffn, "tn", F32)
    df = _matmul("ffn_down_dx", dffn, w["w_ffn_down"], "nt", F32)
    du_raw, dconv = _conv_bwd("ffn_act_bwd", u_raw, 0, [df], [(0, c0) for c0 in range(0, FFN_DIM, CONV_LANES)],
                              w["ffn_conv_w"], w["ffn_conv_b"], n_rows, True)
    g["ffn_conv_w"], g["ffn_conv_b"] = dconv[0:3], dconv[3:4]
    g["w_ffn_up"] = _matmul("ffn_up_dw", n2, du_raw, "tn", F32)
    dn2 = run("ffn_up_dx", _matmul, "ffn_up_dx", du_raw, w["w_ffn_up"], "nt", F32)

    def post_mix_bwd(r0, dn, d2, h, mx, w_pre, w_post):
        dx, dw_pre = _rms_bwd(dn, h, w_pre)
        dh1 = jnp.where(_valid_rows(r0, dn.shape[0], PAD), dx + d2, 0.0)
        dmix, dw_post = _rms_bwd(dh1, mx, w_post)
        return [dh1, dmix, dw_pre, dw_post]

    dh1, dmix, g["norm_pre_ffn"], g["norm_post_mix"] = _rowwise(
        "post_mix_bwd", post_mix_bwd, n_rows, tm,
        [(dn2, D_MODEL, 0), (dh2, D_MODEL, 0), (h1, D_MODEL, 0), (mix, D_MODEL, 0)],
        [w["norm_pre_ffn"], w["norm_post_mix"]], [(D_MODEL, F32), (D_MODEL, BF16)], [D_MODEL, D_MODEL])
    g["w_mix_out"] = _matmul("mix_out_dw", mixed, dmix, "tn", F32)
    dmixed = _matmul("mix_out_dx", dmix, w["w_mix_out"], "nt", F32)

    def mix_gate_bwd(r0, dm, ys, ya, gs, ga):
        ss, sa = _sigmoid(gs), _sigmoid(ga)
        dgate = jnp.concatenate([dm * ys * ss * (1.0 - ss), dm * ya * sa * (1.0 - sa)], axis=1)
        return [dm * ss, dm * sa, dgate]

    dys, dya, dp = _rowwise(
        "mix_gate_bwd", mix_gate_bwd, n_rows, tm,
        [(dmixed, D_MODEL, 0), (y_ssm, D_MODEL, 0), (y_attn, D_MODEL, 0)] + gate_ins,
        [], [(D_MODEL, BF16), (D_MODEL, BF16), (2 * D_MODEL, BF16, "new", P_W, P_GATE // (2 * D_MODEL))], [])
    g["w_ssm_out"] = _matmul("ssm_out_dw", yn, dys, "tn", F32)
    dyn = _matmul("ssm_out_dx", dys, w["w_ssm_out"], "nt", F32)
    g["w_attn_out"] = _matmul("attn_out_dw", ao, dya, "tn", F32)
    dao = _matmul("attn_out_dx", dya, w["w_attn_out"], "nt", BF16)

    def gate_norm_bwd(r0, dn, y, z, wn):
        sz, dsz = _silu_grad(z)
        dyz, dw = _rms_bwd(dn, y * sz, wn)
        live = _valid_rows(r0, dn.shape[0], PAD)
        return [jnp.where(live, dyz * sz, 0.0), jnp.where(live, dyz * y * dsz, 0.0), dw]

    dy_ssd, dp, g["ssm_norm"] = run(
        "ssm_gate_norm_bwd", _rowwise, "ssm_gate_norm_bwd", gate_norm_bwd, n_rows, tm,
        [(dyn, D_INNER, 0), (y_ssd, D_INNER, 0), (p, D_INNER, P_Z // D_INNER)],
        [w["ssm_norm"]], [(D_INNER, F32), (D_INNER, BF16, "into", dp, P_Z // D_INNER)], [D_INNER])
    dp, dsink = run("attn_bwd", _attn_bwd, p, sinks, ao, lse, dao, dp, n_chunks)
    g["attn_sinks"] = dsink[:, 0:ATTN_HEADS]
    dxs, dbm, dcm, dp, dpar = run("ssd_bwd", _ssd_bwd, p, w["ssm_conv_w"], w["ssm_conv_b"], dt_bias, a_log,
                                  d_skip, hin, dy_ssd, dp, n_chunks)
    dpar = jnp.sum(dpar, axis=0)
    g["ssm_dt_bias"], g["ssm_a_log"], g["ssm_d_skip"] = (dpar[i:i + 1, 0:SSM_HEADS] for i in range(3))
    x_chunks = [(src, c0) for src, arr in enumerate((dxs, dbm, dcm)) for c0 in range(0, arr.shape[1], CONV_LANES)]
    dp, dconv = run("ssm_conv_bwd", _conv_bwd, "ssm_conv_bwd", p, P_XBC // CONV_DIM, [dxs, dbm, dcm], x_chunks,
                    w["ssm_conv_w"], w["ssm_conv_b"], n_rows, False, into=dp, into_blk=P_XBC // CONV_DIM)
    g["ssm_conv_w"], g["ssm_conv_b"] = dconv[0:4], dconv[4:5]
    g["w_cat"] = _matmul("in_proj_dw", n1, dp, "tn", F32)
    dn1 = run("in_proj_dx", _matmul, "in_proj_dx", dp, w["w_cat"], "nt", F32)

    def pre_mix_bwd(r0, dn, d1, xt, hd, wn):
        dx, dw = _rms_bwd(dn, h0_tile(r0, xt, hd), wn)
        dh0 = jnp.where(_valid_rows(r0, dn.shape[0], PAD), dx + d1, 0.0)
        return [dh0, dh0, dw]

    dx_out, dhead, g["norm_pre_mix"] = _rowwise(
        "pre_mix_bwd", pre_mix_bwd, n_rows, T, [(dn1, D_MODEL, 0), (dh1, D_MODEL, 0)] + x_in,
        [w["norm_pre_mix"]], [(D_MODEL, F32, "prev", n_tok), (D_MODEL, F32, "first")], [D_MODEL])
    return jnp.sum(loss_cols), dx_out, dhead


_IN_SECTIONS = [((5152, 6176), P_Q), ((6176, 6432), P_K), ((6432, 6688), P_V), ((5120, 5152), P_DT),
                ((0, 2048), P_Z), ((6688, 8736), P_GATE), ((2048, 5120), P_XBC)]


IN_SHARD = N_IN // 4


def _shard_pieces(a, b):
    return [(j, max(a, j * IN_SHARD) - j * IN_SHARD, min(b, (j + 1) * IN_SHARD) - j * IN_SHARD)
            for j in range(4) if max(a, j * IN_SHARD) < min(b, (j + 1) * IN_SHARD)]


def _to_cat(w4):
    parts, at = [], 0
    for (a, b), off in _IN_SECTIONS:
        if off > at:
            parts.append(jnp.zeros((w4.shape[1], off - at), w4.dtype))
        parts += [w4[j, :, lo:hi] for j, lo, hi in _shard_pieces(a, b)]
        at = off + (b - a)
    return jnp.concatenate(parts, axis=1)


def _from_cat(g_cat):
    shards = [[] for _ in range(4)]
    for (a, b), off in sorted(_IN_SECTIONS):
        for j, lo, hi in _shard_pieces(a, b):
            start = off + j * IN_SHARD + lo - a
            shards[j].append(g_cat[:, start:start + hi - lo])
    return jnp.stack([jnp.concatenate(s, axis=1) for s in shards])


LANES = 1024
_BIG = [("w_in", 1024, 2184, "chip"), ("w_ssm_out", 512, 1024, "row"), ("w_attn_out", 256, 1024, "row"),
        ("w_mix_out", 256, 1024, "row"), ("w_ffn_up", 1024, 1408, "col"), ("w_ffn_down", 704, 1024, "row"),
        ("small", 32, LANES, "chip")]
_SMALL_SHARDED = [("ssm_conv_w", (4, 768), 1), ("ffn_conv_w", (3, 1408), 1), ("meta_tokens", (16, 256), 1)]
_REPLICATED = [("norm_pre_mix", 1024), ("ssm_conv_b", 3072), ("ssm_dt_bias", 32), ("ssm_a_log", 32),
               ("ssm_d_skip", 32), ("ssm_norm", 2048), ("attn_sinks", 16), ("norm_post_mix", 1024),
               ("norm_pre_ffn", 1024), ("ffn_conv_b", 5632), ("norm_post_ffn", 1024)]
SMALL_ROWS = 24


def _rep_rows():
    out, at = [], 0
    for _, width in _REPLICATED:
        out.append((at, -(-width // LANES)))
        at += out[-1][1]
    return out, at


def _in_rows(parts):
    rows = [jnp.pad(a, ((0, 0), (0, -a.shape[1] % LANES))).reshape(-1, LANES) for a in parts]
    flat = jnp.concatenate(rows, axis=0)
    return jnp.pad(flat, ((0, SMALL_ROWS - flat.shape[0]), (0, 0)))
WEIGHT_ORDER = ["meta_tokens", "norm_pre_mix", "w_in", "ssm_conv_w", "ssm_conv_b", "ssm_dt_bias", "ssm_a_log",
                "ssm_d_skip", "ssm_norm", "w_ssm_out", "attn_sinks", "w_attn_out", "w_mix_out", "norm_post_mix",
                "norm_pre_ffn", "w_ffn_up", "ffn_conv_w", "ffn_conv_b", "w_ffn_down", "norm_post_ffn"]


def _flatten(parts, rows):
    flat = jnp.concatenate([a.reshape(-1) for a in parts])
    return jnp.pad(flat, (0, rows * LANES - flat.shape[0])).reshape(rows, LANES)


def _unflatten(flat, shapes):
    flat = flat.reshape(-1)
    out, off = [], 0
    for shp in shapes:
        n = math.prod(shp)
        out.append(flat[off:off + n].reshape(shp))
        off += n
    return out


def _shard_of(full, chip, shape, axis):
    return lax.slice_in_dim(full, chip * shape[axis], (chip + 1) * shape[axis], axis=axis)


def _full_shape(r, c, layout):
    return {"row": (4 * r, c), "col": (r, 4 * c), "chip": (4, r, c)}[layout]


def _shard_view(ref, r, c, layout, chip):
    if layout == "row":
        return ref.at[pl.ds(pl.multiple_of(chip * r, 16), r), :]
    if layout == "col":
        return ref.at[:, pl.ds(pl.multiple_of(chip * c, 128), c)]
    return ref.at[chip]


def _half_view(ref, r, c, layout, chip, half):
    hr = r // 2
    if layout == "row":
        return ref.at[pl.ds(pl.multiple_of(chip * r + half * hr, 16), hr), :]
    r0 = pl.multiple_of(half * hr, 16)
    if layout == "col":
        return ref.at[pl.ds(r0, hr), pl.ds(pl.multiple_of(chip * c, 128), c)]
    return ref.at[chip, pl.ds(r0, hr), :]


def _mesh_pos():
    return lax.axis_index("x"), lax.axis_index("y"), lax.axis_index("c")


def _other_chips(x, y):
    return [(1 - x, y), (x, 1 - y), (1 - x, 1 - y)]


def _chip_index(x, y):
    return 2 * x + y


def _run_exchange(name, ex):
    n_in, n_out = len(ex.ins), len(ex.out_shapes)

    def body(*refs):
        in_refs, out_refs = refs[:n_in], refs[n_in:n_in + n_out]
        send_sems, recv_sems = refs[n_in + n_out:]
        copies = [pltpu.make_async_remote_copy(src_ref=s, dst_ref=d, send_sem=send_sems.at[i], recv_sem=recv_sems.at[i],
                                               device_id=dev, device_id_type=MESH)
                  for i, (s, d, dev) in enumerate(ex.make_copies(in_refs, out_refs))]
        assert len(copies) == ex.n_copies
        for cp in copies:
            cp.start()
        for cp in copies:
            cp.wait()

    return pl.pallas_call(
        body, name=name, in_specs=[ANY] * n_in, out_specs=[ANY] * n_out, out_shape=list(ex.out_shapes),
        scratch_shapes=[pltpu.SemaphoreType.DMA((ex.n_copies,)), pltpu.SemaphoreType.DMA((ex.n_copies,))],
        compiler_params=pltpu.CompilerParams(has_side_effects=True),
    )(*ex.ins)


def _join(*exs):
    def make(in_refs, out_refs):
        copies, i0, o0 = [], 0, 0
        for ex in exs:
            copies += ex.make_copies(in_refs[i0:i0 + len(ex.ins)], out_refs[o0:o0 + len(ex.out_shapes)])
            i0, o0 = i0 + len(ex.ins), o0 + len(ex.out_shapes)
        return copies

    aliases, i0, o0 = {}, 0, 0
    for ex in exs:
        aliases.update({i0 + k: o0 + v for k, v in ex.aliases.items()})
        i0, o0 = i0 + len(ex.ins), o0 + len(ex.out_shapes)
    return _Exchange([a for ex in exs for a in ex.ins], [s for ex in exs for s in ex.out_shapes], make,
                     sum(ex.n_copies for ex in exs), aliases)


def _split(exs, results):
    out, o0 = [], 0
    for ex in exs:
        out.append(list(results[o0:o0 + len(ex.out_shapes)]))
        o0 += len(ex.out_shapes)
    return out


def _gather_ici(entries, shards):
    def make(in_refs, out_refs):
        x, y, c = _mesh_pos()
        j = _chip_index(x, y)
        copies = []
        for ref_in, ref_out, (_, r, cc, lay) in zip(in_refs, out_refs, entries):
            copies.append((ref_in, _shard_view(ref_out, r, cc, lay, j), None))
            mine = ref_in.at[pl.ds(pl.multiple_of(c * (r // 2), 16), r // 2), :]
            copies += [(mine, _half_view(ref_out, r, cc, lay, j, c), (*ch, c)) for ch in _other_chips(x, y)]
        return copies

    shapes = [jax.ShapeDtypeStruct(_full_shape(r, cc, lay), s.dtype) for s, (_, r, cc, lay) in zip(shards, entries)]
    return _Exchange(list(shards), shapes, make, 4 * len(entries))


def _gather_pass_on(entries, fulls):
    def make(in_refs, out_refs):
        x, y, c = _mesh_pos()
        copies = []
        for ref, (_, r, cc, lay) in zip(out_refs, entries):
            for ch in _other_chips(x, y):
                landed = _half_view(ref, r, cc, lay, _chip_index(*ch), c)
                copies.append((landed, landed, (x, y, 1 - c)))
        return copies

    return _Exchange(list(fulls), [jax.ShapeDtypeStruct(f.shape, f.dtype) for f in fulls], make, 3 * len(entries),
                     {a: a for a in range(len(entries))})


def _gather_weights(entries, shards):
    n = len(entries)

    def body(*refs):
        ins, outs = refs[:n], refs[n:2 * n]
        send_sems, recv_sems, local_sems = refs[2 * n:]
        x, y, c = _mesh_pos()
        j = _chip_index(x, y)
        sibling = (x, y, 1 - c)
        chips = _other_chips(x, y)
        idx = [_chip_index(*ch) for ch in chips]

        def remote(k, src, dst, dev):
            return pltpu.make_async_remote_copy(src_ref=src, dst_ref=dst, send_sem=send_sems.at[k],
                                                recv_sem=recv_sems.at[k], device_id=dev, device_id_type=MESH)

        own = [pltpu.make_async_copy(ins[a], _shard_view(outs[a], r, cc, lay, j), local_sems.at[a])
               for a, (_, r, cc, lay) in enumerate(entries)]
        for cp in own:
            cp.start()
        first, passed = [], []
        for a, (_, r, cc, lay) in enumerate(entries):
            mine = ins[a].at[pl.ds(pl.multiple_of(c * (r // 2), 16), r // 2), :]
            for k, ch in enumerate(chips):
                first.append(remote(6 * a + k, mine, _half_view(outs[a], r, cc, lay, j, c), (*ch, c)))
                landed = _half_view(outs[a], r, cc, lay, idx[k], c)
                passed.append(remote(6 * a + 3 + k, landed, landed, sibling))
        for cp in first:
            cp.start()
        for a, (_, r, cc, lay) in enumerate(entries):
            for k in range(3):
                landed = _half_view(outs[a], r, cc, lay, idx[k], c)
                remote(6 * a + k, landed, landed, sibling).wait_recv()
                passed[3 * a + k].start()
        for a, (_, r, cc, lay) in enumerate(entries):
            for k in range(3):
                theirs = _half_view(outs[a], r, cc, lay, idx[k], 1 - c)
                remote(6 * a + 3 + k, theirs, theirs, sibling).wait_recv()
        for cp in first + passed:
            cp.wait_send()
        for cp in own:
            cp.wait()

    return pl.pallas_call(
        body, name="gather_weights", in_specs=[ANY] * n, out_specs=[ANY] * n,
        out_shape=[jax.ShapeDtypeStruct(_full_shape(r, cc, lay), s.dtype) for s, (_, r, cc, lay) in zip(shards, entries)],
        scratch_shapes=[pltpu.SemaphoreType.DMA((6 * n,)), pltpu.SemaphoreType.DMA((6 * n,)), pltpu.SemaphoreType.DMA((n,))],
        compiler_params=pltpu.CompilerParams(has_side_effects=True),
    )(*shards)


def _pair_exchange(entries, grads):
    def make(in_refs, out_refs):
        x, y, c = _mesh_pos()
        return [(_half_view(ref_in, r, cc, lay, i, 1 - c), ref_out.at[i], (x, y, 1 - c))
                for ref_in, ref_out, (_, r, cc, lay) in zip(in_refs, out_refs, entries) for i in range(4)]

    return _Exchange(list(grads), [jax.ShapeDtypeStruct((4, r // 2, cc), F32) for _, r, cc, _ in entries], make,
                     4 * len(entries))


def _whole_to_sibling(arrays):
    def make(in_refs, out_refs):
        x, y, c = _mesh_pos()
        return [(r, o, (x, y, 1 - c)) for r, o in zip(in_refs, out_refs)]

    return _Exchange(list(arrays), [jax.ShapeDtypeStruct(a.shape, a.dtype) for a in arrays], make, len(arrays))


def _chip_exchange(psends):
    def make(in_refs, out_refs):
        x, y, c = _mesh_pos()
        return [(ref_in.at[_chip_index(*ch)], ref_out.at[k], (*ch, c))
                for ref_in, ref_out in zip(in_refs, out_refs) for k, ch in enumerate(_other_chips(x, y))]

    return _Exchange(list(psends), [jax.ShapeDtypeStruct((3,) + p.shape[1:], p.dtype) for p in psends], make,
                     3 * len(psends))


def _to_all_chips(array):
    def make(in_refs, out_refs):
        x, y, c = _mesh_pos()
        return [(in_refs[0], out_refs[0].at[k], (*ch, c)) for k, ch in enumerate(_other_chips(x, y))]

    return _Exchange([array], [jax.ShapeDtypeStruct((3,) + array.shape, array.dtype)], make, 3)


SUM_ROWS = 256
ADAM_ROWS = 128


def _pair_sum(name, grad, recv, ids, r, c, layout):
    hr = r // 2
    tr = _row_tile(hr, SUM_ROWS)
    nb = hr // tr

    def body(ids_ref, g_ref, r_ref, send_ref, own_ref):
        s = g_ref[...] + r_ref[...]
        send_ref[...] = s.astype(send_ref.dtype)

        @pl.when(pl.program_id(1) == ids_ref[1])
        def _():
            own_ref[...] = s

    if layout == "row":
        g_spec = pl.BlockSpec((tr, c), lambda t, j, ids_ref: ((j * r + ids_ref[0] * hr) // tr + t, 0))
    elif layout == "col":
        g_spec = pl.BlockSpec((tr, c), lambda t, j, ids_ref: (ids_ref[0] * nb + t, j))
    else:
        g_spec = pl.BlockSpec((None, tr, c), lambda t, j, ids_ref: (j, ids_ref[0] * nb + t, 0))
    grid_spec = pltpu.PrefetchScalarGridSpec(
        num_scalar_prefetch=1, grid=(nb, 4),
        in_specs=[g_spec, pl.BlockSpec((None, tr, c), lambda t, j, ids_ref: (j, t, 0))],
        out_specs=[pl.BlockSpec((None, tr, c), lambda t, j, ids_ref: (j, t, 0)),
                   pl.BlockSpec((tr, c), lambda t, j, ids_ref: (t, 0))])
    return pl.pallas_call(
        body, name=name, grid_spec=grid_spec,
        out_shape=[jax.ShapeDtypeStruct((4, hr, c), BF16), jax.ShapeDtypeStruct((hr, c), F32)],
        compiler_params=_cparams(2),
    )(ids, grad, recv)


def _chip_sum(name, own, recv):
    hr, c = own.shape
    tr = _row_tile(hr, SUM_ROWS)

    def body(o_ref, r_ref, out_ref):
        out_ref[...] = ((o_ref[...] + r_ref[0].astype(F32)) + r_ref[1].astype(F32)) + r_ref[2].astype(F32)

    return pl.pallas_call(
        body, name=name, grid=(hr // tr,),
        in_specs=[pl.BlockSpec((tr, c), lambda i: (i, 0)), pl.BlockSpec((3, tr, c), lambda i: (0, i, 0))],
        out_specs=pl.BlockSpec((tr, c), lambda i: (i, 0)),
        out_shape=jax.ShapeDtypeStruct((hr, c), F32), compiler_params=_cparams(1),
    )(own, recv)


def _chip_sum_small(own, recv, ids):
    def body(ids_ref, o_ref, r_ref, out_ref):
        j = ids_ref[1]
        total = None
        for i in range(4):
            m = jnp.bitwise_xor(i, j)
            term = jnp.where(m == 0, o_ref[...], jnp.where(m == 2, r_ref[0], jnp.where(m == 1, r_ref[1], r_ref[2])))
            total = term if total is None else total + term
        out_ref[...] = total

    grid_spec = pltpu.PrefetchScalarGridSpec(
        num_scalar_prefetch=1, grid=(1,),
        in_specs=[pl.BlockSpec(own.shape, lambda i, ids_ref: (0, 0)), pl.BlockSpec(recv.shape, lambda i, ids_ref: (0, 0, 0))],
        out_specs=pl.BlockSpec(own.shape, lambda i, ids_ref: (0, 0)))
    return pl.pallas_call(body, name="chip_sum_small", grid_spec=grid_spec,
                          out_shape=jax.ShapeDtypeStruct(own.shape, F32), compiler_params=_cparams(1))(ids, own, recv)


def _adamw(name, w, m, v, mine, theirs, ids):
    lead = (None,) * (w.ndim - 2)
    rows, cols = w.shape[-2:]
    half = rows // 2
    tr = _row_tile(half, ADAM_ROWS, unit=8)
    nb = half // tr
    c1 = 1.0 / (1.0 - ADAM_B1 ** ADAM_STEP)
    c2 = 1.0 / (1.0 - ADAM_B2 ** ADAM_STEP)

    def body(ids_ref, w_ref, m_ref, v_ref, mine_ref, theirs_ref, g_out, d_out, m_out, v_out):
        g = jnp.where(pl.program_id(0) == ids_ref[0], mine_ref[...], theirs_ref[...])
        m_new = ADAM_B1 * m_ref[...] + (1.0 - ADAM_B1) * g
        v_new = ADAM_B2 * v_ref[...] + (1.0 - ADAM_B2) * (g * g)
        d_out[...] = -ADAM_LR * ((m_new * c1) / (jnp.sqrt(v_new * c2) + ADAM_EPS) + ADAM_WD * w_ref[...])
        g_out[...] = g
        m_out[...] = m_new
        v_out[...] = v_new

    full = pl.BlockSpec(lead + (tr, cols), lambda h, i, ids_ref: (0,) * len(lead) + (h * nb + i, 0))
    part = pl.BlockSpec((tr, cols), lambda h, i, ids_ref: (i, 0))
    grid_spec = pltpu.PrefetchScalarGridSpec(num_scalar_prefetch=1, grid=(2, nb),
                                             in_specs=[full, full, full, part, part], out_specs=[full] * 4)
    return pl.pallas_call(
        body, name=name, grid_spec=grid_spec,
        out_shape=[jax.ShapeDtypeStruct(w.shape, F32)] * 4, compiler_params=_cparams(2),
    )(ids, w, m, v, mine, theirs)


def _adamw_whole(name, w, m, v, g):
    rows, cols = w.shape[-2:]
    tr = _row_tile(rows, 2 * ADAM_ROWS, unit=8)
    c1 = 1.0 / (1.0 - ADAM_B1 ** ADAM_STEP)
    c2 = 1.0 / (1.0 - ADAM_B2 ** ADAM_STEP)

    def body(w_ref, m_ref, v_ref, g_ref, g_out, d_out, m_out, v_out):
        g = g_ref[...]
        m_new = ADAM_B1 * m_ref[...] + (1.0 - ADAM_B1) * g
        v_new = ADAM_B2 * v_ref[...] + (1.0 - ADAM_B2) * (g * g)
        d_out[...] = -ADAM_LR * ((m_new * c1) / (jnp.sqrt(v_new * c2) + ADAM_EPS) + ADAM_WD * w_ref[...])
        g_out[...] = g
        m_out[...] = m_new
        v_out[...] = v_new

    full = pl.BlockSpec((None, tr, cols), lambda i: (0, i, 0))
    return pl.pallas_call(
        body, name=name, grid=(rows // tr,), in_specs=[full, full, full, pl.BlockSpec((tr, cols), lambda i: (i, 0))],
        out_specs=[full] * 4, out_shape=[jax.ShapeDtypeStruct(w.shape, F32)] * 4, compiler_params=_cparams(1),
    )(w, m, v, g)


def _adamw_replicated(g_rows, ws, ms, vs):
    n = len(ws)
    layout, _ = _rep_rows()
    c1 = 1.0 / (1.0 - ADAM_B1 ** ADAM_STEP)
    c2 = 1.0 / (1.0 - ADAM_B2 ** ADAM_STEP)

    def body(g_ref, *refs):
        w_refs, m_refs, v_refs = refs[0:n], refs[n:2 * n], refs[2 * n:3 * n]
        outs = refs[3 * n:]
        for k, (r0, rows) in enumerate(layout):
            width = w_refs[k].shape[1]
            g = jnp.concatenate([g_ref[r0 + j:r0 + j + 1, :] for j in range(rows)], axis=1)[:, 0:width]
            m_new = ADAM_B1 * m_refs[k][...] + (1.0 - ADAM_B1) * g
            v_new = ADAM_B2 * v_refs[k][...] + (1.0 - ADAM_B2) * (g * g)
            outs[k][...] = g
            outs[n + k][...] = -ADAM_LR * ((m_new * c1) / (jnp.sqrt(v_new * c2) + ADAM_EPS) + ADAM_WD * w_refs[k][...])
            outs[2 * n + k][...] = m_new
            outs[3 * n + k][...] = v_new

    res = pl.pallas_call(body, name="adamw_replicated",
                         out_shape=[jax.ShapeDtypeStruct(w.shape, F32) for _ in range(4) for w in ws])(g_rows, *ws, *ms, *vs)
    return [res[k * n:(k + 1) * n] for k in range(4)]


def _small_shard(parts):
    return _flatten(parts, _BIG[-1][1])


_ENTRY = {e[0]: e for e in _BIG}
FFN_MATS = ("w_ffn_down", "w_ffn_up")
MIXER_MATS = ("w_mix_out", "w_ssm_out", "w_attn_out")


class _StepPlan:
    def __init__(self, w, late_shards, shards, ids):
        self.w, self.g = w, {}
        self.late_shards, self.shards, self.ids = late_shards, shards, ids
        self.sums, self.halves, self.results = {}, {}, {}

    def run(self, name, fn, *args, **kw):
        at = getattr(self, "_at_" + name, None)
        if at is None:
            return fn(*args, **kw)
        exchange, landed = at()
        res, extra = fn(*args, bg=exchange, **kw)
        landed(extra)
        return res

    def _at_ssd_fwd(self):
        def landed(fulls):
            self.partly_gathered = fulls

        return _gather_ici([_ENTRY[n] for n in MIXER_MATS], [self.late_shards[n] for n in MIXER_MATS]), landed

    def _at_attn_fwd(self):
        stages = (_gather_pass_on([_ENTRY[n] for n in MIXER_MATS], self.partly_gathered),
                  _gather_ici([_ENTRY[n] for n in FFN_MATS], [self.late_shards[n] for n in FFN_MATS]))

        def landed(extra):
            mixer, self.partly_gathered = _split(stages, extra)
            self.w.update(zip(MIXER_MATS, mixer))

        return _join(*stages), landed

    def _at_ssm_gate_norm(self):
        return (_gather_pass_on([_ENTRY[n] for n in FFN_MATS], self.partly_gathered),
                lambda fulls: self.w.update(zip(FFN_MATS, fulls)))

    def pair_sums(self, names, grads, recv):
        for n, gr, rv in zip(names, grads, recv):
            _, r, c, lay = _ENTRY[n]
            self.sums[n] = _pair_sum("pair_sum_" + n, gr, rv, self.ids, r, c, lay)

    def chip_sums(self, names, recv):
        for n, rv in zip(names, recv):
            self.halves[n] = _chip_sum("chip_sum_" + n, self.sums[n][1], rv)

    def adamw(self, names, theirs):
        for n, th in zip(names, theirs):
            sh = self.shards[n]
            if n == "w_in":
                mine_first = self.ids[0] == 0
                g_t = jnp.where(mine_first, jnp.concatenate([self.halves[n], th], axis=0),
                                jnp.concatenate([th, self.halves[n]], axis=0)).T
                res = _adamw_whole("adamw_" + n, *[jnp.swapaxes(sh[k], -1, -2) for k in ("w", "m", "v")], g_t)
                self.results[n] = [jnp.swapaxes(r, -1, -2) for r in res]
            else:
                self.results[n] = _adamw("adamw_" + n, sh["w"], sh["m"], sh["v"], self.halves[n], th, self.ids)

    def _pair_stage(self, names, grads):
        return (_pair_exchange([_ENTRY[n] for n in names], grads),
                lambda recv: self.pair_sums(names, grads, recv))

    def _at_ffn_up_dx(self):
        return self._pair_stage(FFN_MATS, [self.g[n] for n in FFN_MATS])

    def _at_ssm_gate_norm_bwd(self):
        return self._pair_stage(MIXER_MATS, [self.g[n] for n in MIXER_MATS])

    def _at_attn_bwd(self):
        return _chip_exchange([self.sums[n][0] for n in FFN_MATS]), lambda recv: self.chip_sums(FFN_MATS, recv)

    def _at_ssd_bwd(self):
        stages = (_chip_exchange([self.sums[n][0] for n in MIXER_MATS]),
                  _whole_to_sibling([self.halves[n] for n in FFN_MATS]))

        def landed(extra):
            recv, theirs = _split(stages, extra)
            self.chip_sums(MIXER_MATS, recv)
            self.adamw(FFN_MATS, theirs)

        return _join(*stages), landed

    def _at_ssm_conv_bwd(self):
        return _whole_to_sibling([self.halves[n] for n in MIXER_MATS]), lambda theirs: self.adamw(MIXER_MATS, theirs)

    def _at_in_proj_dx(self):
        grads = [_from_cat(self.g.pop("w_cat"))]
        self.pair_sums(("w_in",), grads, _run_exchange("grad_pair_exchange_w_in", _pair_exchange([_ENTRY["w_in"]], grads)))
        return _chip_exchange([self.sums["w_in"][0]]), lambda recv: self.chip_sums(("w_in",), recv)

    def finish(self, g_small, g_rep, rep_shards):
        stages = (_pair_exchange([_ENTRY["small"]], [g_small]), _whole_to_sibling([g_rep]))
        recv_small, recv_rep = _split(stages, _run_exchange("grad_pair_exchange_tail", _join(*stages)))
        self.pair_sums(("small",), [g_small], recv_small)
        p_rep, = _rowwise("pair_sum_replicated", lambda r0, a, b: [a + b], SMALL_ROWS, SMALL_ROWS,
                          [(g_rep, LANES, 0), (recv_rep[0], LANES, 0)], [], [(LANES, F32)], [])
        stages = (_chip_exchange([self.sums["small"][0]]), _to_all_chips(p_rep))
        recv, recv_rep = _split(stages, _run_exchange("grad_chip_exchange_tail", _join(*stages)))
        self.chip_sums(("small",), recv)
        g_rep_tot = _chip_sum_small(p_rep, recv_rep[0], self.ids)
        last = ("w_in", "small")
        self.adamw(last, _run_exchange("grad_half_share_tail", _whole_to_sibling([self.halves[n] for n in last])))
        self.results["replicated"] = _adamw_replicated(g_rep_tot, rep_shards["w"], rep_shards["m"], rep_shards["v"])
        return g_rep_tot[_rep_rows()[1], 0]


def kernel(x, meta_tokens, norm_pre_mix, w_in, ssm_conv_w, ssm_conv_b, ssm_dt_bias, ssm_a_log, ssm_d_skip, ssm_norm, w_ssm_out, attn_sinks, w_attn_out, w_mix_out, norm_post_mix, norm_pre_ffn, w_ffn_up, ffn_conv_w, ffn_conv_b, w_ffn_down, norm_post_ffn, loss_target, m_meta_tokens, m_norm_pre_mix, m_w_in, m_ssm_conv_w, m_ssm_conv_b, m_ssm_dt_bias, m_ssm_a_log, m_ssm_d_skip, m_ssm_norm, m_w_ssm_out, m_attn_sinks, m_w_attn_out, m_w_mix_out, m_norm_post_mix, m_norm_pre_ffn, m_w_ffn_up, m_ffn_conv_w, m_ffn_conv_b, m_w_ffn_down, m_norm_post_ffn, v_meta_tokens, v_norm_pre_mix, v_w_in, v_ssm_conv_w, v_ssm_conv_b, v_ssm_dt_bias, v_ssm_a_log, v_ssm_d_skip, v_ssm_norm, v_w_ssm_out, v_attn_sinks, v_w_attn_out, v_w_mix_out, v_norm_post_mix, v_norm_pre_ffn, v_w_ffn_up, v_ffn_conv_w, v_ffn_conv_b, v_w_ffn_down, v_norm_post_ffn):
    args = dict(locals())
    squeeze = lambda a: a.reshape(a.shape[-2:])
    wts = {n: squeeze(args[n]) for n in WEIGHT_ORDER}
    mom = {n: squeeze(args["m_" + n]) for n in WEIGHT_ORDER}
    var = {n: squeeze(args["v_" + n]) for n in WEIGHT_ORDER}
    x_i, y_i, c_i = _mesh_pos()
    ids = jnp.stack([c_i, _chip_index(x_i, y_i)]).astype(jnp.int32)
    big_names = [n for n, _, _, _ in _BIG[:-1]]
    small_names = [n for n, _, _ in _SMALL_SHARDED]
    rep_names = [n for n, _ in _REPLICATED]

    stacks = {"w": wts, "m": mom, "v": var}
    shards = {n: {"w": args[n], "m": args["m_" + n], "v": args["v_" + n]} for n in big_names}
    shards["small"] = {k: _small_shard([d[n] for n in small_names]) for k, d in stacks.items()}
    rep_shards = {k: [d[n] for n in rep_names] for k, d in stacks.items()}

    w_in4, small_all = _gather_weights([_ENTRY["w_in"], _ENTRY["small"]], [wts["w_in"].astype(BF16), shards["small"]["w"]])
    w = {n: wts[n] for n in rep_names}
    w["w_cat"] = _to_cat(w_in4)
    small_parts = [_unflatten(small_all[i], [shp for _, shp, _ in _SMALL_SHARDED]) for i in range(4)]
    for k, (n, _, axis) in enumerate(_SMALL_SHARDED):
        w[n] = jnp.concatenate([small_parts[i][k] for i in range(4)], axis=axis)
    plan = _StepPlan(w, {n: wts[n].astype(BF16) for n in MIXER_MATS + FFN_MATS}, shards, ids)

    head = jnp.concatenate([jnp.zeros((PAD, D_MODEL), F32), w["meta_tokens"]], axis=0)
    loss_sum, dx, dhead = _local_step(x[0], head, loss_target[0], plan)
    g = plan.g
    g["meta_tokens"] = dhead[PAD:]
    g_small = jnp.stack([_small_shard([_shard_of(g[n], i, shp, ax) for n, shp, ax in _SMALL_SHARDED]) for i in range(4)])
    loss_part = (loss_sum * (0.5 / D_MODEL)).reshape(1, 1)
    loss = plan.finish(g_small, _in_rows([g[n] for n in rep_names] + [loss_part]), rep_shards)

    results = {}
    for kind in range(4):
        results.update({(kind, n): plan.results[n][kind] for n in big_names})
        parts = _unflatten(plan.results["small"][kind], [shp for _, shp, _ in _SMALL_SHARDED])
        results.update({(kind, n): parts[k] for k, n in enumerate(small_names)})
        results.update({(kind, n): plan.results["replicated"][kind][k] for k, n in enumerate(rep_names)})
    outs = [results[kind, n].reshape(args[n].shape) for kind in range(4) for n in WEIGHT_ORDER]
    return (loss, dx[None], *outs)
```

```python
import math
from typing import Any, Callable, NamedTuple, Sequence

import jax
import jax.numpy as jnp
from jax import lax
from jax.experimental import pallas as pl
from jax.experimental.pallas import tpu as pltpu

F32 = jnp.float32
BF16 = jnp.bfloat16

D_MODEL = 1024
N_META = 16
T = 128
PAD = T - N_META
D_INNER = 2048
SSM_HEADS = 32
HEAD_P = 64
SSM_GROUPS = 4
GROUP_W = D_INNER // SSM_GROUPS
D_STATE = 128
CONV_DIM = D_INNER + 2 * SSM_GROUPS * D_STATE
ATTN_HEADS = 16
KV_HEADS = 4
ATTN_W = 1024
KV_W = 256
FFN_DIM = 2816
N_IN = 8736
EPS = 1e-6
NEG = -1e30
SCALE = 0.125

P_Q, P_K, P_V, P_DT, P_Z, P_GATE, P_XBC = 0, 1024, 1280, 1536, 2048, 4096, 6144
QKV_W = 1536
P_W = 9216

ADAM_LR, ADAM_B1, ADAM_B2, ADAM_EPS, ADAM_WD, ADAM_STEP = 0.001, 0.9, 0.999, 1e-08, 0.01, 10

VMEM_BUDGET = 40 * 1024 * 1024
VMEM_LIMIT = 56 * 1024 * 1024
MESH = pl.DeviceIdType.MESH
ANY = pl.BlockSpec(memory_space=pl.ANY)


def _cparams(n_axes, **kw):
    return pltpu.CompilerParams(dimension_semantics=("arbitrary",) * n_axes, vmem_limit_bytes=VMEM_LIMIT, **kw)


class _Exchange(NamedTuple):
    ins: Sequence[Any]
    out_shapes: Sequence[Any]
    make_copies: Callable
    n_copies: int
    aliases: dict = {}


def _call(body, name, grid, in_specs, out_specs, out_shape, operands, scratch_shapes=(), aliases=None, bg=None):
    aliases = dict(aliases or {})
    if bg is None:
        return pl.pallas_call(body, name=name, grid=grid, in_specs=in_specs, out_specs=out_specs, out_shape=out_shape,
                              scratch_shapes=list(scratch_shapes), input_output_aliases=aliases,
                              compiler_params=_cparams(len(grid)))(*operands)
    n_in, n_out, n_scr = len(in_specs), len(out_specs), len(scratch_shapes)
    nb_in, nb_out = len(bg.ins), len(bg.out_shapes)

    def hosted(*refs):
        ins, bg_ins = refs[:n_in], refs[n_in:n_in + nb_in]
        outs = refs[n_in + nb_in:n_in + nb_in + n_out]
        bg_outs = refs[n_in + nb_in + n_out:n_in + nb_in + n_out + nb_out]
        scratch = refs[n_in + nb_in + n_out + nb_out:n_in + nb_in + n_out + nb_out + n_scr]
        send_sems, recv_sems = refs[-2:]
        pids = [pl.program_id(a) for a in range(len(grid))]
        first, last = pids[0] == 0, pids[0] == grid[0] - 1
        for p, g in zip(pids[1:], grid[1:]):
            first, last = first & (p == 0), last & (p == g - 1)
        copies = []
        for k, (src, dst, peer) in enumerate(bg.make_copies(bg_ins, bg_outs)):
            if peer is None:
                copies.append(pltpu.make_async_copy(src, dst, send_sems.at[k]))
            else:
                copies.append(pltpu.make_async_remote_copy(src_ref=src, dst_ref=dst, send_sem=send_sems.at[k],
                                                           recv_sem=recv_sems.at[k], device_id=peer, device_id_type=MESH))
        assert len(copies) == bg.n_copies

        @pl.when(first)
        def _():
            for cp in copies:
                cp.start()

        body(*ins, *outs, *scratch)

        @pl.when(last)
        def _():
            for cp in copies:
                cp.wait()

    aliases = {(k if k < n_in else k + nb_in): v for k, v in aliases.items()}
    aliases.update({n_in + k: n_out + v for k, v in bg.aliases.items()})
    res = pl.pallas_call(
        hosted, name=name, grid=grid, in_specs=list(in_specs) + [ANY] * nb_in, out_specs=list(out_specs) + [ANY] * nb_out,
        out_shape=list(out_shape) + list(bg.out_shapes), input_output_aliases=aliases,
        scratch_shapes=list(scratch_shapes) + [pltpu.SemaphoreType.DMA((bg.n_copies,))] * 2,
        compiler_params=_cparams(len(grid), has_side_effects=True))(*operands, *bg.ins)
    return res[:n_out], res[n_out:]


def _sigmoid(x):
    return 1.0 / (1.0 + jnp.exp(-x))


def _silu(x):
    return x * _sigmoid(x)


def _silu_grad(x):
    s = _sigmoid(x)
    return x * s, s * (1.0 + x * (1.0 - s))


def _dsilu(x):
    return _silu_grad(x)[1]


def _softplus(x):
    e = jnp.exp(-jnp.abs(x))
    small = e * (1.0 - e * (0.5 - e * (1.0 / 3.0)))
    return jnp.maximum(x, 0.0) + jnp.where(e < 0.01, small, jnp.log(1.0 + e))


def _rms(x, w):
    r = lax.rsqrt(jnp.mean(x * x, axis=-1, keepdims=True) + EPS)
    return x * r * w


def _rms_bwd(dy, x, w):
    r = lax.rsqrt(jnp.mean(x * x, axis=-1, keepdims=True) + EPS)
    xh = x * r
    g = dy * w
    dx = r * (g - xh * jnp.mean(g * xh, axis=-1, keepdims=True))
    dw = jnp.sum(dy * xh, axis=0, keepdims=True)
    return dx, dw


def _dot(a, b):
    return jnp.dot(a, b, preferred_element_type=F32)


def _dot_nt(a, b):
    return lax.dot_general(a, b, (((1,), (1,)), ((), ())), preferred_element_type=F32)


def _dot_tn(a, b):
    return lax.dot_general(a, b, (((0,), (0,)), ((), ())), preferred_element_type=F32)


def _split3(x):
    hi = x.astype(BF16)
    r = x - hi.astype(F32)
    mid = r.astype(BF16)
    lo = (r - mid.astype(F32)).astype(BF16)
    return hi, mid, lo


def _xdot(x, e):
    hi, mid, lo = _split3(x)
    return _dot(hi, e) + _dot(mid, e) + _dot(lo, e)


def _xdot_l(e, x):
    hi, mid, lo = _split3(x)
    return _dot(e, hi) + _dot(e, mid) + _dot(e, lo)


def _iota(shape, dim):
    return lax.broadcasted_iota(jnp.int32, shape, dim)


def _divisors(n, unit):
    return [t for t in range(unit, n + 1, unit) if n % t == 0]


MIN_MATMUL_STEPS = 8


def _matmul_tiles(m, n, k, a_bytes, b_bytes, o_bytes, m_unit):
    best = None
    for tm in _divisors(m, m_unit):
        for tn in _divisors(n, 128):
            for tk in _divisors(k, 128):
                acc = 0 if tk == k else tm * tn * 4
                vm = 2 * (tm * tk * a_bytes + tk * tn * b_bytes + tm * tn * o_bytes) + acc
                if vm > VMEM_BUDGET:
                    continue
                steps = (m // tm) * (n // tn) * (k // tk)
                score = (tk == k, min(steps, MIN_MATMUL_STEPS), min(tm, 256), tm * tn * tk)
                if best is None or score > best[0]:
                    best = (score, (tm, tn, tk))
    return best[1]


def _matmul(name, a, b, mode, out_dtype, bg=None):
    if mode == "nn":
        (m, k), n = a.shape, b.shape[1]
    elif mode == "nt":
        (m, k), n = a.shape, b.shape[0]
    else:
        (k, m), n = a.shape, b.shape[1]
    ab, bb, ob = a.dtype.itemsize, b.dtype.itemsize, jnp.dtype(out_dtype).itemsize
    tm, tn, tk = _matmul_tiles(m, n, k, ab, bb, ob, 128 if mode == "tn" else 16)
    nk = k // tk
    dot = {"nn": _dot, "nt": _dot_nt, "tn": _dot_tn}[mode]

    def body(a_ref, b_ref, o_ref, *scratch):
        prod = dot(a_ref[...].astype(BF16), b_ref[...].astype(BF16))
        if nk == 1:
            o_ref[...] = prod.astype(o_ref.dtype)
        else:
            acc_ref, = scratch
            kk = pl.program_id(2)

            @pl.when(kk == 0)
            def _():
                acc_ref[...] = prod

            @pl.when(kk > 0)
            def _():
                acc_ref[...] += prod

            @pl.when(kk == nk - 1)
            def _():
                o_ref[...] = acc_ref[...].astype(o_ref.dtype)

    a_spec = pl.BlockSpec((tk, tm), lambda i, j, kk: (kk, i)) if mode == "tn" else pl.BlockSpec((tm, tk), lambda i, j, kk: (i, kk))
    b_spec = pl.BlockSpec((tn, tk), lambda i, j, kk: (j, kk)) if mode == "nt" else pl.BlockSpec((tk, tn), lambda i, j, kk: (kk, j))
    res = _call(body, name, (m // tm, n // tn, nk), [a_spec, b_spec], [pl.BlockSpec((tm, tn), lambda i, j, kk: (i, j))],
                [jax.ShapeDtypeStruct((m, n), out_dtype)], [a, b],
                scratch_shapes=[] if nk == 1 else [pltpu.VMEM((tm, tn), F32)], bg=bg)
    return res[0] if bg is None else (res[0][0], res[1])


def _row_tile(n_rows, cap, unit=16):
    return max([t for t in _divisors(n_rows, unit) if t <= cap], default=n_rows)


ROW_SUB = 384
GROUP_UNROLL = 4


def _rowwise(name, fn, n_rows, tm, row_ins, full_ins, row_outs, acc_outs, bg=None):
    n_in = len(row_ins) + len(full_ins)
    n_ro = len(row_outs)
    into = [(k, o[3]) for k, o in enumerate(row_outs) if len(o) > 2 and o[2] == "into"]

    n_row_in = len(row_ins)
    sub = min(tm, ROW_SUB)

    def body(*refs):
        i = pl.program_id(0)
        outs = refs[n_in + len(into):]

        sums = tuple(jnp.zeros((1, w), F32) for w in acc_outs)
        for s in range(tm // sub):
            rows = pl.ds(s * sub, sub)
            vals = [r[rows, :] for r in refs[:n_row_in]] + [r[...] for r in refs[n_row_in:n_in]]
            res = fn(i * tm + s * sub, *vals)
            for o, r, v in zip(row_outs, outs[:n_ro], res[:n_ro]):
                if len(o) > 2 and o[2] == "first":
                    @pl.when(i == 0)
                    def _(r=r, v=v, rows=rows):
                        r[rows, :] = v.astype(r.dtype)
                else:
                    r[rows, :] = v.astype(r.dtype)
            sums = tuple(a + v for a, v in zip(sums, res[n_ro:]))

        @pl.when(i == 0)
        def _():
            for r, v in zip(outs[n_ro:], sums):
                r[...] = v

        @pl.when(i > 0)
        def _():
            for r, v in zip(outs[n_ro:], sums):
                r[...] += v

    def in_spec(entry):
        w, cb = entry[1], entry[2]
        if len(entry) > 3 and entry[3] == "prev":
            return pl.BlockSpec((tm, w), lambda i: (jnp.maximum(i - 1, 0), cb))
        if len(entry) > 3 and entry[3] == "first":
            return pl.BlockSpec((tm, w), lambda i: (0, cb))
        return pl.BlockSpec((tm, w), lambda i: (i, cb))

    def out_spec(o):
        if len(o) == 2:
            return pl.BlockSpec((tm, o[0]), lambda i: (i, 0)), jax.ShapeDtypeStruct((n_rows, o[0]), o[1])
        if o[2] == "new":
            return pl.BlockSpec((tm, o[0]), lambda i: (i, o[4])), jax.ShapeDtypeStruct((n_rows, o[3]), o[1])
        if o[2] == "into":
            return pl.BlockSpec((tm, o[0]), lambda i: (i, o[4])), jax.ShapeDtypeStruct(o[3].shape, o[3].dtype)
        if o[2] == "first":
            return pl.BlockSpec((tm, o[0]), lambda i: (0, 0)), jax.ShapeDtypeStruct((tm, o[0]), o[1])
        return pl.BlockSpec((tm, o[0]), lambda i: (jnp.maximum(i - 1, 0), 0)), jax.ShapeDtypeStruct((o[3], o[0]), o[1])

    in_specs = [in_spec(e) for e in row_ins]
    in_specs += [pl.BlockSpec(a.shape, lambda i: (0, 0)) for a in full_ins]
    in_specs += [pl.BlockSpec(memory_space=pl.ANY) for _ in into]
    specs_shapes = [out_spec(o) for o in row_outs]
    out_specs = [s for s, _ in specs_shapes] + [pl.BlockSpec((1, w), lambda i: (0, 0)) for w in acc_outs]
    out_shape = [s for _, s in specs_shapes] + [jax.ShapeDtypeStruct((1, w), F32) for w in acc_outs]
    return _call(body, name, (n_rows // tm,), in_specs, out_specs, out_shape,
                 [e[0] for e in row_ins] + list(full_ins) + [arr for _, arr in into],
                 aliases={n_in + a: k for a, (k, _) in enumerate(into)}, bg=bg)


def _valid_rows(first_row, tm, lo):
    return (first_row + _iota((tm, 1), 0)) >= lo


CONV_ROWS = 128
CONV_SUB = 16
CONV_LANES = 256


def _conv_specs(tm, width, blk, n_rows, after):
    specs = [pl.BlockSpec((tm, width), lambda i: (i, blk)),
             pl.BlockSpec((8, width), lambda i: (jnp.maximum(i * (tm // 8) - 1, 0), blk))]
    if after:
        specs.append(pl.BlockSpec((16, width), lambda i: (jnp.minimum((i + 1) * (tm // 16), n_rows // 16 - 1), blk)))
    return specs


def _conv_window(win, w_ref, b_ref, taps, c0, cw, n):
    acc = b_ref[:, c0:c0 + cw] + w_ref[taps - 1:taps, c0:c0 + cw] * win[8:8 + n]
    for k in range(taps - 1):
        acc = acc + w_ref[k:k + 1, c0:c0 + cw] * win[8 - (taps - 1) + k:8 - (taps - 1) + k + n]
    return acc


def _ffn_act(name, u_raw, conv_w, conv_b, n_rows):
    tm, sub, cw = CONV_ROWS, CONV_SUB, CONV_LANES
    taps, width = conv_w.shape
    half = width // 2

    def body(cur_ref, prev_ref, w_ref, b_ref, f_ref, ext_ref):
        i = pl.program_id(0)
        ext_ref[0:8, :] = jnp.where(i > 0, prev_ref[...], 0.0)
        ext_ref[8:8 + tm, :] = cur_ref[...]
        for q in range(half // cw):
            a0, g0 = q * cw, half + q * cw

            def group(s, carry):
                r = pl.multiple_of(s * sub, sub)
                a = _conv_window(ext_ref[pl.ds(r, sub + 8), a0:a0 + cw], w_ref, b_ref, taps, a0, cw, sub)
                g = _conv_window(ext_ref[pl.ds(r, sub + 8), g0:g0 + cw], w_ref, b_ref, taps, g0, cw, sub)
                f = jnp.where(_valid_rows(i * tm + r, sub, PAD), _silu(a) * g, 0.0)
                f_ref[pl.ds(r, sub), a0:a0 + cw] = f.astype(f_ref.dtype)
                return carry

            lax.fori_loop(0, tm // sub, group, 0, unroll=GROUP_UNROLL)

    return pl.pallas_call(
        body, name=name, grid=(n_rows // tm,),
        in_specs=_conv_specs(tm, width, 0, n_rows, False) + [pl.BlockSpec((taps, width), lambda i: (0, 0)),
                                                             pl.BlockSpec((1, width), lambda i: (0, 0))],
        out_specs=pl.BlockSpec((tm, half), lambda i: (i, 0)),
        out_shape=jax.ShapeDtypeStruct((n_rows, half), BF16),
        scratch_shapes=[pltpu.VMEM((tm + 8, width), F32)],
        compiler_params=_cparams(1),
    )(u_raw, u_raw, conv_w, conv_b)


def _conv_bwd(name, raw, raw_blk, dsrcs, chunk_src, conv_w, conv_b, n_rows, gated, into=None, into_blk=0, bg=None):
    taps, width = conv_w.shape
    half = width // 2 if gated else width
    tm, sub, cw = CONV_ROWS, CONV_SUB, CONV_LANES
    te = tm + 16
    nd = len(dsrcs)
    n_parts = 2 if gated else 1

    def body(*refs):
        cur_ref, prev_ref, next_ref = refs[0:3]
        dcur, dnext = refs[3:3 + nd], refs[3 + nd:3 + 2 * nd]
        w_ref, b_ref = refs[3 + 2 * nd:5 + 2 * nd]
        out_ref, acc_ref, ext_ref, du_ref = refs[-4:]
        i = pl.program_id(0)
        ext_ref[0:8, :] = jnp.where(i > 0, prev_ref[...], 0.0)
        ext_ref[8:8 + tm, :] = cur_ref[...]
        ext_ref[8 + tm:24 + tm, :] = next_ref[...]

        for q, (src, off) in enumerate(chunk_src):
            cols = [q * cw, half + q * cw][:n_parts]

            def conv_grad(r, d):
                pre = [_conv_window(ext_ref[pl.ds(r, sub + 8), c0:c0 + cw], w_ref, b_ref, taps, c0, cw, sub) for c0 in cols]
                row = i * tm + r + _iota((sub, 1), 0)
                live = (row >= PAD) & (row < n_rows)
                if gated:
                    act, dact = _silu_grad(pre[0])
                    dus = [d * pre[1] * dact, d * act]
                else:
                    dus = [d * _dsilu(pre[0])]
                for part, du in enumerate(dus):
                    du_ref[part, pl.ds(r, sub), :] = jnp.where(live, du, 0.0)

            def tile_rows(s, carry):
                r = pl.multiple_of(s * sub, sub)
                conv_grad(r, dcur[src][pl.ds(r, sub), off:off + cw].astype(F32))
                return carry

            lax.fori_loop(0, tm // sub, tile_rows, 0, unroll=GROUP_UNROLL)
            conv_grad(tm, dnext[src][:, off:off + cw].astype(F32))

            for part, c0 in enumerate(cols):
                taps_w = [w_ref[k:k + 1, c0:c0 + cw] for k in range(taps)]

                def back(s, sums):
                    new = list(sums)
                    for u in range(2):
                        r = pl.multiple_of((2 * s + u) * sub, sub)
                        win = du_ref[part, pl.ds(r, sub + 8), :]
                        raw_rows = ext_ref[pl.ds(8 + r, sub), c0:c0 + cw]
                        draw = jnp.zeros((sub, cw), F32)
                        for k in range(taps):
                            shifted = win[taps - 1 - k:taps - 1 - k + sub]
                            draw = draw + taps_w[k] * shifted
                            new[k] = new[k] + shifted * raw_rows
                        new[taps] = new[taps] + win[0:sub]
                        out_ref[pl.ds(r, sub), c0:c0 + cw] = jnp.where(_valid_rows(i * tm + r, sub, PAD), draw, 0.0).astype(out_ref.dtype)
                    return tuple(new)

                sums = lax.fori_loop(0, tm // (2 * sub), back, tuple(jnp.zeros((sub, cw), F32) for _ in range(taps + 1)))
                for k in range(taps + 1):
                    total = jnp.sum(sums[k], axis=0, keepdims=True)
                    acc_ref[k:k + 1, c0:c0 + cw] = jnp.where(i == 0, total, acc_ref[k:k + 1, c0:c0 + cw] + total)

    in_specs = _conv_specs(tm, width, raw_blk, n_rows, True)
    in_specs += [pl.BlockSpec((tm, d.shape[1]), lambda i: (i, 0)) for d in dsrcs]
    in_specs += [pl.BlockSpec((16, d.shape[1]), lambda i: (jnp.minimum((i + 1) * (tm // 16), n_rows // 16 - 1), 0)) for d in dsrcs]
    in_specs += [pl.BlockSpec((taps, width), lambda i: (0, 0)), pl.BlockSpec((1, width), lambda i: (0, 0))]
    operands = [raw, raw, raw] + list(dsrcs) + list(dsrcs) + [conv_w, conv_b]
    aliases = {}
    if into is None:
        out0 = jax.ShapeDtypeStruct((n_rows, width), BF16)
    else:
        in_specs.append(pl.BlockSpec(memory_space=pl.ANY))
        operands.append(into)
        aliases = {len(operands) - 1: 0}
        out0 = jax.ShapeDtypeStruct(into.shape, into.dtype)
    return _call(body, name, (n_rows // tm,), in_specs,
                 [pl.BlockSpec((tm, width), lambda i: (i, into_blk)), pl.BlockSpec((8, width), lambda i: (0, 0))],
                 [out0, jax.ShapeDtypeStruct((8, width), F32)], operands,
                 scratch_shapes=[pltpu.VMEM((tm + 24, width), F32), pltpu.VMEM((n_parts, te + 8, cw), F32)],
                 aliases=aliases, bg=bg)


def _ssd_specs(n_chunks, rev, per_step=1):
    cidx = (lambda c: n_chunks - 1 - c) if rev else (lambda c: c)
    xw, nw = per_step * GROUP_W, per_step * D_STATE
    xg0, bg0, cg0 = P_XBC // xw, (P_XBC + D_INNER) // nw, (P_XBC + D_INNER + SSM_GROUPS * D_STATE) // nw

    def cur(width, blk0):
        return pl.BlockSpec((T, width), lambda g, c: (cidx(c), blk0 + g))

    def prev(width, blk0):
        return pl.BlockSpec((8, width), lambda g, c: (jnp.maximum(cidx(c) * (T // 8) - 1, 0), blk0 + g))

    specs = [cur(xw, xg0), prev(xw, xg0), cur(nw, bg0), prev(nw, bg0), cur(nw, cg0), prev(nw, cg0),
             pl.BlockSpec((T, 128), lambda g, c: (cidx(c), P_DT // 128))]
    wb, wc = D_INNER // nw, (D_INNER + SSM_GROUPS * D_STATE) // nw
    specs += [pl.BlockSpec((4, xw), lambda g, c: (0, g)),
              pl.BlockSpec((4, nw), lambda g, c: (0, wb + g)),
              pl.BlockSpec((4, nw), lambda g, c: (0, wc + g)),
              pl.BlockSpec((1, xw), lambda g, c: (0, g)),
              pl.BlockSpec((1, nw), lambda g, c: (0, wb + g)),
              pl.BlockSpec((1, nw), lambda g, c: (0, wc + g))]
    specs += [pl.BlockSpec((1, 128), lambda g, c: (0, 0))] * 3
    return specs, cidx


def _ssd_chunk_forward(refs, ext_ref, g, c):
    (xc_ref, xp_ref, bc_ref, bp_ref, cc_ref, cp_ref, dt_ref, wx_ref, wb_ref, wc_ref,
     bx_ref, bb_ref, bcb_ref, dtb_ref, alog_ref, dsk_ref) = refs

    def conv_pre(cur_ref, prev_ref, w_ref, b_ref, width):
        ext_ref[0:8, 0:width] = jnp.where(c > 0, prev_ref[...], 0.0)
        ext_ref[8:8 + T, 0:width] = cur_ref[...]
        w = w_ref[...]
        acc = b_ref[...] + w[3:4] * cur_ref[...]
        for k in range(3):
            acc = acc + w[k:k + 1] * ext_ref[pl.ds(5 + k, T), 0:width]
        return acc

    valid = _valid_rows(c * T, T, PAD)
    v = {}
    v["valid"] = valid
    v["x_pre"] = conv_pre(xc_ref, xp_ref, wx_ref, bx_ref, GROUP_W)
    v["b_pre"] = conv_pre(bc_ref, bp_ref, wb_ref, bb_ref, D_STATE)
    v["c_pre"] = conv_pre(cc_ref, cp_ref, wc_ref, bcb_ref, D_STATE)
    xs = _silu(v["x_pre"])
    bm = jnp.where(valid, _silu(v["b_pre"]), 0.0)
    cm = jnp.where(valid, _silu(v["c_pre"]), 0.0)
    dtr = dt_ref[...] + dtb_ref[...]
    dt = jnp.where(valid, _softplus(dtr), 0.0)
    a_neg = -jnp.exp(alog_ref[...])
    a = dt * a_neg
    tril = _iota((T, T), 0) >= _iota((T, T), 1)
    cs = _xdot_l(tril.astype(BF16), a)
    hh, ll = _iota((128, GROUP_W), 0), _iota((128, GROUP_W), 1)
    expand = (hh == 8 * g + jnp.right_shift(ll, 6)).astype(BF16)
    sh, sj = _iota((128, 128), 0), _iota((128, 128), 1)
    select = ((sh == 8 * g + sj) & (sj < 8)).astype(BF16)
    hh_t, ll_t = _iota((GROUP_W, 128), 1), _iota((GROUP_W, 128), 0)
    v["expand_t"] = (hh_t == 8 * g + jnp.right_shift(ll_t, 6)).astype(BF16)
    v["select_t"] = ((sj == 8 * g + sh) & (sh < 8)).astype(BF16)
    cs_e = _xdot(cs, expand)
    dt_e = _xdot(dt, expand)
    cs_loc = _xdot(cs, select)
    cs_loc_t = cs_loc.T
    cs_last_e = cs_e[T - 1:T, :]
    v.update(xs=xs, bm=bm, cm=cm, dtr=dtr, dt=dt, a_neg=a_neg, tril=tril, expand=expand, select=select,
             cs_e=cs_e, dt_e=dt_e, cs_loc=cs_loc, cs_loc_t=cs_loc_t, cs_last_e=cs_last_e)
    v["xdt"] = xs * dt_e
    v["decay_e"] = jnp.exp(cs_last_e - cs_e)
    v["ecs_e"] = jnp.exp(cs_e)
    v["elast_e"] = jnp.exp(cs_last_e)
    v["d_e"] = _xdot(dsk_ref[...], expand)
    v["gmat"] = _dot_nt(cm.astype(BF16), bm.astype(BF16))
    return v


def _ssd_decay_pair(v, jp):
    out = []
    for j in (2 * jp, 2 * jp + 1):
        diff = v["cs_loc"][:, j:j + 1] - v["cs_loc_t"][j:j + 1, :]
        out.append(jnp.where(v["tril"], jnp.exp(jnp.where(v["tril"], diff, 0.0)), 0.0))
    return out


def _block_diag_pair(xp):
    lane = _iota(xp.shape, 1)
    return jnp.concatenate([jnp.where(lane < HEAD_P, xp, 0.0), jnp.where(lane >= HEAD_P, xp, 0.0)], axis=0)


SSD_GROUPS_PER_STEP = 4


def _ssd_group_refs(refs, gg):
    x_w, n_w = pl.ds(GROUP_W * gg, GROUP_W), pl.ds(D_STATE * gg, D_STATE)
    lanes = [x_w, x_w, n_w, n_w, n_w, n_w, None, x_w, n_w, n_w, x_w, n_w, n_w, None, None, None]
    return [r if w is None else r.at[:, w] for r, w in zip(refs, lanes)]


def _ssd_fwd(p, conv_w, conv_b, dt_bias, a_log, d_skip, n_chunks, bg=None):
    n_rows = n_chunks * T
    in_specs, _ = _ssd_specs(n_chunks, rev=False, per_step=SSD_GROUPS_PER_STEP)
    per = SSD_GROUPS_PER_STEP

    def body(*refs):
        y_ref, hin_ref, st_ref, ext_ref = refs[16:]
        g2, c = pl.program_id(0), pl.program_id(1)

        @pl.when(c == 0)
        def _():
            st_ref[...] = jnp.zeros_like(st_ref)

        for gg in range(per):
            v = _ssd_chunk_forward(_ssd_group_refs(refs[:16], gg), ext_ref.at[gg], per * g2 + gg, c)
            state = st_ref[gg]
            hin_ref[gg] = state
            ys = []
            for jp in range(4):
                l0, l1 = _ssd_decay_pair(v, jp)
                lhs = jnp.concatenate([v["gmat"] * l0, v["gmat"] * l1], axis=1).astype(BF16)
                rhs = _block_diag_pair(v["xdt"][:, 128 * jp:128 * jp + 128]).astype(BF16)
                ys.append(_dot(lhs, rhs))
            y = jnp.concatenate(ys, axis=1)
            y = y + _dot(v["cm"].astype(BF16), state.astype(BF16)) * v["ecs_e"] + v["xs"] * v["d_e"]
            y_ref[:, GROUP_W * gg:GROUP_W * gg + GROUP_W] = y
            s_new = _dot_tn(v["bm"].astype(BF16), (v["xdt"] * v["decay_e"]).astype(BF16))
            st_ref[gg] = state * v["elast_e"] + s_new

    return _call(
        body, "ssd_fwd", (SSM_GROUPS // per, n_chunks), in_specs,
        [pl.BlockSpec((T, per * GROUP_W), lambda g, c: (c, g)),
         pl.BlockSpec((per, None, D_STATE, GROUP_W), lambda g, c: (g, c, 0, 0))],
        [jax.ShapeDtypeStruct((n_rows, D_INNER), F32),
         jax.ShapeDtypeStruct((SSM_GROUPS, n_chunks, D_STATE, GROUP_W), F32)],
        [p, p, p, p, p, p, p, conv_w, conv_w, conv_w, conv_b, conv_b, conv_b, dt_bias, a_log, d_skip],
        scratch_shapes=[pltpu.VMEM((per, D_STATE, GROUP_W), F32), pltpu.VMEM((per, T + 8, GROUP_W), F32)], bg=bg)


def _ssd_bwd(p, conv_w, conv_b, dt_bias, a_log, d_skip, hin, dy, dp, n_chunks, bg=None):
    n_rows = n_chunks * T
    per = SSD_GROUPS_PER_STEP
    assert per == SSM_GROUPS
    dt_w = P_Z - P_DT
    in_specs, cidx = _ssd_specs(n_chunks, rev=True, per_step=per)
    in_specs = in_specs + [pl.BlockSpec((per, None, D_STATE, GROUP_W), lambda g, c: (g, cidx(c), 0, 0)),
                           pl.BlockSpec((T, per * GROUP_W), lambda g, c: (cidx(c), g)), ANY]

    def body(*refs):
        hin_ref, dy_ref = refs[16:18]
        dx_ref, db_ref, dc_ref, dp_ref, dpar_ref, dst_ref, ext_ref, ddt_ref = refs[19:]
        for gg in range(per):
            x_w, n_w = pl.ds(GROUP_W * gg, GROUP_W), pl.ds(D_STATE * gg, D_STATE)
            group_body(_ssd_group_refs(refs[:16], gg), hin_ref.at[gg], dy_ref.at[:, x_w], dx_ref.at[:, x_w],
                       db_ref.at[:, n_w], dc_ref.at[:, n_w], ddt_ref.at[:, n_w], dpar_ref.at[gg], dst_ref.at[gg],
                       ext_ref.at[gg], per * pl.program_id(0) + gg)
        ddt = ddt_ref[:, 0:128] + ddt_ref[:, 128:256] + ddt_ref[:, 256:384] + ddt_ref[:, 384:512]
        dp_ref[...] = jnp.concatenate([ddt, jnp.zeros((T, dt_w - 128), F32)], axis=1).astype(dp_ref.dtype)

    def group_body(in_refs, hin_ref, dy_ref, dx_ref, db_ref, dc_ref, ddt_ref, dpar_ref, dst_ref, ext_ref, g):
        step = pl.program_id(1)
        c = n_chunks - 1 - step

        @pl.when(step == 0)
        def _():
            dst_ref[...] = jnp.zeros_like(dst_ref)

        v = _ssd_chunk_forward(in_refs, ext_ref, g, c)
        hin_f = hin_ref[...]
        hin_b = hin_f.astype(BF16)
        dyv = dy_ref[...]
        dst = dst_ref[...]
        dst_b = dst.astype(BF16)
        xs, bm, cm, xdt = v["xs"], v["bm"], v["cm"], v["xdt"]
        bm_b, cm_b = bm.astype(BF16), cm.astype(BF16)

        dd_e = jnp.sum(dyv * xs, axis=0, keepdims=True)
        dxs = dyv * v["d_e"]
        ch = _dot(cm_b, hin_b)
        dch = (dyv * v["ecs_e"]).astype(BF16)
        dcm = _dot_nt(dch, hin_b)
        dhin = _dot_tn(cm_b, dch) + dst * v["elast_e"]
        dcs_e = dyv * ch * v["ecs_e"]
        dxd = _dot(bm_b, dst_b)
        dbm = _dot_nt((xdt * v["decay_e"]).astype(BF16), dst_b)
        dxdt_state = dxd * v["decay_e"]
        q = dxdt_state * xdt
        dcs_e = dcs_e - q
        dlast_e = jnp.sum(q, axis=0, keepdims=True) + jnp.sum(dst * hin_f, axis=0, keepdims=True) * v["elast_e"]
        dg = jnp.zeros((T, T), F32)
        rs_cols = jnp.zeros((T, 128), F32)
        cs_rows = jnp.zeros((128, T), F32)
        lane_i, sub_i = _iota((T, 128), 1), _iota((128, T), 0)
        dxdt_parts = []
        for jp in range(4):
            l0, l1 = _ssd_decay_pair(v, jp)
            m0, m1 = v["gmat"] * l0, v["gmat"] * l1
            xbd = _block_diag_pair(xdt[:, 128 * jp:128 * jp + 128]).astype(BF16)
            dyp = dyv[:, 128 * jp:128 * jp + 128]
            dm = _dot_nt(dyp.astype(BF16), xbd)
            dm0, dm1 = dm[:, 0:T], dm[:, T:2 * T]
            dg = dg + dm0 * l0 + dm1 * l1
            for j, qq in ((2 * jp, dm0 * m0), (2 * jp + 1, dm1 * m1)):
                rs_cols = jnp.where(lane_i == j, jnp.sum(qq, axis=1, keepdims=True), rs_cols)
                cs_rows = jnp.where(sub_i == j, jnp.sum(qq, axis=0, keepdims=True), cs_rows)
            mv = jnp.concatenate([m0, m1], axis=0).astype(BF16)
            dxdt_parts.append(_dot_tn(mv, _block_diag_pair(dyp).astype(BF16)))
        dxdt = jnp.concatenate(dxdt_parts, axis=1) + dxdt_state
        dg_b = dg.astype(BF16)
        dcm = dcm + _dot(dg_b, bm_b)
        dbm = dbm + _dot_tn(dg_b, cm_b)
        expand_t = v["expand_t"]
        dcs_loc = rs_cols - cs_rows.T
        last_row = _iota((T, 1), 0) == T - 1
        dcs_full_e = dcs_e + jnp.where(last_row, dlast_e, 0.0)
        dcs = _xdot(dcs_full_e, expand_t) + _xdot(dcs_loc, v["select_t"])
        triu = (_iota((T, T), 0) <= _iota((T, T), 1)).astype(BF16)
        da = _xdot_l(triu, dcs)
        ddt = da * v["a_neg"] + _xdot(dxdt * xs, expand_t)
        dxs = dxs + dxdt * v["dt_e"]
        ddtr = jnp.where(v["valid"], ddt * _sigmoid(v["dtr"]), 0.0)
        dx_ref[...] = dxs
        db_ref[...] = jnp.where(v["valid"], dbm, 0.0)
        dc_ref[...] = jnp.where(v["valid"], dcm, 0.0)
        ddt_ref[...] = ddtr
        dpar = jnp.concatenate([
            jnp.sum(ddtr, axis=0, keepdims=True),
            jnp.sum(da * v["dt"], axis=0, keepdims=True) * v["a_neg"],
            _xdot(dd_e, expand_t),
            jnp.zeros((5, 128), F32)], axis=0)

        @pl.when(step == 0)
        def _():
            dpar_ref[...] = dpar

        @pl.when(step > 0)
        def _():
            dpar_ref[...] += dpar

        dst_ref[...] = dhin

    return _call(
        body, "ssd_bwd", (SSM_GROUPS // per, n_chunks), in_specs,
        [pl.BlockSpec((T, per * GROUP_W), lambda g, c: (cidx(c), g)),
         pl.BlockSpec((T, per * D_STATE), lambda g, c: (cidx(c), g)),
         pl.BlockSpec((T, per * D_STATE), lambda g, c: (cidx(c), g)),
         pl.BlockSpec((T, dt_w), lambda g, c: (cidx(c), P_DT // dt_w)),
         pl.BlockSpec((per, 8, 128), lambda g, c: (g, 0, 0))],
        [jax.ShapeDtypeStruct((n_rows, D_INNER), F32),
         jax.ShapeDtypeStruct((n_rows, SSM_GROUPS * D_STATE), F32),
         jax.ShapeDtypeStruct((n_rows, SSM_GROUPS * D_STATE), F32),
         jax.ShapeDtypeStruct(dp.shape, dp.dtype),
         jax.ShapeDtypeStruct((SSM_GROUPS, 8, 128), F32)],
        [p, p, p, p, p, p, p, conv_w, conv_w, conv_w, conv_b, conv_b, conv_b, dt_bias, a_log, d_skip, hin, dy, dp],
        scratch_shapes=[pltpu.VMEM((per, D_STATE, GROUP_W), F32), pltpu.VMEM((per, T + 8, GROUP_W), F32),
                        pltpu.VMEM((T, per * 128), F32)],
        aliases={18: 3}, bg=bg)


def _alibi_slope(h):
    return 2.0 ** (-8.0 * (h + 1) / ATTN_HEADS)


def _dup_half(x256, kvh):
    xb = x256[:, 128 * (kvh // 2):128 * (kvh // 2) + 128]
    rolled = pltpu.roll(xb, 64, 1)
    lane = _iota(xb.shape, 1)
    if kvh % 2 == 0:
        return jnp.where(lane < 64, xb, rolled)
    return jnp.where(lane < 64, rolled, xb)


def _attn_masks(c):
    qi, j = _iota((T, T), 0), _iota((T, T), 1)
    tri = j <= qi
    meta_ok = (j >= PAD) & (j - PAD <= c * T + qi - PAD)
    band_ok = c >= jnp.where(tri, 1, 2)
    dist = jnp.bitwise_and(qi - j, T - 1).astype(F32)
    return tri, meta_ok, band_ok, dist


def _fold(x3, tri):
    return jnp.concatenate([x3[:, 0:T], jnp.where(tri, x3[:, 2 * T:3 * T], x3[:, T:2 * T])], axis=1)


def _unfold(x2, tri):
    band = x2[:, T:2 * T]
    return jnp.concatenate([x2[:, 0:T], jnp.where(tri, 0.0, band), jnp.where(tri, band, 0.0)], axis=1)


def _attn_scores(qp, k3, masks, h0):
    tri, meta_ok, band_ok, dist = masks
    lane = _iota(qp.shape, 1)
    s = []
    for half, h in ((0, h0), (1, h0 + 1)):
        qh = jnp.where((lane < 64) if half == 0 else (lane >= 64), qp, 0.0).astype(BF16)
        raw = _dot_nt(qh, k3)
        band = jnp.where(tri, raw[:, 2 * T:3 * T], raw[:, T:2 * T]) - _alibi_slope(h) * dist
        s.append((qh, jnp.concatenate([jnp.where(meta_ok, raw[:, 0:T], NEG), jnp.where(band_ok, band, NEG)], axis=1)))
    return s


def _attn_fwd(p, sinks, n_chunks, bg=None):
    n_rows = n_chunks * T
    kb, vb = P_K // KV_W, P_V // KV_W

    def body(q_ref, kc_ref, kp_ref, km_ref, vc_ref, vp_ref, vm_ref, sink_ref, o_ref, lse_ref):
        c = pl.program_id(0)
        sinks_v = sink_ref[...]
        masks = _attn_masks(c)
        tri, meta_ok, band_ok, dist = masks
        lane = _iota((T, 128), 1)
        for kvh in range(KV_HEADS):
            k3 = jnp.concatenate([_dup_half(r[...], kvh) for r in (km_ref, kp_ref, kc_ref)], axis=0).astype(BF16)
            v3 = jnp.concatenate([_dup_half(r[...], kvh) for r in (vm_ref, vp_ref, vc_ref)], axis=0)
            v3bd = _block_diag_rows(v3).astype(BF16)
            q2 = q_ref[:, 256 * kvh:256 * kvh + 256] * SCALE
            q4 = jnp.concatenate([jnp.where((lane < 64) if half == 0 else (lane >= 64), q2[:, 128 * pr:128 * pr + 128], 0.0)
                                  for pr in range(2) for half in range(2)], axis=0).astype(BF16)
            raw4 = _dot_nt(q4, k3)
            probs = []
            for hh in range(4):
                h = 4 * kvh + hh
                raw = raw4[T * hh:T * hh + T]
                band = jnp.where(tri, raw[:, 2 * T:3 * T], raw[:, T:2 * T]) - _alibi_slope(h) * dist
                sc = jnp.concatenate([jnp.where(meta_ok, raw[:, 0:T], NEG), jnp.where(band_ok, band, NEG)], axis=1)
                sink = sinks_v[:, h:h + 1]
                m = jnp.maximum(jnp.max(sc, axis=1, keepdims=True), sink)
                e = jnp.exp(sc - m)
                den = jnp.sum(e, axis=1, keepdims=True) + jnp.exp(sink - m)
                probs.append(_unfold(e * (1.0 / den), tri))
                lse_ref[:, h:h + 1] = m + jnp.log(den)
            p4 = jnp.concatenate([jnp.concatenate(probs[0:2], axis=1), jnp.concatenate(probs[2:4], axis=1)], axis=0)
            out = _dot(p4.astype(BF16), v3bd)
            o_ref[:, 256 * kvh:256 * kvh + 256] = jnp.concatenate([out[0:T], out[T:2 * T]], axis=1).astype(o_ref.dtype)

    blk = lambda width, col: pl.BlockSpec((T, width), lambda c: (c, col))
    prev = lambda width, col: pl.BlockSpec((T, width), lambda c: (jnp.maximum(c - 1, 0), col))
    first = lambda width, col: pl.BlockSpec((T, width), lambda c: (0, col))
    return _call(
        body, "attn_fwd", (n_chunks,),
        [blk(ATTN_W, P_Q // ATTN_W), blk(KV_W, kb), prev(KV_W, kb), first(KV_W, kb),
         blk(KV_W, vb), prev(KV_W, vb), first(KV_W, vb), pl.BlockSpec((1, 128), lambda c: (0, 0))],
        [pl.BlockSpec((T, ATTN_W), lambda c: (c, 0)), pl.BlockSpec((T, 128), lambda c: (c, 0))],
        [jax.ShapeDtypeStruct((n_rows, ATTN_W), BF16), jax.ShapeDtypeStruct((n_rows, 128), F32)],
        [p, p, p, p, p, p, p, sinks], bg=bg)


def _block_diag_rows(x3):
    lane = _iota(x3.shape, 1)
    return jnp.concatenate([jnp.where(lane < 64, x3, 0.0), jnp.where(lane >= 64, x3, 0.0)], axis=0)


def _fold_halves(x):
    return x + pltpu.roll(x, 64, 1)


def _attn_bwd(p, sinks, ao, lse, dao, dp, n_chunks, bg=None):
    kb, vb = P_K // KV_W, P_V // KV_W
    rc = lambda s: n_chunks - 1 - s

    def body(q_ref, kc_ref, kp_ref, km_ref, vc_ref, vp_ref, vm_ref, sink_ref, o_ref, lse_ref, do_ref, dp_in_ref,
             dqkv_ref, dsink_ref, kcar_ref, vcar_ref, kmeta_ref, vmeta_ref):
        step = pl.program_id(0)
        c = n_chunks - 1 - step

        @pl.when(step == 0)
        def _():
            for r in (kcar_ref, vcar_ref, kmeta_ref, vmeta_ref):
                r[...] = jnp.zeros_like(r)

        masks = _attn_masks(c)
        tri = masks[0]
        q = q_ref[...] * SCALE
        sinks_v = sink_ref[...]
        lse_v = lse_ref[...]
        ov = o_ref[...].astype(F32)
        dov = do_ref[...].astype(F32)
        lane = _iota((T, 128), 1)
        lane256 = _iota((3 * T, KV_W), 1)
        dsink = jnp.zeros((1, 128), F32)
        dk3_all = jnp.zeros((3 * T, KV_W), F32)
        dv3_all = jnp.zeros((3 * T, KV_W), F32)
        dqs = []
        for kvh in range(KV_HEADS):
            k3 = jnp.concatenate([_dup_half(r[...], kvh) for r in (km_ref, kp_ref, kc_ref)], axis=0).astype(BF16)
            v3 = jnp.concatenate([_dup_half(r[...], kvh) for r in (vm_ref, vp_ref, vc_ref)], axis=0).astype(BF16)
            dk3 = jnp.zeros((3 * T, 128), F32)
            dv3 = jnp.zeros((3 * T, 128), F32)
            for pr in range(2):
                h0 = 4 * kvh + 2 * pr
                blk = 2 * kvh + pr
                qp = q[:, 128 * blk:128 * blk + 128]
                dop = dov[:, 128 * blk:128 * blk + 128]
                prod = dop * ov[:, 128 * blk:128 * blk + 128]
                dq_pair = jnp.zeros((T, 128), F32)
                for half, ((qh, sc), h) in enumerate(zip(_attn_scores(qp, k3, masks, h0), (h0, h0 + 1))):
                    mine = (lane < 64) if half == 0 else (lane >= 64)
                    lse_h = lse_v[:, h:h + 1]
                    pm = jnp.exp(sc - lse_h)
                    doh = jnp.where(mine, dop, 0.0).astype(BF16)
                    delta = jnp.sum(jnp.where(mine, prod, 0.0), axis=1, keepdims=True)
                    dp = _fold(_dot_nt(doh, v3), tri)
                    ds = _unfold(pm * (dp - delta), tri).astype(BF16)
                    p_sink = jnp.exp(sinks_v[:, h:h + 1] - lse_h)
                    dsink = jnp.where(_iota((1, 128), 1) == h, jnp.sum(-p_sink * delta, axis=0, keepdims=True), dsink)
                    dq_pair = jnp.where(mine, _dot(ds, k3), dq_pair)
                    dk3 = dk3 + _dot_tn(ds, qh)
                    dv3 = dv3 + _dot_tn(_unfold(pm, tri).astype(BF16), doh)
                dqs.append(dq_pair * SCALE)
            in_place = (lane256 >= 64 * kvh) & (lane256 < 64 * kvh + 64)
            wide = lambda x: jnp.concatenate([x, x], axis=1)
            dk3_all = jnp.where(in_place, wide(_fold_halves(dk3)), dk3_all)
            dv3_all = jnp.where(in_place, wide(_fold_halves(dv3)), dv3_all)
        dsink_all = dsink

        @pl.when(step == 0)
        def _():
            dsink_ref[...] = dsink_all

        @pl.when(step > 0)
        def _():
            dsink_ref[...] += dsink_all

        kmeta = kmeta_ref[...] + dk3_all[0:T]
        vmeta = vmeta_ref[...] + dv3_all[0:T]
        kmeta_ref[...] = kmeta
        vmeta_ref[...] = vmeta
        is_first = c == 0
        dk = jnp.where(is_first, kmeta, dk3_all[2 * T:3 * T] + kcar_ref[...])
        dv = jnp.where(is_first, vmeta, dv3_all[2 * T:3 * T] + vcar_ref[...])
        dqkv_ref[...] = jnp.concatenate(dqs + [dk, dv], axis=1).astype(dqkv_ref.dtype)
        kcar_ref[...] = dk3_all[T:2 * T]
        vcar_ref[...] = dv3_all[T:2 * T]

    blk = lambda width, col: pl.BlockSpec((T, width), lambda s: (rc(s), col))
    prev = lambda width, col: pl.BlockSpec((T, width), lambda s: (jnp.maximum(rc(s) - 1, 0), col))
    first = lambda width, col: pl.BlockSpec((T, width), lambda s: (0, col))
    return _call(
        body, "attn_bwd", (n_chunks,),
        [blk(ATTN_W, P_Q // ATTN_W), blk(KV_W, kb), prev(KV_W, kb), first(KV_W, kb),
         blk(KV_W, vb), prev(KV_W, vb), first(KV_W, vb), pl.BlockSpec((1, 128), lambda s: (0, 0)),
         blk(ATTN_W, 0), blk(128, 0), blk(ATTN_W, 0), ANY],
        [blk(QKV_W, P_Q // QKV_W), pl.BlockSpec((1, 128), lambda s: (0, 0))],
        [jax.ShapeDtypeStruct(dp.shape, dp.dtype), jax.ShapeDtypeStruct((1, 128), F32)],
        [p, p, p, p, p, p, p, sinks, ao, lse, dao, dp],
        scratch_shapes=[pltpu.VMEM((T, KV_W), F32)] * 4, aliases={11: 0}, bg=bg)


def _pad_lanes(v, width=128):
    return jnp.pad(v, ((0, 0), (0, width - v.shape[1])))


def _local_step(x, head, tgt, plan):
    w, g, run = plan.w, plan.g, plan.run
    n_tok = x.shape[0]
    n_rows = n_tok + T
    n_chunks = n_rows // T
    tm = _row_tile(n_rows, 384)
    dt_bias, a_log, d_skip = (_pad_lanes(w[k]) for k in ("ssm_dt_bias", "ssm_a_log", "ssm_d_skip"))
    sinks = _pad_lanes(w["attn_sinks"])
    x_in = [(x, D_MODEL, 0, "prev"), (head, D_MODEL, 0, "first")]

    def h0_tile(r0, xt, hd):
        return jnp.where(r0 < T, hd, xt)

    n1, = _rowwise("norm_pre_mix", lambda r0, xt, hd, wn: [_rms(h0_tile(r0, xt, hd), wn)], n_rows, T,
                   x_in, [w["norm_pre_mix"]], [(D_MODEL, BF16)], [])
    p = _matmul("in_proj", n1, w["w_cat"], "nn", F32)
    y_ssd, hin = run("ssd_fwd", _ssd_fwd, p, w["ssm_conv_w"], w["ssm_conv_b"], dt_bias, a_log, d_skip, n_chunks)
    ao, lse = run("attn_fwd", _attn_fwd, p, sinks, n_chunks)

    def gate_norm(r0, y, z, wn):
        return [_rms(y * _silu(z), wn)]

    yn, = run("ssm_gate_norm", _rowwise, "ssm_gate_norm", gate_norm, n_rows, tm,
              [(y_ssd, D_INNER, 0), (p, D_INNER, P_Z // D_INNER)], [w["ssm_norm"]], [(D_INNER, BF16)], [])
    y_ssm = _matmul("ssm_out", yn, w["w_ssm_out"], "nn", F32)
    y_attn = _matmul("attn_out", ao, w["w_attn_out"], "nn", F32)

    def mix_gate(r0, ys, ya, gs, ga):
        return [_sigmoid(gs) * ys + _sigmoid(ga) * ya]

    gate_ins = [(p, D_MODEL, P_GATE // D_MODEL), (p, D_MODEL, P_GATE // D_MODEL + 1)]
    mixed, = _rowwise("mix_gate", mix_gate, n_rows, tm, [(y_ssm, D_MODEL, 0), (y_attn, D_MODEL, 0)] + gate_ins,
                      [], [(D_MODEL, BF16)], [])
    mix = _matmul("mix_out", mixed, w["w_mix_out"], "nn", F32)

    def post_mix(r0, mx, xt, hd, w_post, w_pre):
        h1 = jnp.where(_valid_rows(r0, mx.shape[0], PAD), h0_tile(r0, xt, hd) + _rms(mx, w_post), 0.0)
        return [h1, _rms(h1, w_pre)]

    h1, n2 = _rowwise("post_mix", post_mix, n_rows, T, [(mix, D_MODEL, 0)] + x_in,
                      [w["norm_post_mix"], w["norm_pre_ffn"]], [(D_MODEL, F32), (D_MODEL, BF16)], [])
    u_raw = _matmul("ffn_up", n2, w["w_ffn_up"], "nn", F32)
    f = _ffn_act("ffn_act", u_raw, w["ffn_conv_w"], w["ffn_conv_b"], n_rows)
    ffn = _matmul("ffn_down", f, w["w_ffn_down"], "nn", F32)

    def final(r0, fo, h, t, w_post):
        real = r0 >= T
        err = jnp.where(real, h + _rms(fo, w_post) - t, 0.0)
        dy = err * (1.0 / D_MODEL)
        dffn, dw = _rms_bwd(dy, fo, w_post)
        return [dffn, dy, jnp.sum(err * err, axis=0, keepdims=True), dw]

    dffn, dh2, loss_cols, g_norm_post_ffn = _rowwise(
        "loss_head", final, n_rows, T, [(ffn, D_MODEL, 0), (h1, D_MODEL, 0), (tgt, D_MODEL, 0, "prev")],
        [w["norm_post_ffn"]], [(D_MODEL, BF16), (D_MODEL, F32)], [D_MODEL, D_MODEL])

    g["norm_post_ffn"] = g_norm_post_ffn
    g["w_ffn_down"] = _matmul("ffn_down_dw", f, dffn, "tn", F32)
    df = _matmul("ffn_down_dx", dffn, w["w_ffn_down"], "nt", F32)
    du_raw, dconv = _conv_bwd("ffn_act_bwd", u_raw, 0, [df], [(0, c0) for c0 in range(0, FFN_DIM, CONV_LANES)],
                              w["ffn_conv_w"], w["ffn_conv_b"], n_rows, True)
    g["ffn_conv_w"], g["ffn_conv_b"] = dconv[0:3], dconv[3:4]
    g["w_ffn_up"] = _matmul("ffn_up_dw", n2, du_raw, "tn", F32)
    dn2 = run("ffn_up_dx", _matmul, "ffn_up_dx", du_raw, w["w_ffn_up"], "nt", F32)

    def post_mix_bwd(r0, dn, d2, h, mx, w_pre, w_post):
        dx, dw_pre = _rms_bwd(dn, h, w_pre)
        dh1 = jnp.where(_valid_rows(r0, dn.shape[0], PAD), dx + d2, 0.0)
        dmix, dw_post = _rms_bwd(dh1, mx, w_post)
        return [dh1, dmix, dw_pre, dw_post]

    dh1, dmix, g["norm_pre_ffn"], g["norm_post_mix"] = _rowwise(
        "post_mix_bwd", post_mix_bwd, n_rows, tm,
        [(dn2, D_MODEL, 0), (dh2, D_MODEL, 0), (h1, D_MODEL, 0), (mix, D_MODEL, 0)],
        [w["norm_pre_ffn"], w["norm_post_mix"]], [(D_MODEL, F32), (D_MODEL, BF16)], [D_MODEL, D_MODEL])
    g["w_mix_out"] = _matmul("mix_out_dw", mixed, dmix, "tn", F32)
    dmixed = _matmul("mix_out_dx", dmix, w["w_mix_out"], "nt", F32)

    def mix_gate_bwd(r0, dm, ys, ya, gs, ga):
        ss, sa = _sigmoid(gs), _sigmoid(ga)
        dgate = jnp.concatenate([dm * ys * ss * (1.0 - ss), dm * ya * sa * (1.0 - sa)], axis=1)
        return [dm * ss, dm * sa, dgate]

    dys, dya, dp = _rowwise(
        "mix_gate_bwd", mix_gate_bwd, n_rows, tm,
        [(dmixed, D_MODEL, 0), (y_ssm, D_MODEL, 0), (y_attn, D_MODEL, 0)] + gate_ins,
        [], [(D_MODEL, BF16), (D_MODEL, BF16), (2 * D_MODEL, BF16, "new", P_W, P_GATE // (2 * D_MODEL))], [])
    g["w_ssm_out"] = _matmul("ssm_out_dw", yn, dys, "tn", F32)
    dyn = _matmul("ssm_out_dx", dys, w["w_ssm_out"], "nt", F32)
    g["w_attn_out"] = _matmul("attn_out_dw", ao, dya, "tn", F32)
    dao = _matmul("attn_out_dx", dya, w["w_attn_out"], "nt", BF16)

    def gate_norm_bwd(r0, dn, y, z, wn):
        sz, dsz = _silu_grad(z)
        dyz, dw = _rms_bwd(dn, y * sz, wn)
        live = _valid_rows(r0, dn.shape[0], PAD)
        return [jnp.where(live, dyz * sz, 0.0), jnp.where(live, dyz * y * dsz, 0.0), dw]

    dy_ssd, dp, g["ssm_norm"] = run(
        "ssm_gate_norm_bwd", _rowwise, "ssm_gate_norm_bwd", gate_norm_bwd, n_rows, tm,
        [(dyn, D_INNER, 0), (y_ssd, D_INNER, 0), (p, D_INNER, P_Z // D_INNER)],
        [w["ssm_norm"]], [(D_INNER, F32), (D_INNER, BF16, "into", dp, P_Z // D_INNER)], [D_INNER])
    dp, dsink = run("attn_bwd", _attn_bwd, p, sinks, ao, lse, dao, dp, n_chunks)
    g["attn_sinks"] = dsink[:, 0:ATTN_HEADS]
    dxs, dbm, dcm, dp, dpar = run("ssd_bwd", _ssd_bwd, p, w["ssm_conv_w"], w["ssm_conv_b"], dt_bias, a_log,
                                  d_skip, hin, dy_ssd, dp, n_chunks)
    dpar = jnp.sum(dpar, axis=0)
    g["ssm_dt_bias"], g["ssm_a_log"], g["ssm_d_skip"] = (dpar[i:i + 1, 0:SSM_HEADS] for i in range(3))
    x_chunks = [(src, c0) for src, arr in enumerate((dxs, dbm, dcm)) for c0 in range(0, arr.shape[1], CONV_LANES)]
    dp, dconv = run("ssm_conv_bwd", _conv_bwd, "ssm_conv_bwd", p, P_XBC // CONV_DIM, [dxs, dbm, dcm], x_chunks,
                    w["ssm_conv_w"], w["ssm_conv_b"], n_rows, False, into=dp, into_blk=P_XBC // CONV_DIM)
    g["ssm_conv_w"], g["ssm_conv_b"] = dconv[0:4], dconv[4:5]
    g["w_cat_t"] = _matmul("in_proj_dw", dp, n1, "tn", F32)
    dn1 = run("in_proj_dx", _matmul, "in_proj_dx", dp, w["w_cat"], "nt", F32)

    def pre_mix_bwd(r0, dn, d1, xt, hd, wn):
        dx, dw = _rms_bwd(dn, h0_tile(r0, xt, hd), wn)
        dh0 = jnp.where(_valid_rows(r0, dn.shape[0], PAD), dx + d1, 0.0)
        return [dh0, dh0, dw]

    dx_out, dhead, g["norm_pre_mix"] = _rowwise(
        "pre_mix_bwd", pre_mix_bwd, n_rows, T, [(dn1, D_MODEL, 0), (dh1, D_MODEL, 0)] + x_in,
        [w["norm_pre_mix"]], [(D_MODEL, F32, "prev", n_tok), (D_MODEL, F32, "first")], [D_MODEL])
    return jnp.sum(loss_cols), dx_out, dhead


_IN_SECTIONS = [((5152, 6176), P_Q), ((6176, 6432), P_K), ((6432, 6688), P_V), ((5120, 5152), P_DT),
                ((0, 2048), P_Z), ((6688, 8736), P_GATE), ((2048, 5120), P_XBC)]


IN_SHARD = N_IN // 4


def _shard_pieces(a, b):
    return [(j, max(a, j * IN_SHARD) - j * IN_SHARD, min(b, (j + 1) * IN_SHARD) - j * IN_SHARD)
            for j in range(4) if max(a, j * IN_SHARD) < min(b, (j + 1) * IN_SHARD)]


def _to_cat(w4):
    parts, at = [], 0
    for (a, b), off in _IN_SECTIONS:
        if off > at:
            parts.append(jnp.zeros((w4.shape[1], off - at), w4.dtype))
        parts += [w4[j, :, lo:hi] for j, lo, hi in _shard_pieces(a, b)]
        at = off + (b - a)
    return jnp.concatenate(parts, axis=1)


def _from_cat_t(g_cat_t):
    shards = [[] for _ in range(4)]
    for (a, b), off in sorted(_IN_SECTIONS):
        for j, lo, hi in _shard_pieces(a, b):
            start = off + j * IN_SHARD + lo - a
            shards[j].append(g_cat_t[start:start + hi - lo])
    return jnp.stack([jnp.concatenate(s, axis=0) for s in shards])


LANES = 1024
_BIG = [("w_in", 1024, 2184, "chip"), ("w_ssm_out", 512, 1024, "row"), ("w_attn_out", 256, 1024, "row"),
        ("w_mix_out", 256, 1024, "row"), ("w_ffn_up", 1024, 1408, "col"), ("w_ffn_down", 704, 1024, "row"),
        ("small", 32, LANES, "chip")]
_SMALL_SHARDED = [("ssm_conv_w", (4, 768), 1), ("ffn_conv_w", (3, 1408), 1), ("meta_tokens", (16, 256), 1)]
_REPLICATED = [("norm_pre_mix", 1024), ("ssm_conv_b", 3072), ("ssm_dt_bias", 32), ("ssm_a_log", 32),
               ("ssm_d_skip", 32), ("ssm_norm", 2048), ("attn_sinks", 16), ("norm_post_mix", 1024),
               ("norm_pre_ffn", 1024), ("ffn_conv_b", 5632), ("norm_post_ffn", 1024)]
SMALL_ROWS = 24


def _rep_rows():
    out, at = [], 0
    for _, width in _REPLICATED:
        out.append((at, -(-width // LANES)))
        at += out[-1][1]
    return out, at


def _in_rows(parts):
    rows = [jnp.pad(a, ((0, 0), (0, -a.shape[1] % LANES))).reshape(-1, LANES) for a in parts]
    flat = jnp.concatenate(rows, axis=0)
    return jnp.pad(flat, ((0, SMALL_ROWS - flat.shape[0]), (0, 0)))
WEIGHT_ORDER = ["meta_tokens", "norm_pre_mix", "w_in", "ssm_conv_w", "ssm_conv_b", "ssm_dt_bias", "ssm_a_log",
                "ssm_d_skip", "ssm_norm", "w_ssm_out", "attn_sinks", "w_attn_out", "w_mix_out", "norm_post_mix",
                "norm_pre_ffn", "w_ffn_up", "ffn_conv_w", "ffn_conv_b", "w_ffn_down", "norm_post_ffn"]


def _flatten(parts, rows):
    flat = jnp.concatenate([a.reshape(-1) for a in parts])
    return jnp.pad(flat, (0, rows * LANES - flat.shape[0])).reshape(rows, LANES)


def _unflatten(flat, shapes):
    flat = flat.reshape(-1)
    out, off = [], 0
    for shp in shapes:
        n = math.prod(shp)
        out.append(flat[off:off + n].reshape(shp))
        off += n
    return out


def _shard_of(full, chip, shape, axis):
    return lax.slice_in_dim(full, chip * shape[axis], (chip + 1) * shape[axis], axis=axis)


def _full_shape(r, c, layout):
    return {"row": (4 * r, c), "col": (r, 4 * c), "chip": (4, r, c), "chip_cols": (4, r, c)}[layout]


def _half_shape(r, c, layout):
    return (r, c // 2) if layout == "chip_cols" else (r // 2, c)


def _shard_view(ref, r, c, layout, chip):
    if layout == "row":
        return ref.at[pl.ds(pl.multiple_of(chip * r, 16), r), :]
    if layout == "col":
        return ref.at[:, pl.ds(pl.multiple_of(chip * c, 128), c)]
    return ref.at[chip]


def _half_view(ref, r, c, layout, chip, half):
    if layout == "chip_cols":
        return ref.at[chip, :, pl.ds(pl.multiple_of(half * (c // 2), 128), c // 2)]
    hr = r // 2
    if layout == "row":
        return ref.at[pl.ds(pl.multiple_of(chip * r + half * hr, 16), hr), :]
    r0 = pl.multiple_of(half * hr, 16)
    if layout == "col":
        return ref.at[pl.ds(r0, hr), pl.ds(pl.multiple_of(chip * c, 128), c)]
    return ref.at[chip, pl.ds(r0, hr), :]


def _mesh_pos():
    return lax.axis_index("x"), lax.axis_index("y"), lax.axis_index("c")


def _other_chips(x, y):
    return [(1 - x, y), (x, 1 - y), (1 - x, 1 - y)]


def _chip_index(x, y):
    return 2 * x + y


def _run_exchange(name, ex):
    n_in, n_out = len(ex.ins), len(ex.out_shapes)

    def body(*refs):
        in_refs, out_refs = refs[:n_in], refs[n_in:n_in + n_out]
        send_sems, recv_sems = refs[n_in + n_out:]
        copies = [pltpu.make_async_remote_copy(src_ref=s, dst_ref=d, send_sem=send_sems.at[i], recv_sem=recv_sems.at[i],
                                               device_id=dev, device_id_type=MESH)
                  for i, (s, d, dev) in enumerate(ex.make_copies(in_refs, out_refs))]
        assert len(copies) == ex.n_copies
        for cp in copies:
            cp.start()
        for cp in copies:
            cp.wait()

    return pl.pallas_call(
        body, name=name, in_specs=[ANY] * n_in, out_specs=[ANY] * n_out, out_shape=list(ex.out_shapes),
        scratch_shapes=[pltpu.SemaphoreType.DMA((ex.n_copies,)), pltpu.SemaphoreType.DMA((ex.n_copies,))],
        compiler_params=pltpu.CompilerParams(has_side_effects=True),
    )(*ex.ins)


def _join(*exs):
    def make(in_refs, out_refs):
        copies, i0, o0 = [], 0, 0
        for ex in exs:
            copies += ex.make_copies(in_refs[i0:i0 + len(ex.ins)], out_refs[o0:o0 + len(ex.out_shapes)])
            i0, o0 = i0 + len(ex.ins), o0 + len(ex.out_shapes)
        return copies

    aliases, i0, o0 = {}, 0, 0
    for ex in exs:
        aliases.update({i0 + k: o0 + v for k, v in ex.aliases.items()})
        i0, o0 = i0 + len(ex.ins), o0 + len(ex.out_shapes)
    return _Exchange([a for ex in exs for a in ex.ins], [s for ex in exs for s in ex.out_shapes], make,
                     sum(ex.n_copies for ex in exs), aliases)


def _split(exs, results):
    out, o0 = [], 0
    for ex in exs:
        out.append(list(results[o0:o0 + len(ex.out_shapes)]))
        o0 += len(ex.out_shapes)
    return out


def _gather_ici(entries, shards):
    def make(in_refs, out_refs):
        x, y, c = _mesh_pos()
        j = _chip_index(x, y)
        copies = []
        for ref_in, ref_out, (_, r, cc, lay) in zip(in_refs, out_refs, entries):
            copies.append((ref_in, _shard_view(ref_out, r, cc, lay, j), None))
            mine = ref_in.at[pl.ds(pl.multiple_of(c * (r // 2), 16), r // 2), :]
            copies += [(mine, _half_view(ref_out, r, cc, lay, j, c), (*ch, c)) for ch in _other_chips(x, y)]
        return copies

    shapes = [jax.ShapeDtypeStruct(_full_shape(r, cc, lay), s.dtype) for s, (_, r, cc, lay) in zip(shards, entries)]
    return _Exchange(list(shards), shapes, make, 4 * len(entries))


def _gather_pass_on(entries, fulls):
    def make(in_refs, out_refs):
        x, y, c = _mesh_pos()
        copies = []
        for ref, (_, r, cc, lay) in zip(out_refs, entries):
            for ch in _other_chips(x, y):
                landed = _half_view(ref, r, cc, lay, _chip_index(*ch), c)
                copies.append((landed, landed, (x, y, 1 - c)))
        return copies

    return _Exchange(list(fulls), [jax.ShapeDtypeStruct(f.shape, f.dtype) for f in fulls], make, 3 * len(entries),
                     {a: a for a in range(len(entries))})


def _gather_weights(entries, shards):
    n = len(entries)

    def body(*refs):
        ins, outs = refs[:n], refs[n:2 * n]
        send_sems, recv_sems, local_sems = refs[2 * n:]
        x, y, c = _mesh_pos()
        j = _chip_index(x, y)
        sibling = (x, y, 1 - c)
        chips = _other_chips(x, y)
        idx = [_chip_index(*ch) for ch in chips]

        def remote(k, src, dst, dev):
            return pltpu.make_async_remote_copy(src_ref=src, dst_ref=dst, send_sem=send_sems.at[k],
                                                recv_sem=recv_sems.at[k], device_id=dev, device_id_type=MESH)

        own = [pltpu.make_async_copy(ins[a], _shard_view(outs[a], r, cc, lay, j), local_sems.at[a])
               for a, (_, r, cc, lay) in enumerate(entries)]
        for cp in own:
            cp.start()
        first, passed = [], []
        for a, (_, r, cc, lay) in enumerate(entries):
            mine = ins[a].at[pl.ds(pl.multiple_of(c * (r // 2), 16), r // 2), :]
            for k, ch in enumerate(chips):
                first.append(remote(6 * a + k, mine, _half_view(outs[a], r, cc, lay, j, c), (*ch, c)))
                landed = _half_view(outs[a], r, cc, lay, idx[k], c)
                passed.append(remote(6 * a + 3 + k, landed, landed, sibling))
        for cp in first:
            cp.start()
        for a, (_, r, cc, lay) in enumerate(entries):
            for k in range(3):
                landed = _half_view(outs[a], r, cc, lay, idx[k], c)
                remote(6 * a + k, landed, landed, sibling).wait_recv()
                passed[3 * a + k].start()
        for a, (_, r, cc, lay) in enumerate(entries):
            for k in range(3):
                theirs = _half_view(outs[a], r, cc, lay, idx[k], 1 - c)
                remote(6 * a + 3 + k, theirs, theirs, sibling).wait_recv()
        for cp in first + passed:
            cp.wait_send()
        for cp in own:
            cp.wait()

    return pl.pallas_call(
        body, name="gather_weights", in_specs=[ANY] * n, out_specs=[ANY] * n,
        out_shape=[jax.ShapeDtypeStruct(_full_shape(r, cc, lay), s.dtype) for s, (_, r, cc, lay) in zip(shards, entries)],
        scratch_shapes=[pltpu.SemaphoreType.DMA((6 * n,)), pltpu.SemaphoreType.DMA((6 * n,)), pltpu.SemaphoreType.DMA((n,))],
        compiler_params=pltpu.CompilerParams(has_side_effects=True),
    )(*shards)


def _pair_exchange(entries, grads):
    def make(in_refs, out_refs):
        x, y, c = _mesh_pos()
        return [(_half_view(ref_in, r, cc, lay, i, 1 - c), ref_out.at[i], (x, y, 1 - c))
                for ref_in, ref_out, (_, r, cc, lay) in zip(in_refs, out_refs, entries) for i in range(4)]

    return _Exchange(list(grads), [jax.ShapeDtypeStruct((4,) + _half_shape(r, cc, lay), F32) for _, r, cc, lay in entries],
                     make, 4 * len(entries))


def _whole_to_sibling(arrays):
    def make(in_refs, out_refs):
        x, y, c = _mesh_pos()
        return [(r, o, (x, y, 1 - c)) for r, o in zip(in_refs, out_refs)]

    return _Exchange(list(arrays), [jax.ShapeDtypeStruct(a.shape, a.dtype) for a in arrays], make, len(arrays))


def _chip_exchange(psends):
    def make(in_refs, out_refs):
        x, y, c = _mesh_pos()
        return [(ref_in.at[_chip_index(*ch)], ref_out.at[k], (*ch, c))
                for ref_in, ref_out in zip(in_refs, out_refs) for k, ch in enumerate(_other_chips(x, y))]

    return _Exchange(list(psends), [jax.ShapeDtypeStruct((3,) + p.shape[1:], p.dtype) for p in psends], make,
                     3 * len(psends))


def _to_all_chips(array):
    def make(in_refs, out_refs):
        x, y, c = _mesh_pos()
        return [(in_refs[0], out_refs[0].at[k], (*ch, c)) for k, ch in enumerate(_other_chips(x, y))]

    return _Exchange([array], [jax.ShapeDtypeStruct((3,) + array.shape, array.dtype)], make, 3)


SUM_ROWS = 256
ADAM_ROWS = 128


def _pair_sum(name, grad, recv, ids, r, c, layout):
    hr, c = _half_shape(r, c, layout)
    tr = _row_tile(hr, SUM_ROWS)
    nb = hr // tr

    def body(ids_ref, g_ref, r_ref, send_ref, own_ref):
        s = g_ref[...] + r_ref[...]
        send_ref[...] = s.astype(send_ref.dtype)

        @pl.when(pl.program_id(1) == ids_ref[1])
        def _():
            own_ref[...] = s

    if layout == "row":
        g_spec = pl.BlockSpec((tr, c), lambda t, j, ids_ref: ((j * r + ids_ref[0] * hr) // tr + t, 0))
    elif layout == "col":
        g_spec = pl.BlockSpec((tr, c), lambda t, j, ids_ref: (ids_ref[0] * nb + t, j))
    elif layout == "chip_cols":
        g_spec = pl.BlockSpec((None, tr, c), lambda t, j, ids_ref: (j, t, ids_ref[0]))
    else:
        g_spec = pl.BlockSpec((None, tr, c), lambda t, j, ids_ref: (j, ids_ref[0] * nb + t, 0))
    grid_spec = pltpu.PrefetchScalarGridSpec(
        num_scalar_prefetch=1, grid=(nb, 4),
        in_specs=[g_spec, pl.BlockSpec((None, tr, c), lambda t, j, ids_ref: (j, t, 0))],
        out_specs=[pl.BlockSpec((None, tr, c), lambda t, j, ids_ref: (j, t, 0)),
                   pl.BlockSpec((tr, c), lambda t, j, ids_ref: (t, 0))])
    return pl.pallas_call(
        body, name=name, grid_spec=grid_spec,
        out_shape=[jax.ShapeDtypeStruct((4, hr, c), BF16), jax.ShapeDtypeStruct((hr, c), F32)],
        compiler_params=_cparams(2),
    )(ids, grad, recv)


def _chip_sum(name, own, recv):
    hr, c = own.shape
    tr = _row_tile(hr, SUM_ROWS)

    def body(o_ref, r_ref, out_ref):
        out_ref[...] = ((o_ref[...] + r_ref[0].astype(F32)) + r_ref[1].astype(F32)) + r_ref[2].astype(F32)

    return pl.pallas_call(
        body, name=name, grid=(hr // tr,),
        in_specs=[pl.BlockSpec((tr, c), lambda i: (i, 0)), pl.BlockSpec((3, tr, c), lambda i: (0, i, 0))],
        out_specs=pl.BlockSpec((tr, c), lambda i: (i, 0)),
        out_shape=jax.ShapeDtypeStruct((hr, c), F32), compiler_params=_cparams(1),
    )(own, recv)


def _chip_sum_small(own, recv, ids):
    def body(ids_ref, o_ref, r_ref, out_ref):
        j = ids_ref[1]
        total = None
        for i in range(4):
            m = jnp.bitwise_xor(i, j)
            term = jnp.where(m == 0, o_ref[...], jnp.where(m == 2, r_ref[0], jnp.where(m == 1, r_ref[1], r_ref[2])))
            total = term if total is None else total + term
        out_ref[...] = total

    grid_spec = pltpu.PrefetchScalarGridSpec(
        num_scalar_prefetch=1, grid=(1,),
        in_specs=[pl.BlockSpec(own.shape, lambda i, ids_ref: (0, 0)), pl.BlockSpec(recv.shape, lambda i, ids_ref: (0, 0, 0))],
        out_specs=pl.BlockSpec(own.shape, lambda i, ids_ref: (0, 0)))
    return pl.pallas_call(body, name="chip_sum_small", grid_spec=grid_spec,
                          out_shape=jax.ShapeDtypeStruct(own.shape, F32), compiler_params=_cparams(1))(ids, own, recv)


def _adamw(name, w, m, v, mine, theirs, ids):
    lead = (None,) * (w.ndim - 2)
    rows, cols = w.shape[-2:]
    half = rows // 2
    tr = _row_tile(half, ADAM_ROWS, unit=8)
    nb = half // tr
    c1 = 1.0 / (1.0 - ADAM_B1 ** ADAM_STEP)
    c2 = 1.0 / (1.0 - ADAM_B2 ** ADAM_STEP)

    def body(ids_ref, w_ref, m_ref, v_ref, mine_ref, theirs_ref, g_out, d_out, m_out, v_out):
        g = jnp.where(pl.program_id(0) == ids_ref[0], mine_ref[...], theirs_ref[...])
        m_new = ADAM_B1 * m_ref[...] + (1.0 - ADAM_B1) * g
        v_new = ADAM_B2 * v_ref[...] + (1.0 - ADAM_B2) * (g * g)
        d_out[...] = -ADAM_LR * ((m_new * c1) / (jnp.sqrt(v_new * c2) + ADAM_EPS) + ADAM_WD * w_ref[...])
        g_out[...] = g
        m_out[...] = m_new
        v_out[...] = v_new

    full = pl.BlockSpec(lead + (tr, cols), lambda h, i, ids_ref: (0,) * len(lead) + (h * nb + i, 0))
    part = pl.BlockSpec((tr, cols), lambda h, i, ids_ref: (i, 0))
    grid_spec = pltpu.PrefetchScalarGridSpec(num_scalar_prefetch=1, grid=(2, nb),
                                             in_specs=[full, full, full, part, part], out_specs=[full] * 4)
    return pl.pallas_call(
        body, name=name, grid_spec=grid_spec,
        out_shape=[jax.ShapeDtypeStruct(w.shape, F32)] * 4, compiler_params=_cparams(2),
    )(ids, w, m, v, mine, theirs)


def _adamw_whole(name, w, m, v, g):
    rows, cols = w.shape[-2:]
    tr = _row_tile(rows, 2 * ADAM_ROWS, unit=8)
    c1 = 1.0 / (1.0 - ADAM_B1 ** ADAM_STEP)
    c2 = 1.0 / (1.0 - ADAM_B2 ** ADAM_STEP)

    def body(w_ref, m_ref, v_ref, g_ref, g_out, d_out, m_out, v_out):
        g = g_ref[...]
        m_new = ADAM_B1 * m_ref[...] + (1.0 - ADAM_B1) * g
        v_new = ADAM_B2 * v_ref[...] + (1.0 - ADAM_B2) * (g * g)
        d_out[...] = -ADAM_LR * ((m_new * c1) / (jnp.sqrt(v_new * c2) + ADAM_EPS) + ADAM_WD * w_ref[...])
        g_out[...] = g
        m_out[...] = m_new
        v_out[...] = v_new

    full = pl.BlockSpec((None, tr, cols), lambda i: (0, i, 0))
    return pl.pallas_call(
        body, name=name, grid=(rows // tr,), in_specs=[full, full, full, pl.BlockSpec((tr, cols), lambda i: (i, 0))],
        out_specs=[full] * 4, out_shape=[jax.ShapeDtypeStruct(w.shape, F32)] * 4, compiler_params=_cparams(1),
    )(w, m, v, g)


def _adamw_replicated(g_rows, ws, ms, vs):
    n = len(ws)
    layout, _ = _rep_rows()
    c1 = 1.0 / (1.0 - ADAM_B1 ** ADAM_STEP)
    c2 = 1.0 / (1.0 - ADAM_B2 ** ADAM_STEP)

    def body(g_ref, *refs):
        w_refs, m_refs, v_refs = refs[0:n], refs[n:2 * n], refs[2 * n:3 * n]
        outs = refs[3 * n:]
        for k, (r0, rows) in enumerate(layout):
            width = w_refs[k].shape[1]
            g = jnp.concatenate([g_ref[r0 + j:r0 + j + 1, :] for j in range(rows)], axis=1)[:, 0:width]
            m_new = ADAM_B1 * m_refs[k][...] + (1.0 - ADAM_B1) * g
            v_new = ADAM_B2 * v_refs[k][...] + (1.0 - ADAM_B2) * (g * g)
            outs[k][...] = g
            outs[n + k][...] = -ADAM_LR * ((m_new * c1) / (jnp.sqrt(v_new * c2) + ADAM_EPS) + ADAM_WD * w_refs[k][...])
            outs[2 * n + k][...] = m_new
            outs[3 * n + k][...] = v_new

    res = pl.pallas_call(body, name="adamw_replicated",
                         out_shape=[jax.ShapeDtypeStruct(w.shape, F32) for _ in range(4) for w in ws])(g_rows, *ws, *ms, *vs)
    return [res[k * n:(k + 1) * n] for k in range(4)]


def _small_shard(parts):
    return _flatten(parts, _BIG[-1][1])


_ENTRY = {e[0]: e for e in _BIG}
_GRAD_ENTRY = {**_ENTRY, "w_in": ("w_in", IN_SHARD, D_MODEL, "chip_cols")}
FFN_MATS = ("w_ffn_down", "w_ffn_up")
MIXER_MATS = ("w_mix_out", "w_ssm_out", "w_attn_out")


class _StepPlan:
    def __init__(self, w, late_shards, shards, ids):
        self.w, self.g = w, {}
        self.late_shards, self.shards, self.ids = late_shards, shards, ids
        self.sums, self.halves, self.results = {}, {}, {}

    def run(self, name, fn, *args, **kw):
        at = getattr(self, "_at_" + name, None)
        if at is None:
            return fn(*args, **kw)
        exchange, landed = at()
        res, extra = fn(*args, bg=exchange, **kw)
        landed(extra)
        return res

    def _at_ssd_fwd(self):
        def landed(fulls):
            self.partly_gathered = fulls

        return _gather_ici([_ENTRY[n] for n in MIXER_MATS], [self.late_shards[n] for n in MIXER_MATS]), landed

    def _at_attn_fwd(self):
        stages = (_gather_pass_on([_ENTRY[n] for n in MIXER_MATS], self.partly_gathered),
                  _gather_ici([_ENTRY[n] for n in FFN_MATS], [self.late_shards[n] for n in FFN_MATS]))

        def landed(extra):
            mixer, self.partly_gathered = _split(stages, extra)
            self.w.update(zip(MIXER_MATS, mixer))

        return _join(*stages), landed

    def _at_ssm_gate_norm(self):
        return (_gather_pass_on([_ENTRY[n] for n in FFN_MATS], self.partly_gathered),
                lambda fulls: self.w.update(zip(FFN_MATS, fulls)))

    def pair_sums(self, names, grads, recv):
        for n, gr, rv in zip(names, grads, recv):
            _, r, c, lay = _GRAD_ENTRY[n]
            self.sums[n] = _pair_sum("pair_sum_" + n, gr, rv, self.ids, r, c, lay)

    def chip_sums(self, names, recv):
        for n, rv in zip(names, recv):
            self.halves[n] = _chip_sum("chip_sum_" + n, self.sums[n][1], rv)

    def adamw(self, names, theirs):
        for n, th in zip(names, theirs):
            sh = self.shards[n]
            if n == "w_in":
                mine_first = self.ids[0] == 0
                g_t = jnp.where(mine_first, jnp.concatenate([self.halves[n], th], axis=1),
                                jnp.concatenate([th, self.halves[n]], axis=1))
                res = _adamw_whole("adamw_" + n, *[jnp.swapaxes(sh[k], -1, -2) for k in ("w", "m", "v")], g_t)
                self.results[n] = [jnp.swapaxes(r, -1, -2) for r in res]
            else:
                self.results[n] = _adamw("adamw_" + n, sh["w"], sh["m"], sh["v"], self.halves[n], th, self.ids)

    def _pair_stage(self, names, grads):
        return (_pair_exchange([_GRAD_ENTRY[n] for n in names], grads),
                lambda recv: self.pair_sums(names, grads, recv))

    def _at_ffn_up_dx(self):
        return self._pair_stage(FFN_MATS, [self.g[n] for n in FFN_MATS])

    def _at_ssm_gate_norm_bwd(self):
        return self._pair_stage(MIXER_MATS, [self.g[n] for n in MIXER_MATS])

    def _at_attn_bwd(self):
        return _chip_exchange([self.sums[n][0] for n in FFN_MATS]), lambda recv: self.chip_sums(FFN_MATS, recv)

    def _at_ssd_bwd(self):
        stages = (_chip_exchange([self.sums[n][0] for n in MIXER_MATS]),
                  _whole_to_sibling([self.halves[n] for n in FFN_MATS]))

        def landed(extra):
            recv, theirs = _split(stages, extra)
            self.chip_sums(MIXER_MATS, recv)
            self.adamw(FFN_MATS, theirs)

        return _join(*stages), landed

    def _at_ssm_conv_bwd(self):
        return _whole_to_sibling([self.halves[n] for n in MIXER_MATS]), lambda theirs: self.adamw(MIXER_MATS, theirs)

    def _at_in_proj_dx(self):
        grads = [_from_cat_t(self.g.pop("w_cat_t"))]
        self.pair_sums(("w_in",), grads,
                       _run_exchange("grad_pair_exchange_w_in", _pair_exchange([_GRAD_ENTRY["w_in"]], grads)))
        return _chip_exchange([self.sums["w_in"][0]]), lambda recv: self.chip_sums(("w_in",), recv)

    def finish(self, g_small, g_rep, rep_shards):
        stages = (_pair_exchange([_ENTRY["small"]], [g_small]), _whole_to_sibling([g_rep]))
        recv_small, recv_rep = _split(stages, _run_exchange("grad_pair_exchange_tail", _join(*stages)))
        self.pair_sums(("small",), [g_small], recv_small)
        p_rep, = _rowwise("pair_sum_replicated", lambda r0, a, b: [a + b], SMALL_ROWS, SMALL_ROWS,
                          [(g_rep, LANES, 0), (recv_rep[0], LANES, 0)], [], [(LANES, F32)], [])
        stages = (_chip_exchange([self.sums["small"][0]]), _to_all_chips(p_rep))
        recv, recv_rep = _split(stages, _run_exchange("grad_chip_exchange_tail", _join(*stages)))
        self.chip_sums(("small",), recv)
        g_rep_tot = _chip_sum_small(p_rep, recv_rep[0], self.ids)
        last = ("w_in", "small")
        self.adamw(last, _run_exchange("grad_half_share_tail", _whole_to_sibling([self.halves[n] for n in last])))
        self.results["replicated"] = _adamw_replicated(g_rep_tot, rep_shards["w"], rep_shards["m"], rep_shards["v"])
        return g_rep_tot[_rep_rows()[1], 0]


def kernel(x, meta_tokens, norm_pre_mix, w_in, ssm_conv_w, ssm_conv_b, ssm_dt_bias, ssm_a_log, ssm_d_skip, ssm_norm, w_ssm_out, attn_sinks, w_attn_out, w_mix_out, norm_post_mix, norm_pre_ffn, w_ffn_up, ffn_conv_w, ffn_conv_b, w_ffn_down, norm_post_ffn, loss_target, m_meta_tokens, m_norm_pre_mix, m_w_in, m_ssm_conv_w, m_ssm_conv_b, m_ssm_dt_bias, m_ssm_a_log, m_ssm_d_skip, m_ssm_norm, m_w_ssm_out, m_attn_sinks, m_w_attn_out, m_w_mix_out, m_norm_post_mix, m_norm_pre_ffn, m_w_ffn_up, m_ffn_conv_w, m_ffn_conv_b, m_w_ffn_down, m_norm_post_ffn, v_meta_tokens, v_norm_pre_mix, v_w_in, v_ssm_conv_w, v_ssm_conv_b, v_ssm_dt_bias, v_ssm_a_log, v_ssm_d_skip, v_ssm_norm, v_w_ssm_out, v_attn_sinks, v_w_attn_out, v_w_mix_out, v_norm_post_mix, v_norm_pre_ffn, v_w_ffn_up, v_ffn_conv_w, v_ffn_conv_b, v_w_ffn_down, v_norm_post_ffn):
    args = dict(locals())
    squeeze = lambda a: a.reshape(a.shape[-2:])
    wts = {n: squeeze(args[n]) for n in WEIGHT_ORDER}
    mom = {n: squeeze(args["m_" + n]) for n in WEIGHT_ORDER}
    var = {n: squeeze(args["v_" + n]) for n in WEIGHT_ORDER}
    x_i, y_i, c_i = _mesh_pos()
    ids = jnp.stack([c_i, _chip_index(x_i, y_i)]).astype(jnp.int32)
    big_names = [n for n, _, _, _ in _BIG[:-1]]
    small_names = [n for n, _, _ in _SMALL_SHARDED]
    rep_names = [n for n, _ in _REPLICATED]

    stacks = {"w": wts, "m": mom, "v": var}
    shards = {n: {"w": args[n], "m": args["m_" + n], "v": args["v_" + n]} for n in big_names}
    shards["small"] = {k: _small_shard([d[n] for n in small_names]) for k, d in stacks.items()}
    rep_shards = {k: [d[n] for n in rep_names] for k, d in stacks.items()}

    w_in4, small_all = _gather_weights([_ENTRY["w_in"], _ENTRY["small"]], [wts["w_in"].astype(BF16), shards["small"]["w"]])
    w = {n: wts[n] for n in rep_names}
    w["w_cat"] = _to_cat(w_in4)
    small_parts = [_unflatten(small_all[i], [shp for _, shp, _ in _SMALL_SHARDED]) for i in range(4)]
    for k, (n, _, axis) in enumerate(_SMALL_SHARDED):
        w[n] = jnp.concatenate([small_parts[i][k] for i in range(4)], axis=axis)
    plan = _StepPlan(w, {n: wts[n].astype(BF16) for n in MIXER_MATS + FFN_MATS}, shards, ids)

    head = jnp.concatenate([jnp.zeros((PAD, D_MODEL), F32), w["meta_tokens"]], axis=0)
    loss_sum, dx, dhead = _local_step(x[0], head, loss_target[0], plan)
    g = plan.g
    g["meta_tokens"] = dhead[PAD:]
    g_small = jnp.stack([_small_shard([_shard_of(g[n], i, shp, ax) for n, shp, ax in _SMALL_SHARDED]) for i in range(4)])
    loss_part = (loss_sum * (0.5 / D_MODEL)).reshape(1, 1)
    loss = plan.finish(g_small, _in_rows([g[n] for n in rep_names] + [loss_part]), rep_shards)

    results = {}
    for kind in range(4):
        results.update({(kind, n): plan.results[n][kind] for n in big_names})
        parts = _unflatten(plan.results["small"][kind], [shp for _, shp, _ in _SMALL_SHARDED])
        results.update({(kind, n): parts[k] for k, n in enumerate(small_names)})
        results.update({(kind, n): plan.results["replicated"][kind][k] for k, n in enumerate(rep_names)})
    outs = [results[kind, n].reshape(args[n].shape) for kind in range(4) for n in WEIGHT_ORDER]
    return (loss, dx[None], *outs)
```

```python
import math
from typing import Any, Callable, NamedTuple, Sequence

import jax
import jax.numpy as jnp
from jax import lax
from jax.experimental import pallas as pl
from jax.experimental.pallas import tpu as pltpu

F32 = jnp.float32
BF16 = jnp.bfloat16

D_MODEL = 1024
N_META = 16
T = 128
PAD = T - N_META
D_INNER = 2048
SSM_HEADS = 32
HEAD_P = 64
SSM_GROUPS = 4
GROUP_W = D_INNER // SSM_GROUPS
D_STATE = 128
CONV_DIM = D_INNER + 2 * SSM_GROUPS * D_STATE
ATTN_HEADS = 16
KV_HEADS = 4
ATTN_W = 1024
KV_W = 256
FFN_DIM = 2816
N_IN = 8736
EPS = 1e-6
NEG = -1e30
SCALE = 0.125

P_Q, P_K, P_V, P_DT, P_Z, P_GATE, P_XBC = 0, 1024, 1280, 1536, 2048, 4096, 6144
QKV_W = 1536
P_W = 9216

ADAM_LR, ADAM_B1, ADAM_B2, ADAM_EPS, ADAM_WD, ADAM_STEP = 0.001, 0.9, 0.999, 1e-08, 0.01, 10

VMEM_BUDGET = 40 * 1024 * 1024
VMEM_LIMIT = 56 * 1024 * 1024
MESH = pl.DeviceIdType.MESH
ANY = pl.BlockSpec(memory_space=pl.ANY)


def _cparams(n_axes, **kw):
    return pltpu.CompilerParams(dimension_semantics=("arbitrary",) * n_axes, vmem_limit_bytes=VMEM_LIMIT, **kw)


class _Exchange(NamedTuple):
    ins: Sequence[Any]
    out_shapes: Sequence[Any]
    make_copies: Callable
    n_copies: int
    aliases: dict = {}


def _call(body, name, grid, in_specs, out_specs, out_shape, operands, scratch_shapes=(), aliases=None, bg=None):
    aliases = dict(aliases or {})
    if bg is None:
        return pl.pallas_call(body, name=name, grid=grid, in_specs=in_specs, out_specs=out_specs, out_shape=out_shape,
                              scratch_shapes=list(scratch_shapes), input_output_aliases=aliases,
                              compiler_params=_cparams(len(grid)))(*operands)
    n_in, n_out, n_scr = len(in_specs), len(out_specs), len(scratch_shapes)
    nb_in, nb_out = len(bg.ins), len(bg.out_shapes)

    def hosted(*refs):
        ins, bg_ins = refs[:n_in], refs[n_in:n_in + nb_in]
        outs = refs[n_in + nb_in:n_in + nb_in + n_out]
        bg_outs = refs[n_in + nb_in + n_out:n_in + nb_in + n_out + nb_out]
        scratch = refs[n_in + nb_in + n_out + nb_out:n_in + nb_in + n_out + nb_out + n_scr]
        send_sems, recv_sems = refs[-2:]
        pids = [pl.program_id(a) for a in range(len(grid))]
        first, last = pids[0] == 0, pids[0] == grid[0] - 1
        for p, g in zip(pids[1:], grid[1:]):
            first, last = first & (p == 0), last & (p == g - 1)
        copies = []
        for k, (src, dst, peer) in enumerate(bg.make_copies(bg_ins, bg_outs)):
            if peer is None:
                copies.append(pltpu.make_async_copy(src, dst, send_sems.at[k]))
            else:
                copies.append(pltpu.make_async_remote_copy(src_ref=src, dst_ref=dst, send_sem=send_sems.at[k],
                                                           recv_sem=recv_sems.at[k], device_id=peer, device_id_type=MESH))
        assert len(copies) == bg.n_copies

        @pl.when(first)
        def _():
            for cp in copies:
                cp.start()

        body(*ins, *outs, *scratch)

        @pl.when(last)
        def _():
            for cp in copies:
                cp.wait()

    aliases = {(k if k < n_in else k + nb_in): v for k, v in aliases.items()}
    aliases.update({n_in + k: n_out + v for k, v in bg.aliases.items()})
    res = pl.pallas_call(
        hosted, name=name, grid=grid, in_specs=list(in_specs) + [ANY] * nb_in, out_specs=list(out_specs) + [ANY] * nb_out,
        out_shape=list(out_shape) + list(bg.out_shapes), input_output_aliases=aliases,
        scratch_shapes=list(scratch_shapes) + [pltpu.SemaphoreType.DMA((bg.n_copies,))] * 2,
        compiler_params=_cparams(len(grid), has_side_effects=True))(*operands, *bg.ins)
    return res[:n_out], res[n_out:]


def _sigmoid(x):
    return 1.0 / (1.0 + jnp.exp(-x))


def _silu(x):
    return x * _sigmoid(x)


def _silu_grad(x):
    s = _sigmoid(x)
    return x * s, s * (1.0 + x * (1.0 - s))


def _dsilu(x):
    return _silu_grad(x)[1]


def _softplus(x):
    e = jnp.exp(-jnp.abs(x))
    small = e * (1.0 - e * (0.5 - e * (1.0 / 3.0)))
    return jnp.maximum(x, 0.0) + jnp.where(e < 0.01, small, jnp.log(1.0 + e))


def _rms(x, w):
    r = lax.rsqrt(jnp.mean(x * x, axis=-1, keepdims=True) + EPS)
    return x * r * w


def _rms_bwd(dy, x, w):
    r = lax.rsqrt(jnp.mean(x * x, axis=-1, keepdims=True) + EPS)
    xh = x * r
    g = dy * w
    dx = r * (g - xh * jnp.mean(g * xh, axis=-1, keepdims=True))
    dw = jnp.sum(dy * xh, axis=0, keepdims=True)
    return dx, dw


def _dot(a, b):
    return jnp.dot(a, b, preferred_element_type=F32)


def _dot_nt(a, b):
    return lax.dot_general(a, b, (((1,), (1,)), ((), ())), preferred_element_type=F32)


def _dot_tn(a, b):
    return lax.dot_general(a, b, (((0,), (0,)), ((), ())), preferred_element_type=F32)


def _split3(x):
    hi = x.astype(BF16)
    r = x - hi.astype(F32)
    mid = r.astype(BF16)
    lo = (r - mid.astype(F32)).astype(BF16)
    return hi, mid, lo


def _xdot(x, e):
    hi, mid, lo = _split3(x)
    return _dot(hi, e) + _dot(mid, e) + _dot(lo, e)


def _xdot_l(e, x):
    hi, mid, lo = _split3(x)
    return _dot(e, hi) + _dot(e, mid) + _dot(e, lo)


def _iota(shape, dim):
    return lax.broadcasted_iota(jnp.int32, shape, dim)


def _divisors(n, unit):
    return [t for t in range(unit, n + 1, unit) if n % t == 0]


MIN_MATMUL_STEPS = 8


def _matmul_tiles(m, n, k, a_bytes, b_bytes, o_bytes, m_unit):
    best = None
    for tm in _divisors(m, m_unit):
        for tn in _divisors(n, 128):
            for tk in _divisors(k, 128):
                acc = 0 if tk == k else tm * tn * 4
                vm = 2 * (tm * tk * a_bytes + tk * tn * b_bytes + tm * tn * o_bytes) + acc
                if vm > VMEM_BUDGET:
                    continue
                steps = (m // tm) * (n // tn) * (k // tk)
                score = (tk == k, min(steps, MIN_MATMUL_STEPS), min(tm, 256), tm * tn * tk)
                if best is None or score > best[0]:
                    best = (score, (tm, tn, tk))
    return best[1]


def _matmul(name, a, b, mode, out_dtype, bg=None):
    if mode == "nn":
        (m, k), n = a.shape, b.shape[1]
    elif mode == "nt":
        (m, k), n = a.shape, b.shape[0]
    else:
        (k, m), n = a.shape, b.shape[1]
    ab, bb, ob = a.dtype.itemsize, b.dtype.itemsize, jnp.dtype(out_dtype).itemsize
    tm, tn, tk = _matmul_tiles(m, n, k, ab, bb, ob, 128 if mode == "tn" else 16)
    nk = k // tk
    dot = {"nn": _dot, "nt": _dot_nt, "tn": _dot_tn}[mode]

    def body(a_ref, b_ref, o_ref, *scratch):
        prod = dot(a_ref[...].astype(BF16), b_ref[...].astype(BF16))
        if nk == 1:
            o_ref[...] = prod.astype(o_ref.dtype)
        else:
            acc_ref, = scratch
            kk = pl.program_id(2)

            @pl.when(kk == 0)
            def _():
                acc_ref[...] = prod

            @pl.when(kk > 0)
            def _():
                acc_ref[...] += prod

            @pl.when(kk == nk - 1)
            def _():
                o_ref[...] = acc_ref[...].astype(o_ref.dtype)

    a_spec = pl.BlockSpec((tk, tm), lambda i, j, kk: (kk, i)) if mode == "tn" else pl.BlockSpec((tm, tk), lambda i, j, kk: (i, kk))
    b_spec = pl.BlockSpec((tn, tk), lambda i, j, kk: (j, kk)) if mode == "nt" else pl.BlockSpec((tk, tn), lambda i, j, kk: (kk, j))
    res = _call(body, name, (m // tm, n // tn, nk), [a_spec, b_spec], [pl.BlockSpec((tm, tn), lambda i, j, kk: (i, j))],
                [jax.ShapeDtypeStruct((m, n), out_dtype)], [a, b],
                scratch_shapes=[] if nk == 1 else [pltpu.VMEM((tm, tn), F32)], bg=bg)
    return res[0] if bg is None else (res[0][0], res[1])


def _row_tile(n_rows, cap, unit=16):
    return max([t for t in _divisors(n_rows, unit) if t <= cap], default=n_rows)


ROW_SUB = 384
GROUP_UNROLL = 4


def _rowwise(name, fn, n_rows, tm, row_ins, full_ins, row_outs, acc_outs, bg=None):
    n_in = len(row_ins) + len(full_ins)
    n_ro = len(row_outs)
    into = [(k, o[3]) for k, o in enumerate(row_outs) if len(o) > 2 and o[2] == "into"]

    n_row_in = len(row_ins)
    sub = min(tm, ROW_SUB)

    def body(*refs):
        i = pl.program_id(0)
        outs = refs[n_in + len(into):]

        sums = tuple(jnp.zeros((1, w), F32) for w in acc_outs)
        for s in range(tm // sub):
            rows = pl.ds(s * sub, sub)
            vals = [r[rows, :] for r in refs[:n_row_in]] + [r[...] for r in refs[n_row_in:n_in]]
            res = fn(i * tm + s * sub, *vals)
            for o, r, v in zip(row_outs, outs[:n_ro], res[:n_ro]):
                if len(o) > 2 and o[2] == "first":
                    @pl.when(i == 0)
                    def _(r=r, v=v, rows=rows):
                        r[rows, :] = v.astype(r.dtype)
                else:
                    r[rows, :] = v.astype(r.dtype)
            sums = tuple(a + v for a, v in zip(sums, res[n_ro:]))

        @pl.when(i == 0)
        def _():
            for r, v in zip(outs[n_ro:], sums):
                r[...] = v

        @pl.when(i > 0)
        def _():
            for r, v in zip(outs[n_ro:], sums):
                r[...] += v

    def in_spec(entry):
        w, cb = entry[1], entry[2]
        if len(entry) > 3 and entry[3] == "prev":
            return pl.BlockSpec((tm, w), lambda i: (jnp.maximum(i - 1, 0), cb))
        if len(entry) > 3 and entry[3] == "first":
            return pl.BlockSpec((tm, w), lambda i: (0, cb))
        return pl.BlockSpec((tm, w), lambda i: (i, cb))

    def out_spec(o):
        if len(o) == 2:
            return pl.BlockSpec((tm, o[0]), lambda i: (i, 0)), jax.ShapeDtypeStruct((n_rows, o[0]), o[1])
        if o[2] == "new":
            return pl.BlockSpec((tm, o[0]), lambda i: (i, o[4])), jax.ShapeDtypeStruct((n_rows, o[3]), o[1])
        if o[2] == "into":
            return pl.BlockSpec((tm, o[0]), lambda i: (i, o[4])), jax.ShapeDtypeStruct(o[3].shape, o[3].dtype)
        if o[2] == "first":
            return pl.BlockSpec((tm, o[0]), lambda i: (0, 0)), jax.ShapeDtypeStruct((tm, o[0]), o[1])
        return pl.BlockSpec((tm, o[0]), lambda i: (jnp.maximum(i - 1, 0), 0)), jax.ShapeDtypeStruct((o[3], o[0]), o[1])

    in_specs = [in_spec(e) for e in row_ins]
    in_specs += [pl.BlockSpec(a.shape, lambda i: (0, 0)) for a in full_ins]
    in_specs += [pl.BlockSpec(memory_space=pl.ANY) for _ in into]
    specs_shapes = [out_spec(o) for o in row_outs]
    out_specs = [s for s, _ in specs_shapes] + [pl.BlockSpec((1, w), lambda i: (0, 0)) for w in acc_outs]
    out_shape = [s for _, s in specs_shapes] + [jax.ShapeDtypeStruct((1, w), F32) for w in acc_outs]
    return _call(body, name, (n_rows // tm,), in_specs, out_specs, out_shape,
                 [e[0] for e in row_ins] + list(full_ins) + [arr for _, arr in into],
                 aliases={n_in + a: k for a, (k, _) in enumerate(into)}, bg=bg)


def _valid_rows(first_row, tm, lo):
    return (first_row + _iota((tm, 1), 0)) >= lo


CONV_ROWS = 128
CONV_SUB = 16
CONV_LANES = 256


def _conv_specs(tm, width, blk, n_rows, after):
    specs = [pl.BlockSpec((tm, width), lambda i: (i, blk)),
             pl.BlockSpec((8, width), lambda i: (jnp.maximum(i * (tm // 8) - 1, 0), blk))]
    if after:
        specs.append(pl.BlockSpec((16, width), lambda i: (jnp.minimum((i + 1) * (tm // 16), n_rows // 16 - 1), blk)))
    return specs


def _conv_window(win, w_ref, b_ref, taps, c0, cw, n):
    acc = b_ref[:, c0:c0 + cw] + w_ref[taps - 1:taps, c0:c0 + cw] * win[8:8 + n]
    for k in range(taps - 1):
        acc = acc + w_ref[k:k + 1, c0:c0 + cw] * win[8 - (taps - 1) + k:8 - (taps - 1) + k + n]
    return acc


def _ffn_act(name, u_raw, conv_w, conv_b, n_rows):
    tm, sub, cw = CONV_ROWS, CONV_SUB, CONV_LANES
    taps, width = conv_w.shape
    half = width // 2

    def body(cur_ref, prev_ref, w_ref, b_ref, f_ref, ext_ref):
        i = pl.program_id(0)
        ext_ref[0:8, :] = jnp.where(i > 0, prev_ref[...], 0.0)
        ext_ref[8:8 + tm, :] = cur_ref[...]
        for q in range(half // cw):
            a0, g0 = q * cw, half + q * cw

            def group(s, carry):
                r = pl.multiple_of(s * sub, sub)
                a = _conv_window(ext_ref[pl.ds(r, sub + 8), a0:a0 + cw], w_ref, b_ref, taps, a0, cw, sub)
                g = _conv_window(ext_ref[pl.ds(r, sub + 8), g0:g0 + cw], w_ref, b_ref, taps, g0, cw, sub)
                f = jnp.where(_valid_rows(i * tm + r, sub, PAD), _silu(a) * g, 0.0)
                f_ref[pl.ds(r, sub), a0:a0 + cw] = f.astype(f_ref.dtype)
                return carry

            lax.fori_loop(0, tm // sub, group, 0, unroll=GROUP_UNROLL)

    return pl.pallas_call(
        body, name=name, grid=(n_rows // tm,),
        in_specs=_conv_specs(tm, width, 0, n_rows, False) + [pl.BlockSpec((taps, width), lambda i: (0, 0)),
                                                             pl.BlockSpec((1, width), lambda i: (0, 0))],
        out_specs=pl.BlockSpec((tm, half), lambda i: (i, 0)),
        out_shape=jax.ShapeDtypeStruct((n_rows, half), BF16),
        scratch_shapes=[pltpu.VMEM((tm + 8, width), F32)],
        compiler_params=_cparams(1),
    )(u_raw, u_raw, conv_w, conv_b)


def _conv_bwd(name, raw, raw_blk, dsrcs, chunk_src, conv_w, conv_b, n_rows, gated, into=None, into_blk=0, bg=None):
    taps, width = conv_w.shape
    half = width // 2 if gated else width
    tm, sub, cw = CONV_ROWS, CONV_SUB, CONV_LANES
    te = tm + 16
    nd = len(dsrcs)
    n_parts = 2 if gated else 1

    def body(*refs):
        cur_ref, prev_ref, next_ref = refs[0:3]
        dcur, dnext = refs[3:3 + nd], refs[3 + nd:3 + 2 * nd]
        w_ref, b_ref = refs[3 + 2 * nd:5 + 2 * nd]
        out_ref, acc_ref, ext_ref, du_ref = refs[-4:]
        i = pl.program_id(0)
        ext_ref[0:8, :] = jnp.where(i > 0, prev_ref[...], 0.0)
        ext_ref[8:8 + tm, :] = cur_ref[...]
        ext_ref[8 + tm:24 + tm, :] = next_ref[...]

        for q, (src, off) in enumerate(chunk_src):
            cols = [q * cw, half + q * cw][:n_parts]

            def conv_grad(r, d):
                pre = [_conv_window(ext_ref[pl.ds(r, sub + 8), c0:c0 + cw], w_ref, b_ref, taps, c0, cw, sub) for c0 in cols]
                row = i * tm + r + _iota((sub, 1), 0)
                live = (row >= PAD) & (row < n_rows)
                if gated:
                    act, dact = _silu_grad(pre[0])
                    dus = [d * pre[1] * dact, d * act]
                else:
                    dus = [d * _dsilu(pre[0])]
                for part, du in enumerate(dus):
                    du_ref[part, pl.ds(r, sub), :] = jnp.where(live, du, 0.0)

            def tile_rows(s, carry):
                r = pl.multiple_of(s * sub, sub)
                conv_grad(r, dcur[src][pl.ds(r, sub), off:off + cw].astype(F32))
                return carry

            lax.fori_loop(0, tm // sub, tile_rows, 0, unroll=GROUP_UNROLL)
            conv_grad(tm, dnext[src][:, off:off + cw].astype(F32))

            for part, c0 in enumerate(cols):
                taps_w = [w_ref[k:k + 1, c0:c0 + cw] for k in range(taps)]

                def back(s, sums):
                    new = list(sums)
                    for u in range(2):
                        r = pl.multiple_of((2 * s + u) * sub, sub)
                        win = du_ref[part, pl.ds(r, sub + 8), :]
                        raw_rows = ext_ref[pl.ds(8 + r, sub), c0:c0 + cw]
                        draw = jnp.zeros((sub, cw), F32)
                        for k in range(taps):
                            shifted = win[taps - 1 - k:taps - 1 - k + sub]
                            draw = draw + taps_w[k] * shifted
                            new[k] = new[k] + shifted * raw_rows
                        new[taps] = new[taps] + win[0:sub]
                        out_ref[pl.ds(r, sub), c0:c0 + cw] = jnp.where(_valid_rows(i * tm + r, sub, PAD), draw, 0.0).astype(out_ref.dtype)
                    return tuple(new)

                sums = lax.fori_loop(0, tm // (2 * sub), back, tuple(jnp.zeros((sub, cw), F32) for _ in range(taps + 1)))
                for k in range(taps + 1):
                    total = jnp.sum(sums[k], axis=0, keepdims=True)
                    acc_ref[k:k + 1, c0:c0 + cw] = jnp.where(i == 0, total, acc_ref[k:k + 1, c0:c0 + cw] + total)

    in_specs = _conv_specs(tm, width, raw_blk, n_rows, True)
    in_specs += [pl.BlockSpec((tm, d.shape[1]), lambda i: (i, 0)) for d in dsrcs]
    in_specs += [pl.BlockSpec((16, d.shape[1]), lambda i: (jnp.minimum((i + 1) * (tm // 16), n_rows // 16 - 1), 0)) for d in dsrcs]
    in_specs += [pl.BlockSpec((taps, width), lambda i: (0, 0)), pl.BlockSpec((1, width), lambda i: (0, 0))]
    operands = [raw, raw, raw] + list(dsrcs) + list(dsrcs) + [conv_w, conv_b]
    aliases = {}
    if into is None:
        out0 = jax.ShapeDtypeStruct((n_rows, width), BF16)
    else:
        in_specs.append(pl.BlockSpec(memory_space=pl.ANY))
        operands.append(into)
        aliases = {len(operands) - 1: 0}
        out0 = jax.ShapeDtypeStruct(into.shape, into.dtype)
    return _call(body, name, (n_rows // tm,), in_specs,
                 [pl.BlockSpec((tm, width), lambda i: (i, into_blk)), pl.BlockSpec((8, width), lambda i: (0, 0))],
                 [out0, jax.ShapeDtypeStruct((8, width), F32)], operands,
                 scratch_shapes=[pltpu.VMEM((tm + 24, width), F32), pltpu.VMEM((n_parts, te + 8, cw), F32)],
                 aliases=aliases, bg=bg)


def _ssd_specs(n_chunks, rev, per_step=1):
    cidx = (lambda c: n_chunks - 1 - c) if rev else (lambda c: c)
    xw, nw = per_step * GROUP_W, per_step * D_STATE
    xg0, bg0, cg0 = P_XBC // xw, (P_XBC + D_INNER) // nw, (P_XBC + D_INNER + SSM_GROUPS * D_STATE) // nw

    def cur(width, blk0):
        return pl.BlockSpec((T, width), lambda g, c: (cidx(c), blk0 + g))

    def prev(width, blk0):
        return pl.BlockSpec((8, width), lambda g, c: (jnp.maximum(cidx(c) * (T // 8) - 1, 0), blk0 + g))

    specs = [cur(xw, xg0), prev(xw, xg0), cur(nw, bg0), prev(nw, bg0), cur(nw, cg0), prev(nw, cg0),
             pl.BlockSpec((T, 128), lambda g, c: (cidx(c), P_DT // 128))]
    wb, wc = D_INNER // nw, (D_INNER + SSM_GROUPS * D_STATE) // nw
    specs += [pl.BlockSpec((4, xw), lambda g, c: (0, g)),
              pl.BlockSpec((4, nw), lambda g, c: (0, wb + g)),
              pl.BlockSpec((4, nw), lambda g, c: (0, wc + g)),
              pl.BlockSpec((1, xw), lambda g, c: (0, g)),
              pl.BlockSpec((1, nw), lambda g, c: (0, wb + g)),
              pl.BlockSpec((1, nw), lambda g, c: (0, wc + g))]
    specs += [pl.BlockSpec((1, 128), lambda g, c: (0, 0))] * 3
    return specs, cidx


def _ssd_chunk_forward(refs, ext_ref, g, c):
    (xc_ref, xp_ref, bc_ref, bp_ref, cc_ref, cp_ref, dt_ref, wx_ref, wb_ref, wc_ref,
     bx_ref, bb_ref, bcb_ref, dtb_ref, alog_ref, dsk_ref) = refs

    def conv_pre(cur_ref, prev_ref, w_ref, b_ref, width):
        ext_ref[0:8, 0:width] = jnp.where(c > 0, prev_ref[...], 0.0)
        ext_ref[8:8 + T, 0:width] = cur_ref[...]
        w = w_ref[...]
        acc = b_ref[...] + w[3:4] * cur_ref[...]
        for k in range(3):
            acc = acc + w[k:k + 1] * ext_ref[pl.ds(5 + k, T), 0:width]
        return acc

    valid = _valid_rows(c * T, T, PAD)
    v = {}
    v["valid"] = valid
    v["x_pre"] = conv_pre(xc_ref, xp_ref, wx_ref, bx_ref, GROUP_W)
    v["b_pre"] = conv_pre(bc_ref, bp_ref, wb_ref, bb_ref, D_STATE)
    v["c_pre"] = conv_pre(cc_ref, cp_ref, wc_ref, bcb_ref, D_STATE)
    xs = _silu(v["x_pre"])
    bm = jnp.where(valid, _silu(v["b_pre"]), 0.0)
    cm = jnp.where(valid, _silu(v["c_pre"]), 0.0)
    dtr = dt_ref[...] + dtb_ref[...]
    dt = jnp.where(valid, _softplus(dtr), 0.0)
    a_neg = -jnp.exp(alog_ref[...])
    a = dt * a_neg
    tril = _iota((T, T), 0) >= _iota((T, T), 1)
    cs = _xdot_l(tril.astype(BF16), a)
    hh, ll = _iota((128, GROUP_W), 0), _iota((128, GROUP_W), 1)
    expand = (hh == 8 * g + jnp.right_shift(ll, 6)).astype(BF16)
    sh, sj = _iota((128, 128), 0), _iota((128, 128), 1)
    select = ((sh == 8 * g + sj) & (sj < 8)).astype(BF16)
    hh_t, ll_t = _iota((GROUP_W, 128), 1), _iota((GROUP_W, 128), 0)
    v["expand_t"] = (hh_t == 8 * g + jnp.right_shift(ll_t, 6)).astype(BF16)
    v["select_t"] = ((sj == 8 * g + sh) & (sh < 8)).astype(BF16)
    cs_e = _xdot(cs, expand)
    dt_e = _xdot(dt, expand)
    cs_loc = _xdot(cs, select)
    cs_loc_t = cs_loc.T
    cs_last_e = cs_e[T - 1:T, :]
    v.update(xs=xs, bm=bm, cm=cm, dtr=dtr, dt=dt, a_neg=a_neg, tril=tril, expand=expand, select=select,
             cs_e=cs_e, dt_e=dt_e, cs_loc=cs_loc, cs_loc_t=cs_loc_t, cs_last_e=cs_last_e)
    v["xdt"] = xs * dt_e
    v["decay_e"] = jnp.exp(cs_last_e - cs_e)
    v["ecs_e"] = jnp.exp(cs_e)
    v["elast_e"] = jnp.exp(cs_last_e)
    v["d_e"] = _xdot(dsk_ref[...], expand)
    v["gmat"] = _dot_nt(cm.astype(BF16), bm.astype(BF16))
    return v


def _ssd_decay_pair(v, jp):
    out = []
    for j in (2 * jp, 2 * jp + 1):
        diff = v["cs_loc"][:, j:j + 1] - v["cs_loc_t"][j:j + 1, :]
        out.append(jnp.where(v["tril"], jnp.exp(jnp.where(v["tril"], diff, 0.0)), 0.0))
    return out


def _block_diag_pair(xp):
    lane = _iota(xp.shape, 1)
    return jnp.concatenate([jnp.where(lane < HEAD_P, xp, 0.0), jnp.where(lane >= HEAD_P, xp, 0.0)], axis=0)


SSD_GROUPS_PER_STEP = 4


def _ssd_group_refs(refs, gg):
    x_w, n_w = pl.ds(GROUP_W * gg, GROUP_W), pl.ds(D_STATE * gg, D_STATE)
    lanes = [x_w, x_w, n_w, n_w, n_w, n_w, None, x_w, n_w, n_w, x_w, n_w, n_w, None, None, None]
    return [r if w is None else r.at[:, w] for r, w in zip(refs, lanes)]


def _ssd_fwd(p, conv_w, conv_b, dt_bias, a_log, d_skip, n_chunks, bg=None):
    n_rows = n_chunks * T
    in_specs, _ = _ssd_specs(n_chunks, rev=False, per_step=SSD_GROUPS_PER_STEP)
    per = SSD_GROUPS_PER_STEP

    def body(*refs):
        y_ref, hin_ref, st_ref, ext_ref = refs[16:]
        g2, c = pl.program_id(0), pl.program_id(1)

        @pl.when(c == 0)
        def _():
            st_ref[...] = jnp.zeros_like(st_ref)

        for gg in range(per):
            v = _ssd_chunk_forward(_ssd_group_refs(refs[:16], gg), ext_ref.at[gg], per * g2 + gg, c)
            state = st_ref[gg]
            hin_ref[gg] = state
            ys = []
            for jp in range(4):
                l0, l1 = _ssd_decay_pair(v, jp)
                lhs = jnp.concatenate([v["gmat"] * l0, v["gmat"] * l1], axis=1).astype(BF16)
                rhs = _block_diag_pair(v["xdt"][:, 128 * jp:128 * jp + 128]).astype(BF16)
                ys.append(_dot(lhs, rhs))
            y = jnp.concatenate(ys, axis=1)
            y = y + _dot(v["cm"].astype(BF16), state.astype(BF16)) * v["ecs_e"] + v["xs"] * v["d_e"]
            y_ref[:, GROUP_W * gg:GROUP_W * gg + GROUP_W] = y
            s_new = _dot_tn(v["bm"].astype(BF16), (v["xdt"] * v["decay_e"]).astype(BF16))
            st_ref[gg] = state * v["elast_e"] + s_new

    return _call(
        body, "ssd_fwd", (SSM_GROUPS // per, n_chunks), in_specs,
        [pl.BlockSpec((T, per * GROUP_W), lambda g, c: (c, g)),
         pl.BlockSpec((per, None, D_STATE, GROUP_W), lambda g, c: (g, c, 0, 0))],
        [jax.ShapeDtypeStruct((n_rows, D_INNER), F32),
         jax.ShapeDtypeStruct((SSM_GROUPS, n_chunks, D_STATE, GROUP_W), F32)],
        [p, p, p, p, p, p, p, conv_w, conv_w, conv_w, conv_b, conv_b, conv_b, dt_bias, a_log, d_skip],
        scratch_shapes=[pltpu.VMEM((per, D_STATE, GROUP_W), F32), pltpu.VMEM((per, T + 8, GROUP_W), F32)], bg=bg)


def _ssd_bwd(p, conv_w, conv_b, dt_bias, a_log, d_skip, hin, dy, dp, n_chunks, bg=None):
    n_rows = n_chunks * T
    per = SSD_GROUPS_PER_STEP
    assert per == SSM_GROUPS
    dt_w = P_Z - P_DT
    in_specs, cidx = _ssd_specs(n_chunks, rev=True, per_step=per)
    in_specs = in_specs + [pl.BlockSpec((per, None, D_STATE, GROUP_W), lambda g, c: (g, cidx(c), 0, 0)),
                           pl.BlockSpec((T, per * GROUP_W), lambda g, c: (cidx(c), g)), ANY]

    def body(*refs):
        hin_ref, dy_ref = refs[16:18]
        dx_ref, db_ref, dc_ref, dp_ref, dpar_ref, dst_ref, ext_ref, ddt_ref = refs[19:]
        for gg in range(per):
            x_w, n_w = pl.ds(GROUP_W * gg, GROUP_W), pl.ds(D_STATE * gg, D_STATE)
            group_body(_ssd_group_refs(refs[:16], gg), hin_ref.at[gg], dy_ref.at[:, x_w], dx_ref.at[:, x_w],
                       db_ref.at[:, n_w], dc_ref.at[:, n_w], ddt_ref.at[:, n_w], dpar_ref.at[gg], dst_ref.at[gg],
                       ext_ref.at[gg], per * pl.program_id(0) + gg)
        ddt = ddt_ref[:, 0:128] + ddt_ref[:, 128:256] + ddt_ref[:, 256:384] + ddt_ref[:, 384:512]
        dp_ref[...] = jnp.concatenate([ddt, jnp.zeros((T, dt_w - 128), F32)], axis=1).astype(dp_ref.dtype)

    def group_body(in_refs, hin_ref, dy_ref, dx_ref, db_ref, dc_ref, ddt_ref, dpar_ref, dst_ref, ext_ref, g):
        step = pl.program_id(1)
        c = n_chunks - 1 - step

        @pl.when(step == 0)
        def _():
            dst_ref[...] = jnp.zeros_like(dst_ref)

        v = _ssd_chunk_forward(in_refs, ext_ref, g, c)
        hin_f = hin_ref[...]
        hin_b = hin_f.astype(BF16)
        dyv = dy_ref[...]
        dst = dst_ref[...]
        dst_b = dst.astype(BF16)
        xs, bm, cm, xdt = v["xs"], v["bm"], v["cm"], v["xdt"]
        bm_b, cm_b = bm.astype(BF16), cm.astype(BF16)

        dd_e = jnp.sum(dyv * xs, axis=0, keepdims=True)
        dxs = dyv * v["d_e"]
        ch = _dot(cm_b, hin_b)
        dch = (dyv * v["ecs_e"]).astype(BF16)
        dcm = _dot_nt(dch, hin_b)
        dhin = _dot_tn(cm_b, dch) + dst * v["elast_e"]
        dcs_e = dyv * ch * v["ecs_e"]
        dxd = _dot(bm_b, dst_b)
        dbm = _dot_nt((xdt * v["decay_e"]).astype(BF16), dst_b)
        dxdt_state = dxd * v["decay_e"]
        q = dxdt_state * xdt
        dcs_e = dcs_e - q
        dlast_e = jnp.sum(q, axis=0, keepdims=True) + jnp.sum(dst * hin_f, axis=0, keepdims=True) * v["elast_e"]
        dg = jnp.zeros((T, T), F32)
        rs_cols = jnp.zeros((T, 128), F32)
        cs_rows = jnp.zeros((128, T), F32)
        lane_i, sub_i = _iota((T, 128), 1), _iota((128, T), 0)
        dxdt_parts = []
        for jp in range(4):
            l0, l1 = _ssd_decay_pair(v, jp)
            m0, m1 = v["gmat"] * l0, v["gmat"] * l1
            xbd = _block_diag_pair(xdt[:, 128 * jp:128 * jp + 128]).astype(BF16)
            dyp = dyv[:, 128 * jp:128 * jp + 128]
            dm = _dot_nt(dyp.astype(BF16), xbd)
            dm0, dm1 = dm[:, 0:T], dm[:, T:2 * T]
            dg = dg + dm0 * l0 + dm1 * l1
            for j, qq in ((2 * jp, dm0 * m0), (2 * jp + 1, dm1 * m1)):
                rs_cols = jnp.where(lane_i == j, jnp.sum(qq, axis=1, keepdims=True), rs_cols)
                cs_rows = jnp.where(sub_i == j, jnp.sum(qq, axis=0, keepdims=True), cs_rows)
            mv = jnp.concatenate([m0, m1], axis=0).astype(BF16)
            dxdt_parts.append(_dot_tn(mv, _block_diag_pair(dyp).astype(BF16)))
        dxdt = jnp.concatenate(dxdt_parts, axis=1) + dxdt_state
        dg_b = dg.astype(BF16)
        dcm = dcm + _dot(dg_b, bm_b)
        dbm = dbm + _dot_tn(dg_b, cm_b)
        expand_t = v["expand_t"]
        dcs_loc = rs_cols - cs_rows.T
        last_row = _iota((T, 1), 0) == T - 1
        dcs_full_e = dcs_e + jnp.where(last_row, dlast_e, 0.0)
        dcs = _xdot(dcs_full_e, expand_t) + _xdot(dcs_loc, v["select_t"])
        triu = (_iota((T, T), 0) <= _iota((T, T), 1)).astype(BF16)
        da = _xdot_l(triu, dcs)
        ddt = da * v["a_neg"] + _xdot(dxdt * xs, expand_t)
        dxs = dxs + dxdt * v["dt_e"]
        ddtr = jnp.where(v["valid"], ddt * _sigmoid(v["dtr"]), 0.0)
        dx_ref[...] = dxs
        db_ref[...] = jnp.where(v["valid"], dbm, 0.0)
        dc_ref[...] = jnp.where(v["valid"], dcm, 0.0)
        ddt_ref[...] = ddtr
        dpar = jnp.concatenate([
            jnp.sum(ddtr, axis=0, keepdims=True),
            jnp.sum(da * v["dt"], axis=0, keepdims=True) * v["a_neg"],
            _xdot(dd_e, expand_t),
            jnp.zeros((5, 128), F32)], axis=0)

        @pl.when(step == 0)
        def _():
            dpar_ref[...] = dpar

        @pl.when(step > 0)
        def _():
            dpar_ref[...] += dpar

        dst_ref[...] = dhin

    return _call(
        body, "ssd_bwd", (SSM_GROUPS // per, n_chunks), in_specs,
        [pl.BlockSpec((T, per * GROUP_W), lambda g, c: (cidx(c), g)),
         pl.BlockSpec((T, per * D_STATE), lambda g, c: (cidx(c), g)),
         pl.BlockSpec((T, per * D_STATE), lambda g, c: (cidx(c), g)),
         pl.BlockSpec((T, dt_w), lambda g, c: (cidx(c), P_DT // dt_w)),
         pl.BlockSpec((per, 8, 128), lambda g, c: (g, 0, 0))],
        [jax.ShapeDtypeStruct((n_rows, D_INNER), F32),
         jax.ShapeDtypeStruct((n_rows, SSM_GROUPS * D_STATE), F32),
         jax.ShapeDtypeStruct((n_rows, SSM_GROUPS * D_STATE), F32),
         jax.ShapeDtypeStruct(dp.shape, dp.dtype),
         jax.ShapeDtypeStruct((SSM_GROUPS, 8, 128), F32)],
        [p, p, p, p, p, p, p, conv_w, conv_w, conv_w, conv_b, conv_b, conv_b, dt_bias, a_log, d_skip, hin, dy, dp],
        scratch_shapes=[pltpu.VMEM((per, D_STATE, GROUP_W), F32), pltpu.VMEM((per, T + 8, GROUP_W), F32),
                        pltpu.VMEM((T, per * 128), F32)],
        aliases={18: 3}, bg=bg)


def _alibi_slope(h):
    return 2.0 ** (-8.0 * (h + 1) / ATTN_HEADS)


def _dup_half(x256, kvh):
    xb = x256[:, 128 * (kvh // 2):128 * (kvh // 2) + 128]
    rolled = pltpu.roll(xb, 64, 1)
    lane = _iota(xb.shape, 1)
    if kvh % 2 == 0:
        return jnp.where(lane < 64, xb, rolled)
    return jnp.where(lane < 64, rolled, xb)


def _attn_masks(c):
    qi, j = _iota((T, T), 0), _iota((T, T), 1)
    tri = j <= qi
    meta_ok = (j >= PAD) & (j - PAD <= c * T + qi - PAD)
    band_ok = c >= jnp.where(tri, 1, 2)
    dist = jnp.bitwise_and(qi - j, T - 1).astype(F32)
    return tri, meta_ok, band_ok, dist


def _fold(x3, tri):
    return jnp.concatenate([x3[:, 0:T], jnp.where(tri, x3[:, 2 * T:3 * T], x3[:, T:2 * T])], axis=1)


def _unfold(x2, tri):
    band = x2[:, T:2 * T]
    return jnp.concatenate([x2[:, 0:T], jnp.where(tri, 0.0, band), jnp.where(tri, band, 0.0)], axis=1)


def _attn_scores(qp, k3, masks, h0):
    tri, meta_ok, band_ok, dist = masks
    lane = _iota(qp.shape, 1)
    s = []
    for half, h in ((0, h0), (1, h0 + 1)):
        qh = jnp.where((lane < 64) if half == 0 else (lane >= 64), qp, 0.0).astype(BF16)
        raw = _dot_nt(qh, k3)
        band = jnp.where(tri, raw[:, 2 * T:3 * T], raw[:, T:2 * T]) - _alibi_slope(h) * dist
        s.append((qh, jnp.concatenate([jnp.where(meta_ok, raw[:, 0:T], NEG), jnp.where(band_ok, band, NEG)], axis=1)))
    return s


def _attn_fwd(p, sinks, n_chunks, bg=None):
    n_rows = n_chunks * T
    kb, vb = P_K // KV_W, P_V // KV_W

    def body(q_ref, kc_ref, kp_ref, km_ref, vc_ref, vp_ref, vm_ref, sink_ref, o_ref, lse_ref):
        c = pl.program_id(0)
        sinks_v = sink_ref[...]
        masks = _attn_masks(c)
        tri, meta_ok, band_ok, dist = masks
        lane = _iota((T, 128), 1)
        for kvh in range(KV_HEADS):
            k3 = jnp.concatenate([_dup_half(r[...], kvh) for r in (km_ref, kp_ref, kc_ref)], axis=0).astype(BF16)
            v3 = jnp.concatenate([_dup_half(r[...], kvh) for r in (vm_ref, vp_ref, vc_ref)], axis=0)
            v3bd = _block_diag_rows(v3).astype(BF16)
            q2 = q_ref[:, 256 * kvh:256 * kvh + 256] * SCALE
            q4 = jnp.concatenate([jnp.where((lane < 64) if half == 0 else (lane >= 64), q2[:, 128 * pr:128 * pr + 128], 0.0)
                                  for pr in range(2) for half in range(2)], axis=0).astype(BF16)
            raw4 = _dot_nt(q4, k3)
            probs = []
            for hh in range(4):
                h = 4 * kvh + hh
                raw = raw4[T * hh:T * hh + T]
                band = jnp.where(tri, raw[:, 2 * T:3 * T], raw[:, T:2 * T]) - _alibi_slope(h) * dist
                sc = jnp.concatenate([jnp.where(meta_ok, raw[:, 0:T], NEG), jnp.where(band_ok, band, NEG)], axis=1)
                sink = sinks_v[:, h:h + 1]
                m = jnp.maximum(jnp.max(sc, axis=1, keepdims=True), sink)
                e = jnp.exp(sc - m)
                den = jnp.sum(e, axis=1, keepdims=True) + jnp.exp(sink - m)
                probs.append(_unfold(e * (1.0 / den), tri))
                lse_ref[:, h:h + 1] = m + jnp.log(den)
            p4 = jnp.concatenate([jnp.concatenate(probs[0:2], axis=1), jnp.concatenate(probs[2:4], axis=1)], axis=0)
            out = _dot(p4.astype(BF16), v3bd)
            o_ref[:, 256 * kvh:256 * kvh + 256] = jnp.concatenate([out[0:T], out[T:2 * T]], axis=1).astype(o_ref.dtype)

    blk = lambda width, col: pl.BlockSpec((T, width), lambda c: (c, col))
    prev = lambda width, col: pl.BlockSpec((T, width), lambda c: (jnp.maximum(c - 1, 0), col))
    first = lambda width, col: pl.BlockSpec((T, width), lambda c: (0, col))
    return _call(
        body, "attn_fwd", (n_chunks,),
        [blk(ATTN_W, P_Q // ATTN_W), blk(KV_W, kb), prev(KV_W, kb), first(KV_W, kb),
         blk(KV_W, vb), prev(KV_W, vb), first(KV_W, vb), pl.BlockSpec((1, 128), lambda c: (0, 0))],
        [pl.BlockSpec((T, ATTN_W), lambda c: (c, 0)), pl.BlockSpec((T, 128), lambda c: (c, 0))],
        [jax.ShapeDtypeStruct((n_rows, ATTN_W), BF16), jax.ShapeDtypeStruct((n_rows, 128), F32)],
        [p, p, p, p, p, p, p, sinks], bg=bg)


def _block_diag_rows(x3):
    lane = _iota(x3.shape, 1)
    return jnp.concatenate([jnp.where(lane < 64, x3, 0.0), jnp.where(lane >= 64, x3, 0.0)], axis=0)


def _fold_halves(x):
    return x + pltpu.roll(x, 64, 1)


def _attn_bwd(p, sinks, ao, lse, dao, dp, n_chunks, bg=None):
    kb, vb = P_K // KV_W, P_V // KV_W
    rc = lambda s: n_chunks - 1 - s

    def body(q_ref, kc_ref, kp_ref, km_ref, vc_ref, vp_ref, vm_ref, sink_ref, o_ref, lse_ref, do_ref, dp_in_ref,
             dqkv_ref, dsink_ref, kcar_ref, vcar_ref, kmeta_ref, vmeta_ref):
        step = pl.program_id(0)
        c = n_chunks - 1 - step

        @pl.when(step == 0)
        def _():
            for r in (kcar_ref, vcar_ref, kmeta_ref, vmeta_ref):
                r[...] = jnp.zeros_like(r)

        masks = _attn_masks(c)
        tri = masks[0]
        q = q_ref[...] * SCALE
        sinks_v = sink_ref[...]
        lse_v = lse_ref[...]
        ov = o_ref[...].astype(F32)
        dov = do_ref[...].astype(F32)
        lane = _iota((T, 128), 1)
        lane256 = _iota((3 * T, KV_W), 1)
        dsink = jnp.zeros((1, 128), F32)
        dk3_all = jnp.zeros((3 * T, KV_W), F32)
        dv3_all = jnp.zeros((3 * T, KV_W), F32)
        dqs = []
        for kvh in range(KV_HEADS):
            k3 = jnp.concatenate([_dup_half(r[...], kvh) for r in (km_ref, kp_ref, kc_ref)], axis=0).astype(BF16)
            v3 = jnp.concatenate([_dup_half(r[...], kvh) for r in (vm_ref, vp_ref, vc_ref)], axis=0).astype(BF16)
            dk3 = jnp.zeros((3 * T, 128), F32)
            dv3 = jnp.zeros((3 * T, 128), F32)
            for pr in range(2):
                h0 = 4 * kvh + 2 * pr
                blk = 2 * kvh + pr
                qp = q[:, 128 * blk:128 * blk + 128]
                dop = dov[:, 128 * blk:128 * blk + 128]
                prod = dop * ov[:, 128 * blk:128 * blk + 128]
                dq_pair = jnp.zeros((T, 128), F32)
                for half, ((qh, sc), h) in enumerate(zip(_attn_scores(qp, k3, masks, h0), (h0, h0 + 1))):
                    mine = (lane < 64) if half == 0 else (lane >= 64)
                    lse_h = lse_v[:, h:h + 1]
                    pm = jnp.exp(sc - lse_h)
                    doh = jnp.where(mine, dop, 0.0).astype(BF16)
                    delta = jnp.sum(jnp.where(mine, prod, 0.0), axis=1, keepdims=True)
                    dp = _fold(_dot_nt(doh, v3), tri)
                    ds = _unfold(pm * (dp - delta), tri).astype(BF16)
                    p_sink = jnp.exp(sinks_v[:, h:h + 1] - lse_h)
                    dsink = jnp.where(_iota((1, 128), 1) == h, jnp.sum(-p_sink * delta, axis=0, keepdims=True), dsink)
                    dq_pair = jnp.where(mine, _dot(ds, k3), dq_pair)
                    dk3 = dk3 + _dot_tn(ds, qh)
                    dv3 = dv3 + _dot_tn(_unfold(pm, tri).astype(BF16), doh)
                dqs.append(dq_pair * SCALE)
            in_place = (lane256 >= 64 * kvh) & (lane256 < 64 * kvh + 64)
            wide = lambda x: jnp.concatenate([x, x], axis=1)
            dk3_all = jnp.where(in_place, wide(_fold_halves(dk3)), dk3_all)
            dv3_all = jnp.where(in_place, wide(_fold_halves(dv3)), dv3_all)
        dsink_all = dsink

        @pl.when(step == 0)
        def _():
            dsink_ref[...] = dsink_all

        @pl.when(step > 0)
        def _():
            dsink_ref[...] += dsink_all

        kmeta = kmeta_ref[...] + dk3_all[0:T]
        vmeta = vmeta_ref[...] + dv3_all[0:T]
        kmeta_ref[...] = kmeta
        vmeta_ref[...] = vmeta
        is_first = c == 0
        dk = jnp.where(is_first, kmeta, dk3_all[2 * T:3 * T] + kcar_ref[...])
        dv = jnp.where(is_first, vmeta, dv3_all[2 * T:3 * T] + vcar_ref[...])
        dqkv_ref[...] = jnp.concatenate(dqs + [dk, dv], axis=1).astype(dqkv_ref.dtype)
        kcar_ref[...] = dk3_all[T:2 * T]
        vcar_ref[...] = dv3_all[T:2 * T]

    blk = lambda width, col: pl.BlockSpec((T, width), lambda s: (rc(s), col))
    prev = lambda width, col: pl.BlockSpec((T, width), lambda s: (jnp.maximum(rc(s) - 1, 0), col))
    first = lambda width, col: pl.BlockSpec((T, width), lambda s: (0, col))
    return _call(
        body, "attn_bwd", (n_chunks,),
        [blk(ATTN_W, P_Q // ATTN_W), blk(KV_W, kb), prev(KV_W, kb), first(KV_W, kb),
         blk(KV_W, vb), prev(KV_W, vb), first(KV_W, vb), pl.BlockSpec((1, 128), lambda s: (0, 0)),
         blk(ATTN_W, 0), blk(128, 0), blk(ATTN_W, 0), ANY],
        [blk(QKV_W, P_Q // QKV_W), pl.BlockSpec((1, 128), lambda s: (0, 0))],
        [jax.ShapeDtypeStruct(dp.shape, dp.dtype), jax.ShapeDtypeStruct((1, 128), F32)],
        [p, p, p, p, p, p, p, sinks, ao, lse, dao, dp],
        scratch_shapes=[pltpu.VMEM((T, KV_W), F32)] * 4, aliases={11: 0}, bg=bg)


def _pad_lanes(v, width=128):
    return jnp.pad(v, ((0, 0), (0, width - v.shape[1])))


def _local_step(x, head, tgt, plan):
    w, g, run = plan.w, plan.g, plan.run
    n_tok = x.shape[0]
    n_rows = n_tok + T
    n_chunks = n_rows // T
    tm = _row_tile(n_rows, 384)
    dt_bias, a_log, d_skip = (_pad_lanes(w[k]) for k in ("ssm_dt_bias", "ssm_a_log", "ssm_d_skip"))
    sinks = _pad_lanes(w["attn_sinks"])
    x_in = [(x, D_MODEL, 0, "prev"), (head, D_MODEL, 0, "first")]

    def h0_tile(r0, xt, hd):
        return jnp.where(r0 < T, hd, xt)

    n1, = _rowwise("norm_pre_mix", lambda r0, xt, hd, wn: [_rms(h0_tile(r0, xt, hd), wn)], n_rows, T,
                   x_in, [w["norm_pre_mix"]], [(D_MODEL, BF16)], [])
    p = _matmul("in_proj", n1, w["w_cat"], "nn", F32)
    y_ssd, hin = run("ssd_fwd", _ssd_fwd, p, w["ssm_conv_w"], w["ssm_conv_b"], dt_bias, a_log, d_skip, n_chunks)
    ao, lse = run("attn_fwd", _attn_fwd, p, sinks, n_chunks)

    def gate_norm(r0, y, z, wn):
        return [_rms(y * _silu(z), wn)]

    yn, = run("ssm_gate_norm", _rowwise, "ssm_gate_norm", gate_norm, n_rows, tm,
              [(y_ssd, D_INNER, 0), (p, D_INNER, P_Z // D_INNER)], [w["ssm_norm"]], [(D_INNER, BF16)], [])
    y_ssm = _matmul("ssm_out", yn, w["w_ssm_out"], "nn", F32)
    y_attn = _matmul("attn_out", ao, w["w_attn_out"], "nn", F32)

    def mix_gate(r0, ys, ya, gs, ga):
        return [_sigmoid(gs) * ys + _sigmoid(ga) * ya]

    gate_ins = [(p, D_MODEL, P_GATE // D_MODEL), (p, D_MODEL, P_GATE // D_MODEL + 1)]
    mixed, = _rowwise("mix_gate", mix_gate, n_rows, tm, [(y_ssm, D_MODEL, 0), (y_attn, D_MODEL, 0)] + gate_ins,
                      [], [(D_MODEL, BF16)], [])
    mix = _matmul("mix_out", mixed, w["w_mix_out"], "nn", F32)

    def post_mix(r0, mx, xt, hd, w_post, w_pre):
        h1 = jnp.where(_valid_rows(r0, mx.shape[0], PAD), h0_tile(r0, xt, hd) + _rms(mx, w_post), 0.0)
        return [h1, _rms(h1, w_pre)]

    h1, n2 = _rowwise("post_mix", post_mix, n_rows, T, [(mix, D_MODEL, 0)] + x_in,
                      [w["norm_post_mix"], w["norm_pre_ffn"]], [(D_MODEL, F32), (D_MODEL, BF16)], [])
    u_raw = _matmul("ffn_up", n2, w["w_ffn_up"], "nn", F32)
    f = _ffn_act("ffn_act", u_raw, w["ffn_conv_w"], w["ffn_conv_b"], n_rows)
    ffn = _matmul("ffn_down", f, w["w_ffn_down"], "nn", F32)

    def final(r0, fo, h, t, w_post):
        real = r0 >= T
        err = jnp.where(real, h + _rms(fo, w_post) - t, 0.0)
        dy = err * (1.0 / D_MODEL)
        dffn, dw = _rms_bwd(dy, fo, w_post)
        return [dffn, dy, jnp.sum(err * err, axis=0, keepdims=True), dw]

    dffn, dh2, loss_cols, g_norm_post_ffn = _rowwise(
        "loss_head", final, n_rows, T, [(ffn, D_MODEL, 0), (h1, D_MODEL, 0), (tgt, D_MODEL, 0, "prev")],
        [w["norm_post_ffn"]], [(D_MODEL, BF16), (D_MODEL, F32)], [D_MODEL, D_MODEL])

    g["norm_post_ffn"] = g_norm_post_ffn
    g["w_ffn_down"] = _matmul("ffn_down_dw", f, dffn, "tn", F32)
    df = _matmul("ffn_down_dx", dffn, w["w_ffn_down"], "nt", F32)
    du_raw, dconv = _conv_bwd("ffn_act_bwd", u_raw, 0, [df], [(0, c0) for c0 in range(0, FFN_DIM, CONV_LANES)],
                              w["ffn_conv_w"], w["ffn_conv_b"], n_rows, True)
    g["ffn_conv_w"], g["ffn_conv_b"] = dconv[0:3], dconv[3:4]
    g["w_ffn_up"] = _matmul("ffn_up_dw", n2, du_raw, "tn", F32)
    dn2 = run("ffn_up_dx", _matmul, "ffn_up_dx", du_raw, w["w_ffn_up"], "nt", F32)

    def post_mix_bwd(r0, dn, d2, h, mx, w_pre, w_post):
        dx, dw_pre = _rms_bwd(dn, h, w_pre)
        dh1 = jnp.where(_valid_rows(r0, dn.shape[0], PAD), dx + d2, 0.0)
        dmix, dw_post = _rms_bwd(dh1, mx, w_post)
        return [dh1, dmix, dw_pre, dw_post]

    dh1, dmix, g["norm_pre_ffn"], g["norm_post_mix"] = _rowwise(
        "post_mix_bwd", post_mix_bwd, n_rows, tm,
        [(dn2, D_MODEL, 0), (dh2, D_MODEL, 0), (h1, D_MODEL, 0), (mix, D_MODEL, 0)],
        [w["norm_pre_ffn"], w["norm_post_mix"]], [(D_MODEL, F32), (D_MODEL, BF16)], [D_MODEL, D_MODEL])
    g["w_mix_out"] = _matmul("mix_out_dw", mixed, dmix, "tn", F32)
    dmixed = _matmul("mix_out_dx", dmix, w["w_mix_out"], "nt", F32)

    def mix_gate_bwd(r0, dm, ys, ya, gs, ga):
        ss, sa = _sigmoid(gs), _sigmoid(ga)
        dgate = jnp.concatenate([dm * ys * ss * (1.0 - ss), dm * ya * sa * (1.0 - sa)], axis=1)
        return [dm * ss, dm * sa, dgate]

    dys, dya, dp = _rowwise(
        "mix_gate_bwd", mix_gate_bwd, n_rows, tm,
        [(dmixed, D_MODEL, 0), (y_ssm, D_MODEL, 0), (y_attn, D_MODEL, 0)] + gate_ins,
        [], [(D_MODEL, BF16), (D_MODEL, BF16), (2 * D_MODEL, BF16, "new", P_W, P_GATE // (2 * D_MODEL))], [])
    g["w_ssm_out"] = _matmul("ssm_out_dw", yn, dys, "tn", F32)
    dyn = _matmul("ssm_out_dx", dys, w["w_ssm_out"], "nt", F32)
    g["w_attn_out"] = _matmul("attn_out_dw", ao, dya, "tn", F32)
    dao = _matmul("attn_out_dx", dya, w["w_attn_out"], "nt", BF16)

    def gate_norm_bwd(r0, dn, y, z, wn):
        sz, dsz = _silu_grad(z)
        dyz, dw = _rms_bwd(dn, y * sz, wn)
        live = _valid_rows(r0, dn.shape[0], PAD)
        return [jnp.where(live, dyz * sz, 0.0), jnp.where(live, dyz * y * dsz, 0.0), dw]

    dy_ssd, dp, g["ssm_norm"] = run(
        "ssm_gate_norm_bwd", _rowwise, "ssm_gate_norm_bwd", gate_norm_bwd, n_rows, tm,
        [(dyn, D_INNER, 0), (y_ssd, D_INNER, 0), (p, D_INNER, P_Z // D_INNER)],
        [w["ssm_norm"]], [(D_INNER, F32), (D_INNER, BF16, "into", dp, P_Z // D_INNER)], [D_INNER])
    dp, dsink = run("attn_bwd", _attn_bwd, p, sinks, ao, lse, dao, dp, n_chunks)
    g["attn_sinks"] = dsink[:, 0:ATTN_HEADS]
    dxs, dbm, dcm, dp, dpar = run("ssd_bwd", _ssd_bwd, p, w["ssm_conv_w"], w["ssm_conv_b"], dt_bias, a_log,
                                  d_skip, hin, dy_ssd, dp, n_chunks)
    dpar = jnp.sum(dpar, axis=0)
    g["ssm_dt_bias"], g["ssm_a_log"], g["ssm_d_skip"] = (dpar[i:i + 1, 0:SSM_HEADS] for i in range(3))
    x_chunks = [(src, c0) for src, arr in enumerate((dxs, dbm, dcm)) for c0 in range(0, arr.shape[1], CONV_LANES)]
    dp, dconv = run("ssm_conv_bwd", _conv_bwd, "ssm_conv_bwd", p, P_XBC // CONV_DIM, [dxs, dbm, dcm], x_chunks,
                    w["ssm_conv_w"], w["ssm_conv_b"], n_rows, False, into=dp, into_blk=P_XBC // CONV_DIM)
    g["ssm_conv_w"], g["ssm_conv_b"] = dconv[0:4], dconv[4:5]
    g["w_cat_t"] = _matmul("in_proj_dw", dp, n1, "tn", F32)
    dn1 = run("in_proj_dx", _matmul, "in_proj_dx", dp, w["w_cat"], "nt", F32)

    def pre_mix_bwd(r0, dn, d1, xt, hd, wn):
        dx, dw = _rms_bwd(dn, h0_tile(r0, xt, hd), wn)
        dh0 = jnp.where(_valid_rows(r0, dn.shape[0], PAD), dx + d1, 0.0)
        return [dh0, dh0, dw]

    dx_out, dhead, g["norm_pre_mix"] = _rowwise(
        "pre_mix_bwd", pre_mix_bwd, n_rows, T, [(dn1, D_MODEL, 0), (dh1, D_MODEL, 0)] + x_in,
        [w["norm_pre_mix"]], [(D_MODEL, F32, "prev", n_tok), (D_MODEL, F32, "first")], [D_MODEL])
    return jnp.sum(loss_cols), dx_out, dhead


_IN_SECTIONS = [((5152, 6176), P_Q), ((6176, 6432), P_K), ((6432, 6688), P_V), ((5120, 5152), P_DT),
                ((0, 2048), P_Z), ((6688, 8736), P_GATE), ((2048, 5120), P_XBC)]


IN_SHARD = N_IN // 4


def _shard_pieces(a, b):
    return [(j, max(a, j * IN_SHARD) - j * IN_SHARD, min(b, (j + 1) * IN_SHARD) - j * IN_SHARD)
            for j in range(4) if max(a, j * IN_SHARD) < min(b, (j + 1) * IN_SHARD)]


def _to_cat(w4):
    parts, at = [], 0
    for (a, b), off in _IN_SECTIONS:
        if off > at:
            parts.append(jnp.zeros((w4.shape[1], off - at), w4.dtype))
        parts += [w4[j, :, lo:hi] for j, lo, hi in _shard_pieces(a, b)]
        at = off + (b - a)
    return jnp.concatenate(parts, axis=1)


def _from_cat_t(g_cat_t):
    pieces = [(j, lo, off + j * IN_SHARD + lo - a, hi - lo)
              for (a, b), off in _IN_SECTIONS for j, lo, hi in _shard_pieces(a, b)]
    assert all(lo % 8 == 0 and start % 8 == 0 and n % 8 == 0 for _, lo, start, n in pieces)

    def body(src_ref, dst_ref, sems):
        copies = [pltpu.make_async_copy(src_ref.at[pl.ds(start, n), :], dst_ref.at[j, pl.ds(lo, n), :], sems.at[k])
                  for k, (j, lo, start, n) in enumerate(pieces)]
        for cp in copies:
            cp.start()
        for cp in copies:
            cp.wait()

    return pl.pallas_call(
        body, name="w_in_grad_relayout", in_specs=[ANY], out_specs=ANY,
        out_shape=jax.ShapeDtypeStruct((4, IN_SHARD, g_cat_t.shape[1]), g_cat_t.dtype),
        scratch_shapes=[pltpu.SemaphoreType.DMA((len(pieces),))],
    )(g_cat_t)


LANES = 1024
_BIG = [("w_in", 1024, 2184, "chip"), ("w_ssm_out", 512, 1024, "row"), ("w_attn_out", 256, 1024, "row"),
        ("w_mix_out", 256, 1024, "row"), ("w_ffn_up", 1024, 1408, "col"), ("w_ffn_down", 704, 1024, "row"),
        ("small", 32, LANES, "chip")]
_SMALL_SHARDED = [("ssm_conv_w", (4, 768), 1), ("ffn_conv_w", (3, 1408), 1), ("meta_tokens", (16, 256), 1)]
_REPLICATED = [("norm_pre_mix", 1024), ("ssm_conv_b", 3072), ("ssm_dt_bias", 32), ("ssm_a_log", 32),
               ("ssm_d_skip", 32), ("ssm_norm", 2048), ("attn_sinks", 16), ("norm_post_mix", 1024),
               ("norm_pre_ffn", 1024), ("ffn_conv_b", 5632), ("norm_post_ffn", 1024)]
SMALL_ROWS = 24


def _rep_rows():
    out, at = [], 0
    for _, width in _REPLICATED:
        out.append((at, -(-width // LANES)))
        at += out[-1][1]
    return out, at


def _in_rows(parts):
    rows = [jnp.pad(a, ((0, 0), (0, -a.shape[1] % LANES))).reshape(-1, LANES) for a in parts]
    flat = jnp.concatenate(rows, axis=0)
    return jnp.pad(flat, ((0, SMALL_ROWS - flat.shape[0]), (0, 0)))
WEIGHT_ORDER = ["meta_tokens", "norm_pre_mix", "w_in", "ssm_conv_w", "ssm_conv_b", "ssm_dt_bias", "ssm_a_log",
                "ssm_d_skip", "ssm_norm", "w_ssm_out", "attn_sinks", "w_attn_out", "w_mix_out", "norm_post_mix",
                "norm_pre_ffn", "w_ffn_up", "ffn_conv_w", "ffn_conv_b", "w_ffn_down", "norm_post_ffn"]


def _flatten(parts, rows):
    flat = jnp.concatenate([a.reshape(-1) for a in parts])
    return jnp.pad(flat, (0, rows * LANES - flat.shape[0])).reshape(rows, LANES)


def _unflatten(flat, shapes):
    flat = flat.reshape(-1)
    out, off = [], 0
    for shp in shapes:
        n = math.prod(shp)
        out.append(flat[off:off + n].reshape(shp))
        off += n
    return out


def _shard_of(full, chip, shape, axis):
    return lax.slice_in_dim(full, chip * shape[axis], (chip + 1) * shape[axis], axis=axis)


def _full_shape(r, c, layout):
    return {"row": (4 * r, c), "col": (r, 4 * c), "chip": (4, r, c), "chip_cols": (4, r, c)}[layout]


def _half_shape(r, c, layout):
    return (r, c // 2) if layout == "chip_cols" else (r // 2, c)


def _shard_view(ref, r, c, layout, chip):
    if layout == "row":
        return ref.at[pl.ds(pl.multiple_of(chip * r, 16), r), :]
    if layout == "col":
        return ref.at[:, pl.ds(pl.multiple_of(chip * c, 128), c)]
    return ref.at[chip]


def _half_view(ref, r, c, layout, chip, half):
    if layout == "chip_cols":
        return ref.at[chip, :, pl.ds(pl.multiple_of(half * (c // 2), 128), c // 2)]
    hr = r // 2
    if layout == "row":
        return ref.at[pl.ds(pl.multiple_of(chip * r + half * hr, 16), hr), :]
    r0 = pl.multiple_of(half * hr, 16)
    if layout == "col":
        return ref.at[pl.ds(r0, hr), pl.ds(pl.multiple_of(chip * c, 128), c)]
    return ref.at[chip, pl.ds(r0, hr), :]


def _mesh_pos():
    return lax.axis_index("x"), lax.axis_index("y"), lax.axis_index("c")


def _other_chips(x, y):
    return [(1 - x, y), (x, 1 - y), (1 - x, 1 - y)]


def _chip_index(x, y):
    return 2 * x + y


def _run_exchange(name, ex):
    n_in, n_out = len(ex.ins), len(ex.out_shapes)

    def body(*refs):
        in_refs, out_refs = refs[:n_in], refs[n_in:n_in + n_out]
        send_sems, recv_sems = refs[n_in + n_out:]
        copies = [pltpu.make_async_remote_copy(src_ref=s, dst_ref=d, send_sem=send_sems.at[i], recv_sem=recv_sems.at[i],
                                               device_id=dev, device_id_type=MESH)
                  for i, (s, d, dev) in enumerate(ex.make_copies(in_refs, out_refs))]
        assert len(copies) == ex.n_copies
        for cp in copies:
            cp.start()
        for cp in copies:
            cp.wait()

    return pl.pallas_call(
        body, name=name, in_specs=[ANY] * n_in, out_specs=[ANY] * n_out, out_shape=list(ex.out_shapes),
        scratch_shapes=[pltpu.SemaphoreType.DMA((ex.n_copies,)), pltpu.SemaphoreType.DMA((ex.n_copies,))],
        compiler_params=pltpu.CompilerParams(has_side_effects=True),
    )(*ex.ins)


def _join(*exs):
    def make(in_refs, out_refs):
        copies, i0, o0 = [], 0, 0
        for ex in exs:
            copies += ex.make_copies(in_refs[i0:i0 + len(ex.ins)], out_refs[o0:o0 + len(ex.out_shapes)])
            i0, o0 = i0 + len(ex.ins), o0 + len(ex.out_shapes)
        return copies

    aliases, i0, o0 = {}, 0, 0
    for ex in exs:
        aliases.update({i0 + k: o0 + v for k, v in ex.aliases.items()})
        i0, o0 = i0 + len(ex.ins), o0 + len(ex.out_shapes)
    return _Exchange([a for ex in exs for a in ex.ins], [s for ex in exs for s in ex.out_shapes], make,
                     sum(ex.n_copies for ex in exs), aliases)


def _split(exs, results):
    out, o0 = [], 0
    for ex in exs:
        out.append(list(results[o0:o0 + len(ex.out_shapes)]))
        o0 += len(ex.out_shapes)
    return out


def _gather_ici(entries, shards):
    def make(in_refs, out_refs):
        x, y, c = _mesh_pos()
        j = _chip_index(x, y)
        copies = []
        for ref_in, ref_out, (_, r, cc, lay) in zip(in_refs, out_refs, entries):
            copies.append((ref_in, _shard_view(ref_out, r, cc, lay, j), None))
            mine = ref_in.at[pl.ds(pl.multiple_of(c * (r // 2), 16), r // 2), :]
            copies += [(mine, _half_view(ref_out, r, cc, lay, j, c), (*ch, c)) for ch in _other_chips(x, y)]
        return copies

    shapes = [jax.ShapeDtypeStruct(_full_shape(r, cc, lay), s.dtype) for s, (_, r, cc, lay) in zip(shards, entries)]
    return _Exchange(list(shards), shapes, make, 4 * len(entries))


def _gather_pass_on(entries, fulls):
    def make(in_refs, out_refs):
        x, y, c = _mesh_pos()
        copies = []
        for ref, (_, r, cc, lay) in zip(out_refs, entries):
            for ch in _other_chips(x, y):
                landed = _half_view(ref, r, cc, lay, _chip_index(*ch), c)
                copies.append((landed, landed, (x, y, 1 - c)))
        return copies

    return _Exchange(list(fulls), [jax.ShapeDtypeStruct(f.shape, f.dtype) for f in fulls], make, 3 * len(entries),
                     {a: a for a in range(len(entries))})


def _gather_weights(entries, shards):
    n = len(entries)

    def body(*refs):
        ins, outs = refs[:n], refs[n:2 * n]
        send_sems, recv_sems, local_sems = refs[2 * n:]
        x, y, c = _mesh_pos()
        j = _chip_index(x, y)
        sibling = (x, y, 1 - c)
        chips = _other_chips(x, y)
        idx = [_chip_index(*ch) for ch in chips]

        def remote(k, src, dst, dev):
            return pltpu.make_async_remote_copy(src_ref=src, dst_ref=dst, send_sem=send_sems.at[k],
                                                recv_sem=recv_sems.at[k], device_id=dev, device_id_type=MESH)

        own = [pltpu.make_async_copy(ins[a], _shard_view(outs[a], r, cc, lay, j), local_sems.at[a])
               for a, (_, r, cc, lay) in enumerate(entries)]
        for cp in own:
            cp.start()
        first, passed = [], []
        for a, (_, r, cc, lay) in enumerate(entries):
            mine = ins[a].at[pl.ds(pl.multiple_of(c * (r // 2), 16), r // 2), :]
            for k, ch in enumerate(chips):
                first.append(remote(6 * a + k, mine, _half_view(outs[a], r, cc, lay, j, c), (*ch, c)))
                landed = _half_view(outs[a], r, cc, lay, idx[k], c)
                passed.append(remote(6 * a + 3 + k, landed, landed, sibling))
        for cp in first:
            cp.start()
        for a, (_, r, cc, lay) in enumerate(entries):
            for k in range(3):
                landed = _half_view(outs[a], r, cc, lay, idx[k], c)
                remote(6 * a + k, landed, landed, sibling).wait_recv()
                passed[3 * a + k].start()
        for a, (_, r, cc, lay) in enumerate(entries):
            for k in range(3):
                theirs = _half_view(outs[a], r, cc, lay, idx[k], 1 - c)
                remote(6 * a + 3 + k, theirs, theirs, sibling).wait_recv()
        for cp in first + passed:
            cp.wait_send()
        for cp in own:
            cp.wait()

    return pl.pallas_call(
        body, name="gather_weights", in_specs=[ANY] * n, out_specs=[ANY] * n,
        out_shape=[jax.ShapeDtypeStruct(_full_shape(r, cc, lay), s.dtype) for s, (_, r, cc, lay) in zip(shards, entries)],
        scratch_shapes=[pltpu.SemaphoreType.DMA((6 * n,)), pltpu.SemaphoreType.DMA((6 * n,)), pltpu.SemaphoreType.DMA((n,))],
        compiler_params=pltpu.CompilerParams(has_side_effects=True),
    )(*shards)


def _pair_exchange(entries, grads):
    def make(in_refs, out_refs):
        x, y, c = _mesh_pos()
        return [(_half_view(ref_in, r, cc, lay, i, 1 - c), ref_out.at[i], (x, y, 1 - c))
                for ref_in, ref_out, (_, r, cc, lay) in zip(in_refs, out_refs, entries) for i in range(4)]

    return _Exchange(list(grads), [jax.ShapeDtypeStruct((4,) + _half_shape(r, cc, lay), F32) for _, r, cc, lay in entries],
                     make, 4 * len(entries))


def _whole_to_sibling(arrays):
    def make(in_refs, out_refs):
        x, y, c = _mesh_pos()
        return [(r, o, (x, y, 1 - c)) for r, o in zip(in_refs, out_refs)]

    return _Exchange(list(arrays), [jax.ShapeDtypeStruct(a.shape, a.dtype) for a in arrays], make, len(arrays))


def _chip_exchange(psends):
    def make(in_refs, out_refs):
        x, y, c = _mesh_pos()
        return [(ref_in.at[_chip_index(*ch)], ref_out.at[k], (*ch, c))
                for ref_in, ref_out in zip(in_refs, out_refs) for k, ch in enumerate(_other_chips(x, y))]

    return _Exchange(list(psends), [jax.ShapeDtypeStruct((3,) + p.shape[1:], p.dtype) for p in psends], make,
                     3 * len(psends))


def _to_all_chips(array):
    def make(in_refs, out_refs):
        x, y, c = _mesh_pos()
        return [(in_refs[0], out_refs[0].at[k], (*ch, c)) for k, ch in enumerate(_other_chips(x, y))]

    return _Exchange([array], [jax.ShapeDtypeStruct((3,) + array.shape, array.dtype)], make, 3)


SUM_ROWS = 256
ADAM_ROWS = 128


def _pair_sum(name, grad, recv, ids, r, c, layout):
    hr, c = _half_shape(r, c, layout)
    tr = _row_tile(hr, SUM_ROWS)
    nb = hr // tr

    def body(ids_ref, g_ref, r_ref, send_ref, own_ref):
        s = g_ref[...] + r_ref[...]
        send_ref[...] = s.astype(send_ref.dtype)

        @pl.when(pl.program_id(1) == ids_ref[1])
        def _():
            own_ref[...] = s

    if layout == "row":
        g_spec = pl.BlockSpec((tr, c), lambda t, j, ids_ref: ((j * r + ids_ref[0] * hr) // tr + t, 0))
    elif layout == "col":
        g_spec = pl.BlockSpec((tr, c), lambda t, j, ids_ref: (ids_ref[0] * nb + t, j))
    elif layout == "chip_cols":
        g_spec = pl.BlockSpec((None, tr, c), lambda t, j, ids_ref: (j, t, ids_ref[0]))
    else:
        g_spec = pl.BlockSpec((None, tr, c), lambda t, j, ids_ref: (j, ids_ref[0] * nb + t, 0))
    grid_spec = pltpu.PrefetchScalarGridSpec(
        num_scalar_prefetch=1, grid=(nb, 4),
        in_specs=[g_spec, pl.BlockSpec((None, tr, c), lambda t, j, ids_ref: (j, t, 0))],
        out_specs=[pl.BlockSpec((None, tr, c), lambda t, j, ids_ref: (j, t, 0)),
                   pl.BlockSpec((tr, c), lambda t, j, ids_ref: (t, 0))])
    return pl.pallas_call(
        body, name=name, grid_spec=grid_spec,
        out_shape=[jax.ShapeDtypeStruct((4, hr, c), BF16), jax.ShapeDtypeStruct((hr, c), F32)],
        compiler_params=_cparams(2),
    )(ids, grad, recv)


def _chip_sum(name, own, recv):
    hr, c = own.shape
    tr = _row_tile(hr, SUM_ROWS)

    def body(o_ref, r_ref, out_ref):
        out_ref[...] = ((o_ref[...] + r_ref[0].astype(F32)) + r_ref[1].astype(F32)) + r_ref[2].astype(F32)

    return pl.pallas_call(
        body, name=name, grid=(hr // tr,),
        in_specs=[pl.BlockSpec((tr, c), lambda i: (i, 0)), pl.BlockSpec((3, tr, c), lambda i: (0, i, 0))],
        out_specs=pl.BlockSpec((tr, c), lambda i: (i, 0)),
        out_shape=jax.ShapeDtypeStruct((hr, c), F32), compiler_params=_cparams(1),
    )(own, recv)


def _chip_sum_small(own, recv, ids):
    def body(ids_ref, o_ref, r_ref, out_ref):
        j = ids_ref[1]
        total = None
        for i in range(4):
            m = jnp.bitwise_xor(i, j)
            term = jnp.where(m == 0, o_ref[...], jnp.where(m == 2, r_ref[0], jnp.where(m == 1, r_ref[1], r_ref[2])))
            total = term if total is None else total + term
        out_ref[...] = total

    grid_spec = pltpu.PrefetchScalarGridSpec(
        num_scalar_prefetch=1, grid=(1,),
        in_specs=[pl.BlockSpec(own.shape, lambda i, ids_ref: (0, 0)), pl.BlockSpec(recv.shape, lambda i, ids_ref: (0, 0, 0))],
        out_specs=pl.BlockSpec(own.shape, lambda i, ids_ref: (0, 0)))
    return pl.pallas_call(body, name="chip_sum_small", grid_spec=grid_spec,
                          out_shape=jax.ShapeDtypeStruct(own.shape, F32), compiler_params=_cparams(1))(ids, own, recv)


def _adamw(name, w, m, v, mine, theirs, ids):
    lead = (None,) * (w.ndim - 2)
    rows, cols = w.shape[-2:]
    half = rows // 2
    tr = _row_tile(half, ADAM_ROWS, unit=8)
    nb = half // tr
    c1 = 1.0 / (1.0 - ADAM_B1 ** ADAM_STEP)
    c2 = 1.0 / (1.0 - ADAM_B2 ** ADAM_STEP)

    def body(ids_ref, w_ref, m_ref, v_ref, mine_ref, theirs_ref, g_out, d_out, m_out, v_out):
        g = jnp.where(pl.program_id(0) == ids_ref[0], mine_ref[...], theirs_ref[...])
        m_new = ADAM_B1 * m_ref[...] + (1.0 - ADAM_B1) * g
        v_new = ADAM_B2 * v_ref[...] + (1.0 - ADAM_B2) * (g * g)
        d_out[...] = -ADAM_LR * ((m_new * c1) / (jnp.sqrt(v_new * c2) + ADAM_EPS) + ADAM_WD * w_ref[...])
        g_out[...] = g
        m_out[...] = m_new
        v_out[...] = v_new

    full = pl.BlockSpec(lead + (tr, cols), lambda h, i, ids_ref: (0,) * len(lead) + (h * nb + i, 0))
    part = pl.BlockSpec((tr, cols), lambda h, i, ids_ref: (i, 0))
    grid_spec = pltpu.PrefetchScalarGridSpec(num_scalar_prefetch=1, grid=(2, nb),
                                             in_specs=[full, full, full, part, part], out_specs=[full] * 4)
    return pl.pallas_call(
        body, name=name, grid_spec=grid_spec,
        out_shape=[jax.ShapeDtypeStruct(w.shape, F32)] * 4, compiler_params=_cparams(2),
    )(ids, w, m, v, mine, theirs)


def _adamw_whole(name, w, m, v, g):
    rows, cols = w.shape[-2:]
    tr = _row_tile(rows, 2 * ADAM_ROWS, unit=8)
    c1 = 1.0 / (1.0 - ADAM_B1 ** ADAM_STEP)
    c2 = 1.0 / (1.0 - ADAM_B2 ** ADAM_STEP)

    def body(w_ref, m_ref, v_ref, g_ref, g_out, d_out, m_out, v_out):
        g = g_ref[...]
        m_new = ADAM_B1 * m_ref[...] + (1.0 - ADAM_B1) * g
        v_new = ADAM_B2 * v_ref[...] + (1.0 - ADAM_B2) * (g * g)
        d_out[...] = -ADAM_LR * ((m_new * c1) / (jnp.sqrt(v_new * c2) + ADAM_EPS) + ADAM_WD * w_ref[...])
        g_out[...] = g
        m_out[...] = m_new
        v_out[...] = v_new

    full = pl.BlockSpec((None, tr, cols), lambda i: (0, i, 0))
    return pl.pallas_call(
        body, name=name, grid=(rows // tr,), in_specs=[full, full, full, pl.BlockSpec((tr, cols), lambda i: (i, 0))],
        out_specs=[full] * 4, out_shape=[jax.ShapeDtypeStruct(w.shape, F32)] * 4, compiler_params=_cparams(1),
    )(w, m, v, g)


def _adamw_replicated(g_rows, ws, ms, vs):
    n = len(ws)
    layout, _ = _rep_rows()
    c1 = 1.0 / (1.0 - ADAM_B1 ** ADAM_STEP)
    c2 = 1.0 / (1.0 - ADAM_B2 ** ADAM_STEP)

    def body(g_ref, *refs):
        w_refs, m_refs, v_refs = refs[0:n], refs[n:2 * n], refs[2 * n:3 * n]
        outs = refs[3 * n:]
        for k, (r0, rows) in enumerate(layout):
            width = w_refs[k].shape[1]
            g = jnp.concatenate([g_ref[r0 + j:r0 + j + 1, :] for j in range(rows)], axis=1)[:, 0:width]
            m_new = ADAM_B1 * m_refs[k][...] + (1.0 - ADAM_B1) * g
            v_new = ADAM_B2 * v_refs[k][...] + (1.0 - ADAM_B2) * (g * g)
            outs[k][...] = g
            outs[n + k][...] = -ADAM_LR * ((m_new * c1) / (jnp.sqrt(v_new * c2) + ADAM_EPS) + ADAM_WD * w_refs[k][...])
            outs[2 * n + k][...] = m_new
            outs[3 * n + k][...] = v_new

    res = pl.pallas_call(body, name="adamw_replicated",
                         out_shape=[jax.ShapeDtypeStruct(w.shape, F32) for _ in range(4) for w in ws])(g_rows, *ws, *ms, *vs)
    return [res[k * n:(k + 1) * n] for k in range(4)]


def _small_shard(parts):
    return _flatten(parts, _BIG[-1][1])


_ENTRY = {e[0]: e for e in _BIG}
_GRAD_ENTRY = {**_ENTRY, "w_in": ("w_in", IN_SHARD, D_MODEL, "chip_cols")}
FFN_MATS = ("w_ffn_down", "w_ffn_up")
MIXER_MATS = ("w_mix_out", "w_ssm_out", "w_attn_out")


class _StepPlan:
    def __init__(self, w, late_shards, shards, ids):
        self.w, self.g = w, {}
        self.late_shards, self.shards, self.ids = late_shards, shards, ids
        self.sums, self.halves, self.results = {}, {}, {}

    def run(self, name, fn, *args, **kw):
        at = getattr(self, "_at_" + name, None)
        if at is None:
            return fn(*args, **kw)
        exchange, landed = at()
        res, extra = fn(*args, bg=exchange, **kw)
        landed(extra)
        return res

    def _at_ssd_fwd(self):
        def landed(fulls):
            self.partly_gathered = fulls

        return _gather_ici([_ENTRY[n] for n in MIXER_MATS], [self.late_shards[n] for n in MIXER_MATS]), landed

    def _at_attn_fwd(self):
        stages = (_gather_pass_on([_ENTRY[n] for n in MIXER_MATS], self.partly_gathered),
                  _gather_ici([_ENTRY[n] for n in FFN_MATS], [self.late_shards[n] for n in FFN_MATS]))

        def landed(extra):
            mixer, self.partly_gathered = _split(stages, extra)
            self.w.update(zip(MIXER_MATS, mixer))

        return _join(*stages), landed

    def _at_ssm_gate_norm(self):
        return (_gather_pass_on([_ENTRY[n] for n in FFN_MATS], self.partly_gathered),
                lambda fulls: self.w.update(zip(FFN_MATS, fulls)))

    def pair_sums(self, names, grads, recv):
        for n, gr, rv in zip(names, grads, recv):
            _, r, c, lay = _GRAD_ENTRY[n]
            self.sums[n] = _pair_sum("pair_sum_" + n, gr, rv, self.ids, r, c, lay)

    def chip_sums(self, names, recv):
        for n, rv in zip(names, recv):
            self.halves[n] = _chip_sum("chip_sum_" + n, self.sums[n][1], rv)

    def adamw(self, names, theirs):
        for n, th in zip(names, theirs):
            sh = self.shards[n]
            if n == "w_in":
                mine_first = self.ids[0] == 0
                g_t = jnp.where(mine_first, jnp.concatenate([self.halves[n], th], axis=1),
                                jnp.concatenate([th, self.halves[n]], axis=1))
                res = _adamw_whole("adamw_" + n, *[jnp.swapaxes(sh[k], -1, -2) for k in ("w", "m", "v")], g_t)
                self.results[n] = [jnp.swapaxes(r, -1, -2) for r in res]
            else:
                self.results[n] = _adamw("adamw_" + n, sh["w"], sh["m"], sh["v"], self.halves[n], th, self.ids)

    def _pair_stage(self, names, grads):
        return (_pair_exchange([_GRAD_ENTRY[n] for n in names], grads),
                lambda recv: self.pair_sums(names, grads, recv))

    def _at_ffn_up_dx(self):
        return self._pair_stage(FFN_MATS, [self.g[n] for n in FFN_MATS])

    def _at_ssm_gate_norm_bwd(self):
        return self._pair_stage(MIXER_MATS, [self.g[n] for n in MIXER_MATS])

    def _at_attn_bwd(self):
        return _chip_exchange([self.sums[n][0] for n in FFN_MATS]), lambda recv: self.chip_sums(FFN_MATS, recv)

    def _at_ssd_bwd(self):
        stages = (_chip_exchange([self.sums[n][0] for n in MIXER_MATS]),
                  _whole_to_sibling([self.halves[n] for n in FFN_MATS]))

        def landed(extra):
            recv, theirs = _split(stages, extra)
            self.chip_sums(MIXER_MATS, recv)
            self.adamw(FFN_MATS, theirs)

        return _join(*stages), landed

    def _at_ssm_conv_bwd(self):
        return _whole_to_sibling([self.halves[n] for n in MIXER_MATS]), lambda theirs: self.adamw(MIXER_MATS, theirs)

    def _at_in_proj_dx(self):
        grads = [_from_cat_t(self.g.pop("w_cat_t"))]
        self.pair_sums(("w_in",), grads,
                       _run_exchange("grad_pair_exchange_w_in", _pair_exchange([_GRAD_ENTRY["w_in"]], grads)))
        return _chip_exchange([self.sums["w_in"][0]]), lambda recv: self.chip_sums(("w_in",), recv)

    def finish(self, g_small, g_rep, rep_shards):
        stages = (_pair_exchange([_ENTRY["small"]], [g_small]), _whole_to_sibling([g_rep]))
        recv_small, recv_rep = _split(stages, _run_exchange("grad_pair_exchange_tail", _join(*stages)))
        self.pair_sums(("small",), [g_small], recv_small)
        p_rep, = _rowwise("pair_sum_replicated", lambda r0, a, b: [a + b], SMALL_ROWS, SMALL_ROWS,
                          [(g_rep, LANES, 0), (recv_rep[0], LANES, 0)], [], [(LANES, F32)], [])
        stages = (_chip_exchange([self.sums["small"][0]]), _to_all_chips(p_rep))
        recv, recv_rep = _split(stages, _run_exchange("grad_chip_exchange_tail", _join(*stages)))
        self.chip_sums(("small",), recv)
        g_rep_tot = _chip_sum_small(p_rep, recv_rep[0], self.ids)
        last = ("w_in", "small")
        self.adamw(last, _run_exchange("grad_half_share_tail", _whole_to_sibling([self.halves[n] for n in last])))
        self.results["replicated"] = _adamw_replicated(g_rep_tot, rep_shards["w"], rep_shards["m"], rep_shards["v"])
        return g_rep_tot[_rep_rows()[1], 0]


def kernel(x, meta_tokens, norm_pre_mix, w_in, ssm_conv_w, ssm_conv_b, ssm_dt_bias, ssm_a_log, ssm_d_skip, ssm_norm, w_ssm_out, attn_sinks, w_attn_out, w_mix_out, norm_post_mix, norm_pre_ffn, w_ffn_up, ffn_conv_w, ffn_conv_b, w_ffn_down, norm_post_ffn, loss_target, m_meta_tokens, m_norm_pre_mix, m_w_in, m_ssm_conv_w, m_ssm_conv_b, m_ssm_dt_bias, m_ssm_a_log, m_ssm_d_skip, m_ssm_norm, m_w_ssm_out, m_attn_sinks, m_w_attn_out, m_w_mix_out, m_norm_post_mix, m_norm_pre_ffn, m_w_ffn_up, m_ffn_conv_w, m_ffn_conv_b, m_w_ffn_down, m_norm_post_ffn, v_meta_tokens, v_norm_pre_mix, v_w_in, v_ssm_conv_w, v_ssm_conv_b, v_ssm_dt_bias, v_ssm_a_log, v_ssm_d_skip, v_ssm_norm, v_w_ssm_out, v_attn_sinks, v_w_attn_out, v_w_mix_out, v_norm_post_mix, v_norm_pre_ffn, v_w_ffn_up, v_ffn_conv_w, v_ffn_conv_b, v_w_ffn_down, v_norm_post_ffn):
    args = dict(locals())
    squeeze = lambda a: a.reshape(a.shape[-2:])
    wts = {n: squeeze(args[n]) for n in WEIGHT_ORDER}
    mom = {n: squeeze(args["m_" + n]) for n in WEIGHT_ORDER}
    var = {n: squeeze(args["v_" + n]) for n in WEIGHT_ORDER}
    x_i, y_i, c_i = _mesh_pos()
    ids = jnp.stack([c_i, _chip_index(x_i, y_i)]).astype(jnp.int32)
    big_names = [n for n, _, _, _ in _BIG[:-1]]
    small_names = [n for n, _, _ in _SMALL_SHARDED]
    rep_names = [n for n, _ in _REPLICATED]

    stacks = {"w": wts, "m": mom, "v": var}
    shards = {n: {"w": args[n], "m": args["m_" + n], "v": args["v_" + n]} for n in big_names}
    shards["small"] = {k: _small_shard([d[n] for n in small_names]) for k, d in stacks.items()}
    rep_shards = {k: [d[n] for n in rep_names] for k, d in stacks.items()}

    w_in4, small_all = _gather_weights([_ENTRY["w_in"], _ENTRY["small"]], [wts["w_in"].astype(BF16), shards["small"]["w"]])
    w = {n: wts[n] for n in rep_names}
    w["w_cat"] = _to_cat(w_in4)
    small_parts = [_unflatten(small_all[i], [shp for _, shp, _ in _SMALL_SHARDED]) for i in range(4)]
    for k, (n, _, axis) in enumerate(_SMALL_SHARDED):
        w[n] = jnp.concatenate([small_parts[i][k] for i in range(4)], axis=axis)
    plan = _StepPlan(w, {n: wts[n].astype(BF16) for n in MIXER_MATS + FFN_MATS}, shards, ids)

    head = jnp.concatenate([jnp.zeros((PAD, D_MODEL), F32), w["meta_tokens"]], axis=0)
    loss_sum, dx, dhead = _local_step(x[0], head, loss_target[0], plan)
    g = plan.g
    g["meta_tokens"] = dhead[PAD:]
    g_small = jnp.stack([_small_shard([_shard_of(g[n], i, shp, ax) for n, shp, ax in _SMALL_SHARDED]) for i in range(4)])
    loss_part = (loss_sum * (0.5 / D_MODEL)).reshape(1, 1)
    loss = plan.finish(g_small, _in_rows([g[n] for n in rep_names] + [loss_part]), rep_shards)

    results = {}
    for kind in range(4):
        results.update({(kind, n): plan.results[n][kind] for n in big_names})
        parts = _unflatten(plan.results["small"][kind], [shp for _, shp, _ in _SMALL_SHARDED])
        results.update({(kind, n): parts[k] for k, n in enumerate(small_names)})
        results.update({(kind, n): plan.results["replicated"][kind][k] for k, n in enumerate(rep_names)})
    outs = [results[kind, n].reshape(args[n].shape) for kind in range(4) for n in WEIGHT_ORDER]
    return (loss, dx[None], *outs)
```

```python
import math
from typing import Any, Callable, NamedTuple, Sequence

import jax
import jax.numpy as jnp
from jax import lax
from jax.experimental import pallas as pl
from jax.experimental.pallas import tpu as pltpu

F32 = jnp.float32
BF16 = jnp.bfloat16

D_MODEL = 1024
N_META = 16
T = 128
PAD = T - N_META
D_INNER = 2048
SSM_HEADS = 32
HEAD_P = 64
SSM_GROUPS = 4
GROUP_W = D_INNER // SSM_GROUPS
D_STATE = 128
CONV_DIM = D_INNER + 2 * SSM_GROUPS * D_STATE
ATTN_HEADS = 16
KV_HEADS = 4
ATTN_W = 1024
KV_W = 256
FFN_DIM = 2816
N_IN = 8736
EPS = 1e-6
NEG = -1e30
SCALE = 0.125

P_Q, P_K, P_V, P_DT, P_Z, P_GATE, P_XBC = 0, 1024, 1280, 1536, 2048, 4096, 6144
QKV_W = 1536
P_W = 9216

ADAM_LR, ADAM_B1, ADAM_B2, ADAM_EPS, ADAM_WD, ADAM_STEP = 0.001, 0.9, 0.999, 1e-08, 0.01, 10

VMEM_BUDGET = 40 * 1024 * 1024
VMEM_LIMIT = 56 * 1024 * 1024
MESH = pl.DeviceIdType.MESH
ANY = pl.BlockSpec(memory_space=pl.ANY)


def _cparams(n_axes, **kw):
    return pltpu.CompilerParams(dimension_semantics=("arbitrary",) * n_axes, vmem_limit_bytes=VMEM_LIMIT, **kw)


class _Exchange(NamedTuple):
    ins: Sequence[Any]
    out_shapes: Sequence[Any]
    make_copies: Callable
    n_copies: int
    aliases: dict = {}


def _call(body, name, grid, in_specs, out_specs, out_shape, operands, scratch_shapes=(), aliases=None, bg=None):
    aliases = dict(aliases or {})
    if bg is None:
        return pl.pallas_call(body, name=name, grid=grid, in_specs=in_specs, out_specs=out_specs, out_shape=out_shape,
                              scratch_shapes=list(scratch_shapes), input_output_aliases=aliases,
                              compiler_params=_cparams(len(grid)))(*operands)
    n_in, n_out, n_scr = len(in_specs), len(out_specs), len(scratch_shapes)
    nb_in, nb_out = len(bg.ins), len(bg.out_shapes)

    def hosted(*refs):
        ins, bg_ins = refs[:n_in], refs[n_in:n_in + nb_in]
        outs = refs[n_in + nb_in:n_in + nb_in + n_out]
        bg_outs = refs[n_in + nb_in + n_out:n_in + nb_in + n_out + nb_out]
        scratch = refs[n_in + nb_in + n_out + nb_out:n_in + nb_in + n_out + nb_out + n_scr]
        send_sems, recv_sems = refs[-2:]
        pids = [pl.program_id(a) for a in range(len(grid))]
        first, last = pids[0] == 0, pids[0] == grid[0] - 1
        for p, g in zip(pids[1:], grid[1:]):
            first, last = first & (p == 0), last & (p == g - 1)
        copies = []
        for k, (src, dst, peer) in enumerate(bg.make_copies(bg_ins, bg_outs)):
            if peer is None:
                copies.append(pltpu.make_async_copy(src, dst, send_sems.at[k]))
            else:
                copies.append(pltpu.make_async_remote_copy(src_ref=src, dst_ref=dst, send_sem=send_sems.at[k],
                                                           recv_sem=recv_sems.at[k], device_id=peer, device_id_type=MESH))
        assert len(copies) == bg.n_copies

        @pl.when(first)
        def _():
            for cp in copies:
                cp.start()

        body(*ins, *outs, *scratch)

        @pl.when(last)
        def _():
            for cp in copies:
                cp.wait()

    aliases = {(k if k < n_in else k + nb_in): v for k, v in aliases.items()}
    aliases.update({n_in + k: n_out + v for k, v in bg.aliases.items()})
    res = pl.pallas_call(
        hosted, name=name, grid=grid, in_specs=list(in_specs) + [ANY] * nb_in, out_specs=list(out_specs) + [ANY] * nb_out,
        out_shape=list(out_shape) + list(bg.out_shapes), input_output_aliases=aliases,
        scratch_shapes=list(scratch_shapes) + [pltpu.SemaphoreType.DMA((bg.n_copies,))] * 2,
        compiler_params=_cparams(len(grid), has_side_effects=True))(*operands, *bg.ins)
    return res[:n_out], res[n_out:]


def _sigmoid(x):
    return 1.0 / (1.0 + jnp.exp(-x))


def _silu(x):
    return x * _sigmoid(x)


def _silu_grad(x):
    s = _sigmoid(x)
    return x * s, s * (1.0 + x * (1.0 - s))


def _dsilu(x):
    return _silu_grad(x)[1]


def _softplus(x):
    e = jnp.exp(-jnp.abs(x))
    small = e * (1.0 - e * (0.5 - e * (1.0 / 3.0)))
    return jnp.maximum(x, 0.0) + jnp.where(e < 0.01, small, jnp.log(1.0 + e))


def _rms(x, w):
    r = lax.rsqrt(jnp.mean(x * x, axis=-1, keepdims=True) + EPS)
    return x * r * w


def _rms_bwd(dy, x, w):
    r = lax.rsqrt(jnp.mean(x * x, axis=-1, keepdims=True) + EPS)
    xh = x * r
    g = dy * w
    dx = r * (g - xh * jnp.mean(g * xh, axis=-1, keepdims=True))
    dw = jnp.sum(dy * xh, axis=0, keepdims=True)
    return dx, dw


def _dot(a, b):
    return jnp.dot(a, b, preferred_element_type=F32)


def _dot_nt(a, b):
    return lax.dot_general(a, b, (((1,), (1,)), ((), ())), preferred_element_type=F32)


def _dot_tn(a, b):
    return lax.dot_general(a, b, (((0,), (0,)), ((), ())), preferred_element_type=F32)


def _split3(x):
    hi = x.astype(BF16)
    r = x - hi.astype(F32)
    mid = r.astype(BF16)
    lo = (r - mid.astype(F32)).astype(BF16)
    return hi, mid, lo


def _xdot(x, e):
    hi, mid, lo = _split3(x)
    return _dot(hi, e) + _dot(mid, e) + _dot(lo, e)


def _xdot_l(e, x):
    hi, mid, lo = _split3(x)
    return _dot(e, hi) + _dot(e, mid) + _dot(e, lo)


def _iota(shape, dim):
    return lax.broadcasted_iota(jnp.int32, shape, dim)


def _divisors(n, unit):
    return [t for t in range(unit, n + 1, unit) if n % t == 0]


MIN_MATMUL_STEPS = 8


def _matmul_tiles(m, n, k, a_bytes, b_bytes, o_bytes, m_unit):
    best = None
    for tm in _divisors(m, m_unit):
        for tn in _divisors(n, 128):
            for tk in _divisors(k, 128):
                acc = 0 if tk == k else tm * tn * 4
                vm = 2 * (tm * tk * a_bytes + tk * tn * b_bytes + tm * tn * o_bytes) + acc
                if vm > VMEM_BUDGET:
                    continue
                steps = (m // tm) * (n // tn) * (k // tk)
                score = (tk == k, min(steps, MIN_MATMUL_STEPS), min(tm, 256), tm * tn * tk)
                if best is None or score > best[0]:
                    best = (score, (tm, tn, tk))
    return best[1]


def _matmul(name, a, b, mode, out_dtype, bg=None):
    if mode == "nn":
        (m, k), n = a.shape, b.shape[1]
    elif mode == "nt":
        (m, k), n = a.shape, b.shape[0]
    else:
        (k, m), n = a.shape, b.shape[1]
    ab, bb, ob = a.dtype.itemsize, b.dtype.itemsize, jnp.dtype(out_dtype).itemsize
    tm, tn, tk = _matmul_tiles(m, n, k, ab, bb, ob, 128 if mode == "tn" else 16)
    nk = k // tk
    dot = {"nn": _dot, "nt": _dot_nt, "tn": _dot_tn}[mode]

    def body(a_ref, b_ref, o_ref, *scratch):
        prod = dot(a_ref[...].astype(BF16), b_ref[...].astype(BF16))
        if nk == 1:
            o_ref[...] = prod.astype(o_ref.dtype)
        else:
            acc_ref, = scratch
            kk = pl.program_id(2)

            @pl.when(kk == 0)
            def _():
                acc_ref[...] = prod

            @pl.when(kk > 0)
            def _():
                acc_ref[...] += prod

            @pl.when(kk == nk - 1)
            def _():
                o_ref[...] = acc_ref[...].astype(o_ref.dtype)

    a_spec = pl.BlockSpec((tk, tm), lambda i, j, kk: (kk, i)) if mode == "tn" else pl.BlockSpec((tm, tk), lambda i, j, kk: (i, kk))
    b_spec = pl.BlockSpec((tn, tk), lambda i, j, kk: (j, kk)) if mode == "nt" else pl.BlockSpec((tk, tn), lambda i, j, kk: (kk, j))
    res = _call(body, name, (m // tm, n // tn, nk), [a_spec, b_spec], [pl.BlockSpec((tm, tn), lambda i, j, kk: (i, j))],
                [jax.ShapeDtypeStruct((m, n), out_dtype)], [a, b],
                scratch_shapes=[] if nk == 1 else [pltpu.VMEM((tm, tn), F32)], bg=bg)
    return res[0] if bg is None else (res[0][0], res[1])


def _row_tile(n_rows, cap, unit=16):
    return max([t for t in _divisors(n_rows, unit) if t <= cap], default=n_rows)


ROW_SUB = 384
GROUP_UNROLL = 4


def _rowwise(name, fn, n_rows, tm, row_ins, full_ins, row_outs, acc_outs, bg=None):
    n_in = len(row_ins) + len(full_ins)
    n_ro = len(row_outs)
    into = [(k, o[3]) for k, o in enumerate(row_outs) if len(o) > 2 and o[2] == "into"]

    n_row_in = len(row_ins)
    sub = min(tm, ROW_SUB)

    def body(*refs):
        i = pl.program_id(0)
        outs = refs[n_in + len(into):]

        sums = tuple(jnp.zeros((1, w), F32) for w in acc_outs)
        for s in range(tm // sub):
            rows = pl.ds(s * sub, sub)
            vals = [r[rows, :] for r in refs[:n_row_in]] + [r[...] for r in refs[n_row_in:n_in]]
            res = fn(i * tm + s * sub, *vals)
            for o, r, v in zip(row_outs, outs[:n_ro], res[:n_ro]):
                if len(o) > 2 and o[2] == "first":
                    @pl.when(i == 0)
                    def _(r=r, v=v, rows=rows):
                        r[rows, :] = v.astype(r.dtype)
                else:
                    r[rows, :] = v.astype(r.dtype)
            sums = tuple(a + v for a, v in zip(sums, res[n_ro:]))

        @pl.when(i == 0)
        def _():
            for r, v in zip(outs[n_ro:], sums):
                r[...] = v

        @pl.when(i > 0)
        def _():
            for r, v in zip(outs[n_ro:], sums):
                r[...] += v

    def in_spec(entry):
        w, cb = entry[1], entry[2]
        if len(entry) > 3 and entry[3] == "prev":
            return pl.BlockSpec((tm, w), lambda i: (jnp.maximum(i - 1, 0), cb))
        if len(entry) > 3 and entry[3] == "first":
            return pl.BlockSpec((tm, w), lambda i: (0, cb))
        return pl.BlockSpec((tm, w), lambda i: (i, cb))

    def out_spec(o):
        if len(o) == 2:
            return pl.BlockSpec((tm, o[0]), lambda i: (i, 0)), jax.ShapeDtypeStruct((n_rows, o[0]), o[1])
        if o[2] == "new":
            return pl.BlockSpec((tm, o[0]), lambda i: (i, o[4])), jax.ShapeDtypeStruct((n_rows, o[3]), o[1])
        if o[2] == "into":
            return pl.BlockSpec((tm, o[0]), lambda i: (i, o[4])), jax.ShapeDtypeStruct(o[3].shape, o[3].dtype)
        if o[2] == "first":
            return pl.BlockSpec((tm, o[0]), lambda i: (0, 0)), jax.ShapeDtypeStruct((tm, o[0]), o[1])
        return pl.BlockSpec((tm, o[0]), lambda i: (jnp.maximum(i - 1, 0), 0)), jax.ShapeDtypeStruct((o[3], o[0]), o[1])

    in_specs = [in_spec(e) for e in row_ins]
    in_specs += [pl.BlockSpec(a.shape, lambda i: (0, 0)) for a in full_ins]
    in_specs += [pl.BlockSpec(memory_space=pl.ANY) for _ in into]
    specs_shapes = [out_spec(o) for o in row_outs]
    out_specs = [s for s, _ in specs_shapes] + [pl.BlockSpec((1, w), lambda i: (0, 0)) for w in acc_outs]
    out_shape = [s for _, s in specs_shapes] + [jax.ShapeDtypeStruct((1, w), F32) for w in acc_outs]
    return _call(body, name, (n_rows // tm,), in_specs, out_specs, out_shape,
                 [e[0] for e in row_ins] + list(full_ins) + [arr for _, arr in into],
                 aliases={n_in + a: k for a, (k, _) in enumerate(into)}, bg=bg)


def _valid_rows(first_row, tm, lo):
    return (first_row + _iota((tm, 1), 0)) >= lo


CONV_ROWS = 128
CONV_SUB = 16
CONV_LANES = 256


def _conv_specs(tm, width, blk, n_rows, after):
    specs = [pl.BlockSpec((tm, width), lambda i: (i, blk)),
             pl.BlockSpec((8, width), lambda i: (jnp.maximum(i * (tm // 8) - 1, 0), blk))]
    if after:
        specs.append(pl.BlockSpec((16, width), lambda i: (jnp.minimum((i + 1) * (tm // 16), n_rows // 16 - 1), blk)))
    return specs


def _conv_window(win, w_ref, b_ref, taps, c0, cw, n):
    acc = b_ref[:, c0:c0 + cw] + w_ref[taps - 1:taps, c0:c0 + cw] * win[8:8 + n]
    for k in range(taps - 1):
        acc = acc + w_ref[k:k + 1, c0:c0 + cw] * win[8 - (taps - 1) + k:8 - (taps - 1) + k + n]
    return acc


def _ffn_act(name, u_raw, conv_w, conv_b, n_rows):
    tm, sub, cw = CONV_ROWS, CONV_SUB, CONV_LANES
    taps, width = conv_w.shape
    half = width // 2

    def body(cur_ref, prev_ref, w_ref, b_ref, f_ref, ext_ref):
        i = pl.program_id(0)
        ext_ref[0:8, :] = jnp.where(i > 0, prev_ref[...], 0.0)
        ext_ref[8:8 + tm, :] = cur_ref[...]
        for q in range(half // cw):
            a0, g0 = q * cw, half + q * cw

            def group(s, carry):
                r = pl.multiple_of(s * sub, sub)
                a = _conv_window(ext_ref[pl.ds(r, sub + 8), a0:a0 + cw], w_ref, b_ref, taps, a0, cw, sub)
                g = _conv_window(ext_ref[pl.ds(r, sub + 8), g0:g0 + cw], w_ref, b_ref, taps, g0, cw, sub)
                f = jnp.where(_valid_rows(i * tm + r, sub, PAD), _silu(a) * g, 0.0)
                f_ref[pl.ds(r, sub), a0:a0 + cw] = f.astype(f_ref.dtype)
                return carry

            lax.fori_loop(0, tm // sub, group, 0, unroll=GROUP_UNROLL)

    return pl.pallas_call(
        body, name=name, grid=(n_rows // tm,),
        in_specs=_conv_specs(tm, width, 0, n_rows, False) + [pl.BlockSpec((taps, width), lambda i: (0, 0)),
                                                             pl.BlockSpec((1, width), lambda i: (0, 0))],
        out_specs=pl.BlockSpec((tm, half), lambda i: (i, 0)),
        out_shape=jax.ShapeDtypeStruct((n_rows, half), BF16),
        scratch_shapes=[pltpu.VMEM((tm + 8, width), F32)],
        compiler_params=_cparams(1),
    )(u_raw, u_raw, conv_w, conv_b)


def _conv_bwd(name, raw, raw_blk, dsrcs, chunk_src, conv_w, conv_b, n_rows, gated, into=None, into_blk=0, bg=None):
    taps, width = conv_w.shape
    half = width // 2 if gated else width
    tm, sub, cw = CONV_ROWS, CONV_SUB, CONV_LANES
    te = tm + 16
    nd = len(dsrcs)
    n_parts = 2 if gated else 1

    def body(*refs):
        cur_ref, prev_ref, next_ref = refs[0:3]
        dcur, dnext = refs[3:3 + nd], refs[3 + nd:3 + 2 * nd]
        w_ref, b_ref = refs[3 + 2 * nd:5 + 2 * nd]
        out_ref, acc_ref, ext_ref, du_ref = refs[-4:]
        i = pl.program_id(0)
        ext_ref[0:8, :] = jnp.where(i > 0, prev_ref[...], 0.0)
        ext_ref[8:8 + tm, :] = cur_ref[...]
        ext_ref[8 + tm:24 + tm, :] = next_ref[...]

        for q, (src, off) in enumerate(chunk_src):
            cols = [q * cw, half + q * cw][:n_parts]

            def conv_grad(r, d):
                pre = [_conv_window(ext_ref[pl.ds(r, sub + 8), c0:c0 + cw], w_ref, b_ref, taps, c0, cw, sub) for c0 in cols]
                row = i * tm + r + _iota((sub, 1), 0)
                live = (row >= PAD) & (row < n_rows)
                if gated:
                    act, dact = _silu_grad(pre[0])
                    dus = [d * pre[1] * dact, d * act]
                else:
                    dus = [d * _dsilu(pre[0])]
                for part, du in enumerate(dus):
                    du_ref[part, pl.ds(r, sub), :] = jnp.where(live, du, 0.0)

            def tile_rows(s, carry):
                r = pl.multiple_of(s * sub, sub)
                conv_grad(r, dcur[src][pl.ds(r, sub), off:off + cw].astype(F32))
                return carry

            lax.fori_loop(0, tm // sub, tile_rows, 0, unroll=GROUP_UNROLL)
            conv_grad(tm, dnext[src][:, off:off + cw].astype(F32))

            for part, c0 in enumerate(cols):
                taps_w = [w_ref[k:k + 1, c0:c0 + cw] for k in range(taps)]

                def back(s, sums):
                    new = list(sums)
                    for u in range(2):
                        r = pl.multiple_of((2 * s + u) * sub, sub)
                        win = du_ref[part, pl.ds(r, sub + 8), :]
                        raw_rows = ext_ref[pl.ds(8 + r, sub), c0:c0 + cw]
                        draw = jnp.zeros((sub, cw), F32)
                        for k in range(taps):
                            shifted = win[taps - 1 - k:taps - 1 - k + sub]
                            draw = draw + taps_w[k] * shifted
                            new[k] = new[k] + shifted * raw_rows
                        new[taps] = new[taps] + win[0:sub]
                        out_ref[pl.ds(r, sub), c0:c0 + cw] = jnp.where(_valid_rows(i * tm + r, sub, PAD), draw, 0.0).astype(out_ref.dtype)
                    return tuple(new)

                sums = lax.fori_loop(0, tm // (2 * sub), back, tuple(jnp.zeros((sub, cw), F32) for _ in range(taps + 1)))
                for k in range(taps + 1):
                    total = jnp.sum(sums[k], axis=0, keepdims=True)
                    acc_ref[k:k + 1, c0:c0 + cw] = jnp.where(i == 0, total, acc_ref[k:k + 1, c0:c0 + cw] + total)

    in_specs = _conv_specs(tm, width, raw_blk, n_rows, True)
    in_specs += [pl.BlockSpec((tm, d.shape[1]), lambda i: (i, 0)) for d in dsrcs]
    in_specs += [pl.BlockSpec((16, d.shape[1]), lambda i: (jnp.minimum((i + 1) * (tm // 16), n_rows // 16 - 1), 0)) for d in dsrcs]
    in_specs += [pl.BlockSpec((taps, width), lambda i: (0, 0)), pl.BlockSpec((1, width), lambda i: (0, 0))]
    operands = [raw, raw, raw] + list(dsrcs) + list(dsrcs) + [conv_w, conv_b]
    aliases = {}
    if into is None:
        out0 = jax.ShapeDtypeStruct((n_rows, width), BF16)
    else:
        in_specs.append(pl.BlockSpec(memory_space=pl.ANY))
        operands.append(into)
        aliases = {len(operands) - 1: 0}
        out0 = jax.ShapeDtypeStruct(into.shape, into.dtype)
    return _call(body, name, (n_rows // tm,), in_specs,
                 [pl.BlockSpec((tm, width), lambda i: (i, into_blk)), pl.BlockSpec((8, width), lambda i: (0, 0))],
                 [out0, jax.ShapeDtypeStruct((8, width), F32)], operands,
                 scratch_shapes=[pltpu.VMEM((tm + 24, width), F32), pltpu.VMEM((n_parts, te + 8, cw), F32)],
                 aliases=aliases, bg=bg)


def _ssd_specs(n_chunks, rev, per_step=1):
    cidx = (lambda c: n_chunks - 1 - c) if rev else (lambda c: c)
    xw, nw = per_step * GROUP_W, per_step * D_STATE
    xg0, bg0, cg0 = P_XBC // xw, (P_XBC + D_INNER) // nw, (P_XBC + D_INNER + SSM_GROUPS * D_STATE) // nw

    def cur(width, blk0):
        return pl.BlockSpec((T, width), lambda g, c: (cidx(c), blk0 + g))

    def prev(width, blk0):
        return pl.BlockSpec((8, width), lambda g, c: (jnp.maximum(cidx(c) * (T // 8) - 1, 0), blk0 + g))

    specs = [cur(xw, xg0), prev(xw, xg0), cur(nw, bg0), prev(nw, bg0), cur(nw, cg0), prev(nw, cg0),
             pl.BlockSpec((T, 128), lambda g, c: (cidx(c), P_DT // 128))]
    wb, wc = D_INNER // nw, (D_INNER + SSM_GROUPS * D_STATE) // nw
    specs += [pl.BlockSpec((4, xw), lambda g, c: (0, g)),
              pl.BlockSpec((4, nw), lambda g, c: (0, wb + g)),
              pl.BlockSpec((4, nw), lambda g, c: (0, wc + g)),
              pl.BlockSpec((1, xw), lambda g, c: (0, g)),
              pl.BlockSpec((1, nw), lambda g, c: (0, wb + g)),
              pl.BlockSpec((1, nw), lambda g, c: (0, wc + g))]
    specs += [pl.BlockSpec((1, 128), lambda g, c: (0, 0))] * 3
    return specs, cidx


def _ssd_chunk_forward(refs, ext_ref, g, c):
    (xc_ref, xp_ref, bc_ref, bp_ref, cc_ref, cp_ref, dt_ref, wx_ref, wb_ref, wc_ref,
     bx_ref, bb_ref, bcb_ref, dtb_ref, alog_ref, dsk_ref) = refs

    def conv_pre(cur_ref, prev_ref, w_ref, b_ref, width):
        ext_ref[0:8, 0:width] = jnp.where(c > 0, prev_ref[...], 0.0)
        ext_ref[8:8 + T, 0:width] = cur_ref[...]
        w = w_ref[...]
        acc = b_ref[...] + w[3:4] * cur_ref[...]
        for k in range(3):
            acc = acc + w[k:k + 1] * ext_ref[pl.ds(5 + k, T), 0:width]
        return acc

    valid = _valid_rows(c * T, T, PAD)
    v = {}
    v["valid"] = valid
    v["x_pre"] = conv_pre(xc_ref, xp_ref, wx_ref, bx_ref, GROUP_W)
    v["b_pre"] = conv_pre(bc_ref, bp_ref, wb_ref, bb_ref, D_STATE)
    v["c_pre"] = conv_pre(cc_ref, cp_ref, wc_ref, bcb_ref, D_STATE)
    xs = _silu(v["x_pre"])
    bm = jnp.where(valid, _silu(v["b_pre"]), 0.0)
    cm = jnp.where(valid, _silu(v["c_pre"]), 0.0)
    dtr = dt_ref[...] + dtb_ref[...]
    dt = jnp.where(valid, _softplus(dtr), 0.0)
    a_neg = -jnp.exp(alog_ref[...])
    a = dt * a_neg
    tril = _iota((T, T), 0) >= _iota((T, T), 1)
    cs = _xdot_l(tril.astype(BF16), a)
    hh, ll = _iota((128, GROUP_W), 0), _iota((128, GROUP_W), 1)
    expand = (hh == 8 * g + jnp.right_shift(ll, 6)).astype(BF16)
    sh, sj = _iota((128, 128), 0), _iota((128, 128), 1)
    select = ((sh == 8 * g + sj) & (sj < 8)).astype(BF16)
    hh_t, ll_t = _iota((GROUP_W, 128), 1), _iota((GROUP_W, 128), 0)
    v["expand_t"] = (hh_t == 8 * g + jnp.right_shift(ll_t, 6)).astype(BF16)
    v["select_t"] = ((sj == 8 * g + sh) & (sh < 8)).astype(BF16)
    cs_e = _xdot(cs, expand)
    dt_e = _xdot(dt, expand)
    cs_loc = _xdot(cs, select)
    cs_loc_t = cs_loc.T
    cs_last_e = cs_e[T - 1:T, :]
    v.update(xs=xs, bm=bm, cm=cm, dtr=dtr, dt=dt, a_neg=a_neg, tril=tril, expand=expand, select=select,
             cs_e=cs_e, dt_e=dt_e, cs_loc=cs_loc, cs_loc_t=cs_loc_t, cs_last_e=cs_last_e)
    v["xdt"] = xs * dt_e
    v["decay_e"] = jnp.exp(cs_last_e - cs_e)
    v["ecs_e"] = jnp.exp(cs_e)
    v["elast_e"] = jnp.exp(cs_last_e)
    v["d_e"] = _xdot(dsk_ref[...], expand)
    v["gmat"] = _dot_nt(cm.astype(BF16), bm.astype(BF16))
    return v


def _ssd_decay_pair(v, jp):
    out = []
    for j in (2 * jp, 2 * jp + 1):
        diff = v["cs_loc"][:, j:j + 1] - v["cs_loc_t"][j:j + 1, :]
        out.append(jnp.where(v["tril"], jnp.exp(jnp.where(v["tril"], diff, 0.0)), 0.0))
    return out


def _block_diag_pair(xp):
    lane = _iota(xp.shape, 1)
    return jnp.concatenate([jnp.where(lane < HEAD_P, xp, 0.0), jnp.where(lane >= HEAD_P, xp, 0.0)], axis=0)


SSD_GROUPS_PER_STEP = 4


def _ssd_group_refs(refs, gg):
    x_w, n_w = pl.ds(GROUP_W * gg, GROUP_W), pl.ds(D_STATE * gg, D_STATE)
    lanes = [x_w, x_w, n_w, n_w, n_w, n_w, None, x_w, n_w, n_w, x_w, n_w, n_w, None, None, None]
    return [r if w is None else r.at[:, w] for r, w in zip(refs, lanes)]


def _ssd_fwd(p, conv_w, conv_b, dt_bias, a_log, d_skip, n_chunks, bg=None):
    n_rows = n_chunks * T
    in_specs, _ = _ssd_specs(n_chunks, rev=False, per_step=SSD_GROUPS_PER_STEP)
    per = SSD_GROUPS_PER_STEP

    def body(*refs):
        y_ref, hin_ref, st_ref, ext_ref = refs[16:]
        g2, c = pl.program_id(0), pl.program_id(1)

        @pl.when(c == 0)
        def _():
            st_ref[...] = jnp.zeros_like(st_ref)

        for gg in range(per):
            v = _ssd_chunk_forward(_ssd_group_refs(refs[:16], gg), ext_ref.at[gg], per * g2 + gg, c)
            state = st_ref[gg]
            hin_ref[gg] = state
            ys = []
            for jp in range(4):
                l0, l1 = _ssd_decay_pair(v, jp)
                lhs = jnp.concatenate([v["gmat"] * l0, v["gmat"] * l1], axis=1).astype(BF16)
                rhs = _block_diag_pair(v["xdt"][:, 128 * jp:128 * jp + 128]).astype(BF16)
                ys.append(_dot(lhs, rhs))
            y = jnp.concatenate(ys, axis=1)
            y = y + _dot(v["cm"].astype(BF16), state.astype(BF16)) * v["ecs_e"] + v["xs"] * v["d_e"]
            y_ref[:, GROUP_W * gg:GROUP_W * gg + GROUP_W] = y
            s_new = _dot_tn(v["bm"].astype(BF16), (v["xdt"] * v["decay_e"]).astype(BF16))
            st_ref[gg] = state * v["elast_e"] + s_new

    return _call(
        body, "ssd_fwd", (SSM_GROUPS // per, n_chunks), in_specs,
        [pl.BlockSpec((T, per * GROUP_W), lambda g, c: (c, g)),
         pl.BlockSpec((per, None, D_STATE, GROUP_W), lambda g, c: (g, c, 0, 0))],
        [jax.ShapeDtypeStruct((n_rows, D_INNER), F32),
         jax.ShapeDtypeStruct((SSM_GROUPS, n_chunks, D_STATE, GROUP_W), F32)],
        [p, p, p, p, p, p, p, conv_w, conv_w, conv_w, conv_b, conv_b, conv_b, dt_bias, a_log, d_skip],
        scratch_shapes=[pltpu.VMEM((per, D_STATE, GROUP_W), F32), pltpu.VMEM((per, T + 8, GROUP_W), F32)], bg=bg)


def _ssd_bwd(p, conv_w, conv_b, dt_bias, a_log, d_skip, hin, dy, dp, n_chunks, bg=None):
    n_rows = n_chunks * T
    per = SSD_GROUPS_PER_STEP
    assert per == SSM_GROUPS
    dt_w = P_Z - P_DT
    in_specs, cidx = _ssd_specs(n_chunks, rev=True, per_step=per)
    in_specs = in_specs + [pl.BlockSpec((per, None, D_STATE, GROUP_W), lambda g, c: (g, cidx(c), 0, 0)),
                           pl.BlockSpec((T, per * GROUP_W), lambda g, c: (cidx(c), g)), ANY]

    def body(*refs):
        hin_ref, dy_ref = refs[16:18]
        dx_ref, db_ref, dc_ref, dp_ref, dpar_ref, dst_ref, ext_ref, ddt_ref = refs[19:]
        for gg in range(per):
            x_w, n_w = pl.ds(GROUP_W * gg, GROUP_W), pl.ds(D_STATE * gg, D_STATE)
            group_body(_ssd_group_refs(refs[:16], gg), hin_ref.at[gg], dy_ref.at[:, x_w], dx_ref.at[:, x_w],
                       db_ref.at[:, n_w], dc_ref.at[:, n_w], ddt_ref.at[:, n_w], dpar_ref.at[gg], dst_ref.at[gg],
                       ext_ref.at[gg], per * pl.program_id(0) + gg)
        ddt = ddt_ref[:, 0:128] + ddt_ref[:, 128:256] + ddt_ref[:, 256:384] + ddt_ref[:, 384:512]
        dp_ref[...] = jnp.concatenate([ddt, jnp.zeros((T, dt_w - 128), F32)], axis=1).astype(dp_ref.dtype)

    def group_body(in_refs, hin_ref, dy_ref, dx_ref, db_ref, dc_ref, ddt_ref, dpar_ref, dst_ref, ext_ref, g):
        step = pl.program_id(1)
        c = n_chunks - 1 - step

        @pl.when(step == 0)
        def _():
            dst_ref[...] = jnp.zeros_like(dst_ref)

        v = _ssd_chunk_forward(in_refs, ext_ref, g, c)
        hin_f = hin_ref[...]
        hin_b = hin_f.astype(BF16)
        dyv = dy_ref[...]
        dst = dst_ref[...]
        dst_b = dst.astype(BF16)
        xs, bm, cm, xdt = v["xs"], v["bm"], v["cm"], v["xdt"]
        bm_b, cm_b = bm.astype(BF16), cm.astype(BF16)

        dd_e = jnp.sum(dyv * xs, axis=0, keepdims=True)
        dxs = dyv * v["d_e"]
        ch = _dot(cm_b, hin_b)
        dch = (dyv * v["ecs_e"]).astype(BF16)
        dcm = _dot_nt(dch, hin_b)
        dhin = _dot_tn(cm_b, dch) + dst * v["elast_e"]
        dcs_e = dyv * ch * v["ecs_e"]
        dxd = _dot(bm_b, dst_b)
        dbm = _dot_nt((xdt * v["decay_e"]).astype(BF16), dst_b)
        dxdt_state = dxd * v["decay_e"]
        q = dxdt_state * xdt
        dcs_e = dcs_e - q
        dlast_e = jnp.sum(q, axis=0, keepdims=True) + jnp.sum(dst * hin_f, axis=0, keepdims=True) * v["elast_e"]
        dg = jnp.zeros((T, T), F32)
        rs_cols = jnp.zeros((T, 128), F32)
        cs_rows = jnp.zeros((128, T), F32)
        lane_i, sub_i = _iota((T, 128), 1), _iota((128, T), 0)
        dxdt_parts = []
        for jp in range(4):
            l0, l1 = _ssd_decay_pair(v, jp)
            m0, m1 = v["gmat"] * l0, v["gmat"] * l1
            xbd = _block_diag_pair(xdt[:, 128 * jp:128 * jp + 128]).astype(BF16)
            dyp = dyv[:, 128 * jp:128 * jp + 128]
            dm = _dot_nt(dyp.astype(BF16), xbd)
            dm0, dm1 = dm[:, 0:T], dm[:, T:2 * T]
            dg = dg + dm0 * l0 + dm1 * l1
            for j, qq in ((2 * jp, dm0 * m0), (2 * jp + 1, dm1 * m1)):
                rs_cols = jnp.where(lane_i == j, jnp.sum(qq, axis=1, keepdims=True), rs_cols)
                cs_rows = jnp.where(sub_i == j, jnp.sum(qq, axis=0, keepdims=True), cs_rows)
            mv = jnp.concatenate([m0, m1], axis=0).astype(BF16)
            dxdt_parts.append(_dot_tn(mv, _block_diag_pair(dyp).astype(BF16)))
        dxdt = jnp.concatenate(dxdt_parts, axis=1) + dxdt_state
        dg_b = dg.astype(BF16)
        dcm = dcm + _dot(dg_b, bm_b)
        dbm = dbm + _dot_tn(dg_b, cm_b)
        expand_t = v["expand_t"]
        dcs_loc = rs_cols - cs_rows.T
        last_row = _iota((T, 1), 0) == T - 1
        dcs_full_e = dcs_e + jnp.where(last_row, dlast_e, 0.0)
        dcs = _xdot(dcs_full_e, expand_t) + _xdot(dcs_loc, v["select_t"])
        triu = (_iota((T, T), 0) <= _iota((T, T), 1)).astype(BF16)
        da = _xdot_l(triu, dcs)
        ddt = da * v["a_neg"] + _xdot(dxdt * xs, expand_t)
        dxs = dxs + dxdt * v["dt_e"]
        ddtr = jnp.where(v["valid"], ddt * _sigmoid(v["dtr"]), 0.0)
        dx_ref[...] = dxs
        db_ref[...] = jnp.where(v["valid"], dbm, 0.0)
        dc_ref[...] = jnp.where(v["valid"], dcm, 0.0)
        ddt_ref[...] = ddtr
        dpar = jnp.concatenate([
            jnp.sum(ddtr, axis=0, keepdims=True),
            jnp.sum(da * v["dt"], axis=0, keepdims=True) * v["a_neg"],
            _xdot(dd_e, expand_t),
            jnp.zeros((5, 128), F32)], axis=0)

        @pl.when(step == 0)
        def _():
            dpar_ref[...] = dpar

        @pl.when(step > 0)
        def _():
            dpar_ref[...] += dpar

        dst_ref[...] = dhin

    return _call(
        body, "ssd_bwd", (SSM_GROUPS // per, n_chunks), in_specs,
        [pl.BlockSpec((T, per * GROUP_W), lambda g, c: (cidx(c), g)),
         pl.BlockSpec((T, per * D_STATE), lambda g, c: (cidx(c), g)),
         pl.BlockSpec((T, per * D_STATE), lambda g, c: (cidx(c), g)),
         pl.BlockSpec((T, dt_w), lambda g, c: (cidx(c), P_DT // dt_w)),
         pl.BlockSpec((per, 8, 128), lambda g, c: (g, 0, 0))],
        [jax.ShapeDtypeStruct((n_rows, D_INNER), F32),
         jax.ShapeDtypeStruct((n_rows, SSM_GROUPS * D_STATE), F32),
         jax.ShapeDtypeStruct((n_rows, SSM_GROUPS * D_STATE), F32),
         jax.ShapeDtypeStruct(dp.shape, dp.dtype),
         jax.ShapeDtypeStruct((SSM_GROUPS, 8, 128), F32)],
        [p, p, p, p, p, p, p, conv_w, conv_w, conv_w, conv_b, conv_b, conv_b, dt_bias, a_log, d_skip, hin, dy, dp],
        scratch_shapes=[pltpu.VMEM((per, D_STATE, GROUP_W), F32), pltpu.VMEM((per, T + 8, GROUP_W), F32),
                        pltpu.VMEM((T, per * 128), F32)],
        aliases={18: 3}, bg=bg)


def _alibi_slope(h):
    return 2.0 ** (-8.0 * (h + 1) / ATTN_HEADS)


def _dup_half(x256, kvh):
    xb = x256[:, 128 * (kvh // 2):128 * (kvh // 2) + 128]
    rolled = pltpu.roll(xb, 64, 1)
    lane = _iota(xb.shape, 1)
    if kvh % 2 == 0:
        return jnp.where(lane < 64, xb, rolled)
    return jnp.where(lane < 64, rolled, xb)


def _attn_masks(c):
    qi, j = _iota((T, T), 0), _iota((T, T), 1)
    tri = j <= qi
    meta_ok = (j >= PAD) & (j - PAD <= c * T + qi - PAD)
    band_ok = c >= jnp.where(tri, 1, 2)
    dist = jnp.bitwise_and(qi - j, T - 1).astype(F32)
    return tri, meta_ok, band_ok, dist


def _fold(x3, tri):
    return jnp.concatenate([x3[:, 0:T], jnp.where(tri, x3[:, 2 * T:3 * T], x3[:, T:2 * T])], axis=1)


def _unfold(x2, tri):
    band = x2[:, T:2 * T]
    return jnp.concatenate([x2[:, 0:T], jnp.where(tri, 0.0, band), jnp.where(tri, band, 0.0)], axis=1)


def _attn_fwd(p, sinks, n_chunks, bg=None):
    n_rows = n_chunks * T
    kb, vb = P_K // KV_W, P_V // KV_W

    def body(q_ref, kc_ref, kp_ref, km_ref, vc_ref, vp_ref, vm_ref, sink_ref, o_ref, lse_ref):
        c = pl.program_id(0)
        sinks_v = sink_ref[...]
        masks = _attn_masks(c)
        tri, meta_ok, band_ok, dist = masks
        lane = _iota((T, 128), 1)
        for kvh in range(KV_HEADS):
            k3 = jnp.concatenate([_dup_half(r[...], kvh) for r in (km_ref, kp_ref, kc_ref)], axis=0).astype(BF16)
            v3 = jnp.concatenate([_dup_half(r[...], kvh) for r in (vm_ref, vp_ref, vc_ref)], axis=0)
            v3bd = _block_diag_rows(v3).astype(BF16)
            q2 = q_ref[:, 256 * kvh:256 * kvh + 256] * SCALE
            q4 = jnp.concatenate([jnp.where((lane < 64) if half == 0 else (lane >= 64), q2[:, 128 * pr:128 * pr + 128], 0.0)
                                  for pr in range(2) for half in range(2)], axis=0).astype(BF16)
            raw4 = _dot_nt(q4, k3)
            probs = []
            for hh in range(4):
                h = 4 * kvh + hh
                raw = raw4[T * hh:T * hh + T]
                band = jnp.where(tri, raw[:, 2 * T:3 * T], raw[:, T:2 * T]) - _alibi_slope(h) * dist
                sc = jnp.concatenate([jnp.where(meta_ok, raw[:, 0:T], NEG), jnp.where(band_ok, band, NEG)], axis=1)
                sink = sinks_v[:, h:h + 1]
                m = jnp.maximum(jnp.max(sc, axis=1, keepdims=True), sink)
                e = jnp.exp(sc - m)
                den = jnp.sum(e, axis=1, keepdims=True) + jnp.exp(sink - m)
                probs.append(_unfold(e * (1.0 / den), tri))
                lse_ref[:, h:h + 1] = m + jnp.log(den)
            p4 = jnp.concatenate([jnp.concatenate(probs[0:2], axis=1), jnp.concatenate(probs[2:4], axis=1)], axis=0)
            out = _dot(p4.astype(BF16), v3bd)
            o_ref[:, 256 * kvh:256 * kvh + 256] = jnp.concatenate([out[0:T], out[T:2 * T]], axis=1).astype(o_ref.dtype)

    blk = lambda width, col: pl.BlockSpec((T, width), lambda c: (c, col))
    prev = lambda width, col: pl.BlockSpec((T, width), lambda c: (jnp.maximum(c - 1, 0), col))
    first = lambda width, col: pl.BlockSpec((T, width), lambda c: (0, col))
    return _call(
        body, "attn_fwd", (n_chunks,),
        [blk(ATTN_W, P_Q // ATTN_W), blk(KV_W, kb), prev(KV_W, kb), first(KV_W, kb),
         blk(KV_W, vb), prev(KV_W, vb), first(KV_W, vb), pl.BlockSpec((1, 128), lambda c: (0, 0))],
        [pl.BlockSpec((T, ATTN_W), lambda c: (c, 0)), pl.BlockSpec((T, 128), lambda c: (c, 0))],
        [jax.ShapeDtypeStruct((n_rows, ATTN_W), BF16), jax.ShapeDtypeStruct((n_rows, 128), F32)],
        [p, p, p, p, p, p, p, sinks], bg=bg)


def _block_diag_rows(x3):
    lane = _iota(x3.shape, 1)
    return jnp.concatenate([jnp.where(lane < 64, x3, 0.0), jnp.where(lane >= 64, x3, 0.0)], axis=0)


def _fold_halves(x):
    return x + pltpu.roll(x, 64, 1)


def _attn_bwd(p, sinks, ao, lse, dao, dp, n_chunks, bg=None):
    kb, vb = P_K // KV_W, P_V // KV_W
    rc = lambda s: n_chunks - 1 - s

    def body(q_ref, kc_ref, kp_ref, km_ref, vc_ref, vp_ref, vm_ref, sink_ref, o_ref, lse_ref, do_ref, dp_in_ref,
             dqkv_ref, dsink_ref, kcar_ref, vcar_ref, kmeta_ref, vmeta_ref):
        step = pl.program_id(0)
        c = n_chunks - 1 - step

        @pl.when(step == 0)
        def _():
            for r in (kcar_ref, vcar_ref, kmeta_ref, vmeta_ref):
                r[...] = jnp.zeros_like(r)

        masks = _attn_masks(c)
        tri = masks[0]
        q = q_ref[...] * SCALE
        sinks_v = sink_ref[...]
        lse_v = lse_ref[...]
        ov = o_ref[...].astype(F32)
        dov = do_ref[...].astype(F32)
        lane = _iota((T, 128), 1)
        lane256 = _iota((3 * T, KV_W), 1)
        dsink = jnp.zeros((1, 128), F32)
        dk3_all = jnp.zeros((3 * T, KV_W), F32)
        dv3_all = jnp.zeros((3 * T, KV_W), F32)
        dqs = []
        for kvh in range(KV_HEADS):
            k3 = jnp.concatenate([_dup_half(r[...], kvh) for r in (km_ref, kp_ref, kc_ref)], axis=0).astype(BF16)
            v3 = jnp.concatenate([_dup_half(r[...], kvh) for r in (vm_ref, vp_ref, vc_ref)], axis=0).astype(BF16)
            halves = [(pr, half, (lane < 64) if half == 0 else (lane >= 64)) for pr in range(2) for half in range(2)]
            cols = [slice(128 * (2 * kvh + pr), 128 * (2 * kvh + pr) + 128) for pr in range(2)]
            q4 = jnp.concatenate([jnp.where(mine, q[:, cols[pr]], 0.0) for pr, _, mine in halves], axis=0).astype(BF16)
            do4 = jnp.concatenate([jnp.where(mine, dov[:, cols[pr]], 0.0) for pr, _, mine in halves], axis=0).astype(BF16)
            raw4 = _dot_nt(q4, k3)
            dp4 = _dot_nt(do4, v3)
            ds_rows, pm_rows = [], []
            for hh, (pr, half, mine) in enumerate(halves):
                h = 4 * kvh + hh
                raw = raw4[T * hh:T * hh + T]
                band = jnp.where(tri, raw[:, 2 * T:3 * T], raw[:, T:2 * T]) - _alibi_slope(h) * masks[3]
                sc = jnp.concatenate([jnp.where(masks[1], raw[:, 0:T], NEG), jnp.where(masks[2], band, NEG)], axis=1)
                lse_h = lse_v[:, h:h + 1]
                pm = jnp.exp(sc - lse_h)
                prod = dov[:, cols[pr]] * ov[:, cols[pr]]
                delta = jnp.sum(jnp.where(mine, prod, 0.0), axis=1, keepdims=True)
                dp = _fold(dp4[T * hh:T * hh + T], tri)
                ds_rows.append(_unfold(pm * (dp - delta), tri))
                pm_rows.append(_unfold(pm, tri))
                p_sink = jnp.exp(sinks_v[:, h:h + 1] - lse_h)
                dsink = jnp.where(_iota((1, 128), 1) == h, jnp.sum(-p_sink * delta, axis=0, keepdims=True), dsink)
            ds4 = jnp.concatenate(ds_rows, axis=0).astype(BF16)
            dq4 = _dot(ds4, k3)
            dk3 = _dot_tn(ds4, q4)
            dv3 = _dot_tn(jnp.concatenate(pm_rows, axis=0).astype(BF16), do4)
            for pr in range(2):
                dqs.append(jnp.where(lane < 64, dq4[2 * T * pr:2 * T * pr + T], dq4[2 * T * pr + T:2 * T * pr + 2 * T]) * SCALE)
            in_place = (lane256 >= 64 * kvh) & (lane256 < 64 * kvh + 64)
            wide = lambda x: jnp.concatenate([x, x], axis=1)
            dk3_all = jnp.where(in_place, wide(_fold_halves(dk3)), dk3_all)
            dv3_all = jnp.where(in_place, wide(_fold_halves(dv3)), dv3_all)
        dsink_all = dsink

        @pl.when(step == 0)
        def _():
            dsink_ref[...] = dsink_all

        @pl.when(step > 0)
        def _():
            dsink_ref[...] += dsink_all

        kmeta = kmeta_ref[...] + dk3_all[0:T]
        vmeta = vmeta_ref[...] + dv3_all[0:T]
        kmeta_ref[...] = kmeta
        vmeta_ref[...] = vmeta
        is_first = c == 0
        dk = jnp.where(is_first, kmeta, dk3_all[2 * T:3 * T] + kcar_ref[...])
        dv = jnp.where(is_first, vmeta, dv3_all[2 * T:3 * T] + vcar_ref[...])
        dqkv_ref[...] = jnp.concatenate(dqs + [dk, dv], axis=1).astype(dqkv_ref.dtype)
        kcar_ref[...] = dk3_all[T:2 * T]
        vcar_ref[...] = dv3_all[T:2 * T]

    blk = lambda width, col: pl.BlockSpec((T, width), lambda s: (rc(s), col))
    prev = lambda width, col: pl.BlockSpec((T, width), lambda s: (jnp.maximum(rc(s) - 1, 0), col))
    first = lambda width, col: pl.BlockSpec((T, width), lambda s: (0, col))
    return _call(
        body, "attn_bwd", (n_chunks,),
        [blk(ATTN_W, P_Q // ATTN_W), blk(KV_W, kb), prev(KV_W, kb), first(KV_W, kb),
         blk(KV_W, vb), prev(KV_W, vb), first(KV_W, vb), pl.BlockSpec((1, 128), lambda s: (0, 0)),
         blk(ATTN_W, 0), blk(128, 0), blk(ATTN_W, 0), ANY],
        [blk(QKV_W, P_Q // QKV_W), pl.BlockSpec((1, 128), lambda s: (0, 0))],
        [jax.ShapeDtypeStruct(dp.shape, dp.dtype), jax.ShapeDtypeStruct((1, 128), F32)],
        [p, p, p, p, p, p, p, sinks, ao, lse, dao, dp],
        scratch_shapes=[pltpu.VMEM((T, KV_W), F32)] * 4, aliases={11: 0}, bg=bg)


def _pad_lanes(v, width=128):
    return jnp.pad(v, ((0, 0), (0, width - v.shape[1])))


def _local_step(x, head, tgt, plan):
    w, g, run = plan.w, plan.g, plan.run
    n_tok = x.shape[0]
    n_rows = n_tok + T
    n_chunks = n_rows // T
    tm = _row_tile(n_rows, 384)
    dt_bias, a_log, d_skip = (_pad_lanes(w[k]) for k in ("ssm_dt_bias", "ssm_a_log", "ssm_d_skip"))
    sinks = _pad_lanes(w["attn_sinks"])
    x_in = [(x, D_MODEL, 0, "prev"), (head, D_MODEL, 0, "first")]

    def h0_tile(r0, xt, hd):
        return jnp.where(r0 < T, hd, xt)

    n1, = _rowwise("norm_pre_mix", lambda r0, xt, hd, wn: [_rms(h0_tile(r0, xt, hd), wn)], n_rows, T,
                   x_in, [w["norm_pre_mix"]], [(D_MODEL, BF16)], [])
    p = _matmul("in_proj", n1, w["w_cat"], "nn", F32)
    y_ssd, hin = run("ssd_fwd", _ssd_fwd, p, w["ssm_conv_w"], w["ssm_conv_b"], dt_bias, a_log, d_skip, n_chunks)
    ao, lse = run("attn_fwd", _attn_fwd, p, sinks, n_chunks)

    def gate_norm(r0, y, z, wn):
        return [_rms(y * _silu(z), wn)]

    yn, = run("ssm_gate_norm", _rowwise, "ssm_gate_norm", gate_norm, n_rows, tm,
              [(y_ssd, D_INNER, 0), (p, D_INNER, P_Z // D_INNER)], [w["ssm_norm"]], [(D_INNER, BF16)], [])
    y_ssm = _matmul("ssm_out", yn, w["w_ssm_out"], "nn", F32)
    y_attn = _matmul("attn_out", ao, w["w_attn_out"], "nn", F32)

    def mix_gate(r0, ys, ya, gs, ga):
        return [_sigmoid(gs) * ys + _sigmoid(ga) * ya]

    gate_ins = [(p, D_MODEL, P_GATE // D_MODEL), (p, D_MODEL, P_GATE // D_MODEL + 1)]
    mixed, = _rowwise("mix_gate", mix_gate, n_rows, tm, [(y_ssm, D_MODEL, 0), (y_attn, D_MODEL, 0)] + gate_ins,
                      [], [(D_MODEL, BF16)], [])
    mix = _matmul("mix_out", mixed, w["w_mix_out"], "nn", F32)

    def post_mix(r0, mx, xt, hd, w_post, w_pre):
        h1 = jnp.where(_valid_rows(r0, mx.shape[0], PAD), h0_tile(r0, xt, hd) + _rms(mx, w_post), 0.0)
        return [h1, _rms(h1, w_pre)]

    h1, n2 = _rowwise("post_mix", post_mix, n_rows, T, [(mix, D_MODEL, 0)] + x_in,
                      [w["norm_post_mix"], w["norm_pre_ffn"]], [(D_MODEL, F32), (D_MODEL, BF16)], [])
    u_raw = _matmul("ffn_up", n2, w["w_ffn_up"], "nn", F32)
    f = _ffn_act("ffn_act", u_raw, w["ffn_conv_w"], w["ffn_conv_b"], n_rows)
    ffn = _matmul("ffn_down", f, w["w_ffn_down"], "nn", F32)

    def final(r0, fo, h, t, w_post):
        real = r0 >= T
        err = jnp.where(real, h + _rms(fo, w_post) - t, 0.0)
        dy = err * (1.0 / D_MODEL)
        dffn, dw = _rms_bwd(dy, fo, w_post)
        return [dffn, dy, jnp.sum(err * err, axis=0, keepdims=True), dw]

    dffn, dh2, loss_cols, g_norm_post_ffn = _rowwise(
        "loss_head", final, n_rows, T, [(ffn, D_MODEL, 0), (h1, D_MODEL, 0), (tgt, D_MODEL, 0, "prev")],
        [w["norm_post_ffn"]], [(D_MODEL, BF16), (D_MODEL, F32)], [D_MODEL, D_MODEL])

    g["norm_post_ffn"] = g_norm_post_ffn
    g["w_ffn_down"] = _matmul("ffn_down_dw", f, dffn, "tn", F32)
    df = _matmul("ffn_down_dx", dffn, w["w_ffn_down"], "nt", F32)
    du_raw, dconv = _conv_bwd("ffn_act_bwd", u_raw, 0, [df], [(0, c0) for c0 in range(0, FFN_DIM, CONV_LANES)],
                              w["ffn_conv_w"], w["ffn_conv_b"], n_rows, True)
    g["ffn_conv_w"], g["ffn_conv_b"] = dconv[0:3], dconv[3:4]
    g["w_ffn_up"] = _matmul("ffn_up_dw", n2, du_raw, "tn", F32)
    dn2 = run("ffn_up_dx", _matmul, "ffn_up_dx", du_raw, w["w_ffn_up"], "nt", F32)

    def post_mix_bwd(r0, dn, d2, h, mx, w_pre, w_post):
        dx, dw_pre = _rms_bwd(dn, h, w_pre)
        dh1 = jnp.where(_valid_rows(r0, dn.shape[0], PAD), dx + d2, 0.0)
        dmix, dw_post = _rms_bwd(dh1, mx, w_post)
        return [dh1, dmix, dw_pre, dw_post]

    dh1, dmix, g["norm_pre_ffn"], g["norm_post_mix"] = _rowwise(
        "post_mix_bwd", post_mix_bwd, n_rows, tm,
        [(dn2, D_MODEL, 0), (dh2, D_MODEL, 0), (h1, D_MODEL, 0), (mix, D_MODEL, 0)],
        [w["norm_pre_ffn"], w["norm_post_mix"]], [(D_MODEL, F32), (D_MODEL, BF16)], [D_MODEL, D_MODEL])
    g["w_mix_out"] = _matmul("mix_out_dw", mixed, dmix, "tn", F32)
    dmixed = _matmul("mix_out_dx", dmix, w["w_mix_out"], "nt", F32)

    def mix_gate_bwd(r0, dm, ys, ya, gs, ga):
        ss, sa = _sigmoid(gs), _sigmoid(ga)
        dgate = jnp.concatenate([dm * ys * ss * (1.0 - ss), dm * ya * sa * (1.0 - sa)], axis=1)
        return [dm * ss, dm * sa, dgate]

    dys, dya, dp = _rowwise(
        "mix_gate_bwd", mix_gate_bwd, n_rows, tm,
        [(dmixed, D_MODEL, 0), (y_ssm, D_MODEL, 0), (y_attn, D_MODEL, 0)] + gate_ins,
        [], [(D_MODEL, BF16), (D_MODEL, BF16), (2 * D_MODEL, BF16, "new", P_W, P_GATE // (2 * D_MODEL))], [])
    g["w_ssm_out"] = _matmul("ssm_out_dw", yn, dys, "tn", F32)
    dyn = _matmul("ssm_out_dx", dys, w["w_ssm_out"], "nt", F32)
    g["w_attn_out"] = _matmul("attn_out_dw", ao, dya, "tn", F32)
    dao = _matmul("attn_out_dx", dya, w["w_attn_out"], "nt", BF16)

    def gate_norm_bwd(r0, dn, y, z, wn):
        sz, dsz = _silu_grad(z)
        dyz, dw = _rms_bwd(dn, y * sz, wn)
        live = _valid_rows(r0, dn.shape[0], PAD)
        return [jnp.where(live, dyz * sz, 0.0), jnp.where(live, dyz * y * dsz, 0.0), dw]

    dy_ssd, dp, g["ssm_norm"] = run(
        "ssm_gate_norm_bwd", _rowwise, "ssm_gate_norm_bwd", gate_norm_bwd, n_rows, tm,
        [(dyn, D_INNER, 0), (y_ssd, D_INNER, 0), (p, D_INNER, P_Z // D_INNER)],
        [w["ssm_norm"]], [(D_INNER, F32), (D_INNER, BF16, "into", dp, P_Z // D_INNER)], [D_INNER])
    dp, dsink = run("attn_bwd", _attn_bwd, p, sinks, ao, lse, dao, dp, n_chunks)
    g["attn_sinks"] = dsink[:, 0:ATTN_HEADS]
    dxs, dbm, dcm, dp, dpar = run("ssd_bwd", _ssd_bwd, p, w["ssm_conv_w"], w["ssm_conv_b"], dt_bias, a_log,
                                  d_skip, hin, dy_ssd, dp, n_chunks)
    dpar = jnp.sum(dpar, axis=0)
    g["ssm_dt_bias"], g["ssm_a_log"], g["ssm_d_skip"] = (dpar[i:i + 1, 0:SSM_HEADS] for i in range(3))
    x_chunks = [(src, c0) for src, arr in enumerate((dxs, dbm, dcm)) for c0 in range(0, arr.shape[1], CONV_LANES)]
    dp, dconv = run("ssm_conv_bwd", _conv_bwd, "ssm_conv_bwd", p, P_XBC // CONV_DIM, [dxs, dbm, dcm], x_chunks,
                    w["ssm_conv_w"], w["ssm_conv_b"], n_rows, False, into=dp, into_blk=P_XBC // CONV_DIM)
    g["ssm_conv_w"], g["ssm_conv_b"] = dconv[0:4], dconv[4:5]
    g["w_cat_t"] = _matmul("in_proj_dw", dp, n1, "tn", F32)
    dn1 = run("in_proj_dx", _matmul, "in_proj_dx", dp, w["w_cat"], "nt", F32)

    def pre_mix_bwd(r0, dn, d1, xt, hd, wn):
        dx, dw = _rms_bwd(dn, h0_tile(r0, xt, hd), wn)
        dh0 = jnp.where(_valid_rows(r0, dn.shape[0], PAD), dx + d1, 0.0)
        return [dh0, dh0, dw]

    dx_out, dhead, g["norm_pre_mix"] = _rowwise(
        "pre_mix_bwd", pre_mix_bwd, n_rows, T, [(dn1, D_MODEL, 0), (dh1, D_MODEL, 0)] + x_in,
        [w["norm_pre_mix"]], [(D_MODEL, F32, "prev", n_tok), (D_MODEL, F32, "first")], [D_MODEL])
    return jnp.sum(loss_cols), dx_out, dhead


_IN_SECTIONS = [((5152, 6176), P_Q), ((6176, 6432), P_K), ((6432, 6688), P_V), ((5120, 5152), P_DT),
                ((0, 2048), P_Z), ((6688, 8736), P_GATE), ((2048, 5120), P_XBC)]


IN_SHARD = N_IN // 4


def _shard_pieces(a, b):
    return [(j, max(a, j * IN_SHARD) - j * IN_SHARD, min(b, (j + 1) * IN_SHARD) - j * IN_SHARD)
            for j in range(4) if max(a, j * IN_SHARD) < min(b, (j + 1) * IN_SHARD)]


def _to_cat(w4):
    parts, at = [], 0
    for (a, b), off in _IN_SECTIONS:
        if off > at:
            parts.append(jnp.zeros((w4.shape[1], off - at), w4.dtype))
        parts += [w4[j, :, lo:hi] for j, lo, hi in _shard_pieces(a, b)]
        at = off + (b - a)
    return jnp.concatenate(parts, axis=1)


def _from_cat_t(g_cat_t):
    shards = [[] for _ in range(4)]
    for (a, b), off in sorted(_IN_SECTIONS):
        for j, lo, hi in _shard_pieces(a, b):
            start = off + j * IN_SHARD + lo - a
            shards[j].append(g_cat_t[start:start + hi - lo])
    return jnp.stack([jnp.concatenate(s, axis=0) for s in shards])


LANES = 1024
_BIG = [("w_in", 1024, 2184, "chip"), ("w_ssm_out", 512, 1024, "row"), ("w_attn_out", 256, 1024, "row"),
        ("w_mix_out", 256, 1024, "row"), ("w_ffn_up", 1024, 1408, "col"), ("w_ffn_down", 704, 1024, "row"),
        ("small", 32, LANES, "chip")]
_SMALL_SHARDED = [("ssm_conv_w", (4, 768), 1), ("ffn_conv_w", (3, 1408), 1), ("meta_tokens", (16, 256), 1)]
_REPLICATED = [("norm_pre_mix", 1024), ("ssm_conv_b", 3072), ("ssm_dt_bias", 32), ("ssm_a_log", 32),
               ("ssm_d_skip", 32), ("ssm_norm", 2048), ("attn_sinks", 16), ("norm_post_mix", 1024),
               ("norm_pre_ffn", 1024), ("ffn_conv_b", 5632), ("norm_post_ffn", 1024)]
SMALL_ROWS = 24


def _rep_rows():
    out, at = [], 0
    for _, width in _REPLICATED:
        out.append((at, -(-width // LANES)))
        at += out[-1][1]
    return out, at


def _in_rows(parts):
    rows = [jnp.pad(a, ((0, 0), (0, -a.shape[1] % LANES))).reshape(-1, LANES) for a in parts]
    flat = jnp.concatenate(rows, axis=0)
    return jnp.pad(flat, ((0, SMALL_ROWS - flat.shape[0]), (0, 0)))
WEIGHT_ORDER = ["meta_tokens", "norm_pre_mix", "w_in", "ssm_conv_w", "ssm_conv_b", "ssm_dt_bias", "ssm_a_log",
                "ssm_d_skip", "ssm_norm", "w_ssm_out", "attn_sinks", "w_attn_out", "w_mix_out", "norm_post_mix",
                "norm_pre_ffn", "w_ffn_up", "ffn_conv_w", "ffn_conv_b", "w_ffn_down", "norm_post_ffn"]


def _flatten(parts, rows):
    flat = jnp.concatenate([a.reshape(-1) for a in parts])
    return jnp.pad(flat, (0, rows * LANES - flat.shape[0])).reshape(rows, LANES)


def _unflatten(flat, shapes):
    flat = flat.reshape(-1)
    out, off = [], 0
    for shp in shapes:
        n = math.prod(shp)
        out.append(flat[off:off + n].reshape(shp))
        off += n
    return out


def _shard_of(full, chip, shape, axis):
    return lax.slice_in_dim(full, chip * shape[axis], (chip + 1) * shape[axis], axis=axis)


def _full_shape(r, c, layout):
    return {"row": (4 * r, c), "col": (r, 4 * c), "chip": (4, r, c), "chip_cols": (4, r, c)}[layout]


def _half_shape(r, c, layout):
    return (r, c // 2) if layout == "chip_cols" else (r // 2, c)


def _shard_view(ref, r, c, layout, chip):
    if layout == "row":
        return ref.at[pl.ds(pl.multiple_of(chip * r, 16), r), :]
    if layout == "col":
        return ref.at[:, pl.ds(pl.multiple_of(chip * c, 128), c)]
    return ref.at[chip]


def _half_view(ref, r, c, layout, chip, half):
    if layout == "chip_cols":
        return ref.at[chip, :, pl.ds(pl.multiple_of(half * (c // 2), 128), c // 2)]
    hr = r // 2
    if layout == "row":
        return ref.at[pl.ds(pl.multiple_of(chip * r + half * hr, 16), hr), :]
    r0 = pl.multiple_of(half * hr, 16)
    if layout == "col":
        return ref.at[pl.ds(r0, hr), pl.ds(pl.multiple_of(chip * c, 128), c)]
    return ref.at[chip, pl.ds(r0, hr), :]


def _mesh_pos():
    return lax.axis_index("x"), lax.axis_index("y"), lax.axis_index("c")


def _other_chips(x, y):
    return [(1 - x, y), (x, 1 - y), (1 - x, 1 - y)]


def _chip_index(x, y):
    return 2 * x + y


def _run_exchange(name, ex):
    n_in, n_out = len(ex.ins), len(ex.out_shapes)

    def body(*refs):
        in_refs, out_refs = refs[:n_in], refs[n_in:n_in + n_out]
        send_sems, recv_sems = refs[n_in + n_out:]
        copies = [pltpu.make_async_remote_copy(src_ref=s, dst_ref=d, send_sem=send_sems.at[i], recv_sem=recv_sems.at[i],
                                               device_id=dev, device_id_type=MESH)
                  for i, (s, d, dev) in enumerate(ex.make_copies(in_refs, out_refs))]
        assert len(copies) == ex.n_copies
        for cp in copies:
            cp.start()
        for cp in copies:
            cp.wait()

    return pl.pallas_call(
        body, name=name, in_specs=[ANY] * n_in, out_specs=[ANY] * n_out, out_shape=list(ex.out_shapes),
        scratch_shapes=[pltpu.SemaphoreType.DMA((ex.n_copies,)), pltpu.SemaphoreType.DMA((ex.n_copies,))],
        compiler_params=pltpu.CompilerParams(has_side_effects=True),
    )(*ex.ins)


def _join(*exs):
    def make(in_refs, out_refs):
        copies, i0, o0 = [], 0, 0
        for ex in exs:
            copies += ex.make_copies(in_refs[i0:i0 + len(ex.ins)], out_refs[o0:o0 + len(ex.out_shapes)])
            i0, o0 = i0 + len(ex.ins), o0 + len(ex.out_shapes)
        return copies

    aliases, i0, o0 = {}, 0, 0
    for ex in exs:
        aliases.update({i0 + k: o0 + v for k, v in ex.aliases.items()})
        i0, o0 = i0 + len(ex.ins), o0 + len(ex.out_shapes)
    return _Exchange([a for ex in exs for a in ex.ins], [s for ex in exs for s in ex.out_shapes], make,
                     sum(ex.n_copies for ex in exs), aliases)


def _split(exs, results):
    out, o0 = [], 0
    for ex in exs:
        out.append(list(results[o0:o0 + len(ex.out_shapes)]))
        o0 += len(ex.out_shapes)
    return out


def _gather_ici(entries, shards):
    def make(in_refs, out_refs):
        x, y, c = _mesh_pos()
        j = _chip_index(x, y)
        copies = []
        for ref_in, ref_out, (_, r, cc, lay) in zip(in_refs, out_refs, entries):
            copies.append((ref_in, _shard_view(ref_out, r, cc, lay, j), None))
            mine = ref_in.at[pl.ds(pl.multiple_of(c * (r // 2), 16), r // 2), :]
            copies += [(mine, _half_view(ref_out, r, cc, lay, j, c), (*ch, c)) for ch in _other_chips(x, y)]
        return copies

    shapes = [jax.ShapeDtypeStruct(_full_shape(r, cc, lay), s.dtype) for s, (_, r, cc, lay) in zip(shards, entries)]
    return _Exchange(list(shards), shapes, make, 4 * len(entries))


def _gather_pass_on(entries, fulls):
    def make(in_refs, out_refs):
        x, y, c = _mesh_pos()
        copies = []
        for ref, (_, r, cc, lay) in zip(out_refs, entries):
            for ch in _other_chips(x, y):
                landed = _half_view(ref, r, cc, lay, _chip_index(*ch), c)
                copies.append((landed, landed, (x, y, 1 - c)))
        return copies

    return _Exchange(list(fulls), [jax.ShapeDtypeStruct(f.shape, f.dtype) for f in fulls], make, 3 * len(entries),
                     {a: a for a in range(len(entries))})


def _gather_weights(entries, shards):
    n = len(entries)

    def body(*refs):
        ins, outs = refs[:n], refs[n:2 * n]
        send_sems, recv_sems, local_sems = refs[2 * n:]
        x, y, c = _mesh_pos()
        j = _chip_index(x, y)
        sibling = (x, y, 1 - c)
        chips = _other_chips(x, y)
        idx = [_chip_index(*ch) for ch in chips]

        def remote(k, src, dst, dev):
            return pltpu.make_async_remote_copy(src_ref=src, dst_ref=dst, send_sem=send_sems.at[k],
                                                recv_sem=recv_sems.at[k], device_id=dev, device_id_type=MESH)

        own = [pltpu.make_async_copy(ins[a], _shard_view(outs[a], r, cc, lay, j), local_sems.at[a])
               for a, (_, r, cc, lay) in enumerate(entries)]
        for cp in own:
            cp.start()
        first, passed = [], []
        for a, (_, r, cc, lay) in enumerate(entries):
            mine = ins[a].at[pl.ds(pl.multiple_of(c * (r // 2), 16), r // 2), :]
            for k, ch in enumerate(chips):
                first.append(remote(6 * a + k, mine, _half_view(outs[a], r, cc, lay, j, c), (*ch, c)))
                landed = _half_view(outs[a], r, cc, lay, idx[k], c)
                passed.append(remote(6 * a + 3 + k, landed, landed, sibling))
        for cp in first:
            cp.start()
        for a, (_, r, cc, lay) in enumerate(entries):
            for k in range(3):
                landed = _half_view(outs[a], r, cc, lay, idx[k], c)
                remote(6 * a + k, landed, landed, sibling).wait_recv()
                passed[3 * a + k].start()
        for a, (_, r, cc, lay) in enumerate(entries):
            for k in range(3):
                theirs = _half_view(outs[a], r, cc, lay, idx[k], 1 - c)
                remote(6 * a + 3 + k, theirs, theirs, sibling).wait_recv()
        for cp in first + passed:
            cp.wait_send()
        for cp in own:
            cp.wait()

    return pl.pallas_call(
        body, name="gather_weights", in_specs=[ANY] * n, out_specs=[ANY] * n,
        out_shape=[jax.ShapeDtypeStruct(_full_shape(r, cc, lay), s.dtype) for s, (_, r, cc, lay) in zip(shards, entries)],
        scratch_shapes=[pltpu.SemaphoreType.DMA((6 * n,)), pltpu.SemaphoreType.DMA((6 * n,)), pltpu.SemaphoreType.DMA((n,))],
        compiler_params=pltpu.CompilerParams(has_side_effects=True),
    )(*shards)


def _pair_exchange(entries, grads):
    def make(in_refs, out_refs):
        x, y, c = _mesh_pos()
        return [(_half_view(ref_in, r, cc, lay, i, 1 - c), ref_out.at[i], (x, y, 1 - c))
                for ref_in, ref_out, (_, r, cc, lay) in zip(in_refs, out_refs, entries) for i in range(4)]

    return _Exchange(list(grads), [jax.ShapeDtypeStruct((4,) + _half_shape(r, cc, lay), F32) for _, r, cc, lay in entries],
                     make, 4 * len(entries))


def _whole_to_sibling(arrays):
    def make(in_refs, out_refs):
        x, y, c = _mesh_pos()
        return [(r, o, (x, y, 1 - c)) for r, o in zip(in_refs, out_refs)]

    return _Exchange(list(arrays), [jax.ShapeDtypeStruct(a.shape, a.dtype) for a in arrays], make, len(arrays))


def _chip_exchange(psends):
    def make(in_refs, out_refs):
        x, y, c = _mesh_pos()
        return [(ref_in.at[_chip_index(*ch)], ref_out.at[k], (*ch, c))
                for ref_in, ref_out in zip(in_refs, out_refs) for k, ch in enumerate(_other_chips(x, y))]

    return _Exchange(list(psends), [jax.ShapeDtypeStruct((3,) + p.shape[1:], p.dtype) for p in psends], make,
                     3 * len(psends))


def _to_all_chips(array):
    def make(in_refs, out_refs):
        x, y, c = _mesh_pos()
        return [(in_refs[0], out_refs[0].at[k], (*ch, c)) for k, ch in enumerate(_other_chips(x, y))]

    return _Exchange([array], [jax.ShapeDtypeStruct((3,) + array.shape, array.dtype)], make, 3)


SUM_ROWS = 256
ADAM_ROWS = 128


def _pair_sum(name, grad, recv, ids, r, c, layout):
    hr, c = _half_shape(r, c, layout)
    tr = _row_tile(hr, SUM_ROWS)
    nb = hr // tr

    def body(ids_ref, g_ref, r_ref, send_ref, own_ref):
        s = g_ref[...] + r_ref[...]
        send_ref[...] = s.astype(send_ref.dtype)

        @pl.when(pl.program_id(1) == ids_ref[1])
        def _():
            own_ref[...] = s

    if layout == "row":
        g_spec = pl.BlockSpec((tr, c), lambda t, j, ids_ref: ((j * r + ids_ref[0] * hr) // tr + t, 0))
    elif layout == "col":
        g_spec = pl.BlockSpec((tr, c), lambda t, j, ids_ref: (ids_ref[0] * nb + t, j))
    elif layout == "chip_cols":
        g_spec = pl.BlockSpec((None, tr, c), lambda t, j, ids_ref: (j, t, ids_ref[0]))
    else:
        g_spec = pl.BlockSpec((None, tr, c), lambda t, j, ids_ref: (j, ids_ref[0] * nb + t, 0))
    grid_spec = pltpu.PrefetchScalarGridSpec(
        num_scalar_prefetch=1, grid=(nb, 4),
        in_specs=[g_spec, pl.BlockSpec((None, tr, c), lambda t, j, ids_ref: (j, t, 0))],
        out_specs=[pl.BlockSpec((None, tr, c), lambda t, j, ids_ref: (j, t, 0)),
                   pl.BlockSpec((tr, c), lambda t, j, ids_ref: (t, 0))])
    return pl.pallas_call(
        body, name=name, grid_spec=grid_spec,
        out_shape=[jax.ShapeDtypeStruct((4, hr, c), BF16), jax.ShapeDtypeStruct((hr, c), F32)],
        compiler_params=_cparams(2),
    )(ids, grad, recv)


def _chip_sum(name, own, recv):
    hr, c = own.shape
    tr = _row_tile(hr, SUM_ROWS)

    def body(o_ref, r_ref, out_ref):
        out_ref[...] = ((o_ref[...] + r_ref[0].astype(F32)) + r_ref[1].astype(F32)) + r_ref[2].astype(F32)

    return pl.pallas_call(
        body, name=name, grid=(hr // tr,),
        in_specs=[pl.BlockSpec((tr, c), lambda i: (i, 0)), pl.BlockSpec((3, tr, c), lambda i: (0, i, 0))],
        out_specs=pl.BlockSpec((tr, c), lambda i: (i, 0)),
        out_shape=jax.ShapeDtypeStruct((hr, c), F32), compiler_params=_cparams(1),
    )(own, recv)


def _chip_sum_small(own, recv, ids):
    def body(ids_ref, o_ref, r_ref, out_ref):
        j = ids_ref[1]
        total = None
        for i in range(4):
            m = jnp.bitwise_xor(i, j)
            term = jnp.where(m == 0, o_ref[...], jnp.where(m == 2, r_ref[0], jnp.where(m == 1, r_ref[1], r_ref[2])))
            total = term if total is None else total + term
        out_ref[...] = total

    grid_spec = pltpu.PrefetchScalarGridSpec(
        num_scalar_prefetch=1, grid=(1,),
        in_specs=[pl.BlockSpec(own.shape, lambda i, ids_ref: (0, 0)), pl.BlockSpec(recv.shape, lambda i, ids_ref: (0, 0, 0))],
        out_specs=pl.BlockSpec(own.shape, lambda i, ids_ref: (0, 0)))
    return pl.pallas_call(body, name="chip_sum_small", grid_spec=grid_spec,
                          out_shape=jax.ShapeDtypeStruct(own.shape, F32), compiler_params=_cparams(1))(ids, own, recv)


def _adamw(name, w, m, v, mine, theirs, ids):
    lead = (None,) * (w.ndim - 2)
    rows, cols = w.shape[-2:]
    half = rows // 2
    tr = _row_tile(half, ADAM_ROWS, unit=8)
    nb = half // tr
    c1 = 1.0 / (1.0 - ADAM_B1 ** ADAM_STEP)
    c2 = 1.0 / (1.0 - ADAM_B2 ** ADAM_STEP)

    def body(ids_ref, w_ref, m_ref, v_ref, mine_ref, theirs_ref, g_out, d_out, m_out, v_out):
        g = jnp.where(pl.program_id(0) == ids_ref[0], mine_ref[...], theirs_ref[...])
        m_new = ADAM_B1 * m_ref[...] + (1.0 - ADAM_B1) * g
        v_new = ADAM_B2 * v_ref[...] + (1.0 - ADAM_B2) * (g * g)
        d_out[...] = -ADAM_LR * ((m_new * c1) / (jnp.sqrt(v_new * c2) + ADAM_EPS) + ADAM_WD * w_ref[...])
        g_out[...] = g
        m_out[...] = m_new
        v_out[...] = v_new

    full = pl.BlockSpec(lead + (tr, cols), lambda h, i, ids_ref: (0,) * len(lead) + (h * nb + i, 0))
    part = pl.BlockSpec((tr, cols), lambda h, i, ids_ref: (i, 0))
    grid_spec = pltpu.PrefetchScalarGridSpec(num_scalar_prefetch=1, grid=(2, nb),
                                             in_specs=[full, full, full, part, part], out_specs=[full] * 4)
    return pl.pallas_call(
        body, name=name, grid_spec=grid_spec,
        out_shape=[jax.ShapeDtypeStruct(w.shape, F32)] * 4, compiler_params=_cparams(2),
    )(ids, w, m, v, mine, theirs)


def _adamw_whole(name, w, m, v, g):
    rows, cols = w.shape[-2:]
    tr = _row_tile(rows, 2 * ADAM_ROWS, unit=8)
    c1 = 1.0 / (1.0 - ADAM_B1 ** ADAM_STEP)
    c2 = 1.0 / (1.0 - ADAM_B2 ** ADAM_STEP)

    def body(w_ref, m_ref, v_ref, g_ref, g_out, d_out, m_out, v_out):
        g = g_ref[...]
        m_new = ADAM_B1 * m_ref[...] + (1.0 - ADAM_B1) * g
        v_new = ADAM_B2 * v_ref[...] + (1.0 - ADAM_B2) * (g * g)
        d_out[...] = -ADAM_LR * ((m_new * c1) / (jnp.sqrt(v_new * c2) + ADAM_EPS) + ADAM_WD * w_ref[...])
        g_out[...] = g
        m_out[...] = m_new
        v_out[...] = v_new

    full = pl.BlockSpec((None, tr, cols), lambda i: (0, i, 0))
    return pl.pallas_call(
        body, name=name, grid=(rows // tr,), in_specs=[full, full, full, pl.BlockSpec((tr, cols), lambda i: (i, 0))],
        out_specs=[full] * 4, out_shape=[jax.ShapeDtypeStruct(w.shape, F32)] * 4, compiler_params=_cparams(1),
    )(w, m, v, g)


def _adamw_replicated(g_rows, ws, ms, vs):
    n = len(ws)
    layout, _ = _rep_rows()
    c1 = 1.0 / (1.0 - ADAM_B1 ** ADAM_STEP)
    c2 = 1.0 / (1.0 - ADAM_B2 ** ADAM_STEP)

    def body(g_ref, *refs):
        w_refs, m_refs, v_refs = refs[0:n], refs[n:2 * n], refs[2 * n:3 * n]
        outs = refs[3 * n:]
        for k, (r0, rows) in enumerate(layout):
            width = w_refs[k].shape[1]
            g = jnp.concatenate([g_ref[r0 + j:r0 + j + 1, :] for j in range(rows)], axis=1)[:, 0:width]
            m_new = ADAM_B1 * m_refs[k][...] + (1.0 - ADAM_B1) * g
            v_new = ADAM_B2 * v_refs[k][...] + (1.0 - ADAM_B2) * (g * g)
            outs[k][...] = g
            outs[n + k][...] = -ADAM_LR * ((m_new * c1) / (jnp.sqrt(v_new * c2) + ADAM_EPS) + ADAM_WD * w_refs[k][...])
            outs[2 * n + k][...] = m_new
            outs[3 * n + k][...] = v_new

    res = pl.pallas_call(body, name="adamw_replicated",
                         out_shape=[jax.ShapeDtypeStruct(w.shape, F32) for _ in range(4) for w in ws])(g_rows, *ws, *ms, *vs)
    return [res[k * n:(k + 1) * n] for k in range(4)]


def _small_shard(parts):
    return _flatten(parts, _BIG[-1][1])


_ENTRY = {e[0]: e for e in _BIG}
_GRAD_ENTRY = {**_ENTRY, "w_in": ("w_in", IN_SHARD, D_MODEL, "chip_cols")}
FFN_MATS = ("w_ffn_down", "w_ffn_up")
MIXER_MATS = ("w_mix_out", "w_ssm_out", "w_attn_out")


class _StepPlan:
    def __init__(self, w, late_shards, shards, ids):
        self.w, self.g = w, {}
        self.late_shards, self.shards, self.ids = late_shards, shards, ids
        self.sums, self.halves, self.results = {}, {}, {}

    def run(self, name, fn, *args, **kw):
        at = getattr(self, "_at_" + name, None)
        if at is None:
            return fn(*args, **kw)
        exchange, landed = at()
        res, extra = fn(*args, bg=exchange, **kw)
        landed(extra)
        return res

    def _at_ssd_fwd(self):
        def landed(fulls):
            self.partly_gathered = fulls

        return _gather_ici([_ENTRY[n] for n in MIXER_MATS], [self.late_shards[n] for n in MIXER_MATS]), landed

    def _at_attn_fwd(self):
        stages = (_gather_pass_on([_ENTRY[n] for n in MIXER_MATS], self.partly_gathered),
                  _gather_ici([_ENTRY[n] for n in FFN_MATS], [self.late_shards[n] for n in FFN_MATS]))

        def landed(extra):
            mixer, self.partly_gathered = _split(stages, extra)
            self.w.update(zip(MIXER_MATS, mixer))

        return _join(*stages), landed

    def _at_ssm_gate_norm(self):
        return (_gather_pass_on([_ENTRY[n] for n in FFN_MATS], self.partly_gathered),
                lambda fulls: self.w.update(zip(FFN_MATS, fulls)))

    def pair_sums(self, names, grads, recv):
        for n, gr, rv in zip(names, grads, recv):
            _, r, c, lay = _GRAD_ENTRY[n]
            self.sums[n] = _pair_sum("pair_sum_" + n, gr, rv, self.ids, r, c, lay)

    def chip_sums(self, names, recv):
        for n, rv in zip(names, recv):
            self.halves[n] = _chip_sum("chip_sum_" + n, self.sums[n][1], rv)

    def adamw(self, names, theirs):
        for n, th in zip(names, theirs):
            sh = self.shards[n]
            if n == "w_in":
                mine_first = self.ids[0] == 0
                g_t = jnp.where(mine_first, jnp.concatenate([self.halves[n], th], axis=1),
                                jnp.concatenate([th, self.halves[n]], axis=1))
                res = _adamw_whole("adamw_" + n, *[jnp.swapaxes(sh[k], -1, -2) for k in ("w", "m", "v")], g_t)
                self.results[n] = [jnp.swapaxes(r, -1, -2) for r in res]
            else:
                self.results[n] = _adamw("adamw_" + n, sh["w"], sh["m"], sh["v"], self.halves[n], th, self.ids)

    def _pair_stage(self, names, grads):
        return (_pair_exchange([_GRAD_ENTRY[n] for n in names], grads),
                lambda recv: self.pair_sums(names, grads, recv))

    def _at_ffn_up_dx(self):
        return self._pair_stage(FFN_MATS, [self.g[n] for n in FFN_MATS])

    def _at_ssm_gate_norm_bwd(self):
        return self._pair_stage(MIXER_MATS, [self.g[n] for n in MIXER_MATS])

    def _at_attn_bwd(self):
        return _chip_exchange([self.sums[n][0] for n in FFN_MATS]), lambda recv: self.chip_sums(FFN_MATS, recv)

    def _at_ssd_bwd(self):
        stages = (_chip_exchange([self.sums[n][0] for n in MIXER_MATS]),
                  _whole_to_sibling([self.halves[n] for n in FFN_MATS]))

        def landed(extra):
            recv, theirs = _split(stages, extra)
            self.chip_sums(MIXER_MATS, recv)
            self.adamw(FFN_MATS, theirs)

        return _join(*stages), landed

    def _at_ssm_conv_bwd(self):
        return _whole_to_sibling([self.halves[n] for n in MIXER_MATS]), lambda theirs: self.adamw(MIXER_MATS, theirs)

    def _at_in_proj_dx(self):
        grads = [_from_cat_t(self.g.pop("w_cat_t"))]
        self.pair_sums(("w_in",), grads,
                       _run_exchange("grad_pair_exchange_w_in", _pair_exchange([_GRAD_ENTRY["w_in"]], grads)))
        return _chip_exchange([self.sums["w_in"][0]]), lambda recv: self.chip_sums(("w_in",), recv)

    def finish(self, g_small, g_rep, rep_shards):
        stages = (_pair_exchange([_ENTRY["small"]], [g_small]), _whole_to_sibling([g_rep]))
        recv_small, recv_rep = _split(stages, _run_exchange("grad_pair_exchange_tail", _join(*stages)))
        self.pair_sums(("small",), [g_small], recv_small)
        p_rep, = _rowwise("pair_sum_replicated", lambda r0, a, b: [a + b], SMALL_ROWS, SMALL_ROWS,
                          [(g_rep, LANES, 0), (recv_rep[0], LANES, 0)], [], [(LANES, F32)], [])
        stages = (_chip_exchange([self.sums["small"][0]]), _to_all_chips(p_rep))
        recv, recv_rep = _split(stages, _run_exchange("grad_chip_exchange_tail", _join(*stages)))
        self.chip_sums(("small",), recv)
        g_rep_tot = _chip_sum_small(p_rep, recv_rep[0], self.ids)
        last = ("w_in", "small")
        self.adamw(last, _run_exchange("grad_half_share_tail", _whole_to_sibling([self.halves[n] for n in last])))
        self.results["replicated"] = _adamw_replicated(g_rep_tot, rep_shards["w"], rep_shards["m"], rep_shards["v"])
        return g_rep_tot[_rep_rows()[1], 0]


def kernel(x, meta_tokens, norm_pre_mix, w_in, ssm_conv_w, ssm_conv_b, ssm_dt_bias, ssm_a_log, ssm_d_skip, ssm_norm, w_ssm_out, attn_sinks, w_attn_out, w_mix_out, norm_post_mix, norm_pre_ffn, w_ffn_up, ffn_conv_w, ffn_conv_b, w_ffn_down, norm_post_ffn, loss_target, m_meta_tokens, m_norm_pre_mix, m_w_in, m_ssm_conv_w, m_ssm_conv_b, m_ssm_dt_bias, m_ssm_a_log, m_ssm_d_skip, m_ssm_norm, m_w_ssm_out, m_attn_sinks, m_w_attn_out, m_w_mix_out, m_norm_post_mix, m_norm_pre_ffn, m_w_ffn_up, m_ffn_conv_w, m_ffn_conv_b, m_w_ffn_down, m_norm_post_ffn, v_meta_tokens, v_norm_pre_mix, v_w_in, v_ssm_conv_w, v_ssm_conv_b, v_ssm_dt_bias, v_ssm_a_log, v_ssm_d_skip, v_ssm_norm, v_w_ssm_out, v_attn_sinks, v_w_attn_out, v_w_mix_out, v_norm_post_mix, v_norm_pre_ffn, v_w_ffn_up, v_ffn_conv_w, v_ffn_conv_b, v_w_ffn_down, v_norm_post_ffn):
    args = dict(locals())
    squeeze = lambda a: a.reshape(a.shape[-2:])
    wts = {n: squeeze(args[n]) for n in WEIGHT_ORDER}
    mom = {n: squeeze(args["m_" + n]) for n in WEIGHT_ORDER}
    var = {n: squeeze(args["v_" + n]) for n in WEIGHT_ORDER}
    x_i, y_i, c_i = _mesh_pos()
    ids = jnp.stack([c_i, _chip_index(x_i, y_i)]).astype(jnp.int32)
    big_names = [n for n, _, _, _ in _BIG[:-1]]
    small_names = [n for n, _, _ in _SMALL_SHARDED]
    rep_names = [n for n, _ in _REPLICATED]

    stacks = {"w": wts, "m": mom, "v": var}
    shards = {n: {"w": args[n], "m": args["m_" + n], "v": args["v_" + n]} for n in big_names}
    shards["small"] = {k: _small_shard([d[n] for n in small_names]) for k, d in stacks.items()}
    rep_shards = {k: [d[n] for n in rep_names] for k, d in stacks.items()}

    w_in4, small_all = _gather_weights([_ENTRY["w_in"], _ENTRY["small"]], [wts["w_in"].astype(BF16), shards["small"]["w"]])
    w = {n: wts[n] for n in rep_names}
    w["w_cat"] = _to_cat(w_in4)
    small_parts = [_unflatten(small_all[i], [shp for _, shp, _ in _SMALL_SHARDED]) for i in range(4)]
    for k, (n, _, axis) in enumerate(_SMALL_SHARDED):
        w[n] = jnp.concatenate([small_parts[i][k] for i in range(4)], axis=axis)
    plan = _StepPlan(w, {n: wts[n].astype(BF16) for n in MIXER_MATS + FFN_MATS}, shards, ids)

    head = jnp.concatenate([jnp.zeros((PAD, D_MODEL), F32), w["meta_tokens"]], axis=0)
    loss_sum, dx, dhead = _local_step(x[0], head, loss_target[0], plan)
    g = plan.g
    g["meta_tokens"] = dhead[PAD:]
    g_small = jnp.stack([_small_shard([_shard_of(g[n], i, shp, ax) for n, shp, ax in _SMALL_SHARDED]) for i in range(4)])
    loss_part = (loss_sum * (0.5 / D_MODEL)).reshape(1, 1)
    loss = plan.finish(g_small, _in_rows([g[n] for n in rep_names] + [loss_part]), rep_shards)

    results = {}
    for kind in range(4):
        results.update({(kind, n): plan.results[n][kind] for n in big_names})
        parts = _unflatten(plan.results["small"][kind], [shp for _, shp, _ in _SMALL_SHARDED])
        results.update({(kind, n): parts[k] for k, n in enumerate(small_names)})
        results.update({(kind, n): plan.results["replicated"][kind][k] for k, n in enumerate(rep_names)})
    outs = [results[kind, n].reshape(args[n].shape) for kind in range(4) for n in WEIGHT_ORDER]
    return (loss, dx[None], *outs)
```

```python
import math
from typing import Any, Callable, NamedTuple, Sequence

import jax
import jax.numpy as jnp
from jax import lax
from jax.experimental import pallas as pl
from jax.experimental.pallas import tpu as pltpu

F32 = jnp.float32
BF16 = jnp.bfloat16

D_MODEL = 1024
N_META = 16
T = 128
PAD = T - N_META
D_INNER = 2048
SSM_HEADS = 32
HEAD_P = 64
SSM_GROUPS = 4
GROUP_W = D_INNER // SSM_GROUPS
D_STATE = 128
CONV_DIM = D_INNER + 2 * SSM_GROUPS * D_STATE
ATTN_HEADS = 16
KV_HEADS = 4
ATTN_W = 1024
KV_W = 256
FFN_DIM = 2816
N_IN = 8736
EPS = 1e-6
NEG = -1e30
SCALE = 0.125

P_Q, P_K, P_V, P_DT, P_Z, P_GATE, P_XBC = 0, 1024, 1280, 1536, 2048, 4096, 6144
QKV_W = 1536
P_W = 9216

ADAM_LR, ADAM_B1, ADAM_B2, ADAM_EPS, ADAM_WD, ADAM_STEP = 0.001, 0.9, 0.999, 1e-08, 0.01, 10

VMEM_BUDGET = 40 * 1024 * 1024
VMEM_LIMIT = 56 * 1024 * 1024
MESH = pl.DeviceIdType.MESH
ANY = pl.BlockSpec(memory_space=pl.ANY)


def _cparams(n_axes, **kw):
    return pltpu.CompilerParams(dimension_semantics=("arbitrary",) * n_axes, vmem_limit_bytes=VMEM_LIMIT, **kw)


class _Exchange(NamedTuple):
    ins: Sequence[Any]
    out_shapes: Sequence[Any]
    make_copies: Callable
    n_copies: int
    aliases: dict = {}


def _call(body, name, grid, in_specs, out_specs, out_shape, operands, scratch_shapes=(), aliases=None, bg=None):
    aliases = dict(aliases or {})
    if bg is None:
        return pl.pallas_call(body, name=name, grid=grid, in_specs=in_specs, out_specs=out_specs, out_shape=out_shape,
                              scratch_shapes=list(scratch_shapes), input_output_aliases=aliases,
                              compiler_params=_cparams(len(grid)))(*operands)
    n_in, n_out, n_scr = len(in_specs), len(out_specs), len(scratch_shapes)
    nb_in, nb_out = len(bg.ins), len(bg.out_shapes)

    def hosted(*refs):
        ins, bg_ins = refs[:n_in], refs[n_in:n_in + nb_in]
        outs = refs[n_in + nb_in:n_in + nb_in + n_out]
        bg_outs = refs[n_in + nb_in + n_out:n_in + nb_in + n_out + nb_out]
        scratch = refs[n_in + nb_in + n_out + nb_out:n_in + nb_in + n_out + nb_out + n_scr]
        send_sems, recv_sems = refs[-2:]
        pids = [pl.program_id(a) for a in range(len(grid))]
        first, last = pids[0] == 0, pids[0] == grid[0] - 1
        for p, g in zip(pids[1:], grid[1:]):
            first, last = first & (p == 0), last & (p == g - 1)
        copies = []
        for k, (src, dst, peer) in enumerate(bg.make_copies(bg_ins, bg_outs)):
            if peer is None:
                copies.append(pltpu.make_async_copy(src, dst, send_sems.at[k]))
            else:
                copies.append(pltpu.make_async_remote_copy(src_ref=src, dst_ref=dst, send_sem=send_sems.at[k],
                                                           recv_sem=recv_sems.at[k], device_id=peer, device_id_type=MESH))
        assert len(copies) == bg.n_copies

        @pl.when(first)
        def _():
            for cp in copies:
                cp.start()

        body(*ins, *outs, *scratch)

        @pl.when(last)
        def _():
            for cp in copies:
                cp.wait()

    aliases = {(k if k < n_in else k + nb_in): v for k, v in aliases.items()}
    aliases.update({n_in + k: n_out + v for k, v in bg.aliases.items()})
    res = pl.pallas_call(
        hosted, name=name, grid=grid, in_specs=list(in_specs) + [ANY] * nb_in, out_specs=list(out_specs) + [ANY] * nb_out,
        out_shape=list(out_shape) + list(bg.out_shapes), input_output_aliases=aliases,
        scratch_shapes=list(scratch_shapes) + [pltpu.SemaphoreType.DMA((bg.n_copies,))] * 2,
        compiler_params=_cparams(len(grid), has_side_effects=True))(*operands, *bg.ins)
    return res[:n_out], res[n_out:]


def _sigmoid(x):
    return 1.0 / (1.0 + jnp.exp(-x))


def _silu(x):
    return x * _sigmoid(x)


def _silu_grad(x):
    s = _sigmoid(x)
    return x * s, s * (1.0 + x * (1.0 - s))


def _dsilu(x):
    return _silu_grad(x)[1]


def _softplus(x):
    e = jnp.exp(-jnp.abs(x))
    small = e * (1.0 - e * (0.5 - e * (1.0 / 3.0)))
    return jnp.maximum(x, 0.0) + jnp.where(e < 0.01, small, jnp.log(1.0 + e))


def _rms(x, w):
    r = lax.rsqrt(jnp.mean(x * x, axis=-1, keepdims=True) + EPS)
    return x * r * w


def _rms_bwd(dy, x, w):
    r = lax.rsqrt(jnp.mean(x * x, axis=-1, keepdims=True) + EPS)
    xh = x * r
    g = dy * w
    dx = r * (g - xh * jnp.mean(g * xh, axis=-1, keepdims=True))
    dw = jnp.sum(dy * xh, axis=0, keepdims=True)
    return dx, dw


def _dot(a, b):
    return jnp.dot(a, b, preferred_element_type=F32)


def _dot_nt(a, b):
    return lax.dot_general(a, b, (((1,), (1,)), ((), ())), preferred_element_type=F32)


def _dot_tn(a, b):
    return lax.dot_general(a, b, (((0,), (0,)), ((), ())), preferred_element_type=F32)


def _split3(x):
    hi = x.astype(BF16)
    r = x - hi.astype(F32)
    mid = r.astype(BF16)
    lo = (r - mid.astype(F32)).astype(BF16)
    return hi, mid, lo


def _xdot(x, e):
    hi, mid, lo = _split3(x)
    return _dot(hi, e) + _dot(mid, e) + _dot(lo, e)


def _xdot_l(e, x):
    hi, mid, lo = _split3(x)
    return _dot(e, hi) + _dot(e, mid) + _dot(e, lo)


def _iota(shape, dim):
    return lax.broadcasted_iota(jnp.int32, shape, dim)


def _divisors(n, unit):
    return [t for t in range(unit, n + 1, unit) if n % t == 0]


MIN_MATMUL_STEPS = 8


def _matmul_tiles(m, n, k, a_bytes, b_bytes, o_bytes, m_unit):
    best = None
    for tm in _divisors(m, m_unit):
        for tn in _divisors(n, 128):
            for tk in _divisors(k, 128):
                acc = 0 if tk == k else tm * tn * 4
                vm = 2 * (tm * tk * a_bytes + tk * tn * b_bytes + tm * tn * o_bytes) + acc
                if vm > VMEM_BUDGET:
                    continue
                steps = (m // tm) * (n // tn) * (k // tk)
                score = (tk == k, min(steps, MIN_MATMUL_STEPS), min(tm, 256), tm * tn * tk)
                if best is None or score > best[0]:
                    best = (score, (tm, tn, tk))
    return best[1]


def _matmul(name, a, b, mode, out_dtype, bg=None):
    if mode == "nn":
        (m, k), n = a.shape, b.shape[1]
    elif mode == "nt":
        (m, k), n = a.shape, b.shape[0]
    else:
        (k, m), n = a.shape, b.shape[1]
    ab, bb, ob = a.dtype.itemsize, b.dtype.itemsize, jnp.dtype(out_dtype).itemsize
    tm, tn, tk = _matmul_tiles(m, n, k, ab, bb, ob, 128 if mode == "tn" else 16)
    nk = k // tk
    dot = {"nn": _dot, "nt": _dot_nt, "tn": _dot_tn}[mode]

    def body(a_ref, b_ref, o_ref, *scratch):
        prod = dot(a_ref[...].astype(BF16), b_ref[...].astype(BF16))
        if nk == 1:
            o_ref[...] = prod.astype(o_ref.dtype)
        else:
            acc_ref, = scratch
            kk = pl.program_id(2)

            @pl.when(kk == 0)
            def _():
                acc_ref[...] = prod

            @pl.when(kk > 0)
            def _():
                acc_ref[...] += prod

            @pl.when(kk == nk - 1)
            def _():
                o_ref[...] = acc_ref[...].astype(o_ref.dtype)

    a_spec = pl.BlockSpec((tk, tm), lambda i, j, kk: (kk, i)) if mode == "tn" else pl.BlockSpec((tm, tk), lambda i, j, kk: (i, kk))
    b_spec = pl.BlockSpec((tn, tk), lambda i, j, kk: (j, kk)) if mode == "nt" else pl.BlockSpec((tk, tn), lambda i, j, kk: (kk, j))
    res = _call(body, name, (m // tm, n // tn, nk), [a_spec, b_spec], [pl.BlockSpec((tm, tn), lambda i, j, kk: (i, j))],
                [jax.ShapeDtypeStruct((m, n), out_dtype)], [a, b],
                scratch_shapes=[] if nk == 1 else [pltpu.VMEM((tm, tn), F32)], bg=bg)
    return res[0] if bg is None else (res[0][0], res[1])


def _row_tile(n_rows, cap, unit=16):
    return max([t for t in _divisors(n_rows, unit) if t <= cap], default=n_rows)


ROW_SUB = 384
GROUP_UNROLL = 4


def _rowwise(name, fn, n_rows, tm, row_ins, full_ins, row_outs, acc_outs, bg=None):
    n_in = len(row_ins) + len(full_ins)
    n_ro = len(row_outs)
    into = [(k, o[3]) for k, o in enumerate(row_outs) if len(o) > 2 and o[2] == "into"]

    n_row_in = len(row_ins)
    sub = min(tm, ROW_SUB)

    def body(*refs):
        i = pl.program_id(0)
        outs = refs[n_in + len(into):]

        sums = tuple(jnp.zeros((1, w), F32) for w in acc_outs)
        for s in range(tm // sub):
            rows = pl.ds(s * sub, sub)
            vals = [r[rows, :] for r in refs[:n_row_in]] + [r[...] for r in refs[n_row_in:n_in]]
            res = fn(i * tm + s * sub, *vals)
            for o, r, v in zip(row_outs, outs[:n_ro], res[:n_ro]):
                if len(o) > 2 and o[2] == "first":
                    @pl.when(i == 0)
                    def _(r=r, v=v, rows=rows):
                        r[rows, :] = v.astype(r.dtype)
                else:
                    r[rows, :] = v.astype(r.dtype)
            sums = tuple(a + v for a, v in zip(sums, res[n_ro:]))

        @pl.when(i == 0)
        def _():
            for r, v in zip(outs[n_ro:], sums):
                r[...] = v

        @pl.when(i > 0)
        def _():
            for r, v in zip(outs[n_ro:], sums):
                r[...] += v

    def in_spec(entry):
        w, cb = entry[1], entry[2]
        if len(entry) > 3 and entry[3] == "prev":
            return pl.BlockSpec((tm, w), lambda i: (jnp.maximum(i - 1, 0), cb))
        if len(entry) > 3 and entry[3] == "first":
            return pl.BlockSpec((tm, w), lambda i: (0, cb))
        return pl.BlockSpec((tm, w), lambda i: (i, cb))

    def out_spec(o):
        if len(o) == 2:
            return pl.BlockSpec((tm, o[0]), lambda i: (i, 0)), jax.ShapeDtypeStruct((n_rows, o[0]), o[1])
        if o[2] == "new":
            return pl.BlockSpec((tm, o[0]), lambda i: (i, o[4])), jax.ShapeDtypeStruct((n_rows, o[3]), o[1])
        if o[2] == "into":
            return pl.BlockSpec((tm, o[0]), lambda i: (i, o[4])), jax.ShapeDtypeStruct(o[3].shape, o[3].dtype)
        if o[2] == "first":
            return pl.BlockSpec((tm, o[0]), lambda i: (0, 0)), jax.ShapeDtypeStruct((tm, o[0]), o[1])
        return pl.BlockSpec((tm, o[0]), lambda i: (jnp.maximum(i - 1, 0), 0)), jax.ShapeDtypeStruct((o[3], o[0]), o[1])

    in_specs = [in_spec(e) for e in row_ins]
    in_specs += [pl.BlockSpec(a.shape, lambda i: (0, 0)) for a in full_ins]
    in_specs += [pl.BlockSpec(memory_space=pl.ANY) for _ in into]
    specs_shapes = [out_spec(o) for o in row_outs]
    out_specs = [s for s, _ in specs_shapes] + [pl.BlockSpec((1, w), lambda i: (0, 0)) for w in acc_outs]
    out_shape = [s for _, s in specs_shapes] + [jax.ShapeDtypeStruct((1, w), F32) for w in acc_outs]
    return _call(body, name, (n_rows // tm,), in_specs, out_specs, out_shape,
                 [e[0] for e in row_ins] + list(full_ins) + [arr for _, arr in into],
                 aliases={n_in + a: k for a, (k, _) in enumerate(into)}, bg=bg)


def _valid_rows(first_row, tm, lo):
    return (first_row + _iota((tm, 1), 0)) >= lo


CONV_ROWS = 128
CONV_SUB = 16
CONV_LANES = 256


def _conv_specs(tm, width, blk, n_rows, after):
    specs = [pl.BlockSpec((tm, width), lambda i: (i, blk)),
             pl.BlockSpec((8, width), lambda i: (jnp.maximum(i * (tm // 8) - 1, 0), blk))]
    if after:
        specs.append(pl.BlockSpec((16, width), lambda i: (jnp.minimum((i + 1) * (tm // 16), n_rows // 16 - 1), blk)))
    return specs


def _conv_window(win, w_ref, b_ref, taps, c0, cw, n):
    acc = b_ref[:, c0:c0 + cw] + w_ref[taps - 1:taps, c0:c0 + cw] * win[8:8 + n]
    for k in range(taps - 1):
        acc = acc + w_ref[k:k + 1, c0:c0 + cw] * win[8 - (taps - 1) + k:8 - (taps - 1) + k + n]
    return acc


def _ffn_act(name, u_raw, conv_w, conv_b, n_rows):
    tm, sub, cw = CONV_ROWS, CONV_SUB, CONV_LANES
    taps, width = conv_w.shape
    half = width // 2

    def body(cur_ref, prev_ref, w_ref, b_ref, f_ref, ext_ref):
        i = pl.program_id(0)
        ext_ref[0:8, :] = jnp.where(i > 0, prev_ref[...], 0.0)
        ext_ref[8:8 + tm, :] = cur_ref[...]
        for q in range(half // cw):
            a0, g0 = q * cw, half + q * cw

            def group(s, carry):
                r = pl.multiple_of(s * sub, sub)
                a = _conv_window(ext_ref[pl.ds(r, sub + 8), a0:a0 + cw], w_ref, b_ref, taps, a0, cw, sub)
                g = _conv_window(ext_ref[pl.ds(r, sub + 8), g0:g0 + cw], w_ref, b_ref, taps, g0, cw, sub)
                f_ref[pl.ds(r, sub), a0:a0 + cw] = (_silu(a) * g).astype(f_ref.dtype)
                return carry

            lax.fori_loop(0, tm // sub, group, 0, unroll=GROUP_UNROLL)

        @pl.when(i == 0)
        def _():
            f_ref[0:PAD, :] = jnp.zeros((PAD, half), f_ref.dtype)

    return pl.pallas_call(
        body, name=name, grid=(n_rows // tm,),
        in_specs=_conv_specs(tm, width, 0, n_rows, False) + [pl.BlockSpec((taps, width), lambda i: (0, 0)),
                                                             pl.BlockSpec((1, width), lambda i: (0, 0))],
        out_specs=pl.BlockSpec((tm, half), lambda i: (i, 0)),
        out_shape=jax.ShapeDtypeStruct((n_rows, half), BF16),
        scratch_shapes=[pltpu.VMEM((tm + 8, width), F32)],
        compiler_params=_cparams(1),
    )(u_raw, u_raw, conv_w, conv_b)


def _conv_bwd(name, raw, raw_blk, dsrcs, chunk_src, conv_w, conv_b, n_rows, gated, into=None, into_blk=0, bg=None):
    taps, width = conv_w.shape
    half = width // 2 if gated else width
    tm, sub, cw = CONV_ROWS, CONV_SUB, CONV_LANES
    te = tm + 16
    nd = len(dsrcs)
    n_parts = 2 if gated else 1

    def body(*refs):
        cur_ref, prev_ref, next_ref = refs[0:3]
        dcur, dnext = refs[3:3 + nd], refs[3 + nd:3 + 2 * nd]
        w_ref, b_ref = refs[3 + 2 * nd:5 + 2 * nd]
        out_ref, acc_ref, ext_ref, du_ref = refs[-4:]
        i = pl.program_id(0)
        ext_ref[0:8, :] = jnp.where(i > 0, prev_ref[...], 0.0)
        ext_ref[8:8 + tm, :] = cur_ref[...]
        ext_ref[8 + tm:24 + tm, :] = next_ref[...]

        for q, (src, off) in enumerate(chunk_src):
            cols = [q * cw, half + q * cw][:n_parts]

            def conv_grad(r, d, past_end):
                pre = [_conv_window(ext_ref[pl.ds(r, sub + 8), c0:c0 + cw], w_ref, b_ref, taps, c0, cw, sub) for c0 in cols]
                if gated:
                    act, dact = _silu_grad(pre[0])
                    dus = [d * pre[1] * dact, d * act]
                else:
                    dus = [d * _dsilu(pre[0])]
                for part, du in enumerate(dus):
                    if past_end:
                        du = jnp.where(i * tm + r + _iota((sub, 1), 0) < n_rows, du, 0.0)
                    du_ref[part, pl.ds(r, sub), :] = du

            def tile_rows(s, carry):
                r = pl.multiple_of(s * sub, sub)
                conv_grad(r, dcur[src][pl.ds(r, sub), off:off + cw].astype(F32), False)
                return carry

            lax.fori_loop(0, tm // sub, tile_rows, 0, unroll=GROUP_UNROLL)
            conv_grad(tm, dnext[src][:, off:off + cw].astype(F32), True)

            @pl.when(i == 0)
            def _():
                du_ref[:, 0:PAD, :] = jnp.zeros((n_parts, PAD, cw), F32)

            for part, c0 in enumerate(cols):
                taps_w = [w_ref[k:k + 1, c0:c0 + cw] for k in range(taps)]

                def back(s, sums):
                    new = list(sums)
                    for u in range(2):
                        r = pl.multiple_of((2 * s + u) * sub, sub)
                        win = du_ref[part, pl.ds(r, sub + 8), :]
                        raw_rows = ext_ref[pl.ds(8 + r, sub), c0:c0 + cw]
                        draw = jnp.zeros((sub, cw), F32)
                        for k in range(taps):
                            shifted = win[taps - 1 - k:taps - 1 - k + sub]
                            draw = draw + taps_w[k] * shifted
                            new[k] = new[k] + shifted * raw_rows
                        new[taps] = new[taps] + win[0:sub]
                        out_ref[pl.ds(r, sub), c0:c0 + cw] = draw.astype(out_ref.dtype)
                    return tuple(new)

                sums = lax.fori_loop(0, tm // (2 * sub), back, tuple(jnp.zeros((sub, cw), F32) for _ in range(taps + 1)))

                @pl.when(i == 0)
                def _(c0=c0):
                    out_ref[PAD - sub:PAD, c0:c0 + cw] = jnp.zeros((sub, cw), out_ref.dtype)

                for k in range(taps + 1):
                    total = jnp.sum(sums[k], axis=0, keepdims=True)
                    acc_ref[k:k + 1, c0:c0 + cw] = jnp.where(i == 0, total, acc_ref[k:k + 1, c0:c0 + cw] + total)

    in_specs = _conv_specs(tm, width, raw_blk, n_rows, True)
    in_specs += [pl.BlockSpec((tm, d.shape[1]), lambda i: (i, 0)) for d in dsrcs]
    in_specs += [pl.BlockSpec((16, d.shape[1]), lambda i: (jnp.minimum((i + 1) * (tm // 16), n_rows // 16 - 1), 0)) for d in dsrcs]
    in_specs += [pl.BlockSpec((taps, width), lambda i: (0, 0)), pl.BlockSpec((1, width), lambda i: (0, 0))]
    operands = [raw, raw, raw] + list(dsrcs) + list(dsrcs) + [conv_w, conv_b]
    aliases = {}
    if into is None:
        out0 = jax.ShapeDtypeStruct((n_rows, width), BF16)
    else:
        in_specs.append(pl.BlockSpec(memory_space=pl.ANY))
        operands.append(into)
        aliases = {len(operands) - 1: 0}
        out0 = jax.ShapeDtypeStruct(into.shape, into.dtype)
    return _call(body, name, (n_rows // tm,), in_specs,
                 [pl.BlockSpec((tm, width), lambda i: (i, into_blk)), pl.BlockSpec((8, width), lambda i: (0, 0))],
                 [out0, jax.ShapeDtypeStruct((8, width), F32)], operands,
                 scratch_shapes=[pltpu.VMEM((tm + 24, width), F32), pltpu.VMEM((n_parts, te + 8, cw), F32)],
                 aliases=aliases, bg=bg)


def _ssd_specs(n_chunks, rev, per_step=1):
    cidx = (lambda c: n_chunks - 1 - c) if rev else (lambda c: c)
    xw, nw = per_step * GROUP_W, per_step * D_STATE
    xg0, bg0, cg0 = P_XBC // xw, (P_XBC + D_INNER) // nw, (P_XBC + D_INNER + SSM_GROUPS * D_STATE) // nw

    def cur(width, blk0):
        return pl.BlockSpec((T, width), lambda g, c: (cidx(c), blk0 + g))

    def prev(width, blk0):
        return pl.BlockSpec((8, width), lambda g, c: (jnp.maximum(cidx(c) * (T // 8) - 1, 0), blk0 + g))

    specs = [cur(xw, xg0), prev(xw, xg0), cur(nw, bg0), prev(nw, bg0), cur(nw, cg0), prev(nw, cg0),
             pl.BlockSpec((T, 128), lambda g, c: (cidx(c), P_DT // 128))]
    wb, wc = D_INNER // nw, (D_INNER + SSM_GROUPS * D_STATE) // nw
    specs += [pl.BlockSpec((4, xw), lambda g, c: (0, g)),
              pl.BlockSpec((4, nw), lambda g, c: (0, wb + g)),
              pl.BlockSpec((4, nw), lambda g, c: (0, wc + g)),
              pl.BlockSpec((1, xw), lambda g, c: (0, g)),
              pl.BlockSpec((1, nw), lambda g, c: (0, wb + g)),
              pl.BlockSpec((1, nw), lambda g, c: (0, wc + g))]
    specs += [pl.BlockSpec((1, 128), lambda g, c: (0, 0))] * 3
    return specs, cidx


def _ssd_chunk_forward(refs, ext_ref, g, c):
    (xc_ref, xp_ref, bc_ref, bp_ref, cc_ref, cp_ref, dt_ref, wx_ref, wb_ref, wc_ref,
     bx_ref, bb_ref, bcb_ref, dtb_ref, alog_ref, dsk_ref) = refs

    def conv_pre(cur_ref, prev_ref, w_ref, b_ref, width):
        ext_ref[0:8, 0:width] = jnp.where(c > 0, prev_ref[...], 0.0)
        ext_ref[8:8 + T, 0:width] = cur_ref[...]
        w = w_ref[...]
        acc = b_ref[...] + w[3:4] * cur_ref[...]
        for k in range(3):
            acc = acc + w[k:k + 1] * ext_ref[pl.ds(5 + k, T), 0:width]
        return acc

    valid = _valid_rows(c * T, T, PAD)
    v = {}
    v["valid"] = valid
    v["x_pre"] = conv_pre(xc_ref, xp_ref, wx_ref, bx_ref, GROUP_W)
    v["b_pre"] = conv_pre(bc_ref, bp_ref, wb_ref, bb_ref, D_STATE)
    v["c_pre"] = conv_pre(cc_ref, cp_ref, wc_ref, bcb_ref, D_STATE)
    xs = _silu(v["x_pre"])
    bm = jnp.where(valid, _silu(v["b_pre"]), 0.0)
    cm = jnp.where(valid, _silu(v["c_pre"]), 0.0)
    dtr = dt_ref[...] + dtb_ref[...]
    dt = jnp.where(valid, _softplus(dtr), 0.0)
    a_neg = -jnp.exp(alog_ref[...])
    a = dt * a_neg
    tril = _iota((T, T), 0) >= _iota((T, T), 1)
    cs = _xdot_l(tril.astype(BF16), a)
    hh, ll = _iota((128, GROUP_W), 0), _iota((128, GROUP_W), 1)
    expand = (hh == 8 * g + jnp.right_shift(ll, 6)).astype(BF16)
    sh, sj = _iota((128, 128), 0), _iota((128, 128), 1)
    select = ((sh == 8 * g + sj) & (sj < 8)).astype(BF16)
    hh_t, ll_t = _iota((GROUP_W, 128), 1), _iota((GROUP_W, 128), 0)
    v["expand_t"] = (hh_t == 8 * g + jnp.right_shift(ll_t, 6)).astype(BF16)
    v["select_t"] = ((sj == 8 * g + sh) & (sh < 8)).astype(BF16)
    cs_e = _xdot(cs, expand)
    dt_e = _xdot(dt, expand)
    cs_loc = _xdot(cs, select)
    cs_loc_t = cs_loc.T
    cs_last_e = cs_e[T - 1:T, :]
    v.update(xs=xs, bm=bm, cm=cm, dtr=dtr, dt=dt, a_neg=a_neg, tril=tril, expand=expand, select=select,
             cs_e=cs_e, dt_e=dt_e, cs_loc=cs_loc, cs_loc_t=cs_loc_t, cs_last_e=cs_last_e)
    v["xdt"] = xs * dt_e
    v["decay_e"] = jnp.exp(cs_last_e - cs_e)
    v["ecs_e"] = jnp.exp(cs_e)
    v["elast_e"] = jnp.exp(cs_last_e)
    v["d_e"] = _xdot(dsk_ref[...], expand)
    v["gmat"] = _dot_nt(cm.astype(BF16), bm.astype(BF16))
    return v


def _ssd_decay_pair(v, jp):
    out = []
    for j in (2 * jp, 2 * jp + 1):
        diff = v["cs_loc"][:, j:j + 1] - v["cs_loc_t"][j:j + 1, :]
        out.append(jnp.where(v["tril"], jnp.exp(jnp.where(v["tril"], diff, 0.0)), 0.0))
    return out


def _block_diag_pair(xp):
    lane = _iota(xp.shape, 1)
    return jnp.concatenate([jnp.where(lane < HEAD_P, xp, 0.0), jnp.where(lane >= HEAD_P, xp, 0.0)], axis=0)


SSD_GROUPS_PER_STEP = 4


def _ssd_group_refs(refs, gg):
    x_w, n_w = pl.ds(GROUP_W * gg, GROUP_W), pl.ds(D_STATE * gg, D_STATE)
    lanes = [x_w, x_w, n_w, n_w, n_w, n_w, None, x_w, n_w, n_w, x_w, n_w, n_w, None, None, None]
    return [r if w is None else r.at[:, w] for r, w in zip(refs, lanes)]


def _ssd_fwd(p, conv_w, conv_b, dt_bias, a_log, d_skip, n_chunks, bg=None):
    n_rows = n_chunks * T
    in_specs, _ = _ssd_specs(n_chunks, rev=False, per_step=SSD_GROUPS_PER_STEP)
    per = SSD_GROUPS_PER_STEP

    def body(*refs):
        y_ref, hin_ref, st_ref, ext_ref = refs[16:]
        g2, c = pl.program_id(0), pl.program_id(1)

        @pl.when(c == 0)
        def _():
            st_ref[...] = jnp.zeros_like(st_ref)

        for gg in range(per):
            v = _ssd_chunk_forward(_ssd_group_refs(refs[:16], gg), ext_ref.at[gg], per * g2 + gg, c)
            state = st_ref[gg]
            hin_ref[gg] = state
            ys = []
            for jp in range(4):
                l0, l1 = _ssd_decay_pair(v, jp)
                lhs = jnp.concatenate([v["gmat"] * l0, v["gmat"] * l1], axis=1).astype(BF16)
                rhs = _block_diag_pair(v["xdt"][:, 128 * jp:128 * jp + 128]).astype(BF16)
                ys.append(_dot(lhs, rhs))
            y = jnp.concatenate(ys, axis=1)
            y = y + _dot(v["cm"].astype(BF16), state.astype(BF16)) * v["ecs_e"] + v["xs"] * v["d_e"]
            y_ref[:, GROUP_W * gg:GROUP_W * gg + GROUP_W] = y
            s_new = _dot_tn(v["bm"].astype(BF16), (v["xdt"] * v["decay_e"]).astype(BF16))
            st_ref[gg] = state * v["elast_e"] + s_new

    return _call(
        body, "ssd_fwd", (SSM_GROUPS // per, n_chunks), in_specs,
        [pl.BlockSpec((T, per * GROUP_W), lambda g, c: (c, g)),
         pl.BlockSpec((per, None, D_STATE, GROUP_W), lambda g, c: (g, c, 0, 0))],
        [jax.ShapeDtypeStruct((n_rows, D_INNER), F32),
         jax.ShapeDtypeStruct((SSM_GROUPS, n_chunks, D_STATE, GROUP_W), F32)],
        [p, p, p, p, p, p, p, conv_w, conv_w, conv_w, conv_b, conv_b, conv_b, dt_bias, a_log, d_skip],
        scratch_shapes=[pltpu.VMEM((per, D_STATE, GROUP_W), F32), pltpu.VMEM((per, T + 8, GROUP_W), F32)], bg=bg)


def _ssd_bwd(p, conv_w, conv_b, dt_bias, a_log, d_skip, hin, dy, dp, n_chunks, bg=None):
    n_rows = n_chunks * T
    per = SSD_GROUPS_PER_STEP
    assert per == SSM_GROUPS
    dt_w = P_Z - P_DT
    in_specs, cidx = _ssd_specs(n_chunks, rev=True, per_step=per)
    in_specs = in_specs + [pl.BlockSpec((per, None, D_STATE, GROUP_W), lambda g, c: (g, cidx(c), 0, 0)),
                           pl.BlockSpec((T, per * GROUP_W), lambda g, c: (cidx(c), g)), ANY]

    def body(*refs):
        hin_ref, dy_ref = refs[16:18]
        dx_ref, db_ref, dc_ref, dp_ref, dpar_ref, dst_ref, ext_ref, ddt_ref = refs[19:]
        for gg in range(per):
            x_w, n_w = pl.ds(GROUP_W * gg, GROUP_W), pl.ds(D_STATE * gg, D_STATE)
            group_body(_ssd_group_refs(refs[:16], gg), hin_ref.at[gg], dy_ref.at[:, x_w], dx_ref.at[:, x_w],
                       db_ref.at[:, n_w], dc_ref.at[:, n_w], ddt_ref.at[:, n_w], dpar_ref.at[gg], dst_ref.at[gg],
                       ext_ref.at[gg], per * pl.program_id(0) + gg)
        ddt = ddt_ref[:, 0:128] + ddt_ref[:, 128:256] + ddt_ref[:, 256:384] + ddt_ref[:, 384:512]
        dp_ref[...] = jnp.concatenate([ddt, jnp.zeros((T, dt_w - 128), F32)], axis=1).astype(dp_ref.dtype)

    def group_body(in_refs, hin_ref, dy_ref, dx_ref, db_ref, dc_ref, ddt_ref, dpar_ref, dst_ref, ext_ref, g):
        step = pl.program_id(1)
        c = n_chunks - 1 - step

        @pl.when(step == 0)
        def _():
            dst_ref[...] = jnp.zeros_like(dst_ref)

        v = _ssd_chunk_forward(in_refs, ext_ref, g, c)
        hin_f = hin_ref[...]
        hin_b = hin_f.astype(BF16)
        dyv = dy_ref[...]
        dst = dst_ref[...]
        dst_b = dst.astype(BF16)
        xs, bm, cm, xdt = v["xs"], v["bm"], v["cm"], v["xdt"]
        bm_b, cm_b = bm.astype(BF16), cm.astype(BF16)

        dd_e = jnp.sum(dyv * xs, axis=0, keepdims=True)
        dxs = dyv * v["d_e"]
        ch = _dot(cm_b, hin_b)
        dch = (dyv * v["ecs_e"]).astype(BF16)
        dcm = _dot_nt(dch, hin_b)
        dhin = _dot_tn(cm_b, dch) + dst * v["elast_e"]
        dcs_e = dyv * ch * v["ecs_e"]
        dxd = _dot(bm_b, dst_b)
        dbm = _dot_nt((xdt * v["decay_e"]).astype(BF16), dst_b)
        dxdt_state = dxd * v["decay_e"]
        q = dxdt_state * xdt
        dcs_e = dcs_e - q
        dlast_e = jnp.sum(q, axis=0, keepdims=True) + jnp.sum(dst * hin_f, axis=0, keepdims=True) * v["elast_e"]
        dg = jnp.zeros((T, T), F32)
        rs_cols = jnp.zeros((T, 128), F32)
        cs_rows = jnp.zeros((128, T), F32)
        lane_i, sub_i = _iota((T, 128), 1), _iota((128, T), 0)
        dxdt_parts = []
        for jp in range(4):
            l0, l1 = _ssd_decay_pair(v, jp)
            m0, m1 = v["gmat"] * l0, v["gmat"] * l1
            xbd = _block_diag_pair(xdt[:, 128 * jp:128 * jp + 128]).astype(BF16)
            dyp = dyv[:, 128 * jp:128 * jp + 128]
            dm = _dot_nt(dyp.astype(BF16), xbd)
            dm0, dm1 = dm[:, 0:T], dm[:, T:2 * T]
            dg = dg + dm0 * l0 + dm1 * l1
            for j, qq in ((2 * jp, dm0 * m0), (2 * jp + 1, dm1 * m1)):
                rs_cols = jnp.where(lane_i == j, jnp.sum(qq, axis=1, keepdims=True), rs_cols)
                cs_rows = jnp.where(sub_i == j, jnp.sum(qq, axis=0, keepdims=True), cs_rows)
            mv = jnp.concatenate([m0, m1], axis=0).astype(BF16)
            dxdt_parts.append(_dot_tn(mv, _block_diag_pair(dyp).astype(BF16)))
        dxdt = jnp.concatenate(dxdt_parts, axis=1) + dxdt_state
        dg_b = dg.astype(BF16)
        dcm = dcm + _dot(dg_b, bm_b)
        dbm = dbm + _dot_tn(dg_b, cm_b)
        expand_t = v["expand_t"]
        dcs_loc = rs_cols - cs_rows.T
        last_row = _iota((T, 1), 0) == T - 1
        dcs_full_e = dcs_e + jnp.where(last_row, dlast_e, 0.0)
        dcs = _xdot(dcs_full_e, expand_t) + _xdot(dcs_loc, v["select_t"])
        triu = (_iota((T, T), 0) <= _iota((T, T), 1)).astype(BF16)
        da = _xdot_l(triu, dcs)
        ddt = da * v["a_neg"] + _xdot(dxdt * xs, expand_t)
        dxs = dxs + dxdt * v["dt_e"]
        ddtr = jnp.where(v["valid"], ddt * _sigmoid(v["dtr"]), 0.0)
        dx_ref[...] = dxs
        db_ref[...] = jnp.where(v["valid"], dbm, 0.0)
        dc_ref[...] = jnp.where(v["valid"], dcm, 0.0)
        ddt_ref[...] = ddtr
        dpar = jnp.concatenate([
            jnp.sum(ddtr, axis=0, keepdims=True),
            jnp.sum(da * v["dt"], axis=0, keepdims=True) * v["a_neg"],
            _xdot(dd_e, expand_t),
            jnp.zeros((5, 128), F32)], axis=0)

        @pl.when(step == 0)
        def _():
            dpar_ref[...] = dpar

        @pl.when(step > 0)
        def _():
            dpar_ref[...] += dpar

        dst_ref[...] = dhin

    return _call(
        body, "ssd_bwd", (SSM_GROUPS // per, n_chunks), in_specs,
        [pl.BlockSpec((T, per * GROUP_W), lambda g, c: (cidx(c), g)),
         pl.BlockSpec((T, per * D_STATE), lambda g, c: (cidx(c), g)),
         pl.BlockSpec((T, per * D_STATE), lambda g, c: (cidx(c), g)),
         pl.BlockSpec((T, dt_w), lambda g, c: (cidx(c), P_DT // dt_w)),
         pl.BlockSpec((per, 8, 128), lambda g, c: (g, 0, 0))],
        [jax.ShapeDtypeStruct((n_rows, D_INNER), F32),
         jax.ShapeDtypeStruct((n_rows, SSM_GROUPS * D_STATE), F32),
         jax.ShapeDtypeStruct((n_rows, SSM_GROUPS * D_STATE), F32),
         jax.ShapeDtypeStruct(dp.shape, dp.dtype),
         jax.ShapeDtypeStruct((SSM_GROUPS, 8, 128), F32)],
        [p, p, p, p, p, p, p, conv_w, conv_w, conv_w, conv_b, conv_b, conv_b, dt_bias, a_log, d_skip, hin, dy, dp],
        scratch_shapes=[pltpu.VMEM((per, D_STATE, GROUP_W), F32), pltpu.VMEM((per, T + 8, GROUP_W), F32),
                        pltpu.VMEM((T, per * 128), F32)],
        aliases={18: 3}, bg=bg)


def _alibi_slope(h):
    return 2.0 ** (-8.0 * (h + 1) / ATTN_HEADS)


def _dup_half(x256, kvh):
    xb = x256[:, 128 * (kvh // 2):128 * (kvh // 2) + 128]
    rolled = pltpu.roll(xb, 64, 1)
    lane = _iota(xb.shape, 1)
    if kvh % 2 == 0:
        return jnp.where(lane < 64, xb, rolled)
    return jnp.where(lane < 64, rolled, xb)


def _attn_masks(c):
    qi, j = _iota((T, T), 0), _iota((T, T), 1)
    tri = j <= qi
    meta_ok = (j >= PAD) & (j - PAD <= c * T + qi - PAD)
    band_ok = c >= jnp.where(tri, 1, 2)
    dist = jnp.bitwise_and(qi - j, T - 1).astype(F32)
    return tri, meta_ok, band_ok, dist


def _fold(x3, tri):
    return jnp.concatenate([x3[:, 0:T], jnp.where(tri, x3[:, 2 * T:3 * T], x3[:, T:2 * T])], axis=1)


def _unfold(x2, tri):
    band = x2[:, T:2 * T]
    return jnp.concatenate([x2[:, 0:T], jnp.where(tri, 0.0, band), jnp.where(tri, band, 0.0)], axis=1)


def _attn_fwd(p, sinks, n_chunks, bg=None):
    n_rows = n_chunks * T
    kb, vb = P_K // KV_W, P_V // KV_W

    def body(q_ref, kc_ref, kp_ref, km_ref, vc_ref, vp_ref, vm_ref, sink_ref, o_ref, lse_ref):
        c = pl.program_id(0)
        sinks_v = sink_ref[...]
        masks = _attn_masks(c)
        tri, meta_ok, band_ok, dist = masks
        lane = _iota((T, 128), 1)
        for kvh in range(KV_HEADS):
            k3 = jnp.concatenate([_dup_half(r[...], kvh) for r in (km_ref, kp_ref, kc_ref)], axis=0).astype(BF16)
            v3 = jnp.concatenate([_dup_half(r[...], kvh) for r in (vm_ref, vp_ref, vc_ref)], axis=0)
            v3bd = _block_diag_rows(v3).astype(BF16)
            q2 = q_ref[:, 256 * kvh:256 * kvh + 256] * SCALE
            q4 = jnp.concatenate([jnp.where((lane < 64) if half == 0 else (lane >= 64), q2[:, 128 * pr:128 * pr + 128], 0.0)
                                  for pr in range(2) for half in range(2)], axis=0).astype(BF16)
            raw4 = _dot_nt(q4, k3)
            probs = []
            for hh in range(4):
                h = 4 * kvh + hh
                raw = raw4[T * hh:T * hh + T]
                band = jnp.where(tri, raw[:, 2 * T:3 * T], raw[:, T:2 * T]) - _alibi_slope(h) * dist
                sc = jnp.concatenate([jnp.where(meta_ok, raw[:, 0:T], NEG), jnp.where(band_ok, band, NEG)], axis=1)
                sink = sinks_v[:, h:h + 1]
                m = jnp.maximum(jnp.max(sc, axis=1, keepdims=True), sink)
                e = jnp.exp(sc - m)
                den = jnp.sum(e, axis=1, keepdims=True) + jnp.exp(sink - m)
                probs.append(_unfold(e * (1.0 / den), tri))
                lse_ref[:, h:h + 1] = m + jnp.log(den)
            p4 = jnp.concatenate([jnp.concatenate(probs[0:2], axis=1), jnp.concatenate(probs[2:4], axis=1)], axis=0)
            out = _dot(p4.astype(BF16), v3bd)
            o_ref[:, 256 * kvh:256 * kvh + 256] = jnp.concatenate([out[0:T], out[T:2 * T]], axis=1).astype(o_ref.dtype)

    blk = lambda width, col: pl.BlockSpec((T, width), lambda c: (c, col))
    prev = lambda width, col: pl.BlockSpec((T, width), lambda c: (jnp.maximum(c - 1, 0), col))
    first = lambda width, col: pl.BlockSpec((T, width), lambda c: (0, col))
    return _call(
        body, "attn_fwd", (n_chunks,),
        [blk(ATTN_W, P_Q // ATTN_W), blk(KV_W, kb), prev(KV_W, kb), first(KV_W, kb),
         blk(KV_W, vb), prev(KV_W, vb), first(KV_W, vb), pl.BlockSpec((1, 128), lambda c: (0, 0))],
        [pl.BlockSpec((T, ATTN_W), lambda c: (c, 0)), pl.BlockSpec((T, 128), lambda c: (c, 0))],
        [jax.ShapeDtypeStruct((n_rows, ATTN_W), BF16), jax.ShapeDtypeStruct((n_rows, 128), F32)],
        [p, p, p, p, p, p, p, sinks], bg=bg)


def _block_diag_rows(x3):
    lane = _iota(x3.shape, 1)
    return jnp.concatenate([jnp.where(lane < 64, x3, 0.0), jnp.where(lane >= 64, x3, 0.0)], axis=0)


def _fold_halves(x):
    return x + pltpu.roll(x, 64, 1)


def _attn_bwd(p, sinks, ao, lse, dao, dp, n_chunks, bg=None):
    kb, vb = P_K // KV_W, P_V // KV_W
    rc = lambda s: n_chunks - 1 - s

    def body(q_ref, kc_ref, kp_ref, km_ref, vc_ref, vp_ref, vm_ref, sink_ref, o_ref, lse_ref, do_ref, dp_in_ref,
             dqkv_ref, dsink_ref, kcar_ref, vcar_ref, kmeta_ref, vmeta_ref):
        step = pl.program_id(0)
        c = n_chunks - 1 - step

        @pl.when(step == 0)
        def _():
            for r in (kcar_ref, vcar_ref, kmeta_ref, vmeta_ref):
                r[...] = jnp.zeros_like(r)

        masks = _attn_masks(c)
        tri = masks[0]
        q = q_ref[...] * SCALE
        sinks_v = sink_ref[...]
        lse_v = lse_ref[...]
        ov = o_ref[...].astype(F32)
        dov = do_ref[...].astype(F32)
        lane = _iota((T, 128), 1)
        lane256 = _iota((3 * T, KV_W), 1)
        dsink = jnp.zeros((1, 128), F32)
        dk3_all = jnp.zeros((3 * T, KV_W), F32)
        dv3_all = jnp.zeros((3 * T, KV_W), F32)
        dqs = []
        for kvh in range(KV_HEADS):
            k3 = jnp.concatenate([_dup_half(r[...], kvh) for r in (km_ref, kp_ref, kc_ref)], axis=0).astype(BF16)
            v3 = jnp.concatenate([_dup_half(r[...], kvh) for r in (vm_ref, vp_ref, vc_ref)], axis=0).astype(BF16)
            halves = [(pr, half, (lane < 64) if half == 0 else (lane >= 64)) for pr in range(2) for half in range(2)]
            cols = [slice(128 * (2 * kvh + pr), 128 * (2 * kvh + pr) + 128) for pr in range(2)]
            q4 = jnp.concatenate([jnp.where(mine, q[:, cols[pr]], 0.0) for pr, _, mine in halves], axis=0).astype(BF16)
            do4 = jnp.concatenate([jnp.where(mine, dov[:, cols[pr]], 0.0) for pr, _, mine in halves], axis=0).astype(BF16)
            raw4 = _dot_nt(q4, k3)
            dp4 = _dot_nt(do4, v3)
            ds_rows, pm_rows = [], []
            for hh, (pr, half, mine) in enumerate(halves):
                h = 4 * kvh + hh
                raw = raw4[T * hh:T * hh + T]
                band = jnp.where(tri, raw[:, 2 * T:3 * T], raw[:, T:2 * T]) - _alibi_slope(h) * masks[3]
                sc = jnp.concatenate([jnp.where(masks[1], raw[:, 0:T], NEG), jnp.where(masks[2], band, NEG)], axis=1)
                lse_h = lse_v[:, h:h + 1]
                pm = jnp.exp(sc - lse_h)
                prod = dov[:, cols[pr]] * ov[:, cols[pr]]
                delta = jnp.sum(jnp.where(mine, prod, 0.0), axis=1, keepdims=True)
                dp = _fold(dp4[T * hh:T * hh + T], tri)
                ds_rows.append(_unfold(pm * (dp - delta), tri))
                pm_rows.append(_unfold(pm, tri))
                p_sink = jnp.exp(sinks_v[:, h:h + 1] - lse_h)
                dsink = jnp.where(_iota((1, 128), 1) == h, jnp.sum(-p_sink * delta, axis=0, keepdims=True), dsink)
            ds4 = jnp.concatenate(ds_rows, axis=0).astype(BF16)
            dq4 = _dot(ds4, k3)
            dk3 = _dot_tn(ds4, q4)
            dv3 = _dot_tn(jnp.concatenate(pm_rows, axis=0).astype(BF16), do4)
            for pr in range(2):
                dqs.append(jnp.where(lane < 64, dq4[2 * T * pr:2 * T * pr + T], dq4[2 * T * pr + T:2 * T * pr + 2 * T]) * SCALE)
            in_place = (lane256 >= 64 * kvh) & (lane256 < 64 * kvh + 64)
            wide = lambda x: jnp.concatenate([x, x], axis=1)
            dk3_all = jnp.where(in_place, wide(_fold_halves(dk3)), dk3_all)
            dv3_all = jnp.where(in_place, wide(_fold_halves(dv3)), dv3_all)
        dsink_all = dsink

        @pl.when(step == 0)
        def _():
            dsink_ref[...] = dsink_all

        @pl.when(step > 0)
        def _():
            dsink_ref[...] += dsink_all

        kmeta = kmeta_ref[...] + dk3_all[0:T]
        vmeta = vmeta_ref[...] + dv3_all[0:T]
        kmeta_ref[...] = kmeta
        vmeta_ref[...] = vmeta
        is_first = c == 0
        dk = jnp.where(is_first, kmeta, dk3_all[2 * T:3 * T] + kcar_ref[...])
        dv = jnp.where(is_first, vmeta, dv3_all[2 * T:3 * T] + vcar_ref[...])
        dqkv_ref[...] = jnp.concatenate(dqs + [dk, dv], axis=1).astype(dqkv_ref.dtype)
        kcar_ref[...] = dk3_all[T:2 * T]
        vcar_ref[...] = dv3_all[T:2 * T]

    blk = lambda width, col: pl.BlockSpec((T, width), lambda s: (rc(s), col))
    prev = lambda width, col: pl.BlockSpec((T, width), lambda s: (jnp.maximum(rc(s) - 1, 0), col))
    first = lambda width, col: pl.BlockSpec((T, width), lambda s: (0, col))
    return _call(
        body, "attn_bwd", (n_chunks,),
        [blk(ATTN_W, P_Q // ATTN_W), blk(KV_W, kb), prev(KV_W, kb), first(KV_W, kb),
         blk(KV_W, vb), prev(KV_W, vb), first(KV_W, vb), pl.BlockSpec((1, 128), lambda s: (0, 0)),
         blk(ATTN_W, 0), blk(128, 0), blk(ATTN_W, 0), ANY],
        [blk(QKV_W, P_Q // QKV_W), pl.BlockSpec((1, 128), lambda s: (0, 0))],
        [jax.ShapeDtypeStruct(dp.shape, dp.dtype), jax.ShapeDtypeStruct((1, 128), F32)],
        [p, p, p, p, p, p, p, sinks, ao, lse, dao, dp],
        scratch_shapes=[pltpu.VMEM((T, KV_W), F32)] * 4, aliases={11: 0}, bg=bg)


def _pad_lanes(v, width=128):
    return jnp.pad(v, ((0, 0), (0, width - v.shape[1])))


def _local_step(x, head, tgt, plan):
    w, g, run = plan.w, plan.g, plan.run
    n_tok = x.shape[0]
    n_rows = n_tok + T
    n_chunks = n_rows // T
    tm = _row_tile(n_rows, 384)
    dt_bias, a_log, d_skip = (_pad_lanes(w[k]) for k in ("ssm_dt_bias", "ssm_a_log", "ssm_d_skip"))
    sinks = _pad_lanes(w["attn_sinks"])
    x_in = [(x, D_MODEL, 0, "prev"), (head, D_MODEL, 0, "first")]

    def h0_tile(r0, xt, hd):
        return jnp.where(r0 < T, hd, xt)

    n1, = _rowwise("norm_pre_mix", lambda r0, xt, hd, wn: [_rms(h0_tile(r0, xt, hd), wn)], n_rows, T,
                   x_in, [w["norm_pre_mix"]], [(D_MODEL, BF16)], [])
    p = _matmul("in_proj", n1, w["w_cat"], "nn", F32)
    y_ssd, hin = run("ssd_fwd", _ssd_fwd, p, w["ssm_conv_w"], w["ssm_conv_b"], dt_bias, a_log, d_skip, n_chunks)
    ao, lse = run("attn_fwd", _attn_fwd, p, sinks, n_chunks)

    def gate_norm(r0, y, z, wn):
        return [_rms(y * _silu(z), wn)]

    yn, = run("ssm_gate_norm", _rowwise, "ssm_gate_norm", gate_norm, n_rows, tm,
              [(y_ssd, D_INNER, 0), (p, D_INNER, P_Z // D_INNER)], [w["ssm_norm"]], [(D_INNER, BF16)], [])
    y_ssm = _matmul("ssm_out", yn, w["w_ssm_out"], "nn", F32)
    y_attn = _matmul("attn_out", ao, w["w_attn_out"], "nn", F32)

    def mix_gate(r0, ys, ya, gs, ga):
        return [_sigmoid(gs) * ys + _sigmoid(ga) * ya]

    gate_ins = [(p, D_MODEL, P_GATE // D_MODEL), (p, D_MODEL, P_GATE // D_MODEL + 1)]
    mixed, = _rowwise("mix_gate", mix_gate, n_rows, tm, [(y_ssm, D_MODEL, 0), (y_attn, D_MODEL, 0)] + gate_ins,
                      [], [(D_MODEL, BF16)], [])
    mix = _matmul("mix_out", mixed, w["w_mix_out"], "nn", F32)

    def post_mix(r0, mx, xt, hd, w_post, w_pre):
        h1 = jnp.where(_valid_rows(r0, mx.shape[0], PAD), h0_tile(r0, xt, hd) + _rms(mx, w_post), 0.0)
        return [h1, _rms(h1, w_pre)]

    h1, n2 = _rowwise("post_mix", post_mix, n_rows, T, [(mix, D_MODEL, 0)] + x_in,
                      [w["norm_post_mix"], w["norm_pre_ffn"]], [(D_MODEL, F32), (D_MODEL, BF16)], [])
    u_raw = _matmul("ffn_up", n2, w["w_ffn_up"], "nn", F32)
    f = _ffn_act("ffn_act", u_raw, w["ffn_conv_w"], w["ffn_conv_b"], n_rows)
    ffn = _matmul("ffn_down", f, w["w_ffn_down"], "nn", F32)

    def final(r0, fo, h, t, w_post):
        real = r0 >= T
        err = jnp.where(real, h + _rms(fo, w_post) - t, 0.0)
        dy = err * (1.0 / D_MODEL)
        dffn, dw = _rms_bwd(dy, fo, w_post)
        return [dffn, dy, jnp.sum(err * err, axis=0, keepdims=True), dw]

    dffn, dh2, loss_cols, g_norm_post_ffn = _rowwise(
        "loss_head", final, n_rows, T, [(ffn, D_MODEL, 0), (h1, D_MODEL, 0), (tgt, D_MODEL, 0, "prev")],
        [w["norm_post_ffn"]], [(D_MODEL, BF16), (D_MODEL, F32)], [D_MODEL, D_MODEL])

    g["norm_post_ffn"] = g_norm_post_ffn
    g["w_ffn_down"] = _matmul("ffn_down_dw", f, dffn, "tn", F32)
    df = _matmul("ffn_down_dx", dffn, w["w_ffn_down"], "nt", F32)
    du_raw, dconv = _conv_bwd("ffn_act_bwd", u_raw, 0, [df], [(0, c0) for c0 in range(0, FFN_DIM, CONV_LANES)],
                              w["ffn_conv_w"], w["ffn_conv_b"], n_rows, True)
    g["ffn_conv_w"], g["ffn_conv_b"] = dconv[0:3], dconv[3:4]
    g["w_ffn_up"] = _matmul("ffn_up_dw", n2, du_raw, "tn", F32)
    dn2 = run("ffn_up_dx", _matmul, "ffn_up_dx", du_raw, w["w_ffn_up"], "nt", F32)

    def post_mix_bwd(r0, dn, d2, h, mx, w_pre, w_post):
        dx, dw_pre = _rms_bwd(dn, h, w_pre)
        dh1 = jnp.where(_valid_rows(r0, dn.shape[0], PAD), dx + d2, 0.0)
        dmix, dw_post = _rms_bwd(dh1, mx, w_post)
        return [dh1, dmix, dw_pre, dw_post]

    dh1, dmix, g["norm_pre_ffn"], g["norm_post_mix"] = _rowwise(
        "post_mix_bwd", post_mix_bwd, n_rows, tm,
        [(dn2, D_MODEL, 0), (dh2, D_MODEL, 0), (h1, D_MODEL, 0), (mix, D_MODEL, 0)],
        [w["norm_pre_ffn"], w["norm_post_mix"]], [(D_MODEL, F32), (D_MODEL, BF16)], [D_MODEL, D_MODEL])
    g["w_mix_out"] = _matmul("mix_out_dw", mixed, dmix, "tn", F32)
    dmixed = _matmul("mix_out_dx", dmix, w["w_mix_out"], "nt", F32)

    def mix_gate_bwd(r0, dm, ys, ya, gs, ga):
        ss, sa = _sigmoid(gs), _sigmoid(ga)
        dgate = jnp.concatenate([dm * ys * ss * (1.0 - ss), dm * ya * sa * (1.0 - sa)], axis=1)
        return [dm * ss, dm * sa, dgate]

    dys, dya, dp = _rowwise(
        "mix_gate_bwd", mix_gate_bwd, n_rows, tm,
        [(dmixed, D_MODEL, 0), (y_ssm, D_MODEL, 0), (y_attn, D_MODEL, 0)] + gate_ins,
        [], [(D_MODEL, BF16), (D_MODEL, BF16), (2 * D_MODEL, BF16, "new", P_W, P_GATE // (2 * D_MODEL))], [])
    g["w_ssm_out"] = _matmul("ssm_out_dw", yn, dys, "tn", F32)
    dyn = _matmul("ssm_out_dx", dys, w["w_ssm_out"], "nt", F32)
    g["w_attn_out"] = _matmul("attn_out_dw", ao, dya, "tn", F32)
    dao = _matmul("attn_out_dx", dya, w["w_attn_out"], "nt", BF16)

    def gate_norm_bwd(r0, dn, y, z, wn):
        sz, dsz = _silu_grad(z)
        dyz, dw = _rms_bwd(dn, y * sz, wn)
        live = _valid_rows(r0, dn.shape[0], PAD)
        return [jnp.where(live, dyz * sz, 0.0), jnp.where(live, dyz * y * dsz, 0.0), dw]

    dy_ssd, dp, g["ssm_norm"] = run(
        "ssm_gate_norm_bwd", _rowwise, "ssm_gate_norm_bwd", gate_norm_bwd, n_rows, tm,
        [(dyn, D_INNER, 0), (y_ssd, D_INNER, 0), (p, D_INNER, P_Z // D_INNER)],
        [w["ssm_norm"]], [(D_INNER, F32), (D_INNER, BF16, "into", dp, P_Z // D_INNER)], [D_INNER])
    dp, dsink = run("attn_bwd", _attn_bwd, p, sinks, ao, lse, dao, dp, n_chunks)
    g["attn_sinks"] = dsink[:, 0:ATTN_HEADS]
    dxs, dbm, dcm, dp, dpar = run("ssd_bwd", _ssd_bwd, p, w["ssm_conv_w"], w["ssm_conv_b"], dt_bias, a_log,
                                  d_skip, hin, dy_ssd, dp, n_chunks)
    dpar = jnp.sum(dpar, axis=0)
    g["ssm_dt_bias"], g["ssm_a_log"], g["ssm_d_skip"] = (dpar[i:i + 1, 0:SSM_HEADS] for i in range(3))
    x_chunks = [(src, c0) for src, arr in enumerate((dxs, dbm, dcm)) for c0 in range(0, arr.shape[1], CONV_LANES)]
    dp, dconv = run("ssm_conv_bwd", _conv_bwd, "ssm_conv_bwd", p, P_XBC // CONV_DIM, [dxs, dbm, dcm], x_chunks,
                    w["ssm_conv_w"], w["ssm_conv_b"], n_rows, False, into=dp, into_blk=P_XBC // CONV_DIM)
    g["ssm_conv_w"], g["ssm_conv_b"] = dconv[0:4], dconv[4:5]
    g["w_cat_t"] = _matmul("in_proj_dw", dp, n1, "tn", F32)
    dn1 = run("in_proj_dx", _matmul, "in_proj_dx", dp, w["w_cat"], "nt", F32)

    def pre_mix_bwd(r0, dn, d1, xt, hd, wn):
        dx, dw = _rms_bwd(dn, h0_tile(r0, xt, hd), wn)
        dh0 = jnp.where(_valid_rows(r0, dn.shape[0], PAD), dx + d1, 0.0)
        return [dh0, dh0, dw]

    dx_out, dhead, g["norm_pre_mix"] = _rowwise(
        "pre_mix_bwd", pre_mix_bwd, n_rows, T, [(dn1, D_MODEL, 0), (dh1, D_MODEL, 0)] + x_in,
        [w["norm_pre_mix"]], [(D_MODEL, F32, "prev", n_tok), (D_MODEL, F32, "first")], [D_MODEL])
    return jnp.sum(loss_cols), dx_out, dhead


_IN_SECTIONS = [((5152, 6176), P_Q), ((6176, 6432), P_K), ((6432, 6688), P_V), ((5120, 5152), P_DT),
                ((0, 2048), P_Z), ((6688, 8736), P_GATE), ((2048, 5120), P_XBC)]


IN_SHARD = N_IN // 4


def _shard_pieces(a, b):
    return [(j, max(a, j * IN_SHARD) - j * IN_SHARD, min(b, (j + 1) * IN_SHARD) - j * IN_SHARD)
            for j in range(4) if max(a, j * IN_SHARD) < min(b, (j + 1) * IN_SHARD)]


def _to_cat(w4):
    parts, at = [], 0
    for (a, b), off in _IN_SECTIONS:
        if off > at:
            parts.append(jnp.zeros((w4.shape[1], off - at), w4.dtype))
        parts += [w4[j, :, lo:hi] for j, lo, hi in _shard_pieces(a, b)]
        at = off + (b - a)
    return jnp.concatenate(parts, axis=1)


def _from_cat_t(g_cat_t):
    shards = [[] for _ in range(4)]
    for (a, b), off in sorted(_IN_SECTIONS):
        for j, lo, hi in _shard_pieces(a, b):
            start = off + j * IN_SHARD + lo - a
            shards[j].append(g_cat_t[start:start + hi - lo])
    return jnp.stack([jnp.concatenate(s, axis=0) for s in shards])


LANES = 1024
_BIG = [("w_in", 1024, 2184, "chip"), ("w_ssm_out", 512, 1024, "row"), ("w_attn_out", 256, 1024, "row"),
        ("w_mix_out", 256, 1024, "row"), ("w_ffn_up", 1024, 1408, "col"), ("w_ffn_down", 704, 1024, "row"),
        ("small", 32, LANES, "chip")]
_SMALL_SHARDED = [("ssm_conv_w", (4, 768), 1), ("ffn_conv_w", (3, 1408), 1), ("meta_tokens", (16, 256), 1)]
_REPLICATED = [("norm_pre_mix", 1024), ("ssm_conv_b", 3072), ("ssm_dt_bias", 32), ("ssm_a_log", 32),
               ("ssm_d_skip", 32), ("ssm_norm", 2048), ("attn_sinks", 16), ("norm_post_mix", 1024),
               ("norm_pre_ffn", 1024), ("ffn_conv_b", 5632), ("norm_post_ffn", 1024)]
SMALL_ROWS = 24


def _rep_rows():
    out, at = [], 0
    for _, width in _REPLICATED:
        out.append((at, -(-width // LANES)))
        at += out[-1][1]
    return out, at


def _in_rows(parts):
    rows = [jnp.pad(a, ((0, 0), (0, -a.shape[1] % LANES))).reshape(-1, LANES) for a in parts]
    flat = jnp.concatenate(rows, axis=0)
    return jnp.pad(flat, ((0, SMALL_ROWS - flat.shape[0]), (0, 0)))
WEIGHT_ORDER = ["meta_tokens", "norm_pre_mix", "w_in", "ssm_conv_w", "ssm_conv_b", "ssm_dt_bias", "ssm_a_log",
                "ssm_d_skip", "ssm_norm", "w_ssm_out", "attn_sinks", "w_attn_out", "w_mix_out", "norm_post_mix",
                "norm_pre_ffn", "w_ffn_up", "ffn_conv_w", "ffn_conv_b", "w_ffn_down", "norm_post_ffn"]


def _flatten(parts, rows):
    flat = jnp.concatenate([a.reshape(-1) for a in parts])
    return jnp.pad(flat, (0, rows * LANES - flat.shape[0])).reshape(rows, LANES)


def _unflatten(flat, shapes):
    flat = flat.reshape(-1)
    out, off = [], 0
    for shp in shapes:
        n = math.prod(shp)
        out.append(flat[off:off + n].reshape(shp))
        off += n
    return out


def _shard_of(full, chip, shape, axis):
    return lax.slice_in_dim(full, chip * shape[axis], (chip + 1) * shape[axis], axis=axis)


def _full_shape(r, c, layout):
    return {"row": (4 * r, c), "col": (r, 4 * c), "chip": (4, r, c), "chip_cols": (4, r, c)}[layout]


def _half_shape(r, c, layout):
    return (r, c // 2) if layout == "chip_cols" else (r // 2, c)


def _shard_view(ref, r, c, layout, chip):
    if layout == "row":
        return ref.at[pl.ds(pl.multiple_of(chip * r, 16), r), :]
    if layout == "col":
        return ref.at[:, pl.ds(pl.multiple_of(chip * c, 128), c)]
    return ref.at[chip]


def _half_view(ref, r, c, layout, chip, half):
    if layout == "chip_cols":
        return ref.at[chip, :, pl.ds(pl.multiple_of(half * (c // 2), 128), c // 2)]
    hr = r // 2
    if layout == "row":
        return ref.at[pl.ds(pl.multiple_of(chip * r + half * hr, 16), hr), :]
    r0 = pl.multiple_of(half * hr, 16)
    if layout == "col":
        return ref.at[pl.ds(r0, hr), pl.ds(pl.multiple_of(chip * c, 128), c)]
    return ref.at[chip, pl.ds(r0, hr), :]


def _mesh_pos():
    return lax.axis_index("x"), lax.axis_index("y"), lax.axis_index("c")


def _other_chips(x, y):
    return [(1 - x, y), (x, 1 - y), (1 - x, 1 - y)]


def _chip_index(x, y):
    return 2 * x + y


def _run_exchange(name, ex):
    n_in, n_out = len(ex.ins), len(ex.out_shapes)

    def body(*refs):
        in_refs, out_refs = refs[:n_in], refs[n_in:n_in + n_out]
        send_sems, recv_sems = refs[n_in + n_out:]
        copies = [pltpu.make_async_remote_copy(src_ref=s, dst_ref=d, send_sem=send_sems.at[i], recv_sem=recv_sems.at[i],
                                               device_id=dev, device_id_type=MESH)
                  for i, (s, d, dev) in enumerate(ex.make_copies(in_refs, out_refs))]
        assert len(copies) == ex.n_copies
        for cp in copies:
            cp.start()
        for cp in copies:
            cp.wait()

    return pl.pallas_call(
        body, name=name, in_specs=[ANY] * n_in, out_specs=[ANY] * n_out, out_shape=list(ex.out_shapes),
        scratch_shapes=[pltpu.SemaphoreType.DMA((ex.n_copies,)), pltpu.SemaphoreType.DMA((ex.n_copies,))],
        compiler_params=pltpu.CompilerParams(has_side_effects=True),
    )(*ex.ins)


def _join(*exs):
    def make(in_refs, out_refs):
        copies, i0, o0 = [], 0, 0
        for ex in exs:
            copies += ex.make_copies(in_refs[i0:i0 + len(ex.ins)], out_refs[o0:o0 + len(ex.out_shapes)])
            i0, o0 = i0 + len(ex.ins), o0 + len(ex.out_shapes)
        return copies

    aliases, i0, o0 = {}, 0, 0
    for ex in exs:
        aliases.update({i0 + k: o0 + v for k, v in ex.aliases.items()})
        i0, o0 = i0 + len(ex.ins), o0 + len(ex.out_shapes)
    return _Exchange([a for ex in exs for a in ex.ins], [s for ex in exs for s in ex.out_shapes], make,
                     sum(ex.n_copies for ex in exs), aliases)


def _split(exs, results):
    out, o0 = [], 0
    for ex in exs:
        out.append(list(results[o0:o0 + len(ex.out_shapes)]))
        o0 += len(ex.out_shapes)
    return out


def _gather_ici(entries, shards):
    def make(in_refs, out_refs):
        x, y, c = _mesh_pos()
        j = _chip_index(x, y)
        copies = []
        for ref_in, ref_out, (_, r, cc, lay) in zip(in_refs, out_refs, entries):
            copies.append((ref_in, _shard_view(ref_out, r, cc, lay, j), None))
            mine = ref_in.at[pl.ds(pl.multiple_of(c * (r // 2), 16), r // 2), :]
            copies += [(mine, _half_view(ref_out, r, cc, lay, j, c), (*ch, c)) for ch in _other_chips(x, y)]
        return copies

    shapes = [jax.ShapeDtypeStruct(_full_shape(r, cc, lay), s.dtype) for s, (_, r, cc, lay) in zip(shards, entries)]
    return _Exchange(list(shards), shapes, make, 4 * len(entries))


def _gather_pass_on(entries, fulls):
    def make(in_refs, out_refs):
        x, y, c = _mesh_pos()
        copies = []
        for ref, (_, r, cc, lay) in zip(out_refs, entries):
            for ch in _other_chips(x, y):
                landed = _half_view(ref, r, cc, lay, _chip_index(*ch), c)
                copies.append((landed, landed, (x, y, 1 - c)))
        return copies

    return _Exchange(list(fulls), [jax.ShapeDtypeStruct(f.shape, f.dtype) for f in fulls], make, 3 * len(entries),
                     {a: a for a in range(len(entries))})


def _gather_weights(entries, shards):
    n = len(entries)

    def body(*refs):
        ins, outs = refs[:n], refs[n:2 * n]
        send_sems, recv_sems, local_sems = refs[2 * n:]
        x, y, c = _mesh_pos()
        j = _chip_index(x, y)
        sibling = (x, y, 1 - c)
        chips = _other_chips(x, y)
        idx = [_chip_index(*ch) for ch in chips]

        def remote(k, src, dst, dev):
            return pltpu.make_async_remote_copy(src_ref=src, dst_ref=dst, send_sem=send_sems.at[k],
                                                recv_sem=recv_sems.at[k], device_id=dev, device_id_type=MESH)

        own = [pltpu.make_async_copy(ins[a], _shard_view(outs[a], r, cc, lay, j), local_sems.at[a])
               for a, (_, r, cc, lay) in enumerate(entries)]
        for cp in own:
            cp.start()
        first, passed = [], []
        for a, (_, r, cc, lay) in enumerate(entries):
            mine = ins[a].at[pl.ds(pl.multiple_of(c * (r // 2), 16), r // 2), :]
            for k, ch in enumerate(chips):
                first.append(remote(6 * a + k, mine, _half_view(outs[a], r, cc, lay, j, c), (*ch, c)))
                landed = _half_view(outs[a], r, cc, lay, idx[k], c)
                passed.append(remote(6 * a + 3 + k, landed, landed, sibling))
        for cp in first:
            cp.start()
        for a, (_, r, cc, lay) in enumerate(entries):
            for k in range(3):
                landed = _half_view(outs[a], r, cc, lay, idx[k], c)
                remote(6 * a + k, landed, landed, sibling).wait_recv()
                passed[3 * a + k].start()
        for a, (_, r, cc, lay) in enumerate(entries):
            for k in range(3):
                theirs = _half_view(outs[a], r, cc, lay, idx[k], 1 - c)
                remote(6 * a + 3 + k, theirs, theirs, sibling).wait_recv()
        for cp in first + passed:
            cp.wait_send()
        for cp in own:
            cp.wait()

    return pl.pallas_call(
        body, name="gather_weights", in_specs=[ANY] * n, out_specs=[ANY] * n,
        out_shape=[jax.ShapeDtypeStruct(_full_shape(r, cc, lay), s.dtype) for s, (_, r, cc, lay) in zip(shards, entries)],
        scratch_shapes=[pltpu.SemaphoreType.DMA((6 * n,)), pltpu.SemaphoreType.DMA((6 * n,)), pltpu.SemaphoreType.DMA((n,))],
        compiler_params=pltpu.CompilerParams(has_side_effects=True),
    )(*shards)


def _pair_exchange(entries, grads):
    def make(in_refs, out_refs):
        x, y, c = _mesh_pos()
        return [(_half_view(ref_in, r, cc, lay, i, 1 - c), ref_out.at[i], (x, y, 1 - c))
                for ref_in, ref_out, (_, r, cc, lay) in zip(in_refs, out_refs, entries) for i in range(4)]

    return _Exchange(list(grads), [jax.ShapeDtypeStruct((4,) + _half_shape(r, cc, lay), F32) for _, r, cc, lay in entries],
                     make, 4 * len(entries))


def _whole_to_sibling(arrays):
    def make(in_refs, out_refs):
        x, y, c = _mesh_pos()
        return [(r, o, (x, y, 1 - c)) for r, o in zip(in_refs, out_refs)]

    return _Exchange(list(arrays), [jax.ShapeDtypeStruct(a.shape, a.dtype) for a in arrays], make, len(arrays))


def _chip_exchange(psends):
    def make(in_refs, out_refs):
        x, y, c = _mesh_pos()
        return [(ref_in.at[_chip_index(*ch)], ref_out.at[k], (*ch, c))
                for ref_in, ref_out in zip(in_refs, out_refs) for k, ch in enumerate(_other_chips(x, y))]

    return _Exchange(list(psends), [jax.ShapeDtypeStruct((3,) + p.shape[1:], p.dtype) for p in psends], make,
                     3 * len(psends))


def _to_all_chips(array):
    def make(in_refs, out_refs):
        x, y, c = _mesh_pos()
        return [(in_refs[0], out_refs[0].at[k], (*ch, c)) for k, ch in enumerate(_other_chips(x, y))]

    return _Exchange([array], [jax.ShapeDtypeStruct((3,) + array.shape, array.dtype)], make, 3)


SUM_ROWS = 256
ADAM_ROWS = 128


def _pair_sum(name, grad, recv, ids, r, c, layout):
    hr, c = _half_shape(r, c, layout)
    tr = _row_tile(hr, SUM_ROWS)
    nb = hr // tr

    def body(ids_ref, g_ref, r_ref, send_ref, own_ref):
        s = g_ref[...] + r_ref[...]
        send_ref[...] = s.astype(send_ref.dtype)

        @pl.when(pl.program_id(1) == ids_ref[1])
        def _():
            own_ref[...] = s

    if layout == "row":
        g_spec = pl.BlockSpec((tr, c), lambda t, j, ids_ref: ((j * r + ids_ref[0] * hr) // tr + t, 0))
    elif layout == "col":
        g_spec = pl.BlockSpec((tr, c), lambda t, j, ids_ref: (ids_ref[0] * nb + t, j))
    elif layout == "chip_cols":
        g_spec = pl.BlockSpec((None, tr, c), lambda t, j, ids_ref: (j, t, ids_ref[0]))
    else:
        g_spec = pl.BlockSpec((None, tr, c), lambda t, j, ids_ref: (j, ids_ref[0] * nb + t, 0))
    grid_spec = pltpu.PrefetchScalarGridSpec(
        num_scalar_prefetch=1, grid=(nb, 4),
        in_specs=[g_spec, pl.BlockSpec((None, tr, c), lambda t, j, ids_ref: (j, t, 0))],
        out_specs=[pl.BlockSpec((None, tr, c), lambda t, j, ids_ref: (j, t, 0)),
                   pl.BlockSpec((tr, c), lambda t, j, ids_ref: (t, 0))])
    return pl.pallas_call(
        body, name=name, grid_spec=grid_spec,
        out_shape=[jax.ShapeDtypeStruct((4, hr, c), BF16), jax.ShapeDtypeStruct((hr, c), F32)],
        compiler_params=_cparams(2),
    )(ids, grad, recv)


def _chip_sum(name, own, recv):
    hr, c = own.shape
    tr = _row_tile(hr, SUM_ROWS)

    def body(o_ref, r_ref, out_ref):
        out_ref[...] = ((o_ref[...] + r_ref[0].astype(F32)) + r_ref[1].astype(F32)) + r_ref[2].astype(F32)

    return pl.pallas_call(
        body, name=name, grid=(hr // tr,),
        in_specs=[pl.BlockSpec((tr, c), lambda i: (i, 0)), pl.BlockSpec((3, tr, c), lambda i: (0, i, 0))],
        out_specs=pl.BlockSpec((tr, c), lambda i: (i, 0)),
        out_shape=jax.ShapeDtypeStruct((hr, c), F32), compiler_params=_cparams(1),
    )(own, recv)


def _chip_sum_small(own, recv, ids):
    def body(ids_ref, o_ref, r_ref, out_ref):
        j = ids_ref[1]
        total = None
        for i in range(4):
            m = jnp.bitwise_xor(i, j)
            term = jnp.where(m == 0, o_ref[...], jnp.where(m == 2, r_ref[0], jnp.where(m == 1, r_ref[1], r_ref[2])))
            total = term if total is None else total + term
        out_ref[...] = total

    grid_spec = pltpu.PrefetchScalarGridSpec(
        num_scalar_prefetch=1, grid=(1,),
        in_specs=[pl.BlockSpec(own.shape, lambda i, ids_ref: (0, 0)), pl.BlockSpec(recv.shape, lambda i, ids_ref: (0, 0, 0))],
        out_specs=pl.BlockSpec(own.shape, lambda i, ids_ref: (0, 0)))
    return pl.pallas_call(body, name="chip_sum_small", grid_spec=grid_spec,
                          out_shape=jax.ShapeDtypeStruct(own.shape, F32), compiler_params=_cparams(1))(ids, own, recv)


def _adamw(name, w, m, v, mine, theirs, ids):
    lead = (None,) * (w.ndim - 2)
    rows, cols = w.shape[-2:]
    half = rows // 2
    tr = _row_tile(half, ADAM_ROWS, unit=8)
    nb = half // tr
    c1 = 1.0 / (1.0 - ADAM_B1 ** ADAM_STEP)
    c2 = 1.0 / (1.0 - ADAM_B2 ** ADAM_STEP)

    def body(ids_ref, w_ref, m_ref, v_ref, mine_ref, theirs_ref, g_out, d_out, m_out, v_out):
        g = jnp.where(pl.program_id(0) == ids_ref[0], mine_ref[...], theirs_ref[...])
        m_new = ADAM_B1 * m_ref[...] + (1.0 - ADAM_B1) * g
        v_new = ADAM_B2 * v_ref[...] + (1.0 - ADAM_B2) * (g * g)
        d_out[...] = -ADAM_LR * ((m_new * c1) / (jnp.sqrt(v_new * c2) + ADAM_EPS) + ADAM_WD * w_ref[...])
        g_out[...] = g
        m_out[...] = m_new
        v_out[...] = v_new

    full = pl.BlockSpec(lead + (tr, cols), lambda h, i, ids_ref: (0,) * len(lead) + (h * nb + i, 0))
    part = pl.BlockSpec((tr, cols), lambda h, i, ids_ref: (i, 0))
    grid_spec = pltpu.PrefetchScalarGridSpec(num_scalar_prefetch=1, grid=(2, nb),
                                             in_specs=[full, full, full, part, part], out_specs=[full] * 4)
    return pl.pallas_call(
        body, name=name, grid_spec=grid_spec,
        out_shape=[jax.ShapeDtypeStruct(w.shape, F32)] * 4, compiler_params=_cparams(2),
    )(ids, w, m, v, mine, theirs)


def _adamw_whole(name, w, m, v, g):
    rows, cols = w.shape[-2:]
    tr = _row_tile(rows, 2 * ADAM_ROWS, unit=8)
    c1 = 1.0 / (1.0 - ADAM_B1 ** ADAM_STEP)
    c2 = 1.0 / (1.0 - ADAM_B2 ** ADAM_STEP)

    def body(w_ref, m_ref, v_ref, g_ref, g_out, d_out, m_out, v_out):
        g = g_ref[...]
        m_new = ADAM_B1 * m_ref[...] + (1.0 - ADAM_B1) * g
        v_new = ADAM_B2 * v_ref[...] + (1.0 - ADAM_B2) * (g * g)
        d_out[...] = -ADAM_LR * ((m_new * c1) / (jnp.sqrt(v_new * c2) + ADAM_EPS) + ADAM_WD * w_ref[...])
        g_out[...] = g
        m_out[...] = m_new
        v_out[...] = v_new

    full = pl.BlockSpec((None, tr, cols), lambda i: (0, i, 0))
    return pl.pallas_call(
        body, name=name, grid=(rows // tr,), in_specs=[full, full, full, pl.BlockSpec((tr, cols), lambda i: (i, 0))],
        out_specs=[full] * 4, out_shape=[jax.ShapeDtypeStruct(w.shape, F32)] * 4, compiler_params=_cparams(1),
    )(w, m, v, g)


def _adamw_replicated(g_rows, ws, ms, vs):
    n = len(ws)
    layout, _ = _rep_rows()
    c1 = 1.0 / (1.0 - ADAM_B1 ** ADAM_STEP)
    c2 = 1.0 / (1.0 - ADAM_B2 ** ADAM_STEP)

    def body(g_ref, *refs):
        w_refs, m_refs, v_refs = refs[0:n], refs[n:2 * n], refs[2 * n:3 * n]
        outs = refs[3 * n:]
        for k, (r0, rows) in enumerate(layout):
            width = w_refs[k].shape[1]
            g = jnp.concatenate([g_ref[r0 + j:r0 + j + 1, :] for j in range(rows)], axis=1)[:, 0:width]
            m_new = ADAM_B1 * m_refs[k][...] + (1.0 - ADAM_B1) * g
            v_new = ADAM_B2 * v_refs[k][...] + (1.0 - ADAM_B2) * (g * g)
            outs[k][...] = g
            outs[n + k][...] = -ADAM_LR * ((m_new * c1) / (jnp.sqrt(v_new * c2) + ADAM_EPS) + ADAM_WD * w_refs[k][...])
            outs[2 * n + k][...] = m_new
            outs[3 * n + k][...] = v_new

    res = pl.pallas_call(body, name="adamw_replicated",
                         out_shape=[jax.ShapeDtypeStruct(w.shape, F32) for _ in range(4) for w in ws])(g_rows, *ws, *ms, *vs)
    return [res[k * n:(k + 1) * n] for k in range(4)]


def _small_shard(parts):
    return _flatten(parts, _BIG[-1][1])


_ENTRY = {e[0]: e for e in _BIG}
_GRAD_ENTRY = {**_ENTRY, "w_in": ("w_in", IN_SHARD, D_MODEL, "chip_cols")}
FFN_MATS = ("w_ffn_down", "w_ffn_up")
MIXER_MATS = ("w_mix_out", "w_ssm_out", "w_attn_out")


class _StepPlan:
    def __init__(self, w, late_shards, shards, ids):
        self.w, self.g = w, {}
        self.late_shards, self.shards, self.ids = late_shards, shards, ids
        self.sums, self.halves, self.results = {}, {}, {}

    def run(self, name, fn, *args, **kw):
        at = getattr(self, "_at_" + name, None)
        if at is None:
            return fn(*args, **kw)
        exchange, landed = at()
        res, extra = fn(*args, bg=exchange, **kw)
        landed(extra)
        return res

    def _at_ssd_fwd(self):
        def landed(fulls):
            self.partly_gathered = fulls

        return _gather_ici([_ENTRY[n] for n in MIXER_MATS], [self.late_shards[n] for n in MIXER_MATS]), landed

    def _at_attn_fwd(self):
        stages = (_gather_pass_on([_ENTRY[n] for n in MIXER_MATS], self.partly_gathered),
                  _gather_ici([_ENTRY[n] for n in FFN_MATS], [self.late_shards[n] for n in FFN_MATS]))

        def landed(extra):
            mixer, self.partly_gathered = _split(stages, extra)
            self.w.update(zip(MIXER_MATS, mixer))

        return _join(*stages), landed

    def _at_ssm_gate_norm(self):
        return (_gather_pass_on([_ENTRY[n] for n in FFN_MATS], self.partly_gathered),
                lambda fulls: self.w.update(zip(FFN_MATS, fulls)))

    def pair_sums(self, names, grads, recv):
        for n, gr, rv in zip(names, grads, recv):
            _, r, c, lay = _GRAD_ENTRY[n]
            self.sums[n] = _pair_sum("pair_sum_" + n, gr, rv, self.ids, r, c, lay)

    def chip_sums(self, names, recv):
        for n, rv in zip(names, recv):
            self.halves[n] = _chip_sum("chip_sum_" + n, self.sums[n][1], rv)

    def adamw(self, names, theirs):
        for n, th in zip(names, theirs):
            sh = self.shards[n]
            if n == "w_in":
                mine_first = self.ids[0] == 0
                g_t = jnp.where(mine_first, jnp.concatenate([self.halves[n], th], axis=1),
                                jnp.concatenate([th, self.halves[n]], axis=1))
                res = _adamw_whole("adamw_" + n, *[jnp.swapaxes(sh[k], -1, -2) for k in ("w", "m", "v")], g_t)
                self.results[n] = [jnp.swapaxes(r, -1, -2) for r in res]
            else:
                self.results[n] = _adamw("adamw_" + n, sh["w"], sh["m"], sh["v"], self.halves[n], th, self.ids)

    def _pair_stage(self, names, grads):
        return (_pair_exchange([_GRAD_ENTRY[n] for n in names], grads),
                lambda recv: self.pair_sums(names, grads, recv))

    def _at_ffn_up_dx(self):
        return self._pair_stage(FFN_MATS, [self.g[n] for n in FFN_MATS])

    def _at_ssm_gate_norm_bwd(self):
        return self._pair_stage(MIXER_MATS, [self.g[n] for n in MIXER_MATS])

    def _at_attn_bwd(self):
        return _chip_exchange([self.sums[n][0] for n in FFN_MATS]), lambda recv: self.chip_sums(FFN_MATS, recv)

    def _at_ssd_bwd(self):
        stages = (_chip_exchange([self.sums[n][0] for n in MIXER_MATS]),
                  _whole_to_sibling([self.halves[n] for n in FFN_MATS]))

        def landed(extra):
            recv, theirs = _split(stages, extra)
            self.chip_sums(MIXER_MATS, recv)
            self.adamw(FFN_MATS, theirs)

        return _join(*stages), landed

    def _at_ssm_conv_bwd(self):
        return _whole_to_sibling([self.halves[n] for n in MIXER_MATS]), lambda theirs: self.adamw(MIXER_MATS, theirs)

    def _at_in_proj_dx(self):
        grads = [_from_cat_t(self.g.pop("w_cat_t"))]
        self.pair_sums(("w_in",), grads,
                       _run_exchange("grad_pair_exchange_w_in", _pair_exchange([_GRAD_ENTRY["w_in"]], grads)))
        return _chip_exchange([self.sums["w_in"][0]]), lambda recv: self.chip_sums(("w_in",), recv)

    def finish(self, g_small, g_rep, rep_shards):
        stages = (_pair_exchange([_ENTRY["small"]], [g_small]), _whole_to_sibling([g_rep]))
        recv_small, recv_rep = _split(stages, _run_exchange("grad_pair_exchange_tail", _join(*stages)))
        self.pair_sums(("small",), [g_small], recv_small)
        p_rep, = _rowwise("pair_sum_replicated", lambda r0, a, b: [a + b], SMALL_ROWS, SMALL_ROWS,
                          [(g_rep, LANES, 0), (recv_rep[0], LANES, 0)], [], [(LANES, F32)], [])
        stages = (_chip_exchange([self.sums["small"][0]]), _to_all_chips(p_rep))
        recv, recv_rep = _split(stages, _run_exchange("grad_chip_exchange_tail", _join(*stages)))
        self.chip_sums(("small",), recv)
        g_rep_tot = _chip_sum_small(p_rep, recv_rep[0], self.ids)
        last = ("w_in", "small")
        self.adamw(last, _run_exchange("grad_half_share_tail", _whole_to_sibling([self.halves[n] for n in last])))
        self.results["replicated"] = _adamw_replicated(g_rep_tot, rep_shards["w"], rep_shards["m"], rep_shards["v"])
        return g_rep_tot[_rep_rows()[1], 0]


def kernel(x, meta_tokens, norm_pre_mix, w_in, ssm_conv_w, ssm_conv_b, ssm_dt_bias, ssm_a_log, ssm_d_skip, ssm_norm, w_ssm_out, attn_sinks, w_attn_out, w_mix_out, norm_post_mix, norm_pre_ffn, w_ffn_up, ffn_conv_w, ffn_conv_b, w_ffn_down, norm_post_ffn, loss_target, m_meta_tokens, m_norm_pre_mix, m_w_in, m_ssm_conv_w, m_ssm_conv_b, m_ssm_dt_bias, m_ssm_a_log, m_ssm_d_skip, m_ssm_norm, m_w_ssm_out, m_attn_sinks, m_w_attn_out, m_w_mix_out, m_norm_post_mix, m_norm_pre_ffn, m_w_ffn_up, m_ffn_conv_w, m_ffn_conv_b, m_w_ffn_down, m_norm_post_ffn, v_meta_tokens, v_norm_pre_mix, v_w_in, v_ssm_conv_w, v_ssm_conv_b, v_ssm_dt_bias, v_ssm_a_log, v_ssm_d_skip, v_ssm_norm, v_w_ssm_out, v_attn_sinks, v_w_attn_out, v_w_mix_out, v_norm_post_mix, v_norm_pre_ffn, v_w_ffn_up, v_ffn_conv_w, v_ffn_conv_b, v_w_ffn_down, v_norm_post_ffn):
    args = dict(locals())
    squeeze = lambda a: a.reshape(a.shape[-2:])
    wts = {n: squeeze(args[n]) for n in WEIGHT_ORDER}
    mom = {n: squeeze(args["m_" + n]) for n in WEIGHT_ORDER}
    var = {n: squeeze(args["v_" + n]) for n in WEIGHT_ORDER}
    x_i, y_i, c_i = _mesh_pos()
    ids = jnp.stack([c_i, _chip_index(x_i, y_i)]).astype(jnp.int32)
    big_names = [n for n, _, _, _ in _BIG[:-1]]
    small_names = [n for n, _, _ in _SMALL_SHARDED]
    rep_names = [n for n, _ in _REPLICATED]

    stacks = {"w": wts, "m": mom, "v": var}
    shards = {n: {"w": args[n], "m": args["m_" + n], "v": args["v_" + n]} for n in big_names}
    shards["small"] = {k: _small_shard([d[n] for n in small_names]) for k, d in stacks.items()}
    rep_shards = {k: [d[n] for n in rep_names] for k, d in stacks.items()}

    w_in4, small_all = _gather_weights([_ENTRY["w_in"], _ENTRY["small"]], [wts["w_in"].astype(BF16), shards["small"]["w"]])
    w = {n: wts[n] for n in rep_names}
    w["w_cat"] = _to_cat(w_in4)
    small_parts = [_unflatten(small_all[i], [shp for _, shp, _ in _SMALL_SHARDED]) for i in range(4)]
    for k, (n, _, axis) in enumerate(_SMALL_SHARDED):
        w[n] = jnp.concatenate([small_parts[i][k] for i in range(4)], axis=axis)
    plan = _StepPlan(w, {n: wts[n].astype(BF16) for n in MIXER_MATS + FFN_MATS}, shards, ids)

    head = jnp.concatenate([jnp.zeros((PAD, D_MODEL), F32), w["meta_tokens"]], axis=0)
    loss_sum, dx, dhead = _local_step(x[0], head, loss_target[0], plan)
    g = plan.g
    g["meta_tokens"] = dhead[PAD:]
    g_small = jnp.stack([_small_shard([_shard_of(g[n], i, shp, ax) for n, shp, ax in _SMALL_SHARDED]) for i in range(4)])
    loss_part = (loss_sum * (0.5 / D_MODEL)).reshape(1, 1)
    loss = plan.finish(g_small, _in_rows([g[n] for n in rep_names] + [loss_part]), rep_shards)

    results = {}
    for kind in range(4):
        results.update({(kind, n): plan.results[n][kind] for n in big_names})
        parts = _unflatten(plan.results["small"][kind], [shp for _, shp, _ in _SMALL_SHARDED])
        results.update({(kind, n): parts[k] for k, n in enumerate(small_names)})
        results.update({(kind, n): plan.results["replicated"][kind][k] for k, n in enumerate(rep_names)})
    outs = [results[kind, n].reshape(args[n].shape) for kind in range(4) for n in WEIGHT_ORDER]
    return (loss, dx[None], *outs)
```

```python
import math
from typing import Any, Callable, NamedTuple, Sequence

import jax
import jax.numpy as jnp
from jax import lax
from jax.experimental import pallas as pl
from jax.experimental.pallas import tpu as pltpu

F32 = jnp.float32
BF16 = jnp.bfloat16

D_MODEL = 1024
N_META = 16
T = 128
PAD = T - N_META
D_INNER = 2048
SSM_HEADS = 32
HEAD_P = 64
SSM_GROUPS = 4
GROUP_W = D_INNER // SSM_GROUPS
D_STATE = 128
CONV_DIM = D_INNER + 2 * SSM_GROUPS * D_STATE
ATTN_HEADS = 16
KV_HEADS = 4
ATTN_W = 1024
KV_W = 256
FFN_DIM = 2816
N_IN = 8736
EPS = 1e-6
NEG = -1e30
SCALE = 0.125

P_Q, P_K, P_V, P_DT, P_Z, P_GATE, P_XBC = 0, 1024, 1280, 1536, 2048, 4096, 6144
QKV_W = 1536
P_W = 9216

ADAM_LR, ADAM_B1, ADAM_B2, ADAM_EPS, ADAM_WD, ADAM_STEP = 0.001, 0.9, 0.999, 1e-08, 0.01, 10

VMEM_BUDGET = 40 * 1024 * 1024
VMEM_LIMIT = 56 * 1024 * 1024
MESH = pl.DeviceIdType.MESH
ANY = pl.BlockSpec(memory_space=pl.ANY)


def _cparams(n_axes, **kw):
    return pltpu.CompilerParams(dimension_semantics=("arbitrary",) * n_axes, vmem_limit_bytes=VMEM_LIMIT, **kw)


class _Exchange(NamedTuple):
    ins: Sequence[Any]
    out_shapes: Sequence[Any]
    make_copies: Callable
    n_copies: int
    aliases: dict = {}


def _call(body, name, grid, in_specs, out_specs, out_shape, operands, scratch_shapes=(), aliases=None, bg=None):
    aliases = dict(aliases or {})
    if bg is None:
        return pl.pallas_call(body, name=name, grid=grid, in_specs=in_specs, out_specs=out_specs, out_shape=out_shape,
                              scratch_shapes=list(scratch_shapes), input_output_aliases=aliases,
                              compiler_params=_cparams(len(grid)))(*operands)
    n_in, n_out, n_scr = len(in_specs), len(out_specs), len(scratch_shapes)
    nb_in, nb_out = len(bg.ins), len(bg.out_shapes)

    def hosted(*refs):
        ins, bg_ins = refs[:n_in], refs[n_in:n_in + nb_in]
        outs = refs[n_in + nb_in:n_in + nb_in + n_out]
        bg_outs = refs[n_in + nb_in + n_out:n_in + nb_in + n_out + nb_out]
        scratch = refs[n_in + nb_in + n_out + nb_out:n_in + nb_in + n_out + nb_out + n_scr]
        send_sems, recv_sems = refs[-2:]
        pids = [pl.program_id(a) for a in range(len(grid))]
        first, last = pids[0] == 0, pids[0] == grid[0] - 1
        for p, g in zip(pids[1:], grid[1:]):
            first, last = first & (p == 0), last & (p == g - 1)
        copies = []
        for k, (src, dst, peer) in enumerate(bg.make_copies(bg_ins, bg_outs)):
            if peer is None:
                copies.append(pltpu.make_async_copy(src, dst, send_sems.at[k]))
            else:
                copies.append(pltpu.make_async_remote_copy(src_ref=src, dst_ref=dst, send_sem=send_sems.at[k],
                                                           recv_sem=recv_sems.at[k], device_id=peer, device_id_type=MESH))
        assert len(copies) == bg.n_copies

        @pl.when(first)
        def _():
            for cp in copies:
                cp.start()

        body(*ins, *outs, *scratch)

        @pl.when(last)
        def _():
            for cp in copies:
                cp.wait()

    aliases = {(k if k < n_in else k + nb_in): v for k, v in aliases.items()}
    aliases.update({n_in + k: n_out + v for k, v in bg.aliases.items()})
    res = pl.pallas_call(
        hosted, name=name, grid=grid, in_specs=list(in_specs) + [ANY] * nb_in, out_specs=list(out_specs) + [ANY] * nb_out,
        out_shape=list(out_shape) + list(bg.out_shapes), input_output_aliases=aliases,
        scratch_shapes=list(scratch_shapes) + [pltpu.SemaphoreType.DMA((bg.n_copies,))] * 2,
        compiler_params=_cparams(len(grid), has_side_effects=True))(*operands, *bg.ins)
    return res[:n_out], res[n_out:]


def _sigmoid(x):
    return 1.0 / (1.0 + jnp.exp(-x))


def _silu(x):
    return x * _sigmoid(x)


def _silu_grad(x):
    s = _sigmoid(x)
    return x * s, s * (1.0 + x * (1.0 - s))


def _dsilu(x):
    return _silu_grad(x)[1]


def _softplus(x):
    e = jnp.exp(-jnp.abs(x))
    small = e * (1.0 - e * (0.5 - e * (1.0 / 3.0)))
    return jnp.maximum(x, 0.0) + jnp.where(e < 0.01, small, jnp.log(1.0 + e))


def _rms(x, w):
    r = lax.rsqrt(jnp.mean(x * x, axis=-1, keepdims=True) + EPS)
    return x * r * w


def _rms_bwd(dy, x, w):
    r = lax.rsqrt(jnp.mean(x * x, axis=-1, keepdims=True) + EPS)
    xh = x * r
    g = dy * w
    dx = r * (g - xh * jnp.mean(g * xh, axis=-1, keepdims=True))
    dw = jnp.sum(dy * xh, axis=0, keepdims=True)
    return dx, dw


def _dot(a, b):
    return jnp.dot(a, b, preferred_element_type=F32)


def _dot_nt(a, b):
    return lax.dot_general(a, b, (((1,), (1,)), ((), ())), preferred_element_type=F32)


def _dot_tn(a, b):
    return lax.dot_general(a, b, (((0,), (0,)), ((), ())), preferred_element_type=F32)


def _split3(x):
    hi = x.astype(BF16)
    r = x - hi.astype(F32)
    mid = r.astype(BF16)
    lo = (r - mid.astype(F32)).astype(BF16)
    return hi, mid, lo


def _xdot(x, e):
    hi, mid, lo = _split3(x)
    return _dot(hi, e) + _dot(mid, e) + _dot(lo, e)


def _xdot_l(e, x):
    hi, mid, lo = _split3(x)
    return _dot(e, hi) + _dot(e, mid) + _dot(e, lo)


def _iota(shape, dim):
    return lax.broadcasted_iota(jnp.int32, shape, dim)


def _divisors(n, unit):
    return [t for t in range(unit, n + 1, unit) if n % t == 0]


MIN_MATMUL_STEPS = 8


def _matmul_tiles(m, n, k, a_bytes, b_bytes, o_bytes, m_unit):
    best = None
    for tm in _divisors(m, m_unit):
        for tn in _divisors(n, 128):
            for tk in _divisors(k, 128):
                acc = 0 if tk == k else tm * tn * 4
                vm = 2 * (tm * tk * a_bytes + tk * tn * b_bytes + tm * tn * o_bytes) + acc
                if vm > VMEM_BUDGET:
                    continue
                steps = (m // tm) * (n // tn) * (k // tk)
                score = (tk == k, min(steps, MIN_MATMUL_STEPS), min(tm, 256), tm * tn * tk)
                if best is None or score > best[0]:
                    best = (score, (tm, tn, tk))
    return best[1]


def _matmul(name, a, b, mode, out_dtype, bg=None):
    if mode == "nn":
        (m, k), n = a.shape, b.shape[1]
    elif mode == "nt":
        (m, k), n = a.shape, b.shape[0]
    else:
        (k, m), n = a.shape, b.shape[1]
    ab, bb, ob = a.dtype.itemsize, b.dtype.itemsize, jnp.dtype(out_dtype).itemsize
    tm, tn, tk = _matmul_tiles(m, n, k, ab, bb, ob, 128 if mode == "tn" else 16)
    nk = k // tk
    dot = {"nn": _dot, "nt": _dot_nt, "tn": _dot_tn}[mode]

    def body(a_ref, b_ref, o_ref, *scratch):
        prod = dot(a_ref[...].astype(BF16), b_ref[...].astype(BF16))
        if nk == 1:
            o_ref[...] = prod.astype(o_ref.dtype)
        else:
            acc_ref, = scratch
            kk = pl.program_id(2)

            @pl.when(kk == 0)
            def _():
                acc_ref[...] = prod

            @pl.when(kk > 0)
            def _():
                acc_ref[...] += prod

            @pl.when(kk == nk - 1)
            def _():
                o_ref[...] = acc_ref[...].astype(o_ref.dtype)

    a_spec = pl.BlockSpec((tk, tm), lambda i, j, kk: (kk, i)) if mode == "tn" else pl.BlockSpec((tm, tk), lambda i, j, kk: (i, kk))
    b_spec = pl.BlockSpec((tn, tk), lambda i, j, kk: (j, kk)) if mode == "nt" else pl.BlockSpec((tk, tn), lambda i, j, kk: (kk, j))
    res = _call(body, name, (m // tm, n // tn, nk), [a_spec, b_spec], [pl.BlockSpec((tm, tn), lambda i, j, kk: (i, j))],
                [jax.ShapeDtypeStruct((m, n), out_dtype)], [a, b],
                scratch_shapes=[] if nk == 1 else [pltpu.VMEM((tm, tn), F32)], bg=bg)
    return res[0] if bg is None else (res[0][0], res[1])


def _row_tile(n_rows, cap, unit=16):
    return max([t for t in _divisors(n_rows, unit) if t <= cap], default=n_rows)


ROW_SUB = 384
GROUP_UNROLL = 4


def _rowwise(name, fn, n_rows, tm, row_ins, full_ins, row_outs, acc_outs, bg=None):
    n_in = len(row_ins) + len(full_ins)
    n_ro = len(row_outs)
    into = [(k, o[3]) for k, o in enumerate(row_outs) if len(o) > 2 and o[2] == "into"]

    n_row_in = len(row_ins)
    sub = min(tm, ROW_SUB)

    def body(*refs):
        i = pl.program_id(0)
        outs = refs[n_in + len(into):]

        sums = tuple(jnp.zeros((1, w), F32) for w in acc_outs)
        for s in range(tm // sub):
            rows = pl.ds(s * sub, sub)
            vals = [r[rows, :] for r in refs[:n_row_in]] + [r[...] for r in refs[n_row_in:n_in]]
            res = fn(i * tm + s * sub, *vals)
            for o, r, v in zip(row_outs, outs[:n_ro], res[:n_ro]):
                if len(o) > 2 and o[2] == "first":
                    @pl.when(i == 0)
                    def _(r=r, v=v, rows=rows):
                        r[rows, :] = v.astype(r.dtype)
                else:
                    r[rows, :] = v.astype(r.dtype)
            sums = tuple(a + v for a, v in zip(sums, res[n_ro:]))

        @pl.when(i == 0)
        def _():
            for r, v in zip(outs[n_ro:], sums):
                r[...] = v

        @pl.when(i > 0)
        def _():
            for r, v in zip(outs[n_ro:], sums):
                r[...] += v

    def in_spec(entry):
        w, cb = entry[1], entry[2]
        if len(entry) > 3 and entry[3] == "prev":
            return pl.BlockSpec((tm, w), lambda i: (jnp.maximum(i - 1, 0), cb))
        if len(entry) > 3 and entry[3] == "first":
            return pl.BlockSpec((tm, w), lambda i: (0, cb))
        return pl.BlockSpec((tm, w), lambda i: (i, cb))

    def out_spec(o):
        if len(o) == 2:
            return pl.BlockSpec((tm, o[0]), lambda i: (i, 0)), jax.ShapeDtypeStruct((n_rows, o[0]), o[1])
        if o[2] == "new":
            return pl.BlockSpec((tm, o[0]), lambda i: (i, o[4])), jax.ShapeDtypeStruct((n_rows, o[3]), o[1])
        if o[2] == "into":
            return pl.BlockSpec((tm, o[0]), lambda i: (i, o[4])), jax.ShapeDtypeStruct(o[3].shape, o[3].dtype)
        if o[2] == "first":
            return pl.BlockSpec((tm, o[0]), lambda i: (0, 0)), jax.ShapeDtypeStruct((tm, o[0]), o[1])
        return pl.BlockSpec((tm, o[0]), lambda i: (jnp.maximum(i - 1, 0), 0)), jax.ShapeDtypeStruct((o[3], o[0]), o[1])

    in_specs = [in_spec(e) for e in row_ins]
    in_specs += [pl.BlockSpec(a.shape, lambda i: (0, 0)) for a in full_ins]
    in_specs += [pl.BlockSpec(memory_space=pl.ANY) for _ in into]
    specs_shapes = [out_spec(o) for o in row_outs]
    out_specs = [s for s, _ in specs_shapes] + [pl.BlockSpec((1, w), lambda i: (0, 0)) for w in acc_outs]
    out_shape = [s for _, s in specs_shapes] + [jax.ShapeDtypeStruct((1, w), F32) for w in acc_outs]
    return _call(body, name, (n_rows // tm,), in_specs, out_specs, out_shape,
                 [e[0] for e in row_ins] + list(full_ins) + [arr for _, arr in into],
                 aliases={n_in + a: k for a, (k, _) in enumerate(into)}, bg=bg)


def _valid_rows(first_row, tm, lo):
    return (first_row + _iota((tm, 1), 0)) >= lo


CONV_ROWS = 128
CONV_SUB = 16
CONV_LANES = 256


def _conv_specs(tm, width, blk, n_rows, after):
    specs = [pl.BlockSpec((tm, width), lambda i: (i, blk)),
             pl.BlockSpec((8, width), lambda i: (jnp.maximum(i * (tm // 8) - 1, 0), blk))]
    if after:
        specs.append(pl.BlockSpec((16, width), lambda i: (jnp.minimum((i + 1) * (tm // 16), n_rows // 16 - 1), blk)))
    return specs


def _conv_window(win, w_ref, b_ref, taps, c0, cw, n):
    acc = b_ref[:, c0:c0 + cw] + w_ref[taps - 1:taps, c0:c0 + cw] * win[8:8 + n]
    for k in range(taps - 1):
        acc = acc + w_ref[k:k + 1, c0:c0 + cw] * win[8 - (taps - 1) + k:8 - (taps - 1) + k + n]
    return acc


def _ffn_act(name, u_raw, conv_w, conv_b, n_rows):
    tm, sub, cw = CONV_ROWS, CONV_SUB, CONV_LANES
    taps, width = conv_w.shape
    half = width // 2

    def body(cur_ref, prev_ref, w_ref, b_ref, f_ref, ext_ref):
        i = pl.program_id(0)
        ext_ref[0:8, :] = jnp.where(i > 0, prev_ref[...], 0.0)
        ext_ref[8:8 + tm, :] = cur_ref[...]
        for q in range(half // cw):
            a0, g0 = q * cw, half + q * cw

            def group(s, carry):
                r = pl.multiple_of(s * sub, sub)
                a = _conv_window(ext_ref[pl.ds(r, sub + 8), a0:a0 + cw], w_ref, b_ref, taps, a0, cw, sub)
                g = _conv_window(ext_ref[pl.ds(r, sub + 8), g0:g0 + cw], w_ref, b_ref, taps, g0, cw, sub)
                f_ref[pl.ds(r, sub), a0:a0 + cw] = (_silu(a) * g).astype(f_ref.dtype)
                return carry

            lax.fori_loop(0, tm // sub, group, 0, unroll=GROUP_UNROLL)

        @pl.when(i == 0)
        def _():
            f_ref[0:PAD, :] = jnp.zeros((PAD, half), f_ref.dtype)

    return pl.pallas_call(
        body, name=name, grid=(n_rows // tm,),
        in_specs=_conv_specs(tm, width, 0, n_rows, False) + [pl.BlockSpec((taps, width), lambda i: (0, 0)),
                                                             pl.BlockSpec((1, width), lambda i: (0, 0))],
        out_specs=pl.BlockSpec((tm, half), lambda i: (i, 0)),
        out_shape=jax.ShapeDtypeStruct((n_rows, half), BF16),
        scratch_shapes=[pltpu.VMEM((tm + 8, width), F32)],
        compiler_params=_cparams(1),
    )(u_raw, u_raw, conv_w, conv_b)


def _conv_bwd(name, raw, raw_blk, dsrcs, chunk_src, conv_w, conv_b, n_rows, gated, into=None, into_blk=0, bg=None):
    taps, width = conv_w.shape
    half = width // 2 if gated else width
    tm, sub, cw = CONV_ROWS, CONV_SUB, CONV_LANES
    te = tm + 16
    nd = len(dsrcs)
    n_parts = 2 if gated else 1

    def body(*refs):
        cur_ref, prev_ref, next_ref = refs[0:3]
        dcur, dnext = refs[3:3 + nd], refs[3 + nd:3 + 2 * nd]
        w_ref, b_ref = refs[3 + 2 * nd:5 + 2 * nd]
        out_ref, acc_ref, ext_ref, du_ref = refs[-4:]
        i = pl.program_id(0)
        ext_ref[0:8, :] = jnp.where(i > 0, prev_ref[...], 0.0)
        ext_ref[8:8 + tm, :] = cur_ref[...]
        ext_ref[8 + tm:24 + tm, :] = next_ref[...]

        for q, (src, off) in enumerate(chunk_src):
            cols = [q * cw, half + q * cw][:n_parts]

            def conv_grad(r, d, past_end):
                pre = [_conv_window(ext_ref[pl.ds(r, sub + 8), c0:c0 + cw], w_ref, b_ref, taps, c0, cw, sub) for c0 in cols]
                if gated:
                    act, dact = _silu_grad(pre[0])
                    dus = [d * pre[1] * dact, d * act]
                else:
                    dus = [d * _dsilu(pre[0])]
                for part, du in enumerate(dus):
                    if past_end:
                        du = jnp.where(i * tm + r + _iota((sub, 1), 0) < n_rows, du, 0.0)
                    du_ref[part, pl.ds(r, sub), :] = du

            def tile_rows(s, carry):
                r = pl.multiple_of(s * sub, sub)
                conv_grad(r, dcur[src][pl.ds(r, sub), off:off + cw].astype(F32), False)
                return carry

            lax.fori_loop(0, tm // sub, tile_rows, 0, unroll=GROUP_UNROLL)
            conv_grad(tm, dnext[src][:, off:off + cw].astype(F32), True)

            @pl.when(i == 0)
            def _():
                du_ref[:, 0:PAD, :] = jnp.zeros((n_parts, PAD, cw), F32)

            for part, c0 in enumerate(cols):
                taps_w = [w_ref[k:k + 1, c0:c0 + cw] for k in range(taps)]

                def back(s, sums):
                    new = list(sums)
                    for u in range(2):
                        r = pl.multiple_of((2 * s + u) * sub, sub)
                        win = du_ref[part, pl.ds(r, sub + 8), :]
                        raw_rows = ext_ref[pl.ds(8 + r, sub), c0:c0 + cw]
                        draw = jnp.zeros((sub, cw), F32)
                        for k in range(taps):
                            shifted = win[taps - 1 - k:taps - 1 - k + sub]
                            draw = draw + taps_w[k] * shifted
                            new[k] = new[k] + shifted * raw_rows
                        new[taps] = new[taps] + win[0:sub]
                        out_ref[pl.ds(r, sub), c0:c0 + cw] = draw.astype(out_ref.dtype)
                    return tuple(new)

                sums = lax.fori_loop(0, tm // (2 * sub), back, tuple(jnp.zeros((sub, cw), F32) for _ in range(taps + 1)))

                @pl.when(i == 0)
                def _(c0=c0):
                    out_ref[PAD - sub:PAD, c0:c0 + cw] = jnp.zeros((sub, cw), out_ref.dtype)

                for k in range(taps + 1):
                    total = jnp.sum(sums[k], axis=0, keepdims=True)
                    acc_ref[k:k + 1, c0:c0 + cw] = jnp.where(i == 0, total, acc_ref[k:k + 1, c0:c0 + cw] + total)

    in_specs = _conv_specs(tm, width, raw_blk, n_rows, True)
    in_specs += [pl.BlockSpec((tm, d.shape[1]), lambda i: (i, 0)) for d in dsrcs]
    in_specs += [pl.BlockSpec((16, d.shape[1]), lambda i: (jnp.minimum((i + 1) * (tm // 16), n_rows // 16 - 1), 0)) for d in dsrcs]
    in_specs += [pl.BlockSpec((taps, width), lambda i: (0, 0)), pl.BlockSpec((1, width), lambda i: (0, 0))]
    operands = [raw, raw, raw] + list(dsrcs) + list(dsrcs) + [conv_w, conv_b]
    aliases = {}
    if into is None:
        out0 = jax.ShapeDtypeStruct((n_rows, width), BF16)
    else:
        in_specs.append(pl.BlockSpec(memory_space=pl.ANY))
        operands.append(into)
        aliases = {len(operands) - 1: 0}
        out0 = jax.ShapeDtypeStruct(into.shape, into.dtype)
    return _call(body, name, (n_rows // tm,), in_specs,
                 [pl.BlockSpec((tm, width), lambda i: (i, into_blk)), pl.BlockSpec((8, width), lambda i: (0, 0))],
                 [out0, jax.ShapeDtypeStruct((8, width), F32)], operands,
                 scratch_shapes=[pltpu.VMEM((tm + 24, width), F32), pltpu.VMEM((n_parts, te + 8, cw), F32)],
                 aliases=aliases, bg=bg)


def _ssd_specs(n_chunks, rev, per_step=1):
    cidx = (lambda c: n_chunks - 1 - c) if rev else (lambda c: c)
    xw, nw = per_step * GROUP_W, per_step * D_STATE
    xg0, bg0, cg0 = P_XBC // xw, (P_XBC + D_INNER) // nw, (P_XBC + D_INNER + SSM_GROUPS * D_STATE) // nw

    def cur(width, blk0):
        return pl.BlockSpec((T, width), lambda g, c: (cidx(c), blk0 + g))

    def prev(width, blk0):
        return pl.BlockSpec((8, width), lambda g, c: (jnp.maximum(cidx(c) * (T // 8) - 1, 0), blk0 + g))

    specs = [cur(xw, xg0), prev(xw, xg0), cur(nw, bg0), prev(nw, bg0), cur(nw, cg0), prev(nw, cg0),
             pl.BlockSpec((T, 128), lambda g, c: (cidx(c), P_DT // 128))]
    wb, wc = D_INNER // nw, (D_INNER + SSM_GROUPS * D_STATE) // nw
    specs += [pl.BlockSpec((4, xw), lambda g, c: (0, g)),
              pl.BlockSpec((4, nw), lambda g, c: (0, wb + g)),
              pl.BlockSpec((4, nw), lambda g, c: (0, wc + g)),
              pl.BlockSpec((1, xw), lambda g, c: (0, g)),
              pl.BlockSpec((1, nw), lambda g, c: (0, wb + g)),
              pl.BlockSpec((1, nw), lambda g, c: (0, wc + g))]
    specs += [pl.BlockSpec((1, 128), lambda g, c: (0, 0))] * 3
    return specs, cidx


def _ssd_shared(refs, c):
    dt_ref, dtb_ref, alog_ref = refs[6], refs[13], refs[14]
    valid = _valid_rows(c * T, T, PAD)
    dtr = dt_ref[...] + dtb_ref[...]
    dt = jnp.where(valid, _softplus(dtr), 0.0)
    a_neg = -jnp.exp(alog_ref[...])
    tril = _iota((T, T), 0) >= _iota((T, T), 1)
    cs = _xdot_l(tril.astype(BF16), dt * a_neg)
    return dict(valid=valid, dtr=dtr, dt=dt, a_neg=a_neg, tril=tril, cs=cs, cs_t=cs.T)


def _ssd_chunk_forward(refs, ext_ref, g, c, shared):
    (xc_ref, xp_ref, bc_ref, bp_ref, cc_ref, cp_ref, dt_ref, wx_ref, wb_ref, wc_ref,
     bx_ref, bb_ref, bcb_ref, dtb_ref, alog_ref, dsk_ref) = refs

    def conv_pre(cur_ref, prev_ref, w_ref, b_ref, width):
        ext_ref[0:8, 0:width] = jnp.where(c > 0, prev_ref[...], 0.0)
        ext_ref[8:8 + T, 0:width] = cur_ref[...]
        w = w_ref[...]
        acc = b_ref[...] + w[3:4] * cur_ref[...]
        for k in range(3):
            acc = acc + w[k:k + 1] * ext_ref[pl.ds(5 + k, T), 0:width]
        return acc

    v = dict(shared)
    valid, dt, cs = v["valid"], v["dt"], v["cs"]
    v["head0"] = 8 * g
    v["x_pre"] = conv_pre(xc_ref, xp_ref, wx_ref, bx_ref, GROUP_W)
    v["b_pre"] = conv_pre(bc_ref, bp_ref, wb_ref, bb_ref, D_STATE)
    v["c_pre"] = conv_pre(cc_ref, cp_ref, wc_ref, bcb_ref, D_STATE)
    xs = _silu(v["x_pre"])
    bm = jnp.where(valid, _silu(v["b_pre"]), 0.0)
    cm = jnp.where(valid, _silu(v["c_pre"]), 0.0)
    hh, ll = _iota((128, GROUP_W), 0), _iota((128, GROUP_W), 1)
    expand = (hh == 8 * g + jnp.right_shift(ll, 6)).astype(BF16)
    hh_t, ll_t = _iota((GROUP_W, 128), 1), _iota((GROUP_W, 128), 0)
    v["expand_t"] = (hh_t == 8 * g + jnp.right_shift(ll_t, 6)).astype(BF16)
    cs_e = _xdot(cs, expand)
    dt_e = _xdot(dt, expand)
    cs_last_e = cs_e[T - 1:T, :]
    v.update(xs=xs, bm=bm, cm=cm, expand=expand, cs_e=cs_e, dt_e=dt_e, cs_last_e=cs_last_e)
    v["xdt"] = xs * dt_e
    v["decay_e"] = jnp.exp(cs_last_e - cs_e)
    v["ecs_e"] = jnp.exp(cs_e)
    v["elast_e"] = jnp.exp(cs_last_e)
    v["d_e"] = _xdot(dsk_ref[...], expand)
    v["gmat"] = _dot_nt(cm.astype(BF16), bm.astype(BF16))
    return v


def _ssd_decay_pair(v, jp):
    out = []
    for j in (v["head0"] + 2 * jp, v["head0"] + 2 * jp + 1):
        diff = v["cs"][:, j:j + 1] - v["cs_t"][j:j + 1, :]
        out.append(jnp.where(v["tril"], jnp.exp(jnp.where(v["tril"], diff, 0.0)), 0.0))
    return out


def _block_diag_pair(xp):
    lane = _iota(xp.shape, 1)
    return jnp.concatenate([jnp.where(lane < HEAD_P, xp, 0.0), jnp.where(lane >= HEAD_P, xp, 0.0)], axis=0)


SSD_GROUPS_PER_STEP = 4


def _ssd_group_refs(refs, gg):
    x_w, n_w = pl.ds(GROUP_W * gg, GROUP_W), pl.ds(D_STATE * gg, D_STATE)
    lanes = [x_w, x_w, n_w, n_w, n_w, n_w, None, x_w, n_w, n_w, x_w, n_w, n_w, None, None, None]
    return [r if w is None else r.at[:, w] for r, w in zip(refs, lanes)]


def _ssd_fwd(p, conv_w, conv_b, dt_bias, a_log, d_skip, n_chunks, bg=None):
    n_rows = n_chunks * T
    in_specs, _ = _ssd_specs(n_chunks, rev=False, per_step=SSD_GROUPS_PER_STEP)
    per = SSD_GROUPS_PER_STEP
    assert per == SSM_GROUPS

    def body(*refs):
        y_ref, hin_ref, st_ref, ext_ref = refs[16:]
        c = pl.program_id(1)

        @pl.when(c == 0)
        def _():
            st_ref[...] = jnp.zeros_like(st_ref)

        shared = _ssd_shared(refs[:16], c)
        for gg in range(per):
            v = _ssd_chunk_forward(_ssd_group_refs(refs[:16], gg), ext_ref.at[gg], gg, c, shared)
            state = st_ref[gg]
            hin_ref[gg] = state
            ys = []
            for jp in range(4):
                l0, l1 = _ssd_decay_pair(v, jp)
                lhs = jnp.concatenate([v["gmat"] * l0, v["gmat"] * l1], axis=1).astype(BF16)
                rhs = _block_diag_pair(v["xdt"][:, 128 * jp:128 * jp + 128]).astype(BF16)
                ys.append(_dot(lhs, rhs))
            y = jnp.concatenate(ys, axis=1)
            y = y + _dot(v["cm"].astype(BF16), state.astype(BF16)) * v["ecs_e"] + v["xs"] * v["d_e"]
            y_ref[:, GROUP_W * gg:GROUP_W * gg + GROUP_W] = y
            s_new = _dot_tn(v["bm"].astype(BF16), (v["xdt"] * v["decay_e"]).astype(BF16))
            st_ref[gg] = state * v["elast_e"] + s_new

    return _call(
        body, "ssd_fwd", (SSM_GROUPS // per, n_chunks), in_specs,
        [pl.BlockSpec((T, per * GROUP_W), lambda g, c: (c, g)),
         pl.BlockSpec((per, None, D_STATE, GROUP_W), lambda g, c: (g, c, 0, 0))],
        [jax.ShapeDtypeStruct((n_rows, D_INNER), F32),
         jax.ShapeDtypeStruct((SSM_GROUPS, n_chunks, D_STATE, GROUP_W), F32)],
        [p, p, p, p, p, p, p, conv_w, conv_w, conv_w, conv_b, conv_b, conv_b, dt_bias, a_log, d_skip],
        scratch_shapes=[pltpu.VMEM((per, D_STATE, GROUP_W), F32), pltpu.VMEM((per, T + 8, GROUP_W), F32)], bg=bg)


def _ssd_bwd(p, conv_w, conv_b, dt_bias, a_log, d_skip, hin, dy, dp, n_chunks, bg=None):
    n_rows = n_chunks * T
    per = SSD_GROUPS_PER_STEP
    assert per == SSM_GROUPS
    dt_w = P_Z - P_DT
    in_specs, cidx = _ssd_specs(n_chunks, rev=True, per_step=per)
    in_specs = in_specs + [pl.BlockSpec((per, None, D_STATE, GROUP_W), lambda g, c: (g, cidx(c), 0, 0)),
                           pl.BlockSpec((T, per * GROUP_W), lambda g, c: (cidx(c), g)), ANY]

    def body(*refs):
        hin_ref, dy_ref = refs[16:18]
        dx_ref, db_ref, dc_ref, dp_ref, dpar_ref, dst_ref, ext_ref, ddt_ref = refs[19:]
        shared = _ssd_shared(refs[:16], n_chunks - 1 - pl.program_id(1))
        for gg in range(per):
            x_w, n_w = pl.ds(GROUP_W * gg, GROUP_W), pl.ds(D_STATE * gg, D_STATE)
            group_body(_ssd_group_refs(refs[:16], gg), hin_ref.at[gg], dy_ref.at[:, x_w], dx_ref.at[:, x_w],
                       db_ref.at[:, n_w], dc_ref.at[:, n_w], ddt_ref.at[:, n_w], dpar_ref.at[gg], dst_ref.at[gg],
                       ext_ref.at[gg], gg, shared)
        ddt = ddt_ref[:, 0:128] + ddt_ref[:, 128:256] + ddt_ref[:, 256:384] + ddt_ref[:, 384:512]
        dp_ref[...] = jnp.concatenate([ddt, jnp.zeros((T, dt_w - 128), F32)], axis=1).astype(dp_ref.dtype)

    def group_body(in_refs, hin_ref, dy_ref, dx_ref, db_ref, dc_ref, ddt_ref, dpar_ref, dst_ref, ext_ref, g, shared):
        step = pl.program_id(1)
        c = n_chunks - 1 - step

        @pl.when(step == 0)
        def _():
            dst_ref[...] = jnp.zeros_like(dst_ref)

        v = _ssd_chunk_forward(in_refs, ext_ref, g, c, shared)
        hin_f = hin_ref[...]
        hin_b = hin_f.astype(BF16)
        dyv = dy_ref[...]
        dst = dst_ref[...]
        dst_b = dst.astype(BF16)
        xs, bm, cm, xdt = v["xs"], v["bm"], v["cm"], v["xdt"]
        bm_b, cm_b = bm.astype(BF16), cm.astype(BF16)

        dd_e = jnp.sum(dyv * xs, axis=0, keepdims=True)
        dxs = dyv * v["d_e"]
        ch = _dot(cm_b, hin_b)
        dch = (dyv * v["ecs_e"]).astype(BF16)
        dcm = _dot_nt(dch, hin_b)
        dhin = _dot_tn(cm_b, dch) + dst * v["elast_e"]
        dcs_e = dyv * ch * v["ecs_e"]
        dxd = _dot(bm_b, dst_b)
        dbm = _dot_nt((xdt * v["decay_e"]).astype(BF16), dst_b)
        dxdt_state = dxd * v["decay_e"]
        q = dxdt_state * xdt
        dcs_e = dcs_e - q
        dlast_e = jnp.sum(q, axis=0, keepdims=True) + jnp.sum(dst * hin_f, axis=0, keepdims=True) * v["elast_e"]
        dg = jnp.zeros((T, T), F32)
        rs_cols = jnp.zeros((T, 128), F32)
        cs_rows = jnp.zeros((128, T), F32)
        lane_i, sub_i = _iota((T, 128), 1), _iota((128, T), 0)
        dxdt_parts = []
        for jp in range(4):
            l0, l1 = _ssd_decay_pair(v, jp)
            m0, m1 = v["gmat"] * l0, v["gmat"] * l1
            xbd = _block_diag_pair(xdt[:, 128 * jp:128 * jp + 128]).astype(BF16)
            dyp = dyv[:, 128 * jp:128 * jp + 128]
            dm = _dot_nt(dyp.astype(BF16), xbd)
            dm0, dm1 = dm[:, 0:T], dm[:, T:2 * T]
            dg = dg + dm0 * l0 + dm1 * l1
            for j, qq in ((v["head0"] + 2 * jp, dm0 * m0), (v["head0"] + 2 * jp + 1, dm1 * m1)):
                rs_cols = jnp.where(lane_i == j, jnp.sum(qq, axis=1, keepdims=True), rs_cols)
                cs_rows = jnp.where(sub_i == j, jnp.sum(qq, axis=0, keepdims=True), cs_rows)
            mv = jnp.concatenate([m0, m1], axis=0).astype(BF16)
            dxdt_parts.append(_dot_tn(mv, _block_diag_pair(dyp).astype(BF16)))
        dxdt = jnp.concatenate(dxdt_parts, axis=1) + dxdt_state
        dg_b = dg.astype(BF16)
        dcm = dcm + _dot(dg_b, bm_b)
        dbm = dbm + _dot_tn(dg_b, cm_b)
        expand_t = v["expand_t"]
        last_row = _iota((T, 1), 0) == T - 1
        dcs_full_e = dcs_e + jnp.where(last_row, dlast_e, 0.0)
        dcs = _xdot(dcs_full_e, expand_t) + (rs_cols - cs_rows.T)
        triu = (_iota((T, T), 0) <= _iota((T, T), 1)).astype(BF16)
        da = _xdot_l(triu, dcs)
        ddt = da * v["a_neg"] + _xdot(dxdt * xs, expand_t)
        dxs = dxs + dxdt * v["dt_e"]
        ddtr = jnp.where(v["valid"], ddt * _sigmoid(v["dtr"]), 0.0)
        dx_ref[...] = dxs
        db_ref[...] = jnp.where(v["valid"], dbm, 0.0)
        dc_ref[...] = jnp.where(v["valid"], dcm, 0.0)
        ddt_ref[...] = ddtr
        dpar = jnp.concatenate([
            jnp.sum(ddtr, axis=0, keepdims=True),
            jnp.sum(da * v["dt"], axis=0, keepdims=True) * v["a_neg"],
            _xdot(dd_e, expand_t),
            jnp.zeros((5, 128), F32)], axis=0)

        @pl.when(step == 0)
        def _():
            dpar_ref[...] = dpar

        @pl.when(step > 0)
        def _():
            dpar_ref[...] += dpar

        dst_ref[...] = dhin

    return _call(
        body, "ssd_bwd", (SSM_GROUPS // per, n_chunks), in_specs,
        [pl.BlockSpec((T, per * GROUP_W), lambda g, c: (cidx(c), g)),
         pl.BlockSpec((T, per * D_STATE), lambda g, c: (cidx(c), g)),
         pl.BlockSpec((T, per * D_STATE), lambda g, c: (cidx(c), g)),
         pl.BlockSpec((T, dt_w), lambda g, c: (cidx(c), P_DT // dt_w)),
         pl.BlockSpec((per, 8, 128), lambda g, c: (g, 0, 0))],
        [jax.ShapeDtypeStruct((n_rows, D_INNER), F32),
         jax.ShapeDtypeStruct((n_rows, SSM_GROUPS * D_STATE), F32),
         jax.ShapeDtypeStruct((n_rows, SSM_GROUPS * D_STATE), F32),
         jax.ShapeDtypeStruct(dp.shape, dp.dtype),
         jax.ShapeDtypeStruct((SSM_GROUPS, 8, 128), F32)],
        [p, p, p, p, p, p, p, conv_w, conv_w, conv_w, conv_b, conv_b, conv_b, dt_bias, a_log, d_skip, hin, dy, dp],
        scratch_shapes=[pltpu.VMEM((per, D_STATE, GROUP_W), F32), pltpu.VMEM((per, T + 8, GROUP_W), F32),
                        pltpu.VMEM((T, per * 128), F32)],
        aliases={18: 3}, bg=bg)


def _alibi_slope(h):
    return 2.0 ** (-8.0 * (h + 1) / ATTN_HEADS)


def _dup_half(x256, kvh):
    xb = x256[:, 128 * (kvh // 2):128 * (kvh // 2) + 128]
    rolled = pltpu.roll(xb, 64, 1)
    lane = _iota(xb.shape, 1)
    if kvh % 2 == 0:
        return jnp.where(lane < 64, xb, rolled)
    return jnp.where(lane < 64, rolled, xb)


def _attn_masks(c):
    qi, j = _iota((T, T), 0), _iota((T, T), 1)
    tri = j <= qi
    meta_ok = (j >= PAD) & (j - PAD <= c * T + qi - PAD)
    band_ok = c >= jnp.where(tri, 1, 2)
    dist = jnp.bitwise_and(qi - j, T - 1).astype(F32)
    return tri, meta_ok, band_ok, dist


def _fold(x3, tri):
    return jnp.concatenate([x3[:, 0:T], jnp.where(tri, x3[:, 2 * T:3 * T], x3[:, T:2 * T])], axis=1)


def _unfold(x2, tri):
    band = x2[:, T:2 * T]
    return jnp.concatenate([x2[:, 0:T], jnp.where(tri, 0.0, band), jnp.where(tri, band, 0.0)], axis=1)


def _attn_fwd(p, sinks, n_chunks, bg=None):
    n_rows = n_chunks * T
    kb, vb = P_K // KV_W, P_V // KV_W

    def body(q_ref, kc_ref, kp_ref, km_ref, vc_ref, vp_ref, vm_ref, sink_ref, o_ref, lse_ref):
        c = pl.program_id(0)
        sinks_v = sink_ref[...]
        masks = _attn_masks(c)
        tri, meta_ok, band_ok, dist = masks
        lane = _iota((T, 128), 1)
        for kvh in range(KV_HEADS):
            k3 = jnp.concatenate([_dup_half(r[...], kvh) for r in (km_ref, kp_ref, kc_ref)], axis=0).astype(BF16)
            v3 = jnp.concatenate([_dup_half(r[...], kvh) for r in (vm_ref, vp_ref, vc_ref)], axis=0)
            v3bd = _block_diag_rows(v3).astype(BF16)
            q2 = q_ref[:, 256 * kvh:256 * kvh + 256] * SCALE
            q4 = jnp.concatenate([jnp.where((lane < 64) if half == 0 else (lane >= 64), q2[:, 128 * pr:128 * pr + 128], 0.0)
                                  for pr in range(2) for half in range(2)], axis=0).astype(BF16)
            raw4 = _dot_nt(q4, k3)
            probs = []
            for hh in range(4):
                h = 4 * kvh + hh
                raw = raw4[T * hh:T * hh + T]
                band = jnp.where(tri, raw[:, 2 * T:3 * T], raw[:, T:2 * T]) - _alibi_slope(h) * dist
                sc = jnp.concatenate([jnp.where(meta_ok, raw[:, 0:T], NEG), jnp.where(band_ok, band, NEG)], axis=1)
                sink = sinks_v[:, h:h + 1]
                m = jnp.maximum(jnp.max(sc, axis=1, keepdims=True), sink)
                e = jnp.exp(sc - m)
                den = jnp.sum(e, axis=1, keepdims=True) + jnp.exp(sink - m)
                probs.append(_unfold(e * (1.0 / den), tri))
                lse_ref[:, h:h + 1] = m + jnp.log(den)
            p4 = jnp.concatenate([jnp.concatenate(probs[0:2], axis=1), jnp.concatenate(probs[2:4], axis=1)], axis=0)
            out = _dot(p4.astype(BF16), v3bd)
            o_ref[:, 256 * kvh:256 * kvh + 256] = jnp.concatenate([out[0:T], out[T:2 * T]], axis=1).astype(o_ref.dtype)

    blk = lambda width, col: pl.BlockSpec((T, width), lambda c: (c, col))
    prev = lambda width, col: pl.BlockSpec((T, width), lambda c: (jnp.maximum(c - 1, 0), col))
    first = lambda width, col: pl.BlockSpec((T, width), lambda c: (0, col))
    return _call(
        body, "attn_fwd", (n_chunks,),
        [blk(ATTN_W, P_Q // ATTN_W), blk(KV_W, kb), prev(KV_W, kb), first(KV_W, kb),
         blk(KV_W, vb), prev(KV_W, vb), first(KV_W, vb), pl.BlockSpec((1, 128), lambda c: (0, 0))],
        [pl.BlockSpec((T, ATTN_W), lambda c: (c, 0)), pl.BlockSpec((T, 128), lambda c: (c, 0))],
        [jax.ShapeDtypeStruct((n_rows, ATTN_W), BF16), jax.ShapeDtypeStruct((n_rows, 128), F32)],
        [p, p, p, p, p, p, p, sinks], bg=bg)


def _block_diag_rows(x3):
    lane = _iota(x3.shape, 1)
    return jnp.concatenate([jnp.where(lane < 64, x3, 0.0), jnp.where(lane >= 64, x3, 0.0)], axis=0)


def _fold_halves(x):
    return x + pltpu.roll(x, 64, 1)


def _attn_bwd(p, sinks, ao, lse, dao, dp, n_chunks, bg=None):
    kb, vb = P_K // KV_W, P_V // KV_W
    rc = lambda s: n_chunks - 1 - s

    def body(q_ref, kc_ref, kp_ref, km_ref, vc_ref, vp_ref, vm_ref, sink_ref, o_ref, lse_ref, do_ref, dp_in_ref,
             dqkv_ref, dsink_ref, kcar_ref, vcar_ref, kmeta_ref, vmeta_ref):
        step = pl.program_id(0)
        c = n_chunks - 1 - step

        @pl.when(step == 0)
        def _():
            for r in (kcar_ref, vcar_ref, kmeta_ref, vmeta_ref):
                r[...] = jnp.zeros_like(r)

        masks = _attn_masks(c)
        tri = masks[0]
        q = q_ref[...] * SCALE
        sinks_v = sink_ref[...]
        lse_v = lse_ref[...]
        ov = o_ref[...].astype(F32)
        dov = do_ref[...].astype(F32)
        lane = _iota((T, 128), 1)
        lane256 = _iota((3 * T, KV_W), 1)
        dsink = jnp.zeros((1, 128), F32)
        dk3_all = jnp.zeros((3 * T, KV_W), F32)
        dv3_all = jnp.zeros((3 * T, KV_W), F32)
        dqs = []
        for kvh in range(KV_HEADS):
            k3 = jnp.concatenate([_dup_half(r[...], kvh) for r in (km_ref, kp_ref, kc_ref)], axis=0).astype(BF16)
            v3 = jnp.concatenate([_dup_half(r[...], kvh) for r in (vm_ref, vp_ref, vc_ref)], axis=0).astype(BF16)
            halves = [(pr, half, (lane < 64) if half == 0 else (lane >= 64)) for pr in range(2) for half in range(2)]
            cols = [slice(128 * (2 * kvh + pr), 128 * (2 * kvh + pr) + 128) for pr in range(2)]
            q4 = jnp.concatenate([jnp.where(mine, q[:, cols[pr]], 0.0) for pr, _, mine in halves], axis=0).astype(BF16)
            do4 = jnp.concatenate([jnp.where(mine, dov[:, cols[pr]], 0.0) for pr, _, mine in halves], axis=0).astype(BF16)
            raw4 = _dot_nt(q4, k3)
            dp4 = _dot_nt(do4, v3)
            ds_rows, pm_rows = [], []
            for hh, (pr, half, mine) in enumerate(halves):
                h = 4 * kvh + hh
                raw = raw4[T * hh:T * hh + T]
                band = jnp.where(tri, raw[:, 2 * T:3 * T], raw[:, T:2 * T]) - _alibi_slope(h) * masks[3]
                sc = jnp.concatenate([jnp.where(masks[1], raw[:, 0:T], NEG), jnp.where(masks[2], band, NEG)], axis=1)
                lse_h = lse_v[:, h:h + 1]
                pm = jnp.exp(sc - lse_h)
                prod = dov[:, cols[pr]] * ov[:, cols[pr]]
                delta = jnp.sum(jnp.where(mine, prod, 0.0), axis=1, keepdims=True)
                dp = _fold(dp4[T * hh:T * hh + T], tri)
                ds_rows.append(_unfold(pm * (dp - delta), tri))
                pm_rows.append(_unfold(pm, tri))
                p_sink = jnp.exp(sinks_v[:, h:h + 1] - lse_h)
                dsink = jnp.where(_iota((1, 128), 1) == h, jnp.sum(-p_sink * delta, axis=0, keepdims=True), dsink)
            ds4 = jnp.concatenate(ds_rows, axis=0).astype(BF16)
            dq4 = _dot(ds4, k3)
            dk3 = _dot_tn(ds4, q4)
            dv3 = _dot_tn(jnp.concatenate(pm_rows, axis=0).astype(BF16), do4)
            for pr in range(2):
                dqs.append(jnp.where(lane < 64, dq4[2 * T * pr:2 * T * pr + T], dq4[2 * T * pr + T:2 * T * pr + 2 * T]) * SCALE)
            in_place = (lane256 >= 64 * kvh) & (lane256 < 64 * kvh + 64)
            wide = lambda x: jnp.concatenate([x, x], axis=1)
            dk3_all = jnp.where(in_place, wide(_fold_halves(dk3)), dk3_all)
            dv3_all = jnp.where(in_place, wide(_fold_halves(dv3)), dv3_all)
        dsink_all = dsink

        @pl.when(step == 0)
        def _():
            dsink_ref[...] = dsink_all

        @pl.when(step > 0)
        def _():
            dsink_ref[...] += dsink_all

        kmeta = kmeta_ref[...] + dk3_all[0:T]
        vmeta = vmeta_ref[...] + dv3_all[0:T]
        kmeta_ref[...] = kmeta
        vmeta_ref[...] = vmeta
        is_first = c == 0
        dk = jnp.where(is_first, kmeta, dk3_all[2 * T:3 * T] + kcar_ref[...])
        dv = jnp.where(is_first, vmeta, dv3_all[2 * T:3 * T] + vcar_ref[...])
        dqkv_ref[...] = jnp.concatenate(dqs + [dk, dv], axis=1).astype(dqkv_ref.dtype)
        kcar_ref[...] = dk3_all[T:2 * T]
        vcar_ref[...] = dv3_all[T:2 * T]

    blk = lambda width, col: pl.BlockSpec((T, width), lambda s: (rc(s), col))
    prev = lambda width, col: pl.BlockSpec((T, width), lambda s: (jnp.maximum(rc(s) - 1, 0), col))
    first = lambda width, col: pl.BlockSpec((T, width), lambda s: (0, col))
    return _call(
        body, "attn_bwd", (n_chunks,),
        [blk(ATTN_W, P_Q // ATTN_W), blk(KV_W, kb), prev(KV_W, kb), first(KV_W, kb),
         blk(KV_W, vb), prev(KV_W, vb), first(KV_W, vb), pl.BlockSpec((1, 128), lambda s: (0, 0)),
         blk(ATTN_W, 0), blk(128, 0), blk(ATTN_W, 0), ANY],
        [blk(QKV_W, P_Q // QKV_W), pl.BlockSpec((1, 128), lambda s: (0, 0))],
        [jax.ShapeDtypeStruct(dp.shape, dp.dtype), jax.ShapeDtypeStruct((1, 128), F32)],
        [p, p, p, p, p, p, p, sinks, ao, lse, dao, dp],
        scratch_shapes=[pltpu.VMEM((T, KV_W), F32)] * 4, aliases={11: 0}, bg=bg)


def _pad_lanes(v, width=128):
    return jnp.pad(v, ((0, 0), (0, width - v.shape[1])))


def _local_step(x, head, tgt, plan):
    w, g, run = plan.w, plan.g, plan.run
    n_tok = x.shape[0]
    n_rows = n_tok + T
    n_chunks = n_rows // T
    tm = _row_tile(n_rows, 384)
    dt_bias, a_log, d_skip = (_pad_lanes(w[k]) for k in ("ssm_dt_bias", "ssm_a_log", "ssm_d_skip"))
    sinks = _pad_lanes(w["attn_sinks"])
    x_in = [(x, D_MODEL, 0, "prev"), (head, D_MODEL, 0, "first")]

    def h0_tile(r0, xt, hd):
        return jnp.where(r0 < T, hd, xt)

    n1, = _rowwise("norm_pre_mix", lambda r0, xt, hd, wn: [_rms(h0_tile(r0, xt, hd), wn)], n_rows, T,
                   x_in, [w["norm_pre_mix"]], [(D_MODEL, BF16)], [])
    p = _matmul("in_proj", n1, w["w_cat"], "nn", F32)
    y_ssd, hin = run("ssd_fwd", _ssd_fwd, p, w["ssm_conv_w"], w["ssm_conv_b"], dt_bias, a_log, d_skip, n_chunks)
    ao, lse = run("attn_fwd", _attn_fwd, p, sinks, n_chunks)

    def gate_norm(r0, y, z, wn):
        return [_rms(y * _silu(z), wn)]

    yn, = run("ssm_gate_norm", _rowwise, "ssm_gate_norm", gate_norm, n_rows, tm,
              [(y_ssd, D_INNER, 0), (p, D_INNER, P_Z // D_INNER)], [w["ssm_norm"]], [(D_INNER, BF16)], [])
    y_ssm = _matmul("ssm_out", yn, w["w_ssm_out"], "nn", F32)
    y_attn = _matmul("attn_out", ao, w["w_attn_out"], "nn", F32)

    def mix_gate(r0, ys, ya, gs, ga):
        return [_sigmoid(gs) * ys + _sigmoid(ga) * ya]

    gate_ins = [(p, D_MODEL, P_GATE // D_MODEL), (p, D_MODEL, P_GATE // D_MODEL + 1)]
    mixed, = _rowwise("mix_gate", mix_gate, n_rows, tm, [(y_ssm, D_MODEL, 0), (y_attn, D_MODEL, 0)] + gate_ins,
                      [], [(D_MODEL, BF16)], [])
    mix = _matmul("mix_out", mixed, w["w_mix_out"], "nn", F32)

    def post_mix(r0, mx, xt, hd, w_post, w_pre):
        h1 = jnp.where(_valid_rows(r0, mx.shape[0], PAD), h0_tile(r0, xt, hd) + _rms(mx, w_post), 0.0)
        return [h1, _rms(h1, w_pre)]

    h1, n2 = _rowwise("post_mix", post_mix, n_rows, T, [(mix, D_MODEL, 0)] + x_in,
                      [w["norm_post_mix"], w["norm_pre_ffn"]], [(D_MODEL, F32), (D_MODEL, BF16)], [])
    u_raw = _matmul("ffn_up", n2, w["w_ffn_up"], "nn", F32)
    f = _ffn_act("ffn_act", u_raw, w["ffn_conv_w"], w["ffn_conv_b"], n_rows)
    ffn = _matmul("ffn_down", f, w["w_ffn_down"], "nn", F32)

    def final(r0, fo, h, t, w_post):
        real = r0 >= T
        err = jnp.where(real, h + _rms(fo, w_post) - t, 0.0)
        dy = err * (1.0 / D_MODEL)
        dffn, dw = _rms_bwd(dy, fo, w_post)
        return [dffn, dy, jnp.sum(err * err, axis=0, keepdims=True), dw]

    dffn, dh2, loss_cols, g_norm_post_ffn = _rowwise(
        "loss_head", final, n_rows, T, [(ffn, D_MODEL, 0), (h1, D_MODEL, 0), (tgt, D_MODEL, 0, "prev")],
        [w["norm_post_ffn"]], [(D_MODEL, BF16), (D_MODEL, F32)], [D_MODEL, D_MODEL])

    g["norm_post_ffn"] = g_norm_post_ffn
    g["w_ffn_down"] = _matmul("ffn_down_dw", f, dffn, "tn", F32)
    df = _matmul("ffn_down_dx", dffn, w["w_ffn_down"], "nt", F32)
    du_raw, dconv = _conv_bwd("ffn_act_bwd", u_raw, 0, [df], [(0, c0) for c0 in range(0, FFN_DIM, CONV_LANES)],
                              w["ffn_conv_w"], w["ffn_conv_b"], n_rows, True)
    g["ffn_conv_w"], g["ffn_conv_b"] = dconv[0:3], dconv[3:4]
    g["w_ffn_up"] = _matmul("ffn_up_dw", n2, du_raw, "tn", F32)
    dn2 = run("ffn_up_dx", _matmul, "ffn_up_dx", du_raw, w["w_ffn_up"], "nt", F32)

    def post_mix_bwd(r0, dn, d2, h, mx, w_pre, w_post):
        dx, dw_pre = _rms_bwd(dn, h, w_pre)
        dh1 = jnp.where(_valid_rows(r0, dn.shape[0], PAD), dx + d2, 0.0)
        dmix, dw_post = _rms_bwd(dh1, mx, w_post)
        return [dh1, dmix, dw_pre, dw_post]

    dh1, dmix, g["norm_pre_ffn"], g["norm_post_mix"] = _rowwise(
        "post_mix_bwd", post_mix_bwd, n_rows, tm,
        [(dn2, D_MODEL, 0), (dh2, D_MODEL, 0), (h1, D_MODEL, 0), (mix, D_MODEL, 0)],
        [w["norm_pre_ffn"], w["norm_post_mix"]], [(D_MODEL, F32), (D_MODEL, BF16)], [D_MODEL, D_MODEL])
    g["w_mix_out"] = _matmul("mix_out_dw", mixed, dmix, "tn", F32)
    dmixed = _matmul("mix_out_dx", dmix, w["w_mix_out"], "nt", F32)

    def mix_gate_bwd(r0, dm, ys, ya, gs, ga):
        ss, sa = _sigmoid(gs), _sigmoid(ga)
        dgate = jnp.concatenate([dm * ys * ss * (1.0 - ss), dm * ya * sa * (1.0 - sa)], axis=1)
        return [dm * ss, dm * sa, dgate]

    dys, dya, dp = _rowwise(
        "mix_gate_bwd", mix_gate_bwd, n_rows, tm,
        [(dmixed, D_MODEL, 0), (y_ssm, D_MODEL, 0), (y_attn, D_MODEL, 0)] + gate_ins,
        [], [(D_MODEL, BF16), (D_MODEL, BF16), (2 * D_MODEL, BF16, "new", P_W, P_GATE // (2 * D_MODEL))], [])
    g["w_ssm_out"] = _matmul("ssm_out_dw", yn, dys, "tn", F32)
    dyn = _matmul("ssm_out_dx", dys, w["w_ssm_out"], "nt", F32)
    g["w_attn_out"] = _matmul("attn_out_dw", ao, dya, "tn", F32)
    dao = _matmul("attn_out_dx", dya, w["w_attn_out"], "nt", BF16)

    def gate_norm_bwd(r0, dn, y, z, wn):
        sz, dsz = _silu_grad(z)
        dyz, dw = _rms_bwd(dn, y * sz, wn)
        live = _valid_rows(r0, dn.shape[0], PAD)
        return [jnp.where(live, dyz * sz, 0.0), jnp.where(live, dyz * y * dsz, 0.0), dw]

    dy_ssd, dp, g["ssm_norm"] = run(
        "ssm_gate_norm_bwd", _rowwise, "ssm_gate_norm_bwd", gate_norm_bwd, n_rows, tm,
        [(dyn, D_INNER, 0), (y_ssd, D_INNER, 0), (p, D_INNER, P_Z // D_INNER)],
        [w["ssm_norm"]], [(D_INNER, F32), (D_INNER, BF16, "into", dp, P_Z // D_INNER)], [D_INNER])
    dp, dsink = run("attn_bwd", _attn_bwd, p, sinks, ao, lse, dao, dp, n_chunks)
    g["attn_sinks"] = dsink[:, 0:ATTN_HEADS]
    dxs, dbm, dcm, dp, dpar = run("ssd_bwd", _ssd_bwd, p, w["ssm_conv_w"], w["ssm_conv_b"], dt_bias, a_log,
                                  d_skip, hin, dy_ssd, dp, n_chunks)
    dpar = jnp.sum(dpar, axis=0)
    g["ssm_dt_bias"], g["ssm_a_log"], g["ssm_d_skip"] = (dpar[i:i + 1, 0:SSM_HEADS] for i in range(3))
    x_chunks = [(src, c0) for src, arr in enumerate((dxs, dbm, dcm)) for c0 in range(0, arr.shape[1], CONV_LANES)]
    dp, dconv = run("ssm_conv_bwd", _conv_bwd, "ssm_conv_bwd", p, P_XBC // CONV_DIM, [dxs, dbm, dcm], x_chunks,
                    w["ssm_conv_w"], w["ssm_conv_b"], n_rows, False, into=dp, into_blk=P_XBC // CONV_DIM)
    g["ssm_conv_w"], g["ssm_conv_b"] = dconv[0:4], dconv[4:5]
    g["w_cat_t"] = _matmul("in_proj_dw", dp, n1, "tn", F32)
    dn1 = run("in_proj_dx", _matmul, "in_proj_dx", dp, w["w_cat"], "nt", F32)

    def pre_mix_bwd(r0, dn, d1, xt, hd, wn):
        dx, dw = _rms_bwd(dn, h0_tile(r0, xt, hd), wn)
        dh0 = jnp.where(_valid_rows(r0, dn.shape[0], PAD), dx + d1, 0.0)
        return [dh0, dh0, dw]

    dx_out, dhead, g["norm_pre_mix"] = _rowwise(
        "pre_mix_bwd", pre_mix_bwd, n_rows, T, [(dn1, D_MODEL, 0), (dh1, D_MODEL, 0)] + x_in,
        [w["norm_pre_mix"]], [(D_MODEL, F32, "prev", n_tok), (D_MODEL, F32, "first")], [D_MODEL])
    return jnp.sum(loss_cols), dx_out, dhead


_IN_SECTIONS = [((5152, 6176), P_Q), ((6176, 6432), P_K), ((6432, 6688), P_V), ((5120, 5152), P_DT),
                ((0, 2048), P_Z), ((6688, 8736), P_GATE), ((2048, 5120), P_XBC)]


IN_SHARD = N_IN // 4


def _shard_pieces(a, b):
    return [(j, max(a, j * IN_SHARD) - j * IN_SHARD, min(b, (j + 1) * IN_SHARD) - j * IN_SHARD)
            for j in range(4) if max(a, j * IN_SHARD) < min(b, (j + 1) * IN_SHARD)]


def _to_cat(w4):
    parts, at = [], 0
    for (a, b), off in _IN_SECTIONS:
        if off > at:
            parts.append(jnp.zeros((w4.shape[1], off - at), w4.dtype))
        parts += [w4[j, :, lo:hi] for j, lo, hi in _shard_pieces(a, b)]
        at = off + (b - a)
    return jnp.concatenate(parts, axis=1)


def _from_cat_t(g_cat_t):
    shards = [[] for _ in range(4)]
    for (a, b), off in sorted(_IN_SECTIONS):
        for j, lo, hi in _shard_pieces(a, b):
            start = off + j * IN_SHARD + lo - a
            shards[j].append(g_cat_t[start:start + hi - lo])
    return jnp.stack([jnp.concatenate(s, axis=0) for s in shards])


LANES = 1024
_BIG = [("w_in", 1024, 2184, "chip"), ("w_ssm_out", 512, 1024, "row"), ("w_attn_out", 256, 1024, "row"),
        ("w_mix_out", 256, 1024, "row"), ("w_ffn_up", 1024, 1408, "col"), ("w_ffn_down", 704, 1024, "row"),
        ("small", 32, LANES, "chip")]
_SMALL_SHARDED = [("ssm_conv_w", (4, 768), 1), ("ffn_conv_w", (3, 1408), 1), ("meta_tokens", (16, 256), 1)]
_REPLICATED = [("norm_pre_mix", 1024), ("ssm_conv_b", 3072), ("ssm_dt_bias", 32), ("ssm_a_log", 32),
               ("ssm_d_skip", 32), ("ssm_norm", 2048), ("attn_sinks", 16), ("norm_post_mix", 1024),
               ("norm_pre_ffn", 1024), ("ffn_conv_b", 5632), ("norm_post_ffn", 1024)]
SMALL_ROWS = 24


def _rep_rows():
    out, at = [], 0
    for _, width in _REPLICATED:
        out.append((at, -(-width // LANES)))
        at += out[-1][1]
    return out, at


def _in_rows(parts):
    rows = [jnp.pad(a, ((0, 0), (0, -a.shape[1] % LANES))).reshape(-1, LANES) for a in parts]
    flat = jnp.concatenate(rows, axis=0)
    return jnp.pad(flat, ((0, SMALL_ROWS - flat.shape[0]), (0, 0)))
WEIGHT_ORDER = ["meta_tokens", "norm_pre_mix", "w_in", "ssm_conv_w", "ssm_conv_b", "ssm_dt_bias", "ssm_a_log",
                "ssm_d_skip", "ssm_norm", "w_ssm_out", "attn_sinks", "w_attn_out", "w_mix_out", "norm_post_mix",
                "norm_pre_ffn", "w_ffn_up", "ffn_conv_w", "ffn_conv_b", "w_ffn_down", "norm_post_ffn"]


def _flatten(parts, rows):
    flat = jnp.concatenate([a.reshape(-1) for a in parts])
    return jnp.pad(flat, (0, rows * LANES - flat.shape[0])).reshape(rows, LANES)


def _unflatten(flat, shapes):
    flat = flat.reshape(-1)
    out, off = [], 0
    for shp in shapes:
        n = math.prod(shp)
        out.append(flat[off:off + n].reshape(shp))
        off += n
    return out


def _shard_of(full, chip, shape, axis):
    return lax.slice_in_dim(full, chip * shape[axis], (chip + 1) * shape[axis], axis=axis)


def _full_shape(r, c, layout):
    return {"row": (4 * r, c), "col": (r, 4 * c), "chip": (4, r, c), "chip_cols": (4, r, c)}[layout]


def _half_shape(r, c, layout):
    return (r, c // 2) if layout == "chip_cols" else (r // 2, c)


def _shard_view(ref, r, c, layout, chip):
    if layout == "row":
        return ref.at[pl.ds(pl.multiple_of(chip * r, 16), r), :]
    if layout == "col":
        return ref.at[:, pl.ds(pl.multiple_of(chip * c, 128), c)]
    return ref.at[chip]


def _half_view(ref, r, c, layout, chip, half):
    if layout == "chip_cols":
        return ref.at[chip, :, pl.ds(pl.multiple_of(half * (c // 2), 128), c // 2)]
    hr = r // 2
    if layout == "row":
        return ref.at[pl.ds(pl.multiple_of(chip * r + half * hr, 16), hr), :]
    r0 = pl.multiple_of(half * hr, 16)
    if layout == "col":
        return ref.at[pl.ds(r0, hr), pl.ds(pl.multiple_of(chip * c, 128), c)]
    return ref.at[chip, pl.ds(r0, hr), :]


def _mesh_pos():
    return lax.axis_index("x"), lax.axis_index("y"), lax.axis_index("c")


def _other_chips(x, y):
    return [(1 - x, y), (x, 1 - y), (1 - x, 1 - y)]


def _chip_index(x, y):
    return 2 * x + y


def _run_exchange(name, ex):
    n_in, n_out = len(ex.ins), len(ex.out_shapes)

    def body(*refs):
        in_refs, out_refs = refs[:n_in], refs[n_in:n_in + n_out]
        send_sems, recv_sems = refs[n_in + n_out:]
        copies = [pltpu.make_async_remote_copy(src_ref=s, dst_ref=d, send_sem=send_sems.at[i], recv_sem=recv_sems.at[i],
                                               device_id=dev, device_id_type=MESH)
                  for i, (s, d, dev) in enumerate(ex.make_copies(in_refs, out_refs))]
        assert len(copies) == ex.n_copies
        for cp in copies:
            cp.start()
        for cp in copies:
            cp.wait()

    return pl.pallas_call(
        body, name=name, in_specs=[ANY] * n_in, out_specs=[ANY] * n_out, out_shape=list(ex.out_shapes),
        scratch_shapes=[pltpu.SemaphoreType.DMA((ex.n_copies,)), pltpu.SemaphoreType.DMA((ex.n_copies,))],
        compiler_params=pltpu.CompilerParams(has_side_effects=True),
    )(*ex.ins)


def _join(*exs):
    def make(in_refs, out_refs):
        copies, i0, o0 = [], 0, 0
        for ex in exs:
            copies += ex.make_copies(in_refs[i0:i0 + len(ex.ins)], out_refs[o0:o0 + len(ex.out_shapes)])
            i0, o0 = i0 + len(ex.ins), o0 + len(ex.out_shapes)
        return copies

    aliases, i0, o0 = {}, 0, 0
    for ex in exs:
        aliases.update({i0 + k: o0 + v for k, v in ex.aliases.items()})
        i0, o0 = i0 + len(ex.ins), o0 + len(ex.out_shapes)
    return _Exchange([a for ex in exs for a in ex.ins], [s for ex in exs for s in ex.out_shapes], make,
                     sum(ex.n_copies for ex in exs), aliases)


def _split(exs, results):
    out, o0 = [], 0
    for ex in exs:
        out.append(list(results[o0:o0 + len(ex.out_shapes)]))
        o0 += len(ex.out_shapes)
    return out


def _gather_ici(entries, shards):
    def make(in_refs, out_refs):
        x, y, c = _mesh_pos()
        j = _chip_index(x, y)
        copies = []
        for ref_in, ref_out, (_, r, cc, lay) in zip(in_refs, out_refs, entries):
            copies.append((ref_in, _shard_view(ref_out, r, cc, lay, j), None))
            mine = ref_in.at[pl.ds(pl.multiple_of(c * (r // 2), 16), r // 2), :]
            copies += [(mine, _half_view(ref_out, r, cc, lay, j, c), (*ch, c)) for ch in _other_chips(x, y)]
        return copies

    shapes = [jax.ShapeDtypeStruct(_full_shape(r, cc, lay), s.dtype) for s, (_, r, cc, lay) in zip(shards, entries)]
    return _Exchange(list(shards), shapes, make, 4 * len(entries))


def _gather_pass_on(entries, fulls):
    def make(in_refs, out_refs):
        x, y, c = _mesh_pos()
        copies = []
        for ref, (_, r, cc, lay) in zip(out_refs, entries):
            for ch in _other_chips(x, y):
                landed = _half_view(ref, r, cc, lay, _chip_index(*ch), c)
                copies.append((landed, landed, (x, y, 1 - c)))
        return copies

    return _Exchange(list(fulls), [jax.ShapeDtypeStruct(f.shape, f.dtype) for f in fulls], make, 3 * len(entries),
                     {a: a for a in range(len(entries))})


def _gather_weights(entries, shards):
    n = len(entries)

    def body(*refs):
        ins, outs = refs[:n], refs[n:2 * n]
        send_sems, recv_sems, local_sems = refs[2 * n:]
        x, y, c = _mesh_pos()
        j = _chip_index(x, y)
        sibling = (x, y, 1 - c)
        chips = _other_chips(x, y)
        idx = [_chip_index(*ch) for ch in chips]

        def remote(k, src, dst, dev):
            return pltpu.make_async_remote_copy(src_ref=src, dst_ref=dst, send_sem=send_sems.at[k],
                                                recv_sem=recv_sems.at[k], device_id=dev, device_id_type=MESH)

        own = [pltpu.make_async_copy(ins[a], _shard_view(outs[a], r, cc, lay, j), local_sems.at[a])
               for a, (_, r, cc, lay) in enumerate(entries)]
        for cp in own:
            cp.start()
        first, passed = [], []
        for a, (_, r, cc, lay) in enumerate(entries):
            mine = ins[a].at[pl.ds(pl.multiple_of(c * (r // 2), 16), r // 2), :]
            for k, ch in enumerate(chips):
                first.append(remote(6 * a + k, mine, _half_view(outs[a], r, cc, lay, j, c), (*ch, c)))
                landed = _half_view(outs[a], r, cc, lay, idx[k], c)
                passed.append(remote(6 * a + 3 + k, landed, landed, sibling))
        for cp in first:
            cp.start()
        for a, (_, r, cc, lay) in enumerate(entries):
            for k in range(3):
                landed = _half_view(outs[a], r, cc, lay, idx[k], c)
                remote(6 * a + k, landed, landed, sibling).wait_recv()
                passed[3 * a + k].start()
        for a, (_, r, cc, lay) in enumerate(entries):
            for k in range(3):
                theirs = _half_view(outs[a], r, cc, lay, idx[k], 1 - c)
                remote(6 * a + 3 + k, theirs, theirs, sibling).wait_recv()
        for cp in first + passed:
            cp.wait_send()
        for cp in own:
            cp.wait()

    return pl.pallas_call(
        body, name="gather_weights", in_specs=[ANY] * n, out_specs=[ANY] * n,
        out_shape=[jax.ShapeDtypeStruct(_full_shape(r, cc, lay), s.dtype) for s, (_, r, cc, lay) in zip(shards, entries)],
        scratch_shapes=[pltpu.SemaphoreType.DMA((6 * n,)), pltpu.SemaphoreType.DMA((6 * n,)), pltpu.SemaphoreType.DMA((n,))],
        compiler_params=pltpu.CompilerParams(has_side_effects=True),
    )(*shards)


def _pair_exchange(entries, grads):
    def make(in_refs, out_refs):
        x, y, c = _mesh_pos()
        return [(_half_view(ref_in, r, cc, lay, i, 1 - c), ref_out.at[i], (x, y, 1 - c))
                for ref_in, ref_out, (_, r, cc, lay) in zip(in_refs, out_refs, entries) for i in range(4)]

    return _Exchange(list(grads), [jax.ShapeDtypeStruct((4,) + _half_shape(r, cc, lay), F32) for _, r, cc, lay in entries],
                     make, 4 * len(entries))


def _whole_to_sibling(arrays):
    def make(in_refs, out_refs):
        x, y, c = _mesh_pos()
        return [(r, o, (x, y, 1 - c)) for r, o in zip(in_refs, out_refs)]

    return _Exchange(list(arrays), [jax.ShapeDtypeStruct(a.shape, a.dtype) for a in arrays], make, len(arrays))


def _chip_exchange(psends):
    def make(in_refs, out_refs):
        x, y, c = _mesh_pos()
        return [(ref_in.at[_chip_index(*ch)], ref_out.at[k], (*ch, c))
                for ref_in, ref_out in zip(in_refs, out_refs) for k, ch in enumerate(_other_chips(x, y))]

    return _Exchange(list(psends), [jax.ShapeDtypeStruct((3,) + p.shape[1:], p.dtype) for p in psends], make,
                     3 * len(psends))


def _to_all_chips(array):
    def make(in_refs, out_refs):
        x, y, c = _mesh_pos()
        return [(in_refs[0], out_refs[0].at[k], (*ch, c)) for k, ch in enumerate(_other_chips(x, y))]

    return _Exchange([array], [jax.ShapeDtypeStruct((3,) + array.shape, array.dtype)], make, 3)


SUM_ROWS = 256
ADAM_ROWS = 128


def _pair_sum(name, grad, recv, ids, r, c, layout):
    hr, c = _half_shape(r, c, layout)
    tr = _row_tile(hr, SUM_ROWS)
    nb = hr // tr

    def body(ids_ref, g_ref, r_ref, send_ref, own_ref):
        s = g_ref[...] + r_ref[...]
        send_ref[...] = s.astype(send_ref.dtype)

        @pl.when(pl.program_id(1) == ids_ref[1])
        def _():
            own_ref[...] = s

    if layout == "row":
        g_spec = pl.BlockSpec((tr, c), lambda t, j, ids_ref: ((j * r + ids_ref[0] * hr) // tr + t, 0))
    elif layout == "col":
        g_spec = pl.BlockSpec((tr, c), lambda t, j, ids_ref: (ids_ref[0] * nb + t, j))
    elif layout == "chip_cols":
        g_spec = pl.BlockSpec((None, tr, c), lambda t, j, ids_ref: (j, t, ids_ref[0]))
    else:
        g_spec = pl.BlockSpec((None, tr, c), lambda t, j, ids_ref: (j, ids_ref[0] * nb + t, 0))
    grid_spec = pltpu.PrefetchScalarGridSpec(
        num_scalar_prefetch=1, grid=(nb, 4),
        in_specs=[g_spec, pl.BlockSpec((None, tr, c), lambda t, j, ids_ref: (j, t, 0))],
        out_specs=[pl.BlockSpec((None, tr, c), lambda t, j, ids_ref: (j, t, 0)),
                   pl.BlockSpec((tr, c), lambda t, j, ids_ref: (t, 0))])
    return pl.pallas_call(
        body, name=name, grid_spec=grid_spec,
        out_shape=[jax.ShapeDtypeStruct((4, hr, c), BF16), jax.ShapeDtypeStruct((hr, c), F32)],
        compiler_params=_cparams(2),
    )(ids, grad, recv)


def _chip_sum(name, own, recv):
    hr, c = own.shape
    tr = _row_tile(hr, SUM_ROWS)

    def body(o_ref, r_ref, out_ref):
        out_ref[...] = ((o_ref[...] + r_ref[0].astype(F32)) + r_ref[1].astype(F32)) + r_ref[2].astype(F32)

    return pl.pallas_call(
        body, name=name, grid=(hr // tr,),
        in_specs=[pl.BlockSpec((tr, c), lambda i: (i, 0)), pl.BlockSpec((3, tr, c), lambda i: (0, i, 0))],
        out_specs=pl.BlockSpec((tr, c), lambda i: (i, 0)),
        out_shape=jax.ShapeDtypeStruct((hr, c), F32), compiler_params=_cparams(1),
    )(own, recv)


def _chip_sum_small(own, recv, ids):
    def body(ids_ref, o_ref, r_ref, out_ref):
        j = ids_ref[1]
        total = None
        for i in range(4):
            m = jnp.bitwise_xor(i, j)
            term = jnp.where(m == 0, o_ref[...], jnp.where(m == 2, r_ref[0], jnp.where(m == 1, r_ref[1], r_ref[2])))
            total = term if total is None else total + term
        out_ref[...] = total

    grid_spec = pltpu.PrefetchScalarGridSpec(
        num_scalar_prefetch=1, grid=(1,),
        in_specs=[pl.BlockSpec(own.shape, lambda i, ids_ref: (0, 0)), pl.BlockSpec(recv.shape, lambda i, ids_ref: (0, 0, 0))],
        out_specs=pl.BlockSpec(own.shape, lambda i, ids_ref: (0, 0)))
    return pl.pallas_call(body, name="chip_sum_small", grid_spec=grid_spec,
                          out_shape=jax.ShapeDtypeStruct(own.shape, F32), compiler_params=_cparams(1))(ids, own, recv)


def _adamw(name, w, m, v, mine, theirs, ids):
    lead = (None,) * (w.ndim - 2)
    rows, cols = w.shape[-2:]
    half = rows // 2
    tr = _row_tile(half, ADAM_ROWS, unit=8)
    nb = half // tr
    c1 = 1.0 / (1.0 - ADAM_B1 ** ADAM_STEP)
    c2 = 1.0 / (1.0 - ADAM_B2 ** ADAM_STEP)

    def body(ids_ref, w_ref, m_ref, v_ref, mine_ref, theirs_ref, g_out, d_out, m_out, v_out):
        g = jnp.where(pl.program_id(0) == ids_ref[0], mine_ref[...], theirs_ref[...])
        m_new = ADAM_B1 * m_ref[...] + (1.0 - ADAM_B1) * g
        v_new = ADAM_B2 * v_ref[...] + (1.0 - ADAM_B2) * (g * g)
        d_out[...] = -ADAM_LR * ((m_new * c1) / (jnp.sqrt(v_new * c2) + ADAM_EPS) + ADAM_WD * w_ref[...])
        g_out[...] = g
        m_out[...] = m_new
        v_out[...] = v_new

    full = pl.BlockSpec(lead + (tr, cols), lambda h, i, ids_ref: (0,) * len(lead) + (h * nb + i, 0))
    part = pl.BlockSpec((tr, cols), lambda h, i, ids_ref: (i, 0))
    grid_spec = pltpu.PrefetchScalarGridSpec(num_scalar_prefetch=1, grid=(2, nb),
                                             in_specs=[full, full, full, part, part], out_specs=[full] * 4)
    return pl.pallas_call(
        body, name=name, grid_spec=grid_spec,
        out_shape=[jax.ShapeDtypeStruct(w.shape, F32)] * 4, compiler_params=_cparams(2),
    )(ids, w, m, v, mine, theirs)


def _adamw_whole(name, w, m, v, g):
    rows, cols = w.shape[-2:]
    tr = _row_tile(rows, 2 * ADAM_ROWS, unit=8)
    c1 = 1.0 / (1.0 - ADAM_B1 ** ADAM_STEP)
    c2 = 1.0 / (1.0 - ADAM_B2 ** ADAM_STEP)

    def body(w_ref, m_ref, v_ref, g_ref, g_out, d_out, m_out, v_out):
        g = g_ref[...]
        m_new = ADAM_B1 * m_ref[...] + (1.0 - ADAM_B1) * g
        v_new = ADAM_B2 * v_ref[...] + (1.0 - ADAM_B2) * (g * g)
        d_out[...] = -ADAM_LR * ((m_new * c1) / (jnp.sqrt(v_new * c2) + ADAM_EPS) + ADAM_WD * w_ref[...])
        g_out[...] = g
        m_out[...] = m_new
        v_out[...] = v_new

    full = pl.BlockSpec((None, tr, cols), lambda i: (0, i, 0))
    return pl.pallas_call(
        body, name=name, grid=(rows // tr,), in_specs=[full, full, full, pl.BlockSpec((tr, cols), lambda i: (i, 0))],
        out_specs=[full] * 4, out_shape=[jax.ShapeDtypeStruct(w.shape, F32)] * 4, compiler_params=_cparams(1),
    )(w, m, v, g)


def _adamw_replicated(g_rows, ws, ms, vs):
    n = len(ws)
    layout, _ = _rep_rows()
    c1 = 1.0 / (1.0 - ADAM_B1 ** ADAM_STEP)
    c2 = 1.0 / (1.0 - ADAM_B2 ** ADAM_STEP)

    def body(g_ref, *refs):
        w_refs, m_refs, v_refs = refs[0:n], refs[n:2 * n], refs[2 * n:3 * n]
        outs = refs[3 * n:]
        for k, (r0, rows) in enumerate(layout):
            width = w_refs[k].shape[1]
            g = jnp.concatenate([g_ref[r0 + j:r0 + j + 1, :] for j in range(rows)], axis=1)[:, 0:width]
            m_new = ADAM_B1 * m_refs[k][...] + (1.0 - ADAM_B1) * g
            v_new = ADAM_B2 * v_refs[k][...] + (1.0 - ADAM_B2) * (g * g)
            outs[k][...] = g
            outs[n + k][...] = -ADAM_LR * ((m_new * c1) / (jnp.sqrt(v_new * c2) + ADAM_EPS) + ADAM_WD * w_refs[k][...])
            outs[2 * n + k][...] = m_new
            outs[3 * n + k][...] = v_new

    res = pl.pallas_call(body, name="adamw_replicated",
                         out_shape=[jax.ShapeDtypeStruct(w.shape, F32) for _ in range(4) for w in ws])(g_rows, *ws, *ms, *vs)
    return [res[k * n:(k + 1) * n] for k in range(4)]


def _small_shard(parts):
    return _flatten(parts, _BIG[-1][1])


_ENTRY = {e[0]: e for e in _BIG}
_GRAD_ENTRY = {**_ENTRY, "w_in": ("w_in", IN_SHARD, D_MODEL, "chip_cols")}
FFN_MATS = ("w_ffn_down", "w_ffn_up")
MIXER_MATS = ("w_mix_out", "w_ssm_out", "w_attn_out")


class _StepPlan:
    def __init__(self, w, late_shards, shards, ids):
        self.w, self.g = w, {}
        self.late_shards, self.shards, self.ids = late_shards, shards, ids
        self.sums, self.halves, self.results = {}, {}, {}

    def run(self, name, fn, *args, **kw):
        at = getattr(self, "_at_" + name, None)
        if at is None:
            return fn(*args, **kw)
        exchange, landed = at()
        res, extra = fn(*args, bg=exchange, **kw)
        landed(extra)
        return res

    def _at_ssd_fwd(self):
        def landed(fulls):
            self.partly_gathered = fulls

        return _gather_ici([_ENTRY[n] for n in MIXER_MATS], [self.late_shards[n] for n in MIXER_MATS]), landed

    def _at_attn_fwd(self):
        stages = (_gather_pass_on([_ENTRY[n] for n in MIXER_MATS], self.partly_gathered),
                  _gather_ici([_ENTRY[n] for n in FFN_MATS], [self.late_shards[n] for n in FFN_MATS]))

        def landed(extra):
            mixer, self.partly_gathered = _split(stages, extra)
            self.w.update(zip(MIXER_MATS, mixer))

        return _join(*stages), landed

    def _at_ssm_gate_norm(self):
        return (_gather_pass_on([_ENTRY[n] for n in FFN_MATS], self.partly_gathered),
                lambda fulls: self.w.update(zip(FFN_MATS, fulls)))

    def pair_sums(self, names, grads, recv):
        for n, gr, rv in zip(names, grads, recv):
            _, r, c, lay = _GRAD_ENTRY[n]
            self.sums[n] = _pair_sum("pair_sum_" + n, gr, rv, self.ids, r, c, lay)

    def chip_sums(self, names, recv):
        for n, rv in zip(names, recv):
            self.halves[n] = _chip_sum("chip_sum_" + n, self.sums[n][1], rv)

    def adamw(self, names, theirs):
        for n, th in zip(names, theirs):
            sh = self.shards[n]
            if n == "w_in":
                mine_first = self.ids[0] == 0
                g_t = jnp.where(mine_first, jnp.concatenate([self.halves[n], th], axis=1),
                                jnp.concatenate([th, self.halves[n]], axis=1))
                res = _adamw_whole("adamw_" + n, *[jnp.swapaxes(sh[k], -1, -2) for k in ("w", "m", "v")], g_t)
                self.results[n] = [jnp.swapaxes(r, -1, -2) for r in res]
            else:
                self.results[n] = _adamw("adamw_" + n, sh["w"], sh["m"], sh["v"], self.halves[n], th, self.ids)

    def _pair_stage(self, names, grads):
        return (_pair_exchange([_GRAD_ENTRY[n] for n in names], grads),
                lambda recv: self.pair_sums(names, grads, recv))

    def _at_ffn_up_dx(self):
        return self._pair_stage(FFN_MATS, [self.g[n] for n in FFN_MATS])

    def _at_ssm_gate_norm_bwd(self):
        return self._pair_stage(MIXER_MATS, [self.g[n] for n in MIXER_MATS])

    def _at_attn_bwd(self):
        return _chip_exchange([self.sums[n][0] for n in FFN_MATS]), lambda recv: self.chip_sums(FFN_MATS, recv)

    def _at_ssd_bwd(self):
        stages = (_chip_exchange([self.sums[n][0] for n in MIXER_MATS]),
                  _whole_to_sibling([self.halves[n] for n in FFN_MATS]))

        def landed(extra):
            recv, theirs = _split(stages, extra)
            self.chip_sums(MIXER_MATS, recv)
            self.adamw(FFN_MATS, theirs)

        return _join(*stages), landed

    def _at_ssm_conv_bwd(self):
        return _whole_to_sibling([self.halves[n] for n in MIXER_MATS]), lambda theirs: self.adamw(MIXER_MATS, theirs)

    def _at_in_proj_dx(self):
        grads = [_from_cat_t(self.g.pop("w_cat_t"))]
        self.pair_sums(("w_in",), grads,
                       _run_exchange("grad_pair_exchange_w_in", _pair_exchange([_GRAD_ENTRY["w_in"]], grads)))
        return _chip_exchange([self.sums["w_in"][0]]), lambda recv: self.chip_sums(("w_in",), recv)

    def finish(self, g_small, g_rep, rep_shards):
        stages = (_pair_exchange([_ENTRY["small"]], [g_small]), _whole_to_sibling([g_rep]))
        recv_small, recv_rep = _split(stages, _run_exchange("grad_pair_exchange_tail", _join(*stages)))
        self.pair_sums(("small",), [g_small], recv_small)
        p_rep, = _rowwise("pair_sum_replicated", lambda r0, a, b: [a + b], SMALL_ROWS, SMALL_ROWS,
                          [(g_rep, LANES, 0), (recv_rep[0], LANES, 0)], [], [(LANES, F32)], [])
        stages = (_chip_exchange([self.sums["small"][0]]), _to_all_chips(p_rep))
        recv, recv_rep = _split(stages, _run_exchange("grad_chip_exchange_tail", _join(*stages)))
        self.chip_sums(("small",), recv)
        g_rep_tot = _chip_sum_small(p_rep, recv_rep[0], self.ids)
        last = ("w_in", "small")
        self.adamw(last, _run_exchange("grad_half_share_tail", _whole_to_sibling([self.halves[n] for n in last])))
        self.results["replicated"] = _adamw_replicated(g_rep_tot, rep_shards["w"], rep_shards["m"], rep_shards["v"])
        return g_rep_tot[_rep_rows()[1], 0]


def kernel(x, meta_tokens, norm_pre_mix, w_in, ssm_conv_w, ssm_conv_b, ssm_dt_bias, ssm_a_log, ssm_d_skip, ssm_norm, w_ssm_out, attn_sinks, w_attn_out, w_mix_out, norm_post_mix, norm_pre_ffn, w_ffn_up, ffn_conv_w, ffn_conv_b, w_ffn_down, norm_post_ffn, loss_target, m_meta_tokens, m_norm_pre_mix, m_w_in, m_ssm_conv_w, m_ssm_conv_b, m_ssm_dt_bias, m_ssm_a_log, m_ssm_d_skip, m_ssm_norm, m_w_ssm_out, m_attn_sinks, m_w_attn_out, m_w_mix_out, m_norm_post_mix, m_norm_pre_ffn, m_w_ffn_up, m_ffn_conv_w, m_ffn_conv_b, m_w_ffn_down, m_norm_post_ffn, v_meta_tokens, v_norm_pre_mix, v_w_in, v_ssm_conv_w, v_ssm_conv_b, v_ssm_dt_bias, v_ssm_a_log, v_ssm_d_skip, v_ssm_norm, v_w_ssm_out, v_attn_sinks, v_w_attn_out, v_w_mix_out, v_norm_post_mix, v_norm_pre_ffn, v_w_ffn_up, v_ffn_conv_w, v_ffn_conv_b, v_w_ffn_down, v_norm_post_ffn):
    args = dict(locals())
    squeeze = lambda a: a.reshape(a.shape[-2:])
    wts = {n: squeeze(args[n]) for n in WEIGHT_ORDER}
    mom = {n: squeeze(args["m_" + n]) for n in WEIGHT_ORDER}
    var = {n: squeeze(args["v_" + n]) for n in WEIGHT_ORDER}
    x_i, y_i, c_i = _mesh_pos()
    ids = jnp.stack([c_i, _chip_index(x_i, y_i)]).astype(jnp.int32)
    big_names = [n for n, _, _, _ in _BIG[:-1]]
    small_names = [n for n, _, _ in _SMALL_SHARDED]
    rep_names = [n for n, _ in _REPLICATED]

    stacks = {"w": wts, "m": mom, "v": var}
    shards = {n: {"w": args[n], "m": args["m_" + n], "v": args["v_" + n]} for n in big_names}
    shards["small"] = {k: _small_shard([d[n] for n in small_names]) for k, d in stacks.items()}
    rep_shards = {k: [d[n] for n in rep_names] for k, d in stacks.items()}

    w_in4, small_all = _gather_weights([_ENTRY["w_in"], _ENTRY["small"]], [wts["w_in"].astype(BF16), shards["small"]["w"]])
    w = {n: wts[n] for n in rep_names}
    w["w_cat"] = _to_cat(w_in4)
    small_parts = [_unflatten(small_all[i], [shp for _, shp, _ in _SMALL_SHARDED]) for i in range(4)]
    for k, (n, _, axis) in enumerate(_SMALL_SHARDED):
        w[n] = jnp.concatenate([small_parts[i][k] for i in range(4)], axis=axis)
    plan = _StepPlan(w, {n: wts[n].astype(BF16) for n in MIXER_MATS + FFN_MATS}, shards, ids)

    head = jnp.concatenate([jnp.zeros((PAD, D_MODEL), F32), w["meta_tokens"]], axis=0)
    loss_sum, dx, dhead = _local_step(x[0], head, loss_target[0], plan)
    g = plan.g
    g["meta_tokens"] = dhead[PAD:]
    g_small = jnp.stack([_small_shard([_shard_of(g[n], i, shp, ax) for n, shp, ax in _SMALL_SHARDED]) for i in range(4)])
    loss_part = (loss_sum * (0.5 / D_MODEL)).reshape(1, 1)
    loss = plan.finish(g_small, _in_rows([g[n] for n in rep_names] + [loss_part]), rep_shards)

    results = {}
    for kind in range(4):
        results.update({(kind, n): plan.results[n][kind] for n in big_names})
        parts = _unflatten(plan.results["small"][kind], [shp for _, shp, _ in _SMALL_SHARDED])
        results.update({(kind, n): parts[k] for k, n in enumerate(small_names)})
        results.update({(kind, n): plan.results["replicated"][kind][k] for k, n in enumerate(rep_names)})
    outs = [results[kind, n].reshape(args[n].shape) for kind in range(4) for n in WEIGHT_ORDER]
    return (loss, dx[None], *outs)
```

```python
import math
from typing import Any, Callable, NamedTuple, Sequence

import jax
import jax.numpy as jnp
from jax import lax
from jax.experimental import pallas as pl
from jax.experimental.pallas import tpu as pltpu

F32 = jnp.float32
BF16 = jnp.bfloat16

D_MODEL = 1024
N_META = 16
T = 128
PAD = T - N_META
D_INNER = 2048
SSM_HEADS = 32
HEAD_P = 64
SSM_GROUPS = 4
GROUP_W = D_INNER // SSM_GROUPS
D_STATE = 128
CONV_DIM = D_INNER + 2 * SSM_GROUPS * D_STATE
ATTN_HEADS = 16
KV_HEADS = 4
ATTN_W = 1024
KV_W = 256
FFN_DIM = 2816
N_IN = 8736
EPS = 1e-6
NEG = -1e30
SCALE = 0.125

P_Q, P_K, P_V, P_DT, P_Z, P_GATE, P_XBC = 0, 1024, 1280, 1536, 2048, 4096, 6144
QKV_W = 1536
P_W = 9216

ADAM_LR, ADAM_B1, ADAM_B2, ADAM_EPS, ADAM_WD, ADAM_STEP = 0.001, 0.9, 0.999, 1e-08, 0.01, 10

VMEM_BUDGET = 40 * 1024 * 1024
VMEM_LIMIT = 56 * 1024 * 1024
MESH = pl.DeviceIdType.MESH
ANY = pl.BlockSpec(memory_space=pl.ANY)


def _cparams(n_axes, **kw):
    return pltpu.CompilerParams(dimension_semantics=("arbitrary",) * n_axes, vmem_limit_bytes=VMEM_LIMIT, **kw)


class _Exchange(NamedTuple):
    ins: Sequence[Any]
    out_shapes: Sequence[Any]
    make_copies: Callable
    n_copies: int
    aliases: dict = {}


def _call(body, name, grid, in_specs, out_specs, out_shape, operands, scratch_shapes=(), aliases=None, bg=None):
    aliases = dict(aliases or {})
    if bg is None:
        return pl.pallas_call(body, name=name, grid=grid, in_specs=in_specs, out_specs=out_specs, out_shape=out_shape,
                              scratch_shapes=list(scratch_shapes), input_output_aliases=aliases,
                              compiler_params=_cparams(len(grid)))(*operands)
    n_in, n_out, n_scr = len(in_specs), len(out_specs), len(scratch_shapes)
    nb_in, nb_out = len(bg.ins), len(bg.out_shapes)

    def hosted(*refs):
        ins, bg_ins = refs[:n_in], refs[n_in:n_in + nb_in]
        outs = refs[n_in + nb_in:n_in + nb_in + n_out]
        bg_outs = refs[n_in + nb_in + n_out:n_in + nb_in + n_out + nb_out]
        scratch = refs[n_in + nb_in + n_out + nb_out:n_in + nb_in + n_out + nb_out + n_scr]
        send_sems, recv_sems = refs[-2:]
        pids = [pl.program_id(a) for a in range(len(grid))]
        first, last = pids[0] == 0, pids[0] == grid[0] - 1
        for p, g in zip(pids[1:], grid[1:]):
            first, last = first & (p == 0), last & (p == g - 1)
        copies = []
        for k, (src, dst, peer) in enumerate(bg.make_copies(bg_ins, bg_outs)):
            if peer is None:
                copies.append(pltpu.make_async_copy(src, dst, send_sems.at[k]))
            else:
                copies.append(pltpu.make_async_remote_copy(src_ref=src, dst_ref=dst, send_sem=send_sems.at[k],
                                                           recv_sem=recv_sems.at[k], device_id=peer, device_id_type=MESH))
        assert len(copies) == bg.n_copies

        @pl.when(first)
        def _():
            for cp in copies:
                cp.start()

        body(*ins, *outs, *scratch)

        @pl.when(last)
        def _():
            for cp in copies:
                cp.wait()

    aliases = {(k if k < n_in else k + nb_in): v for k, v in aliases.items()}
    aliases.update({n_in + k: n_out + v for k, v in bg.aliases.items()})
    res = pl.pallas_call(
        hosted, name=name, grid=grid, in_specs=list(in_specs) + [ANY] * nb_in, out_specs=list(out_specs) + [ANY] * nb_out,
        out_shape=list(out_shape) + list(bg.out_shapes), input_output_aliases=aliases,
        scratch_shapes=list(scratch_shapes) + [pltpu.SemaphoreType.DMA((bg.n_copies,))] * 2,
        compiler_params=_cparams(len(grid), has_side_effects=True))(*operands, *bg.ins)
    return res[:n_out], res[n_out:]


def _sigmoid(x):
    return 1.0 / (1.0 + jnp.exp(-x))


def _silu(x):
    return x * _sigmoid(x)


def _silu_grad(x):
    s = _sigmoid(x)
    return x * s, s * (1.0 + x * (1.0 - s))


def _dsilu(x):
    return _silu_grad(x)[1]


def _softplus(x):
    e = jnp.exp(-jnp.abs(x))
    small = e * (1.0 - e * (0.5 - e * (1.0 / 3.0)))
    return jnp.maximum(x, 0.0) + jnp.where(e < 0.01, small, jnp.log(1.0 + e))


def _rms(x, w):
    r = lax.rsqrt(jnp.mean(x * x, axis=-1, keepdims=True) + EPS)
    return x * r * w


def _rms_bwd(dy, x, w):
    r = lax.rsqrt(jnp.mean(x * x, axis=-1, keepdims=True) + EPS)
    xh = x * r
    g = dy * w
    dx = r * (g - xh * jnp.mean(g * xh, axis=-1, keepdims=True))
    dw = jnp.sum(dy * xh, axis=0, keepdims=True)
    return dx, dw


def _dot(a, b):
    return jnp.dot(a, b, preferred_element_type=F32)


def _dot_nt(a, b):
    return lax.dot_general(a, b, (((1,), (1,)), ((), ())), preferred_element_type=F32)


def _dot_tn(a, b):
    return lax.dot_general(a, b, (((0,), (0,)), ((), ())), preferred_element_type=F32)


def _split3(x):
    hi = x.astype(BF16)
    r = x - hi.astype(F32)
    mid = r.astype(BF16)
    lo = (r - mid.astype(F32)).astype(BF16)
    return hi, mid, lo


def _xdot(x, e):
    hi, mid, lo = _split3(x)
    return _dot(hi, e) + _dot(mid, e) + _dot(lo, e)


def _xdot_l(e, x):
    hi, mid, lo = _split3(x)
    return _dot(e, hi) + _dot(e, mid) + _dot(e, lo)


def _iota(shape, dim):
    return lax.broadcasted_iota(jnp.int32, shape, dim)


def _divisors(n, unit):
    return [t for t in range(unit, n + 1, unit) if n % t == 0]


MIN_MATMUL_STEPS = 8


def _matmul_tiles(m, n, k, a_bytes, b_bytes, o_bytes, m_unit):
    best = None
    for tm in _divisors(m, m_unit):
        for tn in _divisors(n, 128):
            for tk in _divisors(k, 128):
                acc = 0 if tk == k else tm * tn * 4
                vm = 2 * (tm * tk * a_bytes + tk * tn * b_bytes + tm * tn * o_bytes) + acc
                if vm > VMEM_BUDGET:
                    continue
                steps = (m // tm) * (n // tn) * (k // tk)
                score = (tk == k, min(steps, MIN_MATMUL_STEPS), min(tm, 256), tm * tn * tk)
                if best is None or score > best[0]:
                    best = (score, (tm, tn, tk))
    return best[1]


def _matmul(name, a, b, mode, out_dtype, bg=None):
    if mode == "nn":
        (m, k), n = a.shape, b.shape[1]
    elif mode == "nt":
        (m, k), n = a.shape, b.shape[0]
    else:
        (k, m), n = a.shape, b.shape[1]
    ab, bb, ob = a.dtype.itemsize, b.dtype.itemsize, jnp.dtype(out_dtype).itemsize
    tm, tn, tk = _matmul_tiles(m, n, k, ab, bb, ob, 128 if mode == "tn" else 16)
    nk = k // tk
    dot = {"nn": _dot, "nt": _dot_nt, "tn": _dot_tn}[mode]

    def body(a_ref, b_ref, o_ref, *scratch):
        prod = dot(a_ref[...].astype(BF16), b_ref[...].astype(BF16))
        if nk == 1:
            o_ref[...] = prod.astype(o_ref.dtype)
        else:
            acc_ref, = scratch
            kk = pl.program_id(2)

            @pl.when(kk == 0)
            def _():
                acc_ref[...] = prod

            @pl.when(kk > 0)
            def _():
                acc_ref[...] += prod

            @pl.when(kk == nk - 1)
            def _():
                o_ref[...] = acc_ref[...].astype(o_ref.dtype)

    a_spec = pl.BlockSpec((tk, tm), lambda i, j, kk: (kk, i)) if mode == "tn" else pl.BlockSpec((tm, tk), lambda i, j, kk: (i, kk))
    b_spec = pl.BlockSpec((tn, tk), lambda i, j, kk: (j, kk)) if mode == "nt" else pl.BlockSpec((tk, tn), lambda i, j, kk: (kk, j))
    res = _call(body, name, (m // tm, n // tn, nk), [a_spec, b_spec], [pl.BlockSpec((tm, tn), lambda i, j, kk: (i, j))],
                [jax.ShapeDtypeStruct((m, n), out_dtype)], [a, b],
                scratch_shapes=[] if nk == 1 else [pltpu.VMEM((tm, tn), F32)], bg=bg)
    return res[0] if bg is None else (res[0][0], res[1])


def _row_tile(n_rows, cap, unit=16):
    return max([t for t in _divisors(n_rows, unit) if t <= cap], default=n_rows)


ROW_SUB = 384
GROUP_UNROLL = 4


def _rowwise(name, fn, n_rows, tm, row_ins, full_ins, row_outs, acc_outs, bg=None):
    n_in = len(row_ins) + len(full_ins)
    n_ro = len(row_outs)
    into = [(k, o[3]) for k, o in enumerate(row_outs) if len(o) > 2 and o[2] == "into"]

    n_row_in = len(row_ins)
    sub = min(tm, ROW_SUB)

    def body(*refs):
        i = pl.program_id(0)
        outs = refs[n_in + len(into):]

        sums = tuple(jnp.zeros((1, w), F32) for w in acc_outs)
        for s in range(tm // sub):
            rows = pl.ds(s * sub, sub)
            vals = [r[rows, :] for r in refs[:n_row_in]] + [r[...] for r in refs[n_row_in:n_in]]
            res = fn(i * tm + s * sub, *vals)
            for o, r, v in zip(row_outs, outs[:n_ro], res[:n_ro]):
                if len(o) > 2 and o[2] == "first":
                    @pl.when(i == 0)
                    def _(r=r, v=v, rows=rows):
                        r[rows, :] = v.astype(r.dtype)
                else:
                    r[rows, :] = v.astype(r.dtype)
            sums = tuple(a + v for a, v in zip(sums, res[n_ro:]))

        @pl.when(i == 0)
        def _():
            for r, v in zip(outs[n_ro:], sums):
                r[...] = v

        @pl.when(i > 0)
        def _():
            for r, v in zip(outs[n_ro:], sums):
                r[...] += v

    def in_spec(entry):
        w, cb = entry[1], entry[2]
        if len(entry) > 3 and entry[3] == "prev":
            return pl.BlockSpec((tm, w), lambda i: (jnp.maximum(i - 1, 0), cb))
        if len(entry) > 3 and entry[3] == "first":
            return pl.BlockSpec((tm, w), lambda i: (0, cb))
        return pl.BlockSpec((tm, w), lambda i: (i, cb))

    def out_spec(o):
        if len(o) == 2:
            return pl.BlockSpec((tm, o[0]), lambda i: (i, 0)), jax.ShapeDtypeStruct((n_rows, o[0]), o[1])
        if o[2] == "new":
            return pl.BlockSpec((tm, o[0]), lambda i: (i, o[4])), jax.ShapeDtypeStruct((n_rows, o[3]), o[1])
        if o[2] == "into":
            return pl.BlockSpec((tm, o[0]), lambda i: (i, o[4])), jax.ShapeDtypeStruct(o[3].shape, o[3].dtype)
        if o[2] == "first":
            return pl.BlockSpec((tm, o[0]), lambda i: (0, 0)), jax.ShapeDtypeStruct((tm, o[0]), o[1])
        return pl.BlockSpec((tm, o[0]), lambda i: (jnp.maximum(i - 1, 0), 0)), jax.ShapeDtypeStruct((o[3], o[0]), o[1])

    in_specs = [in_spec(e) for e in row_ins]
    in_specs += [pl.BlockSpec(a.shape, lambda i: (0, 0)) for a in full_ins]
    in_specs += [pl.BlockSpec(memory_space=pl.ANY) for _ in into]
    specs_shapes = [out_spec(o) for o in row_outs]
    out_specs = [s for s, _ in specs_shapes] + [pl.BlockSpec((1, w), lambda i: (0, 0)) for w in acc_outs]
    out_shape = [s for _, s in specs_shapes] + [jax.ShapeDtypeStruct((1, w), F32) for w in acc_outs]
    return _call(body, name, (n_rows // tm,), in_specs, out_specs, out_shape,
                 [e[0] for e in row_ins] + list(full_ins) + [arr for _, arr in into],
                 aliases={n_in + a: k for a, (k, _) in enumerate(into)}, bg=bg)


def _valid_rows(first_row, tm, lo):
    return (first_row + _iota((tm, 1), 0)) >= lo


CONV_ROWS = 128
CONV_SUB = 16
CONV_LANES = 256


def _conv_specs(tm, width, blk, n_rows, after):
    specs = [pl.BlockSpec((tm, width), lambda i: (i, blk)),
             pl.BlockSpec((8, width), lambda i: (jnp.maximum(i * (tm // 8) - 1, 0), blk))]
    if after:
        specs.append(pl.BlockSpec((16, width), lambda i: (jnp.minimum((i + 1) * (tm // 16), n_rows // 16 - 1), blk)))
    return specs


def _conv_window(win, w_ref, b_ref, taps, c0, cw, n):
    acc = b_ref[:, c0:c0 + cw] + w_ref[taps - 1:taps, c0:c0 + cw] * win[8:8 + n]
    for k in range(taps - 1):
        acc = acc + w_ref[k:k + 1, c0:c0 + cw] * win[8 - (taps - 1) + k:8 - (taps - 1) + k + n]
    return acc


def _ffn_act(name, u_raw, conv_w, conv_b, n_rows):
    tm, sub, cw = CONV_ROWS, CONV_SUB, CONV_LANES
    taps, width = conv_w.shape
    half = width // 2

    def body(cur_ref, prev_ref, w_ref, b_ref, f_ref, ext_ref):
        i = pl.program_id(0)
        ext_ref[0:8, :] = jnp.where(i > 0, prev_ref[...], 0.0)
        ext_ref[8:8 + tm, :] = cur_ref[...]
        for q in range(half // cw):
            a0, g0 = q * cw, half + q * cw

            def group(s, carry):
                r = pl.multiple_of(s * sub, sub)
                a = _conv_window(ext_ref[pl.ds(r, sub + 8), a0:a0 + cw], w_ref, b_ref, taps, a0, cw, sub)
                g = _conv_window(ext_ref[pl.ds(r, sub + 8), g0:g0 + cw], w_ref, b_ref, taps, g0, cw, sub)
                f_ref[pl.ds(r, sub), a0:a0 + cw] = (_silu(a) * g).astype(f_ref.dtype)
                return carry

            lax.fori_loop(0, tm // sub, group, 0, unroll=GROUP_UNROLL)

        @pl.when(i == 0)
        def _():
            f_ref[0:PAD, :] = jnp.zeros((PAD, half), f_ref.dtype)

    return pl.pallas_call(
        body, name=name, grid=(n_rows // tm,),
        in_specs=_conv_specs(tm, width, 0, n_rows, False) + [pl.BlockSpec((taps, width), lambda i: (0, 0)),
                                                             pl.BlockSpec((1, width), lambda i: (0, 0))],
        out_specs=pl.BlockSpec((tm, half), lambda i: (i, 0)),
        out_shape=jax.ShapeDtypeStruct((n_rows, half), BF16),
        scratch_shapes=[pltpu.VMEM((tm + 8, width), F32)],
        compiler_params=_cparams(1),
    )(u_raw, u_raw, conv_w, conv_b)


def _conv_bwd(name, raw, raw_blk, dsrcs, chunk_src, conv_w, conv_b, n_rows, gated, into=None, into_blk=0, bg=None):
    taps, width = conv_w.shape
    half = width // 2 if gated else width
    tm, sub, cw = CONV_ROWS, CONV_SUB, CONV_LANES
    te = tm + 16
    nd = len(dsrcs)
    n_parts = 2 if gated else 1

    def body(*refs):
        cur_ref, prev_ref, next_ref = refs[0:3]
        dcur, dnext = refs[3:3 + nd], refs[3 + nd:3 + 2 * nd]
        w_ref, b_ref = refs[3 + 2 * nd:5 + 2 * nd]
        out_ref, acc_ref, ext_ref, du_ref = refs[-4:]
        i = pl.program_id(0)
        ext_ref[0:8, :] = jnp.where(i > 0, prev_ref[...], 0.0)
        ext_ref[8:8 + tm, :] = cur_ref[...]
        ext_ref[8 + tm:24 + tm, :] = next_ref[...]

        for q, (src, off) in enumerate(chunk_src):
            cols = [q * cw, half + q * cw][:n_parts]

            def conv_grad(r, d, past_end):
                pre = [_conv_window(ext_ref[pl.ds(r, sub + 8), c0:c0 + cw], w_ref, b_ref, taps, c0, cw, sub) for c0 in cols]
                if gated:
                    act, dact = _silu_grad(pre[0])
                    dus = [d * pre[1] * dact, d * act]
                else:
                    dus = [d * _dsilu(pre[0])]
                for part, du in enumerate(dus):
                    if past_end:
                        du = jnp.where(i * tm + r + _iota((sub, 1), 0) < n_rows, du, 0.0)
                    du_ref[part, pl.ds(r, sub), :] = du

            def tile_rows(s, carry):
                r = pl.multiple_of(s * sub, sub)
                conv_grad(r, dcur[src][pl.ds(r, sub), off:off + cw].astype(F32), False)
                return carry

            lax.fori_loop(0, tm // sub, tile_rows, 0, unroll=GROUP_UNROLL)
            conv_grad(tm, dnext[src][:, off:off + cw].astype(F32), True)

            @pl.when(i == 0)
            def _():
                du_ref[:, 0:PAD, :] = jnp.zeros((n_parts, PAD, cw), F32)

            for part, c0 in enumerate(cols):
                taps_w = [w_ref[k:k + 1, c0:c0 + cw] for k in range(taps)]

                def back(s, sums):
                    new = list(sums)
                    for u in range(2):
                        r = pl.multiple_of((2 * s + u) * sub, sub)
                        win = du_ref[part, pl.ds(r, sub + 8), :]
                        raw_rows = ext_ref[pl.ds(8 + r, sub), c0:c0 + cw]
                        draw = jnp.zeros((sub, cw), F32)
                        for k in range(taps):
                            shifted = win[taps - 1 - k:taps - 1 - k + sub]
                            draw = draw + taps_w[k] * shifted
                            new[k] = new[k] + shifted * raw_rows
                        new[taps] = new[taps] + win[0:sub]
                        out_ref[pl.ds(r, sub), c0:c0 + cw] = draw.astype(out_ref.dtype)
                    return tuple(new)

                sums = lax.fori_loop(0, tm // (2 * sub), back, tuple(jnp.zeros((sub, cw), F32) for _ in range(taps + 1)))

                @pl.when(i == 0)
                def _(c0=c0):
                    out_ref[PAD - sub:PAD, c0:c0 + cw] = jnp.zeros((sub, cw), out_ref.dtype)

                for k in range(taps + 1):
                    total = jnp.sum(sums[k], axis=0, keepdims=True)
                    acc_ref[k:k + 1, c0:c0 + cw] = jnp.where(i == 0, total, acc_ref[k:k + 1, c0:c0 + cw] + total)

    in_specs = _conv_specs(tm, width, raw_blk, n_rows, True)
    in_specs += [pl.BlockSpec((tm, d.shape[1]), lambda i: (i, 0)) for d in dsrcs]
    in_specs += [pl.BlockSpec((16, d.shape[1]), lambda i: (jnp.minimum((i + 1) * (tm // 16), n_rows // 16 - 1), 0)) for d in dsrcs]
    in_specs += [pl.BlockSpec((taps, width), lambda i: (0, 0)), pl.BlockSpec((1, width), lambda i: (0, 0))]
    operands = [raw, raw, raw] + list(dsrcs) + list(dsrcs) + [conv_w, conv_b]
    aliases = {}
    if into is None:
        out0 = jax.ShapeDtypeStruct((n_rows, width), BF16)
    else:
        in_specs.append(pl.BlockSpec(memory_space=pl.ANY))
        operands.append(into)
        aliases = {len(operands) - 1: 0}
        out0 = jax.ShapeDtypeStruct(into.shape, into.dtype)
    return _call(body, name, (n_rows // tm,), in_specs,
                 [pl.BlockSpec((tm, width), lambda i: (i, into_blk)), pl.BlockSpec((8, width), lambda i: (0, 0))],
                 [out0, jax.ShapeDtypeStruct((8, width), F32)], operands,
                 scratch_shapes=[pltpu.VMEM((tm + 24, width), F32), pltpu.VMEM((n_parts, te + 8, cw), F32)],
                 aliases=aliases, bg=bg)


def _ssd_specs(n_chunks, rev, per_step=1):
    cidx = (lambda c: n_chunks - 1 - c) if rev else (lambda c: c)
    xw, nw = per_step * GROUP_W, per_step * D_STATE
    xg0, bg0, cg0 = P_XBC // xw, (P_XBC + D_INNER) // nw, (P_XBC + D_INNER + SSM_GROUPS * D_STATE) // nw

    def cur(width, blk0):
        return pl.BlockSpec((T, width), lambda g, c: (cidx(c), blk0 + g))

    def prev(width, blk0):
        return pl.BlockSpec((8, width), lambda g, c: (jnp.maximum(cidx(c) * (T // 8) - 1, 0), blk0 + g))

    specs = [cur(xw, xg0), prev(xw, xg0), cur(nw, bg0), prev(nw, bg0), cur(nw, cg0), prev(nw, cg0),
             pl.BlockSpec((T, 128), lambda g, c: (cidx(c), P_DT // 128))]
    wb, wc = D_INNER // nw, (D_INNER + SSM_GROUPS * D_STATE) // nw
    specs += [pl.BlockSpec((4, xw), lambda g, c: (0, g)),
              pl.BlockSpec((4, nw), lambda g, c: (0, wb + g)),
              pl.BlockSpec((4, nw), lambda g, c: (0, wc + g)),
              pl.BlockSpec((1, xw), lambda g, c: (0, g)),
              pl.BlockSpec((1, nw), lambda g, c: (0, wb + g)),
              pl.BlockSpec((1, nw), lambda g, c: (0, wc + g))]
    specs += [pl.BlockSpec((1, 128), lambda g, c: (0, 0))] * 3
    return specs, cidx


def _ssd_shared(refs, c):
    dt_ref, dtb_ref, alog_ref = refs[6], refs[13], refs[14]
    valid = _valid_rows(c * T, T, PAD)
    dtr = dt_ref[...] + dtb_ref[...]
    dt = jnp.where(valid, _softplus(dtr), 0.0)
    a_neg = -jnp.exp(alog_ref[...])
    tril = _iota((T, T), 0) >= _iota((T, T), 1)
    cs = _xdot_l(tril.astype(BF16), dt * a_neg)
    return dict(valid=valid, dtr=dtr, dt=dt, a_neg=a_neg, tril=tril, cs=cs, cs_t=cs.T)


def _heads_of_lanes():
    hh_t, ll_t = _iota((D_INNER, 128), 1), _iota((D_INNER, 128), 0)
    return (hh_t == jnp.right_shift(ll_t, 6)).astype(BF16)


def _ssd_chunk_forward(refs, ext_ref, g, c, shared):
    (xc_ref, xp_ref, bc_ref, bp_ref, cc_ref, cp_ref, dt_ref, wx_ref, wb_ref, wc_ref,
     bx_ref, bb_ref, bcb_ref, dtb_ref, alog_ref, dsk_ref) = refs

    def conv_pre(cur_ref, prev_ref, w_ref, b_ref, width):
        ext_ref[0:8, 0:width] = jnp.where(c > 0, prev_ref[...], 0.0)
        ext_ref[8:8 + T, 0:width] = cur_ref[...]
        w = w_ref[...]
        acc = b_ref[...] + w[3:4] * cur_ref[...]
        for k in range(3):
            acc = acc + w[k:k + 1] * ext_ref[pl.ds(5 + k, T), 0:width]
        return acc

    v = dict(shared)
    valid = v["valid"]
    v["head0"] = 8 * g
    v["x_pre"] = conv_pre(xc_ref, xp_ref, wx_ref, bx_ref, GROUP_W)
    v["b_pre"] = conv_pre(bc_ref, bp_ref, wb_ref, bb_ref, D_STATE)
    v["c_pre"] = conv_pre(cc_ref, cp_ref, wc_ref, bcb_ref, D_STATE)
    xs = _silu(v["x_pre"])
    bm = jnp.where(valid, _silu(v["b_pre"]), 0.0)
    cm = jnp.where(valid, _silu(v["c_pre"]), 0.0)
    hh, ll = _iota((128, GROUP_W), 0), _iota((128, GROUP_W), 1)
    expand = (hh == 8 * g + jnp.right_shift(ll, 6)).astype(BF16)
    cs_e = _xdot(v["cs"], expand)
    dt_e = _xdot(v["dt"], expand)
    cs_last_e = cs_e[T - 1:T, :]
    v.update(xs=xs, bm=bm, cm=cm, cs_e=cs_e, dt_e=dt_e, cs_last_e=cs_last_e)
    v["xdt"] = xs * dt_e
    v["decay_e"] = jnp.exp(cs_last_e - cs_e)
    v["ecs_e"] = jnp.exp(cs_e)
    v["elast_e"] = jnp.exp(cs_last_e)
    v["d_e"] = _xdot(dsk_ref[...], expand)
    v["gmat"] = _dot_nt(cm.astype(BF16), bm.astype(BF16))
    return v


def _ssd_decay_pair(v, jp):
    out = []
    for j in (v["head0"] + 2 * jp, v["head0"] + 2 * jp + 1):
        diff = v["cs"][:, j:j + 1] - v["cs_t"][j:j + 1, :]
        out.append(jnp.where(v["tril"], jnp.exp(jnp.where(v["tril"], diff, 0.0)), 0.0))
    return out


def _block_diag_pair(xp):
    lane = _iota(xp.shape, 1)
    return jnp.concatenate([jnp.where(lane < HEAD_P, xp, 0.0), jnp.where(lane >= HEAD_P, xp, 0.0)], axis=0)


SSD_GROUPS_PER_STEP = 4


def _ssd_group_refs(refs, gg):
    x_w, n_w = pl.ds(GROUP_W * gg, GROUP_W), pl.ds(D_STATE * gg, D_STATE)
    lanes = [x_w, x_w, n_w, n_w, n_w, n_w, None, x_w, n_w, n_w, x_w, n_w, n_w, None, None, None]
    return [r if w is None else r.at[:, w] for r, w in zip(refs, lanes)]


def _ssd_fwd(p, conv_w, conv_b, dt_bias, a_log, d_skip, n_chunks, bg=None):
    n_rows = n_chunks * T
    in_specs, _ = _ssd_specs(n_chunks, rev=False, per_step=SSD_GROUPS_PER_STEP)
    per = SSD_GROUPS_PER_STEP
    assert per == SSM_GROUPS

    def body(*refs):
        y_ref, hin_ref, st_ref, ext_ref = refs[16:]
        c = pl.program_id(1)

        @pl.when(c == 0)
        def _():
            st_ref[...] = jnp.zeros_like(st_ref)

        shared = _ssd_shared(refs[:16], c)
        for gg in range(per):
            v = _ssd_chunk_forward(_ssd_group_refs(refs[:16], gg), ext_ref.at[gg], gg, c, shared)
            state = st_ref[gg]
            hin_ref[gg] = state
            ys = []
            for jp in range(4):
                l0, l1 = _ssd_decay_pair(v, jp)
                lhs = jnp.concatenate([v["gmat"] * l0, v["gmat"] * l1], axis=1).astype(BF16)
                rhs = _block_diag_pair(v["xdt"][:, 128 * jp:128 * jp + 128]).astype(BF16)
                ys.append(_dot(lhs, rhs))
            y = jnp.concatenate(ys, axis=1)
            y = y + _dot(v["cm"].astype(BF16), state.astype(BF16)) * v["ecs_e"] + v["xs"] * v["d_e"]
            y_ref[:, GROUP_W * gg:GROUP_W * gg + GROUP_W] = y
            s_new = _dot_tn(v["bm"].astype(BF16), (v["xdt"] * v["decay_e"]).astype(BF16))
            st_ref[gg] = state * v["elast_e"] + s_new

    return _call(
        body, "ssd_fwd", (SSM_GROUPS // per, n_chunks), in_specs,
        [pl.BlockSpec((T, per * GROUP_W), lambda g, c: (c, g)),
         pl.BlockSpec((per, None, D_STATE, GROUP_W), lambda g, c: (g, c, 0, 0))],
        [jax.ShapeDtypeStruct((n_rows, D_INNER), F32),
         jax.ShapeDtypeStruct((SSM_GROUPS, n_chunks, D_STATE, GROUP_W), F32)],
        [p, p, p, p, p, p, p, conv_w, conv_w, conv_w, conv_b, conv_b, conv_b, dt_bias, a_log, d_skip],
        scratch_shapes=[pltpu.VMEM((per, D_STATE, GROUP_W), F32), pltpu.VMEM((per, T + 8, GROUP_W), F32)], bg=bg)


def _ssd_bwd(p, conv_w, conv_b, dt_bias, a_log, d_skip, hin, dy, dp, n_chunks, bg=None):
    n_rows = n_chunks * T
    per = SSD_GROUPS_PER_STEP
    assert per == SSM_GROUPS
    dt_w = P_Z - P_DT
    in_specs, cidx = _ssd_specs(n_chunks, rev=True, per_step=per)
    in_specs = in_specs + [pl.BlockSpec((per, None, D_STATE, GROUP_W), lambda g, c: (g, cidx(c), 0, 0)),
                           pl.BlockSpec((T, per * GROUP_W), lambda g, c: (cidx(c), g)), ANY]

    def body(*refs):
        hin_ref, dy_ref = refs[16:18]
        dx_ref, db_ref, dc_ref, dp_ref, dpar_ref, dst_ref, ext_ref, red_ref, dd_ref = refs[19:]
        step = pl.program_id(1)
        shared = _ssd_shared(refs[:16], n_chunks - 1 - step)
        local = jnp.zeros((T, 128), F32)
        for gg in range(per):
            x_w, n_w = pl.ds(GROUP_W * gg, GROUP_W), pl.ds(D_STATE * gg, D_STATE)
            local = local + group_body(_ssd_group_refs(refs[:16], gg), hin_ref.at[gg], dy_ref.at[:, x_w],
                                       dx_ref.at[:, x_w], db_ref.at[:, n_w], dc_ref.at[:, n_w], red_ref.at[:, :, x_w],
                                       dd_ref.at[:, x_w], dst_ref.at[gg], ext_ref.at[gg], gg, shared)
        to_heads = _heads_of_lanes()
        dcs = _xdot(red_ref[0], to_heads) + local
        triu = (_iota((T, T), 0) <= _iota((T, T), 1)).astype(BF16)
        da = _xdot_l(triu, dcs)
        ddt = da * shared["a_neg"] + _xdot(red_ref[1], to_heads)
        ddtr = jnp.where(shared["valid"], ddt * _sigmoid(shared["dtr"]), 0.0)
        dp_ref[...] = jnp.concatenate([ddtr, jnp.zeros((T, dt_w - 128), F32)], axis=1).astype(dp_ref.dtype)
        dpar = jnp.concatenate([
            jnp.sum(ddtr, axis=0, keepdims=True),
            jnp.sum(da * shared["dt"], axis=0, keepdims=True) * shared["a_neg"],
            _xdot(dd_ref[0:1, :], to_heads),
            jnp.zeros((5, 128), F32)], axis=0)
        dpar_ref[...] = jnp.where(step == 0, dpar, dpar_ref[...] + dpar)

    def group_body(in_refs, hin_ref, dy_ref, dx_ref, db_ref, dc_ref, red_ref, dd_ref, dst_ref, ext_ref, g, shared):
        step = pl.program_id(1)
        c = n_chunks - 1 - step

        @pl.when(step == 0)
        def _():
            dst_ref[...] = jnp.zeros_like(dst_ref)

        v = _ssd_chunk_forward(in_refs, ext_ref, g, c, shared)
        hin_f = hin_ref[...]
        hin_b = hin_f.astype(BF16)
        dyv = dy_ref[...]
        dst = dst_ref[...]
        dst_b = dst.astype(BF16)
        xs, bm, cm, xdt = v["xs"], v["bm"], v["cm"], v["xdt"]
        bm_b, cm_b = bm.astype(BF16), cm.astype(BF16)

        dd_e = jnp.sum(dyv * xs, axis=0, keepdims=True)
        dxs = dyv * v["d_e"]
        ch = _dot(cm_b, hin_b)
        dch = (dyv * v["ecs_e"]).astype(BF16)
        dcm = _dot_nt(dch, hin_b)
        dhin = _dot_tn(cm_b, dch) + dst * v["elast_e"]
        dcs_e = dyv * ch * v["ecs_e"]
        dxd = _dot(bm_b, dst_b)
        dbm = _dot_nt((xdt * v["decay_e"]).astype(BF16), dst_b)
        dxdt_state = dxd * v["decay_e"]
        q = dxdt_state * xdt
        dcs_e = dcs_e - q
        dlast_e = jnp.sum(q, axis=0, keepdims=True) + jnp.sum(dst * hin_f, axis=0, keepdims=True) * v["elast_e"]
        dg = jnp.zeros((T, T), F32)
        rs_cols = jnp.zeros((T, 128), F32)
        cs_rows = jnp.zeros((128, T), F32)
        lane_i, sub_i = _iota((T, 128), 1), _iota((128, T), 0)
        dxdt_parts = []
        for jp in range(4):
            l0, l1 = _ssd_decay_pair(v, jp)
            m0, m1 = v["gmat"] * l0, v["gmat"] * l1
            xbd = _block_diag_pair(xdt[:, 128 * jp:128 * jp + 128]).astype(BF16)
            dyp = dyv[:, 128 * jp:128 * jp + 128]
            dm = _dot_nt(dyp.astype(BF16), xbd)
            dm0, dm1 = dm[:, 0:T], dm[:, T:2 * T]
            dg = dg + dm0 * l0 + dm1 * l1
            for j, qq in ((v["head0"] + 2 * jp, dm0 * m0), (v["head0"] + 2 * jp + 1, dm1 * m1)):
                rs_cols = jnp.where(lane_i == j, jnp.sum(qq, axis=1, keepdims=True), rs_cols)
                cs_rows = jnp.where(sub_i == j, jnp.sum(qq, axis=0, keepdims=True), cs_rows)
            mv = jnp.concatenate([m0, m1], axis=0).astype(BF16)
            dxdt_parts.append(_dot_tn(mv, _block_diag_pair(dyp).astype(BF16)))
        dxdt = jnp.concatenate(dxdt_parts, axis=1) + dxdt_state
        dg_b = dg.astype(BF16)
        dcm = dcm + _dot(dg_b, bm_b)
        dbm = dbm + _dot_tn(dg_b, cm_b)
        last_row = _iota((T, 1), 0) == T - 1
        red_ref[0] = dcs_e + jnp.where(last_row, dlast_e, 0.0)
        red_ref[1] = dxdt * xs
        dd_ref[0:1, :] = dd_e
        dx_ref[...] = dxs + dxdt * v["dt_e"]
        db_ref[...] = jnp.where(v["valid"], dbm, 0.0)
        dc_ref[...] = jnp.where(v["valid"], dcm, 0.0)
        dst_ref[...] = dhin
        return rs_cols - cs_rows.T

    return _call(
        body, "ssd_bwd", (SSM_GROUPS // per, n_chunks), in_specs,
        [pl.BlockSpec((T, per * GROUP_W), lambda g, c: (cidx(c), g)),
         pl.BlockSpec((T, per * D_STATE), lambda g, c: (cidx(c), g)),
         pl.BlockSpec((T, per * D_STATE), lambda g, c: (cidx(c), g)),
         pl.BlockSpec((T, dt_w), lambda g, c: (cidx(c), P_DT // dt_w)),
         pl.BlockSpec((8, 128), lambda g, c: (0, 0))],
        [jax.ShapeDtypeStruct((n_rows, D_INNER), F32),
         jax.ShapeDtypeStruct((n_rows, SSM_GROUPS * D_STATE), F32),
         jax.ShapeDtypeStruct((n_rows, SSM_GROUPS * D_STATE), F32),
         jax.ShapeDtypeStruct(dp.shape, dp.dtype),
         jax.ShapeDtypeStruct((8, 128), F32)],
        [p, p, p, p, p, p, p, conv_w, conv_w, conv_w, conv_b, conv_b, conv_b, dt_bias, a_log, d_skip, hin, dy, dp],
        scratch_shapes=[pltpu.VMEM((per, D_STATE, GROUP_W), F32), pltpu.VMEM((per, T + 8, GROUP_W), F32),
                        pltpu.VMEM((2, T, D_INNER), F32), pltpu.VMEM((8, D_INNER), F32)],
        aliases={18: 3}, bg=bg)


def _alibi_slope(h):
    return 2.0 ** (-8.0 * (h + 1) / ATTN_HEADS)


def _dup_half(x256, kvh):
    xb = x256[:, 128 * (kvh // 2):128 * (kvh // 2) + 128]
    rolled = pltpu.roll(xb, 64, 1)
    lane = _iota(xb.shape, 1)
    if kvh % 2 == 0:
        return jnp.where(lane < 64, xb, rolled)
    return jnp.where(lane < 64, rolled, xb)


def _attn_masks(c):
    qi, j = _iota((T, T), 0), _iota((T, T), 1)
    tri = j <= qi
    meta_ok = (j >= PAD) & (j - PAD <= c * T + qi - PAD)
    band_ok = c >= jnp.where(tri, 1, 2)
    dist = jnp.bitwise_and(qi - j, T - 1).astype(F32)
    return tri, meta_ok, band_ok, dist


def _fold(x3, tri):
    return jnp.concatenate([x3[:, 0:T], jnp.where(tri, x3[:, 2 * T:3 * T], x3[:, T:2 * T])], axis=1)


def _unfold(x2, tri):
    band = x2[:, T:2 * T]
    return jnp.concatenate([x2[:, 0:T], jnp.where(tri, 0.0, band), jnp.where(tri, band, 0.0)], axis=1)


def _attn_fwd(p, sinks, n_chunks, bg=None):
    n_rows = n_chunks * T
    kb, vb = P_K // KV_W, P_V // KV_W

    def body(q_ref, kc_ref, kp_ref, km_ref, vc_ref, vp_ref, vm_ref, sink_ref, o_ref, lse_ref):
        c = pl.program_id(0)
        sinks_v = sink_ref[...]
        masks = _attn_masks(c)
        tri, meta_ok, band_ok, dist = masks
        lane = _iota((T, 128), 1)
        for kvh in range(KV_HEADS):
            k3 = jnp.concatenate([_dup_half(r[...], kvh) for r in (km_ref, kp_ref, kc_ref)], axis=0).astype(BF16)
            v3 = jnp.concatenate([_dup_half(r[...], kvh) for r in (vm_ref, vp_ref, vc_ref)], axis=0)
            v3bd = _block_diag_rows(v3).astype(BF16)
            q2 = q_ref[:, 256 * kvh:256 * kvh + 256] * SCALE
            q4 = jnp.concatenate([jnp.where((lane < 64) if half == 0 else (lane >= 64), q2[:, 128 * pr:128 * pr + 128], 0.0)
                                  for pr in range(2) for half in range(2)], axis=0).astype(BF16)
            raw4 = _dot_nt(q4, k3)
            probs = []
            for hh in range(4):
                h = 4 * kvh + hh
                raw = raw4[T * hh:T * hh + T]
                band = jnp.where(tri, raw[:, 2 * T:3 * T], raw[:, T:2 * T]) - _alibi_slope(h) * dist
                sc = jnp.concatenate([jnp.where(meta_ok, raw[:, 0:T], NEG), jnp.where(band_ok, band, NEG)], axis=1)
                sink = sinks_v[:, h:h + 1]
                m = jnp.maximum(jnp.max(sc, axis=1, keepdims=True), sink)
                e = jnp.exp(sc - m)
                den = jnp.sum(e, axis=1, keepdims=True) + jnp.exp(sink - m)
                probs.append(_unfold(e * (1.0 / den), tri))
                lse_ref[:, h:h + 1] = m + jnp.log(den)
            p4 = jnp.concatenate([jnp.concatenate(probs[0:2], axis=1), jnp.concatenate(probs[2:4], axis=1)], axis=0)
            out = _dot(p4.astype(BF16), v3bd)
            o_ref[:, 256 * kvh:256 * kvh + 256] = jnp.concatenate([out[0:T], out[T:2 * T]], axis=1).astype(o_ref.dtype)

    blk = lambda width, col: pl.BlockSpec((T, width), lambda c: (c, col))
    prev = lambda width, col: pl.BlockSpec((T, width), lambda c: (jnp.maximum(c - 1, 0), col))
    first = lambda width, col: pl.BlockSpec((T, width), lambda c: (0, col))
    return _call(
        body, "attn_fwd", (n_chunks,),
        [blk(ATTN_W, P_Q // ATTN_W), blk(KV_W, kb), prev(KV_W, kb), first(KV_W, kb),
         blk(KV_W, vb), prev(KV_W, vb), first(KV_W, vb), pl.BlockSpec((1, 128), lambda c: (0, 0))],
        [pl.BlockSpec((T, ATTN_W), lambda c: (c, 0)), pl.BlockSpec((T, 128), lambda c: (c, 0))],
        [jax.ShapeDtypeStruct((n_rows, ATTN_W), BF16), jax.ShapeDtypeStruct((n_rows, 128), F32)],
        [p, p, p, p, p, p, p, sinks], bg=bg)


def _block_diag_rows(x3):
    lane = _iota(x3.shape, 1)
    return jnp.concatenate([jnp.where(lane < 64, x3, 0.0), jnp.where(lane >= 64, x3, 0.0)], axis=0)


def _fold_halves(x):
    return x + pltpu.roll(x, 64, 1)


def _attn_bwd(p, sinks, ao, lse, dao, dp, n_chunks, bg=None):
    kb, vb = P_K // KV_W, P_V // KV_W
    rc = lambda s: n_chunks - 1 - s

    def body(q_ref, kc_ref, kp_ref, km_ref, vc_ref, vp_ref, vm_ref, sink_ref, o_ref, lse_ref, do_ref, dp_in_ref,
             dqkv_ref, dsink_ref, kcar_ref, vcar_ref, kmeta_ref, vmeta_ref):
        step = pl.program_id(0)
        c = n_chunks - 1 - step

        @pl.when(step == 0)
        def _():
            for r in (kcar_ref, vcar_ref, kmeta_ref, vmeta_ref):
                r[...] = jnp.zeros_like(r)

        masks = _attn_masks(c)
        tri = masks[0]
        q = q_ref[...] * SCALE
        sinks_v = sink_ref[...]
        lse_v = lse_ref[...]
        ov = o_ref[...].astype(F32)
        dov = do_ref[...].astype(F32)
        lane = _iota((T, 128), 1)
        lane256 = _iota((3 * T, KV_W), 1)
        dsink = jnp.zeros((1, 128), F32)
        dk3_all = jnp.zeros((3 * T, KV_W), F32)
        dv3_all = jnp.zeros((3 * T, KV_W), F32)
        dqs = []
        for kvh in range(KV_HEADS):
            k3 = jnp.concatenate([_dup_half(r[...], kvh) for r in (km_ref, kp_ref, kc_ref)], axis=0).astype(BF16)
            v3 = jnp.concatenate([_dup_half(r[...], kvh) for r in (vm_ref, vp_ref, vc_ref)], axis=0).astype(BF16)
            halves = [(pr, half, (lane < 64) if half == 0 else (lane >= 64)) for pr in range(2) for half in range(2)]
            cols = [slice(128 * (2 * kvh + pr), 128 * (2 * kvh + pr) + 128) for pr in range(2)]
            q4 = jnp.concatenate([jnp.where(mine, q[:, cols[pr]], 0.0) for pr, _, mine in halves], axis=0).astype(BF16)
            do4 = jnp.concatenate([jnp.where(mine, dov[:, cols[pr]], 0.0) for pr, _, mine in halves], axis=0).astype(BF16)
            raw4 = _dot_nt(q4, k3)
            dp4 = _dot_nt(do4, v3)
            ds_rows, pm_rows = [], []
            for hh, (pr, half, mine) in enumerate(halves):
                h = 4 * kvh + hh
                raw = raw4[T * hh:T * hh + T]
                band = jnp.where(tri, raw[:, 2 * T:3 * T], raw[:, T:2 * T]) - _alibi_slope(h) * masks[3]
                sc = jnp.concatenate([jnp.where(masks[1], raw[:, 0:T], NEG), jnp.where(masks[2], band, NEG)], axis=1)
                lse_h = lse_v[:, h:h + 1]
                pm = jnp.exp(sc - lse_h)
                prod = dov[:, cols[pr]] * ov[:, cols[pr]]
                delta = jnp.sum(jnp.where(mine, prod, 0.0), axis=1, keepdims=True)
                dp = _fold(dp4[T * hh:T * hh + T], tri)
                ds_rows.append(_unfold(pm * (dp - delta), tri))
                pm_rows.append(_unfold(pm, tri))
                p_sink = jnp.exp(sinks_v[:, h:h + 1] - lse_h)
                dsink = jnp.where(_iota((1, 128), 1) == h, jnp.sum(-p_sink * delta, axis=0, keepdims=True), dsink)
            ds4 = jnp.concatenate(ds_rows, axis=0).astype(BF16)
            dq4 = _dot(ds4, k3)
            dk3 = _dot_tn(ds4, q4)
            dv3 = _dot_tn(jnp.concatenate(pm_rows, axis=0).astype(BF16), do4)
            for pr in range(2):
                dqs.append(jnp.where(lane < 64, dq4[2 * T * pr:2 * T * pr + T], dq4[2 * T * pr + T:2 * T * pr + 2 * T]) * SCALE)
            in_place = (lane256 >= 64 * kvh) & (lane256 < 64 * kvh + 64)
            wide = lambda x: jnp.concatenate([x, x], axis=1)
            dk3_all = jnp.where(in_place, wide(_fold_halves(dk3)), dk3_all)
            dv3_all = jnp.where(in_place, wide(_fold_halves(dv3)), dv3_all)
        dsink_all = dsink

        @pl.when(step == 0)
        def _():
            dsink_ref[...] = dsink_all

        @pl.when(step > 0)
        def _():
            dsink_ref[...] += dsink_all

        kmeta = kmeta_ref[...] + dk3_all[0:T]
        vmeta = vmeta_ref[...] + dv3_all[0:T]
        kmeta_ref[...] = kmeta
        vmeta_ref[...] = vmeta
        is_first = c == 0
        dk = jnp.where(is_first, kmeta, dk3_all[2 * T:3 * T] + kcar_ref[...])
        dv = jnp.where(is_first, vmeta, dv3_all[2 * T:3 * T] + vcar_ref[...])
        dqkv_ref[...] = jnp.concatenate(dqs + [dk, dv], axis=1).astype(dqkv_ref.dtype)
        kcar_ref[...] = dk3_all[T:2 * T]
        vcar_ref[...] = dv3_all[T:2 * T]

    blk = lambda width, col: pl.BlockSpec((T, width), lambda s: (rc(s), col))
    prev = lambda width, col: pl.BlockSpec((T, width), lambda s: (jnp.maximum(rc(s) - 1, 0), col))
    first = lambda width, col: pl.BlockSpec((T, width), lambda s: (0, col))
    return _call(
        body, "attn_bwd", (n_chunks,),
        [blk(ATTN_W, P_Q // ATTN_W), blk(KV_W, kb), prev(KV_W, kb), first(KV_W, kb),
         blk(KV_W, vb), prev(KV_W, vb), first(KV_W, vb), pl.BlockSpec((1, 128), lambda s: (0, 0)),
         blk(ATTN_W, 0), blk(128, 0), blk(ATTN_W, 0), ANY],
        [blk(QKV_W, P_Q // QKV_W), pl.BlockSpec((1, 128), lambda s: (0, 0))],
        [jax.ShapeDtypeStruct(dp.shape, dp.dtype), jax.ShapeDtypeStruct((1, 128), F32)],
        [p, p, p, p, p, p, p, sinks, ao, lse, dao, dp],
        scratch_shapes=[pltpu.VMEM((T, KV_W), F32)] * 4, aliases={11: 0}, bg=bg)


def _pad_lanes(v, width=128):
    return jnp.pad(v, ((0, 0), (0, width - v.shape[1])))


def _local_step(x, head, tgt, plan):
    w, g, run = plan.w, plan.g, plan.run
    n_tok = x.shape[0]
    n_rows = n_tok + T
    n_chunks = n_rows // T
    tm = _row_tile(n_rows, 384)
    dt_bias, a_log, d_skip = (_pad_lanes(w[k]) for k in ("ssm_dt_bias", "ssm_a_log", "ssm_d_skip"))
    sinks = _pad_lanes(w["attn_sinks"])
    x_in = [(x, D_MODEL, 0, "prev"), (head, D_MODEL, 0, "first")]

    def h0_tile(r0, xt, hd):
        return jnp.where(r0 < T, hd, xt)

    n1, = _rowwise("norm_pre_mix", lambda r0, xt, hd, wn: [_rms(h0_tile(r0, xt, hd), wn)], n_rows, T,
                   x_in, [w["norm_pre_mix"]], [(D_MODEL, BF16)], [])
    p = _matmul("in_proj", n1, w["w_cat"], "nn", F32)
    y_ssd, hin = run("ssd_fwd", _ssd_fwd, p, w["ssm_conv_w"], w["ssm_conv_b"], dt_bias, a_log, d_skip, n_chunks)
    ao, lse = run("attn_fwd", _attn_fwd, p, sinks, n_chunks)

    def gate_norm(r0, y, z, wn):
        return [_rms(y * _silu(z), wn)]

    yn, = run("ssm_gate_norm", _rowwise, "ssm_gate_norm", gate_norm, n_rows, tm,
              [(y_ssd, D_INNER, 0), (p, D_INNER, P_Z // D_INNER)], [w["ssm_norm"]], [(D_INNER, BF16)], [])
    y_ssm = _matmul("ssm_out", yn, w["w_ssm_out"], "nn", F32)
    y_attn = _matmul("attn_out", ao, w["w_attn_out"], "nn", F32)

    def mix_gate(r0, ys, ya, gs, ga):
        return [_sigmoid(gs) * ys + _sigmoid(ga) * ya]

    gate_ins = [(p, D_MODEL, P_GATE // D_MODEL), (p, D_MODEL, P_GATE // D_MODEL + 1)]
    mixed, = _rowwise("mix_gate", mix_gate, n_rows, tm, [(y_ssm, D_MODEL, 0), (y_attn, D_MODEL, 0)] + gate_ins,
                      [], [(D_MODEL, BF16)], [])
    mix = _matmul("mix_out", mixed, w["w_mix_out"], "nn", F32)

    def post_mix(r0, mx, xt, hd, w_post, w_pre):
        h1 = jnp.where(_valid_rows(r0, mx.shape[0], PAD), h0_tile(r0, xt, hd) + _rms(mx, w_post), 0.0)
        return [h1, _rms(h1, w_pre)]

    h1, n2 = _rowwise("post_mix", post_mix, n_rows, T, [(mix, D_MODEL, 0)] + x_in,
                      [w["norm_post_mix"], w["norm_pre_ffn"]], [(D_MODEL, F32), (D_MODEL, BF16)], [])
    u_raw = _matmul("ffn_up", n2, w["w_ffn_up"], "nn", F32)
    f = _ffn_act("ffn_act", u_raw, w["ffn_conv_w"], w["ffn_conv_b"], n_rows)
    ffn = _matmul("ffn_down", f, w["w_ffn_down"], "nn", F32)

    def final(r0, fo, h, t, w_post):
        real = r0 >= T
        err = jnp.where(real, h + _rms(fo, w_post) - t, 0.0)
        dy = err * (1.0 / D_MODEL)
        dffn, dw = _rms_bwd(dy, fo, w_post)
        return [dffn, dy, jnp.sum(err * err, axis=0, keepdims=True), dw]

    dffn, dh2, loss_cols, g_norm_post_ffn = _rowwise(
        "loss_head", final, n_rows, T, [(ffn, D_MODEL, 0), (h1, D_MODEL, 0), (tgt, D_MODEL, 0, "prev")],
        [w["norm_post_ffn"]], [(D_MODEL, BF16), (D_MODEL, F32)], [D_MODEL, D_MODEL])

    g["norm_post_ffn"] = g_norm_post_ffn
    g["w_ffn_down"] = _matmul("ffn_down_dw", f, dffn, "tn", F32)
    df = _matmul("ffn_down_dx", dffn, w["w_ffn_down"], "nt", F32)
    du_raw, dconv = _conv_bwd("ffn_act_bwd", u_raw, 0, [df], [(0, c0) for c0 in range(0, FFN_DIM, CONV_LANES)],
                              w["ffn_conv_w"], w["ffn_conv_b"], n_rows, True)
    g["ffn_conv_w"], g["ffn_conv_b"] = dconv[0:3], dconv[3:4]
    g["w_ffn_up"] = _matmul("ffn_up_dw", n2, du_raw, "tn", F32)
    dn2 = run("ffn_up_dx", _matmul, "ffn_up_dx", du_raw, w["w_ffn_up"], "nt", F32)

    def post_mix_bwd(r0, dn, d2, h, mx, w_pre, w_post):
        dx, dw_pre = _rms_bwd(dn, h, w_pre)
        dh1 = jnp.where(_valid_rows(r0, dn.shape[0], PAD), dx + d2, 0.0)
        dmix, dw_post = _rms_bwd(dh1, mx, w_post)
        return [dh1, dmix, dw_pre, dw_post]

    dh1, dmix, g["norm_pre_ffn"], g["norm_post_mix"] = _rowwise(
        "post_mix_bwd", post_mix_bwd, n_rows, tm,
        [(dn2, D_MODEL, 0), (dh2, D_MODEL, 0), (h1, D_MODEL, 0), (mix, D_MODEL, 0)],
        [w["norm_pre_ffn"], w["norm_post_mix"]], [(D_MODEL, F32), (D_MODEL, BF16)], [D_MODEL, D_MODEL])
    g["w_mix_out"] = _matmul("mix_out_dw", mixed, dmix, "tn", F32)
    dmixed = _matmul("mix_out_dx", dmix, w["w_mix_out"], "nt", F32)

    def mix_gate_bwd(r0, dm, ys, ya, gs, ga):
        ss, sa = _sigmoid(gs), _sigmoid(ga)
        dgate = jnp.concatenate([dm * ys * ss * (1.0 - ss), dm * ya * sa * (1.0 - sa)], axis=1)
        return [dm * ss, dm * sa, dgate]

    dys, dya, dp = _rowwise(
        "mix_gate_bwd", mix_gate_bwd, n_rows, tm,
        [(dmixed, D_MODEL, 0), (y_ssm, D_MODEL, 0), (y_attn, D_MODEL, 0)] + gate_ins,
        [], [(D_MODEL, BF16), (D_MODEL, BF16), (2 * D_MODEL, BF16, "new", P_W, P_GATE // (2 * D_MODEL))], [])
    g["w_ssm_out"] = _matmul("ssm_out_dw", yn, dys, "tn", F32)
    dyn = _matmul("ssm_out_dx", dys, w["w_ssm_out"], "nt", F32)
    g["w_attn_out"] = _matmul("attn_out_dw", ao, dya, "tn", F32)
    dao = _matmul("attn_out_dx", dya, w["w_attn_out"], "nt", BF16)

    def gate_norm_bwd(r0, dn, y, z, wn):
        sz, dsz = _silu_grad(z)
        dyz, dw = _rms_bwd(dn, y * sz, wn)
        live = _valid_rows(r0, dn.shape[0], PAD)
        return [jnp.where(live, dyz * sz, 0.0), jnp.where(live, dyz * y * dsz, 0.0), dw]

    dy_ssd, dp, g["ssm_norm"] = run(
        "ssm_gate_norm_bwd", _rowwise, "ssm_gate_norm_bwd", gate_norm_bwd, n_rows, tm,
        [(dyn, D_INNER, 0), (y_ssd, D_INNER, 0), (p, D_INNER, P_Z // D_INNER)],
        [w["ssm_norm"]], [(D_INNER, F32), (D_INNER, BF16, "into", dp, P_Z // D_INNER)], [D_INNER])
    dp, dsink = run("attn_bwd", _attn_bwd, p, sinks, ao, lse, dao, dp, n_chunks)
    g["attn_sinks"] = dsink[:, 0:ATTN_HEADS]
    dxs, dbm, dcm, dp, dpar = run("ssd_bwd", _ssd_bwd, p, w["ssm_conv_w"], w["ssm_conv_b"], dt_bias, a_log,
                                  d_skip, hin, dy_ssd, dp, n_chunks)
    g["ssm_dt_bias"], g["ssm_a_log"], g["ssm_d_skip"] = (dpar[i:i + 1, 0:SSM_HEADS] for i in range(3))
    x_chunks = [(src, c0) for src, arr in enumerate((dxs, dbm, dcm)) for c0 in range(0, arr.shape[1], CONV_LANES)]
    dp, dconv = run("ssm_conv_bwd", _conv_bwd, "ssm_conv_bwd", p, P_XBC // CONV_DIM, [dxs, dbm, dcm], x_chunks,
                    w["ssm_conv_w"], w["ssm_conv_b"], n_rows, False, into=dp, into_blk=P_XBC // CONV_DIM)
    g["ssm_conv_w"], g["ssm_conv_b"] = dconv[0:4], dconv[4:5]
    g["w_cat_t"] = _matmul("in_proj_dw", dp, n1, "tn", F32)
    dn1 = run("in_proj_dx", _matmul, "in_proj_dx", dp, w["w_cat"], "nt", F32)

    def pre_mix_bwd(r0, dn, d1, xt, hd, wn):
        dx, dw = _rms_bwd(dn, h0_tile(r0, xt, hd), wn)
        dh0 = jnp.where(_valid_rows(r0, dn.shape[0], PAD), dx + d1, 0.0)
        return [dh0, dh0, dw]

    dx_out, dhead, g["norm_pre_mix"] = _rowwise(
        "pre_mix_bwd", pre_mix_bwd, n_rows, T, [(dn1, D_MODEL, 0), (dh1, D_MODEL, 0)] + x_in,
        [w["norm_pre_mix"]], [(D_MODEL, F32, "prev", n_tok), (D_MODEL, F32, "first")], [D_MODEL])
    return jnp.sum(loss_cols), dx_out, dhead


_IN_SECTIONS = [((5152, 6176), P_Q), ((6176, 6432), P_K), ((6432, 6688), P_V), ((5120, 5152), P_DT),
                ((0, 2048), P_Z), ((6688, 8736), P_GATE), ((2048, 5120), P_XBC)]


IN_SHARD = N_IN // 4


def _shard_pieces(a, b):
    return [(j, max(a, j * IN_SHARD) - j * IN_SHARD, min(b, (j + 1) * IN_SHARD) - j * IN_SHARD)
            for j in range(4) if max(a, j * IN_SHARD) < min(b, (j + 1) * IN_SHARD)]


def _to_cat(w4):
    parts, at = [], 0
    for (a, b), off in _IN_SECTIONS:
        if off > at:
            parts.append(jnp.zeros((w4.shape[1], off - at), w4.dtype))
        parts += [w4[j, :, lo:hi] for j, lo, hi in _shard_pieces(a, b)]
        at = off + (b - a)
    return jnp.concatenate(parts, axis=1)


def _from_cat_t(g_cat_t):
    shards = [[] for _ in range(4)]
    for (a, b), off in sorted(_IN_SECTIONS):
        for j, lo, hi in _shard_pieces(a, b):
            start = off + j * IN_SHARD + lo - a
            shards[j].append(g_cat_t[start:start + hi - lo])
    return jnp.stack([jnp.concatenate(s, axis=0) for s in shards])


LANES = 1024
_BIG = [("w_in", 1024, 2184, "chip"), ("w_ssm_out", 512, 1024, "row"), ("w_attn_out", 256, 1024, "row"),
        ("w_mix_out", 256, 1024, "row"), ("w_ffn_up", 1024, 1408, "col"), ("w_ffn_down", 704, 1024, "row"),
        ("small", 32, LANES, "chip")]
_SMALL_SHARDED = [("ssm_conv_w", (4, 768), 1), ("ffn_conv_w", (3, 1408), 1), ("meta_tokens", (16, 256), 1)]
_REPLICATED = [("norm_pre_mix", 1024), ("ssm_conv_b", 3072), ("ssm_dt_bias", 32), ("ssm_a_log", 32),
               ("ssm_d_skip", 32), ("ssm_norm", 2048), ("attn_sinks", 16), ("norm_post_mix", 1024),
               ("norm_pre_ffn", 1024), ("ffn_conv_b", 5632), ("norm_post_ffn", 1024)]
SMALL_ROWS = 24


def _rep_rows():
    out, at = [], 0
    for _, width in _REPLICATED:
        out.append((at, -(-width // LANES)))
        at += out[-1][1]
    return out, at


def _in_rows(parts):
    rows = [jnp.pad(a, ((0, 0), (0, -a.shape[1] % LANES))).reshape(-1, LANES) for a in parts]
    flat = jnp.concatenate(rows, axis=0)
    return jnp.pad(flat, ((0, SMALL_ROWS - flat.shape[0]), (0, 0)))
WEIGHT_ORDER = ["meta_tokens", "norm_pre_mix", "w_in", "ssm_conv_w", "ssm_conv_b", "ssm_dt_bias", "ssm_a_log",
                "ssm_d_skip", "ssm_norm", "w_ssm_out", "attn_sinks", "w_attn_out", "w_mix_out", "norm_post_mix",
                "norm_pre_ffn", "w_ffn_up", "ffn_conv_w", "ffn_conv_b", "w_ffn_down", "norm_post_ffn"]


def _flatten(parts, rows):
    flat = jnp.concatenate([a.reshape(-1) for a in parts])
    return jnp.pad(flat, (0, rows * LANES - flat.shape[0])).reshape(rows, LANES)


def _unflatten(flat, shapes):
    flat = flat.reshape(-1)
    out, off = [], 0
    for shp in shapes:
        n = math.prod(shp)
        out.append(flat[off:off + n].reshape(shp))
        off += n
    return out


def _shard_of(full, chip, shape, axis):
    return lax.slice_in_dim(full, chip * shape[axis], (chip + 1) * shape[axis], axis=axis)


def _full_shape(r, c, layout):
    return {"row": (4 * r, c), "col": (r, 4 * c), "chip": (4, r, c), "chip_cols": (4, r, c)}[layout]


def _half_shape(r, c, layout):
    return (r, c // 2) if layout == "chip_cols" else (r // 2, c)


def _shard_view(ref, r, c, layout, chip):
    if layout == "row":
        return ref.at[pl.ds(pl.multiple_of(chip * r, 16), r), :]
    if layout == "col":
        return ref.at[:, pl.ds(pl.multiple_of(chip * c, 128), c)]
    return ref.at[chip]


def _half_view(ref, r, c, layout, chip, half):
    if layout == "chip_cols":
        return ref.at[chip, :, pl.ds(pl.multiple_of(half * (c // 2), 128), c // 2)]
    hr = r // 2
    if layout == "row":
        return ref.at[pl.ds(pl.multiple_of(chip * r + half * hr, 16), hr), :]
    r0 = pl.multiple_of(half * hr, 16)
    if layout == "col":
        return ref.at[pl.ds(r0, hr), pl.ds(pl.multiple_of(chip * c, 128), c)]
    return ref.at[chip, pl.ds(r0, hr), :]


def _mesh_pos():
    return lax.axis_index("x"), lax.axis_index("y"), lax.axis_index("c")


def _other_chips(x, y):
    return [(1 - x, y), (x, 1 - y), (1 - x, 1 - y)]


def _chip_index(x, y):
    return 2 * x + y


def _run_exchange(name, ex):
    n_in, n_out = len(ex.ins), len(ex.out_shapes)

    def body(*refs):
        in_refs, out_refs = refs[:n_in], refs[n_in:n_in + n_out]
        send_sems, recv_sems = refs[n_in + n_out:]
        copies = [pltpu.make_async_remote_copy(src_ref=s, dst_ref=d, send_sem=send_sems.at[i], recv_sem=recv_sems.at[i],
                                               device_id=dev, device_id_type=MESH)
                  for i, (s, d, dev) in enumerate(ex.make_copies(in_refs, out_refs))]
        assert len(copies) == ex.n_copies
        for cp in copies:
            cp.start()
        for cp in copies:
            cp.wait()

    return pl.pallas_call(
        body, name=name, in_specs=[ANY] * n_in, out_specs=[ANY] * n_out, out_shape=list(ex.out_shapes),
        scratch_shapes=[pltpu.SemaphoreType.DMA((ex.n_copies,)), pltpu.SemaphoreType.DMA((ex.n_copies,))],
        compiler_params=pltpu.CompilerParams(has_side_effects=True),
    )(*ex.ins)


def _join(*exs):
    def make(in_refs, out_refs):
        copies, i0, o0 = [], 0, 0
        for ex in exs:
            copies += ex.make_copies(in_refs[i0:i0 + len(ex.ins)], out_refs[o0:o0 + len(ex.out_shapes)])
            i0, o0 = i0 + len(ex.ins), o0 + len(ex.out_shapes)
        return copies

    aliases, i0, o0 = {}, 0, 0
    for ex in exs:
        aliases.update({i0 + k: o0 + v for k, v in ex.aliases.items()})
        i0, o0 = i0 + len(ex.ins), o0 + len(ex.out_shapes)
    return _Exchange([a for ex in exs for a in ex.ins], [s for ex in exs for s in ex.out_shapes], make,
                     sum(ex.n_copies for ex in exs), aliases)


def _split(exs, results):
    out, o0 = [], 0
    for ex in exs:
        out.append(list(results[o0:o0 + len(ex.out_shapes)]))
        o0 += len(ex.out_shapes)
    return out


def _gather_ici(entries, shards):
    def make(in_refs, out_refs):
        x, y, c = _mesh_pos()
        j = _chip_index(x, y)
        copies = []
        for ref_in, ref_out, (_, r, cc, lay) in zip(in_refs, out_refs, entries):
            copies.append((ref_in, _shard_view(ref_out, r, cc, lay, j), None))
            mine = ref_in.at[pl.ds(pl.multiple_of(c * (r // 2), 16), r // 2), :]
            copies += [(mine, _half_view(ref_out, r, cc, lay, j, c), (*ch, c)) for ch in _other_chips(x, y)]
        return copies

    shapes = [jax.ShapeDtypeStruct(_full_shape(r, cc, lay), s.dtype) for s, (_, r, cc, lay) in zip(shards, entries)]
    return _Exchange(list(shards), shapes, make, 4 * len(entries))


def _gather_pass_on(entries, fulls):
    def make(in_refs, out_refs):
        x, y, c = _mesh_pos()
        copies = []
        for ref, (_, r, cc, lay) in zip(out_refs, entries):
            for ch in _other_chips(x, y):
                landed = _half_view(ref, r, cc, lay, _chip_index(*ch), c)
                copies.append((landed, landed, (x, y, 1 - c)))
        return copies

    return _Exchange(list(fulls), [jax.ShapeDtypeStruct(f.shape, f.dtype) for f in fulls], make, 3 * len(entries),
                     {a: a for a in range(len(entries))})


def _gather_weights(entries, shards):
    n = len(entries)

    def body(*refs):
        ins, outs = refs[:n], refs[n:2 * n]
        send_sems, recv_sems, local_sems = refs[2 * n:]
        x, y, c = _mesh_pos()
        j = _chip_index(x, y)
        sibling = (x, y, 1 - c)
        chips = _other_chips(x, y)
        idx = [_chip_index(*ch) for ch in chips]

        def remote(k, src, dst, dev):
            return pltpu.make_async_remote_copy(src_ref=src, dst_ref=dst, send_sem=send_sems.at[k],
                                                recv_sem=recv_sems.at[k], device_id=dev, device_id_type=MESH)

        own = [pltpu.make_async_copy(ins[a], _shard_view(outs[a], r, cc, lay, j), local_sems.at[a])
               for a, (_, r, cc, lay) in enumerate(entries)]
        for cp in own:
            cp.start()
        first, passed = [], []
        for a, (_, r, cc, lay) in enumerate(entries):
            mine = ins[a].at[pl.ds(pl.multiple_of(c * (r // 2), 16), r // 2), :]
            for k, ch in enumerate(chips):
                first.append(remote(6 * a + k, mine, _half_view(outs[a], r, cc, lay, j, c), (*ch, c)))
                landed = _half_view(outs[a], r, cc, lay, idx[k], c)
                passed.append(remote(6 * a + 3 + k, landed, landed, sibling))
        for cp in first:
            cp.start()
        for a, (_, r, cc, lay) in enumerate(entries):
            for k in range(3):
                landed = _half_view(outs[a], r, cc, lay, idx[k], c)
                remote(6 * a + k, landed, landed, sibling).wait_recv()
                passed[3 * a + k].start()
        for a, (_, r, cc, lay) in enumerate(entries):
            for k in range(3):
                theirs = _half_view(outs[a], r, cc, lay, idx[k], 1 - c)
                remote(6 * a + 3 + k, theirs, theirs, sibling).wait_recv()
        for cp in first + passed:
            cp.wait_send()
        for cp in own:
            cp.wait()

    return pl.pallas_call(
        body, name="gather_weights", in_specs=[ANY] * n, out_specs=[ANY] * n,
        out_shape=[jax.ShapeDtypeStruct(_full_shape(r, cc, lay), s.dtype) for s, (_, r, cc, lay) in zip(shards, entries)],
        scratch_shapes=[pltpu.SemaphoreType.DMA((6 * n,)), pltpu.SemaphoreType.DMA((6 * n,)), pltpu.SemaphoreType.DMA((n,))],
        compiler_params=pltpu.CompilerParams(has_side_effects=True),
    )(*shards)


def _pair_exchange(entries, grads):
    def make(in_refs, out_refs):
        x, y, c = _mesh_pos()
        return [(_half_view(ref_in, r, cc, lay, i, 1 - c), ref_out.at[i], (x, y, 1 - c))
                for ref_in, ref_out, (_, r, cc, lay) in zip(in_refs, out_refs, entries) for i in range(4)]

    return _Exchange(list(grads), [jax.ShapeDtypeStruct((4,) + _half_shape(r, cc, lay), F32) for _, r, cc, lay in entries],
                     make, 4 * len(entries))


def _whole_to_sibling(arrays):
    def make(in_refs, out_refs):
        x, y, c = _mesh_pos()
        return [(r, o, (x, y, 1 - c)) for r, o in zip(in_refs, out_refs)]

    return _Exchange(list(arrays), [jax.ShapeDtypeStruct(a.shape, a.dtype) for a in arrays], make, len(arrays))


def _chip_exchange(psends):
    def make(in_refs, out_refs):
        x, y, c = _mesh_pos()
        return [(ref_in.at[_chip_index(*ch)], ref_out.at[k], (*ch, c))
                for ref_in, ref_out in zip(in_refs, out_refs) for k, ch in enumerate(_other_chips(x, y))]

    return _Exchange(list(psends), [jax.ShapeDtypeStruct((3,) + p.shape[1:], p.dtype) for p in psends], make,
                     3 * len(psends))


def _to_all_chips(array):
    def make(in_refs, out_refs):
        x, y, c = _mesh_pos()
        return [(in_refs[0], out_refs[0].at[k], (*ch, c)) for k, ch in enumerate(_other_chips(x, y))]

    return _Exchange([array], [jax.ShapeDtypeStruct((3,) + array.shape, array.dtype)], make, 3)


SUM_ROWS = 256
ADAM_ROWS = 128


def _pair_sum(name, grad, recv, ids, r, c, layout):
    hr, c = _half_shape(r, c, layout)
    tr = _row_tile(hr, SUM_ROWS)
    nb = hr // tr

    def body(ids_ref, g_ref, r_ref, send_ref, own_ref):
        s = g_ref[...] + r_ref[...]
        send_ref[...] = s.astype(send_ref.dtype)

        @pl.when(pl.program_id(1) == ids_ref[1])
        def _():
            own_ref[...] = s

    if layout == "row":
        g_spec = pl.BlockSpec((tr, c), lambda t, j, ids_ref: ((j * r + ids_ref[0] * hr) // tr + t, 0))
    elif layout == "col":
        g_spec = pl.BlockSpec((tr, c), lambda t, j, ids_ref: (ids_ref[0] * nb + t, j))
    elif layout == "chip_cols":
        g_spec = pl.BlockSpec((None, tr, c), lambda t, j, ids_ref: (j, t, ids_ref[0]))
    else:
        g_spec = pl.BlockSpec((None, tr, c), lambda t, j, ids_ref: (j, ids_ref[0] * nb + t, 0))
    grid_spec = pltpu.PrefetchScalarGridSpec(
        num_scalar_prefetch=1, grid=(nb, 4),
        in_specs=[g_spec, pl.BlockSpec((None, tr, c), lambda t, j, ids_ref: (j, t, 0))],
        out_specs=[pl.BlockSpec((None, tr, c), lambda t, j, ids_ref: (j, t, 0)),
                   pl.BlockSpec((tr, c), lambda t, j, ids_ref: (t, 0))])
    return pl.pallas_call(
        body, name=name, grid_spec=grid_spec,
        out_shape=[jax.ShapeDtypeStruct((4, hr, c), BF16), jax.ShapeDtypeStruct((hr, c), F32)],
        compiler_params=_cparams(2),
    )(ids, grad, recv)


def _chip_sum(name, own, recv):
    hr, c = own.shape
    tr = _row_tile(hr, SUM_ROWS)

    def body(o_ref, r_ref, out_ref):
        out_ref[...] = ((o_ref[...] + r_ref[0].astype(F32)) + r_ref[1].astype(F32)) + r_ref[2].astype(F32)

    return pl.pallas_call(
        body, name=name, grid=(hr // tr,),
        in_specs=[pl.BlockSpec((tr, c), lambda i: (i, 0)), pl.BlockSpec((3, tr, c), lambda i: (0, i, 0))],
        out_specs=pl.BlockSpec((tr, c), lambda i: (i, 0)),
        out_shape=jax.ShapeDtypeStruct((hr, c), F32), compiler_params=_cparams(1),
    )(own, recv)


def _chip_sum_small(own, recv, ids):
    def body(ids_ref, o_ref, r_ref, out_ref):
        j = ids_ref[1]
        total = None
        for i in range(4):
            m = jnp.bitwise_xor(i, j)
            term = jnp.where(m == 0, o_ref[...], jnp.where(m == 2, r_ref[0], jnp.where(m == 1, r_ref[1], r_ref[2])))
            total = term if total is None else total + term
        out_ref[...] = total

    grid_spec = pltpu.PrefetchScalarGridSpec(
        num_scalar_prefetch=1, grid=(1,),
        in_specs=[pl.BlockSpec(own.shape, lambda i, ids_ref: (0, 0)), pl.BlockSpec(recv.shape, lambda i, ids_ref: (0, 0, 0))],
        out_specs=pl.BlockSpec(own.shape, lambda i, ids_ref: (0, 0)))
    return pl.pallas_call(body, name="chip_sum_small", grid_spec=grid_spec,
                          out_shape=jax.ShapeDtypeStruct(own.shape, F32), compiler_params=_cparams(1))(ids, own, recv)


def _adamw(name, w, m, v, mine, theirs, ids):
    lead = (None,) * (w.ndim - 2)
    rows, cols = w.shape[-2:]
    half = rows // 2
    tr = _row_tile(half, ADAM_ROWS, unit=8)
    nb = half // tr
    c1 = 1.0 / (1.0 - ADAM_B1 ** ADAM_STEP)
    c2 = 1.0 / (1.0 - ADAM_B2 ** ADAM_STEP)

    def body(ids_ref, w_ref, m_ref, v_ref, mine_ref, theirs_ref, g_out, d_out, m_out, v_out):
        g = jnp.where(pl.program_id(0) == ids_ref[0], mine_ref[...], theirs_ref[...])
        m_new = ADAM_B1 * m_ref[...] + (1.0 - ADAM_B1) * g
        v_new = ADAM_B2 * v_ref[...] + (1.0 - ADAM_B2) * (g * g)
        d_out[...] = -ADAM_LR * ((m_new * c1) / (jnp.sqrt(v_new * c2) + ADAM_EPS) + ADAM_WD * w_ref[...])
        g_out[...] = g
        m_out[...] = m_new
        v_out[...] = v_new

    full = pl.BlockSpec(lead + (tr, cols), lambda h, i, ids_ref: (0,) * len(lead) + (h * nb + i, 0))
    part = pl.BlockSpec((tr, cols), lambda h, i, ids_ref: (i, 0))
    grid_spec = pltpu.PrefetchScalarGridSpec(num_scalar_prefetch=1, grid=(2, nb),
                                             in_specs=[full, full, full, part, part], out_specs=[full] * 4)
    return pl.pallas_call(
        body, name=name, grid_spec=grid_spec,
        out_shape=[jax.ShapeDtypeStruct(w.shape, F32)] * 4, compiler_params=_cparams(2),
    )(ids, w, m, v, mine, theirs)


def _adamw_whole(name, w, m, v, g):
    rows, cols = w.shape[-2:]
    tr = _row_tile(rows, 2 * ADAM_ROWS, unit=8)
    c1 = 1.0 / (1.0 - ADAM_B1 ** ADAM_STEP)
    c2 = 1.0 / (1.0 - ADAM_B2 ** ADAM_STEP)

    def body(w_ref, m_ref, v_ref, g_ref, g_out, d_out, m_out, v_out):
        g = g_ref[...]
        m_new = ADAM_B1 * m_ref[...] + (1.0 - ADAM_B1) * g
        v_new = ADAM_B2 * v_ref[...] + (1.0 - ADAM_B2) * (g * g)
        d_out[...] = -ADAM_LR * ((m_new * c1) / (jnp.sqrt(v_new * c2) + ADAM_EPS) + ADAM_WD * w_ref[...])
        g_out[...] = g
        m_out[...] = m_new
        v_out[...] = v_new

    full = pl.BlockSpec((None, tr, cols), lambda i: (0, i, 0))
    return pl.pallas_call(
        body, name=name, grid=(rows // tr,), in_specs=[full, full, full, pl.BlockSpec((tr, cols), lambda i: (i, 0))],
        out_specs=[full] * 4, out_shape=[jax.ShapeDtypeStruct(w.shape, F32)] * 4, compiler_params=_cparams(1),
    )(w, m, v, g)


def _adamw_replicated(g_rows, ws, ms, vs):
    n = len(ws)
    layout, _ = _rep_rows()
    c1 = 1.0 / (1.0 - ADAM_B1 ** ADAM_STEP)
    c2 = 1.0 / (1.0 - ADAM_B2 ** ADAM_STEP)

    def body(g_ref, *refs):
        w_refs, m_refs, v_refs = refs[0:n], refs[n:2 * n], refs[2 * n:3 * n]
        outs = refs[3 * n:]
        for k, (r0, rows) in enumerate(layout):
            width = w_refs[k].shape[1]
            g = jnp.concatenate([g_ref[r0 + j:r0 + j + 1, :] for j in range(rows)], axis=1)[:, 0:width]
            m_new = ADAM_B1 * m_refs[k][...] + (1.0 - ADAM_B1) * g
            v_new = ADAM_B2 * v_refs[k][...] + (1.0 - ADAM_B2) * (g * g)
            outs[k][...] = g
            outs[n + k][...] = -ADAM_LR * ((m_new * c1) / (jnp.sqrt(v_new * c2) + ADAM_EPS) + ADAM_WD * w_refs[k][...])
            outs[2 * n + k][...] = m_new
            outs[3 * n + k][...] = v_new

    res = pl.pallas_call(body, name="adamw_replicated",
                         out_shape=[jax.ShapeDtypeStruct(w.shape, F32) for _ in range(4) for w in ws])(g_rows, *ws, *ms, *vs)
    return [res[k * n:(k + 1) * n] for k in range(4)]


def _small_shard(parts):
    return _flatten(parts, _BIG[-1][1])


_ENTRY = {e[0]: e for e in _BIG}
_GRAD_ENTRY = {**_ENTRY, "w_in": ("w_in", IN_SHARD, D_MODEL, "chip_cols")}
FFN_MATS = ("w_ffn_down", "w_ffn_up")
MIXER_MATS = ("w_mix_out", "w_ssm_out", "w_attn_out")


class _StepPlan:
    def __init__(self, w, late_shards, shards, ids):
        self.w, self.g = w, {}
        self.late_shards, self.shards, self.ids = late_shards, shards, ids
        self.sums, self.halves, self.results = {}, {}, {}

    def run(self, name, fn, *args, **kw):
        at = getattr(self, "_at_" + name, None)
        if at is None:
            return fn(*args, **kw)
        exchange, landed = at()
        res, extra = fn(*args, bg=exchange, **kw)
        landed(extra)
        return res

    def _at_ssd_fwd(self):
        def landed(fulls):
            self.partly_gathered = fulls

        return _gather_ici([_ENTRY[n] for n in MIXER_MATS], [self.late_shards[n] for n in MIXER_MATS]), landed

    def _at_attn_fwd(self):
        stages = (_gather_pass_on([_ENTRY[n] for n in MIXER_MATS], self.partly_gathered),
                  _gather_ici([_ENTRY[n] for n in FFN_MATS], [self.late_shards[n] for n in FFN_MATS]))

        def landed(extra):
            mixer, self.partly_gathered = _split(stages, extra)
            self.w.update(zip(MIXER_MATS, mixer))

        return _join(*stages), landed

    def _at_ssm_gate_norm(self):
        return (_gather_pass_on([_ENTRY[n] for n in FFN_MATS], self.partly_gathered),
                lambda fulls: self.w.update(zip(FFN_MATS, fulls)))

    def pair_sums(self, names, grads, recv):
        for n, gr, rv in zip(names, grads, recv):
            _, r, c, lay = _GRAD_ENTRY[n]
            self.sums[n] = _pair_sum("pair_sum_" + n, gr, rv, self.ids, r, c, lay)

    def chip_sums(self, names, recv):
        for n, rv in zip(names, recv):
            self.halves[n] = _chip_sum("chip_sum_" + n, self.sums[n][1], rv)

    def adamw(self, names, theirs):
        for n, th in zip(names, theirs):
            sh = self.shards[n]
            if n == "w_in":
                mine_first = self.ids[0] == 0
                g_t = jnp.where(mine_first, jnp.concatenate([self.halves[n], th], axis=1),
                                jnp.concatenate([th, self.halves[n]], axis=1))
                res = _adamw_whole("adamw_" + n, *[jnp.swapaxes(sh[k], -1, -2) for k in ("w", "m", "v")], g_t)
                self.results[n] = [jnp.swapaxes(r, -1, -2) for r in res]
            else:
                self.results[n] = _adamw("adamw_" + n, sh["w"], sh["m"], sh["v"], self.halves[n], th, self.ids)

    def _pair_stage(self, names, grads):
        return (_pair_exchange([_GRAD_ENTRY[n] for n in names], grads),
                lambda recv: self.pair_sums(names, grads, recv))

    def _at_ffn_up_dx(self):
        return self._pair_stage(FFN_MATS, [self.g[n] for n in FFN_MATS])

    def _at_ssm_gate_norm_bwd(self):
        return self._pair_stage(MIXER_MATS, [self.g[n] for n in MIXER_MATS])

    def _at_attn_bwd(self):
        return _chip_exchange([self.sums[n][0] for n in FFN_MATS]), lambda recv: self.chip_sums(FFN_MATS, recv)

    def _at_ssd_bwd(self):
        stages = (_chip_exchange([self.sums[n][0] for n in MIXER_MATS]),
                  _whole_to_sibling([self.halves[n] for n in FFN_MATS]))

        def landed(extra):
            recv, theirs = _split(stages, extra)
            self.chip_sums(MIXER_MATS, recv)
            self.adamw(FFN_MATS, theirs)

        return _join(*stages), landed

    def _at_ssm_conv_bwd(self):
        return _whole_to_sibling([self.halves[n] for n in MIXER_MATS]), lambda theirs: self.adamw(MIXER_MATS, theirs)

    def _at_in_proj_dx(self):
        grads = [_from_cat_t(self.g.pop("w_cat_t"))]
        self.pair_sums(("w_in",), grads,
                       _run_exchange("grad_pair_exchange_w_in", _pair_exchange([_GRAD_ENTRY["w_in"]], grads)))
        return _chip_exchange([self.sums["w_in"][0]]), lambda recv: self.chip_sums(("w_in",), recv)

    def finish(self, g_small, g_rep, rep_shards):
        stages = (_pair_exchange([_ENTRY["small"]], [g_small]), _whole_to_sibling([g_rep]))
        recv_small, recv_rep = _split(stages, _run_exchange("grad_pair_exchange_tail", _join(*stages)))
        self.pair_sums(("small",), [g_small], recv_small)
        p_rep, = _rowwise("pair_sum_replicated", lambda r0, a, b: [a + b], SMALL_ROWS, SMALL_ROWS,
                          [(g_rep, LANES, 0), (recv_rep[0], LANES, 0)], [], [(LANES, F32)], [])
        stages = (_chip_exchange([self.sums["small"][0]]), _to_all_chips(p_rep))
        recv, recv_rep = _split(stages, _run_exchange("grad_chip_exchange_tail", _join(*stages)))
        self.chip_sums(("small",), recv)
        g_rep_tot = _chip_sum_small(p_rep, recv_rep[0], self.ids)
        last = ("w_in", "small")
        self.adamw(last, _run_exchange("grad_half_share_tail", _whole_to_sibling([self.halves[n] for n in last])))
        self.results["replicated"] = _adamw_replicated(g_rep_tot, rep_shards["w"], rep_shards["m"], rep_shards["v"])
        return g_rep_tot[_rep_rows()[1], 0]


def kernel(x, meta_tokens, norm_pre_mix, w_in, ssm_conv_w, ssm_conv_b, ssm_dt_bias, ssm_a_log, ssm_d_skip, ssm_norm, w_ssm_out, attn_sinks, w_attn_out, w_mix_out, norm_post_mix, norm_pre_ffn, w_ffn_up, ffn_conv_w, ffn_conv_b, w_ffn_down, norm_post_ffn, loss_target, m_meta_tokens, m_norm_pre_mix, m_w_in, m_ssm_conv_w, m_ssm_conv_b, m_ssm_dt_bias, m_ssm_a_log, m_ssm_d_skip, m_ssm_norm, m_w_ssm_out, m_attn_sinks, m_w_attn_out, m_w_mix_out, m_norm_post_mix, m_norm_pre_ffn, m_w_ffn_up, m_ffn_conv_w, m_ffn_conv_b, m_w_ffn_down, m_norm_post_ffn, v_meta_tokens, v_norm_pre_mix, v_w_in, v_ssm_conv_w, v_ssm_conv_b, v_ssm_dt_bias, v_ssm_a_log, v_ssm_d_skip, v_ssm_norm, v_w_ssm_out, v_attn_sinks, v_w_attn_out, v_w_mix_out, v_norm_post_mix, v_norm_pre_ffn, v_w_ffn_up, v_ffn_conv_w, v_ffn_conv_b, v_w_ffn_down, v_norm_post_ffn):
    args = dict(locals())
    squeeze = lambda a: a.reshape(a.shape[-2:])
    wts = {n: squeeze(args[n]) for n in WEIGHT_ORDER}
    mom = {n: squeeze(args["m_" + n]) for n in WEIGHT_ORDER}
    var = {n: squeeze(args["v_" + n]) for n in WEIGHT_ORDER}
    x_i, y_i, c_i = _mesh_pos()
    ids = jnp.stack([c_i, _chip_index(x_i, y_i)]).astype(jnp.int32)
    big_names = [n for n, _, _, _ in _BIG[:-1]]
    small_names = [n for n, _, _ in _SMALL_SHARDED]
    rep_names = [n for n, _ in _REPLICATED]

    stacks = {"w": wts, "m": mom, "v": var}
    shards = {n: {"w": args[n], "m": args["m_" + n], "v": args["v_" + n]} for n in big_names}
    shards["small"] = {k: _small_shard([d[n] for n in small_names]) for k, d in stacks.items()}
    rep_shards = {k: [d[n] for n in rep_names] for k, d in stacks.items()}

    w_in4, small_all = _gather_weights([_ENTRY["w_in"], _ENTRY["small"]], [wts["w_in"].astype(BF16), shards["small"]["w"]])
    w = {n: wts[n] for n in rep_names}
    w["w_cat"] = _to_cat(w_in4)
    small_parts = [_unflatten(small_all[i], [shp for _, shp, _ in _SMALL_SHARDED]) for i in range(4)]
    for k, (n, _, axis) in enumerate(_SMALL_SHARDED):
        w[n] = jnp.concatenate([small_parts[i][k] for i in range(4)], axis=axis)
    plan = _StepPlan(w, {n: wts[n].astype(BF16) for n in MIXER_MATS + FFN_MATS}, shards, ids)

    head = jnp.concatenate([jnp.zeros((PAD, D_MODEL), F32), w["meta_tokens"]], axis=0)
    loss_sum, dx, dhead = _local_step(x[0], head, loss_target[0], plan)
    g = plan.g
    g["meta_tokens"] = dhead[PAD:]
    g_small = jnp.stack([_small_shard([_shard_of(g[n], i, shp, ax) for n, shp, ax in _SMALL_SHARDED]) for i in range(4)])
    loss_part = (loss_sum * (0.5 / D_MODEL)).reshape(1, 1)
    loss = plan.finish(g_small, _in_rows([g[n] for n in rep_names] + [loss_part]), rep_shards)

    results = {}
    for kind in range(4):
        results.update({(kind, n): plan.results[n][kind] for n in big_names})
        parts = _unflatten(plan.results["small"][kind], [shp for _, shp, _ in _SMALL_SHARDED])
        results.update({(kind, n): parts[k] for k, n in enumerate(small_names)})
        results.update({(kind, n): plan.results["replicated"][kind][k] for k, n in enumerate(rep_names)})
    outs = [results[kind, n].reshape(args[n].shape) for kind in range(4) for n in WEIGHT_ORDER]
    return (loss, dx[None], *outs)
```

```python
import math
from typing import Any, Callable, NamedTuple, Sequence

import jax
import jax.numpy as jnp
from jax import lax
from jax.experimental import pallas as pl
from jax.experimental.pallas import tpu as pltpu

F32 = jnp.float32
BF16 = jnp.bfloat16

D_MODEL = 1024
N_META = 16
T = 128
PAD = T - N_META
D_INNER = 2048
SSM_HEADS = 32
HEAD_P = 64
SSM_GROUPS = 4
GROUP_W = D_INNER // SSM_GROUPS
D_STATE = 128
CONV_DIM = D_INNER + 2 * SSM_GROUPS * D_STATE
ATTN_HEADS = 16
KV_HEADS = 4
ATTN_W = 1024
KV_W = 256
FFN_DIM = 2816
N_IN = 8736
EPS = 1e-6
NEG = -1e30
SCALE = 0.125

P_Q, P_K, P_V, P_DT, P_Z, P_GATE, P_XBC = 0, 1024, 1280, 1536, 2048, 4096, 6144
QKV_W = 1536
P_W = 9216

ADAM_LR, ADAM_B1, ADAM_B2, ADAM_EPS, ADAM_WD, ADAM_STEP = 0.001, 0.9, 0.999, 1e-08, 0.01, 10

VMEM_BUDGET = 40 * 1024 * 1024
VMEM_LIMIT = 56 * 1024 * 1024
MESH = pl.DeviceIdType.MESH
ANY = pl.BlockSpec(memory_space=pl.ANY)


def _cparams(n_axes, **kw):
    return pltpu.CompilerParams(dimension_semantics=("arbitrary",) * n_axes, vmem_limit_bytes=VMEM_LIMIT, **kw)


class _Exchange(NamedTuple):
    ins: Sequence[Any]
    out_shapes: Sequence[Any]
    make_copies: Callable
    n_copies: int
    aliases: dict = {}


def _call(body, name, grid, in_specs, out_specs, out_shape, operands, scratch_shapes=(), aliases=None, bg=None):
    aliases = dict(aliases or {})
    if bg is None:
        return pl.pallas_call(body, name=name, grid=grid, in_specs=in_specs, out_specs=out_specs, out_shape=out_shape,
                              scratch_shapes=list(scratch_shapes), input_output_aliases=aliases,
                              compiler_params=_cparams(len(grid)))(*operands)
    n_in, n_out, n_scr = len(in_specs), len(out_specs), len(scratch_shapes)
    nb_in, nb_out = len(bg.ins), len(bg.out_shapes)

    def hosted(*refs):
        ins, bg_ins = refs[:n_in], refs[n_in:n_in + nb_in]
        outs = refs[n_in + nb_in:n_in + nb_in + n_out]
        bg_outs = refs[n_in + nb_in + n_out:n_in + nb_in + n_out + nb_out]
        scratch = refs[n_in + nb_in + n_out + nb_out:n_in + nb_in + n_out + nb_out + n_scr]
        send_sems, recv_sems = refs[-2:]
        pids = [pl.program_id(a) for a in range(len(grid))]
        first, last = pids[0] == 0, pids[0] == grid[0] - 1
        for p, g in zip(pids[1:], grid[1:]):
            first, last = first & (p == 0), last & (p == g - 1)
        copies = []
        for k, (src, dst, peer) in enumerate(bg.make_copies(bg_ins, bg_outs)):
            if peer is None:
                copies.append(pltpu.make_async_copy(src, dst, send_sems.at[k]))
            else:
                copies.append(pltpu.make_async_remote_copy(src_ref=src, dst_ref=dst, send_sem=send_sems.at[k],
                                                           recv_sem=recv_sems.at[k], device_id=peer, device_id_type=MESH))
        assert len(copies) == bg.n_copies

        @pl.when(first)
        def _():
            for cp in copies:
                cp.start()

        body(*ins, *outs, *scratch)

        @pl.when(last)
        def _():
            for cp in copies:
                cp.wait()

    aliases = {(k if k < n_in else k + nb_in): v for k, v in aliases.items()}
    aliases.update({n_in + k: n_out + v for k, v in bg.aliases.items()})
    res = pl.pallas_call(
        hosted, name=name, grid=grid, in_specs=list(in_specs) + [ANY] * nb_in, out_specs=list(out_specs) + [ANY] * nb_out,
        out_shape=list(out_shape) + list(bg.out_shapes), input_output_aliases=aliases,
        scratch_shapes=list(scratch_shapes) + [pltpu.SemaphoreType.DMA((bg.n_copies,))] * 2,
        compiler_params=_cparams(len(grid), has_side_effects=True))(*operands, *bg.ins)
    return res[:n_out], res[n_out:]


def _sigmoid(x):
    return 1.0 / (1.0 + jnp.exp(-x))


def _silu(x):
    return x * _sigmoid(x)


def _silu_grad(x):
    s = _sigmoid(x)
    return x * s, s * (1.0 + x * (1.0 - s))


def _dsilu(x):
    return _silu_grad(x)[1]


def _softplus(x):
    e = jnp.exp(-jnp.abs(x))
    small = e * (1.0 - e * (0.5 - e * (1.0 / 3.0)))
    return jnp.maximum(x, 0.0) + jnp.where(e < 0.01, small, jnp.log(1.0 + e))


def _rms(x, w):
    r = lax.rsqrt(jnp.mean(x * x, axis=-1, keepdims=True) + EPS)
    return x * r * w


def _rms_bwd(dy, x, w):
    r = lax.rsqrt(jnp.mean(x * x, axis=-1, keepdims=True) + EPS)
    xh = x * r
    g = dy * w
    dx = r * (g - xh * jnp.mean(g * xh, axis=-1, keepdims=True))
    dw = jnp.sum(dy * xh, axis=0, keepdims=True)
    return dx, dw


def _dot(a, b):
    return jnp.dot(a, b, preferred_element_type=F32)


def _dot_nt(a, b):
    return lax.dot_general(a, b, (((1,), (1,)), ((), ())), preferred_element_type=F32)


def _dot_tn(a, b):
    return lax.dot_general(a, b, (((0,), (0,)), ((), ())), preferred_element_type=F32)


def _split3(x):
    hi = x.astype(BF16)
    r = x - hi.astype(F32)
    mid = r.astype(BF16)
    lo = (r - mid.astype(F32)).astype(BF16)
    return hi, mid, lo


def _xdot(x, e):
    hi, mid, lo = _split3(x)
    return _dot(hi, e) + _dot(mid, e) + _dot(lo, e)


def _xdot_l(e, x):
    hi, mid, lo = _split3(x)
    return _dot(e, hi) + _dot(e, mid) + _dot(e, lo)


def _iota(shape, dim):
    return lax.broadcasted_iota(jnp.int32, shape, dim)


def _divisors(n, unit):
    return [t for t in range(unit, n + 1, unit) if n % t == 0]


MIN_MATMUL_STEPS = 8


def _matmul_tiles(m, n, k, a_bytes, b_bytes, o_bytes, m_unit):
    best = None
    for tm in _divisors(m, m_unit):
        for tn in _divisors(n, 128):
            for tk in _divisors(k, 128):
                acc = 0 if tk == k else tm * tn * 4
                vm = 2 * (tm * tk * a_bytes + tk * tn * b_bytes + tm * tn * o_bytes) + acc
                if vm > VMEM_BUDGET:
                    continue
                steps = (m // tm) * (n // tn) * (k // tk)
                score = (tk == k, min(steps, MIN_MATMUL_STEPS), min(tm, 256), tm * tn * tk)
                if best is None or score > best[0]:
                    best = (score, (tm, tn, tk))
    return best[1]


def _matmul(name, a, b, mode, out_dtype, bg=None):
    if mode == "nn":
        (m, k), n = a.shape, b.shape[1]
    elif mode == "nt":
        (m, k), n = a.shape, b.shape[0]
    else:
        (k, m), n = a.shape, b.shape[1]
    ab, bb, ob = a.dtype.itemsize, b.dtype.itemsize, jnp.dtype(out_dtype).itemsize
    tm, tn, tk = _matmul_tiles(m, n, k, ab, bb, ob, 128 if mode == "tn" else 16)
    nk = k // tk
    dot = {"nn": _dot, "nt": _dot_nt, "tn": _dot_tn}[mode]

    def body(a_ref, b_ref, o_ref, *scratch):
        prod = dot(a_ref[...].astype(BF16), b_ref[...].astype(BF16))
        if nk == 1:
            o_ref[...] = prod.astype(o_ref.dtype)
        else:
            acc_ref, = scratch
            kk = pl.program_id(2)

            @pl.when(kk == 0)
            def _():
                acc_ref[...] = prod

            @pl.when(kk > 0)
            def _():
                acc_ref[...] += prod

            @pl.when(kk == nk - 1)
            def _():
                o_ref[...] = acc_ref[...].astype(o_ref.dtype)

    a_spec = pl.BlockSpec((tk, tm), lambda i, j, kk: (kk, i)) if mode == "tn" else pl.BlockSpec((tm, tk), lambda i, j, kk: (i, kk))
    b_spec = pl.BlockSpec((tn, tk), lambda i, j, kk: (j, kk)) if mode == "nt" else pl.BlockSpec((tk, tn), lambda i, j, kk: (kk, j))
    res = _call(body, name, (m // tm, n // tn, nk), [a_spec, b_spec], [pl.BlockSpec((tm, tn), lambda i, j, kk: (i, j))],
                [jax.ShapeDtypeStruct((m, n), out_dtype)], [a, b],
                scratch_shapes=[] if nk == 1 else [pltpu.VMEM((tm, tn), F32)], bg=bg)
    return res[0] if bg is None else (res[0][0], res[1])


def _row_tile(n_rows, cap, unit=16):
    return max([t for t in _divisors(n_rows, unit) if t <= cap], default=n_rows)


ROW_SUB = 384
GROUP_UNROLL = 4


def _rowwise(name, fn, n_rows, tm, row_ins, full_ins, row_outs, acc_outs, bg=None):
    n_in = len(row_ins) + len(full_ins)
    n_ro = len(row_outs)
    into = [(k, o[3]) for k, o in enumerate(row_outs) if len(o) > 2 and o[2] == "into"]

    sub = min(tm, ROW_SUB)
    counts = [tm // T if len(e) > 3 and e[3] == "prev" else 1 for e in row_ins]
    assert all(cnt == 1 for cnt in counts) or sub == tm
    starts = [sum(counts[:k]) for k in range(len(counts))]
    n_row_in = sum(counts)
    n_in = n_row_in + len(full_ins)

    def body(*refs):
        i = pl.program_id(0)
        outs = refs[n_in + len(into):]

        sums = tuple(jnp.zeros((1, w), F32) for w in acc_outs)
        for s in range(tm // sub):
            rows = pl.ds(s * sub, sub)
            vals = [refs[st][rows, :] if cnt == 1 else jnp.concatenate([refs[st + k][...] for k in range(cnt)], axis=0)
                    for st, cnt in zip(starts, counts)]
            vals += [r[...] for r in refs[n_row_in:n_in]]
            res = fn(i * tm + s * sub, *vals)
            for o, r, v in zip(row_outs, outs[:n_ro], res[:n_ro]):
                if len(o) > 2 and o[2] == "first":
                    @pl.when(i == 0)
                    def _(r=r, v=v, rows=rows):
                        r[rows, :] = v.astype(r.dtype)
                else:
                    r[rows, :] = v.astype(r.dtype)
            sums = tuple(a + v for a, v in zip(sums, res[n_ro:]))

        @pl.when(i == 0)
        def _():
            for r, v in zip(outs[n_ro:], sums):
                r[...] = v

        @pl.when(i > 0)
        def _():
            for r, v in zip(outs[n_ro:], sums):
                r[...] += v

    def in_spec(entry, cnt):
        w, cb = entry[1], entry[2]
        if len(entry) > 3 and entry[3] == "prev":
            return [pl.BlockSpec((tm // cnt, w), lambda i, k=k: (jnp.maximum(cnt * i - 1 + k, 0), cb)) for k in range(cnt)]
        if len(entry) > 3 and entry[3] == "first":
            return [pl.BlockSpec((tm, w), lambda i: (0, cb))]
        return [pl.BlockSpec((tm, w), lambda i: (i, cb))]

    def out_spec(o):
        if len(o) == 2:
            return pl.BlockSpec((tm, o[0]), lambda i: (i, 0)), jax.ShapeDtypeStruct((n_rows, o[0]), o[1])
        if o[2] == "new":
            return pl.BlockSpec((tm, o[0]), lambda i: (i, o[4])), jax.ShapeDtypeStruct((n_rows, o[3]), o[1])
        if o[2] == "into":
            return pl.BlockSpec((tm, o[0]), lambda i: (i, o[4])), jax.ShapeDtypeStruct(o[3].shape, o[3].dtype)
        if o[2] == "first":
            return pl.BlockSpec((tm, o[0]), lambda i: (0, 0)), jax.ShapeDtypeStruct((tm, o[0]), o[1])
        return pl.BlockSpec((tm, o[0]), lambda i: (jnp.maximum(i - 1, 0), 0)), jax.ShapeDtypeStruct((o[3], o[0]), o[1])

    in_specs = [s for e, cnt in zip(row_ins, counts) for s in in_spec(e, cnt)]
    in_specs += [pl.BlockSpec(a.shape, lambda i: (0, 0)) for a in full_ins]
    in_specs += [pl.BlockSpec(memory_space=pl.ANY) for _ in into]
    specs_shapes = [out_spec(o) for o in row_outs]
    out_specs = [s for s, _ in specs_shapes] + [pl.BlockSpec((1, w), lambda i: (0, 0)) for w in acc_outs]
    out_shape = [s for _, s in specs_shapes] + [jax.ShapeDtypeStruct((1, w), F32) for w in acc_outs]
    return _call(body, name, (n_rows // tm,), in_specs, out_specs, out_shape,
                 [e[0] for e, cnt in zip(row_ins, counts) for _ in range(cnt)] + list(full_ins) + [arr for _, arr in into],
                 aliases={n_in + a: k for a, (k, _) in enumerate(into)}, bg=bg)


def _valid_rows(first_row, tm, lo):
    return (first_row + _iota((tm, 1), 0)) >= lo


CONV_ROWS = 128
CONV_SUB = 16
CONV_LANES = 256


def _conv_specs(tm, width, blk, n_rows, after):
    specs = [pl.BlockSpec((tm, width), lambda i: (i, blk)),
             pl.BlockSpec((8, width), lambda i: (jnp.maximum(i * (tm // 8) - 1, 0), blk))]
    if after:
        specs.append(pl.BlockSpec((16, width), lambda i: (jnp.minimum((i + 1) * (tm // 16), n_rows // 16 - 1), blk)))
    return specs


def _conv_window(win, w_ref, b_ref, taps, c0, cw, n):
    acc = b_ref[:, c0:c0 + cw] + w_ref[taps - 1:taps, c0:c0 + cw] * win[8:8 + n]
    for k in range(taps - 1):
        acc = acc + w_ref[k:k + 1, c0:c0 + cw] * win[8 - (taps - 1) + k:8 - (taps - 1) + k + n]
    return acc


def _ffn_act(name, u_raw, conv_w, conv_b, n_rows):
    tm, sub, cw = CONV_ROWS, CONV_SUB, CONV_LANES
    taps, width = conv_w.shape
    half = width // 2

    def body(cur_ref, prev_ref, w_ref, b_ref, f_ref, ext_ref):
        i = pl.program_id(0)
        ext_ref[0:8, :] = jnp.where(i > 0, prev_ref[...], 0.0)
        ext_ref[8:8 + tm, :] = cur_ref[...]
        for q in range(half // cw):
            a0, g0 = q * cw, half + q * cw

            def group(s, carry):
                r = pl.multiple_of(s * sub, sub)
                a = _conv_window(ext_ref[pl.ds(r, sub + 8), a0:a0 + cw], w_ref, b_ref, taps, a0, cw, sub)
                g = _conv_window(ext_ref[pl.ds(r, sub + 8), g0:g0 + cw], w_ref, b_ref, taps, g0, cw, sub)
                f_ref[pl.ds(r, sub), a0:a0 + cw] = (_silu(a) * g).astype(f_ref.dtype)
                return carry

            lax.fori_loop(0, tm // sub, group, 0, unroll=GROUP_UNROLL)

        @pl.when(i == 0)
        def _():
            f_ref[0:PAD, :] = jnp.zeros((PAD, half), f_ref.dtype)

    return pl.pallas_call(
        body, name=name, grid=(n_rows // tm,),
        in_specs=_conv_specs(tm, width, 0, n_rows, False) + [pl.BlockSpec((taps, width), lambda i: (0, 0)),
                                                             pl.BlockSpec((1, width), lambda i: (0, 0))],
        out_specs=pl.BlockSpec((tm, half), lambda i: (i, 0)),
        out_shape=jax.ShapeDtypeStruct((n_rows, half), BF16),
        scratch_shapes=[pltpu.VMEM((tm + 8, width), F32)],
        compiler_params=_cparams(1),
    )(u_raw, u_raw, conv_w, conv_b)


def _conv_bwd(name, raw, raw_blk, dsrcs, chunk_src, conv_w, conv_b, n_rows, gated, into=None, into_blk=0, bg=None):
    taps, width = conv_w.shape
    half = width // 2 if gated else width
    tm, sub, cw = CONV_ROWS, CONV_SUB, CONV_LANES
    te = tm + 16
    nd = len(dsrcs)
    n_parts = 2 if gated else 1

    def body(*refs):
        cur_ref, prev_ref, next_ref = refs[0:3]
        dcur, dnext = refs[3:3 + nd], refs[3 + nd:3 + 2 * nd]
        w_ref, b_ref = refs[3 + 2 * nd:5 + 2 * nd]
        out_ref, acc_ref, ext_ref, du_ref = refs[-4:]
        i = pl.program_id(0)
        ext_ref[0:8, :] = jnp.where(i > 0, prev_ref[...], 0.0)
        ext_ref[8:8 + tm, :] = cur_ref[...]
        ext_ref[8 + tm:24 + tm, :] = next_ref[...]

        for q, (src, off) in enumerate(chunk_src):
            cols = [q * cw, half + q * cw][:n_parts]

            def conv_grad(r, d, past_end):
                pre = [_conv_window(ext_ref[pl.ds(r, sub + 8), c0:c0 + cw], w_ref, b_ref, taps, c0, cw, sub) for c0 in cols]
                if gated:
                    act, dact = _silu_grad(pre[0])
                    dus = [d * pre[1] * dact, d * act]
                else:
                    dus = [d * _dsilu(pre[0])]
                for part, du in enumerate(dus):
                    if past_end:
                        du = jnp.where(i * tm + r + _iota((sub, 1), 0) < n_rows, du, 0.0)
                    du_ref[part, pl.ds(r, sub), :] = du

            def tile_rows(s, carry):
                r = pl.multiple_of(s * sub, sub)
                conv_grad(r, dcur[src][pl.ds(r, sub), off:off + cw].astype(F32), False)
                return carry

            lax.fori_loop(0, tm // sub, tile_rows, 0, unroll=GROUP_UNROLL)
            conv_grad(tm, dnext[src][:, off:off + cw].astype(F32), True)

            @pl.when(i == 0)
            def _():
                du_ref[:, 0:PAD, :] = jnp.zeros((n_parts, PAD, cw), F32)

            for part, c0 in enumerate(cols):
                taps_w = [w_ref[k:k + 1, c0:c0 + cw] for k in range(taps)]

                def back(s, sums):
                    new = list(sums)
                    for u in range(2):
                        r = pl.multiple_of((2 * s + u) * sub, sub)
                        win = du_ref[part, pl.ds(r, sub + 8), :]
                        raw_rows = ext_ref[pl.ds(8 + r, sub), c0:c0 + cw]
                        draw = jnp.zeros((sub, cw), F32)
                        for k in range(taps):
                            shifted = win[taps - 1 - k:taps - 1 - k + sub]
                            draw = draw + taps_w[k] * shifted
                            new[k] = new[k] + shifted * raw_rows
                        new[taps] = new[taps] + win[0:sub]
                        out_ref[pl.ds(r, sub), c0:c0 + cw] = draw.astype(out_ref.dtype)
                    return tuple(new)

                sums = lax.fori_loop(0, tm // (2 * sub), back, tuple(jnp.zeros((sub, cw), F32) for _ in range(taps + 1)))

                @pl.when(i == 0)
                def _(c0=c0):
                    out_ref[PAD - sub:PAD, c0:c0 + cw] = jnp.zeros((sub, cw), out_ref.dtype)

                for k in range(taps + 1):
                    total = jnp.sum(sums[k], axis=0, keepdims=True)
                    acc_ref[k:k + 1, c0:c0 + cw] = jnp.where(i == 0, total, acc_ref[k:k + 1, c0:c0 + cw] + total)

    in_specs = _conv_specs(tm, width, raw_blk, n_rows, True)
    in_specs += [pl.BlockSpec((tm, d.shape[1]), lambda i: (i, 0)) for d in dsrcs]
    in_specs += [pl.BlockSpec((16, d.shape[1]), lambda i: (jnp.minimum((i + 1) * (tm // 16), n_rows // 16 - 1), 0)) for d in dsrcs]
    in_specs += [pl.BlockSpec((taps, width), lambda i: (0, 0)), pl.BlockSpec((1, width), lambda i: (0, 0))]
    operands = [raw, raw, raw] + list(dsrcs) + list(dsrcs) + [conv_w, conv_b]
    aliases = {}
    if into is None:
        out0 = jax.ShapeDtypeStruct((n_rows, width), BF16)
    else:
        in_specs.append(pl.BlockSpec(memory_space=pl.ANY))
        operands.append(into)
        aliases = {len(operands) - 1: 0}
        out0 = jax.ShapeDtypeStruct(into.shape, into.dtype)
    return _call(body, name, (n_rows // tm,), in_specs,
                 [pl.BlockSpec((tm, width), lambda i: (i, into_blk)), pl.BlockSpec((8, width), lambda i: (0, 0))],
                 [out0, jax.ShapeDtypeStruct((8, width), F32)], operands,
                 scratch_shapes=[pltpu.VMEM((tm + 24, width), F32), pltpu.VMEM((n_parts, te + 8, cw), F32)],
                 aliases=aliases, bg=bg)


def _ssd_specs(n_chunks, rev, per_step=1):
    cidx = (lambda c: n_chunks - 1 - c) if rev else (lambda c: c)
    xw, nw = per_step * GROUP_W, per_step * D_STATE
    xg0, bg0, cg0 = P_XBC // xw, (P_XBC + D_INNER) // nw, (P_XBC + D_INNER + SSM_GROUPS * D_STATE) // nw

    def cur(width, blk0):
        return pl.BlockSpec((T, width), lambda g, c: (cidx(c), blk0 + g))

    def prev(width, blk0):
        return pl.BlockSpec((8, width), lambda g, c: (jnp.maximum(cidx(c) * (T // 8) - 1, 0), blk0 + g))

    specs = [cur(xw, xg0), prev(xw, xg0), cur(nw, bg0), prev(nw, bg0), cur(nw, cg0), prev(nw, cg0),
             pl.BlockSpec((T, 128), lambda g, c: (cidx(c), P_DT // 128))]
    wb, wc = D_INNER // nw, (D_INNER + SSM_GROUPS * D_STATE) // nw
    specs += [pl.BlockSpec((4, xw), lambda g, c: (0, g)),
              pl.BlockSpec((4, nw), lambda g, c: (0, wb + g)),
              pl.BlockSpec((4, nw), lambda g, c: (0, wc + g)),
              pl.BlockSpec((1, xw), lambda g, c: (0, g)),
              pl.BlockSpec((1, nw), lambda g, c: (0, wb + g)),
              pl.BlockSpec((1, nw), lambda g, c: (0, wc + g))]
    specs += [pl.BlockSpec((1, 128), lambda g, c: (0, 0))] * 3
    return specs, cidx


def _ssd_shared(refs, c):
    dt_ref, dtb_ref, alog_ref = refs[6], refs[13], refs[14]
    valid = _valid_rows(c * T, T, PAD)
    dtr = dt_ref[...] + dtb_ref[...]
    dt = jnp.where(valid, _softplus(dtr), 0.0)
    a_neg = -jnp.exp(alog_ref[...])
    tril = _iota((T, T), 0) >= _iota((T, T), 1)
    cs = _xdot_l(tril.astype(BF16), dt * a_neg)
    return dict(valid=valid, dtr=dtr, dt=dt, a_neg=a_neg, tril=tril, cs=cs, cs_t=cs.T)


def _heads_of_lanes():
    hh_t, ll_t = _iota((D_INNER, 128), 1), _iota((D_INNER, 128), 0)
    return (hh_t == jnp.right_shift(ll_t, 6)).astype(BF16)


def _ssd_chunk_forward(refs, ext_ref, g, c, shared):
    (xc_ref, xp_ref, bc_ref, bp_ref, cc_ref, cp_ref, dt_ref, wx_ref, wb_ref, wc_ref,
     bx_ref, bb_ref, bcb_ref, dtb_ref, alog_ref, dsk_ref) = refs

    def conv_pre(cur_ref, prev_ref, w_ref, b_ref, width):
        ext_ref[0:8, 0:width] = jnp.where(c > 0, prev_ref[...], 0.0)
        ext_ref[8:8 + T, 0:width] = cur_ref[...]
        w = w_ref[...]
        acc = b_ref[...] + w[3:4] * cur_ref[...]
        for k in range(3):
            acc = acc + w[k:k + 1] * ext_ref[pl.ds(5 + k, T), 0:width]
        return acc

    v = dict(shared)
    valid = v["valid"]
    v["head0"] = 8 * g
    v["x_pre"] = conv_pre(xc_ref, xp_ref, wx_ref, bx_ref, GROUP_W)
    v["b_pre"] = conv_pre(bc_ref, bp_ref, wb_ref, bb_ref, D_STATE)
    v["c_pre"] = conv_pre(cc_ref, cp_ref, wc_ref, bcb_ref, D_STATE)
    xs = _silu(v["x_pre"])
    bm = jnp.where(valid, _silu(v["b_pre"]), 0.0)
    cm = jnp.where(valid, _silu(v["c_pre"]), 0.0)
    hh, ll = _iota((128, GROUP_W), 0), _iota((128, GROUP_W), 1)
    expand = (hh == 8 * g + jnp.right_shift(ll, 6)).astype(BF16)
    cs_e = _xdot(v["cs"], expand)
    dt_e = _xdot(v["dt"], expand)
    cs_last_e = cs_e[T - 1:T, :]
    v.update(xs=xs, bm=bm, cm=cm, cs_e=cs_e, dt_e=dt_e, cs_last_e=cs_last_e)
    v["xdt"] = xs * dt_e
    v["decay_e"] = jnp.exp(cs_last_e - cs_e)
    v["ecs_e"] = jnp.exp(cs_e)
    v["elast_e"] = jnp.exp(cs_last_e)
    v["d_e"] = _xdot(dsk_ref[...], expand)
    v["gmat"] = _dot_nt(cm.astype(BF16), bm.astype(BF16))
    return v


def _ssd_decay_pair(v, jp):
    out = []
    for j in (v["head0"] + 2 * jp, v["head0"] + 2 * jp + 1):
        diff = v["cs"][:, j:j + 1] - v["cs_t"][j:j + 1, :]
        out.append(jnp.where(v["tril"], jnp.exp(jnp.where(v["tril"], diff, 0.0)), 0.0))
    return out


def _block_diag_pair(xp):
    lane = _iota(xp.shape, 1)
    return jnp.concatenate([jnp.where(lane < HEAD_P, xp, 0.0), jnp.where(lane >= HEAD_P, xp, 0.0)], axis=0)


SSD_GROUPS_PER_STEP = 4


def _ssd_group_refs(refs, gg):
    x_w, n_w = pl.ds(GROUP_W * gg, GROUP_W), pl.ds(D_STATE * gg, D_STATE)
    lanes = [x_w, x_w, n_w, n_w, n_w, n_w, None, x_w, n_w, n_w, x_w, n_w, n_w, None, None, None]
    return [r if w is None else r.at[:, w] for r, w in zip(refs, lanes)]


def _ssd_fwd(p, conv_w, conv_b, dt_bias, a_log, d_skip, n_chunks, bg=None):
    n_rows = n_chunks * T
    in_specs, _ = _ssd_specs(n_chunks, rev=False, per_step=SSD_GROUPS_PER_STEP)
    per = SSD_GROUPS_PER_STEP
    assert per == SSM_GROUPS

    def body(*refs):
        y_ref, hin_ref, st_ref, ext_ref = refs[16:]
        c = pl.program_id(1)

        @pl.when(c == 0)
        def _():
            st_ref[...] = jnp.zeros_like(st_ref)

        shared = _ssd_shared(refs[:16], c)
        for gg in range(per):
            v = _ssd_chunk_forward(_ssd_group_refs(refs[:16], gg), ext_ref.at[gg], gg, c, shared)
            state = st_ref[gg]
            hin_ref[gg] = state
            ys = []
            for jp in range(4):
                l0, l1 = _ssd_decay_pair(v, jp)
                lhs = jnp.concatenate([v["gmat"] * l0, v["gmat"] * l1], axis=1).astype(BF16)
                rhs = _block_diag_pair(v["xdt"][:, 128 * jp:128 * jp + 128]).astype(BF16)
                ys.append(_dot(lhs, rhs))
            y = jnp.concatenate(ys, axis=1)
            y = y + _dot(v["cm"].astype(BF16), state.astype(BF16)) * v["ecs_e"] + v["xs"] * v["d_e"]
            y_ref[:, GROUP_W * gg:GROUP_W * gg + GROUP_W] = y
            s_new = _dot_tn(v["bm"].astype(BF16), (v["xdt"] * v["decay_e"]).astype(BF16))
            st_ref[gg] = state * v["elast_e"] + s_new

    return _call(
        body, "ssd_fwd", (SSM_GROUPS // per, n_chunks), in_specs,
        [pl.BlockSpec((T, per * GROUP_W), lambda g, c: (c, g)),
         pl.BlockSpec((per, None, D_STATE, GROUP_W), lambda g, c: (g, c, 0, 0))],
        [jax.ShapeDtypeStruct((n_rows, D_INNER), F32),
         jax.ShapeDtypeStruct((SSM_GROUPS, n_chunks, D_STATE, GROUP_W), F32)],
        [p, p, p, p, p, p, p, conv_w, conv_w, conv_w, conv_b, conv_b, conv_b, dt_bias, a_log, d_skip],
        scratch_shapes=[pltpu.VMEM((per, D_STATE, GROUP_W), F32), pltpu.VMEM((per, T + 8, GROUP_W), F32)], bg=bg)


def _ssd_bwd(p, conv_w, conv_b, dt_bias, a_log, d_skip, hin, dy, dp, n_chunks, bg=None):
    n_rows = n_chunks * T
    per = SSD_GROUPS_PER_STEP
    assert per == SSM_GROUPS
    dt_w = P_Z - P_DT
    in_specs, cidx = _ssd_specs(n_chunks, rev=True, per_step=per)
    in_specs = in_specs + [pl.BlockSpec((per, None, D_STATE, GROUP_W), lambda g, c: (g, cidx(c), 0, 0)),
                           pl.BlockSpec((T, per * GROUP_W), lambda g, c: (cidx(c), g)), ANY]

    def body(*refs):
        hin_ref, dy_ref = refs[16:18]
        dx_ref, db_ref, dc_ref, dp_ref, dpar_ref, dst_ref, ext_ref, red_ref, dd_ref = refs[19:]
        step = pl.program_id(1)
        shared = _ssd_shared(refs[:16], n_chunks - 1 - step)
        local = jnp.zeros((T, 128), F32)
        for gg in range(per):
            x_w, n_w = pl.ds(GROUP_W * gg, GROUP_W), pl.ds(D_STATE * gg, D_STATE)
            local = local + group_body(_ssd_group_refs(refs[:16], gg), hin_ref.at[gg], dy_ref.at[:, x_w],
                                       dx_ref.at[:, x_w], db_ref.at[:, n_w], dc_ref.at[:, n_w], red_ref.at[:, :, x_w],
                                       dd_ref.at[:, x_w], dst_ref.at[gg], ext_ref.at[gg], gg, shared)
        to_heads = _heads_of_lanes()
        dcs = _xdot(red_ref[0], to_heads) + local
        triu = (_iota((T, T), 0) <= _iota((T, T), 1)).astype(BF16)
        da = _xdot_l(triu, dcs)
        ddt = da * shared["a_neg"] + _xdot(red_ref[1], to_heads)
        ddtr = jnp.where(shared["valid"], ddt * _sigmoid(shared["dtr"]), 0.0)
        dp_ref[...] = jnp.concatenate([ddtr, jnp.zeros((T, dt_w - 128), F32)], axis=1).astype(dp_ref.dtype)
        dpar = jnp.concatenate([
            jnp.sum(ddtr, axis=0, keepdims=True),
            jnp.sum(da * shared["dt"], axis=0, keepdims=True) * shared["a_neg"],
            _xdot(dd_ref[0:1, :], to_heads),
            jnp.zeros((5, 128), F32)], axis=0)
        dpar_ref[...] = jnp.where(step == 0, dpar, dpar_ref[...] + dpar)

    def group_body(in_refs, hin_ref, dy_ref, dx_ref, db_ref, dc_ref, red_ref, dd_ref, dst_ref, ext_ref, g, shared):
        step = pl.program_id(1)
        c = n_chunks - 1 - step

        @pl.when(step == 0)
        def _():
            dst_ref[...] = jnp.zeros_like(dst_ref)

        v = _ssd_chunk_forward(in_refs, ext_ref, g, c, shared)
        hin_f = hin_ref[...]
        hin_b = hin_f.astype(BF16)
        dyv = dy_ref[...]
        dst = dst_ref[...]
        dst_b = dst.astype(BF16)
        xs, bm, cm, xdt = v["xs"], v["bm"], v["cm"], v["xdt"]
        bm_b, cm_b = bm.astype(BF16), cm.astype(BF16)

        dd_e = jnp.sum(dyv * xs, axis=0, keepdims=True)
        dxs = dyv * v["d_e"]
        ch = _dot(cm_b, hin_b)
        dch = (dyv * v["ecs_e"]).astype(BF16)
        dcm = _dot_nt(dch, hin_b)
        dhin = _dot_tn(cm_b, dch) + dst * v["elast_e"]
        dcs_e = dyv * ch * v["ecs_e"]
        dxd = _dot(bm_b, dst_b)
        dbm = _dot_nt((xdt * v["decay_e"]).astype(BF16), dst_b)
        dxdt_state = dxd * v["decay_e"]
        q = dxdt_state * xdt
        dcs_e = dcs_e - q
        dlast_e = jnp.sum(q, axis=0, keepdims=True) + jnp.sum(dst * hin_f, axis=0, keepdims=True) * v["elast_e"]
        dg = jnp.zeros((T, T), F32)
        rs_cols = jnp.zeros((T, 128), F32)
        cs_rows = jnp.zeros((128, T), F32)
        lane_i, sub_i = _iota((T, 128), 1), _iota((128, T), 0)
        dxdt_parts = []
        for jp in range(4):
            l0, l1 = _ssd_decay_pair(v, jp)
            m0, m1 = v["gmat"] * l0, v["gmat"] * l1
            xbd = _block_diag_pair(xdt[:, 128 * jp:128 * jp + 128]).astype(BF16)
            dyp = dyv[:, 128 * jp:128 * jp + 128]
            dm = _dot_nt(dyp.astype(BF16), xbd)
            dm0, dm1 = dm[:, 0:T], dm[:, T:2 * T]
            dg = dg + dm0 * l0 + dm1 * l1
            for j, qq in ((v["head0"] + 2 * jp, dm0 * m0), (v["head0"] + 2 * jp + 1, dm1 * m1)):
                rs_cols = jnp.where(lane_i == j, jnp.sum(qq, axis=1, keepdims=True), rs_cols)
                cs_rows = jnp.where(sub_i == j, jnp.sum(qq, axis=0, keepdims=True), cs_rows)
            mv = jnp.concatenate([m0, m1], axis=0).astype(BF16)
            dxdt_parts.append(_dot_tn(mv, _block_diag_pair(dyp).astype(BF16)))
        dxdt = jnp.concatenate(dxdt_parts, axis=1) + dxdt_state
        dg_b = dg.astype(BF16)
        dcm = dcm + _dot(dg_b, bm_b)
        dbm = dbm + _dot_tn(dg_b, cm_b)
        last_row = _iota((T, 1), 0) == T - 1
        red_ref[0] = dcs_e + jnp.where(last_row, dlast_e, 0.0)
        red_ref[1] = dxdt * xs
        dd_ref[0:1, :] = dd_e
        dx_ref[...] = dxs + dxdt * v["dt_e"]
        db_ref[...] = jnp.where(v["valid"], dbm, 0.0)
        dc_ref[...] = jnp.where(v["valid"], dcm, 0.0)
        dst_ref[...] = dhin
        return rs_cols - cs_rows.T

    return _call(
        body, "ssd_bwd", (SSM_GROUPS // per, n_chunks), in_specs,
        [pl.BlockSpec((T, per * GROUP_W), lambda g, c: (cidx(c), g)),
         pl.BlockSpec((T, per * D_STATE), lambda g, c: (cidx(c), g)),
         pl.BlockSpec((T, per * D_STATE), lambda g, c: (cidx(c), g)),
         pl.BlockSpec((T, dt_w), lambda g, c: (cidx(c), P_DT // dt_w)),
         pl.BlockSpec((8, 128), lambda g, c: (0, 0))],
        [jax.ShapeDtypeStruct((n_rows, D_INNER), F32),
         jax.ShapeDtypeStruct((n_rows, SSM_GROUPS * D_STATE), F32),
         jax.ShapeDtypeStruct((n_rows, SSM_GROUPS * D_STATE), F32),
         jax.ShapeDtypeStruct(dp.shape, dp.dtype),
         jax.ShapeDtypeStruct((8, 128), F32)],
        [p, p, p, p, p, p, p, conv_w, conv_w, conv_w, conv_b, conv_b, conv_b, dt_bias, a_log, d_skip, hin, dy, dp],
        scratch_shapes=[pltpu.VMEM((per, D_STATE, GROUP_W), F32), pltpu.VMEM((per, T + 8, GROUP_W), F32),
                        pltpu.VMEM((2, T, D_INNER), F32), pltpu.VMEM((8, D_INNER), F32)],
        aliases={18: 3}, bg=bg)


def _alibi_slope(h):
    return 2.0 ** (-8.0 * (h + 1) / ATTN_HEADS)


def _dup_half(x256, kvh):
    xb = x256[:, 128 * (kvh // 2):128 * (kvh // 2) + 128]
    rolled = pltpu.roll(xb, 64, 1)
    lane = _iota(xb.shape, 1)
    if kvh % 2 == 0:
        return jnp.where(lane < 64, xb, rolled)
    return jnp.where(lane < 64, rolled, xb)


def _attn_masks(c):
    qi, j = _iota((T, T), 0), _iota((T, T), 1)
    tri = j <= qi
    meta_ok = (j >= PAD) & (j - PAD <= c * T + qi - PAD)
    band_ok = c >= jnp.where(tri, 1, 2)
    dist = jnp.bitwise_and(qi - j, T - 1).astype(F32)
    return tri, meta_ok, band_ok, dist


def _fold(x3, tri):
    return jnp.concatenate([x3[:, 0:T], jnp.where(tri, x3[:, 2 * T:3 * T], x3[:, T:2 * T])], axis=1)


def _unfold(x2, tri):
    band = x2[:, T:2 * T]
    return jnp.concatenate([x2[:, 0:T], jnp.where(tri, 0.0, band), jnp.where(tri, band, 0.0)], axis=1)


def _attn_fwd(p, sinks, n_chunks, bg=None):
    n_rows = n_chunks * T
    kb, vb = P_K // KV_W, P_V // KV_W

    def body(q_ref, kc_ref, kp_ref, km_ref, vc_ref, vp_ref, vm_ref, sink_ref, o_ref, lse_ref):
        c = pl.program_id(0)
        sinks_v = sink_ref[...]
        masks = _attn_masks(c)
        tri, meta_ok, band_ok, dist = masks
        lane = _iota((T, 128), 1)
        for kvh in range(KV_HEADS):
            k3 = jnp.concatenate([_dup_half(r[...], kvh) for r in (km_ref, kp_ref, kc_ref)], axis=0).astype(BF16)
            v3 = jnp.concatenate([_dup_half(r[...], kvh) for r in (vm_ref, vp_ref, vc_ref)], axis=0)
            v3bd = _block_diag_rows(v3).astype(BF16)
            q2 = q_ref[:, 256 * kvh:256 * kvh + 256] * SCALE
            q4 = jnp.concatenate([jnp.where((lane < 64) if half == 0 else (lane >= 64), q2[:, 128 * pr:128 * pr + 128], 0.0)
                                  for pr in range(2) for half in range(2)], axis=0).astype(BF16)
            raw4 = _dot_nt(q4, k3)
            probs = []
            for hh in range(4):
                h = 4 * kvh + hh
                raw = raw4[T * hh:T * hh + T]
                band = jnp.where(tri, raw[:, 2 * T:3 * T], raw[:, T:2 * T]) - _alibi_slope(h) * dist
                sc = jnp.concatenate([jnp.where(meta_ok, raw[:, 0:T], NEG), jnp.where(band_ok, band, NEG)], axis=1)
                sink = sinks_v[:, h:h + 1]
                m = jnp.maximum(jnp.max(sc, axis=1, keepdims=True), sink)
                e = jnp.exp(sc - m)
                den = jnp.sum(e, axis=1, keepdims=True) + jnp.exp(sink - m)
                probs.append(_unfold(e * (1.0 / den), tri))
                lse_ref[:, h:h + 1] = m + jnp.log(den)
            p4 = jnp.concatenate([jnp.concatenate(probs[0:2], axis=1), jnp.concatenate(probs[2:4], axis=1)], axis=0)
            out = _dot(p4.astype(BF16), v3bd)
            o_ref[:, 256 * kvh:256 * kvh + 256] = jnp.concatenate([out[0:T], out[T:2 * T]], axis=1).astype(o_ref.dtype)

    blk = lambda width, col: pl.BlockSpec((T, width), lambda c: (c, col))
    prev = lambda width, col: pl.BlockSpec((T, width), lambda c: (jnp.maximum(c - 1, 0), col))
    first = lambda width, col: pl.BlockSpec((T, width), lambda c: (0, col))
    return _call(
        body, "attn_fwd", (n_chunks,),
        [blk(ATTN_W, P_Q // ATTN_W), blk(KV_W, kb), prev(KV_W, kb), first(KV_W, kb),
         blk(KV_W, vb), prev(KV_W, vb), first(KV_W, vb), pl.BlockSpec((1, 128), lambda c: (0, 0))],
        [pl.BlockSpec((T, ATTN_W), lambda c: (c, 0)), pl.BlockSpec((T, 128), lambda c: (c, 0))],
        [jax.ShapeDtypeStruct((n_rows, ATTN_W), BF16), jax.ShapeDtypeStruct((n_rows, 128), F32)],
        [p, p, p, p, p, p, p, sinks], bg=bg)


def _block_diag_rows(x3):
    lane = _iota(x3.shape, 1)
    return jnp.concatenate([jnp.where(lane < 64, x3, 0.0), jnp.where(lane >= 64, x3, 0.0)], axis=0)


def _fold_halves(x):
    return x + pltpu.roll(x, 64, 1)


def _attn_bwd(p, sinks, ao, lse, dao, dp, n_chunks, bg=None):
    kb, vb = P_K // KV_W, P_V // KV_W
    rc = lambda s: n_chunks - 1 - s

    def body(q_ref, kc_ref, kp_ref, km_ref, vc_ref, vp_ref, vm_ref, sink_ref, o_ref, lse_ref, do_ref, dp_in_ref,
             dqkv_ref, dsink_ref, kcar_ref, vcar_ref, kmeta_ref, vmeta_ref):
        step = pl.program_id(0)
        c = n_chunks - 1 - step

        @pl.when(step == 0)
        def _():
            for r in (kcar_ref, vcar_ref, kmeta_ref, vmeta_ref):
                r[...] = jnp.zeros_like(r)

        masks = _attn_masks(c)
        tri = masks[0]
        q = q_ref[...] * SCALE
        sinks_v = sink_ref[...]
        lse_v = lse_ref[...]
        ov = o_ref[...].astype(F32)
        dov = do_ref[...].astype(F32)
        lane = _iota((T, 128), 1)
        lane256 = _iota((3 * T, KV_W), 1)
        dsink = jnp.zeros((1, 128), F32)
        dk3_all = jnp.zeros((3 * T, KV_W), F32)
        dv3_all = jnp.zeros((3 * T, KV_W), F32)
        dqs = []
        for kvh in range(KV_HEADS):
            k3 = jnp.concatenate([_dup_half(r[...], kvh) for r in (km_ref, kp_ref, kc_ref)], axis=0).astype(BF16)
            v3 = jnp.concatenate([_dup_half(r[...], kvh) for r in (vm_ref, vp_ref, vc_ref)], axis=0).astype(BF16)
            halves = [(pr, half, (lane < 64) if half == 0 else (lane >= 64)) for pr in range(2) for half in range(2)]
            cols = [slice(128 * (2 * kvh + pr), 128 * (2 * kvh + pr) + 128) for pr in range(2)]
            q4 = jnp.concatenate([jnp.where(mine, q[:, cols[pr]], 0.0) for pr, _, mine in halves], axis=0).astype(BF16)
            do4 = jnp.concatenate([jnp.where(mine, dov[:, cols[pr]], 0.0) for pr, _, mine in halves], axis=0).astype(BF16)
            raw4 = _dot_nt(q4, k3)
            dp4 = _dot_nt(do4, v3)
            ds_rows, pm_rows = [], []
            for hh, (pr, half, mine) in enumerate(halves):
                h = 4 * kvh + hh
                raw = raw4[T * hh:T * hh + T]
                band = jnp.where(tri, raw[:, 2 * T:3 * T], raw[:, T:2 * T]) - _alibi_slope(h) * masks[3]
                sc = jnp.concatenate([jnp.where(masks[1], raw[:, 0:T], NEG), jnp.where(masks[2], band, NEG)], axis=1)
                lse_h = lse_v[:, h:h + 1]
                pm = jnp.exp(sc - lse_h)
                prod = dov[:, cols[pr]] * ov[:, cols[pr]]
                delta = jnp.sum(jnp.where(mine, prod, 0.0), axis=1, keepdims=True)
                dp = _fold(dp4[T * hh:T * hh + T], tri)
                ds_rows.append(_unfold(pm * (dp - delta), tri))
                pm_rows.append(_unfold(pm, tri))
                p_sink = jnp.exp(sinks_v[:, h:h + 1] - lse_h)
                dsink = jnp.where(_iota((1, 128), 1) == h, jnp.sum(-p_sink * delta, axis=0, keepdims=True), dsink)
            ds4 = jnp.concatenate(ds_rows, axis=0).astype(BF16)
            dq4 = _dot(ds4, k3)
            dk3 = _dot_tn(ds4, q4)
            dv3 = _dot_tn(jnp.concatenate(pm_rows, axis=0).astype(BF16), do4)
            for pr in range(2):
                dqs.append(jnp.where(lane < 64, dq4[2 * T * pr:2 * T * pr + T], dq4[2 * T * pr + T:2 * T * pr + 2 * T]) * SCALE)
            in_place = (lane256 >= 64 * kvh) & (lane256 < 64 * kvh + 64)
            wide = lambda x: jnp.concatenate([x, x], axis=1)
            dk3_all = jnp.where(in_place, wide(_fold_halves(dk3)), dk3_all)
            dv3_all = jnp.where(in_place, wide(_fold_halves(dv3)), dv3_all)
        dsink_all = dsink

        @pl.when(step == 0)
        def _():
            dsink_ref[...] = dsink_all

        @pl.when(step > 0)
        def _():
            dsink_ref[...] += dsink_all

        kmeta = kmeta_ref[...] + dk3_all[0:T]
        vmeta = vmeta_ref[...] + dv3_all[0:T]
        kmeta_ref[...] = kmeta
        vmeta_ref[...] = vmeta
        is_first = c == 0
        dk = jnp.where(is_first, kmeta, dk3_all[2 * T:3 * T] + kcar_ref[...])
        dv = jnp.where(is_first, vmeta, dv3_all[2 * T:3 * T] + vcar_ref[...])
        dqkv_ref[...] = jnp.concatenate(dqs + [dk, dv], axis=1).astype(dqkv_ref.dtype)
        kcar_ref[...] = dk3_all[T:2 * T]
        vcar_ref[...] = dv3_all[T:2 * T]

    blk = lambda width, col: pl.BlockSpec((T, width), lambda s: (rc(s), col))
    prev = lambda width, col: pl.BlockSpec((T, width), lambda s: (jnp.maximum(rc(s) - 1, 0), col))
    first = lambda width, col: pl.BlockSpec((T, width), lambda s: (0, col))
    return _call(
        body, "attn_bwd", (n_chunks,),
        [blk(ATTN_W, P_Q // ATTN_W), blk(KV_W, kb), prev(KV_W, kb), first(KV_W, kb),
         blk(KV_W, vb), prev(KV_W, vb), first(KV_W, vb), pl.BlockSpec((1, 128), lambda s: (0, 0)),
         blk(ATTN_W, 0), blk(128, 0), blk(ATTN_W, 0), ANY],
        [blk(QKV_W, P_Q // QKV_W), pl.BlockSpec((1, 128), lambda s: (0, 0))],
        [jax.ShapeDtypeStruct(dp.shape, dp.dtype), jax.ShapeDtypeStruct((1, 128), F32)],
        [p, p, p, p, p, p, p, sinks, ao, lse, dao, dp],
        scratch_shapes=[pltpu.VMEM((T, KV_W), F32)] * 4, aliases={11: 0}, bg=bg)


def _pad_lanes(v, width=128):
    return jnp.pad(v, ((0, 0), (0, width - v.shape[1])))


def _local_step(x, head, tgt, plan):
    w, g, run = plan.w, plan.g, plan.run
    n_tok = x.shape[0]
    n_rows = n_tok + T
    n_chunks = n_rows // T
    tm = _row_tile(n_rows, 384)
    dt_bias, a_log, d_skip = (_pad_lanes(w[k]) for k in ("ssm_dt_bias", "ssm_a_log", "ssm_d_skip"))
    sinks = _pad_lanes(w["attn_sinks"])
    x_in = [(x, D_MODEL, 0, "prev"), (head, D_MODEL, 0, "first")]
    head_tm = jnp.concatenate([head, jnp.zeros((tm - T, D_MODEL), F32)], axis=0)
    x_in_tm = [(x, D_MODEL, 0, "prev"), (head_tm, D_MODEL, 0, "first")]

    def h0_tile(r0, xt, hd):
        return jnp.where(_valid_rows(r0, xt.shape[0], T), xt, hd)

    n1, = _rowwise("norm_pre_mix", lambda r0, xt, hd, wn: [_rms(h0_tile(r0, xt, hd), wn)], n_rows, tm,
                   x_in_tm, [w["norm_pre_mix"]], [(D_MODEL, BF16)], [])
    p = _matmul("in_proj", n1, w["w_cat"], "nn", F32)
    y_ssd, hin = run("ssd_fwd", _ssd_fwd, p, w["ssm_conv_w"], w["ssm_conv_b"], dt_bias, a_log, d_skip, n_chunks)
    ao, lse = run("attn_fwd", _attn_fwd, p, sinks, n_chunks)

    def gate_norm(r0, y, z, wn):
        return [_rms(y * _silu(z), wn)]

    yn, = run("ssm_gate_norm", _rowwise, "ssm_gate_norm", gate_norm, n_rows, tm,
              [(y_ssd, D_INNER, 0), (p, D_INNER, P_Z // D_INNER)], [w["ssm_norm"]], [(D_INNER, BF16)], [])
    y_ssm = _matmul("ssm_out", yn, w["w_ssm_out"], "nn", F32)
    y_attn = _matmul("attn_out", ao, w["w_attn_out"], "nn", F32)

    def mix_gate(r0, ys, ya, gs, ga):
        return [_sigmoid(gs) * ys + _sigmoid(ga) * ya]

    gate_ins = [(p, D_MODEL, P_GATE // D_MODEL), (p, D_MODEL, P_GATE // D_MODEL + 1)]
    mixed, = _rowwise("mix_gate", mix_gate, n_rows, tm, [(y_ssm, D_MODEL, 0), (y_attn, D_MODEL, 0)] + gate_ins,
                      [], [(D_MODEL, BF16)], [])
    mix = _matmul("mix_out", mixed, w["w_mix_out"], "nn", F32)

    def post_mix(r0, mx, xt, hd, w_post, w_pre):
        h1 = jnp.where(_valid_rows(r0, mx.shape[0], PAD), h0_tile(r0, xt, hd) + _rms(mx, w_post), 0.0)
        return [h1, _rms(h1, w_pre)]

    h1, n2 = _rowwise("post_mix", post_mix, n_rows, tm, [(mix, D_MODEL, 0)] + x_in_tm,
                      [w["norm_post_mix"], w["norm_pre_ffn"]], [(D_MODEL, F32), (D_MODEL, BF16)], [])
    u_raw = _matmul("ffn_up", n2, w["w_ffn_up"], "nn", F32)
    f = _ffn_act("ffn_act", u_raw, w["ffn_conv_w"], w["ffn_conv_b"], n_rows)
    ffn = _matmul("ffn_down", f, w["w_ffn_down"], "nn", F32)

    def final(r0, fo, h, t, w_post):
        real = _valid_rows(r0, fo.shape[0], T)
        err = jnp.where(real, h + _rms(fo, w_post) - t, 0.0)
        dy = err * (1.0 / D_MODEL)
        dffn, dw = _rms_bwd(dy, fo, w_post)
        return [dffn, dy, jnp.sum(err * err, axis=0, keepdims=True), dw]

    dffn, dh2, loss_cols, g_norm_post_ffn = _rowwise(
        "loss_head", final, n_rows, tm, [(ffn, D_MODEL, 0), (h1, D_MODEL, 0), (tgt, D_MODEL, 0, "prev")],
        [w["norm_post_ffn"]], [(D_MODEL, BF16), (D_MODEL, F32)], [D_MODEL, D_MODEL])

    g["norm_post_ffn"] = g_norm_post_ffn
    g["w_ffn_down"] = _matmul("ffn_down_dw", f, dffn, "tn", F32)
    df = _matmul("ffn_down_dx", dffn, w["w_ffn_down"], "nt", F32)
    du_raw, dconv = _conv_bwd("ffn_act_bwd", u_raw, 0, [df], [(0, c0) for c0 in range(0, FFN_DIM, CONV_LANES)],
                              w["ffn_conv_w"], w["ffn_conv_b"], n_rows, True)
    g["ffn_conv_w"], g["ffn_conv_b"] = dconv[0:3], dconv[3:4]
    g["w_ffn_up"] = _matmul("ffn_up_dw", n2, du_raw, "tn", F32)
    dn2 = run("ffn_up_dx", _matmul, "ffn_up_dx", du_raw, w["w_ffn_up"], "nt", F32)

    def post_mix_bwd(r0, dn, d2, h, mx, w_pre, w_post):
        dx, dw_pre = _rms_bwd(dn, h, w_pre)
        dh1 = jnp.where(_valid_rows(r0, dn.shape[0], PAD), dx + d2, 0.0)
        dmix, dw_post = _rms_bwd(dh1, mx, w_post)
        return [dh1, dmix, dw_pre, dw_post]

    dh1, dmix, g["norm_pre_ffn"], g["norm_post_mix"] = _rowwise(
        "post_mix_bwd", post_mix_bwd, n_rows, tm,
        [(dn2, D_MODEL, 0), (dh2, D_MODEL, 0), (h1, D_MODEL, 0), (mix, D_MODEL, 0)],
        [w["norm_pre_ffn"], w["norm_post_mix"]], [(D_MODEL, F32), (D_MODEL, BF16)], [D_MODEL, D_MODEL])
    g["w_mix_out"] = _matmul("mix_out_dw", mixed, dmix, "tn", F32)
    dmixed = _matmul("mix_out_dx", dmix, w["w_mix_out"], "nt", F32)

    def mix_gate_bwd(r0, dm, ys, ya, gs, ga):
        ss, sa = _sigmoid(gs), _sigmoid(ga)
        dgate = jnp.concatenate([dm * ys * ss * (1.0 - ss), dm * ya * sa * (1.0 - sa)], axis=1)
        return [dm * ss, dm * sa, dgate]

    dys, dya, dp = _rowwise(
        "mix_gate_bwd", mix_gate_bwd, n_rows, tm,
        [(dmixed, D_MODEL, 0), (y_ssm, D_MODEL, 0), (y_attn, D_MODEL, 0)] + gate_ins,
        [], [(D_MODEL, BF16), (D_MODEL, BF16), (2 * D_MODEL, BF16, "new", P_W, P_GATE // (2 * D_MODEL))], [])
    g["w_ssm_out"] = _matmul("ssm_out_dw", yn, dys, "tn", F32)
    dyn = _matmul("ssm_out_dx", dys, w["w_ssm_out"], "nt", F32)
    g["w_attn_out"] = _matmul("attn_out_dw", ao, dya, "tn", F32)
    dao = _matmul("attn_out_dx", dya, w["w_attn_out"], "nt", BF16)

    def gate_norm_bwd(r0, dn, y, z, wn):
        sz, dsz = _silu_grad(z)
        dyz, dw = _rms_bwd(dn, y * sz, wn)
        live = _valid_rows(r0, dn.shape[0], PAD)
        return [jnp.where(live, dyz * sz, 0.0), jnp.where(live, dyz * y * dsz, 0.0), dw]

    dy_ssd, dp, g["ssm_norm"] = run(
        "ssm_gate_norm_bwd", _rowwise, "ssm_gate_norm_bwd", gate_norm_bwd, n_rows, tm,
        [(dyn, D_INNER, 0), (y_ssd, D_INNER, 0), (p, D_INNER, P_Z // D_INNER)],
        [w["ssm_norm"]], [(D_INNER, F32), (D_INNER, BF16, "into", dp, P_Z // D_INNER)], [D_INNER])
    dp, dsink = run("attn_bwd", _attn_bwd, p, sinks, ao, lse, dao, dp, n_chunks)
    g["attn_sinks"] = dsink[:, 0:ATTN_HEADS]
    dxs, dbm, dcm, dp, dpar = run("ssd_bwd", _ssd_bwd, p, w["ssm_conv_w"], w["ssm_conv_b"], dt_bias, a_log,
                                  d_skip, hin, dy_ssd, dp, n_chunks)
    g["ssm_dt_bias"], g["ssm_a_log"], g["ssm_d_skip"] = (dpar[i:i + 1, 0:SSM_HEADS] for i in range(3))
    x_chunks = [(src, c0) for src, arr in enumerate((dxs, dbm, dcm)) for c0 in range(0, arr.shape[1], CONV_LANES)]
    dp, dconv = run("ssm_conv_bwd", _conv_bwd, "ssm_conv_bwd", p, P_XBC // CONV_DIM, [dxs, dbm, dcm], x_chunks,
                    w["ssm_conv_w"], w["ssm_conv_b"], n_rows, False, into=dp, into_blk=P_XBC // CONV_DIM)
    g["ssm_conv_w"], g["ssm_conv_b"] = dconv[0:4], dconv[4:5]
    g["w_cat_t"] = _matmul("in_proj_dw", dp, n1, "tn", F32)
    dn1 = run("in_proj_dx", _matmul, "in_proj_dx", dp, w["w_cat"], "nt", F32)

    def pre_mix_bwd(r0, dn, d1, xt, hd, wn):
        dx, dw = _rms_bwd(dn, h0_tile(r0, xt, hd), wn)
        dh0 = jnp.where(_valid_rows(r0, dn.shape[0], PAD), dx + d1, 0.0)
        return [dh0, dh0, dw]

    dx_out, dhead, g["norm_pre_mix"] = _rowwise(
        "pre_mix_bwd", pre_mix_bwd, n_rows, T, [(dn1, D_MODEL, 0), (dh1, D_MODEL, 0)] + x_in,
        [w["norm_pre_mix"]], [(D_MODEL, F32, "prev", n_tok), (D_MODEL, F32, "first")], [D_MODEL])
    return jnp.sum(loss_cols), dx_out, dhead


_IN_SECTIONS = [((5152, 6176), P_Q), ((6176, 6432), P_K), ((6432, 6688), P_V), ((5120, 5152), P_DT),
                ((0, 2048), P_Z), ((6688, 8736), P_GATE), ((2048, 5120), P_XBC)]


IN_SHARD = N_IN // 4


def _shard_pieces(a, b):
    return [(j, max(a, j * IN_SHARD) - j * IN_SHARD, min(b, (j + 1) * IN_SHARD) - j * IN_SHARD)
            for j in range(4) if max(a, j * IN_SHARD) < min(b, (j + 1) * IN_SHARD)]


def _to_cat(w4):
    parts, at = [], 0
    for (a, b), off in _IN_SECTIONS:
        if off > at:
            parts.append(jnp.zeros((w4.shape[1], off - at), w4.dtype))
        parts += [w4[j, :, lo:hi] for j, lo, hi in _shard_pieces(a, b)]
        at = off + (b - a)
    return jnp.concatenate(parts, axis=1)


def _from_cat_t(g_cat_t):
    shards = [[] for _ in range(4)]
    for (a, b), off in sorted(_IN_SECTIONS):
        for j, lo, hi in _shard_pieces(a, b):
            start = off + j * IN_SHARD + lo - a
            shards[j].append(g_cat_t[start:start + hi - lo])
    return jnp.stack([jnp.concatenate(s, axis=0) for s in shards])


LANES = 1024
_BIG = [("w_in", 1024, 2184, "chip"), ("w_ssm_out", 512, 1024, "row"), ("w_attn_out", 256, 1024, "row"),
        ("w_mix_out", 256, 1024, "row"), ("w_ffn_up", 1024, 1408, "col"), ("w_ffn_down", 704, 1024, "row"),
        ("small", 32, LANES, "chip")]
_SMALL_SHARDED = [("ssm_conv_w", (4, 768), 1), ("ffn_conv_w", (3, 1408), 1), ("meta_tokens", (16, 256), 1)]
_REPLICATED = [("norm_pre_mix", 1024), ("ssm_conv_b", 3072), ("ssm_dt_bias", 32), ("ssm_a_log", 32),
               ("ssm_d_skip", 32), ("ssm_norm", 2048), ("attn_sinks", 16), ("norm_post_mix", 1024),
               ("norm_pre_ffn", 1024), ("ffn_conv_b", 5632), ("norm_post_ffn", 1024)]
SMALL_ROWS = 24


def _rep_rows():
    out, at = [], 0
    for _, width in _REPLICATED:
        out.append((at, -(-width // LANES)))
        at += out[-1][1]
    return out, at


def _in_rows(parts):
    rows = [jnp.pad(a, ((0, 0), (0, -a.shape[1] % LANES))).reshape(-1, LANES) for a in parts]
    flat = jnp.concatenate(rows, axis=0)
    return jnp.pad(flat, ((0, SMALL_ROWS - flat.shape[0]), (0, 0)))
WEIGHT_ORDER = ["meta_tokens", "norm_pre_mix", "w_in", "ssm_conv_w", "ssm_conv_b", "ssm_dt_bias", "ssm_a_log",
                "ssm_d_skip", "ssm_norm", "w_ssm_out", "attn_sinks", "w_attn_out", "w_mix_out", "norm_post_mix",
                "norm_pre_ffn", "w_ffn_up", "ffn_conv_w", "ffn_conv_b", "w_ffn_down", "norm_post_ffn"]


def _flatten(parts, rows):
    flat = jnp.concatenate([a.reshape(-1) for a in parts])
    return jnp.pad(flat, (0, rows * LANES - flat.shape[0])).reshape(rows, LANES)


def _unflatten(flat, shapes):
    flat = flat.reshape(-1)
    out, off = [], 0
    for shp in shapes:
        n = math.prod(shp)
        out.append(flat[off:off + n].reshape(shp))
        off += n
    return out


def _shard_of(full, chip, shape, axis):
    return lax.slice_in_dim(full, chip * shape[axis], (chip + 1) * shape[axis], axis=axis)


def _full_shape(r, c, layout):
    return {"row": (4 * r, c), "col": (r, 4 * c), "chip": (4, r, c), "chip_cols": (4, r, c)}[layout]


def _half_shape(r, c, layout):
    return (r, c // 2) if layout == "chip_cols" else (r // 2, c)


def _shard_view(ref, r, c, layout, chip):
    if layout == "row":
        return ref.at[pl.ds(pl.multiple_of(chip * r, 16), r), :]
    if layout == "col":
        return ref.at[:, pl.ds(pl.multiple_of(chip * c, 128), c)]
    return ref.at[chip]


def _half_view(ref, r, c, layout, chip, half):
    if layout == "chip_cols":
        return ref.at[chip, :, pl.ds(pl.multiple_of(half * (c // 2), 128), c // 2)]
    hr = r // 2
    if layout == "row":
        return ref.at[pl.ds(pl.multiple_of(chip * r + half * hr, 16), hr), :]
    r0 = pl.multiple_of(half * hr, 16)
    if layout == "col":
        return ref.at[pl.ds(r0, hr), pl.ds(pl.multiple_of(chip * c, 128), c)]
    return ref.at[chip, pl.ds(r0, hr), :]


def _mesh_pos():
    return lax.axis_index("x"), lax.axis_index("y"), lax.axis_index("c")


def _other_chips(x, y):
    return [(1 - x, y), (x, 1 - y), (1 - x, 1 - y)]


def _chip_index(x, y):
    return 2 * x + y


def _run_exchange(name, ex):
    n_in, n_out = len(ex.ins), len(ex.out_shapes)

    def body(*refs):
        in_refs, out_refs = refs[:n_in], refs[n_in:n_in + n_out]
        send_sems, recv_sems = refs[n_in + n_out:]
        copies = [pltpu.make_async_remote_copy(src_ref=s, dst_ref=d, send_sem=send_sems.at[i], recv_sem=recv_sems.at[i],
                                               device_id=dev, device_id_type=MESH)
                  for i, (s, d, dev) in enumerate(ex.make_copies(in_refs, out_refs))]
        assert len(copies) == ex.n_copies
        for cp in copies:
            cp.start()
        for cp in copies:
            cp.wait()

    return pl.pallas_call(
        body, name=name, in_specs=[ANY] * n_in, out_specs=[ANY] * n_out, out_shape=list(ex.out_shapes),
        scratch_shapes=[pltpu.SemaphoreType.DMA((ex.n_copies,)), pltpu.SemaphoreType.DMA((ex.n_copies,))],
        compiler_params=pltpu.CompilerParams(has_side_effects=True),
    )(*ex.ins)


def _join(*exs):
    def make(in_refs, out_refs):
        copies, i0, o0 = [], 0, 0
        for ex in exs:
            copies += ex.make_copies(in_refs[i0:i0 + len(ex.ins)], out_refs[o0:o0 + len(ex.out_shapes)])
            i0, o0 = i0 + len(ex.ins), o0 + len(ex.out_shapes)
        return copies

    aliases, i0, o0 = {}, 0, 0
    for ex in exs:
        aliases.update({i0 + k: o0 + v for k, v in ex.aliases.items()})
        i0, o0 = i0 + len(ex.ins), o0 + len(ex.out_shapes)
    return _Exchange([a for ex in exs for a in ex.ins], [s for ex in exs for s in ex.out_shapes], make,
                     sum(ex.n_copies for ex in exs), aliases)


def _split(exs, results):
    out, o0 = [], 0
    for ex in exs:
        out.append(list(results[o0:o0 + len(ex.out_shapes)]))
        o0 += len(ex.out_shapes)
    return out


def _gather_ici(entries, shards):
    def make(in_refs, out_refs):
        x, y, c = _mesh_pos()
        j = _chip_index(x, y)
        copies = []
        for ref_in, ref_out, (_, r, cc, lay) in zip(in_refs, out_refs, entries):
            copies.append((ref_in, _shard_view(ref_out, r, cc, lay, j), None))
            mine = ref_in.at[pl.ds(pl.multiple_of(c * (r // 2), 16), r // 2), :]
            copies += [(mine, _half_view(ref_out, r, cc, lay, j, c), (*ch, c)) for ch in _other_chips(x, y)]
        return copies

    shapes = [jax.ShapeDtypeStruct(_full_shape(r, cc, lay), s.dtype) for s, (_, r, cc, lay) in zip(shards, entries)]
    return _Exchange(list(shards), shapes, make, 4 * len(entries))


def _gather_pass_on(entries, fulls):
    def make(in_refs, out_refs):
        x, y, c = _mesh_pos()
        copies = []
        for ref, (_, r, cc, lay) in zip(out_refs, entries):
            for ch in _other_chips(x, y):
                landed = _half_view(ref, r, cc, lay, _chip_index(*ch), c)
                copies.append((landed, landed, (x, y, 1 - c)))
        return copies

    return _Exchange(list(fulls), [jax.ShapeDtypeStruct(f.shape, f.dtype) for f in fulls], make, 3 * len(entries),
                     {a: a for a in range(len(entries))})


def _gather_weights(entries, shards):
    n = len(entries)

    def body(*refs):
        ins, outs = refs[:n], refs[n:2 * n]
        send_sems, recv_sems, local_sems = refs[2 * n:]
        x, y, c = _mesh_pos()
        j = _chip_index(x, y)
        sibling = (x, y, 1 - c)
        chips = _other_chips(x, y)
        idx = [_chip_index(*ch) for ch in chips]

        def remote(k, src, dst, dev):
            return pltpu.make_async_remote_copy(src_ref=src, dst_ref=dst, send_sem=send_sems.at[k],
                                                recv_sem=recv_sems.at[k], device_id=dev, device_id_type=MESH)

        own = [pltpu.make_async_copy(ins[a], _shard_view(outs[a], r, cc, lay, j), local_sems.at[a])
               for a, (_, r, cc, lay) in enumerate(entries)]
        for cp in own:
            cp.start()
        first, passed = [], []
        for a, (_, r, cc, lay) in enumerate(entries):
            mine = ins[a].at[pl.ds(pl.multiple_of(c * (r // 2), 16), r // 2), :]
            for k, ch in enumerate(chips):
                first.append(remote(6 * a + k, mine, _half_view(outs[a], r, cc, lay, j, c), (*ch, c)))
                landed = _half_view(outs[a], r, cc, lay, idx[k], c)
                passed.append(remote(6 * a + 3 + k, landed, landed, sibling))
        for cp in first:
            cp.start()
        for a, (_, r, cc, lay) in enumerate(entries):
            for k in range(3):
                landed = _half_view(outs[a], r, cc, lay, idx[k], c)
                remote(6 * a + k, landed, landed, sibling).wait_recv()
                passed[3 * a + k].start()
        for a, (_, r, cc, lay) in enumerate(entries):
            for k in range(3):
                theirs = _half_view(outs[a], r, cc, lay, idx[k], 1 - c)
                remote(6 * a + 3 + k, theirs, theirs, sibling).wait_recv()
        for cp in first + passed:
            cp.wait_send()
        for cp in own:
            cp.wait()

    return pl.pallas_call(
        body, name="gather_weights", in_specs=[ANY] * n, out_specs=[ANY] * n,
        out_shape=[jax.ShapeDtypeStruct(_full_shape(r, cc, lay), s.dtype) for s, (_, r, cc, lay) in zip(shards, entries)],
        scratch_shapes=[pltpu.SemaphoreType.DMA((6 * n,)), pltpu.SemaphoreType.DMA((6 * n,)), pltpu.SemaphoreType.DMA((n,))],
        compiler_params=pltpu.CompilerParams(has_side_effects=True),
    )(*shards)


def _pair_exchange(entries, grads):
    def make(in_refs, out_refs):
        x, y, c = _mesh_pos()
        return [(_half_view(ref_in, r, cc, lay, i, 1 - c), ref_out.at[i], (x, y, 1 - c))
                for ref_in, ref_out, (_, r, cc, lay) in zip(in_refs, out_refs, entries) for i in range(4)]

    return _Exchange(list(grads), [jax.ShapeDtypeStruct((4,) + _half_shape(r, cc, lay), F32) for _, r, cc, lay in entries],
                     make, 4 * len(entries))


def _whole_to_sibling(arrays):
    def make(in_refs, out_refs):
        x, y, c = _mesh_pos()
        return [(r, o, (x, y, 1 - c)) for r, o in zip(in_refs, out_refs)]

    return _Exchange(list(arrays), [jax.ShapeDtypeStruct(a.shape, a.dtype) for a in arrays], make, len(arrays))


def _chip_exchange(psends):
    def make(in_refs, out_refs):
        x, y, c = _mesh_pos()
        return [(ref_in.at[_chip_index(*ch)], ref_out.at[k], (*ch, c))
                for ref_in, ref_out in zip(in_refs, out_refs) for k, ch in enumerate(_other_chips(x, y))]

    return _Exchange(list(psends), [jax.ShapeDtypeStruct((3,) + p.shape[1:], p.dtype) for p in psends], make,
                     3 * len(psends))


def _to_all_chips(array):
    def make(in_refs, out_refs):
        x, y, c = _mesh_pos()
        return [(in_refs[0], out_refs[0].at[k], (*ch, c)) for k, ch in enumerate(_other_chips(x, y))]

    return _Exchange([array], [jax.ShapeDtypeStruct((3,) + array.shape, array.dtype)], make, 3)


SUM_ROWS = 256
ADAM_ROWS = 128


def _pair_sum(name, grad, recv, ids, r, c, layout):
    hr, c = _half_shape(r, c, layout)
    tr = _row_tile(hr, SUM_ROWS)
    nb = hr // tr

    def body(ids_ref, g_ref, r_ref, send_ref, own_ref):
        s = g_ref[...] + r_ref[...]
        send_ref[...] = s.astype(send_ref.dtype)

        @pl.when(pl.program_id(1) == ids_ref[1])
        def _():
            own_ref[...] = s

    if layout == "row":
        g_spec = pl.BlockSpec((tr, c), lambda t, j, ids_ref: ((j * r + ids_ref[0] * hr) // tr + t, 0))
    elif layout == "col":
        g_spec = pl.BlockSpec((tr, c), lambda t, j, ids_ref: (ids_ref[0] * nb + t, j))
    elif layout == "chip_cols":
        g_spec = pl.BlockSpec((None, tr, c), lambda t, j, ids_ref: (j, t, ids_ref[0]))
    else:
        g_spec = pl.BlockSpec((None, tr, c), lambda t, j, ids_ref: (j, ids_ref[0] * nb + t, 0))
    grid_spec = pltpu.PrefetchScalarGridSpec(
        num_scalar_prefetch=1, grid=(nb, 4),
        in_specs=[g_spec, pl.BlockSpec((None, tr, c), lambda t, j, ids_ref: (j, t, 0))],
        out_specs=[pl.BlockSpec((None, tr, c), lambda t, j, ids_ref: (j, t, 0)),
                   pl.BlockSpec((tr, c), lambda t, j, ids_ref: (t, 0))])
    return pl.pallas_call(
        body, name=name, grid_spec=grid_spec,
        out_shape=[jax.ShapeDtypeStruct((4, hr, c), BF16), jax.ShapeDtypeStruct((hr, c), F32)],
        compiler_params=_cparams(2),
    )(ids, grad, recv)


def _chip_sum(name, own, recv):
    hr, c = own.shape
    tr = _row_tile(hr, SUM_ROWS)

    def body(o_ref, r_ref, out_ref):
        out_ref[...] = ((o_ref[...] + r_ref[0].astype(F32)) + r_ref[1].astype(F32)) + r_ref[2].astype(F32)

    return pl.pallas_call(
        body, name=name, grid=(hr // tr,),
        in_specs=[pl.BlockSpec((tr, c), lambda i: (i, 0)), pl.BlockSpec((3, tr, c), lambda i: (0, i, 0))],
        out_specs=pl.BlockSpec((tr, c), lambda i: (i, 0)),
        out_shape=jax.ShapeDtypeStruct((hr, c), F32), compiler_params=_cparams(1),
    )(own, recv)


def _chip_sum_small(own, recv, ids):
    def body(ids_ref, o_ref, r_ref, out_ref):
        j = ids_ref[1]
        total = None
        for i in range(4):
            m = jnp.bitwise_xor(i, j)
            term = jnp.where(m == 0, o_ref[...], jnp.where(m == 2, r_ref[0], jnp.where(m == 1, r_ref[1], r_ref[2])))
            total = term if total is None else total + term
        out_ref[...] = total

    grid_spec = pltpu.PrefetchScalarGridSpec(
        num_scalar_prefetch=1, grid=(1,),
        in_specs=[pl.BlockSpec(own.shape, lambda i, ids_ref: (0, 0)), pl.BlockSpec(recv.shape, lambda i, ids_ref: (0, 0, 0))],
        out_specs=pl.BlockSpec(own.shape, lambda i, ids_ref: (0, 0)))
    return pl.pallas_call(body, name="chip_sum_small", grid_spec=grid_spec,
                          out_shape=jax.ShapeDtypeStruct(own.shape, F32), compiler_params=_cparams(1))(ids, own, recv)


def _adamw(name, w, m, v, mine, theirs, ids):
    lead = (None,) * (w.ndim - 2)
    rows, cols = w.shape[-2:]
    half = rows // 2
    tr = _row_tile(half, ADAM_ROWS, unit=8)
    nb = half // tr
    c1 = 1.0 / (1.0 - ADAM_B1 ** ADAM_STEP)
    c2 = 1.0 / (1.0 - ADAM_B2 ** ADAM_STEP)

    def body(ids_ref, w_ref, m_ref, v_ref, mine_ref, theirs_ref, g_out, d_out, m_out, v_out):
        g = jnp.where(pl.program_id(0) == ids_ref[0], mine_ref[...], theirs_ref[...])
        m_new = ADAM_B1 * m_ref[...] + (1.0 - ADAM_B1) * g
        v_new = ADAM_B2 * v_ref[...] + (1.0 - ADAM_B2) * (g * g)
        d_out[...] = -ADAM_LR * ((m_new * c1) / (jnp.sqrt(v_new * c2) + ADAM_EPS) + ADAM_WD * w_ref[...])
        g_out[...] = g
        m_out[...] = m_new
        v_out[...] = v_new

    full = pl.BlockSpec(lead + (tr, cols), lambda h, i, ids_ref: (0,) * len(lead) + (h * nb + i, 0))
    part = pl.BlockSpec((tr, cols), lambda h, i, ids_ref: (i, 0))
    grid_spec = pltpu.PrefetchScalarGridSpec(num_scalar_prefetch=1, grid=(2, nb),
                                             in_specs=[full, full, full, part, part], out_specs=[full] * 4)
    return pl.pallas_call(
        body, name=name, grid_spec=grid_spec,
        out_shape=[jax.ShapeDtypeStruct(w.shape, F32)] * 4, compiler_params=_cparams(2),
    )(ids, w, m, v, mine, theirs)


def _adamw_whole(name, w, m, v, g):
    rows, cols = w.shape[-2:]
    tr = _row_tile(rows, 2 * ADAM_ROWS, unit=8)
    c1 = 1.0 / (1.0 - ADAM_B1 ** ADAM_STEP)
    c2 = 1.0 / (1.0 - ADAM_B2 ** ADAM_STEP)

    def body(w_ref, m_ref, v_ref, g_ref, g_out, d_out, m_out, v_out):
        g = g_ref[...]
        m_new = ADAM_B1 * m_ref[...] + (1.0 - ADAM_B1) * g
        v_new = ADAM_B2 * v_ref[...] + (1.0 - ADAM_B2) * (g * g)
        d_out[...] = -ADAM_LR * ((m_new * c1) / (jnp.sqrt(v_new * c2) + ADAM_EPS) + ADAM_WD * w_ref[...])
        g_out[...] = g
        m_out[...] = m_new
        v_out[...] = v_new

    full = pl.BlockSpec((None, tr, cols), lambda i: (0, i, 0))
    return pl.pallas_call(
        body, name=name, grid=(rows // tr,), in_specs=[full, full, full, pl.BlockSpec((tr, cols), lambda i: (i, 0))],
        out_specs=[full] * 4, out_shape=[jax.ShapeDtypeStruct(w.shape, F32)] * 4, compiler_params=_cparams(1),
    )(w, m, v, g)


def _adamw_replicated(g_rows, ws, ms, vs):
    n = len(ws)
    layout, _ = _rep_rows()
    c1 = 1.0 / (1.0 - ADAM_B1 ** ADAM_STEP)
    c2 = 1.0 / (1.0 - ADAM_B2 ** ADAM_STEP)

    def body(g_ref, *refs):
        w_refs, m_refs, v_refs = refs[0:n], refs[n:2 * n], refs[2 * n:3 * n]
        outs = refs[3 * n:]
        for k, (r0, rows) in enumerate(layout):
            width = w_refs[k].shape[1]
            g = jnp.concatenate([g_ref[r0 + j:r0 + j + 1, :] for j in range(rows)], axis=1)[:, 0:width]
            m_new = ADAM_B1 * m_refs[k][...] + (1.0 - ADAM_B1) * g
            v_new = ADAM_B2 * v_refs[k][...] + (1.0 - ADAM_B2) * (g * g)
            outs[k][...] = g
            outs[n + k][...] = -ADAM_LR * ((m_new * c1) / (jnp.sqrt(v_new * c2) + ADAM_EPS) + ADAM_WD * w_refs[k][...])
            outs[2 * n + k][...] = m_new
            outs[3 * n + k][...] = v_new

    res = pl.pallas_call(body, name="adamw_replicated",
                         out_shape=[jax.ShapeDtypeStruct(w.shape, F32) for _ in range(4) for w in ws])(g_rows, *ws, *ms, *vs)
    return [res[k * n:(k + 1) * n] for k in range(4)]


def _small_shard(parts):
    return _flatten(parts, _BIG[-1][1])


_ENTRY = {e[0]: e for e in _BIG}
_GRAD_ENTRY = {**_ENTRY, "w_in": ("w_in", IN_SHARD, D_MODEL, "chip_cols")}
FFN_MATS = ("w_ffn_down", "w_ffn_up")
MIXER_MATS = ("w_mix_out", "w_ssm_out", "w_attn_out")


class _StepPlan:
    def __init__(self, w, late_shards, shards, ids):
        self.w, self.g = w, {}
        self.late_shards, self.shards, self.ids = late_shards, shards, ids
        self.sums, self.halves, self.results = {}, {}, {}

    def run(self, name, fn, *args, **kw):
        at = getattr(self, "_at_" + name, None)
        if at is None:
            return fn(*args, **kw)
        exchange, landed = at()
        res, extra = fn(*args, bg=exchange, **kw)
        landed(extra)
        return res

    def _at_ssd_fwd(self):
        def landed(fulls):
            self.partly_gathered = fulls

        return _gather_ici([_ENTRY[n] for n in MIXER_MATS], [self.late_shards[n] for n in MIXER_MATS]), landed

    def _at_attn_fwd(self):
        stages = (_gather_pass_on([_ENTRY[n] for n in MIXER_MATS], self.partly_gathered),
                  _gather_ici([_ENTRY[n] for n in FFN_MATS], [self.late_shards[n] for n in FFN_MATS]))

        def landed(extra):
            mixer, self.partly_gathered = _split(stages, extra)
            self.w.update(zip(MIXER_MATS, mixer))

        return _join(*stages), landed

    def _at_ssm_gate_norm(self):
        return (_gather_pass_on([_ENTRY[n] for n in FFN_MATS], self.partly_gathered),
                lambda fulls: self.w.update(zip(FFN_MATS, fulls)))

    def pair_sums(self, names, grads, recv):
        for n, gr, rv in zip(names, grads, recv):
            _, r, c, lay = _GRAD_ENTRY[n]
            self.sums[n] = _pair_sum("pair_sum_" + n, gr, rv, self.ids, r, c, lay)

    def chip_sums(self, names, recv):
        for n, rv in zip(names, recv):
            self.halves[n] = _chip_sum("chip_sum_" + n, self.sums[n][1], rv)

    def adamw(self, names, theirs):
        for n, th in zip(names, theirs):
            sh = self.shards[n]
            if n == "w_in":
                mine_first = self.ids[0] == 0
                g_t = jnp.where(mine_first, jnp.concatenate([self.halves[n], th], axis=1),
                                jnp.concatenate([th, self.halves[n]], axis=1))
                res = _adamw_whole("adamw_" + n, *[jnp.swapaxes(sh[k], -1, -2) for k in ("w", "m", "v")], g_t)
                self.results[n] = [jnp.swapaxes(r, -1, -2) for r in res]
            else:
                self.results[n] = _adamw("adamw_" + n, sh["w"], sh["m"], sh["v"], self.halves[n], th, self.ids)

    def _pair_stage(self, names, grads):
        return (_pair_exchange([_GRAD_ENTRY[n] for n in names], grads),
                lambda recv: self.pair_sums(names, grads, recv))

    def _at_ffn_up_dx(self):
        return self._pair_stage(FFN_MATS, [self.g[n] for n in FFN_MATS])

    def _at_ssm_gate_norm_bwd(self):
        return self._pair_stage(MIXER_MATS, [self.g[n] for n in MIXER_MATS])

    def _at_attn_bwd(self):
        return _chip_exchange([self.sums[n][0] for n in FFN_MATS]), lambda recv: self.chip_sums(FFN_MATS, recv)

    def _at_ssd_bwd(self):
        stages = (_chip_exchange([self.sums[n][0] for n in MIXER_MATS]),
                  _whole_to_sibling([self.halves[n] for n in FFN_MATS]))

        def landed(extra):
            recv, theirs = _split(stages, extra)
            self.chip_sums(MIXER_MATS, recv)
            self.adamw(FFN_MATS, theirs)

        return _join(*stages), landed

    def _at_ssm_conv_bwd(self):
        return _whole_to_sibling([self.halves[n] for n in MIXER_MATS]), lambda theirs: self.adamw(MIXER_MATS, theirs)

    def _at_in_proj_dx(self):
        grads = [_from_cat_t(self.g.pop("w_cat_t"))]
        self.pair_sums(("w_in",), grads,
                       _run_exchange("grad_pair_exchange_w_in", _pair_exchange([_GRAD_ENTRY["w_in"]], grads)))
        return _chip_exchange([self.sums["w_in"][0]]), lambda recv: self.chip_sums(("w_in",), recv)

    def finish(self, g_small, g_rep, rep_shards):
        stages = (_pair_exchange([_ENTRY["small"]], [g_small]), _whole_to_sibling([g_rep]))
        recv_small, recv_rep = _split(stages, _run_exchange("grad_pair_exchange_tail", _join(*stages)))
        self.pair_sums(("small",), [g_small], recv_small)
        p_rep, = _rowwise("pair_sum_replicated", lambda r0, a, b: [a + b], SMALL_ROWS, SMALL_ROWS,
                          [(g_rep, LANES, 0), (recv_rep[0], LANES, 0)], [], [(LANES, F32)], [])
        stages = (_chip_exchange([self.sums["small"][0]]), _to_all_chips(p_rep))
        recv, recv_rep = _split(stages, _run_exchange("grad_chip_exchange_tail", _join(*stages)))
        self.chip_sums(("small",), recv)
        g_rep_tot = _chip_sum_small(p_rep, recv_rep[0], self.ids)
        last = ("w_in", "small")
        self.adamw(last, _run_exchange("grad_half_share_tail", _whole_to_sibling([self.halves[n] for n in last])))
        self.results["replicated"] = _adamw_replicated(g_rep_tot, rep_shards["w"], rep_shards["m"], rep_shards["v"])
        return g_rep_tot[_rep_rows()[1], 0]


def kernel(x, meta_tokens, norm_pre_mix, w_in, ssm_conv_w, ssm_conv_b, ssm_dt_bias, ssm_a_log, ssm_d_skip, ssm_norm, w_ssm_out, attn_sinks, w_attn_out, w_mix_out, norm_post_mix, norm_pre_ffn, w_ffn_up, ffn_conv_w, ffn_conv_b, w_ffn_down, norm_post_ffn, loss_target, m_meta_tokens, m_norm_pre_mix, m_w_in, m_ssm_conv_w, m_ssm_conv_b, m_ssm_dt_bias, m_ssm_a_log, m_ssm_d_skip, m_ssm_norm, m_w_ssm_out, m_attn_sinks, m_w_attn_out, m_w_mix_out, m_norm_post_mix, m_norm_pre_ffn, m_w_ffn_up, m_ffn_conv_w, m_ffn_conv_b, m_w_ffn_down, m_norm_post_ffn, v_meta_tokens, v_norm_pre_mix, v_w_in, v_ssm_conv_w, v_ssm_conv_b, v_ssm_dt_bias, v_ssm_a_log, v_ssm_d_skip, v_ssm_norm, v_w_ssm_out, v_attn_sinks, v_w_attn_out, v_w_mix_out, v_norm_post_mix, v_norm_pre_ffn, v_w_ffn_up, v_ffn_conv_w, v_ffn_conv_b, v_w_ffn_down, v_norm_post_ffn):
    args = dict(locals())
    squeeze = lambda a: a.reshape(a.shape[-2:])
    wts = {n: squeeze(args[n]) for n in WEIGHT_ORDER}
    mom = {n: squeeze(args["m_" + n]) for n in WEIGHT_ORDER}
    var = {n: squeeze(args["v_" + n]) for n in WEIGHT_ORDER}
    x_i, y_i, c_i = _mesh_pos()
    ids = jnp.stack([c_i, _chip_index(x_i, y_i)]).astype(jnp.int32)
    big_names = [n for n, _, _, _ in _BIG[:-1]]
    small_names = [n for n, _, _ in _SMALL_SHARDED]
    rep_names = [n for n, _ in _REPLICATED]

    stacks = {"w": wts, "m": mom, "v": var}
    shards = {n: {"w": args[n], "m": args["m_" + n], "v": args["v_" + n]} for n in big_names}
    shards["small"] = {k: _small_shard([d[n] for n in small_names]) for k, d in stacks.items()}
    rep_shards = {k: [d[n] for n in rep_names] for k, d in stacks.items()}

    w_in4, small_all = _gather_weights([_ENTRY["w_in"], _ENTRY["small"]], [wts["w_in"].astype(BF16), shards["small"]["w"]])
    w = {n: wts[n] for n in rep_names}
    w["w_cat"] = _to_cat(w_in4)
    small_parts = [_unflatten(small_all[i], [shp for _, shp, _ in _SMALL_SHARDED]) for i in range(4)]
    for k, (n, _, axis) in enumerate(_SMALL_SHARDED):
        w[n] = jnp.concatenate([small_parts[i][k] for i in range(4)], axis=axis)
    plan = _StepPlan(w, {n: wts[n].astype(BF16) for n in MIXER_MATS + FFN_MATS}, shards, ids)

    head = jnp.concatenate([jnp.zeros((PAD, D_MODEL), F32), w["meta_tokens"]], axis=0)
    loss_sum, dx, dhead = _local_step(x[0], head, loss_target[0], plan)
    g = plan.g
    g["meta_tokens"] = dhead[PAD:]
    g_small = jnp.stack([_small_shard([_shard_of(g[n], i, shp, ax) for n, shp, ax in _SMALL_SHARDED]) for i in range(4)])
    loss_part = (loss_sum * (0.5 / D_MODEL)).reshape(1, 1)
    loss = plan.finish(g_small, _in_rows([g[n] for n in rep_names] + [loss_part]), rep_shards)

    results = {}
    for kind in range(4):
        results.update({(kind, n): plan.results[n][kind] for n in big_names})
        parts = _unflatten(plan.results["small"][kind], [shp for _, shp, _ in _SMALL_SHARDED])
        results.update({(kind, n): parts[k] for k, n in enumerate(small_names)})
        results.update({(kind, n): plan.results["replicated"][kind][k] for k, n in enumerate(rep_names)})
    outs = [results[kind, n].reshape(args[n].shape) for kind in range(4) for n in WEIGHT_ORDER]
    return (loss, dx[None], *outs)
```

```python
import math
from typing import Any, Callable, NamedTuple, Sequence

import jax
import jax.numpy as jnp
from jax import lax
from jax.experimental import pallas as pl
from jax.experimental.pallas import tpu as pltpu

F32 = jnp.float32
BF16 = jnp.bfloat16

D_MODEL = 1024
N_META = 16
T = 128
PAD = T - N_META
D_INNER = 2048
SSM_HEADS = 32
HEAD_P = 64
SSM_GROUPS = 4
GROUP_W = D_INNER // SSM_GROUPS
D_STATE = 128
CONV_DIM = D_INNER + 2 * SSM_GROUPS * D_STATE
ATTN_HEADS = 16
KV_HEADS = 4
ATTN_W = 1024
KV_W = 256
FFN_DIM = 2816
N_IN = 8736
EPS = 1e-6
NEG = -1e30
SCALE = 0.125

P_Q, P_K, P_V, P_DT, P_Z, P_GATE, P_XBC = 0, 1024, 1280, 1536, 2048, 4096, 6144
QKV_W = 1536
P_W = 9216

ADAM_LR, ADAM_B1, ADAM_B2, ADAM_EPS, ADAM_WD, ADAM_STEP = 0.001, 0.9, 0.999, 1e-08, 0.01, 10

VMEM_BUDGET = 40 * 1024 * 1024
VMEM_LIMIT = 56 * 1024 * 1024
MESH = pl.DeviceIdType.MESH
ANY = pl.BlockSpec(memory_space=pl.ANY)


def _cparams(n_axes, **kw):
    return pltpu.CompilerParams(dimension_semantics=("arbitrary",) * n_axes, vmem_limit_bytes=VMEM_LIMIT, **kw)


class _Exchange(NamedTuple):
    ins: Sequence[Any]
    out_shapes: Sequence[Any]
    make_copies: Callable
    n_copies: int
    aliases: dict = {}


def _call(body, name, grid, in_specs, out_specs, out_shape, operands, scratch_shapes=(), aliases=None, bg=None):
    aliases = dict(aliases or {})
    if bg is None:
        return pl.pallas_call(body, name=name, grid=grid, in_specs=in_specs, out_specs=out_specs, out_shape=out_shape,
                              scratch_shapes=list(scratch_shapes), input_output_aliases=aliases,
                              compiler_params=_cparams(len(grid)))(*operands)
    n_in, n_out, n_scr = len(in_specs), len(out_specs), len(scratch_shapes)
    nb_in, nb_out = len(bg.ins), len(bg.out_shapes)

    def hosted(*refs):
        ins, bg_ins = refs[:n_in], refs[n_in:n_in + nb_in]
        outs = refs[n_in + nb_in:n_in + nb_in + n_out]
        bg_outs = refs[n_in + nb_in + n_out:n_in + nb_in + n_out + nb_out]
        scratch = refs[n_in + nb_in + n_out + nb_out:n_in + nb_in + n_out + nb_out + n_scr]
        send_sems, recv_sems = refs[-2:]
        pids = [pl.program_id(a) for a in range(len(grid))]
        first, last = pids[0] == 0, pids[0] == grid[0] - 1
        for p, g in zip(pids[1:], grid[1:]):
            first, last = first & (p == 0), last & (p == g - 1)
        copies = []
        for k, (src, dst, peer) in enumerate(bg.make_copies(bg_ins, bg_outs)):
            if peer is None:
                copies.append(pltpu.make_async_copy(src, dst, send_sems.at[k]))
            else:
                copies.append(pltpu.make_async_remote_copy(src_ref=src, dst_ref=dst, send_sem=send_sems.at[k],
                                                           recv_sem=recv_sems.at[k], device_id=peer, device_id_type=MESH))
        assert len(copies) == bg.n_copies

        @pl.when(first)
        def _():
            for cp in copies:
                cp.start()

        body(*ins, *outs, *scratch)

        @pl.when(last)
        def _():
            for cp in copies:
                cp.wait()

    aliases = {(k if k < n_in else k + nb_in): v for k, v in aliases.items()}
    aliases.update({n_in + k: n_out + v for k, v in bg.aliases.items()})
    res = pl.pallas_call(
        hosted, name=name, grid=grid, in_specs=list(in_specs) + [ANY] * nb_in, out_specs=list(out_specs) + [ANY] * nb_out,
        out_shape=list(out_shape) + list(bg.out_shapes), input_output_aliases=aliases,
        scratch_shapes=list(scratch_shapes) + [pltpu.SemaphoreType.DMA((bg.n_copies,))] * 2,
        compiler_params=_cparams(len(grid), has_side_effects=True))(*operands, *bg.ins)
    return res[:n_out], res[n_out:]


def _sigmoid(x):
    return 1.0 / (1.0 + jnp.exp(-x))


def _silu(x):
    return x * _sigmoid(x)


def _silu_grad(x):
    s = _sigmoid(x)
    return x * s, s * (1.0 + x * (1.0 - s))


def _dsilu(x):
    return _silu_grad(x)[1]


def _softplus(x):
    e = jnp.exp(-jnp.abs(x))
    small = e * (1.0 - e * (0.5 - e * (1.0 / 3.0)))
    return jnp.maximum(x, 0.0) + jnp.where(e < 0.01, small, jnp.log(1.0 + e))


def _rms(x, w):
    r = lax.rsqrt(jnp.mean(x * x, axis=-1, keepdims=True) + EPS)
    return x * r * w


def _rms_bwd(dy, x, w):
    r = lax.rsqrt(jnp.mean(x * x, axis=-1, keepdims=True) + EPS)
    xh = x * r
    g = dy * w
    dx = r * (g - xh * jnp.mean(g * xh, axis=-1, keepdims=True))
    dw = jnp.sum(dy * xh, axis=0, keepdims=True)
    return dx, dw


def _dot(a, b):
    return jnp.dot(a, b, preferred_element_type=F32)


def _dot_nt(a, b):
    return lax.dot_general(a, b, (((1,), (1,)), ((), ())), preferred_element_type=F32)


def _dot_tn(a, b):
    return lax.dot_general(a, b, (((0,), (0,)), ((), ())), preferred_element_type=F32)


def _split3(x):
    hi = x.astype(BF16)
    r = x - hi.astype(F32)
    mid = r.astype(BF16)
    lo = (r - mid.astype(F32)).astype(BF16)
    return hi, mid, lo


def _xdot(x, e):
    hi, mid, lo = _split3(x)
    return _dot(hi, e) + _dot(mid, e) + _dot(lo, e)


def _xdot_l(e, x):
    hi, mid, lo = _split3(x)
    return _dot(e, hi) + _dot(e, mid) + _dot(e, lo)


def _iota(shape, dim):
    return lax.broadcasted_iota(jnp.int32, shape, dim)


def _divisors(n, unit):
    return [t for t in range(unit, n + 1, unit) if n % t == 0]


MIN_MATMUL_STEPS = 8
SMALL_MATMUL = 2 ** 33


def _matmul_tiles(m, n, k, a_bytes, b_bytes, o_bytes, m_unit):
    best = None
    for tm in _divisors(m, m_unit):
        for tn in _divisors(n, 128):
            for tk in _divisors(k, 128):
                acc = 0 if tk == k else tm * tn * 4
                vm = 2 * (tm * tk * a_bytes + tk * tn * b_bytes + tm * tn * o_bytes) + acc
                if vm > VMEM_BUDGET:
                    continue
                steps = (m // tm) * (n // tn) * (k // tk)
                want = MIN_MATMUL_STEPS if m * n * k >= SMALL_MATMUL else 2
                score = (tk == k, min(steps, want), min(tm, 256), tm * tn * tk)
                if best is None or score > best[0]:
                    best = (score, (tm, tn, tk))
    return best[1]


def _matmul(name, a, b, mode, out_dtype, bg=None):
    if mode == "nn":
        (m, k), n = a.shape, b.shape[1]
    elif mode == "nt":
        (m, k), n = a.shape, b.shape[0]
    else:
        (k, m), n = a.shape, b.shape[1]
    ab, bb, ob = a.dtype.itemsize, b.dtype.itemsize, jnp.dtype(out_dtype).itemsize
    tm, tn, tk = _matmul_tiles(m, n, k, ab, bb, ob, 128 if mode == "tn" else 16)
    nk = k // tk
    dot = {"nn": _dot, "nt": _dot_nt, "tn": _dot_tn}[mode]

    def body(a_ref, b_ref, o_ref, *scratch):
        prod = dot(a_ref[...].astype(BF16), b_ref[...].astype(BF16))
        if nk == 1:
            o_ref[...] = prod.astype(o_ref.dtype)
        else:
            acc_ref, = scratch
            kk = pl.program_id(2)

            @pl.when(kk == 0)
            def _():
                acc_ref[...] = prod

            @pl.when(kk > 0)
            def _():
                acc_ref[...] += prod

            @pl.when(kk == nk - 1)
            def _():
                o_ref[...] = acc_ref[...].astype(o_ref.dtype)

    a_spec = pl.BlockSpec((tk, tm), lambda i, j, kk: (kk, i)) if mode == "tn" else pl.BlockSpec((tm, tk), lambda i, j, kk: (i, kk))
    b_spec = pl.BlockSpec((tn, tk), lambda i, j, kk: (j, kk)) if mode == "nt" else pl.BlockSpec((tk, tn), lambda i, j, kk: (kk, j))
    res = _call(body, name, (m // tm, n // tn, nk), [a_spec, b_spec], [pl.BlockSpec((tm, tn), lambda i, j, kk: (i, j))],
                [jax.ShapeDtypeStruct((m, n), out_dtype)], [a, b],
                scratch_shapes=[] if nk == 1 else [pltpu.VMEM((tm, tn), F32)], bg=bg)
    return res[0] if bg is None else (res[0][0], res[1])


def _row_tile(n_rows, cap, unit=16):
    return max([t for t in _divisors(n_rows, unit) if t <= cap], default=n_rows)


ROW_SUB = 384
GROUP_UNROLL = 4


def _rowwise(name, fn, n_rows, tm, row_ins, full_ins, row_outs, acc_outs, bg=None):
    n_in = len(row_ins) + len(full_ins)
    n_ro = len(row_outs)
    into = [(k, o[3]) for k, o in enumerate(row_outs) if len(o) > 2 and o[2] == "into"]

    sub = min(tm, ROW_SUB)
    counts = [tm // T if len(e) > 3 and e[3] == "prev" else 1 for e in row_ins]
    assert all(cnt == 1 for cnt in counts) or sub == tm
    starts = [sum(counts[:k]) for k in range(len(counts))]
    n_row_in = sum(counts)
    n_in = n_row_in + len(full_ins)

    def body(*refs):
        i = pl.program_id(0)
        outs = refs[n_in + len(into):]

        sums = tuple(jnp.zeros((1, w), F32) for w in acc_outs)
        for s in range(tm // sub):
            rows = pl.ds(s * sub, sub)
            vals = [refs[st][rows, :] if cnt == 1 else jnp.concatenate([refs[st + k][...] for k in range(cnt)], axis=0)
                    for st, cnt in zip(starts, counts)]
            vals += [r[...] for r in refs[n_row_in:n_in]]
            res = fn(i * tm + s * sub, *vals)
            for o, r, v in zip(row_outs, outs[:n_ro], res[:n_ro]):
                if len(o) > 2 and o[2] == "first":
                    @pl.when(i == 0)
                    def _(r=r, v=v, rows=rows):
                        r[rows, :] = v.astype(r.dtype)
                else:
                    r[rows, :] = v.astype(r.dtype)
            sums = tuple(a + v for a, v in zip(sums, res[n_ro:]))

        @pl.when(i == 0)
        def _():
            for r, v in zip(outs[n_ro:], sums):
                r[...] = v

        @pl.when(i > 0)
        def _():
            for r, v in zip(outs[n_ro:], sums):
                r[...] += v

    def in_spec(entry, cnt):
        w, cb = entry[1], entry[2]
        if len(entry) > 3 and entry[3] == "prev":
            return [pl.BlockSpec((tm // cnt, w), lambda i, k=k: (jnp.maximum(cnt * i - 1 + k, 0), cb)) for k in range(cnt)]
        if len(entry) > 3 and entry[3] == "first":
            return [pl.BlockSpec((tm, w), lambda i: (0, cb))]
        return [pl.BlockSpec((tm, w), lambda i: (i, cb))]

    def out_spec(o):
        if len(o) == 2:
            return pl.BlockSpec((tm, o[0]), lambda i: (i, 0)), jax.ShapeDtypeStruct((n_rows, o[0]), o[1])
        if o[2] == "new":
            return pl.BlockSpec((tm, o[0]), lambda i: (i, o[4])), jax.ShapeDtypeStruct((n_rows, o[3]), o[1])
        if o[2] == "into":
            return pl.BlockSpec((tm, o[0]), lambda i: (i, o[4])), jax.ShapeDtypeStruct(o[3].shape, o[3].dtype)
        if o[2] == "first":
            return pl.BlockSpec((tm, o[0]), lambda i: (0, 0)), jax.ShapeDtypeStruct((tm, o[0]), o[1])
        return pl.BlockSpec((tm, o[0]), lambda i: (jnp.maximum(i - 1, 0), 0)), jax.ShapeDtypeStruct((o[3], o[0]), o[1])

    in_specs = [s for e, cnt in zip(row_ins, counts) for s in in_spec(e, cnt)]
    in_specs += [pl.BlockSpec(a.shape, lambda i: (0, 0)) for a in full_ins]
    in_specs += [pl.BlockSpec(memory_space=pl.ANY) for _ in into]
    specs_shapes = [out_spec(o) for o in row_outs]
    out_specs = [s for s, _ in specs_shapes] + [pl.BlockSpec((1, w), lambda i: (0, 0)) for w in acc_outs]
    out_shape = [s for _, s in specs_shapes] + [jax.ShapeDtypeStruct((1, w), F32) for w in acc_outs]
    return _call(body, name, (n_rows // tm,), in_specs, out_specs, out_shape,
                 [e[0] for e, cnt in zip(row_ins, counts) for _ in range(cnt)] + list(full_ins) + [arr for _, arr in into],
                 aliases={n_in + a: k for a, (k, _) in enumerate(into)}, bg=bg)


def _valid_rows(first_row, tm, lo):
    return (first_row + _iota((tm, 1), 0)) >= lo


CONV_ROWS = 128
CONV_SUB = 16
CONV_LANES = 256


def _conv_specs(tm, width, blk, n_rows, after):
    specs = [pl.BlockSpec((tm, width), lambda i: (i, blk)),
             pl.BlockSpec((8, width), lambda i: (jnp.maximum(i * (tm // 8) - 1, 0), blk))]
    if after:
        specs.append(pl.BlockSpec((16, width), lambda i: (jnp.minimum((i + 1) * (tm // 16), n_rows // 16 - 1), blk)))
    return specs


def _conv_window(win, w_ref, b_ref, taps, c0, cw, n):
    acc = b_ref[:, c0:c0 + cw] + w_ref[taps - 1:taps, c0:c0 + cw] * win[8:8 + n]
    for k in range(taps - 1):
        acc = acc + w_ref[k:k + 1, c0:c0 + cw] * win[8 - (taps - 1) + k:8 - (taps - 1) + k + n]
    return acc


def _ffn_act(name, u_raw, conv_w, conv_b, n_rows):
    tm, sub, cw = CONV_ROWS, CONV_SUB, CONV_LANES
    taps, width = conv_w.shape
    half = width // 2

    def body(cur_ref, prev_ref, w_ref, b_ref, f_ref, ext_ref):
        i = pl.program_id(0)
        ext_ref[0:8, :] = jnp.where(i > 0, prev_ref[...], 0.0)
        ext_ref[8:8 + tm, :] = cur_ref[...]
        for q in range(half // cw):
            a0, g0 = q * cw, half + q * cw

            def group(s, carry):
                r = pl.multiple_of(s * sub, sub)
                a = _conv_window(ext_ref[pl.ds(r, sub + 8), a0:a0 + cw], w_ref, b_ref, taps, a0, cw, sub)
                g = _conv_window(ext_ref[pl.ds(r, sub + 8), g0:g0 + cw], w_ref, b_ref, taps, g0, cw, sub)
                f_ref[pl.ds(r, sub), a0:a0 + cw] = (_silu(a) * g).astype(f_ref.dtype)
                return carry

            lax.fori_loop(0, tm // sub, group, 0, unroll=GROUP_UNROLL)

        @pl.when(i == 0)
        def _():
            f_ref[0:PAD, :] = jnp.zeros((PAD, half), f_ref.dtype)

    return pl.pallas_call(
        body, name=name, grid=(n_rows // tm,),
        in_specs=_conv_specs(tm, width, 0, n_rows, False) + [pl.BlockSpec((taps, width), lambda i: (0, 0)),
                                                             pl.BlockSpec((1, width), lambda i: (0, 0))],
        out_specs=pl.BlockSpec((tm, half), lambda i: (i, 0)),
        out_shape=jax.ShapeDtypeStruct((n_rows, half), BF16),
        scratch_shapes=[pltpu.VMEM((tm + 8, width), F32)],
        compiler_params=_cparams(1),
    )(u_raw, u_raw, conv_w, conv_b)


def _conv_bwd(name, raw, raw_blk, dsrcs, chunk_src, conv_w, conv_b, n_rows, gated, into=None, into_blk=0, bg=None):
    taps, width = conv_w.shape
    half = width // 2 if gated else width
    tm, sub, cw = CONV_ROWS, CONV_SUB, CONV_LANES
    te = tm + 16
    nd = len(dsrcs)
    n_parts = 2 if gated else 1

    def body(*refs):
        cur_ref, prev_ref, next_ref = refs[0:3]
        dcur, dnext = refs[3:3 + nd], refs[3 + nd:3 + 2 * nd]
        w_ref, b_ref = refs[3 + 2 * nd:5 + 2 * nd]
        out_ref, acc_ref, ext_ref, du_ref = refs[-4:]
        i = pl.program_id(0)
        ext_ref[0:8, :] = jnp.where(i > 0, prev_ref[...], 0.0)
        ext_ref[8:8 + tm, :] = cur_ref[...]
        ext_ref[8 + tm:24 + tm, :] = next_ref[...]

        for q, (src, off) in enumerate(chunk_src):
            cols = [q * cw, half + q * cw][:n_parts]

            def conv_grad(r, d, past_end):
                pre = [_conv_window(ext_ref[pl.ds(r, sub + 8), c0:c0 + cw], w_ref, b_ref, taps, c0, cw, sub) for c0 in cols]
                if gated:
                    act, dact = _silu_grad(pre[0])
                    dus = [d * pre[1] * dact, d * act]
                else:
                    dus = [d * _dsilu(pre[0])]
                for part, du in enumerate(dus):
                    if past_end:
                        du = jnp.where(i * tm + r + _iota((sub, 1), 0) < n_rows, du, 0.0)
                    du_ref[part, pl.ds(r, sub), :] = du

            def tile_rows(s, carry):
                r = pl.multiple_of(s * sub, sub)
                conv_grad(r, dcur[src][pl.ds(r, sub), off:off + cw].astype(F32), False)
                return carry

            lax.fori_loop(0, tm // sub, tile_rows, 0, unroll=GROUP_UNROLL)
            conv_grad(tm, dnext[src][:, off:off + cw].astype(F32), True)

            @pl.when(i == 0)
            def _():
                du_ref[:, 0:PAD, :] = jnp.zeros((n_parts, PAD, cw), F32)

            for part, c0 in enumerate(cols):
                taps_w = [w_ref[k:k + 1, c0:c0 + cw] for k in range(taps)]

                def back(s, sums):
                    new = list(sums)
                    for u in range(2):
                        r = pl.multiple_of((2 * s + u) * sub, sub)
                        win = du_ref[part, pl.ds(r, sub + 8), :]
                        raw_rows = ext_ref[pl.ds(8 + r, sub), c0:c0 + cw]
                        draw = jnp.zeros((sub, cw), F32)
                        for k in range(taps):
                            shifted = win[taps - 1 - k:taps - 1 - k + sub]
                            draw = draw + taps_w[k] * shifted
                            new[k] = new[k] + shifted * raw_rows
                        new[taps] = new[taps] + win[0:sub]
                        out_ref[pl.ds(r, sub), c0:c0 + cw] = draw.astype(out_ref.dtype)
                    return tuple(new)

                sums = lax.fori_loop(0, tm // (2 * sub), back, tuple(jnp.zeros((sub, cw), F32) for _ in range(taps + 1)))

                @pl.when(i == 0)
                def _(c0=c0):
                    out_ref[PAD - sub:PAD, c0:c0 + cw] = jnp.zeros((sub, cw), out_ref.dtype)

                for k in range(taps + 1):
                    total = jnp.sum(sums[k], axis=0, keepdims=True)
                    acc_ref[k:k + 1, c0:c0 + cw] = jnp.where(i == 0, total, acc_ref[k:k + 1, c0:c0 + cw] + total)

    in_specs = _conv_specs(tm, width, raw_blk, n_rows, True)
    in_specs += [pl.BlockSpec((tm, d.shape[1]), lambda i: (i, 0)) for d in dsrcs]
    in_specs += [pl.BlockSpec((16, d.shape[1]), lambda i: (jnp.minimum((i + 1) * (tm // 16), n_rows // 16 - 1), 0)) for d in dsrcs]
    in_specs += [pl.BlockSpec((taps, width), lambda i: (0, 0)), pl.BlockSpec((1, width), lambda i: (0, 0))]
    operands = [raw, raw, raw] + list(dsrcs) + list(dsrcs) + [conv_w, conv_b]
    aliases = {}
    if into is None:
        out0 = jax.ShapeDtypeStruct((n_rows, width), BF16)
    else:
        in_specs.append(pl.BlockSpec(memory_space=pl.ANY))
        operands.append(into)
        aliases = {len(operands) - 1: 0}
        out0 = jax.ShapeDtypeStruct(into.shape, into.dtype)
    return _call(body, name, (n_rows // tm,), in_specs,
                 [pl.BlockSpec((tm, width), lambda i: (i, into_blk)), pl.BlockSpec((8, width), lambda i: (0, 0))],
                 [out0, jax.ShapeDtypeStruct((8, width), F32)], operands,
                 scratch_shapes=[pltpu.VMEM((tm + 24, width), F32), pltpu.VMEM((n_parts, te + 8, cw), F32)],
                 aliases=aliases, bg=bg)


def _ssd_specs(n_chunks, rev, per_step=1):
    cidx = (lambda c: n_chunks - 1 - c) if rev else (lambda c: c)
    xw, nw = per_step * GROUP_W, per_step * D_STATE
    xg0, bg0, cg0 = P_XBC // xw, (P_XBC + D_INNER) // nw, (P_XBC + D_INNER + SSM_GROUPS * D_STATE) // nw

    def cur(width, blk0):
        return pl.BlockSpec((T, width), lambda g, c: (cidx(c), blk0 + g))

    def prev(width, blk0):
        return pl.BlockSpec((8, width), lambda g, c: (jnp.maximum(cidx(c) * (T // 8) - 1, 0), blk0 + g))

    specs = [cur(xw, xg0), prev(xw, xg0), cur(nw, bg0), prev(nw, bg0), cur(nw, cg0), prev(nw, cg0),
             pl.BlockSpec((T, 128), lambda g, c: (cidx(c), P_DT // 128))]
    wb, wc = D_INNER // nw, (D_INNER + SSM_GROUPS * D_STATE) // nw
    specs += [pl.BlockSpec((4, xw), lambda g, c: (0, g)),
              pl.BlockSpec((4, nw), lambda g, c: (0, wb + g)),
              pl.BlockSpec((4, nw), lambda g, c: (0, wc + g)),
              pl.BlockSpec((1, xw), lambda g, c: (0, g)),
              pl.BlockSpec((1, nw), lambda g, c: (0, wb + g)),
              pl.BlockSpec((1, nw), lambda g, c: (0, wc + g))]
    specs += [pl.BlockSpec((1, 128), lambda g, c: (0, 0))] * 3
    return specs, cidx


def _ssd_shared(refs, c):
    dt_ref, dtb_ref, alog_ref = refs[6], refs[13], refs[14]
    valid = _valid_rows(c * T, T, PAD)
    dtr = dt_ref[...] + dtb_ref[...]
    dt = jnp.where(valid, _softplus(dtr), 0.0)
    a_neg = -jnp.exp(alog_ref[...])
    tril = _iota((T, T), 0) >= _iota((T, T), 1)
    cs = _xdot_l(tril.astype(BF16), dt * a_neg)
    return dict(valid=valid, dtr=dtr, dt=dt, a_neg=a_neg, tril=tril, cs=cs, cs_t=cs.T)


def _heads_of_lanes():
    hh_t, ll_t = _iota((D_INNER, 128), 1), _iota((D_INNER, 128), 0)
    return (hh_t == jnp.right_shift(ll_t, 6)).astype(BF16)


def _ssd_chunk_forward(refs, ext_ref, g, c, shared):
    (xc_ref, xp_ref, bc_ref, bp_ref, cc_ref, cp_ref, dt_ref, wx_ref, wb_ref, wc_ref,
     bx_ref, bb_ref, bcb_ref, dtb_ref, alog_ref, dsk_ref) = refs

    def conv_pre(cur_ref, prev_ref, w_ref, b_ref, width):
        ext_ref[0:8, 0:width] = jnp.where(c > 0, prev_ref[...], 0.0)
        ext_ref[8:8 + T, 0:width] = cur_ref[...]
        w = w_ref[...]
        acc = b_ref[...] + w[3:4] * cur_ref[...]
        for k in range(3):
            acc = acc + w[k:k + 1] * ext_ref[pl.ds(5 + k, T), 0:width]
        return acc

    v = dict(shared)
    valid = v["valid"]
    v["head0"] = 8 * g
    v["x_pre"] = conv_pre(xc_ref, xp_ref, wx_ref, bx_ref, GROUP_W)
    v["b_pre"] = conv_pre(bc_ref, bp_ref, wb_ref, bb_ref, D_STATE)
    v["c_pre"] = conv_pre(cc_ref, cp_ref, wc_ref, bcb_ref, D_STATE)
    xs = _silu(v["x_pre"])
    bm = jnp.where(valid, _silu(v["b_pre"]), 0.0)
    cm = jnp.where(valid, _silu(v["c_pre"]), 0.0)
    hh, ll = _iota((128, GROUP_W), 0), _iota((128, GROUP_W), 1)
    expand = (hh == 8 * g + jnp.right_shift(ll, 6)).astype(BF16)
    cs_e = _xdot(v["cs"], expand)
    dt_e = _xdot(v["dt"], expand)
    cs_last_e = cs_e[T - 1:T, :]
    v.update(xs=xs, bm=bm, cm=cm, cs_e=cs_e, dt_e=dt_e, cs_last_e=cs_last_e)
    v["xdt"] = xs * dt_e
    v["decay_e"] = jnp.exp(cs_last_e - cs_e)
    v["ecs_e"] = jnp.exp(cs_e)
    v["elast_e"] = jnp.exp(cs_last_e)
    v["d_e"] = _xdot(dsk_ref[...], expand)
    v["gmat"] = _dot_nt(cm.astype(BF16), bm.astype(BF16))
    return v


def _ssd_decay_pair(v, jp):
    out = []
    for j in (v["head0"] + 2 * jp, v["head0"] + 2 * jp + 1):
        diff = v["cs"][:, j:j + 1] - v["cs_t"][j:j + 1, :]
        out.append(jnp.where(v["tril"], jnp.exp(jnp.where(v["tril"], diff, 0.0)), 0.0))
    return out


def _block_diag_pair(xp):
    lane = _iota(xp.shape, 1)
    return jnp.concatenate([jnp.where(lane < HEAD_P, xp, 0.0), jnp.where(lane >= HEAD_P, xp, 0.0)], axis=0)


SSD_GROUPS_PER_STEP = 4


def _ssd_group_refs(refs, gg):
    x_w, n_w = pl.ds(GROUP_W * gg, GROUP_W), pl.ds(D_STATE * gg, D_STATE)
    lanes = [x_w, x_w, n_w, n_w, n_w, n_w, None, x_w, n_w, n_w, x_w, n_w, n_w, None, None, None]
    return [r if w is None else r.at[:, w] for r, w in zip(refs, lanes)]


def _ssd_fwd(p, conv_w, conv_b, dt_bias, a_log, d_skip, n_chunks, bg=None):
    n_rows = n_chunks * T
    in_specs, _ = _ssd_specs(n_chunks, rev=False, per_step=SSD_GROUPS_PER_STEP)
    per = SSD_GROUPS_PER_STEP
    assert per == SSM_GROUPS

    def body(*refs):
        y_ref, hin_ref, st_ref, ext_ref = refs[16:]
        c = pl.program_id(1)

        @pl.when(c == 0)
        def _():
            st_ref[...] = jnp.zeros_like(st_ref)

        shared = _ssd_shared(refs[:16], c)
        for gg in range(per):
            v = _ssd_chunk_forward(_ssd_group_refs(refs[:16], gg), ext_ref.at[gg], gg, c, shared)
            state = st_ref[gg]
            hin_ref[gg] = state
            ys = []
            for jp in range(4):
                l0, l1 = _ssd_decay_pair(v, jp)
                lhs = jnp.concatenate([v["gmat"] * l0, v["gmat"] * l1], axis=1).astype(BF16)
                rhs = _block_diag_pair(v["xdt"][:, 128 * jp:128 * jp + 128]).astype(BF16)
                ys.append(_dot(lhs, rhs))
            y = jnp.concatenate(ys, axis=1)
            y = y + _dot(v["cm"].astype(BF16), state.astype(BF16)) * v["ecs_e"] + v["xs"] * v["d_e"]
            y_ref[:, GROUP_W * gg:GROUP_W * gg + GROUP_W] = y
            s_new = _dot_tn(v["bm"].astype(BF16), (v["xdt"] * v["decay_e"]).astype(BF16))
            st_ref[gg] = state * v["elast_e"] + s_new

    return _call(
        body, "ssd_fwd", (SSM_GROUPS // per, n_chunks), in_specs,
        [pl.BlockSpec((T, per * GROUP_W), lambda g, c: (c, g)),
         pl.BlockSpec((per, None, D_STATE, GROUP_W), lambda g, c: (g, c, 0, 0))],
        [jax.ShapeDtypeStruct((n_rows, D_INNER), F32),
         jax.ShapeDtypeStruct((SSM_GROUPS, n_chunks, D_STATE, GROUP_W), F32)],
        [p, p, p, p, p, p, p, conv_w, conv_w, conv_w, conv_b, conv_b, conv_b, dt_bias, a_log, d_skip],
        scratch_shapes=[pltpu.VMEM((per, D_STATE, GROUP_W), F32), pltpu.VMEM((per, T + 8, GROUP_W), F32)], bg=bg)


def _ssd_bwd(p, conv_w, conv_b, dt_bias, a_log, d_skip, hin, dy, dp, n_chunks, bg=None):
    n_rows = n_chunks * T
    per = SSD_GROUPS_PER_STEP
    assert per == SSM_GROUPS
    dt_w = P_Z - P_DT
    in_specs, cidx = _ssd_specs(n_chunks, rev=True, per_step=per)
    in_specs = in_specs + [pl.BlockSpec((per, None, D_STATE, GROUP_W), lambda g, c: (g, cidx(c), 0, 0)),
                           pl.BlockSpec((T, per * GROUP_W), lambda g, c: (cidx(c), g)), ANY]

    def body(*refs):
        hin_ref, dy_ref = refs[16:18]
        dx_ref, db_ref, dc_ref, dp_ref, dpar_ref, dst_ref, ext_ref, red_ref, dd_ref = refs[19:]
        step = pl.program_id(1)
        shared = _ssd_shared(refs[:16], n_chunks - 1 - step)
        local = jnp.zeros((T, 128), F32)
        for gg in range(per):
            x_w, n_w = pl.ds(GROUP_W * gg, GROUP_W), pl.ds(D_STATE * gg, D_STATE)
            local = local + group_body(_ssd_group_refs(refs[:16], gg), hin_ref.at[gg], dy_ref.at[:, x_w],
                                       dx_ref.at[:, x_w], db_ref.at[:, n_w], dc_ref.at[:, n_w], red_ref.at[:, :, x_w],
                                       dd_ref.at[:, x_w], dst_ref.at[gg], ext_ref.at[gg], gg, shared)
        to_heads = _heads_of_lanes()
        dcs = _xdot(red_ref[0], to_heads) + local
        triu = (_iota((T, T), 0) <= _iota((T, T), 1)).astype(BF16)
        da = _xdot_l(triu, dcs)
        ddt = da * shared["a_neg"] + _xdot(red_ref[1], to_heads)
        ddtr = jnp.where(shared["valid"], ddt * _sigmoid(shared["dtr"]), 0.0)
        dp_ref[...] = jnp.concatenate([ddtr, jnp.zeros((T, dt_w - 128), F32)], axis=1).astype(dp_ref.dtype)
        dpar = jnp.concatenate([
            jnp.sum(ddtr, axis=0, keepdims=True),
            jnp.sum(da * shared["dt"], axis=0, keepdims=True) * shared["a_neg"],
            _xdot(dd_ref[0:1, :], to_heads),
            jnp.zeros((5, 128), F32)], axis=0)
        dpar_ref[...] = jnp.where(step == 0, dpar, dpar_ref[...] + dpar)

    def group_body(in_refs, hin_ref, dy_ref, dx_ref, db_ref, dc_ref, red_ref, dd_ref, dst_ref, ext_ref, g, shared):
        step = pl.program_id(1)
        c = n_chunks - 1 - step

        @pl.when(step == 0)
        def _():
            dst_ref[...] = jnp.zeros_like(dst_ref)

        v = _ssd_chunk_forward(in_refs, ext_ref, g, c, shared)
        hin_f = hin_ref[...]
        hin_b = hin_f.astype(BF16)
        dyv = dy_ref[...]
        dst = dst_ref[...]
        dst_b = dst.astype(BF16)
        xs, bm, cm, xdt = v["xs"], v["bm"], v["cm"], v["xdt"]
        bm_b, cm_b = bm.astype(BF16), cm.astype(BF16)

        dd_e = jnp.sum(dyv * xs, axis=0, keepdims=True)
        dxs = dyv * v["d_e"]
        ch = _dot(cm_b, hin_b)
        dch = (dyv * v["ecs_e"]).astype(BF16)
        dcm = _dot_nt(dch, hin_b)
        dhin = _dot_tn(cm_b, dch) + dst * v["elast_e"]
        dcs_e = dyv * ch * v["ecs_e"]
        dxd = _dot(bm_b, dst_b)
        dbm = _dot_nt((xdt * v["decay_e"]).astype(BF16), dst_b)
        dxdt_state = dxd * v["decay_e"]
        q = dxdt_state * xdt
        dcs_e = dcs_e - q
        dlast_e = jnp.sum(q, axis=0, keepdims=True) + jnp.sum(dst * hin_f, axis=0, keepdims=True) * v["elast_e"]
        dg = jnp.zeros((T, T), F32)
        rs_cols = jnp.zeros((T, 128), F32)
        cs_rows = jnp.zeros((128, T), F32)
        lane_i, sub_i = _iota((T, 128), 1), _iota((128, T), 0)
        dxdt_parts = []
        for jp in range(4):
            l0, l1 = _ssd_decay_pair(v, jp)
            m0, m1 = v["gmat"] * l0, v["gmat"] * l1
            xbd = _block_diag_pair(xdt[:, 128 * jp:128 * jp + 128]).astype(BF16)
            dyp = dyv[:, 128 * jp:128 * jp + 128]
            dm = _dot_nt(dyp.astype(BF16), xbd)
            dm0, dm1 = dm[:, 0:T], dm[:, T:2 * T]
            dg = dg + dm0 * l0 + dm1 * l1
            for j, qq in ((v["head0"] + 2 * jp, dm0 * m0), (v["head0"] + 2 * jp + 1, dm1 * m1)):
                rs_cols = jnp.where(lane_i == j, jnp.sum(qq, axis=1, keepdims=True), rs_cols)
                cs_rows = jnp.where(sub_i == j, jnp.sum(qq, axis=0, keepdims=True), cs_rows)
            mv = jnp.concatenate([m0, m1], axis=0).astype(BF16)
            dxdt_parts.append(_dot_tn(mv, _block_diag_pair(dyp).astype(BF16)))
        dxdt = jnp.concatenate(dxdt_parts, axis=1) + dxdt_state
        dg_b = dg.astype(BF16)
        dcm = dcm + _dot(dg_b, bm_b)
        dbm = dbm + _dot_tn(dg_b, cm_b)
        last_row = _iota((T, 1), 0) == T - 1
        red_ref[0] = dcs_e + jnp.where(last_row, dlast_e, 0.0)
        red_ref[1] = dxdt * xs
        dd_ref[0:1, :] = dd_e
        dx_ref[...] = dxs + dxdt * v["dt_e"]
        db_ref[...] = jnp.where(v["valid"], dbm, 0.0)
        dc_ref[...] = jnp.where(v["valid"], dcm, 0.0)
        dst_ref[...] = dhin
        return rs_cols - cs_rows.T

    return _call(
        body, "ssd_bwd", (SSM_GROUPS // per, n_chunks), in_specs,
        [pl.BlockSpec((T, per * GROUP_W), lambda g, c: (cidx(c), g)),
         pl.BlockSpec((T, per * D_STATE), lambda g, c: (cidx(c), g)),
         pl.BlockSpec((T, per * D_STATE), lambda g, c: (cidx(c), g)),
         pl.BlockSpec((T, dt_w), lambda g, c: (cidx(c), P_DT // dt_w)),
         pl.BlockSpec((8, 128), lambda g, c: (0, 0))],
        [jax.ShapeDtypeStruct((n_rows, D_INNER), F32),
         jax.ShapeDtypeStruct((n_rows, SSM_GROUPS * D_STATE), F32),
         jax.ShapeDtypeStruct((n_rows, SSM_GROUPS * D_STATE), F32),
         jax.ShapeDtypeStruct(dp.shape, dp.dtype),
         jax.ShapeDtypeStruct((8, 128), F32)],
        [p, p, p, p, p, p, p, conv_w, conv_w, conv_w, conv_b, conv_b, conv_b, dt_bias, a_log, d_skip, hin, dy, dp],
        scratch_shapes=[pltpu.VMEM((per, D_STATE, GROUP_W), F32), pltpu.VMEM((per, T + 8, GROUP_W), F32),
                        pltpu.VMEM((2, T, D_INNER), F32), pltpu.VMEM((8, D_INNER), F32)],
        aliases={18: 3}, bg=bg)


def _alibi_slope(h):
    return 2.0 ** (-8.0 * (h + 1) / ATTN_HEADS)


def _dup_half(x256, kvh):
    xb = x256[:, 128 * (kvh // 2):128 * (kvh // 2) + 128]
    rolled = pltpu.roll(xb, 64, 1)
    lane = _iota(xb.shape, 1)
    if kvh % 2 == 0:
        return jnp.where(lane < 64, xb, rolled)
    return jnp.where(lane < 64, rolled, xb)


def _attn_masks(c):
    qi, j = _iota((T, T), 0), _iota((T, T), 1)
    tri = j <= qi
    meta_ok = (j >= PAD) & (j - PAD <= c * T + qi - PAD)
    band_ok = c >= jnp.where(tri, 1, 2)
    dist = jnp.bitwise_and(qi - j, T - 1).astype(F32)
    return tri, meta_ok, band_ok, dist


def _fold(x3, tri):
    return jnp.concatenate([x3[:, 0:T], jnp.where(tri, x3[:, 2 * T:3 * T], x3[:, T:2 * T])], axis=1)


def _unfold(x2, tri):
    band = x2[:, T:2 * T]
    return jnp.concatenate([x2[:, 0:T], jnp.where(tri, 0.0, band), jnp.where(tri, band, 0.0)], axis=1)


def _attn_fwd(p, sinks, n_chunks, bg=None):
    n_rows = n_chunks * T
    kb, vb = P_K // KV_W, P_V // KV_W

    def body(q_ref, kc_ref, kp_ref, km_ref, vc_ref, vp_ref, vm_ref, sink_ref, o_ref, lse_ref):
        c = pl.program_id(0)
        sinks_v = sink_ref[...]
        masks = _attn_masks(c)
        tri, meta_ok, band_ok, dist = masks
        lane = _iota((T, 128), 1)
        for kvh in range(KV_HEADS):
            k3 = jnp.concatenate([_dup_half(r[...], kvh) for r in (km_ref, kp_ref, kc_ref)], axis=0).astype(BF16)
            v3 = jnp.concatenate([_dup_half(r[...], kvh) for r in (vm_ref, vp_ref, vc_ref)], axis=0)
            v3bd = _block_diag_rows(v3).astype(BF16)
            q2 = q_ref[:, 256 * kvh:256 * kvh + 256] * SCALE
            q4 = jnp.concatenate([jnp.where((lane < 64) if half == 0 else (lane >= 64), q2[:, 128 * pr:128 * pr + 128], 0.0)
                                  for pr in range(2) for half in range(2)], axis=0).astype(BF16)
            raw4 = _dot_nt(q4, k3)
            probs = []
            for hh in range(4):
                h = 4 * kvh + hh
                raw = raw4[T * hh:T * hh + T]
                band = jnp.where(tri, raw[:, 2 * T:3 * T], raw[:, T:2 * T]) - _alibi_slope(h) * dist
                sc = jnp.concatenate([jnp.where(meta_ok, raw[:, 0:T], NEG), jnp.where(band_ok, band, NEG)], axis=1)
                sink = sinks_v[:, h:h + 1]
                m = jnp.maximum(jnp.max(sc, axis=1, keepdims=True), sink)
                e = jnp.exp(sc - m)
                den = jnp.sum(e, axis=1, keepdims=True) + jnp.exp(sink - m)
                probs.append(_unfold(e * (1.0 / den), tri))
                lse_ref[:, h:h + 1] = m + jnp.log(den)
            p4 = jnp.concatenate([jnp.concatenate(probs[0:2], axis=1), jnp.concatenate(probs[2:4], axis=1)], axis=0)
            out = _dot(p4.astype(BF16), v3bd)
            o_ref[:, 256 * kvh:256 * kvh + 256] = jnp.concatenate([out[0:T], out[T:2 * T]], axis=1).astype(o_ref.dtype)

    blk = lambda width, col: pl.BlockSpec((T, width), lambda c: (c, col))
    prev = lambda width, col: pl.BlockSpec((T, width), lambda c: (jnp.maximum(c - 1, 0), col))
    first = lambda width, col: pl.BlockSpec((T, width), lambda c: (0, col))
    return _call(
        body, "attn_fwd", (n_chunks,),
        [blk(ATTN_W, P_Q // ATTN_W), blk(KV_W, kb), prev(KV_W, kb), first(KV_W, kb),
         blk(KV_W, vb), prev(KV_W, vb), first(KV_W, vb), pl.BlockSpec((1, 128), lambda c: (0, 0))],
        [pl.BlockSpec((T, ATTN_W), lambda c: (c, 0)), pl.BlockSpec((T, 128), lambda c: (c, 0))],
        [jax.ShapeDtypeStruct((n_rows, ATTN_W), BF16), jax.ShapeDtypeStruct((n_rows, 128), F32)],
        [p, p, p, p, p, p, p, sinks], bg=bg)


def _block_diag_rows(x3):
    lane = _iota(x3.shape, 1)
    return jnp.concatenate([jnp.where(lane < 64, x3, 0.0), jnp.where(lane >= 64, x3, 0.0)], axis=0)


def _fold_halves(x):
    return x + pltpu.roll(x, 64, 1)


def _attn_bwd(p, sinks, ao, lse, dao, dp, n_chunks, bg=None):
    kb, vb = P_K // KV_W, P_V // KV_W
    rc = lambda s: n_chunks - 1 - s

    def body(q_ref, kc_ref, kp_ref, km_ref, vc_ref, vp_ref, vm_ref, sink_ref, o_ref, lse_ref, do_ref, dp_in_ref,
             dqkv_ref, dsink_ref, kcar_ref, vcar_ref, kmeta_ref, vmeta_ref):
        step = pl.program_id(0)
        c = n_chunks - 1 - step

        @pl.when(step == 0)
        def _():
            for r in (kcar_ref, vcar_ref, kmeta_ref, vmeta_ref):
                r[...] = jnp.zeros_like(r)

        masks = _attn_masks(c)
        tri = masks[0]
        q = q_ref[...] * SCALE
        sinks_v = sink_ref[...]
        lse_v = lse_ref[...]
        ov = o_ref[...].astype(F32)
        dov = do_ref[...].astype(F32)
        lane = _iota((T, 128), 1)
        lane256 = _iota((3 * T, KV_W), 1)
        dsink = jnp.zeros((1, 128), F32)
        dk3_all = jnp.zeros((3 * T, KV_W), F32)
        dv3_all = jnp.zeros((3 * T, KV_W), F32)
        dqs = []
        for kvh in range(KV_HEADS):
            k3 = jnp.concatenate([_dup_half(r[...], kvh) for r in (km_ref, kp_ref, kc_ref)], axis=0).astype(BF16)
            v3 = jnp.concatenate([_dup_half(r[...], kvh) for r in (vm_ref, vp_ref, vc_ref)], axis=0).astype(BF16)
            halves = [(pr, half, (lane < 64) if half == 0 else (lane >= 64)) for pr in range(2) for half in range(2)]
            cols = [slice(128 * (2 * kvh + pr), 128 * (2 * kvh + pr) + 128) for pr in range(2)]
            q4 = jnp.concatenate([jnp.where(mine, q[:, cols[pr]], 0.0) for pr, _, mine in halves], axis=0).astype(BF16)
            do4 = jnp.concatenate([jnp.where(mine, dov[:, cols[pr]], 0.0) for pr, _, mine in halves], axis=0).astype(BF16)
            raw4 = _dot_nt(q4, k3)
            dp4 = _dot_nt(do4, v3)
            ds_rows, pm_rows = [], []
            for hh, (pr, half, mine) in enumerate(halves):
                h = 4 * kvh + hh
                raw = raw4[T * hh:T * hh + T]
                band = jnp.where(tri, raw[:, 2 * T:3 * T], raw[:, T:2 * T]) - _alibi_slope(h) * masks[3]
                sc = jnp.concatenate([jnp.where(masks[1], raw[:, 0:T], NEG), jnp.where(masks[2], band, NEG)], axis=1)
                lse_h = lse_v[:, h:h + 1]
                pm = jnp.exp(sc - lse_h)
                prod = dov[:, cols[pr]] * ov[:, cols[pr]]
                delta = jnp.sum(jnp.where(mine, prod, 0.0), axis=1, keepdims=True)
                dp = _fold(dp4[T * hh:T * hh + T], tri)
                ds_rows.append(_unfold(pm * (dp - delta), tri))
                pm_rows.append(_unfold(pm, tri))
                p_sink = jnp.exp(sinks_v[:, h:h + 1] - lse_h)
                dsink = jnp.where(_iota((1, 128), 1) == h, jnp.sum(-p_sink * delta, axis=0, keepdims=True), dsink)
            ds4 = jnp.concatenate(ds_rows, axis=0).astype(BF16)
            dq4 = _dot(ds4, k3)
            dk3 = _dot_tn(ds4, q4)
            dv3 = _dot_tn(jnp.concatenate(pm_rows, axis=0).astype(BF16), do4)
            for pr in range(2):
                dqs.append(jnp.where(lane < 64, dq4[2 * T * pr:2 * T * pr + T], dq4[2 * T * pr + T:2 * T * pr + 2 * T]) * SCALE)
            in_place = (lane256 >= 64 * kvh) & (lane256 < 64 * kvh + 64)
            wide = lambda x: jnp.concatenate([x, x], axis=1)
            dk3_all = jnp.where(in_place, wide(_fold_halves(dk3)), dk3_all)
            dv3_all = jnp.where(in_place, wide(_fold_halves(dv3)), dv3_all)
        dsink_all = dsink

        @pl.when(step == 0)
        def _():
            dsink_ref[...] = dsink_all

        @pl.when(step > 0)
        def _():
            dsink_ref[...] += dsink_all

        kmeta = kmeta_ref[...] + dk3_all[0:T]
        vmeta = vmeta_ref[...] + dv3_all[0:T]
        kmeta_ref[...] = kmeta
        vmeta_ref[...] = vmeta
        is_first = c == 0
        dk = jnp.where(is_first, kmeta, dk3_all[2 * T:3 * T] + kcar_ref[...])
        dv = jnp.where(is_first, vmeta, dv3_all[2 * T:3 * T] + vcar_ref[...])
        dqkv_ref[...] = jnp.concatenate(dqs + [dk, dv], axis=1).astype(dqkv_ref.dtype)
        kcar_ref[...] = dk3_all[T:2 * T]
        vcar_ref[...] = dv3_all[T:2 * T]

    blk = lambda width, col: pl.BlockSpec((T, width), lambda s: (rc(s), col))
    prev = lambda width, col: pl.BlockSpec((T, width), lambda s: (jnp.maximum(rc(s) - 1, 0), col))
    first = lambda width, col: pl.BlockSpec((T, width), lambda s: (0, col))
    return _call(
        body, "attn_bwd", (n_chunks,),
        [blk(ATTN_W, P_Q // ATTN_W), blk(KV_W, kb), prev(KV_W, kb), first(KV_W, kb),
         blk(KV_W, vb), prev(KV_W, vb), first(KV_W, vb), pl.BlockSpec((1, 128), lambda s: (0, 0)),
         blk(ATTN_W, 0), blk(128, 0), blk(ATTN_W, 0), ANY],
        [blk(QKV_W, P_Q // QKV_W), pl.BlockSpec((1, 128), lambda s: (0, 0))],
        [jax.ShapeDtypeStruct(dp.shape, dp.dtype), jax.ShapeDtypeStruct((1, 128), F32)],
        [p, p, p, p, p, p, p, sinks, ao, lse, dao, dp],
        scratch_shapes=[pltpu.VMEM((T, KV_W), F32)] * 4, aliases={11: 0}, bg=bg)


def _pad_lanes(v, width=128):
    return jnp.pad(v, ((0, 0), (0, width - v.shape[1])))


def _local_step(x, head, tgt, plan):
    w, g, run = plan.w, plan.g, plan.run
    n_tok = x.shape[0]
    n_rows = n_tok + T
    n_chunks = n_rows // T
    tm = _row_tile(n_rows, 384)
    dt_bias, a_log, d_skip = (_pad_lanes(w[k]) for k in ("ssm_dt_bias", "ssm_a_log", "ssm_d_skip"))
    sinks = _pad_lanes(w["attn_sinks"])
    x_in = [(x, D_MODEL, 0, "prev"), (head, D_MODEL, 0, "first")]
    head_tm = jnp.concatenate([head, jnp.zeros((tm - T, D_MODEL), F32)], axis=0)
    x_in_tm = [(x, D_MODEL, 0, "prev"), (head_tm, D_MODEL, 0, "first")]

    def h0_tile(r0, xt, hd):
        return jnp.where(_valid_rows(r0, xt.shape[0], T), xt, hd)

    n1, = _rowwise("norm_pre_mix", lambda r0, xt, hd, wn: [_rms(h0_tile(r0, xt, hd), wn)], n_rows, tm,
                   x_in_tm, [w["norm_pre_mix"]], [(D_MODEL, BF16)], [])
    p = _matmul("in_proj", n1, w["w_cat"], "nn", F32)
    y_ssd, hin = run("ssd_fwd", _ssd_fwd, p, w["ssm_conv_w"], w["ssm_conv_b"], dt_bias, a_log, d_skip, n_chunks)
    ao, lse = run("attn_fwd", _attn_fwd, p, sinks, n_chunks)

    def gate_norm(r0, y, z, wn):
        return [_rms(y * _silu(z), wn)]

    yn, = run("ssm_gate_norm", _rowwise, "ssm_gate_norm", gate_norm, n_rows, tm,
              [(y_ssd, D_INNER, 0), (p, D_INNER, P_Z // D_INNER)], [w["ssm_norm"]], [(D_INNER, BF16)], [])
    y_ssm = _matmul("ssm_out", yn, w["w_ssm_out"], "nn", F32)
    y_attn = _matmul("attn_out", ao, w["w_attn_out"], "nn", F32)

    def mix_gate(r0, ys, ya, gs, ga):
        return [_sigmoid(gs) * ys + _sigmoid(ga) * ya]

    gate_ins = [(p, D_MODEL, P_GATE // D_MODEL), (p, D_MODEL, P_GATE // D_MODEL + 1)]
    mixed, = _rowwise("mix_gate", mix_gate, n_rows, tm, [(y_ssm, D_MODEL, 0), (y_attn, D_MODEL, 0)] + gate_ins,
                      [], [(D_MODEL, BF16)], [])
    mix = _matmul("mix_out", mixed, w["w_mix_out"], "nn", F32)

    def post_mix(r0, mx, xt, hd, w_post, w_pre):
        h1 = jnp.where(_valid_rows(r0, mx.shape[0], PAD), h0_tile(r0, xt, hd) + _rms(mx, w_post), 0.0)
        return [h1, _rms(h1, w_pre)]

    h1, n2 = _rowwise("post_mix", post_mix, n_rows, tm, [(mix, D_MODEL, 0)] + x_in_tm,
                      [w["norm_post_mix"], w["norm_pre_ffn"]], [(D_MODEL, F32), (D_MODEL, BF16)], [])
    u_raw = _matmul("ffn_up", n2, w["w_ffn_up"], "nn", F32)
    f = _ffn_act("ffn_act", u_raw, w["ffn_conv_w"], w["ffn_conv_b"], n_rows)
    ffn = _matmul("ffn_down", f, w["w_ffn_down"], "nn", F32)

    def final(r0, fo, h, t, w_post):
        real = _valid_rows(r0, fo.shape[0], T)
        err = jnp.where(real, h + _rms(fo, w_post) - t, 0.0)
        dy = err * (1.0 / D_MODEL)
        dffn, dw = _rms_bwd(dy, fo, w_post)
        return [dffn, dy, jnp.sum(err * err, axis=0, keepdims=True), dw]

    dffn, dh2, loss_cols, g_norm_post_ffn = _rowwise(
        "loss_head", final, n_rows, tm, [(ffn, D_MODEL, 0), (h1, D_MODEL, 0), (tgt, D_MODEL, 0, "prev")],
        [w["norm_post_ffn"]], [(D_MODEL, BF16), (D_MODEL, F32)], [D_MODEL, D_MODEL])

    g["norm_post_ffn"] = g_norm_post_ffn
    g["w_ffn_down"] = _matmul("ffn_down_dw", f, dffn, "tn", F32)
    df = _matmul("ffn_down_dx", dffn, w["w_ffn_down"], "nt", F32)
    du_raw, dconv = _conv_bwd("ffn_act_bwd", u_raw, 0, [df], [(0, c0) for c0 in range(0, FFN_DIM, CONV_LANES)],
                              w["ffn_conv_w"], w["ffn_conv_b"], n_rows, True)
    g["ffn_conv_w"], g["ffn_conv_b"] = dconv[0:3], dconv[3:4]
    g["w_ffn_up"] = _matmul("ffn_up_dw", n2, du_raw, "tn", F32)
    dn2 = run("ffn_up_dx", _matmul, "ffn_up_dx", du_raw, w["w_ffn_up"], "nt", F32)

    def post_mix_bwd(r0, dn, d2, h, mx, w_pre, w_post):
        dx, dw_pre = _rms_bwd(dn, h, w_pre)
        dh1 = jnp.where(_valid_rows(r0, dn.shape[0], PAD), dx + d2, 0.0)
        dmix, dw_post = _rms_bwd(dh1, mx, w_post)
        return [dh1, dmix, dw_pre, dw_post]

    dh1, dmix, g["norm_pre_ffn"], g["norm_post_mix"] = _rowwise(
        "post_mix_bwd", post_mix_bwd, n_rows, tm,
        [(dn2, D_MODEL, 0), (dh2, D_MODEL, 0), (h1, D_MODEL, 0), (mix, D_MODEL, 0)],
        [w["norm_pre_ffn"], w["norm_post_mix"]], [(D_MODEL, F32), (D_MODEL, BF16)], [D_MODEL, D_MODEL])
    g["w_mix_out"] = _matmul("mix_out_dw", mixed, dmix, "tn", F32)
    dmixed = _matmul("mix_out_dx", dmix, w["w_mix_out"], "nt", F32)

    def mix_gate_bwd(r0, dm, ys, ya, gs, ga):
        ss, sa = _sigmoid(gs), _sigmoid(ga)
        dgate = jnp.concatenate([dm * ys * ss * (1.0 - ss), dm * ya * sa * (1.0 - sa)], axis=1)
        return [dm * ss, dm * sa, dgate]

    dys, dya, dp = _rowwise(
        "mix_gate_bwd", mix_gate_bwd, n_rows, tm,
        [(dmixed, D_MODEL, 0), (y_ssm, D_MODEL, 0), (y_attn, D_MODEL, 0)] + gate_ins,
        [], [(D_MODEL, BF16), (D_MODEL, BF16), (2 * D_MODEL, BF16, "new", P_W, P_GATE // (2 * D_MODEL))], [])
    g["w_ssm_out"] = _matmul("ssm_out_dw", yn, dys, "tn", F32)
    dyn = _matmul("ssm_out_dx", dys, w["w_ssm_out"], "nt", F32)
    g["w_attn_out"] = _matmul("attn_out_dw", ao, dya, "tn", F32)
    dao = _matmul("attn_out_dx", dya, w["w_attn_out"], "nt", BF16)

    def gate_norm_bwd(r0, dn, y, z, wn):
        sz, dsz = _silu_grad(z)
        dyz, dw = _rms_bwd(dn, y * sz, wn)
        live = _valid_rows(r0, dn.shape[0], PAD)
        return [jnp.where(live, dyz * sz, 0.0), jnp.where(live, dyz * y * dsz, 0.0), dw]

    dy_ssd, dp, g["ssm_norm"] = run(
        "ssm_gate_norm_bwd", _rowwise, "ssm_gate_norm_bwd", gate_norm_bwd, n_rows, tm,
        [(dyn, D_INNER, 0), (y_ssd, D_INNER, 0), (p, D_INNER, P_Z // D_INNER)],
        [w["ssm_norm"]], [(D_INNER, F32), (D_INNER, BF16, "into", dp, P_Z // D_INNER)], [D_INNER])
    dp, dsink = run("attn_bwd", _attn_bwd, p, sinks, ao, lse, dao, dp, n_chunks)
    g["attn_sinks"] = dsink[:, 0:ATTN_HEADS]
    dxs, dbm, dcm, dp, dpar = run("ssd_bwd", _ssd_bwd, p, w["ssm_conv_w"], w["ssm_conv_b"], dt_bias, a_log,
                                  d_skip, hin, dy_ssd, dp, n_chunks)
    g["ssm_dt_bias"], g["ssm_a_log"], g["ssm_d_skip"] = (dpar[i:i + 1, 0:SSM_HEADS] for i in range(3))
    x_chunks = [(src, c0) for src, arr in enumerate((dxs, dbm, dcm)) for c0 in range(0, arr.shape[1], CONV_LANES)]
    dp, dconv = run("ssm_conv_bwd", _conv_bwd, "ssm_conv_bwd", p, P_XBC // CONV_DIM, [dxs, dbm, dcm], x_chunks,
                    w["ssm_conv_w"], w["ssm_conv_b"], n_rows, False, into=dp, into_blk=P_XBC // CONV_DIM)
    g["ssm_conv_w"], g["ssm_conv_b"] = dconv[0:4], dconv[4:5]
    g["w_cat_t"] = _matmul("in_proj_dw", dp, n1, "tn", F32)
    dn1 = run("in_proj_dx", _matmul, "in_proj_dx", dp, w["w_cat"], "nt", F32)

    def pre_mix_bwd(r0, dn, d1, xt, hd, wn):
        dx, dw = _rms_bwd(dn, h0_tile(r0, xt, hd), wn)
        dh0 = jnp.where(_valid_rows(r0, dn.shape[0], PAD), dx + d1, 0.0)
        return [dh0, dh0, dw]

    dx_out, dhead, g["norm_pre_mix"] = _rowwise(
        "pre_mix_bwd", pre_mix_bwd, n_rows, T, [(dn1, D_MODEL, 0), (dh1, D_MODEL, 0)] + x_in,
        [w["norm_pre_mix"]], [(D_MODEL, F32, "prev", n_tok), (D_MODEL, F32, "first")], [D_MODEL])
    return jnp.sum(loss_cols), dx_out, dhead


_IN_SECTIONS = [((5152, 6176), P_Q), ((6176, 6432), P_K), ((6432, 6688), P_V), ((5120, 5152), P_DT),
                ((0, 2048), P_Z), ((6688, 8736), P_GATE), ((2048, 5120), P_XBC)]


IN_SHARD = N_IN // 4


def _shard_pieces(a, b):
    return [(j, max(a, j * IN_SHARD) - j * IN_SHARD, min(b, (j + 1) * IN_SHARD) - j * IN_SHARD)
            for j in range(4) if max(a, j * IN_SHARD) < min(b, (j + 1) * IN_SHARD)]


def _to_cat(w4):
    parts, at = [], 0
    for (a, b), off in _IN_SECTIONS:
        if off > at:
            parts.append(jnp.zeros((w4.shape[1], off - at), w4.dtype))
        parts += [w4[j, :, lo:hi] for j, lo, hi in _shard_pieces(a, b)]
        at = off + (b - a)
    return jnp.concatenate(parts, axis=1)


def _from_cat_t(g_cat_t):
    shards = [[] for _ in range(4)]
    for (a, b), off in sorted(_IN_SECTIONS):
        for j, lo, hi in _shard_pieces(a, b):
            start = off + j * IN_SHARD + lo - a
            shards[j].append(g_cat_t[start:start + hi - lo])
    return jnp.stack([jnp.concatenate(s, axis=0) for s in shards])


LANES = 1024
_BIG = [("w_in", 1024, 2184, "chip"), ("w_ssm_out", 512, 1024, "row"), ("w_attn_out", 256, 1024, "row"),
        ("w_mix_out", 256, 1024, "row"), ("w_ffn_up", 1024, 1408, "col"), ("w_ffn_down", 704, 1024, "row"),
        ("small", 32, LANES, "chip")]
_SMALL_SHARDED = [("ssm_conv_w", (4, 768), 1), ("ffn_conv_w", (3, 1408), 1), ("meta_tokens", (16, 256), 1)]
_REPLICATED = [("norm_pre_mix", 1024), ("ssm_conv_b", 3072), ("ssm_dt_bias", 32), ("ssm_a_log", 32),
               ("ssm_d_skip", 32), ("ssm_norm", 2048), ("attn_sinks", 16), ("norm_post_mix", 1024),
               ("norm_pre_ffn", 1024), ("ffn_conv_b", 5632), ("norm_post_ffn", 1024)]
SMALL_ROWS = 24


def _rep_rows():
    out, at = [], 0
    for _, width in _REPLICATED:
        out.append((at, -(-width // LANES)))
        at += out[-1][1]
    return out, at


def _in_rows(parts):
    rows = [jnp.pad(a, ((0, 0), (0, -a.shape[1] % LANES))).reshape(-1, LANES) for a in parts]
    flat = jnp.concatenate(rows, axis=0)
    return jnp.pad(flat, ((0, SMALL_ROWS - flat.shape[0]), (0, 0)))
WEIGHT_ORDER = ["meta_tokens", "norm_pre_mix", "w_in", "ssm_conv_w", "ssm_conv_b", "ssm_dt_bias", "ssm_a_log",
                "ssm_d_skip", "ssm_norm", "w_ssm_out", "attn_sinks", "w_attn_out", "w_mix_out", "norm_post_mix",
                "norm_pre_ffn", "w_ffn_up", "ffn_conv_w", "ffn_conv_b", "w_ffn_down", "norm_post_ffn"]


def _flatten(parts, rows):
    flat = jnp.concatenate([a.reshape(-1) for a in parts])
    return jnp.pad(flat, (0, rows * LANES - flat.shape[0])).reshape(rows, LANES)


def _unflatten(flat, shapes):
    flat = flat.reshape(-1)
    out, off = [], 0
    for shp in shapes:
        n = math.prod(shp)
        out.append(flat[off:off + n].reshape(shp))
        off += n
    return out


def _shard_of(full, chip, shape, axis):
    return lax.slice_in_dim(full, chip * shape[axis], (chip + 1) * shape[axis], axis=axis)


def _full_shape(r, c, layout):
    return {"row": (4 * r, c), "col": (r, 4 * c), "chip": (4, r, c), "chip_cols": (4, r, c)}[layout]


def _half_shape(r, c, layout):
    return (r, c // 2) if layout == "chip_cols" else (r // 2, c)


def _shard_view(ref, r, c, layout, chip):
    if layout == "row":
        return ref.at[pl.ds(pl.multiple_of(chip * r, 16), r), :]
    if layout == "col":
        return ref.at[:, pl.ds(pl.multiple_of(chip * c, 128), c)]
    return ref.at[chip]


def _half_view(ref, r, c, layout, chip, half):
    if layout == "chip_cols":
        return ref.at[chip, :, pl.ds(pl.multiple_of(half * (c // 2), 128), c // 2)]
    hr = r // 2
    if layout == "row":
        return ref.at[pl.ds(pl.multiple_of(chip * r + half * hr, 16), hr), :]
    r0 = pl.multiple_of(half * hr, 16)
    if layout == "col":
        return ref.at[pl.ds(r0, hr), pl.ds(pl.multiple_of(chip * c, 128), c)]
    return ref.at[chip, pl.ds(r0, hr), :]


def _mesh_pos():
    return lax.axis_index("x"), lax.axis_index("y"), lax.axis_index("c")


def _other_chips(x, y):
    return [(1 - x, y), (x, 1 - y), (1 - x, 1 - y)]


def _chip_index(x, y):
    return 2 * x + y


def _run_exchange(name, ex):
    n_in, n_out = len(ex.ins), len(ex.out_shapes)

    def body(*refs):
        in_refs, out_refs = refs[:n_in], refs[n_in:n_in + n_out]
        send_sems, recv_sems = refs[n_in + n_out:]
        copies = [pltpu.make_async_remote_copy(src_ref=s, dst_ref=d, send_sem=send_sems.at[i], recv_sem=recv_sems.at[i],
                                               device_id=dev, device_id_type=MESH)
                  for i, (s, d, dev) in enumerate(ex.make_copies(in_refs, out_refs))]
        assert len(copies) == ex.n_copies
        for cp in copies:
            cp.start()
        for cp in copies:
            cp.wait()

    return pl.pallas_call(
        body, name=name, in_specs=[ANY] * n_in, out_specs=[ANY] * n_out, out_shape=list(ex.out_shapes),
        scratch_shapes=[pltpu.SemaphoreType.DMA((ex.n_copies,)), pltpu.SemaphoreType.DMA((ex.n_copies,))],
        compiler_params=pltpu.CompilerParams(has_side_effects=True),
    )(*ex.ins)


def _join(*exs):
    def make(in_refs, out_refs):
        copies, i0, o0 = [], 0, 0
        for ex in exs:
            copies += ex.make_copies(in_refs[i0:i0 + len(ex.ins)], out_refs[o0:o0 + len(ex.out_shapes)])
            i0, o0 = i0 + len(ex.ins), o0 + len(ex.out_shapes)
        return copies

    aliases, i0, o0 = {}, 0, 0
    for ex in exs:
        aliases.update({i0 + k: o0 + v for k, v in ex.aliases.items()})
        i0, o0 = i0 + len(ex.ins), o0 + len(ex.out_shapes)
    return _Exchange([a for ex in exs for a in ex.ins], [s for ex in exs for s in ex.out_shapes], make,
                     sum(ex.n_copies for ex in exs), aliases)


def _split(exs, results):
    out, o0 = [], 0
    for ex in exs:
        out.append(list(results[o0:o0 + len(ex.out_shapes)]))
        o0 += len(ex.out_shapes)
    return out


def _gather_ici(entries, shards):
    def make(in_refs, out_refs):
        x, y, c = _mesh_pos()
        j = _chip_index(x, y)
        copies = []
        for ref_in, ref_out, (_, r, cc, lay) in zip(in_refs, out_refs, entries):
            copies.append((ref_in, _shard_view(ref_out, r, cc, lay, j), None))
            mine = ref_in.at[pl.ds(pl.multiple_of(c * (r // 2), 16), r // 2), :]
            copies += [(mine, _half_view(ref_out, r, cc, lay, j, c), (*ch, c)) for ch in _other_chips(x, y)]
        return copies

    shapes = [jax.ShapeDtypeStruct(_full_shape(r, cc, lay), s.dtype) for s, (_, r, cc, lay) in zip(shards, entries)]
    return _Exchange(list(shards), shapes, make, 4 * len(entries))


def _gather_pass_on(entries, fulls):
    def make(in_refs, out_refs):
        x, y, c = _mesh_pos()
        copies = []
        for ref, (_, r, cc, lay) in zip(out_refs, entries):
            for ch in _other_chips(x, y):
                landed = _half_view(ref, r, cc, lay, _chip_index(*ch), c)
                copies.append((landed, landed, (x, y, 1 - c)))
        return copies

    return _Exchange(list(fulls), [jax.ShapeDtypeStruct(f.shape, f.dtype) for f in fulls], make, 3 * len(entries),
                     {a: a for a in range(len(entries))})


def _gather_weights(entries, shards):
    n = len(entries)

    def body(*refs):
        ins, outs = refs[:n], refs[n:2 * n]
        send_sems, recv_sems, local_sems = refs[2 * n:]
        x, y, c = _mesh_pos()
        j = _chip_index(x, y)
        sibling = (x, y, 1 - c)
        chips = _other_chips(x, y)
        idx = [_chip_index(*ch) for ch in chips]

        def remote(k, src, dst, dev):
            return pltpu.make_async_remote_copy(src_ref=src, dst_ref=dst, send_sem=send_sems.at[k],
                                                recv_sem=recv_sems.at[k], device_id=dev, device_id_type=MESH)

        own = [pltpu.make_async_copy(ins[a], _shard_view(outs[a], r, cc, lay, j), local_sems.at[a])
               for a, (_, r, cc, lay) in enumerate(entries)]
        for cp in own:
            cp.start()
        first, passed = [], []
        for a, (_, r, cc, lay) in enumerate(entries):
            mine = ins[a].at[pl.ds(pl.multiple_of(c * (r // 2), 16), r // 2), :]
            for k, ch in enumerate(chips):
                first.append(remote(6 * a + k, mine, _half_view(outs[a], r, cc, lay, j, c), (*ch, c)))
                landed = _half_view(outs[a], r, cc, lay, idx[k], c)
                passed.append(remote(6 * a + 3 + k, landed, landed, sibling))
        for cp in first:
            cp.start()
        for a, (_, r, cc, lay) in enumerate(entries):
            for k in range(3):
                landed = _half_view(outs[a], r, cc, lay, idx[k], c)
                remote(6 * a + k, landed, landed, sibling).wait_recv()
                passed[3 * a + k].start()
        for a, (_, r, cc, lay) in enumerate(entries):
            for k in range(3):
                theirs = _half_view(outs[a], r, cc, lay, idx[k], 1 - c)
                remote(6 * a + 3 + k, theirs, theirs, sibling).wait_recv()
        for cp in first + passed:
            cp.wait_send()
        for cp in own:
            cp.wait()

    return pl.pallas_call(
        body, name="gather_weights", in_specs=[ANY] * n, out_specs=[ANY] * n,
        out_shape=[jax.ShapeDtypeStruct(_full_shape(r, cc, lay), s.dtype) for s, (_, r, cc, lay) in zip(shards, entries)],
        scratch_shapes=[pltpu.SemaphoreType.DMA((6 * n,)), pltpu.SemaphoreType.DMA((6 * n,)), pltpu.SemaphoreType.DMA((n,))],
        compiler_params=pltpu.CompilerParams(has_side_effects=True),
    )(*shards)


def _pair_exchange(entries, grads):
    def make(in_refs, out_refs):
        x, y, c = _mesh_pos()
        return [(_half_view(ref_in, r, cc, lay, i, 1 - c), ref_out.at[i], (x, y, 1 - c))
                for ref_in, ref_out, (_, r, cc, lay) in zip(in_refs, out_refs, entries) for i in range(4)]

    return _Exchange(list(grads), [jax.ShapeDtypeStruct((4,) + _half_shape(r, cc, lay), F32) for _, r, cc, lay in entries],
                     make, 4 * len(entries))


def _whole_to_sibling(arrays):
    def make(in_refs, out_refs):
        x, y, c = _mesh_pos()
        return [(r, o, (x, y, 1 - c)) for r, o in zip(in_refs, out_refs)]

    return _Exchange(list(arrays), [jax.ShapeDtypeStruct(a.shape, a.dtype) for a in arrays], make, len(arrays))


def _chip_exchange(psends):
    def make(in_refs, out_refs):
        x, y, c = _mesh_pos()
        return [(ref_in.at[_chip_index(*ch)], ref_out.at[k], (*ch, c))
                for ref_in, ref_out in zip(in_refs, out_refs) for k, ch in enumerate(_other_chips(x, y))]

    return _Exchange(list(psends), [jax.ShapeDtypeStruct((3,) + p.shape[1:], p.dtype) for p in psends], make,
                     3 * len(psends))


def _to_all_chips(array):
    def make(in_refs, out_refs):
        x, y, c = _mesh_pos()
        return [(in_refs[0], out_refs[0].at[k], (*ch, c)) for k, ch in enumerate(_other_chips(x, y))]

    return _Exchange([array], [jax.ShapeDtypeStruct((3,) + array.shape, array.dtype)], make, 3)


SUM_ROWS = 256
ADAM_ROWS = 128


def _pair_sum(name, grad, recv, ids, r, c, layout):
    hr, c = _half_shape(r, c, layout)
    tr = _row_tile(hr, SUM_ROWS)
    nb = hr // tr

    def body(ids_ref, g_ref, r_ref, send_ref, own_ref):
        s = g_ref[...] + r_ref[...]
        send_ref[...] = s.astype(send_ref.dtype)

        @pl.when(pl.program_id(1) == ids_ref[1])
        def _():
            own_ref[...] = s

    if layout == "row":
        g_spec = pl.BlockSpec((tr, c), lambda t, j, ids_ref: ((j * r + ids_ref[0] * hr) // tr + t, 0))
    elif layout == "col":
        g_spec = pl.BlockSpec((tr, c), lambda t, j, ids_ref: (ids_ref[0] * nb + t, j))
    elif layout == "chip_cols":
        g_spec = pl.BlockSpec((None, tr, c), lambda t, j, ids_ref: (j, t, ids_ref[0]))
    else:
        g_spec = pl.BlockSpec((None, tr, c), lambda t, j, ids_ref: (j, ids_ref[0] * nb + t, 0))
    grid_spec = pltpu.PrefetchScalarGridSpec(
        num_scalar_prefetch=1, grid=(nb, 4),
        in_specs=[g_spec, pl.BlockSpec((None, tr, c), lambda t, j, ids_ref: (j, t, 0))],
        out_specs=[pl.BlockSpec((None, tr, c), lambda t, j, ids_ref: (j, t, 0)),
                   pl.BlockSpec((tr, c), lambda t, j, ids_ref: (t, 0))])
    return pl.pallas_call(
        body, name=name, grid_spec=grid_spec,
        out_shape=[jax.ShapeDtypeStruct((4, hr, c), BF16), jax.ShapeDtypeStruct((hr, c), F32)],
        compiler_params=_cparams(2),
    )(ids, grad, recv)


def _chip_sum(name, own, recv):
    hr, c = own.shape
    tr = _row_tile(hr, SUM_ROWS)

    def body(o_ref, r_ref, out_ref):
        out_ref[...] = ((o_ref[...] + r_ref[0].astype(F32)) + r_ref[1].astype(F32)) + r_ref[2].astype(F32)

    return pl.pallas_call(
        body, name=name, grid=(hr // tr,),
        in_specs=[pl.BlockSpec((tr, c), lambda i: (i, 0)), pl.BlockSpec((3, tr, c), lambda i: (0, i, 0))],
        out_specs=pl.BlockSpec((tr, c), lambda i: (i, 0)),
        out_shape=jax.ShapeDtypeStruct((hr, c), F32), compiler_params=_cparams(1),
    )(own, recv)


def _chip_sum_small(own, recv, ids):
    def body(ids_ref, o_ref, r_ref, out_ref):
        j = ids_ref[1]
        total = None
        for i in range(4):
            m = jnp.bitwise_xor(i, j)
            term = jnp.where(m == 0, o_ref[...], jnp.where(m == 2, r_ref[0], jnp.where(m == 1, r_ref[1], r_ref[2])))
            total = term if total is None else total + term
        out_ref[...] = total

    grid_spec = pltpu.PrefetchScalarGridSpec(
        num_scalar_prefetch=1, grid=(1,),
        in_specs=[pl.BlockSpec(own.shape, lambda i, ids_ref: (0, 0)), pl.BlockSpec(recv.shape, lambda i, ids_ref: (0, 0, 0))],
        out_specs=pl.BlockSpec(own.shape, lambda i, ids_ref: (0, 0)))
    return pl.pallas_call(body, name="chip_sum_small", grid_spec=grid_spec,
                          out_shape=jax.ShapeDtypeStruct(own.shape, F32), compiler_params=_cparams(1))(ids, own, recv)


def _adamw(name, w, m, v, mine, theirs, ids):
    lead = (None,) * (w.ndim - 2)
    rows, cols = w.shape[-2:]
    half = rows // 2
    tr = _row_tile(half, ADAM_ROWS, unit=8)
    nb = half // tr
    c1 = 1.0 / (1.0 - ADAM_B1 ** ADAM_STEP)
    c2 = 1.0 / (1.0 - ADAM_B2 ** ADAM_STEP)

    def body(ids_ref, w_ref, m_ref, v_ref, mine_ref, theirs_ref, g_out, d_out, m_out, v_out):
        g = jnp.where(pl.program_id(0) == ids_ref[0], mine_ref[...], theirs_ref[...])
        m_new = ADAM_B1 * m_ref[...] + (1.0 - ADAM_B1) * g
        v_new = ADAM_B2 * v_ref[...] + (1.0 - ADAM_B2) * (g * g)
        d_out[...] = -ADAM_LR * ((m_new * c1) / (jnp.sqrt(v_new * c2) + ADAM_EPS) + ADAM_WD * w_ref[...])
        g_out[...] = g
        m_out[...] = m_new
        v_out[...] = v_new

    full = pl.BlockSpec(lead + (tr, cols), lambda h, i, ids_ref: (0,) * len(lead) + (h * nb + i, 0))
    part = pl.BlockSpec((tr, cols), lambda h, i, ids_ref: (i, 0))
    grid_spec = pltpu.PrefetchScalarGridSpec(num_scalar_prefetch=1, grid=(2, nb),
                                             in_specs=[full, full, full, part, part], out_specs=[full] * 4)
    return pl.pallas_call(
        body, name=name, grid_spec=grid_spec,
        out_shape=[jax.ShapeDtypeStruct(w.shape, F32)] * 4, compiler_params=_cparams(2),
    )(ids, w, m, v, mine, theirs)


def _adamw_whole(name, w, m, v, g):
    rows, cols = w.shape[-2:]
    tr = _row_tile(rows, 2 * ADAM_ROWS, unit=8)
    c1 = 1.0 / (1.0 - ADAM_B1 ** ADAM_STEP)
    c2 = 1.0 / (1.0 - ADAM_B2 ** ADAM_STEP)

    def body(w_ref, m_ref, v_ref, g_ref, g_out, d_out, m_out, v_out):
        g = g_ref[...]
        m_new = ADAM_B1 * m_ref[...] + (1.0 - ADAM_B1) * g
        v_new = ADAM_B2 * v_ref[...] + (1.0 - ADAM_B2) * (g * g)
        d_out[...] = -ADAM_LR * ((m_new * c1) / (jnp.sqrt(v_new * c2) + ADAM_EPS) + ADAM_WD * w_ref[...])
        g_out[...] = g
        m_out[...] = m_new
        v_out[...] = v_new

    full = pl.BlockSpec((None, tr, cols), lambda i: (0, i, 0))
    return pl.pallas_call(
        body, name=name, grid=(rows // tr,), in_specs=[full, full, full, pl.BlockSpec((tr, cols), lambda i: (i, 0))],
        out_specs=[full] * 4, out_shape=[jax.ShapeDtypeStruct(w.shape, F32)] * 4, compiler_params=_cparams(1),
    )(w, m, v, g)


def _adamw_replicated(g_rows, ws, ms, vs):
    n = len(ws)
    layout, _ = _rep_rows()
    c1 = 1.0 / (1.0 - ADAM_B1 ** ADAM_STEP)
    c2 = 1.0 / (1.0 - ADAM_B2 ** ADAM_STEP)

    def body(g_ref, *refs):
        w_refs, m_refs, v_refs = refs[0:n], refs[n:2 * n], refs[2 * n:3 * n]
        outs = refs[3 * n:]
        for k, (r0, rows) in enumerate(layout):
            width = w_refs[k].shape[1]
            g = jnp.concatenate([g_ref[r0 + j:r0 + j + 1, :] for j in range(rows)], axis=1)[:, 0:width]
            m_new = ADAM_B1 * m_refs[k][...] + (1.0 - ADAM_B1) * g
            v_new = ADAM_B2 * v_refs[k][...] + (1.0 - ADAM_B2) * (g * g)
            outs[k][...] = g
            outs[n + k][...] = -ADAM_LR * ((m_new * c1) / (jnp.sqrt(v_new * c2) + ADAM_EPS) + ADAM_WD * w_refs[k][...])
            outs[2 * n + k][...] = m_new
            outs[3 * n + k][...] = v_new

    res = pl.pallas_call(body, name="adamw_replicated",
                         out_shape=[jax.ShapeDtypeStruct(w.shape, F32) for _ in range(4) for w in ws])(g_rows, *ws, *ms, *vs)
    return [res[k * n:(k + 1) * n] for k in range(4)]


def _small_shard(parts):
    return _flatten(parts, _BIG[-1][1])


_ENTRY = {e[0]: e for e in _BIG}
_GRAD_ENTRY = {**_ENTRY, "w_in": ("w_in", IN_SHARD, D_MODEL, "chip_cols")}
FFN_MATS = ("w_ffn_down", "w_ffn_up")
MIXER_MATS = ("w_mix_out", "w_ssm_out", "w_attn_out")


class _StepPlan:
    def __init__(self, w, late_shards, shards, ids):
        self.w, self.g = w, {}
        self.late_shards, self.shards, self.ids = late_shards, shards, ids
        self.sums, self.halves, self.results = {}, {}, {}

    def run(self, name, fn, *args, **kw):
        at = getattr(self, "_at_" + name, None)
        if at is None:
            return fn(*args, **kw)
        exchange, landed = at()
        res, extra = fn(*args, bg=exchange, **kw)
        landed(extra)
        return res

    def _at_ssd_fwd(self):
        def landed(fulls):
            self.partly_gathered = fulls

        return _gather_ici([_ENTRY[n] for n in MIXER_MATS], [self.late_shards[n] for n in MIXER_MATS]), landed

    def _at_attn_fwd(self):
        stages = (_gather_pass_on([_ENTRY[n] for n in MIXER_MATS], self.partly_gathered),
                  _gather_ici([_ENTRY[n] for n in FFN_MATS], [self.late_shards[n] for n in FFN_MATS]))

        def landed(extra):
            mixer, self.partly_gathered = _split(stages, extra)
            self.w.update(zip(MIXER_MATS, mixer))

        return _join(*stages), landed

    def _at_ssm_gate_norm(self):
        return (_gather_pass_on([_ENTRY[n] for n in FFN_MATS], self.partly_gathered),
                lambda fulls: self.w.update(zip(FFN_MATS, fulls)))

    def pair_sums(self, names, grads, recv):
        for n, gr, rv in zip(names, grads, recv):
            _, r, c, lay = _GRAD_ENTRY[n]
            self.sums[n] = _pair_sum("pair_sum_" + n, gr, rv, self.ids, r, c, lay)

    def chip_sums(self, names, recv):
        for n, rv in zip(names, recv):
            self.halves[n] = _chip_sum("chip_sum_" + n, self.sums[n][1], rv)

    def adamw(self, names, theirs):
        for n, th in zip(names, theirs):
            sh = self.shards[n]
            if n == "w_in":
                mine_first = self.ids[0] == 0
                g_t = jnp.where(mine_first, jnp.concatenate([self.halves[n], th], axis=1),
                                jnp.concatenate([th, self.halves[n]], axis=1))
                res = _adamw_whole("adamw_" + n, *[jnp.swapaxes(sh[k], -1, -2) for k in ("w", "m", "v")], g_t)
                self.results[n] = [jnp.swapaxes(r, -1, -2) for r in res]
            else:
                self.results[n] = _adamw("adamw_" + n, sh["w"], sh["m"], sh["v"], self.halves[n], th, self.ids)

    def _pair_stage(self, names, grads):
        return (_pair_exchange([_GRAD_ENTRY[n] for n in names], grads),
                lambda recv: self.pair_sums(names, grads, recv))

    def _at_ffn_up_dx(self):
        return self._pair_stage(FFN_MATS, [self.g[n] for n in FFN_MATS])

    def _at_ssm_gate_norm_bwd(self):
        return self._pair_stage(MIXER_MATS, [self.g[n] for n in MIXER_MATS])

    def _at_attn_bwd(self):
        return _chip_exchange([self.sums[n][0] for n in FFN_MATS]), lambda recv: self.chip_sums(FFN_MATS, recv)

    def _at_ssd_bwd(self):
        stages = (_chip_exchange([self.sums[n][0] for n in MIXER_MATS]),
                  _whole_to_sibling([self.halves[n] for n in FFN_MATS]))

        def landed(extra):
            recv, theirs = _split(stages, extra)
            self.chip_sums(MIXER_MATS, recv)
            self.adamw(FFN_MATS, theirs)

        return _join(*stages), landed

    def _at_ssm_conv_bwd(self):
        return _whole_to_sibling([self.halves[n] for n in MIXER_MATS]), lambda theirs: self.adamw(MIXER_MATS, theirs)

    def _at_in_proj_dx(self):
        grads = [_from_cat_t(self.g.pop("w_cat_t"))]
        self.pair_sums(("w_in",), grads,
                       _run_exchange("grad_pair_exchange_w_in", _pair_exchange([_GRAD_ENTRY["w_in"]], grads)))
        return _chip_exchange([self.sums["w_in"][0]]), lambda recv: self.chip_sums(("w_in",), recv)

    def finish(self, g_small, g_rep, rep_shards):
        stages = (_pair_exchange([_ENTRY["small"]], [g_small]), _whole_to_sibling([g_rep]))
        recv_small, recv_rep = _split(stages, _run_exchange("grad_pair_exchange_tail", _join(*stages)))
        self.pair_sums(("small",), [g_small], recv_small)
        p_rep, = _rowwise("pair_sum_replicated", lambda r0, a, b: [a + b], SMALL_ROWS, SMALL_ROWS,
                          [(g_rep, LANES, 0), (recv_rep[0], LANES, 0)], [], [(LANES, F32)], [])
        stages = (_chip_exchange([self.sums["small"][0]]), _to_all_chips(p_rep))
        recv, recv_rep = _split(stages, _run_exchange("grad_chip_exchange_tail", _join(*stages)))
        self.chip_sums(("small",), recv)
        g_rep_tot = _chip_sum_small(p_rep, recv_rep[0], self.ids)
        last = ("w_in", "small")
        self.adamw(last, _run_exchange("grad_half_share_tail", _whole_to_sibling([self.halves[n] for n in last])))
        self.results["replicated"] = _adamw_replicated(g_rep_tot, rep_shards["w"], rep_shards["m"], rep_shards["v"])
        return g_rep_tot[_rep_rows()[1], 0]


def kernel(x, meta_tokens, norm_pre_mix, w_in, ssm_conv_w, ssm_conv_b, ssm_dt_bias, ssm_a_log, ssm_d_skip, ssm_norm, w_ssm_out, attn_sinks, w_attn_out, w_mix_out, norm_post_mix, norm_pre_ffn, w_ffn_up, ffn_conv_w, ffn_conv_b, w_ffn_down, norm_post_ffn, loss_target, m_meta_tokens, m_norm_pre_mix, m_w_in, m_ssm_conv_w, m_ssm_conv_b, m_ssm_dt_bias, m_ssm_a_log, m_ssm_d_skip, m_ssm_norm, m_w_ssm_out, m_attn_sinks, m_w_attn_out, m_w_mix_out, m_norm_post_mix, m_norm_pre_ffn, m_w_ffn_up, m_ffn_conv_w, m_ffn_conv_b, m_w_ffn_down, m_norm_post_ffn, v_meta_tokens, v_norm_pre_mix, v_w_in, v_ssm_conv_w, v_ssm_conv_b, v_ssm_dt_bias, v_ssm_a_log, v_ssm_d_skip, v_ssm_norm, v_w_ssm_out, v_attn_sinks, v_w_attn_out, v_w_mix_out, v_norm_post_mix, v_norm_pre_ffn, v_w_ffn_up, v_ffn_conv_w, v_ffn_conv_b, v_w_ffn_down, v_norm_post_ffn):
    args = dict(locals())
    squeeze = lambda a: a.reshape(a.shape[-2:])
    wts = {n: squeeze(args[n]) for n in WEIGHT_ORDER}
    mom = {n: squeeze(args["m_" + n]) for n in WEIGHT_ORDER}
    var = {n: squeeze(args["v_" + n]) for n in WEIGHT_ORDER}
    x_i, y_i, c_i = _mesh_pos()
    ids = jnp.stack([c_i, _chip_index(x_i, y_i)]).astype(jnp.int32)
    big_names = [n for n, _, _, _ in _BIG[:-1]]
    small_names = [n for n, _, _ in _SMALL_SHARDED]
    rep_names = [n for n, _ in _REPLICATED]

    stacks = {"w": wts, "m": mom, "v": var}
    shards = {n: {"w": args[n], "m": args["m_" + n], "v": args["v_" + n]} for n in big_names}
    shards["small"] = {k: _small_shard([d[n] for n in small_names]) for k, d in stacks.items()}
    rep_shards = {k: [d[n] for n in rep_names] for k, d in stacks.items()}

    w_in4, small_all = _gather_weights([_ENTRY["w_in"], _ENTRY["small"]], [wts["w_in"].astype(BF16), shards["small"]["w"]])
    w = {n: wts[n] for n in rep_names}
    w["w_cat"] = _to_cat(w_in4)
    small_parts = [_unflatten(small_all[i], [shp for _, shp, _ in _SMALL_SHARDED]) for i in range(4)]
    for k, (n, _, axis) in enumerate(_SMALL_SHARDED):
        w[n] = jnp.concatenate([small_parts[i][k] for i in range(4)], axis=axis)
    plan = _StepPlan(w, {n: wts[n].astype(BF16) for n in MIXER_MATS + FFN_MATS}, shards, ids)

    head = jnp.concatenate([jnp.zeros((PAD, D_MODEL), F32), w["meta_tokens"]], axis=0)
    loss_sum, dx, dhead = _local_step(x[0], head, loss_target[0], plan)
    g = plan.g
    g["meta_tokens"] = dhead[PAD:]
    g_small = jnp.stack([_small_shard([_shard_of(g[n], i, shp, ax) for n, shp, ax in _SMALL_SHARDED]) for i in range(4)])
    loss_part = (loss_sum * (0.5 / D_MODEL)).reshape(1, 1)
    loss = plan.finish(g_small, _in_rows([g[n] for n in rep_names] + [loss_part]), rep_shards)

    results = {}
    for kind in range(4):
        results.update({(kind, n): plan.results[n][kind] for n in big_names})
        parts = _unflatten(plan.results["small"][kind], [shp for _, shp, _ in _SMALL_SHARDED])
        results.update({(kind, n): parts[k] for k, n in enumerate(small_names)})
        results.update({(kind, n): plan.results["replicated"][kind][k] for k, n in enumerate(rep_names)})
    outs = [results[kind, n].reshape(args[n].shape) for kind in range(4) for n in WEIGHT_ORDER]
    return (loss, dx[None], *outs)
```

```python
import math
from typing import Any, Callable, NamedTuple, Sequence

import jax
import jax.numpy as jnp
from jax import lax
from jax.experimental import pallas as pl
from jax.experimental.pallas import tpu as pltpu

F32 = jnp.float32
BF16 = jnp.bfloat16

D_MODEL = 1024
N_META = 16
T = 128
PAD = T - N_META
D_INNER = 2048
SSM_HEADS = 32
HEAD_P = 64
SSM_GROUPS = 4
GROUP_W = D_INNER // SSM_GROUPS
D_STATE = 128
CONV_DIM = D_INNER + 2 * SSM_GROUPS * D_STATE
ATTN_HEADS = 16
KV_HEADS = 4
ATTN_W = 1024
KV_W = 256
FFN_DIM = 2816
N_IN = 8736
EPS = 1e-6
NEG = -1e30
SCALE = 0.125

P_Q, P_K, P_V, P_DT, P_Z, P_GATE, P_XBC = 0, 1024, 1280, 1536, 2048, 4096, 6144
QKV_W = 1536
P_W = 9216

ADAM_LR, ADAM_B1, ADAM_B2, ADAM_EPS, ADAM_WD, ADAM_STEP = 0.001, 0.9, 0.999, 1e-08, 0.01, 10

VMEM_BUDGET = 40 * 1024 * 1024
VMEM_LIMIT = 56 * 1024 * 1024
MESH = pl.DeviceIdType.MESH
ANY = pl.BlockSpec(memory_space=pl.ANY)


def _cparams(n_axes, **kw):
    return pltpu.CompilerParams(dimension_semantics=("arbitrary",) * n_axes, vmem_limit_bytes=VMEM_LIMIT, **kw)


class _Exchange(NamedTuple):
    ins: Sequence[Any]
    out_shapes: Sequence[Any]
    make_copies: Callable
    n_copies: int
    aliases: dict = {}


def _call(body, name, grid, in_specs, out_specs, out_shape, operands, scratch_shapes=(), aliases=None, bg=None):
    aliases = dict(aliases or {})
    if bg is None:
        return pl.pallas_call(body, name=name, grid=grid, in_specs=in_specs, out_specs=out_specs, out_shape=out_shape,
                              scratch_shapes=list(scratch_shapes), input_output_aliases=aliases,
                              compiler_params=_cparams(len(grid)))(*operands)
    n_in, n_out, n_scr = len(in_specs), len(out_specs), len(scratch_shapes)
    nb_in, nb_out = len(bg.ins), len(bg.out_shapes)

    def hosted(*refs):
        ins, bg_ins = refs[:n_in], refs[n_in:n_in + nb_in]
        outs = refs[n_in + nb_in:n_in + nb_in + n_out]
        bg_outs = refs[n_in + nb_in + n_out:n_in + nb_in + n_out + nb_out]
        scratch = refs[n_in + nb_in + n_out + nb_out:n_in + nb_in + n_out + nb_out + n_scr]
        send_sems, recv_sems = refs[-2:]
        pids = [pl.program_id(a) for a in range(len(grid))]
        first, last = pids[0] == 0, pids[0] == grid[0] - 1
        for p, g in zip(pids[1:], grid[1:]):
            first, last = first & (p == 0), last & (p == g - 1)
        copies = []
        for k, (src, dst, peer) in enumerate(bg.make_copies(bg_ins, bg_outs)):
            if peer is None:
                copies.append(pltpu.make_async_copy(src, dst, send_sems.at[k]))
            else:
                copies.append(pltpu.make_async_remote_copy(src_ref=src, dst_ref=dst, send_sem=send_sems.at[k],
                                                           recv_sem=recv_sems.at[k], device_id=peer, device_id_type=MESH))
        assert len(copies) == bg.n_copies

        @pl.when(first)
        def _():
            for cp in copies:
                cp.start()

        body(*ins, *outs, *scratch)

        @pl.when(last)
        def _():
            for cp in copies:
                cp.wait()

    aliases = {(k if k < n_in else k + nb_in): v for k, v in aliases.items()}
    aliases.update({n_in + k: n_out + v for k, v in bg.aliases.items()})
    res = pl.pallas_call(
        hosted, name=name, grid=grid, in_specs=list(in_specs) + [ANY] * nb_in, out_specs=list(out_specs) + [ANY] * nb_out,
        out_shape=list(out_shape) + list(bg.out_shapes), input_output_aliases=aliases,
        scratch_shapes=list(scratch_shapes) + [pltpu.SemaphoreType.DMA((bg.n_copies,))] * 2,
        compiler_params=_cparams(len(grid), has_side_effects=True))(*operands, *bg.ins)
    return res[:n_out], res[n_out:]


def _sigmoid(x):
    return 1.0 / (1.0 + jnp.exp(-x))


def _silu(x):
    return x * _sigmoid(x)


def _silu_grad(x):
    s = _sigmoid(x)
    return x * s, s * (1.0 + x * (1.0 - s))


def _dsilu(x):
    return _silu_grad(x)[1]


def _softplus(x):
    e = jnp.exp(-jnp.abs(x))
    small = e * (1.0 - e * (0.5 - e * (1.0 / 3.0)))
    return jnp.maximum(x, 0.0) + jnp.where(e < 0.01, small, jnp.log(1.0 + e))


def _rms(x, w):
    r = lax.rsqrt(jnp.mean(x * x, axis=-1, keepdims=True) + EPS)
    return x * r * w


def _rms_bwd(dy, x, w):
    r = lax.rsqrt(jnp.mean(x * x, axis=-1, keepdims=True) + EPS)
    xh = x * r
    g = dy * w
    dx = r * (g - xh * jnp.mean(g * xh, axis=-1, keepdims=True))
    dw = jnp.sum(dy * xh, axis=0, keepdims=True)
    return dx, dw


def _dot(a, b):
    return jnp.dot(a, b, preferred_element_type=F32)


def _dot_nt(a, b):
    return lax.dot_general(a, b, (((1,), (1,)), ((), ())), preferred_element_type=F32)


def _dot_tn(a, b):
    return lax.dot_general(a, b, (((0,), (0,)), ((), ())), preferred_element_type=F32)


def _split3(x):
    hi = x.astype(BF16)
    r = x - hi.astype(F32)
    mid = r.astype(BF16)
    lo = (r - mid.astype(F32)).astype(BF16)
    return hi, mid, lo


def _xdot(x, e):
    hi, mid, lo = _split3(x)
    return _dot(hi, e) + _dot(mid, e) + _dot(lo, e)


def _xdot_l(e, x):
    hi, mid, lo = _split3(x)
    return _dot(e, hi) + _dot(e, mid) + _dot(e, lo)


def _iota(shape, dim):
    return lax.broadcasted_iota(jnp.int32, shape, dim)


def _divisors(n, unit):
    return [t for t in range(unit, n + 1, unit) if n % t == 0]


MIN_MATMUL_STEPS = 8
SMALL_MATMUL = 2 ** 33


def _matmul_tiles(m, n, k, a_bytes, b_bytes, o_bytes, m_unit):
    best = None
    for tm in _divisors(m, m_unit):
        for tn in _divisors(n, 128):
            for tk in _divisors(k, 128):
                acc = 0 if tk == k else tm * tn * 4
                vm = 2 * (tm * tk * a_bytes + tk * tn * b_bytes + tm * tn * o_bytes) + acc
                if vm > VMEM_BUDGET:
                    continue
                steps = (m // tm) * (n // tn) * (k // tk)
                want = MIN_MATMUL_STEPS if m * n * k >= SMALL_MATMUL else 2
                score = (tk == k, min(steps, want), min(tm, 256), tm * tn * tk)
                if best is None or score > best[0]:
                    best = (score, (tm, tn, tk))
    return best[1]


def _matmul(name, a, b, mode, out_dtype, bg=None):
    if mode == "nn":
        (m, k), n = a.shape, b.shape[1]
    elif mode == "nt":
        (m, k), n = a.shape, b.shape[0]
    else:
        (k, m), n = a.shape, b.shape[1]
    ab, bb, ob = a.dtype.itemsize, b.dtype.itemsize, jnp.dtype(out_dtype).itemsize
    tm, tn, tk = _matmul_tiles(m, n, k, ab, bb, ob, 128 if mode == "tn" else 16)
    nk = k // tk
    dot = {"nn": _dot, "nt": _dot_nt, "tn": _dot_tn}[mode]

    def body(a_ref, b_ref, o_ref, *scratch):
        prod = dot(a_ref[...].astype(BF16), b_ref[...].astype(BF16))
        if nk == 1:
            o_ref[...] = prod.astype(o_ref.dtype)
        else:
            acc_ref, = scratch
            kk = pl.program_id(2)

            @pl.when(kk == 0)
            def _():
                acc_ref[...] = prod

            @pl.when(kk > 0)
            def _():
                acc_ref[...] += prod

            @pl.when(kk == nk - 1)
            def _():
                o_ref[...] = acc_ref[...].astype(o_ref.dtype)

    a_spec = pl.BlockSpec((tk, tm), lambda i, j, kk: (kk, i)) if mode == "tn" else pl.BlockSpec((tm, tk), lambda i, j, kk: (i, kk))
    b_spec = pl.BlockSpec((tn, tk), lambda i, j, kk: (j, kk)) if mode == "nt" else pl.BlockSpec((tk, tn), lambda i, j, kk: (kk, j))
    res = _call(body, name, (m // tm, n // tn, nk), [a_spec, b_spec], [pl.BlockSpec((tm, tn), lambda i, j, kk: (i, j))],
                [jax.ShapeDtypeStruct((m, n), out_dtype)], [a, b],
                scratch_shapes=[] if nk == 1 else [pltpu.VMEM((tm, tn), F32)], bg=bg)
    return res[0] if bg is None else (res[0][0], res[1])


def _row_tile(n_rows, cap, unit=16):
    return max([t for t in _divisors(n_rows, unit) if t <= cap], default=n_rows)


ROW_SUB = 384
GROUP_UNROLL = 4


def _rowwise(name, fn, n_rows, tm, row_ins, full_ins, row_outs, acc_outs, bg=None):
    n_in = len(row_ins) + len(full_ins)
    n_ro = len(row_outs)
    into = [(k, o[3]) for k, o in enumerate(row_outs) if len(o) > 2 and o[2] == "into"]

    sub = min(tm, ROW_SUB)
    counts = [tm // T if len(e) > 3 and e[3] == "prev" else 1 for e in row_ins]
    assert all(cnt == 1 for cnt in counts) or sub == tm
    starts = [sum(counts[:k]) for k in range(len(counts))]
    n_row_in = sum(counts)
    n_in = n_row_in + len(full_ins)

    def body(*refs):
        i = pl.program_id(0)
        outs = refs[n_in + len(into):]

        sums = tuple(jnp.zeros((1, w), F32) for w in acc_outs)
        for s in range(tm // sub):
            rows = pl.ds(s * sub, sub)
            vals = [refs[st][rows, :] if cnt == 1 else jnp.concatenate([refs[st + k][...] for k in range(cnt)], axis=0)
                    for st, cnt in zip(starts, counts)]
            vals += [r[...] for r in refs[n_row_in:n_in]]
            res = fn(i * tm + s * sub, *vals)
            for o, r, v in zip(row_outs, outs[:n_ro], res[:n_ro]):
                if len(o) > 2 and o[2] == "first":
                    @pl.when(i == 0)
                    def _(r=r, v=v, rows=rows):
                        r[rows, :] = v.astype(r.dtype)
                else:
                    r[rows, :] = v.astype(r.dtype)
            sums = tuple(a + v for a, v in zip(sums, res[n_ro:]))

        @pl.when(i == 0)
        def _():
            for r, v in zip(outs[n_ro:], sums):
                r[...] = v

        @pl.when(i > 0)
        def _():
            for r, v in zip(outs[n_ro:], sums):
                r[...] += v

    def in_spec(entry, cnt):
        w, cb = entry[1], entry[2]
        if len(entry) > 3 and entry[3] == "prev":
            return [pl.BlockSpec((tm // cnt, w), lambda i, k=k: (jnp.maximum(cnt * i - 1 + k, 0), cb)) for k in range(cnt)]
        if len(entry) > 3 and entry[3] == "first":
            return [pl.BlockSpec((tm, w), lambda i: (0, cb))]
        return [pl.BlockSpec((tm, w), lambda i: (i, cb))]

    def out_spec(o):
        if len(o) == 2:
            return pl.BlockSpec((tm, o[0]), lambda i: (i, 0)), jax.ShapeDtypeStruct((n_rows, o[0]), o[1])
        if o[2] == "new":
            return pl.BlockSpec((tm, o[0]), lambda i: (i, o[4])), jax.ShapeDtypeStruct((n_rows, o[3]), o[1])
        if o[2] == "into":
            return pl.BlockSpec((tm, o[0]), lambda i: (i, o[4])), jax.ShapeDtypeStruct(o[3].shape, o[3].dtype)
        if o[2] == "first":
            return pl.BlockSpec((tm, o[0]), lambda i: (0, 0)), jax.ShapeDtypeStruct((tm, o[0]), o[1])
        return pl.BlockSpec((tm, o[0]), lambda i: (jnp.maximum(i - 1, 0), 0)), jax.ShapeDtypeStruct((o[3], o[0]), o[1])

    in_specs = [s for e, cnt in zip(row_ins, counts) for s in in_spec(e, cnt)]
    in_specs += [pl.BlockSpec(a.shape, lambda i: (0, 0)) for a in full_ins]
    in_specs += [pl.BlockSpec(memory_space=pl.ANY) for _ in into]
    specs_shapes = [out_spec(o) for o in row_outs]
    out_specs = [s for s, _ in specs_shapes] + [pl.BlockSpec((1, w), lambda i: (0, 0)) for w in acc_outs]
    out_shape = [s for _, s in specs_shapes] + [jax.ShapeDtypeStruct((1, w), F32) for w in acc_outs]
    return _call(body, name, (n_rows // tm,), in_specs, out_specs, out_shape,
                 [e[0] for e, cnt in zip(row_ins, counts) for _ in range(cnt)] + list(full_ins) + [arr for _, arr in into],
                 aliases={n_in + a: k for a, (k, _) in enumerate(into)}, bg=bg)


def _valid_rows(first_row, tm, lo):
    return (first_row + _iota((tm, 1), 0)) >= lo


CONV_ROWS = 128
CONV_SUB = 16
CONV_LANES = 256


def _conv_specs(tm, width, blk, n_rows, after):
    specs = [pl.BlockSpec((tm, width), lambda i: (i, blk)),
             pl.BlockSpec((8, width), lambda i: (jnp.maximum(i * (tm // 8) - 1, 0), blk))]
    if after:
        specs.append(pl.BlockSpec((16, width), lambda i: (jnp.minimum((i + 1) * (tm // 16), n_rows // 16 - 1), blk)))
    return specs


def _conv_window(win, w_ref, b_ref, taps, c0, cw, n):
    acc = b_ref[:, c0:c0 + cw] + w_ref[taps - 1:taps, c0:c0 + cw] * win[8:8 + n]
    for k in range(taps - 1):
        acc = acc + w_ref[k:k + 1, c0:c0 + cw] * win[8 - (taps - 1) + k:8 - (taps - 1) + k + n]
    return acc


def _ffn_act(name, u_raw, conv_w, conv_b, n_rows):
    tm, sub, cw = CONV_ROWS, CONV_SUB, CONV_LANES
    taps, width = conv_w.shape
    half = width // 2

    def body(cur_ref, prev_ref, w_ref, b_ref, f_ref, ext_ref):
        i = pl.program_id(0)
        ext_ref[0:8, :] = jnp.where(i > 0, prev_ref[...], 0.0)
        ext_ref[8:8 + tm, :] = cur_ref[...]
        for q in range(half // cw):
            a0, g0 = q * cw, half + q * cw

            def group(s, carry):
                r = pl.multiple_of(s * sub, sub)
                a = _conv_window(ext_ref[pl.ds(r, sub + 8), a0:a0 + cw], w_ref, b_ref, taps, a0, cw, sub)
                g = _conv_window(ext_ref[pl.ds(r, sub + 8), g0:g0 + cw], w_ref, b_ref, taps, g0, cw, sub)
                f_ref[pl.ds(r, sub), a0:a0 + cw] = (_silu(a) * g).astype(f_ref.dtype)
                return carry

            lax.fori_loop(0, tm // sub, group, 0, unroll=GROUP_UNROLL)

        @pl.when(i == 0)
        def _():
            f_ref[0:PAD, :] = jnp.zeros((PAD, half), f_ref.dtype)

    return pl.pallas_call(
        body, name=name, grid=(n_rows // tm,),
        in_specs=_conv_specs(tm, width, 0, n_rows, False) + [pl.BlockSpec((taps, width), lambda i: (0, 0)),
                                                             pl.BlockSpec((1, width), lambda i: (0, 0))],
        out_specs=pl.BlockSpec((tm, half), lambda i: (i, 0)),
        out_shape=jax.ShapeDtypeStruct((n_rows, half), BF16),
        scratch_shapes=[pltpu.VMEM((tm + 8, width), F32)],
        compiler_params=_cparams(1),
    )(u_raw, u_raw, conv_w, conv_b)


def _conv_bwd(name, raw, raw_blk, dsrcs, chunk_src, conv_w, conv_b, n_rows, gated, into=None, into_blk=0, bg=None):
    taps, width = conv_w.shape
    half = width // 2 if gated else width
    tm, sub, cw = CONV_ROWS, CONV_SUB, CONV_LANES
    te = tm + 16
    nd = len(dsrcs)
    n_parts = 2 if gated else 1

    def body(*refs):
        cur_ref, prev_ref, next_ref = refs[0:3]
        dcur, dnext = refs[3:3 + nd], refs[3 + nd:3 + 2 * nd]
        w_ref, b_ref = refs[3 + 2 * nd:5 + 2 * nd]
        out_ref, acc_ref, ext_ref, du_ref = refs[-4:]
        i = pl.program_id(0)
        ext_ref[0:8, :] = jnp.where(i > 0, prev_ref[...], 0.0)
        ext_ref[8:8 + tm, :] = cur_ref[...]
        ext_ref[8 + tm:24 + tm, :] = next_ref[...]

        for q, (src, off) in enumerate(chunk_src):
            cols = [q * cw, half + q * cw][:n_parts]

            def conv_grad(r, d, past_end):
                pre = [_conv_window(ext_ref[pl.ds(r, sub + 8), c0:c0 + cw], w_ref, b_ref, taps, c0, cw, sub) for c0 in cols]
                if gated:
                    act, dact = _silu_grad(pre[0])
                    dus = [d * pre[1] * dact, d * act]
                else:
                    dus = [d * _dsilu(pre[0])]
                for part, du in enumerate(dus):
                    if past_end:
                        du = jnp.where(i * tm + r + _iota((sub, 1), 0) < n_rows, du, 0.0)
                    du_ref[part, pl.ds(r, sub), :] = du

            def tile_rows(s, carry):
                r = pl.multiple_of(s * sub, sub)
                conv_grad(r, dcur[src][pl.ds(r, sub), off:off + cw].astype(F32), False)
                return carry

            lax.fori_loop(0, tm // sub, tile_rows, 0, unroll=GROUP_UNROLL)
            conv_grad(tm, dnext[src][:, off:off + cw].astype(F32), True)

            @pl.when(i == 0)
            def _():
                du_ref[:, 0:PAD, :] = jnp.zeros((n_parts, PAD, cw), F32)

            for part, c0 in enumerate(cols):
                taps_w = [w_ref[k:k + 1, c0:c0 + cw] for k in range(taps)]

                def back(s, sums):
                    new = list(sums)
                    for u in range(2):
                        r = pl.multiple_of((2 * s + u) * sub, sub)
                        win = du_ref[part, pl.ds(r, sub + 8), :]
                        raw_rows = ext_ref[pl.ds(8 + r, sub), c0:c0 + cw]
                        draw = jnp.zeros((sub, cw), F32)
                        for k in range(taps):
                            shifted = win[taps - 1 - k:taps - 1 - k + sub]
                            draw = draw + taps_w[k] * shifted
                            new[k] = new[k] + shifted * raw_rows
                        new[taps] = new[taps] + win[0:sub]
                        out_ref[pl.ds(r, sub), c0:c0 + cw] = draw.astype(out_ref.dtype)
                    return tuple(new)

                sums = lax.fori_loop(0, tm // (2 * sub), back, tuple(jnp.zeros((sub, cw), F32) for _ in range(taps + 1)))

                @pl.when(i == 0)
                def _(c0=c0):
                    out_ref[PAD - sub:PAD, c0:c0 + cw] = jnp.zeros((sub, cw), out_ref.dtype)

                for k in range(taps + 1):
                    total = jnp.sum(sums[k], axis=0, keepdims=True)
                    acc_ref[k:k + 1, c0:c0 + cw] = jnp.where(i == 0, total, acc_ref[k:k + 1, c0:c0 + cw] + total)

    in_specs = _conv_specs(tm, width, raw_blk, n_rows, True)
    in_specs += [pl.BlockSpec((tm, d.shape[1]), lambda i: (i, 0)) for d in dsrcs]
    in_specs += [pl.BlockSpec((16, d.shape[1]), lambda i: (jnp.minimum((i + 1) * (tm // 16), n_rows // 16 - 1), 0)) for d in dsrcs]
    in_specs += [pl.BlockSpec((taps, width), lambda i: (0, 0)), pl.BlockSpec((1, width), lambda i: (0, 0))]
    operands = [raw, raw, raw] + list(dsrcs) + list(dsrcs) + [conv_w, conv_b]
    aliases = {}
    if into is None:
        out0 = jax.ShapeDtypeStruct((n_rows, width), BF16)
    else:
        in_specs.append(pl.BlockSpec(memory_space=pl.ANY))
        operands.append(into)
        aliases = {len(operands) - 1: 0}
        out0 = jax.ShapeDtypeStruct(into.shape, into.dtype)
    return _call(body, name, (n_rows // tm,), in_specs,
                 [pl.BlockSpec((tm, width), lambda i: (i, into_blk)), pl.BlockSpec((8, width), lambda i: (0, 0))],
                 [out0, jax.ShapeDtypeStruct((8, width), F32)], operands,
                 scratch_shapes=[pltpu.VMEM((tm + 24, width), F32), pltpu.VMEM((n_parts, te + 8, cw), F32)],
                 aliases=aliases, bg=bg)


def _ssd_specs(n_chunks, rev, per_step=1):
    cidx = (lambda c: n_chunks - 1 - c) if rev else (lambda c: c)
    xw, nw = per_step * GROUP_W, per_step * D_STATE
    xg0, bg0, cg0 = P_XBC // xw, (P_XBC + D_INNER) // nw, (P_XBC + D_INNER + SSM_GROUPS * D_STATE) // nw

    def cur(width, blk0):
        return pl.BlockSpec((T, width), lambda g, c: (cidx(c), blk0 + g))

    def prev(width, blk0):
        return pl.BlockSpec((8, width), lambda g, c: (jnp.maximum(cidx(c) * (T // 8) - 1, 0), blk0 + g))

    specs = [cur(xw, xg0), prev(xw, xg0), cur(nw, bg0), prev(nw, bg0), cur(nw, cg0), prev(nw, cg0),
             pl.BlockSpec((T, 128), lambda g, c: (cidx(c), P_DT // 128))]
    wb, wc = D_INNER // nw, (D_INNER + SSM_GROUPS * D_STATE) // nw
    specs += [pl.BlockSpec((4, xw), lambda g, c: (0, g)),
              pl.BlockSpec((4, nw), lambda g, c: (0, wb + g)),
              pl.BlockSpec((4, nw), lambda g, c: (0, wc + g)),
              pl.BlockSpec((1, xw), lambda g, c: (0, g)),
              pl.BlockSpec((1, nw), lambda g, c: (0, wb + g)),
              pl.BlockSpec((1, nw), lambda g, c: (0, wc + g))]
    specs += [pl.BlockSpec((1, 128), lambda g, c: (0, 0))] * 3
    return specs, cidx


def _ssd_shared(refs, c):
    dt_ref, dtb_ref, alog_ref = refs[6], refs[13], refs[14]
    valid = _valid_rows(c * T, T, PAD)
    dtr = dt_ref[...] + dtb_ref[...]
    dt = jnp.where(valid, _softplus(dtr), 0.0)
    a_neg = -jnp.exp(alog_ref[...])
    tril = _iota((T, T), 0) >= _iota((T, T), 1)
    cs = _xdot_l(tril.astype(BF16), dt * a_neg)
    return dict(valid=valid, dtr=dtr, dt=dt, a_neg=a_neg, tril=tril, cs=cs, cs_t=cs.T)


def _heads_of_lanes():
    hh_t, ll_t = _iota((D_INNER, 128), 1), _iota((D_INNER, 128), 0)
    return (hh_t == jnp.right_shift(ll_t, 6)).astype(BF16)


def _ssd_chunk_forward(refs, ext_ref, g, c, shared):
    (xc_ref, xp_ref, bc_ref, bp_ref, cc_ref, cp_ref, dt_ref, wx_ref, wb_ref, wc_ref,
     bx_ref, bb_ref, bcb_ref, dtb_ref, alog_ref, dsk_ref) = refs

    def conv_pre(cur_ref, prev_ref, w_ref, b_ref, width):
        ext_ref[0:8, 0:width] = jnp.where(c > 0, prev_ref[...], 0.0)
        ext_ref[8:8 + T, 0:width] = cur_ref[...]
        w = w_ref[...]
        acc = b_ref[...] + w[3:4] * cur_ref[...]
        for k in range(3):
            acc = acc + w[k:k + 1] * ext_ref[pl.ds(5 + k, T), 0:width]
        return acc

    v = dict(shared)
    valid = v["valid"]
    v["head0"] = 8 * g
    v["x_pre"] = conv_pre(xc_ref, xp_ref, wx_ref, bx_ref, GROUP_W)
    v["b_pre"] = conv_pre(bc_ref, bp_ref, wb_ref, bb_ref, D_STATE)
    v["c_pre"] = conv_pre(cc_ref, cp_ref, wc_ref, bcb_ref, D_STATE)
    xs = _silu(v["x_pre"])
    bm = jnp.where(valid, _silu(v["b_pre"]), 0.0)
    cm = jnp.where(valid, _silu(v["c_pre"]), 0.0)
    hh, ll = _iota((128, GROUP_W), 0), _iota((128, GROUP_W), 1)
    expand = (hh == 8 * g + jnp.right_shift(ll, 6)).astype(BF16)
    cs_e = _xdot(v["cs"], expand)
    dt_e = _xdot(v["dt"], expand)
    cs_last_e = cs_e[T - 1:T, :]
    v.update(xs=xs, bm=bm, cm=cm, cs_e=cs_e, dt_e=dt_e, cs_last_e=cs_last_e)
    v["xdt"] = xs * dt_e
    v["decay_e"] = jnp.exp(cs_last_e - cs_e)
    v["ecs_e"] = jnp.exp(cs_e)
    v["elast_e"] = jnp.exp(cs_last_e)
    v["d_e"] = _xdot(dsk_ref[...], expand)
    v["gmat"] = _dot_nt(cm.astype(BF16), bm.astype(BF16))
    return v


def _ssd_decay_pair(v, jp):
    out = []
    for j in (v["head0"] + 2 * jp, v["head0"] + 2 * jp + 1):
        diff = v["cs"][:, j:j + 1] - v["cs_t"][j:j + 1, :]
        out.append(jnp.where(v["tril"], jnp.exp(jnp.where(v["tril"], diff, 0.0)), 0.0))
    return out


def _block_diag_pair(xp):
    lane = _iota(xp.shape, 1)
    return jnp.concatenate([jnp.where(lane < HEAD_P, xp, 0.0), jnp.where(lane >= HEAD_P, xp, 0.0)], axis=0)


SSD_GROUPS_PER_STEP = 4


def _ssd_group_refs(refs, gg):
    x_w, n_w = pl.ds(GROUP_W * gg, GROUP_W), pl.ds(D_STATE * gg, D_STATE)
    lanes = [x_w, x_w, n_w, n_w, n_w, n_w, None, x_w, n_w, n_w, x_w, n_w, n_w, None, None, None]
    return [r if w is None else r.at[:, w] for r, w in zip(refs, lanes)]


def _ssd_fwd(p, conv_w, conv_b, dt_bias, a_log, d_skip, n_chunks, bg=None):
    n_rows = n_chunks * T
    in_specs, _ = _ssd_specs(n_chunks, rev=False, per_step=SSD_GROUPS_PER_STEP)
    per = SSD_GROUPS_PER_STEP
    assert per == SSM_GROUPS

    def body(*refs):
        y_ref, hin_ref, st_ref, ext_ref = refs[16:]
        c = pl.program_id(1)

        @pl.when(c == 0)
        def _():
            st_ref[...] = jnp.zeros_like(st_ref)

        shared = _ssd_shared(refs[:16], c)
        for gg in range(per):
            v = _ssd_chunk_forward(_ssd_group_refs(refs[:16], gg), ext_ref.at[gg], gg, c, shared)
            state = st_ref[gg]
            hin_ref[gg] = state
            ys = []
            for jp in range(4):
                l0, l1 = _ssd_decay_pair(v, jp)
                lhs = jnp.concatenate([v["gmat"] * l0, v["gmat"] * l1], axis=1).astype(BF16)
                rhs = _block_diag_pair(v["xdt"][:, 128 * jp:128 * jp + 128]).astype(BF16)
                ys.append(_dot(lhs, rhs))
            y = jnp.concatenate(ys, axis=1)
            y = y + _dot(v["cm"].astype(BF16), state.astype(BF16)) * v["ecs_e"] + v["xs"] * v["d_e"]
            y_ref[:, GROUP_W * gg:GROUP_W * gg + GROUP_W] = y
            s_new = _dot_tn(v["bm"].astype(BF16), (v["xdt"] * v["decay_e"]).astype(BF16))
            st_ref[gg] = state * v["elast_e"] + s_new

    return _call(
        body, "ssd_fwd", (SSM_GROUPS // per, n_chunks), in_specs,
        [pl.BlockSpec((T, per * GROUP_W), lambda g, c: (c, g)),
         pl.BlockSpec((per, None, D_STATE, GROUP_W), lambda g, c: (g, c, 0, 0))],
        [jax.ShapeDtypeStruct((n_rows, D_INNER), F32),
         jax.ShapeDtypeStruct((SSM_GROUPS, n_chunks, D_STATE, GROUP_W), F32)],
        [p, p, p, p, p, p, p, conv_w, conv_w, conv_w, conv_b, conv_b, conv_b, dt_bias, a_log, d_skip],
        scratch_shapes=[pltpu.VMEM((per, D_STATE, GROUP_W), F32), pltpu.VMEM((per, T + 8, GROUP_W), F32)], bg=bg)


def _ssd_bwd(p, conv_w, conv_b, dt_bias, a_log, d_skip, hin, dy, dp, n_chunks, bg=None):
    n_rows = n_chunks * T
    per = SSD_GROUPS_PER_STEP
    assert per == SSM_GROUPS
    dt_w = P_Z - P_DT
    in_specs, cidx = _ssd_specs(n_chunks, rev=True, per_step=per)
    in_specs = in_specs + [pl.BlockSpec((per, None, D_STATE, GROUP_W), lambda g, c: (g, cidx(c), 0, 0)),
                           pl.BlockSpec((T, per * GROUP_W), lambda g, c: (cidx(c), g)), ANY]

    def body(*refs):
        hin_ref, dy_ref = refs[16:18]
        dx_ref, db_ref, dc_ref, dp_ref, dpar_ref, dst_ref, ext_ref, red_ref, dd_ref = refs[19:]
        step = pl.program_id(1)
        shared = _ssd_shared(refs[:16], n_chunks - 1 - step)
        local = jnp.zeros((T, 128), F32)
        for gg in range(per):
            x_w, n_w = pl.ds(GROUP_W * gg, GROUP_W), pl.ds(D_STATE * gg, D_STATE)
            local = local + group_body(_ssd_group_refs(refs[:16], gg), hin_ref.at[gg], dy_ref.at[:, x_w],
                                       dx_ref.at[:, x_w], db_ref.at[:, n_w], dc_ref.at[:, n_w], red_ref.at[:, :, x_w],
                                       dd_ref.at[:, x_w], dst_ref.at[gg], ext_ref.at[gg], gg, shared)
        to_heads = _heads_of_lanes()
        dcs = _xdot(red_ref[0], to_heads) + local
        triu = (_iota((T, T), 0) <= _iota((T, T), 1)).astype(BF16)
        da = _xdot_l(triu, dcs)
        ddt = da * shared["a_neg"] + _xdot(red_ref[1], to_heads)
        ddtr = jnp.where(shared["valid"], ddt * _sigmoid(shared["dtr"]), 0.0)
        dp_ref[...] = jnp.concatenate([ddtr, jnp.zeros((T, dt_w - 128), F32)], axis=1).astype(dp_ref.dtype)
        dpar = jnp.concatenate([
            jnp.sum(ddtr, axis=0, keepdims=True),
            jnp.sum(da * shared["dt"], axis=0, keepdims=True) * shared["a_neg"],
            _xdot(dd_ref[0:1, :], to_heads),
            jnp.zeros((5, 128), F32)], axis=0)
        dpar_ref[...] = jnp.where(step == 0, dpar, dpar_ref[...] + dpar)

    def group_body(in_refs, hin_ref, dy_ref, dx_ref, db_ref, dc_ref, red_ref, dd_ref, dst_ref, ext_ref, g, shared):
        step = pl.program_id(1)
        c = n_chunks - 1 - step

        @pl.when(step == 0)
        def _():
            dst_ref[...] = jnp.zeros_like(dst_ref)

        v = _ssd_chunk_forward(in_refs, ext_ref, g, c, shared)
        hin_f = hin_ref[...]
        hin_b = hin_f.astype(BF16)
        dyv = dy_ref[...]
        dst = dst_ref[...]
        dst_b = dst.astype(BF16)
        xs, bm, cm, xdt = v["xs"], v["bm"], v["cm"], v["xdt"]
        bm_b, cm_b = bm.astype(BF16), cm.astype(BF16)

        dd_e = jnp.sum(dyv * xs, axis=0, keepdims=True)
        dxs = dyv * v["d_e"]
        ch = _dot(cm_b, hin_b)
        dch = (dyv * v["ecs_e"]).astype(BF16)
        dcm = _dot_nt(dch, hin_b)
        dhin = _dot_tn(cm_b, dch) + dst * v["elast_e"]
        dcs_e = dyv * ch * v["ecs_e"]
        dxd = _dot(bm_b, dst_b)
        dbm = _dot_nt((xdt * v["decay_e"]).astype(BF16), dst_b)
        dxdt_state = dxd * v["decay_e"]
        q = dxdt_state * xdt
        dcs_e = dcs_e - q
        dlast_e = jnp.sum(q, axis=0, keepdims=True) + jnp.sum(dst * hin_f, axis=0, keepdims=True) * v["elast_e"]
        dg = jnp.zeros((T, T), F32)
        rs_cols = jnp.zeros((T, 128), F32)
        cs_rows = jnp.zeros((128, T), F32)
        lane_i, sub_i = _iota((T, 128), 1), _iota((128, T), 0)
        dxdt_parts = []
        for jp in range(4):
            l0, l1 = _ssd_decay_pair(v, jp)
            m0, m1 = v["gmat"] * l0, v["gmat"] * l1
            xbd = _block_diag_pair(xdt[:, 128 * jp:128 * jp + 128]).astype(BF16)
            dyp = dyv[:, 128 * jp:128 * jp + 128]
            dm = _dot_nt(dyp.astype(BF16), xbd)
            dm0, dm1 = dm[:, 0:T], dm[:, T:2 * T]
            dg = dg + dm0 * l0 + dm1 * l1
            for j, qq in ((v["head0"] + 2 * jp, dm0 * m0), (v["head0"] + 2 * jp + 1, dm1 * m1)):
                rs_cols = jnp.where(lane_i == j, jnp.sum(qq, axis=1, keepdims=True), rs_cols)
                cs_rows = jnp.where(sub_i == j, jnp.sum(qq, axis=0, keepdims=True), cs_rows)
            mv = jnp.concatenate([m0, m1], axis=0).astype(BF16)
            dxdt_parts.append(_dot_tn(mv, _block_diag_pair(dyp).astype(BF16)))
        dxdt = jnp.concatenate(dxdt_parts, axis=1) + dxdt_state
        dg_b = dg.astype(BF16)
        dcm = dcm + _dot(dg_b, bm_b)
        dbm = dbm + _dot_tn(dg_b, cm_b)
        last_row = _iota((T, 1), 0) == T - 1
        red_ref[0] = dcs_e + jnp.where(last_row, dlast_e, 0.0)
        red_ref[1] = dxdt * xs
        dd_ref[0:1, :] = dd_e
        dx_ref[...] = dxs + dxdt * v["dt_e"]
        db_ref[...] = jnp.where(v["valid"], dbm, 0.0)
        dc_ref[...] = jnp.where(v["valid"], dcm, 0.0)
        dst_ref[...] = dhin
        return rs_cols - cs_rows.T

    return _call(
        body, "ssd_bwd", (SSM_GROUPS // per, n_chunks), in_specs,
        [pl.BlockSpec((T, per * GROUP_W), lambda g, c: (cidx(c), g)),
         pl.BlockSpec((T, per * D_STATE), lambda g, c: (cidx(c), g)),
         pl.BlockSpec((T, per * D_STATE), lambda g, c: (cidx(c), g)),
         pl.BlockSpec((T, dt_w), lambda g, c: (cidx(c), P_DT // dt_w)),
         pl.BlockSpec((8, 128), lambda g, c: (0, 0))],
        [jax.ShapeDtypeStruct((n_rows, D_INNER), F32),
         jax.ShapeDtypeStruct((n_rows, SSM_GROUPS * D_STATE), F32),
         jax.ShapeDtypeStruct((n_rows, SSM_GROUPS * D_STATE), F32),
         jax.ShapeDtypeStruct(dp.shape, dp.dtype),
         jax.ShapeDtypeStruct((8, 128), F32)],
        [p, p, p, p, p, p, p, conv_w, conv_w, conv_w, conv_b, conv_b, conv_b, dt_bias, a_log, d_skip, hin, dy, dp],
        scratch_shapes=[pltpu.VMEM((per, D_STATE, GROUP_W), F32), pltpu.VMEM((per, T + 8, GROUP_W), F32),
                        pltpu.VMEM((2, T, D_INNER), F32), pltpu.VMEM((8, D_INNER), F32)],
        aliases={18: 3}, bg=bg)


def _alibi_slope(h):
    return 2.0 ** (-8.0 * (h + 1) / ATTN_HEADS)


def _dup_half(x256, kvh):
    xb = x256[:, 128 * (kvh // 2):128 * (kvh // 2) + 128]
    rolled = pltpu.roll(xb, 64, 1)
    lane = _iota(xb.shape, 1)
    if kvh % 2 == 0:
        return jnp.where(lane < 64, xb, rolled)
    return jnp.where(lane < 64, rolled, xb)


def _attn_masks(c):
    qi, j = _iota((T, T), 0), _iota((T, T), 1)
    tri = j <= qi
    meta_ok = (j >= PAD) & (j - PAD <= c * T + qi - PAD)
    band_ok = c >= jnp.where(tri, 1, 2)
    dist = jnp.bitwise_and(qi - j, T - 1).astype(F32)
    return tri, meta_ok, band_ok, dist


def _fold(x3, tri):
    return jnp.concatenate([x3[:, 0:T], jnp.where(tri, x3[:, 2 * T:3 * T], x3[:, T:2 * T])], axis=1)


def _unfold(x2, tri):
    band = x2[:, T:2 * T]
    return jnp.concatenate([x2[:, 0:T], jnp.where(tri, 0.0, band), jnp.where(tri, band, 0.0)], axis=1)


def _attn_fwd(p, sinks, n_chunks, bg=None):
    n_rows = n_chunks * T
    kb, vb = P_K // KV_W, P_V // KV_W

    def body(q_ref, kc_ref, kp_ref, km_ref, vc_ref, vp_ref, vm_ref, sink_ref, o_ref, lse_ref):
        c = pl.program_id(0)
        sinks_v = sink_ref[...]
        masks = _attn_masks(c)
        tri, meta_ok, band_ok, dist = masks
        lane = _iota((T, 128), 1)
        for kvh in range(KV_HEADS):
            k3 = jnp.concatenate([_dup_half(r[...], kvh) for r in (km_ref, kp_ref, kc_ref)], axis=0).astype(BF16)
            v3 = jnp.concatenate([_dup_half(r[...], kvh) for r in (vm_ref, vp_ref, vc_ref)], axis=0)
            v3bd = _block_diag_rows(v3).astype(BF16)
            q2 = q_ref[:, 256 * kvh:256 * kvh + 256] * SCALE
            q4 = jnp.concatenate([jnp.where((lane < 64) if half == 0 else (lane >= 64), q2[:, 128 * pr:128 * pr + 128], 0.0)
                                  for pr in range(2) for half in range(2)], axis=0).astype(BF16)
            raw4 = _dot_nt(q4, k3)
            probs = []
            for hh in range(4):
                h = 4 * kvh + hh
                raw = raw4[T * hh:T * hh + T]
                band = jnp.where(tri, raw[:, 2 * T:3 * T], raw[:, T:2 * T]) - _alibi_slope(h) * dist
                sc = jnp.concatenate([jnp.where(meta_ok, raw[:, 0:T], NEG), jnp.where(band_ok, band, NEG)], axis=1)
                sink = sinks_v[:, h:h + 1]
                m = jnp.maximum(jnp.max(sc, axis=1, keepdims=True), sink)
                e = jnp.exp(sc - m)
                den = jnp.sum(e, axis=1, keepdims=True) + jnp.exp(sink - m)
                probs.append(_unfold(e * (1.0 / den), tri))
                lse_ref[:, h:h + 1] = m + jnp.log(den)
            p4 = jnp.concatenate([jnp.concatenate(probs[0:2], axis=1), jnp.concatenate(probs[2:4], axis=1)], axis=0)
            out = _dot(p4.astype(BF16), v3bd)
            o_ref[:, 256 * kvh:256 * kvh + 256] = jnp.concatenate([out[0:T], out[T:2 * T]], axis=1).astype(o_ref.dtype)

    blk = lambda width, col: pl.BlockSpec((T, width), lambda c: (c, col))
    prev = lambda width, col: pl.BlockSpec((T, width), lambda c: (jnp.maximum(c - 1, 0), col))
    first = lambda width, col: pl.BlockSpec((T, width), lambda c: (0, col))
    return _call(
        body, "attn_fwd", (n_chunks,),
        [blk(ATTN_W, P_Q // ATTN_W), blk(KV_W, kb), prev(KV_W, kb), first(KV_W, kb),
         blk(KV_W, vb), prev(KV_W, vb), first(KV_W, vb), pl.BlockSpec((1, 128), lambda c: (0, 0))],
        [pl.BlockSpec((T, ATTN_W), lambda c: (c, 0)), pl.BlockSpec((T, 128), lambda c: (c, 0))],
        [jax.ShapeDtypeStruct((n_rows, ATTN_W), BF16), jax.ShapeDtypeStruct((n_rows, 128), F32)],
        [p, p, p, p, p, p, p, sinks], bg=bg)


def _block_diag_rows(x3):
    lane = _iota(x3.shape, 1)
    return jnp.concatenate([jnp.where(lane < 64, x3, 0.0), jnp.where(lane >= 64, x3, 0.0)], axis=0)


def _fold_halves(x):
    return x + pltpu.roll(x, 64, 1)


def _attn_bwd(p, sinks, ao, lse, dao, dp, n_chunks, bg=None):
    kb, vb = P_K // KV_W, P_V // KV_W
    rc = lambda s: n_chunks - 1 - s

    def body(q_ref, kc_ref, kp_ref, km_ref, vc_ref, vp_ref, vm_ref, sink_ref, o_ref, lse_ref, do_ref, dp_in_ref,
             dqkv_ref, dsink_ref, kcar_ref, vcar_ref, kmeta_ref, vmeta_ref):
        step = pl.program_id(0)
        c = n_chunks - 1 - step

        @pl.when(step == 0)
        def _():
            for r in (kcar_ref, vcar_ref, kmeta_ref, vmeta_ref):
                r[...] = jnp.zeros_like(r)

        masks = _attn_masks(c)
        tri = masks[0]
        q = q_ref[...] * SCALE
        sinks_v = sink_ref[...]
        lse_v = lse_ref[...]
        ov = o_ref[...].astype(F32)
        dov = do_ref[...].astype(F32)
        lane = _iota((T, 128), 1)
        lane256 = _iota((3 * T, KV_W), 1)
        dsink = jnp.zeros((1, 128), F32)
        dk3_all = jnp.zeros((3 * T, KV_W), F32)
        dv3_all = jnp.zeros((3 * T, KV_W), F32)
        dqs = []
        for kvh in range(KV_HEADS):
            k3 = jnp.concatenate([_dup_half(r[...], kvh) for r in (km_ref, kp_ref, kc_ref)], axis=0).astype(BF16)
            v3 = jnp.concatenate([_dup_half(r[...], kvh) for r in (vm_ref, vp_ref, vc_ref)], axis=0).astype(BF16)
            halves = [(pr, half, (lane < 64) if half == 0 else (lane >= 64)) for pr in range(2) for half in range(2)]
            cols = [slice(128 * (2 * kvh + pr), 128 * (2 * kvh + pr) + 128) for pr in range(2)]
            q4 = jnp.concatenate([jnp.where(mine, q[:, cols[pr]], 0.0) for pr, _, mine in halves], axis=0).astype(BF16)
            do4 = jnp.concatenate([jnp.where(mine, dov[:, cols[pr]], 0.0) for pr, _, mine in halves], axis=0).astype(BF16)
            raw4 = _dot_nt(q4, k3)
            dp4 = _dot_nt(do4, v3)
            ds_rows, pm_rows = [], []
            for hh, (pr, half, mine) in enumerate(halves):
                h = 4 * kvh + hh
                raw = raw4[T * hh:T * hh + T]
                band = jnp.where(tri, raw[:, 2 * T:3 * T], raw[:, T:2 * T]) - _alibi_slope(h) * masks[3]
                sc = jnp.concatenate([jnp.where(masks[1], raw[:, 0:T], NEG), jnp.where(masks[2], band, NEG)], axis=1)
                lse_h = lse_v[:, h:h + 1]
                pm = jnp.exp(sc - lse_h)
                prod = dov[:, cols[pr]] * ov[:, cols[pr]]
                delta = jnp.sum(jnp.where(mine, prod, 0.0), axis=1, keepdims=True)
                dp = _fold(dp4[T * hh:T * hh + T], tri)
                ds_rows.append(_unfold(pm * (dp - delta), tri))
                pm_rows.append(_unfold(pm, tri))
                p_sink = jnp.exp(sinks_v[:, h:h + 1] - lse_h)
                dsink = jnp.where(_iota((1, 128), 1) == h, jnp.sum(-p_sink * delta, axis=0, keepdims=True), dsink)
            ds4 = jnp.concatenate(ds_rows, axis=0).astype(BF16)
            dq4 = _dot(ds4, k3)
            dk3 = _dot_tn(ds4, q4)
            dv3 = _dot_tn(jnp.concatenate(pm_rows, axis=0).astype(BF16), do4)
            for pr in range(2):
                dqs.append(jnp.where(lane < 64, dq4[2 * T * pr:2 * T * pr + T], dq4[2 * T * pr + T:2 * T * pr + 2 * T]) * SCALE)
            in_place = (lane256 >= 64 * kvh) & (lane256 < 64 * kvh + 64)
            wide = lambda x: jnp.concatenate([x, x], axis=1)
            dk3_all = jnp.where(in_place, wide(_fold_halves(dk3)), dk3_all)
            dv3_all = jnp.where(in_place, wide(_fold_halves(dv3)), dv3_all)
        dsink_all = dsink

        @pl.when(step == 0)
        def _():
            dsink_ref[...] = dsink_all

        @pl.when(step > 0)
        def _():
            dsink_ref[...] += dsink_all

        kmeta = kmeta_ref[...] + dk3_all[0:T]
        vmeta = vmeta_ref[...] + dv3_all[0:T]
        kmeta_ref[...] = kmeta
        vmeta_ref[...] = vmeta
        is_first = c == 0
        dk = jnp.where(is_first, kmeta, dk3_all[2 * T:3 * T] + kcar_ref[...])
        dv = jnp.where(is_first, vmeta, dv3_all[2 * T:3 * T] + vcar_ref[...])
        dqkv_ref[...] = jnp.concatenate(dqs + [dk, dv], axis=1).astype(dqkv_ref.dtype)
        kcar_ref[...] = dk3_all[T:2 * T]
        vcar_ref[...] = dv3_all[T:2 * T]

    blk = lambda width, col: pl.BlockSpec((T, width), lambda s: (rc(s), col))
    prev = lambda width, col: pl.BlockSpec((T, width), lambda s: (jnp.maximum(rc(s) - 1, 0), col))
    first = lambda width, col: pl.BlockSpec((T, width), lambda s: (0, col))
    return _call(
        body, "attn_bwd", (n_chunks,),
        [blk(ATTN_W, P_Q // ATTN_W), blk(KV_W, kb), prev(KV_W, kb), first(KV_W, kb),
         blk(KV_W, vb), prev(KV_W, vb), first(KV_W, vb), pl.BlockSpec((1, 128), lambda s: (0, 0)),
         blk(ATTN_W, 0), blk(128, 0), blk(ATTN_W, 0), ANY],
        [blk(QKV_W, P_Q // QKV_W), pl.BlockSpec((1, 128), lambda s: (0, 0))],
        [jax.ShapeDtypeStruct(dp.shape, dp.dtype), jax.ShapeDtypeStruct((1, 128), F32)],
        [p, p, p, p, p, p, p, sinks, ao, lse, dao, dp],
        scratch_shapes=[pltpu.VMEM((T, KV_W), F32)] * 4, aliases={11: 0}, bg=bg)


def _pad_lanes(v, width=128):
    return jnp.pad(v, ((0, 0), (0, width - v.shape[1])))


def _local_step(x, head, tgt, plan):
    w, g, run = plan.w, plan.g, plan.run
    n_tok = x.shape[0]
    n_rows = n_tok + T
    n_chunks = n_rows // T
    tm = _row_tile(n_rows, 384)
    dt_bias, a_log, d_skip = (_pad_lanes(w[k]) for k in ("ssm_dt_bias", "ssm_a_log", "ssm_d_skip"))
    sinks = _pad_lanes(w["attn_sinks"])
    x_in = [(x, D_MODEL, 0, "prev"), (head, D_MODEL, 0, "first")]
    head_tm = jnp.concatenate([head, jnp.zeros((tm - T, D_MODEL), F32)], axis=0)
    x_in_tm = [(x, D_MODEL, 0, "prev"), (head_tm, D_MODEL, 0, "first")]

    def h0_tile(r0, xt, hd):
        return jnp.where(_valid_rows(r0, xt.shape[0], T), xt, hd)

    n1, = _rowwise("norm_pre_mix", lambda r0, xt, hd, wn: [_rms(h0_tile(r0, xt, hd), wn)], n_rows, tm,
                   x_in_tm, [w["norm_pre_mix"]], [(D_MODEL, BF16)], [])
    p = _matmul("in_proj", n1, w["w_cat"], "nn", F32)
    y_ssd, hin = run("ssd_fwd", _ssd_fwd, p, w["ssm_conv_w"], w["ssm_conv_b"], dt_bias, a_log, d_skip, n_chunks)
    ao, lse = run("attn_fwd", _attn_fwd, p, sinks, n_chunks)

    def gate_norm(r0, y, z, wn):
        return [_rms(y * _silu(z), wn)]

    yn, = run("ssm_gate_norm", _rowwise, "ssm_gate_norm", gate_norm, n_rows, tm,
              [(y_ssd, D_INNER, 0), (p, D_INNER, P_Z // D_INNER)], [w["ssm_norm"]], [(D_INNER, BF16)], [])
    y_ssm = _matmul("ssm_out", yn, w["w_ssm_out"], "nn", F32)
    y_attn = _matmul("attn_out", ao, w["w_attn_out"], "nn", F32)

    def mix_gate(r0, ys, ya, gs, ga):
        return [_sigmoid(gs) * ys + _sigmoid(ga) * ya]

    gate_ins = [(p, D_MODEL, P_GATE // D_MODEL), (p, D_MODEL, P_GATE // D_MODEL + 1)]
    mixed, = _rowwise("mix_gate", mix_gate, n_rows, tm, [(y_ssm, D_MODEL, 0), (y_attn, D_MODEL, 0)] + gate_ins,
                      [], [(D_MODEL, BF16)], [])
    mix = _matmul("mix_out", mixed, w["w_mix_out"], "nn", F32)

    def post_mix(r0, mx, xt, hd, w_post, w_pre):
        h1 = jnp.where(_valid_rows(r0, mx.shape[0], PAD), h0_tile(r0, xt, hd) + _rms(mx, w_post), 0.0)
        return [h1, _rms(h1, w_pre)]

    h1, n2 = _rowwise("post_mix", post_mix, n_rows, tm, [(mix, D_MODEL, 0)] + x_in_tm,
                      [w["norm_post_mix"], w["norm_pre_ffn"]], [(D_MODEL, F32), (D_MODEL, BF16)], [])
    u_raw = _matmul("ffn_up", n2, w["w_ffn_up"], "nn", F32)
    f = _ffn_act("ffn_act", u_raw, w["ffn_conv_w"], w["ffn_conv_b"], n_rows)
    ffn = _matmul("ffn_down", f, w["w_ffn_down"], "nn", F32)

    def final(r0, fo, h, t, w_post):
        real = _valid_rows(r0, fo.shape[0], T)
        err = jnp.where(real, h + _rms(fo, w_post) - t, 0.0)
        dy = err * (1.0 / D_MODEL)
        dffn, dw = _rms_bwd(dy, fo, w_post)
        return [dffn, dy, jnp.sum(err * err, axis=0, keepdims=True), dw]

    dffn, dh2, loss_cols, g_norm_post_ffn = _rowwise(
        "loss_head", final, n_rows, tm, [(ffn, D_MODEL, 0), (h1, D_MODEL, 0), (tgt, D_MODEL, 0, "prev")],
        [w["norm_post_ffn"]], [(D_MODEL, BF16), (D_MODEL, F32)], [D_MODEL, D_MODEL])

    g["norm_post_ffn"] = g_norm_post_ffn
    g["w_ffn_down"] = _matmul("ffn_down_dw", f, dffn, "tn", F32)
    df = _matmul("ffn_down_dx", dffn, w["w_ffn_down"], "nt", F32)
    du_raw, dconv = _conv_bwd("ffn_act_bwd", u_raw, 0, [df], [(0, c0) for c0 in range(0, FFN_DIM, CONV_LANES)],
                              w["ffn_conv_w"], w["ffn_conv_b"], n_rows, True)
    g["ffn_conv_w"], g["ffn_conv_b"] = dconv[0:3], dconv[3:4]
    g["w_ffn_up"] = _matmul("ffn_up_dw", n2, du_raw, "tn", F32)
    dn2 = run("ffn_up_dx", _matmul, "ffn_up_dx", du_raw, w["w_ffn_up"], "nt", F32)

    def post_mix_bwd(r0, dn, d2, h, mx, w_pre, w_post):
        dx, dw_pre = _rms_bwd(dn, h, w_pre)
        dh1 = jnp.where(_valid_rows(r0, dn.shape[0], PAD), dx + d2, 0.0)
        dmix, dw_post = _rms_bwd(dh1, mx, w_post)
        return [dh1, dmix, dw_pre, dw_post]

    dh1, dmix, g["norm_pre_ffn"], g["norm_post_mix"] = _rowwise(
        "post_mix_bwd", post_mix_bwd, n_rows, tm,
        [(dn2, D_MODEL, 0), (dh2, D_MODEL, 0), (h1, D_MODEL, 0), (mix, D_MODEL, 0)],
        [w["norm_pre_ffn"], w["norm_post_mix"]], [(D_MODEL, F32), (D_MODEL, BF16)], [D_MODEL, D_MODEL])
    g["w_mix_out"] = _matmul("mix_out_dw", mixed, dmix, "tn", F32)
    dmixed = _matmul("mix_out_dx", dmix, w["w_mix_out"], "nt", F32)

    def mix_gate_bwd(r0, dm, ys, ya, gs, ga):
        ss, sa = _sigmoid(gs), _sigmoid(ga)
        dgate = jnp.concatenate([dm * ys * ss * (1.0 - ss), dm * ya * sa * (1.0 - sa)], axis=1)
        return [dm * ss, dm * sa, dgate]

    dys, dya, dp = _rowwise(
        "mix_gate_bwd", mix_gate_bwd, n_rows, tm,
        [(dmixed, D_MODEL, 0), (y_ssm, D_MODEL, 0), (y_attn, D_MODEL, 0)] + gate_ins,
        [], [(D_MODEL, BF16), (D_MODEL, BF16), (2 * D_MODEL, BF16, "new", P_W, P_GATE // (2 * D_MODEL))], [])
    g["w_ssm_out"] = _matmul("ssm_out_dw", yn, dys, "tn", F32)
    dyn = _matmul("ssm_out_dx", dys, w["w_ssm_out"], "nt", F32)
    g["w_attn_out"] = _matmul("attn_out_dw", ao, dya, "tn", F32)
    dao = _matmul("attn_out_dx", dya, w["w_attn_out"], "nt", BF16)

    def gate_norm_bwd(r0, dn, y, z, wn):
        sz, dsz = _silu_grad(z)
        dyz, dw = _rms_bwd(dn, y * sz, wn)
        live = _valid_rows(r0, dn.shape[0], PAD)
        return [jnp.where(live, dyz * sz, 0.0), jnp.where(live, dyz * y * dsz, 0.0), dw]

    dy_ssd, dp, g["ssm_norm"] = run(
        "ssm_gate_norm_bwd", _rowwise, "ssm_gate_norm_bwd", gate_norm_bwd, n_rows, tm,
        [(dyn, D_INNER, 0), (y_ssd, D_INNER, 0), (p, D_INNER, P_Z // D_INNER)],
        [w["ssm_norm"]], [(D_INNER, F32), (D_INNER, BF16, "into", dp, P_Z // D_INNER)], [D_INNER])
    dp, dsink = run("attn_bwd", _attn_bwd, p, sinks, ao, lse, dao, dp, n_chunks)
    g["attn_sinks"] = dsink[:, 0:ATTN_HEADS]
    dxs, dbm, dcm, dp, dpar = run("ssd_bwd", _ssd_bwd, p, w["ssm_conv_w"], w["ssm_conv_b"], dt_bias, a_log,
                                  d_skip, hin, dy_ssd, dp, n_chunks)
    g["ssm_dt_bias"], g["ssm_a_log"], g["ssm_d_skip"] = (dpar[i:i + 1, 0:SSM_HEADS] for i in range(3))
    x_chunks = [(src, c0) for src, arr in enumerate((dxs, dbm, dcm)) for c0 in range(0, arr.shape[1], CONV_LANES)]
    dp, dconv = run("ssm_conv_bwd", _conv_bwd, "ssm_conv_bwd", p, P_XBC // CONV_DIM, [dxs, dbm, dcm], x_chunks,
                    w["ssm_conv_w"], w["ssm_conv_b"], n_rows, False, into=dp, into_blk=P_XBC // CONV_DIM)
    g["ssm_conv_w"], g["ssm_conv_b"] = dconv[0:4], dconv[4:5]
    g["w_cat_t"] = _matmul("in_proj_dw", dp, n1, "tn", F32)
    dn1 = run("in_proj_dx", _matmul, "in_proj_dx", dp, w["w_cat"], "nt", F32)

    def pre_mix_bwd(r0, dn, d1, xt, hd, wn):
        dx, dw = _rms_bwd(dn, h0_tile(r0, xt, hd), wn)
        dh0 = jnp.where(_valid_rows(r0, dn.shape[0], PAD), dx + d1, 0.0)
        return [dh0, dh0, dw]

    dx_out, dhead, g["norm_pre_mix"] = _rowwise(
        "pre_mix_bwd", pre_mix_bwd, n_rows, T, [(dn1, D_MODEL, 0), (dh1, D_MODEL, 0)] + x_in,
        [w["norm_pre_mix"]], [(D_MODEL, F32, "prev", n_tok), (D_MODEL, F32, "first")], [D_MODEL])
    return jnp.sum(loss_cols), dx_out, dhead


_IN_SECTIONS = [((5152, 6176), P_Q), ((6176, 6432), P_K), ((6432, 6688), P_V), ((5120, 5152), P_DT),
                ((0, 2048), P_Z), ((6688, 8736), P_GATE), ((2048, 5120), P_XBC)]


IN_SHARD = N_IN // 4


def _shard_pieces(a, b):
    return [(j, max(a, j * IN_SHARD) - j * IN_SHARD, min(b, (j + 1) * IN_SHARD) - j * IN_SHARD)
            for j in range(4) if max(a, j * IN_SHARD) < min(b, (j + 1) * IN_SHARD)]


def _to_cat(w4):
    parts, at = [], 0
    for (a, b), off in _IN_SECTIONS:
        if off > at:
            parts.append(jnp.zeros((w4.shape[1], off - at), w4.dtype))
        parts += [w4[j, :, lo:hi] for j, lo, hi in _shard_pieces(a, b)]
        at = off + (b - a)
    return jnp.concatenate(parts, axis=1)


def _from_cat_t(g_cat_t):
    shards = [[] for _ in range(4)]
    for (a, b), off in sorted(_IN_SECTIONS):
        for j, lo, hi in _shard_pieces(a, b):
            start = off + j * IN_SHARD + lo - a
            shards[j].append(g_cat_t[start:start + hi - lo])
    return jnp.stack([jnp.concatenate(s, axis=0) for s in shards])


LANES = 1024
_BIG = [("w_in", 1024, 2184, "chip"), ("w_ssm_out", 512, 1024, "row"), ("w_attn_out", 256, 1024, "row"),
        ("w_mix_out", 256, 1024, "row"), ("w_ffn_up", 1024, 1408, "col"), ("w_ffn_down", 704, 1024, "row"),
        ("small", 32, LANES, "chip")]
_SMALL_SHARDED = [("ssm_conv_w", (4, 768), 1), ("ffn_conv_w", (3, 1408), 1), ("meta_tokens", (16, 256), 1)]
_REPLICATED = [("norm_pre_mix", 1024), ("ssm_conv_b", 3072), ("ssm_dt_bias", 32), ("ssm_a_log", 32),
               ("ssm_d_skip", 32), ("ssm_norm", 2048), ("attn_sinks", 16), ("norm_post_mix", 1024),
               ("norm_pre_ffn", 1024), ("ffn_conv_b", 5632), ("norm_post_ffn", 1024)]
SMALL_ROWS = 24


def _rep_rows():
    out, at = [], 0
    for _, width in _REPLICATED:
        out.append((at, -(-width // LANES)))
        at += out[-1][1]
    return out, at


def _in_rows(parts):
    rows = [jnp.pad(a, ((0, 0), (0, -a.shape[1] % LANES))).reshape(-1, LANES) for a in parts]
    flat = jnp.concatenate(rows, axis=0)
    return jnp.pad(flat, ((0, SMALL_ROWS - flat.shape[0]), (0, 0)))
WEIGHT_ORDER = ["meta_tokens", "norm_pre_mix", "w_in", "ssm_conv_w", "ssm_conv_b", "ssm_dt_bias", "ssm_a_log",
                "ssm_d_skip", "ssm_norm", "w_ssm_out", "attn_sinks", "w_attn_out", "w_mix_out", "norm_post_mix",
                "norm_pre_ffn", "w_ffn_up", "ffn_conv_w", "ffn_conv_b", "w_ffn_down", "norm_post_ffn"]


def _flatten(parts, rows):
    flat = jnp.concatenate([a.reshape(-1) for a in parts])
    return jnp.pad(flat, (0, rows * LANES - flat.shape[0])).reshape(rows, LANES)


def _unflatten(flat, shapes):
    flat = flat.reshape(-1)
    out, off = [], 0
    for shp in shapes:
        n = math.prod(shp)
        out.append(flat[off:off + n].reshape(shp))
        off += n
    return out


def _shard_of(full, chip, shape, axis):
    return lax.slice_in_dim(full, chip * shape[axis], (chip + 1) * shape[axis], axis=axis)


def _full_shape(r, c, layout):
    return {"row": (4 * r, c), "col": (r, 4 * c), "chip": (4, r, c), "chip_cols": (4, r, c)}[layout]


def _half_shape(r, c, layout):
    return (r, c // 2) if layout == "chip_cols" else (r // 2, c)


def _shard_view(ref, r, c, layout, chip):
    if layout == "row":
        return ref.at[pl.ds(pl.multiple_of(chip * r, 16), r), :]
    if layout == "col":
        return ref.at[:, pl.ds(pl.multiple_of(chip * c, 128), c)]
    return ref.at[chip]


def _half_view(ref, r, c, layout, chip, half):
    if layout == "chip_cols":
        return ref.at[chip, :, pl.ds(pl.multiple_of(half * (c // 2), 128), c // 2)]
    hr = r // 2
    if layout == "row":
        return ref.at[pl.ds(pl.multiple_of(chip * r + half * hr, 16), hr), :]
    r0 = pl.multiple_of(half * hr, 16)
    if layout == "col":
        return ref.at[pl.ds(r0, hr), pl.ds(pl.multiple_of(chip * c, 128), c)]
    return ref.at[chip, pl.ds(r0, hr), :]


def _mesh_pos():
    return lax.axis_index("x"), lax.axis_index("y"), lax.axis_index("c")


def _other_chips(x, y):
    return [(1 - x, y), (x, 1 - y), (1 - x, 1 - y)]


def _chip_index(x, y):
    return 2 * x + y


def _run_exchange(name, ex):
    n_in, n_out = len(ex.ins), len(ex.out_shapes)

    def body(*refs):
        in_refs, out_refs = refs[:n_in], refs[n_in:n_in + n_out]
        send_sems, recv_sems = refs[n_in + n_out:]
        copies = [pltpu.make_async_remote_copy(src_ref=s, dst_ref=d, send_sem=send_sems.at[i], recv_sem=recv_sems.at[i],
                                               device_id=dev, device_id_type=MESH)
                  for i, (s, d, dev) in enumerate(ex.make_copies(in_refs, out_refs))]
        assert len(copies) == ex.n_copies
        for cp in copies:
            cp.start()
        for cp in copies:
            cp.wait()

    return pl.pallas_call(
        body, name=name, in_specs=[ANY] * n_in, out_specs=[ANY] * n_out, out_shape=list(ex.out_shapes),
        scratch_shapes=[pltpu.SemaphoreType.DMA((ex.n_copies,)), pltpu.SemaphoreType.DMA((ex.n_copies,))],
        compiler_params=pltpu.CompilerParams(has_side_effects=True),
    )(*ex.ins)


def _join(*exs):
    def make(in_refs, out_refs):
        copies, i0, o0 = [], 0, 0
        for ex in exs:
            copies += ex.make_copies(in_refs[i0:i0 + len(ex.ins)], out_refs[o0:o0 + len(ex.out_shapes)])
            i0, o0 = i0 + len(ex.ins), o0 + len(ex.out_shapes)
        return copies

    aliases, i0, o0 = {}, 0, 0
    for ex in exs:
        aliases.update({i0 + k: o0 + v for k, v in ex.aliases.items()})
        i0, o0 = i0 + len(ex.ins), o0 + len(ex.out_shapes)
    return _Exchange([a for ex in exs for a in ex.ins], [s for ex in exs for s in ex.out_shapes], make,
                     sum(ex.n_copies for ex in exs), aliases)


def _split(exs, results):
    out, o0 = [], 0
    for ex in exs:
        out.append(list(results[o0:o0 + len(ex.out_shapes)]))
        o0 += len(ex.out_shapes)
    return out


def _gather_ici(entries, shards):
    def make(in_refs, out_refs):
        x, y, c = _mesh_pos()
        j = _chip_index(x, y)
        copies = []
        for ref_in, ref_out, (_, r, cc, lay) in zip(in_refs, out_refs, entries):
            copies.append((ref_in, _shard_view(ref_out, r, cc, lay, j), None))
            mine = ref_in.at[pl.ds(pl.multiple_of(c * (r // 2), 16), r // 2), :]
            copies += [(mine, _half_view(ref_out, r, cc, lay, j, c), (*ch, c)) for ch in _other_chips(x, y)]
        return copies

    shapes = [jax.ShapeDtypeStruct(_full_shape(r, cc, lay), s.dtype) for s, (_, r, cc, lay) in zip(shards, entries)]
    return _Exchange(list(shards), shapes, make, 4 * len(entries))


def _gather_pass_on(entries, fulls):
    def make(in_refs, out_refs):
        x, y, c = _mesh_pos()
        copies = []
        for ref, (_, r, cc, lay) in zip(out_refs, entries):
            for ch in _other_chips(x, y):
                landed = _half_view(ref, r, cc, lay, _chip_index(*ch), c)
                copies.append((landed, landed, (x, y, 1 - c)))
        return copies

    return _Exchange(list(fulls), [jax.ShapeDtypeStruct(f.shape, f.dtype) for f in fulls], make, 3 * len(entries),
                     {a: a for a in range(len(entries))})


def _gather_weights(entries, shards):
    n = len(entries)

    def body(*refs):
        ins, outs = refs[:n], refs[n:2 * n]
        send_sems, recv_sems, local_sems = refs[2 * n:]
        x, y, c = _mesh_pos()
        j = _chip_index(x, y)
        sibling = (x, y, 1 - c)
        chips = _other_chips(x, y)
        idx = [_chip_index(*ch) for ch in chips]

        def remote(k, src, dst, dev):
            return pltpu.make_async_remote_copy(src_ref=src, dst_ref=dst, send_sem=send_sems.at[k],
                                                recv_sem=recv_sems.at[k], device_id=dev, device_id_type=MESH)

        own = [pltpu.make_async_copy(ins[a], _shard_view(outs[a], r, cc, lay, j), local_sems.at[a])
               for a, (_, r, cc, lay) in enumerate(entries)]
        for cp in own:
            cp.start()
        first, passed = [], []
        for a, (_, r, cc, lay) in enumerate(entries):
            mine = ins[a].at[pl.ds(pl.multiple_of(c * (r // 2), 16), r // 2), :]
            for k, ch in enumerate(chips):
                first.append(remote(6 * a + k, mine, _half_view(outs[a], r, cc, lay, j, c), (*ch, c)))
                landed = _half_view(outs[a], r, cc, lay, idx[k], c)
                passed.append(remote(6 * a + 3 + k, landed, landed, sibling))
        for cp in first:
            cp.start()
        for a, (_, r, cc, lay) in enumerate(entries):
            for k in range(3):
                landed = _half_view(outs[a], r, cc, lay, idx[k], c)
                remote(6 * a + k, landed, landed, sibling).wait_recv()
                passed[3 * a + k].start()
        for a, (_, r, cc, lay) in enumerate(entries):
            for k in range(3):
                theirs = _half_view(outs[a], r, cc, lay, idx[k], 1 - c)
                remote(6 * a + 3 + k, theirs, theirs, sibling).wait_recv()
        for cp in first + passed:
            cp.wait_send()
        for cp in own:
            cp.wait()

    return pl.pallas_call(
        body, name="gather_weights", in_specs=[ANY] * n, out_specs=[ANY] * n,
        out_shape=[jax.ShapeDtypeStruct(_full_shape(r, cc, lay), s.dtype) for s, (_, r, cc, lay) in zip(shards, entries)],
        scratch_shapes=[pltpu.SemaphoreType.DMA((6 * n,)), pltpu.SemaphoreType.DMA((6 * n,)), pltpu.SemaphoreType.DMA((n,))],
        compiler_params=pltpu.CompilerParams(has_side_effects=True),
    )(*shards)


def _pair_exchange(entries, grads):
    def make(in_refs, out_refs):
        x, y, c = _mesh_pos()
        return [(_half_view(ref_in, r, cc, lay, i, 1 - c), ref_out.at[i], (x, y, 1 - c))
                for ref_in, ref_out, (_, r, cc, lay) in zip(in_refs, out_refs, entries) for i in range(4)]

    return _Exchange(list(grads), [jax.ShapeDtypeStruct((4,) + _half_shape(r, cc, lay), F32) for _, r, cc, lay in entries],
                     make, 4 * len(entries))


def _whole_to_sibling(arrays):
    def make(in_refs, out_refs):
        x, y, c = _mesh_pos()
        return [(r, o, (x, y, 1 - c)) for r, o in zip(in_refs, out_refs)]

    return _Exchange(list(arrays), [jax.ShapeDtypeStruct(a.shape, a.dtype) for a in arrays], make, len(arrays))


def _chip_exchange(psends):
    def make(in_refs, out_refs):
        x, y, c = _mesh_pos()
        return [(ref_in.at[_chip_index(*ch)], ref_out.at[k], (*ch, c))
                for ref_in, ref_out in zip(in_refs, out_refs) for k, ch in enumerate(_other_chips(x, y))]

    return _Exchange(list(psends), [jax.ShapeDtypeStruct((3,) + p.shape[1:], p.dtype) for p in psends], make,
                     3 * len(psends))


def _to_all_chips(array):
    def make(in_refs, out_refs):
        x, y, c = _mesh_pos()
        return [(in_refs[0], out_refs[0].at[k], (*ch, c)) for k, ch in enumerate(_other_chips(x, y))]

    return _Exchange([array], [jax.ShapeDtypeStruct((3,) + array.shape, array.dtype)], make, 3)


SUM_ROWS = 512
ADAM_ROWS = 256


def _pair_sum(name, grad, recv, ids, r, c, layout):
    hr, c = _half_shape(r, c, layout)
    tr = _row_tile(hr, SUM_ROWS)
    nb = hr // tr

    def body(ids_ref, g_ref, r_ref, send_ref, own_ref):
        s = g_ref[...] + r_ref[...]
        send_ref[...] = s.astype(send_ref.dtype)

        @pl.when(pl.program_id(1) == ids_ref[1])
        def _():
            own_ref[...] = s

    if layout == "row":
        g_spec = pl.BlockSpec((tr, c), lambda t, j, ids_ref: ((j * r + ids_ref[0] * hr) // tr + t, 0))
    elif layout == "col":
        g_spec = pl.BlockSpec((tr, c), lambda t, j, ids_ref: (ids_ref[0] * nb + t, j))
    elif layout == "chip_cols":
        g_spec = pl.BlockSpec((None, tr, c), lambda t, j, ids_ref: (j, t, ids_ref[0]))
    else:
        g_spec = pl.BlockSpec((None, tr, c), lambda t, j, ids_ref: (j, ids_ref[0] * nb + t, 0))
    grid_spec = pltpu.PrefetchScalarGridSpec(
        num_scalar_prefetch=1, grid=(nb, 4),
        in_specs=[g_spec, pl.BlockSpec((None, tr, c), lambda t, j, ids_ref: (j, t, 0))],
        out_specs=[pl.BlockSpec((None, tr, c), lambda t, j, ids_ref: (j, t, 0)),
                   pl.BlockSpec((tr, c), lambda t, j, ids_ref: (t, 0))])
    return pl.pallas_call(
        body, name=name, grid_spec=grid_spec,
        out_shape=[jax.ShapeDtypeStruct((4, hr, c), BF16), jax.ShapeDtypeStruct((hr, c), F32)],
        compiler_params=_cparams(2),
    )(ids, grad, recv)


def _chip_sum(name, own, recv):
    hr, c = own.shape
    tr = _row_tile(hr, SUM_ROWS)

    def body(o_ref, r_ref, out_ref):
        out_ref[...] = ((o_ref[...] + r_ref[0].astype(F32)) + r_ref[1].astype(F32)) + r_ref[2].astype(F32)

    return pl.pallas_call(
        body, name=name, grid=(hr // tr,),
        in_specs=[pl.BlockSpec((tr, c), lambda i: (i, 0)), pl.BlockSpec((3, tr, c), lambda i: (0, i, 0))],
        out_specs=pl.BlockSpec((tr, c), lambda i: (i, 0)),
        out_shape=jax.ShapeDtypeStruct((hr, c), F32), compiler_params=_cparams(1),
    )(own, recv)


def _chip_sum_small(own, recv, ids):
    def body(ids_ref, o_ref, r_ref, out_ref):
        j = ids_ref[1]
        total = None
        for i in range(4):
            m = jnp.bitwise_xor(i, j)
            term = jnp.where(m == 0, o_ref[...], jnp.where(m == 2, r_ref[0], jnp.where(m == 1, r_ref[1], r_ref[2])))
            total = term if total is None else total + term
        out_ref[...] = total

    grid_spec = pltpu.PrefetchScalarGridSpec(
        num_scalar_prefetch=1, grid=(1,),
        in_specs=[pl.BlockSpec(own.shape, lambda i, ids_ref: (0, 0)), pl.BlockSpec(recv.shape, lambda i, ids_ref: (0, 0, 0))],
        out_specs=pl.BlockSpec(own.shape, lambda i, ids_ref: (0, 0)))
    return pl.pallas_call(body, name="chip_sum_small", grid_spec=grid_spec,
                          out_shape=jax.ShapeDtypeStruct(own.shape, F32), compiler_params=_cparams(1))(ids, own, recv)


def _adamw(name, w, m, v, mine, theirs, ids):
    lead = (None,) * (w.ndim - 2)
    rows, cols = w.shape[-2:]
    half = rows // 2
    tr = _row_tile(half, ADAM_ROWS, unit=8)
    nb = half // tr
    c1 = 1.0 / (1.0 - ADAM_B1 ** ADAM_STEP)
    c2 = 1.0 / (1.0 - ADAM_B2 ** ADAM_STEP)

    def body(ids_ref, w_ref, m_ref, v_ref, mine_ref, theirs_ref, g_out, d_out, m_out, v_out):
        g = jnp.where(pl.program_id(0) == ids_ref[0], mine_ref[...], theirs_ref[...])
        m_new = ADAM_B1 * m_ref[...] + (1.0 - ADAM_B1) * g
        v_new = ADAM_B2 * v_ref[...] + (1.0 - ADAM_B2) * (g * g)
        d_out[...] = -ADAM_LR * ((m_new * c1) / (jnp.sqrt(v_new * c2) + ADAM_EPS) + ADAM_WD * w_ref[...])
        g_out[...] = g
        m_out[...] = m_new
        v_out[...] = v_new

    full = pl.BlockSpec(lead + (tr, cols), lambda h, i, ids_ref: (0,) * len(lead) + (h * nb + i, 0))
    part = pl.BlockSpec((tr, cols), lambda h, i, ids_ref: (i, 0))
    grid_spec = pltpu.PrefetchScalarGridSpec(num_scalar_prefetch=1, grid=(2, nb),
                                             in_specs=[full, full, full, part, part], out_specs=[full] * 4)
    return pl.pallas_call(
        body, name=name, grid_spec=grid_spec,
        out_shape=[jax.ShapeDtypeStruct(w.shape, F32)] * 4, compiler_params=_cparams(2),
    )(ids, w, m, v, mine, theirs)


def _adamw_whole(name, w, m, v, g):
    rows, cols = w.shape[-2:]
    tr = _row_tile(rows, 2 * ADAM_ROWS, unit=8)
    c1 = 1.0 / (1.0 - ADAM_B1 ** ADAM_STEP)
    c2 = 1.0 / (1.0 - ADAM_B2 ** ADAM_STEP)

    def body(w_ref, m_ref, v_ref, g_ref, g_out, d_out, m_out, v_out):
        g = g_ref[...]
        m_new = ADAM_B1 * m_ref[...] + (1.0 - ADAM_B1) * g
        v_new = ADAM_B2 * v_ref[...] + (1.0 - ADAM_B2) * (g * g)
        d_out[...] = -ADAM_LR * ((m_new * c1) / (jnp.sqrt(v_new * c2) + ADAM_EPS) + ADAM_WD * w_ref[...])
        g_out[...] = g
        m_out[...] = m_new
        v_out[...] = v_new

    full = pl.BlockSpec((None, tr, cols), lambda i: (0, i, 0))
    return pl.pallas_call(
        body, name=name, grid=(rows // tr,), in_specs=[full, full, full, pl.BlockSpec((tr, cols), lambda i: (i, 0))],
        out_specs=[full] * 4, out_shape=[jax.ShapeDtypeStruct(w.shape, F32)] * 4, compiler_params=_cparams(1),
    )(w, m, v, g)


def _adamw_replicated(g_rows, ws, ms, vs):
    n = len(ws)
    layout, _ = _rep_rows()
    c1 = 1.0 / (1.0 - ADAM_B1 ** ADAM_STEP)
    c2 = 1.0 / (1.0 - ADAM_B2 ** ADAM_STEP)

    def body(g_ref, *refs):
        w_refs, m_refs, v_refs = refs[0:n], refs[n:2 * n], refs[2 * n:3 * n]
        outs = refs[3 * n:]
        for k, (r0, rows) in enumerate(layout):
            width = w_refs[k].shape[1]
            g = jnp.concatenate([g_ref[r0 + j:r0 + j + 1, :] for j in range(rows)], axis=1)[:, 0:width]
            m_new = ADAM_B1 * m_refs[k][...] + (1.0 - ADAM_B1) * g
            v_new = ADAM_B2 * v_refs[k][...] + (1.0 - ADAM_B2) * (g * g)
            outs[k][...] = g
            outs[n + k][...] = -ADAM_LR * ((m_new * c1) / (jnp.sqrt(v_new * c2) + ADAM_EPS) + ADAM_WD * w_refs[k][...])
            outs[2 * n + k][...] = m_new
            outs[3 * n + k][...] = v_new

    res = pl.pallas_call(body, name="adamw_replicated",
                         out_shape=[jax.ShapeDtypeStruct(w.shape, F32) for _ in range(4) for w in ws])(g_rows, *ws, *ms, *vs)
    return [res[k * n:(k + 1) * n] for k in range(4)]


def _small_shard(parts):
    return _flatten(parts, _BIG[-1][1])


_ENTRY = {e[0]: e for e in _BIG}
_GRAD_ENTRY = {**_ENTRY, "w_in": ("w_in", IN_SHARD, D_MODEL, "chip_cols")}
FFN_MATS = ("w_ffn_down", "w_ffn_up")
MIXER_MATS = ("w_mix_out", "w_ssm_out", "w_attn_out")


class _StepPlan:
    def __init__(self, w, late_shards, shards, ids):
        self.w, self.g = w, {}
        self.late_shards, self.shards, self.ids = late_shards, shards, ids
        self.sums, self.halves, self.results = {}, {}, {}

    def run(self, name, fn, *args, **kw):
        at = getattr(self, "_at_" + name, None)
        if at is None:
            return fn(*args, **kw)
        exchange, landed = at()
        res, extra = fn(*args, bg=exchange, **kw)
        landed(extra)
        return res

    def _at_ssd_fwd(self):
        def landed(fulls):
            self.partly_gathered = fulls

        return _gather_ici([_ENTRY[n] for n in MIXER_MATS], [self.late_shards[n] for n in MIXER_MATS]), landed

    def _at_attn_fwd(self):
        stages = (_gather_pass_on([_ENTRY[n] for n in MIXER_MATS], self.partly_gathered),
                  _gather_ici([_ENTRY[n] for n in FFN_MATS], [self.late_shards[n] for n in FFN_MATS]))

        def landed(extra):
            mixer, self.partly_gathered = _split(stages, extra)
            self.w.update(zip(MIXER_MATS, mixer))

        return _join(*stages), landed

    def _at_ssm_gate_norm(self):
        return (_gather_pass_on([_ENTRY[n] for n in FFN_MATS], self.partly_gathered),
                lambda fulls: self.w.update(zip(FFN_MATS, fulls)))

    def pair_sums(self, names, grads, recv):
        for n, gr, rv in zip(names, grads, recv):
            _, r, c, lay = _GRAD_ENTRY[n]
            self.sums[n] = _pair_sum("pair_sum_" + n, gr, rv, self.ids, r, c, lay)

    def chip_sums(self, names, recv):
        for n, rv in zip(names, recv):
            self.halves[n] = _chip_sum("chip_sum_" + n, self.sums[n][1], rv)

    def adamw(self, names, theirs):
        for n, th in zip(names, theirs):
            sh = self.shards[n]
            if n == "w_in":
                mine_first = self.ids[0] == 0
                g_t = jnp.where(mine_first, jnp.concatenate([self.halves[n], th], axis=1),
                                jnp.concatenate([th, self.halves[n]], axis=1))
                res = _adamw_whole("adamw_" + n, *[jnp.swapaxes(sh[k], -1, -2) for k in ("w", "m", "v")], g_t)
                self.results[n] = [jnp.swapaxes(r, -1, -2) for r in res]
            else:
                self.results[n] = _adamw("adamw_" + n, sh["w"], sh["m"], sh["v"], self.halves[n], th, self.ids)

    def _pair_stage(self, names, grads):
        return (_pair_exchange([_GRAD_ENTRY[n] for n in names], grads),
                lambda recv: self.pair_sums(names, grads, recv))

    def _at_ffn_up_dx(self):
        return self._pair_stage(FFN_MATS, [self.g[n] for n in FFN_MATS])

    def _at_ssm_gate_norm_bwd(self):
        return self._pair_stage(MIXER_MATS, [self.g[n] for n in MIXER_MATS])

    def _at_attn_bwd(self):
        return _chip_exchange([self.sums[n][0] for n in FFN_MATS]), lambda recv: self.chip_sums(FFN_MATS, recv)

    def _at_ssd_bwd(self):
        stages = (_chip_exchange([self.sums[n][0] for n in MIXER_MATS]),
                  _whole_to_sibling([self.halves[n] for n in FFN_MATS]))

        def landed(extra):
            recv, theirs = _split(stages, extra)
            self.chip_sums(MIXER_MATS, recv)
            self.adamw(FFN_MATS, theirs)

        return _join(*stages), landed

    def _at_ssm_conv_bwd(self):
        return _whole_to_sibling([self.halves[n] for n in MIXER_MATS]), lambda theirs: self.adamw(MIXER_MATS, theirs)

    def _at_in_proj_dx(self):
        grads = [_from_cat_t(self.g.pop("w_cat_t"))]
        self.pair_sums(("w_in",), grads,
                       _run_exchange("grad_pair_exchange_w_in", _pair_exchange([_GRAD_ENTRY["w_in"]], grads)))
        return _chip_exchange([self.sums["w_in"][0]]), lambda recv: self.chip_sums(("w_in",), recv)

    def finish(self, g_small, g_rep, rep_shards):
        stages = (_pair_exchange([_ENTRY["small"]], [g_small]), _whole_to_sibling([g_rep]))
        recv_small, recv_rep = _split(stages, _run_exchange("grad_pair_exchange_tail", _join(*stages)))
        self.pair_sums(("small",), [g_small], recv_small)
        p_rep, = _rowwise("pair_sum_replicated", lambda r0, a, b: [a + b], SMALL_ROWS, SMALL_ROWS,
                          [(g_rep, LANES, 0), (recv_rep[0], LANES, 0)], [], [(LANES, F32)], [])
        stages = (_chip_exchange([self.sums["small"][0]]), _to_all_chips(p_rep))
        recv, recv_rep = _split(stages, _run_exchange("grad_chip_exchange_tail", _join(*stages)))
        self.chip_sums(("small",), recv)
        g_rep_tot = _chip_sum_small(p_rep, recv_rep[0], self.ids)
        last = ("w_in", "small")
        self.adamw(last, _run_exchange("grad_half_share_tail", _whole_to_sibling([self.halves[n] for n in last])))
        self.results["replicated"] = _adamw_replicated(g_rep_tot, rep_shards["w"], rep_shards["m"], rep_shards["v"])
        return g_rep_tot[_rep_rows()[1], 0]


def kernel(x, meta_tokens, norm_pre_mix, w_in, ssm_conv_w, ssm_conv_b, ssm_dt_bias, ssm_a_log, ssm_d_skip, ssm_norm, w_ssm_out, attn_sinks, w_attn_out, w_mix_out, norm_post_mix, norm_pre_ffn, w_ffn_up, ffn_conv_w, ffn_conv_b, w_ffn_down, norm_post_ffn, loss_target, m_meta_tokens, m_norm_pre_mix, m_w_in, m_ssm_conv_w, m_ssm_conv_b, m_ssm_dt_bias, m_ssm_a_log, m_ssm_d_skip, m_ssm_norm, m_w_ssm_out, m_attn_sinks, m_w_attn_out, m_w_mix_out, m_norm_post_mix, m_norm_pre_ffn, m_w_ffn_up, m_ffn_conv_w, m_ffn_conv_b, m_w_ffn_down, m_norm_post_ffn, v_meta_tokens, v_norm_pre_mix, v_w_in, v_ssm_conv_w, v_ssm_conv_b, v_ssm_dt_bias, v_ssm_a_log, v_ssm_d_skip, v_ssm_norm, v_w_ssm_out, v_attn_sinks, v_w_attn_out, v_w_mix_out, v_norm_post_mix, v_norm_pre_ffn, v_w_ffn_up, v_ffn_conv_w, v_ffn_conv_b, v_w_ffn_down, v_norm_post_ffn):
    args = dict(locals())
    squeeze = lambda a: a.reshape(a.shape[-2:])
    wts = {n: squeeze(args[n]) for n in WEIGHT_ORDER}
    mom = {n: squeeze(args["m_" + n]) for n in WEIGHT_ORDER}
    var = {n: squeeze(args["v_" + n]) for n in WEIGHT_ORDER}
    x_i, y_i, c_i = _mesh_pos()
    ids = jnp.stack([c_i, _chip_index(x_i, y_i)]).astype(jnp.int32)
    big_names = [n for n, _, _, _ in _BIG[:-1]]
    small_names = [n for n, _, _ in _SMALL_SHARDED]
    rep_names = [n for n, _ in _REPLICATED]

    stacks = {"w": wts, "m": mom, "v": var}
    shards = {n: {"w": args[n], "m": args["m_" + n], "v": args["v_" + n]} for n in big_names}
    shards["small"] = {k: _small_shard([d[n] for n in small_names]) for k, d in stacks.items()}
    rep_shards = {k: [d[n] for n in rep_names] for k, d in stacks.items()}

    w_in4, small_all = _gather_weights([_ENTRY["w_in"], _ENTRY["small"]], [wts["w_in"].astype(BF16), shards["small"]["w"]])
    w = {n: wts[n] for n in rep_names}
    w["w_cat"] = _to_cat(w_in4)
    small_parts = [_unflatten(small_all[i], [shp for _, shp, _ in _SMALL_SHARDED]) for i in range(4)]
    for k, (n, _, axis) in enumerate(_SMALL_SHARDED):
        w[n] = jnp.concatenate([small_parts[i][k] for i in range(4)], axis=axis)
    plan = _StepPlan(w, {n: wts[n].astype(BF16) for n in MIXER_MATS + FFN_MATS}, shards, ids)

    head = jnp.concatenate([jnp.zeros((PAD, D_MODEL), F32), w["meta_tokens"]], axis=0)
    loss_sum, dx, dhead = _local_step(x[0], head, loss_target[0], plan)
    g = plan.g
    g["meta_tokens"] = dhead[PAD:]
    g_small = jnp.stack([_small_shard([_shard_of(g[n], i, shp, ax) for n, shp, ax in _SMALL_SHARDED]) for i in range(4)])
    loss_part = (loss_sum * (0.5 / D_MODEL)).reshape(1, 1)
    loss = plan.finish(g_small, _in_rows([g[n] for n in rep_names] + [loss_part]), rep_shards)

    results = {}
    for kind in range(4):
        results.update({(kind, n): plan.results[n][kind] for n in big_names})
        parts = _unflatten(plan.results["small"][kind], [shp for _, shp, _ in _SMALL_SHARDED])
        results.update({(kind, n): parts[k] for k, n in enumerate(small_names)})
        results.update({(kind, n): plan.results["replicated"][kind][k] for k, n in enumerate(rep_names)})
    outs = [results[kind, n].reshape(args[n].shape) for kind in range(4) for n in WEIGHT_ORDER]
    return (loss, dx[None], *outs)
```

```python
import math
from typing import Any, Callable, NamedTuple, Sequence

import jax
import jax.numpy as jnp
from jax import lax
from jax.experimental import pallas as pl
from jax.experimental.pallas import tpu as pltpu

F32 = jnp.float32
BF16 = jnp.bfloat16

D_MODEL = 1024
N_META = 16
T = 128
PAD = T - N_META
D_INNER = 2048
SSM_HEADS = 32
HEAD_P = 64
SSM_GROUPS = 4
GROUP_W = D_INNER // SSM_GROUPS
D_STATE = 128
CONV_DIM = D_INNER + 2 * SSM_GROUPS * D_STATE
ATTN_HEADS = 16
KV_HEADS = 4
ATTN_W = 1024
KV_W = 256
FFN_DIM = 2816
N_IN = 8736
EPS = 1e-6
NEG = -1e30
SCALE = 0.125

P_Q, P_K, P_V, P_DT, P_Z, P_GATE, P_XBC = 0, 1024, 1280, 1536, 2048, 4096, 6144
QKV_W = 1536
P_W = 9216

ADAM_LR, ADAM_B1, ADAM_B2, ADAM_EPS, ADAM_WD, ADAM_STEP = 0.001, 0.9, 0.999, 1e-08, 0.01, 10

VMEM_BUDGET = 40 * 1024 * 1024
VMEM_LIMIT = 56 * 1024 * 1024
MESH = pl.DeviceIdType.MESH
ANY = pl.BlockSpec(memory_space=pl.ANY)


def _cparams(n_axes, **kw):
    return pltpu.CompilerParams(dimension_semantics=("arbitrary",) * n_axes, vmem_limit_bytes=VMEM_LIMIT, **kw)


class _Exchange(NamedTuple):
    ins: Sequence[Any]
    out_shapes: Sequence[Any]
    make_copies: Callable
    n_copies: int
    aliases: dict = {}


def _call(body, name, grid, in_specs, out_specs, out_shape, operands, scratch_shapes=(), aliases=None, bg=None):
    aliases = dict(aliases or {})
    if bg is None:
        return pl.pallas_call(body, name=name, grid=grid, in_specs=in_specs, out_specs=out_specs, out_shape=out_shape,
                              scratch_shapes=list(scratch_shapes), input_output_aliases=aliases,
                              compiler_params=_cparams(len(grid)))(*operands)
    n_in, n_out, n_scr = len(in_specs), len(out_specs), len(scratch_shapes)
    nb_in, nb_out = len(bg.ins), len(bg.out_shapes)

    def hosted(*refs):
        ins, bg_ins = refs[:n_in], refs[n_in:n_in + nb_in]
        outs = refs[n_in + nb_in:n_in + nb_in + n_out]
        bg_outs = refs[n_in + nb_in + n_out:n_in + nb_in + n_out + nb_out]
        scratch = refs[n_in + nb_in + n_out + nb_out:n_in + nb_in + n_out + nb_out + n_scr]
        send_sems, recv_sems = refs[-2:]
        pids = [pl.program_id(a) for a in range(len(grid))]
        first, last = pids[0] == 0, pids[0] == grid[0] - 1
        for p, g in zip(pids[1:], grid[1:]):
            first, last = first & (p == 0), last & (p == g - 1)
        copies = []
        for k, (src, dst, peer) in enumerate(bg.make_copies(bg_ins, bg_outs)):
            if peer is None:
                copies.append(pltpu.make_async_copy(src, dst, send_sems.at[k]))
            else:
                copies.append(pltpu.make_async_remote_copy(src_ref=src, dst_ref=dst, send_sem=send_sems.at[k],
                                                           recv_sem=recv_sems.at[k], device_id=peer, device_id_type=MESH))
        assert len(copies) == bg.n_copies

        @pl.when(first)
        def _():
            for cp in copies:
                cp.start()

        body(*ins, *outs, *scratch)

        @pl.when(last)
        def _():
            for cp in copies:
                cp.wait()

    aliases = {(k if k < n_in else k + nb_in): v for k, v in aliases.items()}
    aliases.update({n_in + k: n_out + v for k, v in bg.aliases.items()})
    res = pl.pallas_call(
        hosted, name=name, grid=grid, in_specs=list(in_specs) + [ANY] * nb_in, out_specs=list(out_specs) + [ANY] * nb_out,
        out_shape=list(out_shape) + list(bg.out_shapes), input_output_aliases=aliases,
        scratch_shapes=list(scratch_shapes) + [pltpu.SemaphoreType.DMA((bg.n_copies,))] * 2,
        compiler_params=_cparams(len(grid), has_side_effects=True))(*operands, *bg.ins)
    return res[:n_out], res[n_out:]


def _sigmoid(x):
    return 1.0 / (1.0 + jnp.exp(-x))


def _silu(x):
    return x * _sigmoid(x)


def _silu_grad(x):
    s = _sigmoid(x)
    return x * s, s * (1.0 + x * (1.0 - s))


def _dsilu(x):
    return _silu_grad(x)[1]


def _softplus(x):
    e = jnp.exp(-jnp.abs(x))
    small = e * (1.0 - e * (0.5 - e * (1.0 / 3.0)))
    return jnp.maximum(x, 0.0) + jnp.where(e < 0.01, small, jnp.log(1.0 + e))


def _rms(x, w):
    r = lax.rsqrt(jnp.mean(x * x, axis=-1, keepdims=True) + EPS)
    return x * r * w


def _rms_bwd(dy, x, w):
    r = lax.rsqrt(jnp.mean(x * x, axis=-1, keepdims=True) + EPS)
    xh = x * r
    g = dy * w
    dx = r * (g - xh * jnp.mean(g * xh, axis=-1, keepdims=True))
    dw = jnp.sum(dy * xh, axis=0, keepdims=True)
    return dx, dw


def _dot(a, b):
    return jnp.dot(a, b, preferred_element_type=F32)


def _dot_nt(a, b):
    return lax.dot_general(a, b, (((1,), (1,)), ((), ())), preferred_element_type=F32)


def _dot_tn(a, b):
    return lax.dot_general(a, b, (((0,), (0,)), ((), ())), preferred_element_type=F32)


def _split3(x):
    hi = x.astype(BF16)
    r = x - hi.astype(F32)
    mid = r.astype(BF16)
    lo = (r - mid.astype(F32)).astype(BF16)
    return hi, mid, lo


def _xdot(x, e):
    hi, mid, lo = _split3(x)
    return _dot(hi, e) + _dot(mid, e) + _dot(lo, e)


def _xdot_l(e, x):
    hi, mid, lo = _split3(x)
    return _dot(e, hi) + _dot(e, mid) + _dot(e, lo)


def _iota(shape, dim):
    return lax.broadcasted_iota(jnp.int32, shape, dim)


def _divisors(n, unit):
    return [t for t in range(unit, n + 1, unit) if n % t == 0]


MIN_MATMUL_STEPS = 8
SMALL_MATMUL = 2 ** 33


def _matmul_tiles(m, n, k, a_bytes, b_bytes, o_bytes, m_unit):
    best = None
    for tm in _divisors(m, m_unit):
        for tn in _divisors(n, 128):
            for tk in _divisors(k, 128):
                acc = 0 if tk == k else tm * tn * 4
                vm = 2 * (tm * tk * a_bytes + tk * tn * b_bytes + tm * tn * o_bytes) + acc
                if vm > VMEM_BUDGET:
                    continue
                steps = (m // tm) * (n // tn) * (k // tk)
                want = MIN_MATMUL_STEPS if m * n * k >= SMALL_MATMUL else 2
                score = (tk == k, min(steps, want), min(tm, 256), tm * tn * tk)
                if best is None or score > best[0]:
                    best = (score, (tm, tn, tk))
    return best[1]


def _matmul(name, a, b, mode, out_dtype, bg=None):
    if mode == "nn":
        (m, k), n = a.shape, b.shape[1]
    elif mode == "nt":
        (m, k), n = a.shape, b.shape[0]
    else:
        (k, m), n = a.shape, b.shape[1]
    ab, bb, ob = a.dtype.itemsize, b.dtype.itemsize, jnp.dtype(out_dtype).itemsize
    tm, tn, tk = _matmul_tiles(m, n, k, ab, bb, ob, 128 if mode == "tn" else 16)
    nk = k // tk
    dot = {"nn": _dot, "nt": _dot_nt, "tn": _dot_tn}[mode]

    def body(a_ref, b_ref, o_ref, *scratch):
        prod = dot(a_ref[...].astype(BF16), b_ref[...].astype(BF16))
        if nk == 1:
            o_ref[...] = prod.astype(o_ref.dtype)
        else:
            acc_ref, = scratch
            kk = pl.program_id(2)

            @pl.when(kk == 0)
            def _():
                acc_ref[...] = prod

            @pl.when(kk > 0)
            def _():
                acc_ref[...] += prod

            @pl.when(kk == nk - 1)
            def _():
                o_ref[...] = acc_ref[...].astype(o_ref.dtype)

    a_spec = pl.BlockSpec((tk, tm), lambda i, j, kk: (kk, i)) if mode == "tn" else pl.BlockSpec((tm, tk), lambda i, j, kk: (i, kk))
    b_spec = pl.BlockSpec((tn, tk), lambda i, j, kk: (j, kk)) if mode == "nt" else pl.BlockSpec((tk, tn), lambda i, j, kk: (kk, j))
    res = _call(body, name, (m // tm, n // tn, nk), [a_spec, b_spec], [pl.BlockSpec((tm, tn), lambda i, j, kk: (i, j))],
                [jax.ShapeDtypeStruct((m, n), out_dtype)], [a, b],
                scratch_shapes=[] if nk == 1 else [pltpu.VMEM((tm, tn), F32)], bg=bg)
    return res[0] if bg is None else (res[0][0], res[1])


def _row_tile(n_rows, cap, unit=16):
    return max([t for t in _divisors(n_rows, unit) if t <= cap], default=n_rows)


ROW_SUB = 384
GROUP_UNROLL = 4


def _rowwise(name, fn, n_rows, tm, row_ins, full_ins, row_outs, acc_outs, bg=None):
    n_in = len(row_ins) + len(full_ins)
    n_ro = len(row_outs)
    into = [(k, o[3]) for k, o in enumerate(row_outs) if len(o) > 2 and o[2] == "into"]

    sub = min(tm, ROW_SUB)
    counts = [tm // T if len(e) > 3 and e[3] == "prev" else 1 for e in row_ins]
    assert all(cnt == 1 for cnt in counts) or sub == tm
    starts = [sum(counts[:k]) for k in range(len(counts))]
    n_row_in = sum(counts)
    n_in = n_row_in + len(full_ins)

    def body(*refs):
        i = pl.program_id(0)
        outs = refs[n_in + len(into):]

        sums = tuple(jnp.zeros((1, w), F32) for w in acc_outs)
        for s in range(tm // sub):
            rows = pl.ds(s * sub, sub)
            vals = [refs[st][rows, :] if cnt == 1 else jnp.concatenate([refs[st + k][...] for k in range(cnt)], axis=0)
                    for st, cnt in zip(starts, counts)]
            vals += [r[...] for r in refs[n_row_in:n_in]]
            res = fn(i * tm + s * sub, *vals)
            for o, r, v in zip(row_outs, outs[:n_ro], res[:n_ro]):
                if len(o) > 2 and o[2] == "first":
                    @pl.when(i == 0)
                    def _(r=r, v=v, rows=rows):
                        r[rows, :] = v.astype(r.dtype)
                else:
                    r[rows, :] = v.astype(r.dtype)
            sums = tuple(a + v for a, v in zip(sums, res[n_ro:]))

        @pl.when(i == 0)
        def _():
            for r, v in zip(outs[n_ro:], sums):
                r[...] = v

        @pl.when(i > 0)
        def _():
            for r, v in zip(outs[n_ro:], sums):
                r[...] += v

    def in_spec(entry, cnt):
        w, cb = entry[1], entry[2]
        if len(entry) > 3 and entry[3] == "prev":
            return [pl.BlockSpec((tm // cnt, w), lambda i, k=k: (jnp.maximum(cnt * i - 1 + k, 0), cb)) for k in range(cnt)]
        if len(entry) > 3 and entry[3] == "first":
            return [pl.BlockSpec((tm, w), lambda i: (0, cb))]
        return [pl.BlockSpec((tm, w), lambda i: (i, cb))]

    def out_spec(o):
        if len(o) == 2:
            return pl.BlockSpec((tm, o[0]), lambda i: (i, 0)), jax.ShapeDtypeStruct((n_rows, o[0]), o[1])
        if o[2] == "new":
            return pl.BlockSpec((tm, o[0]), lambda i: (i, o[4])), jax.ShapeDtypeStruct((n_rows, o[3]), o[1])
        if o[2] == "into":
            return pl.BlockSpec((tm, o[0]), lambda i: (i, o[4])), jax.ShapeDtypeStruct(o[3].shape, o[3].dtype)
        if o[2] == "first":
            return pl.BlockSpec((tm, o[0]), lambda i: (0, 0)), jax.ShapeDtypeStruct((tm, o[0]), o[1])
        return pl.BlockSpec((tm, o[0]), lambda i: (jnp.maximum(i - 1, 0), 0)), jax.ShapeDtypeStruct((o[3], o[0]), o[1])

    in_specs = [s for e, cnt in zip(row_ins, counts) for s in in_spec(e, cnt)]
    in_specs += [pl.BlockSpec(a.shape, lambda i: (0, 0)) for a in full_ins]
    in_specs += [pl.BlockSpec(memory_space=pl.ANY) for _ in into]
    specs_shapes = [out_spec(o) for o in row_outs]
    out_specs = [s for s, _ in specs_shapes] + [pl.BlockSpec((1, w), lambda i: (0, 0)) for w in acc_outs]
    out_shape = [s for _, s in specs_shapes] + [jax.ShapeDtypeStruct((1, w), F32) for w in acc_outs]
    return _call(body, name, (n_rows // tm,), in_specs, out_specs, out_shape,
                 [e[0] for e, cnt in zip(row_ins, counts) for _ in range(cnt)] + list(full_ins) + [arr for _, arr in into],
                 aliases={n_in + a: k for a, (k, _) in enumerate(into)}, bg=bg)


def _valid_rows(first_row, tm, lo):
    return (first_row + _iota((tm, 1), 0)) >= lo


CONV_ROWS = 192
CONV_SUB = 16
CONV_LANES = 256


def _conv_specs(tm, width, blk, n_rows, after):
    specs = [pl.BlockSpec((tm, width), lambda i: (i, blk)),
             pl.BlockSpec((8, width), lambda i: (jnp.maximum(i * (tm // 8) - 1, 0), blk))]
    if after:
        specs.append(pl.BlockSpec((16, width), lambda i: (jnp.minimum((i + 1) * (tm // 16), n_rows // 16 - 1), blk)))
    return specs


def _conv_window(win, w_ref, b_ref, taps, c0, cw, n):
    acc = b_ref[:, c0:c0 + cw] + w_ref[taps - 1:taps, c0:c0 + cw] * win[8:8 + n]
    for k in range(taps - 1):
        acc = acc + w_ref[k:k + 1, c0:c0 + cw] * win[8 - (taps - 1) + k:8 - (taps - 1) + k + n]
    return acc


def _ffn_act(name, u_raw, conv_w, conv_b, n_rows):
    tm, sub, cw = CONV_ROWS, CONV_SUB, CONV_LANES
    taps, width = conv_w.shape
    half = width // 2

    def body(cur_ref, prev_ref, w_ref, b_ref, f_ref, ext_ref):
        i = pl.program_id(0)
        ext_ref[0:8, :] = jnp.where(i > 0, prev_ref[...], 0.0)
        ext_ref[8:8 + tm, :] = cur_ref[...]
        for q in range(half // cw):
            a0, g0 = q * cw, half + q * cw

            def group(s, carry):
                r = pl.multiple_of(s * sub, sub)
                a = _conv_window(ext_ref[pl.ds(r, sub + 8), a0:a0 + cw], w_ref, b_ref, taps, a0, cw, sub)
                g = _conv_window(ext_ref[pl.ds(r, sub + 8), g0:g0 + cw], w_ref, b_ref, taps, g0, cw, sub)
                f_ref[pl.ds(r, sub), a0:a0 + cw] = (_silu(a) * g).astype(f_ref.dtype)
                return carry

            lax.fori_loop(0, tm // sub, group, 0, unroll=GROUP_UNROLL)

        @pl.when(i == 0)
        def _():
            f_ref[0:PAD, :] = jnp.zeros((PAD, half), f_ref.dtype)

    return pl.pallas_call(
        body, name=name, grid=(n_rows // tm,),
        in_specs=_conv_specs(tm, width, 0, n_rows, False) + [pl.BlockSpec((taps, width), lambda i: (0, 0)),
                                                             pl.BlockSpec((1, width), lambda i: (0, 0))],
        out_specs=pl.BlockSpec((tm, half), lambda i: (i, 0)),
        out_shape=jax.ShapeDtypeStruct((n_rows, half), BF16),
        scratch_shapes=[pltpu.VMEM((tm + 8, width), F32)],
        compiler_params=_cparams(1),
    )(u_raw, u_raw, conv_w, conv_b)


def _conv_bwd(name, raw, raw_blk, dsrcs, chunk_src, conv_w, conv_b, n_rows, gated, into=None, into_blk=0, bg=None):
    taps, width = conv_w.shape
    half = width // 2 if gated else width
    tm, sub, cw = CONV_ROWS, CONV_SUB, CONV_LANES
    te = tm + 16
    nd = len(dsrcs)
    n_parts = 2 if gated else 1

    def body(*refs):
        cur_ref, prev_ref, next_ref = refs[0:3]
        dcur, dnext = refs[3:3 + nd], refs[3 + nd:3 + 2 * nd]
        w_ref, b_ref = refs[3 + 2 * nd:5 + 2 * nd]
        out_ref, acc_ref, ext_ref, du_ref = refs[-4:]
        i = pl.program_id(0)
        ext_ref[0:8, :] = jnp.where(i > 0, prev_ref[...], 0.0)
        ext_ref[8:8 + tm, :] = cur_ref[...]
        ext_ref[8 + tm:24 + tm, :] = next_ref[...]

        for q, (src, off) in enumerate(chunk_src):
            cols = [q * cw, half + q * cw][:n_parts]

            def conv_grad(r, d, past_end):
                pre = [_conv_window(ext_ref[pl.ds(r, sub + 8), c0:c0 + cw], w_ref, b_ref, taps, c0, cw, sub) for c0 in cols]
                if gated:
                    act, dact = _silu_grad(pre[0])
                    dus = [d * pre[1] * dact, d * act]
                else:
                    dus = [d * _dsilu(pre[0])]
                for part, du in enumerate(dus):
                    if past_end:
                        du = jnp.where(i * tm + r + _iota((sub, 1), 0) < n_rows, du, 0.0)
                    du_ref[part, pl.ds(r, sub), :] = du

            def tile_rows(s, carry):
                r = pl.multiple_of(s * sub, sub)
                conv_grad(r, dcur[src][pl.ds(r, sub), off:off + cw].astype(F32), False)
                return carry

            lax.fori_loop(0, tm // sub, tile_rows, 0, unroll=GROUP_UNROLL)
            conv_grad(tm, dnext[src][:, off:off + cw].astype(F32), True)

            @pl.when(i == 0)
            def _():
                du_ref[:, 0:PAD, :] = jnp.zeros((n_parts, PAD, cw), F32)

            for part, c0 in enumerate(cols):
                taps_w = [w_ref[k:k + 1, c0:c0 + cw] for k in range(taps)]

                def back(s, sums):
                    new = list(sums)
                    for u in range(2):
                        r = pl.multiple_of((2 * s + u) * sub, sub)
                        win = du_ref[part, pl.ds(r, sub + 8), :]
                        raw_rows = ext_ref[pl.ds(8 + r, sub), c0:c0 + cw]
                        draw = jnp.zeros((sub, cw), F32)
                        for k in range(taps):
                            shifted = win[taps - 1 - k:taps - 1 - k + sub]
                            draw = draw + taps_w[k] * shifted
                            new[k] = new[k] + shifted * raw_rows
                        new[taps] = new[taps] + win[0:sub]
                        out_ref[pl.ds(r, sub), c0:c0 + cw] = draw.astype(out_ref.dtype)
                    return tuple(new)

                sums = lax.fori_loop(0, tm // (2 * sub), back, tuple(jnp.zeros((sub, cw), F32) for _ in range(taps + 1)))

                @pl.when(i == 0)
                def _(c0=c0):
                    out_ref[PAD - sub:PAD, c0:c0 + cw] = jnp.zeros((sub, cw), out_ref.dtype)

                for k in range(taps + 1):
                    total = jnp.sum(sums[k], axis=0, keepdims=True)
                    acc_ref[k:k + 1, c0:c0 + cw] = jnp.where(i == 0, total, acc_ref[k:k + 1, c0:c0 + cw] + total)

    in_specs = _conv_specs(tm, width, raw_blk, n_rows, True)
    in_specs += [pl.BlockSpec((tm, d.shape[1]), lambda i: (i, 0)) for d in dsrcs]
    in_specs += [pl.BlockSpec((16, d.shape[1]), lambda i: (jnp.minimum((i + 1) * (tm // 16), n_rows // 16 - 1), 0)) for d in dsrcs]
    in_specs += [pl.BlockSpec((taps, width), lambda i: (0, 0)), pl.BlockSpec((1, width), lambda i: (0, 0))]
    operands = [raw, raw, raw] + list(dsrcs) + list(dsrcs) + [conv_w, conv_b]
    aliases = {}
    if into is None:
        out0 = jax.ShapeDtypeStruct((n_rows, width), BF16)
    else:
        in_specs.append(pl.BlockSpec(memory_space=pl.ANY))
        operands.append(into)
        aliases = {len(operands) - 1: 0}
        out0 = jax.ShapeDtypeStruct(into.shape, into.dtype)
    return _call(body, name, (n_rows // tm,), in_specs,
                 [pl.BlockSpec((tm, width), lambda i: (i, into_blk)), pl.BlockSpec((8, width), lambda i: (0, 0))],
                 [out0, jax.ShapeDtypeStruct((8, width), F32)], operands,
                 scratch_shapes=[pltpu.VMEM((tm + 24, width), F32), pltpu.VMEM((n_parts, te + 8, cw), F32)],
                 aliases=aliases, bg=bg)


def _ssd_specs(n_chunks, rev, per_step=1):
    cidx = (lambda c: n_chunks - 1 - c) if rev else (lambda c: c)
    xw, nw = per_step * GROUP_W, per_step * D_STATE
    xg0, bg0, cg0 = P_XBC // xw, (P_XBC + D_INNER) // nw, (P_XBC + D_INNER + SSM_GROUPS * D_STATE) // nw

    def cur(width, blk0):
        return pl.BlockSpec((T, width), lambda g, c: (cidx(c), blk0 + g))

    def prev(width, blk0):
        return pl.BlockSpec((8, width), lambda g, c: (jnp.maximum(cidx(c) * (T // 8) - 1, 0), blk0 + g))

    specs = [cur(xw, xg0), prev(xw, xg0), cur(nw, bg0), prev(nw, bg0), cur(nw, cg0), prev(nw, cg0),
             pl.BlockSpec((T, 128), lambda g, c: (cidx(c), P_DT // 128))]
    wb, wc = D_INNER // nw, (D_INNER + SSM_GROUPS * D_STATE) // nw
    specs += [pl.BlockSpec((4, xw), lambda g, c: (0, g)),
              pl.BlockSpec((4, nw), lambda g, c: (0, wb + g)),
              pl.BlockSpec((4, nw), lambda g, c: (0, wc + g)),
              pl.BlockSpec((1, xw), lambda g, c: (0, g)),
              pl.BlockSpec((1, nw), lambda g, c: (0, wb + g)),
              pl.BlockSpec((1, nw), lambda g, c: (0, wc + g))]
    specs += [pl.BlockSpec((1, 128), lambda g, c: (0, 0))] * 3
    return specs, cidx


def _ssd_shared(refs, c):
    dt_ref, dtb_ref, alog_ref = refs[6], refs[13], refs[14]
    valid = _valid_rows(c * T, T, PAD)
    dtr = dt_ref[...] + dtb_ref[...]
    dt = jnp.where(valid, _softplus(dtr), 0.0)
    a_neg = -jnp.exp(alog_ref[...])
    tril = _iota((T, T), 0) >= _iota((T, T), 1)
    cs = _xdot_l(tril.astype(BF16), dt * a_neg)
    return dict(valid=valid, dtr=dtr, dt=dt, a_neg=a_neg, tril=tril, cs=cs, cs_t=cs.T)


def _heads_of_lanes():
    hh_t, ll_t = _iota((D_INNER, 128), 1), _iota((D_INNER, 128), 0)
    return (hh_t == jnp.right_shift(ll_t, 6)).astype(BF16)


def _ssd_chunk_forward(refs, ext_ref, g, c, shared):
    (xc_ref, xp_ref, bc_ref, bp_ref, cc_ref, cp_ref, dt_ref, wx_ref, wb_ref, wc_ref,
     bx_ref, bb_ref, bcb_ref, dtb_ref, alog_ref, dsk_ref) = refs

    def conv_pre(cur_ref, prev_ref, w_ref, b_ref, width):
        ext_ref[0:8, 0:width] = jnp.where(c > 0, prev_ref[...], 0.0)
        ext_ref[8:8 + T, 0:width] = cur_ref[...]
        w = w_ref[...]
        acc = b_ref[...] + w[3:4] * cur_ref[...]
        for k in range(3):
            acc = acc + w[k:k + 1] * ext_ref[pl.ds(5 + k, T), 0:width]
        return acc

    v = dict(shared)
    valid = v["valid"]
    v["head0"] = 8 * g
    v["x_pre"] = conv_pre(xc_ref, xp_ref, wx_ref, bx_ref, GROUP_W)
    v["b_pre"] = conv_pre(bc_ref, bp_ref, wb_ref, bb_ref, D_STATE)
    v["c_pre"] = conv_pre(cc_ref, cp_ref, wc_ref, bcb_ref, D_STATE)
    xs = _silu(v["x_pre"])
    bm = jnp.where(valid, _silu(v["b_pre"]), 0.0)
    cm = jnp.where(valid, _silu(v["c_pre"]), 0.0)
    hh, ll = _iota((128, GROUP_W), 0), _iota((128, GROUP_W), 1)
    expand = (hh == 8 * g + jnp.right_shift(ll, 6)).astype(BF16)
    cs_e = _xdot(v["cs"], expand)
    dt_e = _xdot(v["dt"], expand)
    cs_last_e = cs_e[T - 1:T, :]
    v.update(xs=xs, bm=bm, cm=cm, cs_e=cs_e, dt_e=dt_e, cs_last_e=cs_last_e)
    v["xdt"] = xs * dt_e
    v["decay_e"] = jnp.exp(cs_last_e - cs_e)
    v["ecs_e"] = jnp.exp(cs_e)
    v["elast_e"] = jnp.exp(cs_last_e)
    v["d_e"] = _xdot(dsk_ref[...], expand)
    v["gmat"] = _dot_nt(cm.astype(BF16), bm.astype(BF16))
    return v


def _ssd_decay_pair(v, jp):
    out = []
    for j in (v["head0"] + 2 * jp, v["head0"] + 2 * jp + 1):
        diff = v["cs"][:, j:j + 1] - v["cs_t"][j:j + 1, :]
        out.append(jnp.where(v["tril"], jnp.exp(jnp.where(v["tril"], diff, 0.0)), 0.0))
    return out


def _block_diag_pair(xp):
    lane = _iota(xp.shape, 1)
    return jnp.concatenate([jnp.where(lane < HEAD_P, xp, 0.0), jnp.where(lane >= HEAD_P, xp, 0.0)], axis=0)


SSD_GROUPS_PER_STEP = 4


def _ssd_group_refs(refs, gg):
    x_w, n_w = pl.ds(GROUP_W * gg, GROUP_W), pl.ds(D_STATE * gg, D_STATE)
    lanes = [x_w, x_w, n_w, n_w, n_w, n_w, None, x_w, n_w, n_w, x_w, n_w, n_w, None, None, None]
    return [r if w is None else r.at[:, w] for r, w in zip(refs, lanes)]


def _ssd_fwd(p, conv_w, conv_b, dt_bias, a_log, d_skip, n_chunks, bg=None):
    n_rows = n_chunks * T
    in_specs, _ = _ssd_specs(n_chunks, rev=False, per_step=SSD_GROUPS_PER_STEP)
    per = SSD_GROUPS_PER_STEP
    assert per == SSM_GROUPS

    def body(*refs):
        y_ref, hin_ref, st_ref, ext_ref = refs[16:]
        c = pl.program_id(1)

        @pl.when(c == 0)
        def _():
            st_ref[...] = jnp.zeros_like(st_ref)

        shared = _ssd_shared(refs[:16], c)
        for gg in range(per):
            v = _ssd_chunk_forward(_ssd_group_refs(refs[:16], gg), ext_ref.at[gg], gg, c, shared)
            state = st_ref[gg]
            hin_ref[gg] = state
            ys = []
            for jp in range(4):
                l0, l1 = _ssd_decay_pair(v, jp)
                lhs = jnp.concatenate([v["gmat"] * l0, v["gmat"] * l1], axis=1).astype(BF16)
                rhs = _block_diag_pair(v["xdt"][:, 128 * jp:128 * jp + 128]).astype(BF16)
                ys.append(_dot(lhs, rhs))
            y = jnp.concatenate(ys, axis=1)
            y = y + _dot(v["cm"].astype(BF16), state.astype(BF16)) * v["ecs_e"] + v["xs"] * v["d_e"]
            y_ref[:, GROUP_W * gg:GROUP_W * gg + GROUP_W] = y
            s_new = _dot_tn(v["bm"].astype(BF16), (v["xdt"] * v["decay_e"]).astype(BF16))
            st_ref[gg] = state * v["elast_e"] + s_new

    return _call(
        body, "ssd_fwd", (SSM_GROUPS // per, n_chunks), in_specs,
        [pl.BlockSpec((T, per * GROUP_W), lambda g, c: (c, g)),
         pl.BlockSpec((per, None, D_STATE, GROUP_W), lambda g, c: (g, c, 0, 0))],
        [jax.ShapeDtypeStruct((n_rows, D_INNER), F32),
         jax.ShapeDtypeStruct((SSM_GROUPS, n_chunks, D_STATE, GROUP_W), F32)],
        [p, p, p, p, p, p, p, conv_w, conv_w, conv_w, conv_b, conv_b, conv_b, dt_bias, a_log, d_skip],
        scratch_shapes=[pltpu.VMEM((per, D_STATE, GROUP_W), F32), pltpu.VMEM((per, T + 8, GROUP_W), F32)], bg=bg)


def _ssd_bwd(p, conv_w, conv_b, dt_bias, a_log, d_skip, hin, dy, dp, n_chunks, bg=None):
    n_rows = n_chunks * T
    per = SSD_GROUPS_PER_STEP
    assert per == SSM_GROUPS
    dt_w = P_Z - P_DT
    in_specs, cidx = _ssd_specs(n_chunks, rev=True, per_step=per)
    in_specs = in_specs + [pl.BlockSpec((per, None, D_STATE, GROUP_W), lambda g, c: (g, cidx(c), 0, 0)),
                           pl.BlockSpec((T, per * GROUP_W), lambda g, c: (cidx(c), g)), ANY]

    def body(*refs):
        hin_ref, dy_ref = refs[16:18]
        dx_ref, db_ref, dc_ref, dp_ref, dpar_ref, dst_ref, ext_ref, red_ref, dd_ref = refs[19:]
        step = pl.program_id(1)
        shared = _ssd_shared(refs[:16], n_chunks - 1 - step)
        local = jnp.zeros((T, 128), F32)
        for gg in range(per):
            x_w, n_w = pl.ds(GROUP_W * gg, GROUP_W), pl.ds(D_STATE * gg, D_STATE)
            local = local + group_body(_ssd_group_refs(refs[:16], gg), hin_ref.at[gg], dy_ref.at[:, x_w],
                                       dx_ref.at[:, x_w], db_ref.at[:, n_w], dc_ref.at[:, n_w], red_ref.at[:, :, x_w],
                                       dd_ref.at[:, x_w], dst_ref.at[gg], ext_ref.at[gg], gg, shared)
        to_heads = _heads_of_lanes()
        dcs = _xdot(red_ref[0], to_heads) + local
        triu = (_iota((T, T), 0) <= _iota((T, T), 1)).astype(BF16)
        da = _xdot_l(triu, dcs)
        ddt = da * shared["a_neg"] + _xdot(red_ref[1], to_heads)
        ddtr = jnp.where(shared["valid"], ddt * _sigmoid(shared["dtr"]), 0.0)
        dp_ref[...] = jnp.concatenate([ddtr, jnp.zeros((T, dt_w - 128), F32)], axis=1).astype(dp_ref.dtype)
        dpar = jnp.concatenate([
            jnp.sum(ddtr, axis=0, keepdims=True),
            jnp.sum(da * shared["dt"], axis=0, keepdims=True) * shared["a_neg"],
            _xdot(dd_ref[0:1, :], to_heads),
            jnp.zeros((5, 128), F32)], axis=0)
        dpar_ref[...] = jnp.where(step == 0, dpar, dpar_ref[...] + dpar)

    def group_body(in_refs, hin_ref, dy_ref, dx_ref, db_ref, dc_ref, red_ref, dd_ref, dst_ref, ext_ref, g, shared):
        step = pl.program_id(1)
        c = n_chunks - 1 - step

        @pl.when(step == 0)
        def _():
            dst_ref[...] = jnp.zeros_like(dst_ref)

        v = _ssd_chunk_forward(in_refs, ext_ref, g, c, shared)
        hin_f = hin_ref[...]
        hin_b = hin_f.astype(BF16)
        dyv = dy_ref[...]
        dst = dst_ref[...]
        dst_b = dst.astype(BF16)
        xs, bm, cm, xdt = v["xs"], v["bm"], v["cm"], v["xdt"]
        bm_b, cm_b = bm.astype(BF16), cm.astype(BF16)

        dd_e = jnp.sum(dyv * xs, axis=0, keepdims=True)
        dxs = dyv * v["d_e"]
        ch = _dot(cm_b, hin_b)
        dch = (dyv * v["ecs_e"]).astype(BF16)
        dcm = _dot_nt(dch, hin_b)
        dhin = _dot_tn(cm_b, dch) + dst * v["elast_e"]
        dcs_e = dyv * ch * v["ecs_e"]
        dxd = _dot(bm_b, dst_b)
        dbm = _dot_nt((xdt * v["decay_e"]).astype(BF16), dst_b)
        dxdt_state = dxd * v["decay_e"]
        q = dxdt_state * xdt
        dcs_e = dcs_e - q
        dlast_e = jnp.sum(q, axis=0, keepdims=True) + jnp.sum(dst * hin_f, axis=0, keepdims=True) * v["elast_e"]
        dg = jnp.zeros((T, T), F32)
        rs_cols = jnp.zeros((T, 128), F32)
        cs_rows = jnp.zeros((128, T), F32)
        lane_i, sub_i = _iota((T, 128), 1), _iota((128, T), 0)
        dxdt_parts = []
        for jp in range(4):
            l0, l1 = _ssd_decay_pair(v, jp)
            m0, m1 = v["gmat"] * l0, v["gmat"] * l1
            xbd = _block_diag_pair(xdt[:, 128 * jp:128 * jp + 128]).astype(BF16)
            dyp = dyv[:, 128 * jp:128 * jp + 128]
            dm = _dot_nt(dyp.astype(BF16), xbd)
            dm0, dm1 = dm[:, 0:T], dm[:, T:2 * T]
            dg = dg + dm0 * l0 + dm1 * l1
            for j, qq in ((v["head0"] + 2 * jp, dm0 * m0), (v["head0"] + 2 * jp + 1, dm1 * m1)):
                rs_cols = jnp.where(lane_i == j, jnp.sum(qq, axis=1, keepdims=True), rs_cols)
                cs_rows = jnp.where(sub_i == j, jnp.sum(qq, axis=0, keepdims=True), cs_rows)
            mv = jnp.concatenate([m0, m1], axis=0).astype(BF16)
            dxdt_parts.append(_dot_tn(mv, _block_diag_pair(dyp).astype(BF16)))
        dxdt = jnp.concatenate(dxdt_parts, axis=1) + dxdt_state
        dg_b = dg.astype(BF16)
        dcm = dcm + _dot(dg_b, bm_b)
        dbm = dbm + _dot_tn(dg_b, cm_b)
        last_row = _iota((T, 1), 0) == T - 1
        red_ref[0] = dcs_e + jnp.where(last_row, dlast_e, 0.0)
        red_ref[1] = dxdt * xs
        dd_ref[0:1, :] = dd_e
        dx_ref[...] = dxs + dxdt * v["dt_e"]
        db_ref[...] = jnp.where(v["valid"], dbm, 0.0)
        dc_ref[...] = jnp.where(v["valid"], dcm, 0.0)
        dst_ref[...] = dhin
        return rs_cols - cs_rows.T

    return _call(
        body, "ssd_bwd", (SSM_GROUPS // per, n_chunks), in_specs,
        [pl.BlockSpec((T, per * GROUP_W), lambda g, c: (cidx(c), g)),
         pl.BlockSpec((T, per * D_STATE), lambda g, c: (cidx(c), g)),
         pl.BlockSpec((T, per * D_STATE), lambda g, c: (cidx(c), g)),
         pl.BlockSpec((T, dt_w), lambda g, c: (cidx(c), P_DT // dt_w)),
         pl.BlockSpec((8, 128), lambda g, c: (0, 0))],
        [jax.ShapeDtypeStruct((n_rows, D_INNER), F32),
         jax.ShapeDtypeStruct((n_rows, SSM_GROUPS * D_STATE), F32),
         jax.ShapeDtypeStruct((n_rows, SSM_GROUPS * D_STATE), F32),
         jax.ShapeDtypeStruct(dp.shape, dp.dtype),
         jax.ShapeDtypeStruct((8, 128), F32)],
        [p, p, p, p, p, p, p, conv_w, conv_w, conv_w, conv_b, conv_b, conv_b, dt_bias, a_log, d_skip, hin, dy, dp],
        scratch_shapes=[pltpu.VMEM((per, D_STATE, GROUP_W), F32), pltpu.VMEM((per, T + 8, GROUP_W), F32),
                        pltpu.VMEM((2, T, D_INNER), F32), pltpu.VMEM((8, D_INNER), F32)],
        aliases={18: 3}, bg=bg)


def _alibi_slope(h):
    return 2.0 ** (-8.0 * (h + 1) / ATTN_HEADS)


def _dup_half(x256, kvh):
    xb = x256[:, 128 * (kvh // 2):128 * (kvh // 2) + 128]
    rolled = pltpu.roll(xb, 64, 1)
    lane = _iota(xb.shape, 1)
    if kvh % 2 == 0:
        return jnp.where(lane < 64, xb, rolled)
    return jnp.where(lane < 64, rolled, xb)


def _attn_masks(c):
    qi, j = _iota((T, T), 0), _iota((T, T), 1)
    tri = j <= qi
    meta_ok = (j >= PAD) & (j - PAD <= c * T + qi - PAD)
    band_ok = c >= jnp.where(tri, 1, 2)
    dist = jnp.bitwise_and(qi - j, T - 1).astype(F32)
    return tri, meta_ok, band_ok, dist


def _fold(x3, tri):
    return jnp.concatenate([x3[:, 0:T], jnp.where(tri, x3[:, 2 * T:3 * T], x3[:, T:2 * T])], axis=1)


def _unfold(x2, tri):
    band = x2[:, T:2 * T]
    return jnp.concatenate([x2[:, 0:T], jnp.where(tri, 0.0, band), jnp.where(tri, band, 0.0)], axis=1)


def _attn_fwd(p, sinks, n_chunks, bg=None):
    n_rows = n_chunks * T
    kb, vb = P_K // KV_W, P_V // KV_W

    def body(q_ref, kc_ref, kp_ref, km_ref, vc_ref, vp_ref, vm_ref, sink_ref, o_ref, lse_ref):
        c = pl.program_id(0)
        sinks_v = sink_ref[...]
        masks = _attn_masks(c)
        tri, meta_ok, band_ok, dist = masks
        lane = _iota((T, 128), 1)
        for kvh in range(KV_HEADS):
            k3 = jnp.concatenate([_dup_half(r[...], kvh) for r in (km_ref, kp_ref, kc_ref)], axis=0).astype(BF16)
            v3 = jnp.concatenate([_dup_half(r[...], kvh) for r in (vm_ref, vp_ref, vc_ref)], axis=0)
            v3bd = _block_diag_rows(v3).astype(BF16)
            q2 = q_ref[:, 256 * kvh:256 * kvh + 256] * SCALE
            q4 = jnp.concatenate([jnp.where((lane < 64) if half == 0 else (lane >= 64), q2[:, 128 * pr:128 * pr + 128], 0.0)
                                  for pr in range(2) for half in range(2)], axis=0).astype(BF16)
            raw4 = _dot_nt(q4, k3)
            probs = []
            for hh in range(4):
                h = 4 * kvh + hh
                raw = raw4[T * hh:T * hh + T]
                band = jnp.where(tri, raw[:, 2 * T:3 * T], raw[:, T:2 * T]) - _alibi_slope(h) * dist
                sc = jnp.concatenate([jnp.where(meta_ok, raw[:, 0:T], NEG), jnp.where(band_ok, band, NEG)], axis=1)
                sink = sinks_v[:, h:h + 1]
                m = jnp.maximum(jnp.max(sc, axis=1, keepdims=True), sink)
                e = jnp.exp(sc - m)
                den = jnp.sum(e, axis=1, keepdims=True) + jnp.exp(sink - m)
                probs.append(_unfold(e * (1.0 / den), tri))
                lse_ref[:, h:h + 1] = m + jnp.log(den)
            p4 = jnp.concatenate([jnp.concatenate(probs[0:2], axis=1), jnp.concatenate(probs[2:4], axis=1)], axis=0)
            out = _dot(p4.astype(BF16), v3bd)
            o_ref[:, 256 * kvh:256 * kvh + 256] = jnp.concatenate([out[0:T], out[T:2 * T]], axis=1).astype(o_ref.dtype)

    blk = lambda width, col: pl.BlockSpec((T, width), lambda c: (c, col))
    prev = lambda width, col: pl.BlockSpec((T, width), lambda c: (jnp.maximum(c - 1, 0), col))
    first = lambda width, col: pl.BlockSpec((T, width), lambda c: (0, col))
    return _call(
        body, "attn_fwd", (n_chunks,),
        [blk(ATTN_W, P_Q // ATTN_W), blk(KV_W, kb), prev(KV_W, kb), first(KV_W, kb),
         blk(KV_W, vb), prev(KV_W, vb), first(KV_W, vb), pl.BlockSpec((1, 128), lambda c: (0, 0))],
        [pl.BlockSpec((T, ATTN_W), lambda c: (c, 0)), pl.BlockSpec((T, 128), lambda c: (c, 0))],
        [jax.ShapeDtypeStruct((n_rows, ATTN_W), BF16), jax.ShapeDtypeStruct((n_rows, 128), F32)],
        [p, p, p, p, p, p, p, sinks], bg=bg)


def _block_diag_rows(x3):
    lane = _iota(x3.shape, 1)
    return jnp.concatenate([jnp.where(lane < 64, x3, 0.0), jnp.where(lane >= 64, x3, 0.0)], axis=0)


def _fold_halves(x):
    return x + pltpu.roll(x, 64, 1)


def _attn_bwd(p, sinks, ao, lse, dao, dp, n_chunks, bg=None):
    kb, vb = P_K // KV_W, P_V // KV_W
    rc = lambda s: n_chunks - 1 - s

    def body(q_ref, kc_ref, kp_ref, km_ref, vc_ref, vp_ref, vm_ref, sink_ref, o_ref, lse_ref, do_ref, dp_in_ref,
             dqkv_ref, dsink_ref, kcar_ref, vcar_ref, kmeta_ref, vmeta_ref):
        step = pl.program_id(0)
        c = n_chunks - 1 - step

        @pl.when(step == 0)
        def _():
            for r in (kcar_ref, vcar_ref, kmeta_ref, vmeta_ref):
                r[...] = jnp.zeros_like(r)

        masks = _attn_masks(c)
        tri = masks[0]
        q = q_ref[...] * SCALE
        sinks_v = sink_ref[...]
        lse_v = lse_ref[...]
        ov = o_ref[...].astype(F32)
        dov = do_ref[...].astype(F32)
        lane = _iota((T, 128), 1)
        lane256 = _iota((3 * T, KV_W), 1)
        dsink = jnp.zeros((1, 128), F32)
        dk3_all = jnp.zeros((3 * T, KV_W), F32)
        dv3_all = jnp.zeros((3 * T, KV_W), F32)
        dqs = []
        for kvh in range(KV_HEADS):
            k3 = jnp.concatenate([_dup_half(r[...], kvh) for r in (km_ref, kp_ref, kc_ref)], axis=0).astype(BF16)
            v3 = jnp.concatenate([_dup_half(r[...], kvh) for r in (vm_ref, vp_ref, vc_ref)], axis=0).astype(BF16)
            halves = [(pr, half, (lane < 64) if half == 0 else (lane >= 64)) for pr in range(2) for half in range(2)]
            cols = [slice(128 * (2 * kvh + pr), 128 * (2 * kvh + pr) + 128) for pr in range(2)]
            q4 = jnp.concatenate([jnp.where(mine, q[:, cols[pr]], 0.0) for pr, _, mine in halves], axis=0).astype(BF16)
            do4 = jnp.concatenate([jnp.where(mine, dov[:, cols[pr]], 0.0) for pr, _, mine in halves], axis=0).astype(BF16)
            raw4 = _dot_nt(q4, k3)
            dp4 = _dot_nt(do4, v3)
            ds_rows, pm_rows = [], []
            for hh, (pr, half, mine) in enumerate(halves):
                h = 4 * kvh + hh
                raw = raw4[T * hh:T * hh + T]
                band = jnp.where(tri, raw[:, 2 * T:3 * T], raw[:, T:2 * T]) - _alibi_slope(h) * masks[3]
                sc = jnp.concatenate([jnp.where(masks[1], raw[:, 0:T], NEG), jnp.where(masks[2], band, NEG)], axis=1)
                lse_h = lse_v[:, h:h + 1]
                pm = jnp.exp(sc - lse_h)
                prod = dov[:, cols[pr]] * ov[:, cols[pr]]
                delta = jnp.sum(jnp.where(mine, prod, 0.0), axis=1, keepdims=True)
                dp = _fold(dp4[T * hh:T * hh + T], tri)
                ds_rows.append(_unfold(pm * (dp - delta), tri))
                pm_rows.append(_unfold(pm, tri))
                p_sink = jnp.exp(sinks_v[:, h:h + 1] - lse_h)
                dsink = jnp.where(_iota((1, 128), 1) == h, jnp.sum(-p_sink * delta, axis=0, keepdims=True), dsink)
            ds4 = jnp.concatenate(ds_rows, axis=0).astype(BF16)
            dq4 = _dot(ds4, k3)
            dk3 = _dot_tn(ds4, q4)
            dv3 = _dot_tn(jnp.concatenate(pm_rows, axis=0).astype(BF16), do4)
            for pr in range(2):
                dqs.append(jnp.where(lane < 64, dq4[2 * T * pr:2 * T * pr + T], dq4[2 * T * pr + T:2 * T * pr + 2 * T]) * SCALE)
            in_place = (lane256 >= 64 * kvh) & (lane256 < 64 * kvh + 64)
            wide = lambda x: jnp.concatenate([x, x], axis=1)
            dk3_all = jnp.where(in_place, wide(_fold_halves(dk3)), dk3_all)
            dv3_all = jnp.where(in_place, wide(_fold_halves(dv3)), dv3_all)
        dsink_all = dsink

        @pl.when(step == 0)
        def _():
            dsink_ref[...] = dsink_all

        @pl.when(step > 0)
        def _():
            dsink_ref[...] += dsink_all

        kmeta = kmeta_ref[...] + dk3_all[0:T]
        vmeta = vmeta_ref[...] + dv3_all[0:T]
        kmeta_ref[...] = kmeta
        vmeta_ref[...] = vmeta
        is_first = c == 0
        dk = jnp.where(is_first, kmeta, dk3_all[2 * T:3 * T] + kcar_ref[...])
        dv = jnp.where(is_first, vmeta, dv3_all[2 * T:3 * T] + vcar_ref[...])
        dqkv_ref[...] = jnp.concatenate(dqs + [dk, dv], axis=1).astype(dqkv_ref.dtype)
        kcar_ref[...] = dk3_all[T:2 * T]
        vcar_ref[...] = dv3_all[T:2 * T]

    blk = lambda width, col: pl.BlockSpec((T, width), lambda s: (rc(s), col))
    prev = lambda width, col: pl.BlockSpec((T, width), lambda s: (jnp.maximum(rc(s) - 1, 0), col))
    first = lambda width, col: pl.BlockSpec((T, width), lambda s: (0, col))
    return _call(
        body, "attn_bwd", (n_chunks,),
        [blk(ATTN_W, P_Q // ATTN_W), blk(KV_W, kb), prev(KV_W, kb), first(KV_W, kb),
         blk(KV_W, vb), prev(KV_W, vb), first(KV_W, vb), pl.BlockSpec((1, 128), lambda s: (0, 0)),
         blk(ATTN_W, 0), blk(128, 0), blk(ATTN_W, 0), ANY],
        [blk(QKV_W, P_Q // QKV_W), pl.BlockSpec((1, 128), lambda s: (0, 0))],
        [jax.ShapeDtypeStruct(dp.shape, dp.dtype), jax.ShapeDtypeStruct((1, 128), F32)],
        [p, p, p, p, p, p, p, sinks, ao, lse, dao, dp],
        scratch_shapes=[pltpu.VMEM((T, KV_W), F32)] * 4, aliases={11: 0}, bg=bg)


def _pad_lanes(v, width=128):
    return jnp.pad(v, ((0, 0), (0, width - v.shape[1])))


def _local_step(x, head, tgt, plan):
    w, g, run = plan.w, plan.g, plan.run
    n_tok = x.shape[0]
    n_rows = n_tok + T
    n_chunks = n_rows // T
    tm = _row_tile(n_rows, 384)
    dt_bias, a_log, d_skip = (_pad_lanes(w[k]) for k in ("ssm_dt_bias", "ssm_a_log", "ssm_d_skip"))
    sinks = _pad_lanes(w["attn_sinks"])
    x_in = [(x, D_MODEL, 0, "prev"), (head, D_MODEL, 0, "first")]
    head_tm = jnp.concatenate([head, jnp.zeros((tm - T, D_MODEL), F32)], axis=0)
    x_in_tm = [(x, D_MODEL, 0, "prev"), (head_tm, D_MODEL, 0, "first")]

    def h0_tile(r0, xt, hd):
        return jnp.where(_valid_rows(r0, xt.shape[0], T), xt, hd)

    n1, = _rowwise("norm_pre_mix", lambda r0, xt, hd, wn: [_rms(h0_tile(r0, xt, hd), wn)], n_rows, tm,
                   x_in_tm, [w["norm_pre_mix"]], [(D_MODEL, BF16)], [])
    p = _matmul("in_proj", n1, w["w_cat"], "nn", F32)
    y_ssd, hin = run("ssd_fwd", _ssd_fwd, p, w["ssm_conv_w"], w["ssm_conv_b"], dt_bias, a_log, d_skip, n_chunks)
    ao, lse = run("attn_fwd", _attn_fwd, p, sinks, n_chunks)

    def gate_norm(r0, y, z, wn):
        return [_rms(y * _silu(z), wn)]

    yn, = run("ssm_gate_norm", _rowwise, "ssm_gate_norm", gate_norm, n_rows, tm,
              [(y_ssd, D_INNER, 0), (p, D_INNER, P_Z // D_INNER)], [w["ssm_norm"]], [(D_INNER, BF16)], [])
    y_ssm = _matmul("ssm_out", yn, w["w_ssm_out"], "nn", F32)
    y_attn = _matmul("attn_out", ao, w["w_attn_out"], "nn", F32)

    def mix_gate(r0, ys, ya, gs, ga):
        return [_sigmoid(gs) * ys + _sigmoid(ga) * ya]

    gate_ins = [(p, D_MODEL, P_GATE // D_MODEL), (p, D_MODEL, P_GATE // D_MODEL + 1)]
    mixed, = _rowwise("mix_gate", mix_gate, n_rows, tm, [(y_ssm, D_MODEL, 0), (y_attn, D_MODEL, 0)] + gate_ins,
                      [], [(D_MODEL, BF16)], [])
    mix = _matmul("mix_out", mixed, w["w_mix_out"], "nn", F32)

    def post_mix(r0, mx, xt, hd, w_post, w_pre):
        h1 = jnp.where(_valid_rows(r0, mx.shape[0], PAD), h0_tile(r0, xt, hd) + _rms(mx, w_post), 0.0)
        return [h1, _rms(h1, w_pre)]

    h1, n2 = _rowwise("post_mix", post_mix, n_rows, tm, [(mix, D_MODEL, 0)] + x_in_tm,
                      [w["norm_post_mix"], w["norm_pre_ffn"]], [(D_MODEL, F32), (D_MODEL, BF16)], [])
    u_raw = _matmul("ffn_up", n2, w["w_ffn_up"], "nn", F32)
    f = _ffn_act("ffn_act", u_raw, w["ffn_conv_w"], w["ffn_conv_b"], n_rows)
    ffn = _matmul("ffn_down", f, w["w_ffn_down"], "nn", F32)

    def final(r0, fo, h, t, w_post):
        real = _valid_rows(r0, fo.shape[0], T)
        err = jnp.where(real, h + _rms(fo, w_post) - t, 0.0)
        dy = err * (1.0 / D_MODEL)
        dffn, dw = _rms_bwd(dy, fo, w_post)
        return [dffn, dy, jnp.sum(err * err, axis=0, keepdims=True), dw]

    dffn, dh2, loss_cols, g_norm_post_ffn = _rowwise(
        "loss_head", final, n_rows, tm, [(ffn, D_MODEL, 0), (h1, D_MODEL, 0), (tgt, D_MODEL, 0, "prev")],
        [w["norm_post_ffn"]], [(D_MODEL, BF16), (D_MODEL, F32)], [D_MODEL, D_MODEL])

    g["norm_post_ffn"] = g_norm_post_ffn
    g["w_ffn_down"] = _matmul("ffn_down_dw", f, dffn, "tn", F32)
    df = _matmul("ffn_down_dx", dffn, w["w_ffn_down"], "nt", F32)
    du_raw, dconv = _conv_bwd("ffn_act_bwd", u_raw, 0, [df], [(0, c0) for c0 in range(0, FFN_DIM, CONV_LANES)],
                              w["ffn_conv_w"], w["ffn_conv_b"], n_rows, True)
    g["ffn_conv_w"], g["ffn_conv_b"] = dconv[0:3], dconv[3:4]
    g["w_ffn_up"] = _matmul("ffn_up_dw", n2, du_raw, "tn", F32)
    dn2 = run("ffn_up_dx", _matmul, "ffn_up_dx", du_raw, w["w_ffn_up"], "nt", F32)

    def post_mix_bwd(r0, dn, d2, h, mx, w_pre, w_post):
        dx, dw_pre = _rms_bwd(dn, h, w_pre)
        dh1 = jnp.where(_valid_rows(r0, dn.shape[0], PAD), dx + d2, 0.0)
        dmix, dw_post = _rms_bwd(dh1, mx, w_post)
        return [dh1, dmix, dw_pre, dw_post]

    dh1, dmix, g["norm_pre_ffn"], g["norm_post_mix"] = _rowwise(
        "post_mix_bwd", post_mix_bwd, n_rows, tm,
        [(dn2, D_MODEL, 0), (dh2, D_MODEL, 0), (h1, D_MODEL, 0), (mix, D_MODEL, 0)],
        [w["norm_pre_ffn"], w["norm_post_mix"]], [(D_MODEL, F32), (D_MODEL, BF16)], [D_MODEL, D_MODEL])
    g["w_mix_out"] = _matmul("mix_out_dw", mixed, dmix, "tn", F32)
    dmixed = _matmul("mix_out_dx", dmix, w["w_mix_out"], "nt", F32)

    def mix_gate_bwd(r0, dm, ys, ya, gs, ga):
        ss, sa = _sigmoid(gs), _sigmoid(ga)
        dgate = jnp.concatenate([dm * ys * ss * (1.0 - ss), dm * ya * sa * (1.0 - sa)], axis=1)
        return [dm * ss, dm * sa, dgate]

    dys, dya, dp = _rowwise(
        "mix_gate_bwd", mix_gate_bwd, n_rows, tm,
        [(dmixed, D_MODEL, 0), (y_ssm, D_MODEL, 0), (y_attn, D_MODEL, 0)] + gate_ins,
        [], [(D_MODEL, BF16), (D_MODEL, BF16), (2 * D_MODEL, BF16, "new", P_W, P_GATE // (2 * D_MODEL))], [])
    g["w_ssm_out"] = _matmul("ssm_out_dw", yn, dys, "tn", F32)
    dyn = _matmul("ssm_out_dx", dys, w["w_ssm_out"], "nt", F32)
    g["w_attn_out"] = _matmul("attn_out_dw", ao, dya, "tn", F32)
    dao = _matmul("attn_out_dx", dya, w["w_attn_out"], "nt", BF16)

    def gate_norm_bwd(r0, dn, y, z, wn):
        sz, dsz = _silu_grad(z)
        dyz, dw = _rms_bwd(dn, y * sz, wn)
        live = _valid_rows(r0, dn.shape[0], PAD)
        return [jnp.where(live, dyz * sz, 0.0), jnp.where(live, dyz * y * dsz, 0.0), dw]

    dy_ssd, dp, g["ssm_norm"] = run(
        "ssm_gate_norm_bwd", _rowwise, "ssm_gate_norm_bwd", gate_norm_bwd, n_rows, tm,
        [(dyn, D_INNER, 0), (y_ssd, D_INNER, 0), (p, D_INNER, P_Z // D_INNER)],
        [w["ssm_norm"]], [(D_INNER, F32), (D_INNER, BF16, "into", dp, P_Z // D_INNER)], [D_INNER])
    dp, dsink = run("attn_bwd", _attn_bwd, p, sinks, ao, lse, dao, dp, n_chunks)
    g["attn_sinks"] = dsink[:, 0:ATTN_HEADS]
    dxs, dbm, dcm, dp, dpar = run("ssd_bwd", _ssd_bwd, p, w["ssm_conv_w"], w["ssm_conv_b"], dt_bias, a_log,
                                  d_skip, hin, dy_ssd, dp, n_chunks)
    g["ssm_dt_bias"], g["ssm_a_log"], g["ssm_d_skip"] = (dpar[i:i + 1, 0:SSM_HEADS] for i in range(3))
    x_chunks = [(src, c0) for src, arr in enumerate((dxs, dbm, dcm)) for c0 in range(0, arr.shape[1], CONV_LANES)]
    dp, dconv = run("ssm_conv_bwd", _conv_bwd, "ssm_conv_bwd", p, P_XBC // CONV_DIM, [dxs, dbm, dcm], x_chunks,
                    w["ssm_conv_w"], w["ssm_conv_b"], n_rows, False, into=dp, into_blk=P_XBC // CONV_DIM)
    g["ssm_conv_w"], g["ssm_conv_b"] = dconv[0:4], dconv[4:5]
    g["w_cat_t"] = _matmul("in_proj_dw", dp, n1, "tn", F32)
    dn1 = run("in_proj_dx", _matmul, "in_proj_dx", dp, w["w_cat"], "nt", F32)

    def pre_mix_bwd(r0, dn, d1, xt, hd, wn):
        dx, dw = _rms_bwd(dn, h0_tile(r0, xt, hd), wn)
        dh0 = jnp.where(_valid_rows(r0, dn.shape[0], PAD), dx + d1, 0.0)
        return [dh0, dh0, dw]

    dx_out, dhead, g["norm_pre_mix"] = _rowwise(
        "pre_mix_bwd", pre_mix_bwd, n_rows, T, [(dn1, D_MODEL, 0), (dh1, D_MODEL, 0)] + x_in,
        [w["norm_pre_mix"]], [(D_MODEL, F32, "prev", n_tok), (D_MODEL, F32, "first")], [D_MODEL])
    return jnp.sum(loss_cols), dx_out, dhead


_IN_SECTIONS = [((5152, 6176), P_Q), ((6176, 6432), P_K), ((6432, 6688), P_V), ((5120, 5152), P_DT),
                ((0, 2048), P_Z), ((6688, 8736), P_GATE), ((2048, 5120), P_XBC)]


IN_SHARD = N_IN // 4


def _shard_pieces(a, b):
    return [(j, max(a, j * IN_SHARD) - j * IN_SHARD, min(b, (j + 1) * IN_SHARD) - j * IN_SHARD)
            for j in range(4) if max(a, j * IN_SHARD) < min(b, (j + 1) * IN_SHARD)]


def _to_cat(w4):
    parts, at = [], 0
    for (a, b), off in _IN_SECTIONS:
        if off > at:
            parts.append(jnp.zeros((w4.shape[1], off - at), w4.dtype))
        parts += [w4[j, :, lo:hi] for j, lo, hi in _shard_pieces(a, b)]
        at = off + (b - a)
    return jnp.concatenate(parts, axis=1)


def _from_cat_t(g_cat_t):
    shards = [[] for _ in range(4)]
    for (a, b), off in sorted(_IN_SECTIONS):
        for j, lo, hi in _shard_pieces(a, b):
            start = off + j * IN_SHARD + lo - a
            shards[j].append(g_cat_t[start:start + hi - lo])
    return jnp.stack([jnp.concatenate(s, axis=0) for s in shards])


LANES = 1024
_BIG = [("w_in", 1024, 2184, "chip"), ("w_ssm_out", 512, 1024, "row"), ("w_attn_out", 256, 1024, "row"),
        ("w_mix_out", 256, 1024, "row"), ("w_ffn_up", 1024, 1408, "col"), ("w_ffn_down", 704, 1024, "row"),
        ("small", 32, LANES, "chip")]
_SMALL_SHARDED = [("ssm_conv_w", (4, 768), 1), ("ffn_conv_w", (3, 1408), 1), ("meta_tokens", (16, 256), 1)]
_REPLICATED = [("norm_pre_mix", 1024), ("ssm_conv_b", 3072), ("ssm_dt_bias", 32), ("ssm_a_log", 32),
               ("ssm_d_skip", 32), ("ssm_norm", 2048), ("attn_sinks", 16), ("norm_post_mix", 1024),
               ("norm_pre_ffn", 1024), ("ffn_conv_b", 5632), ("norm_post_ffn", 1024)]
SMALL_ROWS = 24


def _rep_rows():
    out, at = [], 0
    for _, width in _REPLICATED:
        out.append((at, -(-width // LANES)))
        at += out[-1][1]
    return out, at


def _in_rows(parts):
    rows = [jnp.pad(a, ((0, 0), (0, -a.shape[1] % LANES))).reshape(-1, LANES) for a in parts]
    flat = jnp.concatenate(rows, axis=0)
    return jnp.pad(flat, ((0, SMALL_ROWS - flat.shape[0]), (0, 0)))
WEIGHT_ORDER = ["meta_tokens", "norm_pre_mix", "w_in", "ssm_conv_w", "ssm_conv_b", "ssm_dt_bias", "ssm_a_log",
                "ssm_d_skip", "ssm_norm", "w_ssm_out", "attn_sinks", "w_attn_out", "w_mix_out", "norm_post_mix",
                "norm_pre_ffn", "w_ffn_up", "ffn_conv_w", "ffn_conv_b", "w_ffn_down", "norm_post_ffn"]


def _flatten(parts, rows):
    flat = jnp.concatenate([a.reshape(-1) for a in parts])
    return jnp.pad(flat, (0, rows * LANES - flat.shape[0])).reshape(rows, LANES)


def _unflatten(flat, shapes):
    flat = flat.reshape(-1)
    out, off = [], 0
    for shp in shapes:
        n = math.prod(shp)
        out.append(flat[off:off + n].reshape(shp))
        off += n
    return out


def _shard_of(full, chip, shape, axis):
    return lax.slice_in_dim(full, chip * shape[axis], (chip + 1) * shape[axis], axis=axis)


def _full_shape(r, c, layout):
    return {"row": (4 * r, c), "col": (r, 4 * c), "chip": (4, r, c), "chip_cols": (4, r, c)}[layout]


def _half_shape(r, c, layout):
    return (r, c // 2) if layout == "chip_cols" else (r // 2, c)


def _shard_view(ref, r, c, layout, chip):
    if layout == "row":
        return ref.at[pl.ds(pl.multiple_of(chip * r, 16), r), :]
    if layout == "col":
        return ref.at[:, pl.ds(pl.multiple_of(chip * c, 128), c)]
    return ref.at[chip]


def _half_view(ref, r, c, layout, chip, half):
    if layout == "chip_cols":
        return ref.at[chip, :, pl.ds(pl.multiple_of(half * (c // 2), 128), c // 2)]
    hr = r // 2
    if layout == "row":
        return ref.at[pl.ds(pl.multiple_of(chip * r + half * hr, 16), hr), :]
    r0 = pl.multiple_of(half * hr, 16)
    if layout == "col":
        return ref.at[pl.ds(r0, hr), pl.ds(pl.multiple_of(chip * c, 128), c)]
    return ref.at[chip, pl.ds(r0, hr), :]


def _mesh_pos():
    return lax.axis_index("x"), lax.axis_index("y"), lax.axis_index("c")


def _other_chips(x, y):
    return [(1 - x, y), (x, 1 - y), (1 - x, 1 - y)]


def _chip_index(x, y):
    return 2 * x + y


def _run_exchange(name, ex):
    n_in, n_out = len(ex.ins), len(ex.out_shapes)

    def body(*refs):
        in_refs, out_refs = refs[:n_in], refs[n_in:n_in + n_out]
        send_sems, recv_sems = refs[n_in + n_out:]
        copies = [pltpu.make_async_remote_copy(src_ref=s, dst_ref=d, send_sem=send_sems.at[i], recv_sem=recv_sems.at[i],
                                               device_id=dev, device_id_type=MESH)
                  for i, (s, d, dev) in enumerate(ex.make_copies(in_refs, out_refs))]
        assert len(copies) == ex.n_copies
        for cp in copies:
            cp.start()
        for cp in copies:
            cp.wait()

    return pl.pallas_call(
        body, name=name, in_specs=[ANY] * n_in, out_specs=[ANY] * n_out, out_shape=list(ex.out_shapes),
        scratch_shapes=[pltpu.SemaphoreType.DMA((ex.n_copies,)), pltpu.SemaphoreType.DMA((ex.n_copies,))],
        compiler_params=pltpu.CompilerParams(has_side_effects=True),
    )(*ex.ins)


def _join(*exs):
    def make(in_refs, out_refs):
        copies, i0, o0 = [], 0, 0
        for ex in exs:
            copies += ex.make_copies(in_refs[i0:i0 + len(ex.ins)], out_refs[o0:o0 + len(ex.out_shapes)])
            i0, o0 = i0 + len(ex.ins), o0 + len(ex.out_shapes)
        return copies

    aliases, i0, o0 = {}, 0, 0
    for ex in exs:
        aliases.update({i0 + k: o0 + v for k, v in ex.aliases.items()})
        i0, o0 = i0 + len(ex.ins), o0 + len(ex.out_shapes)
    return _Exchange([a for ex in exs for a in ex.ins], [s for ex in exs for s in ex.out_shapes], make,
                     sum(ex.n_copies for ex in exs), aliases)


def _split(exs, results):
    out, o0 = [], 0
    for ex in exs:
        out.append(list(results[o0:o0 + len(ex.out_shapes)]))
        o0 += len(ex.out_shapes)
    return out


def _gather_ici(entries, shards):
    def make(in_refs, out_refs):
        x, y, c = _mesh_pos()
        j = _chip_index(x, y)
        copies = []
        for ref_in, ref_out, (_, r, cc, lay) in zip(in_refs, out_refs, entries):
            copies.append((ref_in, _shard_view(ref_out, r, cc, lay, j), None))
            mine = ref_in.at[pl.ds(pl.multiple_of(c * (r // 2), 16), r // 2), :]
            copies += [(mine, _half_view(ref_out, r, cc, lay, j, c), (*ch, c)) for ch in _other_chips(x, y)]
        return copies

    shapes = [jax.ShapeDtypeStruct(_full_shape(r, cc, lay), s.dtype) for s, (_, r, cc, lay) in zip(shards, entries)]
    return _Exchange(list(shards), shapes, make, 4 * len(entries))


def _gather_pass_on(entries, fulls):
    def make(in_refs, out_refs):
        x, y, c = _mesh_pos()
        copies = []
        for ref, (_, r, cc, lay) in zip(out_refs, entries):
            for ch in _other_chips(x, y):
                landed = _half_view(ref, r, cc, lay, _chip_index(*ch), c)
                copies.append((landed, landed, (x, y, 1 - c)))
        return copies

    return _Exchange(list(fulls), [jax.ShapeDtypeStruct(f.shape, f.dtype) for f in fulls], make, 3 * len(entries),
                     {a: a for a in range(len(entries))})


def _gather_weights(entries, shards):
    n = len(entries)

    def body(*refs):
        ins, outs = refs[:n], refs[n:2 * n]
        send_sems, recv_sems, local_sems = refs[2 * n:]
        x, y, c = _mesh_pos()
        j = _chip_index(x, y)
        sibling = (x, y, 1 - c)
        chips = _other_chips(x, y)
        idx = [_chip_index(*ch) for ch in chips]

        def remote(k, src, dst, dev):
            return pltpu.make_async_remote_copy(src_ref=src, dst_ref=dst, send_sem=send_sems.at[k],
                                                recv_sem=recv_sems.at[k], device_id=dev, device_id_type=MESH)

        own = [pltpu.make_async_copy(ins[a], _shard_view(outs[a], r, cc, lay, j), local_sems.at[a])
               for a, (_, r, cc, lay) in enumerate(entries)]
        for cp in own:
            cp.start()
        first, passed = [], []
        for a, (_, r, cc, lay) in enumerate(entries):
            mine = ins[a].at[pl.ds(pl.multiple_of(c * (r // 2), 16), r // 2), :]
            for k, ch in enumerate(chips):
                first.append(remote(6 * a + k, mine, _half_view(outs[a], r, cc, lay, j, c), (*ch, c)))
                landed = _half_view(outs[a], r, cc, lay, idx[k], c)
                passed.append(remote(6 * a + 3 + k, landed, landed, sibling))
        for cp in first:
            cp.start()
        for a, (_, r, cc, lay) in enumerate(entries):
            for k in range(3):
                landed = _half_view(outs[a], r, cc, lay, idx[k], c)
                remote(6 * a + k, landed, landed, sibling).wait_recv()
                passed[3 * a + k].start()
        for a, (_, r, cc, lay) in enumerate(entries):
            for k in range(3):
                theirs = _half_view(outs[a], r, cc, lay, idx[k], 1 - c)
                remote(6 * a + 3 + k, theirs, theirs, sibling).wait_recv()
        for cp in first + passed:
            cp.wait_send()
        for cp in own:
            cp.wait()

    return pl.pallas_call(
        body, name="gather_weights", in_specs=[ANY] * n, out_specs=[ANY] * n,
        out_shape=[jax.ShapeDtypeStruct(_full_shape(r, cc, lay), s.dtype) for s, (_, r, cc, lay) in zip(shards, entries)],
        scratch_shapes=[pltpu.SemaphoreType.DMA((6 * n,)), pltpu.SemaphoreType.DMA((6 * n,)), pltpu.SemaphoreType.DMA((n,))],
        compiler_params=pltpu.CompilerParams(has_side_effects=True),
    )(*shards)


def _pair_exchange(entries, grads):
    def make(in_refs, out_refs):
        x, y, c = _mesh_pos()
        return [(_half_view(ref_in, r, cc, lay, i, 1 - c), ref_out.at[i], (x, y, 1 - c))
                for ref_in, ref_out, (_, r, cc, lay) in zip(in_refs, out_refs, entries) for i in range(4)]

    return _Exchange(list(grads), [jax.ShapeDtypeStruct((4,) + _half_shape(r, cc, lay), F32) for _, r, cc, lay in entries],
                     make, 4 * len(entries))


def _whole_to_sibling(arrays):
    def make(in_refs, out_refs):
        x, y, c = _mesh_pos()
        return [(r, o, (x, y, 1 - c)) for r, o in zip(in_refs, out_refs)]

    return _Exchange(list(arrays), [jax.ShapeDtypeStruct(a.shape, a.dtype) for a in arrays], make, len(arrays))


def _chip_exchange(psends):
    def make(in_refs, out_refs):
        x, y, c = _mesh_pos()
        return [(ref_in.at[_chip_index(*ch)], ref_out.at[k], (*ch, c))
                for ref_in, ref_out in zip(in_refs, out_refs) for k, ch in enumerate(_other_chips(x, y))]

    return _Exchange(list(psends), [jax.ShapeDtypeStruct((3,) + p.shape[1:], p.dtype) for p in psends], make,
                     3 * len(psends))


def _to_all_chips(array):
    def make(in_refs, out_refs):
        x, y, c = _mesh_pos()
        return [(in_refs[0], out_refs[0].at[k], (*ch, c)) for k, ch in enumerate(_other_chips(x, y))]

    return _Exchange([array], [jax.ShapeDtypeStruct((3,) + array.shape, array.dtype)], make, 3)


SUM_ROWS = 512
ADAM_ROWS = 256


def _pair_sum(name, grad, recv, ids, r, c, layout):
    hr, c = _half_shape(r, c, layout)
    tr = _row_tile(hr, SUM_ROWS)
    nb = hr // tr

    def body(ids_ref, g_ref, r_ref, send_ref, own_ref):
        s = g_ref[...] + r_ref[...]
        send_ref[...] = s.astype(send_ref.dtype)

        @pl.when(pl.program_id(1) == ids_ref[1])
        def _():
            own_ref[...] = s

    if layout == "row":
        g_spec = pl.BlockSpec((tr, c), lambda t, j, ids_ref: ((j * r + ids_ref[0] * hr) // tr + t, 0))
    elif layout == "col":
        g_spec = pl.BlockSpec((tr, c), lambda t, j, ids_ref: (ids_ref[0] * nb + t, j))
    elif layout == "chip_cols":
        g_spec = pl.BlockSpec((None, tr, c), lambda t, j, ids_ref: (j, t, ids_ref[0]))
    else:
        g_spec = pl.BlockSpec((None, tr, c), lambda t, j, ids_ref: (j, ids_ref[0] * nb + t, 0))
    grid_spec = pltpu.PrefetchScalarGridSpec(
        num_scalar_prefetch=1, grid=(nb, 4),
        in_specs=[g_spec, pl.BlockSpec((None, tr, c), lambda t, j, ids_ref: (j, t, 0))],
        out_specs=[pl.BlockSpec((None, tr, c), lambda t, j, ids_ref: (j, t, 0)),
                   pl.BlockSpec((tr, c), lambda t, j, ids_ref: (t, 0))])
    return pl.pallas_call(
        body, name=name, grid_spec=grid_spec,
        out_shape=[jax.ShapeDtypeStruct((4, hr, c), BF16), jax.ShapeDtypeStruct((hr, c), F32)],
        compiler_params=_cparams(2),
    )(ids, grad, recv)


def _chip_sum(name, own, recv):
    hr, c = own.shape
    tr = _row_tile(hr, SUM_ROWS)

    def body(o_ref, r_ref, out_ref):
        out_ref[...] = ((o_ref[...] + r_ref[0].astype(F32)) + r_ref[1].astype(F32)) + r_ref[2].astype(F32)

    return pl.pallas_call(
        body, name=name, grid=(hr // tr,),
        in_specs=[pl.BlockSpec((tr, c), lambda i: (i, 0)), pl.BlockSpec((3, tr, c), lambda i: (0, i, 0))],
        out_specs=pl.BlockSpec((tr, c), lambda i: (i, 0)),
        out_shape=jax.ShapeDtypeStruct((hr, c), F32), compiler_params=_cparams(1),
    )(own, recv)


def _chip_sum_small(own, recv, ids):
    def body(ids_ref, o_ref, r_ref, out_ref):
        j = ids_ref[1]
        total = None
        for i in range(4):
            m = jnp.bitwise_xor(i, j)
            term = jnp.where(m == 0, o_ref[...], jnp.where(m == 2, r_ref[0], jnp.where(m == 1, r_ref[1], r_ref[2])))
            total = term if total is None else total + term
        out_ref[...] = total

    grid_spec = pltpu.PrefetchScalarGridSpec(
        num_scalar_prefetch=1, grid=(1,),
        in_specs=[pl.BlockSpec(own.shape, lambda i, ids_ref: (0, 0)), pl.BlockSpec(recv.shape, lambda i, ids_ref: (0, 0, 0))],
        out_specs=pl.BlockSpec(own.shape, lambda i, ids_ref: (0, 0)))
    return pl.pallas_call(body, name="chip_sum_small", grid_spec=grid_spec,
                          out_shape=jax.ShapeDtypeStruct(own.shape, F32), compiler_params=_cparams(1))(ids, own, recv)


def _adamw(name, w, m, v, mine, theirs, ids):
    lead = (None,) * (w.ndim - 2)
    rows, cols = w.shape[-2:]
    half = rows // 2
    tr = _row_tile(half, ADAM_ROWS, unit=8)
    nb = half // tr
    c1 = 1.0 / (1.0 - ADAM_B1 ** ADAM_STEP)
    c2 = 1.0 / (1.0 - ADAM_B2 ** ADAM_STEP)

    def body(ids_ref, w_ref, m_ref, v_ref, mine_ref, theirs_ref, g_out, d_out, m_out, v_out):
        g = jnp.where(pl.program_id(0) == ids_ref[0], mine_ref[...], theirs_ref[...])
        m_new = ADAM_B1 * m_ref[...] + (1.0 - ADAM_B1) * g
        v_new = ADAM_B2 * v_ref[...] + (1.0 - ADAM_B2) * (g * g)
        d_out[...] = -ADAM_LR * ((m_new * c1) / (jnp.sqrt(v_new * c2) + ADAM_EPS) + ADAM_WD * w_ref[...])
        g_out[...] = g
        m_out[...] = m_new
        v_out[...] = v_new

    full = pl.BlockSpec(lead + (tr, cols), lambda h, i, ids_ref: (0,) * len(lead) + (h * nb + i, 0))
    part = pl.BlockSpec((tr, cols), lambda h, i, ids_ref: (i, 0))
    grid_spec = pltpu.PrefetchScalarGridSpec(num_scalar_prefetch=1, grid=(2, nb),
                                             in_specs=[full, full, full, part, part], out_specs=[full] * 4)
    return pl.pallas_call(
        body, name=name, grid_spec=grid_spec,
        out_shape=[jax.ShapeDtypeStruct(w.shape, F32)] * 4, compiler_params=_cparams(2),
    )(ids, w, m, v, mine, theirs)


def _adamw_whole(name, w, m, v, g):
    rows, cols = w.shape[-2:]
    tr = _row_tile(rows, 2 * ADAM_ROWS, unit=8)
    c1 = 1.0 / (1.0 - ADAM_B1 ** ADAM_STEP)
    c2 = 1.0 / (1.0 - ADAM_B2 ** ADAM_STEP)

    def body(w_ref, m_ref, v_ref, g_ref, g_out, d_out, m_out, v_out):
        g = g_ref[...]
        m_new = ADAM_B1 * m_ref[...] + (1.0 - ADAM_B1) * g
        v_new = ADAM_B2 * v_ref[...] + (1.0 - ADAM_B2) * (g * g)
        d_out[...] = -ADAM_LR * ((m_new * c1) / (jnp.sqrt(v_new * c2) + ADAM_EPS) + ADAM_WD * w_ref[...])
        g_out[...] = g
        m_out[...] = m_new
        v_out[...] = v_new

    full = pl.BlockSpec((None, tr, cols), lambda i: (0, i, 0))
    return pl.pallas_call(
        body, name=name, grid=(rows // tr,), in_specs=[full, full, full, pl.BlockSpec((tr, cols), lambda i: (i, 0))],
        out_specs=[full] * 4, out_shape=[jax.ShapeDtypeStruct(w.shape, F32)] * 4, compiler_params=_cparams(1),
    )(w, m, v, g)


def _adamw_replicated(g_rows, ws, ms, vs):
    n = len(ws)
    layout, _ = _rep_rows()
    c1 = 1.0 / (1.0 - ADAM_B1 ** ADAM_STEP)
    c2 = 1.0 / (1.0 - ADAM_B2 ** ADAM_STEP)

    def body(g_ref, *refs):
        w_refs, m_refs, v_refs = refs[0:n], refs[n:2 * n], refs[2 * n:3 * n]
        outs = refs[3 * n:]
        for k, (r0, rows) in enumerate(layout):
            width = w_refs[k].shape[1]
            g = jnp.concatenate([g_ref[r0 + j:r0 + j + 1, :] for j in range(rows)], axis=1)[:, 0:width]
            m_new = ADAM_B1 * m_refs[k][...] + (1.0 - ADAM_B1) * g
            v_new = ADAM_B2 * v_refs[k][...] + (1.0 - ADAM_B2) * (g * g)
            outs[k][...] = g
            outs[n + k][...] = -ADAM_LR * ((m_new * c1) / (jnp.sqrt(v_new * c2) + ADAM_EPS) + ADAM_WD * w_refs[k][...])
            outs[2 * n + k][...] = m_new
            outs[3 * n + k][...] = v_new

    res = pl.pallas_call(body, name="adamw_replicated",
                         out_shape=[jax.ShapeDtypeStruct(w.shape, F32) for _ in range(4) for w in ws])(g_rows, *ws, *ms, *vs)
    return [res[k * n:(k + 1) * n] for k in range(4)]


def _small_shard(parts):
    return _flatten(parts, _BIG[-1][1])


_ENTRY = {e[0]: e for e in _BIG}
_GRAD_ENTRY = {**_ENTRY, "w_in": ("w_in", IN_SHARD, D_MODEL, "chip_cols")}
FFN_MATS = ("w_ffn_down", "w_ffn_up")
MIXER_MATS = ("w_mix_out", "w_ssm_out", "w_attn_out")


class _StepPlan:
    def __init__(self, w, late_shards, shards, ids):
        self.w, self.g = w, {}
        self.late_shards, self.shards, self.ids = late_shards, shards, ids
        self.sums, self.halves, self.results = {}, {}, {}

    def run(self, name, fn, *args, **kw):
        at = getattr(self, "_at_" + name, None)
        if at is None:
            return fn(*args, **kw)
        exchange, landed = at()
        res, extra = fn(*args, bg=exchange, **kw)
        landed(extra)
        return res

    def _at_ssd_fwd(self):
        def landed(fulls):
            self.partly_gathered = fulls

        return _gather_ici([_ENTRY[n] for n in MIXER_MATS], [self.late_shards[n] for n in MIXER_MATS]), landed

    def _at_attn_fwd(self):
        stages = (_gather_pass_on([_ENTRY[n] for n in MIXER_MATS], self.partly_gathered),
                  _gather_ici([_ENTRY[n] for n in FFN_MATS], [self.late_shards[n] for n in FFN_MATS]))

        def landed(extra):
            mixer, self.partly_gathered = _split(stages, extra)
            self.w.update(zip(MIXER_MATS, mixer))

        return _join(*stages), landed

    def _at_ssm_gate_norm(self):
        return (_gather_pass_on([_ENTRY[n] for n in FFN_MATS], self.partly_gathered),
                lambda fulls: self.w.update(zip(FFN_MATS, fulls)))

    def pair_sums(self, names, grads, recv):
        for n, gr, rv in zip(names, grads, recv):
            _, r, c, lay = _GRAD_ENTRY[n]
            self.sums[n] = _pair_sum("pair_sum_" + n, gr, rv, self.ids, r, c, lay)

    def chip_sums(self, names, recv):
        for n, rv in zip(names, recv):
            self.halves[n] = _chip_sum("chip_sum_" + n, self.sums[n][1], rv)

    def adamw(self, names, theirs):
        for n, th in zip(names, theirs):
            sh = self.shards[n]
            if n == "w_in":
                mine_first = self.ids[0] == 0
                g_t = jnp.where(mine_first, jnp.concatenate([self.halves[n], th], axis=1),
                                jnp.concatenate([th, self.halves[n]], axis=1))
                res = _adamw_whole("adamw_" + n, *[jnp.swapaxes(sh[k], -1, -2) for k in ("w", "m", "v")], g_t)
                self.results[n] = [jnp.swapaxes(r, -1, -2) for r in res]
            else:
                self.results[n] = _adamw("adamw_" + n, sh["w"], sh["m"], sh["v"], self.halves[n], th, self.ids)

    def _pair_stage(self, names, grads):
        return (_pair_exchange([_GRAD_ENTRY[n] for n in names], grads),
                lambda recv: self.pair_sums(names, grads, recv))

    def _at_ffn_up_dx(self):
        return self._pair_stage(FFN_MATS, [self.g[n] for n in FFN_MATS])

    def _at_ssm_gate_norm_bwd(self):
        return self._pair_stage(MIXER_MATS, [self.g[n] for n in MIXER_MATS])

    def _at_attn_bwd(self):
        return _chip_exchange([self.sums[n][0] for n in FFN_MATS]), lambda recv: self.chip_sums(FFN_MATS, recv)

    def _at_ssd_bwd(self):
        stages = (_chip_exchange([self.sums[n][0] for n in MIXER_MATS]),
                  _whole_to_sibling([self.halves[n] for n in FFN_MATS]))

        def landed(extra):
            recv, theirs = _split(stages, extra)
            self.chip_sums(MIXER_MATS, recv)
            self.adamw(FFN_MATS, theirs)

        return _join(*stages), landed

    def _at_ssm_conv_bwd(self):
        return _whole_to_sibling([self.halves[n] for n in MIXER_MATS]), lambda theirs: self.adamw(MIXER_MATS, theirs)

    def _at_in_proj_dx(self):
        grads = [_from_cat_t(self.g.pop("w_cat_t"))]
        self.pair_sums(("w_in",), grads,
                       _run_exchange("grad_pair_exchange_w_in", _pair_exchange([_GRAD_ENTRY["w_in"]], grads)))
        return _chip_exchange([self.sums["w_in"][0]]), lambda recv: self.chip_sums(("w_in",), recv)

    def finish(self, g_small, g_rep, rep_shards):
        stages = (_pair_exchange([_ENTRY["small"]], [g_small]), _whole_to_sibling([g_rep]))
        recv_small, recv_rep = _split(stages, _run_exchange("grad_pair_exchange_tail", _join(*stages)))
        self.pair_sums(("small",), [g_small], recv_small)
        p_rep, = _rowwise("pair_sum_replicated", lambda r0, a, b: [a + b], SMALL_ROWS, SMALL_ROWS,
                          [(g_rep, LANES, 0), (recv_rep[0], LANES, 0)], [], [(LANES, F32)], [])
        stages = (_chip_exchange([self.sums["small"][0]]), _to_all_chips(p_rep))
        recv, recv_rep = _split(stages, _run_exchange("grad_chip_exchange_tail", _join(*stages)))
        self.chip_sums(("small",), recv)
        g_rep_tot = _chip_sum_small(p_rep, recv_rep[0], self.ids)
        last = ("w_in", "small")
        self.adamw(last, _run_exchange("grad_half_share_tail", _whole_to_sibling([self.halves[n] for n in last])))
        self.results["replicated"] = _adamw_replicated(g_rep_tot, rep_shards["w"], rep_shards["m"], rep_shards["v"])
        return g_rep_tot[_rep_rows()[1], 0]


def kernel(x, meta_tokens, norm_pre_mix, w_in, ssm_conv_w, ssm_conv_b, ssm_dt_bias, ssm_a_log, ssm_d_skip, ssm_norm, w_ssm_out, attn_sinks, w_attn_out, w_mix_out, norm_post_mix, norm_pre_ffn, w_ffn_up, ffn_conv_w, ffn_conv_b, w_ffn_down, norm_post_ffn, loss_target, m_meta_tokens, m_norm_pre_mix, m_w_in, m_ssm_conv_w, m_ssm_conv_b, m_ssm_dt_bias, m_ssm_a_log, m_ssm_d_skip, m_ssm_norm, m_w_ssm_out, m_attn_sinks, m_w_attn_out, m_w_mix_out, m_norm_post_mix, m_norm_pre_ffn, m_w_ffn_up, m_ffn_conv_w, m_ffn_conv_b, m_w_ffn_down, m_norm_post_ffn, v_meta_tokens, v_norm_pre_mix, v_w_in, v_ssm_conv_w, v_ssm_conv_b, v_ssm_dt_bias, v_ssm_a_log, v_ssm_d_skip, v_ssm_norm, v_w_ssm_out, v_attn_sinks, v_w_attn_out, v_w_mix_out, v_norm_post_mix, v_norm_pre_ffn, v_w_ffn_up, v_ffn_conv_w, v_ffn_conv_b, v_w_ffn_down, v_norm_post_ffn):
    args = dict(locals())
    squeeze = lambda a: a.reshape(a.shape[-2:])
    wts = {n: squeeze(args[n]) for n in WEIGHT_ORDER}
    mom = {n: squeeze(args["m_" + n]) for n in WEIGHT_ORDER}
    var = {n: squeeze(args["v_" + n]) for n in WEIGHT_ORDER}
    x_i, y_i, c_i = _mesh_pos()
    ids = jnp.stack([c_i, _chip_index(x_i, y_i)]).astype(jnp.int32)
    big_names = [n for n, _, _, _ in _BIG[:-1]]
    small_names = [n for n, _, _ in _SMALL_SHARDED]
    rep_names = [n for n, _ in _REPLICATED]

    stacks = {"w": wts, "m": mom, "v": var}
    shards = {n: {"w": args[n], "m": args["m_" + n], "v": args["v_" + n]} for n in big_names}
    shards["small"] = {k: _small_shard([d[n] for n in small_names]) for k, d in stacks.items()}
    rep_shards = {k: [d[n] for n in rep_names] for k, d in stacks.items()}

    w_in4, small_all = _gather_weights([_ENTRY["w_in"], _ENTRY["small"]], [wts["w_in"].astype(BF16), shards["small"]["w"]])
    w = {n: wts[n] for n in rep_names}
    w["w_cat"] = _to_cat(w_in4)
    small_parts = [_unflatten(small_all[i], [shp for _, shp, _ in _SMALL_SHARDED]) for i in range(4)]
    for k, (n, _, axis) in enumerate(_SMALL_SHARDED):
        w[n] = jnp.concatenate([small_parts[i][k] for i in range(4)], axis=axis)
    plan = _StepPlan(w, {n: wts[n].astype(BF16) for n in MIXER_MATS + FFN_MATS}, shards, ids)

    head = jnp.concatenate([jnp.zeros((PAD, D_MODEL), F32), w["meta_tokens"]], axis=0)
    loss_sum, dx, dhead = _local_step(x[0], head, loss_target[0], plan)
    g = plan.g
    g["meta_tokens"] = dhead[PAD:]
    g_small = jnp.stack([_small_shard([_shard_of(g[n], i, shp, ax) for n, shp, ax in _SMALL_SHARDED]) for i in range(4)])
    loss_part = (loss_sum * (0.5 / D_MODEL)).reshape(1, 1)
    loss = plan.finish(g_small, _in_rows([g[n] for n in rep_names] + [loss_part]), rep_shards)

    results = {}
    for kind in range(4):
        results.update({(kind, n): plan.results[n][kind] for n in big_names})
        parts = _unflatten(plan.results["small"][kind], [shp for _, shp, _ in _SMALL_SHARDED])
        results.update({(kind, n): parts[k] for k, n in enumerate(small_names)})
        results.update({(kind, n): plan.results["replicated"][kind][k] for k, n in enumerate(rep_names)})
    outs = [results[kind, n].reshape(args[n].shape) for kind in range(4) for n in WEIGHT_ORDER]
    return (loss, dx[None], *outs)
```

```python
import math
from typing import Any, Callable, NamedTuple, Sequence

import jax
import jax.numpy as jnp
from jax import lax
from jax.experimental import pallas as pl
from jax.experimental.pallas import tpu as pltpu

F32 = jnp.float32
BF16 = jnp.bfloat16

D_MODEL = 1024
N_META = 16
T = 128
PAD = T - N_META
D_INNER = 2048
SSM_HEADS = 32
HEAD_P = 64
SSM_GROUPS = 4
GROUP_W = D_INNER // SSM_GROUPS
D_STATE = 128
CONV_DIM = D_INNER + 2 * SSM_GROUPS * D_STATE
ATTN_HEADS = 16
KV_HEADS = 4
ATTN_W = 1024
KV_W = 256
FFN_DIM = 2816
N_IN = 8736
EPS = 1e-6
NEG = -1e30
SCALE = 0.125

P_Q, P_K, P_V, P_DT, P_Z, P_GATE, P_XBC = 0, 1024, 1280, 1536, 2048, 4096, 6144
QKV_W = 1536
P_W = 9216

ADAM_LR, ADAM_B1, ADAM_B2, ADAM_EPS, ADAM_WD, ADAM_STEP = 0.001, 0.9, 0.999, 1e-08, 0.01, 10

VMEM_BUDGET = 40 * 1024 * 1024
VMEM_LIMIT = 56 * 1024 * 1024
MESH = pl.DeviceIdType.MESH
ANY = pl.BlockSpec(memory_space=pl.ANY)


def _cparams(n_axes, **kw):
    return pltpu.CompilerParams(dimension_semantics=("arbitrary",) * n_axes, vmem_limit_bytes=VMEM_LIMIT, **kw)


class _Exchange(NamedTuple):
    ins: Sequence[Any]
    out_shapes: Sequence[Any]
    make_copies: Callable
    n_copies: int
    aliases: dict = {}


def _call(body, name, grid, in_specs, out_specs, out_shape, operands, scratch_shapes=(), aliases=None, bg=None):
    aliases = dict(aliases or {})
    if bg is None:
        return pl.pallas_call(body, name=name, grid=grid, in_specs=in_specs, out_specs=out_specs, out_shape=out_shape,
                              scratch_shapes=list(scratch_shapes), input_output_aliases=aliases,
                              compiler_params=_cparams(len(grid)))(*operands)
    n_in, n_out, n_scr = len(in_specs), len(out_specs), len(scratch_shapes)
    nb_in, nb_out = len(bg.ins), len(bg.out_shapes)

    def hosted(*refs):
        ins, bg_ins = refs[:n_in], refs[n_in:n_in + nb_in]
        outs = refs[n_in + nb_in:n_in + nb_in + n_out]
        bg_outs = refs[n_in + nb_in + n_out:n_in + nb_in + n_out + nb_out]
        scratch = refs[n_in + nb_in + n_out + nb_out:n_in + nb_in + n_out + nb_out + n_scr]
        send_sems, recv_sems = refs[-2:]
        pids = [pl.program_id(a) for a in range(len(grid))]
        first, last = pids[0] == 0, pids[0] == grid[0] - 1
        for p, g in zip(pids[1:], grid[1:]):
            first, last = first & (p == 0), last & (p == g - 1)
        copies = []
        for k, (src, dst, peer) in enumerate(bg.make_copies(bg_ins, bg_outs)):
            if peer is None:
                copies.append(pltpu.make_async_copy(src, dst, send_sems.at[k]))
            else:
                copies.append(pltpu.make_async_remote_copy(src_ref=src, dst_ref=dst, send_sem=send_sems.at[k],
                                                           recv_sem=recv_sems.at[k], device_id=peer, device_id_type=MESH))
        assert len(copies) == bg.n_copies

        @pl.when(first)
        def _():
            for cp in copies:
                cp.start()

        body(*ins, *outs, *scratch)

        @pl.when(last)
        def _():
            for cp in copies:
                cp.wait()

    aliases = {(k if k < n_in else k + nb_in): v for k, v in aliases.items()}
    aliases.update({n_in + k: n_out + v for k, v in bg.aliases.items()})
    res = pl.pallas_call(
        hosted, name=name, grid=grid, in_specs=list(in_specs) + [ANY] * nb_in, out_specs=list(out_specs) + [ANY] * nb_out,
        out_shape=list(out_shape) + list(bg.out_shapes), input_output_aliases=aliases,
        scratch_shapes=list(scratch_shapes) + [pltpu.SemaphoreType.DMA((bg.n_copies,))] * 2,
        compiler_params=_cparams(len(grid), has_side_effects=True))(*operands, *bg.ins)
    return res[:n_out], res[n_out:]


def _sigmoid(x):
    return 1.0 / (1.0 + jnp.exp(-x))


def _silu(x):
    return x * _sigmoid(x)


def _silu_grad(x):
    s = _sigmoid(x)
    return x * s, s * (1.0 + x * (1.0 - s))


def _dsilu(x):
    return _silu_grad(x)[1]


def _softplus(x):
    e = jnp.exp(-jnp.abs(x))
    small = e * (1.0 - e * (0.5 - e * (1.0 / 3.0)))
    return jnp.maximum(x, 0.0) + jnp.where(e < 0.01, small, jnp.log(1.0 + e))


def _rms(x, w):
    r = lax.rsqrt(jnp.mean(x * x, axis=-1, keepdims=True) + EPS)
    return x * r * w


def _rms_bwd(dy, x, w):
    r = lax.rsqrt(jnp.mean(x * x, axis=-1, keepdims=True) + EPS)
    xh = x * r
    g = dy * w
    dx = r * (g - xh * jnp.mean(g * xh, axis=-1, keepdims=True))
    dw = jnp.sum(dy * xh, axis=0, keepdims=True)
    return dx, dw


def _dot(a, b):
    return jnp.dot(a, b, preferred_element_type=F32)


def _dot_nt(a, b):
    return lax.dot_general(a, b, (((1,), (1,)), ((), ())), preferred_element_type=F32)


def _dot_tn(a, b):
    return lax.dot_general(a, b, (((0,), (0,)), ((), ())), preferred_element_type=F32)


def _split3(x):
    hi = x.astype(BF16)
    r = x - hi.astype(F32)
    mid = r.astype(BF16)
    lo = (r - mid.astype(F32)).astype(BF16)
    return hi, mid, lo


def _xdot(x, e):
    hi, mid, lo = _split3(x)
    return _dot(hi, e) + _dot(mid, e) + _dot(lo, e)


def _xdot_l(e, x):
    hi, mid, lo = _split3(x)
    return _dot(e, hi) + _dot(e, mid) + _dot(e, lo)


def _iota(shape, dim):
    return lax.broadcasted_iota(jnp.int32, shape, dim)


def _divisors(n, unit):
    return [t for t in range(unit, n + 1, unit) if n % t == 0]


MIN_MATMUL_STEPS = 8
SMALL_MATMUL = 2 ** 33


def _matmul_tiles(m, n, k, a_bytes, b_bytes, o_bytes, m_unit):
    best = None
    for tm in _divisors(m, m_unit):
        for tn in _divisors(n, 128):
            for tk in _divisors(k, 128):
                acc = 0 if tk == k else tm * tn * 4
                vm = 2 * (tm * tk * a_bytes + tk * tn * b_bytes + tm * tn * o_bytes) + acc
                if vm > VMEM_BUDGET:
                    continue
                steps = (m // tm) * (n // tn) * (k // tk)
                want = MIN_MATMUL_STEPS if m * n * k >= SMALL_MATMUL else 2
                score = (tk == k, min(steps, want), min(tm, 256), tm * tn * tk)
                if best is None or score > best[0]:
                    best = (score, (tm, tn, tk))
    return best[1]


def _matmul(name, a, b, mode, out_dtype, bg=None):
    if mode == "nn":
        (m, k), n = a.shape, b.shape[1]
    elif mode == "nt":
        (m, k), n = a.shape, b.shape[0]
    else:
        (k, m), n = a.shape, b.shape[1]
    ab, bb, ob = a.dtype.itemsize, b.dtype.itemsize, jnp.dtype(out_dtype).itemsize
    tm, tn, tk = _matmul_tiles(m, n, k, ab, bb, ob, 128 if mode == "tn" else 16)
    nk = k // tk
    dot = {"nn": _dot, "nt": _dot_nt, "tn": _dot_tn}[mode]

    def body(a_ref, b_ref, o_ref, *scratch):
        prod = dot(a_ref[...].astype(BF16), b_ref[...].astype(BF16))
        if nk == 1:
            o_ref[...] = prod.astype(o_ref.dtype)
        else:
            acc_ref, = scratch
            kk = pl.program_id(2)

            @pl.when(kk == 0)
            def _():
                acc_ref[...] = prod

            @pl.when(kk > 0)
            def _():
                acc_ref[...] += prod

            @pl.when(kk == nk - 1)
            def _():
                o_ref[...] = acc_ref[...].astype(o_ref.dtype)

    a_spec = pl.BlockSpec((tk, tm), lambda i, j, kk: (kk, i)) if mode == "tn" else pl.BlockSpec((tm, tk), lambda i, j, kk: (i, kk))
    b_spec = pl.BlockSpec((tn, tk), lambda i, j, kk: (j, kk)) if mode == "nt" else pl.BlockSpec((tk, tn), lambda i, j, kk: (kk, j))
    res = _call(body, name, (m // tm, n // tn, nk), [a_spec, b_spec], [pl.BlockSpec((tm, tn), lambda i, j, kk: (i, j))],
                [jax.ShapeDtypeStruct((m, n), out_dtype)], [a, b],
                scratch_shapes=[] if nk == 1 else [pltpu.VMEM((tm, tn), F32)], bg=bg)
    return res[0] if bg is None else (res[0][0], res[1])


def _row_tile(n_rows, cap, unit=16):
    return max([t for t in _divisors(n_rows, unit) if t <= cap], default=n_rows)


ROW_SUB = 384
GROUP_UNROLL = 4


def _rowwise(name, fn, n_rows, tm, row_ins, full_ins, row_outs, acc_outs, bg=None):
    n_in = len(row_ins) + len(full_ins)
    n_ro = len(row_outs)
    into = [(k, o[3]) for k, o in enumerate(row_outs) if len(o) > 2 and o[2] == "into"]

    sub = min(tm, ROW_SUB)
    counts = [tm // T if len(e) > 3 and e[3] == "prev" else 1 for e in row_ins]
    assert all(cnt == 1 for cnt in counts) or sub == tm
    starts = [sum(counts[:k]) for k in range(len(counts))]
    n_row_in = sum(counts)
    n_in = n_row_in + len(full_ins)

    def body(*refs):
        i = pl.program_id(0)
        outs = refs[n_in + len(into):]

        sums = tuple(jnp.zeros((1, w), F32) for w in acc_outs)
        for s in range(tm // sub):
            rows = pl.ds(s * sub, sub)
            vals = [refs[st][rows, :] if cnt == 1 else jnp.concatenate([refs[st + k][...] for k in range(cnt)], axis=0)
                    for st, cnt in zip(starts, counts)]
            vals += [r[...] for r in refs[n_row_in:n_in]]
            res = fn(i * tm + s * sub, *vals)
            for o, r, v in zip(row_outs, outs[:n_ro], res[:n_ro]):
                if len(o) > 2 and o[2] == "first":
                    @pl.when(i == 0)
                    def _(r=r, v=v, rows=rows):
                        r[rows, :] = v.astype(r.dtype)
                else:
                    r[rows, :] = v.astype(r.dtype)
            sums = tuple(a + v for a, v in zip(sums, res[n_ro:]))

        @pl.when(i == 0)
        def _():
            for r, v in zip(outs[n_ro:], sums):
                r[...] = v

        @pl.when(i > 0)
        def _():
            for r, v in zip(outs[n_ro:], sums):
                r[...] += v

    def in_spec(entry, cnt):
        w, cb = entry[1], entry[2]
        if len(entry) > 3 and entry[3] == "prev":
            return [pl.BlockSpec((tm // cnt, w), lambda i, k=k: (jnp.maximum(cnt * i - 1 + k, 0), cb)) for k in range(cnt)]
        if len(entry) > 3 and entry[3] == "first":
            return [pl.BlockSpec((tm, w), lambda i: (0, cb))]
        return [pl.BlockSpec((tm, w), lambda i: (i, cb))]

    def out_spec(o):
        if len(o) == 2:
            return pl.BlockSpec((tm, o[0]), lambda i: (i, 0)), jax.ShapeDtypeStruct((n_rows, o[0]), o[1])
        if o[2] == "new":
            return pl.BlockSpec((tm, o[0]), lambda i: (i, o[4])), jax.ShapeDtypeStruct((n_rows, o[3]), o[1])
        if o[2] == "into":
            return pl.BlockSpec((tm, o[0]), lambda i: (i, o[4])), jax.ShapeDtypeStruct(o[3].shape, o[3].dtype)
        if o[2] == "first":
            return pl.BlockSpec((tm, o[0]), lambda i: (0, 0)), jax.ShapeDtypeStruct((tm, o[0]), o[1])
        return pl.BlockSpec((tm, o[0]), lambda i: (jnp.maximum(i - 1, 0), 0)), jax.ShapeDtypeStruct((o[3], o[0]), o[1])

    in_specs = [s for e, cnt in zip(row_ins, counts) for s in in_spec(e, cnt)]
    in_specs += [pl.BlockSpec(a.shape, lambda i: (0, 0)) for a in full_ins]
    in_specs += [pl.BlockSpec(memory_space=pl.ANY) for _ in into]
    specs_shapes = [out_spec(o) for o in row_outs]
    out_specs = [s for s, _ in specs_shapes] + [pl.BlockSpec((1, w), lambda i: (0, 0)) for w in acc_outs]
    out_shape = [s for _, s in specs_shapes] + [jax.ShapeDtypeStruct((1, w), F32) for w in acc_outs]
    return _call(body, name, (n_rows // tm,), in_specs, out_specs, out_shape,
                 [e[0] for e, cnt in zip(row_ins, counts) for _ in range(cnt)] + list(full_ins) + [arr for _, arr in into],
                 aliases={n_in + a: k for a, (k, _) in enumerate(into)}, bg=bg)


def _valid_rows(first_row, tm, lo):
    return (first_row + _iota((tm, 1), 0)) >= lo


CONV_ROWS = 384
CONV_SUB = 16
CONV_LANES = 256


def _conv_specs(tm, width, blk, n_rows, after):
    specs = [pl.BlockSpec((tm, width), lambda i: (i, blk)),
             pl.BlockSpec((8, width), lambda i: (jnp.maximum(i * (tm // 8) - 1, 0), blk))]
    if after:
        specs.append(pl.BlockSpec((16, width), lambda i: (jnp.minimum((i + 1) * (tm // 16), n_rows // 16 - 1), blk)))
    return specs


def _conv_window(win, w_ref, b_ref, taps, c0, cw, n):
    acc = b_ref[:, c0:c0 + cw] + w_ref[taps - 1:taps, c0:c0 + cw] * win[8:8 + n]
    for k in range(taps - 1):
        acc = acc + w_ref[k:k + 1, c0:c0 + cw] * win[8 - (taps - 1) + k:8 - (taps - 1) + k + n]
    return acc


def _ffn_act(name, u_raw, conv_w, conv_b, n_rows):
    tm, sub, cw = CONV_ROWS, CONV_SUB, CONV_LANES
    taps, width = conv_w.shape
    half = width // 2

    def body(cur_ref, prev_ref, w_ref, b_ref, f_ref, ext_ref):
        i = pl.program_id(0)
        ext_ref[0:8, :] = jnp.where(i > 0, prev_ref[...], 0.0)
        ext_ref[8:8 + tm, :] = cur_ref[...]
        for q in range(half // cw):
            a0, g0 = q * cw, half + q * cw

            def group(s, carry):
                r = pl.multiple_of(s * sub, sub)
                a = _conv_window(ext_ref[pl.ds(r, sub + 8), a0:a0 + cw], w_ref, b_ref, taps, a0, cw, sub)
                g = _conv_window(ext_ref[pl.ds(r, sub + 8), g0:g0 + cw], w_ref, b_ref, taps, g0, cw, sub)
                f_ref[pl.ds(r, sub), a0:a0 + cw] = (_silu(a) * g).astype(f_ref.dtype)
                return carry

            lax.fori_loop(0, tm // sub, group, 0, unroll=GROUP_UNROLL)

        @pl.when(i == 0)
        def _():
            f_ref[0:PAD, :] = jnp.zeros((PAD, half), f_ref.dtype)

    return pl.pallas_call(
        body, name=name, grid=(n_rows // tm,),
        in_specs=_conv_specs(tm, width, 0, n_rows, False) + [pl.BlockSpec((taps, width), lambda i: (0, 0)),
                                                             pl.BlockSpec((1, width), lambda i: (0, 0))],
        out_specs=pl.BlockSpec((tm, half), lambda i: (i, 0)),
        out_shape=jax.ShapeDtypeStruct((n_rows, half), BF16),
        scratch_shapes=[pltpu.VMEM((tm + 8, width), F32)],
        compiler_params=_cparams(1),
    )(u_raw, u_raw, conv_w, conv_b)


def _conv_bwd(name, raw, raw_blk, dsrcs, chunk_src, conv_w, conv_b, n_rows, gated, into=None, into_blk=0, bg=None):
    taps, width = conv_w.shape
    half = width // 2 if gated else width
    tm, sub, cw = CONV_ROWS, CONV_SUB, CONV_LANES
    te = tm + 16
    nd = len(dsrcs)
    n_parts = 2 if gated else 1

    def body(*refs):
        cur_ref, prev_ref, next_ref = refs[0:3]
        dcur, dnext = refs[3:3 + nd], refs[3 + nd:3 + 2 * nd]
        w_ref, b_ref = refs[3 + 2 * nd:5 + 2 * nd]
        out_ref, acc_ref, ext_ref, du_ref = refs[-4:]
        i = pl.program_id(0)
        ext_ref[0:8, :] = jnp.where(i > 0, prev_ref[...], 0.0)
        ext_ref[8:8 + tm, :] = cur_ref[...]
        ext_ref[8 + tm:24 + tm, :] = next_ref[...]

        for q, (src, off) in enumerate(chunk_src):
            cols = [q * cw, half + q * cw][:n_parts]

            def conv_grad(r, d, past_end):
                pre = [_conv_window(ext_ref[pl.ds(r, sub + 8), c0:c0 + cw], w_ref, b_ref, taps, c0, cw, sub) for c0 in cols]
                if gated:
                    act, dact = _silu_grad(pre[0])
                    dus = [d * pre[1] * dact, d * act]
                else:
                    dus = [d * _dsilu(pre[0])]
                for part, du in enumerate(dus):
                    if past_end:
                        du = jnp.where(i * tm + r + _iota((sub, 1), 0) < n_rows, du, 0.0)
                    du_ref[part, pl.ds(r, sub), :] = du

            def tile_rows(s, carry):
                r = pl.multiple_of(s * sub, sub)
                conv_grad(r, dcur[src][pl.ds(r, sub), off:off + cw].astype(F32), False)
                return carry

            lax.fori_loop(0, tm // sub, tile_rows, 0, unroll=GROUP_UNROLL)
            conv_grad(tm, dnext[src][:, off:off + cw].astype(F32), True)

            @pl.when(i == 0)
            def _():
                du_ref[:, 0:PAD, :] = jnp.zeros((n_parts, PAD, cw), F32)

            for part, c0 in enumerate(cols):
                taps_w = [w_ref[k:k + 1, c0:c0 + cw] for k in range(taps)]

                def back(s, sums):
                    new = list(sums)
                    for u in range(2):
                        r = pl.multiple_of((2 * s + u) * sub, sub)
                        win = du_ref[part, pl.ds(r, sub + 8), :]
                        raw_rows = ext_ref[pl.ds(8 + r, sub), c0:c0 + cw]
                        draw = jnp.zeros((sub, cw), F32)
                        for k in range(taps):
                            shifted = win[taps - 1 - k:taps - 1 - k + sub]
                            draw = draw + taps_w[k] * shifted
                            new[k] = new[k] + shifted * raw_rows
                        new[taps] = new[taps] + win[0:sub]
                        out_ref[pl.ds(r, sub), c0:c0 + cw] = draw.astype(out_ref.dtype)
                    return tuple(new)

                sums = lax.fori_loop(0, tm // (2 * sub), back, tuple(jnp.zeros((sub, cw), F32) for _ in range(taps + 1)))

                @pl.when(i == 0)
                def _(c0=c0):
                    out_ref[PAD - sub:PAD, c0:c0 + cw] = jnp.zeros((sub, cw), out_ref.dtype)

                for k in range(taps + 1):
                    total = jnp.sum(sums[k], axis=0, keepdims=True)
                    acc_ref[k:k + 1, c0:c0 + cw] = jnp.where(i == 0, total, acc_ref[k:k + 1, c0:c0 + cw] + total)

    in_specs = _conv_specs(tm, width, raw_blk, n_rows, True)
    in_specs += [pl.BlockSpec((tm, d.shape[1]), lambda i: (i, 0)) for d in dsrcs]
    in_specs += [pl.BlockSpec((16, d.shape[1]), lambda i: (jnp.minimum((i + 1) * (tm // 16), n_rows // 16 - 1), 0)) for d in dsrcs]
    in_specs += [pl.BlockSpec((taps, width), lambda i: (0, 0)), pl.BlockSpec((1, width), lambda i: (0, 0))]
    operands = [raw, raw, raw] + list(dsrcs) + list(dsrcs) + [conv_w, conv_b]
    aliases = {}
    if into is None:
        out0 = jax.ShapeDtypeStruct((n_rows, width), BF16)
    else:
        in_specs.append(pl.BlockSpec(memory_space=pl.ANY))
        operands.append(into)
        aliases = {len(operands) - 1: 0}
        out0 = jax.ShapeDtypeStruct(into.shape, into.dtype)
    return _call(body, name, (n_rows // tm,), in_specs,
                 [pl.BlockSpec((tm, width), lambda i: (i, into_blk)), pl.BlockSpec((8, width), lambda i: (0, 0))],
                 [out0, jax.ShapeDtypeStruct((8, width), F32)], operands,
                 scratch_shapes=[pltpu.VMEM((tm + 24, width), F32), pltpu.VMEM((n_parts, te + 8, cw), F32)],
                 aliases=aliases, bg=bg)


def _ssd_specs(n_chunks, rev, per_step=1):
    cidx = (lambda c: n_chunks - 1 - c) if rev else (lambda c: c)
    xw, nw = per_step * GROUP_W, per_step * D_STATE
    xg0, bg0, cg0 = P_XBC // xw, (P_XBC + D_INNER) // nw, (P_XBC + D_INNER + SSM_GROUPS * D_STATE) // nw

    def cur(width, blk0):
        return pl.BlockSpec((T, width), lambda g, c: (cidx(c), blk0 + g))

    def prev(width, blk0):
        return pl.BlockSpec((8, width), lambda g, c: (jnp.maximum(cidx(c) * (T // 8) - 1, 0), blk0 + g))

    specs = [cur(xw, xg0), prev(xw, xg0), cur(nw, bg0), prev(nw, bg0), cur(nw, cg0), prev(nw, cg0),
             pl.BlockSpec((T, 128), lambda g, c: (cidx(c), P_DT // 128))]
    wb, wc = D_INNER // nw, (D_INNER + SSM_GROUPS * D_STATE) // nw
    specs += [pl.BlockSpec((4, xw), lambda g, c: (0, g)),
              pl.BlockSpec((4, nw), lambda g, c: (0, wb + g)),
              pl.BlockSpec((4, nw), lambda g, c: (0, wc + g)),
              pl.BlockSpec((1, xw), lambda g, c: (0, g)),
              pl.BlockSpec((1, nw), lambda g, c: (0, wb + g)),
              pl.BlockSpec((1, nw), lambda g, c: (0, wc + g))]
    specs += [pl.BlockSpec((1, 128), lambda g, c: (0, 0))] * 3
    return specs, cidx


def _ssd_shared(refs, c):
    dt_ref, dtb_ref, alog_ref = refs[6], refs[13], refs[14]
    valid = _valid_rows(c * T, T, PAD)
    dtr = dt_ref[...] + dtb_ref[...]
    dt = jnp.where(valid, _softplus(dtr), 0.0)
    a_neg = -jnp.exp(alog_ref[...])
    tril = _iota((T, T), 0) >= _iota((T, T), 1)
    cs = _xdot_l(tril.astype(BF16), dt * a_neg)
    return dict(valid=valid, dtr=dtr, dt=dt, a_neg=a_neg, tril=tril, cs=cs, cs_t=cs.T)


def _heads_of_lanes():
    hh_t, ll_t = _iota((D_INNER, 128), 1), _iota((D_INNER, 128), 0)
    return (hh_t == jnp.right_shift(ll_t, 6)).astype(BF16)


def _ssd_chunk_forward(refs, ext_ref, g, c, shared):
    (xc_ref, xp_ref, bc_ref, bp_ref, cc_ref, cp_ref, dt_ref, wx_ref, wb_ref, wc_ref,
     bx_ref, bb_ref, bcb_ref, dtb_ref, alog_ref, dsk_ref) = refs

    def conv_pre(cur_ref, prev_ref, w_ref, b_ref, width):
        ext_ref[0:8, 0:width] = jnp.where(c > 0, prev_ref[...], 0.0)
        ext_ref[8:8 + T, 0:width] = cur_ref[...]
        w = w_ref[...]
        acc = b_ref[...] + w[3:4] * cur_ref[...]
        for k in range(3):
            acc = acc + w[k:k + 1] * ext_ref[pl.ds(5 + k, T), 0:width]
        return acc

    v = dict(shared)
    valid = v["valid"]
    v["head0"] = 8 * g
    v["x_pre"] = conv_pre(xc_ref, xp_ref, wx_ref, bx_ref, GROUP_W)
    v["b_pre"] = conv_pre(bc_ref, bp_ref, wb_ref, bb_ref, D_STATE)
    v["c_pre"] = conv_pre(cc_ref, cp_ref, wc_ref, bcb_ref, D_STATE)
    xs = _silu(v["x_pre"])
    bm = jnp.where(valid, _silu(v["b_pre"]), 0.0)
    cm = jnp.where(valid, _silu(v["c_pre"]), 0.0)
    hh, ll = _iota((128, GROUP_W), 0), _iota((128, GROUP_W), 1)
    expand = (hh == 8 * g + jnp.right_shift(ll, 6)).astype(BF16)
    cs_e = _xdot(v["cs"], expand)
    dt_e = _xdot(v["dt"], expand)
    cs_last_e = cs_e[T - 1:T, :]
    v.update(xs=xs, bm=bm, cm=cm, cs_e=cs_e, dt_e=dt_e, cs_last_e=cs_last_e)
    v["xdt"] = xs * dt_e
    v["decay_e"] = jnp.exp(cs_last_e - cs_e)
    v["ecs_e"] = jnp.exp(cs_e)
    v["elast_e"] = jnp.exp(cs_last_e)
    v["d_e"] = _xdot(dsk_ref[...], expand)
    v["gmat"] = _dot_nt(cm.astype(BF16), bm.astype(BF16))
    return v


def _ssd_decay_pair(v, jp):
    out = []
    for j in (v["head0"] + 2 * jp, v["head0"] + 2 * jp + 1):
        diff = v["cs"][:, j:j + 1] - v["cs_t"][j:j + 1, :]
        out.append(jnp.where(v["tril"], jnp.exp(jnp.where(v["tril"], diff, 0.0)), 0.0))
    return out


def _block_diag_pair(xp):
    lane = _iota(xp.shape, 1)
    return jnp.concatenate([jnp.where(lane < HEAD_P, xp, 0.0), jnp.where(lane >= HEAD_P, xp, 0.0)], axis=0)


SSD_GROUPS_PER_STEP = 4


def _ssd_group_refs(refs, gg):
    x_w, n_w = pl.ds(GROUP_W * gg, GROUP_W), pl.ds(D_STATE * gg, D_STATE)
    lanes = [x_w, x_w, n_w, n_w, n_w, n_w, None, x_w, n_w, n_w, x_w, n_w, n_w, None, None, None]
    return [r if w is None else r.at[:, w] for r, w in zip(refs, lanes)]


def _ssd_fwd(p, conv_w, conv_b, dt_bias, a_log, d_skip, n_chunks, bg=None):
    n_rows = n_chunks * T
    in_specs, _ = _ssd_specs(n_chunks, rev=False, per_step=SSD_GROUPS_PER_STEP)
    per = SSD_GROUPS_PER_STEP
    assert per == SSM_GROUPS

    def body(*refs):
        y_ref, hin_ref, st_ref, ext_ref = refs[16:]
        c = pl.program_id(1)

        @pl.when(c == 0)
        def _():
            st_ref[...] = jnp.zeros_like(st_ref)

        shared = _ssd_shared(refs[:16], c)
        for gg in range(per):
            v = _ssd_chunk_forward(_ssd_group_refs(refs[:16], gg), ext_ref.at[gg], gg, c, shared)
            state = st_ref[gg]
            hin_ref[gg] = state
            ys = []
            for jp in range(4):
                l0, l1 = _ssd_decay_pair(v, jp)
                lhs = jnp.concatenate([v["gmat"] * l0, v["gmat"] * l1], axis=1).astype(BF16)
                rhs = _block_diag_pair(v["xdt"][:, 128 * jp:128 * jp + 128]).astype(BF16)
                ys.append(_dot(lhs, rhs))
            y = jnp.concatenate(ys, axis=1)
            y = y + _dot(v["cm"].astype(BF16), state.astype(BF16)) * v["ecs_e"] + v["xs"] * v["d_e"]
            y_ref[:, GROUP_W * gg:GROUP_W * gg + GROUP_W] = y
            s_new = _dot_tn(v["bm"].astype(BF16), (v["xdt"] * v["decay_e"]).astype(BF16))
            st_ref[gg] = state * v["elast_e"] + s_new

    return _call(
        body, "ssd_fwd", (SSM_GROUPS // per, n_chunks), in_specs,
        [pl.BlockSpec((T, per * GROUP_W), lambda g, c: (c, g)),
         pl.BlockSpec((per, None, D_STATE, GROUP_W), lambda g, c: (g, c, 0, 0))],
        [jax.ShapeDtypeStruct((n_rows, D_INNER), F32),
         jax.ShapeDtypeStruct((SSM_GROUPS, n_chunks, D_STATE, GROUP_W), F32)],
        [p, p, p, p, p, p, p, conv_w, conv_w, conv_w, conv_b, conv_b, conv_b, dt_bias, a_log, d_skip],
        scratch_shapes=[pltpu.VMEM((per, D_STATE, GROUP_W), F32), pltpu.VMEM((per, T + 8, GROUP_W), F32)], bg=bg)


def _ssd_bwd(p, conv_w, conv_b, dt_bias, a_log, d_skip, hin, dy, dp, n_chunks, bg=None):
    n_rows = n_chunks * T
    per = SSD_GROUPS_PER_STEP
    assert per == SSM_GROUPS
    dt_w = P_Z - P_DT
    in_specs, cidx = _ssd_specs(n_chunks, rev=True, per_step=per)
    in_specs = in_specs + [pl.BlockSpec((per, None, D_STATE, GROUP_W), lambda g, c: (g, cidx(c), 0, 0)),
                           pl.BlockSpec((T, per * GROUP_W), lambda g, c: (cidx(c), g)), ANY]

    def body(*refs):
        hin_ref, dy_ref = refs[16:18]
        dx_ref, db_ref, dc_ref, dp_ref, dpar_ref, dst_ref, ext_ref, red_ref, dd_ref = refs[19:]
        step = pl.program_id(1)
        shared = _ssd_shared(refs[:16], n_chunks - 1 - step)
        local = jnp.zeros((T, 128), F32)
        for gg in range(per):
            x_w, n_w = pl.ds(GROUP_W * gg, GROUP_W), pl.ds(D_STATE * gg, D_STATE)
            local = local + group_body(_ssd_group_refs(refs[:16], gg), hin_ref.at[gg], dy_ref.at[:, x_w],
                                       dx_ref.at[:, x_w], db_ref.at[:, n_w], dc_ref.at[:, n_w], red_ref.at[:, :, x_w],
                                       dd_ref.at[:, x_w], dst_ref.at[gg], ext_ref.at[gg], gg, shared)
        to_heads = _heads_of_lanes()
        dcs = _xdot(red_ref[0], to_heads) + local
        triu = (_iota((T, T), 0) <= _iota((T, T), 1)).astype(BF16)
        da = _xdot_l(triu, dcs)
        ddt = da * shared["a_neg"] + _xdot(red_ref[1], to_heads)
        ddtr = jnp.where(shared["valid"], ddt * _sigmoid(shared["dtr"]), 0.0)
        dp_ref[...] = jnp.concatenate([ddtr, jnp.zeros((T, dt_w - 128), F32)], axis=1).astype(dp_ref.dtype)
        dpar = jnp.concatenate([
            jnp.sum(ddtr, axis=0, keepdims=True),
            jnp.sum(da * shared["dt"], axis=0, keepdims=True) * shared["a_neg"],
            _xdot(dd_ref[0:1, :], to_heads),
            jnp.zeros((5, 128), F32)], axis=0)
        dpar_ref[...] = jnp.where(step == 0, dpar, dpar_ref[...] + dpar)

    def group_body(in_refs, hin_ref, dy_ref, dx_ref, db_ref, dc_ref, red_ref, dd_ref, dst_ref, ext_ref, g, shared):
        step = pl.program_id(1)
        c = n_chunks - 1 - step

        @pl.when(step == 0)
        def _():
            dst_ref[...] = jnp.zeros_like(dst_ref)

        v = _ssd_chunk_forward(in_refs, ext_ref, g, c, shared)
        hin_f = hin_ref[...]
        hin_b = hin_f.astype(BF16)
        dyv = dy_ref[...]
        dst = dst_ref[...]
        dst_b = dst.astype(BF16)
        xs, bm, cm, xdt = v["xs"], v["bm"], v["cm"], v["xdt"]
        bm_b, cm_b = bm.astype(BF16), cm.astype(BF16)

        dd_e = jnp.sum(dyv * xs, axis=0, keepdims=True)
        dxs = dyv * v["d_e"]
        ch = _dot(cm_b, hin_b)
        dch = (dyv * v["ecs_e"]).astype(BF16)
        dcm = _dot_nt(dch, hin_b)
        dhin = _dot_tn(cm_b, dch) + dst * v["elast_e"]
        dcs_e = dyv * ch * v["ecs_e"]
        dxd = _dot(bm_b, dst_b)
        dbm = _dot_nt((xdt * v["decay_e"]).astype(BF16), dst_b)
        dxdt_state = dxd * v["decay_e"]
        q = dxdt_state * xdt
        dcs_e = dcs_e - q
        dlast_e = jnp.sum(q, axis=0, keepdims=True) + jnp.sum(dst * hin_f, axis=0, keepdims=True) * v["elast_e"]
        dg = jnp.zeros((T, T), F32)
        rs_cols = jnp.zeros((T, 128), F32)
        cs_rows = jnp.zeros((128, T), F32)
        lane_i, sub_i = _iota((T, 128), 1), _iota((128, T), 0)
        dxdt_parts = []
        for jp in range(4):
            l0, l1 = _ssd_decay_pair(v, jp)
            m0, m1 = v["gmat"] * l0, v["gmat"] * l1
            xbd = _block_diag_pair(xdt[:, 128 * jp:128 * jp + 128]).astype(BF16)
            dyp = dyv[:, 128 * jp:128 * jp + 128]
            dm = _dot_nt(dyp.astype(BF16), xbd)
            dm0, dm1 = dm[:, 0:T], dm[:, T:2 * T]
            dg = dg + dm0 * l0 + dm1 * l1
            for j, qq in ((v["head0"] + 2 * jp, dm0 * m0), (v["head0"] + 2 * jp + 1, dm1 * m1)):
                rs_cols = jnp.where(lane_i == j, jnp.sum(qq, axis=1, keepdims=True), rs_cols)
                cs_rows = jnp.where(sub_i == j, jnp.sum(qq, axis=0, keepdims=True), cs_rows)
            mv = jnp.concatenate([m0, m1], axis=0).astype(BF16)
            dxdt_parts.append(_dot_tn(mv, _block_diag_pair(dyp).astype(BF16)))
        dxdt = jnp.concatenate(dxdt_parts, axis=1) + dxdt_state
        dg_b = dg.astype(BF16)
        dcm = dcm + _dot(dg_b, bm_b)
        dbm = dbm + _dot_tn(dg_b, cm_b)
        last_row = _iota((T, 1), 0) == T - 1
        red_ref[0] = dcs_e + jnp.where(last_row, dlast_e, 0.0)
        red_ref[1] = dxdt * xs
        dd_ref[0:1, :] = dd_e
        dx_ref[...] = dxs + dxdt * v["dt_e"]
        db_ref[...] = jnp.where(v["valid"], dbm, 0.0)
        dc_ref[...] = jnp.where(v["valid"], dcm, 0.0)
        dst_ref[...] = dhin
        return rs_cols - cs_rows.T

    return _call(
        body, "ssd_bwd", (SSM_GROUPS // per, n_chunks), in_specs,
        [pl.BlockSpec((T, per * GROUP_W), lambda g, c: (cidx(c), g)),
         pl.BlockSpec((T, per * D_STATE), lambda g, c: (cidx(c), g)),
         pl.BlockSpec((T, per * D_STATE), lambda g, c: (cidx(c), g)),
         pl.BlockSpec((T, dt_w), lambda g, c: (cidx(c), P_DT // dt_w)),
         pl.BlockSpec((8, 128), lambda g, c: (0, 0))],
        [jax.ShapeDtypeStruct((n_rows, D_INNER), F32),
         jax.ShapeDtypeStruct((n_rows, SSM_GROUPS * D_STATE), F32),
         jax.ShapeDtypeStruct((n_rows, SSM_GROUPS * D_STATE), F32),
         jax.ShapeDtypeStruct(dp.shape, dp.dtype),
         jax.ShapeDtypeStruct((8, 128), F32)],
        [p, p, p, p, p, p, p, conv_w, conv_w, conv_w, conv_b, conv_b, conv_b, dt_bias, a_log, d_skip, hin, dy, dp],
        scratch_shapes=[pltpu.VMEM((per, D_STATE, GROUP_W), F32), pltpu.VMEM((per, T + 8, GROUP_W), F32),
                        pltpu.VMEM((2, T, D_INNER), F32), pltpu.VMEM((8, D_INNER), F32)],
        aliases={18: 3}, bg=bg)


def _alibi_slope(h):
    return 2.0 ** (-8.0 * (h + 1) / ATTN_HEADS)


def _dup_half(x256, kvh):
    xb = x256[:, 128 * (kvh // 2):128 * (kvh // 2) + 128]
    rolled = pltpu.roll(xb, 64, 1)
    lane = _iota(xb.shape, 1)
    if kvh % 2 == 0:
        return jnp.where(lane < 64, xb, rolled)
    return jnp.where(lane < 64, rolled, xb)


def _attn_masks(c):
    qi, j = _iota((T, T), 0), _iota((T, T), 1)
    tri = j <= qi
    meta_ok = (j >= PAD) & (j - PAD <= c * T + qi - PAD)
    band_ok = c >= jnp.where(tri, 1, 2)
    dist = jnp.bitwise_and(qi - j, T - 1).astype(F32)
    return tri, meta_ok, band_ok, dist


def _fold(x3, tri):
    return jnp.concatenate([x3[:, 0:T], jnp.where(tri, x3[:, 2 * T:3 * T], x3[:, T:2 * T])], axis=1)


def _unfold(x2, tri):
    band = x2[:, T:2 * T]
    return jnp.concatenate([x2[:, 0:T], jnp.where(tri, 0.0, band), jnp.where(tri, band, 0.0)], axis=1)


def _attn_fwd(p, sinks, n_chunks, bg=None):
    n_rows = n_chunks * T
    kb, vb = P_K // KV_W, P_V // KV_W

    def body(q_ref, kc_ref, kp_ref, km_ref, vc_ref, vp_ref, vm_ref, sink_ref, o_ref, lse_ref):
        c = pl.program_id(0)
        sinks_v = sink_ref[...]
        masks = _attn_masks(c)
        tri, meta_ok, band_ok, dist = masks
        lane = _iota((T, 128), 1)
        for kvh in range(KV_HEADS):
            k3 = jnp.concatenate([_dup_half(r[...], kvh) for r in (km_ref, kp_ref, kc_ref)], axis=0).astype(BF16)
            v3 = jnp.concatenate([_dup_half(r[...], kvh) for r in (vm_ref, vp_ref, vc_ref)], axis=0)
            v3bd = _block_diag_rows(v3).astype(BF16)
            q2 = q_ref[:, 256 * kvh:256 * kvh + 256] * SCALE
            q4 = jnp.concatenate([jnp.where((lane < 64) if half == 0 else (lane >= 64), q2[:, 128 * pr:128 * pr + 128], 0.0)
                                  for pr in range(2) for half in range(2)], axis=0).astype(BF16)
            raw4 = _dot_nt(q4, k3)
            probs = []
            for hh in range(4):
                h = 4 * kvh + hh
                raw = raw4[T * hh:T * hh + T]
                band = jnp.where(tri, raw[:, 2 * T:3 * T], raw[:, T:2 * T]) - _alibi_slope(h) * dist
                sc = jnp.concatenate([jnp.where(meta_ok, raw[:, 0:T], NEG), jnp.where(band_ok, band, NEG)], axis=1)
                sink = sinks_v[:, h:h + 1]
                m = jnp.maximum(jnp.max(sc, axis=1, keepdims=True), sink)
                e = jnp.exp(sc - m)
                den = jnp.sum(e, axis=1, keepdims=True) + jnp.exp(sink - m)
                probs.append(_unfold(e * (1.0 / den), tri))
                lse_ref[:, h:h + 1] = m + jnp.log(den)
            p4 = jnp.concatenate([jnp.concatenate(probs[0:2], axis=1), jnp.concatenate(probs[2:4], axis=1)], axis=0)
            out = _dot(p4.astype(BF16), v3bd)
            o_ref[:, 256 * kvh:256 * kvh + 256] = jnp.concatenate([out[0:T], out[T:2 * T]], axis=1).astype(o_ref.dtype)

    blk = lambda width, col: pl.BlockSpec((T, width), lambda c: (c, col))
    prev = lambda width, col: pl.BlockSpec((T, width), lambda c: (jnp.maximum(c - 1, 0), col))
    first = lambda width, col: pl.BlockSpec((T, width), lambda c: (0, col))
    return _call(
        body, "attn_fwd", (n_chunks,),
        [blk(ATTN_W, P_Q // ATTN_W), blk(KV_W, kb), prev(KV_W, kb), first(KV_W, kb),
         blk(KV_W, vb), prev(KV_W, vb), first(KV_W, vb), pl.BlockSpec((1, 128), lambda c: (0, 0))],
        [pl.BlockSpec((T, ATTN_W), lambda c: (c, 0)), pl.BlockSpec((T, 128), lambda c: (c, 0))],
        [jax.ShapeDtypeStruct((n_rows, ATTN_W), BF16), jax.ShapeDtypeStruct((n_rows, 128), F32)],
        [p, p, p, p, p, p, p, sinks], bg=bg)


def _block_diag_rows(x3):
    lane = _iota(x3.shape, 1)
    return jnp.concatenate([jnp.where(lane < 64, x3, 0.0), jnp.where(lane >= 64, x3, 0.0)], axis=0)


def _fold_halves(x):
    return x + pltpu.roll(x, 64, 1)


def _attn_bwd(p, sinks, ao, lse, dao, dp, n_chunks, bg=None):
    kb, vb = P_K // KV_W, P_V // KV_W
    rc = lambda s: n_chunks - 1 - s

    def body(q_ref, kc_ref, kp_ref, km_ref, vc_ref, vp_ref, vm_ref, sink_ref, o_ref, lse_ref, do_ref, dp_in_ref,
             dqkv_ref, dsink_ref, kcar_ref, vcar_ref, kmeta_ref, vmeta_ref):
        step = pl.program_id(0)
        c = n_chunks - 1 - step

        @pl.when(step == 0)
        def _():
            for r in (kcar_ref, vcar_ref, kmeta_ref, vmeta_ref):
                r[...] = jnp.zeros_like(r)

        masks = _attn_masks(c)
        tri = masks[0]
        q = q_ref[...] * SCALE
        sinks_v = sink_ref[...]
        lse_v = lse_ref[...]
        ov = o_ref[...].astype(F32)
        dov = do_ref[...].astype(F32)
        lane = _iota((T, 128), 1)
        lane256 = _iota((3 * T, KV_W), 1)
        dsink = jnp.zeros((1, 128), F32)
        dk3_all = jnp.zeros((3 * T, KV_W), F32)
        dv3_all = jnp.zeros((3 * T, KV_W), F32)
        dqs = []
        for kvh in range(KV_HEADS):
            k3 = jnp.concatenate([_dup_half(r[...], kvh) for r in (km_ref, kp_ref, kc_ref)], axis=0).astype(BF16)
            v3 = jnp.concatenate([_dup_half(r[...], kvh) for r in (vm_ref, vp_ref, vc_ref)], axis=0).astype(BF16)
            halves = [(pr, half, (lane < 64) if half == 0 else (lane >= 64)) for pr in range(2) for half in range(2)]
            cols = [slice(128 * (2 * kvh + pr), 128 * (2 * kvh + pr) + 128) for pr in range(2)]
            q4 = jnp.concatenate([jnp.where(mine, q[:, cols[pr]], 0.0) for pr, _, mine in halves], axis=0).astype(BF16)
            do4 = jnp.concatenate([jnp.where(mine, dov[:, cols[pr]], 0.0) for pr, _, mine in halves], axis=0).astype(BF16)
            raw4 = _dot_nt(q4, k3)
            dp4 = _dot_nt(do4, v3)
            ds_rows, pm_rows = [], []
            for hh, (pr, half, mine) in enumerate(halves):
                h = 4 * kvh + hh
                raw = raw4[T * hh:T * hh + T]
                band = jnp.where(tri, raw[:, 2 * T:3 * T], raw[:, T:2 * T]) - _alibi_slope(h) * masks[3]
                sc = jnp.concatenate([jnp.where(masks[1], raw[:, 0:T], NEG), jnp.where(masks[2], band, NEG)], axis=1)
                lse_h = lse_v[:, h:h + 1]
                pm = jnp.exp(sc - lse_h)
                prod = dov[:, cols[pr]] * ov[:, cols[pr]]
                delta = jnp.sum(jnp.where(mine, prod, 0.0), axis=1, keepdims=True)
                dp = _fold(dp4[T * hh:T * hh + T], tri)
                ds_rows.append(_unfold(pm * (dp - delta), tri))
                pm_rows.append(_unfold(pm, tri))
                p_sink = jnp.exp(sinks_v[:, h:h + 1] - lse_h)
                dsink = jnp.where(_iota((1, 128), 1) == h, jnp.sum(-p_sink * delta, axis=0, keepdims=True), dsink)
            ds4 = jnp.concatenate(ds_rows, axis=0).astype(BF16)
            dq4 = _dot(ds4, k3)
            dk3 = _dot_tn(ds4, q4)
            dv3 = _dot_tn(jnp.concatenate(pm_rows, axis=0).astype(BF16), do4)
            for pr in range(2):
                dqs.append(jnp.where(lane < 64, dq4[2 * T * pr:2 * T * pr + T], dq4[2 * T * pr + T:2 * T * pr + 2 * T]) * SCALE)
            in_place = (lane256 >= 64 * kvh) & (lane256 < 64 * kvh + 64)
            wide = lambda x: jnp.concatenate([x, x], axis=1)
            dk3_all = jnp.where(in_place, wide(_fold_halves(dk3)), dk3_all)
            dv3_all = jnp.where(in_place, wide(_fold_halves(dv3)), dv3_all)
        dsink_all = dsink

        @pl.when(step == 0)
        def _():
            dsink_ref[...] = dsink_all

        @pl.when(step > 0)
        def _():
            dsink_ref[...] += dsink_all

        kmeta = kmeta_ref[...] + dk3_all[0:T]
        vmeta = vmeta_ref[...] + dv3_all[0:T]
        kmeta_ref[...] = kmeta
        vmeta_ref[...] = vmeta
        is_first = c == 0
        dk = jnp.where(is_first, kmeta, dk3_all[2 * T:3 * T] + kcar_ref[...])
        dv = jnp.where(is_first, vmeta, dv3_all[2 * T:3 * T] + vcar_ref[...])
        dqkv_ref[...] = jnp.concatenate(dqs + [dk, dv], axis=1).astype(dqkv_ref.dtype)
        kcar_ref[...] = dk3_all[T:2 * T]
        vcar_ref[...] = dv3_all[T:2 * T]

    blk = lambda width, col: pl.BlockSpec((T, width), lambda s: (rc(s), col))
    prev = lambda width, col: pl.BlockSpec((T, width), lambda s: (jnp.maximum(rc(s) - 1, 0), col))
    first = lambda width, col: pl.BlockSpec((T, width), lambda s: (0, col))
    return _call(
        body, "attn_bwd", (n_chunks,),
        [blk(ATTN_W, P_Q // ATTN_W), blk(KV_W, kb), prev(KV_W, kb), first(KV_W, kb),
         blk(KV_W, vb), prev(KV_W, vb), first(KV_W, vb), pl.BlockSpec((1, 128), lambda s: (0, 0)),
         blk(ATTN_W, 0), blk(128, 0), blk(ATTN_W, 0), ANY],
        [blk(QKV_W, P_Q // QKV_W), pl.BlockSpec((1, 128), lambda s: (0, 0))],
        [jax.ShapeDtypeStruct(dp.shape, dp.dtype), jax.ShapeDtypeStruct((1, 128), F32)],
        [p, p, p, p, p, p, p, sinks, ao, lse, dao, dp],
        scratch_shapes=[pltpu.VMEM((T, KV_W), F32)] * 4, aliases={11: 0}, bg=bg)


def _pad_lanes(v, width=128):
    return jnp.pad(v, ((0, 0), (0, width - v.shape[1])))


def _local_step(x, head, tgt, plan):
    w, g, run = plan.w, plan.g, plan.run
    n_tok = x.shape[0]
    n_rows = n_tok + T
    n_chunks = n_rows // T
    tm = _row_tile(n_rows, 384)
    dt_bias, a_log, d_skip = (_pad_lanes(w[k]) for k in ("ssm_dt_bias", "ssm_a_log", "ssm_d_skip"))
    sinks = _pad_lanes(w["attn_sinks"])
    x_in = [(x, D_MODEL, 0, "prev"), (head, D_MODEL, 0, "first")]
    head_tm = jnp.concatenate([head, jnp.zeros((tm - T, D_MODEL), F32)], axis=0)
    x_in_tm = [(x, D_MODEL, 0, "prev"), (head_tm, D_MODEL, 0, "first")]

    def h0_tile(r0, xt, hd):
        return jnp.where(_valid_rows(r0, xt.shape[0], T), xt, hd)

    n1, = _rowwise("norm_pre_mix", lambda r0, xt, hd, wn: [_rms(h0_tile(r0, xt, hd), wn)], n_rows, tm,
                   x_in_tm, [w["norm_pre_mix"]], [(D_MODEL, BF16)], [])
    p = _matmul("in_proj", n1, w["w_cat"], "nn", F32)
    y_ssd, hin = run("ssd_fwd", _ssd_fwd, p, w["ssm_conv_w"], w["ssm_conv_b"], dt_bias, a_log, d_skip, n_chunks)
    ao, lse = run("attn_fwd", _attn_fwd, p, sinks, n_chunks)

    def gate_norm(r0, y, z, wn):
        return [_rms(y * _silu(z), wn)]

    yn, = run("ssm_gate_norm", _rowwise, "ssm_gate_norm", gate_norm, n_rows, tm,
              [(y_ssd, D_INNER, 0), (p, D_INNER, P_Z // D_INNER)], [w["ssm_norm"]], [(D_INNER, BF16)], [])
    y_ssm = _matmul("ssm_out", yn, w["w_ssm_out"], "nn", F32)
    y_attn = _matmul("attn_out", ao, w["w_attn_out"], "nn", F32)

    def mix_gate(r0, ys, ya, gs, ga):
        return [_sigmoid(gs) * ys + _sigmoid(ga) * ya]

    gate_ins = [(p, D_MODEL, P_GATE // D_MODEL), (p, D_MODEL, P_GATE // D_MODEL + 1)]
    mixed, = _rowwise("mix_gate", mix_gate, n_rows, tm, [(y_ssm, D_MODEL, 0), (y_attn, D_MODEL, 0)] + gate_ins,
                      [], [(D_MODEL, BF16)], [])
    mix = _matmul("mix_out", mixed, w["w_mix_out"], "nn", F32)

    def post_mix(r0, mx, xt, hd, w_post, w_pre):
        h1 = jnp.where(_valid_rows(r0, mx.shape[0], PAD), h0_tile(r0, xt, hd) + _rms(mx, w_post), 0.0)
        return [h1, _rms(h1, w_pre)]

    h1, n2 = _rowwise("post_mix", post_mix, n_rows, tm, [(mix, D_MODEL, 0)] + x_in_tm,
                      [w["norm_post_mix"], w["norm_pre_ffn"]], [(D_MODEL, F32), (D_MODEL, BF16)], [])
    u_raw = _matmul("ffn_up", n2, w["w_ffn_up"], "nn", F32)
    f = _ffn_act("ffn_act", u_raw, w["ffn_conv_w"], w["ffn_conv_b"], n_rows)
    ffn = _matmul("ffn_down", f, w["w_ffn_down"], "nn", F32)

    def final(r0, fo, h, t, w_post):
        real = _valid_rows(r0, fo.shape[0], T)
        err = jnp.where(real, h + _rms(fo, w_post) - t, 0.0)
        dy = err * (1.0 / D_MODEL)
        dffn, dw = _rms_bwd(dy, fo, w_post)
        return [dffn, dy, jnp.sum(err * err, axis=0, keepdims=True), dw]

    dffn, dh2, loss_cols, g_norm_post_ffn = _rowwise(
        "loss_head", final, n_rows, tm, [(ffn, D_MODEL, 0), (h1, D_MODEL, 0), (tgt, D_MODEL, 0, "prev")],
        [w["norm_post_ffn"]], [(D_MODEL, BF16), (D_MODEL, F32)], [D_MODEL, D_MODEL])

    g["norm_post_ffn"] = g_norm_post_ffn
    g["w_ffn_down"] = _matmul("ffn_down_dw", f, dffn, "tn", F32)
    df = _matmul("ffn_down_dx", dffn, w["w_ffn_down"], "nt", F32)
    du_raw, dconv = _conv_bwd("ffn_act_bwd", u_raw, 0, [df], [(0, c0) for c0 in range(0, FFN_DIM, CONV_LANES)],
                              w["ffn_conv_w"], w["ffn_conv_b"], n_rows, True)
    g["ffn_conv_w"], g["ffn_conv_b"] = dconv[0:3], dconv[3:4]
    g["w_ffn_up"] = _matmul("ffn_up_dw", n2, du_raw, "tn", F32)
    dn2 = run("ffn_up_dx", _matmul, "ffn_up_dx", du_raw, w["w_ffn_up"], "nt", F32)

    def post_mix_bwd(r0, dn, d2, h, mx, w_pre, w_post):
        dx, dw_pre = _rms_bwd(dn, h, w_pre)
        dh1 = jnp.where(_valid_rows(r0, dn.shape[0], PAD), dx + d2, 0.0)
        dmix, dw_post = _rms_bwd(dh1, mx, w_post)
        return [dh1, dmix, dw_pre, dw_post]

    dh1, dmix, g["norm_pre_ffn"], g["norm_post_mix"] = _rowwise(
        "post_mix_bwd", post_mix_bwd, n_rows, tm,
        [(dn2, D_MODEL, 0), (dh2, D_MODEL, 0), (h1, D_MODEL, 0), (mix, D_MODEL, 0)],
        [w["norm_pre_ffn"], w["norm_post_mix"]], [(D_MODEL, F32), (D_MODEL, BF16)], [D_MODEL, D_MODEL])
    g["w_mix_out"] = _matmul("mix_out_dw", mixed, dmix, "tn", F32)
    dmixed = _matmul("mix_out_dx", dmix, w["w_mix_out"], "nt", F32)

    def mix_gate_bwd(r0, dm, ys, ya, gs, ga):
        ss, sa = _sigmoid(gs), _sigmoid(ga)
        dgate = jnp.concatenate([dm * ys * ss * (1.0 - ss), dm * ya * sa * (1.0 - sa)], axis=1)
        return [dm * ss, dm * sa, dgate]

    dys, dya, dp = _rowwise(
        "mix_gate_bwd", mix_gate_bwd, n_rows, tm,
        [(dmixed, D_MODEL, 0), (y_ssm, D_MODEL, 0), (y_attn, D_MODEL, 0)] + gate_ins,
        [], [(D_MODEL, BF16), (D_MODEL, BF16), (2 * D_MODEL, BF16, "new", P_W, P_GATE // (2 * D_MODEL))], [])
    g["w_ssm_out"] = _matmul("ssm_out_dw", yn, dys, "tn", F32)
    dyn = _matmul("ssm_out_dx", dys, w["w_ssm_out"], "nt", F32)
    g["w_attn_out"] = _matmul("attn_out_dw", ao, dya, "tn", F32)
    dao = _matmul("attn_out_dx", dya, w["w_attn_out"], "nt", BF16)

    def gate_norm_bwd(r0, dn, y, z, wn):
        sz, dsz = _silu_grad(z)
        dyz, dw = _rms_bwd(dn, y * sz, wn)
        live = _valid_rows(r0, dn.shape[0], PAD)
        return [jnp.where(live, dyz * sz, 0.0), jnp.where(live, dyz * y * dsz, 0.0), dw]

    dy_ssd, dp, g["ssm_norm"] = run(
        "ssm_gate_norm_bwd", _rowwise, "ssm_gate_norm_bwd", gate_norm_bwd, n_rows, tm,
        [(dyn, D_INNER, 0), (y_ssd, D_INNER, 0), (p, D_INNER, P_Z // D_INNER)],
        [w["ssm_norm"]], [(D_INNER, F32), (D_INNER, BF16, "into", dp, P_Z // D_INNER)], [D_INNER])
    dp, dsink = run("attn_bwd", _attn_bwd, p, sinks, ao, lse, dao, dp, n_chunks)
    g["attn_sinks"] = dsink[:, 0:ATTN_HEADS]
    dxs, dbm, dcm, dp, dpar = run("ssd_bwd", _ssd_bwd, p, w["ssm_conv_w"], w["ssm_conv_b"], dt_bias, a_log,
                                  d_skip, hin, dy_ssd, dp, n_chunks)
    g["ssm_dt_bias"], g["ssm_a_log"], g["ssm_d_skip"] = (dpar[i:i + 1, 0:SSM_HEADS] for i in range(3))
    x_chunks = [(src, c0) for src, arr in enumerate((dxs, dbm, dcm)) for c0 in range(0, arr.shape[1], CONV_LANES)]
    dp, dconv = run("ssm_conv_bwd", _conv_bwd, "ssm_conv_bwd", p, P_XBC // CONV_DIM, [dxs, dbm, dcm], x_chunks,
                    w["ssm_conv_w"], w["ssm_conv_b"], n_rows, False, into=dp, into_blk=P_XBC // CONV_DIM)
    g["ssm_conv_w"], g["ssm_conv_b"] = dconv[0:4], dconv[4:5]
    g["w_cat_t"] = _matmul("in_proj_dw", dp, n1, "tn", F32)
    dn1 = run("in_proj_dx", _matmul, "in_proj_dx", dp, w["w_cat"], "nt", F32)

    def pre_mix_bwd(r0, dn, d1, xt, hd, wn):
        dx, dw = _rms_bwd(dn, h0_tile(r0, xt, hd), wn)
        dh0 = jnp.where(_valid_rows(r0, dn.shape[0], PAD), dx + d1, 0.0)
        return [dh0, dh0, dw]

    dx_out, dhead, g["norm_pre_mix"] = _rowwise(
        "pre_mix_bwd", pre_mix_bwd, n_rows, T, [(dn1, D_MODEL, 0), (dh1, D_MODEL, 0)] + x_in,
        [w["norm_pre_mix"]], [(D_MODEL, F32, "prev", n_tok), (D_MODEL, F32, "first")], [D_MODEL])
    return jnp.sum(loss_cols), dx_out, dhead


_IN_SECTIONS = [((5152, 6176), P_Q), ((6176, 6432), P_K), ((6432, 6688), P_V), ((5120, 5152), P_DT),
                ((0, 2048), P_Z), ((6688, 8736), P_GATE), ((2048, 5120), P_XBC)]


IN_SHARD = N_IN // 4


def _shard_pieces(a, b):
    return [(j, max(a, j * IN_SHARD) - j * IN_SHARD, min(b, (j + 1) * IN_SHARD) - j * IN_SHARD)
            for j in range(4) if max(a, j * IN_SHARD) < min(b, (j + 1) * IN_SHARD)]


def _to_cat(w4):
    parts, at = [], 0
    for (a, b), off in _IN_SECTIONS:
        if off > at:
            parts.append(jnp.zeros((w4.shape[1], off - at), w4.dtype))
        parts += [w4[j, :, lo:hi] for j, lo, hi in _shard_pieces(a, b)]
        at = off + (b - a)
    return jnp.concatenate(parts, axis=1)


def _from_cat_t(g_cat_t):
    shards = [[] for _ in range(4)]
    for (a, b), off in sorted(_IN_SECTIONS):
        for j, lo, hi in _shard_pieces(a, b):
            start = off + j * IN_SHARD + lo - a
            shards[j].append(g_cat_t[start:start + hi - lo])
    return jnp.stack([jnp.concatenate(s, axis=0) for s in shards])


LANES = 1024
_BIG = [("w_in", 1024, 2184, "chip"), ("w_ssm_out", 512, 1024, "row"), ("w_attn_out", 256, 1024, "row"),
        ("w_mix_out", 256, 1024, "row"), ("w_ffn_up", 1024, 1408, "col"), ("w_ffn_down", 704, 1024, "row"),
        ("small", 32, LANES, "chip")]
_SMALL_SHARDED = [("ssm_conv_w", (4, 768), 1), ("ffn_conv_w", (3, 1408), 1), ("meta_tokens", (16, 256), 1)]
_REPLICATED = [("norm_pre_mix", 1024), ("ssm_conv_b", 3072), ("ssm_dt_bias", 32), ("ssm_a_log", 32),
               ("ssm_d_skip", 32), ("ssm_norm", 2048), ("attn_sinks", 16), ("norm_post_mix", 1024),
               ("norm_pre_ffn", 1024), ("ffn_conv_b", 5632), ("norm_post_ffn", 1024)]
SMALL_ROWS = 24


def _rep_rows():
    out, at = [], 0
    for _, width in _REPLICATED:
        out.append((at, -(-width // LANES)))
        at += out[-1][1]
    return out, at


def _in_rows(parts):
    rows = [jnp.pad(a, ((0, 0), (0, -a.shape[1] % LANES))).reshape(-1, LANES) for a in parts]
    flat = jnp.concatenate(rows, axis=0)
    return jnp.pad(flat, ((0, SMALL_ROWS - flat.shape[0]), (0, 0)))
WEIGHT_ORDER = ["meta_tokens", "norm_pre_mix", "w_in", "ssm_conv_w", "ssm_conv_b", "ssm_dt_bias", "ssm_a_log",
                "ssm_d_skip", "ssm_norm", "w_ssm_out", "attn_sinks", "w_attn_out", "w_mix_out", "norm_post_mix",
                "norm_pre_ffn", "w_ffn_up", "ffn_conv_w", "ffn_conv_b", "w_ffn_down", "norm_post_ffn"]


def _flatten(parts, rows):
    flat = jnp.concatenate([a.reshape(-1) for a in parts])
    return jnp.pad(flat, (0, rows * LANES - flat.shape[0])).reshape(rows, LANES)


def _unflatten(flat, shapes):
    flat = flat.reshape(-1)
    out, off = [], 0
    for shp in shapes:
        n = math.prod(shp)
        out.append(flat[off:off + n].reshape(shp))
        off += n
    return out


def _shard_of(full, chip, shape, axis):
    return lax.slice_in_dim(full, chip * shape[axis], (chip + 1) * shape[axis], axis=axis)


def _full_shape(r, c, layout):
    return {"row": (4 * r, c), "col": (r, 4 * c), "chip": (4, r, c), "chip_cols": (4, r, c)}[layout]


def _half_shape(r, c, layout):
    return (r, c // 2) if layout == "chip_cols" else (r // 2, c)


def _shard_view(ref, r, c, layout, chip):
    if layout == "row":
        return ref.at[pl.ds(pl.multiple_of(chip * r, 16), r), :]
    if layout == "col":
        return ref.at[:, pl.ds(pl.multiple_of(chip * c, 128), c)]
    return ref.at[chip]


def _half_view(ref, r, c, layout, chip, half):
    if layout == "chip_cols":
        return ref.at[chip, :, pl.ds(pl.multiple_of(half * (c // 2), 128), c // 2)]
    hr = r // 2
    if layout == "row":
        return ref.at[pl.ds(pl.multiple_of(chip * r + half * hr, 16), hr), :]
    r0 = pl.multiple_of(half * hr, 16)
    if layout == "col":
        return ref.at[pl.ds(r0, hr), pl.ds(pl.multiple_of(chip * c, 128), c)]
    return ref.at[chip, pl.ds(r0, hr), :]


def _mesh_pos():
    return lax.axis_index("x"), lax.axis_index("y"), lax.axis_index("c")


def _other_chips(x, y):
    return [(1 - x, y), (x, 1 - y), (1 - x, 1 - y)]


def _chip_index(x, y):
    return 2 * x + y


def _run_exchange(name, ex):
    n_in, n_out = len(ex.ins), len(ex.out_shapes)

    def body(*refs):
        in_refs, out_refs = refs[:n_in], refs[n_in:n_in + n_out]
        send_sems, recv_sems = refs[n_in + n_out:]
        copies = [pltpu.make_async_remote_copy(src_ref=s, dst_ref=d, send_sem=send_sems.at[i], recv_sem=recv_sems.at[i],
                                               device_id=dev, device_id_type=MESH)
                  for i, (s, d, dev) in enumerate(ex.make_copies(in_refs, out_refs))]
        assert len(copies) == ex.n_copies
        for cp in copies:
            cp.start()
        for cp in copies:
            cp.wait()

    return pl.pallas_call(
        body, name=name, in_specs=[ANY] * n_in, out_specs=[ANY] * n_out, out_shape=list(ex.out_shapes),
        scratch_shapes=[pltpu.SemaphoreType.DMA((ex.n_copies,)), pltpu.SemaphoreType.DMA((ex.n_copies,))],
        compiler_params=pltpu.CompilerParams(has_side_effects=True),
    )(*ex.ins)


def _join(*exs):
    def make(in_refs, out_refs):
        copies, i0, o0 = [], 0, 0
        for ex in exs:
            copies += ex.make_copies(in_refs[i0:i0 + len(ex.ins)], out_refs[o0:o0 + len(ex.out_shapes)])
            i0, o0 = i0 + len(ex.ins), o0 + len(ex.out_shapes)
        return copies

    aliases, i0, o0 = {}, 0, 0
    for ex in exs:
        aliases.update({i0 + k: o0 + v for k, v in ex.aliases.items()})
        i0, o0 = i0 + len(ex.ins), o0 + len(ex.out_shapes)
    return _Exchange([a for ex in exs for a in ex.ins], [s for ex in exs for s in ex.out_shapes], make,
                     sum(ex.n_copies for ex in exs), aliases)


def _split(exs, results):
    out, o0 = [], 0
    for ex in exs:
        out.append(list(results[o0:o0 + len(ex.out_shapes)]))
        o0 += len(ex.out_shapes)
    return out


def _gather_ici(entries, shards):
    def make(in_refs, out_refs):
        x, y, c = _mesh_pos()
        j = _chip_index(x, y)
        copies = []
        for ref_in, ref_out, (_, r, cc, lay) in zip(in_refs, out_refs, entries):
            copies.append((ref_in, _shard_view(ref_out, r, cc, lay, j), None))
            mine = ref_in.at[pl.ds(pl.multiple_of(c * (r // 2), 16), r // 2), :]
            copies += [(mine, _half_view(ref_out, r, cc, lay, j, c), (*ch, c)) for ch in _other_chips(x, y)]
        return copies

    shapes = [jax.ShapeDtypeStruct(_full_shape(r, cc, lay), s.dtype) for s, (_, r, cc, lay) in zip(shards, entries)]
    return _Exchange(list(shards), shapes, make, 4 * len(entries))


def _gather_pass_on(entries, fulls):
    def make(in_refs, out_refs):
        x, y, c = _mesh_pos()
        copies = []
        for ref, (_, r, cc, lay) in zip(out_refs, entries):
            for ch in _other_chips(x, y):
                landed = _half_view(ref, r, cc, lay, _chip_index(*ch), c)
                copies.append((landed, landed, (x, y, 1 - c)))
        return copies

    return _Exchange(list(fulls), [jax.ShapeDtypeStruct(f.shape, f.dtype) for f in fulls], make, 3 * len(entries),
                     {a: a for a in range(len(entries))})


def _gather_weights(entries, shards):
    n = len(entries)

    def body(*refs):
        ins, outs = refs[:n], refs[n:2 * n]
        send_sems, recv_sems, local_sems = refs[2 * n:]
        x, y, c = _mesh_pos()
        j = _chip_index(x, y)
        sibling = (x, y, 1 - c)
        chips = _other_chips(x, y)
        idx = [_chip_index(*ch) for ch in chips]

        def remote(k, src, dst, dev):
            return pltpu.make_async_remote_copy(src_ref=src, dst_ref=dst, send_sem=send_sems.at[k],
                                                recv_sem=recv_sems.at[k], device_id=dev, device_id_type=MESH)

        own = [pltpu.make_async_copy(ins[a], _shard_view(outs[a], r, cc, lay, j), local_sems.at[a])
               for a, (_, r, cc, lay) in enumerate(entries)]
        for cp in own:
            cp.start()
        first, passed = [], []
        for a, (_, r, cc, lay) in enumerate(entries):
            mine = ins[a].at[pl.ds(pl.multiple_of(c * (r // 2), 16), r // 2), :]
            for k, ch in enumerate(chips):
                first.append(remote(6 * a + k, mine, _half_view(outs[a], r, cc, lay, j, c), (*ch, c)))
                landed = _half_view(outs[a], r, cc, lay, idx[k], c)
                passed.append(remote(6 * a + 3 + k, landed, landed, sibling))
        for cp in first:
            cp.start()
        for a, (_, r, cc, lay) in enumerate(entries):
            for k in range(3):
                landed = _half_view(outs[a], r, cc, lay, idx[k], c)
                remote(6 * a + k, landed, landed, sibling).wait_recv()
                passed[3 * a + k].start()
        for a, (_, r, cc, lay) in enumerate(entries):
            for k in range(3):
                theirs = _half_view(outs[a], r, cc, lay, idx[k], 1 - c)
                remote(6 * a + 3 + k, theirs, theirs, sibling).wait_recv()
        for cp in first + passed:
            cp.wait_send()
        for cp in own:
            cp.wait()

    return pl.pallas_call(
        body, name="gather_weights", in_specs=[ANY] * n, out_specs=[ANY] * n,
        out_shape=[jax.ShapeDtypeStruct(_full_shape(r, cc, lay), s.dtype) for s, (_, r, cc, lay) in zip(shards, entries)],
        scratch_shapes=[pltpu.SemaphoreType.DMA((6 * n,)), pltpu.SemaphoreType.DMA((6 * n,)), pltpu.SemaphoreType.DMA((n,))],
        compiler_params=pltpu.CompilerParams(has_side_effects=True),
    )(*shards)


def _pair_exchange(entries, grads):
    def make(in_refs, out_refs):
        x, y, c = _mesh_pos()
        return [(_half_view(ref_in, r, cc, lay, i, 1 - c), ref_out.at[i], (x, y, 1 - c))
                for ref_in, ref_out, (_, r, cc, lay) in zip(in_refs, out_refs, entries) for i in range(4)]

    return _Exchange(list(grads), [jax.ShapeDtypeStruct((4,) + _half_shape(r, cc, lay), F32) for _, r, cc, lay in entries],
                     make, 4 * len(entries))


def _whole_to_sibling(arrays):
    def make(in_refs, out_refs):
        x, y, c = _mesh_pos()
        return [(r, o, (x, y, 1 - c)) for r, o in zip(in_refs, out_refs)]

    return _Exchange(list(arrays), [jax.ShapeDtypeStruct(a.shape, a.dtype) for a in arrays], make, len(arrays))


def _chip_exchange(psends):
    def make(in_refs, out_refs):
        x, y, c = _mesh_pos()
        return [(ref_in.at[_chip_index(*ch)], ref_out.at[k], (*ch, c))
                for ref_in, ref_out in zip(in_refs, out_refs) for k, ch in enumerate(_other_chips(x, y))]

    return _Exchange(list(psends), [jax.ShapeDtypeStruct((3,) + p.shape[1:], p.dtype) for p in psends], make,
                     3 * len(psends))


def _to_all_chips(array):
    def make(in_refs, out_refs):
        x, y, c = _mesh_pos()
        return [(in_refs[0], out_refs[0].at[k], (*ch, c)) for k, ch in enumerate(_other_chips(x, y))]

    return _Exchange([array], [jax.ShapeDtypeStruct((3,) + array.shape, array.dtype)], make, 3)


SUM_ROWS = 512
ADAM_ROWS = 256


def _pair_sum(name, grad, recv, ids, r, c, layout):
    hr, c = _half_shape(r, c, layout)
    tr = _row_tile(hr, SUM_ROWS)
    nb = hr // tr

    def body(ids_ref, g_ref, r_ref, send_ref, own_ref):
        s = g_ref[...] + r_ref[...]
        send_ref[...] = s.astype(send_ref.dtype)

        @pl.when(pl.program_id(1) == ids_ref[1])
        def _():
            own_ref[...] = s

    if layout == "row":
        g_spec = pl.BlockSpec((tr, c), lambda t, j, ids_ref: ((j * r + ids_ref[0] * hr) // tr + t, 0))
    elif layout == "col":
        g_spec = pl.BlockSpec((tr, c), lambda t, j, ids_ref: (ids_ref[0] * nb + t, j))
    elif layout == "chip_cols":
        g_spec = pl.BlockSpec((None, tr, c), lambda t, j, ids_ref: (j, t, ids_ref[0]))
    else:
        g_spec = pl.BlockSpec((None, tr, c), lambda t, j, ids_ref: (j, ids_ref[0] * nb + t, 0))
    grid_spec = pltpu.PrefetchScalarGridSpec(
        num_scalar_prefetch=1, grid=(nb, 4),
        in_specs=[g_spec, pl.BlockSpec((None, tr, c), lambda t, j, ids_ref: (j, t, 0))],
        out_specs=[pl.BlockSpec((None, tr, c), lambda t, j, ids_ref: (j, t, 0)),
                   pl.BlockSpec((tr, c), lambda t, j, ids_ref: (t, 0))])
    return pl.pallas_call(
        body, name=name, grid_spec=grid_spec,
        out_shape=[jax.ShapeDtypeStruct((4, hr, c), BF16), jax.ShapeDtypeStruct((hr, c), F32)],
        compiler_params=_cparams(2),
    )(ids, grad, recv)


def _chip_sum(name, own, recv):
    hr, c = own.shape
    tr = _row_tile(hr, SUM_ROWS)

    def body(o_ref, r_ref, out_ref):
        out_ref[...] = ((o_ref[...] + r_ref[0].astype(F32)) + r_ref[1].astype(F32)) + r_ref[2].astype(F32)

    return pl.pallas_call(
        body, name=name, grid=(hr // tr,),
        in_specs=[pl.BlockSpec((tr, c), lambda i: (i, 0)), pl.BlockSpec((3, tr, c), lambda i: (0, i, 0))],
        out_specs=pl.BlockSpec((tr, c), lambda i: (i, 0)),
        out_shape=jax.ShapeDtypeStruct((hr, c), F32), compiler_params=_cparams(1),
    )(own, recv)


def _chip_sum_small(own, recv, ids):
    def body(ids_ref, o_ref, r_ref, out_ref):
        j = ids_ref[1]
        total = None
        for i in range(4):
            m = jnp.bitwise_xor(i, j)
            term = jnp.where(m == 0, o_ref[...], jnp.where(m == 2, r_ref[0], jnp.where(m == 1, r_ref[1], r_ref[2])))
            total = term if total is None else total + term
        out_ref[...] = total

    grid_spec = pltpu.PrefetchScalarGridSpec(
        num_scalar_prefetch=1, grid=(1,),
        in_specs=[pl.BlockSpec(own.shape, lambda i, ids_ref: (0, 0)), pl.BlockSpec(recv.shape, lambda i, ids_ref: (0, 0, 0))],
        out_specs=pl.BlockSpec(own.shape, lambda i, ids_ref: (0, 0)))
    return pl.pallas_call(body, name="chip_sum_small", grid_spec=grid_spec,
                          out_shape=jax.ShapeDtypeStruct(own.shape, F32), compiler_params=_cparams(1))(ids, own, recv)


def _adamw(name, w, m, v, mine, theirs, ids):
    lead = (None,) * (w.ndim - 2)
    rows, cols = w.shape[-2:]
    half = rows // 2
    tr = _row_tile(half, ADAM_ROWS, unit=8)
    nb = half // tr
    c1 = 1.0 / (1.0 - ADAM_B1 ** ADAM_STEP)
    c2 = 1.0 / (1.0 - ADAM_B2 ** ADAM_STEP)

    def body(ids_ref, w_ref, m_ref, v_ref, mine_ref, theirs_ref, g_out, d_out, m_out, v_out):
        g = jnp.where(pl.program_id(0) == ids_ref[0], mine_ref[...], theirs_ref[...])
        m_new = ADAM_B1 * m_ref[...] + (1.0 - ADAM_B1) * g
        v_new = ADAM_B2 * v_ref[...] + (1.0 - ADAM_B2) * (g * g)
        d_out[...] = -ADAM_LR * ((m_new * c1) / (jnp.sqrt(v_new * c2) + ADAM_EPS) + ADAM_WD * w_ref[...])
        g_out[...] = g
        m_out[...] = m_new
        v_out[...] = v_new

    full = pl.BlockSpec(lead + (tr, cols), lambda h, i, ids_ref: (0,) * len(lead) + (h * nb + i, 0))
    part = pl.BlockSpec((tr, cols), lambda h, i, ids_ref: (i, 0))
    grid_spec = pltpu.PrefetchScalarGridSpec(num_scalar_prefetch=1, grid=(2, nb),
                                             in_specs=[full, full, full, part, part], out_specs=[full] * 4)
    return pl.pallas_call(
        body, name=name, grid_spec=grid_spec,
        out_shape=[jax.ShapeDtypeStruct(w.shape, F32)] * 4, compiler_params=_cparams(2),
    )(ids, w, m, v, mine, theirs)


def _adamw_whole(name, w, m, v, g):
    rows, cols = w.shape[-2:]
    tr = _row_tile(rows, 2 * ADAM_ROWS, unit=8)
    c1 = 1.0 / (1.0 - ADAM_B1 ** ADAM_STEP)
    c2 = 1.0 / (1.0 - ADAM_B2 ** ADAM_STEP)

    def body(w_ref, m_ref, v_ref, g_ref, g_out, d_out, m_out, v_out):
        g = g_ref[...]
        m_new = ADAM_B1 * m_ref[...] + (1.0 - ADAM_B1) * g
        v_new = ADAM_B2 * v_ref[...] + (1.0 - ADAM_B2) * (g * g)
        d_out[...] = -ADAM_LR * ((m_new * c1) / (jnp.sqrt(v_new * c2) + ADAM_EPS) + ADAM_WD * w_ref[...])
        g_out[...] = g
        m_out[...] = m_new
        v_out[...] = v_new

    full = pl.BlockSpec((None, tr, cols), lambda i: (0, i, 0))
    return pl.pallas_call(
        body, name=name, grid=(rows // tr,), in_specs=[full, full, full, pl.BlockSpec((tr, cols), lambda i: (i, 0))],
        out_specs=[full] * 4, out_shape=[jax.ShapeDtypeStruct(w.shape, F32)] * 4, compiler_params=_cparams(1),
    )(w, m, v, g)


def _adamw_replicated(g_rows, ws, ms, vs):
    n = len(ws)
    layout, _ = _rep_rows()
    c1 = 1.0 / (1.0 - ADAM_B1 ** ADAM_STEP)
    c2 = 1.0 / (1.0 - ADAM_B2 ** ADAM_STEP)

    def body(g_ref, *refs):
        w_refs, m_refs, v_refs = refs[0:n], refs[n:2 * n], refs[2 * n:3 * n]
        outs = refs[3 * n:]
        for k, (r0, rows) in enumerate(layout):
            width = w_refs[k].shape[1]
            g = jnp.concatenate([g_ref[r0 + j:r0 + j + 1, :] for j in range(rows)], axis=1)[:, 0:width]
            m_new = ADAM_B1 * m_refs[k][...] + (1.0 - ADAM_B1) * g
            v_new = ADAM_B2 * v_refs[k][...] + (1.0 - ADAM_B2) * (g * g)
            outs[k][...] = g
            outs[n + k][...] = -ADAM_LR * ((m_new * c1) / (jnp.sqrt(v_new * c2) + ADAM_EPS) + ADAM_WD * w_refs[k][...])
            outs[2 * n + k][...] = m_new
            outs[3 * n + k][...] = v_new

    res = pl.pallas_call(body, name="adamw_replicated",
                         out_shape=[jax.ShapeDtypeStruct(w.shape, F32) for _ in range(4) for w in ws])(g_rows, *ws, *ms, *vs)
    return [res[k * n:(k + 1) * n] for k in range(4)]


def _small_shard(parts):
    return _flatten(parts, _BIG[-1][1])


_ENTRY = {e[0]: e for e in _BIG}
_GRAD_ENTRY = {**_ENTRY, "w_in": ("w_in", IN_SHARD, D_MODEL, "chip_cols")}
FFN_MATS = ("w_ffn_down", "w_ffn_up")
MIXER_MATS = ("w_mix_out", "w_ssm_out", "w_attn_out")


class _StepPlan:
    def __init__(self, w, late_shards, shards, ids):
        self.w, self.g = w, {}
        self.late_shards, self.shards, self.ids = late_shards, shards, ids
        self.sums, self.halves, self.results = {}, {}, {}

    def run(self, name, fn, *args, **kw):
        at = getattr(self, "_at_" + name, None)
        if at is None:
            return fn(*args, **kw)
        exchange, landed = at()
        res, extra = fn(*args, bg=exchange, **kw)
        landed(extra)
        return res

    def _at_ssd_fwd(self):
        def landed(fulls):
            self.partly_gathered = fulls

        return _gather_ici([_ENTRY[n] for n in MIXER_MATS], [self.late_shards[n] for n in MIXER_MATS]), landed

    def _at_attn_fwd(self):
        stages = (_gather_pass_on([_ENTRY[n] for n in MIXER_MATS], self.partly_gathered),
                  _gather_ici([_ENTRY[n] for n in FFN_MATS], [self.late_shards[n] for n in FFN_MATS]))

        def landed(extra):
            mixer, self.partly_gathered = _split(stages, extra)
            self.w.update(zip(MIXER_MATS, mixer))

        return _join(*stages), landed

    def _at_ssm_gate_norm(self):
        return (_gather_pass_on([_ENTRY[n] for n in FFN_MATS], self.partly_gathered),
                lambda fulls: self.w.update(zip(FFN_MATS, fulls)))

    def pair_sums(self, names, grads, recv):
        for n, gr, rv in zip(names, grads, recv):
            _, r, c, lay = _GRAD_ENTRY[n]
            self.sums[n] = _pair_sum("pair_sum_" + n, gr, rv, self.ids, r, c, lay)

    def chip_sums(self, names, recv):
        for n, rv in zip(names, recv):
            self.halves[n] = _chip_sum("chip_sum_" + n, self.sums[n][1], rv)

    def adamw(self, names, theirs):
        for n, th in zip(names, theirs):
            sh = self.shards[n]
            if n == "w_in":
                mine_first = self.ids[0] == 0
                g_t = jnp.where(mine_first, jnp.concatenate([self.halves[n], th], axis=1),
                                jnp.concatenate([th, self.halves[n]], axis=1))
                res = _adamw_whole("adamw_" + n, *[jnp.swapaxes(sh[k], -1, -2) for k in ("w", "m", "v")], g_t)
                self.results[n] = [jnp.swapaxes(r, -1, -2) for r in res]
            else:
                self.results[n] = _adamw("adamw_" + n, sh["w"], sh["m"], sh["v"], self.halves[n], th, self.ids)

    def _pair_stage(self, names, grads):
        return (_pair_exchange([_GRAD_ENTRY[n] for n in names], grads),
                lambda recv: self.pair_sums(names, grads, recv))

    def _at_ffn_up_dx(self):
        return self._pair_stage(FFN_MATS, [self.g[n] for n in FFN_MATS])

    def _at_ssm_gate_norm_bwd(self):
        return self._pair_stage(MIXER_MATS, [self.g[n] for n in MIXER_MATS])

    def _at_attn_bwd(self):
        return _chip_exchange([self.sums[n][0] for n in FFN_MATS]), lambda recv: self.chip_sums(FFN_MATS, recv)

    def _at_ssd_bwd(self):
        stages = (_chip_exchange([self.sums[n][0] for n in MIXER_MATS]),
                  _whole_to_sibling([self.halves[n] for n in FFN_MATS]))

        def landed(extra):
            recv, theirs = _split(stages, extra)
            self.chip_sums(MIXER_MATS, recv)
            self.adamw(FFN_MATS, theirs)

        return _join(*stages), landed

    def _at_ssm_conv_bwd(self):
        return _whole_to_sibling([self.halves[n] for n in MIXER_MATS]), lambda theirs: self.adamw(MIXER_MATS, theirs)

    def _at_in_proj_dx(self):
        grads = [_from_cat_t(self.g.pop("w_cat_t"))]
        self.pair_sums(("w_in",), grads,
                       _run_exchange("grad_pair_exchange_w_in", _pair_exchange([_GRAD_ENTRY["w_in"]], grads)))
        return _chip_exchange([self.sums["w_in"][0]]), lambda recv: self.chip_sums(("w_in",), recv)

    def finish(self, g_small, g_rep, rep_shards):
        stages = (_pair_exchange([_ENTRY["small"]], [g_small]), _whole_to_sibling([g_rep]))
        recv_small, recv_rep = _split(stages, _run_exchange("grad_pair_exchange_tail", _join(*stages)))
        self.pair_sums(("small",), [g_small], recv_small)
        p_rep, = _rowwise("pair_sum_replicated", lambda r0, a, b: [a + b], SMALL_ROWS, SMALL_ROWS,
                          [(g_rep, LANES, 0), (recv_rep[0], LANES, 0)], [], [(LANES, F32)], [])
        stages = (_chip_exchange([self.sums["small"][0]]), _to_all_chips(p_rep))
        recv, recv_rep = _split(stages, _run_exchange("grad_chip_exchange_tail", _join(*stages)))
        self.chip_sums(("small",), recv)
        g_rep_tot = _chip_sum_small(p_rep, recv_rep[0], self.ids)
        last = ("w_in", "small")
        self.adamw(last, _run_exchange("grad_half_share_tail", _whole_to_sibling([self.halves[n] for n in last])))
        self.results["replicated"] = _adamw_replicated(g_rep_tot, rep_shards["w"], rep_shards["m"], rep_shards["v"])
        return g_rep_tot[_rep_rows()[1], 0]


def kernel(x, meta_tokens, norm_pre_mix, w_in, ssm_conv_w, ssm_conv_b, ssm_dt_bias, ssm_a_log, ssm_d_skip, ssm_norm, w_ssm_out, attn_sinks, w_attn_out, w_mix_out, norm_post_mix, norm_pre_ffn, w_ffn_up, ffn_conv_w, ffn_conv_b, w_ffn_down, norm_post_ffn, loss_target, m_meta_tokens, m_norm_pre_mix, m_w_in, m_ssm_conv_w, m_ssm_conv_b, m_ssm_dt_bias, m_ssm_a_log, m_ssm_d_skip, m_ssm_norm, m_w_ssm_out, m_attn_sinks, m_w_attn_out, m_w_mix_out, m_norm_post_mix, m_norm_pre_ffn, m_w_ffn_up, m_ffn_conv_w, m_ffn_conv_b, m_w_ffn_down, m_norm_post_ffn, v_meta_tokens, v_norm_pre_mix, v_w_in, v_ssm_conv_w, v_ssm_conv_b, v_ssm_dt_bias, v_ssm_a_log, v_ssm_d_skip, v_ssm_norm, v_w_ssm_out, v_attn_sinks, v_w_attn_out, v_w_mix_out, v_norm_post_mix, v_norm_pre_ffn, v_w_ffn_up, v_ffn_conv_w, v_ffn_conv_b, v_w_ffn_down, v_norm_post_ffn):
    args = dict(locals())
    squeeze = lambda a: a.reshape(a.shape[-2:])
    wts = {n: squeeze(args[n]) for n in WEIGHT_ORDER}
    mom = {n: squeeze(args["m_" + n]) for n in WEIGHT_ORDER}
    var = {n: squeeze(args["v_" + n]) for n in WEIGHT_ORDER}
    x_i, y_i, c_i = _mesh_pos()
    ids = jnp.stack([c_i, _chip_index(x_i, y_i)]).astype(jnp.int32)
    big_names = [n for n, _, _, _ in _BIG[:-1]]
    small_names = [n for n, _, _ in _SMALL_SHARDED]
    rep_names = [n for n, _ in _REPLICATED]

    stacks = {"w": wts, "m": mom, "v": var}
    shards = {n: {"w": args[n], "m": args["m_" + n], "v": args["v_" + n]} for n in big_names}
    shards["small"] = {k: _small_shard([d[n] for n in small_names]) for k, d in stacks.items()}
    rep_shards = {k: [d[n] for n in rep_names] for k, d in stacks.items()}

    w_in4, small_all = _gather_weights([_ENTRY["w_in"], _ENTRY["small"]], [wts["w_in"].astype(BF16), shards["small"]["w"]])
    w = {n: wts[n] for n in rep_names}
    w["w_cat"] = _to_cat(w_in4)
    small_parts = [_unflatten(small_all[i], [shp for _, shp, _ in _SMALL_SHARDED]) for i in range(4)]
    for k, (n, _, axis) in enumerate(_SMALL_SHARDED):
        w[n] = jnp.concatenate([small_parts[i][k] for i in range(4)], axis=axis)
    plan = _StepPlan(w, {n: wts[n].astype(BF16) for n in MIXER_MATS + FFN_MATS}, shards, ids)

    head = jnp.concatenate([jnp.zeros((PAD, D_MODEL), F32), w["meta_tokens"]], axis=0)
    loss_sum, dx, dhead = _local_step(x[0], head, loss_target[0], plan)
    g = plan.g
    g["meta_tokens"] = dhead[PAD:]
    g_small = jnp.stack([_small_shard([_shard_of(g[n], i, shp, ax) for n, shp, ax in _SMALL_SHARDED]) for i in range(4)])
    loss_part = (loss_sum * (0.5 / D_MODEL)).reshape(1, 1)
    loss = plan.finish(g_small, _in_rows([g[n] for n in rep_names] + [loss_part]), rep_shards)

    results = {}
    for kind in range(4):
        results.update({(kind, n): plan.results[n][kind] for n in big_names})
        parts = _unflatten(plan.results["small"][kind], [shp for _, shp, _ in _SMALL_SHARDED])
        results.update({(kind, n): parts[k] for k, n in enumerate(small_names)})
        results.update({(kind, n): plan.results["replicated"][kind][k] for k, n in enumerate(rep_names)})
    outs = [results[kind, n].reshape(args[n].shape) for kind in range(4) for n in WEIGHT_ORDER]
    return (loss, dx[None], *outs)
```

```python
import math
from typing import Any, Callable, NamedTuple, Sequence

import jax
import jax.numpy as jnp
from jax import lax
from jax.experimental import pallas as pl
from jax.experimental.pallas import tpu as pltpu

F32 = jnp.float32
BF16 = jnp.bfloat16

D_MODEL = 1024
N_META = 16
T = 128
PAD = T - N_META
D_INNER = 2048
SSM_HEADS = 32
HEAD_P = 64
SSM_GROUPS = 4
GROUP_W = D_INNER // SSM_GROUPS
D_STATE = 128
CONV_DIM = D_INNER + 2 * SSM_GROUPS * D_STATE
ATTN_HEADS = 16
KV_HEADS = 4
ATTN_W = 1024
KV_W = 256
FFN_DIM = 2816
N_IN = 8736
EPS = 1e-6
NEG = -1e30
SCALE = 0.125

P_Q, P_K, P_V, P_DT, P_Z, P_GATE, P_XBC = 0, 1024, 1280, 1536, 2048, 4096, 6144
QKV_W = 1536
P_W = 9216

ADAM_LR, ADAM_B1, ADAM_B2, ADAM_EPS, ADAM_WD, ADAM_STEP = 0.001, 0.9, 0.999, 1e-08, 0.01, 10

VMEM_BUDGET = 40 * 1024 * 1024
VMEM_LIMIT = 56 * 1024 * 1024
MESH = pl.DeviceIdType.MESH
ANY = pl.BlockSpec(memory_space=pl.ANY)


def _cparams(n_axes, **kw):
    return pltpu.CompilerParams(dimension_semantics=("arbitrary",) * n_axes, vmem_limit_bytes=VMEM_LIMIT, **kw)


class _Exchange(NamedTuple):
    ins: Sequence[Any]
    out_shapes: Sequence[Any]
    make_copies: Callable
    n_copies: int
    aliases: dict = {}


def _call(body, name, grid, in_specs, out_specs, out_shape, operands, scratch_shapes=(), aliases=None, bg=None):
    aliases = dict(aliases or {})
    if bg is None:
        return pl.pallas_call(body, name=name, grid=grid, in_specs=in_specs, out_specs=out_specs, out_shape=out_shape,
                              scratch_shapes=list(scratch_shapes), input_output_aliases=aliases,
                              compiler_params=_cparams(len(grid)))(*operands)
    n_in, n_out, n_scr = len(in_specs), len(out_specs), len(scratch_shapes)
    nb_in, nb_out = len(bg.ins), len(bg.out_shapes)

    def hosted(*refs):
        ins, bg_ins = refs[:n_in], refs[n_in:n_in + nb_in]
        outs = refs[n_in + nb_in:n_in + nb_in + n_out]
        bg_outs = refs[n_in + nb_in + n_out:n_in + nb_in + n_out + nb_out]
        scratch = refs[n_in + nb_in + n_out + nb_out:n_in + nb_in + n_out + nb_out + n_scr]
        send_sems, recv_sems = refs[-2:]
        pids = [pl.program_id(a) for a in range(len(grid))]
        first, last = pids[0] == 0, pids[0] == grid[0] - 1
        for p, g in zip(pids[1:], grid[1:]):
            first, last = first & (p == 0), last & (p == g - 1)
        copies = []
        for k, (src, dst, peer) in enumerate(bg.make_copies(bg_ins, bg_outs)):
            if peer is None:
                copies.append(pltpu.make_async_copy(src, dst, send_sems.at[k]))
            else:
                copies.append(pltpu.make_async_remote_copy(src_ref=src, dst_ref=dst, send_sem=send_sems.at[k],
                                                           recv_sem=recv_sems.at[k], device_id=peer, device_id_type=MESH))
        assert len(copies) == bg.n_copies

        @pl.when(first)
        def _():
            for cp in copies:
                cp.start()

        body(*ins, *outs, *scratch)

        @pl.when(last)
        def _():
            for cp in copies:
                cp.wait()

    aliases = {(k if k < n_in else k + nb_in): v for k, v in aliases.items()}
    aliases.update({n_in + k: n_out + v for k, v in bg.aliases.items()})
    res = pl.pallas_call(
        hosted, name=name, grid=grid, in_specs=list(in_specs) + [ANY] * nb_in, out_specs=list(out_specs) + [ANY] * nb_out,
        out_shape=list(out_shape) + list(bg.out_shapes), input_output_aliases=aliases,
        scratch_shapes=list(scratch_shapes) + [pltpu.SemaphoreType.DMA((bg.n_copies,))] * 2,
        compiler_params=_cparams(len(grid), has_side_effects=True))(*operands, *bg.ins)
    return res[:n_out], res[n_out:]


def _sigmoid(x):
    return 1.0 / (1.0 + jnp.exp(-x))


def _silu(x):
    return x * _sigmoid(x)


def _silu_grad(x):
    s = _sigmoid(x)
    return x * s, s * (1.0 + x * (1.0 - s))


def _dsilu(x):
    return _silu_grad(x)[1]


def _softplus(x):
    e = jnp.exp(-jnp.abs(x))
    small = e * (1.0 - e * (0.5 - e * (1.0 / 3.0)))
    return jnp.maximum(x, 0.0) + jnp.where(e < 0.01, small, jnp.log(1.0 + e))


def _rms(x, w):
    r = lax.rsqrt(jnp.mean(x * x, axis=-1, keepdims=True) + EPS)
    return x * r * w


def _rms_bwd(dy, x, w):
    r = lax.rsqrt(jnp.mean(x * x, axis=-1, keepdims=True) + EPS)
    xh = x * r
    g = dy * w
    dx = r * (g - xh * jnp.mean(g * xh, axis=-1, keepdims=True))
    dw = jnp.sum(dy * xh, axis=0, keepdims=True)
    return dx, dw


def _dot(a, b):
    return jnp.dot(a, b, preferred_element_type=F32)


def _dot_nt(a, b):
    return lax.dot_general(a, b, (((1,), (1,)), ((), ())), preferred_element_type=F32)


def _dot_tn(a, b):
    return lax.dot_general(a, b, (((0,), (0,)), ((), ())), preferred_element_type=F32)


def _split3(x):
    hi = x.astype(BF16)
    r = x - hi.astype(F32)
    mid = r.astype(BF16)
    lo = (r - mid.astype(F32)).astype(BF16)
    return hi, mid, lo


def _xdot(x, e):
    hi, mid, lo = _split3(x)
    return _dot(hi, e) + _dot(mid, e) + _dot(lo, e)


def _xdot_l(e, x):
    hi, mid, lo = _split3(x)
    return _dot(e, hi) + _dot(e, mid) + _dot(e, lo)


def _iota(shape, dim):
    return lax.broadcasted_iota(jnp.int32, shape, dim)


def _divisors(n, unit):
    return [t for t in range(unit, n + 1, unit) if n % t == 0]


MIN_MATMUL_STEPS = 8
SMALL_MATMUL = 2 ** 33


def _matmul_tiles(m, n, k, a_bytes, b_bytes, o_bytes, m_unit):
    best = None
    for tm in _divisors(m, m_unit):
        for tn in _divisors(n, 128):
            for tk in _divisors(k, 128):
                acc = 0 if tk == k else tm * tn * 4
                vm = 2 * (tm * tk * a_bytes + tk * tn * b_bytes + tm * tn * o_bytes) + acc
                if vm > VMEM_BUDGET:
                    continue
                steps = (m // tm) * (n // tn) * (k // tk)
                want = MIN_MATMUL_STEPS if m * n * k >= SMALL_MATMUL else 2
                score = (tk == k, min(steps, want), min(tm, 256), tm * tn * tk)
                if best is None or score > best[0]:
                    best = (score, (tm, tn, tk))
    return best[1]


def _matmul(name, a, b, mode, out_dtype, bg=None):
    if mode == "nn":
        (m, k), n = a.shape, b.shape[1]
    elif mode == "nt":
        (m, k), n = a.shape, b.shape[0]
    else:
        (k, m), n = a.shape, b.shape[1]
    ab, bb, ob = a.dtype.itemsize, b.dtype.itemsize, jnp.dtype(out_dtype).itemsize
    tm, tn, tk = _matmul_tiles(m, n, k, ab, bb, ob, 128 if mode == "tn" else 16)
    nk = k // tk
    dot = {"nn": _dot, "nt": _dot_nt, "tn": _dot_tn}[mode]

    def body(a_ref, b_ref, o_ref, *scratch):
        prod = dot(a_ref[...].astype(BF16), b_ref[...].astype(BF16))
        if nk == 1:
            o_ref[...] = prod.astype(o_ref.dtype)
        else:
            acc_ref, = scratch
            kk = pl.program_id(2)

            @pl.when(kk == 0)
            def _():
                acc_ref[...] = prod

            @pl.when(kk > 0)
            def _():
                acc_ref[...] += prod

            @pl.when(kk == nk - 1)
            def _():
                o_ref[...] = acc_ref[...].astype(o_ref.dtype)

    a_spec = pl.BlockSpec((tk, tm), lambda i, j, kk: (kk, i)) if mode == "tn" else pl.BlockSpec((tm, tk), lambda i, j, kk: (i, kk))
    b_spec = pl.BlockSpec((tn, tk), lambda i, j, kk: (j, kk)) if mode == "nt" else pl.BlockSpec((tk, tn), lambda i, j, kk: (kk, j))
    res = _call(body, name, (m // tm, n // tn, nk), [a_spec, b_spec], [pl.BlockSpec((tm, tn), lambda i, j, kk: (i, j))],
                [jax.ShapeDtypeStruct((m, n), out_dtype)], [a, b],
                scratch_shapes=[] if nk == 1 else [pltpu.VMEM((tm, tn), F32)], bg=bg)
    return res[0] if bg is None else (res[0][0], res[1])


def _row_tile(n_rows, cap, unit=16):
    return max([t for t in _divisors(n_rows, unit) if t <= cap], default=n_rows)


ROW_SUB = 384
GROUP_UNROLL = 4


def _rowwise(name, fn, n_rows, tm, row_ins, full_ins, row_outs, acc_outs, bg=None):
    n_in = len(row_ins) + len(full_ins)
    n_ro = len(row_outs)
    into = [(k, o[3]) for k, o in enumerate(row_outs) if len(o) > 2 and o[2] == "into"]

    sub = min(tm, ROW_SUB)
    counts = [tm // T if len(e) > 3 and e[3] == "prev" else 1 for e in row_ins]
    assert all(cnt == 1 for cnt in counts) or sub == tm
    starts = [sum(counts[:k]) for k in range(len(counts))]
    n_row_in = sum(counts)
    n_in = n_row_in + len(full_ins)

    def body(*refs):
        i = pl.program_id(0)
        outs = refs[n_in + len(into):]

        sums = tuple(jnp.zeros((1, w), F32) for w in acc_outs)
        for s in range(tm // sub):
            rows = pl.ds(s * sub, sub)
            vals = [refs[st][rows, :] if cnt == 1 else jnp.concatenate([refs[st + k][...] for k in range(cnt)], axis=0)
                    for st, cnt in zip(starts, counts)]
            vals += [r[...] for r in refs[n_row_in:n_in]]
            res = fn(i * tm + s * sub, *vals)
            for o, r, v in zip(row_outs, outs[:n_ro], res[:n_ro]):
                if len(o) > 2 and o[2] == "first":
                    @pl.when(i == 0)
                    def _(r=r, v=v, rows=rows):
                        r[rows, :] = v.astype(r.dtype)
                else:
                    r[rows, :] = v.astype(r.dtype)
            sums = tuple(a + v for a, v in zip(sums, res[n_ro:]))

        @pl.when(i == 0)
        def _():
            for r, v in zip(outs[n_ro:], sums):
                r[...] = v

        @pl.when(i > 0)
        def _():
            for r, v in zip(outs[n_ro:], sums):
                r[...] += v

    def in_spec(entry, cnt):
        w, cb = entry[1], entry[2]
        if len(entry) > 3 and entry[3] == "prev":
            return [pl.BlockSpec((tm // cnt, w), lambda i, k=k: (jnp.maximum(cnt * i - 1 + k, 0), cb)) for k in range(cnt)]
        if len(entry) > 3 and entry[3] == "first":
            return [pl.BlockSpec((tm, w), lambda i: (0, cb))]
        return [pl.BlockSpec((tm, w), lambda i: (i, cb))]

    def out_spec(o):
        if len(o) == 2:
            return pl.BlockSpec((tm, o[0]), lambda i: (i, 0)), jax.ShapeDtypeStruct((n_rows, o[0]), o[1])
        if o[2] == "new":
            return pl.BlockSpec((tm, o[0]), lambda i: (i, o[4])), jax.ShapeDtypeStruct((n_rows, o[3]), o[1])
        if o[2] == "into":
            return pl.BlockSpec((tm, o[0]), lambda i: (i, o[4])), jax.ShapeDtypeStruct(o[3].shape, o[3].dtype)
        if o[2] == "first":
            return pl.BlockSpec((tm, o[0]), lambda i: (0, 0)), jax.ShapeDtypeStruct((tm, o[0]), o[1])
        return pl.BlockSpec((tm, o[0]), lambda i: (jnp.maximum(i - 1, 0), 0)), jax.ShapeDtypeStruct((o[3], o[0]), o[1])

    in_specs = [s for e, cnt in zip(row_ins, counts) for s in in_spec(e, cnt)]
    in_specs += [pl.BlockSpec(a.shape, lambda i: (0, 0)) for a in full_ins]
    in_specs += [pl.BlockSpec(memory_space=pl.ANY) for _ in into]
    specs_shapes = [out_spec(o) for o in row_outs]
    out_specs = [s for s, _ in specs_shapes] + [pl.BlockSpec((1, w), lambda i: (0, 0)) for w in acc_outs]
    out_shape = [s for _, s in specs_shapes] + [jax.ShapeDtypeStruct((1, w), F32) for w in acc_outs]
    return _call(body, name, (n_rows // tm,), in_specs, out_specs, out_shape,
                 [e[0] for e, cnt in zip(row_ins, counts) for _ in range(cnt)] + list(full_ins) + [arr for _, arr in into],
                 aliases={n_in + a: k for a, (k, _) in enumerate(into)}, bg=bg)


def _valid_rows(first_row, tm, lo):
    return (first_row + _iota((tm, 1), 0)) >= lo


CONV_ROWS = 384
CONV_SUB = 16
CONV_LANES = 256


def _conv_specs(tm, width, blk, n_rows, after):
    specs = [pl.BlockSpec((tm, width), lambda i: (i, blk)),
             pl.BlockSpec((8, width), lambda i: (jnp.maximum(i * (tm // 8) - 1, 0), blk))]
    if after:
        specs.append(pl.BlockSpec((16, width), lambda i: (jnp.minimum((i + 1) * (tm // 16), n_rows // 16 - 1), blk)))
    return specs


def _conv_window(win, w_ref, b_ref, taps, c0, cw, n):
    acc = b_ref[:, c0:c0 + cw] + w_ref[taps - 1:taps, c0:c0 + cw] * win[8:8 + n]
    for k in range(taps - 1):
        acc = acc + w_ref[k:k + 1, c0:c0 + cw] * win[8 - (taps - 1) + k:8 - (taps - 1) + k + n]
    return acc


def _ffn_act(name, u_raw, conv_w, conv_b, n_rows):
    tm, sub, cw = CONV_ROWS, CONV_SUB, CONV_LANES
    taps, width = conv_w.shape
    half = width // 2

    def body(cur_ref, prev_ref, w_ref, b_ref, f_ref, ext_ref):
        i = pl.program_id(0)
        ext_ref[0:8, :] = jnp.where(i > 0, prev_ref[...], 0.0)
        ext_ref[8:8 + tm, :] = cur_ref[...]
        for q in range(half // cw):
            a0, g0 = q * cw, half + q * cw

            def group(s, carry):
                r = pl.multiple_of(s * sub, sub)
                a = _conv_window(ext_ref[pl.ds(r, sub + 8), a0:a0 + cw], w_ref, b_ref, taps, a0, cw, sub)
                g = _conv_window(ext_ref[pl.ds(r, sub + 8), g0:g0 + cw], w_ref, b_ref, taps, g0, cw, sub)
                f_ref[pl.ds(r, sub), a0:a0 + cw] = (_silu(a) * g).astype(f_ref.dtype)
                return carry

            lax.fori_loop(0, tm // sub, group, 0, unroll=GROUP_UNROLL)

        @pl.when(i == 0)
        def _():
            f_ref[0:PAD, :] = jnp.zeros((PAD, half), f_ref.dtype)

    return pl.pallas_call(
        body, name=name, grid=(n_rows // tm,),
        in_specs=_conv_specs(tm, width, 0, n_rows, False) + [pl.BlockSpec((taps, width), lambda i: (0, 0)),
                                                             pl.BlockSpec((1, width), lambda i: (0, 0))],
        out_specs=pl.BlockSpec((tm, half), lambda i: (i, 0)),
        out_shape=jax.ShapeDtypeStruct((n_rows, half), BF16),
        scratch_shapes=[pltpu.VMEM((tm + 8, width), F32)],
        compiler_params=_cparams(1),
    )(u_raw, u_raw, conv_w, conv_b)


def _conv_bwd(name, raw, raw_blk, dsrcs, chunk_src, conv_w, conv_b, n_rows, gated, into=None, into_blk=0, bg=None):
    taps, width = conv_w.shape
    half = width // 2 if gated else width
    tm, sub, cw = CONV_ROWS, CONV_SUB, CONV_LANES
    te = tm + 16
    nd = len(dsrcs)
    n_parts = 2 if gated else 1

    def body(*refs):
        cur_ref, prev_ref, next_ref = refs[0:3]
        dcur, dnext = refs[3:3 + nd], refs[3 + nd:3 + 2 * nd]
        w_ref, b_ref = refs[3 + 2 * nd:5 + 2 * nd]
        out_ref, acc_ref, ext_ref, du_ref = refs[-4:]
        i = pl.program_id(0)
        ext_ref[0:8, :] = jnp.where(i > 0, prev_ref[...], 0.0)
        ext_ref[8:8 + tm, :] = cur_ref[...]
        ext_ref[8 + tm:24 + tm, :] = next_ref[...]

        for q, (src, off) in enumerate(chunk_src):
            cols = [q * cw, half + q * cw][:n_parts]

            def conv_grad(r, d, past_end):
                pre = [_conv_window(ext_ref[pl.ds(r, sub + 8), c0:c0 + cw], w_ref, b_ref, taps, c0, cw, sub) for c0 in cols]
                if gated:
                    act, dact = _silu_grad(pre[0])
                    dus = [d * pre[1] * dact, d * act]
                else:
                    dus = [d * _dsilu(pre[0])]
                for part, du in enumerate(dus):
                    if past_end:
                        du = jnp.where(i * tm + r + _iota((sub, 1), 0) < n_rows, du, 0.0)
                    du_ref[part, pl.ds(r, sub), :] = du

            def tile_rows(s, carry):
                r = pl.multiple_of(s * sub, sub)
                conv_grad(r, dcur[src][pl.ds(r, sub), off:off + cw].astype(F32), False)
                return carry

            lax.fori_loop(0, tm // sub, tile_rows, 0, unroll=GROUP_UNROLL)
            conv_grad(tm, dnext[src][:, off:off + cw].astype(F32), True)

            @pl.when(i == 0)
            def _():
                du_ref[:, 0:PAD, :] = jnp.zeros((n_parts, PAD, cw), F32)

            for part, c0 in enumerate(cols):
                taps_w = [w_ref[k:k + 1, c0:c0 + cw] for k in range(taps)]

                def back(s, sums):
                    new = list(sums)
                    for u in range(2):
                        r = pl.multiple_of((2 * s + u) * sub, sub)
                        win = du_ref[part, pl.ds(r, sub + 8), :]
                        raw_rows = ext_ref[pl.ds(8 + r, sub), c0:c0 + cw]
                        draw = jnp.zeros((sub, cw), F32)
                        for k in range(taps):
                            shifted = win[taps - 1 - k:taps - 1 - k + sub]
                            draw = draw + taps_w[k] * shifted
                            new[k] = new[k] + shifted * raw_rows
                        new[taps] = new[taps] + win[0:sub]
                        out_ref[pl.ds(r, sub), c0:c0 + cw] = draw.astype(out_ref.dtype)
                    return tuple(new)

                sums = lax.fori_loop(0, tm // (2 * sub), back, tuple(jnp.zeros((sub, cw), F32) for _ in range(taps + 1)))

                @pl.when(i == 0)
                def _(c0=c0):
                    out_ref[PAD - sub:PAD, c0:c0 + cw] = jnp.zeros((sub, cw), out_ref.dtype)

                for k in range(taps + 1):
                    total = jnp.sum(sums[k], axis=0, keepdims=True)
                    acc_ref[k:k + 1, c0:c0 + cw] = jnp.where(i == 0, total, acc_ref[k:k + 1, c0:c0 + cw] + total)

    in_specs = _conv_specs(tm, width, raw_blk, n_rows, True)
    in_specs += [pl.BlockSpec((tm, d.shape[1]), lambda i: (i, 0)) for d in dsrcs]
    in_specs += [pl.BlockSpec((16, d.shape[1]), lambda i: (jnp.minimum((i + 1) * (tm // 16), n_rows // 16 - 1), 0)) for d in dsrcs]
    in_specs += [pl.BlockSpec((taps, width), lambda i: (0, 0)), pl.BlockSpec((1, width), lambda i: (0, 0))]
    operands = [raw, raw, raw] + list(dsrcs) + list(dsrcs) + [conv_w, conv_b]
    aliases = {}
    if into is None:
        out0 = jax.ShapeDtypeStruct((n_rows, width), BF16)
    else:
        in_specs.append(pl.BlockSpec(memory_space=pl.ANY))
        operands.append(into)
        aliases = {len(operands) - 1: 0}
        out0 = jax.ShapeDtypeStruct(into.shape, into.dtype)
    return _call(body, name, (n_rows // tm,), in_specs,
                 [pl.BlockSpec((tm, width), lambda i: (i, into_blk)), pl.BlockSpec((8, width), lambda i: (0, 0))],
                 [out0, jax.ShapeDtypeStruct((8, width), F32)], operands,
                 scratch_shapes=[pltpu.VMEM((tm + 24, width), F32), pltpu.VMEM((n_parts, te + 8, cw), F32)],
                 aliases=aliases, bg=bg)


def _ssd_specs(n_chunks, rev, per_step=1):
    cidx = (lambda c: n_chunks - 1 - c) if rev else (lambda c: c)
    xw, nw = per_step * GROUP_W, per_step * D_STATE
    xg0, bg0, cg0 = P_XBC // xw, (P_XBC + D_INNER) // nw, (P_XBC + D_INNER + SSM_GROUPS * D_STATE) // nw

    def cur(width, blk0):
        return pl.BlockSpec((T, width), lambda g, c: (cidx(c), blk0 + g))

    def prev(width, blk0):
        return pl.BlockSpec((8, width), lambda g, c: (jnp.maximum(cidx(c) * (T // 8) - 1, 0), blk0 + g))

    specs = [cur(xw, xg0), prev(xw, xg0), cur(nw, bg0), prev(nw, bg0), cur(nw, cg0), prev(nw, cg0),
             pl.BlockSpec((T, 128), lambda g, c: (cidx(c), P_DT // 128))]
    wb, wc = D_INNER // nw, (D_INNER + SSM_GROUPS * D_STATE) // nw
    specs += [pl.BlockSpec((4, xw), lambda g, c: (0, g)),
              pl.BlockSpec((4, nw), lambda g, c: (0, wb + g)),
              pl.BlockSpec((4, nw), lambda g, c: (0, wc + g)),
              pl.BlockSpec((1, xw), lambda g, c: (0, g)),
              pl.BlockSpec((1, nw), lambda g, c: (0, wb + g)),
              pl.BlockSpec((1, nw), lambda g, c: (0, wc + g))]
    specs += [pl.BlockSpec((1, 128), lambda g, c: (0, 0))] * 3
    return specs, cidx


def _ssd_shared(refs, c):
    dt_ref, dtb_ref, alog_ref = refs[6], refs[13], refs[14]
    valid = _valid_rows(c * T, T, PAD)
    dtr = dt_ref[...] + dtb_ref[...]
    dt = jnp.where(valid, _softplus(dtr), 0.0)
    a_neg = -jnp.exp(alog_ref[...])
    tril = _iota((T, T), 0) >= _iota((T, T), 1)
    cs = _xdot_l(tril.astype(BF16), dt * a_neg)
    return dict(valid=valid, dtr=dtr, dt=dt, a_neg=a_neg, tril=tril, cs=cs, cs_t=cs.T)


def _heads_of_lanes():
    hh_t, ll_t = _iota((D_INNER, 128), 1), _iota((D_INNER, 128), 0)
    return (hh_t == jnp.right_shift(ll_t, 6)).astype(BF16)


def _ssd_chunk_forward(refs, ext_ref, g, c, shared):
    (xc_ref, xp_ref, bc_ref, bp_ref, cc_ref, cp_ref, dt_ref, wx_ref, wb_ref, wc_ref,
     bx_ref, bb_ref, bcb_ref, dtb_ref, alog_ref, dsk_ref) = refs

    def conv_pre(cur_ref, prev_ref, w_ref, b_ref, width):
        ext_ref[0:8, 0:width] = jnp.where(c > 0, prev_ref[...], 0.0)
        ext_ref[8:8 + T, 0:width] = cur_ref[...]
        w = w_ref[...]
        acc = b_ref[...] + w[3:4] * cur_ref[...]
        for k in range(3):
            acc = acc + w[k:k + 1] * ext_ref[pl.ds(5 + k, T), 0:width]
        return acc

    v = dict(shared)
    valid = v["valid"]
    v["head0"] = 8 * g
    v["x_pre"] = conv_pre(xc_ref, xp_ref, wx_ref, bx_ref, GROUP_W)
    v["b_pre"] = conv_pre(bc_ref, bp_ref, wb_ref, bb_ref, D_STATE)
    v["c_pre"] = conv_pre(cc_ref, cp_ref, wc_ref, bcb_ref, D_STATE)
    xs = _silu(v["x_pre"])
    bm = jnp.where(valid, _silu(v["b_pre"]), 0.0)
    cm = jnp.where(valid, _silu(v["c_pre"]), 0.0)
    hh, ll = _iota((128, GROUP_W), 0), _iota((128, GROUP_W), 1)
    expand = (hh == 8 * g + jnp.right_shift(ll, 6)).astype(BF16)
    cs_e = _xdot(v["cs"], expand)
    dt_e = _xdot(v["dt"], expand)
    cs_last_e = cs_e[T - 1:T, :]
    v.update(xs=xs, bm=bm, cm=cm, cs_e=cs_e, dt_e=dt_e, cs_last_e=cs_last_e)
    v["xdt"] = xs * dt_e
    v["decay_e"] = jnp.exp(cs_last_e - cs_e)
    v["ecs_e"] = jnp.exp(cs_e)
    v["elast_e"] = jnp.exp(cs_last_e)
    v["d_e"] = _xdot(dsk_ref[...], expand)
    v["gmat"] = _dot_nt(cm.astype(BF16), bm.astype(BF16))
    return v


def _ssd_decay_pair(v, jp):
    out = []
    for j in (v["head0"] + 2 * jp, v["head0"] + 2 * jp + 1):
        diff = v["cs"][:, j:j + 1] - v["cs_t"][j:j + 1, :]
        out.append(jnp.where(v["tril"], jnp.exp(jnp.where(v["tril"], diff, 0.0)), 0.0))
    return out


def _block_diag_pair(xp):
    lane = _iota(xp.shape, 1)
    return jnp.concatenate([jnp.where(lane < HEAD_P, xp, 0.0), jnp.where(lane >= HEAD_P, xp, 0.0)], axis=0)


SSD_GROUPS_PER_STEP = 4


def _ssd_group_refs(refs, gg):
    x_w, n_w = pl.ds(GROUP_W * gg, GROUP_W), pl.ds(D_STATE * gg, D_STATE)
    lanes = [x_w, x_w, n_w, n_w, n_w, n_w, None, x_w, n_w, n_w, x_w, n_w, n_w, None, None, None]
    return [r if w is None else r.at[:, w] for r, w in zip(refs, lanes)]


def _ssd_fwd(p, conv_w, conv_b, dt_bias, a_log, d_skip, n_chunks, bg=None):
    n_rows = n_chunks * T
    in_specs, _ = _ssd_specs(n_chunks, rev=False, per_step=SSD_GROUPS_PER_STEP)
    per = SSD_GROUPS_PER_STEP
    assert per == SSM_GROUPS

    def body(*refs):
        y_ref, hin_ref, st_ref, ext_ref = refs[16:]
        c = pl.program_id(1)

        @pl.when(c == 0)
        def _():
            st_ref[...] = jnp.zeros_like(st_ref)

        shared = _ssd_shared(refs[:16], c)
        for gg in range(per):
            v = _ssd_chunk_forward(_ssd_group_refs(refs[:16], gg), ext_ref.at[gg], gg, c, shared)
            state = st_ref[gg]
            hin_ref[gg] = state
            ys = []
            for jp in range(4):
                l0, l1 = _ssd_decay_pair(v, jp)
                lhs = jnp.concatenate([v["gmat"] * l0, v["gmat"] * l1], axis=1).astype(BF16)
                rhs = _block_diag_pair(v["xdt"][:, 128 * jp:128 * jp + 128]).astype(BF16)
                ys.append(_dot(lhs, rhs))
            y = jnp.concatenate(ys, axis=1)
            y = y + _dot(v["cm"].astype(BF16), state.astype(BF16)) * v["ecs_e"] + v["xs"] * v["d_e"]
            y_ref[:, GROUP_W * gg:GROUP_W * gg + GROUP_W] = y
            s_new = _dot_tn(v["bm"].astype(BF16), (v["xdt"] * v["decay_e"]).astype(BF16))
            st_ref[gg] = state * v["elast_e"] + s_new

    return _call(
        body, "ssd_fwd", (SSM_GROUPS // per, n_chunks), in_specs,
        [pl.BlockSpec((T, per * GROUP_W), lambda g, c: (c, g)),
         pl.BlockSpec((per, None, D_STATE, GROUP_W), lambda g, c: (g, c, 0, 0))],
        [jax.ShapeDtypeStruct((n_rows, D_INNER), F32),
         jax.ShapeDtypeStruct((SSM_GROUPS, n_chunks, D_STATE, GROUP_W), F32)],
        [p, p, p, p, p, p, p, conv_w, conv_w, conv_w, conv_b, conv_b, conv_b, dt_bias, a_log, d_skip],
        scratch_shapes=[pltpu.VMEM((per, D_STATE, GROUP_W), F32), pltpu.VMEM((per, T + 8, GROUP_W), F32)], bg=bg)


def _ssd_bwd(p, conv_w, conv_b, dt_bias, a_log, d_skip, hin, dy, dp, n_chunks, bg=None):
    n_rows = n_chunks * T
    per = SSD_GROUPS_PER_STEP
    assert per == SSM_GROUPS
    dt_w = P_Z - P_DT
    in_specs, cidx = _ssd_specs(n_chunks, rev=True, per_step=per)
    in_specs = in_specs + [pl.BlockSpec((per, None, D_STATE, GROUP_W), lambda g, c: (g, cidx(c), 0, 0)),
                           pl.BlockSpec((T, per * GROUP_W), lambda g, c: (cidx(c), g)), ANY]

    def body(*refs):
        hin_ref, dy_ref = refs[16:18]
        dx_ref, db_ref, dc_ref, dp_ref, dpar_ref, dst_ref, ext_ref, red_ref, dd_ref = refs[19:]
        step = pl.program_id(1)
        shared = _ssd_shared(refs[:16], n_chunks - 1 - step)
        local = jnp.zeros((T, 128), F32)
        for gg in range(per):
            x_w, n_w = pl.ds(GROUP_W * gg, GROUP_W), pl.ds(D_STATE * gg, D_STATE)
            local = local + group_body(_ssd_group_refs(refs[:16], gg), hin_ref.at[gg], dy_ref.at[:, x_w],
                                       dx_ref.at[:, x_w], db_ref.at[:, n_w], dc_ref.at[:, n_w], red_ref.at[:, :, x_w],
                                       dd_ref.at[:, x_w], dst_ref.at[gg], ext_ref.at[gg], gg, shared)
        to_heads = _heads_of_lanes()
        dcs = _xdot(red_ref[0], to_heads) + local
        triu = (_iota((T, T), 0) <= _iota((T, T), 1)).astype(BF16)
        da = _xdot_l(triu, dcs)
        ddt = da * shared["a_neg"] + _xdot(red_ref[1], to_heads)
        ddtr = jnp.where(shared["valid"], ddt * _sigmoid(shared["dtr"]), 0.0)
        dp_ref[...] = jnp.concatenate([ddtr, jnp.zeros((T, dt_w - 128), F32)], axis=1).astype(dp_ref.dtype)
        dpar = jnp.concatenate([
            jnp.sum(ddtr, axis=0, keepdims=True),
            jnp.sum(da * shared["dt"], axis=0, keepdims=True) * shared["a_neg"],
            _xdot(dd_ref[0:1, :], to_heads),
            jnp.zeros((5, 128), F32)], axis=0)
        dpar_ref[...] = jnp.where(step == 0, dpar, dpar_ref[...] + dpar)

    def group_body(in_refs, hin_ref, dy_ref, dx_ref, db_ref, dc_ref, red_ref, dd_ref, dst_ref, ext_ref, g, shared):
        step = pl.program_id(1)
        c = n_chunks - 1 - step

        @pl.when(step == 0)
        def _():
            dst_ref[...] = jnp.zeros_like(dst_ref)

        v = _ssd_chunk_forward(in_refs, ext_ref, g, c, shared)
        hin_f = hin_ref[...]
        hin_b = hin_f.astype(BF16)
        dyv = dy_ref[...]
        dst = dst_ref[...]
        dst_b = dst.astype(BF16)
        xs, bm, cm, xdt = v["xs"], v["bm"], v["cm"], v["xdt"]
        bm_b, cm_b = bm.astype(BF16), cm.astype(BF16)

        dd_e = jnp.sum(dyv * xs, axis=0, keepdims=True)
        dxs = dyv * v["d_e"]
        ch = _dot(cm_b, hin_b)
        dch = (dyv * v["ecs_e"]).astype(BF16)
        dcm = _dot_nt(dch, hin_b)
        dhin = _dot_tn(cm_b, dch) + dst * v["elast_e"]
        dcs_e = dyv * ch * v["ecs_e"]
        dxd = _dot(bm_b, dst_b)
        dbm = _dot_nt((xdt * v["decay_e"]).astype(BF16), dst_b)
        dxdt_state = dxd * v["decay_e"]
        q = dxdt_state * xdt
        dcs_e = dcs_e - q
        dlast_e = jnp.sum(q, axis=0, keepdims=True) + jnp.sum(dst * hin_f, axis=0, keepdims=True) * v["elast_e"]
        dg = jnp.zeros((T, T), F32)
        rs_cols = jnp.zeros((T, 128), F32)
        cs_rows = jnp.zeros((128, T), F32)
        lane_i, sub_i = _iota((T, 128), 1), _iota((128, T), 0)
        dxdt_parts = []
        for jp in range(4):
            l0, l1 = _ssd_decay_pair(v, jp)
            m0, m1 = v["gmat"] * l0, v["gmat"] * l1
            xbd = _block_diag_pair(xdt[:, 128 * jp:128 * jp + 128]).astype(BF16)
            dyp = dyv[:, 128 * jp:128 * jp + 128]
            dm = _dot_nt(dyp.astype(BF16), xbd)
            dm0, dm1 = dm[:, 0:T], dm[:, T:2 * T]
            dg = dg + dm0 * l0 + dm1 * l1
            for j, qq in ((v["head0"] + 2 * jp, dm0 * m0), (v["head0"] + 2 * jp + 1, dm1 * m1)):
                rs_cols = jnp.where(lane_i == j, jnp.sum(qq, axis=1, keepdims=True), rs_cols)
                cs_rows = jnp.where(sub_i == j, jnp.sum(qq, axis=0, keepdims=True), cs_rows)
            mv = jnp.concatenate([m0, m1], axis=0).astype(BF16)
            dxdt_parts.append(_dot_tn(mv, _block_diag_pair(dyp).astype(BF16)))
        dxdt = jnp.concatenate(dxdt_parts, axis=1) + dxdt_state
        dg_b = dg.astype(BF16)
        dcm = dcm + _dot(dg_b, bm_b)
        dbm = dbm + _dot_tn(dg_b, cm_b)
        last_row = _iota((T, 1), 0) == T - 1
        red_ref[0] = dcs_e + jnp.where(last_row, dlast_e, 0.0)
        red_ref[1] = dxdt * xs
        dd_ref[0:1, :] = dd_e
        dx_ref[...] = dxs + dxdt * v["dt_e"]
        db_ref[...] = jnp.where(v["valid"], dbm, 0.0)
        dc_ref[...] = jnp.where(v["valid"], dcm, 0.0)
        dst_ref[...] = dhin
        return rs_cols - cs_rows.T

    return _call(
        body, "ssd_bwd", (SSM_GROUPS // per, n_chunks), in_specs,
        [pl.BlockSpec((T, per * GROUP_W), lambda g, c: (cidx(c), g)),
         pl.BlockSpec((T, per * D_STATE), lambda g, c: (cidx(c), g)),
         pl.BlockSpec((T, per * D_STATE), lambda g, c: (cidx(c), g)),
         pl.BlockSpec((T, dt_w), lambda g, c: (cidx(c), P_DT // dt_w)),
         pl.BlockSpec((8, 128), lambda g, c: (0, 0))],
        [jax.ShapeDtypeStruct((n_rows, D_INNER), F32),
         jax.ShapeDtypeStruct((n_rows, SSM_GROUPS * D_STATE), F32),
         jax.ShapeDtypeStruct((n_rows, SSM_GROUPS * D_STATE), F32),
         jax.ShapeDtypeStruct(dp.shape, dp.dtype),
         jax.ShapeDtypeStruct((8, 128), F32)],
        [p, p, p, p, p, p, p, conv_w, conv_w, conv_w, conv_b, conv_b, conv_b, dt_bias, a_log, d_skip, hin, dy, dp],
        scratch_shapes=[pltpu.VMEM((per, D_STATE, GROUP_W), F32), pltpu.VMEM((per, T + 8, GROUP_W), F32),
                        pltpu.VMEM((2, T, D_INNER), F32), pltpu.VMEM((8, D_INNER), F32)],
        aliases={18: 3}, bg=bg)


def _alibi_slope(h):
    return 2.0 ** (-8.0 * (h + 1) / ATTN_HEADS)


def _dup_half(x256, kvh):
    xb = x256[:, 128 * (kvh // 2):128 * (kvh // 2) + 128]
    rolled = pltpu.roll(xb, 64, 1)
    lane = _iota(xb.shape, 1)
    if kvh % 2 == 0:
        return jnp.where(lane < 64, xb, rolled)
    return jnp.where(lane < 64, rolled, xb)


def _attn_masks(c):
    qi, j = _iota((T, T), 0), _iota((T, T), 1)
    tri = j <= qi
    meta_ok = (j >= PAD) & (j - PAD <= c * T + qi - PAD)
    band_ok = c >= jnp.where(tri, 1, 2)
    dist = jnp.bitwise_and(qi - j, T - 1).astype(F32)
    return tri, meta_ok, band_ok, dist


def _fold(x3, tri):
    return jnp.concatenate([x3[:, 0:T], jnp.where(tri, x3[:, 2 * T:3 * T], x3[:, T:2 * T])], axis=1)


def _unfold(x2, tri):
    band = x2[:, T:2 * T]
    return jnp.concatenate([x2[:, 0:T], jnp.where(tri, 0.0, band), jnp.where(tri, band, 0.0)], axis=1)


def _attn_fwd(p, sinks, n_chunks, bg=None):
    n_rows = n_chunks * T
    kb, vb = P_K // KV_W, P_V // KV_W

    def body(q_ref, kc_ref, kp_ref, km_ref, vc_ref, vp_ref, vm_ref, sink_ref, o_ref, lse_ref):
        c = pl.program_id(0)
        sinks_v = sink_ref[...]
        masks = _attn_masks(c)
        tri, meta_ok, band_ok, dist = masks
        lane = _iota((T, 128), 1)
        for kvh in range(KV_HEADS):
            k3 = jnp.concatenate([_dup_half(r[...], kvh) for r in (km_ref, kp_ref, kc_ref)], axis=0).astype(BF16)
            v3 = jnp.concatenate([_dup_half(r[...], kvh) for r in (vm_ref, vp_ref, vc_ref)], axis=0)
            v3bd = _block_diag_rows(v3).astype(BF16)
            q2 = q_ref[:, 256 * kvh:256 * kvh + 256] * SCALE
            q4 = jnp.concatenate([jnp.where((lane < 64) if half == 0 else (lane >= 64), q2[:, 128 * pr:128 * pr + 128], 0.0)
                                  for pr in range(2) for half in range(2)], axis=0).astype(BF16)
            raw4 = _dot_nt(q4, k3)
            probs = []
            for hh in range(4):
                h = 4 * kvh + hh
                raw = raw4[T * hh:T * hh + T]
                band = jnp.where(tri, raw[:, 2 * T:3 * T], raw[:, T:2 * T]) - _alibi_slope(h) * dist
                sc = jnp.concatenate([jnp.where(meta_ok, raw[:, 0:T], NEG), jnp.where(band_ok, band, NEG)], axis=1)
                sink = sinks_v[:, h:h + 1]
                m = jnp.maximum(jnp.max(sc, axis=1, keepdims=True), sink)
                e = jnp.exp(sc - m)
                den = jnp.sum(e, axis=1, keepdims=True) + jnp.exp(sink - m)
                probs.append(_unfold(e * (1.0 / den), tri))
                lse_ref[:, h:h + 1] = m + jnp.log(den)
            p4 = jnp.concatenate([jnp.concatenate(probs[0:2], axis=1), jnp.concatenate(probs[2:4], axis=1)], axis=0)
            out = _dot(p4.astype(BF16), v3bd)
            o_ref[:, 256 * kvh:256 * kvh + 256] = jnp.concatenate([out[0:T], out[T:2 * T]], axis=1).astype(o_ref.dtype)

    blk = lambda width, col: pl.BlockSpec((T, width), lambda c: (c, col))
    prev = lambda width, col: pl.BlockSpec((T, width), lambda c: (jnp.maximum(c - 1, 0), col))
    first = lambda width, col: pl.BlockSpec((T, width), lambda c: (0, col))
    return _call(
        body, "attn_fwd", (n_chunks,),
        [blk(ATTN_W, P_Q // ATTN_W), blk(KV_W, kb), prev(KV_W, kb), first(KV_W, kb),
         blk(KV_W, vb), prev(KV_W, vb), first(KV_W, vb), pl.BlockSpec((1, 128), lambda c: (0, 0))],
        [pl.BlockSpec((T, ATTN_W), lambda c: (c, 0)), pl.BlockSpec((T, 128), lambda c: (c, 0))],
        [jax.ShapeDtypeStruct((n_rows, ATTN_W), BF16), jax.ShapeDtypeStruct((n_rows, 128), F32)],
        [p, p, p, p, p, p, p, sinks], bg=bg)


def _block_diag_rows(x3):
    lane = _iota(x3.shape, 1)
    return jnp.concatenate([jnp.where(lane < 64, x3, 0.0), jnp.where(lane >= 64, x3, 0.0)], axis=0)


def _fold_halves(x):
    return x + pltpu.roll(x, 64, 1)


def _attn_bwd(p, sinks, ao, lse, dao, dp, n_chunks, bg=None):
    kb, vb = P_K // KV_W, P_V // KV_W
    rc = lambda s: n_chunks - 1 - s

    def body(q_ref, kc_ref, kp_ref, km_ref, vc_ref, vp_ref, vm_ref, sink_ref, o_ref, lse_ref, do_ref, dp_in_ref,
             dqkv_ref, dsink_ref, kcar_ref, vcar_ref, kmeta_ref, vmeta_ref):
        step = pl.program_id(0)
        c = n_chunks - 1 - step

        @pl.when(step == 0)
        def _():
            for r in (kcar_ref, vcar_ref, kmeta_ref, vmeta_ref):
                r[...] = jnp.zeros_like(r)

        masks = _attn_masks(c)
        tri = masks[0]
        q = q_ref[...] * SCALE
        sinks_v = sink_ref[...]
        lse_v = lse_ref[...]
        ov = o_ref[...].astype(F32)
        dov = do_ref[...].astype(F32)
        lane = _iota((T, 128), 1)
        lane256 = _iota((3 * T, KV_W), 1)
        dsink = jnp.zeros((1, 128), F32)
        dk3_all = jnp.zeros((3 * T, KV_W), F32)
        dv3_all = jnp.zeros((3 * T, KV_W), F32)
        dqs = []
        for kvh in range(KV_HEADS):
            k3 = jnp.concatenate([_dup_half(r[...], kvh) for r in (km_ref, kp_ref, kc_ref)], axis=0).astype(BF16)
            v3 = jnp.concatenate([_dup_half(r[...], kvh) for r in (vm_ref, vp_ref, vc_ref)], axis=0).astype(BF16)
            halves = [(pr, half, (lane < 64) if half == 0 else (lane >= 64)) for pr in range(2) for half in range(2)]
            cols = [slice(128 * (2 * kvh + pr), 128 * (2 * kvh + pr) + 128) for pr in range(2)]
            q4 = jnp.concatenate([jnp.where(mine, q[:, cols[pr]], 0.0) for pr, _, mine in halves], axis=0).astype(BF16)
            do4 = jnp.concatenate([jnp.where(mine, dov[:, cols[pr]], 0.0) for pr, _, mine in halves], axis=0).astype(BF16)
            raw4 = _dot_nt(q4, k3)
            dp4 = _dot_nt(do4, v3)
            ds_rows, pm_rows = [], []
            for hh, (pr, half, mine) in enumerate(halves):
                h = 4 * kvh + hh
                raw = raw4[T * hh:T * hh + T]
                band = jnp.where(tri, raw[:, 2 * T:3 * T], raw[:, T:2 * T]) - _alibi_slope(h) * masks[3]
                sc = jnp.concatenate([jnp.where(masks[1], raw[:, 0:T], NEG), jnp.where(masks[2], band, NEG)], axis=1)
                lse_h = lse_v[:, h:h + 1]
                pm = jnp.exp(sc - lse_h)
                prod = dov[:, cols[pr]] * ov[:, cols[pr]]
                delta = jnp.sum(jnp.where(mine, prod, 0.0), axis=1, keepdims=True)
                dp = _fold(dp4[T * hh:T * hh + T], tri)
                ds_rows.append(_unfold(pm * (dp - delta), tri))
                pm_rows.append(_unfold(pm, tri))
                p_sink = jnp.exp(sinks_v[:, h:h + 1] - lse_h)
                dsink = jnp.where(_iota((1, 128), 1) == h, jnp.sum(-p_sink * delta, axis=0, keepdims=True), dsink)
            ds4 = jnp.concatenate(ds_rows, axis=0).astype(BF16)
            dq4 = _dot(ds4, k3)
            dk3 = _dot_tn(ds4, q4)
            dv3 = _dot_tn(jnp.concatenate(pm_rows, axis=0).astype(BF16), do4)
            for pr in range(2):
                dqs.append(jnp.where(lane < 64, dq4[2 * T * pr:2 * T * pr + T], dq4[2 * T * pr + T:2 * T * pr + 2 * T]) * SCALE)
            in_place = (lane256 >= 64 * kvh) & (lane256 < 64 * kvh + 64)
            wide = lambda x: jnp.concatenate([x, x], axis=1)
            dk3_all = jnp.where(in_place, wide(_fold_halves(dk3)), dk3_all)
            dv3_all = jnp.where(in_place, wide(_fold_halves(dv3)), dv3_all)
        dsink_all = dsink

        @pl.when(step == 0)
        def _():
            dsink_ref[...] = dsink_all

        @pl.when(step > 0)
        def _():
            dsink_ref[...] += dsink_all

        kmeta = kmeta_ref[...] + dk3_all[0:T]
        vmeta = vmeta_ref[...] + dv3_all[0:T]
        kmeta_ref[...] = kmeta
        vmeta_ref[...] = vmeta
        is_first = c == 0
        dk = jnp.where(is_first, kmeta, dk3_all[2 * T:3 * T] + kcar_ref[...])
        dv = jnp.where(is_first, vmeta, dv3_all[2 * T:3 * T] + vcar_ref[...])
        dqkv_ref[...] = jnp.concatenate(dqs + [dk, dv], axis=1).astype(dqkv_ref.dtype)
        kcar_ref[...] = dk3_all[T:2 * T]
        vcar_ref[...] = dv3_all[T:2 * T]

    blk = lambda width, col: pl.BlockSpec((T, width), lambda s: (rc(s), col))
    prev = lambda width, col: pl.BlockSpec((T, width), lambda s: (jnp.maximum(rc(s) - 1, 0), col))
    first = lambda width, col: pl.BlockSpec((T, width), lambda s: (0, col))
    return _call(
        body, "attn_bwd", (n_chunks,),
        [blk(ATTN_W, P_Q // ATTN_W), blk(KV_W, kb), prev(KV_W, kb), first(KV_W, kb),
         blk(KV_W, vb), prev(KV_W, vb), first(KV_W, vb), pl.BlockSpec((1, 128), lambda s: (0, 0)),
         blk(ATTN_W, 0), blk(128, 0), blk(ATTN_W, 0), ANY],
        [blk(QKV_W, P_Q // QKV_W), pl.BlockSpec((1, 128), lambda s: (0, 0))],
        [jax.ShapeDtypeStruct(dp.shape, dp.dtype), jax.ShapeDtypeStruct((1, 128), F32)],
        [p, p, p, p, p, p, p, sinks, ao, lse, dao, dp],
        scratch_shapes=[pltpu.VMEM((T, KV_W), F32)] * 4, aliases={11: 0}, bg=bg)


def _pad_lanes(v, width=128):
    return jnp.pad(v, ((0, 0), (0, width - v.shape[1])))


def _local_step(x, head, tgt, plan):
    w, g, run = plan.w, plan.g, plan.run
    n_tok = x.shape[0]
    n_rows = n_tok + T
    n_chunks = n_rows // T
    tm = _row_tile(n_rows, 384)
    dt_bias, a_log, d_skip = (_pad_lanes(w[k]) for k in ("ssm_dt_bias", "ssm_a_log", "ssm_d_skip"))
    sinks = _pad_lanes(w["attn_sinks"])
    x_in = [(x, D_MODEL, 0, "prev"), (head, D_MODEL, 0, "first")]
    head_tm = jnp.concatenate([head, jnp.zeros((tm - T, D_MODEL), F32)], axis=0)
    x_in_tm = [(x, D_MODEL, 0, "prev"), (head_tm, D_MODEL, 0, "first")]

    def h0_tile(r0, xt, hd):
        return jnp.where(_valid_rows(r0, xt.shape[0], T), xt, hd)

    n1, = _rowwise("norm_pre_mix", lambda r0, xt, hd, wn: [_rms(h0_tile(r0, xt, hd), wn)], n_rows, tm,
                   x_in_tm, [w["norm_pre_mix"]], [(D_MODEL, BF16)], [])
    p = _matmul("in_proj", n1, w["w_cat"], "nn", F32)
    y_ssd, hin = run("ssd_fwd", _ssd_fwd, p, w["ssm_conv_w"], w["ssm_conv_b"], dt_bias, a_log, d_skip, n_chunks)
    ao, lse = run("attn_fwd", _attn_fwd, p, sinks, n_chunks)

    def gate_norm(r0, y, z, wn):
        return [_rms(y * _silu(z), wn)]

    yn, = run("ssm_gate_norm", _rowwise, "ssm_gate_norm", gate_norm, n_rows, tm,
              [(y_ssd, D_INNER, 0), (p, D_INNER, P_Z // D_INNER)], [w["ssm_norm"]], [(D_INNER, BF16)], [])
    y_ssm = _matmul("ssm_out", yn, w["w_ssm_out"], "nn", F32)
    y_attn = _matmul("attn_out", ao, w["w_attn_out"], "nn", F32)

    def mix_gate(r0, ys, ya, gs, ga):
        return [_sigmoid(gs) * ys + _sigmoid(ga) * ya]

    gate_ins = [(p, D_MODEL, P_GATE // D_MODEL), (p, D_MODEL, P_GATE // D_MODEL + 1)]
    mixed, = _rowwise("mix_gate", mix_gate, n_rows, tm, [(y_ssm, D_MODEL, 0), (y_attn, D_MODEL, 0)] + gate_ins,
                      [], [(D_MODEL, BF16)], [])
    mix = _matmul("mix_out", mixed, w["w_mix_out"], "nn", F32)

    def post_mix(r0, mx, xt, hd, w_post, w_pre):
        h1 = jnp.where(_valid_rows(r0, mx.shape[0], PAD), h0_tile(r0, xt, hd) + _rms(mx, w_post), 0.0)
        return [h1, _rms(h1, w_pre)]

    h1, n2 = _rowwise("post_mix", post_mix, n_rows, tm, [(mix, D_MODEL, 0)] + x_in_tm,
                      [w["norm_post_mix"], w["norm_pre_ffn"]], [(D_MODEL, F32), (D_MODEL, BF16)], [])
    u_raw = _matmul("ffn_up", n2, w["w_ffn_up"], "nn", F32)
    f = _ffn_act("ffn_act", u_raw, w["ffn_conv_w"], w["ffn_conv_b"], n_rows)
    ffn = _matmul("ffn_down", f, w["w_ffn_down"], "nn", F32)

    def final(r0, fo, h, t, w_post):
        real = _valid_rows(r0, fo.shape[0], T)
        err = jnp.where(real, h + _rms(fo, w_post) - t, 0.0)
        dy = err * (1.0 / D_MODEL)
        dffn, dw = _rms_bwd(dy, fo, w_post)
        return [dffn, dy, jnp.sum(err * err, axis=0, keepdims=True), dw]

    dffn, dh2, loss_cols, g_norm_post_ffn = _rowwise(
        "loss_head", final, n_rows, tm, [(ffn, D_MODEL, 0), (h1, D_MODEL, 0), (tgt, D_MODEL, 0, "prev")],
        [w["norm_post_ffn"]], [(D_MODEL, BF16), (D_MODEL, F32)], [D_MODEL, D_MODEL])

    g["norm_post_ffn"] = g_norm_post_ffn
    g["w_ffn_down"] = _matmul("ffn_down_dw", f, dffn, "tn", F32)
    df = _matmul("ffn_down_dx", dffn, w["w_ffn_down"], "nt", F32)
    du_raw, dconv = _conv_bwd("ffn_act_bwd", u_raw, 0, [df], [(0, c0) for c0 in range(0, FFN_DIM, CONV_LANES)],
                              w["ffn_conv_w"], w["ffn_conv_b"], n_rows, True)
    g["ffn_conv_w"], g["ffn_conv_b"] = dconv[0:3], dconv[3:4]
    g["w_ffn_up"] = _matmul("ffn_up_dw", n2, du_raw, "tn", F32)
    dn2 = run("ffn_up_dx", _matmul, "ffn_up_dx", du_raw, w["w_ffn_up"], "nt", F32)

    def post_mix_bwd(r0, dn, d2, h, mx, w_pre, w_post):
        dx, dw_pre = _rms_bwd(dn, h, w_pre)
        dh1 = jnp.where(_valid_rows(r0, dn.shape[0], PAD), dx + d2, 0.0)
        dmix, dw_post = _rms_bwd(dh1, mx, w_post)
        return [dh1, dmix, dw_pre, dw_post]

    dh1, dmix, g["norm_pre_ffn"], g["norm_post_mix"] = _rowwise(
        "post_mix_bwd", post_mix_bwd, n_rows, tm,
        [(dn2, D_MODEL, 0), (dh2, D_MODEL, 0), (h1, D_MODEL, 0), (mix, D_MODEL, 0)],
        [w["norm_pre_ffn"], w["norm_post_mix"]], [(D_MODEL, F32), (D_MODEL, BF16)], [D_MODEL, D_MODEL])
    g["w_mix_out"] = _matmul("mix_out_dw", mixed, dmix, "tn", F32)
    dmixed = _matmul("mix_out_dx", dmix, w["w_mix_out"], "nt", F32)

    def mix_gate_bwd(r0, dm, ys, ya, gs, ga):
        ss, sa = _sigmoid(gs), _sigmoid(ga)
        dgate = jnp.concatenate([dm * ys * ss * (1.0 - ss), dm * ya * sa * (1.0 - sa)], axis=1)
        return [dm * ss, dm * sa, dgate]

    dys, dya, dp = _rowwise(
        "mix_gate_bwd", mix_gate_bwd, n_rows, tm,
        [(dmixed, D_MODEL, 0), (y_ssm, D_MODEL, 0), (y_attn, D_MODEL, 0)] + gate_ins,
        [], [(D_MODEL, BF16), (D_MODEL, BF16), (2 * D_MODEL, BF16, "new", P_W, P_GATE // (2 * D_MODEL))], [])
    g["w_ssm_out"] = _matmul("ssm_out_dw", yn, dys, "tn", F32)
    dyn = _matmul("ssm_out_dx", dys, w["w_ssm_out"], "nt", F32)
    g["w_attn_out"] = _matmul("attn_out_dw", ao, dya, "tn", F32)
    dao = _matmul("attn_out_dx", dya, w["w_attn_out"], "nt", BF16)

    def gate_norm_bwd(r0, dn, y, z, wn):
        sz, dsz = _silu_grad(z)
        dyz, dw = _rms_bwd(dn, y * sz, wn)
        live = _valid_rows(r0, dn.shape[0], PAD)
        return [jnp.where(live, dyz * sz, 0.0), jnp.where(live, dyz * y * dsz, 0.0), dw]

    dy_ssd, dp, g["ssm_norm"] = run(
        "ssm_gate_norm_bwd", _rowwise, "ssm_gate_norm_bwd", gate_norm_bwd, n_rows, tm,
        [(dyn, D_INNER, 0), (y_ssd, D_INNER, 0), (p, D_INNER, P_Z // D_INNER)],
        [w["ssm_norm"]], [(D_INNER, F32), (D_INNER, BF16, "into", dp, P_Z // D_INNER)], [D_INNER])
    dp, dsink = run("attn_bwd", _attn_bwd, p, sinks, ao, lse, dao, dp, n_chunks)
    g["attn_sinks"] = dsink[:, 0:ATTN_HEADS]
    dxs, dbm, dcm, dp, dpar = run("ssd_bwd", _ssd_bwd, p, w["ssm_conv_w"], w["ssm_conv_b"], dt_bias, a_log,
                                  d_skip, hin, dy_ssd, dp, n_chunks)
    g["ssm_dt_bias"], g["ssm_a_log"], g["ssm_d_skip"] = (dpar[i:i + 1, 0:SSM_HEADS] for i in range(3))
    x_chunks = [(src, c0) for src, arr in enumerate((dxs, dbm, dcm)) for c0 in range(0, arr.shape[1], CONV_LANES)]
    dp, dconv = run("ssm_conv_bwd", _conv_bwd, "ssm_conv_bwd", p, P_XBC // CONV_DIM, [dxs, dbm, dcm], x_chunks,
                    w["ssm_conv_w"], w["ssm_conv_b"], n_rows, False, into=dp, into_blk=P_XBC // CONV_DIM)
    g["ssm_conv_w"], g["ssm_conv_b"] = dconv[0:4], dconv[4:5]
    g["w_cat_t"] = _matmul("in_proj_dw", dp, n1, "tn", F32)
    dn1 = run("in_proj_dx", _matmul, "in_proj_dx", dp, w["w_cat"], "nt", F32)

    def pre_mix_bwd(r0, dn, d1, xt, hd, wn):
        dx, dw = _rms_bwd(dn, h0_tile(r0, xt, hd), wn)
        dh0 = jnp.where(_valid_rows(r0, dn.shape[0], PAD), dx + d1, 0.0)
        return [dh0, dh0, dw]

    dx_out, dhead, g["norm_pre_mix"] = _rowwise(
        "pre_mix_bwd", pre_mix_bwd, n_rows, T, [(dn1, D_MODEL, 0), (dh1, D_MODEL, 0)] + x_in,
        [w["norm_pre_mix"]], [(D_MODEL, F32, "prev", n_tok), (D_MODEL, F32, "first")], [D_MODEL])
    return jnp.sum(loss_cols), dx_out, dhead


_IN_SECTIONS = [((5152, 6176), P_Q), ((6176, 6432), P_K), ((6432, 6688), P_V), ((5120, 5152), P_DT),
                ((0, 2048), P_Z), ((6688, 8736), P_GATE), ((2048, 5120), P_XBC)]


IN_SHARD = N_IN // 4


def _shard_pieces(a, b):
    return [(j, max(a, j * IN_SHARD) - j * IN_SHARD, min(b, (j + 1) * IN_SHARD) - j * IN_SHARD)
            for j in range(4) if max(a, j * IN_SHARD) < min(b, (j + 1) * IN_SHARD)]


def _to_cat(w4):
    parts, at = [], 0
    for (a, b), off in _IN_SECTIONS:
        if off > at:
            parts.append(jnp.zeros((w4.shape[1], off - at), w4.dtype))
        parts += [w4[j, :, lo:hi] for j, lo, hi in _shard_pieces(a, b)]
        at = off + (b - a)
    return jnp.concatenate(parts, axis=1)


def _from_cat_t(g_cat_t):
    shards = [[] for _ in range(4)]
    for (a, b), off in sorted(_IN_SECTIONS):
        for j, lo, hi in _shard_pieces(a, b):
            start = off + j * IN_SHARD + lo - a
            shards[j].append(g_cat_t[start:start + hi - lo])
    return jnp.stack([jnp.concatenate(s, axis=0) for s in shards])


LANES = 1024
_BIG = [("w_in", 1024, 2184, "chip"), ("w_ssm_out", 512, 1024, "row"), ("w_attn_out", 256, 1024, "row"),
        ("w_mix_out", 256, 1024, "row"), ("w_ffn_up", 1024, 1408, "col"), ("w_ffn_down", 704, 1024, "row"),
        ("small", 32, LANES, "chip")]
_SMALL_SHARDED = [("ssm_conv_w", (4, 768), 1), ("ffn_conv_w", (3, 1408), 1), ("meta_tokens", (16, 256), 1)]
_REPLICATED = [("norm_pre_mix", 1024), ("ssm_conv_b", 3072), ("ssm_dt_bias", 32), ("ssm_a_log", 32),
               ("ssm_d_skip", 32), ("ssm_norm", 2048), ("attn_sinks", 16), ("norm_post_mix", 1024),
               ("norm_pre_ffn", 1024), ("ffn_conv_b", 5632), ("norm_post_ffn", 1024)]
SMALL_ROWS = 24


def _rep_rows():
    out, at = [], 0
    for _, width in _REPLICATED:
        out.append((at, -(-width // LANES)))
        at += out[-1][1]
    return out, at


def _in_rows(parts):
    rows = [jnp.pad(a, ((0, 0), (0, -a.shape[1] % LANES))).reshape(-1, LANES) for a in parts]
    flat = jnp.concatenate(rows, axis=0)
    return jnp.pad(flat, ((0, SMALL_ROWS - flat.shape[0]), (0, 0)))
WEIGHT_ORDER = ["meta_tokens", "norm_pre_mix", "w_in", "ssm_conv_w", "ssm_conv_b", "ssm_dt_bias", "ssm_a_log",
                "ssm_d_skip", "ssm_norm", "w_ssm_out", "attn_sinks", "w_attn_out", "w_mix_out", "norm_post_mix",
                "norm_pre_ffn", "w_ffn_up", "ffn_conv_w", "ffn_conv_b", "w_ffn_down", "norm_post_ffn"]


def _flatten(parts, rows):
    flat = jnp.concatenate([a.reshape(-1) for a in parts])
    return jnp.pad(flat, (0, rows * LANES - flat.shape[0])).reshape(rows, LANES)


def _unflatten(flat, shapes):
    flat = flat.reshape(-1)
    out, off = [], 0
    for shp in shapes:
        n = math.prod(shp)
        out.append(flat[off:off + n].reshape(shp))
        off += n
    return out


def _shard_of(full, chip, shape, axis):
    return lax.slice_in_dim(full, chip * shape[axis], (chip + 1) * shape[axis], axis=axis)


def _full_shape(r, c, layout):
    return {"row": (4 * r, c), "col": (r, 4 * c), "chip": (4, r, c), "chip_cols": (4, r, c)}[layout]


def _half_shape(r, c, layout):
    return (r, c // 2) if layout == "chip_cols" else (r // 2, c)


def _shard_view(ref, r, c, layout, chip):
    if layout == "row":
        return ref.at[pl.ds(pl.multiple_of(chip * r, 16), r), :]
    if layout == "col":
        return ref.at[:, pl.ds(pl.multiple_of(chip * c, 128), c)]
    return ref.at[chip]


def _half_view(ref, r, c, layout, chip, half):
    if layout == "chip_cols":
        return ref.at[chip, :, pl.ds(pl.multiple_of(half * (c // 2), 128), c // 2)]
    hr = r // 2
    if layout == "row":
        return ref.at[pl.ds(pl.multiple_of(chip * r + half * hr, 16), hr), :]
    r0 = pl.multiple_of(half * hr, 16)
    if layout == "col":
        return ref.at[pl.ds(r0, hr), pl.ds(pl.multiple_of(chip * c, 128), c)]
    return ref.at[chip, pl.ds(r0, hr), :]


def _mesh_pos():
    return lax.axis_index("x"), lax.axis_index("y"), lax.axis_index("c")


def _other_chips(x, y):
    return [(1 - x, y), (x, 1 - y), (1 - x, 1 - y)]


def _chip_index(x, y):
    return 2 * x + y


def _run_exchange(name, ex):
    n_in, n_out = len(ex.ins), len(ex.out_shapes)

    def body(*refs):
        in_refs, out_refs = refs[:n_in], refs[n_in:n_in + n_out]
        send_sems, recv_sems = refs[n_in + n_out:]
        copies = [pltpu.make_async_remote_copy(src_ref=s, dst_ref=d, send_sem=send_sems.at[i], recv_sem=recv_sems.at[i],
                                               device_id=dev, device_id_type=MESH)
                  for i, (s, d, dev) in enumerate(ex.make_copies(in_refs, out_refs))]
        assert len(copies) == ex.n_copies
        for cp in copies:
            cp.start()
        for cp in copies:
            cp.wait()

    return pl.pallas_call(
        body, name=name, in_specs=[ANY] * n_in, out_specs=[ANY] * n_out, out_shape=list(ex.out_shapes),
        scratch_shapes=[pltpu.SemaphoreType.DMA((ex.n_copies,)), pltpu.SemaphoreType.DMA((ex.n_copies,))],
        compiler_params=pltpu.CompilerParams(has_side_effects=True),
    )(*ex.ins)


def _join(*exs):
    def make(in_refs, out_refs):
        copies, i0, o0 = [], 0, 0
        for ex in exs:
            copies += ex.make_copies(in_refs[i0:i0 + len(ex.ins)], out_refs[o0:o0 + len(ex.out_shapes)])
            i0, o0 = i0 + len(ex.ins), o0 + len(ex.out_shapes)
        return copies

    aliases, i0, o0 = {}, 0, 0
    for ex in exs:
        aliases.update({i0 + k: o0 + v for k, v in ex.aliases.items()})
        i0, o0 = i0 + len(ex.ins), o0 + len(ex.out_shapes)
    return _Exchange([a for ex in exs for a in ex.ins], [s for ex in exs for s in ex.out_shapes], make,
                     sum(ex.n_copies for ex in exs), aliases)


def _split(exs, results):
    out, o0 = [], 0
    for ex in exs:
        out.append(list(results[o0:o0 + len(ex.out_shapes)]))
        o0 += len(ex.out_shapes)
    return out


def _gather_ici(entries, shards):
    def make(in_refs, out_refs):
        x, y, c = _mesh_pos()
        j = _chip_index(x, y)
        copies = []
        for ref_in, ref_out, (_, r, cc, lay) in zip(in_refs, out_refs, entries):
            copies.append((ref_in, _shard_view(ref_out, r, cc, lay, j), None))
            mine = ref_in.at[pl.ds(pl.multiple_of(c * (r // 2), 16), r // 2), :]
            copies += [(mine, _half_view(ref_out, r, cc, lay, j, c), (*ch, c)) for ch in _other_chips(x, y)]
        return copies

    shapes = [jax.ShapeDtypeStruct(_full_shape(r, cc, lay), s.dtype) for s, (_, r, cc, lay) in zip(shards, entries)]
    return _Exchange(list(shards), shapes, make, 4 * len(entries))


def _gather_pass_on(entries, fulls):
    def make(in_refs, out_refs):
        x, y, c = _mesh_pos()
        copies = []
        for ref, (_, r, cc, lay) in zip(out_refs, entries):
            for ch in _other_chips(x, y):
                landed = _half_view(ref, r, cc, lay, _chip_index(*ch), c)
                copies.append((landed, landed, (x, y, 1 - c)))
        return copies

    return _Exchange(list(fulls), [jax.ShapeDtypeStruct(f.shape, f.dtype) for f in fulls], make, 3 * len(entries),
                     {a: a for a in range(len(entries))})


def _gather_weights(entries, shards):
    n = len(entries)

    def body(*refs):
        ins, outs = refs[:n], refs[n:2 * n]
        send_sems, recv_sems, local_sems = refs[2 * n:]
        x, y, c = _mesh_pos()
        j = _chip_index(x, y)
        sibling = (x, y, 1 - c)
        chips = _other_chips(x, y)
        idx = [_chip_index(*ch) for ch in chips]

        def remote(k, src, dst, dev):
            return pltpu.make_async_remote_copy(src_ref=src, dst_ref=dst, send_sem=send_sems.at[k],
                                                recv_sem=recv_sems.at[k], device_id=dev, device_id_type=MESH)

        own = [pltpu.make_async_copy(ins[a], _shard_view(outs[a], r, cc, lay, j), local_sems.at[a])
               for a, (_, r, cc, lay) in enumerate(entries)]
        for cp in own:
            cp.start()
        via = jnp.where(c == 0, idx[0], idx[1])
        onward = (jnp.where(c == 0, x, 1 - x), jnp.where(c == 0, 1 - y, y), c)
        first, relays, passed = [], [], []
        for a, (_, r, cc, lay) in enumerate(entries):
            mine = ins[a].at[pl.ds(pl.multiple_of(c * (r // 2), 16), r // 2), :]
            for k, ch in enumerate(chips[:2]):
                first.append(remote(6 * a + k, mine, _half_view(outs[a], r, cc, lay, j, c), (*ch, c)))
            relayed = _half_view(outs[a], r, cc, lay, via, c)
            relays.append(remote(6 * a + 2, relayed, relayed, onward))
            for k in range(3):
                landed = _half_view(outs[a], r, cc, lay, idx[k], c)
                passed.append(remote(6 * a + 3 + k, landed, landed, sibling))
        for cp in first:
            cp.start()
        for a, (_, r, cc, lay) in enumerate(entries):
            for k in range(3):
                landed = _half_view(outs[a], r, cc, lay, idx[k], c)
                remote(6 * a + k, landed, landed, sibling).wait_recv()
                passed[3 * a + k].start()
                if k == 1:
                    relays[a].start()
        first = first + relays
        for a, (_, r, cc, lay) in enumerate(entries):
            for k in range(3):
                theirs = _half_view(outs[a], r, cc, lay, idx[k], 1 - c)
                remote(6 * a + 3 + k, theirs, theirs, sibling).wait_recv()
        for cp in first + passed:
            cp.wait_send()
        for cp in own:
            cp.wait()

    return pl.pallas_call(
        body, name="gather_weights", in_specs=[ANY] * n, out_specs=[ANY] * n,
        out_shape=[jax.ShapeDtypeStruct(_full_shape(r, cc, lay), s.dtype) for s, (_, r, cc, lay) in zip(shards, entries)],
        scratch_shapes=[pltpu.SemaphoreType.DMA((6 * n,)), pltpu.SemaphoreType.DMA((6 * n,)), pltpu.SemaphoreType.DMA((n,))],
        compiler_params=pltpu.CompilerParams(has_side_effects=True),
    )(*shards)


def _pair_exchange(entries, grads):
    def make(in_refs, out_refs):
        x, y, c = _mesh_pos()
        return [(_half_view(ref_in, r, cc, lay, i, 1 - c), ref_out.at[i], (x, y, 1 - c))
                for ref_in, ref_out, (_, r, cc, lay) in zip(in_refs, out_refs, entries) for i in range(4)]

    return _Exchange(list(grads), [jax.ShapeDtypeStruct((4,) + _half_shape(r, cc, lay), F32) for _, r, cc, lay in entries],
                     make, 4 * len(entries))


def _whole_to_sibling(arrays):
    def make(in_refs, out_refs):
        x, y, c = _mesh_pos()
        return [(r, o, (x, y, 1 - c)) for r, o in zip(in_refs, out_refs)]

    return _Exchange(list(arrays), [jax.ShapeDtypeStruct(a.shape, a.dtype) for a in arrays], make, len(arrays))


def _chip_exchange(psends):
    def make(in_refs, out_refs):
        x, y, c = _mesh_pos()
        return [(ref_in.at[_chip_index(*ch)], ref_out.at[k], (*ch, c))
                for ref_in, ref_out in zip(in_refs, out_refs) for k, ch in enumerate(_other_chips(x, y))]

    return _Exchange(list(psends), [jax.ShapeDtypeStruct((3,) + p.shape[1:], p.dtype) for p in psends], make,
                     3 * len(psends))


def _to_all_chips(array):
    def make(in_refs, out_refs):
        x, y, c = _mesh_pos()
        return [(in_refs[0], out_refs[0].at[k], (*ch, c)) for k, ch in enumerate(_other_chips(x, y))]

    return _Exchange([array], [jax.ShapeDtypeStruct((3,) + array.shape, array.dtype)], make, 3)


SUM_ROWS = 512
ADAM_ROWS = 256


def _pair_sum(name, grad, recv, ids, r, c, layout):
    hr, c = _half_shape(r, c, layout)
    tr = _row_tile(hr, SUM_ROWS)
    nb = hr // tr

    def body(ids_ref, g_ref, r_ref, send_ref, own_ref):
        s = g_ref[...] + r_ref[...]
        send_ref[...] = s.astype(send_ref.dtype)

        @pl.when(pl.program_id(1) == ids_ref[1])
        def _():
            own_ref[...] = s

    if layout == "row":
        g_spec = pl.BlockSpec((tr, c), lambda t, j, ids_ref: ((j * r + ids_ref[0] * hr) // tr + t, 0))
    elif layout == "col":
        g_spec = pl.BlockSpec((tr, c), lambda t, j, ids_ref: (ids_ref[0] * nb + t, j))
    elif layout == "chip_cols":
        g_spec = pl.BlockSpec((None, tr, c), lambda t, j, ids_ref: (j, t, ids_ref[0]))
    else:
        g_spec = pl.BlockSpec((None, tr, c), lambda t, j, ids_ref: (j, ids_ref[0] * nb + t, 0))
    grid_spec = pltpu.PrefetchScalarGridSpec(
        num_scalar_prefetch=1, grid=(nb, 4),
        in_specs=[g_spec, pl.BlockSpec((None, tr, c), lambda t, j, ids_ref: (j, t, 0))],
        out_specs=[pl.BlockSpec((None, tr, c), lambda t, j, ids_ref: (j, t, 0)),
                   pl.BlockSpec((tr, c), lambda t, j, ids_ref: (t, 0))])
    return pl.pallas_call(
        body, name=name, grid_spec=grid_spec,
        out_shape=[jax.ShapeDtypeStruct((4, hr, c), BF16), jax.ShapeDtypeStruct((hr, c), F32)],
        compiler_params=_cparams(2),
    )(ids, grad, recv)


def _chip_sum(name, own, recv):
    hr, c = own.shape
    tr = _row_tile(hr, SUM_ROWS)

    def body(o_ref, r_ref, out_ref):
        out_ref[...] = ((o_ref[...] + r_ref[0].astype(F32)) + r_ref[1].astype(F32)) + r_ref[2].astype(F32)

    return pl.pallas_call(
        body, name=name, grid=(hr // tr,),
        in_specs=[pl.BlockSpec((tr, c), lambda i: (i, 0)), pl.BlockSpec((3, tr, c), lambda i: (0, i, 0))],
        out_specs=pl.BlockSpec((tr, c), lambda i: (i, 0)),
        out_shape=jax.ShapeDtypeStruct((hr, c), F32), compiler_params=_cparams(1),
    )(own, recv)


def _chip_sum_small(own, recv, ids):
    def body(ids_ref, o_ref, r_ref, out_ref):
        j = ids_ref[1]
        total = None
        for i in range(4):
            m = jnp.bitwise_xor(i, j)
            term = jnp.where(m == 0, o_ref[...], jnp.where(m == 2, r_ref[0], jnp.where(m == 1, r_ref[1], r_ref[2])))
            total = term if total is None else total + term
        out_ref[...] = total

    grid_spec = pltpu.PrefetchScalarGridSpec(
        num_scalar_prefetch=1, grid=(1,),
        in_specs=[pl.BlockSpec(own.shape, lambda i, ids_ref: (0, 0)), pl.BlockSpec(recv.shape, lambda i, ids_ref: (0, 0, 0))],
        out_specs=pl.BlockSpec(own.shape, lambda i, ids_ref: (0, 0)))
    return pl.pallas_call(body, name="chip_sum_small", grid_spec=grid_spec,
                          out_shape=jax.ShapeDtypeStruct(own.shape, F32), compiler_params=_cparams(1))(ids, own, recv)


def _adamw(name, w, m, v, mine, theirs, ids):
    lead = (None,) * (w.ndim - 2)
    rows, cols = w.shape[-2:]
    half = rows // 2
    tr = _row_tile(half, ADAM_ROWS, unit=8)
    nb = half // tr
    c1 = 1.0 / (1.0 - ADAM_B1 ** ADAM_STEP)
    c2 = 1.0 / (1.0 - ADAM_B2 ** ADAM_STEP)

    def body(ids_ref, w_ref, m_ref, v_ref, mine_ref, theirs_ref, g_out, d_out, m_out, v_out):
        g = jnp.where(pl.program_id(0) == ids_ref[0], mine_ref[...], theirs_ref[...])
        m_new = ADAM_B1 * m_ref[...] + (1.0 - ADAM_B1) * g
        v_new = ADAM_B2 * v_ref[...] + (1.0 - ADAM_B2) * (g * g)
        d_out[...] = -ADAM_LR * ((m_new * c1) / (jnp.sqrt(v_new * c2) + ADAM_EPS) + ADAM_WD * w_ref[...])
        g_out[...] = g
        m_out[...] = m_new
        v_out[...] = v_new

    full = pl.BlockSpec(lead + (tr, cols), lambda h, i, ids_ref: (0,) * len(lead) + (h * nb + i, 0))
    part = pl.BlockSpec((tr, cols), lambda h, i, ids_ref: (i, 0))
    grid_spec = pltpu.PrefetchScalarGridSpec(num_scalar_prefetch=1, grid=(2, nb),
                                             in_specs=[full, full, full, part, part], out_specs=[full] * 4)
    return pl.pallas_call(
        body, name=name, grid_spec=grid_spec,
        out_shape=[jax.ShapeDtypeStruct(w.shape, F32)] * 4, compiler_params=_cparams(2),
    )(ids, w, m, v, mine, theirs)


def _adamw_whole(name, w, m, v, g):
    rows, cols = w.shape[-2:]
    tr = _row_tile(rows, 2 * ADAM_ROWS, unit=8)
    c1 = 1.0 / (1.0 - ADAM_B1 ** ADAM_STEP)
    c2 = 1.0 / (1.0 - ADAM_B2 ** ADAM_STEP)

    def body(w_ref, m_ref, v_ref, g_ref, g_out, d_out, m_out, v_out):
        g = g_ref[...]
        m_new = ADAM_B1 * m_ref[...] + (1.0 - ADAM_B1) * g
        v_new = ADAM_B2 * v_ref[...] + (1.0 - ADAM_B2) * (g * g)
        d_out[...] = -ADAM_LR * ((m_new * c1) / (jnp.sqrt(v_new * c2) + ADAM_EPS) + ADAM_WD * w_ref[...])
        g_out[...] = g
        m_out[...] = m_new
        v_out[...] = v_new

    full = pl.BlockSpec((None, tr, cols), lambda i: (0, i, 0))
    return pl.pallas_call(
        body, name=name, grid=(rows // tr,), in_specs=[full, full, full, pl.BlockSpec((tr, cols), lambda i: (i, 0))],
        out_specs=[full] * 4, out_shape=[jax.ShapeDtypeStruct(w.shape, F32)] * 4, compiler_params=_cparams(1),
    )(w, m, v, g)


def _adamw_replicated(g_rows, ws, ms, vs):
    n = len(ws)
    layout, _ = _rep_rows()
    c1 = 1.0 / (1.0 - ADAM_B1 ** ADAM_STEP)
    c2 = 1.0 / (1.0 - ADAM_B2 ** ADAM_STEP)

    def body(g_ref, *refs):
        w_refs, m_refs, v_refs = refs[0:n], refs[n:2 * n], refs[2 * n:3 * n]
        outs = refs[3 * n:]
        for k, (r0, rows) in enumerate(layout):
            width = w_refs[k].shape[1]
            g = jnp.concatenate([g_ref[r0 + j:r0 + j + 1, :] for j in range(rows)], axis=1)[:, 0:width]
            m_new = ADAM_B1 * m_refs[k][...] + (1.0 - ADAM_B1) * g
            v_new = ADAM_B2 * v_refs[k][...] + (1.0 - ADAM_B2) * (g * g)
            outs[k][...] = g
            outs[n + k][...] = -ADAM_LR * ((m_new * c1) / (jnp.sqrt(v_new * c2) + ADAM_EPS) + ADAM_WD * w_refs[k][...])
            outs[2 * n + k][...] = m_new
            outs[3 * n + k][...] = v_new

    res = pl.pallas_call(body, name="adamw_replicated",
                         out_shape=[jax.ShapeDtypeStruct(w.shape, F32) for _ in range(4) for w in ws])(g_rows, *ws, *ms, *vs)
    return [res[k * n:(k + 1) * n] for k in range(4)]


def _small_shard(parts):
    return _flatten(parts, _BIG[-1][1])


_ENTRY = {e[0]: e for e in _BIG}
_GRAD_ENTRY = {**_ENTRY, "w_in": ("w_in", IN_SHARD, D_MODEL, "chip_cols")}
FFN_MATS = ("w_ffn_down", "w_ffn_up")
MIXER_MATS = ("w_mix_out", "w_ssm_out", "w_attn_out")


class _StepPlan:
    def __init__(self, w, late_shards, shards, ids):
        self.w, self.g = w, {}
        self.late_shards, self.shards, self.ids = late_shards, shards, ids
        self.sums, self.halves, self.results = {}, {}, {}

    def run(self, name, fn, *args, **kw):
        at = getattr(self, "_at_" + name, None)
        if at is None:
            return fn(*args, **kw)
        exchange, landed = at()
        res, extra = fn(*args, bg=exchange, **kw)
        landed(extra)
        return res

    def _at_ssd_fwd(self):
        def landed(fulls):
            self.partly_gathered = fulls

        return _gather_ici([_ENTRY[n] for n in MIXER_MATS], [self.late_shards[n] for n in MIXER_MATS]), landed

    def _at_attn_fwd(self):
        stages = (_gather_pass_on([_ENTRY[n] for n in MIXER_MATS], self.partly_gathered),
                  _gather_ici([_ENTRY[n] for n in FFN_MATS], [self.late_shards[n] for n in FFN_MATS]))

        def landed(extra):
            mixer, self.partly_gathered = _split(stages, extra)
            self.w.update(zip(MIXER_MATS, mixer))

        return _join(*stages), landed

    def _at_ssm_gate_norm(self):
        return (_gather_pass_on([_ENTRY[n] for n in FFN_MATS], self.partly_gathered),
                lambda fulls: self.w.update(zip(FFN_MATS, fulls)))

    def pair_sums(self, names, grads, recv):
        for n, gr, rv in zip(names, grads, recv):
            _, r, c, lay = _GRAD_ENTRY[n]
            self.sums[n] = _pair_sum("pair_sum_" + n, gr, rv, self.ids, r, c, lay)

    def chip_sums(self, names, recv):
        for n, rv in zip(names, recv):
            self.halves[n] = _chip_sum("chip_sum_" + n, self.sums[n][1], rv)

    def adamw(self, names, theirs):
        for n, th in zip(names, theirs):
            sh = self.shards[n]
            if n == "w_in":
                mine_first = self.ids[0] == 0
                g_t = jnp.where(mine_first, jnp.concatenate([self.halves[n], th], axis=1),
                                jnp.concatenate([th, self.halves[n]], axis=1))
                res = _adamw_whole("adamw_" + n, *[jnp.swapaxes(sh[k], -1, -2) for k in ("w", "m", "v")], g_t)
                self.results[n] = [jnp.swapaxes(r, -1, -2) for r in res]
            else:
                self.results[n] = _adamw("adamw_" + n, sh["w"], sh["m"], sh["v"], self.halves[n], th, self.ids)

    def _pair_stage(self, names, grads):
        return (_pair_exchange([_GRAD_ENTRY[n] for n in names], grads),
                lambda recv: self.pair_sums(names, grads, recv))

    def _at_ffn_up_dx(self):
        return self._pair_stage(FFN_MATS, [self.g[n] for n in FFN_MATS])

    def _at_ssm_gate_norm_bwd(self):
        return self._pair_stage(MIXER_MATS, [self.g[n] for n in MIXER_MATS])

    def _at_attn_bwd(self):
        return _chip_exchange([self.sums[n][0] for n in FFN_MATS]), lambda recv: self.chip_sums(FFN_MATS, recv)

    def _at_ssd_bwd(self):
        stages = (_chip_exchange([self.sums[n][0] for n in MIXER_MATS]),
                  _whole_to_sibling([self.halves[n] for n in FFN_MATS]))

        def landed(extra):
            recv, theirs = _split(stages, extra)
            self.chip_sums(MIXER_MATS, recv)
            self.adamw(FFN_MATS, theirs)

        return _join(*stages), landed

    def _at_ssm_conv_bwd(self):
        return _whole_to_sibling([self.halves[n] for n in MIXER_MATS]), lambda theirs: self.adamw(MIXER_MATS, theirs)

    def _at_in_proj_dx(self):
        grads = [_from_cat_t(self.g.pop("w_cat_t"))]
        self.pair_sums(("w_in",), grads,
                       _run_exchange("grad_pair_exchange_w_in", _pair_exchange([_GRAD_ENTRY["w_in"]], grads)))
        return _chip_exchange([self.sums["w_in"][0]]), lambda recv: self.chip_sums(("w_in",), recv)

    def finish(self, g_small, g_rep, rep_shards):
        stages = (_pair_exchange([_ENTRY["small"]], [g_small]), _whole_to_sibling([g_rep]))
        recv_small, recv_rep = _split(stages, _run_exchange("grad_pair_exchange_tail", _join(*stages)))
        self.pair_sums(("small",), [g_small], recv_small)
        p_rep, = _rowwise("pair_sum_replicated", lambda r0, a, b: [a + b], SMALL_ROWS, SMALL_ROWS,
                          [(g_rep, LANES, 0), (recv_rep[0], LANES, 0)], [], [(LANES, F32)], [])
        stages = (_chip_exchange([self.sums["small"][0]]), _to_all_chips(p_rep))
        recv, recv_rep = _split(stages, _run_exchange("grad_chip_exchange_tail", _join(*stages)))
        self.chip_sums(("small",), recv)
        g_rep_tot = _chip_sum_small(p_rep, recv_rep[0], self.ids)
        last = ("w_in", "small")
        self.adamw(last, _run_exchange("grad_half_share_tail", _whole_to_sibling([self.halves[n] for n in last])))
        self.results["replicated"] = _adamw_replicated(g_rep_tot, rep_shards["w"], rep_shards["m"], rep_shards["v"])
        return g_rep_tot[_rep_rows()[1], 0]


def kernel(x, meta_tokens, norm_pre_mix, w_in, ssm_conv_w, ssm_conv_b, ssm_dt_bias, ssm_a_log, ssm_d_skip, ssm_norm, w_ssm_out, attn_sinks, w_attn_out, w_mix_out, norm_post_mix, norm_pre_ffn, w_ffn_up, ffn_conv_w, ffn_conv_b, w_ffn_down, norm_post_ffn, loss_target, m_meta_tokens, m_norm_pre_mix, m_w_in, m_ssm_conv_w, m_ssm_conv_b, m_ssm_dt_bias, m_ssm_a_log, m_ssm_d_skip, m_ssm_norm, m_w_ssm_out, m_attn_sinks, m_w_attn_out, m_w_mix_out, m_norm_post_mix, m_norm_pre_ffn, m_w_ffn_up, m_ffn_conv_w, m_ffn_conv_b, m_w_ffn_down, m_norm_post_ffn, v_meta_tokens, v_norm_pre_mix, v_w_in, v_ssm_conv_w, v_ssm_conv_b, v_ssm_dt_bias, v_ssm_a_log, v_ssm_d_skip, v_ssm_norm, v_w_ssm_out, v_attn_sinks, v_w_attn_out, v_w_mix_out, v_norm_post_mix, v_norm_pre_ffn, v_w_ffn_up, v_ffn_conv_w, v_ffn_conv_b, v_w_ffn_down, v_norm_post_ffn):
    args = dict(locals())
    squeeze = lambda a: a.reshape(a.shape[-2:])
    wts = {n: squeeze(args[n]) for n in WEIGHT_ORDER}
    mom = {n: squeeze(args["m_" + n]) for n in WEIGHT_ORDER}
    var = {n: squeeze(args["v_" + n]) for n in WEIGHT_ORDER}
    x_i, y_i, c_i = _mesh_pos()
    ids = jnp.stack([c_i, _chip_index(x_i, y_i)]).astype(jnp.int32)
    big_names = [n for n, _, _, _ in _BIG[:-1]]
    small_names = [n for n, _, _ in _SMALL_SHARDED]
    rep_names = [n for n, _ in _REPLICATED]

    stacks = {"w": wts, "m": mom, "v": var}
    shards = {n: {"w": args[n], "m": args["m_" + n], "v": args["v_" + n]} for n in big_names}
    shards["small"] = {k: _small_shard([d[n] for n in small_names]) for k, d in stacks.items()}
    rep_shards = {k: [d[n] for n in rep_names] for k, d in stacks.items()}

    w_in4, small_all = _gather_weights([_ENTRY["w_in"], _ENTRY["small"]], [wts["w_in"].astype(BF16), shards["small"]["w"]])
    w = {n: wts[n] for n in rep_names}
    w["w_cat"] = _to_cat(w_in4)
    small_parts = [_unflatten(small_all[i], [shp for _, shp, _ in _SMALL_SHARDED]) for i in range(4)]
    for k, (n, _, axis) in enumerate(_SMALL_SHARDED):
        w[n] = jnp.concatenate([small_parts[i][k] for i in range(4)], axis=axis)
    plan = _StepPlan(w, {n: wts[n].astype(BF16) for n in MIXER_MATS + FFN_MATS}, shards, ids)

    head = jnp.concatenate([jnp.zeros((PAD, D_MODEL), F32), w["meta_tokens"]], axis=0)
    loss_sum, dx, dhead = _local_step(x[0], head, loss_target[0], plan)
    g = plan.g
    g["meta_tokens"] = dhead[PAD:]
    g_small = jnp.stack([_small_shard([_shard_of(g[n], i, shp, ax) for n, shp, ax in _SMALL_SHARDED]) for i in range(4)])
    loss_part = (loss_sum * (0.5 / D_MODEL)).reshape(1, 1)
    loss = plan.finish(g_small, _in_rows([g[n] for n in rep_names] + [loss_part]), rep_shards)

    results = {}
    for kind in range(4):
        results.update({(kind, n): plan.results[n][kind] for n in big_names})
        parts = _unflatten(plan.results["small"][kind], [shp for _, shp, _ in _SMALL_SHARDED])
        results.update({(kind, n): parts[k] for k, n in enumerate(small_names)})
        results.update({(kind, n): plan.results["replicated"][kind][k] for k, n in enumerate(rep_names)})
    outs = [results[kind, n].reshape(args[n].shape) for kind in range(4) for n in WEIGHT_ORDER]
    return (loss, dx[None], *outs)
```

```python
import math
from typing import Any, Callable, NamedTuple, Sequence

import jax
import jax.numpy as jnp
from jax import lax
from jax.experimental import pallas as pl
from jax.experimental.pallas import tpu as pltpu

F32 = jnp.float32
BF16 = jnp.bfloat16

D_MODEL = 1024
N_META = 16
T = 128
PAD = T - N_META
D_INNER = 2048
SSM_HEADS = 32
HEAD_P = 64
SSM_GROUPS = 4
GROUP_W = D_INNER // SSM_GROUPS
D_STATE = 128
CONV_DIM = D_INNER + 2 * SSM_GROUPS * D_STATE
ATTN_HEADS = 16
KV_HEADS = 4
ATTN_W = 1024
KV_W = 256
FFN_DIM = 2816
N_IN = 8736
EPS = 1e-6
NEG = -1e30
SCALE = 0.125

P_Q, P_K, P_V, P_DT, P_Z, P_GATE, P_XBC = 0, 1024, 1280, 1536, 2048, 4096, 6144
QKV_W = 1536
P_W = 9216

ADAM_LR, ADAM_B1, ADAM_B2, ADAM_EPS, ADAM_WD, ADAM_STEP = 0.001, 0.9, 0.999, 1e-08, 0.01, 10

VMEM_BUDGET = 40 * 1024 * 1024
VMEM_LIMIT = 56 * 1024 * 1024
MESH = pl.DeviceIdType.MESH
ANY = pl.BlockSpec(memory_space=pl.ANY)


def _cparams(n_axes, **kw):
    return pltpu.CompilerParams(dimension_semantics=("arbitrary",) * n_axes, vmem_limit_bytes=VMEM_LIMIT, **kw)


class _Exchange(NamedTuple):
    ins: Sequence[Any]
    out_shapes: Sequence[Any]
    make_copies: Callable
    n_copies: int
    aliases: dict = {}


def _call(body, name, grid, in_specs, out_specs, out_shape, operands, scratch_shapes=(), aliases=None, bg=None):
    aliases = dict(aliases or {})
    if bg is None:
        return pl.pallas_call(body, name=name, grid=grid, in_specs=in_specs, out_specs=out_specs, out_shape=out_shape,
                              scratch_shapes=list(scratch_shapes), input_output_aliases=aliases,
                              compiler_params=_cparams(len(grid)))(*operands)
    n_in, n_out, n_scr = len(in_specs), len(out_specs), len(scratch_shapes)
    nb_in, nb_out = len(bg.ins), len(bg.out_shapes)

    def hosted(*refs):
        ins, bg_ins = refs[:n_in], refs[n_in:n_in + nb_in]
        outs = refs[n_in + nb_in:n_in + nb_in + n_out]
        bg_outs = refs[n_in + nb_in + n_out:n_in + nb_in + n_out + nb_out]
        scratch = refs[n_in + nb_in + n_out + nb_out:n_in + nb_in + n_out + nb_out + n_scr]
        send_sems, recv_sems = refs[-2:]
        pids = [pl.program_id(a) for a in range(len(grid))]
        first, last = pids[0] == 0, pids[0] == grid[0] - 1
        for p, g in zip(pids[1:], grid[1:]):
            first, last = first & (p == 0), last & (p == g - 1)
        copies = []
        for k, (src, dst, peer) in enumerate(bg.make_copies(bg_ins, bg_outs)):
            if peer is None:
                copies.append(pltpu.make_async_copy(src, dst, send_sems.at[k]))
            else:
                copies.append(pltpu.make_async_remote_copy(src_ref=src, dst_ref=dst, send_sem=send_sems.at[k],
                                                           recv_sem=recv_sems.at[k], device_id=peer, device_id_type=MESH))
        assert len(copies) == bg.n_copies

        @pl.when(first)
        def _():
            for cp in copies:
                cp.start()

        body(*ins, *outs, *scratch)

        @pl.when(last)
        def _():
            for cp in copies:
                cp.wait()

    aliases = {(k if k < n_in else k + nb_in): v for k, v in aliases.items()}
    aliases.update({n_in + k: n_out + v for k, v in bg.aliases.items()})
    res = pl.pallas_call(
        hosted, name=name, grid=grid, in_specs=list(in_specs) + [ANY] * nb_in, out_specs=list(out_specs) + [ANY] * nb_out,
        out_shape=list(out_shape) + list(bg.out_shapes), input_output_aliases=aliases,
        scratch_shapes=list(scratch_shapes) + [pltpu.SemaphoreType.DMA((bg.n_copies,))] * 2,
        compiler_params=_cparams(len(grid), has_side_effects=True))(*operands, *bg.ins)
    return res[:n_out], res[n_out:]


def _sigmoid(x):
    return 1.0 / (1.0 + jnp.exp(-x))


def _silu(x):
    return x * _sigmoid(x)


def _silu_grad(x):
    s = _sigmoid(x)
    return x * s, s * (1.0 + x * (1.0 - s))


def _dsilu(x):
    return _silu_grad(x)[1]


def _softplus(x):
    e = jnp.exp(-jnp.abs(x))
    small = e * (1.0 - e * (0.5 - e * (1.0 / 3.0)))
    return jnp.maximum(x, 0.0) + jnp.where(e < 0.01, small, jnp.log(1.0 + e))


def _rms(x, w):
    r = lax.rsqrt(jnp.mean(x * x, axis=-1, keepdims=True) + EPS)
    return x * r * w


def _rms_bwd(dy, x, w):
    r = lax.rsqrt(jnp.mean(x * x, axis=-1, keepdims=True) + EPS)
    xh = x * r
    g = dy * w
    dx = r * (g - xh * jnp.mean(g * xh, axis=-1, keepdims=True))
    dw = jnp.sum(dy * xh, axis=0, keepdims=True)
    return dx, dw


def _dot(a, b):
    return jnp.dot(a, b, preferred_element_type=F32)


def _dot_nt(a, b):
    return lax.dot_general(a, b, (((1,), (1,)), ((), ())), preferred_element_type=F32)


def _dot_tn(a, b):
    return lax.dot_general(a, b, (((0,), (0,)), ((), ())), preferred_element_type=F32)


def _split3(x):
    hi = x.astype(BF16)
    r = x - hi.astype(F32)
    mid = r.astype(BF16)
    lo = (r - mid.astype(F32)).astype(BF16)
    return hi, mid, lo


def _xdot(x, e):
    hi, mid, lo = _split3(x)
    return _dot(hi, e) + _dot(mid, e) + _dot(lo, e)


def _xdot_l(e, x):
    hi, mid, lo = _split3(x)
    return _dot(e, hi) + _dot(e, mid) + _dot(e, lo)


def _iota(shape, dim):
    return lax.broadcasted_iota(jnp.int32, shape, dim)


def _divisors(n, unit):
    return [t for t in range(unit, n + 1, unit) if n % t == 0]


MIN_MATMUL_STEPS = 8
SMALL_MATMUL = 2 ** 33


def _matmul_tiles(m, n, k, a_bytes, b_bytes, o_bytes, m_unit):
    best = None
    for tm in _divisors(m, m_unit):
        for tn in _divisors(n, 128):
            for tk in _divisors(k, 128):
                acc = 0 if tk == k else tm * tn * 4
                vm = 2 * (tm * tk * a_bytes + tk * tn * b_bytes + tm * tn * o_bytes) + acc
                if vm > VMEM_BUDGET:
                    continue
                steps = (m // tm) * (n // tn) * (k // tk)
                want = MIN_MATMUL_STEPS if m * n * k >= SMALL_MATMUL else 2
                score = (tk == k, min(steps, want), min(tm, 256), tm * tn * tk)
                if best is None or score > best[0]:
                    best = (score, (tm, tn, tk))
    return best[1]


def _matmul(name, a, b, mode, out_dtype, bg=None):
    if mode == "nn":
        (m, k), n = a.shape, b.shape[1]
    elif mode == "nt":
        (m, k), n = a.shape, b.shape[0]
    else:
        (k, m), n = a.shape, b.shape[1]
    ab, bb, ob = a.dtype.itemsize, b.dtype.itemsize, jnp.dtype(out_dtype).itemsize
    tm, tn, tk = _matmul_tiles(m, n, k, ab, bb, ob, 128 if mode == "tn" else 16)
    nk = k // tk
    dot = {"nn": _dot, "nt": _dot_nt, "tn": _dot_tn}[mode]

    def body(a_ref, b_ref, o_ref, *scratch):
        prod = dot(a_ref[...].astype(BF16), b_ref[...].astype(BF16))
        if nk == 1:
            o_ref[...] = prod.astype(o_ref.dtype)
        else:
            acc_ref, = scratch
            kk = pl.program_id(2)

            @pl.when(kk == 0)
            def _():
                acc_ref[...] = prod

            @pl.when(kk > 0)
            def _():
                acc_ref[...] += prod

            @pl.when(kk == nk - 1)
            def _():
                o_ref[...] = acc_ref[...].astype(o_ref.dtype)

    a_spec = pl.BlockSpec((tk, tm), lambda i, j, kk: (kk, i)) if mode == "tn" else pl.BlockSpec((tm, tk), lambda i, j, kk: (i, kk))
    b_spec = pl.BlockSpec((tn, tk), lambda i, j, kk: (j, kk)) if mode == "nt" else pl.BlockSpec((tk, tn), lambda i, j, kk: (kk, j))
    res = _call(body, name, (m // tm, n // tn, nk), [a_spec, b_spec], [pl.BlockSpec((tm, tn), lambda i, j, kk: (i, j))],
                [jax.ShapeDtypeStruct((m, n), out_dtype)], [a, b],
                scratch_shapes=[] if nk == 1 else [pltpu.VMEM((tm, tn), F32)], bg=bg)
    return res[0] if bg is None else (res[0][0], res[1])


def _row_tile(n_rows, cap, unit=16):
    return max([t for t in _divisors(n_rows, unit) if t <= cap], default=n_rows)


ROW_SUB = 384
GROUP_UNROLL = 4


def _rowwise(name, fn, n_rows, tm, row_ins, full_ins, row_outs, acc_outs, bg=None):
    n_in = len(row_ins) + len(full_ins)
    n_ro = len(row_outs)
    into = [(k, o[3]) for k, o in enumerate(row_outs) if len(o) > 2 and o[2] == "into"]

    sub = min(tm, ROW_SUB)
    counts = [tm // T if len(e) > 3 and e[3] == "prev" else 1 for e in row_ins]
    assert all(cnt == 1 for cnt in counts) or sub == tm
    starts = [sum(counts[:k]) for k in range(len(counts))]
    n_row_in = sum(counts)
    n_in = n_row_in + len(full_ins)

    def body(*refs):
        i = pl.program_id(0)
        outs = refs[n_in + len(into):]

        sums = tuple(jnp.zeros((1, w), F32) for w in acc_outs)
        for s in range(tm // sub):
            rows = pl.ds(s * sub, sub)
            vals = [refs[st][rows, :] if cnt == 1 else jnp.concatenate([refs[st + k][...] for k in range(cnt)], axis=0)
                    for st, cnt in zip(starts, counts)]
            vals += [r[...] for r in refs[n_row_in:n_in]]
            res = fn(i * tm + s * sub, *vals)
            for o, r, v in zip(row_outs, outs[:n_ro], res[:n_ro]):
                if len(o) > 2 and o[2] == "first":
                    @pl.when(i == 0)
                    def _(r=r, v=v, rows=rows):
                        r[rows, :] = v.astype(r.dtype)
                else:
                    r[rows, :] = v.astype(r.dtype)
            sums = tuple(a + v for a, v in zip(sums, res[n_ro:]))

        @pl.when(i == 0)
        def _():
            for r, v in zip(outs[n_ro:], sums):
                r[...] = v

        @pl.when(i > 0)
        def _():
            for r, v in zip(outs[n_ro:], sums):
                r[...] += v

    def in_spec(entry, cnt):
        w, cb = entry[1], entry[2]
        if len(entry) > 3 and entry[3] == "prev":
            return [pl.BlockSpec((tm // cnt, w), lambda i, k=k: (jnp.maximum(cnt * i - 1 + k, 0), cb)) for k in range(cnt)]
        if len(entry) > 3 and entry[3] == "first":
            return [pl.BlockSpec((tm, w), lambda i: (0, cb))]
        return [pl.BlockSpec((tm, w), lambda i: (i, cb))]

    def out_spec(o):
        if len(o) == 2:
            return pl.BlockSpec((tm, o[0]), lambda i: (i, 0)), jax.ShapeDtypeStruct((n_rows, o[0]), o[1])
        if o[2] == "new":
            return pl.BlockSpec((tm, o[0]), lambda i: (i, o[4])), jax.ShapeDtypeStruct((n_rows, o[3]), o[1])
        if o[2] == "into":
            return pl.BlockSpec((tm, o[0]), lambda i: (i, o[4])), jax.ShapeDtypeStruct(o[3].shape, o[3].dtype)
        if o[2] == "first":
            return pl.BlockSpec((tm, o[0]), lambda i: (0, 0)), jax.ShapeDtypeStruct((tm, o[0]), o[1])
        return pl.BlockSpec((tm, o[0]), lambda i: (jnp.maximum(i - 1, 0), 0)), jax.ShapeDtypeStruct((o[3], o[0]), o[1])

    in_specs = [s for e, cnt in zip(row_ins, counts) for s in in_spec(e, cnt)]
    in_specs += [pl.BlockSpec(a.shape, lambda i: (0, 0)) for a in full_ins]
    in_specs += [pl.BlockSpec(memory_space=pl.ANY) for _ in into]
    specs_shapes = [out_spec(o) for o in row_outs]
    out_specs = [s for s, _ in specs_shapes] + [pl.BlockSpec((1, w), lambda i: (0, 0)) for w in acc_outs]
    out_shape = [s for _, s in specs_shapes] + [jax.ShapeDtypeStruct((1, w), F32) for w in acc_outs]
    return _call(body, name, (n_rows // tm,), in_specs, out_specs, out_shape,
                 [e[0] for e, cnt in zip(row_ins, counts) for _ in range(cnt)] + list(full_ins) + [arr for _, arr in into],
                 aliases={n_in + a: k for a, (k, _) in enumerate(into)}, bg=bg)


def _valid_rows(first_row, tm, lo):
    return (first_row + _iota((tm, 1), 0)) >= lo


CONV_ROWS = 384
CONV_SUB = 16
CONV_LANES = 256


def _conv_specs(tm, width, blk, n_rows, after):
    specs = [pl.BlockSpec((tm, width), lambda i: (i, blk)),
             pl.BlockSpec((8, width), lambda i: (jnp.maximum(i * (tm // 8) - 1, 0), blk))]
    if after:
        specs.append(pl.BlockSpec((16, width), lambda i: (jnp.minimum((i + 1) * (tm // 16), n_rows // 16 - 1), blk)))
    return specs


def _conv_window(win, w_ref, b_ref, taps, c0, cw, n):
    acc = b_ref[:, c0:c0 + cw] + w_ref[taps - 1:taps, c0:c0 + cw] * win[8:8 + n]
    for k in range(taps - 1):
        acc = acc + w_ref[k:k + 1, c0:c0 + cw] * win[8 - (taps - 1) + k:8 - (taps - 1) + k + n]
    return acc


def _ffn_act(name, u_raw, conv_w, conv_b, n_rows):
    tm, sub, cw = CONV_ROWS, CONV_SUB, CONV_LANES
    taps, width = conv_w.shape
    half = width // 2

    def body(cur_ref, prev_ref, w_ref, b_ref, f_ref, ext_ref):
        i = pl.program_id(0)
        ext_ref[0:8, :] = jnp.where(i > 0, prev_ref[...], 0.0)
        ext_ref[8:8 + tm, :] = cur_ref[...]
        for q in range(half // cw):
            a0, g0 = q * cw, half + q * cw

            def group(s, carry):
                r = pl.multiple_of(s * sub, sub)
                a = _conv_window(ext_ref[pl.ds(r, sub + 8), a0:a0 + cw], w_ref, b_ref, taps, a0, cw, sub)
                g = _conv_window(ext_ref[pl.ds(r, sub + 8), g0:g0 + cw], w_ref, b_ref, taps, g0, cw, sub)
                f_ref[pl.ds(r, sub), a0:a0 + cw] = (_silu(a) * g).astype(f_ref.dtype)
                return carry

            lax.fori_loop(0, tm // sub, group, 0, unroll=GROUP_UNROLL)

        @pl.when(i == 0)
        def _():
            f_ref[0:PAD, :] = jnp.zeros((PAD, half), f_ref.dtype)

    return pl.pallas_call(
        body, name=name, grid=(n_rows // tm,),
        in_specs=_conv_specs(tm, width, 0, n_rows, False) + [pl.BlockSpec((taps, width), lambda i: (0, 0)),
                                                             pl.BlockSpec((1, width), lambda i: (0, 0))],
        out_specs=pl.BlockSpec((tm, half), lambda i: (i, 0)),
        out_shape=jax.ShapeDtypeStruct((n_rows, half), BF16),
        scratch_shapes=[pltpu.VMEM((tm + 8, width), F32)],
        compiler_params=_cparams(1),
    )(u_raw, u_raw, conv_w, conv_b)


def _conv_bwd(name, raw, raw_blk, dsrcs, chunk_src, conv_w, conv_b, n_rows, gated, into=None, into_blk=0, bg=None):
    taps, width = conv_w.shape
    half = width // 2 if gated else width
    tm, sub, cw = CONV_ROWS, CONV_SUB, CONV_LANES
    te = tm + 16
    nd = len(dsrcs)
    n_parts = 2 if gated else 1

    def body(*refs):
        cur_ref, prev_ref, next_ref = refs[0:3]
        dcur, dnext = refs[3:3 + nd], refs[3 + nd:3 + 2 * nd]
        w_ref, b_ref = refs[3 + 2 * nd:5 + 2 * nd]
        out_ref, acc_ref, ext_ref, du_ref = refs[-4:]
        i = pl.program_id(0)
        ext_ref[0:8, :] = jnp.where(i > 0, prev_ref[...], 0.0)
        ext_ref[8:8 + tm, :] = cur_ref[...]
        ext_ref[8 + tm:24 + tm, :] = next_ref[...]

        for q, (src, off) in enumerate(chunk_src):
            cols = [q * cw, half + q * cw][:n_parts]

            def conv_grad(r, d, past_end):
                pre = [_conv_window(ext_ref[pl.ds(r, sub + 8), c0:c0 + cw], w_ref, b_ref, taps, c0, cw, sub) for c0 in cols]
                if gated:
                    act, dact = _silu_grad(pre[0])
                    dus = [d * pre[1] * dact, d * act]
                else:
                    dus = [d * _dsilu(pre[0])]
                for part, du in enumerate(dus):
                    if past_end:
                        du = jnp.where(i * tm + r + _iota((sub, 1), 0) < n_rows, du, 0.0)
                    du_ref[part, pl.ds(r, sub), :] = du

            def tile_rows(s, carry):
                r = pl.multiple_of(s * sub, sub)
                conv_grad(r, dcur[src][pl.ds(r, sub), off:off + cw].astype(F32), False)
                return carry

            lax.fori_loop(0, tm // sub, tile_rows, 0, unroll=GROUP_UNROLL)
            conv_grad(tm, dnext[src][:, off:off + cw].astype(F32), True)

            @pl.when(i == 0)
            def _():
                du_ref[:, 0:PAD, :] = jnp.zeros((n_parts, PAD, cw), F32)

            for part, c0 in enumerate(cols):
                taps_w = [w_ref[k:k + 1, c0:c0 + cw] for k in range(taps)]

                def back(s, sums):
                    new = list(sums)
                    for u in range(2):
                        r = pl.multiple_of((2 * s + u) * sub, sub)
                        win = du_ref[part, pl.ds(r, sub + 8), :]
                        raw_rows = ext_ref[pl.ds(8 + r, sub), c0:c0 + cw]
                        draw = jnp.zeros((sub, cw), F32)
                        for k in range(taps):
                            shifted = win[taps - 1 - k:taps - 1 - k + sub]
                            draw = draw + taps_w[k] * shifted
                            new[k] = new[k] + shifted * raw_rows
                        new[taps] = new[taps] + win[0:sub]
                        out_ref[pl.ds(r, sub), c0:c0 + cw] = draw.astype(out_ref.dtype)
                    return tuple(new)

                sums = lax.fori_loop(0, tm // (2 * sub), back, tuple(jnp.zeros((sub, cw), F32) for _ in range(taps + 1)))

                @pl.when(i == 0)
                def _(c0=c0):
                    out_ref[PAD - sub:PAD, c0:c0 + cw] = jnp.zeros((sub, cw), out_ref.dtype)

                for k in range(taps + 1):
                    total = jnp.sum(sums[k], axis=0, keepdims=True)
                    acc_ref[k:k + 1, c0:c0 + cw] = jnp.where(i == 0, total, acc_ref[k:k + 1, c0:c0 + cw] + total)

    in_specs = _conv_specs(tm, width, raw_blk, n_rows, True)
    in_specs += [pl.BlockSpec((tm, d.shape[1]), lambda i: (i, 0)) for d in dsrcs]
    in_specs += [pl.BlockSpec((16, d.shape[1]), lambda i: (jnp.minimum((i + 1) * (tm // 16), n_rows // 16 - 1), 0)) for d in dsrcs]
    in_specs += [pl.BlockSpec((taps, width), lambda i: (0, 0)), pl.BlockSpec((1, width), lambda i: (0, 0))]
    operands = [raw, raw, raw] + list(dsrcs) + list(dsrcs) + [conv_w, conv_b]
    aliases = {}
    if into is None:
        out0 = jax.ShapeDtypeStruct((n_rows, width), BF16)
    else:
        in_specs.append(pl.BlockSpec(memory_space=pl.ANY))
        operands.append(into)
        aliases = {len(operands) - 1: 0}
        out0 = jax.ShapeDtypeStruct(into.shape, into.dtype)
    return _call(body, name, (n_rows // tm,), in_specs,
                 [pl.BlockSpec((tm, width), lambda i: (i, into_blk)), pl.BlockSpec((8, width), lambda i: (0, 0))],
                 [out0, jax.ShapeDtypeStruct((8, width), F32)], operands,
                 scratch_shapes=[pltpu.VMEM((tm + 24, width), F32), pltpu.VMEM((n_parts, te + 8, cw), F32)],
                 aliases=aliases, bg=bg)


def _ssd_specs(n_chunks, rev, per_step=1):
    cidx = (lambda c: n_chunks - 1 - c) if rev else (lambda c: c)
    xw, nw = per_step * GROUP_W, per_step * D_STATE
    xg0, bg0, cg0 = P_XBC // xw, (P_XBC + D_INNER) // nw, (P_XBC + D_INNER + SSM_GROUPS * D_STATE) // nw

    def cur(width, blk0):
        return pl.BlockSpec((T, width), lambda g, c: (cidx(c), blk0 + g))

    def prev(width, blk0):
        return pl.BlockSpec((8, width), lambda g, c: (jnp.maximum(cidx(c) * (T // 8) - 1, 0), blk0 + g))

    specs = [cur(xw, xg0), prev(xw, xg0), cur(nw, bg0), prev(nw, bg0), cur(nw, cg0), prev(nw, cg0),
             pl.BlockSpec((T, 128), lambda g, c: (cidx(c), P_DT // 128))]
    wb, wc = D_INNER // nw, (D_INNER + SSM_GROUPS * D_STATE) // nw
    specs += [pl.BlockSpec((4, xw), lambda g, c: (0, g)),
              pl.BlockSpec((4, nw), lambda g, c: (0, wb + g)),
              pl.BlockSpec((4, nw), lambda g, c: (0, wc + g)),
              pl.BlockSpec((1, xw), lambda g, c: (0, g)),
              pl.BlockSpec((1, nw), lambda g, c: (0, wb + g)),
              pl.BlockSpec((1, nw), lambda g, c: (0, wc + g))]
    specs += [pl.BlockSpec((1, 128), lambda g, c: (0, 0))] * 3
    return specs, cidx


def _ssd_shared(refs, c):
    dt_ref, dtb_ref, alog_ref = refs[6], refs[13], refs[14]
    valid = _valid_rows(c * T, T, PAD)
    dtr = dt_ref[...] + dtb_ref[...]
    dt = jnp.where(valid, _softplus(dtr), 0.0)
    a_neg = -jnp.exp(alog_ref[...])
    tril = _iota((T, T), 0) >= _iota((T, T), 1)
    cs = _xdot_l(tril.astype(BF16), dt * a_neg)
    return dict(valid=valid, dtr=dtr, dt=dt, a_neg=a_neg, tril=tril, cs=cs, cs_t=cs.T)


def _heads_of_lanes():
    hh_t, ll_t = _iota((D_INNER, 128), 1), _iota((D_INNER, 128), 0)
    return (hh_t == jnp.right_shift(ll_t, 6)).astype(BF16)


def _ssd_chunk_forward(refs, ext_ref, g, c, shared):
    (xc_ref, xp_ref, bc_ref, bp_ref, cc_ref, cp_ref, dt_ref, wx_ref, wb_ref, wc_ref,
     bx_ref, bb_ref, bcb_ref, dtb_ref, alog_ref, dsk_ref) = refs

    def conv_pre(cur_ref, prev_ref, w_ref, b_ref, width):
        ext_ref[0:8, 0:width] = jnp.where(c > 0, prev_ref[...], 0.0)
        ext_ref[8:8 + T, 0:width] = cur_ref[...]
        w = w_ref[...]
        acc = b_ref[...] + w[3:4] * cur_ref[...]
        for k in range(3):
            acc = acc + w[k:k + 1] * ext_ref[pl.ds(5 + k, T), 0:width]
        return acc

    v = dict(shared)
    valid = v["valid"]
    v["head0"] = 8 * g
    v["x_pre"] = conv_pre(xc_ref, xp_ref, wx_ref, bx_ref, GROUP_W)
    v["b_pre"] = conv_pre(bc_ref, bp_ref, wb_ref, bb_ref, D_STATE)
    v["c_pre"] = conv_pre(cc_ref, cp_ref, wc_ref, bcb_ref, D_STATE)
    xs = _silu(v["x_pre"])
    bm = jnp.where(valid, _silu(v["b_pre"]), 0.0)
    cm = jnp.where(valid, _silu(v["c_pre"]), 0.0)
    hh, ll = _iota((128, GROUP_W), 0), _iota((128, GROUP_W), 1)
    expand = (hh == 8 * g + jnp.right_shift(ll, 6)).astype(BF16)
    cs_e = _xdot(v["cs"], expand)
    dt_e = _xdot(v["dt"], expand)
    cs_last_e = cs_e[T - 1:T, :]
    v.update(xs=xs, bm=bm, cm=cm, cs_e=cs_e, dt_e=dt_e, cs_last_e=cs_last_e)
    v["xdt"] = xs * dt_e
    v["decay_e"] = jnp.exp(cs_last_e - cs_e)
    v["ecs_e"] = jnp.exp(cs_e)
    v["elast_e"] = jnp.exp(cs_last_e)
    v["d_e"] = _xdot(dsk_ref[...], expand)
    v["gmat"] = _dot_nt(cm.astype(BF16), bm.astype(BF16))
    return v


def _ssd_decay_pair(v, jp):
    out = []
    for j in (v["head0"] + 2 * jp, v["head0"] + 2 * jp + 1):
        diff = v["cs"][:, j:j + 1] - v["cs_t"][j:j + 1, :]
        out.append(jnp.where(v["tril"], jnp.exp(jnp.where(v["tril"], diff, 0.0)), 0.0))
    return out


def _block_diag_pair(xp):
    lane = _iota(xp.shape, 1)
    return jnp.concatenate([jnp.where(lane < HEAD_P, xp, 0.0), jnp.where(lane >= HEAD_P, xp, 0.0)], axis=0)


SSD_GROUPS_PER_STEP = 4


def _ssd_group_refs(refs, gg):
    x_w, n_w = pl.ds(GROUP_W * gg, GROUP_W), pl.ds(D_STATE * gg, D_STATE)
    lanes = [x_w, x_w, n_w, n_w, n_w, n_w, None, x_w, n_w, n_w, x_w, n_w, n_w, None, None, None]
    return [r if w is None else r.at[:, w] for r, w in zip(refs, lanes)]


def _ssd_fwd(p, conv_w, conv_b, dt_bias, a_log, d_skip, n_chunks, bg=None):
    n_rows = n_chunks * T
    in_specs, _ = _ssd_specs(n_chunks, rev=False, per_step=SSD_GROUPS_PER_STEP)
    per = SSD_GROUPS_PER_STEP
    assert per == SSM_GROUPS

    def body(*refs):
        y_ref, hin_ref, st_ref, ext_ref = refs[16:]
        c = pl.program_id(1)

        @pl.when(c == 0)
        def _():
            st_ref[...] = jnp.zeros_like(st_ref)

        shared = _ssd_shared(refs[:16], c)
        for gg in range(per):
            v = _ssd_chunk_forward(_ssd_group_refs(refs[:16], gg), ext_ref.at[gg], gg, c, shared)
            state = st_ref[gg]
            hin_ref[gg] = state
            ys = []
            for jp in range(4):
                l0, l1 = _ssd_decay_pair(v, jp)
                lhs = jnp.concatenate([v["gmat"] * l0, v["gmat"] * l1], axis=1).astype(BF16)
                rhs = _block_diag_pair(v["xdt"][:, 128 * jp:128 * jp + 128]).astype(BF16)
                ys.append(_dot(lhs, rhs))
            y = jnp.concatenate(ys, axis=1)
            y = y + _dot(v["cm"].astype(BF16), state.astype(BF16)) * v["ecs_e"] + v["xs"] * v["d_e"]
            y_ref[:, GROUP_W * gg:GROUP_W * gg + GROUP_W] = y
            s_new = _dot_tn(v["bm"].astype(BF16), (v["xdt"] * v["decay_e"]).astype(BF16))
            st_ref[gg] = state * v["elast_e"] + s_new

    return _call(
        body, "ssd_fwd", (SSM_GROUPS // per, n_chunks), in_specs,
        [pl.BlockSpec((T, per * GROUP_W), lambda g, c: (c, g)),
         pl.BlockSpec((per, None, D_STATE, GROUP_W), lambda g, c: (g, c, 0, 0))],
        [jax.ShapeDtypeStruct((n_rows, D_INNER), F32),
         jax.ShapeDtypeStruct((SSM_GROUPS, n_chunks, D_STATE, GROUP_W), F32)],
        [p, p, p, p, p, p, p, conv_w, conv_w, conv_w, conv_b, conv_b, conv_b, dt_bias, a_log, d_skip],
        scratch_shapes=[pltpu.VMEM((per, D_STATE, GROUP_W), F32), pltpu.VMEM((per, T + 8, GROUP_W), F32)], bg=bg)


def _ssd_bwd(p, conv_w, conv_b, dt_bias, a_log, d_skip, hin, dy, dp, n_chunks, bg=None):
    n_rows = n_chunks * T
    per = SSD_GROUPS_PER_STEP
    assert per == SSM_GROUPS
    dt_w = P_Z - P_DT
    in_specs, cidx = _ssd_specs(n_chunks, rev=True, per_step=per)
    in_specs = in_specs + [pl.BlockSpec((per, None, D_STATE, GROUP_W), lambda g, c: (g, cidx(c), 0, 0)),
                           pl.BlockSpec((T, per * GROUP_W), lambda g, c: (cidx(c), g)), ANY]

    def body(*refs):
        hin_ref, dy_ref = refs[16:18]
        dx_ref, db_ref, dc_ref, dp_ref, dpar_ref, dst_ref, ext_ref, red_ref, dd_ref = refs[19:]
        step = pl.program_id(1)
        shared = _ssd_shared(refs[:16], n_chunks - 1 - step)
        local = jnp.zeros((T, 128), F32)
        for gg in range(per):
            x_w, n_w = pl.ds(GROUP_W * gg, GROUP_W), pl.ds(D_STATE * gg, D_STATE)
            local = local + group_body(_ssd_group_refs(refs[:16], gg), hin_ref.at[gg], dy_ref.at[:, x_w],
                                       dx_ref.at[:, x_w], db_ref.at[:, n_w], dc_ref.at[:, n_w], red_ref.at[:, :, x_w],
                                       dd_ref.at[:, x_w], dst_ref.at[gg], ext_ref.at[gg], gg, shared)
        to_heads = _heads_of_lanes()
        dcs = _xdot(red_ref[0], to_heads) + local
        triu = (_iota((T, T), 0) <= _iota((T, T), 1)).astype(BF16)
        da = _xdot_l(triu, dcs)
        ddt = da * shared["a_neg"] + _xdot(red_ref[1], to_heads)
        ddtr = jnp.where(shared["valid"], ddt * _sigmoid(shared["dtr"]), 0.0)
        dp_ref[...] = jnp.concatenate([ddtr, jnp.zeros((T, dt_w - 128), F32)], axis=1).astype(dp_ref.dtype)
        dpar = jnp.concatenate([
            jnp.sum(ddtr, axis=0, keepdims=True),
            jnp.sum(da * shared["dt"], axis=0, keepdims=True) * shared["a_neg"],
            _xdot(dd_ref[0:1, :], to_heads),
            jnp.zeros((5, 128), F32)], axis=0)
        dpar_ref[...] = jnp.where(step == 0, dpar, dpar_ref[...] + dpar)

    def group_body(in_refs, hin_ref, dy_ref, dx_ref, db_ref, dc_ref, red_ref, dd_ref, dst_ref, ext_ref, g, shared):
        step = pl.program_id(1)
        c = n_chunks - 1 - step

        @pl.when(step == 0)
        def _():
            dst_ref[...] = jnp.zeros_like(dst_ref)

        v = _ssd_chunk_forward(in_refs, ext_ref, g, c, shared)
        hin_f = hin_ref[...]
        hin_b = hin_f.astype(BF16)
        dyv = dy_ref[...]
        dst = dst_ref[...]
        dst_b = dst.astype(BF16)
        xs, bm, cm, xdt = v["xs"], v["bm"], v["cm"], v["xdt"]
        bm_b, cm_b = bm.astype(BF16), cm.astype(BF16)

        dd_e = jnp.sum(dyv * xs, axis=0, keepdims=True)
        dxs = dyv * v["d_e"]
        ch = _dot(cm_b, hin_b)
        dch = (dyv * v["ecs_e"]).astype(BF16)
        dcm = _dot_nt(dch, hin_b)
        dhin = _dot_tn(cm_b, dch) + dst * v["elast_e"]
        dcs_e = dyv * ch * v["ecs_e"]
        dxd = _dot(bm_b, dst_b)
        dbm = _dot_nt((xdt * v["decay_e"]).astype(BF16), dst_b)
        dxdt_state = dxd * v["decay_e"]
        q = dxdt_state * xdt
        dcs_e = dcs_e - q
        dlast_e = jnp.sum(q, axis=0, keepdims=True) + jnp.sum(dst * hin_f, axis=0, keepdims=True) * v["elast_e"]
        dg = jnp.zeros((T, T), F32)
        rs_cols = jnp.zeros((T, 128), F32)
        cs_rows = jnp.zeros((128, T), F32)
        lane_i, sub_i = _iota((T, 128), 1), _iota((128, T), 0)
        dxdt_parts = []
        for jp in range(4):
            l0, l1 = _ssd_decay_pair(v, jp)
            m0, m1 = v["gmat"] * l0, v["gmat"] * l1
            xbd = _block_diag_pair(xdt[:, 128 * jp:128 * jp + 128]).astype(BF16)
            dyp = dyv[:, 128 * jp:128 * jp + 128]
            dm = _dot_nt(dyp.astype(BF16), xbd)
            dm0, dm1 = dm[:, 0:T], dm[:, T:2 * T]
            dg = dg + dm0 * l0 + dm1 * l1
            for j, qq in ((v["head0"] + 2 * jp, dm0 * m0), (v["head0"] + 2 * jp + 1, dm1 * m1)):
                rs_cols = jnp.where(lane_i == j, jnp.sum(qq, axis=1, keepdims=True), rs_cols)
                cs_rows = jnp.where(sub_i == j, jnp.sum(qq, axis=0, keepdims=True), cs_rows)
            mv = jnp.concatenate([m0, m1], axis=0).astype(BF16)
            dxdt_parts.append(_dot_tn(mv, _block_diag_pair(dyp).astype(BF16)))
        dxdt = jnp.concatenate(dxdt_parts, axis=1) + dxdt_state
        dg_b = dg.astype(BF16)
        dcm = dcm + _dot(dg_b, bm_b)
        dbm = dbm + _dot_tn(dg_b, cm_b)
        last_row = _iota((T, 1), 0) == T - 1
        red_ref[0] = dcs_e + jnp.where(last_row, dlast_e, 0.0)
        red_ref[1] = dxdt * xs
        dd_ref[0:1, :] = dd_e
        dx_ref[...] = dxs + dxdt * v["dt_e"]
        db_ref[...] = jnp.where(v["valid"], dbm, 0.0)
        dc_ref[...] = jnp.where(v["valid"], dcm, 0.0)
        dst_ref[...] = dhin
        return rs_cols - cs_rows.T

    return _call(
        body, "ssd_bwd", (SSM_GROUPS // per, n_chunks), in_specs,
        [pl.BlockSpec((T, per * GROUP_W), lambda g, c: (cidx(c), g)),
         pl.BlockSpec((T, per * D_STATE), lambda g, c: (cidx(c), g)),
         pl.BlockSpec((T, per * D_STATE), lambda g, c: (cidx(c), g)),
         pl.BlockSpec((T, dt_w), lambda g, c: (cidx(c), P_DT // dt_w)),
         pl.BlockSpec((8, 128), lambda g, c: (0, 0))],
        [jax.ShapeDtypeStruct((n_rows, D_INNER), F32),
         jax.ShapeDtypeStruct((n_rows, SSM_GROUPS * D_STATE), F32),
         jax.ShapeDtypeStruct((n_rows, SSM_GROUPS * D_STATE), F32),
         jax.ShapeDtypeStruct(dp.shape, dp.dtype),
         jax.ShapeDtypeStruct((8, 128), F32)],
        [p, p, p, p, p, p, p, conv_w, conv_w, conv_w, conv_b, conv_b, conv_b, dt_bias, a_log, d_skip, hin, dy, dp],
        scratch_shapes=[pltpu.VMEM((per, D_STATE, GROUP_W), F32), pltpu.VMEM((per, T + 8, GROUP_W), F32),
                        pltpu.VMEM((2, T, D_INNER), F32), pltpu.VMEM((8, D_INNER), F32)],
        aliases={18: 3}, bg=bg)


def _alibi_slope(h):
    return 2.0 ** (-8.0 * (h + 1) / ATTN_HEADS)


def _dup_half(x256, kvh):
    xb = x256[:, 128 * (kvh // 2):128 * (kvh // 2) + 128]
    rolled = pltpu.roll(xb, 64, 1)
    lane = _iota(xb.shape, 1)
    if kvh % 2 == 0:
        return jnp.where(lane < 64, xb, rolled)
    return jnp.where(lane < 64, rolled, xb)


def _attn_masks(c):
    qi, j = _iota((T, T), 0), _iota((T, T), 1)
    tri = j <= qi
    meta_ok = (j >= PAD) & (j - PAD <= c * T + qi - PAD)
    band_ok = c >= jnp.where(tri, 1, 2)
    dist = jnp.bitwise_and(qi - j, T - 1).astype(F32)
    return tri, meta_ok, band_ok, dist


def _fold(x3, tri):
    return jnp.concatenate([x3[:, 0:T], jnp.where(tri, x3[:, 2 * T:3 * T], x3[:, T:2 * T])], axis=1)


def _unfold(x2, tri):
    band = x2[:, T:2 * T]
    return jnp.concatenate([x2[:, 0:T], jnp.where(tri, 0.0, band), jnp.where(tri, band, 0.0)], axis=1)


def _attn_fwd(p, sinks, n_chunks, bg=None):
    n_rows = n_chunks * T
    kb, vb = P_K // KV_W, P_V // KV_W

    def body(q_ref, kc_ref, kp_ref, km_ref, vc_ref, vp_ref, vm_ref, sink_ref, o_ref, lse_ref):
        c = pl.program_id(0)
        sinks_v = sink_ref[...]
        masks = _attn_masks(c)
        tri, meta_ok, band_ok, dist = masks
        lane = _iota((T, 128), 1)
        for kvh in range(KV_HEADS):
            k3 = jnp.concatenate([_dup_half(r[...], kvh) for r in (km_ref, kp_ref, kc_ref)], axis=0).astype(BF16)
            v3 = jnp.concatenate([_dup_half(r[...], kvh) for r in (vm_ref, vp_ref, vc_ref)], axis=0)
            v3bd = _block_diag_rows(v3).astype(BF16)
            q2 = q_ref[:, 256 * kvh:256 * kvh + 256] * SCALE
            q4 = jnp.concatenate([jnp.where((lane < 64) if half == 0 else (lane >= 64), q2[:, 128 * pr:128 * pr + 128], 0.0)
                                  for pr in range(2) for half in range(2)], axis=0).astype(BF16)
            raw4 = _dot_nt(q4, k3)
            probs = []
            for hh in range(4):
                h = 4 * kvh + hh
                raw = raw4[T * hh:T * hh + T]
                band = jnp.where(tri, raw[:, 2 * T:3 * T], raw[:, T:2 * T]) - _alibi_slope(h) * dist
                sc = jnp.concatenate([jnp.where(meta_ok, raw[:, 0:T], NEG), jnp.where(band_ok, band, NEG)], axis=1)
                sink = sinks_v[:, h:h + 1]
                m = jnp.maximum(jnp.max(sc, axis=1, keepdims=True), sink)
                e = jnp.exp(sc - m)
                den = jnp.sum(e, axis=1, keepdims=True) + jnp.exp(sink - m)
                probs.append(_unfold(e * (1.0 / den), tri))
                lse_ref[:, h:h + 1] = m + jnp.log(den)
            p4 = jnp.concatenate([jnp.concatenate(probs[0:2], axis=1), jnp.concatenate(probs[2:4], axis=1)], axis=0)
            out = _dot(p4.astype(BF16), v3bd)
            o_ref[:, 256 * kvh:256 * kvh + 256] = jnp.concatenate([out[0:T], out[T:2 * T]], axis=1).astype(o_ref.dtype)

    blk = lambda width, col: pl.BlockSpec((T, width), lambda c: (c, col))
    prev = lambda width, col: pl.BlockSpec((T, width), lambda c: (jnp.maximum(c - 1, 0), col))
    first = lambda width, col: pl.BlockSpec((T, width), lambda c: (0, col))
    return _call(
        body, "attn_fwd", (n_chunks,),
        [blk(ATTN_W, P_Q // ATTN_W), blk(KV_W, kb), prev(KV_W, kb), first(KV_W, kb),
         blk(KV_W, vb), prev(KV_W, vb), first(KV_W, vb), pl.BlockSpec((1, 128), lambda c: (0, 0))],
        [pl.BlockSpec((T, ATTN_W), lambda c: (c, 0)), pl.BlockSpec((T, 128), lambda c: (c, 0))],
        [jax.ShapeDtypeStruct((n_rows, ATTN_W), BF16), jax.ShapeDtypeStruct((n_rows, 128), F32)],
        [p, p, p, p, p, p, p, sinks], bg=bg)


def _block_diag_rows(x3):
    lane = _iota(x3.shape, 1)
    return jnp.concatenate([jnp.where(lane < 64, x3, 0.0), jnp.where(lane >= 64, x3, 0.0)], axis=0)


def _fold_halves(x):
    return x + pltpu.roll(x, 64, 1)


def _attn_bwd(p, sinks, ao, lse, dao, dp, n_chunks, bg=None):
    kb, vb = P_K // KV_W, P_V // KV_W
    rc = lambda s: n_chunks - 1 - s

    def body(q_ref, kc_ref, kp_ref, km_ref, vc_ref, vp_ref, vm_ref, sink_ref, o_ref, lse_ref, do_ref, dp_in_ref,
             dqkv_ref, dsink_ref, kcar_ref, vcar_ref, kmeta_ref, vmeta_ref):
        step = pl.program_id(0)
        c = n_chunks - 1 - step

        @pl.when(step == 0)
        def _():
            for r in (kcar_ref, vcar_ref, kmeta_ref, vmeta_ref):
                r[...] = jnp.zeros_like(r)

        masks = _attn_masks(c)
        tri = masks[0]
        q = q_ref[...] * SCALE
        sinks_v = sink_ref[...]
        lse_v = lse_ref[...]
        ov = o_ref[...].astype(F32)
        dov = do_ref[...].astype(F32)
        lane = _iota((T, 128), 1)
        lane256 = _iota((3 * T, KV_W), 1)
        dsink = jnp.zeros((1, 128), F32)
        dk3_all = jnp.zeros((3 * T, KV_W), F32)
        dv3_all = jnp.zeros((3 * T, KV_W), F32)
        dqs = []
        for kvh in range(KV_HEADS):
            k3 = jnp.concatenate([_dup_half(r[...], kvh) for r in (km_ref, kp_ref, kc_ref)], axis=0).astype(BF16)
            v3 = jnp.concatenate([_dup_half(r[...], kvh) for r in (vm_ref, vp_ref, vc_ref)], axis=0).astype(BF16)
            halves = [(pr, half, (lane < 64) if half == 0 else (lane >= 64)) for pr in range(2) for half in range(2)]
            cols = [slice(128 * (2 * kvh + pr), 128 * (2 * kvh + pr) + 128) for pr in range(2)]
            q4 = jnp.concatenate([jnp.where(mine, q[:, cols[pr]], 0.0) for pr, _, mine in halves], axis=0).astype(BF16)
            do4 = jnp.concatenate([jnp.where(mine, dov[:, cols[pr]], 0.0) for pr, _, mine in halves], axis=0).astype(BF16)
            raw4 = _dot_nt(q4, k3)
            dp4 = _dot_nt(do4, v3)
            ds_rows, pm_rows = [], []
            for hh, (pr, half, mine) in enumerate(halves):
                h = 4 * kvh + hh
                raw = raw4[T * hh:T * hh + T]
                band = jnp.where(tri, raw[:, 2 * T:3 * T], raw[:, T:2 * T]) - _alibi_slope(h) * masks[3]
                sc = jnp.concatenate([jnp.where(masks[1], raw[:, 0:T], NEG), jnp.where(masks[2], band, NEG)], axis=1)
                lse_h = lse_v[:, h:h + 1]
                pm = jnp.exp(sc - lse_h)
                prod = dov[:, cols[pr]] * ov[:, cols[pr]]
                delta = jnp.sum(jnp.where(mine, prod, 0.0), axis=1, keepdims=True)
                dp = _fold(dp4[T * hh:T * hh + T], tri)
                ds_rows.append(_unfold(pm * (dp - delta), tri))
                pm_rows.append(_unfold(pm, tri))
                p_sink = jnp.exp(sinks_v[:, h:h + 1] - lse_h)
                dsink = jnp.where(_iota((1, 128), 1) == h, jnp.sum(-p_sink * delta, axis=0, keepdims=True), dsink)
            ds4 = jnp.concatenate(ds_rows, axis=0).astype(BF16)
            dq4 = _dot(ds4, k3)
            dk3 = _dot_tn(ds4, q4)
            dv3 = _dot_tn(jnp.concatenate(pm_rows, axis=0).astype(BF16), do4)
            for pr in range(2):
                dqs.append(jnp.where(lane < 64, dq4[2 * T * pr:2 * T * pr + T], dq4[2 * T * pr + T:2 * T * pr + 2 * T]) * SCALE)
            in_place = (lane256 >= 64 * kvh) & (lane256 < 64 * kvh + 64)
            wide = lambda x: jnp.concatenate([x, x], axis=1)
            dk3_all = jnp.where(in_place, wide(_fold_halves(dk3)), dk3_all)
            dv3_all = jnp.where(in_place, wide(_fold_halves(dv3)), dv3_all)
        dsink_all = dsink

        @pl.when(step == 0)
        def _():
            dsink_ref[...] = dsink_all

        @pl.when(step > 0)
        def _():
            dsink_ref[...] += dsink_all

        kmeta = kmeta_ref[...] + dk3_all[0:T]
        vmeta = vmeta_ref[...] + dv3_all[0:T]
        kmeta_ref[...] = kmeta
        vmeta_ref[...] = vmeta
        is_first = c == 0
        dk = jnp.where(is_first, kmeta, dk3_all[2 * T:3 * T] + kcar_ref[...])
        dv = jnp.where(is_first, vmeta, dv3_all[2 * T:3 * T] + vcar_ref[...])
        dqkv_ref[...] = jnp.concatenate(dqs + [dk, dv], axis=1).astype(dqkv_ref.dtype)
        kcar_ref[...] = dk3_all[T:2 * T]
        vcar_ref[...] = dv3_all[T:2 * T]

    blk = lambda width, col: pl.BlockSpec((T, width), lambda s: (rc(s), col))
    prev = lambda width, col: pl.BlockSpec((T, width), lambda s: (jnp.maximum(rc(s) - 1, 0), col))
    first = lambda width, col: pl.BlockSpec((T, width), lambda s: (0, col))
    return _call(
        body, "attn_bwd", (n_chunks,),
        [blk(ATTN_W, P_Q // ATTN_W), blk(KV_W, kb), prev(KV_W, kb), first(KV_W, kb),
         blk(KV_W, vb), prev(KV_W, vb), first(KV_W, vb), pl.BlockSpec((1, 128), lambda s: (0, 0)),
         blk(ATTN_W, 0), blk(128, 0), blk(ATTN_W, 0), ANY],
        [blk(QKV_W, P_Q // QKV_W), pl.BlockSpec((1, 128), lambda s: (0, 0))],
        [jax.ShapeDtypeStruct(dp.shape, dp.dtype), jax.ShapeDtypeStruct((1, 128), F32)],
        [p, p, p, p, p, p, p, sinks, ao, lse, dao, dp],
        scratch_shapes=[pltpu.VMEM((T, KV_W), F32)] * 4, aliases={11: 0}, bg=bg)


def _pad_lanes(v, width=128):
    return jnp.pad(v, ((0, 0), (0, width - v.shape[1])))


def _local_step(x, head, tgt, plan):
    w, g, run = plan.w, plan.g, plan.run
    n_tok = x.shape[0]
    n_rows = n_tok + T
    n_chunks = n_rows // T
    tm = _row_tile(n_rows, 384)
    dt_bias, a_log, d_skip = (_pad_lanes(w[k]) for k in ("ssm_dt_bias", "ssm_a_log", "ssm_d_skip"))
    sinks = _pad_lanes(w["attn_sinks"])
    x_in = [(x, D_MODEL, 0, "prev"), (head, D_MODEL, 0, "first")]
    head_tm = jnp.concatenate([head, jnp.zeros((tm - T, D_MODEL), F32)], axis=0)
    x_in_tm = [(x, D_MODEL, 0, "prev"), (head_tm, D_MODEL, 0, "first")]

    def h0_tile(r0, xt, hd):
        return jnp.where(_valid_rows(r0, xt.shape[0], T), xt, hd)

    n1, = _rowwise("norm_pre_mix", lambda r0, xt, hd, wn: [_rms(h0_tile(r0, xt, hd), wn)], n_rows, tm,
                   x_in_tm, [w["norm_pre_mix"]], [(D_MODEL, BF16)], [])
    p = _matmul("in_proj", n1, w["w_cat"], "nn", F32)
    y_ssd, hin = run("ssd_fwd", _ssd_fwd, p, w["ssm_conv_w"], w["ssm_conv_b"], dt_bias, a_log, d_skip, n_chunks)
    ao, lse = run("attn_fwd", _attn_fwd, p, sinks, n_chunks)

    def gate_norm(r0, y, z, wn):
        return [_rms(y * _silu(z), wn)]

    yn, = run("ssm_gate_norm", _rowwise, "ssm_gate_norm", gate_norm, n_rows, tm,
              [(y_ssd, D_INNER, 0), (p, D_INNER, P_Z // D_INNER)], [w["ssm_norm"]], [(D_INNER, BF16)], [])
    y_ssm = _matmul("ssm_out", yn, w["w_ssm_out"], "nn", F32)
    y_attn = _matmul("attn_out", ao, w["w_attn_out"], "nn", F32)

    def mix_gate(r0, ys, ya, gs, ga):
        return [_sigmoid(gs) * ys + _sigmoid(ga) * ya]

    gate_ins = [(p, D_MODEL, P_GATE // D_MODEL), (p, D_MODEL, P_GATE // D_MODEL + 1)]
    mixed, = _rowwise("mix_gate", mix_gate, n_rows, tm, [(y_ssm, D_MODEL, 0), (y_attn, D_MODEL, 0)] + gate_ins,
                      [], [(D_MODEL, BF16)], [])
    mix = _matmul("mix_out", mixed, w["w_mix_out"], "nn", F32)

    def post_mix(r0, mx, xt, hd, w_post, w_pre):
        h1 = jnp.where(_valid_rows(r0, mx.shape[0], PAD), h0_tile(r0, xt, hd) + _rms(mx, w_post), 0.0)
        return [h1, _rms(h1, w_pre)]

    h1, n2 = _rowwise("post_mix", post_mix, n_rows, tm, [(mix, D_MODEL, 0)] + x_in_tm,
                      [w["norm_post_mix"], w["norm_pre_ffn"]], [(D_MODEL, F32), (D_MODEL, BF16)], [])
    u_raw = _matmul("ffn_up", n2, w["w_ffn_up"], "nn", F32)
    f = _ffn_act("ffn_act", u_raw, w["ffn_conv_w"], w["ffn_conv_b"], n_rows)
    ffn = _matmul("ffn_down", f, w["w_ffn_down"], "nn", F32)

    def final(r0, fo, h, t, w_post):
        real = _valid_rows(r0, fo.shape[0], T)
        err = jnp.where(real, h + _rms(fo, w_post) - t, 0.0)
        dy = err * (1.0 / D_MODEL)
        dffn, dw = _rms_bwd(dy, fo, w_post)
        return [dffn, dy, jnp.sum(err * err, axis=0, keepdims=True), dw]

    dffn, dh2, loss_cols, g_norm_post_ffn = _rowwise(
        "loss_head", final, n_rows, tm, [(ffn, D_MODEL, 0), (h1, D_MODEL, 0), (tgt, D_MODEL, 0, "prev")],
        [w["norm_post_ffn"]], [(D_MODEL, BF16), (D_MODEL, F32)], [D_MODEL, D_MODEL])

    g["norm_post_ffn"] = g_norm_post_ffn
    g["w_ffn_down"] = _matmul("ffn_down_dw", f, dffn, "tn", F32)
    df = _matmul("ffn_down_dx", dffn, w["w_ffn_down"], "nt", F32)
    du_raw, dconv = _conv_bwd("ffn_act_bwd", u_raw, 0, [df], [(0, c0) for c0 in range(0, FFN_DIM, CONV_LANES)],
                              w["ffn_conv_w"], w["ffn_conv_b"], n_rows, True)
    g["ffn_conv_w"], g["ffn_conv_b"] = dconv[0:3], dconv[3:4]
    g["w_ffn_up"] = _matmul("ffn_up_dw", n2, du_raw, "tn", F32)
    dn2 = run("ffn_up_dx", _matmul, "ffn_up_dx", du_raw, w["w_ffn_up"], "nt", F32)

    def post_mix_bwd(r0, dn, d2, h, mx, w_pre, w_post):
        dx, dw_pre = _rms_bwd(dn, h, w_pre)
        dh1 = jnp.where(_valid_rows(r0, dn.shape[0], PAD), dx + d2, 0.0)
        dmix, dw_post = _rms_bwd(dh1, mx, w_post)
        return [dh1, dmix, dw_pre, dw_post]

    dh1, dmix, g["norm_pre_ffn"], g["norm_post_mix"] = _rowwise(
        "post_mix_bwd", post_mix_bwd, n_rows, tm,
        [(dn2, D_MODEL, 0), (dh2, D_MODEL, 0), (h1, D_MODEL, 0), (mix, D_MODEL, 0)],
        [w["norm_pre_ffn"], w["norm_post_mix"]], [(D_MODEL, F32), (D_MODEL, BF16)], [D_MODEL, D_MODEL])
    g["w_mix_out"] = _matmul("mix_out_dw", mixed, dmix, "tn", F32)
    dmixed = _matmul("mix_out_dx", dmix, w["w_mix_out"], "nt", F32)

    def mix_gate_bwd(r0, dm, ys, ya, gs, ga):
        ss, sa = _sigmoid(gs), _sigmoid(ga)
        dgate = jnp.concatenate([dm * ys * ss * (1.0 - ss), dm * ya * sa * (1.0 - sa)], axis=1)
        return [dm * ss, dm * sa, dgate]

    dys, dya, dp = _rowwise(
        "mix_gate_bwd", mix_gate_bwd, n_rows, tm,
        [(dmixed, D_MODEL, 0), (y_ssm, D_MODEL, 0), (y_attn, D_MODEL, 0)] + gate_ins,
        [], [(D_MODEL, BF16), (D_MODEL, BF16), (2 * D_MODEL, BF16, "new", P_W, P_GATE // (2 * D_MODEL))], [])
    g["w_ssm_out"] = _matmul("ssm_out_dw", yn, dys, "tn", F32)
    dyn = _matmul("ssm_out_dx", dys, w["w_ssm_out"], "nt", F32)
    g["w_attn_out"] = _matmul("attn_out_dw", ao, dya, "tn", F32)
    dao = _matmul("attn_out_dx", dya, w["w_attn_out"], "nt", BF16)

    def gate_norm_bwd(r0, dn, y, z, wn):
        sz, dsz = _silu_grad(z)
        dyz, dw = _rms_bwd(dn, y * sz, wn)
        live = _valid_rows(r0, dn.shape[0], PAD)
        return [jnp.where(live, dyz * sz, 0.0), jnp.where(live, dyz * y * dsz, 0.0), dw]

    dy_ssd, dp, g["ssm_norm"] = run(
        "ssm_gate_norm_bwd", _rowwise, "ssm_gate_norm_bwd", gate_norm_bwd, n_rows, tm,
        [(dyn, D_INNER, 0), (y_ssd, D_INNER, 0), (p, D_INNER, P_Z // D_INNER)],
        [w["ssm_norm"]], [(D_INNER, F32), (D_INNER, BF16, "into", dp, P_Z // D_INNER)], [D_INNER])
    dp, dsink = run("attn_bwd", _attn_bwd, p, sinks, ao, lse, dao, dp, n_chunks)
    g["attn_sinks"] = dsink[:, 0:ATTN_HEADS]
    dxs, dbm, dcm, dp, dpar = run("ssd_bwd", _ssd_bwd, p, w["ssm_conv_w"], w["ssm_conv_b"], dt_bias, a_log,
                                  d_skip, hin, dy_ssd, dp, n_chunks)
    g["ssm_dt_bias"], g["ssm_a_log"], g["ssm_d_skip"] = (dpar[i:i + 1, 0:SSM_HEADS] for i in range(3))
    x_chunks = [(src, c0) for src, arr in enumerate((dxs, dbm, dcm)) for c0 in range(0, arr.shape[1], CONV_LANES)]
    dp, dconv = run("ssm_conv_bwd", _conv_bwd, "ssm_conv_bwd", p, P_XBC // CONV_DIM, [dxs, dbm, dcm], x_chunks,
                    w["ssm_conv_w"], w["ssm_conv_b"], n_rows, False, into=dp, into_blk=P_XBC // CONV_DIM)
    g["ssm_conv_w"], g["ssm_conv_b"] = dconv[0:4], dconv[4:5]
    g["w_cat_t"] = _matmul("in_proj_dw", dp, n1, "tn", F32)
    dn1 = run("in_proj_dx", _matmul, "in_proj_dx", dp, w["w_cat"], "nt", F32)

    def pre_mix_bwd(r0, dn, d1, xt, hd, wn):
        dx, dw = _rms_bwd(dn, h0_tile(r0, xt, hd), wn)
        dh0 = jnp.where(_valid_rows(r0, dn.shape[0], PAD), dx + d1, 0.0)
        return [dh0, dh0, dw]

    dx_out, dhead, g["norm_pre_mix"] = _rowwise(
        "pre_mix_bwd", pre_mix_bwd, n_rows, T, [(dn1, D_MODEL, 0), (dh1, D_MODEL, 0)] + x_in,
        [w["norm_pre_mix"]], [(D_MODEL, F32, "prev", n_tok), (D_MODEL, F32, "first")], [D_MODEL])
    return jnp.sum(loss_cols), dx_out, dhead


_IN_SECTIONS = [((5152, 6176), P_Q), ((6176, 6432), P_K), ((6432, 6688), P_V), ((5120, 5152), P_DT),
                ((0, 2048), P_Z), ((6688, 8736), P_GATE), ((2048, 5120), P_XBC)]


IN_SHARD = N_IN // 4


def _shard_pieces(a, b):
    return [(j, max(a, j * IN_SHARD) - j * IN_SHARD, min(b, (j + 1) * IN_SHARD) - j * IN_SHARD)
            for j in range(4) if max(a, j * IN_SHARD) < min(b, (j + 1) * IN_SHARD)]


def _to_cat(w4):
    parts, at = [], 0
    for (a, b), off in _IN_SECTIONS:
        if off > at:
            parts.append(jnp.zeros((w4.shape[1], off - at), w4.dtype))
        parts += [w4[j, :, lo:hi] for j, lo, hi in _shard_pieces(a, b)]
        at = off + (b - a)
    return jnp.concatenate(parts, axis=1)


def _from_cat_t(g_cat_t):
    shards = [[] for _ in range(4)]
    for (a, b), off in sorted(_IN_SECTIONS):
        for j, lo, hi in _shard_pieces(a, b):
            start = off + j * IN_SHARD + lo - a
            shards[j].append(g_cat_t[start:start + hi - lo])
    return jnp.stack([jnp.concatenate(s, axis=0) for s in shards])


LANES = 1024
_BIG = [("w_in", 1024, 2184, "chip"), ("w_ssm_out", 512, 1024, "row"), ("w_attn_out", 256, 1024, "row"),
        ("w_mix_out", 256, 1024, "row"), ("w_ffn_up", 1024, 1408, "col"), ("w_ffn_down", 704, 1024, "row"),
        ("small", 32, LANES, "chip")]
_SMALL_SHARDED = [("ssm_conv_w", (4, 768), 1), ("ffn_conv_w", (3, 1408), 1), ("meta_tokens", (16, 256), 1)]
_REPLICATED = [("norm_pre_mix", 1024), ("ssm_conv_b", 3072), ("ssm_dt_bias", 32), ("ssm_a_log", 32),
               ("ssm_d_skip", 32), ("ssm_norm", 2048), ("attn_sinks", 16), ("norm_post_mix", 1024),
               ("norm_pre_ffn", 1024), ("ffn_conv_b", 5632), ("norm_post_ffn", 1024)]
SMALL_ROWS = 24


def _rep_rows():
    out, at = [], 0
    for _, width in _REPLICATED:
        out.append((at, -(-width // LANES)))
        at += out[-1][1]
    return out, at


def _in_rows(parts):
    rows = [jnp.pad(a, ((0, 0), (0, -a.shape[1] % LANES))).reshape(-1, LANES) for a in parts]
    flat = jnp.concatenate(rows, axis=0)
    return jnp.pad(flat, ((0, SMALL_ROWS - flat.shape[0]), (0, 0)))
WEIGHT_ORDER = ["meta_tokens", "norm_pre_mix", "w_in", "ssm_conv_w", "ssm_conv_b", "ssm_dt_bias", "ssm_a_log",
                "ssm_d_skip", "ssm_norm", "w_ssm_out", "attn_sinks", "w_attn_out", "w_mix_out", "norm_post_mix",
                "norm_pre_ffn", "w_ffn_up", "ffn_conv_w", "ffn_conv_b", "w_ffn_down", "norm_post_ffn"]


def _flatten(parts, rows):
    flat = jnp.concatenate([a.reshape(-1) for a in parts])
    return jnp.pad(flat, (0, rows * LANES - flat.shape[0])).reshape(rows, LANES)


def _unflatten(flat, shapes):
    flat = flat.reshape(-1)
    out, off = [], 0
    for shp in shapes:
        n = math.prod(shp)
        out.append(flat[off:off + n].reshape(shp))
        off += n
    return out


def _shard_of(full, chip, shape, axis):
    return lax.slice_in_dim(full, chip * shape[axis], (chip + 1) * shape[axis], axis=axis)


def _full_shape(r, c, layout):
    return {"row": (4 * r, c), "col": (r, 4 * c), "chip": (4, r, c), "chip_cols": (4, r, c)}[layout]


def _half_shape(r, c, layout):
    return (r, c // 2) if layout == "chip_cols" else (r // 2, c)


def _shard_view(ref, r, c, layout, chip):
    if layout == "row":
        return ref.at[pl.ds(pl.multiple_of(chip * r, 16), r), :]
    if layout == "col":
        return ref.at[:, pl.ds(pl.multiple_of(chip * c, 128), c)]
    return ref.at[chip]


def _half_view(ref, r, c, layout, chip, half):
    if layout == "chip_cols":
        return ref.at[chip, :, pl.ds(pl.multiple_of(half * (c // 2), 128), c // 2)]
    hr = r // 2
    if layout == "row":
        return ref.at[pl.ds(pl.multiple_of(chip * r + half * hr, 16), hr), :]
    r0 = pl.multiple_of(half * hr, 16)
    if layout == "col":
        return ref.at[pl.ds(r0, hr), pl.ds(pl.multiple_of(chip * c, 128), c)]
    return ref.at[chip, pl.ds(r0, hr), :]


def _mesh_pos():
    return lax.axis_index("x"), lax.axis_index("y"), lax.axis_index("c")


def _other_chips(x, y):
    return [(1 - x, y), (x, 1 - y), (1 - x, 1 - y)]


def _chip_index(x, y):
    return 2 * x + y


def _run_exchange(name, ex):
    n_in, n_out = len(ex.ins), len(ex.out_shapes)

    def body(*refs):
        in_refs, out_refs = refs[:n_in], refs[n_in:n_in + n_out]
        send_sems, recv_sems = refs[n_in + n_out:]
        copies = [pltpu.make_async_remote_copy(src_ref=s, dst_ref=d, send_sem=send_sems.at[i], recv_sem=recv_sems.at[i],
                                               device_id=dev, device_id_type=MESH)
                  for i, (s, d, dev) in enumerate(ex.make_copies(in_refs, out_refs))]
        assert len(copies) == ex.n_copies
        for cp in copies:
            cp.start()
        for cp in copies:
            cp.wait()

    return pl.pallas_call(
        body, name=name, in_specs=[ANY] * n_in, out_specs=[ANY] * n_out, out_shape=list(ex.out_shapes),
        scratch_shapes=[pltpu.SemaphoreType.DMA((ex.n_copies,)), pltpu.SemaphoreType.DMA((ex.n_copies,))],
        compiler_params=pltpu.CompilerParams(has_side_effects=True),
    )(*ex.ins)


def _join(*exs):
    def make(in_refs, out_refs):
        copies, i0, o0 = [], 0, 0
        for ex in exs:
            copies += ex.make_copies(in_refs[i0:i0 + len(ex.ins)], out_refs[o0:o0 + len(ex.out_shapes)])
            i0, o0 = i0 + len(ex.ins), o0 + len(ex.out_shapes)
        return copies

    aliases, i0, o0 = {}, 0, 0
    for ex in exs:
        aliases.update({i0 + k: o0 + v for k, v in ex.aliases.items()})
        i0, o0 = i0 + len(ex.ins), o0 + len(ex.out_shapes)
    return _Exchange([a for ex in exs for a in ex.ins], [s for ex in exs for s in ex.out_shapes], make,
                     sum(ex.n_copies for ex in exs), aliases)


def _split(exs, results):
    out, o0 = [], 0
    for ex in exs:
        out.append(list(results[o0:o0 + len(ex.out_shapes)]))
        o0 += len(ex.out_shapes)
    return out


def _gather_ici(entries, shards):
    def make(in_refs, out_refs):
        x, y, c = _mesh_pos()
        j = _chip_index(x, y)
        copies = []
        for ref_in, ref_out, (_, r, cc, lay) in zip(in_refs, out_refs, entries):
            copies.append((ref_in, _shard_view(ref_out, r, cc, lay, j), None))
            mine = ref_in.at[pl.ds(pl.multiple_of(c * (r // 2), 16), r // 2), :]
            copies += [(mine, _half_view(ref_out, r, cc, lay, j, c), (*ch, c)) for ch in _other_chips(x, y)]
        return copies

    shapes = [jax.ShapeDtypeStruct(_full_shape(r, cc, lay), s.dtype) for s, (_, r, cc, lay) in zip(shards, entries)]
    return _Exchange(list(shards), shapes, make, 4 * len(entries))


def _gather_pass_on(entries, fulls):
    def make(in_refs, out_refs):
        x, y, c = _mesh_pos()
        copies = []
        for ref, (_, r, cc, lay) in zip(out_refs, entries):
            for ch in _other_chips(x, y):
                landed = _half_view(ref, r, cc, lay, _chip_index(*ch), c)
                copies.append((landed, landed, (x, y, 1 - c)))
        return copies

    return _Exchange(list(fulls), [jax.ShapeDtypeStruct(f.shape, f.dtype) for f in fulls], make, 3 * len(entries),
                     {a: a for a in range(len(entries))})


GATHER_PIECES = 4


def _gather_weights(entries, shards):
    n = len(entries)
    pieces = [GATHER_PIECES if (r // 2) % (16 * GATHER_PIECES) == 0 else 1 for _, r, _, _ in entries]
    n_sems = 6 * sum(pieces)

    def body(*refs):
        ins, outs = refs[:n], refs[n:2 * n]
        send_sems, recv_sems, local_sems = refs[2 * n:]
        x, y, c = _mesh_pos()
        j = _chip_index(x, y)
        sibling = (x, y, 1 - c)
        chips = _other_chips(x, y)
        idx = [_chip_index(*ch) for ch in chips]

        def remote(k, src, dst, dev):
            return pltpu.make_async_remote_copy(src_ref=src, dst_ref=dst, send_sem=send_sems.at[k],
                                                recv_sem=recv_sems.at[k], device_id=dev, device_id_type=MESH)

        own = [pltpu.make_async_copy(ins[a], _shard_view(outs[a], r, cc, lay, j), local_sems.at[a])
               for a, (_, r, cc, lay) in enumerate(entries)]
        for cp in own:
            cp.start()
        def piece(view, p, a):
            rows = entries[a][1] // 2 // pieces[a]
            return view if pieces[a] == 1 else view.at[pl.ds(p * rows, rows), :]

        first, passed, landings, from_sibling = [], [], [], []
        for a, (_, r, cc, lay) in enumerate(entries):
            mine = ins[a].at[pl.ds(pl.multiple_of(c * (r // 2), 16), r // 2), :]
            for p in range(pieces[a]):
                s0 = 6 * (sum(pieces[:a]) + p)
                for k, ch in enumerate(chips):
                    first.append(remote(s0 + k, piece(mine, p, a), piece(_half_view(outs[a], r, cc, lay, j, c), p, a), (*ch, c)))
                    landed = piece(_half_view(outs[a], r, cc, lay, idx[k], c), p, a)
                    landings.append(remote(s0 + k, landed, landed, sibling))
                    passed.append(remote(s0 + 3 + k, landed, landed, sibling))
                    theirs = piece(_half_view(outs[a], r, cc, lay, idx[k], 1 - c), p, a)
                    from_sibling.append(remote(s0 + 3 + k, theirs, theirs, sibling))
        for cp in first:
            cp.start()
        for landing, cp in zip(landings, passed):
            landing.wait_recv()
            cp.start()
        for cp in from_sibling:
            cp.wait_recv()
        for cp in first + passed:
            cp.wait_send()
        for cp in own:
            cp.wait()

    return pl.pallas_call(
        body, name="gather_weights", in_specs=[ANY] * n, out_specs=[ANY] * n,
        out_shape=[jax.ShapeDtypeStruct(_full_shape(r, cc, lay), s.dtype) for s, (_, r, cc, lay) in zip(shards, entries)],
        scratch_shapes=[pltpu.SemaphoreType.DMA((n_sems,)), pltpu.SemaphoreType.DMA((n_sems,)), pltpu.SemaphoreType.DMA((n,))],
        compiler_params=pltpu.CompilerParams(has_side_effects=True),
    )(*shards)


def _pair_exchange(entries, grads):
    def make(in_refs, out_refs):
        x, y, c = _mesh_pos()
        return [(_half_view(ref_in, r, cc, lay, i, 1 - c), ref_out.at[i], (x, y, 1 - c))
                for ref_in, ref_out, (_, r, cc, lay) in zip(in_refs, out_refs, entries) for i in range(4)]

    return _Exchange(list(grads), [jax.ShapeDtypeStruct((4,) + _half_shape(r, cc, lay), F32) for _, r, cc, lay in entries],
                     make, 4 * len(entries))


def _whole_to_sibling(arrays):
    def make(in_refs, out_refs):
        x, y, c = _mesh_pos()
        return [(r, o, (x, y, 1 - c)) for r, o in zip(in_refs, out_refs)]

    return _Exchange(list(arrays), [jax.ShapeDtypeStruct(a.shape, a.dtype) for a in arrays], make, len(arrays))


def _chip_exchange(psends):
    def make(in_refs, out_refs):
        x, y, c = _mesh_pos()
        return [(ref_in.at[_chip_index(*ch)], ref_out.at[k], (*ch, c))
                for ref_in, ref_out in zip(in_refs, out_refs) for k, ch in enumerate(_other_chips(x, y))]

    return _Exchange(list(psends), [jax.ShapeDtypeStruct((3,) + p.shape[1:], p.dtype) for p in psends], make,
                     3 * len(psends))


def _to_all_chips(array):
    def make(in_refs, out_refs):
        x, y, c = _mesh_pos()
        return [(in_refs[0], out_refs[0].at[k], (*ch, c)) for k, ch in enumerate(_other_chips(x, y))]

    return _Exchange([array], [jax.ShapeDtypeStruct((3,) + array.shape, array.dtype)], make, 3)


SUM_ROWS = 512
ADAM_ROWS = 256


def _pair_sum(name, grad, recv, ids, r, c, layout):
    hr, c = _half_shape(r, c, layout)
    tr = _row_tile(hr, SUM_ROWS)
    nb = hr // tr

    def body(ids_ref, g_ref, r_ref, send_ref, own_ref):
        s = g_ref[...] + r_ref[...]
        send_ref[...] = s.astype(send_ref.dtype)

        @pl.when(pl.program_id(1) == ids_ref[1])
        def _():
            own_ref[...] = s

    if layout == "row":
        g_spec = pl.BlockSpec((tr, c), lambda t, j, ids_ref: ((j * r + ids_ref[0] * hr) // tr + t, 0))
    elif layout == "col":
        g_spec = pl.BlockSpec((tr, c), lambda t, j, ids_ref: (ids_ref[0] * nb + t, j))
    elif layout == "chip_cols":
        g_spec = pl.BlockSpec((None, tr, c), lambda t, j, ids_ref: (j, t, ids_ref[0]))
    else:
        g_spec = pl.BlockSpec((None, tr, c), lambda t, j, ids_ref: (j, ids_ref[0] * nb + t, 0))
    grid_spec = pltpu.PrefetchScalarGridSpec(
        num_scalar_prefetch=1, grid=(nb, 4),
        in_specs=[g_spec, pl.BlockSpec((None, tr, c), lambda t, j, ids_ref: (j, t, 0))],
        out_specs=[pl.BlockSpec((None, tr, c), lambda t, j, ids_ref: (j, t, 0)),
                   pl.BlockSpec((tr, c), lambda t, j, ids_ref: (t, 0))])
    return pl.pallas_call(
        body, name=name, grid_spec=grid_spec,
        out_shape=[jax.ShapeDtypeStruct((4, hr, c), BF16), jax.ShapeDtypeStruct((hr, c), F32)],
        compiler_params=_cparams(2),
    )(ids, grad, recv)


def _chip_sum(name, own, recv):
    hr, c = own.shape
    tr = _row_tile(hr, SUM_ROWS)

    def body(o_ref, r_ref, out_ref):
        out_ref[...] = ((o_ref[...] + r_ref[0].astype(F32)) + r_ref[1].astype(F32)) + r_ref[2].astype(F32)

    return pl.pallas_call(
        body, name=name, grid=(hr // tr,),
        in_specs=[pl.BlockSpec((tr, c), lambda i: (i, 0)), pl.BlockSpec((3, tr, c), lambda i: (0, i, 0))],
        out_specs=pl.BlockSpec((tr, c), lambda i: (i, 0)),
        out_shape=jax.ShapeDtypeStruct((hr, c), F32), compiler_params=_cparams(1),
    )(own, recv)


def _chip_sum_small(own, recv, ids):
    def body(ids_ref, o_ref, r_ref, out_ref):
        j = ids_ref[1]
        total = None
        for i in range(4):
            m = jnp.bitwise_xor(i, j)
            term = jnp.where(m == 0, o_ref[...], jnp.where(m == 2, r_ref[0], jnp.where(m == 1, r_ref[1], r_ref[2])))
            total = term if total is None else total + term
        out_ref[...] = total

    grid_spec = pltpu.PrefetchScalarGridSpec(
        num_scalar_prefetch=1, grid=(1,),
        in_specs=[pl.BlockSpec(own.shape, lambda i, ids_ref: (0, 0)), pl.BlockSpec(recv.shape, lambda i, ids_ref: (0, 0, 0))],
        out_specs=pl.BlockSpec(own.shape, lambda i, ids_ref: (0, 0)))
    return pl.pallas_call(body, name="chip_sum_small", grid_spec=grid_spec,
                          out_shape=jax.ShapeDtypeStruct(own.shape, F32), compiler_params=_cparams(1))(ids, own, recv)


def _adamw(name, w, m, v, mine, theirs, ids):
    lead = (None,) * (w.ndim - 2)
    rows, cols = w.shape[-2:]
    half = rows // 2
    tr = _row_tile(half, ADAM_ROWS, unit=8)
    nb = half // tr
    c1 = 1.0 / (1.0 - ADAM_B1 ** ADAM_STEP)
    c2 = 1.0 / (1.0 - ADAM_B2 ** ADAM_STEP)

    def body(ids_ref, w_ref, m_ref, v_ref, mine_ref, theirs_ref, g_out, d_out, m_out, v_out):
        g = jnp.where(pl.program_id(0) == ids_ref[0], mine_ref[...], theirs_ref[...])
        m_new = ADAM_B1 * m_ref[...] + (1.0 - ADAM_B1) * g
        v_new = ADAM_B2 * v_ref[...] + (1.0 - ADAM_B2) * (g * g)
        d_out[...] = -ADAM_LR * ((m_new * c1) / (jnp.sqrt(v_new * c2) + ADAM_EPS) + ADAM_WD * w_ref[...])
        g_out[...] = g
        m_out[...] = m_new
        v_out[...] = v_new

    full = pl.BlockSpec(lead + (tr, cols), lambda h, i, ids_ref: (0,) * len(lead) + (h * nb + i, 0))
    part = pl.BlockSpec((tr, cols), lambda h, i, ids_ref: (i, 0))
    grid_spec = pltpu.PrefetchScalarGridSpec(num_scalar_prefetch=1, grid=(2, nb),
                                             in_specs=[full, full, full, part, part], out_specs=[full] * 4)
    return pl.pallas_call(
        body, name=name, grid_spec=grid_spec,
        out_shape=[jax.ShapeDtypeStruct(w.shape, F32)] * 4, compiler_params=_cparams(2),
    )(ids, w, m, v, mine, theirs)


def _adamw_whole(name, w, m, v, g):
    rows, cols = w.shape[-2:]
    tr = _row_tile(rows, 2 * ADAM_ROWS, unit=8)
    c1 = 1.0 / (1.0 - ADAM_B1 ** ADAM_STEP)
    c2 = 1.0 / (1.0 - ADAM_B2 ** ADAM_STEP)

    def body(w_ref, m_ref, v_ref, g_ref, g_out, d_out, m_out, v_out):
        g = g_ref[...]
        m_new = ADAM_B1 * m_ref[...] + (1.0 - ADAM_B1) * g
        v_new = ADAM_B2 * v_ref[...] + (1.0 - ADAM_B2) * (g * g)
        d_out[...] = -ADAM_LR * ((m_new * c1) / (jnp.sqrt(v_new * c2) + ADAM_EPS) + ADAM_WD * w_ref[...])
        g_out[...] = g
        m_out[...] = m_new
        v_out[...] = v_new

    full = pl.BlockSpec((None, tr, cols), lambda i: (0, i, 0))
    return pl.pallas_call(
        body, name=name, grid=(rows // tr,), in_specs=[full, full, full, pl.BlockSpec((tr, cols), lambda i: (i, 0))],
        out_specs=[full] * 4, out_shape=[jax.ShapeDtypeStruct(w.shape, F32)] * 4, compiler_params=_cparams(1),
    )(w, m, v, g)


def _adamw_replicated(g_rows, ws, ms, vs):
    n = len(ws)
    layout, _ = _rep_rows()
    c1 = 1.0 / (1.0 - ADAM_B1 ** ADAM_STEP)
    c2 = 1.0 / (1.0 - ADAM_B2 ** ADAM_STEP)

    def body(g_ref, *refs):
        w_refs, m_refs, v_refs = refs[0:n], refs[n:2 * n], refs[2 * n:3 * n]
        outs = refs[3 * n:]
        for k, (r0, rows) in enumerate(layout):
            width = w_refs[k].shape[1]
            g = jnp.concatenate([g_ref[r0 + j:r0 + j + 1, :] for j in range(rows)], axis=1)[:, 0:width]
            m_new = ADAM_B1 * m_refs[k][...] + (1.0 - ADAM_B1) * g
            v_new = ADAM_B2 * v_refs[k][...] + (1.0 - ADAM_B2) * (g * g)
            outs[k][...] = g
            outs[n + k][...] = -ADAM_LR * ((m_new * c1) / (jnp.sqrt(v_new * c2) + ADAM_EPS) + ADAM_WD * w_refs[k][...])
            outs[2 * n + k][...] = m_new
            outs[3 * n + k][...] = v_new

    res = pl.pallas_call(body, name="adamw_replicated",
                         out_shape=[jax.ShapeDtypeStruct(w.shape, F32) for _ in range(4) for w in ws])(g_rows, *ws, *ms, *vs)
    return [res[k * n:(k + 1) * n] for k in range(4)]


def _small_shard(parts):
    return _flatten(parts, _BIG[-1][1])


_ENTRY = {e[0]: e for e in _BIG}
_GRAD_ENTRY = {**_ENTRY, "w_in": ("w_in", IN_SHARD, D_MODEL, "chip_cols")}
FFN_MATS = ("w_ffn_down", "w_ffn_up")
MIXER_MATS = ("w_mix_out", "w_ssm_out", "w_attn_out")


class _StepPlan:
    def __init__(self, w, late_shards, shards, ids):
        self.w, self.g = w, {}
        self.late_shards, self.shards, self.ids = late_shards, shards, ids
        self.sums, self.halves, self.results = {}, {}, {}

    def run(self, name, fn, *args, **kw):
        at = getattr(self, "_at_" + name, None)
        if at is None:
            return fn(*args, **kw)
        exchange, landed = at()
        res, extra = fn(*args, bg=exchange, **kw)
        landed(extra)
        return res

    def _at_ssd_fwd(self):
        def landed(fulls):
            self.partly_gathered = fulls

        return _gather_ici([_ENTRY[n] for n in MIXER_MATS], [self.late_shards[n] for n in MIXER_MATS]), landed

    def _at_attn_fwd(self):
        stages = (_gather_pass_on([_ENTRY[n] for n in MIXER_MATS], self.partly_gathered),
                  _gather_ici([_ENTRY[n] for n in FFN_MATS], [self.late_shards[n] for n in FFN_MATS]))

        def landed(extra):
            mixer, self.partly_gathered = _split(stages, extra)
            self.w.update(zip(MIXER_MATS, mixer))

        return _join(*stages), landed

    def _at_ssm_gate_norm(self):
        return (_gather_pass_on([_ENTRY[n] for n in FFN_MATS], self.partly_gathered),
                lambda fulls: self.w.update(zip(FFN_MATS, fulls)))

    def pair_sums(self, names, grads, recv):
        for n, gr, rv in zip(names, grads, recv):
            _, r, c, lay = _GRAD_ENTRY[n]
            self.sums[n] = _pair_sum("pair_sum_" + n, gr, rv, self.ids, r, c, lay)

    def chip_sums(self, names, recv):
        for n, rv in zip(names, recv):
            self.halves[n] = _chip_sum("chip_sum_" + n, self.sums[n][1], rv)

    def adamw(self, names, theirs):
        for n, th in zip(names, theirs):
            sh = self.shards[n]
            if n == "w_in":
                mine_first = self.ids[0] == 0
                g_t = jnp.where(mine_first, jnp.concatenate([self.halves[n], th], axis=1),
                                jnp.concatenate([th, self.halves[n]], axis=1))
                res = _adamw_whole("adamw_" + n, *[jnp.swapaxes(sh[k], -1, -2) for k in ("w", "m", "v")], g_t)
                self.results[n] = [jnp.swapaxes(r, -1, -2) for r in res]
            else:
                self.results[n] = _adamw("adamw_" + n, sh["w"], sh["m"], sh["v"], self.halves[n], th, self.ids)

    def _pair_stage(self, names, grads):
        return (_pair_exchange([_GRAD_ENTRY[n] for n in names], grads),
                lambda recv: self.pair_sums(names, grads, recv))

    def _at_ffn_up_dx(self):
        return self._pair_stage(FFN_MATS, [self.g[n] for n in FFN_MATS])

    def _at_ssm_gate_norm_bwd(self):
        return self._pair_stage(MIXER_MATS, [self.g[n] for n in MIXER_MATS])

    def _at_attn_bwd(self):
        return _chip_exchange([self.sums[n][0] for n in FFN_MATS]), lambda recv: self.chip_sums(FFN_MATS, recv)

    def _at_ssd_bwd(self):
        stages = (_chip_exchange([self.sums[n][0] for n in MIXER_MATS]),
                  _whole_to_sibling([self.halves[n] for n in FFN_MATS]))

        def landed(extra):
            recv, theirs = _split(stages, extra)
            self.chip_sums(MIXER_MATS, recv)
            self.adamw(FFN_MATS, theirs)

        return _join(*stages), landed

    def _at_ssm_conv_bwd(self):
        return _whole_to_sibling([self.halves[n] for n in MIXER_MATS]), lambda theirs: self.adamw(MIXER_MATS, theirs)

    def _at_in_proj_dx(self):
        grads = [_from_cat_t(self.g.pop("w_cat_t"))]
        self.pair_sums(("w_in",), grads,
                       _run_exchange("grad_pair_exchange_w_in", _pair_exchange([_GRAD_ENTRY["w_in"]], grads)))
        return _chip_exchange([self.sums["w_in"][0]]), lambda recv: self.chip_sums(("w_in",), recv)

    def finish(self, g_small, g_rep, rep_shards):
        stages = (_pair_exchange([_ENTRY["small"]], [g_small]), _whole_to_sibling([g_rep]))
        recv_small, recv_rep = _split(stages, _run_exchange("grad_pair_exchange_tail", _join(*stages)))
        self.pair_sums(("small",), [g_small], recv_small)
        p_rep, = _rowwise("pair_sum_replicated", lambda r0, a, b: [a + b], SMALL_ROWS, SMALL_ROWS,
                          [(g_rep, LANES, 0), (recv_rep[0], LANES, 0)], [], [(LANES, F32)], [])
        stages = (_chip_exchange([self.sums["small"][0]]), _to_all_chips(p_rep))
        recv, recv_rep = _split(stages, _run_exchange("grad_chip_exchange_tail", _join(*stages)))
        self.chip_sums(("small",), recv)
        g_rep_tot = _chip_sum_small(p_rep, recv_rep[0], self.ids)
        last = ("w_in", "small")
        self.adamw(last, _run_exchange("grad_half_share_tail", _whole_to_sibling([self.halves[n] for n in last])))
        self.results["replicated"] = _adamw_replicated(g_rep_tot, rep_shards["w"], rep_shards["m"], rep_shards["v"])
        return g_rep_tot[_rep_rows()[1], 0]


def kernel(x, meta_tokens, norm_pre_mix, w_in, ssm_conv_w, ssm_conv_b, ssm_dt_bias, ssm_a_log, ssm_d_skip, ssm_norm, w_ssm_out, attn_sinks, w_attn_out, w_mix_out, norm_post_mix, norm_pre_ffn, w_ffn_up, ffn_conv_w, ffn_conv_b, w_ffn_down, norm_post_ffn, loss_target, m_meta_tokens, m_norm_pre_mix, m_w_in, m_ssm_conv_w, m_ssm_conv_b, m_ssm_dt_bias, m_ssm_a_log, m_ssm_d_skip, m_ssm_norm, m_w_ssm_out, m_attn_sinks, m_w_attn_out, m_w_mix_out, m_norm_post_mix, m_norm_pre_ffn, m_w_ffn_up, m_ffn_conv_w, m_ffn_conv_b, m_w_ffn_down, m_norm_post_ffn, v_meta_tokens, v_norm_pre_mix, v_w_in, v_ssm_conv_w, v_ssm_conv_b, v_ssm_dt_bias, v_ssm_a_log, v_ssm_d_skip, v_ssm_norm, v_w_ssm_out, v_attn_sinks, v_w_attn_out, v_w_mix_out, v_norm_post_mix, v_norm_pre_ffn, v_w_ffn_up, v_ffn_conv_w, v_ffn_conv_b, v_w_ffn_down, v_norm_post_ffn):
    args = dict(locals())
    squeeze = lambda a: a.reshape(a.shape[-2:])
    wts = {n: squeeze(args[n]) for n in WEIGHT_ORDER}
    mom = {n: squeeze(args["m_" + n]) for n in WEIGHT_ORDER}
    var = {n: squeeze(args["v_" + n]) for n in WEIGHT_ORDER}
    x_i, y_i, c_i = _mesh_pos()
    ids = jnp.stack([c_i, _chip_index(x_i, y_i)]).astype(jnp.int32)
    big_names = [n for n, _, _, _ in _BIG[:-1]]
    small_names = [n for n, _, _ in _SMALL_SHARDED]
    rep_names = [n for n, _ in _REPLICATED]

    stacks = {"w": wts, "m": mom, "v": var}
    shards = {n: {"w": args[n], "m": args["m_" + n], "v": args["v_" + n]} for n in big_names}
    shards["small"] = {k: _small_shard([d[n] for n in small_names]) for k, d in stacks.items()}
    rep_shards = {k: [d[n] for n in rep_names] for k, d in stacks.items()}

    w_in4, small_all = _gather_weights([_ENTRY["w_in"], _ENTRY["small"]], [wts["w_in"].astype(BF16), shards["small"]["w"]])
    w = {n: wts[n] for n in rep_names}
    w["w_cat"] = _to_cat(w_in4)
    small_parts = [_unflatten(small_all[i], [shp for _, shp, _ in _SMALL_SHARDED]) for i in range(4)]
    for k, (n, _, axis) in enumerate(_SMALL_SHARDED):
        w[n] = jnp.concatenate([small_parts[i][k] for i in range(4)], axis=axis)
    plan = _StepPlan(w, {n: wts[n].astype(BF16) for n in MIXER_MATS + FFN_MATS}, shards, ids)

    head = jnp.concatenate([jnp.zeros((PAD, D_MODEL), F32), w["meta_tokens"]], axis=0)
    loss_sum, dx, dhead = _local_step(x[0], head, loss_target[0], plan)
    g = plan.g
    g["meta_tokens"] = dhead[PAD:]
    g_small = jnp.stack([_small_shard([_shard_of(g[n], i, shp, ax) for n, shp, ax in _SMALL_SHARDED]) for i in range(4)])
    loss_part = (loss_sum * (0.5 / D_MODEL)).reshape(1, 1)
    loss = plan.finish(g_small, _in_rows([g[n] for n in rep_names] + [loss_part]), rep_shards)

    results = {}
    for kind in range(4):
        results.update({(kind, n): plan.results[n][kind] for n in big_names})
        parts = _unflatten(plan.results["small"][kind], [shp for _, shp, _ in _SMALL_SHARDED])
        results.update({(kind, n): parts[k] for k, n in enumerate(small_names)})
        results.update({(kind, n): plan.results["replicated"][kind][k] for k, n in enumerate(rep_names)})
    outs = [results[kind, n].reshape(args[n].shape) for kind in range(4) for n in WEIGHT_ORDER]
    return (loss, dx[None], *outs)
```
